```python
import jax, jax.numpy as jnp
from jax import lax
import numpy as np

D_MODEL = 1024
BATCH = 16
SEQ = 4096
DEPTH = 4

MEM_LEN = 256
CONV_WIDTH = 512
CONV_K = 3
HG_HEADS = 4
HG_F = 128
HG_I = 128
HG_QK = HG_HEADS * HG_F
HG_WIDTH = HG_HEADS * HG_I
HG_CHUNK = 32
MEM_HEADS = 4
MEM_HEAD_DIM = 128
MEM_WIDTH = MEM_HEADS * MEM_HEAD_DIM
N_BRANCH = 3
BRANCH_WIDTH = 512
D_FF = 4 * D_MODEL
ALPHA = (2.0 * DEPTH) ** 0.25
BETA = (8.0 * DEPTH) ** -0.25
LN_EPS = 1e-5
RMS_EPS = 1e-6
IN_SPLITS = (CONV_WIDTH, CONV_WIDTH, CONV_WIDTH, HG_QK, HG_QK, HG_WIDTH, HG_WIDTH, MEM_WIDTH, D_MODEL, D_MODEL, D_MODEL)
IN_COLS = sum(IN_SPLITS)

kernel_name = "hybrid_conv_hgrn2_memattn_postnorm"


def _layer_norm(x, g, b):
    xf = x.astype(jnp.float32)
    mu = jnp.mean(xf, axis=-1, keepdims=True)
    var = jnp.mean(jnp.square(xf - mu), axis=-1, keepdims=True)
    y = (xf - mu) * lax.rsqrt(var + LN_EPS) * g.astype(jnp.float32) + b.astype(jnp.float32)
    return y.astype(x.dtype)


def _short_conv_mixer(gate_b, gate_c, h, conv_w):
    u = gate_c * h
    seq = u.shape[1]
    u_pad = jnp.pad(u, ((0, 0), (CONV_K - 1, 0), (0, 0)))
    y = u_pad[:, 0:seq] * conv_w[0]
    for tap in range(1, CONV_K):
        y = y + u_pad[:, tap:tap + seq] * conv_w[tap]
    return gate_b * y


def _hgrn2_mixer(q, f_logit, i, g, lb, norm_w):
    bsz, seq, _ = q.shape
    n_chunk = seq // HG_CHUNK
    f32 = jnp.float32
    fl = f_logit.astype(f32)
    lb = lb.astype(f32)
    sig = jax.nn.sigmoid(fl)
    log_f = jnp.log(lb + (1.0 - lb) * sig)
    k = (1.0 - lb) * jax.nn.sigmoid(-fl)

    def to_chunks(t, d):
        return t.reshape(bsz, n_chunk, HG_CHUNK, HG_HEADS, d).transpose(0, 3, 1, 2, 4)

    qc = to_chunks(jax.nn.silu(q.astype(f32)), HG_F)
    kc = to_chunks(k, HG_F)
    vc = to_chunks(i.astype(f32), HG_I)
    bc = jnp.cumsum(to_chunks(log_f, HG_F), axis=3)
    b_ref = bc[:, :, :, HG_CHUNK // 2 - 1:HG_CHUNK // 2, :]
    b_last = bc[:, :, :, -1:, :]

    causal = jnp.tril(jnp.ones((HG_CHUNK, HG_CHUNK), dtype=bool))
    scores = jnp.einsum('bhntf,bhnsf->bhnts', qc * jnp.exp(bc - b_ref), kc * jnp.exp(b_ref - bc))
    scores = jnp.where(causal, scores, 0.0)
    o_intra = jnp.einsum('bhnts,bhnsv->bhntv', scores, vc)

    q_in = qc * jnp.exp(bc)
    k_out = kc * jnp.exp(b_last - bc)
    decay = jnp.exp(b_last[:, :, :, 0, :])

    def step(state, xs):
        q_n, k_n, v_n, dec_n = xs
        o_n = jnp.einsum('bhtf,bhfv->bhtv', q_n, state)
        state = dec_n[..., None] * state + jnp.einsum('bhsf,bhsv->bhfv', k_n, v_n)
        return state, o_n

    xs = (jnp.moveaxis(q_in, 2, 0), jnp.moveaxis(k_out, 2, 0), jnp.moveaxis(vc, 2, 0), jnp.moveaxis(decay, 2, 0))
    init = jnp.zeros((bsz, HG_HEADS, HG_F, HG_I), f32)
    _, o_inter = lax.scan(step, init, xs)
    o = o_intra + jnp.moveaxis(o_inter, 0, 2)

    o = o.transpose(0, 2, 3, 1, 4).reshape(bsz, seq, HG_HEADS, HG_I)
    o = o * lax.rsqrt(jnp.mean(o * o, axis=-1, keepdims=True) + RMS_EPS) * norm_w.astype(f32)
    o = o.reshape(bsz, seq, HG_WIDTH) * jax.nn.silu(g.astype(f32))
    return o.astype(q.dtype)


def _memory_attention(q, mem_k, mem_v):
    bsz, seq, _ = q.shape
    qh = q.reshape(bsz, seq, MEM_HEADS, MEM_HEAD_DIM)
    kh = mem_k.reshape(bsz, MEM_LEN, MEM_HEADS, MEM_HEAD_DIM)
    vh = mem_v.reshape(bsz, MEM_LEN, MEM_HEADS, MEM_HEAD_DIM)
    s = jnp.einsum('bthd,bmhd->bhtm', qh, kh).astype(jnp.float32) * (MEM_HEAD_DIM ** -0.5)
    p = jax.nn.softmax(s, axis=-1).astype(vh.dtype)
    o = jnp.einsum('bhtm,bmhd->bthd', p, vh)
    return o.reshape(bsz, seq, MEM_WIDTH)


def _hybrid_layer(x, mem, lb, w_in, conv_w, hg_norm_w, w_mem_k, w_mem_v, w_branch, b_gate, w_o,
                  ln1_g, ln1_b, w_up, w_down, ln2_g, ln2_b):
    bsz, seq, _ = x.shape
    proj = jnp.einsum('bsd,dc->bsc', x, w_in)
    split_idx = [int(v) for v in np.cumsum(IN_SPLITS)[:-1]]
    cb, cc, ch, hq, hf, hi, hg, mq, ga, gb, gc = jnp.split(proj, split_idx, axis=-1)

    y_a = _short_conv_mixer(cb, cc, ch, conv_w)
    y_b = _hgrn2_mixer(hq, hf, hi, hg, lb, hg_norm_w)
    y_c = _memory_attention(mq, jnp.einsum('bmd,dw->bmw', mem, w_mem_k), jnp.einsum('bmd,dw->bmw', mem, w_mem_v))

    b_ga, b_gb, b_gc = jnp.split(b_gate, N_BRANCH, axis=-1)
    merged = (jax.nn.sigmoid(ga + b_ga) * jnp.einsum('bsw,wd->bsd', y_a, w_branch[0])
              + jax.nn.sigmoid(gb + b_gb) * jnp.einsum('bsw,wd->bsd', y_b, w_branch[1])
              + jax.nn.sigmoid(gc + b_gc) * jnp.einsum('bsw,wd->bsd', y_c, w_branch[2]))
    mixed = jnp.einsum('bsd,de->bse', merged, w_o)
    x = _layer_norm(ALPHA * x + mixed, ln1_g, ln1_b)

    h = jnp.square(jax.nn.relu(jnp.einsum('bsd,df->bsf', x, w_up)))
    x = _layer_norm(ALPHA * x + jnp.einsum('bsf,fd->bsd', h, w_down), ln2_g, ln2_b)
    return x


def _fwd_setup_inputs(seed: int = 0) -> dict:
    key = jax.random.key(seed)
    ks = jax.random.split(key, 20)
    nrm = jax.random.normal
    f32 = jnp.float32
    return {
        'x': nrm(ks[0], (BATCH, SEQ, D_MODEL), f32),
        'mem': nrm(ks[1], (BATCH, MEM_LEN, D_MODEL), f32),
        'lower_bounds': 0.02 * nrm(ks[2], (DEPTH, HG_QK), f32),
        'w_in': nrm(ks[3], (DEPTH, D_MODEL, IN_COLS), f32) * D_MODEL ** -0.5,
        'conv_w': nrm(ks[4], (DEPTH, CONV_K, CONV_WIDTH), f32) * CONV_K ** -0.5,
        'hg_norm_w': 1.0 + 0.02 * nrm(ks[5], (DEPTH, HG_I), f32),
        'w_mem_k': nrm(ks[6], (DEPTH, D_MODEL, MEM_WIDTH), f32) * D_MODEL ** -0.5,
        'w_mem_v': nrm(ks[7], (DEPTH, D_MODEL, MEM_WIDTH), f32) * (D_MODEL ** -0.5 * BETA),
        'w_branch': nrm(ks[8], (DEPTH, N_BRANCH, BRANCH_WIDTH, D_MODEL), f32) * (BRANCH_WIDTH ** -0.5 * BETA),
        'b_gate': 0.02 * nrm(ks[9], (DEPTH, N_BRANCH * D_MODEL), f32),
        'w_o': nrm(ks[10], (DEPTH, D_MODEL, D_MODEL), f32) * (D_MODEL ** -0.5 * BETA),
        'ln1_g': 1.0 + 0.02 * nrm(ks[11], (DEPTH, D_MODEL), f32),
        'ln1_b': 0.02 * nrm(ks[12], (DEPTH, D_MODEL), f32),
        'w_up': nrm(ks[13], (DEPTH, D_MODEL, D_FF), f32) * (D_MODEL ** -0.5 * BETA),
        'w_down': nrm(ks[14], (DEPTH, D_FF, D_MODEL), f32) * (D_FF ** -0.5 * BETA),
        'ln2_g': 1.0 + 0.02 * nrm(ks[15], (DEPTH, D_MODEL), f32),
        'ln2_b': 0.02 * nrm(ks[16], (DEPTH, D_MODEL), f32),
    }


def _fwd_reference(x, mem, lower_bounds, w_in, conv_w, hg_norm_w, w_mem_k, w_mem_v, w_branch, b_gate, w_o,
              ln1_g, ln1_b, w_up, w_down, ln2_g, ln2_b):
    lb_soft = jax.nn.softmax(lower_bounds.astype(jnp.float32), axis=0)
    lb_all = jnp.cumsum(lb_soft, axis=0) - lb_soft[0:1]
    for layer in range(DEPTH):
        x = _hybrid_layer(x, mem, lb_all[layer], w_in[layer], conv_w[layer], hg_norm_w[layer],
                          w_mem_k[layer], w_mem_v[layer], w_branch[layer], b_gate[layer], w_o[layer],
                          ln1_g[layer], ln1_b[layer], w_up[layer], w_down[layer], ln2_g[layer], ln2_b[layer])
    return x


import jax as _jax
import jax.numpy as _jnp

TWIN_FORMAT = 'train_step'
FWD_PARAMS = ['x', 'mem', 'lower_bounds', 'w_in', 'conv_w', 'hg_norm_w', 'w_mem_k', 'w_mem_v', 'w_branch', 'b_gate', 'w_o', 'ln1_g', 'ln1_b', 'w_up', 'w_down', 'ln2_g', 'ln2_b']
TWIN_WEIGHTS = ['lower_bounds', 'w_in', 'conv_w', 'hg_norm_w', 'w_mem_k', 'w_mem_v', 'w_branch', 'b_gate', 'w_o', 'ln1_g', 'ln1_b', 'w_up', 'w_down', 'ln2_g', 'ln2_b']
TWIN_DIFF_INPUT = 'x'
TWIN_INPUTS = ['x', 'mem', 'lower_bounds', 'w_in', 'conv_w', 'hg_norm_w', 'w_mem_k', 'w_mem_v', 'w_branch', 'b_gate', 'w_o', 'ln1_g', 'ln1_b', 'w_up', 'w_down', 'ln2_g', 'ln2_b', 'loss_target', 'm_lower_bounds', 'm_w_in', 'm_conv_w', 'm_hg_norm_w', 'm_w_mem_k', 'm_w_mem_v', 'm_w_branch', 'm_b_gate', 'm_w_o', 'm_ln1_g', 'm_ln1_b', 'm_w_up', 'm_w_down', 'm_ln2_g', 'm_ln2_b', 'v_lower_bounds', 'v_w_in', 'v_conv_w', 'v_hg_norm_w', 'v_w_mem_k', 'v_w_mem_v', 'v_w_branch', 'v_b_gate', 'v_w_o', 'v_ln1_g', 'v_ln1_b', 'v_w_up', 'v_w_down', 'v_ln2_g', 'v_ln2_b']
TWIN_OUTPUTS = ['loss', 'grad_x', 'grad_lower_bounds', 'grad_w_in', 'grad_conv_w', 'grad_hg_norm_w', 'grad_w_mem_k', 'grad_w_mem_v', 'grad_w_branch', 'grad_b_gate', 'grad_w_o', 'grad_ln1_g', 'grad_ln1_b', 'grad_w_up', 'grad_w_down', 'grad_ln2_g', 'grad_ln2_b', 'delta_lower_bounds', 'delta_w_in', 'delta_conv_w', 'delta_hg_norm_w', 'delta_w_mem_k', 'delta_w_mem_v', 'delta_w_branch', 'delta_b_gate', 'delta_w_o', 'delta_ln1_g', 'delta_ln1_b', 'delta_w_up', 'delta_w_down', 'delta_ln2_g', 'delta_ln2_b', 'new_m_lower_bounds', 'new_m_w_in', 'new_m_conv_w', 'new_m_hg_norm_w', 'new_m_w_mem_k', 'new_m_w_mem_v', 'new_m_w_branch', 'new_m_b_gate', 'new_m_w_o', 'new_m_ln1_g', 'new_m_ln1_b', 'new_m_w_up', 'new_m_w_down', 'new_m_ln2_g', 'new_m_ln2_b', 'new_v_lower_bounds', 'new_v_w_in', 'new_v_conv_w', 'new_v_hg_norm_w', 'new_v_w_mem_k', 'new_v_w_mem_v', 'new_v_w_branch', 'new_v_b_gate', 'new_v_w_o', 'new_v_ln1_g', 'new_v_ln1_b', 'new_v_w_up', 'new_v_w_down', 'new_v_ln2_g', 'new_v_ln2_b']
TWIN_LEAF_KINDS = {'loss': 'loss', 'grad_x': 'grad_x', 'grad_lower_bounds': 'grad_w', 'grad_w_in': 'grad_w', 'grad_conv_w': 'grad_w', 'grad_hg_norm_w': 'grad_w', 'grad_w_mem_k': 'grad_w', 'grad_w_mem_v': 'grad_w', 'grad_w_branch': 'grad_w', 'grad_b_gate': 'grad_w', 'grad_w_o': 'grad_w', 'grad_ln1_g': 'grad_w', 'grad_ln1_b': 'grad_w', 'grad_w_up': 'grad_w', 'grad_w_down': 'grad_w', 'grad_ln2_g': 'grad_w', 'grad_ln2_b': 'grad_w', 'delta_lower_bounds': 'delta_w', 'delta_w_in': 'delta_w', 'delta_conv_w': 'delta_w', 'delta_hg_norm_w': 'delta_w', 'delta_w_mem_k': 'delta_w', 'delta_w_mem_v': 'delta_w', 'delta_w_branch': 'delta_w', 'delta_b_gate': 'delta_w', 'delta_w_o': 'delta_w', 'delta_ln1_g': 'delta_w', 'delta_ln1_b': 'delta_w', 'delta_w_up': 'delta_w', 'delta_w_down': 'delta_w', 'delta_ln2_g': 'delta_w', 'delta_ln2_b': 'delta_w', 'new_m_lower_bounds': 'new_m', 'new_m_w_in': 'new_m', 'new_m_conv_w': 'new_m', 'new_m_hg_norm_w': 'new_m', 'new_m_w_mem_k': 'new_m', 'new_m_w_mem_v': 'new_m', 'new_m_w_branch': 'new_m', 'new_m_b_gate': 'new_m', 'new_m_w_o': 'new_m', 'new_m_ln1_g': 'new_m', 'new_m_ln1_b': 'new_m', 'new_m_w_up': 'new_m', 'new_m_w_down': 'new_m', 'new_m_ln2_g': 'new_m', 'new_m_ln2_b': 'new_m', 'new_v_lower_bounds': 'new_v', 'new_v_w_in': 'new_v', 'new_v_conv_w': 'new_v', 'new_v_hg_norm_w': 'new_v', 'new_v_w_mem_k': 'new_v', 'new_v_w_mem_v': 'new_v', 'new_v_w_branch': 'new_v', 'new_v_b_gate': 'new_v', 'new_v_w_o': 'new_v', 'new_v_ln1_g': 'new_v', 'new_v_ln1_b': 'new_v', 'new_v_w_up': 'new_v', 'new_v_w_down': 'new_v', 'new_v_ln2_g': 'new_v', 'new_v_ln2_b': 'new_v'}


def _forward(args):
    return _fwd_reference(*[args[k] for k in FWD_PARAMS])


def _output_shape():
    out = _jax.eval_shape(lambda: _forward(_fwd_setup_inputs(0)))
    return out.shape, out.dtype

N_MICROBATCH = 1
ADAM_LR = 0.001
ADAM_B1 = 0.9
ADAM_B2 = 0.999
ADAM_EPS = 1e-08
ADAM_WD = 0.01
ADAM_STEP = 10
PER_EXAMPLE_BATCH_AXIS = {'x': 0, 'mem': 0, 'loss_target': 0}
SHARED_INPUTS = []
_WEIGHT_DTYPES = {'lower_bounds': _jnp.float32, 'w_in': _jnp.float32, 'conv_w': _jnp.float32, 'hg_norm_w': _jnp.float32, 'w_mem_k': _jnp.float32, 'w_mem_v': _jnp.float32, 'w_branch': _jnp.float32, 'b_gate': _jnp.float32, 'w_o': _jnp.float32, 'ln1_g': _jnp.float32, 'ln1_b': _jnp.float32, 'w_up': _jnp.float32, 'w_down': _jnp.float32, 'ln2_g': _jnp.float32, 'ln2_b': _jnp.float32}
MOMENT_SCALE = {'lower_bounds': 1.256659e-03, 'w_in': 1.079644e-02, 'conv_w': 2.060240e-02, 'hg_norm_w': 2.689008e-02, 'w_mem_k': 9.026327e-04, 'w_mem_v': 2.461037e-03, 'w_branch': 2.308792e-02, 'b_gate': 3.764250e-03, 'w_o': 3.996581e-02, 'ln1_g': 1.902345e+00, 'ln1_b': 7.795627e-01, 'w_up': 1.876511e-02, 'w_down': 4.418127e-02, 'ln2_g': 3.214767e+01, 'ln2_b': 1.782837e+00}


def _to_microbatches(a, axis):
    t = _jnp.moveaxis(a, axis, 0)
    t = t.reshape((N_MICROBATCH, t.shape[0] // N_MICROBATCH) + t.shape[1:])
    return _jnp.moveaxis(t, 1, axis + 1)


def setup_inputs(seed: int = 0) -> dict:
    inp = _fwd_setup_inputs(seed)
    key = _jax.random.fold_in(_jax.random.key(seed), 7919)
    shape, _ = _output_shape()
    out = dict(inp)
    out["loss_target"] = _jax.random.normal(_jax.random.fold_in(key, 0), shape, _jnp.float32)
    for i, name in enumerate(TWIN_WEIGHTS):
        w = inp[name].astype(_jnp.float32)
        if MOMENT_SCALE is None:
            s = _jnp.sqrt(_jnp.mean(_jnp.square(w)) + 1e-30)
        else:
            s = MOMENT_SCALE[name]
        km, kv = _jax.random.split(_jax.random.fold_in(key, i + 1))
        out[name] = w
        out["m_" + name] = s * _jax.random.normal(km, w.shape, _jnp.float32)
        out["v_" + name] = (s * s) * _jax.random.uniform(kv, w.shape, _jnp.float32, 0.5, 1.5)
    if N_MICROBATCH > 1:
        for name, axis in PER_EXAMPLE_BATCH_AXIS.items():
            out[name] = _to_microbatches(out[name], axis)
    return {'x': out['x'], 'mem': out['mem'], 'lower_bounds': out['lower_bounds'], 'w_in': out['w_in'], 'conv_w': out['conv_w'], 'hg_norm_w': out['hg_norm_w'], 'w_mem_k': out['w_mem_k'], 'w_mem_v': out['w_mem_v'], 'w_branch': out['w_branch'], 'b_gate': out['b_gate'], 'w_o': out['w_o'], 'ln1_g': out['ln1_g'], 'ln1_b': out['ln1_b'], 'w_up': out['w_up'], 'w_down': out['w_down'], 'ln2_g': out['ln2_g'], 'ln2_b': out['ln2_b'], 'loss_target': out['loss_target'], 'm_lower_bounds': out['m_lower_bounds'], 'm_w_in': out['m_w_in'], 'm_conv_w': out['m_conv_w'], 'm_hg_norm_w': out['m_hg_norm_w'], 'm_w_mem_k': out['m_w_mem_k'], 'm_w_mem_v': out['m_w_mem_v'], 'm_w_branch': out['m_w_branch'], 'm_b_gate': out['m_b_gate'], 'm_w_o': out['m_w_o'], 'm_ln1_g': out['m_ln1_g'], 'm_ln1_b': out['m_ln1_b'], 'm_w_up': out['m_w_up'], 'm_w_down': out['m_w_down'], 'm_ln2_g': out['m_ln2_g'], 'm_ln2_b': out['m_ln2_b'], 'v_lower_bounds': out['v_lower_bounds'], 'v_w_in': out['v_w_in'], 'v_conv_w': out['v_conv_w'], 'v_hg_norm_w': out['v_hg_norm_w'], 'v_w_mem_k': out['v_w_mem_k'], 'v_w_mem_v': out['v_w_mem_v'], 'v_w_branch': out['v_w_branch'], 'v_b_gate': out['v_b_gate'], 'v_w_o': out['v_w_o'], 'v_ln1_g': out['v_ln1_g'], 'v_ln1_b': out['v_ln1_b'], 'v_w_up': out['v_w_up'], 'v_w_down': out['v_w_down'], 'v_ln2_g': out['v_ln2_g'], 'v_ln2_b': out['v_ln2_b']}


def _loss(weights, diff, rest, loss_target):
    with _jax.named_scope("forward"):
        args = {**rest, TWIN_DIFF_INPUT: diff, **{k: w.astype(_WEIGHT_DTYPES[k]) for k, w in weights.items()}}
        y = _forward(args)
    with _jax.named_scope("loss_head"):
        err = _jnp.square(y.astype(_jnp.float32) - loss_target)
        return 0.5 * _jnp.sum(_jnp.mean(err, axis=-1)) if err.ndim else 0.5 * err


def _adamw(w, g, m, v):
    m = ADAM_B1 * m + (1.0 - ADAM_B1) * g
    v = ADAM_B2 * v + (1.0 - ADAM_B2) * _jnp.square(g)
    m_hat = m / (1.0 - ADAM_B1 ** ADAM_STEP)
    v_hat = v / (1.0 - ADAM_B2 ** ADAM_STEP)
    delta = -ADAM_LR * (m_hat / (_jnp.sqrt(v_hat) + ADAM_EPS) + ADAM_WD * w)
    return delta, m, v


def reference(x, mem, lower_bounds, w_in, conv_w, hg_norm_w, w_mem_k, w_mem_v, w_branch, b_gate, w_o, ln1_g, ln1_b, w_up, w_down, ln2_g, ln2_b, loss_target, m_lower_bounds, m_w_in, m_conv_w, m_hg_norm_w, m_w_mem_k, m_w_mem_v, m_w_branch, m_b_gate, m_w_o, m_ln1_g, m_ln1_b, m_w_up, m_w_down, m_ln2_g, m_ln2_b, v_lower_bounds, v_w_in, v_conv_w, v_hg_norm_w, v_w_mem_k, v_w_mem_v, v_w_branch, v_b_gate, v_w_o, v_ln1_g, v_ln1_b, v_w_up, v_w_down, v_ln2_g, v_ln2_b):
    given = dict(x=x, mem=mem, lower_bounds=lower_bounds, w_in=w_in, conv_w=conv_w, hg_norm_w=hg_norm_w, w_mem_k=w_mem_k, w_mem_v=w_mem_v, w_branch=w_branch, b_gate=b_gate, w_o=w_o, ln1_g=ln1_g, ln1_b=ln1_b, w_up=w_up, w_down=w_down, ln2_g=ln2_g, ln2_b=ln2_b, loss_target=loss_target, m_lower_bounds=m_lower_bounds, m_w_in=m_w_in, m_conv_w=m_conv_w, m_hg_norm_w=m_hg_norm_w, m_w_mem_k=m_w_mem_k, m_w_mem_v=m_w_mem_v, m_w_branch=m_w_branch, m_b_gate=m_b_gate, m_w_o=m_w_o, m_ln1_g=m_ln1_g, m_ln1_b=m_ln1_b, m_w_up=m_w_up, m_w_down=m_w_down, m_ln2_g=m_ln2_g, m_ln2_b=m_ln2_b, v_lower_bounds=v_lower_bounds, v_w_in=v_w_in, v_conv_w=v_conv_w, v_hg_norm_w=v_hg_norm_w, v_w_mem_k=v_w_mem_k, v_w_mem_v=v_w_mem_v, v_w_branch=v_w_branch, v_b_gate=v_b_gate, v_w_o=v_w_o, v_ln1_g=v_ln1_g, v_ln1_b=v_ln1_b, v_w_up=v_w_up, v_w_down=v_w_down, v_ln2_g=v_ln2_g, v_ln2_b=v_ln2_b)
    weights = {n: given[n] for n in TWIN_WEIGHTS}
    shared = {n: given[n] for n in SHARED_INPUTS}
    per_example = {n: given[n] for n in ['x', 'mem']}
    grad_fn = _jax.value_and_grad(_loss, argnums=(0, 1))

    def one_microbatch(ex, loss_target):
        ex = dict(ex)
        diff = ex.pop(TWIN_DIFF_INPUT)
        return grad_fn(weights, diff, {**shared, **ex}, loss_target)

    if N_MICROBATCH == 1:
        loss, (grad_w, grad_x) = one_microbatch(per_example, given["loss_target"])
    else:
        def body(carry, xs):
            loss_sum, grad_sum = carry
            l_k, (gw_k, gx_k) = one_microbatch(xs[0], xs[1])
            with _jax.named_scope("update"):
                return (loss_sum + l_k, _jax.tree.map(_jnp.add, grad_sum, gw_k)), gx_k

        init = (_jnp.zeros((), _jnp.float32), _jax.tree.map(_jnp.zeros_like, weights))
        (loss, grad_w), grad_x = _jax.lax.scan(body, init, (per_example, given["loss_target"]))
    with _jax.named_scope("update"):
        delta_w, new_m, new_v = {}, {}, {}
        for n in TWIN_WEIGHTS:
            delta_w[n], new_m[n], new_v[n] = _adamw(weights[n], grad_w[n], given["m_" + n], given["v_" + n])
    return (loss, grad_x, *[grad_w[n] for n in TWIN_WEIGHTS], *[delta_w[n] for n in TWIN_WEIGHTS],
            *[new_m[n] for n in TWIN_WEIGHTS], *[new_v[n] for n in TWIN_WEIGHTS])
```

```python
import functools

import jax
import jax.numpy as jnp
from jax import lax
from jax.experimental import pallas as pl
from jax.experimental.pallas import tpu as pltpu

F32 = jnp.float32
BF16 = jnp.bfloat16

HG_HEADS = 4
HG_F = 128
HG_CHUNK = 32
MEM_HEADS = 4
MEM_HEAD_DIM = 128
BRANCH_WIDTH = 512
N_BRANCH = 3
CONV_K = 3
LN_EPS = 1e-5
RMS_EPS = 1e-6
ADAM_LR = 0.001
ADAM_B1 = 0.9
ADAM_B2 = 0.999
ADAM_EPS = 1e-08
ADAM_WD = 0.01
ADAM_STEP = 10

VMEM_LIMIT = 48 * 1024 * 1024


def _cparams(sem):
    return pltpu.CompilerParams(dimension_semantics=sem, vmem_limit_bytes=VMEM_LIMIT)


def _dot(a, b, dims):
    return lax.dot_general(a, b, (dims, ((), ())), preferred_element_type=F32)


NN = ((1,), (0,))
NT = ((1,), (1,))
TN = ((0,), (0,))


def _pick(n, pref):
    for t in pref:
        if n % t == 0:
            return t
    return n


def _matmul(name, a, b, *, mode, out_dtype=F32, a_fn=None, a_extra=(), epi_fn=None, epi_extra=(), n_out=1,
            tm=512, tn=1024, tk=1024):
    M, K = a.shape
    N = b.shape[1] if mode == "nn" else b.shape[0]
    tm, tn, tk = _pick(M, (tm, 256, 128, 8)), _pick(N, (tn, 896, 512, 256, 128)), _pick(K, (tk, 512, 256, 128))
    nk = K // tk
    n_ax, n_ex = len(a_extra), len(epi_extra)
    out_dtypes = out_dtype if isinstance(out_dtype, (tuple, list)) else (out_dtype,) * n_out

    def body(*refs):
        a_ref, b_ref = refs[0], refs[1]
        ax_refs = refs[2:2 + n_ax]
        ex_refs = refs[2 + n_ax:2 + n_ax + n_ex]
        o_refs = refs[2 + n_ax + n_ex:2 + n_ax + n_ex + n_out]
        acc_ref = refs[-1]
        k = pl.program_id(2)
        at = a_ref[...]
        at = a_fn(at, *[r[...] for r in ax_refs]) if a_fn is not None else at.astype(BF16)
        part = _dot(at, b_ref[...].astype(BF16), NN if mode == "nn" else NT)

        @pl.when(k == 0)
        def _():
            acc_ref[...] = part

        @pl.when(k > 0)
        def _():
            acc_ref[...] += part

        @pl.when(k == nk - 1)
        def _():
            acc = acc_ref[...]
            outs = epi_fn(acc, *[r[...] for r in ex_refs]) if epi_fn is not None else (acc,)
            for o_ref, o in zip(o_refs, outs):
                o_ref[...] = o.astype(o_ref.dtype)

    in_specs = [pl.BlockSpec((tm, tk), lambda j, i, k: (i, k)),
                pl.BlockSpec((tk, tn), lambda j, i, k: (k, j)) if mode == "nn" else pl.BlockSpec((tn, tk), lambda j, i, k: (j, k))]
    in_specs += [pl.BlockSpec((1, tk), lambda j, i, k: (0, k)) for _ in a_extra]
    for e in epi_extra:
        if e.shape[0] == 1:
            in_specs.append(pl.BlockSpec((1, tn), lambda j, i, k: (0, j)))
        else:
            in_specs.append(pl.BlockSpec((tm, tn), lambda j, i, k: (i, j)))
    out = pl.pallas_call(
        body,
        name=name,
        grid=(N // tn, M // tm, nk),
        in_specs=in_specs,
        out_specs=[pl.BlockSpec((tm, tn), lambda j, i, k: (i, j)) for _ in range(n_out)],
        out_shape=[jax.ShapeDtypeStruct((M, N), dt) for dt in out_dtypes],
        scratch_shapes=[pltpu.VMEM((tm, tn), F32)],
        compiler_params=_cparams(("parallel", "parallel", "arbitrary")),
    )(a, b, *a_extra, *epi_extra)
    return out[0] if n_out == 1 else out


def _matmul_tn(name, a, b, *, a_fn=None, a_extra=(), a_cols=None, b_cols=None, ta=1024, tb=1024, tt=1024):
    T = a.shape[0]
    a0, Ka = a_cols if a_cols is not None else (0, a.shape[1])
    b0, Nb = b_cols if b_cols is not None else (0, b.shape[1])
    ta, tb, tt = _pick(Ka, (ta, 512, 256, 128)), _pick(Nb, (tb, 896, 512, 256, 128)), _pick(T, (tt, 512, 256, 128))
    assert a0 % ta == 0 and b0 % tb == 0
    a0, b0 = a0 // ta, b0 // tb
    nt = T // tt
    n_ax = len(a_extra)

    def body(*refs):
        a_ref, b_ref = refs[0], refs[1]
        ax_refs = refs[2:2 + n_ax]
        o_ref = refs[2 + n_ax]
        t = pl.program_id(2)
        at = a_ref[...]
        at = a_fn(at, *[r[...] for r in ax_refs]) if a_fn is not None else at.astype(BF16)
        part = _dot(at, b_ref[...].astype(BF16), TN)

        @pl.when(t == 0)
        def _():
            o_ref[...] = part

        @pl.when(t > 0)
        def _():
            o_ref[...] += part

    in_specs = [pl.BlockSpec((tt, ta), lambda i, j, t: (t, a0 + i)), pl.BlockSpec((tt, tb), lambda i, j, t: (t, b0 + j))]
    in_specs += [pl.BlockSpec((1, ta), lambda i, j, t: (0, a0 + i)) for _ in a_extra]
    return pl.pallas_call(
        body,
        name=name,
        grid=(Ka // ta, Nb // tb, nt),
        in_specs=in_specs,
        out_specs=pl.BlockSpec((ta, tb), lambda i, j, t: (i, j)),
        out_shape=jax.ShapeDtypeStruct((Ka, Nb), F32),
        compiler_params=_cparams(("parallel", "parallel", "arbitrary")),
    )(a, b, *a_extra)


W = BRANCH_WIDTH
C_CB, C_CC, C_CH, C_HQ, C_HF, C_HI, C_HG, C_MQ, N_MIX = 0, W, 2 * W, 3 * W, 4 * W, 5 * W, 6 * W, 7 * W, 8 * W
TS_MIX = 256


def _sigmoid(x):
    return jax.nn.sigmoid(x)


def _chunk_pos(shape):
    return lax.broadcasted_iota(jnp.int32, shape, 0) & (HG_CHUNK - 1)


def _seg_cumsum(x, pos):
    sh = 1
    while sh < HG_CHUNK:
        x = x + jnp.where(pos >= sh, pltpu.roll(x, sh, 0), 0.0)
        sh *= 2
    return x


def _seg_rev_cumsum(x, pos):
    n = x.shape[0]
    sh = 1
    while sh < HG_CHUNK:
        x = x + jnp.where(pos < HG_CHUNK - sh, pltpu.roll(x, n - sh, 0), 0.0)
        sh *= 2
    return x


def _chunk_mask(ts):
    r = lax.broadcasted_iota(jnp.int32, (ts, ts), 0)
    c = lax.broadcasted_iota(jnp.int32, (ts, ts), 1)
    return jnp.logical_and((r // HG_CHUNK) == (c // HG_CHUNK), c <= r)


def _hgrn_gates(p_ref, lb):
    q = p_ref[:, C_HQ:C_HQ + W]
    fl = p_ref[:, C_HF:C_HF + W]
    sig = _sigmoid(fl)
    f = lb + (1.0 - lb) * sig
    logf = jnp.log(f)
    k = (1.0 - lb) * _sigmoid(-fl)
    sq = _sigmoid(q)
    qs = q * sq
    return q, sq, qs, sig, f, logf, k


def _hgrn_decays(logf, bc_sc, ts):
    pos = _chunk_pos(logf.shape)
    bc = _seg_cumsum(logf, pos)
    bc_sc[...] = bc
    nc = ts // HG_CHUNK
    bref = jnp.concatenate(
        [jnp.broadcast_to(bc_sc[n * HG_CHUNK + HG_CHUNK // 2 - 1:n * HG_CHUNK + HG_CHUNK // 2, :], (HG_CHUNK, W)) for n in range(nc)], axis=0)
    blast = jnp.concatenate(
        [jnp.broadcast_to(bc_sc[(n + 1) * HG_CHUNK - 1:(n + 1) * HG_CHUNK, :], (HG_CHUNK, W)) for n in range(nc)], axis=0)
    return pos, bc, bref, blast


def _conv_shift_down(u, carry_ref, row):
    u1 = jnp.where(row == 0, carry_ref[7:8, :], pltpu.roll(u, 1, 0))
    u2 = jnp.where(row == 0, carry_ref[6:7, :], jnp.where(row == 1, carry_ref[7:8, :], pltpu.roll(u, 2, 0)))
    return u1, u2


def _attn_probs(qh, kh):
    s = _dot(qh, kh, NT) * (MEM_HEAD_DIM ** -0.5)
    e = jnp.exp(s - jnp.max(s, axis=-1, keepdims=True))
    return e / jnp.sum(e, axis=-1, keepdims=True)


def _mixer_fwd(p, mk, mv, lb, conv_w, norm_w, *, bl, seq):
    T = p.shape[0]
    ts = TS_MIX
    ns = seq // ts
    nc = ts // HG_CHUNK
    ml = mk.shape[0] // bl

    def body(p_ref, mk_ref, mv_ref, lb_ref, cw_ref, nw_ref, y_ref, st_ref, opre_ref, state_sc, carry_sc, bc_sc):
        @pl.when(pl.program_id(1) == 0)
        def _():
            state_sc[...] = jnp.zeros_like(state_sc)
            carry_sc[...] = jnp.zeros_like(carry_sc)

        cb, cc, ch = p_ref[:, C_CB:C_CB + W], p_ref[:, C_CC:C_CC + W], p_ref[:, C_CH:C_CH + W]
        u = cc * ch
        row = lax.broadcasted_iota(jnp.int32, (ts, W), 0)
        u1, u2 = _conv_shift_down(u, carry_sc, row)
        yconv = u2 * cw_ref[0:1, :] + u1 * cw_ref[1:2, :] + u * cw_ref[2:3, :]
        y_ref[:, 0:W] = (cb * yconv).astype(BF16)
        carry_sc[...] = u[ts - 8:ts, :]

        lbv = lb_ref[...]
        _, _, qs, _, _, logf, k = _hgrn_gates(p_ref, lbv)
        pos, bc, bref, blast = _hgrn_decays(logf, bc_sc, ts)
        a_all = (qs * jnp.exp(bc - bref)).astype(BF16)
        bk_all = (k * jnp.exp(bref - bc)).astype(BF16)
        qin_all = (qs * jnp.exp(bc)).astype(BF16)
        kout_all = k * jnp.exp(blast - bc)
        v_all = p_ref[:, C_HI:C_HI + W]
        mask = _chunk_mask(ts)
        chunk_of_row = lax.broadcasted_iota(jnp.int32, (ts, HG_F), 0) // HG_CHUNK
        for h in range(HG_HEADS):
            hs = slice(h * HG_F, (h + 1) * HG_F)
            vb = v_all[:, hs].astype(BF16)
            vt = v_all[:, hs].T.astype(BF16)
            scores = jnp.where(mask, _dot(a_all[:, hs], bk_all[:, hs], NT), 0.0)
            o_intra = _dot(scores.astype(BF16), vb, NN)
            kout = kout_all[:, hs]
            st = state_sc[h]
            o_inter = []
            for n in range(nc):
                st_ref[n, h] = st
                o_inter.append(_dot(qin_all[n * HG_CHUNK:(n + 1) * HG_CHUNK, hs], st.astype(BF16), NT))
                kv = _dot(vt, jnp.where(chunk_of_row == n, kout, 0.0).astype(BF16), NN)
                decay = jnp.exp(bc_sc[(n + 1) * HG_CHUNK - 1:(n + 1) * HG_CHUNK, hs])
                st = st * decay + kv
            state_sc[h] = st
            o = o_intra + jnp.concatenate(o_inter, axis=0)
            opre_ref[:, hs] = o
            on = o * lax.rsqrt(jnp.mean(o * o, axis=-1, keepdims=True) + RMS_EPS) * nw_ref[...]
            g = p_ref[:, C_HG + h * HG_F:C_HG + (h + 1) * HG_F]
            y_ref[:, W + h * HG_F:W + (h + 1) * HG_F] = (on * (g * _sigmoid(g))).astype(BF16)

        for h in range(MEM_HEADS):
            hs = slice(h * MEM_HEAD_DIM, (h + 1) * MEM_HEAD_DIM)
            qh = p_ref[:, C_MQ + h * MEM_HEAD_DIM:C_MQ + (h + 1) * MEM_HEAD_DIM].astype(BF16)
            prob = _attn_probs(qh, mk_ref[:, hs])
            y_ref[:, 2 * W + h * MEM_HEAD_DIM:2 * W + (h + 1) * MEM_HEAD_DIM] = _dot(prob.astype(BF16), mv_ref[:, hs], NN).astype(BF16)

    return pl.pallas_call(
        body,
        name="mixer_fwd",
        grid=(bl, ns),
        in_specs=[
            pl.BlockSpec((ts, N_MIX), lambda b, s: (b * ns + s, 0)),
            pl.BlockSpec((ml, W), lambda b, s: (b, 0)),
            pl.BlockSpec((ml, W), lambda b, s: (b, 0)),
            pl.BlockSpec((1, W), lambda b, s: (0, 0)),
            pl.BlockSpec((CONV_K, W), lambda b, s: (0, 0)),
            pl.BlockSpec((1, HG_F), lambda b, s: (0, 0)),
        ],
        out_specs=[
            pl.BlockSpec((ts, 3 * W), lambda b, s: (b * ns + s, 0)),
            pl.BlockSpec((nc, HG_HEADS, HG_F, HG_F), lambda b, s: (b * ns + s, 0, 0, 0)),
            pl.BlockSpec((ts, W), lambda b, s: (b * ns + s, 0)),
        ],
        out_shape=[
            jax.ShapeDtypeStruct((T, 3 * W), BF16),
            jax.ShapeDtypeStruct((T // HG_CHUNK, HG_HEADS, HG_F, HG_F), F32),
            jax.ShapeDtypeStruct((T, W), F32),
        ],
        scratch_shapes=[pltpu.VMEM((HG_HEADS, HG_F, HG_F), F32), pltpu.VMEM((8, W), F32), pltpu.VMEM((ts, W), F32)],
        compiler_params=_cparams(("arbitrary", "arbitrary")),
    )(p, mk, mv, lb, conv_w, norm_w)


def _mixer_bwd(p, dy, dp_gates, st, opre, mk, mv, lb, conv_w, norm_w, *, bl, seq):
    T, nin = p.shape
    ts = TS_MIX
    ns = seq // ts
    nc = ts // HG_CHUNK
    ml = mk.shape[0] // bl
    mid, last = HG_CHUNK // 2 - 1, HG_CHUNK - 1

    def body(p_ref, pprev_ref, dy_ref, dpin_ref, st_ref, opre_ref, mk_ref, mv_ref, lb_ref, cw_ref, nw_ref,
             dp_ref, dmk_ref, dmv_ref, dcw_ref, dnw_ref, dlb_ref, dstate_sc, carry_sc, uprev_sc, bc_sc):
        del dpin_ref
        b, s = pl.program_id(0), pl.program_id(1)

        @pl.when(s == 0)
        def _():
            dstate_sc[...] = jnp.zeros_like(dstate_sc)
            carry_sc[...] = jnp.zeros_like(carry_sc)
            dmk_ref[...] = jnp.zeros_like(dmk_ref)
            dmv_ref[...] = jnp.zeros_like(dmv_ref)

        @pl.when(jnp.logical_and(b == 0, s == 0))
        def _():
            dcw_ref[...] = jnp.zeros_like(dcw_ref)
            dnw_ref[...] = jnp.zeros_like(dnw_ref)
            dlb_ref[...] = jnp.zeros_like(dlb_ref)

        cb, cc, ch = p_ref[:, C_CB:C_CB + W], p_ref[:, C_CC:C_CC + W], p_ref[:, C_CH:C_CH + W]
        u = cc * ch
        row = lax.broadcasted_iota(jnp.int32, (ts, W), 0)
        uprev = pprev_ref[:, C_CC:C_CC + W] * pprev_ref[:, C_CH:C_CH + W]
        uprev_sc[...] = jnp.where(s == ns - 1, 0.0, uprev)
        u1, u2 = _conv_shift_down(u, uprev_sc, row)
        w0, w1, w2 = cw_ref[0:1, :], cw_ref[1:2, :], cw_ref[2:3, :]
        dya = dy_ref[:, 0:W]
        dp_ref[:, C_CB:C_CB + W] = (dya * (u2 * w0 + u1 * w1 + u * w2)).astype(BF16)
        dv = cb * dya
        dv1 = jnp.where(row == ts - 1, carry_sc[0:1, :], pltpu.roll(dv, ts - 1, 0))
        dv2 = jnp.where(row == ts - 1, carry_sc[1:2, :], jnp.where(row == ts - 2, carry_sc[0:1, :], pltpu.roll(dv, ts - 2, 0)))
        du = dv * w2 + dv1 * w1 + dv2 * w0
        dp_ref[:, C_CC:C_CC + W] = (du * ch).astype(BF16)
        dp_ref[:, C_CH:C_CH + W] = (du * cc).astype(BF16)
        dcw_ref[0:1, :] += jnp.sum(dv * u2, axis=0, keepdims=True)
        dcw_ref[1:2, :] += jnp.sum(dv * u1, axis=0, keepdims=True)
        dcw_ref[2:3, :] += jnp.sum(dv * u, axis=0, keepdims=True)
        carry_sc[...] = dv[0:8, :]

        lbv = lb_ref[...]
        q_all, sq_all, qs_all, sig_all, f_all, logf, k_all = _hgrn_gates(p_ref, lbv)
        pos_all, bc, bref, blast = _hgrn_decays(logf, bc_sc, ts)
        ea_all, eb_all, eq_all, ek_all = jnp.exp(bc - bref), jnp.exp(bref - bc), jnp.exp(bc), jnp.exp(blast - bc)
        mask = _chunk_mask(ts)
        chunk_of_row = lax.broadcasted_iota(jnp.int32, (ts, HG_F), 0) // HG_CHUNK
        pos = _chunk_pos((ts, HG_F))
        pos_c = _chunk_pos((HG_CHUNK, HG_F))
        nw = nw_ref[...]
        for h in range(HG_HEADS):
            hs = slice(h * HG_F, (h + 1) * HG_F)
            qs, k, ea, eb, eq, ek = qs_all[:, hs], k_all[:, hs], ea_all[:, hs], eb_all[:, hs], eq_all[:, hs], ek_all[:, hs]
            a, bk, qin, kout = qs * ea, k * eb, qs * eq, k * ek
            o = opre_ref[:, hs]
            g = p_ref[:, C_HG + h * HG_F:C_HG + (h + 1) * HG_F]
            sg = _sigmoid(g)
            r = lax.rsqrt(jnp.mean(o * o, axis=-1, keepdims=True) + RMS_EPS)
            dyb = dy_ref[:, W + h * HG_F:W + (h + 1) * HG_F]
            dp_ref[:, C_HG + h * HG_F:C_HG + (h + 1) * HG_F] = (dyb * (o * r * nw) * (sg * (1.0 + g * (1.0 - sg)))).astype(BF16)
            don = dyb * (g * sg)
            dnw_ref[0:1, :] += jnp.sum(don * o * r, axis=0, keepdims=True)
            dn = don * nw
            do = r * (dn - o * (r * r) * jnp.mean(dn * o, axis=-1, keepdims=True))
            dob = do.astype(BF16)
            dot_ = do.T.astype(BF16)
            vb = p_ref[:, C_HI + h * HG_F:C_HI + (h + 1) * HG_F].astype(BF16)
            ab, bkb = a.astype(BF16), bk.astype(BF16)
            scores = jnp.where(mask, _dot(ab, bkb, NT), 0.0)
            dscores = jnp.where(mask, _dot(dob, vb, NT), 0.0)
            dv_h = _dot(scores.T.astype(BF16), dob, NN)
            da = _dot(dscores.astype(BF16), bkb, NN)
            dbk = _dot(dscores.T.astype(BF16), ab, NN)
            koutb = kout.astype(BF16)
            dst = dstate_sc[h]
            dqin_p, dkout_p, dvi_p, ddec_p = [None] * nc, [None] * nc, [None] * nc, [None] * nc
            for n in reversed(range(nc)):
                rows = slice(n * HG_CHUNK, (n + 1) * HG_CHUNK)
                st_n = st_ref[n, h]
                decay = jnp.exp(bc_sc[n * HG_CHUNK + last:n * HG_CHUNK + last + 1, hs])
                dstb = dst.astype(BF16)
                dvi_p[n] = _dot(koutb[rows], dstb, NT)
                dkout_p[n] = _dot(vb[rows], dstb, NN)
                ddec_p[n] = jnp.sum(dst * st_n, axis=0, keepdims=True) * decay
                dqin_p[n] = _dot(dob[rows], st_n.astype(BF16), NN)
                dst = dst * decay + _dot(dot_, jnp.where(chunk_of_row == n, qin, 0.0).astype(BF16), NN)
            dstate_sc[h] = dst
            dqin = jnp.concatenate(dqin_p, axis=0)
            dkout = jnp.concatenate(dkout_p, axis=0)
            dp_ref[:, C_HI + h * HG_F:C_HI + (h + 1) * HG_F] = (dv_h + jnp.concatenate(dvi_p, axis=0)).astype(BF16)
            dqs = da * ea + dqin * eq
            dk = dbk * eb + dkout * ek
            t_a, t_b, t_q, t_k = da * a, dbk * bk, dqin * qin, dkout * kout
            dbc = t_a - t_b + t_q - t_k
            t_ref = t_b - t_a
            pieces = []
            for n in range(nc):
                rows = slice(n * HG_CHUNK, (n + 1) * HG_CHUNK)
                s_ref = jnp.sum(t_ref[rows], axis=0, keepdims=True)
                s_last = jnp.sum(t_k[rows], axis=0, keepdims=True) + ddec_p[n]
                pieces.append(dbc[rows] + jnp.where(pos_c == mid, s_ref, 0.0) + jnp.where(pos_c == last, s_last, 0.0))
            dlogf = _seg_rev_cumsum(jnp.concatenate(pieces, axis=0), pos)
            sig, lbh = sig_all[:, hs], lbv[:, hs]
            dfk = dlogf / f_all[:, hs] - dk
            dp_ref[:, C_HF + h * HG_F:C_HF + (h + 1) * HG_F] = (dfk * (1.0 - lbh) * sig * (1.0 - sig)).astype(BF16)
            dlb_ref[0:1, hs] += jnp.sum(dfk * (1.0 - sig), axis=0, keepdims=True)
            q, sq = q_all[:, hs], sq_all[:, hs]
            dp_ref[:, C_HQ + h * HG_F:C_HQ + (h + 1) * HG_F] = (dqs * (sq * (1.0 + q * (1.0 - sq)))).astype(BF16)

        for h in range(MEM_HEADS):
            hs = slice(h * MEM_HEAD_DIM, (h + 1) * MEM_HEAD_DIM)
            qh = p_ref[:, C_MQ + h * MEM_HEAD_DIM:C_MQ + (h + 1) * MEM_HEAD_DIM].astype(BF16)
            kh, vh = mk_ref[:, hs], mv_ref[:, hs]
            prob = _attn_probs(qh, kh)
            dob = dy_ref[:, 2 * W + h * MEM_HEAD_DIM:2 * W + (h + 1) * MEM_HEAD_DIM].astype(BF16)
            dmv_ref[:, hs] += _dot(prob.T.astype(BF16), dob, NN)
            dprob = _dot(dob, vh, NT)
            ds = prob * (dprob - jnp.sum(dprob * prob, axis=-1, keepdims=True)) * (MEM_HEAD_DIM ** -0.5)
            dp_ref[:, C_MQ + h * MEM_HEAD_DIM:C_MQ + (h + 1) * MEM_HEAD_DIM] = _dot(ds.astype(BF16), kh, NN).astype(BF16)
            dmk_ref[:, hs] += _dot(ds.T.astype(BF16), qh, NN)

    def tile(b, s):
        return b * ns + (ns - 1 - s)

    return pl.pallas_call(
        body,
        name="mixer_bwd",
        grid=(bl, ns),
        in_specs=[
            pl.BlockSpec((ts, N_MIX), lambda b, s: (tile(b, s), 0)),
            pl.BlockSpec((8, N_MIX), lambda b, s: (jnp.maximum(tile(b, s) * (ts // 8) - 1, 0), 0)),
            pl.BlockSpec((ts, 3 * W), lambda b, s: (tile(b, s), 0)),
            pl.BlockSpec(memory_space=pl.ANY),
            pl.BlockSpec((nc, HG_HEADS, HG_F, HG_F), lambda b, s: (tile(b, s), 0, 0, 0)),
            pl.BlockSpec((ts, W), lambda b, s: (tile(b, s), 0)),
            pl.BlockSpec((ml, W), lambda b, s: (b, 0)),
            pl.BlockSpec((ml, W), lambda b, s: (b, 0)),
            pl.BlockSpec((1, W), lambda b, s: (0, 0)),
            pl.BlockSpec((CONV_K, W), lambda b, s: (0, 0)),
            pl.BlockSpec((1, HG_F), lambda b, s: (0, 0)),
        ],
        out_specs=[
            pl.BlockSpec((ts, N_MIX), lambda b, s: (tile(b, s), 0)),
            pl.BlockSpec((ml, W), lambda b, s: (b, 0)),
            pl.BlockSpec((ml, W), lambda b, s: (b, 0)),
            pl.BlockSpec((8, W), lambda b, s: (0, 0)),
            pl.BlockSpec((8, HG_F), lambda b, s: (0, 0)),
            pl.BlockSpec((8, W), lambda b, s: (0, 0)),
        ],
        out_shape=[
            jax.ShapeDtypeStruct((T, nin), BF16),
            jax.ShapeDtypeStruct((bl * ml, W), F32),
            jax.ShapeDtypeStruct((bl * ml, W), F32),
            jax.ShapeDtypeStruct((8, W), F32),
            jax.ShapeDtypeStruct((8, HG_F), F32),
            jax.ShapeDtypeStruct((8, W), F32),
        ],
        input_output_aliases={3: 0},
        scratch_shapes=[pltpu.VMEM((HG_HEADS, HG_F, HG_F), F32), pltpu.VMEM((8, W), F32), pltpu.VMEM((8, W), F32),
                        pltpu.VMEM((ts, W), F32)],
        compiler_params=_cparams(("arbitrary", "arbitrary")),
    )(p, p, dy, dp_gates, st, opre, mk, mv, lb, conv_w, norm_w)


def _layer_norm_stats(z):
    mu = jnp.mean(z, axis=-1, keepdims=True)
    zc = z - mu
    rstd = lax.rsqrt(jnp.mean(zc * zc, axis=-1, keepdims=True) + LN_EPS)
    return zc * rstd, rstd


def _gate_specs(tm, d):
    g0 = N_MIX // d
    return [pl.BlockSpec((tm, d), functools.partial(lambda i, k: (i, g0 + k), k=k)) for k in range(N_BRANCH)]


def _merge_fwd(y, p, x0, wb, wo, bg, *, alpha, tm=256):
    T, d = x0.shape
    assert N_MIX % d == 0
    tm = _pick(T, (tm, 128, 8))

    def body(y_ref, g0_ref, g1_ref, g2_ref, x_ref, wb_ref, wo_ref, bg_ref, r_ref, mg_ref, xh_ref, rs_ref):
        merged = None
        for i, g_ref in enumerate((g0_ref, g1_ref, g2_ref)):
            r = _dot(y_ref[:, i * W:(i + 1) * W], wb_ref[i * W:(i + 1) * W, :], NN)
            r_ref[:, i * d:(i + 1) * d] = r
            t = _sigmoid(g_ref[...] + bg_ref[:, i * d:(i + 1) * d]) * r
            merged = t if merged is None else merged + t
        mb = merged.astype(BF16)
        mg_ref[...] = mb
        z = alpha * x_ref[...] + _dot(mb, wo_ref[...], NN)
        xh_ref[...], rs_ref[...] = _layer_norm_stats(z)

    row = lambda i: (i, 0)
    fix = lambda i: (0, 0)
    return pl.pallas_call(
        body,
        name="merge_fwd",
        grid=(T // tm,),
        in_specs=[pl.BlockSpec((tm, 3 * W), row)] + _gate_specs(tm, d) + [
            pl.BlockSpec((tm, d), row), pl.BlockSpec((3 * W, d), fix), pl.BlockSpec((d, d), fix), pl.BlockSpec((1, 3 * d), fix)],
        out_specs=[pl.BlockSpec((tm, 3 * d), row), pl.BlockSpec((tm, d), row), pl.BlockSpec((tm, d), row), pl.BlockSpec((tm, 1), row)],
        out_shape=[jax.ShapeDtypeStruct((T, 3 * d), F32), jax.ShapeDtypeStruct((T, d), BF16),
                   jax.ShapeDtypeStruct((T, d), F32), jax.ShapeDtypeStruct((T, 1), F32)],
        compiler_params=_cparams(("parallel",)),
    )(y, p, p, p, x0, wb, wo, bg)


def _merge_bwd(dz, p, r, wb, wo, bg, *, tm=256):
    T, d = dz.shape
    nin = p.shape[1]
    tm = _pick(T, (tm, 128, 8))

    def body(dz_ref, g0_ref, g1_ref, g2_ref, r_ref, wb_ref, wo_ref, bg_ref, dr_ref, dp_ref, dy_ref, dbg_ref):
        @pl.when(pl.program_id(0) == 0)
        def _():
            dbg_ref[...] = jnp.zeros_like(dbg_ref)

        dmerged = _dot(dz_ref[...].astype(BF16), wo_ref[...], NT)
        dp_ref[:, 0:N_MIX] = jnp.zeros((tm, N_MIX), BF16)
        for i, g_ref in enumerate((g0_ref, g1_ref, g2_ref)):
            cs = slice(i * d, (i + 1) * d)
            s = _sigmoid(g_ref[...] + bg_ref[:, cs])
            drb = (dmerged * s).astype(BF16)
            dr_ref[:, cs] = drb
            dgate = dmerged * r_ref[:, cs] * s * (1.0 - s)
            dp_ref[:, N_MIX + i * d:N_MIX + (i + 1) * d] = dgate.astype(BF16)
            dbg_ref[0:1, cs] += jnp.sum(dgate, axis=0, keepdims=True)
            dy_ref[:, i * W:(i + 1) * W] = _dot(drb, wb_ref[i * W:(i + 1) * W, :], NT)

    row = lambda i: (i, 0)
    fix = lambda i: (0, 0)
    return pl.pallas_call(
        body,
        name="merge_bwd",
        grid=(T // tm,),
        in_specs=[pl.BlockSpec((tm, d), row)] + _gate_specs(tm, d) + [
            pl.BlockSpec((tm, 3 * d), row), pl.BlockSpec((3 * W, d), fix), pl.BlockSpec((d, d), fix), pl.BlockSpec((1, 3 * d), fix)],
        out_specs=[pl.BlockSpec((tm, 3 * d), row), pl.BlockSpec((tm, nin), row), pl.BlockSpec((tm, 3 * W), row),
                   pl.BlockSpec((8, 3 * d), fix)],
        out_shape=[jax.ShapeDtypeStruct((T, 3 * d), BF16), jax.ShapeDtypeStruct((T, nin), BF16),
                   jax.ShapeDtypeStruct((T, 3 * W), F32), jax.ShapeDtypeStruct((8, 3 * d), F32)],
        compiler_params=_cparams(("arbitrary",)),
    )(dz, p, p, p, r, wb, wo, bg)


def _mlp_fwd(xhat1, g1, b1, wu, wd, g2, b2, *, alpha, tm=512, tf=1024):
    T, d = xhat1.shape
    ff = wu.shape[1]
    tm, tf = _pick(T, (tm, 256, 128, 8)), _pick(ff, (tf, 512, 256, 128))
    nf = ff // tf

    def body(xh_ref, g1_ref, b1_ref, wu_ref, wd_ref, g2_ref, b2_ref, a_ref, xh2_ref, rs2_ref, x2_ref, acc_ref):
        f = pl.program_id(1)
        x1 = xh_ref[...] * g1_ref[...] + b1_ref[...]
        a = _dot(x1.astype(BF16), wu_ref[...], NN)
        a_ref[...] = a.astype(BF16)
        h = jnp.square(jnp.maximum(a, 0.0))
        part = _dot(h.astype(BF16), wd_ref[...], NN)

        @pl.when(f == 0)
        def _():
            acc_ref[...] = part

        @pl.when(f > 0)
        def _():
            acc_ref[...] += part

        @pl.when(f == nf - 1)
        def _():
            xh2, rs2 = _layer_norm_stats(alpha * x1 + acc_ref[...])
            xh2_ref[...] = xh2
            rs2_ref[...] = rs2
            x2_ref[...] = xh2 * g2_ref[...] + b2_ref[...]

    row = lambda i, f: (i, 0)
    fix = lambda i, f: (0, 0)
    return pl.pallas_call(
        body,
        name="mlp_fwd",
        grid=(T // tm, nf),
        in_specs=[pl.BlockSpec((tm, d), row), pl.BlockSpec((1, d), fix), pl.BlockSpec((1, d), fix),
                  pl.BlockSpec((d, tf), lambda i, f: (0, f)), pl.BlockSpec((tf, d), lambda i, f: (f, 0)),
                  pl.BlockSpec((1, d), fix), pl.BlockSpec((1, d), fix)],
        out_specs=[pl.BlockSpec((tm, tf), lambda i, f: (i, f)), pl.BlockSpec((tm, d), row), pl.BlockSpec((tm, 1), row),
                   pl.BlockSpec((tm, d), row)],
        out_shape=[jax.ShapeDtypeStruct((T, ff), BF16), jax.ShapeDtypeStruct((T, d), F32), jax.ShapeDtypeStruct((T, 1), F32),
                   jax.ShapeDtypeStruct((T, d), F32)],
        scratch_shapes=[pltpu.VMEM((tm, d), F32)],
        compiler_params=_cparams(("parallel", "arbitrary")),
    )(xhat1, g1, b1, wu, wd, g2, b2)


def _ln_bwd(dy, xhat, rstd, g, *, tm=512):
    T, d = dy.shape
    tm = _pick(T, (tm, 256, 128, 8))

    def body(dy_ref, xh_ref, rs_ref, g_ref, dz_ref, dg_ref, db_ref):
        @pl.when(pl.program_id(0) == 0)
        def _():
            dg_ref[...] = jnp.zeros_like(dg_ref)
            db_ref[...] = jnp.zeros_like(db_ref)

        dy_, xh = dy_ref[...], xh_ref[...]
        dg_ref[0:1, :] += jnp.sum(dy_ * xh, axis=0, keepdims=True)
        db_ref[0:1, :] += jnp.sum(dy_, axis=0, keepdims=True)
        dxh = dy_ * g_ref[...]
        dz_ref[...] = rs_ref[...] * (dxh - jnp.mean(dxh, axis=-1, keepdims=True) - xh * jnp.mean(dxh * xh, axis=-1, keepdims=True))

    row = lambda i: (i, 0)
    fix = lambda i: (0, 0)
    return pl.pallas_call(
        body,
        name="ln_bwd",
        grid=(T // tm,),
        in_specs=[pl.BlockSpec((tm, d), row), pl.BlockSpec((tm, d), row), pl.BlockSpec((tm, 1), row), pl.BlockSpec((1, d), fix)],
        out_specs=[pl.BlockSpec((tm, d), row), pl.BlockSpec((8, d), fix), pl.BlockSpec((8, d), fix)],
        out_shape=[jax.ShapeDtypeStruct((T, d), F32), jax.ShapeDtypeStruct((8, d), F32), jax.ShapeDtypeStruct((8, d), F32)],
        compiler_params=_cparams(("arbitrary",)),
    )(dy, xhat, rstd, g)


def _loss_head(y, target, *, tm=512):
    T, d = y.shape
    tm = _pick(T, (tm, 256, 128, 8))
    n = T // tm

    def body(y_ref, t_ref, loss_ref, dy_ref, acc_ref):
        i = pl.program_id(0)

        @pl.when(i == 0)
        def _():
            acc_ref[...] = jnp.zeros_like(acc_ref)

        e = y_ref[...] - t_ref[...]
        dy_ref[...] = e * (1.0 / d)
        acc_ref[...] += jnp.sum(e * e, axis=0, keepdims=True)

        @pl.when(i == n - 1)
        def _():
            loss_ref[...] = (0.5 / d) * jnp.sum(acc_ref[...], axis=1, keepdims=True)

    row = lambda i: (i, 0)
    return pl.pallas_call(
        body,
        name="loss_head",
        grid=(n,),
        in_specs=[pl.BlockSpec((tm, d), row), pl.BlockSpec((tm, d), row)],
        out_specs=[pl.BlockSpec((1, 1), lambda i: (0, 0)), pl.BlockSpec((tm, d), row)],
        out_shape=[jax.ShapeDtypeStruct((1, 1), F32), jax.ShapeDtypeStruct((T, d), F32)],
        scratch_shapes=[pltpu.VMEM((1, d), F32)],
        compiler_params=_cparams(("arbitrary",)),
    )(y, target)


def _lower_bounds_fwd(lower_bounds):
    depth, n = lower_bounds.shape

    def body(x_ref, soft_ref, lb_ref):
        x = x_ref[...]
        e = jnp.exp(x - jnp.max(x, axis=0, keepdims=True))
        soft_ref[...] = e / jnp.sum(e, axis=0, keepdims=True)
        run = None
        for l in range(depth):
            run = soft_ref[l:l + 1, :] if run is None else run + soft_ref[l:l + 1, :]
            lb_ref[l:l + 1, :] = run - soft_ref[0:1, :]

    return pl.pallas_call(body, name="lower_bounds_fwd",
                          out_shape=[jax.ShapeDtypeStruct((depth, n), F32), jax.ShapeDtypeStruct((depth, n), F32)])(lower_bounds)


def _lower_bounds_bwd(soft, dlb):
    depth, n = soft.shape

    def body(soft_ref, dlb_ref, out_ref, dsoft_ref):
        total = jnp.sum(dlb_ref[...], axis=0, keepdims=True)
        run = None
        for l in reversed(range(depth)):
            run = dlb_ref[l:l + 1, :] if run is None else run + dlb_ref[l:l + 1, :]
            dsoft_ref[l:l + 1, :] = run - total if l == 0 else run
        s, ds = soft_ref[...], dsoft_ref[...]
        out_ref[...] = s * (ds - jnp.sum(s * ds, axis=0, keepdims=True))

    return pl.pallas_call(body, name="lower_bounds_bwd", out_shape=jax.ShapeDtypeStruct((depth, n), F32),
                          scratch_shapes=[pltpu.VMEM((depth, n), F32)])(soft, dlb)


def _layer_fwd(x0, mem2, lb, wts, *, bl, seq, alpha):
    p = _matmul("proj_in", x0, wts["w_in"], mode="nn")
    mk = _matmul("mem_k", mem2, wts["w_mem_k"], mode="nn", out_dtype=BF16)
    mv = _matmul("mem_v", mem2, wts["w_mem_v"], mode="nn", out_dtype=BF16)
    y, st, opre = _mixer_fwd(p, mk, mv, lb, wts["conv_w"], wts["hg_norm_w"], bl=bl, seq=seq)
    r, merged, xhat1, rstd1 = _merge_fwd(y, p, x0, wts["w_branch"], wts["w_o"], wts["b_gate"], alpha=alpha)
    a, xhat2, rstd2, x2 = _mlp_fwd(xhat1, wts["ln1_g"], wts["ln1_b"], wts["w_up"], wts["w_down"], wts["ln2_g"], wts["ln2_b"],
                                   alpha=alpha)
    saved = dict(x0=x0, p=p, mk=mk, mv=mv, y=y, st=st, opre=opre, r=r, merged=merged, xhat1=xhat1, rstd1=rstd1, a=a,
                 xhat2=xhat2, rstd2=rstd2)
    return x2, saved


def _relu2_bf16(a):
    return jnp.square(jnp.maximum(a.astype(F32), 0.0)).astype(BF16)


def _affine_bf16(xh, g, b):
    return (xh * g + b).astype(BF16)


def _layer_bwd(dx2, sv, mem2, lb, wts, *, bl, seq, alpha):
    d = dx2.shape[1]
    g = {}
    dz2, dg2, db2 = _ln_bwd(dx2, sv["xhat2"], sv["rstd2"], wts["ln2_g"])
    g["ln2_g"], g["ln2_b"] = dg2[0:1], db2[0:1]
    da = _matmul("mlp_da", dz2, wts["w_down"], mode="nt", out_dtype=BF16,
                 epi_fn=lambda acc, a: (acc * (2.0 * jnp.maximum(a.astype(F32), 0.0)),), epi_extra=(sv["a"],))
    g["w_down"] = _matmul_tn("grad_w_down", sv["a"], dz2, a_fn=_relu2_bf16)
    g["w_up"] = _matmul_tn("grad_w_up", sv["xhat1"], da, a_fn=_affine_bf16, a_extra=(wts["ln1_g"], wts["ln1_b"]))
    dx1 = _matmul("mlp_dx", da, wts["w_up"], mode="nt", epi_fn=lambda acc, dz: (acc + alpha * dz,), epi_extra=(dz2,))
    dz1, dg1, db1 = _ln_bwd(dx1, sv["xhat1"], sv["rstd1"], wts["ln1_g"])
    g["ln1_g"], g["ln1_b"] = dg1[0:1], db1[0:1]
    g["w_o"] = _matmul_tn("grad_w_o", sv["merged"], dz1)
    dr, dp, dy, dbg = _merge_bwd(dz1, sv["p"], sv["r"], wts["w_branch"], wts["w_o"], wts["b_gate"])
    g["b_gate"] = dbg[0:1]
    g["w_branch"] = jnp.concatenate(
        [_matmul_tn("grad_w_branch", sv["y"], dr, a_cols=(i * W, W), b_cols=(i * d, d)) for i in range(N_BRANCH)], axis=0)
    dp, dmk, dmv, dcw, dnw, dlb = _mixer_bwd(sv["p"], dy, dp, sv["st"], sv["opre"], sv["mk"], sv["mv"], lb,
                                              wts["conv_w"], wts["hg_norm_w"], bl=bl, seq=seq)
    g["conv_w"], g["hg_norm_w"], g["lb"] = dcw[0:CONV_K], dnw[0:1], dlb[0:1]
    g["w_mem_k"] = _matmul_tn("grad_w_mem_k", mem2, dmk)
    g["w_mem_v"] = _matmul_tn("grad_w_mem_v", mem2, dmv)
    g["w_in"] = _matmul_tn("grad_w_in", sv["x0"], dp)
    dx0 = _matmul("proj_in_dx", dp, wts["w_in"], mode="nt", epi_fn=lambda acc, dz: (acc + alpha * dz,), epi_extra=(dz1,))
    return dx0, g


N_CHIPS = 4
MESH_IDS = pl.DeviceIdType.MESH


def _axis_slice(ref, axis, start, size):
    idx = [slice(None)] * len(ref.shape)
    idx[axis] = pl.ds(start, size)
    return ref.at[tuple(idx)]


def _chip_exchange(name, items):
    n = len(items)
    out_shapes, meta = [], []
    for arr, kind, axis in items:
        shp = list(arr.shape)
        if kind == "gather":
            per = shp[axis]
            shp[axis] = per * N_CHIPS
            out_shapes.append(jax.ShapeDtypeStruct(tuple(shp), arr.dtype))
        elif kind == "scatter":
            per = shp[axis] // N_CHIPS
            shp[axis] = per
            out_shapes.append(jax.ShapeDtypeStruct((N_CHIPS, *shp), arr.dtype))
        else:
            per = None
            out_shapes.append(jax.ShapeDtypeStruct((N_CHIPS, *shp), arr.dtype))
        meta.append((kind, axis, per))

    def body(*refs):
        ins, outs = refs[:n], refs[n:2 * n]
        send_sems, recv_sems, local_sems = refs[2 * n:]
        x, y, c = lax.axis_index("x"), lax.axis_index("y"), lax.axis_index("c")
        me = 2 * x + y
        peers = [(1 - x, y), (x, 1 - y), (1 - x, 1 - y)]

        def src_for(t, chip):
            kind, axis, per = meta[t]
            return _axis_slice(ins[t], axis, chip * per, per) if kind == "scatter" else ins[t]

        def dst_from(t, chip):
            kind, axis, per = meta[t]
            return _axis_slice(outs[t], axis, chip * per, per) if kind == "gather" else outs[t].at[chip]

        def remote(t, k):
            px, py = peers[k]
            return pltpu.make_async_remote_copy(
                src_ref=src_for(t, 2 * px + py), dst_ref=dst_from(t, me), send_sem=send_sems.at[t * 3 + k],
                recv_sem=recv_sems.at[t * 3 + k], device_id=(px, py, c), device_id_type=MESH_IDS)

        def arrival(t, k):
            px, py = peers[k]
            return pltpu.make_async_remote_copy(
                src_ref=src_for(t, me), dst_ref=dst_from(t, 2 * px + py), send_sem=send_sems.at[t * 3 + k],
                recv_sem=recv_sems.at[t * 3 + k], device_id=(px, py, c), device_id_type=MESH_IDS)

        sends = [remote(t, k) for t in range(n) for k in range(3)]
        for cp in sends:
            cp.start()
        own = [pltpu.make_async_copy(src_for(t, me), dst_from(t, me), local_sems.at[t]) for t in range(n)]
        for cp in own:
            cp.start()
        for t in range(n):
            for k in range(3):
                arrival(t, k).wait_recv()
        for cp in sends:
            cp.wait_send()
        for cp in own:
            cp.wait()

    any_spec = pl.BlockSpec(memory_space=pl.ANY)
    return pl.pallas_call(
        body,
        name=name,
        in_specs=[any_spec] * n,
        out_specs=[any_spec] * n,
        out_shape=out_shapes,
        scratch_shapes=[pltpu.SemaphoreType.DMA((3 * n,)), pltpu.SemaphoreType.DMA((3 * n,)), pltpu.SemaphoreType.DMA((n,))],
        compiler_params=pltpu.CompilerParams(has_side_effects=True),
    )(*[a for a, _, _ in items])


def _sibling_swap(name, arrays):
    n = len(arrays)

    def body(*refs):
        ins, outs = refs[:n], refs[n:2 * n]
        send_sems, recv_sems = refs[2 * n:]
        sibling = (lax.axis_index("x"), lax.axis_index("y"), 1 - lax.axis_index("c"))
        copies = [pltpu.make_async_remote_copy(src_ref=ins[t], dst_ref=outs[t], send_sem=send_sems.at[t], recv_sem=recv_sems.at[t],
                                               device_id=sibling, device_id_type=MESH_IDS) for t in range(n)]
        for cp in copies:
            cp.start()
        for cp in copies:
            cp.wait()

    any_spec = pl.BlockSpec(memory_space=pl.ANY)
    return pl.pallas_call(
        body,
        name=name,
        in_specs=[any_spec] * n,
        out_specs=[any_spec] * n,
        out_shape=[jax.ShapeDtypeStruct(a.shape, a.dtype) for a in arrays],
        scratch_shapes=[pltpu.SemaphoreType.DMA((n,)), pltpu.SemaphoreType.DMA((n,))],
        compiler_params=pltpu.CompilerParams(has_side_effects=True),
    )(*arrays)


def _row_block(r, c, itemsize=4, target=1 << 20):
    if r % 8 != 0:
        return r
    best = 8
    for tr in range(8, r + 1, 8):
        if r % tr == 0 and tr * c * itemsize <= target:
            best = tr
    return best


def _sum_chips_into(parts, stacked, layer):
    _, r, c = parts.shape
    tr = _row_block(r, c)

    def body(p_ref, s_ref, o_ref):
        del s_ref
        o_ref[...] = ((p_ref[0] + p_ref[1]) + p_ref[2]) + p_ref[3]

    return pl.pallas_call(
        body,
        name="sum_chips",
        grid=(r // tr,),
        in_specs=[pl.BlockSpec((N_CHIPS, tr, c), lambda i: (0, i, 0)), pl.BlockSpec(memory_space=pl.ANY)],
        out_specs=pl.BlockSpec((None, tr, c), lambda i: (layer, i, 0)),
        out_shape=jax.ShapeDtypeStruct(stacked.shape, stacked.dtype),
        input_output_aliases={1: 0},
        compiler_params=_cparams(("parallel",)),
    )(parts, stacked)


def _adamw(w, m, v, g_a, g_b):
    L, r, c = w.shape
    tr = _row_block(r, c, target=1 << 19)

    def body(w_ref, m_ref, v_ref, ga_ref, gb_ref, g_ref, d_ref, nm_ref, nv_ref):
        g = ga_ref[...] + gb_ref[...]
        g_ref[...] = g
        m_new = ADAM_B1 * m_ref[...] + (1.0 - ADAM_B1) * g
        v_new = ADAM_B2 * v_ref[...] + (1.0 - ADAM_B2) * jnp.square(g)
        nm_ref[...] = m_new
        nv_ref[...] = v_new
        m_hat = m_new / (1.0 - ADAM_B1 ** ADAM_STEP)
        v_hat = v_new / (1.0 - ADAM_B2 ** ADAM_STEP)
        d_ref[...] = -ADAM_LR * (m_hat / (jnp.sqrt(v_hat) + ADAM_EPS) + ADAM_WD * w_ref[...])

    spec = pl.BlockSpec((None, tr, c), lambda l, i: (l, i, 0))
    return pl.pallas_call(
        body,
        name="adamw",
        grid=(L, r // tr),
        in_specs=[spec] * 5,
        out_specs=[spec] * 4,
        out_shape=[jax.ShapeDtypeStruct(w.shape, F32)] * 4,
        compiler_params=_cparams(("parallel", "parallel")),
    )(w, m, v, g_a, g_b)


SHARDED = (("w_in", 1), ("conv_w", 1), ("w_mem_k", 0), ("w_mem_v", 0), ("w_branch", 1), ("w_o", 0), ("w_up", 1), ("w_down", 0))
SMALL = ("lower_bounds", "hg_norm_w", "b_gate", "ln1_g", "ln1_b", "ln2_g", "ln2_b")
WEIGHT_ORDER = ("lower_bounds", "w_in", "conv_w", "hg_norm_w", "w_mem_k", "w_mem_v", "w_branch", "b_gate", "w_o", "ln1_g", "ln1_b",
                "w_up", "w_down", "ln2_g", "ln2_b")


def kernel(x, mem, lower_bounds, w_in, conv_w, hg_norm_w, w_mem_k, w_mem_v, w_branch, b_gate, w_o, ln1_g, ln1_b, w_up, w_down, ln2_g, ln2_b, loss_target, m_lower_bounds, m_w_in, m_conv_w, m_hg_norm_w, m_w_mem_k, m_w_mem_v, m_w_branch, m_b_gate, m_w_o, m_ln1_g, m_ln1_b, m_w_up, m_w_down, m_ln2_g, m_ln2_b, v_lower_bounds, v_w_in, v_conv_w, v_hg_norm_w, v_w_mem_k, v_w_mem_v, v_w_branch, v_b_gate, v_w_o, v_ln1_g, v_ln1_b, v_w_up, v_w_down, v_ln2_g, v_ln2_b):
    bl, seq, d = x.shape
    depth = w_in.shape[0]
    weights = dict(lower_bounds=lower_bounds, w_in=w_in, conv_w=conv_w, hg_norm_w=hg_norm_w, w_mem_k=w_mem_k, w_mem_v=w_mem_v,
                   w_branch=w_branch, b_gate=b_gate, w_o=w_o, ln1_g=ln1_g, ln1_b=ln1_b, w_up=w_up, w_down=w_down, ln2_g=ln2_g, ln2_b=ln2_b)
    mom_m = dict(lower_bounds=m_lower_bounds, w_in=m_w_in, conv_w=m_conv_w, hg_norm_w=m_hg_norm_w, w_mem_k=m_w_mem_k, w_mem_v=m_w_mem_v,
                 w_branch=m_w_branch, b_gate=m_b_gate, w_o=m_w_o, ln1_g=m_ln1_g, ln1_b=m_ln1_b, w_up=m_w_up, w_down=m_w_down,
                 ln2_g=m_ln2_g, ln2_b=m_ln2_b)
    mom_v = dict(lower_bounds=v_lower_bounds, w_in=v_w_in, conv_w=v_conv_w, hg_norm_w=v_hg_norm_w, w_mem_k=v_w_mem_k, w_mem_v=v_w_mem_v,
                 w_branch=v_w_branch, b_gate=v_b_gate, w_o=v_w_o, ln1_g=v_ln1_g, ln1_b=v_ln1_b, w_up=v_w_up, w_down=v_w_down,
                 ln2_g=v_ln2_g, ln2_b=v_ln2_b)

    def shard2d(name, l):
        w = weights[name][l]
        if name == "w_branch":
            return w.reshape(N_BRANCH * W, w.shape[-1]).astype(BF16)
        return w if name == "conv_w" else w.astype(BF16)

    full = {name: [] for name, _ in SHARDED}
    for l in range(depth):
        got = _chip_exchange("gather_weights", [(shard2d(name, l), "gather", axis) for name, axis in SHARDED])
        for (name, _), arr in zip(SHARDED, got):
            full[name].append(arr)
    for name in ("hg_norm_w", "b_gate", "ln1_g", "ln1_b", "ln2_g", "ln2_b"):
        full[name] = [weights[name][l][None, :] for l in range(depth)]

    x2d, mem2, t2d = x.reshape(bl * seq, d), mem.reshape(-1, d), loss_target.reshape(bl * seq, d)
    alpha = (2.0 * depth) ** 0.25
    soft, lb_all = _lower_bounds_fwd(lower_bounds)
    h, saved = x2d, []
    for l in range(depth):
        wts = {k: v[l] for k, v in full.items()}
        h, sv = _layer_fwd(h, mem2, lb_all[l:l + 1], wts, bl=bl, seq=seq, alpha=alpha)
        saved.append(sv)
    loss, dh = _loss_head(h, t2d)

    small_width = sum(weights[nm].shape[1] for nm in SMALL)
    partial = None
    for l in reversed(range(depth)):
        wts = {k: v[l] for k, v in full.items()}
        dh, g = _layer_bwd(dh, saved[l], mem2, lb_all[l:l + 1], wts, bl=bl, seq=seq, alpha=alpha)
        g["lower_bounds"] = g.pop("lb")
        small = jnp.concatenate([g[nm] for nm in SMALL], axis=1)
        got = _chip_exchange("reduce_grads", [(g[name], "scatter", axis) for name, axis in SHARDED] + [(small, "bcast", 0)])
        if partial is None:
            partial = [jnp.zeros((depth,) + p.shape[1:], F32) for p in got]
        partial = [_sum_chips_into(p, acc, l) for p, acc in zip(got, partial)]
    theirs = _sibling_swap("swap_partials", partial)

    outs = {}
    for i, (name, _) in enumerate(SHARDED):
        shape3 = partial[i].shape
        res = _adamw(weights[name].reshape(shape3), mom_m[name].reshape(shape3), mom_v[name].reshape(shape3), partial[i], theirs[i])
        outs[name] = [r.reshape(weights[name].shape) for r in res]
    off = 0
    for name in SMALL:
        n = weights[name].shape[1]
        mine, other = partial[-1][:, :, off:off + n], theirs[-1][:, :, off:off + n]
        off += n
        if name == "lower_bounds":
            mine = _lower_bounds_bwd(soft, mine[:, 0, :])[:, None, :]
            other = _lower_bounds_bwd(soft, other[:, 0, :])[:, None, :]
        shape3 = (depth, 1, n)
        res = _adamw(weights[name].reshape(shape3), mom_m[name].reshape(shape3), mom_v[name].reshape(shape3), mine, other)
        outs[name] = [r.reshape(weights[name].shape) for r in res]
    assert off == small_width

    total_loss = lax.psum(loss[0, 0], ("x", "y", "c"))
    result = [total_loss, dh.reshape(bl, seq, d)]
    for k in range(4):
        result += [outs[name][k] for name in WEIGHT_ORDER]
    return tuple(result)
```

```python
import functools

import jax
import jax.numpy as jnp
from jax import lax
from jax.experimental import pallas as pl
from jax.experimental.pallas import tpu as pltpu

F32 = jnp.float32
BF16 = jnp.bfloat16

HG_HEADS = 4
HG_F = 128
HG_CHUNK = 32
MEM_HEADS = 4
MEM_HEAD_DIM = 128
BRANCH_WIDTH = 512
N_BRANCH = 3
CONV_K = 3
LN_EPS = 1e-5
RMS_EPS = 1e-6
ADAM_LR = 0.001
ADAM_B1 = 0.9
ADAM_B2 = 0.999
ADAM_EPS = 1e-08
ADAM_WD = 0.01
ADAM_STEP = 10

VMEM_LIMIT = 48 * 1024 * 1024


def _cparams(sem):
    return pltpu.CompilerParams(dimension_semantics=sem, vmem_limit_bytes=VMEM_LIMIT)


def _dot(a, b, dims):
    return lax.dot_general(a, b, (dims, ((), ())), preferred_element_type=F32)


NN = ((1,), (0,))
NT = ((1,), (1,))
TN = ((0,), (0,))


def _pick(n, pref):
    for t in pref:
        if n % t == 0:
            return t
    return n


ANY_SPEC = pl.BlockSpec(memory_space=pl.ANY)


def _matmul(name, a, b, *, mode, out_dtype=F32, a_fn=None, a_extra=(), epi_fn=None, epi_extra=(), n_out=1,
            tm=512, tn=1024, tk=1024, deps=()):
    M, K = a.shape
    N = b.shape[1] if mode == "nn" else b.shape[0]
    tm, tn, tk = _pick(M, (tm, 256, 128, 8)), _pick(N, (tn, 896, 512, 256, 128)), _pick(K, (tk, 512, 256, 128))
    nk = K // tk
    n_ax, n_ex = len(a_extra), len(epi_extra)
    n_in = 2 + n_ax + n_ex + len(deps)
    out_dtypes = out_dtype if isinstance(out_dtype, (tuple, list)) else (out_dtype,) * n_out

    def body(*refs):
        a_ref, b_ref = refs[0], refs[1]
        ax_refs = refs[2:2 + n_ax]
        ex_refs = refs[2 + n_ax:2 + n_ax + n_ex]
        o_refs = refs[n_in:n_in + n_out]
        acc_ref = refs[-1]
        k = pl.program_id(2)
        at = a_ref[...]
        at = a_fn(at, *[r[...] for r in ax_refs]) if a_fn is not None else at.astype(BF16)
        part = _dot(at, b_ref[...].astype(BF16), NN if mode == "nn" else NT)

        @pl.when(k == 0)
        def _():
            acc_ref[...] = part

        @pl.when(k > 0)
        def _():
            acc_ref[...] += part

        @pl.when(k == nk - 1)
        def _():
            acc = acc_ref[...]
            outs = epi_fn(acc, *[r[...] for r in ex_refs]) if epi_fn is not None else (acc,)
            for o_ref, o in zip(o_refs, outs):
                o_ref[...] = o.astype(o_ref.dtype)

    in_specs = [pl.BlockSpec((tm, tk), lambda j, i, k: (i, k)),
                pl.BlockSpec((tk, tn), lambda j, i, k: (k, j)) if mode == "nn" else pl.BlockSpec((tn, tk), lambda j, i, k: (j, k))]
    in_specs += [pl.BlockSpec((1, tk), lambda j, i, k: (0, k)) for _ in a_extra]
    for e in epi_extra:
        if e.shape[0] == 1:
            in_specs.append(pl.BlockSpec((1, tn), lambda j, i, k: (0, j)))
        else:
            in_specs.append(pl.BlockSpec((tm, tn), lambda j, i, k: (i, j)))
    in_specs += [ANY_SPEC] * len(deps)
    out = pl.pallas_call(
        body,
        name=name,
        grid=(N // tn, M // tm, nk),
        in_specs=in_specs,
        out_specs=[pl.BlockSpec((tm, tn), lambda j, i, k: (i, j)) for _ in range(n_out)],
        out_shape=[jax.ShapeDtypeStruct((M, N), dt) for dt in out_dtypes],
        scratch_shapes=[pltpu.VMEM((tm, tn), F32)],
        compiler_params=_cparams(("parallel", "parallel", "arbitrary")),
    )(a, b, *a_extra, *epi_extra, *deps)
    return out[0] if n_out == 1 else out


def _matmul_tn(name, a, b, *, a_fn=None, a_extra=(), a_cols=None, b_cols=None, ta=1024, tb=1024, tt=1024):
    T = a.shape[0]
    a0, Ka = a_cols if a_cols is not None else (0, a.shape[1])
    b0, Nb = b_cols if b_cols is not None else (0, b.shape[1])
    ta, tb, tt = _pick(Ka, (ta, 512, 256, 128)), _pick(Nb, (tb, 896, 512, 256, 128)), _pick(T, (tt, 512, 256, 128))
    assert a0 % ta == 0 and b0 % tb == 0
    a0, b0 = a0 // ta, b0 // tb
    nt = T // tt
    n_ax = len(a_extra)

    def body(*refs):
        a_ref, b_ref = refs[0], refs[1]
        ax_refs = refs[2:2 + n_ax]
        o_ref = refs[2 + n_ax]
        t = pl.program_id(2)
        at = a_ref[...]
        at = a_fn(at, *[r[...] for r in ax_refs]) if a_fn is not None else at.astype(BF16)
        part = _dot(at, b_ref[...].astype(BF16), TN)

        @pl.when(t == 0)
        def _():
            o_ref[...] = part

        @pl.when(t > 0)
        def _():
            o_ref[...] += part

    in_specs = [pl.BlockSpec((tt, ta), lambda i, j, t: (t, a0 + i)), pl.BlockSpec((tt, tb), lambda i, j, t: (t, b0 + j))]
    in_specs += [pl.BlockSpec((1, ta), lambda i, j, t: (0, a0 + i)) for _ in a_extra]
    return pl.pallas_call(
        body,
        name=name,
        grid=(Ka // ta, Nb // tb, nt),
        in_specs=in_specs,
        out_specs=pl.BlockSpec((ta, tb), lambda i, j, t: (i, j)),
        out_shape=jax.ShapeDtypeStruct((Ka, Nb), F32),
        compiler_params=_cparams(("parallel", "parallel", "arbitrary")),
    )(a, b, *a_extra)


W = BRANCH_WIDTH
C_CB, C_CC, C_CH, C_HQ, C_HF, C_HI, C_HG, C_MQ, N_MIX = 0, W, 2 * W, 3 * W, 4 * W, 5 * W, 6 * W, 7 * W, 8 * W
TS_MIX = 256


def _sigmoid(x):
    return jax.nn.sigmoid(x)


def _chunk_pos(shape):
    return lax.broadcasted_iota(jnp.int32, shape, 0) & (HG_CHUNK - 1)


def _seg_cumsum(x, pos):
    sh = 1
    while sh < HG_CHUNK:
        x = x + jnp.where(pos >= sh, pltpu.roll(x, sh, 0), 0.0)
        sh *= 2
    return x


def _seg_rev_cumsum(x, pos):
    n = x.shape[0]
    sh = 1
    while sh < HG_CHUNK:
        x = x + jnp.where(pos < HG_CHUNK - sh, pltpu.roll(x, n - sh, 0), 0.0)
        sh *= 2
    return x


def _chunk_mask(ts):
    r = lax.broadcasted_iota(jnp.int32, (ts, ts), 0)
    c = lax.broadcasted_iota(jnp.int32, (ts, ts), 1)
    return jnp.logical_and((r // HG_CHUNK) == (c // HG_CHUNK), c <= r)


def _hgrn_gates(p_ref, lb):
    q = p_ref[:, C_HQ:C_HQ + W]
    fl = p_ref[:, C_HF:C_HF + W]
    sig = _sigmoid(fl)
    f = lb + (1.0 - lb) * sig
    logf = jnp.log(f)
    k = (1.0 - lb) * _sigmoid(-fl)
    sq = _sigmoid(q)
    qs = q * sq
    return q, sq, qs, sig, f, logf, k


def _hgrn_decays(logf, bc_sc, ts):
    pos = _chunk_pos(logf.shape)
    bc = _seg_cumsum(logf, pos)
    bc_sc[...] = bc
    nc = ts // HG_CHUNK
    bref = jnp.concatenate(
        [jnp.broadcast_to(bc_sc[n * HG_CHUNK + HG_CHUNK // 2 - 1:n * HG_CHUNK + HG_CHUNK // 2, :], (HG_CHUNK, W)) for n in range(nc)], axis=0)
    blast = jnp.concatenate(
        [jnp.broadcast_to(bc_sc[(n + 1) * HG_CHUNK - 1:(n + 1) * HG_CHUNK, :], (HG_CHUNK, W)) for n in range(nc)], axis=0)
    return pos, bc, bref, blast


def _conv_shift_down(u, carry_ref, row):
    u1 = jnp.where(row == 0, carry_ref[7:8, :], pltpu.roll(u, 1, 0))
    u2 = jnp.where(row == 0, carry_ref[6:7, :], jnp.where(row == 1, carry_ref[7:8, :], pltpu.roll(u, 2, 0)))
    return u1, u2


def _attn_probs(qh, kh):
    s = _dot(qh, kh, NT) * (MEM_HEAD_DIM ** -0.5)
    e = jnp.exp(s - jnp.max(s, axis=-1, keepdims=True))
    return e / jnp.sum(e, axis=-1, keepdims=True)


def _mixer_fwd(p, mk, mv, lb, conv_w, norm_w, *, bl, seq):
    T = p.shape[0]
    ts = TS_MIX
    ns = seq // ts
    nc = ts // HG_CHUNK
    ml = mk.shape[0] // bl

    def body(p_ref, mk_ref, mv_ref, lb_ref, cw_ref, nw_ref, y_ref, st_ref, opre_ref, state_sc, carry_sc, bc_sc):
        @pl.when(pl.program_id(1) == 0)
        def _():
            state_sc[...] = jnp.zeros_like(state_sc)
            carry_sc[...] = jnp.zeros_like(carry_sc)

        cb, cc, ch = p_ref[:, C_CB:C_CB + W], p_ref[:, C_CC:C_CC + W], p_ref[:, C_CH:C_CH + W]
        u = cc * ch
        row = lax.broadcasted_iota(jnp.int32, (ts, W), 0)
        u1, u2 = _conv_shift_down(u, carry_sc, row)
        yconv = u2 * cw_ref[0:1, :] + u1 * cw_ref[1:2, :] + u * cw_ref[2:3, :]
        y_ref[:, 0:W] = (cb * yconv).astype(BF16)
        carry_sc[...] = u[ts - 8:ts, :]

        lbv = lb_ref[...]
        _, _, qs, _, _, logf, k = _hgrn_gates(p_ref, lbv)
        pos, bc, bref, blast = _hgrn_decays(logf, bc_sc, ts)
        a_all = (qs * jnp.exp(bc - bref)).astype(BF16)
        bk_all = (k * jnp.exp(bref - bc)).astype(BF16)
        qin_all = (qs * jnp.exp(bc)).astype(BF16)
        kout_all = k * jnp.exp(blast - bc)
        v_all = p_ref[:, C_HI:C_HI + W]
        mask = _chunk_mask(ts)
        chunk_of_row = lax.broadcasted_iota(jnp.int32, (ts, HG_F), 0) // HG_CHUNK
        for h in range(HG_HEADS):
            hs = slice(h * HG_F, (h + 1) * HG_F)
            vb = v_all[:, hs].astype(BF16)
            vt = v_all[:, hs].T.astype(BF16)
            scores = jnp.where(mask, _dot(a_all[:, hs], bk_all[:, hs], NT), 0.0)
            o_intra = _dot(scores.astype(BF16), vb, NN)
            kout = kout_all[:, hs]
            st = state_sc[h]
            o_inter = []
            for n in range(nc):
                st_ref[n, h] = st
                o_inter.append(_dot(qin_all[n * HG_CHUNK:(n + 1) * HG_CHUNK, hs], st.astype(BF16), NT))
                kv = _dot(vt, jnp.where(chunk_of_row == n, kout, 0.0).astype(BF16), NN)
                decay = jnp.exp(bc_sc[(n + 1) * HG_CHUNK - 1:(n + 1) * HG_CHUNK, hs])
                st = st * decay + kv
            state_sc[h] = st
            o = o_intra + jnp.concatenate(o_inter, axis=0)
            opre_ref[:, hs] = o
            on = o * lax.rsqrt(jnp.mean(o * o, axis=-1, keepdims=True) + RMS_EPS) * nw_ref[...]
            g = p_ref[:, C_HG + h * HG_F:C_HG + (h + 1) * HG_F]
            y_ref[:, W + h * HG_F:W + (h + 1) * HG_F] = (on * (g * _sigmoid(g))).astype(BF16)

        for h in range(MEM_HEADS):
            hs = slice(h * MEM_HEAD_DIM, (h + 1) * MEM_HEAD_DIM)
            qh = p_ref[:, C_MQ + h * MEM_HEAD_DIM:C_MQ + (h + 1) * MEM_HEAD_DIM].astype(BF16)
            prob = _attn_probs(qh, mk_ref[:, hs])
            y_ref[:, 2 * W + h * MEM_HEAD_DIM:2 * W + (h + 1) * MEM_HEAD_DIM] = _dot(prob.astype(BF16), mv_ref[:, hs], NN).astype(BF16)

    return pl.pallas_call(
        body,
        name="mixer_fwd",
        grid=(bl, ns),
        in_specs=[
            pl.BlockSpec((ts, N_MIX), lambda b, s: (b * ns + s, 0)),
            pl.BlockSpec((ml, W), lambda b, s: (b, 0)),
            pl.BlockSpec((ml, W), lambda b, s: (b, 0)),
            pl.BlockSpec((1, W), lambda b, s: (0, 0)),
            pl.BlockSpec((CONV_K, W), lambda b, s: (0, 0)),
            pl.BlockSpec((1, HG_F), lambda b, s: (0, 0)),
        ],
        out_specs=[
            pl.BlockSpec((ts, 3 * W), lambda b, s: (b * ns + s, 0)),
            pl.BlockSpec((nc, HG_HEADS, HG_F, HG_F), lambda b, s: (b * ns + s, 0, 0, 0)),
            pl.BlockSpec((ts, W), lambda b, s: (b * ns + s, 0)),
        ],
        out_shape=[
            jax.ShapeDtypeStruct((T, 3 * W), BF16),
            jax.ShapeDtypeStruct((T // HG_CHUNK, HG_HEADS, HG_F, HG_F), F32),
            jax.ShapeDtypeStruct((T, W), F32),
        ],
        scratch_shapes=[pltpu.VMEM((HG_HEADS, HG_F, HG_F), F32), pltpu.VMEM((8, W), F32), pltpu.VMEM((ts, W), F32)],
        compiler_params=_cparams(("arbitrary", "arbitrary")),
    )(p, mk, mv, lb, conv_w, norm_w)


def _mixer_bwd(p, dy, dp_gates, st, opre, mk, mv, lb, conv_w, norm_w, *, bl, seq):
    T, nin = p.shape
    ts = TS_MIX
    ns = seq // ts
    nc = ts // HG_CHUNK
    ml = mk.shape[0] // bl
    mid, last = HG_CHUNK // 2 - 1, HG_CHUNK - 1

    def body(p_ref, pprev_ref, dy_ref, dpin_ref, st_ref, opre_ref, mk_ref, mv_ref, lb_ref, cw_ref, nw_ref,
             dp_ref, dmk_ref, dmv_ref, dcw_ref, dnw_ref, dlb_ref, dstate_sc, carry_sc, uprev_sc, bc_sc):
        del dpin_ref
        b, s = pl.program_id(0), pl.program_id(1)

        @pl.when(s == 0)
        def _():
            dstate_sc[...] = jnp.zeros_like(dstate_sc)
            carry_sc[...] = jnp.zeros_like(carry_sc)
            dmk_ref[...] = jnp.zeros_like(dmk_ref)
            dmv_ref[...] = jnp.zeros_like(dmv_ref)

        @pl.when(jnp.logical_and(b == 0, s == 0))
        def _():
            dcw_ref[...] = jnp.zeros_like(dcw_ref)
            dnw_ref[...] = jnp.zeros_like(dnw_ref)
            dlb_ref[...] = jnp.zeros_like(dlb_ref)

        cb, cc, ch = p_ref[:, C_CB:C_CB + W], p_ref[:, C_CC:C_CC + W], p_ref[:, C_CH:C_CH + W]
        u = cc * ch
        row = lax.broadcasted_iota(jnp.int32, (ts, W), 0)
        uprev = pprev_ref[:, C_CC:C_CC + W] * pprev_ref[:, C_CH:C_CH + W]
        uprev_sc[...] = jnp.where(s == ns - 1, 0.0, uprev)
        u1, u2 = _conv_shift_down(u, uprev_sc, row)
        w0, w1, w2 = cw_ref[0:1, :], cw_ref[1:2, :], cw_ref[2:3, :]
        dya = dy_ref[:, 0:W]
        dp_ref[:, C_CB:C_CB + W] = (dya * (u2 * w0 + u1 * w1 + u * w2)).astype(BF16)
        dv = cb * dya
        dv1 = jnp.where(row == ts - 1, carry_sc[0:1, :], pltpu.roll(dv, ts - 1, 0))
        dv2 = jnp.where(row == ts - 1, carry_sc[1:2, :], jnp.where(row == ts - 2, carry_sc[0:1, :], pltpu.roll(dv, ts - 2, 0)))
        du = dv * w2 + dv1 * w1 + dv2 * w0
        dp_ref[:, C_CC:C_CC + W] = (du * ch).astype(BF16)
        dp_ref[:, C_CH:C_CH + W] = (du * cc).astype(BF16)
        dcw_ref[0:1, :] += jnp.sum(dv * u2, axis=0, keepdims=True)
        dcw_ref[1:2, :] += jnp.sum(dv * u1, axis=0, keepdims=True)
        dcw_ref[2:3, :] += jnp.sum(dv * u, axis=0, keepdims=True)
        carry_sc[...] = dv[0:8, :]

        lbv = lb_ref[...]
        q_all, sq_all, qs_all, sig_all, f_all, logf, k_all = _hgrn_gates(p_ref, lbv)
        pos_all, bc, bref, blast = _hgrn_decays(logf, bc_sc, ts)
        ea_all, eb_all, eq_all, ek_all = jnp.exp(bc - bref), jnp.exp(bref - bc), jnp.exp(bc), jnp.exp(blast - bc)
        mask = _chunk_mask(ts)
        chunk_of_row = lax.broadcasted_iota(jnp.int32, (ts, HG_F), 0) // HG_CHUNK
        pos = _chunk_pos((ts, HG_F))
        pos_c = _chunk_pos((HG_CHUNK, HG_F))
        nw = nw_ref[...]
        for h in range(HG_HEADS):
            hs = slice(h * HG_F, (h + 1) * HG_F)
            qs, k, ea, eb, eq, ek = qs_all[:, hs], k_all[:, hs], ea_all[:, hs], eb_all[:, hs], eq_all[:, hs], ek_all[:, hs]
            a, bk, qin, kout = qs * ea, k * eb, qs * eq, k * ek
            o = opre_ref[:, hs]
            g = p_ref[:, C_HG + h * HG_F:C_HG + (h + 1) * HG_F]
            sg = _sigmoid(g)
            r = lax.rsqrt(jnp.mean(o * o, axis=-1, keepdims=True) + RMS_EPS)
            dyb = dy_ref[:, W + h * HG_F:W + (h + 1) * HG_F]
            dp_ref[:, C_HG + h * HG_F:C_HG + (h + 1) * HG_F] = (dyb * (o * r * nw) * (sg * (1.0 + g * (1.0 - sg)))).astype(BF16)
            don = dyb * (g * sg)
            dnw_ref[0:1, :] += jnp.sum(don * o * r, axis=0, keepdims=True)
            dn = don * nw
            do = r * (dn - o * (r * r) * jnp.mean(dn * o, axis=-1, keepdims=True))
            dob = do.astype(BF16)
            dot_ = do.T.astype(BF16)
            vb = p_ref[:, C_HI + h * HG_F:C_HI + (h + 1) * HG_F].astype(BF16)
            ab, bkb = a.astype(BF16), bk.astype(BF16)
            scores = jnp.where(mask, _dot(ab, bkb, NT), 0.0)
            dscores = jnp.where(mask, _dot(dob, vb, NT), 0.0)
            dv_h = _dot(scores.T.astype(BF16), dob, NN)
            da = _dot(dscores.astype(BF16), bkb, NN)
            dbk = _dot(dscores.T.astype(BF16), ab, NN)
            koutb = kout.astype(BF16)
            dst = dstate_sc[h]
            dqin_p, dkout_p, dvi_p, ddec_p = [None] * nc, [None] * nc, [None] * nc, [None] * nc
            for n in reversed(range(nc)):
                rows = slice(n * HG_CHUNK, (n + 1) * HG_CHUNK)
                st_n = st_ref[n, h]
                decay = jnp.exp(bc_sc[n * HG_CHUNK + last:n * HG_CHUNK + last + 1, hs])
                dstb = dst.astype(BF16)
                dvi_p[n] = _dot(koutb[rows], dstb, NT)
                dkout_p[n] = _dot(vb[rows], dstb, NN)
                ddec_p[n] = jnp.sum(dst * st_n, axis=0, keepdims=True) * decay
                dqin_p[n] = _dot(dob[rows], st_n.astype(BF16), NN)
                dst = dst * decay + _dot(dot_, jnp.where(chunk_of_row == n, qin, 0.0).astype(BF16), NN)
            dstate_sc[h] = dst
            dqin = jnp.concatenate(dqin_p, axis=0)
            dkout = jnp.concatenate(dkout_p, axis=0)
            dp_ref[:, C_HI + h * HG_F:C_HI + (h + 1) * HG_F] = (dv_h + jnp.concatenate(dvi_p, axis=0)).astype(BF16)
            dqs = da * ea + dqin * eq
            dk = dbk * eb + dkout * ek
            t_a, t_b, t_q, t_k = da * a, dbk * bk, dqin * qin, dkout * kout
            dbc = t_a - t_b + t_q - t_k
            t_ref = t_b - t_a
            pieces = []
            for n in range(nc):
                rows = slice(n * HG_CHUNK, (n + 1) * HG_CHUNK)
                s_ref = jnp.sum(t_ref[rows], axis=0, keepdims=True)
                s_last = jnp.sum(t_k[rows], axis=0, keepdims=True) + ddec_p[n]
                pieces.append(dbc[rows] + jnp.where(pos_c == mid, s_ref, 0.0) + jnp.where(pos_c == last, s_last, 0.0))
            dlogf = _seg_rev_cumsum(jnp.concatenate(pieces, axis=0), pos)
            sig, lbh = sig_all[:, hs], lbv[:, hs]
            dfk = dlogf / f_all[:, hs] - dk
            dp_ref[:, C_HF + h * HG_F:C_HF + (h + 1) * HG_F] = (dfk * (1.0 - lbh) * sig * (1.0 - sig)).astype(BF16)
            dlb_ref[0:1, hs] += jnp.sum(dfk * (1.0 - sig), axis=0, keepdims=True)
            q, sq = q_all[:, hs], sq_all[:, hs]
            dp_ref[:, C_HQ + h * HG_F:C_HQ + (h + 1) * HG_F] = (dqs * (sq * (1.0 + q * (1.0 - sq)))).astype(BF16)

        for h in range(MEM_HEADS):
            hs = slice(h * MEM_HEAD_DIM, (h + 1) * MEM_HEAD_DIM)
            qh = p_ref[:, C_MQ + h * MEM_HEAD_DIM:C_MQ + (h + 1) * MEM_HEAD_DIM].astype(BF16)
            kh, vh = mk_ref[:, hs], mv_ref[:, hs]
            prob = _attn_probs(qh, kh)
            dob = dy_ref[:, 2 * W + h * MEM_HEAD_DIM:2 * W + (h + 1) * MEM_HEAD_DIM].astype(BF16)
            dmv_ref[:, hs] += _dot(prob.T.astype(BF16), dob, NN)
            dprob = _dot(dob, vh, NT)
            ds = prob * (dprob - jnp.sum(dprob * prob, axis=-1, keepdims=True)) * (MEM_HEAD_DIM ** -0.5)
            dp_ref[:, C_MQ + h * MEM_HEAD_DIM:C_MQ + (h + 1) * MEM_HEAD_DIM] = _dot(ds.astype(BF16), kh, NN).astype(BF16)
            dmk_ref[:, hs] += _dot(ds.T.astype(BF16), qh, NN)

    def tile(b, s):
        return b * ns + (ns - 1 - s)

    return pl.pallas_call(
        body,
        name="mixer_bwd",
        grid=(bl, ns),
        in_specs=[
            pl.BlockSpec((ts, N_MIX), lambda b, s: (tile(b, s), 0)),
            pl.BlockSpec((8, N_MIX), lambda b, s: (jnp.maximum(tile(b, s) * (ts // 8) - 1, 0), 0)),
            pl.BlockSpec((ts, 3 * W), lambda b, s: (tile(b, s), 0)),
            pl.BlockSpec(memory_space=pl.ANY),
            pl.BlockSpec((nc, HG_HEADS, HG_F, HG_F), lambda b, s: (tile(b, s), 0, 0, 0)),
            pl.BlockSpec((ts, W), lambda b, s: (tile(b, s), 0)),
            pl.BlockSpec((ml, W), lambda b, s: (b, 0)),
            pl.BlockSpec((ml, W), lambda b, s: (b, 0)),
            pl.BlockSpec((1, W), lambda b, s: (0, 0)),
            pl.BlockSpec((CONV_K, W), lambda b, s: (0, 0)),
            pl.BlockSpec((1, HG_F), lambda b, s: (0, 0)),
        ],
        out_specs=[
            pl.BlockSpec((ts, N_MIX), lambda b, s: (tile(b, s), 0)),
            pl.BlockSpec((ml, W), lambda b, s: (b, 0)),
            pl.BlockSpec((ml, W), lambda b, s: (b, 0)),
            pl.BlockSpec((8, W), lambda b, s: (0, 0)),
            pl.BlockSpec((8, HG_F), lambda b, s: (0, 0)),
            pl.BlockSpec((8, W), lambda b, s: (0, 0)),
        ],
        out_shape=[
            jax.ShapeDtypeStruct((T, nin), BF16),
            jax.ShapeDtypeStruct((bl * ml, W), F32),
            jax.ShapeDtypeStruct((bl * ml, W), F32),
            jax.ShapeDtypeStruct((8, W), F32),
            jax.ShapeDtypeStruct((8, HG_F), F32),
            jax.ShapeDtypeStruct((8, W), F32),
        ],
        input_output_aliases={3: 0},
        scratch_shapes=[pltpu.VMEM((HG_HEADS, HG_F, HG_F), F32), pltpu.VMEM((8, W), F32), pltpu.VMEM((8, W), F32),
                        pltpu.VMEM((ts, W), F32)],
        compiler_params=_cparams(("arbitrary", "arbitrary")),
    )(p, p, dy, dp_gates, st, opre, mk, mv, lb, conv_w, norm_w)


def _layer_norm_stats(z):
    mu = jnp.mean(z, axis=-1, keepdims=True)
    zc = z - mu
    rstd = lax.rsqrt(jnp.mean(zc * zc, axis=-1, keepdims=True) + LN_EPS)
    return zc * rstd, rstd


def _gate_specs(tm, d):
    g0 = N_MIX // d
    return [pl.BlockSpec((tm, d), functools.partial(lambda i, k: (i, g0 + k), k=k)) for k in range(N_BRANCH)]


def _merge_fwd(y, p, x0, wb, wo, bg, *, alpha, tm=256):
    T, d = x0.shape
    assert N_MIX % d == 0
    tm = _pick(T, (tm, 128, 8))

    def body(y_ref, g0_ref, g1_ref, g2_ref, x_ref, wb_ref, wo_ref, bg_ref, r_ref, mg_ref, xh_ref, rs_ref):
        merged = None
        for i, g_ref in enumerate((g0_ref, g1_ref, g2_ref)):
            r = _dot(y_ref[:, i * W:(i + 1) * W], wb_ref[i * W:(i + 1) * W, :], NN)
            r_ref[:, i * d:(i + 1) * d] = r
            t = _sigmoid(g_ref[...] + bg_ref[:, i * d:(i + 1) * d]) * r
            merged = t if merged is None else merged + t
        mb = merged.astype(BF16)
        mg_ref[...] = mb
        z = alpha * x_ref[...] + _dot(mb, wo_ref[...], NN)
        xh_ref[...], rs_ref[...] = _layer_norm_stats(z)

    row = lambda i: (i, 0)
    fix = lambda i: (0, 0)
    return pl.pallas_call(
        body,
        name="merge_fwd",
        grid=(T // tm,),
        in_specs=[pl.BlockSpec((tm, 3 * W), row)] + _gate_specs(tm, d) + [
            pl.BlockSpec((tm, d), row), pl.BlockSpec((3 * W, d), fix), pl.BlockSpec((d, d), fix), pl.BlockSpec((1, 3 * d), fix)],
        out_specs=[pl.BlockSpec((tm, 3 * d), row), pl.BlockSpec((tm, d), row), pl.BlockSpec((tm, d), row), pl.BlockSpec((tm, 1), row)],
        out_shape=[jax.ShapeDtypeStruct((T, 3 * d), F32), jax.ShapeDtypeStruct((T, d), BF16),
                   jax.ShapeDtypeStruct((T, d), F32), jax.ShapeDtypeStruct((T, 1), F32)],
        compiler_params=_cparams(("parallel",)),
    )(y, p, p, p, x0, wb, wo, bg)


def _merge_bwd(dz, p, r, wb, wo, bg, *, tm=256):
    T, d = dz.shape
    nin = p.shape[1]
    tm = _pick(T, (tm, 128, 8))

    def body(dz_ref, g0_ref, g1_ref, g2_ref, r_ref, wb_ref, wo_ref, bg_ref, dr_ref, dp_ref, dy_ref, dbg_ref):
        @pl.when(pl.program_id(0) == 0)
        def _():
            dbg_ref[...] = jnp.zeros_like(dbg_ref)

        dmerged = _dot(dz_ref[...].astype(BF16), wo_ref[...], NT)
        dp_ref[:, 0:N_MIX] = jnp.zeros((tm, N_MIX), BF16)
        for i, g_ref in enumerate((g0_ref, g1_ref, g2_ref)):
            cs = slice(i * d, (i + 1) * d)
            s = _sigmoid(g_ref[...] + bg_ref[:, cs])
            drb = (dmerged * s).astype(BF16)
            dr_ref[:, cs] = drb
            dgate = dmerged * r_ref[:, cs] * s * (1.0 - s)
            dp_ref[:, N_MIX + i * d:N_MIX + (i + 1) * d] = dgate.astype(BF16)
            dbg_ref[0:1, cs] += jnp.sum(dgate, axis=0, keepdims=True)
            dy_ref[:, i * W:(i + 1) * W] = _dot(drb, wb_ref[i * W:(i + 1) * W, :], NT)

    row = lambda i: (i, 0)
    fix = lambda i: (0, 0)
    return pl.pallas_call(
        body,
        name="merge_bwd",
        grid=(T // tm,),
        in_specs=[pl.BlockSpec((tm, d), row)] + _gate_specs(tm, d) + [
            pl.BlockSpec((tm, 3 * d), row), pl.BlockSpec((3 * W, d), fix), pl.BlockSpec((d, d), fix), pl.BlockSpec((1, 3 * d), fix)],
        out_specs=[pl.BlockSpec((tm, 3 * d), row), pl.BlockSpec((tm, nin), row), pl.BlockSpec((tm, 3 * W), row),
                   pl.BlockSpec((8, 3 * d), fix)],
        out_shape=[jax.ShapeDtypeStruct((T, 3 * d), BF16), jax.ShapeDtypeStruct((T, nin), BF16),
                   jax.ShapeDtypeStruct((T, 3 * W), F32), jax.ShapeDtypeStruct((8, 3 * d), F32)],
        compiler_params=_cparams(("arbitrary",)),
    )(dz, p, p, p, r, wb, wo, bg)


def _mlp_fwd(xhat1, g1, b1, wu, wd, g2, b2, *, alpha, tm=512, tf=1024):
    T, d = xhat1.shape
    ff = wu.shape[1]
    tm, tf = _pick(T, (tm, 256, 128, 8)), _pick(ff, (tf, 512, 256, 128))
    nf = ff // tf

    def body(xh_ref, g1_ref, b1_ref, wu_ref, wd_ref, g2_ref, b2_ref, a_ref, xh2_ref, rs2_ref, x2_ref, acc_ref):
        f = pl.program_id(1)
        x1 = xh_ref[...] * g1_ref[...] + b1_ref[...]
        a = _dot(x1.astype(BF16), wu_ref[...], NN)
        a_ref[...] = a.astype(BF16)
        h = jnp.square(jnp.maximum(a, 0.0))
        part = _dot(h.astype(BF16), wd_ref[...], NN)

        @pl.when(f == 0)
        def _():
            acc_ref[...] = part

        @pl.when(f > 0)
        def _():
            acc_ref[...] += part

        @pl.when(f == nf - 1)
        def _():
            xh2, rs2 = _layer_norm_stats(alpha * x1 + acc_ref[...])
            xh2_ref[...] = xh2
            rs2_ref[...] = rs2
            x2_ref[...] = xh2 * g2_ref[...] + b2_ref[...]

    row = lambda i, f: (i, 0)
    fix = lambda i, f: (0, 0)
    return pl.pallas_call(
        body,
        name="mlp_fwd",
        grid=(T // tm, nf),
        in_specs=[pl.BlockSpec((tm, d), row), pl.BlockSpec((1, d), fix), pl.BlockSpec((1, d), fix),
                  pl.BlockSpec((d, tf), lambda i, f: (0, f)), pl.BlockSpec((tf, d), lambda i, f: (f, 0)),
                  pl.BlockSpec((1, d), fix), pl.BlockSpec((1, d), fix)],
        out_specs=[pl.BlockSpec((tm, tf), lambda i, f: (i, f)), pl.BlockSpec((tm, d), row), pl.BlockSpec((tm, 1), row),
                   pl.BlockSpec((tm, d), row)],
        out_shape=[jax.ShapeDtypeStruct((T, ff), BF16), jax.ShapeDtypeStruct((T, d), F32), jax.ShapeDtypeStruct((T, 1), F32),
                   jax.ShapeDtypeStruct((T, d), F32)],
        scratch_shapes=[pltpu.VMEM((tm, d), F32)],
        compiler_params=_cparams(("parallel", "arbitrary")),
    )(xhat1, g1, b1, wu, wd, g2, b2)


def _ln_bwd(dy, xhat, rstd, g, *, tm=512, deps=()):
    T, d = dy.shape
    tm = _pick(T, (tm, 256, 128, 8))

    def body(dy_ref, xh_ref, rs_ref, g_ref, *rest):
        dz_ref, dg_ref, db_ref = rest[len(deps):]

        @pl.when(pl.program_id(0) == 0)
        def _():
            dg_ref[...] = jnp.zeros_like(dg_ref)
            db_ref[...] = jnp.zeros_like(db_ref)

        dy_, xh = dy_ref[...], xh_ref[...]
        dg_ref[0:1, :] += jnp.sum(dy_ * xh, axis=0, keepdims=True)
        db_ref[0:1, :] += jnp.sum(dy_, axis=0, keepdims=True)
        dxh = dy_ * g_ref[...]
        dz_ref[...] = rs_ref[...] * (dxh - jnp.mean(dxh, axis=-1, keepdims=True) - xh * jnp.mean(dxh * xh, axis=-1, keepdims=True))

    row = lambda i: (i, 0)
    fix = lambda i: (0, 0)
    return pl.pallas_call(
        body,
        name="ln_bwd",
        grid=(T // tm,),
        in_specs=[pl.BlockSpec((tm, d), row), pl.BlockSpec((tm, d), row), pl.BlockSpec((tm, 1), row), pl.BlockSpec((1, d), fix)]
        + [ANY_SPEC] * len(deps),
        out_specs=[pl.BlockSpec((tm, d), row), pl.BlockSpec((8, d), fix), pl.BlockSpec((8, d), fix)],
        out_shape=[jax.ShapeDtypeStruct((T, d), F32), jax.ShapeDtypeStruct((8, d), F32), jax.ShapeDtypeStruct((8, d), F32)],
        compiler_params=_cparams(("arbitrary",)),
    )(dy, xhat, rstd, g, *deps)


def _loss_head(y, target, *, tm=512):
    T, d = y.shape
    tm = _pick(T, (tm, 256, 128, 8))
    n = T // tm

    def body(y_ref, t_ref, loss_ref, dy_ref, acc_ref):
        i = pl.program_id(0)

        @pl.when(i == 0)
        def _():
            acc_ref[...] = jnp.zeros_like(acc_ref)

        e = y_ref[...] - t_ref[...]
        dy_ref[...] = e * (1.0 / d)
        acc_ref[...] += jnp.sum(e * e, axis=0, keepdims=True)

        @pl.when(i == n - 1)
        def _():
            loss_ref[...] = (0.5 / d) * jnp.sum(acc_ref[...], axis=1, keepdims=True)

    row = lambda i: (i, 0)
    return pl.pallas_call(
        body,
        name="loss_head",
        grid=(n,),
        in_specs=[pl.BlockSpec((tm, d), row), pl.BlockSpec((tm, d), row)],
        out_specs=[pl.BlockSpec((1, 1), lambda i: (0, 0)), pl.BlockSpec((tm, d), row)],
        out_shape=[jax.ShapeDtypeStruct((1, 1), F32), jax.ShapeDtypeStruct((T, d), F32)],
        scratch_shapes=[pltpu.VMEM((1, d), F32)],
        compiler_params=_cparams(("arbitrary",)),
    )(y, target)


def _lower_bounds_fwd(lower_bounds):
    depth, n = lower_bounds.shape

    def body(x_ref, soft_ref, lb_ref):
        x = x_ref[...]
        e = jnp.exp(x - jnp.max(x, axis=0, keepdims=True))
        soft_ref[...] = e / jnp.sum(e, axis=0, keepdims=True)
        run = None
        for l in range(depth):
            run = soft_ref[l:l + 1, :] if run is None else run + soft_ref[l:l + 1, :]
            lb_ref[l:l + 1, :] = run - soft_ref[0:1, :]

    return pl.pallas_call(body, name="lower_bounds_fwd",
                          out_shape=[jax.ShapeDtypeStruct((depth, n), F32), jax.ShapeDtypeStruct((depth, n), F32)])(lower_bounds)


def _lower_bounds_bwd(soft, dlb):
    depth, n = soft.shape

    def body(soft_ref, dlb_ref, out_ref, dsoft_ref):
        total = jnp.sum(dlb_ref[...], axis=0, keepdims=True)
        run = None
        for l in reversed(range(depth)):
            run = dlb_ref[l:l + 1, :] if run is None else run + dlb_ref[l:l + 1, :]
            dsoft_ref[l:l + 1, :] = run - total if l == 0 else run
        s, ds = soft_ref[...], dsoft_ref[...]
        out_ref[...] = s * (ds - jnp.sum(s * ds, axis=0, keepdims=True))

    return pl.pallas_call(body, name="lower_bounds_bwd", out_shape=jax.ShapeDtypeStruct((depth, n), F32),
                          scratch_shapes=[pltpu.VMEM((depth, n), F32)])(soft, dlb)


def _layer_fwd(x0, mem2, lb, wts, *, bl, seq, alpha, deps=()):
    p = _matmul("proj_in", x0, wts["w_in"], mode="nn", deps=deps)
    mk = _matmul("mem_k", mem2, wts["w_mem_k"], mode="nn", out_dtype=BF16)
    mv = _matmul("mem_v", mem2, wts["w_mem_v"], mode="nn", out_dtype=BF16)
    y, st, opre = _mixer_fwd(p, mk, mv, lb, wts["conv_w"], wts["hg_norm_w"], bl=bl, seq=seq)
    r, merged, xhat1, rstd1 = _merge_fwd(y, p, x0, wts["w_branch"], wts["w_o"], wts["b_gate"], alpha=alpha)
    a, xhat2, rstd2, x2 = _mlp_fwd(xhat1, wts["ln1_g"], wts["ln1_b"], wts["w_up"], wts["w_down"], wts["ln2_g"], wts["ln2_b"],
                                   alpha=alpha)
    saved = dict(x0=x0, p=p, mk=mk, mv=mv, y=y, st=st, opre=opre, r=r, merged=merged, xhat1=xhat1, rstd1=rstd1, a=a,
                 xhat2=xhat2, rstd2=rstd2)
    return x2, saved


def _relu2_bf16(a):
    return jnp.square(jnp.maximum(a.astype(F32), 0.0)).astype(BF16)


def _affine_bf16(xh, g, b):
    return (xh * g + b).astype(BF16)


def _layer_bwd(dx2, sv, mem2, lb, wts, *, bl, seq, alpha, deps=()):
    d = dx2.shape[1]
    g = {}
    dz2, dg2, db2 = _ln_bwd(dx2, sv["xhat2"], sv["rstd2"], wts["ln2_g"], deps=deps)
    g["ln2_g"], g["ln2_b"] = dg2[0:1], db2[0:1]
    da = _matmul("mlp_da", dz2, wts["w_down"], mode="nt", out_dtype=BF16,
                 epi_fn=lambda acc, a: (acc * (2.0 * jnp.maximum(a.astype(F32), 0.0)),), epi_extra=(sv["a"],))
    g["w_down"] = _matmul_tn("grad_w_down", sv["a"], dz2, a_fn=_relu2_bf16)
    g["w_up"] = _matmul_tn("grad_w_up", sv["xhat1"], da, a_fn=_affine_bf16, a_extra=(wts["ln1_g"], wts["ln1_b"]))
    dx1 = _matmul("mlp_dx", da, wts["w_up"], mode="nt", epi_fn=lambda acc, dz: (acc + alpha * dz,), epi_extra=(dz2,))
    dz1, dg1, db1 = _ln_bwd(dx1, sv["xhat1"], sv["rstd1"], wts["ln1_g"])
    g["ln1_g"], g["ln1_b"] = dg1[0:1], db1[0:1]
    g["w_o"] = _matmul_tn("grad_w_o", sv["merged"], dz1)
    dr, dp, dy, dbg = _merge_bwd(dz1, sv["p"], sv["r"], wts["w_branch"], wts["w_o"], wts["b_gate"])
    g["b_gate"] = dbg[0:1]
    g["w_branch"] = jnp.concatenate(
        [_matmul_tn("grad_w_branch", sv["y"], dr, a_cols=(i * W, W), b_cols=(i * d, d)) for i in range(N_BRANCH)], axis=0)
    dp, dmk, dmv, dcw, dnw, dlb = _mixer_bwd(sv["p"], dy, dp, sv["st"], sv["opre"], sv["mk"], sv["mv"], lb,
                                              wts["conv_w"], wts["hg_norm_w"], bl=bl, seq=seq)
    g["conv_w"], g["hg_norm_w"], g["lb"] = dcw[0:CONV_K], dnw[0:1], dlb[0:1]
    g["w_mem_k"] = _matmul_tn("grad_w_mem_k", mem2, dmk)
    g["w_mem_v"] = _matmul_tn("grad_w_mem_v", mem2, dmv)
    g["w_in"] = _matmul_tn("grad_w_in", sv["x0"], dp)
    dx0 = _matmul("proj_in_dx", dp, wts["w_in"], mode="nt", epi_fn=lambda acc, dz: (acc + alpha * dz,), epi_extra=(dz1,))
    return dx0, g


N_CHIPS = 4
MESH_IDS = pl.DeviceIdType.MESH


def _axis_slice(ref, axis, start, size):
    idx = [slice(None)] * len(ref.shape)
    idx[axis] = pl.ds(start, size)
    return ref.at[tuple(idx)]


def _chip_exchange(name, items):
    n = len(items)
    out_shapes, meta = [], []
    for arr, kind, axis in items:
        shp = list(arr.shape)
        if kind == "gather":
            per = shp[axis]
            shp[axis] = per * N_CHIPS
            out_shapes.append(jax.ShapeDtypeStruct(tuple(shp), arr.dtype))
        elif kind == "scatter":
            per = shp[axis] // N_CHIPS
            shp[axis] = per
            out_shapes.append(jax.ShapeDtypeStruct((N_CHIPS, *shp), arr.dtype))
        else:
            per = None
            out_shapes.append(jax.ShapeDtypeStruct((N_CHIPS, *shp), arr.dtype))
        meta.append((kind, axis, per))

    def body(*refs):
        ins, outs = refs[:n], refs[n:2 * n]
        send_sems, recv_sems, local_sems = refs[2 * n:]
        x, y, c = lax.axis_index("x"), lax.axis_index("y"), lax.axis_index("c")
        me = 2 * x + y
        peers = [(1 - x, y), (x, 1 - y), (1 - x, 1 - y)]

        def src_for(t, chip):
            kind, axis, per = meta[t]
            return _axis_slice(ins[t], axis, chip * per, per) if kind == "scatter" else ins[t]

        def dst_from(t, chip):
            kind, axis, per = meta[t]
            return _axis_slice(outs[t], axis, chip * per, per) if kind == "gather" else outs[t].at[chip]

        def remote(t, k):
            px, py = peers[k]
            return pltpu.make_async_remote_copy(
                src_ref=src_for(t, 2 * px + py), dst_ref=dst_from(t, me), send_sem=send_sems.at[t * 3 + k],
                recv_sem=recv_sems.at[t * 3 + k], device_id=(px, py, c), device_id_type=MESH_IDS)

        def arrival(t, k):
            px, py = peers[k]
            return pltpu.make_async_remote_copy(
                src_ref=src_for(t, me), dst_ref=dst_from(t, 2 * px + py), send_sem=send_sems.at[t * 3 + k],
                recv_sem=recv_sems.at[t * 3 + k], device_id=(px, py, c), device_id_type=MESH_IDS)

        sends = [remote(t, k) for t in range(n) for k in range(3)]
        for cp in sends:
            cp.start()
        own = [pltpu.make_async_copy(src_for(t, me), dst_from(t, me), local_sems.at[t]) for t in range(n)]
        for cp in own:
            cp.start()
        for t in range(n):
            for k in range(3):
                arrival(t, k).wait_recv()
        for cp in sends:
            cp.wait_send()
        for cp in own:
            cp.wait()

    any_spec = pl.BlockSpec(memory_space=pl.ANY)
    return pl.pallas_call(
        body,
        name=name,
        in_specs=[any_spec] * n,
        out_specs=[any_spec] * n,
        out_shape=out_shapes,
        scratch_shapes=[pltpu.SemaphoreType.DMA((3 * n,)), pltpu.SemaphoreType.DMA((3 * n,)), pltpu.SemaphoreType.DMA((n,))],
        compiler_params=pltpu.CompilerParams(has_side_effects=True),
    )(*[a for a, _, _ in items])


HBM_SPEC = pl.BlockSpec(memory_space=pltpu.HBM)
SEM_SPEC = pl.BlockSpec(memory_space=pltpu.SEMAPHORE)
N_PEERS = N_CHIPS - 1
OWN_SLOT = N_PEERS


class _Split:
    def __init__(self, name, items):
        self.name, self.n = name, len(items)
        self.srcs = [a for a, _, _ in items]
        self.meta, self.land_shapes = [], []
        for arr, kind, axis in items:
            shp = list(arr.shape)
            if kind == "gather":
                per = shp[axis]
                shp[axis] = per * N_CHIPS
                self.land_shapes.append(jax.ShapeDtypeStruct(tuple(shp), arr.dtype))
            else:
                per = shp[axis] // N_CHIPS
                shp[axis] = per
                self.land_shapes.append(jax.ShapeDtypeStruct((N_CHIPS, *shp), arr.dtype))
            self.meta.append((kind, axis, per))

    def _src(self, ins, t, chip):
        kind, axis, per = self.meta[t]
        return _axis_slice(ins[t], axis, chip * per, per) if kind == "scatter" else ins[t]

    def _dst(self, lands, t, chip, slot):
        kind, axis, per = self.meta[t]
        return _axis_slice(lands[t], axis, chip * per, per) if kind == "gather" else lands[t].at[slot]

    def place_own(self):
        n = self.n

        def body(*refs):
            ins, lands, sems = refs[:n], refs[n:2 * n], refs[2 * n]
            me = 2 * lax.axis_index("x") + lax.axis_index("y")
            own = [pltpu.make_async_copy(self._src(ins, t, me), self._dst(lands, t, me, OWN_SLOT), sems.at[t]) for t in range(n)]
            for cp in own:
                cp.start()
            for cp in own:
                cp.wait()

        return pl.pallas_call(
            body, name=self.name + "_own", in_specs=[ANY_SPEC] * n, out_specs=[ANY_SPEC] * n, out_shape=self.land_shapes,
            scratch_shapes=[pltpu.SemaphoreType.DMA((n,))],
        )(*self.srcs)

    def _copies(self, ins, lands, send_sems, recv_sems):
        x, y, c = lax.axis_index("x"), lax.axis_index("y"), lax.axis_index("c")
        me = 2 * x + y
        peers = [(1 - x, y), (x, 1 - y), (1 - x, 1 - y)]
        out, arrive = [], []
        for t in range(self.n):
            for k, (px, py) in enumerate(peers):
                theirs = 2 * px + py
                sems = dict(send_sem=send_sems.at[t * N_PEERS + k], recv_sem=recv_sems.at[t * N_PEERS + k],
                            device_id=(px, py, c), device_id_type=MESH_IDS)
                out.append(pltpu.make_async_remote_copy(src_ref=self._src(ins, t, theirs), dst_ref=self._dst(lands, t, me, k), **sems))
                arrive.append(pltpu.make_async_remote_copy(src_ref=self._src(ins, t, me), dst_ref=self._dst(lands, t, theirs, k), **sems))
        return out, arrive

    def start(self, lands):
        n = self.n

        def body(*refs):
            ins, lnd = refs[:n], refs[n:2 * n]
            send_sems, recv_sems = refs[2 * n], refs[2 * n + 1]
            token = refs[-1]
            out, _ = self._copies(ins, lnd, send_sems, recv_sems)
            for cp in out:
                cp.start()
            token[...] = jnp.zeros_like(token)

        hbm = lambda a: pltpu.HBM(a.shape, a.dtype)
        res = pl.pallas_call(
            body, name=self.name + "_start",
            in_specs=[HBM_SPEC] * (2 * n),
            out_specs=[SEM_SPEC, SEM_SPEC] + [HBM_SPEC] * (2 * n) + [pl.BlockSpec(memory_space=pltpu.VMEM)],
            out_shape=[pltpu.SemaphoreType.DMA((N_PEERS * n,)), pltpu.SemaphoreType.DMA((N_PEERS * n,))]
            + [hbm(a) for a in self.srcs] + [hbm(a) for a in self.land_shapes] + [jax.ShapeDtypeStruct((8, 128), F32)],
            input_output_aliases={i: 2 + i for i in range(2 * n)},
            compiler_params=pltpu.CompilerParams(has_side_effects=pltpu.SideEffectType.DATAFLOW_SIDE_EFFECTING),
        )(*[pltpu.with_memory_space_constraint(a, pltpu.HBM) for a in self.srcs],
          *[pltpu.with_memory_space_constraint(a, pltpu.HBM) for a in lands])
        return res[:-1], res[-1]

    def wait(self, state, after):
        n = self.n
        send_sems, recv_sems = state[0], state[1]
        srcs, lands = state[2:2 + n], state[2 + n:2 + 2 * n]

        def body(*refs):
            ins, lnd = refs[:n], refs[n:2 * n]
            s_sems, r_sems = refs[2 * n], refs[2 * n + 1]
            out, arrive = self._copies(ins, lnd, s_sems, r_sems)
            for cp in arrive:
                cp.wait_recv()
            for cp in out:
                cp.wait_send()

        hbm = lambda a: pltpu.HBM(a.shape, a.dtype)
        res = pl.pallas_call(
            body, name=self.name + "_wait",
            in_specs=[HBM_SPEC] * (2 * n) + [SEM_SPEC, SEM_SPEC, ANY_SPEC],
            out_specs=[HBM_SPEC] * (2 * n),
            out_shape=[hbm(a) for a in self.srcs] + [hbm(a) for a in self.land_shapes],
            input_output_aliases={i: i for i in range(2 * n)},
            compiler_params=pltpu.CompilerParams(has_side_effects=pltpu.SideEffectType.DATAFLOW_SIDE_EFFECTING),
        )(*srcs, *lands, send_sems, recv_sems, after)
        return res[n:]


def _sibling_swap(name, arrays):
    n = len(arrays)

    def body(*refs):
        ins, outs = refs[:n], refs[n:2 * n]
        send_sems, recv_sems = refs[2 * n:]
        sibling = (lax.axis_index("x"), lax.axis_index("y"), 1 - lax.axis_index("c"))
        copies = [pltpu.make_async_remote_copy(src_ref=ins[t], dst_ref=outs[t], send_sem=send_sems.at[t], recv_sem=recv_sems.at[t],
                                               device_id=sibling, device_id_type=MESH_IDS) for t in range(n)]
        for cp in copies:
            cp.start()
        for cp in copies:
            cp.wait()

    any_spec = pl.BlockSpec(memory_space=pl.ANY)
    return pl.pallas_call(
        body,
        name=name,
        in_specs=[any_spec] * n,
        out_specs=[any_spec] * n,
        out_shape=[jax.ShapeDtypeStruct(a.shape, a.dtype) for a in arrays],
        scratch_shapes=[pltpu.SemaphoreType.DMA((n,)), pltpu.SemaphoreType.DMA((n,))],
        compiler_params=pltpu.CompilerParams(has_side_effects=True),
    )(*arrays)


def _row_block(r, c, itemsize=4, target=1 << 20):
    if r % 8 != 0:
        return r
    best = 8
    for tr in range(8, r + 1, 8):
        if r % tr == 0 and tr * c * itemsize <= target:
            best = tr
    return best


def _sum_chips_into(parts, stacked, layer):
    _, r, c = parts.shape
    tr = _row_block(r, c)

    def body(p_ref, s_ref, o_ref):
        del s_ref
        o_ref[...] = ((p_ref[0] + p_ref[1]) + p_ref[2]) + p_ref[3]

    return pl.pallas_call(
        body,
        name="sum_chips",
        grid=(r // tr,),
        in_specs=[pl.BlockSpec((N_CHIPS, tr, c), lambda i: (0, i, 0)), pl.BlockSpec(memory_space=pl.ANY)],
        out_specs=pl.BlockSpec((None, tr, c), lambda i: (layer, i, 0)),
        out_shape=jax.ShapeDtypeStruct(stacked.shape, stacked.dtype),
        input_output_aliases={1: 0},
        compiler_params=_cparams(("parallel",)),
    )(parts, stacked)


def _adamw(w, m, v, g_a, g_b):
    L, r, c = w.shape
    tr = _row_block(r, c, target=1 << 19)

    def body(w_ref, m_ref, v_ref, ga_ref, gb_ref, g_ref, d_ref, nm_ref, nv_ref):
        g = ga_ref[...] + gb_ref[...]
        g_ref[...] = g
        m_new = ADAM_B1 * m_ref[...] + (1.0 - ADAM_B1) * g
        v_new = ADAM_B2 * v_ref[...] + (1.0 - ADAM_B2) * jnp.square(g)
        nm_ref[...] = m_new
        nv_ref[...] = v_new
        m_hat = m_new / (1.0 - ADAM_B1 ** ADAM_STEP)
        v_hat = v_new / (1.0 - ADAM_B2 ** ADAM_STEP)
        d_ref[...] = -ADAM_LR * (m_hat / (jnp.sqrt(v_hat) + ADAM_EPS) + ADAM_WD * w_ref[...])

    spec = pl.BlockSpec((None, tr, c), lambda l, i: (l, i, 0))
    return pl.pallas_call(
        body,
        name="adamw",
        grid=(L, r // tr),
        in_specs=[spec] * 5,
        out_specs=[spec] * 4,
        out_shape=[jax.ShapeDtypeStruct(w.shape, F32)] * 4,
        compiler_params=_cparams(("parallel", "parallel")),
    )(w, m, v, g_a, g_b)


SHARDED = (("w_in", 1), ("conv_w", 1), ("w_mem_k", 0), ("w_mem_v", 0), ("w_branch", 1), ("w_o", 0), ("w_up", 1), ("w_down", 0))
SMALL = ("lower_bounds", "hg_norm_w", "b_gate", "ln1_g", "ln1_b", "ln2_g", "ln2_b")
WEIGHT_ORDER = ("lower_bounds", "w_in", "conv_w", "hg_norm_w", "w_mem_k", "w_mem_v", "w_branch", "b_gate", "w_o", "ln1_g", "ln1_b",
                "w_up", "w_down", "ln2_g", "ln2_b")


def kernel(x, mem, lower_bounds, w_in, conv_w, hg_norm_w, w_mem_k, w_mem_v, w_branch, b_gate, w_o, ln1_g, ln1_b, w_up, w_down, ln2_g, ln2_b, loss_target, m_lower_bounds, m_w_in, m_conv_w, m_hg_norm_w, m_w_mem_k, m_w_mem_v, m_w_branch, m_b_gate, m_w_o, m_ln1_g, m_ln1_b, m_w_up, m_w_down, m_ln2_g, m_ln2_b, v_lower_bounds, v_w_in, v_conv_w, v_hg_norm_w, v_w_mem_k, v_w_mem_v, v_w_branch, v_b_gate, v_w_o, v_ln1_g, v_ln1_b, v_w_up, v_w_down, v_ln2_g, v_ln2_b):
    bl, seq, d = x.shape
    depth = w_in.shape[0]
    weights = dict(lower_bounds=lower_bounds, w_in=w_in, conv_w=conv_w, hg_norm_w=hg_norm_w, w_mem_k=w_mem_k, w_mem_v=w_mem_v,
                   w_branch=w_branch, b_gate=b_gate, w_o=w_o, ln1_g=ln1_g, ln1_b=ln1_b, w_up=w_up, w_down=w_down, ln2_g=ln2_g, ln2_b=ln2_b)
    mom_m = dict(lower_bounds=m_lower_bounds, w_in=m_w_in, conv_w=m_conv_w, hg_norm_w=m_hg_norm_w, w_mem_k=m_w_mem_k, w_mem_v=m_w_mem_v,
                 w_branch=m_w_branch, b_gate=m_b_gate, w_o=m_w_o, ln1_g=m_ln1_g, ln1_b=m_ln1_b, w_up=m_w_up, w_down=m_w_down,
                 ln2_g=m_ln2_g, ln2_b=m_ln2_b)
    mom_v = dict(lower_bounds=v_lower_bounds, w_in=v_w_in, conv_w=v_conv_w, hg_norm_w=v_hg_norm_w, w_mem_k=v_w_mem_k, w_mem_v=v_w_mem_v,
                 w_branch=v_w_branch, b_gate=v_b_gate, w_o=v_w_o, ln1_g=v_ln1_g, ln1_b=v_ln1_b, w_up=v_w_up, w_down=v_w_down,
                 ln2_g=v_ln2_g, ln2_b=v_ln2_b)

    def shard2d(name, l):
        w = weights[name][l]
        if name == "w_branch":
            return w.reshape(N_BRANCH * W, w.shape[-1]).astype(BF16)
        return w if name == "conv_w" else w.astype(BF16)

    def start_gather(l):
        ex = _Split(f"gather_weights_l{l}", [(shard2d(name, l), "gather", axis) for name, axis in SHARDED])
        state, token = ex.start(ex.place_own())
        return ex, state, token

    def layer_weights(l, gathered):
        wts = dict(zip([name for name, _ in SHARDED], gathered))
        for name in ("hg_norm_w", "b_gate", "ln1_g", "ln1_b", "ln2_g", "ln2_b"):
            wts[name] = weights[name][l][None, :]
        return wts

    x2d, mem2, t2d = x.reshape(bl * seq, d), mem.reshape(-1, d), loss_target.reshape(bl * seq, d)
    alpha = (2.0 * depth) ** 0.25
    soft, lb_all = _lower_bounds_fwd(lower_bounds)

    h, saved, layer_wts = x2d, [], []
    pending = start_gather(0)
    for l in range(depth):
        ex, state, _ = pending
        wts = layer_weights(l, ex.wait(state, after=h))
        layer_wts.append(wts)
        deps = ()
        if l + 1 < depth:
            pending = start_gather(l + 1)
            deps = (pending[2],)
        h, sv = _layer_fwd(h, mem2, lb_all[l:l + 1], wts, bl=bl, seq=seq, alpha=alpha, deps=deps)
        saved.append(sv)
    loss, dh = _loss_head(h, t2d)

    partial = [jnp.zeros((depth, weights[name].size // (depth * weights[name].shape[-1]), weights[name].shape[-1]), F32)
               for name, _ in SHARDED]
    smalls = [None] * depth
    pending, deps = None, ()

    def finish_reduce(pend, after):
        ex, state, l = pend
        got = ex.wait(state, after=after)
        return [_sum_chips_into(p, acc, l) for p, acc in zip(got, partial)]

    for l in reversed(range(depth)):
        dh, g = _layer_bwd(dh, saved[l], mem2, lb_all[l:l + 1], layer_wts[l], bl=bl, seq=seq, alpha=alpha, deps=deps)
        g["lower_bounds"] = g.pop("lb")
        smalls[l] = jnp.concatenate([g[nm] for nm in SMALL], axis=1)
        if pending is not None:
            partial = finish_reduce(pending, dh)
        ex = _Split(f"reduce_grads_l{l}", [(g[name], "scatter", axis) for name, axis in SHARDED])
        state, token = ex.start(ex.place_own())
        pending, deps = (ex, state, l), (token,)
    partial = finish_reduce(pending, dh)
    small_parts = _chip_exchange("reduce_small", [(jnp.stack(smalls), "bcast", 0)])[0]
    small_sum = _sum_chips_into(small_parts.reshape(N_CHIPS, depth, -1), jnp.zeros((1, depth, small_parts.shape[-1]), F32), 0)
    partial = partial + [small_sum.reshape(depth, 1, -1)]
    theirs = _sibling_swap("swap_partials", partial)

    outs = {}
    for i, (name, _) in enumerate(SHARDED):
        shape3 = partial[i].shape
        res = _adamw(weights[name].reshape(shape3), mom_m[name].reshape(shape3), mom_v[name].reshape(shape3), partial[i], theirs[i])
        outs[name] = [r.reshape(weights[name].shape) for r in res]
    off = 0
    for name in SMALL:
        n = weights[name].shape[1]
        mine, other = partial[-1][:, :, off:off + n], theirs[-1][:, :, off:off + n]
        off += n
        if name == "lower_bounds":
            mine = _lower_bounds_bwd(soft, mine[:, 0, :])[:, None, :]
            other = _lower_bounds_bwd(soft, other[:, 0, :])[:, None, :]
        shape3 = (depth, 1, n)
        res = _adamw(weights[name].reshape(shape3), mom_m[name].reshape(shape3), mom_v[name].reshape(shape3), mine, other)
        outs[name] = [r.reshape(weights[name].shape) for r in res]
    assert off == partial[-1].shape[-1]

    total_loss = lax.psum(loss[0, 0], ("x", "y", "c"))
    result = [total_loss, dh.reshape(bl, seq, d)]
    for k in range(4):
        result += [outs[name][k] for name in WEIGHT_ORDER]
    return tuple(result)
```

```python
import functools

import jax
import jax.numpy as jnp
from jax import lax
from jax.experimental import pallas as pl
from jax.experimental.pallas import tpu as pltpu

F32 = jnp.float32
BF16 = jnp.bfloat16

HG_HEADS = 4
HG_F = 128
HG_CHUNK = 32
MEM_HEADS = 4
MEM_HEAD_DIM = 128
BRANCH_WIDTH = 512
N_BRANCH = 3
CONV_K = 3
LN_EPS = 1e-5
RMS_EPS = 1e-6
ADAM_LR = 0.001
ADAM_B1 = 0.9
ADAM_B2 = 0.999
ADAM_EPS = 1e-08
ADAM_WD = 0.01
ADAM_STEP = 10

VMEM_LIMIT = 48 * 1024 * 1024


def _cparams(sem):
    return pltpu.CompilerParams(dimension_semantics=sem, vmem_limit_bytes=VMEM_LIMIT)


def _dot(a, b, dims):
    return lax.dot_general(a, b, (dims, ((), ())), preferred_element_type=F32)


NN = ((1,), (0,))
NT = ((1,), (1,))
TN = ((0,), (0,))


def _pick(n, pref):
    for t in pref:
        if n % t == 0:
            return t
    return n


ANY_SPEC = pl.BlockSpec(memory_space=pl.ANY)


def _matmul(name, a, b, *, mode, out_dtype=F32, a_fn=None, a_extra=(), epi_fn=None, epi_extra=(), n_out=1,
            tm=512, tn=1024, tk=1024, deps=()):
    M, K = a.shape
    N = b.shape[1] if mode == "nn" else b.shape[0]
    tm, tn, tk = _pick(M, (tm, 256, 128, 8)), _pick(N, (tn, 896, 512, 256, 128)), _pick(K, (tk, 512, 256, 128))
    nk = K // tk
    n_ax, n_ex = len(a_extra), len(epi_extra)
    n_in = 2 + n_ax + n_ex + len(deps)
    out_dtypes = out_dtype if isinstance(out_dtype, (tuple, list)) else (out_dtype,) * n_out

    def body(*refs):
        a_ref, b_ref = refs[0], refs[1]
        ax_refs = refs[2:2 + n_ax]
        ex_refs = refs[2 + n_ax:2 + n_ax + n_ex]
        o_refs = refs[n_in:n_in + n_out]
        acc_ref = refs[-1]
        k = pl.program_id(2)
        at = a_ref[...]
        at = a_fn(at, *[r[...] for r in ax_refs]) if a_fn is not None else at.astype(BF16)
        part = _dot(at, b_ref[...].astype(BF16), NN if mode == "nn" else NT)

        @pl.when(k == 0)
        def _():
            acc_ref[...] = part

        @pl.when(k > 0)
        def _():
            acc_ref[...] += part

        @pl.when(k == nk - 1)
        def _():
            acc = acc_ref[...]
            outs = epi_fn(acc, *[r[...] for r in ex_refs]) if epi_fn is not None else (acc,)
            for o_ref, o in zip(o_refs, outs):
                o_ref[...] = o.astype(o_ref.dtype)

    in_specs = [pl.BlockSpec((tm, tk), lambda j, i, k: (i, k)),
                pl.BlockSpec((tk, tn), lambda j, i, k: (k, j)) if mode == "nn" else pl.BlockSpec((tn, tk), lambda j, i, k: (j, k))]
    in_specs += [pl.BlockSpec((1, tk), lambda j, i, k: (0, k)) for _ in a_extra]
    for e in epi_extra:
        if e.shape[0] == 1:
            in_specs.append(pl.BlockSpec((1, tn), lambda j, i, k: (0, j)))
        else:
            in_specs.append(pl.BlockSpec((tm, tn), lambda j, i, k: (i, j)))
    in_specs += [ANY_SPEC] * len(deps)
    out = pl.pallas_call(
        body,
        name=name,
        grid=(N // tn, M // tm, nk),
        in_specs=in_specs,
        out_specs=[pl.BlockSpec((tm, tn), lambda j, i, k: (i, j)) for _ in range(n_out)],
        out_shape=[jax.ShapeDtypeStruct((M, N), dt) for dt in out_dtypes],
        scratch_shapes=[pltpu.VMEM((tm, tn), F32)],
        compiler_params=_cparams(("parallel", "parallel", "arbitrary")),
    )(a, b, *a_extra, *epi_extra, *deps)
    return out[0] if n_out == 1 else out


def _matmul_tn(name, a, b, *, a_fn=None, a_extra=(), a_cols=None, b_cols=None, ta=1024, tb=1024, tt=1024):
    T = a.shape[0]
    a0, Ka = a_cols if a_cols is not None else (0, a.shape[1])
    b0, Nb = b_cols if b_cols is not None else (0, b.shape[1])
    ta, tb, tt = _pick(Ka, (ta, 512, 256, 128)), _pick(Nb, (tb, 896, 512, 256, 128)), _pick(T, (tt, 512, 256, 128))
    assert a0 % ta == 0 and b0 % tb == 0
    a0, b0 = a0 // ta, b0 // tb
    nt = T // tt
    n_ax = len(a_extra)

    def body(*refs):
        a_ref, b_ref = refs[0], refs[1]
        ax_refs = refs[2:2 + n_ax]
        o_ref = refs[2 + n_ax]
        t = pl.program_id(2)
        at = a_ref[...]
        at = a_fn(at, *[r[...] for r in ax_refs]) if a_fn is not None else at.astype(BF16)
        part = _dot(at, b_ref[...].astype(BF16), TN)

        @pl.when(t == 0)
        def _():
            o_ref[...] = part

        @pl.when(t > 0)
        def _():
            o_ref[...] += part

    in_specs = [pl.BlockSpec((tt, ta), lambda i, j, t: (t, a0 + i)), pl.BlockSpec((tt, tb), lambda i, j, t: (t, b0 + j))]
    in_specs += [pl.BlockSpec((1, ta), lambda i, j, t: (0, a0 + i)) for _ in a_extra]
    return pl.pallas_call(
        body,
        name=name,
        grid=(Ka // ta, Nb // tb, nt),
        in_specs=in_specs,
        out_specs=pl.BlockSpec((ta, tb), lambda i, j, t: (i, j)),
        out_shape=jax.ShapeDtypeStruct((Ka, Nb), F32),
        compiler_params=_cparams(("parallel", "parallel", "arbitrary")),
    )(a, b, *a_extra)


W = BRANCH_WIDTH
C_CB, C_CC, C_CH, C_HQ, C_HF, C_HI, C_HG, C_MQ, N_MIX = 0, W, 2 * W, 3 * W, 4 * W, 5 * W, 6 * W, 7 * W, 8 * W
TS_MIX = 256


def _sigmoid(x):
    return jax.nn.sigmoid(x)


def _chunk_pos(shape):
    return lax.broadcasted_iota(jnp.int32, shape, 0) & (HG_CHUNK - 1)


def _seg_cumsum(x, pos):
    sh = 1
    while sh < HG_CHUNK:
        x = x + jnp.where(pos >= sh, pltpu.roll(x, sh, 0), 0.0)
        sh *= 2
    return x


def _seg_rev_cumsum(x, pos):
    n = x.shape[0]
    sh = 1
    while sh < HG_CHUNK:
        x = x + jnp.where(pos < HG_CHUNK - sh, pltpu.roll(x, n - sh, 0), 0.0)
        sh *= 2
    return x


def _chunk_mask(ts):
    r = lax.broadcasted_iota(jnp.int32, (ts, ts), 0)
    c = lax.broadcasted_iota(jnp.int32, (ts, ts), 1)
    return jnp.logical_and((r // HG_CHUNK) == (c // HG_CHUNK), c <= r)


def _hgrn_gates(p_ref, lb):
    q = p_ref[:, C_HQ:C_HQ + W]
    fl = p_ref[:, C_HF:C_HF + W]
    sig = _sigmoid(fl)
    f = lb + (1.0 - lb) * sig
    logf = jnp.log(f)
    k = (1.0 - lb) * _sigmoid(-fl)
    sq = _sigmoid(q)
    qs = q * sq
    return q, sq, qs, sig, f, logf, k


def _hgrn_decays(logf, bc_sc, ts):
    pos = _chunk_pos(logf.shape)
    bc = _seg_cumsum(logf, pos)
    bc_sc[...] = bc
    nc = ts // HG_CHUNK
    bref = jnp.concatenate(
        [jnp.broadcast_to(bc_sc[n * HG_CHUNK + HG_CHUNK // 2 - 1:n * HG_CHUNK + HG_CHUNK // 2, :], (HG_CHUNK, W)) for n in range(nc)], axis=0)
    blast = jnp.concatenate(
        [jnp.broadcast_to(bc_sc[(n + 1) * HG_CHUNK - 1:(n + 1) * HG_CHUNK, :], (HG_CHUNK, W)) for n in range(nc)], axis=0)
    return pos, bc, bref, blast


def _conv_shift_down(u, carry_ref, row):
    u1 = jnp.where(row == 0, carry_ref[7:8, :], pltpu.roll(u, 1, 0))
    u2 = jnp.where(row == 0, carry_ref[6:7, :], jnp.where(row == 1, carry_ref[7:8, :], pltpu.roll(u, 2, 0)))
    return u1, u2


def _attn_probs(qh, kh):
    s = _dot(qh, kh, NT) * (MEM_HEAD_DIM ** -0.5)
    e = jnp.exp(s - jnp.max(s, axis=-1, keepdims=True))
    return e / jnp.sum(e, axis=-1, keepdims=True)


def _mixer_fwd(p, mk, mv, lb, conv_w, norm_w, *, bl, seq):
    T = p.shape[0]
    ts = TS_MIX
    ns = seq // ts
    nc = ts // HG_CHUNK
    ml = mk.shape[0] // bl

    def body(p_ref, mk_ref, mv_ref, lb_ref, cw_ref, nw_ref, y_ref, st_ref, opre_ref, state_sc, carry_sc, bc_sc):
        @pl.when(pl.program_id(1) == 0)
        def _():
            state_sc[...] = jnp.zeros_like(state_sc)
            carry_sc[...] = jnp.zeros_like(carry_sc)

        cb, cc, ch = p_ref[:, C_CB:C_CB + W], p_ref[:, C_CC:C_CC + W], p_ref[:, C_CH:C_CH + W]
        u = cc * ch
        row = lax.broadcasted_iota(jnp.int32, (ts, W), 0)
        u1, u2 = _conv_shift_down(u, carry_sc, row)
        yconv = u2 * cw_ref[0:1, :] + u1 * cw_ref[1:2, :] + u * cw_ref[2:3, :]
        y_ref[:, 0:W] = (cb * yconv).astype(BF16)
        carry_sc[...] = u[ts - 8:ts, :]

        lbv = lb_ref[...]
        _, _, qs, _, _, logf, k = _hgrn_gates(p_ref, lbv)
        pos, bc, bref, blast = _hgrn_decays(logf, bc_sc, ts)
        a_all = (qs * jnp.exp(bc - bref)).astype(BF16)
        bk_all = (k * jnp.exp(bref - bc)).astype(BF16)
        qin_all = (qs * jnp.exp(bc)).astype(BF16)
        kout_all = k * jnp.exp(blast - bc)
        v_all = p_ref[:, C_HI:C_HI + W]
        mask = _chunk_mask(ts)
        chunk_of_row = lax.broadcasted_iota(jnp.int32, (ts, HG_F), 0) // HG_CHUNK
        for h in range(HG_HEADS):
            hs = slice(h * HG_F, (h + 1) * HG_F)
            vb = v_all[:, hs].astype(BF16)
            vt = v_all[:, hs].T.astype(BF16)
            scores = jnp.where(mask, _dot(a_all[:, hs], bk_all[:, hs], NT), 0.0)
            o_intra = _dot(scores.astype(BF16), vb, NN)
            kout = kout_all[:, hs]
            st = state_sc[h]
            o_inter = []
            for n in range(nc):
                st_ref[n, h] = st
                o_inter.append(_dot(qin_all[n * HG_CHUNK:(n + 1) * HG_CHUNK, hs], st.astype(BF16), NT))
                kv = _dot(vt, jnp.where(chunk_of_row == n, kout, 0.0).astype(BF16), NN)
                decay = jnp.exp(bc_sc[(n + 1) * HG_CHUNK - 1:(n + 1) * HG_CHUNK, hs])
                st = st * decay + kv
            state_sc[h] = st
            o = o_intra + jnp.concatenate(o_inter, axis=0)
            opre_ref[:, hs] = o
            on = o * lax.rsqrt(jnp.mean(o * o, axis=-1, keepdims=True) + RMS_EPS) * nw_ref[...]
            g = p_ref[:, C_HG + h * HG_F:C_HG + (h + 1) * HG_F]
            y_ref[:, W + h * HG_F:W + (h + 1) * HG_F] = (on * (g * _sigmoid(g))).astype(BF16)

        for h in range(MEM_HEADS):
            hs = slice(h * MEM_HEAD_DIM, (h + 1) * MEM_HEAD_DIM)
            qh = p_ref[:, C_MQ + h * MEM_HEAD_DIM:C_MQ + (h + 1) * MEM_HEAD_DIM].astype(BF16)
            prob = _attn_probs(qh, mk_ref[:, hs])
            y_ref[:, 2 * W + h * MEM_HEAD_DIM:2 * W + (h + 1) * MEM_HEAD_DIM] = _dot(prob.astype(BF16), mv_ref[:, hs], NN).astype(BF16)

    return pl.pallas_call(
        body,
        name="mixer_fwd",
        grid=(bl, ns),
        in_specs=[
            pl.BlockSpec((ts, N_MIX), lambda b, s: (b * ns + s, 0)),
            pl.BlockSpec((ml, W), lambda b, s: (b, 0)),
            pl.BlockSpec((ml, W), lambda b, s: (b, 0)),
            pl.BlockSpec((1, W), lambda b, s: (0, 0)),
            pl.BlockSpec((CONV_K, W), lambda b, s: (0, 0)),
            pl.BlockSpec((1, HG_F), lambda b, s: (0, 0)),
        ],
        out_specs=[
            pl.BlockSpec((ts, 3 * W), lambda b, s: (b * ns + s, 0)),
            pl.BlockSpec((nc, HG_HEADS, HG_F, HG_F), lambda b, s: (b * ns + s, 0, 0, 0)),
            pl.BlockSpec((ts, W), lambda b, s: (b * ns + s, 0)),
        ],
        out_shape=[
            jax.ShapeDtypeStruct((T, 3 * W), BF16),
            jax.ShapeDtypeStruct((T // HG_CHUNK, HG_HEADS, HG_F, HG_F), F32),
            jax.ShapeDtypeStruct((T, W), F32),
        ],
        scratch_shapes=[pltpu.VMEM((HG_HEADS, HG_F, HG_F), F32), pltpu.VMEM((8, W), F32), pltpu.VMEM((ts, W), F32)],
        compiler_params=_cparams(("arbitrary", "arbitrary")),
    )(p, mk, mv, lb, conv_w, norm_w)


def _mixer_bwd(p, dy, dp_gates, st, opre, mk, mv, lb, conv_w, norm_w, *, bl, seq):
    T, nin = p.shape
    ts = TS_MIX
    ns = seq // ts
    nc = ts // HG_CHUNK
    ml = mk.shape[0] // bl
    mid, last = HG_CHUNK // 2 - 1, HG_CHUNK - 1

    def body(p_ref, pprev_ref, dy_ref, dpin_ref, st_ref, opre_ref, mk_ref, mv_ref, lb_ref, cw_ref, nw_ref,
             dp_ref, dmk_ref, dmv_ref, dcw_ref, dnw_ref, dlb_ref, dstate_sc, carry_sc, uprev_sc, bc_sc):
        del dpin_ref
        b, s = pl.program_id(0), pl.program_id(1)

        @pl.when(s == 0)
        def _():
            dstate_sc[...] = jnp.zeros_like(dstate_sc)
            carry_sc[...] = jnp.zeros_like(carry_sc)
            dmk_ref[...] = jnp.zeros_like(dmk_ref)
            dmv_ref[...] = jnp.zeros_like(dmv_ref)

        @pl.when(jnp.logical_and(b == 0, s == 0))
        def _():
            dcw_ref[...] = jnp.zeros_like(dcw_ref)
            dnw_ref[...] = jnp.zeros_like(dnw_ref)
            dlb_ref[...] = jnp.zeros_like(dlb_ref)

        cb, cc, ch = p_ref[:, C_CB:C_CB + W], p_ref[:, C_CC:C_CC + W], p_ref[:, C_CH:C_CH + W]
        u = cc * ch
        row = lax.broadcasted_iota(jnp.int32, (ts, W), 0)
        uprev = pprev_ref[:, C_CC:C_CC + W] * pprev_ref[:, C_CH:C_CH + W]
        uprev_sc[...] = jnp.where(s == ns - 1, 0.0, uprev)
        u1, u2 = _conv_shift_down(u, uprev_sc, row)
        w0, w1, w2 = cw_ref[0:1, :], cw_ref[1:2, :], cw_ref[2:3, :]
        dya = dy_ref[:, 0:W]
        dp_ref[:, C_CB:C_CB + W] = (dya * (u2 * w0 + u1 * w1 + u * w2)).astype(BF16)
        dv = cb * dya
        dv1 = jnp.where(row == ts - 1, carry_sc[0:1, :], pltpu.roll(dv, ts - 1, 0))
        dv2 = jnp.where(row == ts - 1, carry_sc[1:2, :], jnp.where(row == ts - 2, carry_sc[0:1, :], pltpu.roll(dv, ts - 2, 0)))
        du = dv * w2 + dv1 * w1 + dv2 * w0
        dp_ref[:, C_CC:C_CC + W] = (du * ch).astype(BF16)
        dp_ref[:, C_CH:C_CH + W] = (du * cc).astype(BF16)
        dcw_ref[0:1, :] += jnp.sum(dv * u2, axis=0, keepdims=True)
        dcw_ref[1:2, :] += jnp.sum(dv * u1, axis=0, keepdims=True)
        dcw_ref[2:3, :] += jnp.sum(dv * u, axis=0, keepdims=True)
        carry_sc[...] = dv[0:8, :]

        lbv = lb_ref[...]
        q_all, sq_all, qs_all, sig_all, f_all, logf, k_all = _hgrn_gates(p_ref, lbv)
        pos_all, bc, bref, blast = _hgrn_decays(logf, bc_sc, ts)
        ea_all, eb_all, eq_all, ek_all = jnp.exp(bc - bref), jnp.exp(bref - bc), jnp.exp(bc), jnp.exp(blast - bc)
        mask = _chunk_mask(ts)
        chunk_of_row = lax.broadcasted_iota(jnp.int32, (ts, HG_F), 0) // HG_CHUNK
        pos = _chunk_pos((ts, HG_F))
        pos_c = _chunk_pos((HG_CHUNK, HG_F))
        nw = nw_ref[...]
        for h in range(HG_HEADS):
            hs = slice(h * HG_F, (h + 1) * HG_F)
            qs, k, ea, eb, eq, ek = qs_all[:, hs], k_all[:, hs], ea_all[:, hs], eb_all[:, hs], eq_all[:, hs], ek_all[:, hs]
            a, bk, qin, kout = qs * ea, k * eb, qs * eq, k * ek
            o = opre_ref[:, hs]
            g = p_ref[:, C_HG + h * HG_F:C_HG + (h + 1) * HG_F]
            sg = _sigmoid(g)
            r = lax.rsqrt(jnp.mean(o * o, axis=-1, keepdims=True) + RMS_EPS)
            dyb = dy_ref[:, W + h * HG_F:W + (h + 1) * HG_F]
            dp_ref[:, C_HG + h * HG_F:C_HG + (h + 1) * HG_F] = (dyb * (o * r * nw) * (sg * (1.0 + g * (1.0 - sg)))).astype(BF16)
            don = dyb * (g * sg)
            dnw_ref[0:1, :] += jnp.sum(don * o * r, axis=0, keepdims=True)
            dn = don * nw
            do = r * (dn - o * (r * r) * jnp.mean(dn * o, axis=-1, keepdims=True))
            dob = do.astype(BF16)
            dot_ = do.T.astype(BF16)
            vb = p_ref[:, C_HI + h * HG_F:C_HI + (h + 1) * HG_F].astype(BF16)
            ab, bkb = a.astype(BF16), bk.astype(BF16)
            scores = jnp.where(mask, _dot(ab, bkb, NT), 0.0)
            dscores = jnp.where(mask, _dot(dob, vb, NT), 0.0)
            dv_h = _dot(scores.T.astype(BF16), dob, NN)
            da = _dot(dscores.astype(BF16), bkb, NN)
            dbk = _dot(dscores.T.astype(BF16), ab, NN)
            koutb = kout.astype(BF16)
            dst = dstate_sc[h]
            dqin_p, dkout_p, dvi_p, ddec_p = [None] * nc, [None] * nc, [None] * nc, [None] * nc
            for n in reversed(range(nc)):
                rows = slice(n * HG_CHUNK, (n + 1) * HG_CHUNK)
                st_n = st_ref[n, h]
                decay = jnp.exp(bc_sc[n * HG_CHUNK + last:n * HG_CHUNK + last + 1, hs])
                dstb = dst.astype(BF16)
                dvi_p[n] = _dot(koutb[rows], dstb, NT)
                dkout_p[n] = _dot(vb[rows], dstb, NN)
                ddec_p[n] = jnp.sum(dst * st_n, axis=0, keepdims=True) * decay
                dqin_p[n] = _dot(dob[rows], st_n.astype(BF16), NN)
                dst = dst * decay + _dot(dot_, jnp.where(chunk_of_row == n, qin, 0.0).astype(BF16), NN)
            dstate_sc[h] = dst
            dqin = jnp.concatenate(dqin_p, axis=0)
            dkout = jnp.concatenate(dkout_p, axis=0)
            dp_ref[:, C_HI + h * HG_F:C_HI + (h + 1) * HG_F] = (dv_h + jnp.concatenate(dvi_p, axis=0)).astype(BF16)
            dqs = da * ea + dqin * eq
            dk = dbk * eb + dkout * ek
            t_a, t_b, t_q, t_k = da * a, dbk * bk, dqin * qin, dkout * kout
            dbc = t_a - t_b + t_q - t_k
            t_ref = t_b - t_a
            pieces = []
            for n in range(nc):
                rows = slice(n * HG_CHUNK, (n + 1) * HG_CHUNK)
                s_ref = jnp.sum(t_ref[rows], axis=0, keepdims=True)
                s_last = jnp.sum(t_k[rows], axis=0, keepdims=True) + ddec_p[n]
                pieces.append(dbc[rows] + jnp.where(pos_c == mid, s_ref, 0.0) + jnp.where(pos_c == last, s_last, 0.0))
            dlogf = _seg_rev_cumsum(jnp.concatenate(pieces, axis=0), pos)
            sig, lbh = sig_all[:, hs], lbv[:, hs]
            dfk = dlogf / f_all[:, hs] - dk
            dp_ref[:, C_HF + h * HG_F:C_HF + (h + 1) * HG_F] = (dfk * (1.0 - lbh) * sig * (1.0 - sig)).astype(BF16)
            dlb_ref[0:1, hs] += jnp.sum(dfk * (1.0 - sig), axis=0, keepdims=True)
            q, sq = q_all[:, hs], sq_all[:, hs]
            dp_ref[:, C_HQ + h * HG_F:C_HQ + (h + 1) * HG_F] = (dqs * (sq * (1.0 + q * (1.0 - sq)))).astype(BF16)

        for h in range(MEM_HEADS):
            hs = slice(h * MEM_HEAD_DIM, (h + 1) * MEM_HEAD_DIM)
            qh = p_ref[:, C_MQ + h * MEM_HEAD_DIM:C_MQ + (h + 1) * MEM_HEAD_DIM].astype(BF16)
            kh, vh = mk_ref[:, hs], mv_ref[:, hs]
            prob = _attn_probs(qh, kh)
            dob = dy_ref[:, 2 * W + h * MEM_HEAD_DIM:2 * W + (h + 1) * MEM_HEAD_DIM].astype(BF16)
            dmv_ref[:, hs] += _dot(prob.T.astype(BF16), dob, NN)
            dprob = _dot(dob, vh, NT)
            ds = prob * (dprob - jnp.sum(dprob * prob, axis=-1, keepdims=True)) * (MEM_HEAD_DIM ** -0.5)
            dp_ref[:, C_MQ + h * MEM_HEAD_DIM:C_MQ + (h + 1) * MEM_HEAD_DIM] = _dot(ds.astype(BF16), kh, NN).astype(BF16)
            dmk_ref[:, hs] += _dot(ds.T.astype(BF16), qh, NN)

    def tile(b, s):
        return b * ns + (ns - 1 - s)

    return pl.pallas_call(
        body,
        name="mixer_bwd",
        grid=(bl, ns),
        in_specs=[
            pl.BlockSpec((ts, N_MIX), lambda b, s: (tile(b, s), 0)),
            pl.BlockSpec((8, N_MIX), lambda b, s: (jnp.maximum(tile(b, s) * (ts // 8) - 1, 0), 0)),
            pl.BlockSpec((ts, 3 * W), lambda b, s: (tile(b, s), 0)),
            pl.BlockSpec(memory_space=pl.ANY),
            pl.BlockSpec((nc, HG_HEADS, HG_F, HG_F), lambda b, s: (tile(b, s), 0, 0, 0)),
            pl.BlockSpec((ts, W), lambda b, s: (tile(b, s), 0)),
            pl.BlockSpec((ml, W), lambda b, s: (b, 0)),
            pl.BlockSpec((ml, W), lambda b, s: (b, 0)),
            pl.BlockSpec((1, W), lambda b, s: (0, 0)),
            pl.BlockSpec((CONV_K, W), lambda b, s: (0, 0)),
            pl.BlockSpec((1, HG_F), lambda b, s: (0, 0)),
        ],
        out_specs=[
            pl.BlockSpec((ts, N_MIX), lambda b, s: (tile(b, s), 0)),
            pl.BlockSpec((ml, W), lambda b, s: (b, 0)),
            pl.BlockSpec((ml, W), lambda b, s: (b, 0)),
            pl.BlockSpec((8, W), lambda b, s: (0, 0)),
            pl.BlockSpec((8, HG_F), lambda b, s: (0, 0)),
            pl.BlockSpec((8, W), lambda b, s: (0, 0)),
        ],
        out_shape=[
            jax.ShapeDtypeStruct((T, nin), BF16),
            jax.ShapeDtypeStruct((bl * ml, W), F32),
            jax.ShapeDtypeStruct((bl * ml, W), F32),
            jax.ShapeDtypeStruct((8, W), F32),
            jax.ShapeDtypeStruct((8, HG_F), F32),
            jax.ShapeDtypeStruct((8, W), F32),
        ],
        input_output_aliases={3: 0},
        scratch_shapes=[pltpu.VMEM((HG_HEADS, HG_F, HG_F), F32), pltpu.VMEM((8, W), F32), pltpu.VMEM((8, W), F32),
                        pltpu.VMEM((ts, W), F32)],
        compiler_params=_cparams(("arbitrary", "arbitrary")),
    )(p, p, dy, dp_gates, st, opre, mk, mv, lb, conv_w, norm_w)


def _layer_norm_stats(z):
    mu = jnp.mean(z, axis=-1, keepdims=True)
    zc = z - mu
    rstd = lax.rsqrt(jnp.mean(zc * zc, axis=-1, keepdims=True) + LN_EPS)
    return zc * rstd, rstd


def _gate_specs(tm, d):
    g0 = N_MIX // d
    return [pl.BlockSpec((tm, d), functools.partial(lambda i, k: (i, g0 + k), k=k)) for k in range(N_BRANCH)]


def _merge_fwd(y, p, x0, wb, wo, bg, *, alpha, tm=256):
    T, d = x0.shape
    assert N_MIX % d == 0
    tm = _pick(T, (tm, 128, 8))

    def body(y_ref, g0_ref, g1_ref, g2_ref, x_ref, wb_ref, wo_ref, bg_ref, r_ref, mg_ref, xh_ref, rs_ref):
        merged = None
        for i, g_ref in enumerate((g0_ref, g1_ref, g2_ref)):
            r = _dot(y_ref[:, i * W:(i + 1) * W], wb_ref[i * W:(i + 1) * W, :], NN)
            r_ref[:, i * d:(i + 1) * d] = r
            t = _sigmoid(g_ref[...] + bg_ref[:, i * d:(i + 1) * d]) * r
            merged = t if merged is None else merged + t
        mb = merged.astype(BF16)
        mg_ref[...] = mb
        z = alpha * x_ref[...] + _dot(mb, wo_ref[...], NN)
        xh_ref[...], rs_ref[...] = _layer_norm_stats(z)

    row = lambda i: (i, 0)
    fix = lambda i: (0, 0)
    return pl.pallas_call(
        body,
        name="merge_fwd",
        grid=(T // tm,),
        in_specs=[pl.BlockSpec((tm, 3 * W), row)] + _gate_specs(tm, d) + [
            pl.BlockSpec((tm, d), row), pl.BlockSpec((3 * W, d), fix), pl.BlockSpec((d, d), fix), pl.BlockSpec((1, 3 * d), fix)],
        out_specs=[pl.BlockSpec((tm, 3 * d), row), pl.BlockSpec((tm, d), row), pl.BlockSpec((tm, d), row), pl.BlockSpec((tm, 1), row)],
        out_shape=[jax.ShapeDtypeStruct((T, 3 * d), F32), jax.ShapeDtypeStruct((T, d), BF16),
                   jax.ShapeDtypeStruct((T, d), F32), jax.ShapeDtypeStruct((T, 1), F32)],
        compiler_params=_cparams(("parallel",)),
    )(y, p, p, p, x0, wb, wo, bg)


def _merge_bwd(dz, p, r, wb, wo, bg, *, tm=256):
    T, d = dz.shape
    nin = p.shape[1]
    tm = _pick(T, (tm, 128, 8))

    def body(dz_ref, g0_ref, g1_ref, g2_ref, r_ref, wb_ref, wo_ref, bg_ref, dr_ref, dp_ref, dy_ref, dbg_ref):
        @pl.when(pl.program_id(0) == 0)
        def _():
            dbg_ref[...] = jnp.zeros_like(dbg_ref)

        dmerged = _dot(dz_ref[...].astype(BF16), wo_ref[...], NT)
        dp_ref[:, 0:N_MIX] = jnp.zeros((tm, N_MIX), BF16)
        for i, g_ref in enumerate((g0_ref, g1_ref, g2_ref)):
            cs = slice(i * d, (i + 1) * d)
            s = _sigmoid(g_ref[...] + bg_ref[:, cs])
            drb = (dmerged * s).astype(BF16)
            dr_ref[:, cs] = drb
            dgate = dmerged * r_ref[:, cs] * s * (1.0 - s)
            dp_ref[:, N_MIX + i * d:N_MIX + (i + 1) * d] = dgate.astype(BF16)
            dbg_ref[0:1, cs] += jnp.sum(dgate, axis=0, keepdims=True)
            dy_ref[:, i * W:(i + 1) * W] = _dot(drb, wb_ref[i * W:(i + 1) * W, :], NT)

    row = lambda i: (i, 0)
    fix = lambda i: (0, 0)
    return pl.pallas_call(
        body,
        name="merge_bwd",
        grid=(T // tm,),
        in_specs=[pl.BlockSpec((tm, d), row)] + _gate_specs(tm, d) + [
            pl.BlockSpec((tm, 3 * d), row), pl.BlockSpec((3 * W, d), fix), pl.BlockSpec((d, d), fix), pl.BlockSpec((1, 3 * d), fix)],
        out_specs=[pl.BlockSpec((tm, 3 * d), row), pl.BlockSpec((tm, nin), row), pl.BlockSpec((tm, 3 * W), row),
                   pl.BlockSpec((8, 3 * d), fix)],
        out_shape=[jax.ShapeDtypeStruct((T, 3 * d), BF16), jax.ShapeDtypeStruct((T, nin), BF16),
                   jax.ShapeDtypeStruct((T, 3 * W), F32), jax.ShapeDtypeStruct((8, 3 * d), F32)],
        compiler_params=_cparams(("arbitrary",)),
    )(dz, p, p, p, r, wb, wo, bg)


def _mlp_fwd(xhat1, g1, b1, wu, wd, g2, b2, *, alpha, tm=512, tf=1024):
    T, d = xhat1.shape
    ff = wu.shape[1]
    tm, tf = _pick(T, (tm, 256, 128, 8)), _pick(ff, (tf, 512, 256, 128))
    nf = ff // tf

    def body(xh_ref, g1_ref, b1_ref, wu_ref, wd_ref, g2_ref, b2_ref, a_ref, xh2_ref, rs2_ref, x2_ref, acc_ref):
        f = pl.program_id(1)
        x1 = xh_ref[...] * g1_ref[...] + b1_ref[...]
        a = _dot(x1.astype(BF16), wu_ref[...], NN)
        a_ref[...] = a.astype(BF16)
        h = jnp.square(jnp.maximum(a, 0.0))
        part = _dot(h.astype(BF16), wd_ref[...], NN)

        @pl.when(f == 0)
        def _():
            acc_ref[...] = part

        @pl.when(f > 0)
        def _():
            acc_ref[...] += part

        @pl.when(f == nf - 1)
        def _():
            xh2, rs2 = _layer_norm_stats(alpha * x1 + acc_ref[...])
            xh2_ref[...] = xh2
            rs2_ref[...] = rs2
            x2_ref[...] = xh2 * g2_ref[...] + b2_ref[...]

    row = lambda i, f: (i, 0)
    fix = lambda i, f: (0, 0)
    return pl.pallas_call(
        body,
        name="mlp_fwd",
        grid=(T // tm, nf),
        in_specs=[pl.BlockSpec((tm, d), row), pl.BlockSpec((1, d), fix), pl.BlockSpec((1, d), fix),
                  pl.BlockSpec((d, tf), lambda i, f: (0, f)), pl.BlockSpec((tf, d), lambda i, f: (f, 0)),
                  pl.BlockSpec((1, d), fix), pl.BlockSpec((1, d), fix)],
        out_specs=[pl.BlockSpec((tm, tf), lambda i, f: (i, f)), pl.BlockSpec((tm, d), row), pl.BlockSpec((tm, 1), row),
                   pl.BlockSpec((tm, d), row)],
        out_shape=[jax.ShapeDtypeStruct((T, ff), BF16), jax.ShapeDtypeStruct((T, d), F32), jax.ShapeDtypeStruct((T, 1), F32),
                   jax.ShapeDtypeStruct((T, d), F32)],
        scratch_shapes=[pltpu.VMEM((tm, d), F32)],
        compiler_params=_cparams(("parallel", "arbitrary")),
    )(xhat1, g1, b1, wu, wd, g2, b2)


def _ln_bwd(dy, xhat, rstd, g, *, tm=512, deps=()):
    T, d = dy.shape
    tm = _pick(T, (tm, 256, 128, 8))

    def body(dy_ref, xh_ref, rs_ref, g_ref, *rest):
        dz_ref, dg_ref, db_ref = rest[len(deps):]

        @pl.when(pl.program_id(0) == 0)
        def _():
            dg_ref[...] = jnp.zeros_like(dg_ref)
            db_ref[...] = jnp.zeros_like(db_ref)

        dy_, xh = dy_ref[...], xh_ref[...]
        dg_ref[0:1, :] += jnp.sum(dy_ * xh, axis=0, keepdims=True)
        db_ref[0:1, :] += jnp.sum(dy_, axis=0, keepdims=True)
        dxh = dy_ * g_ref[...]
        dz_ref[...] = rs_ref[...] * (dxh - jnp.mean(dxh, axis=-1, keepdims=True) - xh * jnp.mean(dxh * xh, axis=-1, keepdims=True))

    row = lambda i: (i, 0)
    fix = lambda i: (0, 0)
    return pl.pallas_call(
        body,
        name="ln_bwd",
        grid=(T // tm,),
        in_specs=[pl.BlockSpec((tm, d), row), pl.BlockSpec((tm, d), row), pl.BlockSpec((tm, 1), row), pl.BlockSpec((1, d), fix)]
        + [ANY_SPEC] * len(deps),
        out_specs=[pl.BlockSpec((tm, d), row), pl.BlockSpec((8, d), fix), pl.BlockSpec((8, d), fix)],
        out_shape=[jax.ShapeDtypeStruct((T, d), F32), jax.ShapeDtypeStruct((8, d), F32), jax.ShapeDtypeStruct((8, d), F32)],
        compiler_params=_cparams(("arbitrary",)),
    )(dy, xhat, rstd, g, *deps)


def _loss_head(y, target, *, tm=512):
    T, d = y.shape
    tm = _pick(T, (tm, 256, 128, 8))
    n = T // tm

    def body(y_ref, t_ref, loss_ref, dy_ref, acc_ref):
        i = pl.program_id(0)

        @pl.when(i == 0)
        def _():
            acc_ref[...] = jnp.zeros_like(acc_ref)

        e = y_ref[...] - t_ref[...]
        dy_ref[...] = e * (1.0 / d)
        acc_ref[...] += jnp.sum(e * e, axis=0, keepdims=True)

        @pl.when(i == n - 1)
        def _():
            loss_ref[...] = (0.5 / d) * jnp.sum(acc_ref[...], axis=1, keepdims=True)

    row = lambda i: (i, 0)
    return pl.pallas_call(
        body,
        name="loss_head",
        grid=(n,),
        in_specs=[pl.BlockSpec((tm, d), row), pl.BlockSpec((tm, d), row)],
        out_specs=[pl.BlockSpec((1, 1), lambda i: (0, 0)), pl.BlockSpec((tm, d), row)],
        out_shape=[jax.ShapeDtypeStruct((1, 1), F32), jax.ShapeDtypeStruct((T, d), F32)],
        scratch_shapes=[pltpu.VMEM((1, d), F32)],
        compiler_params=_cparams(("arbitrary",)),
    )(y, target)


def _lower_bounds_fwd(lower_bounds):
    depth, n = lower_bounds.shape

    def body(x_ref, soft_ref, lb_ref):
        x = x_ref[...]
        e = jnp.exp(x - jnp.max(x, axis=0, keepdims=True))
        soft_ref[...] = e / jnp.sum(e, axis=0, keepdims=True)
        run = None
        for l in range(depth):
            run = soft_ref[l:l + 1, :] if run is None else run + soft_ref[l:l + 1, :]
            lb_ref[l:l + 1, :] = run - soft_ref[0:1, :]

    return pl.pallas_call(body, name="lower_bounds_fwd",
                          out_shape=[jax.ShapeDtypeStruct((depth, n), F32), jax.ShapeDtypeStruct((depth, n), F32)])(lower_bounds)


def _lower_bounds_bwd(soft, dlb):
    depth, n = soft.shape

    def body(soft_ref, dlb_ref, out_ref, dsoft_ref):
        total = jnp.sum(dlb_ref[...], axis=0, keepdims=True)
        run = None
        for l in reversed(range(depth)):
            run = dlb_ref[l:l + 1, :] if run is None else run + dlb_ref[l:l + 1, :]
            dsoft_ref[l:l + 1, :] = run - total if l == 0 else run
        s, ds = soft_ref[...], dsoft_ref[...]
        out_ref[...] = s * (ds - jnp.sum(s * ds, axis=0, keepdims=True))

    return pl.pallas_call(body, name="lower_bounds_bwd", out_shape=jax.ShapeDtypeStruct((depth, n), F32),
                          scratch_shapes=[pltpu.VMEM((depth, n), F32)])(soft, dlb)


def _layer_fwd(x0, mem2, lb, w_in, rest_fn, *, bl, seq, alpha, deps=()):
    p = _matmul("proj_in", x0, w_in, mode="nn", deps=deps)
    wts = dict(rest_fn(p), w_in=w_in)
    mk = _matmul("mem_k", mem2, wts["w_mem_k"], mode="nn", out_dtype=BF16)
    mv = _matmul("mem_v", mem2, wts["w_mem_v"], mode="nn", out_dtype=BF16)
    y, st, opre = _mixer_fwd(p, mk, mv, lb, wts["conv_w"], wts["hg_norm_w"], bl=bl, seq=seq)
    r, merged, xhat1, rstd1 = _merge_fwd(y, p, x0, wts["w_branch"], wts["w_o"], wts["b_gate"], alpha=alpha)
    a, xhat2, rstd2, x2 = _mlp_fwd(xhat1, wts["ln1_g"], wts["ln1_b"], wts["w_up"], wts["w_down"], wts["ln2_g"], wts["ln2_b"],
                                   alpha=alpha)
    saved = dict(x0=x0, p=p, mk=mk, mv=mv, y=y, st=st, opre=opre, r=r, merged=merged, xhat1=xhat1, rstd1=rstd1, a=a,
                 xhat2=xhat2, rstd2=rstd2)
    return x2, saved, wts


def _relu2_bf16(a):
    return jnp.square(jnp.maximum(a.astype(F32), 0.0)).astype(BF16)


def _affine_bf16(xh, g, b):
    return (xh * g + b).astype(BF16)


def _mlp_bwd(dx2, sv, wts, *, alpha, deps=()):
    g = {}
    dz2, dg2, db2 = _ln_bwd(dx2, sv["xhat2"], sv["rstd2"], wts["ln2_g"], deps=deps)
    g["ln2_g"], g["ln2_b"] = dg2[0:1], db2[0:1]
    da = _matmul("mlp_da", dz2, wts["w_down"], mode="nt", out_dtype=BF16,
                 epi_fn=lambda acc, a: (acc * (2.0 * jnp.maximum(a.astype(F32), 0.0)),), epi_extra=(sv["a"],))
    g["w_down"] = _matmul_tn("grad_w_down", sv["a"], dz2, a_fn=_relu2_bf16)
    g["w_up"] = _matmul_tn("grad_w_up", sv["xhat1"], da, a_fn=_affine_bf16, a_extra=(wts["ln1_g"], wts["ln1_b"]))
    dx1 = _matmul("mlp_dx", da, wts["w_up"], mode="nt", epi_fn=lambda acc, dz: (acc + alpha * dz,), epi_extra=(dz2,))
    return dx1, g


def _mix_bwd(dx1, sv, mem2, lb, wts, *, bl, seq, alpha, deps=()):
    d = dx1.shape[1]
    g = {}
    dz1, dg1, db1 = _ln_bwd(dx1, sv["xhat1"], sv["rstd1"], wts["ln1_g"], deps=deps)
    g["ln1_g"], g["ln1_b"] = dg1[0:1], db1[0:1]
    g["w_o"] = _matmul_tn("grad_w_o", sv["merged"], dz1)
    dr, dp, dy, dbg = _merge_bwd(dz1, sv["p"], sv["r"], wts["w_branch"], wts["w_o"], wts["b_gate"])
    g["b_gate"] = dbg[0:1]
    g["w_branch"] = jnp.concatenate(
        [_matmul_tn("grad_w_branch", sv["y"], dr, a_cols=(i * W, W), b_cols=(i * d, d)) for i in range(N_BRANCH)], axis=0)
    dp, dmk, dmv, dcw, dnw, dlb = _mixer_bwd(sv["p"], dy, dp, sv["st"], sv["opre"], sv["mk"], sv["mv"], lb,
                                              wts["conv_w"], wts["hg_norm_w"], bl=bl, seq=seq)
    g["conv_w"], g["hg_norm_w"], g["lb"] = dcw[0:CONV_K], dnw[0:1], dlb[0:1]
    g["w_mem_k"] = _matmul_tn("grad_w_mem_k", mem2, dmk)
    g["w_mem_v"] = _matmul_tn("grad_w_mem_v", mem2, dmv)
    g["w_in"] = _matmul_tn("grad_w_in", sv["x0"], dp)
    dx0 = _matmul("proj_in_dx", dp, wts["w_in"], mode="nt", epi_fn=lambda acc, dz: (acc + alpha * dz,), epi_extra=(dz1,))
    return dx0, g


N_CHIPS = 4
MESH_IDS = pl.DeviceIdType.MESH


def _axis_slice(ref, axis, start, size):
    idx = [slice(None)] * len(ref.shape)
    idx[axis] = pl.ds(start, size)
    return ref.at[tuple(idx)]


def _chip_exchange(name, items):
    n = len(items)
    out_shapes, meta = [], []
    for arr, kind, axis in items:
        shp = list(arr.shape)
        if kind == "gather":
            per = shp[axis]
            shp[axis] = per * N_CHIPS
            out_shapes.append(jax.ShapeDtypeStruct(tuple(shp), arr.dtype))
        elif kind == "scatter":
            per = shp[axis] // N_CHIPS
            shp[axis] = per
            out_shapes.append(jax.ShapeDtypeStruct((N_CHIPS, *shp), arr.dtype))
        else:
            per = None
            out_shapes.append(jax.ShapeDtypeStruct((N_CHIPS, *shp), arr.dtype))
        meta.append((kind, axis, per))

    def body(*refs):
        ins, outs = refs[:n], refs[n:2 * n]
        send_sems, recv_sems, local_sems = refs[2 * n:]
        x, y, c = lax.axis_index("x"), lax.axis_index("y"), lax.axis_index("c")
        me = 2 * x + y
        peers = [(1 - x, y), (x, 1 - y), (1 - x, 1 - y)]

        def src_for(t, chip):
            kind, axis, per = meta[t]
            return _axis_slice(ins[t], axis, chip * per, per) if kind == "scatter" else ins[t]

        def dst_from(t, chip):
            kind, axis, per = meta[t]
            return _axis_slice(outs[t], axis, chip * per, per) if kind == "gather" else outs[t].at[chip]

        def remote(t, k):
            px, py = peers[k]
            return pltpu.make_async_remote_copy(
                src_ref=src_for(t, 2 * px + py), dst_ref=dst_from(t, me), send_sem=send_sems.at[t * 3 + k],
                recv_sem=recv_sems.at[t * 3 + k], device_id=(px, py, c), device_id_type=MESH_IDS)

        def arrival(t, k):
            px, py = peers[k]
            return pltpu.make_async_remote_copy(
                src_ref=src_for(t, me), dst_ref=dst_from(t, 2 * px + py), send_sem=send_sems.at[t * 3 + k],
                recv_sem=recv_sems.at[t * 3 + k], device_id=(px, py, c), device_id_type=MESH_IDS)

        sends = [remote(t, k) for t in range(n) for k in range(3)]
        for cp in sends:
            cp.start()
        own = [pltpu.make_async_copy(src_for(t, me), dst_from(t, me), local_sems.at[t]) for t in range(n)]
        for cp in own:
            cp.start()
        for t in range(n):
            for k in range(3):
                arrival(t, k).wait_recv()
        for cp in sends:
            cp.wait_send()
        for cp in own:
            cp.wait()

    any_spec = pl.BlockSpec(memory_space=pl.ANY)
    return pl.pallas_call(
        body,
        name=name,
        in_specs=[any_spec] * n,
        out_specs=[any_spec] * n,
        out_shape=out_shapes,
        scratch_shapes=[pltpu.SemaphoreType.DMA((3 * n,)), pltpu.SemaphoreType.DMA((3 * n,)), pltpu.SemaphoreType.DMA((n,))],
        compiler_params=pltpu.CompilerParams(has_side_effects=True),
    )(*[a for a, _, _ in items])


HBM_SPEC = pl.BlockSpec(memory_space=pltpu.HBM)
SEM_SPEC = pl.BlockSpec(memory_space=pltpu.SEMAPHORE)
N_PEERS = N_CHIPS - 1


def _my_chip():
    return (2 * lax.axis_index("x") + lax.axis_index("y")).astype(jnp.int32).reshape(1)


def _own_block_spec(r, c, axis, tr):
    if axis == 1:
        return pl.BlockSpec((tr, c), lambda i, me: (i, me[0]))
    return pl.BlockSpec((tr, c), lambda i, me: (me[0] * (r // tr) + i, 0))


def _place_shard(name, shard, axis, me):
    r, c = shard.shape
    tr = _row_block(r, c, shard.dtype.itemsize)
    shp = (r, c * N_CHIPS) if axis == 1 else (r * N_CHIPS, c)

    def body(me_ref, s_ref, o_ref):
        del me_ref
        o_ref[...] = s_ref[...]

    return pl.pallas_call(
        body, name=name,
        grid_spec=pltpu.PrefetchScalarGridSpec(
            num_scalar_prefetch=1, grid=(r // tr,),
            in_specs=[pl.BlockSpec((tr, c), lambda i, me: (i, 0))], out_specs=_own_block_spec(r, c, axis, tr)),
        out_shape=jax.ShapeDtypeStruct(shp, shard.dtype),
        compiler_params=_cparams(("parallel",)),
    )(me, shard)


class _Split:
    def __init__(self, name, items):
        self.name, self.n = name, len(items)
        self.srcs = [a for a, _, _ in items]
        self.meta, self.land_shapes = [], []
        for arr, kind, axis in items:
            shp = list(arr.shape)
            if kind == "gather":
                per = shp[axis]
                shp[axis] = per * N_CHIPS
                self.land_shapes.append(jax.ShapeDtypeStruct(tuple(shp), arr.dtype))
            else:
                per = shp[axis] // N_CHIPS
                shp[axis] = per
                self.land_shapes.append(jax.ShapeDtypeStruct((N_PEERS, *shp), arr.dtype))
            self.meta.append((kind, axis, per))

    def _src(self, ins, t, chip):
        kind, axis, per = self.meta[t]
        return _axis_slice(ins[t], axis, chip * per, per) if kind == "scatter" else ins[t]

    def _dst(self, lands, t, chip, slot):
        kind, axis, per = self.meta[t]
        return _axis_slice(lands[t], axis, chip * per, per) if kind == "gather" else lands[t].at[slot]

    def landing_zones(self, me):
        return [_place_shard(self.name + "_own", src, axis, me) if kind == "gather" else lax.empty(ls.shape, ls.dtype)
                for src, ls, (kind, axis, _) in zip(self.srcs, self.land_shapes, self.meta)]

    def _copies(self, ins, lands, send_sems, recv_sems):
        x, y, c = lax.axis_index("x"), lax.axis_index("y"), lax.axis_index("c")
        me = 2 * x + y
        peers = [(1 - x, y), (x, 1 - y), (1 - x, 1 - y)]
        out, arrive = [], []
        for t in range(self.n):
            for k, (px, py) in enumerate(peers):
                theirs = 2 * px + py
                sems = dict(send_sem=send_sems.at[t * N_PEERS + k], recv_sem=recv_sems.at[t * N_PEERS + k],
                            device_id=(px, py, c), device_id_type=MESH_IDS)
                out.append(pltpu.make_async_remote_copy(src_ref=self._src(ins, t, theirs), dst_ref=self._dst(lands, t, me, k), **sems))
                arrive.append(pltpu.make_async_remote_copy(src_ref=self._src(ins, t, me), dst_ref=self._dst(lands, t, theirs, k), **sems))
        return out, arrive

    def start(self, lands):
        n = self.n

        def body(*refs):
            ins, lnd = refs[:n], refs[n:2 * n]
            send_sems, recv_sems = refs[2 * n], refs[2 * n + 1]
            token = refs[-1]
            out, _ = self._copies(ins, lnd, send_sems, recv_sems)
            for cp in out:
                cp.start()
            token[...] = jnp.zeros_like(token)

        hbm = lambda a: pltpu.HBM(a.shape, a.dtype)
        res = pl.pallas_call(
            body, name=self.name + "_start",
            in_specs=[HBM_SPEC] * (2 * n),
            out_specs=[SEM_SPEC, SEM_SPEC] + [HBM_SPEC] * (2 * n) + [pl.BlockSpec(memory_space=pltpu.VMEM)],
            out_shape=[pltpu.SemaphoreType.DMA((N_PEERS * n,)), pltpu.SemaphoreType.DMA((N_PEERS * n,))]
            + [hbm(a) for a in self.srcs] + [hbm(a) for a in self.land_shapes] + [jax.ShapeDtypeStruct((8, 128), F32)],
            input_output_aliases={i: 2 + i for i in range(2 * n)},
            compiler_params=pltpu.CompilerParams(has_side_effects=pltpu.SideEffectType.DATAFLOW_SIDE_EFFECTING),
        )(*[pltpu.with_memory_space_constraint(a, pltpu.HBM) for a in self.srcs],
          *[pltpu.with_memory_space_constraint(a, pltpu.HBM) for a in lands])
        return res[:-1], res[-1]

    def wait(self, state, after):
        n = self.n
        send_sems, recv_sems = state[0], state[1]
        srcs, lands = state[2:2 + n], state[2 + n:2 + 2 * n]

        def body(*refs):
            ins, lnd = refs[:n], refs[n:2 * n]
            s_sems, r_sems = refs[2 * n], refs[2 * n + 1]
            out, arrive = self._copies(ins, lnd, s_sems, r_sems)
            for cp in arrive:
                cp.wait_recv()
            for cp in out:
                cp.wait_send()

        hbm = lambda a: pltpu.HBM(a.shape, a.dtype)
        res = pl.pallas_call(
            body, name=self.name + "_wait",
            in_specs=[HBM_SPEC] * (2 * n) + [SEM_SPEC, SEM_SPEC, ANY_SPEC],
            out_specs=[HBM_SPEC] * (2 * n),
            out_shape=[hbm(a) for a in self.srcs] + [hbm(a) for a in self.land_shapes],
            input_output_aliases={i: i for i in range(2 * n)},
            compiler_params=pltpu.CompilerParams(has_side_effects=pltpu.SideEffectType.DATAFLOW_SIDE_EFFECTING),
        )(*srcs, *lands, send_sems, recv_sems, after)
        return res[:n], res[n:]


def _sibling_swap(name, arrays):
    n = len(arrays)

    def body(*refs):
        ins, outs = refs[:n], refs[n:2 * n]
        send_sems, recv_sems = refs[2 * n:]
        sibling = (lax.axis_index("x"), lax.axis_index("y"), 1 - lax.axis_index("c"))
        copies = [pltpu.make_async_remote_copy(src_ref=ins[t], dst_ref=outs[t], send_sem=send_sems.at[t], recv_sem=recv_sems.at[t],
                                               device_id=sibling, device_id_type=MESH_IDS) for t in range(n)]
        for cp in copies:
            cp.start()
        for cp in copies:
            cp.wait()

    any_spec = pl.BlockSpec(memory_space=pl.ANY)
    return pl.pallas_call(
        body,
        name=name,
        in_specs=[any_spec] * n,
        out_specs=[any_spec] * n,
        out_shape=[jax.ShapeDtypeStruct(a.shape, a.dtype) for a in arrays],
        scratch_shapes=[pltpu.SemaphoreType.DMA((n,)), pltpu.SemaphoreType.DMA((n,))],
        compiler_params=pltpu.CompilerParams(has_side_effects=True),
    )(*arrays)


def _row_block(r, c, itemsize=4, target=1 << 20):
    if r % 8 != 0:
        return r
    best = 8
    for tr in range(8, r + 1, 8):
        if r % tr == 0 and tr * c * itemsize <= target:
            best = tr
    return best


def _sum_chips_into(parts, stacked, layer):
    _, r, c = parts.shape
    tr = _row_block(r, c)

    def body(p_ref, s_ref, o_ref):
        del s_ref
        o_ref[...] = ((p_ref[0] + p_ref[1]) + p_ref[2]) + p_ref[3]

    return pl.pallas_call(
        body,
        name="sum_chips",
        grid=(r // tr,),
        in_specs=[pl.BlockSpec((N_CHIPS, tr, c), lambda i: (0, i, 0)), pl.BlockSpec(memory_space=pl.ANY)],
        out_specs=pl.BlockSpec((None, tr, c), lambda i: (layer, i, 0)),
        out_shape=jax.ShapeDtypeStruct(stacked.shape, stacked.dtype),
        input_output_aliases={1: 0},
        compiler_params=_cparams(("parallel",)),
    )(parts, stacked)


def _sum_own_and_peers(me, g, axis, landed, stacked, layer):
    _, r, c = landed.shape
    tr = _row_block(r, c)

    def body(me_ref, g_ref, p_ref, s_ref, o_ref):
        del me_ref, s_ref
        o_ref[...] = ((g_ref[...] + p_ref[0]) + p_ref[1]) + p_ref[2]

    return pl.pallas_call(
        body, name="sum_chips_own",
        grid_spec=pltpu.PrefetchScalarGridSpec(
            num_scalar_prefetch=1, grid=(r // tr,),
            in_specs=[_own_block_spec(r, c, axis, tr), pl.BlockSpec((N_PEERS, tr, c), lambda i, me: (0, i, 0)), ANY_SPEC],
            out_specs=pl.BlockSpec((None, tr, c), lambda i, me: (layer, i, 0))),
        out_shape=jax.ShapeDtypeStruct(stacked.shape, stacked.dtype),
        input_output_aliases={3: 0},
        compiler_params=_cparams(("parallel",)),
    )(me, g, landed, stacked)


def _adamw(w, m, v, g_a, g_b):
    L, r, c = w.shape
    tr = _row_block(r, c, target=1 << 19)

    def body(w_ref, m_ref, v_ref, ga_ref, gb_ref, g_ref, d_ref, nm_ref, nv_ref):
        g = ga_ref[...] + gb_ref[...]
        g_ref[...] = g
        m_new = ADAM_B1 * m_ref[...] + (1.0 - ADAM_B1) * g
        v_new = ADAM_B2 * v_ref[...] + (1.0 - ADAM_B2) * jnp.square(g)
        nm_ref[...] = m_new
        nv_ref[...] = v_new
        m_hat = m_new / (1.0 - ADAM_B1 ** ADAM_STEP)
        v_hat = v_new / (1.0 - ADAM_B2 ** ADAM_STEP)
        d_ref[...] = -ADAM_LR * (m_hat / (jnp.sqrt(v_hat) + ADAM_EPS) + ADAM_WD * w_ref[...])

    spec = pl.BlockSpec((None, tr, c), lambda l, i: (l, i, 0))
    return pl.pallas_call(
        body,
        name="adamw",
        grid=(L, r // tr),
        in_specs=[spec] * 5,
        out_specs=[spec] * 4,
        out_shape=[jax.ShapeDtypeStruct(w.shape, F32)] * 4,
        compiler_params=_cparams(("parallel", "parallel")),
    )(w, m, v, g_a, g_b)


SHARDED = (("w_in", 1), ("conv_w", 1), ("w_mem_k", 0), ("w_mem_v", 0), ("w_branch", 1), ("w_o", 0), ("w_up", 1), ("w_down", 0))
SMALL = ("lower_bounds", "hg_norm_w", "b_gate", "ln1_g", "ln1_b", "ln2_g", "ln2_b")
WEIGHT_ORDER = ("lower_bounds", "w_in", "conv_w", "hg_norm_w", "w_mem_k", "w_mem_v", "w_branch", "b_gate", "w_o", "ln1_g", "ln1_b",
                "w_up", "w_down", "ln2_g", "ln2_b")


def kernel(x, mem, lower_bounds, w_in, conv_w, hg_norm_w, w_mem_k, w_mem_v, w_branch, b_gate, w_o, ln1_g, ln1_b, w_up, w_down, ln2_g, ln2_b, loss_target, m_lower_bounds, m_w_in, m_conv_w, m_hg_norm_w, m_w_mem_k, m_w_mem_v, m_w_branch, m_b_gate, m_w_o, m_ln1_g, m_ln1_b, m_w_up, m_w_down, m_ln2_g, m_ln2_b, v_lower_bounds, v_w_in, v_conv_w, v_hg_norm_w, v_w_mem_k, v_w_mem_v, v_w_branch, v_b_gate, v_w_o, v_ln1_g, v_ln1_b, v_w_up, v_w_down, v_ln2_g, v_ln2_b):
    bl, seq, d = x.shape
    depth = w_in.shape[0]
    weights = dict(lower_bounds=lower_bounds, w_in=w_in, conv_w=conv_w, hg_norm_w=hg_norm_w, w_mem_k=w_mem_k, w_mem_v=w_mem_v,
                   w_branch=w_branch, b_gate=b_gate, w_o=w_o, ln1_g=ln1_g, ln1_b=ln1_b, w_up=w_up, w_down=w_down, ln2_g=ln2_g, ln2_b=ln2_b)
    mom_m = dict(lower_bounds=m_lower_bounds, w_in=m_w_in, conv_w=m_conv_w, hg_norm_w=m_hg_norm_w, w_mem_k=m_w_mem_k, w_mem_v=m_w_mem_v,
                 w_branch=m_w_branch, b_gate=m_b_gate, w_o=m_w_o, ln1_g=m_ln1_g, ln1_b=m_ln1_b, w_up=m_w_up, w_down=m_w_down,
                 ln2_g=m_ln2_g, ln2_b=m_ln2_b)
    mom_v = dict(lower_bounds=v_lower_bounds, w_in=v_w_in, conv_w=v_conv_w, hg_norm_w=v_hg_norm_w, w_mem_k=v_w_mem_k, w_mem_v=v_w_mem_v,
                 w_branch=v_w_branch, b_gate=v_b_gate, w_o=v_w_o, ln1_g=v_ln1_g, ln1_b=v_ln1_b, w_up=v_w_up, w_down=v_w_down,
                 ln2_g=v_ln2_g, ln2_b=v_ln2_b)

    def shard2d(name, l):
        w = weights[name][l]
        if name == "w_branch":
            return w.reshape(N_BRANCH * W, w.shape[-1]).astype(BF16)
        return w if name == "conv_w" else w.astype(BF16)

    me = _my_chip()

    shard_axis = dict(SHARDED)

    def start_exchange(name, kind, items):
        ex = _Split(name, [(arr, kind, shard_axis[nm]) for nm, arr in items])
        state, token = ex.start(ex.landing_zones(me))
        return ex, state, [nm for nm, _ in items], token

    def start_gathers(l):
        first = start_exchange(f"gather_in_l{l}", "gather", [("w_in", shard2d("w_in", l))])
        rest = start_exchange(f"gather_rest_l{l}", "gather", [(nm, shard2d(nm, l)) for nm, _ in SHARDED if nm != "w_in"])
        return first, rest

    def gathered(pend, after):
        ex, state, names, _ = pend
        return dict(zip(names, ex.wait(state, after=after)[1]))

    x2d, mem2, t2d = x.reshape(bl * seq, d), mem.reshape(-1, d), loss_target.reshape(bl * seq, d)
    alpha = (2.0 * depth) ** 0.25
    soft, lb_all = _lower_bounds_fwd(lower_bounds)

    h, saved, layer_wts = x2d, [], []
    pending = start_gathers(0)
    for l in range(depth):
        first, rest = pending
        w_in_l = gathered(first, h)["w_in"]

        def rest_fn(after, l=l, rest=rest):
            wts = gathered(rest, after)
            for name in ("hg_norm_w", "b_gate", "ln1_g", "ln1_b", "ln2_g", "ln2_b"):
                wts[name] = weights[name][l][None, :]
            return wts

        deps = ()
        if l + 1 < depth:
            pending = start_gathers(l + 1)
            deps = (pending[0][3], pending[1][3])
        h, sv, wts = _layer_fwd(h, mem2, lb_all[l:l + 1], w_in_l, rest_fn, bl=bl, seq=seq, alpha=alpha, deps=deps)
        saved.append(sv)
        layer_wts.append(wts)
    loss, dh = _loss_head(h, t2d)

    partial = {name: jnp.zeros((depth, weights[name].size // (depth * weights[name].shape[-1]), weights[name].shape[-1]), F32)
               for name, _ in SHARDED}
    smalls = [None] * depth

    def finish_reduce(pend, l, after):
        ex, state, names, _ = pend
        sent, got = ex.wait(state, after=after)
        for nm, g_full, landed in zip(names, sent, got):
            partial[nm] = _sum_own_and_peers(me, g_full, shard_axis[nm], landed, partial[nm], l)

    pending_mix, deps = None, ()
    for l in reversed(range(depth)):
        dx1, g_mlp = _mlp_bwd(dh, saved[l], layer_wts[l], alpha=alpha, deps=deps)
        pending_mlp = start_exchange(f"reduce_mlp_l{l}", "scatter", [(nm, g_mlp[nm]) for nm in ("w_up", "w_down")])
        if pending_mix is not None:
            finish_reduce(pending_mix, l + 1, dx1)
        dh, g = _mix_bwd(dx1, saved[l], mem2, lb_all[l:l + 1], layer_wts[l], bl=bl, seq=seq, alpha=alpha, deps=(pending_mlp[3],))
        pending_mix = start_exchange(f"reduce_mix_l{l}", "scatter",
                                     [(nm, g[nm]) for nm, _ in SHARDED if nm not in ("w_up", "w_down")])
        finish_reduce(pending_mlp, l, dh)
        deps = (pending_mix[3],)
        g.update(g_mlp, lower_bounds=g["lb"])
        smalls[l] = jnp.concatenate([g[nm] for nm in SMALL], axis=1)
    small_parts = _chip_exchange("reduce_small", [(jnp.stack(smalls), "bcast", 0)])[0]
    small_sum = _sum_chips_into(small_parts.reshape(N_CHIPS, depth, -1), jnp.zeros((1, depth, small_parts.shape[-1]), F32), 0)
    finish_reduce(pending_mix, 0, small_sum)
    partial = [partial[name] for name, _ in SHARDED] + [small_sum.reshape(depth, 1, -1)]
    theirs = _sibling_swap("swap_partials", partial)

    outs = {}
    for i, (name, _) in enumerate(SHARDED):
        shape3 = partial[i].shape
        res = _adamw(weights[name].reshape(shape3), mom_m[name].reshape(shape3), mom_v[name].reshape(shape3), partial[i], theirs[i])
        outs[name] = [r.reshape(weights[name].shape) for r in res]
    off = 0
    for name in SMALL:
        n = weights[name].shape[1]
        mine, other = partial[-1][:, :, off:off + n], theirs[-1][:, :, off:off + n]
        off += n
        if name == "lower_bounds":
            mine = _lower_bounds_bwd(soft, mine[:, 0, :])[:, None, :]
            other = _lower_bounds_bwd(soft, other[:, 0, :])[:, None, :]
        shape3 = (depth, 1, n)
        res = _adamw(weights[name].reshape(shape3), mom_m[name].reshape(shape3), mom_v[name].reshape(shape3), mine, other)
        outs[name] = [r.reshape(weights[name].shape) for r in res]
    assert off == partial[-1].shape[-1]

    total_loss = lax.psum(loss[0, 0], ("x", "y", "c"))
    result = [total_loss, dh.reshape(bl, seq, d)]
    for k in range(4):
        result += [outs[name][k] for name in WEIGHT_ORDER]
    return tuple(result)
```

```python
import functools

import jax
import jax.numpy as jnp
from jax import lax
from jax.experimental import pallas as pl
from jax.experimental.pallas import tpu as pltpu

F32 = jnp.float32
BF16 = jnp.bfloat16

HG_HEADS = 4
HG_F = 128
HG_CHUNK = 32
MEM_HEADS = 4
MEM_HEAD_DIM = 128
BRANCH_WIDTH = 512
N_BRANCH = 3
CONV_K = 3
LN_EPS = 1e-5
RMS_EPS = 1e-6
ADAM_LR = 0.001
ADAM_B1 = 0.9
ADAM_B2 = 0.999
ADAM_EPS = 1e-08
ADAM_WD = 0.01
ADAM_STEP = 10

VMEM_LIMIT = 48 * 1024 * 1024


def _cparams(sem):
    return pltpu.CompilerParams(dimension_semantics=sem, vmem_limit_bytes=VMEM_LIMIT)


def _dot(a, b, dims):
    return lax.dot_general(a, b, (dims, ((), ())), preferred_element_type=F32)


NN = ((1,), (0,))
NT = ((1,), (1,))
TN = ((0,), (0,))


def _pick(n, pref):
    for t in pref:
        if n % t == 0:
            return t
    return n


ANY_SPEC = pl.BlockSpec(memory_space=pl.ANY)


def _matmul(name, a, b, *, mode, out_dtype=F32, a_fn=None, a_extra=(), epi_fn=None, epi_extra=(), n_out=1,
            tm=512, tn=1024, tk=1024, deps=()):
    M, K = a.shape
    N = b.shape[1] if mode == "nn" else b.shape[0]
    tm, tn, tk = _pick(M, (tm, 256, 128, 8)), _pick(N, (tn, 896, 512, 256, 128)), _pick(K, (tk, 512, 256, 128))
    nk = K // tk
    n_ax, n_ex = len(a_extra), len(epi_extra)
    n_in = 2 + n_ax + n_ex + len(deps)
    out_dtypes = out_dtype if isinstance(out_dtype, (tuple, list)) else (out_dtype,) * n_out

    def body(*refs):
        a_ref, b_ref = refs[0], refs[1]
        ax_refs = refs[2:2 + n_ax]
        ex_refs = refs[2 + n_ax:2 + n_ax + n_ex]
        o_refs = refs[n_in:n_in + n_out]
        at = a_ref[...]
        at = a_fn(at, *[r[...] for r in ax_refs]) if a_fn is not None else at.astype(BF16)
        part = _dot(at, b_ref[...].astype(BF16), NN if mode == "nn" else NT)

        def finish(acc):
            outs = epi_fn(acc, *[r[...] for r in ex_refs]) if epi_fn is not None else (acc,)
            for o_ref, o in zip(o_refs, outs):
                o_ref[...] = o.astype(o_ref.dtype)

        if nk == 1:
            finish(part)
            return
        acc_ref = refs[-1]
        k = pl.program_id(2)

        @pl.when(k == 0)
        def _():
            acc_ref[...] = part

        @pl.when(jnp.logical_and(k > 0, k < nk - 1))
        def _():
            acc_ref[...] += part

        @pl.when(k == nk - 1)
        def _():
            finish(acc_ref[...] + part)

    in_specs = [pl.BlockSpec((tm, tk), lambda j, i, k: (i, k)),
                pl.BlockSpec((tk, tn), lambda j, i, k: (k, j)) if mode == "nn" else pl.BlockSpec((tn, tk), lambda j, i, k: (j, k))]
    in_specs += [pl.BlockSpec((1, tk), lambda j, i, k: (0, k)) for _ in a_extra]
    for e in epi_extra:
        if e.shape[0] == 1:
            in_specs.append(pl.BlockSpec((1, tn), lambda j, i, k: (0, j)))
        else:
            in_specs.append(pl.BlockSpec((tm, tn), lambda j, i, k: (i, j)))
    in_specs += [ANY_SPEC] * len(deps)
    out = pl.pallas_call(
        body,
        name=name,
        grid=(N // tn, M // tm, nk),
        in_specs=in_specs,
        out_specs=[pl.BlockSpec((tm, tn), lambda j, i, k: (i, j)) for _ in range(n_out)],
        out_shape=[jax.ShapeDtypeStruct((M, N), dt) for dt in out_dtypes],
        scratch_shapes=[pltpu.VMEM((tm, tn), F32)] if nk > 1 else [],
        compiler_params=_cparams(("parallel", "parallel", "arbitrary")),
    )(a, b, *a_extra, *epi_extra, *deps)
    return out[0] if n_out == 1 else out


def _matmul_tn(name, a, b, *, a_fn=None, a_extra=(), a_cols=None, b_cols=None, ta=1024, tb=1024, tt=1024, out_dtype=F32, deps=()):
    T = a.shape[0]
    a0, Ka = a_cols if a_cols is not None else (0, a.shape[1])
    b0, Nb = b_cols if b_cols is not None else (0, b.shape[1])
    ta, tb, tt = _pick(Ka, (ta, 512, 256, 128)), _pick(Nb, (tb, 896, 512, 256, 128)), _pick(T, (tt, 512, 256, 128))
    assert a0 % ta == 0 and b0 % tb == 0
    a0, b0 = a0 // ta, b0 // tb
    nt = T // tt
    n_ax = len(a_extra)

    def body(*refs):
        a_ref, b_ref = refs[0], refs[1]
        ax_refs = refs[2:2 + n_ax]
        o_ref = refs[2 + n_ax + len(deps)]
        acc_ref = refs[-1]
        t = pl.program_id(2)
        at = a_ref[...]
        at = a_fn(at, *[r[...] for r in ax_refs]) if a_fn is not None else at.astype(BF16)
        part = _dot(at, b_ref[...].astype(BF16), TN)

        @pl.when(t == 0)
        def _():
            acc_ref[...] = part

        @pl.when(jnp.logical_and(t > 0, t < nt - 1))
        def _():
            acc_ref[...] += part

        @pl.when(t == nt - 1)
        def _():
            o_ref[...] = (acc_ref[...] + part if nt > 1 else part).astype(o_ref.dtype)

    in_specs = [pl.BlockSpec((tt, ta), lambda i, j, t: (t, a0 + i)), pl.BlockSpec((tt, tb), lambda i, j, t: (t, b0 + j))]
    in_specs += [pl.BlockSpec((1, ta), lambda i, j, t: (0, a0 + i)) for _ in a_extra]
    in_specs += [ANY_SPEC] * len(deps)
    return pl.pallas_call(
        body,
        name=name,
        grid=(Ka // ta, Nb // tb, nt),
        in_specs=in_specs,
        out_specs=pl.BlockSpec((ta, tb), lambda i, j, t: (i, j)),
        out_shape=jax.ShapeDtypeStruct((Ka, Nb), out_dtype),
        scratch_shapes=[pltpu.VMEM((ta, tb), F32)],
        compiler_params=_cparams(("parallel", "parallel", "arbitrary")),
    )(a, b, *a_extra, *deps)


W = BRANCH_WIDTH
C_CB, C_CC, C_CH, C_HQ, C_HF, C_HI, C_HG, C_MQ, N_MIX = 0, W, 2 * W, 3 * W, 4 * W, 5 * W, 6 * W, 7 * W, 8 * W
TS_MIX = 256


def _sigmoid(x):
    return jax.nn.sigmoid(x)


def _chunk_pos(shape):
    return lax.broadcasted_iota(jnp.int32, shape, 0) & (HG_CHUNK - 1)


def _seg_cumsum(x, pos):
    sh = 1
    while sh < HG_CHUNK:
        x = x + jnp.where(pos >= sh, pltpu.roll(x, sh, 0), 0.0)
        sh *= 2
    return x


def _seg_rev_cumsum(x, pos):
    n = x.shape[0]
    sh = 1
    while sh < HG_CHUNK:
        x = x + jnp.where(pos < HG_CHUNK - sh, pltpu.roll(x, n - sh, 0), 0.0)
        sh *= 2
    return x


def _chunk_mask(ts):
    r = lax.broadcasted_iota(jnp.int32, (ts, ts), 0)
    c = lax.broadcasted_iota(jnp.int32, (ts, ts), 1)
    return jnp.logical_and((r // HG_CHUNK) == (c // HG_CHUNK), c <= r)


def _hgrn_gates(p_ref, lb):
    q = p_ref[:, C_HQ:C_HQ + W]
    fl = p_ref[:, C_HF:C_HF + W]
    sig = _sigmoid(fl)
    f = lb + (1.0 - lb) * sig
    logf = jnp.log(f)
    k = (1.0 - lb) * _sigmoid(-fl)
    sq = _sigmoid(q)
    qs = q * sq
    return q, sq, qs, sig, f, logf, k


def _hgrn_decays(logf, bc_sc, ts):
    pos = _chunk_pos(logf.shape)
    bc = _seg_cumsum(logf, pos)
    bc_sc[...] = bc
    nc = ts // HG_CHUNK
    bref = jnp.concatenate(
        [jnp.broadcast_to(bc_sc[n * HG_CHUNK + HG_CHUNK // 2 - 1:n * HG_CHUNK + HG_CHUNK // 2, :], (HG_CHUNK, W)) for n in range(nc)], axis=0)
    blast = jnp.concatenate(
        [jnp.broadcast_to(bc_sc[(n + 1) * HG_CHUNK - 1:(n + 1) * HG_CHUNK, :], (HG_CHUNK, W)) for n in range(nc)], axis=0)
    return pos, bc, bref, blast


def _conv_shift_down(u, carry_ref, row):
    u1 = jnp.where(row == 0, carry_ref[7:8, :], pltpu.roll(u, 1, 0))
    u2 = jnp.where(row == 0, carry_ref[6:7, :], jnp.where(row == 1, carry_ref[7:8, :], pltpu.roll(u, 2, 0)))
    return u1, u2


def _attn_probs(qh, kh):
    s = _dot(qh, kh, NT) * (MEM_HEAD_DIM ** -0.5)
    e = jnp.exp(s - jnp.max(s, axis=-1, keepdims=True))
    return e / jnp.sum(e, axis=-1, keepdims=True)


def _mixer_fwd(p, mk, mv, lb, conv_w, norm_w, *, bl, seq):
    T = p.shape[0]
    ts = TS_MIX
    ns = seq // ts
    nc = ts // HG_CHUNK
    ml = mk.shape[0] // bl

    def body(p_ref, mk_ref, mv_ref, lb_ref, cw_ref, nw_ref, y_ref, st_ref, opre_ref, state_sc, carry_sc, bc_sc):
        @pl.when(pl.program_id(1) == 0)
        def _():
            state_sc[...] = jnp.zeros_like(state_sc)
            carry_sc[...] = jnp.zeros_like(carry_sc)

        cb, cc, ch = p_ref[:, C_CB:C_CB + W], p_ref[:, C_CC:C_CC + W], p_ref[:, C_CH:C_CH + W]
        u = cc * ch
        row = lax.broadcasted_iota(jnp.int32, (ts, W), 0)
        u1, u2 = _conv_shift_down(u, carry_sc, row)
        yconv = u2 * cw_ref[0:1, :] + u1 * cw_ref[1:2, :] + u * cw_ref[2:3, :]
        y_ref[:, 0:W] = (cb * yconv).astype(BF16)
        carry_sc[...] = u[ts - 8:ts, :]

        lbv = lb_ref[...]
        _, _, qs, _, _, logf, k = _hgrn_gates(p_ref, lbv)
        pos, bc, bref, blast = _hgrn_decays(logf, bc_sc, ts)
        a_all = (qs * jnp.exp(bc - bref)).astype(BF16)
        bk_all = (k * jnp.exp(bref - bc)).astype(BF16)
        qin_all = (qs * jnp.exp(bc)).astype(BF16)
        kout_all = k * jnp.exp(blast - bc)
        v_all = p_ref[:, C_HI:C_HI + W]
        mask = _chunk_mask(ts)
        chunk_of_row = lax.broadcasted_iota(jnp.int32, (ts, HG_F), 0) // HG_CHUNK
        for h in range(HG_HEADS):
            hs = slice(h * HG_F, (h + 1) * HG_F)
            vb = v_all[:, hs].astype(BF16)
            vt = v_all[:, hs].T.astype(BF16)
            scores = jnp.where(mask, _dot(a_all[:, hs], bk_all[:, hs], NT), 0.0)
            o_intra = _dot(scores.astype(BF16), vb, NN)
            kout = kout_all[:, hs]
            st = state_sc[h]
            o_inter = []
            for n in range(nc):
                st_ref[n, h] = st
                o_inter.append(_dot(qin_all[n * HG_CHUNK:(n + 1) * HG_CHUNK, hs], st.astype(BF16), NT))
                kv = _dot(vt, jnp.where(chunk_of_row == n, kout, 0.0).astype(BF16), NN)
                decay = jnp.exp(bc_sc[(n + 1) * HG_CHUNK - 1:(n + 1) * HG_CHUNK, hs])
                st = st * decay + kv
            state_sc[h] = st
            o = o_intra + jnp.concatenate(o_inter, axis=0)
            opre_ref[:, hs] = o
            on = o * lax.rsqrt(jnp.mean(o * o, axis=-1, keepdims=True) + RMS_EPS) * nw_ref[...]
            g = p_ref[:, C_HG + h * HG_F:C_HG + (h + 1) * HG_F]
            y_ref[:, W + h * HG_F:W + (h + 1) * HG_F] = (on * (g * _sigmoid(g))).astype(BF16)

        for h in range(MEM_HEADS):
            hs = slice(h * MEM_HEAD_DIM, (h + 1) * MEM_HEAD_DIM)
            qh = p_ref[:, C_MQ + h * MEM_HEAD_DIM:C_MQ + (h + 1) * MEM_HEAD_DIM].astype(BF16)
            prob = _attn_probs(qh, mk_ref[:, hs])
            y_ref[:, 2 * W + h * MEM_HEAD_DIM:2 * W + (h + 1) * MEM_HEAD_DIM] = _dot(prob.astype(BF16), mv_ref[:, hs], NN).astype(BF16)

    return pl.pallas_call(
        body,
        name="mixer_fwd",
        grid=(bl, ns),
        in_specs=[
            pl.BlockSpec((ts, N_MIX), lambda b, s: (b * ns + s, 0)),
            pl.BlockSpec((ml, W), lambda b, s: (b, 0)),
            pl.BlockSpec((ml, W), lambda b, s: (b, 0)),
            pl.BlockSpec((1, W), lambda b, s: (0, 0)),
            pl.BlockSpec((CONV_K, W), lambda b, s: (0, 0)),
            pl.BlockSpec((1, HG_F), lambda b, s: (0, 0)),
        ],
        out_specs=[
            pl.BlockSpec((ts, 3 * W), lambda b, s: (b * ns + s, 0)),
            pl.BlockSpec((nc, HG_HEADS, HG_F, HG_F), lambda b, s: (b * ns + s, 0, 0, 0)),
            pl.BlockSpec((ts, W), lambda b, s: (b * ns + s, 0)),
        ],
        out_shape=[
            jax.ShapeDtypeStruct((T, 3 * W), BF16),
            jax.ShapeDtypeStruct((T // HG_CHUNK, HG_HEADS, HG_F, HG_F), F32),
            jax.ShapeDtypeStruct((T, W), F32),
        ],
        scratch_shapes=[pltpu.VMEM((HG_HEADS, HG_F, HG_F), F32), pltpu.VMEM((8, W), F32), pltpu.VMEM((ts, W), F32)],
        compiler_params=_cparams(("arbitrary", "arbitrary")),
    )(p, mk, mv, lb, conv_w, norm_w)


def _mixer_bwd(p, dy, dp_gates, st, opre, mk, mv, lb, conv_w, norm_w, *, bl, seq, deps=()):
    T, nin = p.shape
    ts = TS_MIX
    ns = seq // ts
    nc = ts // HG_CHUNK
    ml = mk.shape[0] // bl
    mid, last = HG_CHUNK // 2 - 1, HG_CHUNK - 1

    def body(p_ref, pprev_ref, dy_ref, dpin_ref, st_ref, opre_ref, mk_ref, mv_ref, lb_ref, cw_ref, nw_ref, *rest):
        dp_ref, dmk_ref, dmv_ref, dcw_ref, dnw_ref, dlb_ref, dstate_sc, carry_sc, uprev_sc, bc_sc = rest[len(deps):]
        del dpin_ref
        b, s = pl.program_id(0), pl.program_id(1)

        @pl.when(s == 0)
        def _():
            dstate_sc[...] = jnp.zeros_like(dstate_sc)
            carry_sc[...] = jnp.zeros_like(carry_sc)
            dmk_ref[...] = jnp.zeros_like(dmk_ref)
            dmv_ref[...] = jnp.zeros_like(dmv_ref)

        @pl.when(jnp.logical_and(b == 0, s == 0))
        def _():
            dcw_ref[...] = jnp.zeros_like(dcw_ref)
            dnw_ref[...] = jnp.zeros_like(dnw_ref)
            dlb_ref[...] = jnp.zeros_like(dlb_ref)

        cb, cc, ch = p_ref[:, C_CB:C_CB + W], p_ref[:, C_CC:C_CC + W], p_ref[:, C_CH:C_CH + W]
        u = cc * ch
        row = lax.broadcasted_iota(jnp.int32, (ts, W), 0)
        uprev = pprev_ref[:, C_CC:C_CC + W] * pprev_ref[:, C_CH:C_CH + W]
        uprev_sc[...] = jnp.where(s == ns - 1, 0.0, uprev)
        u1, u2 = _conv_shift_down(u, uprev_sc, row)
        w0, w1, w2 = cw_ref[0:1, :], cw_ref[1:2, :], cw_ref[2:3, :]
        dya = dy_ref[:, 0:W]
        dp_ref[:, C_CB:C_CB + W] = (dya * (u2 * w0 + u1 * w1 + u * w2)).astype(BF16)
        dv = cb * dya
        dv1 = jnp.where(row == ts - 1, carry_sc[0:1, :], pltpu.roll(dv, ts - 1, 0))
        dv2 = jnp.where(row == ts - 1, carry_sc[1:2, :], jnp.where(row == ts - 2, carry_sc[0:1, :], pltpu.roll(dv, ts - 2, 0)))
        du = dv * w2 + dv1 * w1 + dv2 * w0
        dp_ref[:, C_CC:C_CC + W] = (du * ch).astype(BF16)
        dp_ref[:, C_CH:C_CH + W] = (du * cc).astype(BF16)
        dcw_ref[0:1, :] += jnp.sum(dv * u2, axis=0, keepdims=True)
        dcw_ref[1:2, :] += jnp.sum(dv * u1, axis=0, keepdims=True)
        dcw_ref[2:3, :] += jnp.sum(dv * u, axis=0, keepdims=True)
        carry_sc[...] = dv[0:8, :]

        lbv = lb_ref[...]
        q_all, sq_all, qs_all, sig_all, f_all, logf, k_all = _hgrn_gates(p_ref, lbv)
        pos_all, bc, bref, blast = _hgrn_decays(logf, bc_sc, ts)
        ea_all, eb_all, eq_all, ek_all = jnp.exp(bc - bref), jnp.exp(bref - bc), jnp.exp(bc), jnp.exp(blast - bc)
        mask = _chunk_mask(ts)
        chunk_of_row = lax.broadcasted_iota(jnp.int32, (ts, HG_F), 0) // HG_CHUNK
        pos = _chunk_pos((ts, HG_F))
        pos_c = _chunk_pos((HG_CHUNK, HG_F))
        nw = nw_ref[...]
        for h in range(HG_HEADS):
            hs = slice(h * HG_F, (h + 1) * HG_F)
            qs, k, ea, eb, eq, ek = qs_all[:, hs], k_all[:, hs], ea_all[:, hs], eb_all[:, hs], eq_all[:, hs], ek_all[:, hs]
            a, bk, qin, kout = qs * ea, k * eb, qs * eq, k * ek
            o = opre_ref[:, hs]
            g = p_ref[:, C_HG + h * HG_F:C_HG + (h + 1) * HG_F]
            sg = _sigmoid(g)
            r = lax.rsqrt(jnp.mean(o * o, axis=-1, keepdims=True) + RMS_EPS)
            dyb = dy_ref[:, W + h * HG_F:W + (h + 1) * HG_F]
            dp_ref[:, C_HG + h * HG_F:C_HG + (h + 1) * HG_F] = (dyb * (o * r * nw) * (sg * (1.0 + g * (1.0 - sg)))).astype(BF16)
            don = dyb * (g * sg)
            dnw_ref[0:1, :] += jnp.sum(don * o * r, axis=0, keepdims=True)
            dn = don * nw
            do = r * (dn - o * (r * r) * jnp.mean(dn * o, axis=-1, keepdims=True))
            dob = do.astype(BF16)
            dot_ = do.T.astype(BF16)
            vb = p_ref[:, C_HI + h * HG_F:C_HI + (h + 1) * HG_F].astype(BF16)
            ab, bkb = a.astype(BF16), bk.astype(BF16)
            scores = jnp.where(mask, _dot(ab, bkb, NT), 0.0)
            dscores = jnp.where(mask, _dot(dob, vb, NT), 0.0)
            dv_h = _dot(scores.T.astype(BF16), dob, NN)
            da = _dot(dscores.astype(BF16), bkb, NN)
            dbk = _dot(dscores.T.astype(BF16), ab, NN)
            koutb = kout.astype(BF16)
            dst = dstate_sc[h]
            dqin_p, dkout_p, dvi_p, ddec_p = [None] * nc, [None] * nc, [None] * nc, [None] * nc
            for n in reversed(range(nc)):
                rows = slice(n * HG_CHUNK, (n + 1) * HG_CHUNK)
                st_n = st_ref[n, h]
                decay = jnp.exp(bc_sc[n * HG_CHUNK + last:n * HG_CHUNK + last + 1, hs])
                dstb = dst.astype(BF16)
                dvi_p[n] = _dot(koutb[rows], dstb, NT)
                dkout_p[n] = _dot(vb[rows], dstb, NN)
                ddec_p[n] = jnp.sum(dst * st_n, axis=0, keepdims=True) * decay
                dqin_p[n] = _dot(dob[rows], st_n.astype(BF16), NN)
                dst = dst * decay + _dot(dot_, jnp.where(chunk_of_row == n, qin, 0.0).astype(BF16), NN)
            dstate_sc[h] = dst
            dqin = jnp.concatenate(dqin_p, axis=0)
            dkout = jnp.concatenate(dkout_p, axis=0)
            dp_ref[:, C_HI + h * HG_F:C_HI + (h + 1) * HG_F] = (dv_h + jnp.concatenate(dvi_p, axis=0)).astype(BF16)
            dqs = da * ea + dqin * eq
            dk = dbk * eb + dkout * ek
            t_a, t_b, t_q, t_k = da * a, dbk * bk, dqin * qin, dkout * kout
            dbc = t_a - t_b + t_q - t_k
            t_ref = t_b - t_a
            pieces = []
            for n in range(nc):
                rows = slice(n * HG_CHUNK, (n + 1) * HG_CHUNK)
                s_ref = jnp.sum(t_ref[rows], axis=0, keepdims=True)
                s_last = jnp.sum(t_k[rows], axis=0, keepdims=True) + ddec_p[n]
                pieces.append(dbc[rows] + jnp.where(pos_c == mid, s_ref, 0.0) + jnp.where(pos_c == last, s_last, 0.0))
            dlogf = _seg_rev_cumsum(jnp.concatenate(pieces, axis=0), pos)
            sig, lbh = sig_all[:, hs], lbv[:, hs]
            dfk = dlogf / f_all[:, hs] - dk
            dp_ref[:, C_HF + h * HG_F:C_HF + (h + 1) * HG_F] = (dfk * (1.0 - lbh) * sig * (1.0 - sig)).astype(BF16)
            dlb_ref[0:1, hs] += jnp.sum(dfk * (1.0 - sig), axis=0, keepdims=True)
            q, sq = q_all[:, hs], sq_all[:, hs]
            dp_ref[:, C_HQ + h * HG_F:C_HQ + (h + 1) * HG_F] = (dqs * (sq * (1.0 + q * (1.0 - sq)))).astype(BF16)

        for h in range(MEM_HEADS):
            hs = slice(h * MEM_HEAD_DIM, (h + 1) * MEM_HEAD_DIM)
            qh = p_ref[:, C_MQ + h * MEM_HEAD_DIM:C_MQ + (h + 1) * MEM_HEAD_DIM].astype(BF16)
            kh, vh = mk_ref[:, hs], mv_ref[:, hs]
            prob = _attn_probs(qh, kh)
            dob = dy_ref[:, 2 * W + h * MEM_HEAD_DIM:2 * W + (h + 1) * MEM_HEAD_DIM].astype(BF16)
            dmv_ref[:, hs] += _dot(prob.T.astype(BF16), dob, NN)
            dprob = _dot(dob, vh, NT)
            ds = prob * (dprob - jnp.sum(dprob * prob, axis=-1, keepdims=True)) * (MEM_HEAD_DIM ** -0.5)
            dp_ref[:, C_MQ + h * MEM_HEAD_DIM:C_MQ + (h + 1) * MEM_HEAD_DIM] = _dot(ds.astype(BF16), kh, NN).astype(BF16)
            dmk_ref[:, hs] += _dot(ds.T.astype(BF16), qh, NN)

    def tile(b, s):
        return b * ns + (ns - 1 - s)

    return pl.pallas_call(
        body,
        name="mixer_bwd",
        grid=(bl, ns),
        in_specs=[
            pl.BlockSpec((ts, N_MIX), lambda b, s: (tile(b, s), 0)),
            pl.BlockSpec((8, N_MIX), lambda b, s: (jnp.maximum(tile(b, s) * (ts // 8) - 1, 0), 0)),
            pl.BlockSpec((ts, 3 * W), lambda b, s: (tile(b, s), 0)),
            pl.BlockSpec(memory_space=pl.ANY),
            pl.BlockSpec((nc, HG_HEADS, HG_F, HG_F), lambda b, s: (tile(b, s), 0, 0, 0)),
            pl.BlockSpec((ts, W), lambda b, s: (tile(b, s), 0)),
            pl.BlockSpec((ml, W), lambda b, s: (b, 0)),
            pl.BlockSpec((ml, W), lambda b, s: (b, 0)),
            pl.BlockSpec((1, W), lambda b, s: (0, 0)),
            pl.BlockSpec((CONV_K, W), lambda b, s: (0, 0)),
            pl.BlockSpec((1, HG_F), lambda b, s: (0, 0)),
        ] + [ANY_SPEC] * len(deps),
        out_specs=[
            pl.BlockSpec((ts, N_MIX), lambda b, s: (tile(b, s), 0)),
            pl.BlockSpec((ml, W), lambda b, s: (b, 0)),
            pl.BlockSpec((ml, W), lambda b, s: (b, 0)),
            pl.BlockSpec((8, W), lambda b, s: (0, 0)),
            pl.BlockSpec((8, HG_F), lambda b, s: (0, 0)),
            pl.BlockSpec((8, W), lambda b, s: (0, 0)),
        ],
        out_shape=[
            jax.ShapeDtypeStruct((T, nin), BF16),
            jax.ShapeDtypeStruct((bl * ml, W), F32),
            jax.ShapeDtypeStruct((bl * ml, W), F32),
            jax.ShapeDtypeStruct((8, W), F32),
            jax.ShapeDtypeStruct((8, HG_F), F32),
            jax.ShapeDtypeStruct((8, W), F32),
        ],
        input_output_aliases={3: 0},
        scratch_shapes=[pltpu.VMEM((HG_HEADS, HG_F, HG_F), F32), pltpu.VMEM((8, W), F32), pltpu.VMEM((8, W), F32),
                        pltpu.VMEM((ts, W), F32)],
        compiler_params=_cparams(("arbitrary", "arbitrary")),
    )(p, p, dy, dp_gates, st, opre, mk, mv, lb, conv_w, norm_w, *deps)


def _layer_norm_stats(z):
    mu = jnp.mean(z, axis=-1, keepdims=True)
    zc = z - mu
    rstd = lax.rsqrt(jnp.mean(zc * zc, axis=-1, keepdims=True) + LN_EPS)
    return zc * rstd, rstd


def _gate_specs(tm, d):
    g0 = N_MIX // d
    return [pl.BlockSpec((tm, d), functools.partial(lambda i, k: (i, g0 + k), k=k)) for k in range(N_BRANCH)]


def _merge_fwd(y, p, x0, wb, wo, bg, *, alpha, tm=256):
    T, d = x0.shape
    assert N_MIX % d == 0
    tm = _pick(T, (tm, 128, 8))

    def body(y_ref, g0_ref, g1_ref, g2_ref, x_ref, wb_ref, wo_ref, bg_ref, r_ref, mg_ref, xh_ref, rs_ref):
        merged = None
        for i, g_ref in enumerate((g0_ref, g1_ref, g2_ref)):
            r = _dot(y_ref[:, i * W:(i + 1) * W], wb_ref[i * W:(i + 1) * W, :], NN)
            r_ref[:, i * d:(i + 1) * d] = r
            t = _sigmoid(g_ref[...] + bg_ref[:, i * d:(i + 1) * d]) * r
            merged = t if merged is None else merged + t
        mb = merged.astype(BF16)
        mg_ref[...] = mb
        z = alpha * x_ref[...] + _dot(mb, wo_ref[...], NN)
        xh_ref[...], rs_ref[...] = _layer_norm_stats(z)

    row = lambda i: (i, 0)
    fix = lambda i: (0, 0)
    return pl.pallas_call(
        body,
        name="merge_fwd",
        grid=(T // tm,),
        in_specs=[pl.BlockSpec((tm, 3 * W), row)] + _gate_specs(tm, d) + [
            pl.BlockSpec((tm, d), row), pl.BlockSpec((3 * W, d), fix), pl.BlockSpec((d, d), fix), pl.BlockSpec((1, 3 * d), fix)],
        out_specs=[pl.BlockSpec((tm, 3 * d), row), pl.BlockSpec((tm, d), row), pl.BlockSpec((tm, d), row), pl.BlockSpec((tm, 1), row)],
        out_shape=[jax.ShapeDtypeStruct((T, 3 * d), F32), jax.ShapeDtypeStruct((T, d), BF16),
                   jax.ShapeDtypeStruct((T, d), F32), jax.ShapeDtypeStruct((T, 1), F32)],
        compiler_params=_cparams(("parallel",)),
    )(y, p, p, p, x0, wb, wo, bg)


def _merge_bwd(dz, p, r, wb, wo, bg, *, tm=256):
    T, d = dz.shape
    nin = p.shape[1]
    tm = _pick(T, (tm, 128, 8))

    def body(dz_ref, g0_ref, g1_ref, g2_ref, r_ref, wb_ref, wo_ref, bg_ref, dr_ref, dp_ref, dy_ref, dbg_ref):
        @pl.when(pl.program_id(0) == 0)
        def _():
            dbg_ref[...] = jnp.zeros_like(dbg_ref)

        dmerged = _dot(dz_ref[...].astype(BF16), wo_ref[...], NT)
        dp_ref[:, 0:N_MIX] = jnp.zeros((tm, N_MIX), BF16)
        for i, g_ref in enumerate((g0_ref, g1_ref, g2_ref)):
            cs = slice(i * d, (i + 1) * d)
            s = _sigmoid(g_ref[...] + bg_ref[:, cs])
            drb = (dmerged * s).astype(BF16)
            dr_ref[:, cs] = drb
            dgate = dmerged * r_ref[:, cs] * s * (1.0 - s)
            dp_ref[:, N_MIX + i * d:N_MIX + (i + 1) * d] = dgate.astype(BF16)
            dbg_ref[0:1, cs] += jnp.sum(dgate, axis=0, keepdims=True)
            dy_ref[:, i * W:(i + 1) * W] = _dot(drb, wb_ref[i * W:(i + 1) * W, :], NT)

    row = lambda i: (i, 0)
    fix = lambda i: (0, 0)
    return pl.pallas_call(
        body,
        name="merge_bwd",
        grid=(T // tm,),
        in_specs=[pl.BlockSpec((tm, d), row)] + _gate_specs(tm, d) + [
            pl.BlockSpec((tm, 3 * d), row), pl.BlockSpec((3 * W, d), fix), pl.BlockSpec((d, d), fix), pl.BlockSpec((1, 3 * d), fix)],
        out_specs=[pl.BlockSpec((tm, 3 * d), row), pl.BlockSpec((tm, nin), row), pl.BlockSpec((tm, 3 * W), row),
                   pl.BlockSpec((8, 3 * d), fix)],
        out_shape=[jax.ShapeDtypeStruct((T, 3 * d), BF16), jax.ShapeDtypeStruct((T, nin), BF16),
                   jax.ShapeDtypeStruct((T, 3 * W), F32), jax.ShapeDtypeStruct((8, 3 * d), F32)],
        compiler_params=_cparams(("arbitrary",)),
    )(dz, p, p, p, r, wb, wo, bg)


def _mlp_fwd(xhat1, g1, b1, wu, wd, g2, b2, *, alpha, tm=512, tf=1024):
    T, d = xhat1.shape
    ff = wu.shape[1]
    tm, tf = _pick(T, (tm, 256, 128, 8)), _pick(ff, (tf, 512, 256, 128))
    nf = ff // tf

    def body(xh_ref, g1_ref, b1_ref, wu_ref, wd_ref, g2_ref, b2_ref, a_ref, xh2_ref, rs2_ref, x2_ref, x2b_ref, acc_ref):
        f = pl.program_id(1)
        x1 = xh_ref[...] * g1_ref[...] + b1_ref[...]
        a = _dot(x1.astype(BF16), wu_ref[...], NN)
        a_ref[...] = a.astype(BF16)
        h = jnp.square(jnp.maximum(a, 0.0))
        part = _dot(h.astype(BF16), wd_ref[...], NN)

        @pl.when(f == 0)
        def _():
            acc_ref[...] = part

        @pl.when(f > 0)
        def _():
            acc_ref[...] += part

        @pl.when(f == nf - 1)
        def _():
            xh2, rs2 = _layer_norm_stats(alpha * x1 + acc_ref[...])
            xh2_ref[...] = xh2
            rs2_ref[...] = rs2
            x2 = xh2 * g2_ref[...] + b2_ref[...]
            x2_ref[...] = x2
            x2b_ref[...] = x2.astype(BF16)

    row = lambda i, f: (i, 0)
    fix = lambda i, f: (0, 0)
    return pl.pallas_call(
        body,
        name="mlp_fwd",
        grid=(T // tm, nf),
        in_specs=[pl.BlockSpec((tm, d), row), pl.BlockSpec((1, d), fix), pl.BlockSpec((1, d), fix),
                  pl.BlockSpec((d, tf), lambda i, f: (0, f)), pl.BlockSpec((tf, d), lambda i, f: (f, 0)),
                  pl.BlockSpec((1, d), fix), pl.BlockSpec((1, d), fix)],
        out_specs=[pl.BlockSpec((tm, tf), lambda i, f: (i, f)), pl.BlockSpec((tm, d), row), pl.BlockSpec((tm, 1), row),
                   pl.BlockSpec((tm, d), row), pl.BlockSpec((tm, d), row)],
        out_shape=[jax.ShapeDtypeStruct((T, ff), BF16), jax.ShapeDtypeStruct((T, d), F32), jax.ShapeDtypeStruct((T, 1), F32),
                   jax.ShapeDtypeStruct((T, d), F32), jax.ShapeDtypeStruct((T, d), BF16)],
        scratch_shapes=[pltpu.VMEM((tm, d), F32)],
        compiler_params=_cparams(("parallel", "arbitrary")),
    )(xhat1, g1, b1, wu, wd, g2, b2)


def _ln_bwd(dy, xhat, rstd, g, *, tm=512, deps=()):
    T, d = dy.shape
    tm = _pick(T, (tm, 256, 128, 8))

    def body(dy_ref, xh_ref, rs_ref, g_ref, *rest):
        dz_ref, dg_ref, db_ref = rest[len(deps):]

        @pl.when(pl.program_id(0) == 0)
        def _():
            dg_ref[...] = jnp.zeros_like(dg_ref)
            db_ref[...] = jnp.zeros_like(db_ref)

        dy_, xh = dy_ref[...], xh_ref[...]
        dg_ref[0:1, :] += jnp.sum(dy_ * xh, axis=0, keepdims=True)
        db_ref[0:1, :] += jnp.sum(dy_, axis=0, keepdims=True)
        dxh = dy_ * g_ref[...]
        dz_ref[...] = rs_ref[...] * (dxh - jnp.mean(dxh, axis=-1, keepdims=True) - xh * jnp.mean(dxh * xh, axis=-1, keepdims=True))

    row = lambda i: (i, 0)
    fix = lambda i: (0, 0)
    return pl.pallas_call(
        body,
        name="ln_bwd",
        grid=(T // tm,),
        in_specs=[pl.BlockSpec((tm, d), row), pl.BlockSpec((tm, d), row), pl.BlockSpec((tm, 1), row), pl.BlockSpec((1, d), fix)]
        + [ANY_SPEC] * len(deps),
        out_specs=[pl.BlockSpec((tm, d), row), pl.BlockSpec((8, d), fix), pl.BlockSpec((8, d), fix)],
        out_shape=[jax.ShapeDtypeStruct((T, d), F32), jax.ShapeDtypeStruct((8, d), F32), jax.ShapeDtypeStruct((8, d), F32)],
        compiler_params=_cparams(("arbitrary",)),
    )(dy, xhat, rstd, g, *deps)


def _loss_head(y, target, *, tm=512):
    T, d = y.shape
    tm = _pick(T, (tm, 256, 128, 8))
    n = T // tm

    def body(y_ref, t_ref, loss_ref, dy_ref, acc_ref):
        i = pl.program_id(0)

        @pl.when(i == 0)
        def _():
            acc_ref[...] = jnp.zeros_like(acc_ref)

        e = y_ref[...] - t_ref[...]
        dy_ref[...] = e * (1.0 / d)
        acc_ref[...] += jnp.sum(e * e, axis=0, keepdims=True)

        @pl.when(i == n - 1)
        def _():
            loss_ref[...] = (0.5 / d) * jnp.sum(acc_ref[...], axis=1, keepdims=True)

    row = lambda i: (i, 0)
    return pl.pallas_call(
        body,
        name="loss_head",
        grid=(n,),
        in_specs=[pl.BlockSpec((tm, d), row), pl.BlockSpec((tm, d), row)],
        out_specs=[pl.BlockSpec((1, 1), lambda i: (0, 0)), pl.BlockSpec((tm, d), row)],
        out_shape=[jax.ShapeDtypeStruct((1, 1), F32), jax.ShapeDtypeStruct((T, d), F32)],
        scratch_shapes=[pltpu.VMEM((1, d), F32)],
        compiler_params=_cparams(("arbitrary",)),
    )(y, target)


def _lower_bounds_fwd(lower_bounds):
    depth, n = lower_bounds.shape

    def body(x_ref, soft_ref, lb_ref):
        x = x_ref[...]
        e = jnp.exp(x - jnp.max(x, axis=0, keepdims=True))
        soft_ref[...] = e / jnp.sum(e, axis=0, keepdims=True)
        run = None
        for l in range(depth):
            run = soft_ref[l:l + 1, :] if run is None else run + soft_ref[l:l + 1, :]
            lb_ref[l:l + 1, :] = run - soft_ref[0:1, :]

    return pl.pallas_call(body, name="lower_bounds_fwd",
                          out_shape=[jax.ShapeDtypeStruct((depth, n), F32), jax.ShapeDtypeStruct((depth, n), F32)])(lower_bounds)


def _lower_bounds_bwd(soft, dlb):
    depth, n = soft.shape

    def body(soft_ref, dlb_ref, out_ref, dsoft_ref):
        total = jnp.sum(dlb_ref[...], axis=0, keepdims=True)
        run = None
        for l in reversed(range(depth)):
            run = dlb_ref[l:l + 1, :] if run is None else run + dlb_ref[l:l + 1, :]
            dsoft_ref[l:l + 1, :] = run - total if l == 0 else run
        s, ds = soft_ref[...], dsoft_ref[...]
        out_ref[...] = s * (ds - jnp.sum(s * ds, axis=0, keepdims=True))

    return pl.pallas_call(body, name="lower_bounds_bwd", out_shape=jax.ShapeDtypeStruct((depth, n), F32),
                          scratch_shapes=[pltpu.VMEM((depth, n), F32)])(soft, dlb)


def _layer_fwd(x0, x0b, mem2, lb, w_in, rest_fn, *, bl, seq, alpha, deps=()):
    p = _matmul("proj_in", x0b, w_in, mode="nn", deps=deps, tn=1792)
    wts = dict(rest_fn(p), w_in=w_in)
    mk = _matmul("mem_k", mem2, wts["w_mem_k"], mode="nn", out_dtype=BF16)
    mv = _matmul("mem_v", mem2, wts["w_mem_v"], mode="nn", out_dtype=BF16)
    y, st, opre = _mixer_fwd(p, mk, mv, lb, wts["conv_w"], wts["hg_norm_w"], bl=bl, seq=seq)
    r, merged, xhat1, rstd1 = _merge_fwd(y, p, x0, wts["w_branch"], wts["w_o"], wts["b_gate"], alpha=alpha)
    a, xhat2, rstd2, x2, x2b = _mlp_fwd(xhat1, wts["ln1_g"], wts["ln1_b"], wts["w_up"], wts["w_down"], wts["ln2_g"], wts["ln2_b"],
                                        alpha=alpha)
    saved = dict(x0b=x0b, p=p, mk=mk, mv=mv, y=y, st=st, opre=opre, r=r, merged=merged, xhat1=xhat1, rstd1=rstd1, a=a,
                 xhat2=xhat2, rstd2=rstd2)
    return x2, x2b, saved, wts


def _relu2_bf16(a):
    return jnp.square(jnp.maximum(a.astype(F32), 0.0)).astype(BF16)


def _affine_bf16(xh, g, b):
    return (xh * g + b).astype(BF16)


def _mlp_bwd(dx2, sv, wts, *, alpha, deps=()):
    g = {}
    dz2, dg2, db2 = _ln_bwd(dx2, sv["xhat2"], sv["rstd2"], wts["ln2_g"], deps=deps)
    g["ln2_g"], g["ln2_b"] = dg2[0:1], db2[0:1]
    da = _matmul("mlp_da", dz2, wts["w_down"], mode="nt", out_dtype=BF16,
                 epi_fn=lambda acc, a: (acc * (2.0 * jnp.maximum(a.astype(F32), 0.0)),), epi_extra=(sv["a"],))
    g["w_down"] = _matmul_tn("grad_w_down", sv["a"], dz2, a_fn=_relu2_bf16, out_dtype=BF16)
    g["w_up"] = _matmul_tn("grad_w_up", sv["xhat1"], da, a_fn=_affine_bf16, a_extra=(wts["ln1_g"], wts["ln1_b"]), out_dtype=BF16)
    dx1 = _matmul("mlp_dx", da, wts["w_up"], mode="nt", epi_fn=lambda acc, dz: (acc + alpha * dz,), epi_extra=(dz2,))
    return dx1, g


def _mix_bwd(dx1, sv, mem2, lb, wts, *, bl, seq, alpha, send, deps=()):
    d = dx1.shape[1]
    g = {}
    dz1, dg1, db1 = _ln_bwd(dx1, sv["xhat1"], sv["rstd1"], wts["ln1_g"], deps=deps)
    g["ln1_g"], g["ln1_b"] = dg1[0:1], db1[0:1]
    g["w_o"] = _matmul_tn("grad_w_o", sv["merged"], dz1, out_dtype=BF16)
    dr, dp, dy, dbg = _merge_bwd(dz1, sv["p"], sv["r"], wts["w_branch"], wts["w_o"], wts["b_gate"])
    g["b_gate"] = dbg[0:1]
    g["w_branch"] = jnp.concatenate(
        [_matmul_tn("grad_w_branch", sv["y"], dr, a_cols=(i * W, W), b_cols=(i * d, d), out_dtype=BF16) for i in range(N_BRANCH)],
        axis=0)
    token = send(("w_o", "w_branch"), g)
    dp, dmk, dmv, dcw, dnw, dlb = _mixer_bwd(sv["p"], dy, dp, sv["st"], sv["opre"], sv["mk"], sv["mv"], lb,
                                              wts["conv_w"], wts["hg_norm_w"], bl=bl, seq=seq, deps=(token,))
    g["conv_w"], g["hg_norm_w"], g["lb"] = dcw[0:CONV_K], dnw[0:1], dlb[0:1]
    g["w_mem_k"] = _matmul_tn("grad_w_mem_k", mem2, dmk, out_dtype=BF16)
    g["w_mem_v"] = _matmul_tn("grad_w_mem_v", mem2, dmv, out_dtype=BF16)
    g["w_in"] = _matmul_tn("grad_w_in", sv["x0b"], dp, out_dtype=BF16)
    token = send(("w_in", "w_mem_k", "w_mem_v", "conv_w"), g)
    dx0 = _matmul("proj_in_dx", dp, wts["w_in"], mode="nt", epi_fn=lambda acc, dz: (acc + alpha * dz,), epi_extra=(dz1,),
                  tm=1024, deps=(token,))
    return dx0, g


N_CHIPS = 4
MESH_IDS = pl.DeviceIdType.MESH


def _axis_slice(ref, axis, start, size):
    idx = [slice(None)] * len(ref.shape)
    idx[axis] = pl.ds(start, size)
    return ref.at[tuple(idx)]


def _chip_exchange(name, items):
    n = len(items)
    out_shapes, meta = [], []
    for arr, kind, axis in items:
        shp = list(arr.shape)
        if kind == "gather":
            per = shp[axis]
            shp[axis] = per * N_CHIPS
            out_shapes.append(jax.ShapeDtypeStruct(tuple(shp), arr.dtype))
        elif kind == "scatter":
            per = shp[axis] // N_CHIPS
            shp[axis] = per
            out_shapes.append(jax.ShapeDtypeStruct((N_CHIPS, *shp), arr.dtype))
        else:
            per = None
            out_shapes.append(jax.ShapeDtypeStruct((N_CHIPS, *shp), arr.dtype))
        meta.append((kind, axis, per))

    def body(*refs):
        ins, outs = refs[:n], refs[n:2 * n]
        send_sems, recv_sems, local_sems = refs[2 * n:]
        x, y, c = lax.axis_index("x"), lax.axis_index("y"), lax.axis_index("c")
        me = 2 * x + y
        peers = [(1 - x, y), (x, 1 - y), (1 - x, 1 - y)]

        def src_for(t, chip):
            kind, axis, per = meta[t]
            return _axis_slice(ins[t], axis, chip * per, per) if kind == "scatter" else ins[t]

        def dst_from(t, chip):
            kind, axis, per = meta[t]
            return _axis_slice(outs[t], axis, chip * per, per) if kind == "gather" else outs[t].at[chip]

        def remote(t, k):
            px, py = peers[k]
            return pltpu.make_async_remote_copy(
                src_ref=src_for(t, 2 * px + py), dst_ref=dst_from(t, me), send_sem=send_sems.at[t * 3 + k],
                recv_sem=recv_sems.at[t * 3 + k], device_id=(px, py, c), device_id_type=MESH_IDS)

        def arrival(t, k):
            px, py = peers[k]
            return pltpu.make_async_remote_copy(
                src_ref=src_for(t, me), dst_ref=dst_from(t, 2 * px + py), send_sem=send_sems.at[t * 3 + k],
                recv_sem=recv_sems.at[t * 3 + k], device_id=(px, py, c), device_id_type=MESH_IDS)

        sends = [remote(t, k) for t in range(n) for k in range(3)]
        for cp in sends:
            cp.start()
        own = [pltpu.make_async_copy(src_for(t, me), dst_from(t, me), local_sems.at[t]) for t in range(n)]
        for cp in own:
            cp.start()
        for t in range(n):
            for k in range(3):
                arrival(t, k).wait_recv()
        for cp in sends:
            cp.wait_send()
        for cp in own:
            cp.wait()

    any_spec = pl.BlockSpec(memory_space=pl.ANY)
    return pl.pallas_call(
        body,
        name=name,
        in_specs=[any_spec] * n,
        out_specs=[any_spec] * n,
        out_shape=out_shapes,
        scratch_shapes=[pltpu.SemaphoreType.DMA((3 * n,)), pltpu.SemaphoreType.DMA((3 * n,)), pltpu.SemaphoreType.DMA((n,))],
        compiler_params=pltpu.CompilerParams(has_side_effects=True),
    )(*[a for a, _, _ in items])


HBM_SPEC = pl.BlockSpec(memory_space=pltpu.HBM)
SEM_SPEC = pl.BlockSpec(memory_space=pltpu.SEMAPHORE)
N_PEERS = N_CHIPS - 1


def _my_chip():
    return (2 * lax.axis_index("x") + lax.axis_index("y")).astype(jnp.int32).reshape(1)


def _own_block_spec(r, c, axis, tr):
    if axis == 1:
        return pl.BlockSpec((tr, c), lambda i, me: (i, me[0]))
    return pl.BlockSpec((tr, c), lambda i, me: (me[0] * (r // tr) + i, 0))


def _place_shard(name, shard, axis, me):
    r, c = shard.shape
    tr = _row_block(r, c, shard.dtype.itemsize)
    shp = (r, c * N_CHIPS) if axis == 1 else (r * N_CHIPS, c)

    def body(me_ref, s_ref, o_ref):
        del me_ref
        o_ref[...] = s_ref[...]

    return pl.pallas_call(
        body, name=name,
        grid_spec=pltpu.PrefetchScalarGridSpec(
            num_scalar_prefetch=1, grid=(r // tr,),
            in_specs=[pl.BlockSpec((tr, c), lambda i, me: (i, 0))], out_specs=_own_block_spec(r, c, axis, tr)),
        out_shape=jax.ShapeDtypeStruct(shp, shard.dtype),
        compiler_params=_cparams(("parallel",)),
    )(me, shard)


class _Split:
    def __init__(self, name, items):
        self.name, self.n = name, len(items)
        self.srcs = [a for a, _, _ in items]
        self.meta, self.land_shapes = [], []
        for arr, kind, axis in items:
            shp = list(arr.shape)
            if kind == "gather":
                per = shp[axis]
                shp[axis] = per * N_CHIPS
                self.land_shapes.append(jax.ShapeDtypeStruct(tuple(shp), arr.dtype))
            else:
                per = shp[axis] // N_CHIPS
                shp[axis] = per
                self.land_shapes.append(jax.ShapeDtypeStruct((N_PEERS, *shp), arr.dtype))
            self.meta.append((kind, axis, per))

    def _src(self, ins, t, chip):
        kind, axis, per = self.meta[t]
        return _axis_slice(ins[t], axis, chip * per, per) if kind == "scatter" else ins[t]

    def _dst(self, lands, t, chip, slot):
        kind, axis, per = self.meta[t]
        return _axis_slice(lands[t], axis, chip * per, per) if kind == "gather" else lands[t].at[slot]

    def landing_zones(self, me):
        return [_place_shard(self.name + "_own", src, axis, me) if kind == "gather" else lax.empty(ls.shape, ls.dtype)
                for src, ls, (kind, axis, _) in zip(self.srcs, self.land_shapes, self.meta)]

    def _copies(self, ins, lands, send_sems, recv_sems, arrivals):
        x, y, c = lax.axis_index("x"), lax.axis_index("y"), lax.axis_index("c")
        me = 2 * x + y
        peers = [(1 - x, y), (x, 1 - y), (1 - x, 1 - y)]
        res = []
        for t in range(self.n):
            for k, (px, py) in enumerate(peers):
                theirs = 2 * px + py
                sems = dict(send_sem=send_sems.at[t * N_PEERS + k], recv_sem=recv_sems.at[t * N_PEERS + k],
                            device_id=(px, py, c), device_id_type=MESH_IDS)
                if arrivals:
                    res.append(pltpu.make_async_remote_copy(src_ref=self._src(ins, t, me), dst_ref=self._dst(lands, t, theirs, k), **sems))
                else:
                    res.append(pltpu.make_async_remote_copy(src_ref=self._src(ins, t, theirs), dst_ref=self._dst(lands, t, me, k), **sems))
        return res

    def start(self, lands):
        n = self.n

        def body(*refs):
            ins, lnd = refs[:n], refs[n:2 * n]
            send_sems, recv_sems = refs[2 * n], refs[2 * n + 1]
            token = refs[-1]
            for cp in self._copies(ins, lnd, send_sems, recv_sems, arrivals=False):
                cp.start()
            token[...] = jnp.zeros_like(token)

        hbm = lambda a: pltpu.HBM(a.shape, a.dtype)
        res = pl.pallas_call(
            body, name=self.name + "_start",
            in_specs=[HBM_SPEC] * (2 * n),
            out_specs=[SEM_SPEC, SEM_SPEC] + [HBM_SPEC] * (2 * n) + [pl.BlockSpec(memory_space=pltpu.VMEM)],
            out_shape=[pltpu.SemaphoreType.DMA((N_PEERS * n,)), pltpu.SemaphoreType.DMA((N_PEERS * n,))]
            + [hbm(a) for a in self.srcs] + [hbm(a) for a in self.land_shapes] + [jax.ShapeDtypeStruct((8, 128), F32)],
            input_output_aliases={i: 2 + i for i in range(2 * n)},
            compiler_params=pltpu.CompilerParams(has_side_effects=pltpu.SideEffectType.DATAFLOW_SIDE_EFFECTING),
        )(*[pltpu.with_memory_space_constraint(a, pltpu.HBM) for a in self.srcs],
          *[pltpu.with_memory_space_constraint(a, pltpu.HBM) for a in lands])
        return res[:-1], res[-1]

    def wait(self, state, after):
        n = self.n
        send_sems, recv_sems = state[0], state[1]
        srcs, lands = state[2:2 + n], state[2 + n:2 + 2 * n]

        def body(*refs):
            ins, lnd = refs[:n], refs[n:2 * n]
            s_sems, r_sems = refs[2 * n], refs[2 * n + 1]
            for cp in self._copies(ins, lnd, s_sems, r_sems, arrivals=True):
                cp.wait_recv()
            for cp in self._copies(ins, lnd, s_sems, r_sems, arrivals=False):
                cp.wait_send()

        hbm = lambda a: pltpu.HBM(a.shape, a.dtype)
        res = pl.pallas_call(
            body, name=self.name + "_wait",
            in_specs=[HBM_SPEC] * (2 * n) + [SEM_SPEC, SEM_SPEC, ANY_SPEC],
            out_specs=[HBM_SPEC] * (2 * n),
            out_shape=[hbm(a) for a in self.srcs] + [hbm(a) for a in self.land_shapes],
            input_output_aliases={i: i for i in range(2 * n)},
            compiler_params=pltpu.CompilerParams(has_side_effects=pltpu.SideEffectType.DATAFLOW_SIDE_EFFECTING),
        )(*srcs, *lands, send_sems, recv_sems, after)
        return res[:n], res[n:]


def _sibling_swap(name, arrays):
    n = len(arrays)

    def body(*refs):
        ins, outs = refs[:n], refs[n:2 * n]
        send_sems, recv_sems = refs[2 * n:]
        sibling = (lax.axis_index("x"), lax.axis_index("y"), 1 - lax.axis_index("c"))
        copies = [pltpu.make_async_remote_copy(src_ref=ins[t], dst_ref=outs[t], send_sem=send_sems.at[t], recv_sem=recv_sems.at[t],
                                               device_id=sibling, device_id_type=MESH_IDS) for t in range(n)]
        for cp in copies:
            cp.start()
        for cp in copies:
            cp.wait()

    any_spec = pl.BlockSpec(memory_space=pl.ANY)
    return pl.pallas_call(
        body,
        name=name,
        in_specs=[any_spec] * n,
        out_specs=[any_spec] * n,
        out_shape=[jax.ShapeDtypeStruct(a.shape, a.dtype) for a in arrays],
        scratch_shapes=[pltpu.SemaphoreType.DMA((n,)), pltpu.SemaphoreType.DMA((n,))],
        compiler_params=pltpu.CompilerParams(has_side_effects=True),
    )(*arrays)


def _row_block(r, c, itemsize=4, target=1 << 20):
    if r % 8 != 0:
        return r
    best = 8
    for tr in range(8, r + 1, 8):
        if r % tr == 0 and tr * c * itemsize <= target:
            best = tr
    return best


def _sum_chips_into(parts, stacked, layer):
    _, r, c = parts.shape
    tr = _row_block(r, c)

    def body(p_ref, s_ref, o_ref):
        del s_ref
        o_ref[...] = ((p_ref[0] + p_ref[1]) + p_ref[2]) + p_ref[3]

    return pl.pallas_call(
        body,
        name="sum_chips",
        grid=(r // tr,),
        in_specs=[pl.BlockSpec((N_CHIPS, tr, c), lambda i: (0, i, 0)), pl.BlockSpec(memory_space=pl.ANY)],
        out_specs=pl.BlockSpec((None, tr, c), lambda i: (layer, i, 0)),
        out_shape=jax.ShapeDtypeStruct(stacked.shape, stacked.dtype),
        input_output_aliases={1: 0},
        compiler_params=_cparams(("parallel",)),
    )(parts, stacked)


def _sum_own_and_peers(me, g, axis, landed):
    _, r, c = landed.shape
    tr = _row_block(r, c)

    def body(me_ref, g_ref, p_ref, o_ref):
        del me_ref
        o_ref[...] = ((g_ref[...].astype(F32) + p_ref[0].astype(F32)) + p_ref[1].astype(F32)) + p_ref[2].astype(F32)

    return pl.pallas_call(
        body, name="sum_chips_own",
        grid_spec=pltpu.PrefetchScalarGridSpec(
            num_scalar_prefetch=1, grid=(r // tr,),
            in_specs=[_own_block_spec(r, c, axis, tr), pl.BlockSpec((N_PEERS, tr, c), lambda i, me: (0, i, 0))],
            out_specs=pl.BlockSpec((tr, c), lambda i, me: (i, 0))),
        out_shape=jax.ShapeDtypeStruct((r, c), F32),
        compiler_params=_cparams(("parallel",)),
    )(me, g, landed)


def _adamw_math(w, m, v, g):
    m_new = ADAM_B1 * m + (1.0 - ADAM_B1) * g
    v_new = ADAM_B2 * v + (1.0 - ADAM_B2) * jnp.square(g)
    m_hat = m_new / (1.0 - ADAM_B1 ** ADAM_STEP)
    v_hat = v_new / (1.0 - ADAM_B2 ** ADAM_STEP)
    return -ADAM_LR * (m_hat / (jnp.sqrt(v_hat) + ADAM_EPS) + ADAM_WD * w), m_new, v_new


def _adamw(w, m, v, g_a, g_b):
    L, r, c = w.shape
    tr = _row_block(r, c, target=1 << 19)

    def body(w_ref, m_ref, v_ref, ga_ref, gb_ref, g_ref, d_ref, nm_ref, nv_ref):
        g = ga_ref[...] + gb_ref[...]
        g_ref[...] = g
        d_ref[...], nm_ref[...], nv_ref[...] = _adamw_math(w_ref[...], m_ref[...], v_ref[...], g)

    spec = pl.BlockSpec((None, tr, c), lambda l, i: (l, i, 0))
    return pl.pallas_call(
        body,
        name="adamw",
        grid=(L, r // tr),
        in_specs=[spec] * 5,
        out_specs=[spec] * 4,
        out_shape=[jax.ShapeDtypeStruct(w.shape, F32)] * 4,
        compiler_params=_cparams(("parallel", "parallel")),
    )(w, m, v, g_a, g_b)


def _adamw_layer(w, m, v, g_a, g_b, layer, outs):
    L, r, c = w.shape
    tr = _row_block(r, c, target=1 << 19)
    n_prev = 0 if outs is None else 4

    def body(w_ref, m_ref, v_ref, ga_ref, gb_ref, *rest):
        g_ref, d_ref, nm_ref, nv_ref = rest[n_prev:]
        g = ga_ref[...] + gb_ref[...]
        g_ref[...] = g
        d_ref[...], nm_ref[...], nv_ref[...] = _adamw_math(w_ref[...], m_ref[...], v_ref[...], g)

    at_layer = pl.BlockSpec((None, tr, c), lambda i: (layer, i, 0))
    flat = pl.BlockSpec((tr, c), lambda i: (i, 0))
    return pl.pallas_call(
        body,
        name="adamw_layer",
        grid=(r // tr,),
        in_specs=[at_layer] * 3 + [flat] * 2 + [ANY_SPEC] * n_prev,
        out_specs=[at_layer] * 4,
        out_shape=[jax.ShapeDtypeStruct(w.shape, F32)] * 4,
        input_output_aliases={5 + k: k for k in range(n_prev)},
        compiler_params=_cparams(("parallel",)),
    )(w, m, v, g_a, g_b, *(outs or ()))


SHARDED = (("w_in", 1), ("conv_w", 1), ("w_mem_k", 0), ("w_mem_v", 0), ("w_branch", 1), ("w_o", 0), ("w_up", 1), ("w_down", 0))
SMALL = ("lower_bounds", "hg_norm_w", "b_gate", "ln1_g", "ln1_b", "ln2_g", "ln2_b")
WEIGHT_ORDER = ("lower_bounds", "w_in", "conv_w", "hg_norm_w", "w_mem_k", "w_mem_v", "w_branch", "b_gate", "w_o", "ln1_g", "ln1_b",
                "w_up", "w_down", "ln2_g", "ln2_b")


def kernel(x, mem, lower_bounds, w_in, conv_w, hg_norm_w, w_mem_k, w_mem_v, w_branch, b_gate, w_o, ln1_g, ln1_b, w_up, w_down, ln2_g, ln2_b, loss_target, m_lower_bounds, m_w_in, m_conv_w, m_hg_norm_w, m_w_mem_k, m_w_mem_v, m_w_branch, m_b_gate, m_w_o, m_ln1_g, m_ln1_b, m_w_up, m_w_down, m_ln2_g, m_ln2_b, v_lower_bounds, v_w_in, v_conv_w, v_hg_norm_w, v_w_mem_k, v_w_mem_v, v_w_branch, v_b_gate, v_w_o, v_ln1_g, v_ln1_b, v_w_up, v_w_down, v_ln2_g, v_ln2_b):
    bl, seq, d = x.shape
    depth = w_in.shape[0]
    weights = dict(lower_bounds=lower_bounds, w_in=w_in, conv_w=conv_w, hg_norm_w=hg_norm_w, w_mem_k=w_mem_k, w_mem_v=w_mem_v,
                   w_branch=w_branch, b_gate=b_gate, w_o=w_o, ln1_g=ln1_g, ln1_b=ln1_b, w_up=w_up, w_down=w_down, ln2_g=ln2_g, ln2_b=ln2_b)
    mom_m = dict(lower_bounds=m_lower_bounds, w_in=m_w_in, conv_w=m_conv_w, hg_norm_w=m_hg_norm_w, w_mem_k=m_w_mem_k, w_mem_v=m_w_mem_v,
                 w_branch=m_w_branch, b_gate=m_b_gate, w_o=m_w_o, ln1_g=m_ln1_g, ln1_b=m_ln1_b, w_up=m_w_up, w_down=m_w_down,
                 ln2_g=m_ln2_g, ln2_b=m_ln2_b)
    mom_v = dict(lower_bounds=v_lower_bounds, w_in=v_w_in, conv_w=v_conv_w, hg_norm_w=v_hg_norm_w, w_mem_k=v_w_mem_k, w_mem_v=v_w_mem_v,
                 w_branch=v_w_branch, b_gate=v_b_gate, w_o=v_w_o, ln1_g=v_ln1_g, ln1_b=v_ln1_b, w_up=v_w_up, w_down=v_w_down,
                 ln2_g=v_ln2_g, ln2_b=v_ln2_b)

    def shard2d(name, l):
        w = weights[name][l]
        if name == "w_branch":
            return w.reshape(N_BRANCH * W, w.shape[-1]).astype(BF16)
        return w if name == "conv_w" else w.astype(BF16)

    me = _my_chip()

    shard_axis = dict(SHARDED)

    def start_exchange(name, kind, items):
        ex = _Split(name, [(arr, kind, shard_axis[nm]) for nm, arr in items])
        state, token = ex.start(ex.landing_zones(me))
        return ex, state, [nm for nm, _ in items], token

    def start_gathers(l):
        first = start_exchange(f"gather_in_l{l}", "gather", [("w_in", shard2d("w_in", l))])
        rest = start_exchange(f"gather_rest_l{l}", "gather", [(nm, shard2d(nm, l)) for nm, _ in SHARDED if nm != "w_in"])
        return first, rest

    def gathered(pend, after):
        ex, state, names, _ = pend
        return dict(zip(names, ex.wait(state, after=after)[1]))

    x2d, mem2, t2d = x.reshape(bl * seq, d), mem.reshape(-1, d), loss_target.reshape(bl * seq, d)
    alpha = (2.0 * depth) ** 0.25
    soft, lb_all = _lower_bounds_fwd(lower_bounds)

    h, hb, saved, layer_wts = x2d, x2d.astype(BF16), [], []
    pending = start_gathers(0)
    for l in range(depth):
        first, rest = pending
        w_in_l = gathered(first, h)["w_in"]

        def rest_fn(after, l=l, rest=rest):
            wts = gathered(rest, after)
            for name in ("hg_norm_w", "b_gate", "ln1_g", "ln1_b", "ln2_g", "ln2_b"):
                wts[name] = weights[name][l][None, :]
            return wts

        deps = (rest[3],)
        if l + 1 < depth:
            pending = start_gathers(l + 1)
            deps += (pending[0][3], pending[1][3])
        h, hb, sv, wts = _layer_fwd(h, hb, mem2, lb_all[l:l + 1], w_in_l, rest_fn, bl=bl, seq=seq, alpha=alpha, deps=deps)
        saved.append(sv)
        layer_wts.append(wts)
    loss, dh = _loss_head(h, t2d)

    shape3 = {name: (depth, weights[name].size // (depth * weights[name].shape[-1]), weights[name].shape[-1]) for name, _ in SHARDED}
    partial = [dict() for _ in range(depth)]
    smalls = [None] * depth
    outs = {name: None for name, _ in SHARDED}

    def finish_reduce(pend, l, after):
        ex, state, names, _ = pend
        sent, got = ex.wait(state, after=after)
        for nm, g_full, landed in zip(names, sent, got):
            partial[l][nm] = _sum_own_and_peers(me, g_full, shard_axis[nm], landed)

    def optimizer_step(l):
        names = [name for name, _ in SHARDED]
        theirs = _sibling_swap(f"swap_partials_l{l}", [partial[l][nm] for nm in names])
        for nm, other in zip(names, theirs):
            outs[nm] = _adamw_layer(weights[nm].reshape(shape3[nm]), mom_m[nm].reshape(shape3[nm]), mom_v[nm].reshape(shape3[nm]),
                                    partial[l][nm], other, l, outs[nm])
        return tuple(outs[nm][0] for nm in names)

    pending_mix, deps = [], ()
    for l in reversed(range(depth)):
        dx1, g_mlp = _mlp_bwd(dh, saved[l], layer_wts[l], alpha=alpha, deps=deps)
        pending_mlp = start_exchange(f"reduce_mlp_l{l}", "scatter", [(nm, g_mlp[nm]) for nm in ("w_up", "w_down")])
        deps = (pending_mlp[3],)
        if pending_mix:
            for pend in pending_mix:
                finish_reduce(pend, l + 1, dx1)
            deps += optimizer_step(l + 1)
        pending_mix = []

        def send(names, g, l=l, pending_mix=pending_mix):
            pend = start_exchange(f"reduce_{names[0]}_l{l}", "scatter", [(nm, g[nm]) for nm in names])
            pending_mix.append(pend)
            return pend[3]

        dh, g = _mix_bwd(dx1, saved[l], mem2, lb_all[l:l + 1], layer_wts[l], bl=bl, seq=seq, alpha=alpha, send=send, deps=deps)
        finish_reduce(pending_mlp, l, dh)
        deps = ()
        g.update(g_mlp, lower_bounds=g["lb"])
        smalls[l] = jnp.concatenate([g[nm] for nm in SMALL], axis=1)
    small_parts = _chip_exchange("reduce_small", [(jnp.stack(smalls), "bcast", 0)])[0]
    small_sum = _sum_chips_into(small_parts.reshape(N_CHIPS, depth, -1), jnp.zeros((1, depth, small_parts.shape[-1]), F32), 0)
    small_sum = small_sum.reshape(depth, 1, -1)
    small_theirs = _sibling_swap("swap_small", [small_sum])[0]
    for pend in pending_mix:
        finish_reduce(pend, 0, small_theirs)
    optimizer_step(0)

    outs = {name: [r.reshape(weights[name].shape) for r in res] for name, res in outs.items()}
    off = 0
    for name in SMALL:
        n = weights[name].shape[1]
        mine, other = small_sum[:, :, off:off + n], small_theirs[:, :, off:off + n]
        off += n
        if name == "lower_bounds":
            mine = _lower_bounds_bwd(soft, mine[:, 0, :])[:, None, :]
            other = _lower_bounds_bwd(soft, other[:, 0, :])[:, None, :]
        shp = (depth, 1, n)
        res = _adamw(weights[name].reshape(shp), mom_m[name].reshape(shp), mom_v[name].reshape(shp), mine, other)
        outs[name] = [r.reshape(weights[name].shape) for r in res]
    assert off == small_sum.shape[-1]

    total_loss = lax.psum(loss[0, 0], ("x", "y", "c"))
    result = [total_loss, dh.reshape(bl, seq, d)]
    for k in range(4):
        result += [outs[name][k] for name in WEIGHT_ORDER]
    return tuple(result)
```

```python
import functools

import jax
import jax.numpy as jnp
from jax import lax
from jax.experimental import pallas as pl
from jax.experimental.pallas import tpu as pltpu

F32 = jnp.float32
BF16 = jnp.bfloat16

HG_HEADS = 4
HG_F = 128
HG_CHUNK = 32
MEM_HEADS = 4
MEM_HEAD_DIM = 128
BRANCH_WIDTH = 512
N_BRANCH = 3
CONV_K = 3
LN_EPS = 1e-5
RMS_EPS = 1e-6
ADAM_LR = 0.001
ADAM_B1 = 0.9
ADAM_B2 = 0.999
ADAM_EPS = 1e-08
ADAM_WD = 0.01
ADAM_STEP = 10

VMEM_LIMIT = 48 * 1024 * 1024


def _cparams(sem):
    return pltpu.CompilerParams(dimension_semantics=sem, vmem_limit_bytes=VMEM_LIMIT)


def _dot(a, b, dims):
    return lax.dot_general(a, b, (dims, ((), ())), preferred_element_type=F32)


NN = ((1,), (0,))
NT = ((1,), (1,))
TN = ((0,), (0,))


def _pick(n, pref):
    for t in pref:
        if n % t == 0:
            return t
    return n


ANY_SPEC = pl.BlockSpec(memory_space=pl.ANY)


def _matmul(name, a, b, *, mode, out_dtype=F32, a_fn=None, a_extra=(), epi_fn=None, epi_extra=(), n_out=1,
            tm=512, tn=1024, tk=1024, deps=()):
    M, K = a.shape
    N = b.shape[1] if mode == "nn" else b.shape[0]
    tm, tn, tk = _pick(M, (tm, 256, 128, 8)), _pick(N, (tn, 896, 512, 256, 128)), _pick(K, (tk, 512, 256, 128))
    nk = K // tk
    n_ax, n_ex = len(a_extra), len(epi_extra)
    n_in = 2 + n_ax + n_ex + len(deps)
    out_dtypes = out_dtype if isinstance(out_dtype, (tuple, list)) else (out_dtype,) * n_out

    def body(*refs):
        a_ref, b_ref = refs[0], refs[1]
        ax_refs = refs[2:2 + n_ax]
        ex_refs = refs[2 + n_ax:2 + n_ax + n_ex]
        o_refs = refs[n_in:n_in + n_out]
        at = a_ref[...]
        at = a_fn(at, *[r[...] for r in ax_refs]) if a_fn is not None else at.astype(BF16)
        part = _dot(at, b_ref[...].astype(BF16), NN if mode == "nn" else NT)

        def finish(acc):
            outs = epi_fn(acc, *[r[...] for r in ex_refs]) if epi_fn is not None else (acc,)
            for o_ref, o in zip(o_refs, outs):
                o_ref[...] = o.astype(o_ref.dtype)

        if nk == 1:
            finish(part)
            return
        acc_ref = refs[-1]
        k = pl.program_id(2)

        @pl.when(k == 0)
        def _():
            acc_ref[...] = part

        @pl.when(jnp.logical_and(k > 0, k < nk - 1))
        def _():
            acc_ref[...] += part

        @pl.when(k == nk - 1)
        def _():
            finish(acc_ref[...] + part)

    in_specs = [pl.BlockSpec((tm, tk), lambda j, i, k: (i, k)),
                pl.BlockSpec((tk, tn), lambda j, i, k: (k, j)) if mode == "nn" else pl.BlockSpec((tn, tk), lambda j, i, k: (j, k))]
    in_specs += [pl.BlockSpec((1, tk), lambda j, i, k: (0, k)) for _ in a_extra]
    for e in epi_extra:
        if e.shape[0] == 1:
            in_specs.append(pl.BlockSpec((1, tn), lambda j, i, k: (0, j)))
        else:
            in_specs.append(pl.BlockSpec((tm, tn), lambda j, i, k: (i, j)))
    in_specs += [ANY_SPEC] * len(deps)
    out = pl.pallas_call(
        body,
        name=name,
        grid=(N // tn, M // tm, nk),
        in_specs=in_specs,
        out_specs=[pl.BlockSpec((tm, tn), lambda j, i, k: (i, j)) for _ in range(n_out)],
        out_shape=[jax.ShapeDtypeStruct((M, N), dt) for dt in out_dtypes],
        scratch_shapes=[pltpu.VMEM((tm, tn), F32)] if nk > 1 else [],
        compiler_params=_cparams(("parallel", "parallel", "arbitrary")),
    )(a, b, *a_extra, *epi_extra, *deps)
    return out[0] if n_out == 1 else out


def _matmul_tn(name, a, b, *, a_fn=None, a_extra=(), a_cols=None, b_cols=None, ta=1024, tb=1024, tt=1024, out_dtype=F32, deps=()):
    T = a.shape[0]
    a0, Ka = a_cols if a_cols is not None else (0, a.shape[1])
    b0, Nb = b_cols if b_cols is not None else (0, b.shape[1])
    ta, tb, tt = _pick(Ka, (ta, 512, 256, 128)), _pick(Nb, (tb, 896, 512, 256, 128)), _pick(T, (tt, 512, 256, 128))
    assert a0 % ta == 0 and b0 % tb == 0
    a0, b0 = a0 // ta, b0 // tb
    nt = T // tt
    n_ax = len(a_extra)

    def body(*refs):
        a_ref, b_ref = refs[0], refs[1]
        ax_refs = refs[2:2 + n_ax]
        o_ref = refs[2 + n_ax + len(deps)]
        acc_ref = refs[-1]
        t = pl.program_id(2)
        at = a_ref[...]
        at = a_fn(at, *[r[...] for r in ax_refs]) if a_fn is not None else at.astype(BF16)
        part = _dot(at, b_ref[...].astype(BF16), TN)

        @pl.when(t == 0)
        def _():
            acc_ref[...] = part

        @pl.when(jnp.logical_and(t > 0, t < nt - 1))
        def _():
            acc_ref[...] += part

        @pl.when(t == nt - 1)
        def _():
            o_ref[...] = (acc_ref[...] + part if nt > 1 else part).astype(o_ref.dtype)

    in_specs = [pl.BlockSpec((tt, ta), lambda i, j, t: (t, a0 + i)), pl.BlockSpec((tt, tb), lambda i, j, t: (t, b0 + j))]
    in_specs += [pl.BlockSpec((1, ta), lambda i, j, t: (0, a0 + i)) for _ in a_extra]
    in_specs += [ANY_SPEC] * len(deps)
    return pl.pallas_call(
        body,
        name=name,
        grid=(Ka // ta, Nb // tb, nt),
        in_specs=in_specs,
        out_specs=pl.BlockSpec((ta, tb), lambda i, j, t: (i, j)),
        out_shape=jax.ShapeDtypeStruct((Ka, Nb), out_dtype),
        scratch_shapes=[pltpu.VMEM((ta, tb), F32)],
        compiler_params=_cparams(("parallel", "parallel", "arbitrary")),
    )(a, b, *a_extra, *deps)


W = BRANCH_WIDTH
C_CB, C_CC, C_CH, C_HQ, C_HF, C_HI, C_HG, C_MQ, N_MIX = 0, W, 2 * W, 3 * W, 4 * W, 5 * W, 6 * W, 7 * W, 8 * W
TS_MIX = 256


def _sigmoid(x):
    return jax.nn.sigmoid(x)


def _chunk_pos(shape):
    return lax.broadcasted_iota(jnp.int32, shape, 0) & (HG_CHUNK - 1)


def _seg_cumsum(x, pos):
    sh = 1
    while sh < HG_CHUNK:
        x = x + jnp.where(pos >= sh, pltpu.roll(x, sh, 0), 0.0)
        sh *= 2
    return x


def _seg_rev_cumsum(x, pos):
    n = x.shape[0]
    sh = 1
    while sh < HG_CHUNK:
        x = x + jnp.where(pos < HG_CHUNK - sh, pltpu.roll(x, n - sh, 0), 0.0)
        sh *= 2
    return x


def _chunk_mask(ts):
    r = lax.broadcasted_iota(jnp.int32, (ts, ts), 0)
    c = lax.broadcasted_iota(jnp.int32, (ts, ts), 1)
    return jnp.logical_and((r // HG_CHUNK) == (c // HG_CHUNK), c <= r)


def _hgrn_gates(p_ref, lb):
    q = p_ref[:, C_HQ:C_HQ + W]
    fl = p_ref[:, C_HF:C_HF + W]
    sig = _sigmoid(fl)
    f = lb + (1.0 - lb) * sig
    logf = jnp.log(f)
    k = (1.0 - lb) * _sigmoid(-fl)
    sq = _sigmoid(q)
    qs = q * sq
    return q, sq, qs, sig, f, logf, k


def _hgrn_decays(logf, bc_sc, ts):
    pos = _chunk_pos(logf.shape)
    bc = _seg_cumsum(logf, pos)
    bc_sc[...] = bc
    nc = ts // HG_CHUNK
    bref = jnp.concatenate(
        [jnp.broadcast_to(bc_sc[n * HG_CHUNK + HG_CHUNK // 2 - 1:n * HG_CHUNK + HG_CHUNK // 2, :], (HG_CHUNK, W)) for n in range(nc)], axis=0)
    blast = jnp.concatenate(
        [jnp.broadcast_to(bc_sc[(n + 1) * HG_CHUNK - 1:(n + 1) * HG_CHUNK, :], (HG_CHUNK, W)) for n in range(nc)], axis=0)
    return pos, bc, bref, blast


def _conv_shift_down(u, carry_ref, row):
    u1 = jnp.where(row == 0, carry_ref[7:8, :], pltpu.roll(u, 1, 0))
    u2 = jnp.where(row == 0, carry_ref[6:7, :], jnp.where(row == 1, carry_ref[7:8, :], pltpu.roll(u, 2, 0)))
    return u1, u2


def _attn_probs(qh, kh):
    s = _dot(qh, kh, NT) * (MEM_HEAD_DIM ** -0.5)
    e = jnp.exp(s - jnp.max(s, axis=-1, keepdims=True))
    return e / jnp.sum(e, axis=-1, keepdims=True)


def _mixer_fwd(p, mk, mv, lb, conv_w, norm_w, *, bl, seq):
    T = p.shape[0]
    ts = TS_MIX
    ns = seq // ts
    nc = ts // HG_CHUNK
    ml = mk.shape[0] // bl

    def body(p_ref, mk_ref, mv_ref, lb_ref, cw_ref, nw_ref, y_ref, st_ref, opre_ref, state_sc, carry_sc, bc_sc):
        @pl.when(pl.program_id(1) == 0)
        def _():
            state_sc[...] = jnp.zeros_like(state_sc)
            carry_sc[...] = jnp.zeros_like(carry_sc)

        cb, cc, ch = p_ref[:, C_CB:C_CB + W], p_ref[:, C_CC:C_CC + W], p_ref[:, C_CH:C_CH + W]
        u = cc * ch
        row = lax.broadcasted_iota(jnp.int32, (ts, W), 0)
        u1, u2 = _conv_shift_down(u, carry_sc, row)
        yconv = u2 * cw_ref[0:1, :] + u1 * cw_ref[1:2, :] + u * cw_ref[2:3, :]
        y_ref[:, 0:W] = (cb * yconv).astype(BF16)
        carry_sc[...] = u[ts - 8:ts, :]

        lbv = lb_ref[...]
        _, _, qs, _, _, logf, k = _hgrn_gates(p_ref, lbv)
        pos, bc, bref, blast = _hgrn_decays(logf, bc_sc, ts)
        a_all = (qs * jnp.exp(bc - bref)).astype(BF16)
        bk_all = (k * jnp.exp(bref - bc)).astype(BF16)
        qin_all = (qs * jnp.exp(bc)).astype(BF16)
        kout_all = (k * jnp.exp(blast - bc)).astype(BF16)
        v_all = p_ref[:, C_HI:C_HI + W].astype(BF16)
        mask = _chunk_mask(ts)
        for h in range(HG_HEADS):
            hs = slice(h * HG_F, (h + 1) * HG_F)
            vb = v_all[:, hs]
            scores = jnp.where(mask, _dot(a_all[:, hs], bk_all[:, hs], NT), 0.0)
            o_intra = _dot(scores.astype(BF16), vb, NN)
            st = state_sc[h]
            o_inter = []
            for n in range(nc):
                rows = slice(n * HG_CHUNK, (n + 1) * HG_CHUNK)
                st_ref[n, h] = st
                o_inter.append(_dot(qin_all[rows, hs], st.astype(BF16), NT))
                kv = _dot(vb[rows], kout_all[rows, hs], TN)
                decay = jnp.exp(bc_sc[(n + 1) * HG_CHUNK - 1:(n + 1) * HG_CHUNK, hs])
                st = st * decay + kv
            state_sc[h] = st
            o = o_intra + jnp.concatenate(o_inter, axis=0)
            opre_ref[:, hs] = o
            on = o * lax.rsqrt(jnp.mean(o * o, axis=-1, keepdims=True) + RMS_EPS) * nw_ref[...]
            g = p_ref[:, C_HG + h * HG_F:C_HG + (h + 1) * HG_F]
            y_ref[:, W + h * HG_F:W + (h + 1) * HG_F] = (on * (g * _sigmoid(g))).astype(BF16)

        for h in range(MEM_HEADS):
            hs = slice(h * MEM_HEAD_DIM, (h + 1) * MEM_HEAD_DIM)
            qh = p_ref[:, C_MQ + h * MEM_HEAD_DIM:C_MQ + (h + 1) * MEM_HEAD_DIM].astype(BF16)
            prob = _attn_probs(qh, mk_ref[:, hs])
            y_ref[:, 2 * W + h * MEM_HEAD_DIM:2 * W + (h + 1) * MEM_HEAD_DIM] = _dot(prob.astype(BF16), mv_ref[:, hs], NN).astype(BF16)

    return pl.pallas_call(
        body,
        name="mixer_fwd",
        grid=(bl, ns),
        in_specs=[
            pl.BlockSpec((ts, N_MIX), lambda b, s: (b * ns + s, 0)),
            pl.BlockSpec((ml, W), lambda b, s: (b, 0)),
            pl.BlockSpec((ml, W), lambda b, s: (b, 0)),
            pl.BlockSpec((1, W), lambda b, s: (0, 0)),
            pl.BlockSpec((CONV_K, W), lambda b, s: (0, 0)),
            pl.BlockSpec((1, HG_F), lambda b, s: (0, 0)),
        ],
        out_specs=[
            pl.BlockSpec((ts, 3 * W), lambda b, s: (b * ns + s, 0)),
            pl.BlockSpec((nc, HG_HEADS, HG_F, HG_F), lambda b, s: (b * ns + s, 0, 0, 0)),
            pl.BlockSpec((ts, W), lambda b, s: (b * ns + s, 0)),
        ],
        out_shape=[
            jax.ShapeDtypeStruct((T, 3 * W), BF16),
            jax.ShapeDtypeStruct((T // HG_CHUNK, HG_HEADS, HG_F, HG_F), F32),
            jax.ShapeDtypeStruct((T, W), F32),
        ],
        scratch_shapes=[pltpu.VMEM((HG_HEADS, HG_F, HG_F), F32), pltpu.VMEM((8, W), F32), pltpu.VMEM((ts, W), F32)],
        compiler_params=_cparams(("arbitrary", "arbitrary")),
    )(p, mk, mv, lb, conv_w, norm_w)


def _mixer_bwd(p, dy, dp_gates, st, opre, mk, mv, lb, conv_w, norm_w, *, bl, seq, deps=()):
    T, nin = p.shape
    ts = TS_MIX
    ns = seq // ts
    nc = ts // HG_CHUNK
    ml = mk.shape[0] // bl
    mid, last = HG_CHUNK // 2 - 1, HG_CHUNK - 1

    def body(p_ref, pprev_ref, dy_ref, dpin_ref, st_ref, opre_ref, mk_ref, mv_ref, lb_ref, cw_ref, nw_ref, *rest):
        dp_ref, dmk_ref, dmv_ref, dcw_ref, dnw_ref, dlb_ref, dstate_sc, carry_sc, uprev_sc, bc_sc = rest[len(deps):]
        del dpin_ref
        b, s = pl.program_id(0), pl.program_id(1)

        @pl.when(s == 0)
        def _():
            dstate_sc[...] = jnp.zeros_like(dstate_sc)
            carry_sc[...] = jnp.zeros_like(carry_sc)
            dmk_ref[...] = jnp.zeros_like(dmk_ref)
            dmv_ref[...] = jnp.zeros_like(dmv_ref)

        @pl.when(jnp.logical_and(b == 0, s == 0))
        def _():
            dcw_ref[...] = jnp.zeros_like(dcw_ref)
            dnw_ref[...] = jnp.zeros_like(dnw_ref)
            dlb_ref[...] = jnp.zeros_like(dlb_ref)

        cb, cc, ch = p_ref[:, C_CB:C_CB + W], p_ref[:, C_CC:C_CC + W], p_ref[:, C_CH:C_CH + W]
        u = cc * ch
        row = lax.broadcasted_iota(jnp.int32, (ts, W), 0)
        uprev = pprev_ref[:, C_CC:C_CC + W] * pprev_ref[:, C_CH:C_CH + W]
        uprev_sc[...] = jnp.where(s == ns - 1, 0.0, uprev)
        u1, u2 = _conv_shift_down(u, uprev_sc, row)
        w0, w1, w2 = cw_ref[0:1, :], cw_ref[1:2, :], cw_ref[2:3, :]
        dya = dy_ref[:, 0:W]
        dp_ref[:, C_CB:C_CB + W] = (dya * (u2 * w0 + u1 * w1 + u * w2)).astype(BF16)
        dv = cb * dya
        dv1 = jnp.where(row == ts - 1, carry_sc[0:1, :], pltpu.roll(dv, ts - 1, 0))
        dv2 = jnp.where(row == ts - 1, carry_sc[1:2, :], jnp.where(row == ts - 2, carry_sc[0:1, :], pltpu.roll(dv, ts - 2, 0)))
        du = dv * w2 + dv1 * w1 + dv2 * w0
        dp_ref[:, C_CC:C_CC + W] = (du * ch).astype(BF16)
        dp_ref[:, C_CH:C_CH + W] = (du * cc).astype(BF16)
        dcw_ref[0:1, :] += jnp.sum(dv * u2, axis=0, keepdims=True)
        dcw_ref[1:2, :] += jnp.sum(dv * u1, axis=0, keepdims=True)
        dcw_ref[2:3, :] += jnp.sum(dv * u, axis=0, keepdims=True)
        carry_sc[...] = dv[0:8, :]

        lbv = lb_ref[...]
        q_all, sq_all, qs_all, sig_all, f_all, logf, k_all = _hgrn_gates(p_ref, lbv)
        pos_all, bc, bref, blast = _hgrn_decays(logf, bc_sc, ts)
        ea_all, eb_all, eq_all, ek_all = jnp.exp(bc - bref), jnp.exp(bref - bc), jnp.exp(bc), jnp.exp(blast - bc)
        mask = _chunk_mask(ts)
        pos = _chunk_pos((ts, HG_F))
        pos_c = _chunk_pos((HG_CHUNK, HG_F))
        nw = nw_ref[...]
        for h in range(HG_HEADS):
            hs = slice(h * HG_F, (h + 1) * HG_F)
            qs, k, ea, eb, eq, ek = qs_all[:, hs], k_all[:, hs], ea_all[:, hs], eb_all[:, hs], eq_all[:, hs], ek_all[:, hs]
            a, bk, qin, kout = qs * ea, k * eb, qs * eq, k * ek
            o = opre_ref[:, hs]
            g = p_ref[:, C_HG + h * HG_F:C_HG + (h + 1) * HG_F]
            sg = _sigmoid(g)
            r = lax.rsqrt(jnp.mean(o * o, axis=-1, keepdims=True) + RMS_EPS)
            dyb = dy_ref[:, W + h * HG_F:W + (h + 1) * HG_F]
            dp_ref[:, C_HG + h * HG_F:C_HG + (h + 1) * HG_F] = (dyb * (o * r * nw) * (sg * (1.0 + g * (1.0 - sg)))).astype(BF16)
            don = dyb * (g * sg)
            dnw_ref[0:1, :] += jnp.sum(don * o * r, axis=0, keepdims=True)
            dn = don * nw
            do = r * (dn - o * (r * r) * jnp.mean(dn * o, axis=-1, keepdims=True))
            dob = do.astype(BF16)
            vb = p_ref[:, C_HI + h * HG_F:C_HI + (h + 1) * HG_F].astype(BF16)
            ab, bkb = a.astype(BF16), bk.astype(BF16)
            scores = jnp.where(mask, _dot(ab, bkb, NT), 0.0)
            dscores = jnp.where(mask, _dot(dob, vb, NT), 0.0).astype(BF16)
            dv_h = _dot(scores.astype(BF16), dob, TN)
            da = _dot(dscores, bkb, NN)
            dbk = _dot(dscores, ab, TN)
            koutb, qinb = kout.astype(BF16), qin.astype(BF16)
            dst = dstate_sc[h]
            dqin_p, dkout_p, dvi_p, ddec_p = [None] * nc, [None] * nc, [None] * nc, [None] * nc
            for n in reversed(range(nc)):
                rows = slice(n * HG_CHUNK, (n + 1) * HG_CHUNK)
                st_n = st_ref[n, h]
                decay = jnp.exp(bc_sc[n * HG_CHUNK + last:n * HG_CHUNK + last + 1, hs])
                dstb = dst.astype(BF16)
                dvi_p[n] = _dot(koutb[rows], dstb, NT)
                dkout_p[n] = _dot(vb[rows], dstb, NN)
                ddec_p[n] = jnp.sum(dst * st_n, axis=0, keepdims=True) * decay
                dqin_p[n] = _dot(dob[rows], st_n.astype(BF16), NN)
                dst = dst * decay + _dot(dob[rows], qinb[rows], TN)
            dstate_sc[h] = dst
            dqin = jnp.concatenate(dqin_p, axis=0)
            dkout = jnp.concatenate(dkout_p, axis=0)
            dp_ref[:, C_HI + h * HG_F:C_HI + (h + 1) * HG_F] = (dv_h + jnp.concatenate(dvi_p, axis=0)).astype(BF16)
            dqs = da * ea + dqin * eq
            dk = dbk * eb + dkout * ek
            t_a, t_b, t_q, t_k = da * a, dbk * bk, dqin * qin, dkout * kout
            dbc = t_a - t_b + t_q - t_k
            t_ref = t_b - t_a
            pieces = []
            for n in range(nc):
                rows = slice(n * HG_CHUNK, (n + 1) * HG_CHUNK)
                s_ref = jnp.sum(t_ref[rows], axis=0, keepdims=True)
                s_last = jnp.sum(t_k[rows], axis=0, keepdims=True) + ddec_p[n]
                pieces.append(dbc[rows] + jnp.where(pos_c == mid, s_ref, 0.0) + jnp.where(pos_c == last, s_last, 0.0))
            dlogf = _seg_rev_cumsum(jnp.concatenate(pieces, axis=0), pos)
            sig, lbh = sig_all[:, hs], lbv[:, hs]
            dfk = dlogf / f_all[:, hs] - dk
            dp_ref[:, C_HF + h * HG_F:C_HF + (h + 1) * HG_F] = (dfk * (1.0 - lbh) * sig * (1.0 - sig)).astype(BF16)
            dlb_ref[0:1, hs] += jnp.sum(dfk * (1.0 - sig), axis=0, keepdims=True)
            q, sq = q_all[:, hs], sq_all[:, hs]
            dp_ref[:, C_HQ + h * HG_F:C_HQ + (h + 1) * HG_F] = (dqs * (sq * (1.0 + q * (1.0 - sq)))).astype(BF16)

        for h in range(MEM_HEADS):
            hs = slice(h * MEM_HEAD_DIM, (h + 1) * MEM_HEAD_DIM)
            qh = p_ref[:, C_MQ + h * MEM_HEAD_DIM:C_MQ + (h + 1) * MEM_HEAD_DIM].astype(BF16)
            kh, vh = mk_ref[:, hs], mv_ref[:, hs]
            prob = _attn_probs(qh, kh)
            dob = dy_ref[:, 2 * W + h * MEM_HEAD_DIM:2 * W + (h + 1) * MEM_HEAD_DIM].astype(BF16)
            dmv_ref[:, hs] += _dot(prob.astype(BF16), dob, TN)
            dprob = _dot(dob, vh, NT)
            ds = prob * (dprob - jnp.sum(dprob * prob, axis=-1, keepdims=True)) * (MEM_HEAD_DIM ** -0.5)
            dsb = ds.astype(BF16)
            dp_ref[:, C_MQ + h * MEM_HEAD_DIM:C_MQ + (h + 1) * MEM_HEAD_DIM] = _dot(dsb, kh, NN).astype(BF16)
            dmk_ref[:, hs] += _dot(dsb, qh, TN)

    def tile(b, s):
        return b * ns + (ns - 1 - s)

    return pl.pallas_call(
        body,
        name="mixer_bwd",
        grid=(bl, ns),
        in_specs=[
            pl.BlockSpec((ts, N_MIX), lambda b, s: (tile(b, s), 0)),
            pl.BlockSpec((8, N_MIX), lambda b, s: (jnp.maximum(tile(b, s) * (ts // 8) - 1, 0), 0)),
            pl.BlockSpec((ts, 3 * W), lambda b, s: (tile(b, s), 0)),
            pl.BlockSpec(memory_space=pl.ANY),
            pl.BlockSpec((nc, HG_HEADS, HG_F, HG_F), lambda b, s: (tile(b, s), 0, 0, 0)),
            pl.BlockSpec((ts, W), lambda b, s: (tile(b, s), 0)),
            pl.BlockSpec((ml, W), lambda b, s: (b, 0)),
            pl.BlockSpec((ml, W), lambda b, s: (b, 0)),
            pl.BlockSpec((1, W), lambda b, s: (0, 0)),
            pl.BlockSpec((CONV_K, W), lambda b, s: (0, 0)),
            pl.BlockSpec((1, HG_F), lambda b, s: (0, 0)),
        ] + [ANY_SPEC] * len(deps),
        out_specs=[
            pl.BlockSpec((ts, N_MIX), lambda b, s: (tile(b, s), 0)),
            pl.BlockSpec((ml, W), lambda b, s: (b, 0)),
            pl.BlockSpec((ml, W), lambda b, s: (b, 0)),
            pl.BlockSpec((8, W), lambda b, s: (0, 0)),
            pl.BlockSpec((8, HG_F), lambda b, s: (0, 0)),
            pl.BlockSpec((8, W), lambda b, s: (0, 0)),
        ],
        out_shape=[
            jax.ShapeDtypeStruct((T, nin), BF16),
            jax.ShapeDtypeStruct((bl * ml, W), F32),
            jax.ShapeDtypeStruct((bl * ml, W), F32),
            jax.ShapeDtypeStruct((8, W), F32),
            jax.ShapeDtypeStruct((8, HG_F), F32),
            jax.ShapeDtypeStruct((8, W), F32),
        ],
        input_output_aliases={3: 0},
        scratch_shapes=[pltpu.VMEM((HG_HEADS, HG_F, HG_F), F32), pltpu.VMEM((8, W), F32), pltpu.VMEM((8, W), F32),
                        pltpu.VMEM((ts, W), F32)],
        compiler_params=_cparams(("arbitrary", "arbitrary")),
    )(p, p, dy, dp_gates, st, opre, mk, mv, lb, conv_w, norm_w, *deps)


def _layer_norm_stats(z):
    mu = jnp.mean(z, axis=-1, keepdims=True)
    zc = z - mu
    rstd = lax.rsqrt(jnp.mean(zc * zc, axis=-1, keepdims=True) + LN_EPS)
    return zc * rstd, rstd


def _gate_specs(tm, d):
    g0 = N_MIX // d
    return [pl.BlockSpec((tm, d), functools.partial(lambda i, k: (i, g0 + k), k=k)) for k in range(N_BRANCH)]


def _merge_fwd(y, p, x0, wb, wo, bg, ln_g, ln_b, *, alpha, tm=256):
    T, d = x0.shape
    assert N_MIX % d == 0
    tm = _pick(T, (tm, 128, 8))

    def body(y_ref, g0_ref, g1_ref, g2_ref, x_ref, wb_ref, wo_ref, bg_ref, lg_ref, lb_ref, r_ref, mg_ref, xh_ref, rs_ref, x1b_ref):
        merged = None
        for i, g_ref in enumerate((g0_ref, g1_ref, g2_ref)):
            r = _dot(y_ref[:, i * W:(i + 1) * W], wb_ref[i * W:(i + 1) * W, :], NN)
            r_ref[:, i * d:(i + 1) * d] = r
            t = _sigmoid(g_ref[...] + bg_ref[:, i * d:(i + 1) * d]) * r
            merged = t if merged is None else merged + t
        mb = merged.astype(BF16)
        mg_ref[...] = mb
        z = alpha * x_ref[...] + _dot(mb, wo_ref[...], NN)
        xh, rs = _layer_norm_stats(z)
        xh_ref[...], rs_ref[...] = xh, rs
        x1b_ref[...] = (xh * lg_ref[...] + lb_ref[...]).astype(BF16)

    row = lambda i: (i, 0)
    fix = lambda i: (0, 0)
    return pl.pallas_call(
        body,
        name="merge_fwd",
        grid=(T // tm,),
        in_specs=[pl.BlockSpec((tm, 3 * W), row)] + _gate_specs(tm, d) + [
            pl.BlockSpec((tm, d), row), pl.BlockSpec((3 * W, d), fix), pl.BlockSpec((d, d), fix), pl.BlockSpec((1, 3 * d), fix),
            pl.BlockSpec((1, d), fix), pl.BlockSpec((1, d), fix)],
        out_specs=[pl.BlockSpec((tm, 3 * d), row), pl.BlockSpec((tm, d), row), pl.BlockSpec((tm, d), row), pl.BlockSpec((tm, 1), row),
                   pl.BlockSpec((tm, d), row)],
        out_shape=[jax.ShapeDtypeStruct((T, 3 * d), F32), jax.ShapeDtypeStruct((T, d), BF16),
                   jax.ShapeDtypeStruct((T, d), F32), jax.ShapeDtypeStruct((T, 1), F32), jax.ShapeDtypeStruct((T, d), BF16)],
        compiler_params=_cparams(("parallel",)),
    )(y, p, p, p, x0, wb, wo, bg, ln_g, ln_b)


def _merge_bwd(dz, p, r, wb, wo, bg, *, tm=256):
    T, d = dz.shape
    nin = p.shape[1]
    tm = _pick(T, (tm, 128, 8))

    def body(dz_ref, g0_ref, g1_ref, g2_ref, r_ref, wb_ref, wo_ref, bg_ref, dr_ref, dp_ref, dy_ref, dbg_ref):
        @pl.when(pl.program_id(0) == 0)
        def _():
            dbg_ref[...] = jnp.zeros_like(dbg_ref)

        dmerged = _dot(dz_ref[...].astype(BF16), wo_ref[...], NT)
        dp_ref[:, 0:N_MIX] = jnp.zeros((tm, N_MIX), BF16)
        for i, g_ref in enumerate((g0_ref, g1_ref, g2_ref)):
            cs = slice(i * d, (i + 1) * d)
            s = _sigmoid(g_ref[...] + bg_ref[:, cs])
            drb = (dmerged * s).astype(BF16)
            dr_ref[:, cs] = drb
            dgate = dmerged * r_ref[:, cs] * s * (1.0 - s)
            dp_ref[:, N_MIX + i * d:N_MIX + (i + 1) * d] = dgate.astype(BF16)
            dbg_ref[0:1, cs] += jnp.sum(dgate, axis=0, keepdims=True)
            dy_ref[:, i * W:(i + 1) * W] = _dot(drb, wb_ref[i * W:(i + 1) * W, :], NT)

    row = lambda i: (i, 0)
    fix = lambda i: (0, 0)
    return pl.pallas_call(
        body,
        name="merge_bwd",
        grid=(T // tm,),
        in_specs=[pl.BlockSpec((tm, d), row)] + _gate_specs(tm, d) + [
            pl.BlockSpec((tm, 3 * d), row), pl.BlockSpec((3 * W, d), fix), pl.BlockSpec((d, d), fix), pl.BlockSpec((1, 3 * d), fix)],
        out_specs=[pl.BlockSpec((tm, 3 * d), row), pl.BlockSpec((tm, nin), row), pl.BlockSpec((tm, 3 * W), row),
                   pl.BlockSpec((8, 3 * d), fix)],
        out_shape=[jax.ShapeDtypeStruct((T, 3 * d), BF16), jax.ShapeDtypeStruct((T, nin), BF16),
                   jax.ShapeDtypeStruct((T, 3 * W), F32), jax.ShapeDtypeStruct((8, 3 * d), F32)],
        compiler_params=_cparams(("arbitrary",)),
    )(dz, p, p, p, r, wb, wo, bg)


def _mlp_fwd(xhat1, g1, b1, wu, wd, g2, b2, *, alpha, tm=512, tf=1024):
    T, d = xhat1.shape
    ff = wu.shape[1]
    tm, tf = _pick(T, (tm, 256, 128, 8)), _pick(ff, (tf, 512, 256, 128))
    nf = ff // tf

    def body(xh_ref, g1_ref, b1_ref, wu_ref, wd_ref, g2_ref, b2_ref, a_ref, xh2_ref, rs2_ref, x2_ref, x2b_ref, acc_ref):
        f = pl.program_id(1)
        x1 = xh_ref[...] * g1_ref[...] + b1_ref[...]
        a = _dot(x1.astype(BF16), wu_ref[...], NN)
        a_ref[...] = a.astype(BF16)
        h = jnp.square(jnp.maximum(a, 0.0))
        part = _dot(h.astype(BF16), wd_ref[...], NN)

        @pl.when(f == 0)
        def _():
            acc_ref[...] = part

        @pl.when(f > 0)
        def _():
            acc_ref[...] += part

        @pl.when(f == nf - 1)
        def _():
            xh2, rs2 = _layer_norm_stats(alpha * x1 + acc_ref[...])
            xh2_ref[...] = xh2
            rs2_ref[...] = rs2
            x2 = xh2 * g2_ref[...] + b2_ref[...]
            x2_ref[...] = x2
            x2b_ref[...] = x2.astype(BF16)

    row = lambda i, f: (i, 0)
    fix = lambda i, f: (0, 0)
    return pl.pallas_call(
        body,
        name="mlp_fwd",
        grid=(T // tm, nf),
        in_specs=[pl.BlockSpec((tm, d), row), pl.BlockSpec((1, d), fix), pl.BlockSpec((1, d), fix),
                  pl.BlockSpec((d, tf), lambda i, f: (0, f)), pl.BlockSpec((tf, d), lambda i, f: (f, 0)),
                  pl.BlockSpec((1, d), fix), pl.BlockSpec((1, d), fix)],
        out_specs=[pl.BlockSpec((tm, tf), lambda i, f: (i, f)), pl.BlockSpec((tm, d), row), pl.BlockSpec((tm, 1), row),
                   pl.BlockSpec((tm, d), row), pl.BlockSpec((tm, d), row)],
        out_shape=[jax.ShapeDtypeStruct((T, ff), BF16), jax.ShapeDtypeStruct((T, d), F32), jax.ShapeDtypeStruct((T, 1), F32),
                   jax.ShapeDtypeStruct((T, d), F32), jax.ShapeDtypeStruct((T, d), BF16)],
        scratch_shapes=[pltpu.VMEM((tm, d), F32)],
        compiler_params=_cparams(("parallel", "arbitrary")),
    )(xhat1, g1, b1, wu, wd, g2, b2)


def _ln_bwd(dy, xhat, rstd, g, *, tm=512, deps=()):
    T, d = dy.shape
    tm = _pick(T, (tm, 256, 128, 8))

    def body(dy_ref, xh_ref, rs_ref, g_ref, *rest):
        dz_ref, dzb_ref, dg_ref, db_ref = rest[len(deps):]

        @pl.when(pl.program_id(0) == 0)
        def _():
            dg_ref[...] = jnp.zeros_like(dg_ref)
            db_ref[...] = jnp.zeros_like(db_ref)

        dy_, xh = dy_ref[...], xh_ref[...]
        dg_ref[0:1, :] += jnp.sum(dy_ * xh, axis=0, keepdims=True)
        db_ref[0:1, :] += jnp.sum(dy_, axis=0, keepdims=True)
        dxh = dy_ * g_ref[...]
        dz = rs_ref[...] * (dxh - jnp.mean(dxh, axis=-1, keepdims=True) - xh * jnp.mean(dxh * xh, axis=-1, keepdims=True))
        dz_ref[...] = dz
        dzb_ref[...] = dz.astype(BF16)

    row = lambda i: (i, 0)
    fix = lambda i: (0, 0)
    return pl.pallas_call(
        body,
        name="ln_bwd",
        grid=(T // tm,),
        in_specs=[pl.BlockSpec((tm, d), row), pl.BlockSpec((tm, d), row), pl.BlockSpec((tm, 1), row), pl.BlockSpec((1, d), fix)]
        + [ANY_SPEC] * len(deps),
        out_specs=[pl.BlockSpec((tm, d), row), pl.BlockSpec((tm, d), row), pl.BlockSpec((8, d), fix), pl.BlockSpec((8, d), fix)],
        out_shape=[jax.ShapeDtypeStruct((T, d), F32), jax.ShapeDtypeStruct((T, d), BF16), jax.ShapeDtypeStruct((8, d), F32),
                   jax.ShapeDtypeStruct((8, d), F32)],
        compiler_params=_cparams(("arbitrary",)),
    )(dy, xhat, rstd, g, *deps)


def _loss_head(y, target, *, tm=512):
    T, d = y.shape
    tm = _pick(T, (tm, 256, 128, 8))
    n = T // tm

    def body(y_ref, t_ref, loss_ref, dy_ref, acc_ref):
        i = pl.program_id(0)

        @pl.when(i == 0)
        def _():
            acc_ref[...] = jnp.zeros_like(acc_ref)

        e = y_ref[...] - t_ref[...]
        dy_ref[...] = e * (1.0 / d)
        acc_ref[...] += jnp.sum(e * e, axis=0, keepdims=True)

        @pl.when(i == n - 1)
        def _():
            loss_ref[...] = (0.5 / d) * jnp.sum(acc_ref[...], axis=1, keepdims=True)

    row = lambda i: (i, 0)
    return pl.pallas_call(
        body,
        name="loss_head",
        grid=(n,),
        in_specs=[pl.BlockSpec((tm, d), row), pl.BlockSpec((tm, d), row)],
        out_specs=[pl.BlockSpec((1, 1), lambda i: (0, 0)), pl.BlockSpec((tm, d), row)],
        out_shape=[jax.ShapeDtypeStruct((1, 1), F32), jax.ShapeDtypeStruct((T, d), F32)],
        scratch_shapes=[pltpu.VMEM((1, d), F32)],
        compiler_params=_cparams(("arbitrary",)),
    )(y, target)


def _lower_bounds_fwd(lower_bounds):
    depth, n = lower_bounds.shape

    def body(x_ref, soft_ref, lb_ref):
        x = x_ref[...]
        e = jnp.exp(x - jnp.max(x, axis=0, keepdims=True))
        soft_ref[...] = e / jnp.sum(e, axis=0, keepdims=True)
        run = None
        for l in range(depth):
            run = soft_ref[l:l + 1, :] if run is None else run + soft_ref[l:l + 1, :]
            lb_ref[l:l + 1, :] = run - soft_ref[0:1, :]

    return pl.pallas_call(body, name="lower_bounds_fwd",
                          out_shape=[jax.ShapeDtypeStruct((depth, n), F32), jax.ShapeDtypeStruct((depth, n), F32)])(lower_bounds)


def _lower_bounds_bwd(soft, dlb):
    depth, n = soft.shape

    def body(soft_ref, dlb_ref, out_ref, dsoft_ref):
        total = jnp.sum(dlb_ref[...], axis=0, keepdims=True)
        run = None
        for l in reversed(range(depth)):
            run = dlb_ref[l:l + 1, :] if run is None else run + dlb_ref[l:l + 1, :]
            dsoft_ref[l:l + 1, :] = run - total if l == 0 else run
        s, ds = soft_ref[...], dsoft_ref[...]
        out_ref[...] = s * (ds - jnp.sum(s * ds, axis=0, keepdims=True))

    return pl.pallas_call(body, name="lower_bounds_bwd", out_shape=jax.ShapeDtypeStruct((depth, n), F32),
                          scratch_shapes=[pltpu.VMEM((depth, n), F32)])(soft, dlb)


def _layer_fwd(x0, x0b, mem2, lb, w_in, rest_fn, *, bl, seq, alpha, deps=()):
    p = _matmul("proj_in", x0b, w_in, mode="nn", deps=deps, tm=1024, tn=1792)
    wts = dict(rest_fn(p), w_in=w_in)
    mk = _matmul("mem_k", mem2, wts["w_mem_k"], mode="nn", out_dtype=BF16)
    mv = _matmul("mem_v", mem2, wts["w_mem_v"], mode="nn", out_dtype=BF16)
    y, st, opre = _mixer_fwd(p, mk, mv, lb, wts["conv_w"], wts["hg_norm_w"], bl=bl, seq=seq)
    r, merged, xhat1, rstd1, x1b = _merge_fwd(y, p, x0, wts["w_branch"], wts["w_o"], wts["b_gate"], wts["ln1_g"], wts["ln1_b"],
                                              alpha=alpha)
    a, xhat2, rstd2, x2, x2b = _mlp_fwd(xhat1, wts["ln1_g"], wts["ln1_b"], wts["w_up"], wts["w_down"], wts["ln2_g"], wts["ln2_b"],
                                        alpha=alpha)
    saved = dict(x0b=x0b, p=p, mk=mk, mv=mv, y=y, st=st, opre=opre, r=r, merged=merged, xhat1=xhat1, rstd1=rstd1, x1b=x1b, a=a,
                 xhat2=xhat2, rstd2=rstd2)
    return x2, x2b, saved, wts


def _relu2_bf16(a):
    return jnp.square(jnp.maximum(a.astype(F32), 0.0)).astype(BF16)


def _mlp_bwd(dx2, sv, wts, *, alpha, deps=()):
    g = {}
    dz2, dz2b, dg2, db2 = _ln_bwd(dx2, sv["xhat2"], sv["rstd2"], wts["ln2_g"], deps=deps)
    g["ln2_g"], g["ln2_b"] = dg2[0:1], db2[0:1]
    da = _matmul("mlp_da", dz2b, wts["w_down"], mode="nt", out_dtype=BF16, tm=1024,
                 epi_fn=lambda acc, a: (acc * (2.0 * jnp.maximum(a.astype(F32), 0.0)),), epi_extra=(sv["a"],))
    g["w_down"] = _matmul_tn("grad_w_down", sv["a"], dz2b, a_fn=_relu2_bf16, out_dtype=BF16, tt=2048)
    g["w_up"] = _matmul_tn("grad_w_up", sv["x1b"], da, out_dtype=BF16, tt=2048)
    dx1 = _matmul("mlp_dx", da, wts["w_up"], mode="nt", epi_fn=lambda acc, dz: (acc + alpha * dz,), epi_extra=(dz2,), tm=1024)
    return dx1, g


def _mix_bwd(dx1, sv, mem2, lb, wts, *, bl, seq, alpha, send, deps=()):
    d = dx1.shape[1]
    g = {}
    dz1, dz1b, dg1, db1 = _ln_bwd(dx1, sv["xhat1"], sv["rstd1"], wts["ln1_g"], deps=deps)
    g["ln1_g"], g["ln1_b"] = dg1[0:1], db1[0:1]
    g["w_o"] = _matmul_tn("grad_w_o", sv["merged"], dz1b, out_dtype=BF16, tt=2048)
    dr, dp, dy, dbg = _merge_bwd(dz1b, sv["p"], sv["r"], wts["w_branch"], wts["w_o"], wts["b_gate"])
    g["b_gate"] = dbg[0:1]
    g["w_branch"] = jnp.concatenate(
        [_matmul_tn("grad_w_branch", sv["y"], dr, a_cols=(i * W, W), b_cols=(i * d, d), out_dtype=BF16) for i in range(N_BRANCH)],
        axis=0)
    token = send(("w_o", "w_branch"), g)
    dp, dmk, dmv, dcw, dnw, dlb = _mixer_bwd(sv["p"], dy, dp, sv["st"], sv["opre"], sv["mk"], sv["mv"], lb,
                                              wts["conv_w"], wts["hg_norm_w"], bl=bl, seq=seq, deps=(token,))
    g["conv_w"], g["hg_norm_w"], g["lb"] = dcw[0:CONV_K], dnw[0:1], dlb[0:1]
    g["w_mem_k"] = _matmul_tn("grad_w_mem_k", mem2, dmk, out_dtype=BF16)
    g["w_mem_v"] = _matmul_tn("grad_w_mem_v", mem2, dmv, out_dtype=BF16)
    g["w_in"] = _matmul_tn("grad_w_in", sv["x0b"], dp, out_dtype=BF16, tt=2048)
    token = send(("w_in", "w_mem_k", "w_mem_v", "conv_w"), g)
    dx0 = _matmul("proj_in_dx", dp, wts["w_in"], mode="nt", epi_fn=lambda acc, dz: (acc + alpha * dz,), epi_extra=(dz1,),
                  tm=1024, deps=(token,))
    return dx0, g


N_CHIPS = 4
MESH_IDS = pl.DeviceIdType.MESH


def _axis_slice(ref, axis, start, size):
    idx = [slice(None)] * len(ref.shape)
    idx[axis] = pl.ds(start, size)
    return ref.at[tuple(idx)]


def _chip_exchange(name, items):
    n = len(items)
    out_shapes, meta = [], []
    for arr, kind, axis in items:
        shp = list(arr.shape)
        if kind == "gather":
            per = shp[axis]
            shp[axis] = per * N_CHIPS
            out_shapes.append(jax.ShapeDtypeStruct(tuple(shp), arr.dtype))
        elif kind == "scatter":
            per = shp[axis] // N_CHIPS
            shp[axis] = per
            out_shapes.append(jax.ShapeDtypeStruct((N_CHIPS, *shp), arr.dtype))
        else:
            per = None
            out_shapes.append(jax.ShapeDtypeStruct((N_CHIPS, *shp), arr.dtype))
        meta.append((kind, axis, per))

    def body(*refs):
        ins, outs = refs[:n], refs[n:2 * n]
        send_sems, recv_sems, local_sems = refs[2 * n:]
        x, y, c = lax.axis_index("x"), lax.axis_index("y"), lax.axis_index("c")
        me = 2 * x + y
        peers = [(1 - x, y), (x, 1 - y), (1 - x, 1 - y)]

        def src_for(t, chip):
            kind, axis, per = meta[t]
            return _axis_slice(ins[t], axis, chip * per, per) if kind == "scatter" else ins[t]

        def dst_from(t, chip):
            kind, axis, per = meta[t]
            return _axis_slice(outs[t], axis, chip * per, per) if kind == "gather" else outs[t].at[chip]

        def remote(t, k):
            px, py = peers[k]
            return pltpu.make_async_remote_copy(
                src_ref=src_for(t, 2 * px + py), dst_ref=dst_from(t, me), send_sem=send_sems.at[t * 3 + k],
                recv_sem=recv_sems.at[t * 3 + k], device_id=(px, py, c), device_id_type=MESH_IDS)

        def arrival(t, k):
            px, py = peers[k]
            return pltpu.make_async_remote_copy(
                src_ref=src_for(t, me), dst_ref=dst_from(t, 2 * px + py), send_sem=send_sems.at[t * 3 + k],
                recv_sem=recv_sems.at[t * 3 + k], device_id=(px, py, c), device_id_type=MESH_IDS)

        sends = [remote(t, k) for t in range(n) for k in range(3)]
        for cp in sends:
            cp.start()
        own = [pltpu.make_async_copy(src_for(t, me), dst_from(t, me), local_sems.at[t]) for t in range(n)]
        for cp in own:
            cp.start()
        for t in range(n):
            for k in range(3):
                arrival(t, k).wait_recv()
        for cp in sends:
            cp.wait_send()
        for cp in own:
            cp.wait()

    any_spec = pl.BlockSpec(memory_space=pl.ANY)
    return pl.pallas_call(
        body,
        name=name,
        in_specs=[any_spec] * n,
        out_specs=[any_spec] * n,
        out_shape=out_shapes,
        scratch_shapes=[pltpu.SemaphoreType.DMA((3 * n,)), pltpu.SemaphoreType.DMA((3 * n,)), pltpu.SemaphoreType.DMA((n,))],
        compiler_params=pltpu.CompilerParams(has_side_effects=True),
    )(*[a for a, _, _ in items])


HBM_SPEC = pl.BlockSpec(memory_space=pltpu.HBM)
SEM_SPEC = pl.BlockSpec(memory_space=pltpu.SEMAPHORE)
N_PEERS = N_CHIPS - 1


def _my_chip():
    return (2 * lax.axis_index("x") + lax.axis_index("y")).astype(jnp.int32).reshape(1)


def _own_block_spec(r, c, axis, tr):
    if axis == 1:
        return pl.BlockSpec((tr, c), lambda i, me: (i, me[0]))
    return pl.BlockSpec((tr, c), lambda i, me: (me[0] * (r // tr) + i, 0))


def _place_shard(name, shard, axis, me):
    r, c = shard.shape
    tr = _row_block(r, c, shard.dtype.itemsize)
    shp = (r, c * N_CHIPS) if axis == 1 else (r * N_CHIPS, c)

    def body(me_ref, s_ref, o_ref):
        del me_ref
        o_ref[...] = s_ref[...]

    return pl.pallas_call(
        body, name=name,
        grid_spec=pltpu.PrefetchScalarGridSpec(
            num_scalar_prefetch=1, grid=(r // tr,),
            in_specs=[pl.BlockSpec((tr, c), lambda i, me: (i, 0))], out_specs=_own_block_spec(r, c, axis, tr)),
        out_shape=jax.ShapeDtypeStruct(shp, shard.dtype),
        compiler_params=_cparams(("parallel",)),
    )(me, shard)


class _Split:
    def __init__(self, name, items):
        self.name, self.n = name, len(items)
        self.srcs = [a for a, _, _ in items]
        self.meta, self.land_shapes = [], []
        for arr, kind, axis in items:
            shp = list(arr.shape)
            if kind == "gather":
                per = shp[axis]
                shp[axis] = per * N_CHIPS
                self.land_shapes.append(jax.ShapeDtypeStruct(tuple(shp), arr.dtype))
            else:
                per = shp[axis] // N_CHIPS
                shp[axis] = per
                self.land_shapes.append(jax.ShapeDtypeStruct((N_PEERS, *shp), arr.dtype))
            self.meta.append((kind, axis, per))

    def _src(self, ins, t, chip):
        kind, axis, per = self.meta[t]
        return _axis_slice(ins[t], axis, chip * per, per) if kind == "scatter" else ins[t]

    def _dst(self, lands, t, chip, slot):
        kind, axis, per = self.meta[t]
        return _axis_slice(lands[t], axis, chip * per, per) if kind == "gather" else lands[t].at[slot]

    def landing_zones(self, me):
        return [_place_shard(self.name + "_own", src, axis, me) if kind == "gather" else lax.empty(ls.shape, ls.dtype)
                for src, ls, (kind, axis, _) in zip(self.srcs, self.land_shapes, self.meta)]

    def _copies(self, ins, lands, send_sems, recv_sems, arrivals):
        x, y, c = lax.axis_index("x"), lax.axis_index("y"), lax.axis_index("c")
        me = 2 * x + y
        peers = [(1 - x, y), (x, 1 - y), (1 - x, 1 - y)]
        res = []
        for t in range(self.n):
            for k, (px, py) in enumerate(peers):
                theirs = 2 * px + py
                sems = dict(send_sem=send_sems.at[t * N_PEERS + k], recv_sem=recv_sems.at[t * N_PEERS + k],
                            device_id=(px, py, c), device_id_type=MESH_IDS)
                if arrivals:
                    res.append(pltpu.make_async_remote_copy(src_ref=self._src(ins, t, me), dst_ref=self._dst(lands, t, theirs, k), **sems))
                else:
                    res.append(pltpu.make_async_remote_copy(src_ref=self._src(ins, t, theirs), dst_ref=self._dst(lands, t, me, k), **sems))
        return res

    def start(self, lands):
        n = self.n

        def body(*refs):
            ins, lnd = refs[:n], refs[n:2 * n]
            send_sems, recv_sems = refs[2 * n], refs[2 * n + 1]
            token = refs[-1]
            for cp in self._copies(ins, lnd, send_sems, recv_sems, arrivals=False):
                cp.start()
            token[...] = jnp.zeros_like(token)

        hbm = lambda a: pltpu.HBM(a.shape, a.dtype)
        res = pl.pallas_call(
            body, name=self.name + "_start",
            in_specs=[HBM_SPEC] * (2 * n),
            out_specs=[SEM_SPEC, SEM_SPEC] + [HBM_SPEC] * (2 * n) + [pl.BlockSpec(memory_space=pltpu.VMEM)],
            out_shape=[pltpu.SemaphoreType.DMA((N_PEERS * n,)), pltpu.SemaphoreType.DMA((N_PEERS * n,))]
            + [hbm(a) for a in self.srcs] + [hbm(a) for a in self.land_shapes] + [jax.ShapeDtypeStruct((8, 128), F32)],
            input_output_aliases={i: 2 + i for i in range(2 * n)},
            compiler_params=pltpu.CompilerParams(has_side_effects=pltpu.SideEffectType.DATAFLOW_SIDE_EFFECTING),
        )(*[pltpu.with_memory_space_constraint(a, pltpu.HBM) for a in self.srcs],
          *[pltpu.with_memory_space_constraint(a, pltpu.HBM) for a in lands])
        return res[:-1], res[-1]

    def wait(self, state, after):
        n = self.n
        send_sems, recv_sems = state[0], state[1]
        srcs, lands = state[2:2 + n], state[2 + n:2 + 2 * n]

        def body(*refs):
            ins, lnd = refs[:n], refs[n:2 * n]
            s_sems, r_sems = refs[2 * n], refs[2 * n + 1]
            for cp in self._copies(ins, lnd, s_sems, r_sems, arrivals=True):
                cp.wait_recv()
            for cp in self._copies(ins, lnd, s_sems, r_sems, arrivals=False):
                cp.wait_send()

        hbm = lambda a: pltpu.HBM(a.shape, a.dtype)
        res = pl.pallas_call(
            body, name=self.name + "_wait",
            in_specs=[HBM_SPEC] * (2 * n) + [SEM_SPEC, SEM_SPEC, ANY_SPEC],
            out_specs=[HBM_SPEC] * (2 * n),
            out_shape=[hbm(a) for a in self.srcs] + [hbm(a) for a in self.land_shapes],
            input_output_aliases={i: i for i in range(2 * n)},
            compiler_params=pltpu.CompilerParams(has_side_effects=pltpu.SideEffectType.DATAFLOW_SIDE_EFFECTING),
        )(*srcs, *lands, send_sems, recv_sems, after)
        return res[:n], res[n:]


def _sibling_swap(name, arrays):
    n = len(arrays)

    def body(*refs):
        ins, outs = refs[:n], refs[n:2 * n]
        send_sems, recv_sems = refs[2 * n:]
        sibling = (lax.axis_index("x"), lax.axis_index("y"), 1 - lax.axis_index("c"))
        copies = [pltpu.make_async_remote_copy(src_ref=ins[t], dst_ref=outs[t], send_sem=send_sems.at[t], recv_sem=recv_sems.at[t],
                                               device_id=sibling, device_id_type=MESH_IDS) for t in range(n)]
        for cp in copies:
            cp.start()
        for cp in copies:
            cp.wait()

    any_spec = pl.BlockSpec(memory_space=pl.ANY)
    return pl.pallas_call(
        body,
        name=name,
        in_specs=[any_spec] * n,
        out_specs=[any_spec] * n,
        out_shape=[jax.ShapeDtypeStruct(a.shape, a.dtype) for a in arrays],
        scratch_shapes=[pltpu.SemaphoreType.DMA((n,)), pltpu.SemaphoreType.DMA((n,))],
        compiler_params=pltpu.CompilerParams(has_side_effects=True),
    )(*arrays)


def _row_block(r, c, itemsize=4, target=1 << 20):
    if r % 8 != 0:
        return r
    best = 8
    for tr in range(8, r + 1, 8):
        if r % tr == 0 and tr * c * itemsize <= target:
            best = tr
    return best


def _sum_chips_into(parts, stacked, layer):
    _, r, c = parts.shape
    tr = _row_block(r, c)

    def body(p_ref, s_ref, o_ref):
        del s_ref
        o_ref[...] = ((p_ref[0] + p_ref[1]) + p_ref[2]) + p_ref[3]

    return pl.pallas_call(
        body,
        name="sum_chips",
        grid=(r // tr,),
        in_specs=[pl.BlockSpec((N_CHIPS, tr, c), lambda i: (0, i, 0)), pl.BlockSpec(memory_space=pl.ANY)],
        out_specs=pl.BlockSpec((None, tr, c), lambda i: (layer, i, 0)),
        out_shape=jax.ShapeDtypeStruct(stacked.shape, stacked.dtype),
        input_output_aliases={1: 0},
        compiler_params=_cparams(("parallel",)),
    )(parts, stacked)


def _sum_own_and_peers(me, g, axis, landed):
    _, r, c = landed.shape
    tr = _row_block(r, c)

    def body(me_ref, g_ref, p_ref, o_ref):
        del me_ref
        o_ref[...] = ((g_ref[...].astype(F32) + p_ref[0].astype(F32)) + p_ref[1].astype(F32)) + p_ref[2].astype(F32)

    return pl.pallas_call(
        body, name="sum_chips_own",
        grid_spec=pltpu.PrefetchScalarGridSpec(
            num_scalar_prefetch=1, grid=(r // tr,),
            in_specs=[_own_block_spec(r, c, axis, tr), pl.BlockSpec((N_PEERS, tr, c), lambda i, me: (0, i, 0))],
            out_specs=pl.BlockSpec((tr, c), lambda i, me: (i, 0))),
        out_shape=jax.ShapeDtypeStruct((r, c), F32),
        compiler_params=_cparams(("parallel",)),
    )(me, g, landed)


def _adamw_math(w, m, v, g):
    m_new = ADAM_B1 * m + (1.0 - ADAM_B1) * g
    v_new = ADAM_B2 * v + (1.0 - ADAM_B2) * jnp.square(g)
    m_hat = m_new / (1.0 - ADAM_B1 ** ADAM_STEP)
    v_hat = v_new / (1.0 - ADAM_B2 ** ADAM_STEP)
    return -ADAM_LR * (m_hat / (jnp.sqrt(v_hat) + ADAM_EPS) + ADAM_WD * w), m_new, v_new


def _adamw(w, m, v, g_a, g_b):
    L, r, c = w.shape
    tr = _row_block(r, c, target=1 << 19)

    def body(w_ref, m_ref, v_ref, ga_ref, gb_ref, g_ref, d_ref, nm_ref, nv_ref):
        g = ga_ref[...] + gb_ref[...]
        g_ref[...] = g
        d_ref[...], nm_ref[...], nv_ref[...] = _adamw_math(w_ref[...], m_ref[...], v_ref[...], g)

    spec = pl.BlockSpec((None, tr, c), lambda l, i: (l, i, 0))
    return pl.pallas_call(
        body,
        name="adamw",
        grid=(L, r // tr),
        in_specs=[spec] * 5,
        out_specs=[spec] * 4,
        out_shape=[jax.ShapeDtypeStruct(w.shape, F32)] * 4,
        compiler_params=_cparams(("parallel", "parallel")),
    )(w, m, v, g_a, g_b)


def _adamw_layer(w, m, v, g_a, g_b, layer, outs):
    L, r, c = w.shape
    tr = _row_block(r, c, target=1 << 19)
    n_prev = 0 if outs is None else 4

    def body(w_ref, m_ref, v_ref, ga_ref, gb_ref, *rest):
        g_ref, d_ref, nm_ref, nv_ref = rest[n_prev:]
        g = ga_ref[...] + gb_ref[...]
        g_ref[...] = g
        d_ref[...], nm_ref[...], nv_ref[...] = _adamw_math(w_ref[...], m_ref[...], v_ref[...], g)

    at_layer = pl.BlockSpec((None, tr, c), lambda i: (layer, i, 0))
    flat = pl.BlockSpec((tr, c), lambda i: (i, 0))
    return pl.pallas_call(
        body,
        name="adamw_layer",
        grid=(r // tr,),
        in_specs=[at_layer] * 3 + [flat] * 2 + [ANY_SPEC] * n_prev,
        out_specs=[at_layer] * 4,
        out_shape=[jax.ShapeDtypeStruct(w.shape, F32)] * 4,
        input_output_aliases={5 + k: k for k in range(n_prev)},
        compiler_params=_cparams(("parallel",)),
    )(w, m, v, g_a, g_b, *(outs or ()))


SHARDED = (("w_in", 1), ("conv_w", 1), ("w_mem_k", 0), ("w_mem_v", 0), ("w_branch", 1), ("w_o", 0), ("w_up", 1), ("w_down", 0))
SMALL = ("lower_bounds", "hg_norm_w", "b_gate", "ln1_g", "ln1_b", "ln2_g", "ln2_b")
WEIGHT_ORDER = ("lower_bounds", "w_in", "conv_w", "hg_norm_w", "w_mem_k", "w_mem_v", "w_branch", "b_gate", "w_o", "ln1_g", "ln1_b",
                "w_up", "w_down", "ln2_g", "ln2_b")


def kernel(x, mem, lower_bounds, w_in, conv_w, hg_norm_w, w_mem_k, w_mem_v, w_branch, b_gate, w_o, ln1_g, ln1_b, w_up, w_down, ln2_g, ln2_b, loss_target, m_lower_bounds, m_w_in, m_conv_w, m_hg_norm_w, m_w_mem_k, m_w_mem_v, m_w_branch, m_b_gate, m_w_o, m_ln1_g, m_ln1_b, m_w_up, m_w_down, m_ln2_g, m_ln2_b, v_lower_bounds, v_w_in, v_conv_w, v_hg_norm_w, v_w_mem_k, v_w_mem_v, v_w_branch, v_b_gate, v_w_o, v_ln1_g, v_ln1_b, v_w_up, v_w_down, v_ln2_g, v_ln2_b):
    bl, seq, d = x.shape
    depth = w_in.shape[0]
    weights = dict(lower_bounds=lower_bounds, w_in=w_in, conv_w=conv_w, hg_norm_w=hg_norm_w, w_mem_k=w_mem_k, w_mem_v=w_mem_v,
                   w_branch=w_branch, b_gate=b_gate, w_o=w_o, ln1_g=ln1_g, ln1_b=ln1_b, w_up=w_up, w_down=w_down, ln2_g=ln2_g, ln2_b=ln2_b)
    mom_m = dict(lower_bounds=m_lower_bounds, w_in=m_w_in, conv_w=m_conv_w, hg_norm_w=m_hg_norm_w, w_mem_k=m_w_mem_k, w_mem_v=m_w_mem_v,
                 w_branch=m_w_branch, b_gate=m_b_gate, w_o=m_w_o, ln1_g=m_ln1_g, ln1_b=m_ln1_b, w_up=m_w_up, w_down=m_w_down,
                 ln2_g=m_ln2_g, ln2_b=m_ln2_b)
    mom_v = dict(lower_bounds=v_lower_bounds, w_in=v_w_in, conv_w=v_conv_w, hg_norm_w=v_hg_norm_w, w_mem_k=v_w_mem_k, w_mem_v=v_w_mem_v,
                 w_branch=v_w_branch, b_gate=v_b_gate, w_o=v_w_o, ln1_g=v_ln1_g, ln1_b=v_ln1_b, w_up=v_w_up, w_down=v_w_down,
                 ln2_g=v_ln2_g, ln2_b=v_ln2_b)

    def shard2d(name, l):
        w = weights[name][l]
        if name == "w_branch":
            return w.reshape(N_BRANCH * W, w.shape[-1]).astype(BF16)
        return w if name == "conv_w" else w.astype(BF16)

    me = _my_chip()

    shard_axis = dict(SHARDED)

    def start_exchange(name, kind, items):
        ex = _Split(name, [(arr, kind, shard_axis[nm]) for nm, arr in items])
        state, token = ex.start(ex.landing_zones(me))
        return ex, state, [nm for nm, _ in items], token

    def start_gathers(l):
        first = start_exchange(f"gather_in_l{l}", "gather", [("w_in", shard2d("w_in", l))])
        rest = start_exchange(f"gather_rest_l{l}", "gather", [(nm, shard2d(nm, l)) for nm, _ in SHARDED if nm != "w_in"])
        return first, rest

    def gathered(pend, after):
        ex, state, names, _ = pend
        return dict(zip(names, ex.wait(state, after=after)[1]))

    x2d, mem2, t2d = x.reshape(bl * seq, d), mem.reshape(-1, d), loss_target.reshape(bl * seq, d)
    alpha = (2.0 * depth) ** 0.25
    soft, lb_all = _lower_bounds_fwd(lower_bounds)

    h, hb, saved, layer_wts = x2d, x2d.astype(BF16), [], []
    pending = start_gathers(0)
    for l in range(depth):
        first, rest = pending
        w_in_l = gathered(first, h)["w_in"]

        def rest_fn(after, l=l, rest=rest):
            wts = gathered(rest, after)
            for name in ("hg_norm_w", "b_gate", "ln1_g", "ln1_b", "ln2_g", "ln2_b"):
                wts[name] = weights[name][l][None, :]
            return wts

        deps = (rest[3],)
        if l + 1 < depth:
            pending = start_gathers(l + 1)
            deps += (pending[0][3], pending[1][3])
        h, hb, sv, wts = _layer_fwd(h, hb, mem2, lb_all[l:l + 1], w_in_l, rest_fn, bl=bl, seq=seq, alpha=alpha, deps=deps)
        saved.append(sv)
        layer_wts.append(wts)
    loss, dh = _loss_head(h, t2d)

    shape3 = {name: (depth, weights[name].size // (depth * weights[name].shape[-1]), weights[name].shape[-1]) for name, _ in SHARDED}
    partial = [dict() for _ in range(depth)]
    smalls = [None] * depth
    outs = {name: None for name, _ in SHARDED}

    def finish_reduce(pend, l, after):
        ex, state, names, _ = pend
        sent, got = ex.wait(state, after=after)
        for nm, g_full, landed in zip(names, sent, got):
            partial[l][nm] = _sum_own_and_peers(me, g_full, shard_axis[nm], landed)

    def optimizer_step(l):
        names = [name for name, _ in SHARDED]
        theirs = _sibling_swap(f"swap_partials_l{l}", [partial[l][nm] for nm in names])
        for nm, other in zip(names, theirs):
            outs[nm] = _adamw_layer(weights[nm].reshape(shape3[nm]), mom_m[nm].reshape(shape3[nm]), mom_v[nm].reshape(shape3[nm]),
                                    partial[l][nm], other, l, outs[nm])
        return tuple(outs[nm][0] for nm in names)

    pending_mix, deps = [], ()
    for l in reversed(range(depth)):
        dx1, g_mlp = _mlp_bwd(dh, saved[l], layer_wts[l], alpha=alpha, deps=deps)
        pending_mlp = start_exchange(f"reduce_mlp_l{l}", "scatter", [(nm, g_mlp[nm]) for nm in ("w_up", "w_down")])
        deps = (pending_mlp[3],)
        if pending_mix:
            for pend in pending_mix:
                finish_reduce(pend, l + 1, dx1)
            deps += optimizer_step(l + 1)
        pending_mix = []

        def send(names, g, l=l, pending_mix=pending_mix):
            pend = start_exchange(f"reduce_{names[0]}_l{l}", "scatter", [(nm, g[nm]) for nm in names])
            pending_mix.append(pend)
            return pend[3]

        dh, g = _mix_bwd(dx1, saved[l], mem2, lb_all[l:l + 1], layer_wts[l], bl=bl, seq=seq, alpha=alpha, send=send, deps=deps)
        finish_reduce(pending_mlp, l, dh)
        deps = ()
        g.update(g_mlp, lower_bounds=g["lb"])
        smalls[l] = jnp.concatenate([g[nm] for nm in SMALL], axis=1)
    small_parts = _chip_exchange("reduce_small", [(jnp.stack(smalls), "bcast", 0)])[0]
    small_sum = _sum_chips_into(small_parts.reshape(N_CHIPS, depth, -1), jnp.zeros((1, depth, small_parts.shape[-1]), F32), 0)
    small_sum = small_sum.reshape(depth, 1, -1)
    small_theirs = _sibling_swap("swap_small", [small_sum])[0]
    for pend in pending_mix:
        finish_reduce(pend, 0, small_theirs)
    optimizer_step(0)

    outs = {name: [r.reshape(weights[name].shape) for r in res] for name, res in outs.items()}
    off = 0
    for name in SMALL:
        n = weights[name].shape[1]
        mine, other = small_sum[:, :, off:off + n], small_theirs[:, :, off:off + n]
        off += n
        if name == "lower_bounds":
            mine = _lower_bounds_bwd(soft, mine[:, 0, :])[:, None, :]
            other = _lower_bounds_bwd(soft, other[:, 0, :])[:, None, :]
        shp = (depth, 1, n)
        res = _adamw(weights[name].reshape(shp), mom_m[name].reshape(shp), mom_v[name].reshape(shp), mine, other)
        outs[name] = [r.reshape(weights[name].shape) for r in res]
    assert off == small_sum.shape[-1]

    total_loss = lax.psum(loss[0, 0], ("x", "y", "c"))
    result = [total_loss, dh.reshape(bl, seq, d)]
    for k in range(4):
        result += [outs[name][k] for name in WEIGHT_ORDER]
    return tuple(result)
```

```python
import functools

import jax
import jax.numpy as jnp
from jax import lax
from jax.experimental import pallas as pl
from jax.experimental.pallas import tpu as pltpu

F32 = jnp.float32
BF16 = jnp.bfloat16

HG_HEADS = 4
HG_F = 128
HG_CHUNK = 32
MEM_HEADS = 4
MEM_HEAD_DIM = 128
BRANCH_WIDTH = 512
N_BRANCH = 3
CONV_K = 3
LN_EPS = 1e-5
RMS_EPS = 1e-6
ADAM_LR = 0.001
ADAM_B1 = 0.9
ADAM_B2 = 0.999
ADAM_EPS = 1e-08
ADAM_WD = 0.01
ADAM_STEP = 10

VMEM_LIMIT = 48 * 1024 * 1024


def _cparams(sem):
    return pltpu.CompilerParams(dimension_semantics=sem, vmem_limit_bytes=VMEM_LIMIT)


def _dot(a, b, dims):
    return lax.dot_general(a, b, (dims, ((), ())), preferred_element_type=F32)


NN = ((1,), (0,))
NT = ((1,), (1,))
TN = ((0,), (0,))


def _pick(n, pref):
    for t in pref:
        if n % t == 0:
            return t
    return n


ANY_SPEC = pl.BlockSpec(memory_space=pl.ANY)


def _matmul(name, a, b, *, mode, out_dtype=F32, a_fn=None, a_extra=(), epi_fn=None, epi_extra=(), n_out=1,
            tm=512, tn=1024, tk=1024, deps=()):
    M, K = a.shape
    N = b.shape[1] if mode == "nn" else b.shape[0]
    tm, tn, tk = _pick(M, (tm, 256, 128, 8)), _pick(N, (tn, 896, 512, 256, 128)), _pick(K, (tk, 512, 256, 128))
    nk = K // tk
    n_ax, n_ex = len(a_extra), len(epi_extra)
    n_in = 2 + n_ax + n_ex + len(deps)
    out_dtypes = out_dtype if isinstance(out_dtype, (tuple, list)) else (out_dtype,) * n_out

    def body(*refs):
        a_ref, b_ref = refs[0], refs[1]
        ax_refs = refs[2:2 + n_ax]
        ex_refs = refs[2 + n_ax:2 + n_ax + n_ex]
        o_refs = refs[n_in:n_in + n_out]
        at = a_ref[...]
        at = a_fn(at, *[r[...] for r in ax_refs]) if a_fn is not None else at.astype(BF16)
        part = _dot(at, b_ref[...].astype(BF16), NN if mode == "nn" else NT)

        def finish(acc):
            outs = epi_fn(acc, *[r[...] for r in ex_refs]) if epi_fn is not None else (acc,)
            for o_ref, o in zip(o_refs, outs):
                o_ref[...] = o.astype(o_ref.dtype)

        if nk == 1:
            finish(part)
            return
        acc_ref = refs[-1]
        k = pl.program_id(2)

        @pl.when(k == 0)
        def _():
            acc_ref[...] = part

        @pl.when(jnp.logical_and(k > 0, k < nk - 1))
        def _():
            acc_ref[...] += part

        @pl.when(k == nk - 1)
        def _():
            finish(acc_ref[...] + part)

    in_specs = [pl.BlockSpec((tm, tk), lambda j, i, k: (i, k)),
                pl.BlockSpec((tk, tn), lambda j, i, k: (k, j)) if mode == "nn" else pl.BlockSpec((tn, tk), lambda j, i, k: (j, k))]
    in_specs += [pl.BlockSpec((1, tk), lambda j, i, k: (0, k)) for _ in a_extra]
    for e in epi_extra:
        if e.shape[0] == 1:
            in_specs.append(pl.BlockSpec((1, tn), lambda j, i, k: (0, j)))
        else:
            in_specs.append(pl.BlockSpec((tm, tn), lambda j, i, k: (i, j)))
    in_specs += [ANY_SPEC] * len(deps)
    out = pl.pallas_call(
        body,
        name=name,
        grid=(N // tn, M // tm, nk),
        in_specs=in_specs,
        out_specs=[pl.BlockSpec((tm, tn), lambda j, i, k: (i, j)) for _ in range(n_out)],
        out_shape=[jax.ShapeDtypeStruct((M, N), dt) for dt in out_dtypes],
        scratch_shapes=[pltpu.VMEM((tm, tn), F32)] if nk > 1 else [],
        compiler_params=_cparams(("parallel", "parallel", "arbitrary")),
    )(a, b, *a_extra, *epi_extra, *deps)
    return out[0] if n_out == 1 else out


def _matmul_tn(name, a, b, *, a_fn=None, a_extra=(), a_cols=None, b_cols=None, ta=1024, tb=1024, tt=1024, out_dtype=F32, deps=()):
    T = a.shape[0]
    a0, Ka = a_cols if a_cols is not None else (0, a.shape[1])
    b0, Nb = b_cols if b_cols is not None else (0, b.shape[1])
    ta, tb, tt = _pick(Ka, (ta, 512, 256, 128)), _pick(Nb, (tb, 896, 512, 256, 128)), _pick(T, (tt, 512, 256, 128))
    assert a0 % ta == 0 and b0 % tb == 0
    a0, b0 = a0 // ta, b0 // tb
    nt = T // tt
    n_ax = len(a_extra)

    def body(*refs):
        a_ref, b_ref = refs[0], refs[1]
        ax_refs = refs[2:2 + n_ax]
        o_ref = refs[2 + n_ax + len(deps)]
        acc_ref = refs[-1]
        t = pl.program_id(2)
        at = a_ref[...]
        at = a_fn(at, *[r[...] for r in ax_refs]) if a_fn is not None else at.astype(BF16)
        part = _dot(at, b_ref[...].astype(BF16), TN)

        @pl.when(t == 0)
        def _():
            acc_ref[...] = part

        @pl.when(jnp.logical_and(t > 0, t < nt - 1))
        def _():
            acc_ref[...] += part

        @pl.when(t == nt - 1)
        def _():
            o_ref[...] = (acc_ref[...] + part if nt > 1 else part).astype(o_ref.dtype)

    in_specs = [pl.BlockSpec((tt, ta), lambda i, j, t: (t, a0 + i)), pl.BlockSpec((tt, tb), lambda i, j, t: (t, b0 + j))]
    in_specs += [pl.BlockSpec((1, ta), lambda i, j, t: (0, a0 + i)) for _ in a_extra]
    in_specs += [ANY_SPEC] * len(deps)
    return pl.pallas_call(
        body,
        name=name,
        grid=(Ka // ta, Nb // tb, nt),
        in_specs=in_specs,
        out_specs=pl.BlockSpec((ta, tb), lambda i, j, t: (i, j)),
        out_shape=jax.ShapeDtypeStruct((Ka, Nb), out_dtype),
        scratch_shapes=[pltpu.VMEM((ta, tb), F32)],
        compiler_params=_cparams(("parallel", "parallel", "arbitrary")),
    )(a, b, *a_extra, *deps)


W = BRANCH_WIDTH
C_CB, C_CC, C_CH, C_HQ, C_HF, C_HI, C_HG, C_MQ, N_MIX = 0, W, 2 * W, 3 * W, 4 * W, 5 * W, 6 * W, 7 * W, 8 * W
TS_MIX = 256
PREV_ROWS = 16


def _sigmoid(x):
    return jax.nn.sigmoid(x)


def _chunk_pos(shape):
    return lax.broadcasted_iota(jnp.int32, shape, 0) & (HG_CHUNK - 1)


def _seg_cumsum(x, pos):
    sh = 1
    while sh < HG_CHUNK:
        x = x + jnp.where(pos >= sh, pltpu.roll(x, sh, 0), 0.0)
        sh *= 2
    return x


def _seg_rev_cumsum(x, pos):
    n = x.shape[0]
    sh = 1
    while sh < HG_CHUNK:
        x = x + jnp.where(pos < HG_CHUNK - sh, pltpu.roll(x, n - sh, 0), 0.0)
        sh *= 2
    return x


def _chunk_mask(ts):
    r = lax.broadcasted_iota(jnp.int32, (ts, ts), 0)
    c = lax.broadcasted_iota(jnp.int32, (ts, ts), 1)
    return jnp.logical_and((r // HG_CHUNK) == (c // HG_CHUNK), c <= r)


def _hgrn_gates(p_ref, lb):
    q = p_ref[:, C_HQ:C_HQ + W].astype(F32)
    fl = p_ref[:, C_HF:C_HF + W].astype(F32)
    sig = _sigmoid(fl)
    f = lb + (1.0 - lb) * sig
    logf = jnp.log(f)
    k = (1.0 - lb) * _sigmoid(-fl)
    sq = _sigmoid(q)
    qs = q * sq
    return q, sq, qs, sig, f, logf, k


def _hgrn_decays(logf, bc_sc, ts):
    pos = _chunk_pos(logf.shape)
    bc = _seg_cumsum(logf, pos)
    bc_sc[...] = bc
    nc = ts // HG_CHUNK
    bref = jnp.concatenate(
        [jnp.broadcast_to(bc_sc[n * HG_CHUNK + HG_CHUNK // 2 - 1:n * HG_CHUNK + HG_CHUNK // 2, :], (HG_CHUNK, W)) for n in range(nc)], axis=0)
    blast = jnp.concatenate(
        [jnp.broadcast_to(bc_sc[(n + 1) * HG_CHUNK - 1:(n + 1) * HG_CHUNK, :], (HG_CHUNK, W)) for n in range(nc)], axis=0)
    return pos, bc, bref, blast


def _conv_shift_down(u, carry_ref, row):
    n = carry_ref.shape[0]
    last, before = carry_ref[n - 1:n, :], carry_ref[n - 2:n - 1, :]
    u1 = jnp.where(row == 0, last, pltpu.roll(u, 1, 0))
    u2 = jnp.where(row == 0, before, jnp.where(row == 1, last, pltpu.roll(u, 2, 0)))
    return u1, u2


def _attn_probs(qh, kh):
    s = _dot(qh, kh, NT) * (MEM_HEAD_DIM ** -0.5)
    e = jnp.exp(s - jnp.max(s, axis=-1, keepdims=True))
    return e / jnp.sum(e, axis=-1, keepdims=True)


def _mixer_fwd(p, mk, mv, lb, conv_w, norm_w, *, bl, seq):
    T = p.shape[0]
    ts = TS_MIX
    ns = seq // ts
    nc = ts // HG_CHUNK
    ml = mk.shape[0] // bl

    def body(p_ref, mk_ref, mv_ref, lb_ref, cw_ref, nw_ref, y_ref, st_ref, opre_ref, state_sc, carry_sc, bc_sc):
        @pl.when(pl.program_id(1) == 0)
        def _():
            state_sc[...] = jnp.zeros_like(state_sc)
            carry_sc[...] = jnp.zeros_like(carry_sc)

        cb, cc, ch = (p_ref[:, c0:c0 + W].astype(F32) for c0 in (C_CB, C_CC, C_CH))
        u = cc * ch
        row = lax.broadcasted_iota(jnp.int32, (ts, W), 0)
        u1, u2 = _conv_shift_down(u, carry_sc, row)
        yconv = u2 * cw_ref[0:1, :] + u1 * cw_ref[1:2, :] + u * cw_ref[2:3, :]
        y_ref[:, 0:W] = (cb * yconv).astype(BF16)
        carry_sc[...] = u[ts - 8:ts, :]

        lbv = lb_ref[...]
        _, _, qs, _, _, logf, k = _hgrn_gates(p_ref, lbv)
        pos, bc, bref, blast = _hgrn_decays(logf, bc_sc, ts)
        a_all = (qs * jnp.exp(bc - bref)).astype(BF16)
        bk_all = (k * jnp.exp(bref - bc)).astype(BF16)
        qin_all = (qs * jnp.exp(bc)).astype(BF16)
        kout_all = (k * jnp.exp(blast - bc)).astype(BF16)
        v_all = p_ref[:, C_HI:C_HI + W].astype(BF16)
        mask = _chunk_mask(ts)
        for h in range(HG_HEADS):
            hs = slice(h * HG_F, (h + 1) * HG_F)
            vb = v_all[:, hs]
            scores = jnp.where(mask, _dot(a_all[:, hs], bk_all[:, hs], NT), 0.0)
            o_intra = _dot(scores.astype(BF16), vb, NN)
            st = state_sc[h]
            o_inter = []
            for n in range(nc):
                rows = slice(n * HG_CHUNK, (n + 1) * HG_CHUNK)
                st_ref[n, h] = st
                o_inter.append(_dot(qin_all[rows, hs], st.astype(BF16), NT))
                kv = _dot(vb[rows], kout_all[rows, hs], TN)
                decay = jnp.exp(bc_sc[(n + 1) * HG_CHUNK - 1:(n + 1) * HG_CHUNK, hs])
                st = st * decay + kv
            state_sc[h] = st
            o = o_intra + jnp.concatenate(o_inter, axis=0)
            opre_ref[:, hs] = o
            on = o * lax.rsqrt(jnp.mean(o * o, axis=-1, keepdims=True) + RMS_EPS) * nw_ref[...]
            g = p_ref[:, C_HG + h * HG_F:C_HG + (h + 1) * HG_F].astype(F32)
            y_ref[:, W + h * HG_F:W + (h + 1) * HG_F] = (on * (g * _sigmoid(g))).astype(BF16)

        for h in range(MEM_HEADS):
            hs = slice(h * MEM_HEAD_DIM, (h + 1) * MEM_HEAD_DIM)
            qh = p_ref[:, C_MQ + h * MEM_HEAD_DIM:C_MQ + (h + 1) * MEM_HEAD_DIM].astype(BF16)
            prob = _attn_probs(qh, mk_ref[:, hs])
            y_ref[:, 2 * W + h * MEM_HEAD_DIM:2 * W + (h + 1) * MEM_HEAD_DIM] = _dot(prob.astype(BF16), mv_ref[:, hs], NN).astype(BF16)

    return pl.pallas_call(
        body,
        name="mixer_fwd",
        grid=(bl, ns),
        in_specs=[
            pl.BlockSpec((ts, N_MIX), lambda b, s: (b * ns + s, 0)),
            pl.BlockSpec((ml, W), lambda b, s: (b, 0)),
            pl.BlockSpec((ml, W), lambda b, s: (b, 0)),
            pl.BlockSpec((1, W), lambda b, s: (0, 0)),
            pl.BlockSpec((CONV_K, W), lambda b, s: (0, 0)),
            pl.BlockSpec((1, HG_F), lambda b, s: (0, 0)),
        ],
        out_specs=[
            pl.BlockSpec((ts, 3 * W), lambda b, s: (b * ns + s, 0)),
            pl.BlockSpec((nc, HG_HEADS, HG_F, HG_F), lambda b, s: (b * ns + s, 0, 0, 0)),
            pl.BlockSpec((ts, W), lambda b, s: (b * ns + s, 0)),
        ],
        out_shape=[
            jax.ShapeDtypeStruct((T, 3 * W), BF16),
            jax.ShapeDtypeStruct((T // HG_CHUNK, HG_HEADS, HG_F, HG_F), F32),
            jax.ShapeDtypeStruct((T, W), F32),
        ],
        scratch_shapes=[pltpu.VMEM((HG_HEADS, HG_F, HG_F), F32), pltpu.VMEM((8, W), F32), pltpu.VMEM((ts, W), F32)],
        compiler_params=_cparams(("arbitrary", "arbitrary")),
    )(p, mk, mv, lb, conv_w, norm_w)


def _mixer_bwd(p, dy, dp_gates, st, opre, mk, mv, lb, conv_w, norm_w, *, bl, seq, deps=()):
    T, nin = p.shape
    ts = TS_MIX
    ns = seq // ts
    nc = ts // HG_CHUNK
    ml = mk.shape[0] // bl
    mid, last = HG_CHUNK // 2 - 1, HG_CHUNK - 1

    def body(p_ref, pprev_ref, dy_ref, dpin_ref, st_ref, opre_ref, mk_ref, mv_ref, lb_ref, cw_ref, nw_ref, *rest):
        dp_ref, dmk_ref, dmv_ref, dcw_ref, dnw_ref, dlb_ref, dstate_sc, carry_sc, uprev_sc, bc_sc = rest[len(deps):]
        del dpin_ref
        b, s = pl.program_id(0), pl.program_id(1)

        @pl.when(s == 0)
        def _():
            dstate_sc[...] = jnp.zeros_like(dstate_sc)
            carry_sc[...] = jnp.zeros_like(carry_sc)
            dmk_ref[...] = jnp.zeros_like(dmk_ref)
            dmv_ref[...] = jnp.zeros_like(dmv_ref)

        @pl.when(jnp.logical_and(b == 0, s == 0))
        def _():
            dcw_ref[...] = jnp.zeros_like(dcw_ref)
            dnw_ref[...] = jnp.zeros_like(dnw_ref)
            dlb_ref[...] = jnp.zeros_like(dlb_ref)

        cb, cc, ch = (p_ref[:, c0:c0 + W].astype(F32) for c0 in (C_CB, C_CC, C_CH))
        u = cc * ch
        row = lax.broadcasted_iota(jnp.int32, (ts, W), 0)
        uprev = pprev_ref[:, C_CC:C_CC + W].astype(F32) * pprev_ref[:, C_CH:C_CH + W].astype(F32)
        uprev_sc[...] = jnp.where(s == ns - 1, 0.0, uprev)
        u1, u2 = _conv_shift_down(u, uprev_sc, row)
        w0, w1, w2 = cw_ref[0:1, :], cw_ref[1:2, :], cw_ref[2:3, :]
        dya = dy_ref[:, 0:W].astype(F32)
        dp_ref[:, C_CB:C_CB + W] = (dya * (u2 * w0 + u1 * w1 + u * w2)).astype(BF16)
        dv = cb * dya
        dv1 = jnp.where(row == ts - 1, carry_sc[0:1, :], pltpu.roll(dv, ts - 1, 0))
        dv2 = jnp.where(row == ts - 1, carry_sc[1:2, :], jnp.where(row == ts - 2, carry_sc[0:1, :], pltpu.roll(dv, ts - 2, 0)))
        du = dv * w2 + dv1 * w1 + dv2 * w0
        dp_ref[:, C_CC:C_CC + W] = (du * ch).astype(BF16)
        dp_ref[:, C_CH:C_CH + W] = (du * cc).astype(BF16)
        dcw_ref[0:1, :] += jnp.sum(dv * u2, axis=0, keepdims=True)
        dcw_ref[1:2, :] += jnp.sum(dv * u1, axis=0, keepdims=True)
        dcw_ref[2:3, :] += jnp.sum(dv * u, axis=0, keepdims=True)
        carry_sc[...] = dv[0:8, :]

        lbv = lb_ref[...]
        q_all, sq_all, qs_all, sig_all, f_all, logf, k_all = _hgrn_gates(p_ref, lbv)
        pos_all, bc, bref, blast = _hgrn_decays(logf, bc_sc, ts)
        ea_all, eb_all, eq_all, ek_all = jnp.exp(bc - bref), jnp.exp(bref - bc), jnp.exp(bc), jnp.exp(blast - bc)
        mask = _chunk_mask(ts)
        pos = _chunk_pos((ts, HG_F))
        pos_c = _chunk_pos((HG_CHUNK, HG_F))
        nw = nw_ref[...]
        for h in range(HG_HEADS):
            hs = slice(h * HG_F, (h + 1) * HG_F)
            qs, k, ea, eb, eq, ek = qs_all[:, hs], k_all[:, hs], ea_all[:, hs], eb_all[:, hs], eq_all[:, hs], ek_all[:, hs]
            a, bk, qin, kout = qs * ea, k * eb, qs * eq, k * ek
            o = opre_ref[:, hs]
            g = p_ref[:, C_HG + h * HG_F:C_HG + (h + 1) * HG_F].astype(F32)
            sg = _sigmoid(g)
            r = lax.rsqrt(jnp.mean(o * o, axis=-1, keepdims=True) + RMS_EPS)
            dyb = dy_ref[:, W + h * HG_F:W + (h + 1) * HG_F].astype(F32)
            dp_ref[:, C_HG + h * HG_F:C_HG + (h + 1) * HG_F] = (dyb * (o * r * nw) * (sg * (1.0 + g * (1.0 - sg)))).astype(BF16)
            don = dyb * (g * sg)
            dnw_ref[0:1, :] += jnp.sum(don * o * r, axis=0, keepdims=True)
            dn = don * nw
            do = r * (dn - o * (r * r) * jnp.mean(dn * o, axis=-1, keepdims=True))
            dob = do.astype(BF16)
            vb = p_ref[:, C_HI + h * HG_F:C_HI + (h + 1) * HG_F].astype(BF16)
            ab, bkb = a.astype(BF16), bk.astype(BF16)
            scores = jnp.where(mask, _dot(ab, bkb, NT), 0.0)
            dscores = jnp.where(mask, _dot(dob, vb, NT), 0.0).astype(BF16)
            dv_h = _dot(scores.astype(BF16), dob, TN)
            da = _dot(dscores, bkb, NN)
            dbk = _dot(dscores, ab, TN)
            koutb, qinb = kout.astype(BF16), qin.astype(BF16)
            dst = dstate_sc[h]
            dqin_p, dkout_p, dvi_p, ddec_p = [None] * nc, [None] * nc, [None] * nc, [None] * nc
            for n in reversed(range(nc)):
                rows = slice(n * HG_CHUNK, (n + 1) * HG_CHUNK)
                st_n = st_ref[n, h]
                decay = jnp.exp(bc_sc[n * HG_CHUNK + last:n * HG_CHUNK + last + 1, hs])
                dstb = dst.astype(BF16)
                dvi_p[n] = _dot(koutb[rows], dstb, NT)
                dkout_p[n] = _dot(vb[rows], dstb, NN)
                ddec_p[n] = jnp.sum(dst * st_n, axis=0, keepdims=True) * decay
                dqin_p[n] = _dot(dob[rows], st_n.astype(BF16), NN)
                dst = dst * decay + _dot(dob[rows], qinb[rows], TN)
            dstate_sc[h] = dst
            dqin = jnp.concatenate(dqin_p, axis=0)
            dkout = jnp.concatenate(dkout_p, axis=0)
            dp_ref[:, C_HI + h * HG_F:C_HI + (h + 1) * HG_F] = (dv_h + jnp.concatenate(dvi_p, axis=0)).astype(BF16)
            dqs = da * ea + dqin * eq
            dk = dbk * eb + dkout * ek
            t_a, t_b, t_q, t_k = da * a, dbk * bk, dqin * qin, dkout * kout
            dbc = t_a - t_b + t_q - t_k
            t_ref = t_b - t_a
            pieces = []
            for n in range(nc):
                rows = slice(n * HG_CHUNK, (n + 1) * HG_CHUNK)
                s_ref = jnp.sum(t_ref[rows], axis=0, keepdims=True)
                s_last = jnp.sum(t_k[rows], axis=0, keepdims=True) + ddec_p[n]
                pieces.append(dbc[rows] + jnp.where(pos_c == mid, s_ref, 0.0) + jnp.where(pos_c == last, s_last, 0.0))
            dlogf = _seg_rev_cumsum(jnp.concatenate(pieces, axis=0), pos)
            sig, lbh = sig_all[:, hs], lbv[:, hs]
            dfk = dlogf / f_all[:, hs] - dk
            dp_ref[:, C_HF + h * HG_F:C_HF + (h + 1) * HG_F] = (dfk * (1.0 - lbh) * sig * (1.0 - sig)).astype(BF16)
            dlb_ref[0:1, hs] += jnp.sum(dfk * (1.0 - sig), axis=0, keepdims=True)
            q, sq = q_all[:, hs], sq_all[:, hs]
            dp_ref[:, C_HQ + h * HG_F:C_HQ + (h + 1) * HG_F] = (dqs * (sq * (1.0 + q * (1.0 - sq)))).astype(BF16)

        for h in range(MEM_HEADS):
            hs = slice(h * MEM_HEAD_DIM, (h + 1) * MEM_HEAD_DIM)
            qh = p_ref[:, C_MQ + h * MEM_HEAD_DIM:C_MQ + (h + 1) * MEM_HEAD_DIM].astype(BF16)
            kh, vh = mk_ref[:, hs], mv_ref[:, hs]
            prob = _attn_probs(qh, kh)
            dob = dy_ref[:, 2 * W + h * MEM_HEAD_DIM:2 * W + (h + 1) * MEM_HEAD_DIM].astype(BF16)
            dmv_ref[:, hs] += _dot(prob.astype(BF16), dob, TN)
            dprob = _dot(dob, vh, NT)
            ds = prob * (dprob - jnp.sum(dprob * prob, axis=-1, keepdims=True)) * (MEM_HEAD_DIM ** -0.5)
            dsb = ds.astype(BF16)
            dp_ref[:, C_MQ + h * MEM_HEAD_DIM:C_MQ + (h + 1) * MEM_HEAD_DIM] = _dot(dsb, kh, NN).astype(BF16)
            dmk_ref[:, hs] += _dot(dsb, qh, TN)

    def tile(b, s):
        return b * ns + (ns - 1 - s)

    return pl.pallas_call(
        body,
        name="mixer_bwd",
        grid=(bl, ns),
        in_specs=[
            pl.BlockSpec((ts, N_MIX), lambda b, s: (tile(b, s), 0)),
            pl.BlockSpec((PREV_ROWS, N_MIX), lambda b, s: (jnp.maximum(tile(b, s) * (ts // PREV_ROWS) - 1, 0), 0)),
            pl.BlockSpec((ts, 3 * W), lambda b, s: (tile(b, s), 0)),
            pl.BlockSpec(memory_space=pl.ANY),
            pl.BlockSpec((nc, HG_HEADS, HG_F, HG_F), lambda b, s: (tile(b, s), 0, 0, 0)),
            pl.BlockSpec((ts, W), lambda b, s: (tile(b, s), 0)),
            pl.BlockSpec((ml, W), lambda b, s: (b, 0)),
            pl.BlockSpec((ml, W), lambda b, s: (b, 0)),
            pl.BlockSpec((1, W), lambda b, s: (0, 0)),
            pl.BlockSpec((CONV_K, W), lambda b, s: (0, 0)),
            pl.BlockSpec((1, HG_F), lambda b, s: (0, 0)),
        ] + [ANY_SPEC] * len(deps),
        out_specs=[
            pl.BlockSpec((ts, N_MIX), lambda b, s: (tile(b, s), 0)),
            pl.BlockSpec((ml, W), lambda b, s: (b, 0)),
            pl.BlockSpec((ml, W), lambda b, s: (b, 0)),
            pl.BlockSpec((8, W), lambda b, s: (0, 0)),
            pl.BlockSpec((8, HG_F), lambda b, s: (0, 0)),
            pl.BlockSpec((8, W), lambda b, s: (0, 0)),
        ],
        out_shape=[
            jax.ShapeDtypeStruct((T, nin), BF16),
            jax.ShapeDtypeStruct((bl * ml, W), F32),
            jax.ShapeDtypeStruct((bl * ml, W), F32),
            jax.ShapeDtypeStruct((8, W), F32),
            jax.ShapeDtypeStruct((8, HG_F), F32),
            jax.ShapeDtypeStruct((8, W), F32),
        ],
        input_output_aliases={3: 0},
        scratch_shapes=[pltpu.VMEM((HG_HEADS, HG_F, HG_F), F32), pltpu.VMEM((8, W), F32), pltpu.VMEM((PREV_ROWS, W), F32),
                        pltpu.VMEM((ts, W), F32)],
        compiler_params=_cparams(("arbitrary", "arbitrary")),
    )(p, p, dy, dp_gates, st, opre, mk, mv, lb, conv_w, norm_w, *deps)


def _layer_norm_stats(z):
    mu = jnp.mean(z, axis=-1, keepdims=True)
    zc = z - mu
    rstd = lax.rsqrt(jnp.mean(zc * zc, axis=-1, keepdims=True) + LN_EPS)
    return zc * rstd, rstd


def _gate_specs(tm, d):
    g0 = N_MIX // d
    return [pl.BlockSpec((tm, d), functools.partial(lambda i, k: (i, g0 + k), k=k)) for k in range(N_BRANCH)]


def _merge_fwd(y, p, x0, wb, wo, bg, ln_g, ln_b, *, alpha, tm=256):
    T, d = x0.shape
    assert N_MIX % d == 0
    tm = _pick(T, (tm, 128, 8))

    def body(y_ref, g0_ref, g1_ref, g2_ref, x_ref, wb_ref, wo_ref, bg_ref, lg_ref, lb_ref, r_ref, mg_ref, xh_ref, rs_ref, x1b_ref):
        merged = None
        for i, g_ref in enumerate((g0_ref, g1_ref, g2_ref)):
            r = _dot(y_ref[:, i * W:(i + 1) * W], wb_ref[i * W:(i + 1) * W, :], NN)
            r_ref[:, i * d:(i + 1) * d] = r.astype(BF16)
            t = _sigmoid(g_ref[...].astype(F32) + bg_ref[:, i * d:(i + 1) * d]) * r
            merged = t if merged is None else merged + t
        mb = merged.astype(BF16)
        mg_ref[...] = mb
        z = alpha * x_ref[...] + _dot(mb, wo_ref[...], NN)
        xh, rs = _layer_norm_stats(z)
        xh_ref[...], rs_ref[...] = xh, rs
        x1b_ref[...] = (xh * lg_ref[...] + lb_ref[...]).astype(BF16)

    row = lambda i: (i, 0)
    fix = lambda i: (0, 0)
    return pl.pallas_call(
        body,
        name="merge_fwd",
        grid=(T // tm,),
        in_specs=[pl.BlockSpec((tm, 3 * W), row)] + _gate_specs(tm, d) + [
            pl.BlockSpec((tm, d), row), pl.BlockSpec((3 * W, d), fix), pl.BlockSpec((d, d), fix), pl.BlockSpec((1, 3 * d), fix),
            pl.BlockSpec((1, d), fix), pl.BlockSpec((1, d), fix)],
        out_specs=[pl.BlockSpec((tm, 3 * d), row), pl.BlockSpec((tm, d), row), pl.BlockSpec((tm, d), row), pl.BlockSpec((tm, 1), row),
                   pl.BlockSpec((tm, d), row)],
        out_shape=[jax.ShapeDtypeStruct((T, 3 * d), BF16), jax.ShapeDtypeStruct((T, d), BF16),
                   jax.ShapeDtypeStruct((T, d), F32), jax.ShapeDtypeStruct((T, 1), F32), jax.ShapeDtypeStruct((T, d), BF16)],
        compiler_params=_cparams(("parallel",)),
    )(y, p, p, p, x0, wb, wo, bg, ln_g, ln_b)


def _merge_bwd(dz, p, r, wb, wo, bg, *, tm=256):
    T, d = dz.shape
    nin = p.shape[1]
    tm = _pick(T, (tm, 128, 8))

    def body(dz_ref, g0_ref, g1_ref, g2_ref, r_ref, wb_ref, wo_ref, bg_ref, dr_ref, dp_ref, dy_ref, dbg_ref):
        @pl.when(pl.program_id(0) == 0)
        def _():
            dbg_ref[...] = jnp.zeros_like(dbg_ref)

        dmerged = _dot(dz_ref[...].astype(BF16), wo_ref[...], NT)
        dp_ref[:, 0:N_MIX] = jnp.zeros((tm, N_MIX), BF16)
        for i, g_ref in enumerate((g0_ref, g1_ref, g2_ref)):
            cs = slice(i * d, (i + 1) * d)
            s = _sigmoid(g_ref[...].astype(F32) + bg_ref[:, cs])
            drb = (dmerged * s).astype(BF16)
            dr_ref[:, cs] = drb
            dgate = dmerged * r_ref[:, cs].astype(F32) * s * (1.0 - s)
            dp_ref[:, N_MIX + i * d:N_MIX + (i + 1) * d] = dgate.astype(BF16)
            dbg_ref[0:1, cs] += jnp.sum(dgate, axis=0, keepdims=True)
            dy_ref[:, i * W:(i + 1) * W] = _dot(drb, wb_ref[i * W:(i + 1) * W, :], NT).astype(BF16)

    row = lambda i: (i, 0)
    fix = lambda i: (0, 0)
    return pl.pallas_call(
        body,
        name="merge_bwd",
        grid=(T // tm,),
        in_specs=[pl.BlockSpec((tm, d), row)] + _gate_specs(tm, d) + [
            pl.BlockSpec((tm, 3 * d), row), pl.BlockSpec((3 * W, d), fix), pl.BlockSpec((d, d), fix), pl.BlockSpec((1, 3 * d), fix)],
        out_specs=[pl.BlockSpec((tm, 3 * d), row), pl.BlockSpec((tm, nin), row), pl.BlockSpec((tm, 3 * W), row),
                   pl.BlockSpec((8, 3 * d), fix)],
        out_shape=[jax.ShapeDtypeStruct((T, 3 * d), BF16), jax.ShapeDtypeStruct((T, nin), BF16),
                   jax.ShapeDtypeStruct((T, 3 * W), BF16), jax.ShapeDtypeStruct((8, 3 * d), F32)],
        compiler_params=_cparams(("arbitrary",)),
    )(dz, p, p, p, r, wb, wo, bg)


def _mlp_fwd(xhat1, g1, b1, wu, wd, g2, b2, *, alpha, tm=512, tf=1024):
    T, d = xhat1.shape
    ff = wu.shape[1]
    tm, tf = _pick(T, (tm, 256, 128, 8)), _pick(ff, (tf, 512, 256, 128))
    nf = ff // tf

    def body(xh_ref, g1_ref, b1_ref, wu_ref, wd_ref, g2_ref, b2_ref, a_ref, xh2_ref, rs2_ref, x2_ref, x2b_ref, acc_ref):
        f = pl.program_id(1)
        x1 = xh_ref[...] * g1_ref[...] + b1_ref[...]
        a = _dot(x1.astype(BF16), wu_ref[...], NN)
        a_ref[...] = a.astype(BF16)
        h = jnp.square(jnp.maximum(a, 0.0))
        part = _dot(h.astype(BF16), wd_ref[...], NN)

        @pl.when(f == 0)
        def _():
            acc_ref[...] = part

        @pl.when(f > 0)
        def _():
            acc_ref[...] += part

        @pl.when(f == nf - 1)
        def _():
            xh2, rs2 = _layer_norm_stats(alpha * x1 + acc_ref[...])
            xh2_ref[...] = xh2
            rs2_ref[...] = rs2
            x2 = xh2 * g2_ref[...] + b2_ref[...]
            x2_ref[...] = x2
            x2b_ref[...] = x2.astype(BF16)

    row = lambda i, f: (i, 0)
    fix = lambda i, f: (0, 0)
    return pl.pallas_call(
        body,
        name="mlp_fwd",
        grid=(T // tm, nf),
        in_specs=[pl.BlockSpec((tm, d), row), pl.BlockSpec((1, d), fix), pl.BlockSpec((1, d), fix),
                  pl.BlockSpec((d, tf), lambda i, f: (0, f)), pl.BlockSpec((tf, d), lambda i, f: (f, 0)),
                  pl.BlockSpec((1, d), fix), pl.BlockSpec((1, d), fix)],
        out_specs=[pl.BlockSpec((tm, tf), lambda i, f: (i, f)), pl.BlockSpec((tm, d), row), pl.BlockSpec((tm, 1), row),
                   pl.BlockSpec((tm, d), row), pl.BlockSpec((tm, d), row)],
        out_shape=[jax.ShapeDtypeStruct((T, ff), BF16), jax.ShapeDtypeStruct((T, d), F32), jax.ShapeDtypeStruct((T, 1), F32),
                   jax.ShapeDtypeStruct((T, d), F32), jax.ShapeDtypeStruct((T, d), BF16)],
        scratch_shapes=[pltpu.VMEM((tm, d), F32)],
        compiler_params=_cparams(("parallel", "arbitrary")),
    )(xhat1, g1, b1, wu, wd, g2, b2)


def _ln_bwd(dy, xhat, rstd, g, *, tm=512, deps=()):
    T, d = dy.shape
    tm = _pick(T, (tm, 256, 128, 8))

    def body(dy_ref, xh_ref, rs_ref, g_ref, *rest):
        dz_ref, dzb_ref, dg_ref, db_ref = rest[len(deps):]

        @pl.when(pl.program_id(0) == 0)
        def _():
            dg_ref[...] = jnp.zeros_like(dg_ref)
            db_ref[...] = jnp.zeros_like(db_ref)

        dy_, xh = dy_ref[...], xh_ref[...]
        dg_ref[0:1, :] += jnp.sum(dy_ * xh, axis=0, keepdims=True)
        db_ref[0:1, :] += jnp.sum(dy_, axis=0, keepdims=True)
        dxh = dy_ * g_ref[...]
        dz = rs_ref[...] * (dxh - jnp.mean(dxh, axis=-1, keepdims=True) - xh * jnp.mean(dxh * xh, axis=-1, keepdims=True))
        dz_ref[...] = dz
        dzb_ref[...] = dz.astype(BF16)

    row = lambda i: (i, 0)
    fix = lambda i: (0, 0)
    return pl.pallas_call(
        body,
        name="ln_bwd",
        grid=(T // tm,),
        in_specs=[pl.BlockSpec((tm, d), row), pl.BlockSpec((tm, d), row), pl.BlockSpec((tm, 1), row), pl.BlockSpec((1, d), fix)]
        + [ANY_SPEC] * len(deps),
        out_specs=[pl.BlockSpec((tm, d), row), pl.BlockSpec((tm, d), row), pl.BlockSpec((8, d), fix), pl.BlockSpec((8, d), fix)],
        out_shape=[jax.ShapeDtypeStruct((T, d), F32), jax.ShapeDtypeStruct((T, d), BF16), jax.ShapeDtypeStruct((8, d), F32),
                   jax.ShapeDtypeStruct((8, d), F32)],
        compiler_params=_cparams(("arbitrary",)),
    )(dy, xhat, rstd, g, *deps)


def _loss_head(y, target, *, tm=512):
    T, d = y.shape
    tm = _pick(T, (tm, 256, 128, 8))
    n = T // tm

    def body(y_ref, t_ref, loss_ref, dy_ref, acc_ref):
        i = pl.program_id(0)

        @pl.when(i == 0)
        def _():
            acc_ref[...] = jnp.zeros_like(acc_ref)

        e = y_ref[...] - t_ref[...]
        dy_ref[...] = e * (1.0 / d)
        acc_ref[...] += jnp.sum(e * e, axis=0, keepdims=True)

        @pl.when(i == n - 1)
        def _():
            loss_ref[...] = (0.5 / d) * jnp.sum(acc_ref[...], axis=1, keepdims=True)

    row = lambda i: (i, 0)
    return pl.pallas_call(
        body,
        name="loss_head",
        grid=(n,),
        in_specs=[pl.BlockSpec((tm, d), row), pl.BlockSpec((tm, d), row)],
        out_specs=[pl.BlockSpec((1, 1), lambda i: (0, 0)), pl.BlockSpec((tm, d), row)],
        out_shape=[jax.ShapeDtypeStruct((1, 1), F32), jax.ShapeDtypeStruct((T, d), F32)],
        scratch_shapes=[pltpu.VMEM((1, d), F32)],
        compiler_params=_cparams(("arbitrary",)),
    )(y, target)


def _lower_bounds_fwd(lower_bounds):
    depth, n = lower_bounds.shape

    def body(x_ref, soft_ref, lb_ref):
        x = x_ref[...]
        e = jnp.exp(x - jnp.max(x, axis=0, keepdims=True))
        soft_ref[...] = e / jnp.sum(e, axis=0, keepdims=True)
        run = None
        for l in range(depth):
            run = soft_ref[l:l + 1, :] if run is None else run + soft_ref[l:l + 1, :]
            lb_ref[l:l + 1, :] = run - soft_ref[0:1, :]

    return pl.pallas_call(body, name="lower_bounds_fwd",
                          out_shape=[jax.ShapeDtypeStruct((depth, n), F32), jax.ShapeDtypeStruct((depth, n), F32)])(lower_bounds)


def _lower_bounds_bwd(soft, dlb):
    depth, n = soft.shape

    def body(soft_ref, dlb_ref, out_ref, dsoft_ref):
        total = jnp.sum(dlb_ref[...], axis=0, keepdims=True)
        run = None
        for l in reversed(range(depth)):
            run = dlb_ref[l:l + 1, :] if run is None else run + dlb_ref[l:l + 1, :]
            dsoft_ref[l:l + 1, :] = run - total if l == 0 else run
        s, ds = soft_ref[...], dsoft_ref[...]
        out_ref[...] = s * (ds - jnp.sum(s * ds, axis=0, keepdims=True))

    return pl.pallas_call(body, name="lower_bounds_bwd", out_shape=jax.ShapeDtypeStruct((depth, n), F32),
                          scratch_shapes=[pltpu.VMEM((depth, n), F32)])(soft, dlb)


def _layer_fwd(x0, x0b, mem2, lb, w_in, rest_fn, *, bl, seq, alpha, deps=()):
    p = _matmul("proj_in", x0b, w_in, mode="nn", out_dtype=BF16, deps=deps, tm=1024, tn=1792)
    wts = dict(rest_fn(p), w_in=w_in)
    mk = _matmul("mem_k", mem2, wts["w_mem_k"], mode="nn", out_dtype=BF16)
    mv = _matmul("mem_v", mem2, wts["w_mem_v"], mode="nn", out_dtype=BF16)
    y, st, opre = _mixer_fwd(p, mk, mv, lb, wts["conv_w"], wts["hg_norm_w"], bl=bl, seq=seq)
    r, merged, xhat1, rstd1, x1b = _merge_fwd(y, p, x0, wts["w_branch"], wts["w_o"], wts["b_gate"], wts["ln1_g"], wts["ln1_b"],
                                              alpha=alpha)
    a, xhat2, rstd2, x2, x2b = _mlp_fwd(xhat1, wts["ln1_g"], wts["ln1_b"], wts["w_up"], wts["w_down"], wts["ln2_g"], wts["ln2_b"],
                                        alpha=alpha)
    saved = dict(x0b=x0b, p=p, mk=mk, mv=mv, y=y, st=st, opre=opre, r=r, merged=merged, xhat1=xhat1, rstd1=rstd1, x1b=x1b, a=a,
                 xhat2=xhat2, rstd2=rstd2)
    return x2, x2b, saved, wts


def _relu2_bf16(a):
    return jnp.square(jnp.maximum(a.astype(F32), 0.0)).astype(BF16)


def _mlp_bwd(dx2, sv, wts, *, alpha, deps=()):
    g = {}
    dz2, dz2b, dg2, db2 = _ln_bwd(dx2, sv["xhat2"], sv["rstd2"], wts["ln2_g"], deps=deps)
    g["ln2_g"], g["ln2_b"] = dg2[0:1], db2[0:1]
    da = _matmul("mlp_da", dz2b, wts["w_down"], mode="nt", out_dtype=BF16, tm=1024,
                 epi_fn=lambda acc, a: (acc * (2.0 * jnp.maximum(a.astype(F32), 0.0)),), epi_extra=(sv["a"],))
    g["w_down"] = _matmul_tn("grad_w_down", sv["a"], dz2b, a_fn=_relu2_bf16, out_dtype=BF16, tt=2048)
    g["w_up"] = _matmul_tn("grad_w_up", sv["x1b"], da, out_dtype=BF16, tt=2048)
    dx1 = _matmul("mlp_dx", da, wts["w_up"], mode="nt", epi_fn=lambda acc, dz: (acc + alpha * dz,), epi_extra=(dz2,), tm=1024)
    return dx1, g


def _mix_bwd(dx1, sv, mem2, lb, wts, *, bl, seq, alpha, send, deps=()):
    d = dx1.shape[1]
    g = {}
    dz1, dz1b, dg1, db1 = _ln_bwd(dx1, sv["xhat1"], sv["rstd1"], wts["ln1_g"], deps=deps)
    g["ln1_g"], g["ln1_b"] = dg1[0:1], db1[0:1]
    g["w_o"] = _matmul_tn("grad_w_o", sv["merged"], dz1b, out_dtype=BF16, tt=2048)
    dr, dp, dy, dbg = _merge_bwd(dz1b, sv["p"], sv["r"], wts["w_branch"], wts["w_o"], wts["b_gate"])
    g["b_gate"] = dbg[0:1]
    g["w_branch"] = jnp.concatenate(
        [_matmul_tn("grad_w_branch", sv["y"], dr, a_cols=(i * W, W), b_cols=(i * d, d), out_dtype=BF16) for i in range(N_BRANCH)],
        axis=0)
    token = send(("w_o", "w_branch"), g)
    dp, dmk, dmv, dcw, dnw, dlb = _mixer_bwd(sv["p"], dy, dp, sv["st"], sv["opre"], sv["mk"], sv["mv"], lb,
                                              wts["conv_w"], wts["hg_norm_w"], bl=bl, seq=seq, deps=(token,))
    g["conv_w"], g["hg_norm_w"], g["lb"] = dcw[0:CONV_K], dnw[0:1], dlb[0:1]
    g["w_mem_k"] = _matmul_tn("grad_w_mem_k", mem2, dmk, out_dtype=BF16)
    g["w_mem_v"] = _matmul_tn("grad_w_mem_v", mem2, dmv, out_dtype=BF16)
    g["w_in"] = _matmul_tn("grad_w_in", sv["x0b"], dp, out_dtype=BF16, tt=2048)
    token = send(("w_in", "w_mem_k", "w_mem_v", "conv_w"), g)
    dx0 = _matmul("proj_in_dx", dp, wts["w_in"], mode="nt", epi_fn=lambda acc, dz: (acc + alpha * dz,), epi_extra=(dz1,),
                  tm=1024, tk=1792, deps=(token,))
    return dx0, g


N_CHIPS = 4
MESH_IDS = pl.DeviceIdType.MESH


def _axis_slice(ref, axis, start, size):
    idx = [slice(None)] * len(ref.shape)
    idx[axis] = pl.ds(start, size)
    return ref.at[tuple(idx)]


def _chip_exchange(name, items):
    n = len(items)
    out_shapes, meta = [], []
    for arr, kind, axis in items:
        shp = list(arr.shape)
        if kind == "gather":
            per = shp[axis]
            shp[axis] = per * N_CHIPS
            out_shapes.append(jax.ShapeDtypeStruct(tuple(shp), arr.dtype))
        elif kind == "scatter":
            per = shp[axis] // N_CHIPS
            shp[axis] = per
            out_shapes.append(jax.ShapeDtypeStruct((N_CHIPS, *shp), arr.dtype))
        else:
            per = None
            out_shapes.append(jax.ShapeDtypeStruct((N_CHIPS, *shp), arr.dtype))
        meta.append((kind, axis, per))

    def body(*refs):
        ins, outs = refs[:n], refs[n:2 * n]
        send_sems, recv_sems, local_sems = refs[2 * n:]
        x, y, c = lax.axis_index("x"), lax.axis_index("y"), lax.axis_index("c")
        me = 2 * x + y
        peers = [(1 - x, y), (x, 1 - y), (1 - x, 1 - y)]

        def src_for(t, chip):
            kind, axis, per = meta[t]
            return _axis_slice(ins[t], axis, chip * per, per) if kind == "scatter" else ins[t]

        def dst_from(t, chip):
            kind, axis, per = meta[t]
            return _axis_slice(outs[t], axis, chip * per, per) if kind == "gather" else outs[t].at[chip]

        def remote(t, k):
            px, py = peers[k]
            return pltpu.make_async_remote_copy(
                src_ref=src_for(t, 2 * px + py), dst_ref=dst_from(t, me), send_sem=send_sems.at[t * 3 + k],
                recv_sem=recv_sems.at[t * 3 + k], device_id=(px, py, c), device_id_type=MESH_IDS)

        def arrival(t, k):
            px, py = peers[k]
            return pltpu.make_async_remote_copy(
                src_ref=src_for(t, me), dst_ref=dst_from(t, 2 * px + py), send_sem=send_sems.at[t * 3 + k],
                recv_sem=recv_sems.at[t * 3 + k], device_id=(px, py, c), device_id_type=MESH_IDS)

        sends = [remote(t, k) for t in range(n) for k in range(3)]
        for cp in sends:
            cp.start()
        own = [pltpu.make_async_copy(src_for(t, me), dst_from(t, me), local_sems.at[t]) for t in range(n)]
        for cp in own:
            cp.start()
        for t in range(n):
            for k in range(3):
                arrival(t, k).wait_recv()
        for cp in sends:
            cp.wait_send()
        for cp in own:
            cp.wait()

    any_spec = pl.BlockSpec(memory_space=pl.ANY)
    return pl.pallas_call(
        body,
        name=name,
        in_specs=[any_spec] * n,
        out_specs=[any_spec] * n,
        out_shape=out_shapes,
        scratch_shapes=[pltpu.SemaphoreType.DMA((3 * n,)), pltpu.SemaphoreType.DMA((3 * n,)), pltpu.SemaphoreType.DMA((n,))],
        compiler_params=pltpu.CompilerParams(has_side_effects=True),
    )(*[a for a, _, _ in items])


HBM_SPEC = pl.BlockSpec(memory_space=pltpu.HBM)
SEM_SPEC = pl.BlockSpec(memory_space=pltpu.SEMAPHORE)
N_PEERS = N_CHIPS - 1


def _my_chip():
    return (2 * lax.axis_index("x") + lax.axis_index("y")).astype(jnp.int32).reshape(1)


def _own_block_spec(r, c, axis, tr):
    if axis == 1:
        return pl.BlockSpec((tr, c), lambda i, me: (i, me[0]))
    return pl.BlockSpec((tr, c), lambda i, me: (me[0] * (r // tr) + i, 0))


def _place_shard(name, shard, axis, me):
    r, c = shard.shape
    tr = _row_block(r, c, shard.dtype.itemsize)
    shp = (r, c * N_CHIPS) if axis == 1 else (r * N_CHIPS, c)

    def body(me_ref, s_ref, o_ref):
        del me_ref
        o_ref[...] = s_ref[...]

    return pl.pallas_call(
        body, name=name,
        grid_spec=pltpu.PrefetchScalarGridSpec(
            num_scalar_prefetch=1, grid=(r // tr,),
            in_specs=[pl.BlockSpec((tr, c), lambda i, me: (i, 0))], out_specs=_own_block_spec(r, c, axis, tr)),
        out_shape=jax.ShapeDtypeStruct(shp, shard.dtype),
        compiler_params=_cparams(("parallel",)),
    )(me, shard)


class _Split:
    def __init__(self, name, items):
        self.name, self.n = name, len(items)
        self.srcs = [a for a, _, _ in items]
        self.meta, self.land_shapes = [], []
        for arr, kind, axis in items:
            shp = list(arr.shape)
            if kind == "gather":
                per = shp[axis]
                shp[axis] = per * N_CHIPS
                self.land_shapes.append(jax.ShapeDtypeStruct(tuple(shp), arr.dtype))
            else:
                per = shp[axis] // N_CHIPS
                shp[axis] = per
                self.land_shapes.append(jax.ShapeDtypeStruct((N_PEERS, *shp), arr.dtype))
            self.meta.append((kind, axis, per))

    def _src(self, ins, t, chip):
        kind, axis, per = self.meta[t]
        return _axis_slice(ins[t], axis, chip * per, per) if kind == "scatter" else ins[t]

    def _dst(self, lands, t, chip, slot):
        kind, axis, per = self.meta[t]
        return _axis_slice(lands[t], axis, chip * per, per) if kind == "gather" else lands[t].at[slot]

    def landing_zones(self, me):
        return [_place_shard(self.name + "_own", src, axis, me) if kind == "gather" else lax.empty(ls.shape, ls.dtype)
                for src, ls, (kind, axis, _) in zip(self.srcs, self.land_shapes, self.meta)]

    def _copies(self, ins, lands, send_sems, recv_sems, arrivals):
        x, y, c = lax.axis_index("x"), lax.axis_index("y"), lax.axis_index("c")
        me = 2 * x + y
        peers = [(1 - x, y), (x, 1 - y), (1 - x, 1 - y)]
        res = []
        for t in range(self.n):
            for k, (px, py) in enumerate(peers):
                theirs = 2 * px + py
                sems = dict(send_sem=send_sems.at[t * N_PEERS + k], recv_sem=recv_sems.at[t * N_PEERS + k],
                            device_id=(px, py, c), device_id_type=MESH_IDS)
                if arrivals:
                    res.append(pltpu.make_async_remote_copy(src_ref=self._src(ins, t, me), dst_ref=self._dst(lands, t, theirs, k), **sems))
                else:
                    res.append(pltpu.make_async_remote_copy(src_ref=self._src(ins, t, theirs), dst_ref=self._dst(lands, t, me, k), **sems))
        return res

    def start(self, lands):
        n = self.n

        def body(*refs):
            ins, lnd = refs[:n], refs[n:2 * n]
            send_sems, recv_sems = refs[2 * n], refs[2 * n + 1]
            token = refs[-1]
            for cp in self._copies(ins, lnd, send_sems, recv_sems, arrivals=False):
                cp.start()
            token[...] = jnp.zeros_like(token)

        hbm = lambda a: pltpu.HBM(a.shape, a.dtype)
        res = pl.pallas_call(
            body, name=self.name + "_start",
            in_specs=[HBM_SPEC] * (2 * n),
            out_specs=[SEM_SPEC, SEM_SPEC] + [HBM_SPEC] * (2 * n) + [pl.BlockSpec(memory_space=pltpu.VMEM)],
            out_shape=[pltpu.SemaphoreType.DMA((N_PEERS * n,)), pltpu.SemaphoreType.DMA((N_PEERS * n,))]
            + [hbm(a) for a in self.srcs] + [hbm(a) for a in self.land_shapes] + [jax.ShapeDtypeStruct((8, 128), F32)],
            input_output_aliases={i: 2 + i for i in range(2 * n)},
            compiler_params=pltpu.CompilerParams(has_side_effects=pltpu.SideEffectType.DATAFLOW_SIDE_EFFECTING),
        )(*[pltpu.with_memory_space_constraint(a, pltpu.HBM) for a in self.srcs],
          *[pltpu.with_memory_space_constraint(a, pltpu.HBM) for a in lands])
        return res[:-1], res[-1]

    def wait(self, state, after):
        n = self.n
        send_sems, recv_sems = state[0], state[1]
        srcs, lands = state[2:2 + n], state[2 + n:2 + 2 * n]

        def body(*refs):
            ins, lnd = refs[:n], refs[n:2 * n]
            s_sems, r_sems = refs[2 * n], refs[2 * n + 1]
            for cp in self._copies(ins, lnd, s_sems, r_sems, arrivals=True):
                cp.wait_recv()
            for cp in self._copies(ins, lnd, s_sems, r_sems, arrivals=False):
                cp.wait_send()

        hbm = lambda a: pltpu.HBM(a.shape, a.dtype)
        res = pl.pallas_call(
            body, name=self.name + "_wait",
            in_specs=[HBM_SPEC] * (2 * n) + [SEM_SPEC, SEM_SPEC, ANY_SPEC],
            out_specs=[HBM_SPEC] * (2 * n),
            out_shape=[hbm(a) for a in self.srcs] + [hbm(a) for a in self.land_shapes],
            input_output_aliases={i: i for i in range(2 * n)},
            compiler_params=pltpu.CompilerParams(has_side_effects=pltpu.SideEffectType.DATAFLOW_SIDE_EFFECTING),
        )(*srcs, *lands, send_sems, recv_sems, after)
        return res[:n], res[n:]


def _sibling_swap(name, arrays):
    n = len(arrays)

    def body(*refs):
        ins, outs = refs[:n], refs[n:2 * n]
        send_sems, recv_sems = refs[2 * n:]
        sibling = (lax.axis_index("x"), lax.axis_index("y"), 1 - lax.axis_index("c"))
        copies = [pltpu.make_async_remote_copy(src_ref=ins[t], dst_ref=outs[t], send_sem=send_sems.at[t], recv_sem=recv_sems.at[t],
                                               device_id=sibling, device_id_type=MESH_IDS) for t in range(n)]
        for cp in copies:
            cp.start()
        for cp in copies:
            cp.wait()

    any_spec = pl.BlockSpec(memory_space=pl.ANY)
    return pl.pallas_call(
        body,
        name=name,
        in_specs=[any_spec] * n,
        out_specs=[any_spec] * n,
        out_shape=[jax.ShapeDtypeStruct(a.shape, a.dtype) for a in arrays],
        scratch_shapes=[pltpu.SemaphoreType.DMA((n,)), pltpu.SemaphoreType.DMA((n,))],
        compiler_params=pltpu.CompilerParams(has_side_effects=True),
    )(*arrays)


def _row_block(r, c, itemsize=4, target=1 << 20):
    if r % 8 != 0:
        return r
    best = 8
    for tr in range(8, r + 1, 8):
        if r % tr == 0 and tr * c * itemsize <= target:
            best = tr
    return best


def _sum_chips_into(parts, stacked, layer):
    _, r, c = parts.shape
    tr = _row_block(r, c)

    def body(p_ref, s_ref, o_ref):
        del s_ref
        o_ref[...] = ((p_ref[0] + p_ref[1]) + p_ref[2]) + p_ref[3]

    return pl.pallas_call(
        body,
        name="sum_chips",
        grid=(r // tr,),
        in_specs=[pl.BlockSpec((N_CHIPS, tr, c), lambda i: (0, i, 0)), pl.BlockSpec(memory_space=pl.ANY)],
        out_specs=pl.BlockSpec((None, tr, c), lambda i: (layer, i, 0)),
        out_shape=jax.ShapeDtypeStruct(stacked.shape, stacked.dtype),
        input_output_aliases={1: 0},
        compiler_params=_cparams(("parallel",)),
    )(parts, stacked)


def _sum_own_and_peers(me, g, axis, landed):
    _, r, c = landed.shape
    tr = _row_block(r, c)

    def body(me_ref, g_ref, p_ref, o_ref):
        del me_ref
        o_ref[...] = ((g_ref[...].astype(F32) + p_ref[0].astype(F32)) + p_ref[1].astype(F32)) + p_ref[2].astype(F32)

    return pl.pallas_call(
        body, name="sum_chips_own",
        grid_spec=pltpu.PrefetchScalarGridSpec(
            num_scalar_prefetch=1, grid=(r // tr,),
            in_specs=[_own_block_spec(r, c, axis, tr), pl.BlockSpec((N_PEERS, tr, c), lambda i, me: (0, i, 0))],
            out_specs=pl.BlockSpec((tr, c), lambda i, me: (i, 0))),
        out_shape=jax.ShapeDtypeStruct((r, c), F32),
        compiler_params=_cparams(("parallel",)),
    )(me, g, landed)


def _adamw_math(w, m, v, g):
    m_new = ADAM_B1 * m + (1.0 - ADAM_B1) * g
    v_new = ADAM_B2 * v + (1.0 - ADAM_B2) * jnp.square(g)
    m_hat = m_new / (1.0 - ADAM_B1 ** ADAM_STEP)
    v_hat = v_new / (1.0 - ADAM_B2 ** ADAM_STEP)
    return -ADAM_LR * (m_hat / (jnp.sqrt(v_hat) + ADAM_EPS) + ADAM_WD * w), m_new, v_new


def _adamw(w, m, v, g_a, g_b):
    L, r, c = w.shape
    tr = _row_block(r, c, target=1 << 19)

    def body(w_ref, m_ref, v_ref, ga_ref, gb_ref, g_ref, d_ref, nm_ref, nv_ref):
        g = ga_ref[...] + gb_ref[...]
        g_ref[...] = g
        d_ref[...], nm_ref[...], nv_ref[...] = _adamw_math(w_ref[...], m_ref[...], v_ref[...], g)

    spec = pl.BlockSpec((None, tr, c), lambda l, i: (l, i, 0))
    return pl.pallas_call(
        body,
        name="adamw",
        grid=(L, r // tr),
        in_specs=[spec] * 5,
        out_specs=[spec] * 4,
        out_shape=[jax.ShapeDtypeStruct(w.shape, F32)] * 4,
        compiler_params=_cparams(("parallel", "parallel")),
    )(w, m, v, g_a, g_b)


def _adamw_layer(w, m, v, g_a, g_b, layer, outs):
    L, r, c = w.shape
    tr = _row_block(r, c, target=1 << 19)
    n_prev = 0 if outs is None else 4

    def body(w_ref, m_ref, v_ref, ga_ref, gb_ref, *rest):
        g_ref, d_ref, nm_ref, nv_ref = rest[n_prev:]
        g = ga_ref[...] + gb_ref[...]
        g_ref[...] = g
        d_ref[...], nm_ref[...], nv_ref[...] = _adamw_math(w_ref[...], m_ref[...], v_ref[...], g)

    at_layer = pl.BlockSpec((None, tr, c), lambda i: (layer, i, 0))
    flat = pl.BlockSpec((tr, c), lambda i: (i, 0))
    return pl.pallas_call(
        body,
        name="adamw_layer",
        grid=(r // tr,),
        in_specs=[at_layer] * 3 + [flat] * 2 + [ANY_SPEC] * n_prev,
        out_specs=[at_layer] * 4,
        out_shape=[jax.ShapeDtypeStruct(w.shape, F32)] * 4,
        input_output_aliases={5 + k: k for k in range(n_prev)},
        compiler_params=_cparams(("parallel",)),
    )(w, m, v, g_a, g_b, *(outs or ()))


SHARDED = (("w_in", 1), ("conv_w", 1), ("w_mem_k", 0), ("w_mem_v", 0), ("w_branch", 1), ("w_o", 0), ("w_up", 1), ("w_down", 0))
SMALL = ("lower_bounds", "hg_norm_w", "b_gate", "ln1_g", "ln1_b", "ln2_g", "ln2_b")
WEIGHT_ORDER = ("lower_bounds", "w_in", "conv_w", "hg_norm_w", "w_mem_k", "w_mem_v", "w_branch", "b_gate", "w_o", "ln1_g", "ln1_b",
                "w_up", "w_down", "ln2_g", "ln2_b")


def kernel(x, mem, lower_bounds, w_in, conv_w, hg_norm_w, w_mem_k, w_mem_v, w_branch, b_gate, w_o, ln1_g, ln1_b, w_up, w_down, ln2_g, ln2_b, loss_target, m_lower_bounds, m_w_in, m_conv_w, m_hg_norm_w, m_w_mem_k, m_w_mem_v, m_w_branch, m_b_gate, m_w_o, m_ln1_g, m_ln1_b, m_w_up, m_w_down, m_ln2_g, m_ln2_b, v_lower_bounds, v_w_in, v_conv_w, v_hg_norm_w, v_w_mem_k, v_w_mem_v, v_w_branch, v_b_gate, v_w_o, v_ln1_g, v_ln1_b, v_w_up, v_w_down, v_ln2_g, v_ln2_b):
    bl, seq, d = x.shape
    depth = w_in.shape[0]
    weights = dict(lower_bounds=lower_bounds, w_in=w_in, conv_w=conv_w, hg_norm_w=hg_norm_w, w_mem_k=w_mem_k, w_mem_v=w_mem_v,
                   w_branch=w_branch, b_gate=b_gate, w_o=w_o, ln1_g=ln1_g, ln1_b=ln1_b, w_up=w_up, w_down=w_down, ln2_g=ln2_g, ln2_b=ln2_b)
    mom_m = dict(lower_bounds=m_lower_bounds, w_in=m_w_in, conv_w=m_conv_w, hg_norm_w=m_hg_norm_w, w_mem_k=m_w_mem_k, w_mem_v=m_w_mem_v,
                 w_branch=m_w_branch, b_gate=m_b_gate, w_o=m_w_o, ln1_g=m_ln1_g, ln1_b=m_ln1_b, w_up=m_w_up, w_down=m_w_down,
                 ln2_g=m_ln2_g, ln2_b=m_ln2_b)
    mom_v = dict(lower_bounds=v_lower_bounds, w_in=v_w_in, conv_w=v_conv_w, hg_norm_w=v_hg_norm_w, w_mem_k=v_w_mem_k, w_mem_v=v_w_mem_v,
                 w_branch=v_w_branch, b_gate=v_b_gate, w_o=v_w_o, ln1_g=v_ln1_g, ln1_b=v_ln1_b, w_up=v_w_up, w_down=v_w_down,
                 ln2_g=v_ln2_g, ln2_b=v_ln2_b)

    def shard2d(name, l):
        w = weights[name][l]
        if name == "w_branch":
            return w.reshape(N_BRANCH * W, w.shape[-1]).astype(BF16)
        return w if name == "conv_w" else w.astype(BF16)

    me = _my_chip()

    shard_axis = dict(SHARDED)

    def start_exchange(name, kind, items):
        ex = _Split(name, [(arr, kind, shard_axis[nm]) for nm, arr in items])
        state, token = ex.start(ex.landing_zones(me))
        return ex, state, [nm for nm, _ in items], token

    def start_gathers(l):
        first = start_exchange(f"gather_in_l{l}", "gather", [("w_in", shard2d("w_in", l))])
        rest = start_exchange(f"gather_rest_l{l}", "gather", [(nm, shard2d(nm, l)) for nm, _ in SHARDED if nm != "w_in"])
        return first, rest

    def gathered(pend, after):
        ex, state, names, _ = pend
        return dict(zip(names, ex.wait(state, after=after)[1]))

    x2d, mem2, t2d = x.reshape(bl * seq, d), mem.reshape(-1, d), loss_target.reshape(bl * seq, d)
    alpha = (2.0 * depth) ** 0.25
    soft, lb_all = _lower_bounds_fwd(lower_bounds)

    h, hb, saved, layer_wts = x2d, x2d.astype(BF16), [], []
    pending = start_gathers(0)
    for l in range(depth):
        first, rest = pending
        w_in_l = gathered(first, h)["w_in"]

        def rest_fn(after, l=l, rest=rest):
            wts = gathered(rest, after)
            for name in ("hg_norm_w", "b_gate", "ln1_g", "ln1_b", "ln2_g", "ln2_b"):
                wts[name] = weights[name][l][None, :]
            return wts

        deps = (rest[3],)
        if l + 1 < depth:
            pending = start_gathers(l + 1)
            deps += (pending[0][3], pending[1][3])
        h, hb, sv, wts = _layer_fwd(h, hb, mem2, lb_all[l:l + 1], w_in_l, rest_fn, bl=bl, seq=seq, alpha=alpha, deps=deps)
        saved.append(sv)
        layer_wts.append(wts)
    loss, dh = _loss_head(h, t2d)

    shape3 = {name: (depth, weights[name].size // (depth * weights[name].shape[-1]), weights[name].shape[-1]) for name, _ in SHARDED}
    partial = [dict() for _ in range(depth)]
    smalls = [None] * depth
    outs = {name: None for name, _ in SHARDED}

    def finish_reduce(pend, l, after):
        ex, state, names, _ = pend
        sent, got = ex.wait(state, after=after)
        for nm, g_full, landed in zip(names, sent, got):
            partial[l][nm] = _sum_own_and_peers(me, g_full, shard_axis[nm], landed)

    def optimizer_step(l):
        names = [name for name, _ in SHARDED]
        theirs = _sibling_swap(f"swap_partials_l{l}", [partial[l][nm] for nm in names])
        for nm, other in zip(names, theirs):
            outs[nm] = _adamw_layer(weights[nm].reshape(shape3[nm]), mom_m[nm].reshape(shape3[nm]), mom_v[nm].reshape(shape3[nm]),
                                    partial[l][nm], other, l, outs[nm])
        return tuple(outs[nm][0] for nm in names)

    pending_mix, deps = [], ()
    for l in reversed(range(depth)):
        dx1, g_mlp = _mlp_bwd(dh, saved[l], layer_wts[l], alpha=alpha, deps=deps)
        pending_mlp = start_exchange(f"reduce_mlp_l{l}", "scatter", [(nm, g_mlp[nm]) for nm in ("w_up", "w_down")])
        deps = (pending_mlp[3],)
        if pending_mix:
            for pend in pending_mix:
                finish_reduce(pend, l + 1, dx1)
            deps += optimizer_step(l + 1)
        pending_mix = []

        def send(names, g, l=l, pending_mix=pending_mix):
            pend = start_exchange(f"reduce_{names[0]}_l{l}", "scatter", [(nm, g[nm]) for nm in names])
            pending_mix.append(pend)
            return pend[3]

        dh, g = _mix_bwd(dx1, saved[l], mem2, lb_all[l:l + 1], layer_wts[l], bl=bl, seq=seq, alpha=alpha, send=send, deps=deps)
        finish_reduce(pending_mlp, l, dh)
        deps = ()
        g.update(g_mlp, lower_bounds=g["lb"])
        smalls[l] = jnp.concatenate([g[nm] for nm in SMALL], axis=1)
    small_parts = _chip_exchange("reduce_small", [(jnp.stack(smalls), "bcast", 0)])[0]
    small_sum = _sum_chips_into(small_parts.reshape(N_CHIPS, depth, -1), jnp.zeros((1, depth, small_parts.shape[-1]), F32), 0)
    small_sum = small_sum.reshape(depth, 1, -1)
    small_theirs = _sibling_swap("swap_small", [small_sum])[0]
    for pend in pending_mix:
        finish_reduce(pend, 0, small_theirs)
    optimizer_step(0)

    outs = {name: [r.reshape(weights[name].shape) for r in res] for name, res in outs.items()}
    off = 0
    for name in SMALL:
        n = weights[name].shape[1]
        mine, other = small_sum[:, :, off:off + n], small_theirs[:, :, off:off + n]
        off += n
        if name == "lower_bounds":
            mine = _lower_bounds_bwd(soft, mine[:, 0, :])[:, None, :]
            other = _lower_bounds_bwd(soft, other[:, 0, :])[:, None, :]
        shp = (depth, 1, n)
        res = _adamw(weights[name].reshape(shp), mom_m[name].reshape(shp), mom_v[name].reshape(shp), mine, other)
        outs[name] = [r.reshape(weights[name].shape) for r in res]
    assert off == small_sum.shape[-1]

    total_loss = lax.psum(loss[0, 0], ("x", "y", "c"))
    result = [total_loss, dh.reshape(bl, seq, d)]
    for k in range(4):
        result += [outs[name][k] for name in WEIGHT_ORDER]
    return tuple(result)
```

```python
import functools

import jax
import jax.numpy as jnp
from jax import lax
from jax.experimental import pallas as pl
from jax.experimental.pallas import tpu as pltpu

F32 = jnp.float32
BF16 = jnp.bfloat16

HG_HEADS = 4
HG_F = 128
HG_CHUNK = 32
MEM_HEADS = 4
MEM_HEAD_DIM = 128
BRANCH_WIDTH = 512
N_BRANCH = 3
CONV_K = 3
LN_EPS = 1e-5
RMS_EPS = 1e-6
ADAM_LR = 0.001
ADAM_B1 = 0.9
ADAM_B2 = 0.999
ADAM_EPS = 1e-08
ADAM_WD = 0.01
ADAM_STEP = 10

VMEM_LIMIT = 48 * 1024 * 1024


def _cparams(sem):
    return pltpu.CompilerParams(dimension_semantics=sem, vmem_limit_bytes=VMEM_LIMIT)


def _dot(a, b, dims):
    return lax.dot_general(a, b, (dims, ((), ())), preferred_element_type=F32)


NN = ((1,), (0,))
NT = ((1,), (1,))
TN = ((0,), (0,))


def _pick(n, pref):
    for t in pref:
        if n % t == 0:
            return t
    return n


ANY_SPEC = pl.BlockSpec(memory_space=pl.ANY)


def _matmul(name, a, b, *, mode, out_dtype=F32, a_fn=None, a_extra=(), epi_fn=None, epi_extra=(), n_out=1,
            tm=512, tn=1024, tk=1024, deps=()):
    M, K = a.shape
    N = b.shape[1] if mode == "nn" else b.shape[0]
    tm, tn, tk = _pick(M, (tm, 256, 128, 8)), _pick(N, (tn, 896, 512, 256, 128)), _pick(K, (tk, 512, 256, 128))
    nk = K // tk
    n_ax, n_ex = len(a_extra), len(epi_extra)
    n_in = 2 + n_ax + n_ex + len(deps)
    out_dtypes = out_dtype if isinstance(out_dtype, (tuple, list)) else (out_dtype,) * n_out

    def body(*refs):
        a_ref, b_ref = refs[0], refs[1]
        ax_refs = refs[2:2 + n_ax]
        ex_refs = refs[2 + n_ax:2 + n_ax + n_ex]
        o_refs = refs[n_in:n_in + n_out]
        at = a_ref[...]
        at = a_fn(at, *[r[...] for r in ax_refs]) if a_fn is not None else at.astype(BF16)
        part = _dot(at, b_ref[...].astype(BF16), NN if mode == "nn" else NT)

        def finish(acc):
            outs = epi_fn(acc, *[r[...] for r in ex_refs]) if epi_fn is not None else (acc,)
            for o_ref, o in zip(o_refs, outs):
                o_ref[...] = o.astype(o_ref.dtype)

        if nk == 1:
            finish(part)
            return
        acc_ref = refs[-1]
        k = pl.program_id(2)

        @pl.when(k == 0)
        def _():
            acc_ref[...] = part

        @pl.when(jnp.logical_and(k > 0, k < nk - 1))
        def _():
            acc_ref[...] += part

        @pl.when(k == nk - 1)
        def _():
            finish(acc_ref[...] + part)

    in_specs = [pl.BlockSpec((tm, tk), lambda j, i, k: (i, k)),
                pl.BlockSpec((tk, tn), lambda j, i, k: (k, j)) if mode == "nn" else pl.BlockSpec((tn, tk), lambda j, i, k: (j, k))]
    in_specs += [pl.BlockSpec((1, tk), lambda j, i, k: (0, k)) for _ in a_extra]
    for e in epi_extra:
        if e.shape[0] == 1:
            in_specs.append(pl.BlockSpec((1, tn), lambda j, i, k: (0, j)))
        else:
            in_specs.append(pl.BlockSpec((tm, tn), lambda j, i, k: (i, j)))
    in_specs += [ANY_SPEC] * len(deps)
    out = pl.pallas_call(
        body,
        name=name,
        grid=(N // tn, M // tm, nk),
        in_specs=in_specs,
        out_specs=[pl.BlockSpec((tm, tn), lambda j, i, k: (i, j)) for _ in range(n_out)],
        out_shape=[jax.ShapeDtypeStruct((M, N), dt) for dt in out_dtypes],
        scratch_shapes=[pltpu.VMEM((tm, tn), F32)] if nk > 1 else [],
        compiler_params=_cparams(("parallel", "parallel", "arbitrary")),
    )(a, b, *a_extra, *epi_extra, *deps)
    return out[0] if n_out == 1 else out


def _matmul_tn(name, a, b, *, a_fn=None, a_extra=(), a_cols=None, b_cols=None, ta=1024, tb=1024, tt=1024, out_dtype=F32, deps=()):
    T = a.shape[0]
    a0, Ka = a_cols if a_cols is not None else (0, a.shape[1])
    b0, Nb = b_cols if b_cols is not None else (0, b.shape[1])
    ta, tb, tt = _pick(Ka, (ta, 512, 256, 128)), _pick(Nb, (tb, 896, 512, 256, 128)), _pick(T, (tt, 512, 256, 128))
    assert a0 % ta == 0 and b0 % tb == 0
    a0, b0 = a0 // ta, b0 // tb
    nt = T // tt
    n_ax = len(a_extra)

    def body(*refs):
        a_ref, b_ref = refs[0], refs[1]
        ax_refs = refs[2:2 + n_ax]
        o_ref = refs[2 + n_ax + len(deps)]
        acc_ref = refs[-1]
        t = pl.program_id(2)
        at = a_ref[...]
        at = a_fn(at, *[r[...] for r in ax_refs]) if a_fn is not None else at.astype(BF16)
        part = _dot(at, b_ref[...].astype(BF16), TN)

        @pl.when(t == 0)
        def _():
            acc_ref[...] = part

        @pl.when(jnp.logical_and(t > 0, t < nt - 1))
        def _():
            acc_ref[...] += part

        @pl.when(t == nt - 1)
        def _():
            o_ref[...] = (acc_ref[...] + part if nt > 1 else part).astype(o_ref.dtype)

    in_specs = [pl.BlockSpec((tt, ta), lambda i, j, t: (t, a0 + i)), pl.BlockSpec((tt, tb), lambda i, j, t: (t, b0 + j))]
    in_specs += [pl.BlockSpec((1, ta), lambda i, j, t: (0, a0 + i)) for _ in a_extra]
    in_specs += [ANY_SPEC] * len(deps)
    return pl.pallas_call(
        body,
        name=name,
        grid=(Ka // ta, Nb // tb, nt),
        in_specs=in_specs,
        out_specs=pl.BlockSpec((ta, tb), lambda i, j, t: (i, j)),
        out_shape=jax.ShapeDtypeStruct((Ka, Nb), out_dtype),
        scratch_shapes=[pltpu.VMEM((ta, tb), F32)],
        compiler_params=_cparams(("parallel", "parallel", "arbitrary")),
    )(a, b, *a_extra, *deps)


W = BRANCH_WIDTH
C_CB, C_CC, C_CH, C_HQ, C_HF, C_HI, C_HG, C_MQ, N_MIX = 0, W, 2 * W, 3 * W, 4 * W, 5 * W, 6 * W, 7 * W, 8 * W
TS_MIX = 256
PREV_ROWS = 16


def _sigmoid(x):
    return jax.nn.sigmoid(x)


def _chunk_pos(shape):
    return lax.broadcasted_iota(jnp.int32, shape, 0) & (HG_CHUNK - 1)


def _seg_cumsum(x, pos):
    sh = 1
    while sh < HG_CHUNK:
        x = x + jnp.where(pos >= sh, pltpu.roll(x, sh, 0), 0.0)
        sh *= 2
    return x


def _seg_rev_cumsum(x, pos):
    n = x.shape[0]
    sh = 1
    while sh < HG_CHUNK:
        x = x + jnp.where(pos < HG_CHUNK - sh, pltpu.roll(x, n - sh, 0), 0.0)
        sh *= 2
    return x


def _chunk_mask(ts):
    r = lax.broadcasted_iota(jnp.int32, (ts, ts), 0)
    c = lax.broadcasted_iota(jnp.int32, (ts, ts), 1)
    return jnp.logical_and((r // HG_CHUNK) == (c // HG_CHUNK), c <= r)


def _hgrn_gates(p_ref, lb):
    q = p_ref[:, C_HQ:C_HQ + W].astype(F32)
    fl = p_ref[:, C_HF:C_HF + W].astype(F32)
    sig = _sigmoid(fl)
    f = lb + (1.0 - lb) * sig
    logf = jnp.log(f)
    k = (1.0 - lb) * _sigmoid(-fl)
    sq = _sigmoid(q)
    qs = q * sq
    return q, sq, qs, sig, f, logf, k


def _hgrn_decays(logf, bc_sc, ts):
    pos = _chunk_pos(logf.shape)
    bc = _seg_cumsum(logf, pos)
    bc_sc[...] = bc
    nc = ts // HG_CHUNK
    bref = jnp.concatenate(
        [jnp.broadcast_to(bc_sc[n * HG_CHUNK + HG_CHUNK // 2 - 1:n * HG_CHUNK + HG_CHUNK // 2, :], (HG_CHUNK, W)) for n in range(nc)], axis=0)
    blast = jnp.concatenate(
        [jnp.broadcast_to(bc_sc[(n + 1) * HG_CHUNK - 1:(n + 1) * HG_CHUNK, :], (HG_CHUNK, W)) for n in range(nc)], axis=0)
    return pos, bc, bref, blast


def _conv_shift_down(u, carry_ref, row):
    n = carry_ref.shape[0]
    last, before = carry_ref[n - 1:n, :], carry_ref[n - 2:n - 1, :]
    u1 = jnp.where(row == 0, last, pltpu.roll(u, 1, 0))
    u2 = jnp.where(row == 0, before, jnp.where(row == 1, last, pltpu.roll(u, 2, 0)))
    return u1, u2


def _attn_probs(qh, kh):
    s = _dot(qh, kh, NT) * (MEM_HEAD_DIM ** -0.5)
    e = jnp.exp(s - jnp.max(s, axis=-1, keepdims=True))
    return e / jnp.sum(e, axis=-1, keepdims=True)


def _mixer_fwd(p, mk, mv, lb, conv_w, norm_w, *, bl, seq):
    T = p.shape[0]
    ts = TS_MIX
    ns = seq // ts
    nc = ts // HG_CHUNK
    ml = mk.shape[0] // bl

    def body(p_ref, mk_ref, mv_ref, lb_ref, cw_ref, nw_ref, y_ref, st_ref, opre_ref, state_sc, carry_sc, bc_sc):
        @pl.when(pl.program_id(1) == 0)
        def _():
            state_sc[...] = jnp.zeros_like(state_sc)
            carry_sc[...] = jnp.zeros_like(carry_sc)

        cb, cc, ch = (p_ref[:, c0:c0 + W].astype(F32) for c0 in (C_CB, C_CC, C_CH))
        u = cc * ch
        row = lax.broadcasted_iota(jnp.int32, (ts, W), 0)
        u1, u2 = _conv_shift_down(u, carry_sc, row)
        yconv = u2 * cw_ref[0:1, :] + u1 * cw_ref[1:2, :] + u * cw_ref[2:3, :]
        y_ref[:, 0:W] = (cb * yconv).astype(BF16)
        carry_sc[...] = u[ts - 8:ts, :]

        lbv = lb_ref[...]
        _, _, qs, _, _, logf, k = _hgrn_gates(p_ref, lbv)
        pos, bc, bref, blast = _hgrn_decays(logf, bc_sc, ts)
        a_all = (qs * jnp.exp(bc - bref)).astype(BF16)
        bk_all = (k * jnp.exp(bref - bc)).astype(BF16)
        qin_all = (qs * jnp.exp(bc)).astype(BF16)
        kout_all = (k * jnp.exp(blast - bc)).astype(BF16)
        v_all = p_ref[:, C_HI:C_HI + W].astype(BF16)
        mask = _chunk_mask(ts)
        for h in range(HG_HEADS):
            hs = slice(h * HG_F, (h + 1) * HG_F)
            vb = v_all[:, hs]
            scores = jnp.where(mask, _dot(a_all[:, hs], bk_all[:, hs], NT), 0.0)
            o_intra = _dot(scores.astype(BF16), vb, NN)
            st = state_sc[h]
            o_inter = []
            for n in range(nc):
                rows = slice(n * HG_CHUNK, (n + 1) * HG_CHUNK)
                st_ref[n, h] = st
                o_inter.append(_dot(qin_all[rows, hs], st.astype(BF16), NT))
                kv = _dot(vb[rows], kout_all[rows, hs], TN)
                decay = jnp.exp(bc_sc[(n + 1) * HG_CHUNK - 1:(n + 1) * HG_CHUNK, hs])
                st = st * decay + kv
            state_sc[h] = st
            o = o_intra + jnp.concatenate(o_inter, axis=0)
            opre_ref[:, hs] = o
            on = o * lax.rsqrt(jnp.mean(o * o, axis=-1, keepdims=True) + RMS_EPS) * nw_ref[...]
            g = p_ref[:, C_HG + h * HG_F:C_HG + (h + 1) * HG_F].astype(F32)
            y_ref[:, W + h * HG_F:W + (h + 1) * HG_F] = (on * (g * _sigmoid(g))).astype(BF16)

        for h in range(MEM_HEADS):
            hs = slice(h * MEM_HEAD_DIM, (h + 1) * MEM_HEAD_DIM)
            qh = p_ref[:, C_MQ + h * MEM_HEAD_DIM:C_MQ + (h + 1) * MEM_HEAD_DIM].astype(BF16)
            prob = _attn_probs(qh, mk_ref[:, hs])
            y_ref[:, 2 * W + h * MEM_HEAD_DIM:2 * W + (h + 1) * MEM_HEAD_DIM] = _dot(prob.astype(BF16), mv_ref[:, hs], NN).astype(BF16)

    return pl.pallas_call(
        body,
        name="mixer_fwd",
        grid=(bl, ns),
        in_specs=[
            pl.BlockSpec((ts, N_MIX), lambda b, s: (b * ns + s, 0)),
            pl.BlockSpec((ml, W), lambda b, s: (b, 0)),
            pl.BlockSpec((ml, W), lambda b, s: (b, 0)),
            pl.BlockSpec((1, W), lambda b, s: (0, 0)),
            pl.BlockSpec((CONV_K, W), lambda b, s: (0, 0)),
            pl.BlockSpec((1, HG_F), lambda b, s: (0, 0)),
        ],
        out_specs=[
            pl.BlockSpec((ts, 3 * W), lambda b, s: (b * ns + s, 0)),
            pl.BlockSpec((nc, HG_HEADS, HG_F, HG_F), lambda b, s: (b * ns + s, 0, 0, 0)),
            pl.BlockSpec((ts, W), lambda b, s: (b * ns + s, 0)),
        ],
        out_shape=[
            jax.ShapeDtypeStruct((T, 3 * W), BF16),
            jax.ShapeDtypeStruct((T // HG_CHUNK, HG_HEADS, HG_F, HG_F), F32),
            jax.ShapeDtypeStruct((T, W), F32),
        ],
        scratch_shapes=[pltpu.VMEM((HG_HEADS, HG_F, HG_F), F32), pltpu.VMEM((8, W), F32), pltpu.VMEM((ts, W), F32)],
        compiler_params=_cparams(("arbitrary", "arbitrary")),
    )(p, mk, mv, lb, conv_w, norm_w)


def _mixer_bwd(p, dy, dp_gates, st, opre, mk, mv, lb, conv_w, norm_w, *, bl, seq, deps=()):
    T, nin = p.shape
    ts = TS_MIX
    ns = seq // ts
    nc = ts // HG_CHUNK
    ml = mk.shape[0] // bl
    mid, last = HG_CHUNK // 2 - 1, HG_CHUNK - 1

    def body(p_ref, pprev_ref, dy_ref, dpin_ref, st_ref, opre_ref, mk_ref, mv_ref, lb_ref, cw_ref, nw_ref, *rest):
        dp_ref, dmk_ref, dmv_ref, dcw_ref, dnw_ref, dlb_ref, dstate_sc, carry_sc, uprev_sc, bc_sc = rest[len(deps):]
        del dpin_ref
        b, s = pl.program_id(0), pl.program_id(1)

        @pl.when(s == 0)
        def _():
            dstate_sc[...] = jnp.zeros_like(dstate_sc)
            carry_sc[...] = jnp.zeros_like(carry_sc)
            dmk_ref[...] = jnp.zeros_like(dmk_ref)
            dmv_ref[...] = jnp.zeros_like(dmv_ref)

        @pl.when(jnp.logical_and(b == 0, s == 0))
        def _():
            dcw_ref[...] = jnp.zeros_like(dcw_ref)
            dnw_ref[...] = jnp.zeros_like(dnw_ref)
            dlb_ref[...] = jnp.zeros_like(dlb_ref)

        cb, cc, ch = (p_ref[:, c0:c0 + W].astype(F32) for c0 in (C_CB, C_CC, C_CH))
        u = cc * ch
        row = lax.broadcasted_iota(jnp.int32, (ts, W), 0)
        uprev = pprev_ref[:, C_CC:C_CC + W].astype(F32) * pprev_ref[:, C_CH:C_CH + W].astype(F32)
        uprev_sc[...] = jnp.where(s == ns - 1, 0.0, uprev)
        u1, u2 = _conv_shift_down(u, uprev_sc, row)
        w0, w1, w2 = cw_ref[0:1, :], cw_ref[1:2, :], cw_ref[2:3, :]
        dya = dy_ref[:, 0:W].astype(F32)
        dp_ref[:, C_CB:C_CB + W] = (dya * (u2 * w0 + u1 * w1 + u * w2)).astype(BF16)
        dv = cb * dya
        dv1 = jnp.where(row == ts - 1, carry_sc[0:1, :], pltpu.roll(dv, ts - 1, 0))
        dv2 = jnp.where(row == ts - 1, carry_sc[1:2, :], jnp.where(row == ts - 2, carry_sc[0:1, :], pltpu.roll(dv, ts - 2, 0)))
        du = dv * w2 + dv1 * w1 + dv2 * w0
        dp_ref[:, C_CC:C_CC + W] = (du * ch).astype(BF16)
        dp_ref[:, C_CH:C_CH + W] = (du * cc).astype(BF16)
        dcw_ref[0:1, :] += jnp.sum(dv * u2, axis=0, keepdims=True)
        dcw_ref[1:2, :] += jnp.sum(dv * u1, axis=0, keepdims=True)
        dcw_ref[2:3, :] += jnp.sum(dv * u, axis=0, keepdims=True)
        carry_sc[...] = dv[0:8, :]

        lbv = lb_ref[...]
        q_all, sq_all, qs_all, sig_all, f_all, logf, k_all = _hgrn_gates(p_ref, lbv)
        pos_all, bc, bref, blast = _hgrn_decays(logf, bc_sc, ts)
        ea_all, eb_all, eq_all, ek_all = jnp.exp(bc - bref), jnp.exp(bref - bc), jnp.exp(bc), jnp.exp(blast - bc)
        mask = _chunk_mask(ts)
        pos = _chunk_pos((ts, HG_F))
        pos_c = _chunk_pos((HG_CHUNK, HG_F))
        nw = nw_ref[...]
        for h in range(HG_HEADS):
            hs = slice(h * HG_F, (h + 1) * HG_F)
            qs, k, ea, eb, eq, ek = qs_all[:, hs], k_all[:, hs], ea_all[:, hs], eb_all[:, hs], eq_all[:, hs], ek_all[:, hs]
            a, bk, qin, kout = qs * ea, k * eb, qs * eq, k * ek
            o = opre_ref[:, hs]
            g = p_ref[:, C_HG + h * HG_F:C_HG + (h + 1) * HG_F].astype(F32)
            sg = _sigmoid(g)
            r = lax.rsqrt(jnp.mean(o * o, axis=-1, keepdims=True) + RMS_EPS)
            dyb = dy_ref[:, W + h * HG_F:W + (h + 1) * HG_F].astype(F32)
            dp_ref[:, C_HG + h * HG_F:C_HG + (h + 1) * HG_F] = (dyb * (o * r * nw) * (sg * (1.0 + g * (1.0 - sg)))).astype(BF16)
            don = dyb * (g * sg)
            dnw_ref[0:1, :] += jnp.sum(don * o * r, axis=0, keepdims=True)
            dn = don * nw
            do = r * (dn - o * (r * r) * jnp.mean(dn * o, axis=-1, keepdims=True))
            dob = do.astype(BF16)
            vb = p_ref[:, C_HI + h * HG_F:C_HI + (h + 1) * HG_F].astype(BF16)
            ab, bkb = a.astype(BF16), bk.astype(BF16)
            scores = jnp.where(mask, _dot(ab, bkb, NT), 0.0)
            dscores = jnp.where(mask, _dot(dob, vb, NT), 0.0).astype(BF16)
            dv_h = _dot(scores.astype(BF16), dob, TN)
            da = _dot(dscores, bkb, NN)
            dbk = _dot(dscores, ab, TN)
            koutb, qinb = kout.astype(BF16), qin.astype(BF16)
            dst = dstate_sc[h]
            dqin_p, dkout_p, dvi_p, ddec_p = [None] * nc, [None] * nc, [None] * nc, [None] * nc
            for n in reversed(range(nc)):
                rows = slice(n * HG_CHUNK, (n + 1) * HG_CHUNK)
                st_n = st_ref[n, h]
                decay = jnp.exp(bc_sc[n * HG_CHUNK + last:n * HG_CHUNK + last + 1, hs])
                dstb = dst.astype(BF16)
                dvi_p[n] = _dot(koutb[rows], dstb, NT)
                dkout_p[n] = _dot(vb[rows], dstb, NN)
                ddec_p[n] = jnp.sum(dst * st_n, axis=0, keepdims=True) * decay
                dqin_p[n] = _dot(dob[rows], st_n.astype(BF16), NN)
                dst = dst * decay + _dot(dob[rows], qinb[rows], TN)
            dstate_sc[h] = dst
            dqin = jnp.concatenate(dqin_p, axis=0)
            dkout = jnp.concatenate(dkout_p, axis=0)
            dp_ref[:, C_HI + h * HG_F:C_HI + (h + 1) * HG_F] = (dv_h + jnp.concatenate(dvi_p, axis=0)).astype(BF16)
            dqs = da * ea + dqin * eq
            dk = dbk * eb + dkout * ek
            t_a, t_b, t_q, t_k = da * a, dbk * bk, dqin * qin, dkout * kout
            dbc = t_a - t_b + t_q - t_k
            t_ref = t_b - t_a
            pieces = []
            for n in range(nc):
                rows = slice(n * HG_CHUNK, (n + 1) * HG_CHUNK)
                s_ref = jnp.sum(t_ref[rows], axis=0, keepdims=True)
                s_last = jnp.sum(t_k[rows], axis=0, keepdims=True) + ddec_p[n]
                pieces.append(dbc[rows] + jnp.where(pos_c == mid, s_ref, 0.0) + jnp.where(pos_c == last, s_last, 0.0))
            dlogf = _seg_rev_cumsum(jnp.concatenate(pieces, axis=0), pos)
            sig, lbh = sig_all[:, hs], lbv[:, hs]
            dfk = dlogf / f_all[:, hs] - dk
            dp_ref[:, C_HF + h * HG_F:C_HF + (h + 1) * HG_F] = (dfk * (1.0 - lbh) * sig * (1.0 - sig)).astype(BF16)
            dlb_ref[0:1, hs] += jnp.sum(dfk * (1.0 - sig), axis=0, keepdims=True)
            q, sq = q_all[:, hs], sq_all[:, hs]
            dp_ref[:, C_HQ + h * HG_F:C_HQ + (h + 1) * HG_F] = (dqs * (sq * (1.0 + q * (1.0 - sq)))).astype(BF16)

        for h in range(MEM_HEADS):
            hs = slice(h * MEM_HEAD_DIM, (h + 1) * MEM_HEAD_DIM)
            qh = p_ref[:, C_MQ + h * MEM_HEAD_DIM:C_MQ + (h + 1) * MEM_HEAD_DIM].astype(BF16)
            kh, vh = mk_ref[:, hs], mv_ref[:, hs]
            prob = _attn_probs(qh, kh)
            dob = dy_ref[:, 2 * W + h * MEM_HEAD_DIM:2 * W + (h + 1) * MEM_HEAD_DIM].astype(BF16)
            dmv_ref[:, hs] += _dot(prob.astype(BF16), dob, TN)
            dprob = _dot(dob, vh, NT)
            ds = prob * (dprob - jnp.sum(dprob * prob, axis=-1, keepdims=True)) * (MEM_HEAD_DIM ** -0.5)
            dsb = ds.astype(BF16)
            dp_ref[:, C_MQ + h * MEM_HEAD_DIM:C_MQ + (h + 1) * MEM_HEAD_DIM] = _dot(dsb, kh, NN).astype(BF16)
            dmk_ref[:, hs] += _dot(dsb, qh, TN)

    def tile(b, s):
        return b * ns + (ns - 1 - s)

    return pl.pallas_call(
        body,
        name="mixer_bwd",
        grid=(bl, ns),
        in_specs=[
            pl.BlockSpec((ts, N_MIX), lambda b, s: (tile(b, s), 0)),
            pl.BlockSpec((PREV_ROWS, N_MIX), lambda b, s: (jnp.maximum(tile(b, s) * (ts // PREV_ROWS) - 1, 0), 0)),
            pl.BlockSpec((ts, 3 * W), lambda b, s: (tile(b, s), 0)),
            pl.BlockSpec(memory_space=pl.ANY),
            pl.BlockSpec((nc, HG_HEADS, HG_F, HG_F), lambda b, s: (tile(b, s), 0, 0, 0)),
            pl.BlockSpec((ts, W), lambda b, s: (tile(b, s), 0)),
            pl.BlockSpec((ml, W), lambda b, s: (b, 0)),
            pl.BlockSpec((ml, W), lambda b, s: (b, 0)),
            pl.BlockSpec((1, W), lambda b, s: (0, 0)),
            pl.BlockSpec((CONV_K, W), lambda b, s: (0, 0)),
            pl.BlockSpec((1, HG_F), lambda b, s: (0, 0)),
        ] + [ANY_SPEC] * len(deps),
        out_specs=[
            pl.BlockSpec((ts, N_MIX), lambda b, s: (tile(b, s), 0)),
            pl.BlockSpec((ml, W), lambda b, s: (b, 0)),
            pl.BlockSpec((ml, W), lambda b, s: (b, 0)),
            pl.BlockSpec((8, W), lambda b, s: (0, 0)),
            pl.BlockSpec((8, HG_F), lambda b, s: (0, 0)),
            pl.BlockSpec((8, W), lambda b, s: (0, 0)),
        ],
        out_shape=[
            jax.ShapeDtypeStruct((T, nin), BF16),
            jax.ShapeDtypeStruct((bl * ml, W), F32),
            jax.ShapeDtypeStruct((bl * ml, W), F32),
            jax.ShapeDtypeStruct((8, W), F32),
            jax.ShapeDtypeStruct((8, HG_F), F32),
            jax.ShapeDtypeStruct((8, W), F32),
        ],
        input_output_aliases={3: 0},
        scratch_shapes=[pltpu.VMEM((HG_HEADS, HG_F, HG_F), F32), pltpu.VMEM((8, W), F32), pltpu.VMEM((PREV_ROWS, W), F32),
                        pltpu.VMEM((ts, W), F32)],
        compiler_params=_cparams(("arbitrary", "arbitrary")),
    )(p, p, dy, dp_gates, st, opre, mk, mv, lb, conv_w, norm_w, *deps)


def _layer_norm_stats(z):
    mu = jnp.mean(z, axis=-1, keepdims=True)
    zc = z - mu
    rstd = lax.rsqrt(jnp.mean(zc * zc, axis=-1, keepdims=True) + LN_EPS)
    return zc * rstd, rstd


def _gate_specs(tm, d):
    g0 = N_MIX // d
    return [pl.BlockSpec((tm, d), functools.partial(lambda i, k: (i, g0 + k), k=k)) for k in range(N_BRANCH)]


def _merge_fwd(y, p, x0, wb, wo, bg, ln_g, ln_b, *, alpha, tm=256):
    T, d = x0.shape
    assert N_MIX % d == 0
    tm = _pick(T, (tm, 128, 8))

    def body(y_ref, g0_ref, g1_ref, g2_ref, x_ref, wb_ref, wo_ref, bg_ref, lg_ref, lb_ref, r_ref, mg_ref, xh_ref, rs_ref, x1b_ref):
        merged = None
        for i, g_ref in enumerate((g0_ref, g1_ref, g2_ref)):
            r = _dot(y_ref[:, i * W:(i + 1) * W], wb_ref[i * W:(i + 1) * W, :], NN)
            r_ref[:, i * d:(i + 1) * d] = r.astype(BF16)
            t = _sigmoid(g_ref[...].astype(F32) + bg_ref[:, i * d:(i + 1) * d]) * r
            merged = t if merged is None else merged + t
        mb = merged.astype(BF16)
        mg_ref[...] = mb
        z = alpha * x_ref[...] + _dot(mb, wo_ref[...], NN)
        xh, rs = _layer_norm_stats(z)
        xh_ref[...], rs_ref[...] = xh, rs
        x1b_ref[...] = (xh * lg_ref[...] + lb_ref[...]).astype(BF16)

    row = lambda i: (i, 0)
    fix = lambda i: (0, 0)
    return pl.pallas_call(
        body,
        name="merge_fwd",
        grid=(T // tm,),
        in_specs=[pl.BlockSpec((tm, 3 * W), row)] + _gate_specs(tm, d) + [
            pl.BlockSpec((tm, d), row), pl.BlockSpec((3 * W, d), fix), pl.BlockSpec((d, d), fix), pl.BlockSpec((1, 3 * d), fix),
            pl.BlockSpec((1, d), fix), pl.BlockSpec((1, d), fix)],
        out_specs=[pl.BlockSpec((tm, 3 * d), row), pl.BlockSpec((tm, d), row), pl.BlockSpec((tm, d), row), pl.BlockSpec((tm, 1), row),
                   pl.BlockSpec((tm, d), row)],
        out_shape=[jax.ShapeDtypeStruct((T, 3 * d), BF16), jax.ShapeDtypeStruct((T, d), BF16),
                   jax.ShapeDtypeStruct((T, d), F32), jax.ShapeDtypeStruct((T, 1), F32), jax.ShapeDtypeStruct((T, d), BF16)],
        compiler_params=_cparams(("parallel",)),
    )(y, p, p, p, x0, wb, wo, bg, ln_g, ln_b)


def _merge_bwd(dz, p, r, wb, wo, bg, *, tm=256):
    T, d = dz.shape
    nin = p.shape[1]
    tm = _pick(T, (tm, 128, 8))

    def body(dz_ref, g0_ref, g1_ref, g2_ref, r_ref, wb_ref, wo_ref, bg_ref, dr_ref, dp_ref, dy_ref, dbg_ref):
        @pl.when(pl.program_id(0) == 0)
        def _():
            dbg_ref[...] = jnp.zeros_like(dbg_ref)

        dmerged = _dot(dz_ref[...].astype(BF16), wo_ref[...], NT)
        dp_ref[:, 0:N_MIX] = jnp.zeros((tm, N_MIX), BF16)
        for i, g_ref in enumerate((g0_ref, g1_ref, g2_ref)):
            cs = slice(i * d, (i + 1) * d)
            s = _sigmoid(g_ref[...].astype(F32) + bg_ref[:, cs])
            drb = (dmerged * s).astype(BF16)
            dr_ref[:, cs] = drb
            dgate = dmerged * r_ref[:, cs].astype(F32) * s * (1.0 - s)
            dp_ref[:, N_MIX + i * d:N_MIX + (i + 1) * d] = dgate.astype(BF16)
            dbg_ref[0:1, cs] += jnp.sum(dgate, axis=0, keepdims=True)
            dy_ref[:, i * W:(i + 1) * W] = _dot(drb, wb_ref[i * W:(i + 1) * W, :], NT).astype(BF16)

    row = lambda i: (i, 0)
    fix = lambda i: (0, 0)
    return pl.pallas_call(
        body,
        name="merge_bwd",
        grid=(T // tm,),
        in_specs=[pl.BlockSpec((tm, d), row)] + _gate_specs(tm, d) + [
            pl.BlockSpec((tm, 3 * d), row), pl.BlockSpec((3 * W, d), fix), pl.BlockSpec((d, d), fix), pl.BlockSpec((1, 3 * d), fix)],
        out_specs=[pl.BlockSpec((tm, 3 * d), row), pl.BlockSpec((tm, nin), row), pl.BlockSpec((tm, 3 * W), row),
                   pl.BlockSpec((8, 3 * d), fix)],
        out_shape=[jax.ShapeDtypeStruct((T, 3 * d), BF16), jax.ShapeDtypeStruct((T, nin), BF16),
                   jax.ShapeDtypeStruct((T, 3 * W), BF16), jax.ShapeDtypeStruct((8, 3 * d), F32)],
        compiler_params=_cparams(("arbitrary",)),
    )(dz, p, p, p, r, wb, wo, bg)


def _mlp_fwd(xhat1, g1, b1, wu, wd, g2, b2, *, alpha, tm=512, tf=1024):
    T, d = xhat1.shape
    ff = wu.shape[1]
    tm, tf = _pick(T, (tm, 256, 128, 8)), _pick(ff, (tf, 512, 256, 128))
    nf = ff // tf

    def body(xh_ref, g1_ref, b1_ref, wu_ref, wd_ref, g2_ref, b2_ref, a_ref, xh2_ref, rs2_ref, x2_ref, x2b_ref, acc_ref):
        f = pl.program_id(1)
        x1 = xh_ref[...] * g1_ref[...] + b1_ref[...]
        a = _dot(x1.astype(BF16), wu_ref[...], NN)
        a_ref[...] = a.astype(BF16)
        h = jnp.square(jnp.maximum(a, 0.0))
        part = _dot(h.astype(BF16), wd_ref[...], NN)

        @pl.when(f == 0)
        def _():
            acc_ref[...] = part

        @pl.when(f > 0)
        def _():
            acc_ref[...] += part

        @pl.when(f == nf - 1)
        def _():
            xh2, rs2 = _layer_norm_stats(alpha * x1 + acc_ref[...])
            xh2_ref[...] = xh2
            rs2_ref[...] = rs2
            x2 = xh2 * g2_ref[...] + b2_ref[...]
            x2_ref[...] = x2
            x2b_ref[...] = x2.astype(BF16)

    row = lambda i, f: (i, 0)
    fix = lambda i, f: (0, 0)
    return pl.pallas_call(
        body,
        name="mlp_fwd",
        grid=(T // tm, nf),
        in_specs=[pl.BlockSpec((tm, d), row), pl.BlockSpec((1, d), fix), pl.BlockSpec((1, d), fix),
                  pl.BlockSpec((d, tf), lambda i, f: (0, f)), pl.BlockSpec((tf, d), lambda i, f: (f, 0)),
                  pl.BlockSpec((1, d), fix), pl.BlockSpec((1, d), fix)],
        out_specs=[pl.BlockSpec((tm, tf), lambda i, f: (i, f)), pl.BlockSpec((tm, d), row), pl.BlockSpec((tm, 1), row),
                   pl.BlockSpec((tm, d), row), pl.BlockSpec((tm, d), row)],
        out_shape=[jax.ShapeDtypeStruct((T, ff), BF16), jax.ShapeDtypeStruct((T, d), F32), jax.ShapeDtypeStruct((T, 1), F32),
                   jax.ShapeDtypeStruct((T, d), F32), jax.ShapeDtypeStruct((T, d), BF16)],
        scratch_shapes=[pltpu.VMEM((tm, d), F32)],
        compiler_params=_cparams(("parallel", "arbitrary")),
    )(xhat1, g1, b1, wu, wd, g2, b2)


def _ln_bwd(dy, xhat, rstd, g, *, tm=512, deps=()):
    T, d = dy.shape
    tm = _pick(T, (tm, 256, 128, 8))

    def body(dy_ref, xh_ref, rs_ref, g_ref, *rest):
        dz_ref, dzb_ref, dg_ref, db_ref = rest[len(deps):]

        @pl.when(pl.program_id(0) == 0)
        def _():
            dg_ref[...] = jnp.zeros_like(dg_ref)
            db_ref[...] = jnp.zeros_like(db_ref)

        dy_, xh = dy_ref[...], xh_ref[...]
        dg_ref[0:1, :] += jnp.sum(dy_ * xh, axis=0, keepdims=True)
        db_ref[0:1, :] += jnp.sum(dy_, axis=0, keepdims=True)
        dxh = dy_ * g_ref[...]
        dz = rs_ref[...] * (dxh - jnp.mean(dxh, axis=-1, keepdims=True) - xh * jnp.mean(dxh * xh, axis=-1, keepdims=True))
        dz_ref[...] = dz
        dzb_ref[...] = dz.astype(BF16)

    row = lambda i: (i, 0)
    fix = lambda i: (0, 0)
    return pl.pallas_call(
        body,
        name="ln_bwd",
        grid=(T // tm,),
        in_specs=[pl.BlockSpec((tm, d), row), pl.BlockSpec((tm, d), row), pl.BlockSpec((tm, 1), row), pl.BlockSpec((1, d), fix)]
        + [ANY_SPEC] * len(deps),
        out_specs=[pl.BlockSpec((tm, d), row), pl.BlockSpec((tm, d), row), pl.BlockSpec((8, d), fix), pl.BlockSpec((8, d), fix)],
        out_shape=[jax.ShapeDtypeStruct((T, d), F32), jax.ShapeDtypeStruct((T, d), BF16), jax.ShapeDtypeStruct((8, d), F32),
                   jax.ShapeDtypeStruct((8, d), F32)],
        compiler_params=_cparams(("arbitrary",)),
    )(dy, xhat, rstd, g, *deps)


def _loss_head(y, target, *, tm=512):
    T, d = y.shape
    tm = _pick(T, (tm, 256, 128, 8))
    n = T // tm

    def body(y_ref, t_ref, loss_ref, dy_ref, acc_ref):
        i = pl.program_id(0)

        @pl.when(i == 0)
        def _():
            acc_ref[...] = jnp.zeros_like(acc_ref)

        e = y_ref[...] - t_ref[...]
        dy_ref[...] = e * (1.0 / d)
        acc_ref[...] += jnp.sum(e * e, axis=0, keepdims=True)

        @pl.when(i == n - 1)
        def _():
            loss_ref[...] = (0.5 / d) * jnp.sum(acc_ref[...], axis=1, keepdims=True)

    row = lambda i: (i, 0)
    return pl.pallas_call(
        body,
        name="loss_head",
        grid=(n,),
        in_specs=[pl.BlockSpec((tm, d), row), pl.BlockSpec((tm, d), row)],
        out_specs=[pl.BlockSpec((1, 1), lambda i: (0, 0)), pl.BlockSpec((tm, d), row)],
        out_shape=[jax.ShapeDtypeStruct((1, 1), F32), jax.ShapeDtypeStruct((T, d), F32)],
        scratch_shapes=[pltpu.VMEM((1, d), F32)],
        compiler_params=_cparams(("arbitrary",)),
    )(y, target)


def _lower_bounds_fwd(lower_bounds):
    depth, n = lower_bounds.shape

    def body(x_ref, soft_ref, lb_ref):
        x = x_ref[...]
        e = jnp.exp(x - jnp.max(x, axis=0, keepdims=True))
        soft_ref[...] = e / jnp.sum(e, axis=0, keepdims=True)
        run = None
        for l in range(depth):
            run = soft_ref[l:l + 1, :] if run is None else run + soft_ref[l:l + 1, :]
            lb_ref[l:l + 1, :] = run - soft_ref[0:1, :]

    return pl.pallas_call(body, name="lower_bounds_fwd",
                          out_shape=[jax.ShapeDtypeStruct((depth, n), F32), jax.ShapeDtypeStruct((depth, n), F32)])(lower_bounds)


def _lower_bounds_bwd(soft, dlb):
    depth, n = soft.shape

    def body(soft_ref, dlb_ref, out_ref, dsoft_ref):
        total = jnp.sum(dlb_ref[...], axis=0, keepdims=True)
        run = None
        for l in reversed(range(depth)):
            run = dlb_ref[l:l + 1, :] if run is None else run + dlb_ref[l:l + 1, :]
            dsoft_ref[l:l + 1, :] = run - total if l == 0 else run
        s, ds = soft_ref[...], dsoft_ref[...]
        out_ref[...] = s * (ds - jnp.sum(s * ds, axis=0, keepdims=True))

    return pl.pallas_call(body, name="lower_bounds_bwd", out_shape=jax.ShapeDtypeStruct((depth, n), F32),
                          scratch_shapes=[pltpu.VMEM((depth, n), F32)])(soft, dlb)


def _layer_fwd(x0, x0b, mem2, lb, w_in, rest_fn, *, bl, seq, alpha, deps=()):
    p = _matmul("proj_in", x0b, w_in, mode="nn", out_dtype=BF16, deps=deps, tm=1024, tn=1792)
    wts = dict(rest_fn(p), w_in=w_in)
    mk = _matmul("mem_k", mem2, wts["w_mem_k"], mode="nn", out_dtype=BF16)
    mv = _matmul("mem_v", mem2, wts["w_mem_v"], mode="nn", out_dtype=BF16)
    y, st, opre = _mixer_fwd(p, mk, mv, lb, wts["conv_w"], wts["hg_norm_w"], bl=bl, seq=seq)
    r, merged, xhat1, rstd1, x1b = _merge_fwd(y, p, x0, wts["w_branch"], wts["w_o"], wts["b_gate"], wts["ln1_g"], wts["ln1_b"],
                                              alpha=alpha)
    a, xhat2, rstd2, x2, x2b = _mlp_fwd(xhat1, wts["ln1_g"], wts["ln1_b"], wts["w_up"], wts["w_down"], wts["ln2_g"], wts["ln2_b"],
                                        alpha=alpha)
    saved = dict(x0b=x0b, p=p, mk=mk, mv=mv, y=y, st=st, opre=opre, r=r, merged=merged, xhat1=xhat1, rstd1=rstd1, x1b=x1b, a=a,
                 xhat2=xhat2, rstd2=rstd2)
    return x2, x2b, saved, wts


def _relu2_bf16(a):
    return jnp.square(jnp.maximum(a.astype(F32), 0.0)).astype(BF16)


def _mlp_bwd(dx2, sv, wts, *, alpha, deps=()):
    g = {}
    dz2, dz2b, dg2, db2 = _ln_bwd(dx2, sv["xhat2"], sv["rstd2"], wts["ln2_g"], deps=deps)
    g["ln2_g"], g["ln2_b"] = dg2[0:1], db2[0:1]
    da = _matmul("mlp_da", dz2b, wts["w_down"], mode="nt", out_dtype=BF16, tm=1024,
                 epi_fn=lambda acc, a: (acc * (2.0 * jnp.maximum(a.astype(F32), 0.0)),), epi_extra=(sv["a"],))
    g["w_down"] = _matmul_tn("grad_w_down", sv["a"], dz2b, a_fn=_relu2_bf16, out_dtype=BF16, tt=2048)
    g["w_up"] = _matmul_tn("grad_w_up", sv["x1b"], da, out_dtype=BF16, tt=2048)
    dx1 = _matmul("mlp_dx", da, wts["w_up"], mode="nt", epi_fn=lambda acc, dz: (acc + alpha * dz,), epi_extra=(dz2,), tm=1024)
    return dx1, g


def _mix_bwd(dx1, sv, mem2, lb, wts, *, bl, seq, alpha, send, deps=()):
    d = dx1.shape[1]
    g = {}
    dz1, dz1b, dg1, db1 = _ln_bwd(dx1, sv["xhat1"], sv["rstd1"], wts["ln1_g"], deps=deps)
    g["ln1_g"], g["ln1_b"] = dg1[0:1], db1[0:1]
    g["w_o"] = _matmul_tn("grad_w_o", sv["merged"], dz1b, out_dtype=BF16, tt=2048)
    dr, dp, dy, dbg = _merge_bwd(dz1b, sv["p"], sv["r"], wts["w_branch"], wts["w_o"], wts["b_gate"])
    g["b_gate"] = dbg[0:1]
    g["w_branch"] = jnp.concatenate(
        [_matmul_tn("grad_w_branch", sv["y"], dr, a_cols=(i * W, W), b_cols=(i * d, d), out_dtype=BF16) for i in range(N_BRANCH)],
        axis=0)
    token = send(("w_o", "w_branch"), g)
    dp, dmk, dmv, dcw, dnw, dlb = _mixer_bwd(sv["p"], dy, dp, sv["st"], sv["opre"], sv["mk"], sv["mv"], lb,
                                              wts["conv_w"], wts["hg_norm_w"], bl=bl, seq=seq, deps=(token,))
    g["conv_w"], g["hg_norm_w"], g["lb"] = dcw[0:CONV_K], dnw[0:1], dlb[0:1]
    g["w_mem_k"] = _matmul_tn("grad_w_mem_k", mem2, dmk, out_dtype=BF16)
    g["w_mem_v"] = _matmul_tn("grad_w_mem_v", mem2, dmv, out_dtype=BF16)
    g["w_in"] = _matmul_tn("grad_w_in", sv["x0b"], dp, out_dtype=BF16, tt=2048)
    token = send(("w_in", "w_mem_k", "w_mem_v", "conv_w"), g)
    dx0 = _matmul("proj_in_dx", dp, wts["w_in"], mode="nt", epi_fn=lambda acc, dz: (acc + alpha * dz,), epi_extra=(dz1,),
                  tm=1024, tk=1792, deps=(token,))
    return dx0, g


N_CHIPS = 4
MESH_IDS = pl.DeviceIdType.MESH


def _axis_slice(ref, axis, start, size):
    idx = [slice(None)] * len(ref.shape)
    idx[axis] = pl.ds(start, size)
    return ref.at[tuple(idx)]


def _chip_exchange(name, items):
    n = len(items)
    out_shapes, meta = [], []
    for arr, kind, axis in items:
        shp = list(arr.shape)
        if kind == "gather":
            per = shp[axis]
            shp[axis] = per * N_CHIPS
            out_shapes.append(jax.ShapeDtypeStruct(tuple(shp), arr.dtype))
        elif kind == "scatter":
            per = shp[axis] // N_CHIPS
            shp[axis] = per
            out_shapes.append(jax.ShapeDtypeStruct((N_CHIPS, *shp), arr.dtype))
        else:
            per = None
            out_shapes.append(jax.ShapeDtypeStruct((N_CHIPS, *shp), arr.dtype))
        meta.append((kind, axis, per))

    def body(*refs):
        ins, outs = refs[:n], refs[n:2 * n]
        send_sems, recv_sems, local_sems = refs[2 * n:]
        x, y, c = lax.axis_index("x"), lax.axis_index("y"), lax.axis_index("c")
        me = 2 * x + y
        peers = [(1 - x, y), (x, 1 - y), (1 - x, 1 - y)]

        def src_for(t, chip):
            kind, axis, per = meta[t]
            return _axis_slice(ins[t], axis, chip * per, per) if kind == "scatter" else ins[t]

        def dst_from(t, chip):
            kind, axis, per = meta[t]
            return _axis_slice(outs[t], axis, chip * per, per) if kind == "gather" else outs[t].at[chip]

        def remote(t, k):
            px, py = peers[k]
            return pltpu.make_async_remote_copy(
                src_ref=src_for(t, 2 * px + py), dst_ref=dst_from(t, me), send_sem=send_sems.at[t * 3 + k],
                recv_sem=recv_sems.at[t * 3 + k], device_id=(px, py, c), device_id_type=MESH_IDS)

        def arrival(t, k):
            px, py = peers[k]
            return pltpu.make_async_remote_copy(
                src_ref=src_for(t, me), dst_ref=dst_from(t, 2 * px + py), send_sem=send_sems.at[t * 3 + k],
                recv_sem=recv_sems.at[t * 3 + k], device_id=(px, py, c), device_id_type=MESH_IDS)

        sends = [remote(t, k) for t in range(n) for k in range(3)]
        for cp in sends:
            cp.start()
        own = [pltpu.make_async_copy(src_for(t, me), dst_from(t, me), local_sems.at[t]) for t in range(n)]
        for cp in own:
            cp.start()
        for t in range(n):
            for k in range(3):
                arrival(t, k).wait_recv()
        for cp in sends:
            cp.wait_send()
        for cp in own:
            cp.wait()

    any_spec = pl.BlockSpec(memory_space=pl.ANY)
    return pl.pallas_call(
        body,
        name=name,
        in_specs=[any_spec] * n,
        out_specs=[any_spec] * n,
        out_shape=out_shapes,
        scratch_shapes=[pltpu.SemaphoreType.DMA((3 * n,)), pltpu.SemaphoreType.DMA((3 * n,)), pltpu.SemaphoreType.DMA((n,))],
        compiler_params=pltpu.CompilerParams(has_side_effects=True),
    )(*[a for a, _, _ in items])


HBM_SPEC = pl.BlockSpec(memory_space=pltpu.HBM)
SEM_SPEC = pl.BlockSpec(memory_space=pltpu.SEMAPHORE)
N_PEERS = N_CHIPS - 1


def _my_chip():
    return (2 * lax.axis_index("x") + lax.axis_index("y")).astype(jnp.int32).reshape(1)


def _own_block_spec(r, c, axis, tr):
    if axis == 1:
        return pl.BlockSpec((tr, c), lambda i, me: (i, me[0]))
    return pl.BlockSpec((tr, c), lambda i, me: (me[0] * (r // tr) + i, 0))


def _place_shard(name, shard, axis, me):
    r, c = shard.shape
    tr = _row_block(r, c, shard.dtype.itemsize)
    shp = (r, c * N_CHIPS) if axis == 1 else (r * N_CHIPS, c)

    def body(me_ref, s_ref, o_ref):
        del me_ref
        o_ref[...] = s_ref[...]

    return pl.pallas_call(
        body, name=name,
        grid_spec=pltpu.PrefetchScalarGridSpec(
            num_scalar_prefetch=1, grid=(r // tr,),
            in_specs=[pl.BlockSpec((tr, c), lambda i, me: (i, 0))], out_specs=_own_block_spec(r, c, axis, tr)),
        out_shape=jax.ShapeDtypeStruct(shp, shard.dtype),
        compiler_params=_cparams(("parallel",)),
    )(me, shard)


class _Split:
    def __init__(self, name, items):
        self.name, self.n = name, len(items)
        self.srcs = [a for a, _, _ in items]
        self.meta, self.land_shapes = [], []
        for arr, kind, axis in items:
            shp = list(arr.shape)
            if kind == "gather":
                per = shp[axis]
                shp[axis] = per * N_CHIPS
                self.land_shapes.append(jax.ShapeDtypeStruct(tuple(shp), arr.dtype))
            else:
                per = shp[axis] // N_CHIPS
                shp[axis] = per
                self.land_shapes.append(jax.ShapeDtypeStruct((N_PEERS, *shp), arr.dtype))
            self.meta.append((kind, axis, per))

    def _src(self, ins, t, chip):
        kind, axis, per = self.meta[t]
        return _axis_slice(ins[t], axis, chip * per, per) if kind == "scatter" else ins[t]

    def _dst(self, lands, t, chip, slot):
        kind, axis, per = self.meta[t]
        return _axis_slice(lands[t], axis, chip * per, per) if kind == "gather" else lands[t].at[slot]

    def landing_zones(self, me):
        return [_place_shard(self.name + "_own", src, axis, me) if kind == "gather" else lax.empty(ls.shape, ls.dtype)
                for src, ls, (kind, axis, _) in zip(self.srcs, self.land_shapes, self.meta)]

    def _copies(self, ins, lands, send_sems, recv_sems, arrivals):
        x, y, c = lax.axis_index("x"), lax.axis_index("y"), lax.axis_index("c")
        me = 2 * x + y
        peers = [(1 - x, y), (x, 1 - y), (1 - x, 1 - y)]
        res = []
        for t in range(self.n):
            for k, (px, py) in enumerate(peers):
                theirs = 2 * px + py
                sems = dict(send_sem=send_sems.at[t * N_PEERS + k], recv_sem=recv_sems.at[t * N_PEERS + k],
                            device_id=(px, py, c), device_id_type=MESH_IDS)
                if arrivals:
                    res.append(pltpu.make_async_remote_copy(src_ref=self._src(ins, t, me), dst_ref=self._dst(lands, t, theirs, k), **sems))
                else:
                    res.append(pltpu.make_async_remote_copy(src_ref=self._src(ins, t, theirs), dst_ref=self._dst(lands, t, me, k), **sems))
        return res

    def start(self, lands, deps=()):
        n, nd = self.n, len(deps)

        def body(*refs):
            ins, lnd = refs[:n], refs[n:2 * n]
            send_sems, recv_sems = refs[2 * n + nd], refs[2 * n + nd + 1]
            token = refs[-1]
            for cp in self._copies(ins, lnd, send_sems, recv_sems, arrivals=False):
                cp.start()
            token[...] = jnp.zeros_like(token)

        hbm = lambda a: pltpu.HBM(a.shape, a.dtype)
        res = pl.pallas_call(
            body, name=self.name + "_start",
            in_specs=[HBM_SPEC] * (2 * n) + [ANY_SPEC] * nd,
            out_specs=[SEM_SPEC, SEM_SPEC] + [HBM_SPEC] * (2 * n) + [pl.BlockSpec(memory_space=pltpu.VMEM)],
            out_shape=[pltpu.SemaphoreType.DMA((N_PEERS * n,)), pltpu.SemaphoreType.DMA((N_PEERS * n,))]
            + [hbm(a) for a in self.srcs] + [hbm(a) for a in self.land_shapes] + [jax.ShapeDtypeStruct((8, 128), F32)],
            input_output_aliases={i: 2 + i for i in range(2 * n)},
            compiler_params=pltpu.CompilerParams(has_side_effects=pltpu.SideEffectType.DATAFLOW_SIDE_EFFECTING),
        )(*[pltpu.with_memory_space_constraint(a, pltpu.HBM) for a in self.srcs],
          *[pltpu.with_memory_space_constraint(a, pltpu.HBM) for a in lands], *deps)
        return res[:-1], res[-1]

    def wait(self, state, after):
        n = self.n
        send_sems, recv_sems = state[0], state[1]
        srcs, lands = state[2:2 + n], state[2 + n:2 + 2 * n]

        def body(*refs):
            ins, lnd = refs[:n], refs[n:2 * n]
            s_sems, r_sems = refs[2 * n], refs[2 * n + 1]
            for cp in self._copies(ins, lnd, s_sems, r_sems, arrivals=True):
                cp.wait_recv()
            for cp in self._copies(ins, lnd, s_sems, r_sems, arrivals=False):
                cp.wait_send()

        hbm = lambda a: pltpu.HBM(a.shape, a.dtype)
        res = pl.pallas_call(
            body, name=self.name + "_wait",
            in_specs=[HBM_SPEC] * (2 * n) + [SEM_SPEC, SEM_SPEC, ANY_SPEC],
            out_specs=[HBM_SPEC] * (2 * n),
            out_shape=[hbm(a) for a in self.srcs] + [hbm(a) for a in self.land_shapes],
            input_output_aliases={i: i for i in range(2 * n)},
            compiler_params=pltpu.CompilerParams(has_side_effects=pltpu.SideEffectType.DATAFLOW_SIDE_EFFECTING),
        )(*srcs, *lands, send_sems, recv_sems, after)
        return res[:n], res[n:]


def _sibling_swap(name, arrays):
    n = len(arrays)

    def body(*refs):
        ins, outs = refs[:n], refs[n:2 * n]
        send_sems, recv_sems = refs[2 * n:]
        sibling = (lax.axis_index("x"), lax.axis_index("y"), 1 - lax.axis_index("c"))
        copies = [pltpu.make_async_remote_copy(src_ref=ins[t], dst_ref=outs[t], send_sem=send_sems.at[t], recv_sem=recv_sems.at[t],
                                               device_id=sibling, device_id_type=MESH_IDS) for t in range(n)]
        for cp in copies:
            cp.start()
        for cp in copies:
            cp.wait()

    any_spec = pl.BlockSpec(memory_space=pl.ANY)
    return pl.pallas_call(
        body,
        name=name,
        in_specs=[any_spec] * n,
        out_specs=[any_spec] * n,
        out_shape=[jax.ShapeDtypeStruct(a.shape, a.dtype) for a in arrays],
        scratch_shapes=[pltpu.SemaphoreType.DMA((n,)), pltpu.SemaphoreType.DMA((n,))],
        compiler_params=pltpu.CompilerParams(has_side_effects=True),
    )(*arrays)


def _row_block(r, c, itemsize=4, target=1 << 20):
    if r % 8 != 0:
        return r
    best = 8
    for tr in range(8, r + 1, 8):
        if r % tr == 0 and tr * c * itemsize <= target:
            best = tr
    return best


def _sum_chips_into(parts, stacked, layer):
    _, r, c = parts.shape
    tr = _row_block(r, c)

    def body(p_ref, s_ref, o_ref):
        del s_ref
        o_ref[...] = ((p_ref[0] + p_ref[1]) + p_ref[2]) + p_ref[3]

    return pl.pallas_call(
        body,
        name="sum_chips",
        grid=(r // tr,),
        in_specs=[pl.BlockSpec((N_CHIPS, tr, c), lambda i: (0, i, 0)), pl.BlockSpec(memory_space=pl.ANY)],
        out_specs=pl.BlockSpec((None, tr, c), lambda i: (layer, i, 0)),
        out_shape=jax.ShapeDtypeStruct(stacked.shape, stacked.dtype),
        input_output_aliases={1: 0},
        compiler_params=_cparams(("parallel",)),
    )(parts, stacked)


def _sum_own_and_peers(me, g, axis, landed):
    _, r, c = landed.shape
    tr = _row_block(r, c)

    def body(me_ref, g_ref, p_ref, o_ref):
        del me_ref
        o_ref[...] = ((g_ref[...].astype(F32) + p_ref[0].astype(F32)) + p_ref[1].astype(F32)) + p_ref[2].astype(F32)

    return pl.pallas_call(
        body, name="sum_chips_own",
        grid_spec=pltpu.PrefetchScalarGridSpec(
            num_scalar_prefetch=1, grid=(r // tr,),
            in_specs=[_own_block_spec(r, c, axis, tr), pl.BlockSpec((N_PEERS, tr, c), lambda i, me: (0, i, 0))],
            out_specs=pl.BlockSpec((tr, c), lambda i, me: (i, 0))),
        out_shape=jax.ShapeDtypeStruct((r, c), F32),
        compiler_params=_cparams(("parallel",)),
    )(me, g, landed)


def _adamw_math(w, m, v, g):
    m_new = ADAM_B1 * m + (1.0 - ADAM_B1) * g
    v_new = ADAM_B2 * v + (1.0 - ADAM_B2) * jnp.square(g)
    m_hat = m_new / (1.0 - ADAM_B1 ** ADAM_STEP)
    v_hat = v_new / (1.0 - ADAM_B2 ** ADAM_STEP)
    return -ADAM_LR * (m_hat / (jnp.sqrt(v_hat) + ADAM_EPS) + ADAM_WD * w), m_new, v_new


def _adamw(w, m, v, g_a, g_b):
    L, r, c = w.shape
    tr = _row_block(r, c, target=1 << 19)

    def body(w_ref, m_ref, v_ref, ga_ref, gb_ref, g_ref, d_ref, nm_ref, nv_ref):
        g = ga_ref[...] + gb_ref[...]
        g_ref[...] = g
        d_ref[...], nm_ref[...], nv_ref[...] = _adamw_math(w_ref[...], m_ref[...], v_ref[...], g)

    spec = pl.BlockSpec((None, tr, c), lambda l, i: (l, i, 0))
    return pl.pallas_call(
        body,
        name="adamw",
        grid=(L, r // tr),
        in_specs=[spec] * 5,
        out_specs=[spec] * 4,
        out_shape=[jax.ShapeDtypeStruct(w.shape, F32)] * 4,
        compiler_params=_cparams(("parallel", "parallel")),
    )(w, m, v, g_a, g_b)


def _adamw_layer(w, m, v, g_a, g_b, layer, outs):
    L, r, c = w.shape
    tr = _row_block(r, c, target=1 << 19)
    n_prev = 0 if outs is None else 4

    def body(w_ref, m_ref, v_ref, ga_ref, gb_ref, *rest):
        g_ref, d_ref, nm_ref, nv_ref = rest[n_prev:]
        g = ga_ref[...] + gb_ref[...]
        g_ref[...] = g
        d_ref[...], nm_ref[...], nv_ref[...] = _adamw_math(w_ref[...], m_ref[...], v_ref[...], g)

    at_layer = pl.BlockSpec((None, tr, c), lambda i: (layer, i, 0))
    flat = pl.BlockSpec((tr, c), lambda i: (i, 0))
    return pl.pallas_call(
        body,
        name="adamw_layer",
        grid=(r // tr,),
        in_specs=[at_layer] * 3 + [flat] * 2 + [ANY_SPEC] * n_prev,
        out_specs=[at_layer] * 4,
        out_shape=[jax.ShapeDtypeStruct(w.shape, F32)] * 4,
        input_output_aliases={5 + k: k for k in range(n_prev)},
        compiler_params=_cparams(("parallel",)),
    )(w, m, v, g_a, g_b, *(outs or ()))


SHARDED = (("w_in", 1), ("conv_w", 1), ("w_mem_k", 0), ("w_mem_v", 0), ("w_branch", 1), ("w_o", 0), ("w_up", 1), ("w_down", 0))
SMALL = ("lower_bounds", "hg_norm_w", "b_gate", "ln1_g", "ln1_b", "ln2_g", "ln2_b")
WEIGHT_ORDER = ("lower_bounds", "w_in", "conv_w", "hg_norm_w", "w_mem_k", "w_mem_v", "w_branch", "b_gate", "w_o", "ln1_g", "ln1_b",
                "w_up", "w_down", "ln2_g", "ln2_b")


def kernel(x, mem, lower_bounds, w_in, conv_w, hg_norm_w, w_mem_k, w_mem_v, w_branch, b_gate, w_o, ln1_g, ln1_b, w_up, w_down, ln2_g, ln2_b, loss_target, m_lower_bounds, m_w_in, m_conv_w, m_hg_norm_w, m_w_mem_k, m_w_mem_v, m_w_branch, m_b_gate, m_w_o, m_ln1_g, m_ln1_b, m_w_up, m_w_down, m_ln2_g, m_ln2_b, v_lower_bounds, v_w_in, v_conv_w, v_hg_norm_w, v_w_mem_k, v_w_mem_v, v_w_branch, v_b_gate, v_w_o, v_ln1_g, v_ln1_b, v_w_up, v_w_down, v_ln2_g, v_ln2_b):
    bl, seq, d = x.shape
    depth = w_in.shape[0]
    weights = dict(lower_bounds=lower_bounds, w_in=w_in, conv_w=conv_w, hg_norm_w=hg_norm_w, w_mem_k=w_mem_k, w_mem_v=w_mem_v,
                   w_branch=w_branch, b_gate=b_gate, w_o=w_o, ln1_g=ln1_g, ln1_b=ln1_b, w_up=w_up, w_down=w_down, ln2_g=ln2_g, ln2_b=ln2_b)
    mom_m = dict(lower_bounds=m_lower_bounds, w_in=m_w_in, conv_w=m_conv_w, hg_norm_w=m_hg_norm_w, w_mem_k=m_w_mem_k, w_mem_v=m_w_mem_v,
                 w_branch=m_w_branch, b_gate=m_b_gate, w_o=m_w_o, ln1_g=m_ln1_g, ln1_b=m_ln1_b, w_up=m_w_up, w_down=m_w_down,
                 ln2_g=m_ln2_g, ln2_b=m_ln2_b)
    mom_v = dict(lower_bounds=v_lower_bounds, w_in=v_w_in, conv_w=v_conv_w, hg_norm_w=v_hg_norm_w, w_mem_k=v_w_mem_k, w_mem_v=v_w_mem_v,
                 w_branch=v_w_branch, b_gate=v_b_gate, w_o=v_w_o, ln1_g=v_ln1_g, ln1_b=v_ln1_b, w_up=v_w_up, w_down=v_w_down,
                 ln2_g=v_ln2_g, ln2_b=v_ln2_b)

    def shard2d(name, l):
        w = weights[name][l]
        if name == "w_branch":
            return w.reshape(N_BRANCH * W, w.shape[-1]).astype(BF16)
        return w if name == "conv_w" else w.astype(BF16)

    me = _my_chip()

    shard_axis = dict(SHARDED)

    def start_exchange(name, kind, items, deps=()):
        ex = _Split(name, [(arr, kind, shard_axis[nm]) for nm, arr in items])
        state, token = ex.start(ex.landing_zones(me), deps)
        return ex, state, [nm for nm, _ in items], token

    def start_gathers(l, deps=()):
        first = start_exchange(f"gather_in_l{l}", "gather", [("w_in", shard2d("w_in", l))], deps)
        rest = start_exchange(f"gather_rest_l{l}", "gather", [(nm, shard2d(nm, l)) for nm, _ in SHARDED if nm != "w_in"],
                              (first[3],))
        return first, rest

    def gathered(pend, after):
        ex, state, names, _ = pend
        return dict(zip(names, ex.wait(state, after=after)[1]))

    x2d, mem2, t2d = x.reshape(bl * seq, d), mem.reshape(-1, d), loss_target.reshape(bl * seq, d)
    alpha = (2.0 * depth) ** 0.25
    soft, lb_all = _lower_bounds_fwd(lower_bounds)

    h, hb, saved, layer_wts = x2d, x2d.astype(BF16), [], []
    pending = start_gathers(0)
    for l in range(depth):
        first, rest = pending
        w_in_l = gathered(first, h)["w_in"]

        def rest_fn(after, l=l, rest=rest):
            wts = gathered(rest, after)
            for name in ("hg_norm_w", "b_gate", "ln1_g", "ln1_b", "ln2_g", "ln2_b"):
                wts[name] = weights[name][l][None, :]
            return wts

        deps = (rest[3],)
        if l + 1 < depth:
            pending = start_gathers(l + 1, (w_in_l, rest[3]))
            deps += (pending[0][3], pending[1][3])
        h, hb, sv, wts = _layer_fwd(h, hb, mem2, lb_all[l:l + 1], w_in_l, rest_fn, bl=bl, seq=seq, alpha=alpha, deps=deps)
        saved.append(sv)
        layer_wts.append(wts)
    loss, dh = _loss_head(h, t2d)

    shape3 = {name: (depth, weights[name].size // (depth * weights[name].shape[-1]), weights[name].shape[-1]) for name, _ in SHARDED}
    partial = [dict() for _ in range(depth)]
    smalls = [None] * depth
    outs = {name: None for name, _ in SHARDED}

    def finish_reduce(pend, l, after):
        ex, state, names, _ = pend
        sent, got = ex.wait(state, after=after)
        for nm, g_full, landed in zip(names, sent, got):
            partial[l][nm] = _sum_own_and_peers(me, g_full, shard_axis[nm], landed)

    def optimizer_step(l):
        names = [name for name, _ in SHARDED]
        theirs = _sibling_swap(f"swap_partials_l{l}", [partial[l][nm] for nm in names])
        for nm, other in zip(names, theirs):
            outs[nm] = _adamw_layer(weights[nm].reshape(shape3[nm]), mom_m[nm].reshape(shape3[nm]), mom_v[nm].reshape(shape3[nm]),
                                    partial[l][nm], other, l, outs[nm])
        return tuple(outs[nm][0] for nm in names)

    pending_mix, deps = [], ()
    for l in reversed(range(depth)):
        dx1, g_mlp = _mlp_bwd(dh, saved[l], layer_wts[l], alpha=alpha, deps=deps)
        pending_mlp = start_exchange(f"reduce_mlp_l{l}", "scatter", [(nm, g_mlp[nm]) for nm in ("w_up", "w_down")])
        deps = (pending_mlp[3],)
        if pending_mix:
            for pend in pending_mix:
                finish_reduce(pend, l + 1, dx1)
            deps += optimizer_step(l + 1)
        pending_mix = []

        def send(names, g, l=l, pending_mix=pending_mix):
            pend = start_exchange(f"reduce_{names[0]}_l{l}", "scatter", [(nm, g[nm]) for nm in names])
            pending_mix.append(pend)
            return pend[3]

        dh, g = _mix_bwd(dx1, saved[l], mem2, lb_all[l:l + 1], layer_wts[l], bl=bl, seq=seq, alpha=alpha, send=send, deps=deps)
        finish_reduce(pending_mlp, l, dh)
        deps = ()
        g.update(g_mlp, lower_bounds=g["lb"])
        smalls[l] = jnp.concatenate([g[nm] for nm in SMALL], axis=1)
    small_parts = _chip_exchange("reduce_small", [(jnp.stack(smalls), "bcast", 0)])[0]
    small_sum = _sum_chips_into(small_parts.reshape(N_CHIPS, depth, -1), jnp.zeros((1, depth, small_parts.shape[-1]), F32), 0)
    small_sum = small_sum.reshape(depth, 1, -1)
    small_theirs = _sibling_swap("swap_small", [small_sum])[0]
    for pend in pending_mix:
        finish_reduce(pend, 0, small_theirs)
    optimizer_step(0)

    outs = {name: [r.reshape(weights[name].shape) for r in res] for name, res in outs.items()}
    off = 0
    for name in SMALL:
        n = weights[name].shape[1]
        mine, other = small_sum[:, :, off:off + n], small_theirs[:, :, off:off + n]
        off += n
        if name == "lower_bounds":
            mine = _lower_bounds_bwd(soft, mine[:, 0, :])[:, None, :]
            other = _lower_bounds_bwd(soft, other[:, 0, :])[:, None, :]
        shp = (depth, 1, n)
        res = _adamw(weights[name].reshape(shp), mom_m[name].reshape(shp), mom_v[name].reshape(shp), mine, other)
        outs[name] = [r.reshape(weights[name].shape) for r in res]
    assert off == small_sum.shape[-1]

    total_loss = lax.psum(loss[0, 0], ("x", "y", "c"))
    result = [total_loss, dh.reshape(bl, seq, d)]
    for k in range(4):
        result += [outs[name][k] for name in WEIGHT_ORDER]
    return tuple(result)
```

```python
import functools

import jax
import jax.numpy as jnp
from jax import lax
from jax.experimental import pallas as pl
from jax.experimental.pallas import tpu as pltpu

F32 = jnp.float32
BF16 = jnp.bfloat16

HG_HEADS = 4
HG_F = 128
HG_CHUNK = 32
MEM_HEADS = 4
MEM_HEAD_DIM = 128
BRANCH_WIDTH = 512
N_BRANCH = 3
CONV_K = 3
LN_EPS = 1e-5
RMS_EPS = 1e-6
ADAM_LR = 0.001
ADAM_B1 = 0.9
ADAM_B2 = 0.999
ADAM_EPS = 1e-08
ADAM_WD = 0.01
ADAM_STEP = 10

VMEM_LIMIT = 48 * 1024 * 1024


def _cparams(sem):
    return pltpu.CompilerParams(dimension_semantics=sem, vmem_limit_bytes=VMEM_LIMIT)


def _dot(a, b, dims):
    return lax.dot_general(a, b, (dims, ((), ())), preferred_element_type=F32)


NN = ((1,), (0,))
NT = ((1,), (1,))
TN = ((0,), (0,))


def _pick(n, pref):
    for t in pref:
        if n % t == 0:
            return t
    return n


ANY_SPEC = pl.BlockSpec(memory_space=pl.ANY)


def _matmul(name, a, b, *, mode, out_dtype=F32, a_fn=None, a_extra=(), epi_fn=None, epi_extra=(), n_out=1, out_kinds=None,
            tm=512, tn=1024, tk=1024, deps=()):
    M, K = a.shape
    N = b.shape[1] if mode == "nn" else b.shape[0]
    tm, tn, tk = _pick(M, (tm, 256, 128, 8)), _pick(N, (tn, 896, 512, 256, 128)), _pick(K, (tk, 512, 256, 128))
    nk = K // tk
    n_ax, n_ex = len(a_extra), len(epi_extra)
    n_in = 2 + n_ax + n_ex + len(deps)
    out_dtypes = out_dtype if isinstance(out_dtype, (tuple, list)) else (out_dtype,) * n_out
    out_kinds = out_kinds or ("tile",) * n_out

    def body(*refs):
        a_ref, b_ref = refs[0], refs[1]
        ax_refs = refs[2:2 + n_ax]
        ex_refs = refs[2 + n_ax:2 + n_ax + n_ex]
        o_refs = refs[n_in:n_in + n_out]
        at = a_ref[...]
        at = a_fn(at, *[r[...] for r in ax_refs]) if a_fn is not None else at.astype(BF16)
        part = _dot(at, b_ref[...].astype(BF16), NN if mode == "nn" else NT)

        def finish(acc):
            outs = epi_fn(acc, *[r[...] for r in ex_refs]) if epi_fn is not None else (acc,)
            for o_ref, o, kind in zip(o_refs, outs, out_kinds):
                if kind == "rowsum":
                    @pl.when(pl.program_id(1) == 0)
                    def _(o_ref=o_ref):
                        o_ref[...] = jnp.zeros_like(o_ref)

                    o_ref[0:1, :] += o
                else:
                    o_ref[...] = o.astype(o_ref.dtype)

        if nk == 1:
            finish(part)
            return
        acc_ref = refs[-1]
        k = pl.program_id(2)

        @pl.when(k == 0)
        def _():
            acc_ref[...] = part

        @pl.when(jnp.logical_and(k > 0, k < nk - 1))
        def _():
            acc_ref[...] += part

        @pl.when(k == nk - 1)
        def _():
            finish(acc_ref[...] + part)

    in_specs = [pl.BlockSpec((tm, tk), lambda j, i, k: (i, k)),
                pl.BlockSpec((tk, tn), lambda j, i, k: (k, j)) if mode == "nn" else pl.BlockSpec((tn, tk), lambda j, i, k: (j, k))]
    in_specs += [pl.BlockSpec((1, tk), lambda j, i, k: (0, k)) for _ in a_extra]
    for e in epi_extra:
        if e.shape[0] == 1:
            in_specs.append(pl.BlockSpec((1, tn), lambda j, i, k: (0, j)))
        elif e.shape[1] == 1:
            in_specs.append(pl.BlockSpec((tm, 1), lambda j, i, k: (i, 0)))
        else:
            in_specs.append(pl.BlockSpec((tm, tn), lambda j, i, k: (i, j)))
    in_specs += [ANY_SPEC] * len(deps)
    out_specs, out_shapes = [], []
    for kind, dt in zip(out_kinds, out_dtypes):
        if kind == "col":
            out_specs.append(pl.BlockSpec((tm, 1), lambda j, i, k: (i, 0)))
            out_shapes.append(jax.ShapeDtypeStruct((M, 1), dt))
        elif kind == "rowsum":
            out_specs.append(pl.BlockSpec((8, tn), lambda j, i, k: (0, j)))
            out_shapes.append(jax.ShapeDtypeStruct((8, N), dt))
        else:
            out_specs.append(pl.BlockSpec((tm, tn), lambda j, i, k: (i, j)))
            out_shapes.append(jax.ShapeDtypeStruct((M, N), dt))
    out = pl.pallas_call(
        body,
        name=name,
        grid=(N // tn, M // tm, nk),
        in_specs=in_specs,
        out_specs=out_specs,
        out_shape=out_shapes,
        scratch_shapes=[pltpu.VMEM((tm, tn), F32)] if nk > 1 else [],
        compiler_params=_cparams(("arbitrary", "arbitrary", "arbitrary")),
    )(a, b, *a_extra, *epi_extra, *deps)
    return out[0] if n_out == 1 else out


def _matmul_tn(name, a, b, *, a_fn=None, a_extra=(), a_cols=None, b_cols=None, ta=1024, tb=1024, tt=1024, out_dtype=F32, deps=()):
    T = a.shape[0]
    a0, Ka = a_cols if a_cols is not None else (0, a.shape[1])
    b0, Nb = b_cols if b_cols is not None else (0, b.shape[1])
    ta, tb, tt = _pick(Ka, (ta, 512, 256, 128)), _pick(Nb, (tb, 896, 512, 256, 128)), _pick(T, (tt, 512, 256, 128))
    assert a0 % ta == 0 and b0 % tb == 0
    a0, b0 = a0 // ta, b0 // tb
    nt = T // tt
    n_ax = len(a_extra)

    def body(*refs):
        a_ref, b_ref = refs[0], refs[1]
        ax_refs = refs[2:2 + n_ax]
        o_ref = refs[2 + n_ax + len(deps)]
        acc_ref = refs[-1]
        t = pl.program_id(2)
        at = a_ref[...]
        at = a_fn(at, *[r[...] for r in ax_refs]) if a_fn is not None else at.astype(BF16)
        part = _dot(at, b_ref[...].astype(BF16), TN)

        @pl.when(t == 0)
        def _():
            acc_ref[...] = part

        @pl.when(jnp.logical_and(t > 0, t < nt - 1))
        def _():
            acc_ref[...] += part

        @pl.when(t == nt - 1)
        def _():
            o_ref[...] = (acc_ref[...] + part if nt > 1 else part).astype(o_ref.dtype)

    in_specs = [pl.BlockSpec((tt, ta), lambda i, j, t: (t, a0 + i)), pl.BlockSpec((tt, tb), lambda i, j, t: (t, b0 + j))]
    in_specs += [pl.BlockSpec((1, ta), lambda i, j, t: (0, a0 + i)) for _ in a_extra]
    in_specs += [ANY_SPEC] * len(deps)
    return pl.pallas_call(
        body,
        name=name,
        grid=(Ka // ta, Nb // tb, nt),
        in_specs=in_specs,
        out_specs=pl.BlockSpec((ta, tb), lambda i, j, t: (i, j)),
        out_shape=jax.ShapeDtypeStruct((Ka, Nb), out_dtype),
        scratch_shapes=[pltpu.VMEM((ta, tb), F32)],
        compiler_params=_cparams(("parallel", "parallel", "arbitrary")),
    )(a, b, *a_extra, *deps)


W = BRANCH_WIDTH
C_CB, C_CC, C_CH, C_HQ, C_HF, C_HI, C_HG, C_MQ, N_MIX = 0, W, 2 * W, 3 * W, 4 * W, 5 * W, 6 * W, 7 * W, 8 * W
TS_MIX = 256
PREV_ROWS = 16


def _sigmoid(x):
    return jax.nn.sigmoid(x)


def _chunk_pos(shape):
    return lax.broadcasted_iota(jnp.int32, shape, 0) & (HG_CHUNK - 1)


def _seg_cumsum(x, pos):
    sh = 1
    while sh < HG_CHUNK:
        x = x + jnp.where(pos >= sh, pltpu.roll(x, sh, 0), 0.0)
        sh *= 2
    return x


def _seg_rev_cumsum(x, pos):
    n = x.shape[0]
    sh = 1
    while sh < HG_CHUNK:
        x = x + jnp.where(pos < HG_CHUNK - sh, pltpu.roll(x, n - sh, 0), 0.0)
        sh *= 2
    return x


def _chunk_mask(ts):
    r = lax.broadcasted_iota(jnp.int32, (ts, ts), 0)
    c = lax.broadcasted_iota(jnp.int32, (ts, ts), 1)
    return jnp.logical_and((r // HG_CHUNK) == (c // HG_CHUNK), c <= r)


def _hgrn_gates(p_ref, lb):
    q = p_ref[:, C_HQ:C_HQ + W].astype(F32)
    fl = p_ref[:, C_HF:C_HF + W].astype(F32)
    sig = _sigmoid(fl)
    f = lb + (1.0 - lb) * sig
    logf = jnp.log(f)
    k = (1.0 - lb) * _sigmoid(-fl)
    sq = _sigmoid(q)
    qs = q * sq
    return q, sq, qs, sig, f, logf, k


def _hgrn_decays(logf, bc_sc, ts):
    pos = _chunk_pos(logf.shape)
    bc = _seg_cumsum(logf, pos)
    bc_sc[...] = bc
    nc = ts // HG_CHUNK
    bref = jnp.concatenate(
        [jnp.broadcast_to(bc_sc[n * HG_CHUNK + HG_CHUNK // 2 - 1:n * HG_CHUNK + HG_CHUNK // 2, :], (HG_CHUNK, W)) for n in range(nc)], axis=0)
    blast = jnp.concatenate(
        [jnp.broadcast_to(bc_sc[(n + 1) * HG_CHUNK - 1:(n + 1) * HG_CHUNK, :], (HG_CHUNK, W)) for n in range(nc)], axis=0)
    return pos, bc, bref, blast


def _conv_shift_down(u, carry_ref, row):
    n = carry_ref.shape[0]
    last, before = carry_ref[n - 1:n, :], carry_ref[n - 2:n - 1, :]
    u1 = jnp.where(row == 0, last, pltpu.roll(u, 1, 0))
    u2 = jnp.where(row == 0, before, jnp.where(row == 1, last, pltpu.roll(u, 2, 0)))
    return u1, u2


def _attn_probs(qh, kh):
    s = _dot(qh, kh, NT) * (MEM_HEAD_DIM ** -0.5)
    e = jnp.exp(s - jnp.max(s, axis=-1, keepdims=True))
    return e / jnp.sum(e, axis=-1, keepdims=True)


def _mixer_fwd(p, mk, mv, lb, conv_w, norm_w, *, bl, seq):
    T = p.shape[0]
    ts = TS_MIX
    ns = seq // ts
    nc = ts // HG_CHUNK
    ml = mk.shape[0] // bl

    def body(p_ref, mk_ref, mv_ref, lb_ref, cw_ref, nw_ref, y_ref, st_ref, opre_ref, state_sc, carry_sc, bc_sc):
        @pl.when(pl.program_id(1) == 0)
        def _():
            state_sc[...] = jnp.zeros_like(state_sc)
            carry_sc[...] = jnp.zeros_like(carry_sc)

        cb, cc, ch = (p_ref[:, c0:c0 + W].astype(F32) for c0 in (C_CB, C_CC, C_CH))
        u = cc * ch
        row = lax.broadcasted_iota(jnp.int32, (ts, W), 0)
        u1, u2 = _conv_shift_down(u, carry_sc, row)
        yconv = u2 * cw_ref[0:1, :] + u1 * cw_ref[1:2, :] + u * cw_ref[2:3, :]
        y_ref[:, 0:W] = (cb * yconv).astype(BF16)
        carry_sc[...] = u[ts - 8:ts, :]

        lbv = lb_ref[...]
        _, _, qs, _, _, logf, k = _hgrn_gates(p_ref, lbv)
        pos, bc, bref, blast = _hgrn_decays(logf, bc_sc, ts)
        a_all = (qs * jnp.exp(bc - bref)).astype(BF16)
        bk_all = (k * jnp.exp(bref - bc)).astype(BF16)
        qin_all = (qs * jnp.exp(bc)).astype(BF16)
        kout_all = (k * jnp.exp(blast - bc)).astype(BF16)
        v_all = p_ref[:, C_HI:C_HI + W].astype(BF16)
        mask = _chunk_mask(ts)
        for h in range(HG_HEADS):
            hs = slice(h * HG_F, (h + 1) * HG_F)
            vb = v_all[:, hs]
            scores = jnp.where(mask, _dot(a_all[:, hs], bk_all[:, hs], NT), 0.0)
            o_intra = _dot(scores.astype(BF16), vb, NN)
            st = state_sc[h]
            o_inter = []
            for n in range(nc):
                rows = slice(n * HG_CHUNK, (n + 1) * HG_CHUNK)
                st_ref[n, h] = st
                o_inter.append(_dot(qin_all[rows, hs], st.astype(BF16), NT))
                kv = _dot(vb[rows], kout_all[rows, hs], TN)
                decay = jnp.exp(bc_sc[(n + 1) * HG_CHUNK - 1:(n + 1) * HG_CHUNK, hs])
                st = st * decay + kv
            state_sc[h] = st
            o = o_intra + jnp.concatenate(o_inter, axis=0)
            opre_ref[:, hs] = o
            on = o * lax.rsqrt(jnp.mean(o * o, axis=-1, keepdims=True) + RMS_EPS) * nw_ref[...]
            g = p_ref[:, C_HG + h * HG_F:C_HG + (h + 1) * HG_F].astype(F32)
            y_ref[:, W + h * HG_F:W + (h + 1) * HG_F] = (on * (g * _sigmoid(g))).astype(BF16)

        for h in range(MEM_HEADS):
            hs = slice(h * MEM_HEAD_DIM, (h + 1) * MEM_HEAD_DIM)
            qh = p_ref[:, C_MQ + h * MEM_HEAD_DIM:C_MQ + (h + 1) * MEM_HEAD_DIM].astype(BF16)
            prob = _attn_probs(qh, mk_ref[:, hs])
            y_ref[:, 2 * W + h * MEM_HEAD_DIM:2 * W + (h + 1) * MEM_HEAD_DIM] = _dot(prob.astype(BF16), mv_ref[:, hs], NN).astype(BF16)

    return pl.pallas_call(
        body,
        name="mixer_fwd",
        grid=(bl, ns),
        in_specs=[
            pl.BlockSpec((ts, N_MIX), lambda b, s: (b * ns + s, 0)),
            pl.BlockSpec((ml, W), lambda b, s: (b, 0)),
            pl.BlockSpec((ml, W), lambda b, s: (b, 0)),
            pl.BlockSpec((1, W), lambda b, s: (0, 0)),
            pl.BlockSpec((CONV_K, W), lambda b, s: (0, 0)),
            pl.BlockSpec((1, HG_F), lambda b, s: (0, 0)),
        ],
        out_specs=[
            pl.BlockSpec((ts, 3 * W), lambda b, s: (b * ns + s, 0)),
            pl.BlockSpec((nc, HG_HEADS, HG_F, HG_F), lambda b, s: (b * ns + s, 0, 0, 0)),
            pl.BlockSpec((ts, W), lambda b, s: (b * ns + s, 0)),
        ],
        out_shape=[
            jax.ShapeDtypeStruct((T, 3 * W), BF16),
            jax.ShapeDtypeStruct((T // HG_CHUNK, HG_HEADS, HG_F, HG_F), F32),
            jax.ShapeDtypeStruct((T, W), F32),
        ],
        scratch_shapes=[pltpu.VMEM((HG_HEADS, HG_F, HG_F), F32), pltpu.VMEM((8, W), F32), pltpu.VMEM((ts, W), F32)],
        compiler_params=_cparams(("arbitrary", "arbitrary")),
    )(p, mk, mv, lb, conv_w, norm_w)


def _mixer_bwd(p, dy, dp_gates, st, opre, mk, mv, lb, conv_w, norm_w, *, bl, seq, deps=()):
    T, nin = p.shape
    ts = TS_MIX
    ns = seq // ts
    nc = ts // HG_CHUNK
    ml = mk.shape[0] // bl
    mid, last = HG_CHUNK // 2 - 1, HG_CHUNK - 1

    def body(p_ref, pprev_ref, dy_ref, dpin_ref, st_ref, opre_ref, mk_ref, mv_ref, lb_ref, cw_ref, nw_ref, *rest):
        dp_ref, dmk_ref, dmv_ref, dcw_ref, dnw_ref, dlb_ref, dstate_sc, carry_sc, uprev_sc, bc_sc = rest[len(deps):]
        del dpin_ref
        b, s = pl.program_id(0), pl.program_id(1)

        @pl.when(s == 0)
        def _():
            dstate_sc[...] = jnp.zeros_like(dstate_sc)
            carry_sc[...] = jnp.zeros_like(carry_sc)
            dmk_ref[...] = jnp.zeros_like(dmk_ref)
            dmv_ref[...] = jnp.zeros_like(dmv_ref)

        @pl.when(jnp.logical_and(b == 0, s == 0))
        def _():
            dcw_ref[...] = jnp.zeros_like(dcw_ref)
            dnw_ref[...] = jnp.zeros_like(dnw_ref)
            dlb_ref[...] = jnp.zeros_like(dlb_ref)

        cb, cc, ch = (p_ref[:, c0:c0 + W].astype(F32) for c0 in (C_CB, C_CC, C_CH))
        u = cc * ch
        row = lax.broadcasted_iota(jnp.int32, (ts, W), 0)
        uprev = pprev_ref[:, C_CC:C_CC + W].astype(F32) * pprev_ref[:, C_CH:C_CH + W].astype(F32)
        uprev_sc[...] = jnp.where(s == ns - 1, 0.0, uprev)
        u1, u2 = _conv_shift_down(u, uprev_sc, row)
        w0, w1, w2 = cw_ref[0:1, :], cw_ref[1:2, :], cw_ref[2:3, :]
        dya = dy_ref[:, 0:W].astype(F32)
        dp_ref[:, C_CB:C_CB + W] = (dya * (u2 * w0 + u1 * w1 + u * w2)).astype(BF16)
        dv = cb * dya
        dv1 = jnp.where(row == ts - 1, carry_sc[0:1, :], pltpu.roll(dv, ts - 1, 0))
        dv2 = jnp.where(row == ts - 1, carry_sc[1:2, :], jnp.where(row == ts - 2, carry_sc[0:1, :], pltpu.roll(dv, ts - 2, 0)))
        du = dv * w2 + dv1 * w1 + dv2 * w0
        dp_ref[:, C_CC:C_CC + W] = (du * ch).astype(BF16)
        dp_ref[:, C_CH:C_CH + W] = (du * cc).astype(BF16)
        dcw_ref[0:1, :] += jnp.sum(dv * u2, axis=0, keepdims=True)
        dcw_ref[1:2, :] += jnp.sum(dv * u1, axis=0, keepdims=True)
        dcw_ref[2:3, :] += jnp.sum(dv * u, axis=0, keepdims=True)
        carry_sc[...] = dv[0:8, :]

        lbv = lb_ref[...]
        q_all, sq_all, qs_all, sig_all, f_all, logf, k_all = _hgrn_gates(p_ref, lbv)
        pos_all, bc, bref, blast = _hgrn_decays(logf, bc_sc, ts)
        ea_all, eb_all, eq_all, ek_all = jnp.exp(bc - bref), jnp.exp(bref - bc), jnp.exp(bc), jnp.exp(blast - bc)
        mask = _chunk_mask(ts)
        pos = _chunk_pos((ts, HG_F))
        pos_c = _chunk_pos((HG_CHUNK, HG_F))
        nw = nw_ref[...]
        for h in range(HG_HEADS):
            hs = slice(h * HG_F, (h + 1) * HG_F)
            qs, k, ea, eb, eq, ek = qs_all[:, hs], k_all[:, hs], ea_all[:, hs], eb_all[:, hs], eq_all[:, hs], ek_all[:, hs]
            a, bk, qin, kout = qs * ea, k * eb, qs * eq, k * ek
            o = opre_ref[:, hs]
            g = p_ref[:, C_HG + h * HG_F:C_HG + (h + 1) * HG_F].astype(F32)
            sg = _sigmoid(g)
            r = lax.rsqrt(jnp.mean(o * o, axis=-1, keepdims=True) + RMS_EPS)
            dyb = dy_ref[:, W + h * HG_F:W + (h + 1) * HG_F].astype(F32)
            dp_ref[:, C_HG + h * HG_F:C_HG + (h + 1) * HG_F] = (dyb * (o * r * nw) * (sg * (1.0 + g * (1.0 - sg)))).astype(BF16)
            don = dyb * (g * sg)
            dnw_ref[0:1, :] += jnp.sum(don * o * r, axis=0, keepdims=True)
            dn = don * nw
            do = r * (dn - o * (r * r) * jnp.mean(dn * o, axis=-1, keepdims=True))
            dob = do.astype(BF16)
            vb = p_ref[:, C_HI + h * HG_F:C_HI + (h + 1) * HG_F].astype(BF16)
            ab, bkb = a.astype(BF16), bk.astype(BF16)
            scores = jnp.where(mask, _dot(ab, bkb, NT), 0.0)
            dscores = jnp.where(mask, _dot(dob, vb, NT), 0.0).astype(BF16)
            dv_h = _dot(scores.astype(BF16), dob, TN)
            da = _dot(dscores, bkb, NN)
            dbk = _dot(dscores, ab, TN)
            koutb, qinb = kout.astype(BF16), qin.astype(BF16)
            dst = dstate_sc[h]
            dqin_p, dkout_p, dvi_p, ddec_p = [None] * nc, [None] * nc, [None] * nc, [None] * nc
            for n in reversed(range(nc)):
                rows = slice(n * HG_CHUNK, (n + 1) * HG_CHUNK)
                st_n = st_ref[n, h]
                decay = jnp.exp(bc_sc[n * HG_CHUNK + last:n * HG_CHUNK + last + 1, hs])
                dstb = dst.astype(BF16)
                dvi_p[n] = _dot(koutb[rows], dstb, NT)
                dkout_p[n] = _dot(vb[rows], dstb, NN)
                ddec_p[n] = jnp.sum(dst * st_n, axis=0, keepdims=True) * decay
                dqin_p[n] = _dot(dob[rows], st_n.astype(BF16), NN)
                dst = dst * decay + _dot(dob[rows], qinb[rows], TN)
            dstate_sc[h] = dst
            dqin = jnp.concatenate(dqin_p, axis=0)
            dkout = jnp.concatenate(dkout_p, axis=0)
            dp_ref[:, C_HI + h * HG_F:C_HI + (h + 1) * HG_F] = (dv_h + jnp.concatenate(dvi_p, axis=0)).astype(BF16)
            dqs = da * ea + dqin * eq
            dk = dbk * eb + dkout * ek
            t_a, t_b, t_q, t_k = da * a, dbk * bk, dqin * qin, dkout * kout
            dbc = t_a - t_b + t_q - t_k
            t_ref = t_b - t_a
            pieces = []
            for n in range(nc):
                rows = slice(n * HG_CHUNK, (n + 1) * HG_CHUNK)
                s_ref = jnp.sum(t_ref[rows], axis=0, keepdims=True)
                s_last = jnp.sum(t_k[rows], axis=0, keepdims=True) + ddec_p[n]
                pieces.append(dbc[rows] + jnp.where(pos_c == mid, s_ref, 0.0) + jnp.where(pos_c == last, s_last, 0.0))
            dlogf = _seg_rev_cumsum(jnp.concatenate(pieces, axis=0), pos)
            sig, lbh = sig_all[:, hs], lbv[:, hs]
            dfk = dlogf / f_all[:, hs] - dk
            dp_ref[:, C_HF + h * HG_F:C_HF + (h + 1) * HG_F] = (dfk * (1.0 - lbh) * sig * (1.0 - sig)).astype(BF16)
            dlb_ref[0:1, hs] += jnp.sum(dfk * (1.0 - sig), axis=0, keepdims=True)
            q, sq = q_all[:, hs], sq_all[:, hs]
            dp_ref[:, C_HQ + h * HG_F:C_HQ + (h + 1) * HG_F] = (dqs * (sq * (1.0 + q * (1.0 - sq)))).astype(BF16)

        for h in range(MEM_HEADS):
            hs = slice(h * MEM_HEAD_DIM, (h + 1) * MEM_HEAD_DIM)
            qh = p_ref[:, C_MQ + h * MEM_HEAD_DIM:C_MQ + (h + 1) * MEM_HEAD_DIM].astype(BF16)
            kh, vh = mk_ref[:, hs], mv_ref[:, hs]
            prob = _attn_probs(qh, kh)
            dob = dy_ref[:, 2 * W + h * MEM_HEAD_DIM:2 * W + (h + 1) * MEM_HEAD_DIM].astype(BF16)
            dmv_ref[:, hs] += _dot(prob.astype(BF16), dob, TN)
            dprob = _dot(dob, vh, NT)
            ds = prob * (dprob - jnp.sum(dprob * prob, axis=-1, keepdims=True)) * (MEM_HEAD_DIM ** -0.5)
            dsb = ds.astype(BF16)
            dp_ref[:, C_MQ + h * MEM_HEAD_DIM:C_MQ + (h + 1) * MEM_HEAD_DIM] = _dot(dsb, kh, NN).astype(BF16)
            dmk_ref[:, hs] += _dot(dsb, qh, TN)

    def tile(b, s):
        return b * ns + (ns - 1 - s)

    return pl.pallas_call(
        body,
        name="mixer_bwd",
        grid=(bl, ns),
        in_specs=[
            pl.BlockSpec((ts, N_MIX), lambda b, s: (tile(b, s), 0)),
            pl.BlockSpec((PREV_ROWS, N_MIX), lambda b, s: (jnp.maximum(tile(b, s) * (ts // PREV_ROWS) - 1, 0), 0)),
            pl.BlockSpec((ts, 3 * W), lambda b, s: (tile(b, s), 0)),
            pl.BlockSpec(memory_space=pl.ANY),
            pl.BlockSpec((nc, HG_HEADS, HG_F, HG_F), lambda b, s: (tile(b, s), 0, 0, 0)),
            pl.BlockSpec((ts, W), lambda b, s: (tile(b, s), 0)),
            pl.BlockSpec((ml, W), lambda b, s: (b, 0)),
            pl.BlockSpec((ml, W), lambda b, s: (b, 0)),
            pl.BlockSpec((1, W), lambda b, s: (0, 0)),
            pl.BlockSpec((CONV_K, W), lambda b, s: (0, 0)),
            pl.BlockSpec((1, HG_F), lambda b, s: (0, 0)),
        ] + [ANY_SPEC] * len(deps),
        out_specs=[
            pl.BlockSpec((ts, N_MIX), lambda b, s: (tile(b, s), 0)),
            pl.BlockSpec((ml, W), lambda b, s: (b, 0)),
            pl.BlockSpec((ml, W), lambda b, s: (b, 0)),
            pl.BlockSpec((8, W), lambda b, s: (0, 0)),
            pl.BlockSpec((8, HG_F), lambda b, s: (0, 0)),
            pl.BlockSpec((8, W), lambda b, s: (0, 0)),
        ],
        out_shape=[
            jax.ShapeDtypeStruct((T, nin), BF16),
            jax.ShapeDtypeStruct((bl * ml, W), F32),
            jax.ShapeDtypeStruct((bl * ml, W), F32),
            jax.ShapeDtypeStruct((8, W), F32),
            jax.ShapeDtypeStruct((8, HG_F), F32),
            jax.ShapeDtypeStruct((8, W), F32),
        ],
        input_output_aliases={3: 0},
        scratch_shapes=[pltpu.VMEM((HG_HEADS, HG_F, HG_F), F32), pltpu.VMEM((8, W), F32), pltpu.VMEM((PREV_ROWS, W), F32),
                        pltpu.VMEM((ts, W), F32)],
        compiler_params=_cparams(("arbitrary", "arbitrary")),
    )(p, p, dy, dp_gates, st, opre, mk, mv, lb, conv_w, norm_w, *deps)


def _layer_norm_stats(z):
    mu = jnp.mean(z, axis=-1, keepdims=True)
    zc = z - mu
    rstd = lax.rsqrt(jnp.mean(zc * zc, axis=-1, keepdims=True) + LN_EPS)
    return zc * rstd, rstd


def _gate_specs(tm, d):
    g0 = N_MIX // d
    return [pl.BlockSpec((tm, d), functools.partial(lambda i, k: (i, g0 + k), k=k)) for k in range(N_BRANCH)]


def _merge_fwd(y, p, x0, wb, wo, bg, ln_g, ln_b, *, alpha, tm=256):
    T, d = x0.shape
    assert N_MIX % d == 0
    tm = _pick(T, (tm, 128, 8))

    def body(y_ref, g0_ref, g1_ref, g2_ref, x_ref, wb_ref, wo_ref, bg_ref, lg_ref, lb_ref, r_ref, mg_ref, xh_ref, rs_ref, x1b_ref):
        merged = None
        for i, g_ref in enumerate((g0_ref, g1_ref, g2_ref)):
            r = _dot(y_ref[:, i * W:(i + 1) * W], wb_ref[i * W:(i + 1) * W, :], NN)
            r_ref[:, i * d:(i + 1) * d] = r.astype(BF16)
            t = _sigmoid(g_ref[...].astype(F32) + bg_ref[:, i * d:(i + 1) * d]) * r
            merged = t if merged is None else merged + t
        mb = merged.astype(BF16)
        mg_ref[...] = mb
        z = alpha * x_ref[...] + _dot(mb, wo_ref[...], NN)
        xh, rs = _layer_norm_stats(z)
        xh_ref[...], rs_ref[...] = xh, rs
        x1b_ref[...] = (xh * lg_ref[...] + lb_ref[...]).astype(BF16)

    row = lambda i: (i, 0)
    fix = lambda i: (0, 0)
    return pl.pallas_call(
        body,
        name="merge_fwd",
        grid=(T // tm,),
        in_specs=[pl.BlockSpec((tm, 3 * W), row)] + _gate_specs(tm, d) + [
            pl.BlockSpec((tm, d), row), pl.BlockSpec((3 * W, d), fix), pl.BlockSpec((d, d), fix), pl.BlockSpec((1, 3 * d), fix),
            pl.BlockSpec((1, d), fix), pl.BlockSpec((1, d), fix)],
        out_specs=[pl.BlockSpec((tm, 3 * d), row), pl.BlockSpec((tm, d), row), pl.BlockSpec((tm, d), row), pl.BlockSpec((tm, 1), row),
                   pl.BlockSpec((tm, d), row)],
        out_shape=[jax.ShapeDtypeStruct((T, 3 * d), BF16), jax.ShapeDtypeStruct((T, d), BF16),
                   jax.ShapeDtypeStruct((T, d), F32), jax.ShapeDtypeStruct((T, 1), F32), jax.ShapeDtypeStruct((T, d), BF16)],
        compiler_params=_cparams(("parallel",)),
    )(y, p, p, p, x0, wb, wo, bg, ln_g, ln_b)


def _merge_bwd(dz, p, r, wb, wo, bg, *, tm=256):
    T, d = dz.shape
    nin = p.shape[1]
    tm = _pick(T, (tm, 128, 8))

    def body(dz_ref, g0_ref, g1_ref, g2_ref, r_ref, wb_ref, wo_ref, bg_ref, dr_ref, dp_ref, dy_ref, dbg_ref):
        @pl.when(pl.program_id(0) == 0)
        def _():
            dbg_ref[...] = jnp.zeros_like(dbg_ref)

        dmerged = _dot(dz_ref[...].astype(BF16), wo_ref[...], NT)
        dp_ref[:, 0:N_MIX] = jnp.zeros((tm, N_MIX), BF16)
        for i, g_ref in enumerate((g0_ref, g1_ref, g2_ref)):
            cs = slice(i * d, (i + 1) * d)
            s = _sigmoid(g_ref[...].astype(F32) + bg_ref[:, cs])
            drb = (dmerged * s).astype(BF16)
            dr_ref[:, cs] = drb
            dgate = dmerged * r_ref[:, cs].astype(F32) * s * (1.0 - s)
            dp_ref[:, N_MIX + i * d:N_MIX + (i + 1) * d] = dgate.astype(BF16)
            dbg_ref[0:1, cs] += jnp.sum(dgate, axis=0, keepdims=True)
            dy_ref[:, i * W:(i + 1) * W] = _dot(drb, wb_ref[i * W:(i + 1) * W, :], NT).astype(BF16)

    row = lambda i: (i, 0)
    fix = lambda i: (0, 0)
    return pl.pallas_call(
        body,
        name="merge_bwd",
        grid=(T // tm,),
        in_specs=[pl.BlockSpec((tm, d), row)] + _gate_specs(tm, d) + [
            pl.BlockSpec((tm, 3 * d), row), pl.BlockSpec((3 * W, d), fix), pl.BlockSpec((d, d), fix), pl.BlockSpec((1, 3 * d), fix)],
        out_specs=[pl.BlockSpec((tm, 3 * d), row), pl.BlockSpec((tm, nin), row), pl.BlockSpec((tm, 3 * W), row),
                   pl.BlockSpec((8, 3 * d), fix)],
        out_shape=[jax.ShapeDtypeStruct((T, 3 * d), BF16), jax.ShapeDtypeStruct((T, nin), BF16),
                   jax.ShapeDtypeStruct((T, 3 * W), BF16), jax.ShapeDtypeStruct((8, 3 * d), F32)],
        compiler_params=_cparams(("arbitrary",)),
    )(dz, p, p, p, r, wb, wo, bg)


def _ln_bwd(dy, xhat, rstd, g, *, tm=512, deps=()):
    T, d = dy.shape
    tm = _pick(T, (tm, 256, 128, 8))

    def body(dy_ref, xh_ref, rs_ref, g_ref, *rest):
        dz_ref, dzb_ref, dg_ref, db_ref = rest[len(deps):]

        @pl.when(pl.program_id(0) == 0)
        def _():
            dg_ref[...] = jnp.zeros_like(dg_ref)
            db_ref[...] = jnp.zeros_like(db_ref)

        dy_, xh = dy_ref[...], xh_ref[...]
        dg_ref[0:1, :] += jnp.sum(dy_ * xh, axis=0, keepdims=True)
        db_ref[0:1, :] += jnp.sum(dy_, axis=0, keepdims=True)
        dxh = dy_ * g_ref[...]
        dz = rs_ref[...] * (dxh - jnp.mean(dxh, axis=-1, keepdims=True) - xh * jnp.mean(dxh * xh, axis=-1, keepdims=True))
        dz_ref[...] = dz
        dzb_ref[...] = dz.astype(BF16)

    row = lambda i: (i, 0)
    fix = lambda i: (0, 0)
    return pl.pallas_call(
        body,
        name="ln_bwd",
        grid=(T // tm,),
        in_specs=[pl.BlockSpec((tm, d), row), pl.BlockSpec((tm, d), row), pl.BlockSpec((tm, 1), row), pl.BlockSpec((1, d), fix)]
        + [ANY_SPEC] * len(deps),
        out_specs=[pl.BlockSpec((tm, d), row), pl.BlockSpec((tm, d), row), pl.BlockSpec((8, d), fix), pl.BlockSpec((8, d), fix)],
        out_shape=[jax.ShapeDtypeStruct((T, d), F32), jax.ShapeDtypeStruct((T, d), BF16), jax.ShapeDtypeStruct((8, d), F32),
                   jax.ShapeDtypeStruct((8, d), F32)],
        compiler_params=_cparams(("arbitrary",)),
    )(dy, xhat, rstd, g, *deps)


def _loss_head(y, target, *, tm=512):
    T, d = y.shape
    tm = _pick(T, (tm, 256, 128, 8))
    n = T // tm

    def body(y_ref, t_ref, loss_ref, dy_ref, acc_ref):
        i = pl.program_id(0)

        @pl.when(i == 0)
        def _():
            acc_ref[...] = jnp.zeros_like(acc_ref)

        e = y_ref[...] - t_ref[...]
        dy_ref[...] = e * (1.0 / d)
        acc_ref[...] += jnp.sum(e * e, axis=0, keepdims=True)

        @pl.when(i == n - 1)
        def _():
            loss_ref[...] = (0.5 / d) * jnp.sum(acc_ref[...], axis=1, keepdims=True)

    row = lambda i: (i, 0)
    return pl.pallas_call(
        body,
        name="loss_head",
        grid=(n,),
        in_specs=[pl.BlockSpec((tm, d), row), pl.BlockSpec((tm, d), row)],
        out_specs=[pl.BlockSpec((1, 1), lambda i: (0, 0)), pl.BlockSpec((tm, d), row)],
        out_shape=[jax.ShapeDtypeStruct((1, 1), F32), jax.ShapeDtypeStruct((T, d), F32)],
        scratch_shapes=[pltpu.VMEM((1, d), F32)],
        compiler_params=_cparams(("arbitrary",)),
    )(y, target)


def _lower_bounds_fwd(lower_bounds):
    depth, n = lower_bounds.shape

    def body(x_ref, soft_ref, lb_ref):
        x = x_ref[...]
        e = jnp.exp(x - jnp.max(x, axis=0, keepdims=True))
        soft_ref[...] = e / jnp.sum(e, axis=0, keepdims=True)
        run = None
        for l in range(depth):
            run = soft_ref[l:l + 1, :] if run is None else run + soft_ref[l:l + 1, :]
            lb_ref[l:l + 1, :] = run - soft_ref[0:1, :]

    return pl.pallas_call(body, name="lower_bounds_fwd",
                          out_shape=[jax.ShapeDtypeStruct((depth, n), F32), jax.ShapeDtypeStruct((depth, n), F32)])(lower_bounds)


def _lower_bounds_bwd(soft, dlb):
    depth, n = soft.shape

    def body(soft_ref, dlb_ref, out_ref, dsoft_ref):
        total = jnp.sum(dlb_ref[...], axis=0, keepdims=True)
        run = None
        for l in reversed(range(depth)):
            run = dlb_ref[l:l + 1, :] if run is None else run + dlb_ref[l:l + 1, :]
            dsoft_ref[l:l + 1, :] = run - total if l == 0 else run
        s, ds = soft_ref[...], dsoft_ref[...]
        out_ref[...] = s * (ds - jnp.sum(s * ds, axis=0, keepdims=True))

    return pl.pallas_call(body, name="lower_bounds_bwd", out_shape=jax.ShapeDtypeStruct((depth, n), F32),
                          scratch_shapes=[pltpu.VMEM((depth, n), F32)])(soft, dlb)


def _layer_fwd(x0, x0b, mem2, lb, w_in, rest_fn, *, bl, seq, alpha, deps=()):
    p = _matmul("proj_in", x0b, w_in, mode="nn", out_dtype=BF16, deps=deps, tm=1024, tn=1792)
    wts = dict(rest_fn(p), w_in=w_in)
    mk = _matmul("mem_k", mem2, wts["w_mem_k"], mode="nn", out_dtype=BF16)
    mv = _matmul("mem_v", mem2, wts["w_mem_v"], mode="nn", out_dtype=BF16)
    y, st, opre = _mixer_fwd(p, mk, mv, lb, wts["conv_w"], wts["hg_norm_w"], bl=bl, seq=seq)
    r, merged, xhat1, rstd1, x1b = _merge_fwd(y, p, x0, wts["w_branch"], wts["w_o"], wts["b_gate"], wts["ln1_g"], wts["ln1_b"],
                                              alpha=alpha)
    h = _matmul("mlp_up", x1b, wts["w_up"], mode="nn", out_dtype=BF16, tm=1024,
                epi_fn=lambda acc: (jnp.square(jnp.maximum(acc, 0.0)),))

    def ln2(acc, xh1, g1, b1, g2, b2):
        xh2, rs2 = _layer_norm_stats(alpha * (xh1 * g1 + b1) + acc)
        x2 = xh2 * g2 + b2
        return xh2, rs2, x2, x2

    xhat2, rstd2, x2, x2b = _matmul(
        "mlp_down", h, wts["w_down"], mode="nn", n_out=4, out_kinds=("tile", "col", "tile", "tile"), out_dtype=(F32, F32, F32, BF16),
        epi_fn=ln2, epi_extra=(xhat1, wts["ln1_g"], wts["ln1_b"], wts["ln2_g"], wts["ln2_b"]), tm=512)
    saved = dict(x0b=x0b, p=p, mk=mk, mv=mv, y=y, st=st, opre=opre, r=r, merged=merged, xhat1=xhat1, rstd1=rstd1, x1b=x1b, h=h,
                 xhat2=xhat2, rstd2=rstd2)
    return x2, x2b, saved, wts


def _ln_bwd_after_residual(alpha):
    def epi(acc, dz_above, xh, rs, gain):
        dx = acc + alpha * dz_above
        dxh = dx * gain
        dz = rs * (dxh - jnp.mean(dxh, axis=-1, keepdims=True) - xh * jnp.mean(dxh * xh, axis=-1, keepdims=True))
        return dz, dz, jnp.sum(dx * xh, axis=0, keepdims=True), jnp.sum(dx, axis=0, keepdims=True)
    return epi


def _ln_bwd_outs():
    return dict(n_out=4, out_kinds=("tile", "tile", "rowsum", "rowsum"), out_dtype=(F32, BF16, F32, F32))


def _mlp_bwd(dz2, dz2b, sv, wts, *, alpha, deps=()):
    g = {}
    da = _matmul("mlp_da", dz2b, wts["w_down"], mode="nt", out_dtype=BF16, tm=1024, deps=deps,
                 epi_fn=lambda acc, h: (acc * (2.0 * jnp.sqrt(h.astype(F32))),), epi_extra=(sv["h"],))
    g["w_down"] = _matmul_tn("grad_w_down", sv["h"], dz2b, out_dtype=BF16, tt=2048)
    g["w_up"] = _matmul_tn("grad_w_up", sv["x1b"], da, out_dtype=BF16, tt=2048)
    dz1, dz1b, dg1, db1 = _matmul("mlp_dx", da, wts["w_up"], mode="nt", epi_fn=_ln_bwd_after_residual(alpha),
                                  epi_extra=(dz2, sv["xhat1"], sv["rstd1"], wts["ln1_g"]), tm=512, **_ln_bwd_outs())
    g["ln1_g"], g["ln1_b"] = dg1[0:1], db1[0:1]
    return dz1, dz1b, g


def _mix_bwd(dz1, dz1b, sv, mem2, lb, wts, *, bl, seq, alpha, send, below=None, deps=()):
    d = dz1.shape[1]
    g = {}
    g["w_o"] = _matmul_tn("grad_w_o", sv["merged"], dz1b, out_dtype=BF16, tt=2048, deps=deps)
    dr, dp, dy, dbg = _merge_bwd(dz1b, sv["p"], sv["r"], wts["w_branch"], wts["w_o"], wts["b_gate"])
    g["b_gate"] = dbg[0:1]
    g["w_branch"] = jnp.concatenate(
        [_matmul_tn("grad_w_branch", sv["y"], dr, a_cols=(i * W, W), b_cols=(i * d, d), out_dtype=BF16) for i in range(N_BRANCH)],
        axis=0)
    token = send(("w_o", "w_branch"), g)
    dp, dmk, dmv, dcw, dnw, dlb = _mixer_bwd(sv["p"], dy, dp, sv["st"], sv["opre"], sv["mk"], sv["mv"], lb,
                                              wts["conv_w"], wts["hg_norm_w"], bl=bl, seq=seq, deps=(token,))
    g["conv_w"], g["hg_norm_w"], g["lb"] = dcw[0:CONV_K], dnw[0:1], dlb[0:1]
    g["w_mem_k"] = _matmul_tn("grad_w_mem_k", mem2, dmk, out_dtype=BF16)
    g["w_mem_v"] = _matmul_tn("grad_w_mem_v", mem2, dmv, out_dtype=BF16)
    g["w_in"] = _matmul_tn("grad_w_in", sv["x0b"], dp, out_dtype=BF16, tt=2048)
    token = send(("w_in", "w_mem_k", "w_mem_v", "conv_w"), g)
    if below is None:
        out = _matmul("proj_in_dx", dp, wts["w_in"], mode="nt", epi_fn=lambda acc, dz: (acc + alpha * dz,), epi_extra=(dz1,),
                      tm=1024, tk=1792, deps=(token,))
    else:
        out = _matmul("proj_in_dx", dp, wts["w_in"], mode="nt", epi_fn=_ln_bwd_after_residual(alpha),
                      epi_extra=(dz1,) + tuple(below), tm=512, tk=1792, deps=(token,), **_ln_bwd_outs())
    return out, g


N_CHIPS = 4
MESH_IDS = pl.DeviceIdType.MESH


def _axis_slice(ref, axis, start, size):
    idx = [slice(None)] * len(ref.shape)
    idx[axis] = pl.ds(start, size)
    return ref.at[tuple(idx)]


def _chip_exchange(name, items):
    n = len(items)
    out_shapes, meta = [], []
    for arr, kind, axis in items:
        shp = list(arr.shape)
        if kind == "gather":
            per = shp[axis]
            shp[axis] = per * N_CHIPS
            out_shapes.append(jax.ShapeDtypeStruct(tuple(shp), arr.dtype))
        elif kind == "scatter":
            per = shp[axis] // N_CHIPS
            shp[axis] = per
            out_shapes.append(jax.ShapeDtypeStruct((N_CHIPS, *shp), arr.dtype))
        else:
            per = None
            out_shapes.append(jax.ShapeDtypeStruct((N_CHIPS, *shp), arr.dtype))
        meta.append((kind, axis, per))

    def body(*refs):
        ins, outs = refs[:n], refs[n:2 * n]
        send_sems, recv_sems, local_sems = refs[2 * n:]
        x, y, c = lax.axis_index("x"), lax.axis_index("y"), lax.axis_index("c")
        me = 2 * x + y
        peers = [(1 - x, y), (x, 1 - y), (1 - x, 1 - y)]

        def src_for(t, chip):
            kind, axis, per = meta[t]
            return _axis_slice(ins[t], axis, chip * per, per) if kind == "scatter" else ins[t]

        def dst_from(t, chip):
            kind, axis, per = meta[t]
            return _axis_slice(outs[t], axis, chip * per, per) if kind == "gather" else outs[t].at[chip]

        def remote(t, k):
            px, py = peers[k]
            return pltpu.make_async_remote_copy(
                src_ref=src_for(t, 2 * px + py), dst_ref=dst_from(t, me), send_sem=send_sems.at[t * 3 + k],
                recv_sem=recv_sems.at[t * 3 + k], device_id=(px, py, c), device_id_type=MESH_IDS)

        def arrival(t, k):
            px, py = peers[k]
            return pltpu.make_async_remote_copy(
                src_ref=src_for(t, me), dst_ref=dst_from(t, 2 * px + py), send_sem=send_sems.at[t * 3 + k],
                recv_sem=recv_sems.at[t * 3 + k], device_id=(px, py, c), device_id_type=MESH_IDS)

        sends = [remote(t, k) for t in range(n) for k in range(3)]
        for cp in sends:
            cp.start()
        own = [pltpu.make_async_copy(src_for(t, me), dst_from(t, me), local_sems.at[t]) for t in range(n)]
        for cp in own:
            cp.start()
        for t in range(n):
            for k in range(3):
                arrival(t, k).wait_recv()
        for cp in sends:
            cp.wait_send()
        for cp in own:
            cp.wait()

    any_spec = pl.BlockSpec(memory_space=pl.ANY)
    return pl.pallas_call(
        body,
        name=name,
        in_specs=[any_spec] * n,
        out_specs=[any_spec] * n,
        out_shape=out_shapes,
        scratch_shapes=[pltpu.SemaphoreType.DMA((3 * n,)), pltpu.SemaphoreType.DMA((3 * n,)), pltpu.SemaphoreType.DMA((n,))],
        compiler_params=pltpu.CompilerParams(has_side_effects=True),
    )(*[a for a, _, _ in items])


HBM_SPEC = pl.BlockSpec(memory_space=pltpu.HBM)
SEM_SPEC = pl.BlockSpec(memory_space=pltpu.SEMAPHORE)
N_PEERS = N_CHIPS - 1


def _my_chip():
    return (2 * lax.axis_index("x") + lax.axis_index("y")).astype(jnp.int32).reshape(1)


def _own_block_spec(r, c, axis, tr):
    if axis == 1:
        return pl.BlockSpec((tr, c), lambda i, me: (i, me[0]))
    return pl.BlockSpec((tr, c), lambda i, me: (me[0] * (r // tr) + i, 0))


def _place_shard(name, shard, axis, me):
    r, c = shard.shape
    tr = _row_block(r, c, shard.dtype.itemsize)
    shp = (r, c * N_CHIPS) if axis == 1 else (r * N_CHIPS, c)

    def body(me_ref, s_ref, o_ref):
        del me_ref
        o_ref[...] = s_ref[...]

    return pl.pallas_call(
        body, name=name,
        grid_spec=pltpu.PrefetchScalarGridSpec(
            num_scalar_prefetch=1, grid=(r // tr,),
            in_specs=[pl.BlockSpec((tr, c), lambda i, me: (i, 0))], out_specs=_own_block_spec(r, c, axis, tr)),
        out_shape=jax.ShapeDtypeStruct(shp, shard.dtype),
        compiler_params=_cparams(("parallel",)),
    )(me, shard)


class _Split:
    def __init__(self, name, items):
        self.name, self.n = name, len(items)
        self.srcs = [a for a, _, _ in items]
        self.meta, self.land_shapes = [], []
        for arr, kind, axis in items:
            shp = list(arr.shape)
            if kind == "gather":
                per = shp[axis]
                shp[axis] = per * N_CHIPS
                self.land_shapes.append(jax.ShapeDtypeStruct(tuple(shp), arr.dtype))
            else:
                per = shp[axis] // N_CHIPS
                shp[axis] = per
                self.land_shapes.append(jax.ShapeDtypeStruct((N_PEERS, *shp), arr.dtype))
            self.meta.append((kind, axis, per))

    def _src(self, ins, t, chip):
        kind, axis, per = self.meta[t]
        return _axis_slice(ins[t], axis, chip * per, per) if kind == "scatter" else ins[t]

    def _dst(self, lands, t, chip, slot):
        kind, axis, per = self.meta[t]
        return _axis_slice(lands[t], axis, chip * per, per) if kind == "gather" else lands[t].at[slot]

    def landing_zones(self, me):
        return [_place_shard(self.name + "_own", src, axis, me) if kind == "gather" else lax.empty(ls.shape, ls.dtype)
                for src, ls, (kind, axis, _) in zip(self.srcs, self.land_shapes, self.meta)]

    def _copies(self, ins, lands, send_sems, recv_sems, arrivals):
        x, y, c = lax.axis_index("x"), lax.axis_index("y"), lax.axis_index("c")
        me = 2 * x + y
        peers = [(1 - x, y), (x, 1 - y), (1 - x, 1 - y)]
        res = []
        for t in range(self.n):
            for k, (px, py) in enumerate(peers):
                theirs = 2 * px + py
                sems = dict(send_sem=send_sems.at[t * N_PEERS + k], recv_sem=recv_sems.at[t * N_PEERS + k],
                            device_id=(px, py, c), device_id_type=MESH_IDS)
                if arrivals:
                    res.append(pltpu.make_async_remote_copy(src_ref=self._src(ins, t, me), dst_ref=self._dst(lands, t, theirs, k), **sems))
                else:
                    res.append(pltpu.make_async_remote_copy(src_ref=self._src(ins, t, theirs), dst_ref=self._dst(lands, t, me, k), **sems))
        return res

    def start(self, lands, deps=()):
        n, nd = self.n, len(deps)

        def body(*refs):
            ins, lnd = refs[:n], refs[n:2 * n]
            send_sems, recv_sems = refs[2 * n + nd], refs[2 * n + nd + 1]
            token = refs[-1]
            for cp in self._copies(ins, lnd, send_sems, recv_sems, arrivals=False):
                cp.start()
            token[...] = jnp.zeros_like(token)

        hbm = lambda a: pltpu.HBM(a.shape, a.dtype)
        res = pl.pallas_call(
            body, name=self.name + "_start",
            in_specs=[HBM_SPEC] * (2 * n) + [ANY_SPEC] * nd,
            out_specs=[SEM_SPEC, SEM_SPEC] + [HBM_SPEC] * (2 * n) + [pl.BlockSpec(memory_space=pltpu.VMEM)],
            out_shape=[pltpu.SemaphoreType.DMA((N_PEERS * n,)), pltpu.SemaphoreType.DMA((N_PEERS * n,))]
            + [hbm(a) for a in self.srcs] + [hbm(a) for a in self.land_shapes] + [jax.ShapeDtypeStruct((8, 128), F32)],
            input_output_aliases={i: 2 + i for i in range(2 * n)},
            compiler_params=pltpu.CompilerParams(has_side_effects=pltpu.SideEffectType.DATAFLOW_SIDE_EFFECTING),
        )(*[pltpu.with_memory_space_constraint(a, pltpu.HBM) for a in self.srcs],
          *[pltpu.with_memory_space_constraint(a, pltpu.HBM) for a in lands], *deps)
        return res[:-1], res[-1]

    def wait(self, state, after):
        n = self.n
        send_sems, recv_sems = state[0], state[1]
        srcs, lands = state[2:2 + n], state[2 + n:2 + 2 * n]

        def body(*refs):
            ins, lnd = refs[:n], refs[n:2 * n]
            s_sems, r_sems = refs[2 * n], refs[2 * n + 1]
            for cp in self._copies(ins, lnd, s_sems, r_sems, arrivals=True):
                cp.wait_recv()
            for cp in self._copies(ins, lnd, s_sems, r_sems, arrivals=False):
                cp.wait_send()

        hbm = lambda a: pltpu.HBM(a.shape, a.dtype)
        res = pl.pallas_call(
            body, name=self.name + "_wait",
            in_specs=[HBM_SPEC] * (2 * n) + [SEM_SPEC, SEM_SPEC, ANY_SPEC],
            out_specs=[HBM_SPEC] * (2 * n),
            out_shape=[hbm(a) for a in self.srcs] + [hbm(a) for a in self.land_shapes],
            input_output_aliases={i: i for i in range(2 * n)},
            compiler_params=pltpu.CompilerParams(has_side_effects=pltpu.SideEffectType.DATAFLOW_SIDE_EFFECTING),
        )(*srcs, *lands, send_sems, recv_sems, after)
        return res[:n], res[n:]


def _sibling_swap(name, arrays):
    n = len(arrays)

    def body(*refs):
        ins, outs = refs[:n], refs[n:2 * n]
        send_sems, recv_sems = refs[2 * n:]
        sibling = (lax.axis_index("x"), lax.axis_index("y"), 1 - lax.axis_index("c"))
        copies = [pltpu.make_async_remote_copy(src_ref=ins[t], dst_ref=outs[t], send_sem=send_sems.at[t], recv_sem=recv_sems.at[t],
                                               device_id=sibling, device_id_type=MESH_IDS) for t in range(n)]
        for cp in copies:
            cp.start()
        for cp in copies:
            cp.wait()

    any_spec = pl.BlockSpec(memory_space=pl.ANY)
    return pl.pallas_call(
        body,
        name=name,
        in_specs=[any_spec] * n,
        out_specs=[any_spec] * n,
        out_shape=[jax.ShapeDtypeStruct(a.shape, a.dtype) for a in arrays],
        scratch_shapes=[pltpu.SemaphoreType.DMA((n,)), pltpu.SemaphoreType.DMA((n,))],
        compiler_params=pltpu.CompilerParams(has_side_effects=True),
    )(*arrays)


def _row_block(r, c, itemsize=4, target=1 << 20):
    if r % 8 != 0:
        return r
    best = 8
    for tr in range(8, r + 1, 8):
        if r % tr == 0 and tr * c * itemsize <= target:
            best = tr
    return best


def _sum_chips_into(parts, stacked, layer):
    _, r, c = parts.shape
    tr = _row_block(r, c)

    def body(p_ref, s_ref, o_ref):
        del s_ref
        o_ref[...] = ((p_ref[0] + p_ref[1]) + p_ref[2]) + p_ref[3]

    return pl.pallas_call(
        body,
        name="sum_chips",
        grid=(r // tr,),
        in_specs=[pl.BlockSpec((N_CHIPS, tr, c), lambda i: (0, i, 0)), pl.BlockSpec(memory_space=pl.ANY)],
        out_specs=pl.BlockSpec((None, tr, c), lambda i: (layer, i, 0)),
        out_shape=jax.ShapeDtypeStruct(stacked.shape, stacked.dtype),
        input_output_aliases={1: 0},
        compiler_params=_cparams(("parallel",)),
    )(parts, stacked)


def _sum_own_and_peers(me, g, axis, landed):
    _, r, c = landed.shape
    tr = _row_block(r, c)

    def body(me_ref, g_ref, p_ref, o_ref):
        del me_ref
        o_ref[...] = ((g_ref[...].astype(F32) + p_ref[0].astype(F32)) + p_ref[1].astype(F32)) + p_ref[2].astype(F32)

    return pl.pallas_call(
        body, name="sum_chips_own",
        grid_spec=pltpu.PrefetchScalarGridSpec(
            num_scalar_prefetch=1, grid=(r // tr,),
            in_specs=[_own_block_spec(r, c, axis, tr), pl.BlockSpec((N_PEERS, tr, c), lambda i, me: (0, i, 0))],
            out_specs=pl.BlockSpec((tr, c), lambda i, me: (i, 0))),
        out_shape=jax.ShapeDtypeStruct((r, c), F32),
        compiler_params=_cparams(("parallel",)),
    )(me, g, landed)


def _adamw_math(w, m, v, g):
    m_new = ADAM_B1 * m + (1.0 - ADAM_B1) * g
    v_new = ADAM_B2 * v + (1.0 - ADAM_B2) * jnp.square(g)
    m_hat = m_new / (1.0 - ADAM_B1 ** ADAM_STEP)
    v_hat = v_new / (1.0 - ADAM_B2 ** ADAM_STEP)
    return -ADAM_LR * (m_hat / (jnp.sqrt(v_hat) + ADAM_EPS) + ADAM_WD * w), m_new, v_new


def _adamw(w, m, v, g_a, g_b):
    L, r, c = w.shape
    tr = _row_block(r, c, target=1 << 19)

    def body(w_ref, m_ref, v_ref, ga_ref, gb_ref, g_ref, d_ref, nm_ref, nv_ref):
        g = ga_ref[...] + gb_ref[...]
        g_ref[...] = g
        d_ref[...], nm_ref[...], nv_ref[...] = _adamw_math(w_ref[...], m_ref[...], v_ref[...], g)

    spec = pl.BlockSpec((None, tr, c), lambda l, i: (l, i, 0))
    return pl.pallas_call(
        body,
        name="adamw",
        grid=(L, r // tr),
        in_specs=[spec] * 5,
        out_specs=[spec] * 4,
        out_shape=[jax.ShapeDtypeStruct(w.shape, F32)] * 4,
        compiler_params=_cparams(("parallel", "parallel")),
    )(w, m, v, g_a, g_b)


def _adamw_layer(w, m, v, g_a, g_b, layer, outs):
    L, r, c = w.shape
    tr = _row_block(r, c, target=1 << 19)
    n_prev = 0 if outs is None else 4

    def body(w_ref, m_ref, v_ref, ga_ref, gb_ref, *rest):
        g_ref, d_ref, nm_ref, nv_ref = rest[n_prev:]
        g = ga_ref[...] + gb_ref[...]
        g_ref[...] = g
        d_ref[...], nm_ref[...], nv_ref[...] = _adamw_math(w_ref[...], m_ref[...], v_ref[...], g)

    at_layer = pl.BlockSpec((None, tr, c), lambda i: (layer, i, 0))
    flat = pl.BlockSpec((tr, c), lambda i: (i, 0))
    return pl.pallas_call(
        body,
        name="adamw_layer",
        grid=(r // tr,),
        in_specs=[at_layer] * 3 + [flat] * 2 + [ANY_SPEC] * n_prev,
        out_specs=[at_layer] * 4,
        out_shape=[jax.ShapeDtypeStruct(w.shape, F32)] * 4,
        input_output_aliases={5 + k: k for k in range(n_prev)},
        compiler_params=_cparams(("parallel",)),
    )(w, m, v, g_a, g_b, *(outs or ()))


SHARDED = (("w_in", 1), ("conv_w", 1), ("w_mem_k", 0), ("w_mem_v", 0), ("w_branch", 1), ("w_o", 0), ("w_up", 1), ("w_down", 0))
SMALL = ("lower_bounds", "hg_norm_w", "b_gate", "ln1_g", "ln1_b", "ln2_g", "ln2_b")
WEIGHT_ORDER = ("lower_bounds", "w_in", "conv_w", "hg_norm_w", "w_mem_k", "w_mem_v", "w_branch", "b_gate", "w_o", "ln1_g", "ln1_b",
                "w_up", "w_down", "ln2_g", "ln2_b")


def kernel(x, mem, lower_bounds, w_in, conv_w, hg_norm_w, w_mem_k, w_mem_v, w_branch, b_gate, w_o, ln1_g, ln1_b, w_up, w_down, ln2_g, ln2_b, loss_target, m_lower_bounds, m_w_in, m_conv_w, m_hg_norm_w, m_w_mem_k, m_w_mem_v, m_w_branch, m_b_gate, m_w_o, m_ln1_g, m_ln1_b, m_w_up, m_w_down, m_ln2_g, m_ln2_b, v_lower_bounds, v_w_in, v_conv_w, v_hg_norm_w, v_w_mem_k, v_w_mem_v, v_w_branch, v_b_gate, v_w_o, v_ln1_g, v_ln1_b, v_w_up, v_w_down, v_ln2_g, v_ln2_b):
    bl, seq, d = x.shape
    depth = w_in.shape[0]
    weights = dict(lower_bounds=lower_bounds, w_in=w_in, conv_w=conv_w, hg_norm_w=hg_norm_w, w_mem_k=w_mem_k, w_mem_v=w_mem_v,
                   w_branch=w_branch, b_gate=b_gate, w_o=w_o, ln1_g=ln1_g, ln1_b=ln1_b, w_up=w_up, w_down=w_down, ln2_g=ln2_g, ln2_b=ln2_b)
    mom_m = dict(lower_bounds=m_lower_bounds, w_in=m_w_in, conv_w=m_conv_w, hg_norm_w=m_hg_norm_w, w_mem_k=m_w_mem_k, w_mem_v=m_w_mem_v,
                 w_branch=m_w_branch, b_gate=m_b_gate, w_o=m_w_o, ln1_g=m_ln1_g, ln1_b=m_ln1_b, w_up=m_w_up, w_down=m_w_down,
                 ln2_g=m_ln2_g, ln2_b=m_ln2_b)
    mom_v = dict(lower_bounds=v_lower_bounds, w_in=v_w_in, conv_w=v_conv_w, hg_norm_w=v_hg_norm_w, w_mem_k=v_w_mem_k, w_mem_v=v_w_mem_v,
                 w_branch=v_w_branch, b_gate=v_b_gate, w_o=v_w_o, ln1_g=v_ln1_g, ln1_b=v_ln1_b, w_up=v_w_up, w_down=v_w_down,
                 ln2_g=v_ln2_g, ln2_b=v_ln2_b)

    def shard2d(name, l):
        w = weights[name][l]
        if name == "w_branch":
            return w.reshape(N_BRANCH * W, w.shape[-1]).astype(BF16)
        return w if name == "conv_w" else w.astype(BF16)

    me = _my_chip()

    shard_axis = dict(SHARDED)

    def start_exchange(name, kind, items, deps=()):
        ex = _Split(name, [(arr, kind, shard_axis[nm]) for nm, arr in items])
        state, token = ex.start(ex.landing_zones(me), deps)
        return ex, state, [nm for nm, _ in items], token

    def start_gathers(l, deps=()):
        first = start_exchange(f"gather_in_l{l}", "gather", [("w_in", shard2d("w_in", l))], deps)
        rest = start_exchange(f"gather_rest_l{l}", "gather", [(nm, shard2d(nm, l)) for nm, _ in SHARDED if nm != "w_in"],
                              (first[3],))
        return first, rest

    def gathered(pend, after):
        ex, state, names, _ = pend
        return dict(zip(names, ex.wait(state, after=after)[1]))

    x2d, mem2, t2d = x.reshape(bl * seq, d), mem.reshape(-1, d), loss_target.reshape(bl * seq, d)
    alpha = (2.0 * depth) ** 0.25
    soft, lb_all = _lower_bounds_fwd(lower_bounds)

    h, hb, saved, layer_wts = x2d, x2d.astype(BF16), [], []
    pending = start_gathers(0)
    for l in range(depth):
        first, rest = pending
        w_in_l = gathered(first, h)["w_in"]

        def rest_fn(after, l=l, rest=rest):
            wts = gathered(rest, after)
            for name in ("hg_norm_w", "b_gate", "ln1_g", "ln1_b", "ln2_g", "ln2_b"):
                wts[name] = weights[name][l][None, :]
            return wts

        deps = (rest[3],)
        if l + 1 < depth:
            pending = start_gathers(l + 1, (w_in_l, rest[3]))
            deps += (pending[0][3], pending[1][3])
        h, hb, sv, wts = _layer_fwd(h, hb, mem2, lb_all[l:l + 1], w_in_l, rest_fn, bl=bl, seq=seq, alpha=alpha, deps=deps)
        saved.append(sv)
        layer_wts.append(wts)
    loss, dh = _loss_head(h, t2d)

    shape3 = {name: (depth, weights[name].size // (depth * weights[name].shape[-1]), weights[name].shape[-1]) for name, _ in SHARDED}
    partial = [dict() for _ in range(depth)]
    smalls = [None] * depth
    outs = {name: None for name, _ in SHARDED}

    def finish_reduce(pend, l, after):
        ex, state, names, _ = pend
        sent, got = ex.wait(state, after=after)
        for nm, g_full, landed in zip(names, sent, got):
            partial[l][nm] = _sum_own_and_peers(me, g_full, shard_axis[nm], landed)

    def optimizer_step(l):
        names = [name for name, _ in SHARDED]
        theirs = _sibling_swap(f"swap_partials_l{l}", [partial[l][nm] for nm in names])
        for nm, other in zip(names, theirs):
            outs[nm] = _adamw_layer(weights[nm].reshape(shape3[nm]), mom_m[nm].reshape(shape3[nm]), mom_v[nm].reshape(shape3[nm]),
                                    partial[l][nm], other, l, outs[nm])
        return tuple(outs[nm][0] for nm in names)

    pending_mix, deps = [], ()
    dz2, dz2b, dg2, db2 = _ln_bwd(dh, saved[-1]["xhat2"], saved[-1]["rstd2"], layer_wts[-1]["ln2_g"])
    for l in reversed(range(depth)):
        dz1, dz1b, g_mlp = _mlp_bwd(dz2, dz2b, saved[l], layer_wts[l], alpha=alpha, deps=deps)
        g_mlp["ln2_g"], g_mlp["ln2_b"] = dg2[0:1], db2[0:1]
        pending_mlp = start_exchange(f"reduce_mlp_l{l}", "scatter", [(nm, g_mlp[nm]) for nm in ("w_up", "w_down")])
        deps = (pending_mlp[3],)
        if pending_mix:
            for pend in pending_mix:
                finish_reduce(pend, l + 1, dz1)
            deps += optimizer_step(l + 1)
        pending_mix = []

        def send(names, g, l=l, pending_mix=pending_mix):
            pend = start_exchange(f"reduce_{names[0]}_l{l}", "scatter", [(nm, g[nm]) for nm in names])
            pending_mix.append(pend)
            return pend[3]

        below = (saved[l - 1]["xhat2"], saved[l - 1]["rstd2"], layer_wts[l - 1]["ln2_g"]) if l > 0 else None
        out, g = _mix_bwd(dz1, dz1b, saved[l], mem2, lb_all[l:l + 1], layer_wts[l], bl=bl, seq=seq, alpha=alpha, send=send,
                          below=below, deps=deps)
        if l > 0:
            dz2, dz2b, dg2, db2 = out
        else:
            dh = out
        finish_reduce(pending_mlp, l, out[0] if l > 0 else out)
        deps = ()
        g.update(g_mlp, lower_bounds=g["lb"])
        smalls[l] = jnp.concatenate([g[nm] for nm in SMALL], axis=1)
    small_parts = _chip_exchange("reduce_small", [(jnp.stack(smalls), "bcast", 0)])[0]
    small_sum = _sum_chips_into(small_parts.reshape(N_CHIPS, depth, -1), jnp.zeros((1, depth, small_parts.shape[-1]), F32), 0)
    small_sum = small_sum.reshape(depth, 1, -1)
    small_theirs = _sibling_swap("swap_small", [small_sum])[0]
    for pend in pending_mix:
        finish_reduce(pend, 0, small_theirs)
    optimizer_step(0)

    outs = {name: [r.reshape(weights[name].shape) for r in res] for name, res in outs.items()}
    off = 0
    for name in SMALL:
        n = weights[name].shape[1]
        mine, other = small_sum[:, :, off:off + n], small_theirs[:, :, off:off + n]
        off += n
        if name == "lower_bounds":
            mine = _lower_bounds_bwd(soft, mine[:, 0, :])[:, None, :]
            other = _lower_bounds_bwd(soft, other[:, 0, :])[:, None, :]
        shp = (depth, 1, n)
        res = _adamw(weights[name].reshape(shp), mom_m[name].reshape(shp), mom_v[name].reshape(shp), mine, other)
        outs[name] = [r.reshape(weights[name].shape) for r in res]
    assert off == small_sum.shape[-1]

    total_loss = lax.psum(loss[0, 0], ("x", "y", "c"))
    result = [total_loss, dh.reshape(bl, seq, d)]
    for k in range(4):
        result += [outs[name][k] for name in WEIGHT_ORDER]
    return tuple(result)
```

```python
import functools

import jax
import jax.numpy as jnp
from jax import lax
from jax.experimental import pallas as pl
from jax.experimental.pallas import tpu as pltpu

F32 = jnp.float32
BF16 = jnp.bfloat16

HG_HEADS = 4
HG_F = 128
HG_CHUNK = 32
MEM_HEADS = 4
MEM_HEAD_DIM = 128
BRANCH_WIDTH = 512
N_BRANCH = 3
CONV_K = 3
LN_EPS = 1e-5
RMS_EPS = 1e-6
ADAM_LR = 0.001
ADAM_B1 = 0.9
ADAM_B2 = 0.999
ADAM_EPS = 1e-08
ADAM_WD = 0.01
ADAM_STEP = 10

VMEM_LIMIT = 48 * 1024 * 1024


def _cparams(sem):
    return pltpu.CompilerParams(dimension_semantics=sem, vmem_limit_bytes=VMEM_LIMIT)


def _dot(a, b, dims):
    return lax.dot_general(a, b, (dims, ((), ())), preferred_element_type=F32)


NN = ((1,), (0,))
NT = ((1,), (1,))
TN = ((0,), (0,))


def _pick(n, pref):
    for t in pref:
        if n % t == 0:
            return t
    return n


ANY_SPEC = pl.BlockSpec(memory_space=pl.ANY)


def _matmul(name, a, b, *, mode, out_dtype=F32, a_fn=None, a_extra=(), epi_fn=None, epi_extra=(), n_out=1, out_kinds=None,
            tm=512, tn=1024, tk=1024, deps=()):
    M, K = a.shape
    N = b.shape[1] if mode == "nn" else b.shape[0]
    tm, tn, tk = _pick(M, (tm, 256, 128, 8)), _pick(N, (tn, 896, 512, 256, 128)), _pick(K, (tk, 512, 256, 128))
    nk = K // tk
    n_ax, n_ex = len(a_extra), len(epi_extra)
    n_in = 2 + n_ax + n_ex + len(deps)
    out_dtypes = out_dtype if isinstance(out_dtype, (tuple, list)) else (out_dtype,) * n_out
    out_kinds = out_kinds or ("tile",) * n_out

    def body(*refs):
        a_ref, b_ref = refs[0], refs[1]
        ax_refs = refs[2:2 + n_ax]
        ex_refs = refs[2 + n_ax:2 + n_ax + n_ex]
        o_refs = refs[n_in:n_in + n_out]
        at = a_ref[...]
        at = a_fn(at, *[r[...] for r in ax_refs]) if a_fn is not None else at.astype(BF16)
        part = _dot(at, b_ref[...].astype(BF16), NN if mode == "nn" else NT)

        def finish(acc):
            outs = epi_fn(acc, *[r[...] for r in ex_refs]) if epi_fn is not None else (acc,)
            for o_ref, o, kind in zip(o_refs, outs, out_kinds):
                if kind == "rowsum":
                    @pl.when(pl.program_id(1) == 0)
                    def _(o_ref=o_ref):
                        o_ref[...] = jnp.zeros_like(o_ref)

                    o_ref[0:1, :] += o
                else:
                    o_ref[...] = o.astype(o_ref.dtype)

        if nk == 1:
            finish(part)
            return
        acc_ref = refs[-1]
        k = pl.program_id(2)

        @pl.when(k == 0)
        def _():
            acc_ref[...] = part

        @pl.when(jnp.logical_and(k > 0, k < nk - 1))
        def _():
            acc_ref[...] += part

        @pl.when(k == nk - 1)
        def _():
            finish(acc_ref[...] + part)

    in_specs = [pl.BlockSpec((tm, tk), lambda j, i, k: (i, k)),
                pl.BlockSpec((tk, tn), lambda j, i, k: (k, j)) if mode == "nn" else pl.BlockSpec((tn, tk), lambda j, i, k: (j, k))]
    in_specs += [pl.BlockSpec((1, tk), lambda j, i, k: (0, k)) for _ in a_extra]
    for e in epi_extra:
        if e.shape[0] == 1:
            in_specs.append(pl.BlockSpec((1, tn), lambda j, i, k: (0, j)))
        elif e.shape[1] == 1:
            in_specs.append(pl.BlockSpec((tm, 1), lambda j, i, k: (i, 0)))
        else:
            in_specs.append(pl.BlockSpec((tm, tn), lambda j, i, k: (i, j)))
    in_specs += [ANY_SPEC] * len(deps)
    out_specs, out_shapes = [], []
    for kind, dt in zip(out_kinds, out_dtypes):
        if kind == "col":
            out_specs.append(pl.BlockSpec((tm, 1), lambda j, i, k: (i, 0)))
            out_shapes.append(jax.ShapeDtypeStruct((M, 1), dt))
        elif kind == "rowsum":
            out_specs.append(pl.BlockSpec((8, tn), lambda j, i, k: (0, j)))
            out_shapes.append(jax.ShapeDtypeStruct((8, N), dt))
        else:
            out_specs.append(pl.BlockSpec((tm, tn), lambda j, i, k: (i, j)))
            out_shapes.append(jax.ShapeDtypeStruct((M, N), dt))
    out = pl.pallas_call(
        body,
        name=name,
        grid=(N // tn, M // tm, nk),
        in_specs=in_specs,
        out_specs=out_specs,
        out_shape=out_shapes,
        scratch_shapes=[pltpu.VMEM((tm, tn), F32)] if nk > 1 else [],
        compiler_params=_cparams(("arbitrary", "arbitrary", "arbitrary")),
    )(a, b, *a_extra, *epi_extra, *deps)
    return out[0] if n_out == 1 else out


def _matmul_tn(name, a, b, *, a_fn=None, a_extra=(), a_cols=None, b_cols=None, ta=1024, tb=1024, tt=1024, out_dtype=F32, deps=()):
    T = a.shape[0]
    a0, Ka = a_cols if a_cols is not None else (0, a.shape[1])
    b0, Nb = b_cols if b_cols is not None else (0, b.shape[1])
    ta, tb, tt = _pick(Ka, (ta, 512, 256, 128)), _pick(Nb, (tb, 896, 512, 256, 128)), _pick(T, (tt, 512, 256, 128))
    assert a0 % ta == 0 and b0 % tb == 0
    a0, b0 = a0 // ta, b0 // tb
    nt = T // tt
    n_ax = len(a_extra)

    def body(*refs):
        a_ref, b_ref = refs[0], refs[1]
        ax_refs = refs[2:2 + n_ax]
        o_ref = refs[2 + n_ax + len(deps)]
        acc_ref = refs[-1]
        t = pl.program_id(2)
        at = a_ref[...]
        at = a_fn(at, *[r[...] for r in ax_refs]) if a_fn is not None else at.astype(BF16)
        part = _dot(at, b_ref[...].astype(BF16), TN)

        @pl.when(t == 0)
        def _():
            acc_ref[...] = part

        @pl.when(jnp.logical_and(t > 0, t < nt - 1))
        def _():
            acc_ref[...] += part

        @pl.when(t == nt - 1)
        def _():
            o_ref[...] = (acc_ref[...] + part if nt > 1 else part).astype(o_ref.dtype)

    in_specs = [pl.BlockSpec((tt, ta), lambda i, j, t: (t, a0 + i)), pl.BlockSpec((tt, tb), lambda i, j, t: (t, b0 + j))]
    in_specs += [pl.BlockSpec((1, ta), lambda i, j, t: (0, a0 + i)) for _ in a_extra]
    in_specs += [ANY_SPEC] * len(deps)
    return pl.pallas_call(
        body,
        name=name,
        grid=(Ka // ta, Nb // tb, nt),
        in_specs=in_specs,
        out_specs=pl.BlockSpec((ta, tb), lambda i, j, t: (i, j)),
        out_shape=jax.ShapeDtypeStruct((Ka, Nb), out_dtype),
        scratch_shapes=[pltpu.VMEM((ta, tb), F32)],
        compiler_params=_cparams(("parallel", "parallel", "arbitrary")),
    )(a, b, *a_extra, *deps)


W = BRANCH_WIDTH
C_CB, C_CC, C_CH, C_HQ, C_HF, C_HI, C_HG, C_MQ, N_MIX = 0, W, 2 * W, 3 * W, 4 * W, 5 * W, 6 * W, 7 * W, 8 * W
TS_MIX = 256
PREV_ROWS = 16


def _sigmoid(x):
    return jax.nn.sigmoid(x)


def _chunk_pos(shape):
    return lax.broadcasted_iota(jnp.int32, shape, 0) & (HG_CHUNK - 1)


def _seg_cumsum(x, pos):
    sh = 1
    while sh < HG_CHUNK:
        x = x + jnp.where(pos >= sh, pltpu.roll(x, sh, 0), 0.0)
        sh *= 2
    return x


def _seg_rev_cumsum(x, pos):
    n = x.shape[0]
    sh = 1
    while sh < HG_CHUNK:
        x = x + jnp.where(pos < HG_CHUNK - sh, pltpu.roll(x, n - sh, 0), 0.0)
        sh *= 2
    return x


def _chunk_mask(ts):
    r = lax.broadcasted_iota(jnp.int32, (ts, ts), 0)
    c = lax.broadcasted_iota(jnp.int32, (ts, ts), 1)
    return jnp.logical_and((r // HG_CHUNK) == (c // HG_CHUNK), c <= r)


def _hgrn_gates(p_ref, lb):
    q = p_ref[:, C_HQ:C_HQ + W].astype(F32)
    fl = p_ref[:, C_HF:C_HF + W].astype(F32)
    sig = _sigmoid(fl)
    f = lb + (1.0 - lb) * sig
    logf = jnp.log(f)
    k = (1.0 - lb) * _sigmoid(-fl)
    sq = _sigmoid(q)
    qs = q * sq
    return q, sq, qs, sig, f, logf, k


def _hgrn_decays(logf, bc_sc, ts):
    pos = _chunk_pos(logf.shape)
    bc = _seg_cumsum(logf, pos)
    bc_sc[...] = bc
    nc = ts // HG_CHUNK
    bref = jnp.concatenate(
        [jnp.broadcast_to(bc_sc[n * HG_CHUNK + HG_CHUNK // 2 - 1:n * HG_CHUNK + HG_CHUNK // 2, :], (HG_CHUNK, W)) for n in range(nc)], axis=0)
    blast = jnp.concatenate(
        [jnp.broadcast_to(bc_sc[(n + 1) * HG_CHUNK - 1:(n + 1) * HG_CHUNK, :], (HG_CHUNK, W)) for n in range(nc)], axis=0)
    return pos, bc, bref, blast


def _conv_shift_down(u, carry_ref, row):
    n = carry_ref.shape[0]
    last, before = carry_ref[n - 1:n, :], carry_ref[n - 2:n - 1, :]
    u1 = jnp.where(row == 0, last, pltpu.roll(u, 1, 0))
    u2 = jnp.where(row == 0, before, jnp.where(row == 1, last, pltpu.roll(u, 2, 0)))
    return u1, u2


def _attn_probs(qh, kh):
    s = _dot(qh, kh, NT) * (MEM_HEAD_DIM ** -0.5)
    e = jnp.exp(s - jnp.max(s, axis=-1, keepdims=True))
    return e / jnp.sum(e, axis=-1, keepdims=True)


def _mixer_fwd(p, mk, mv, lb, conv_w, norm_w, *, bl, seq):
    T = p.shape[0]
    ts = TS_MIX
    ns = seq // ts
    nc = ts // HG_CHUNK
    ml = mk.shape[0] // bl

    def body(p_ref, mk_ref, mv_ref, lb_ref, cw_ref, nw_ref, y_ref, st_ref, opre_ref, state_sc, carry_sc, bc_sc):
        @pl.when(pl.program_id(1) == 0)
        def _():
            state_sc[...] = jnp.zeros_like(state_sc)
            carry_sc[...] = jnp.zeros_like(carry_sc)

        cb, cc, ch = (p_ref[:, c0:c0 + W].astype(F32) for c0 in (C_CB, C_CC, C_CH))
        u = cc * ch
        row = lax.broadcasted_iota(jnp.int32, (ts, W), 0)
        u1, u2 = _conv_shift_down(u, carry_sc, row)
        yconv = u2 * cw_ref[0:1, :] + u1 * cw_ref[1:2, :] + u * cw_ref[2:3, :]
        y_ref[:, 0:W] = (cb * yconv).astype(BF16)
        carry_sc[...] = u[ts - 8:ts, :]

        lbv = lb_ref[...]
        _, _, qs, _, _, logf, k = _hgrn_gates(p_ref, lbv)
        pos, bc, bref, blast = _hgrn_decays(logf, bc_sc, ts)
        a_all = (qs * jnp.exp(bc - bref)).astype(BF16)
        bk_all = (k * jnp.exp(bref - bc)).astype(BF16)
        qin_all = (qs * jnp.exp(bc)).astype(BF16)
        kout_all = (k * jnp.exp(blast - bc)).astype(BF16)
        v_all = p_ref[:, C_HI:C_HI + W].astype(BF16)
        mask = _chunk_mask(ts)
        for h in range(HG_HEADS):
            hs = slice(h * HG_F, (h + 1) * HG_F)
            vb = v_all[:, hs]
            scores = jnp.where(mask, _dot(a_all[:, hs], bk_all[:, hs], NT), 0.0)
            o_intra = _dot(scores.astype(BF16), vb, NN)
            st = state_sc[h]
            o_inter = []
            for n in range(nc):
                rows = slice(n * HG_CHUNK, (n + 1) * HG_CHUNK)
                st_ref[n, h] = st
                o_inter.append(_dot(qin_all[rows, hs], st.astype(BF16), NT))
                kv = _dot(vb[rows], kout_all[rows, hs], TN)
                decay = jnp.exp(bc_sc[(n + 1) * HG_CHUNK - 1:(n + 1) * HG_CHUNK, hs])
                st = st * decay + kv
            state_sc[h] = st
            o = o_intra + jnp.concatenate(o_inter, axis=0)
            opre_ref[:, hs] = o
            on = o * lax.rsqrt(jnp.mean(o * o, axis=-1, keepdims=True) + RMS_EPS) * nw_ref[...]
            g = p_ref[:, C_HG + h * HG_F:C_HG + (h + 1) * HG_F].astype(F32)
            y_ref[:, W + h * HG_F:W + (h + 1) * HG_F] = (on * (g * _sigmoid(g))).astype(BF16)

        for h in range(MEM_HEADS):
            hs = slice(h * MEM_HEAD_DIM, (h + 1) * MEM_HEAD_DIM)
            qh = p_ref[:, C_MQ + h * MEM_HEAD_DIM:C_MQ + (h + 1) * MEM_HEAD_DIM].astype(BF16)
            prob = _attn_probs(qh, mk_ref[:, hs])
            y_ref[:, 2 * W + h * MEM_HEAD_DIM:2 * W + (h + 1) * MEM_HEAD_DIM] = _dot(prob.astype(BF16), mv_ref[:, hs], NN).astype(BF16)

    return pl.pallas_call(
        body,
        name="mixer_fwd",
        grid=(bl, ns),
        in_specs=[
            pl.BlockSpec((ts, N_MIX), lambda b, s: (b * ns + s, 0)),
            pl.BlockSpec((ml, W), lambda b, s: (b, 0)),
            pl.BlockSpec((ml, W), lambda b, s: (b, 0)),
            pl.BlockSpec((1, W), lambda b, s: (0, 0)),
            pl.BlockSpec((CONV_K, W), lambda b, s: (0, 0)),
            pl.BlockSpec((1, HG_F), lambda b, s: (0, 0)),
        ],
        out_specs=[
            pl.BlockSpec((ts, 3 * W), lambda b, s: (b * ns + s, 0)),
            pl.BlockSpec((nc, HG_HEADS, HG_F, HG_F), lambda b, s: (b * ns + s, 0, 0, 0)),
            pl.BlockSpec((ts, W), lambda b, s: (b * ns + s, 0)),
        ],
        out_shape=[
            jax.ShapeDtypeStruct((T, 3 * W), BF16),
            jax.ShapeDtypeStruct((T // HG_CHUNK, HG_HEADS, HG_F, HG_F), F32),
            jax.ShapeDtypeStruct((T, W), F32),
        ],
        scratch_shapes=[pltpu.VMEM((HG_HEADS, HG_F, HG_F), F32), pltpu.VMEM((8, W), F32), pltpu.VMEM((ts, W), F32)],
        compiler_params=_cparams(("arbitrary", "arbitrary")),
    )(p, mk, mv, lb, conv_w, norm_w)


def _mixer_bwd(p, dy, dp_gates, st, opre, mk, mv, lb, conv_w, norm_w, *, bl, seq, deps=()):
    T, nin = p.shape
    ts = TS_MIX
    ns = seq // ts
    nc = ts // HG_CHUNK
    ml = mk.shape[0] // bl
    mid, last = HG_CHUNK // 2 - 1, HG_CHUNK - 1

    def body(p_ref, pprev_ref, dy_ref, dpin_ref, st_ref, opre_ref, mk_ref, mv_ref, lb_ref, cw_ref, nw_ref, *rest):
        dp_ref, dmk_ref, dmv_ref, dcw_ref, dnw_ref, dlb_ref, dstate_sc, carry_sc, uprev_sc, bc_sc = rest[len(deps):]
        del dpin_ref
        b, s = pl.program_id(0), pl.program_id(1)

        @pl.when(s == 0)
        def _():
            dstate_sc[...] = jnp.zeros_like(dstate_sc)
            carry_sc[...] = jnp.zeros_like(carry_sc)
            dmk_ref[...] = jnp.zeros_like(dmk_ref)
            dmv_ref[...] = jnp.zeros_like(dmv_ref)

        @pl.when(jnp.logical_and(b == 0, s == 0))
        def _():
            dcw_ref[...] = jnp.zeros_like(dcw_ref)
            dnw_ref[...] = jnp.zeros_like(dnw_ref)
            dlb_ref[...] = jnp.zeros_like(dlb_ref)

        cb, cc, ch = (p_ref[:, c0:c0 + W].astype(F32) for c0 in (C_CB, C_CC, C_CH))
        u = cc * ch
        row = lax.broadcasted_iota(jnp.int32, (ts, W), 0)
        uprev = pprev_ref[:, C_CC:C_CC + W].astype(F32) * pprev_ref[:, C_CH:C_CH + W].astype(F32)
        uprev_sc[...] = jnp.where(s == ns - 1, 0.0, uprev)
        u1, u2 = _conv_shift_down(u, uprev_sc, row)
        w0, w1, w2 = cw_ref[0:1, :], cw_ref[1:2, :], cw_ref[2:3, :]
        dya = dy_ref[:, 0:W].astype(F32)
        dp_ref[:, C_CB:C_CB + W] = (dya * (u2 * w0 + u1 * w1 + u * w2)).astype(BF16)
        dv = cb * dya
        dv1 = jnp.where(row == ts - 1, carry_sc[0:1, :], pltpu.roll(dv, ts - 1, 0))
        dv2 = jnp.where(row == ts - 1, carry_sc[1:2, :], jnp.where(row == ts - 2, carry_sc[0:1, :], pltpu.roll(dv, ts - 2, 0)))
        du = dv * w2 + dv1 * w1 + dv2 * w0
        dp_ref[:, C_CC:C_CC + W] = (du * ch).astype(BF16)
        dp_ref[:, C_CH:C_CH + W] = (du * cc).astype(BF16)
        dcw_ref[0:1, :] += jnp.sum(dv * u2, axis=0, keepdims=True)
        dcw_ref[1:2, :] += jnp.sum(dv * u1, axis=0, keepdims=True)
        dcw_ref[2:3, :] += jnp.sum(dv * u, axis=0, keepdims=True)
        carry_sc[...] = dv[0:8, :]

        lbv = lb_ref[...]
        q_all, sq_all, qs_all, sig_all, f_all, logf, k_all = _hgrn_gates(p_ref, lbv)
        pos_all, bc, bref, blast = _hgrn_decays(logf, bc_sc, ts)
        ea_all, eb_all, eq_all, ek_all = jnp.exp(bc - bref), jnp.exp(bref - bc), jnp.exp(bc), jnp.exp(blast - bc)
        mask = _chunk_mask(ts)
        pos = _chunk_pos((ts, HG_F))
        pos_c = _chunk_pos((HG_CHUNK, HG_F))
        nw = nw_ref[...]
        for h in range(HG_HEADS):
            hs = slice(h * HG_F, (h + 1) * HG_F)
            qs, k, ea, eb, eq, ek = qs_all[:, hs], k_all[:, hs], ea_all[:, hs], eb_all[:, hs], eq_all[:, hs], ek_all[:, hs]
            a, bk, qin, kout = qs * ea, k * eb, qs * eq, k * ek
            o = opre_ref[:, hs]
            g = p_ref[:, C_HG + h * HG_F:C_HG + (h + 1) * HG_F].astype(F32)
            sg = _sigmoid(g)
            r = lax.rsqrt(jnp.mean(o * o, axis=-1, keepdims=True) + RMS_EPS)
            dyb = dy_ref[:, W + h * HG_F:W + (h + 1) * HG_F].astype(F32)
            dp_ref[:, C_HG + h * HG_F:C_HG + (h + 1) * HG_F] = (dyb * (o * r * nw) * (sg * (1.0 + g * (1.0 - sg)))).astype(BF16)
            don = dyb * (g * sg)
            dnw_ref[0:1, :] += jnp.sum(don * o * r, axis=0, keepdims=True)
            dn = don * nw
            do = r * (dn - o * (r * r) * jnp.mean(dn * o, axis=-1, keepdims=True))
            dob = do.astype(BF16)
            vb = p_ref[:, C_HI + h * HG_F:C_HI + (h + 1) * HG_F].astype(BF16)
            ab, bkb = a.astype(BF16), bk.astype(BF16)
            scores = jnp.where(mask, _dot(ab, bkb, NT), 0.0)
            dscores = jnp.where(mask, _dot(dob, vb, NT), 0.0).astype(BF16)
            dv_h = _dot(scores.astype(BF16), dob, TN)
            da = _dot(dscores, bkb, NN)
            dbk = _dot(dscores, ab, TN)
            koutb, qinb = kout.astype(BF16), qin.astype(BF16)
            dst = dstate_sc[h]
            dqin_p, dkout_p, dvi_p, ddec_p = [None] * nc, [None] * nc, [None] * nc, [None] * nc
            for n in reversed(range(nc)):
                rows = slice(n * HG_CHUNK, (n + 1) * HG_CHUNK)
                st_n = st_ref[n, h]
                decay = jnp.exp(bc_sc[n * HG_CHUNK + last:n * HG_CHUNK + last + 1, hs])
                dstb = dst.astype(BF16)
                dvi_p[n] = _dot(koutb[rows], dstb, NT)
                dkout_p[n] = _dot(vb[rows], dstb, NN)
                ddec_p[n] = jnp.sum(dst * st_n, axis=0, keepdims=True) * decay
                dqin_p[n] = _dot(dob[rows], st_n.astype(BF16), NN)
                dst = dst * decay + _dot(dob[rows], qinb[rows], TN)
            dstate_sc[h] = dst
            dqin = jnp.concatenate(dqin_p, axis=0)
            dkout = jnp.concatenate(dkout_p, axis=0)
            dp_ref[:, C_HI + h * HG_F:C_HI + (h + 1) * HG_F] = (dv_h + jnp.concatenate(dvi_p, axis=0)).astype(BF16)
            dqs = da * ea + dqin * eq
            dk = dbk * eb + dkout * ek
            t_a, t_b, t_q, t_k = da * a, dbk * bk, dqin * qin, dkout * kout
            dbc = t_a - t_b + t_q - t_k
            t_ref = t_b - t_a
            pieces = []
            for n in range(nc):
                rows = slice(n * HG_CHUNK, (n + 1) * HG_CHUNK)
                s_ref = jnp.sum(t_ref[rows], axis=0, keepdims=True)
                s_last = jnp.sum(t_k[rows], axis=0, keepdims=True) + ddec_p[n]
                pieces.append(dbc[rows] + jnp.where(pos_c == mid, s_ref, 0.0) + jnp.where(pos_c == last, s_last, 0.0))
            dlogf = _seg_rev_cumsum(jnp.concatenate(pieces, axis=0), pos)
            sig, lbh = sig_all[:, hs], lbv[:, hs]
            dfk = dlogf / f_all[:, hs] - dk
            dp_ref[:, C_HF + h * HG_F:C_HF + (h + 1) * HG_F] = (dfk * (1.0 - lbh) * sig * (1.0 - sig)).astype(BF16)
            dlb_ref[0:1, hs] += jnp.sum(dfk * (1.0 - sig), axis=0, keepdims=True)
            q, sq = q_all[:, hs], sq_all[:, hs]
            dp_ref[:, C_HQ + h * HG_F:C_HQ + (h + 1) * HG_F] = (dqs * (sq * (1.0 + q * (1.0 - sq)))).astype(BF16)

        for h in range(MEM_HEADS):
            hs = slice(h * MEM_HEAD_DIM, (h + 1) * MEM_HEAD_DIM)
            qh = p_ref[:, C_MQ + h * MEM_HEAD_DIM:C_MQ + (h + 1) * MEM_HEAD_DIM].astype(BF16)
            kh, vh = mk_ref[:, hs], mv_ref[:, hs]
            prob = _attn_probs(qh, kh)
            dob = dy_ref[:, 2 * W + h * MEM_HEAD_DIM:2 * W + (h + 1) * MEM_HEAD_DIM].astype(BF16)
            dmv_ref[:, hs] += _dot(prob.astype(BF16), dob, TN)
            dprob = _dot(dob, vh, NT)
            ds = prob * (dprob - jnp.sum(dprob * prob, axis=-1, keepdims=True)) * (MEM_HEAD_DIM ** -0.5)
            dsb = ds.astype(BF16)
            dp_ref[:, C_MQ + h * MEM_HEAD_DIM:C_MQ + (h + 1) * MEM_HEAD_DIM] = _dot(dsb, kh, NN).astype(BF16)
            dmk_ref[:, hs] += _dot(dsb, qh, TN)

    def tile(b, s):
        return b * ns + (ns - 1 - s)

    return pl.pallas_call(
        body,
        name="mixer_bwd",
        grid=(bl, ns),
        in_specs=[
            pl.BlockSpec((ts, N_MIX), lambda b, s: (tile(b, s), 0)),
            pl.BlockSpec((PREV_ROWS, N_MIX), lambda b, s: (jnp.maximum(tile(b, s) * (ts // PREV_ROWS) - 1, 0), 0)),
            pl.BlockSpec((ts, 3 * W), lambda b, s: (tile(b, s), 0)),
            pl.BlockSpec(memory_space=pl.ANY),
            pl.BlockSpec((nc, HG_HEADS, HG_F, HG_F), lambda b, s: (tile(b, s), 0, 0, 0)),
            pl.BlockSpec((ts, W), lambda b, s: (tile(b, s), 0)),
            pl.BlockSpec((ml, W), lambda b, s: (b, 0)),
            pl.BlockSpec((ml, W), lambda b, s: (b, 0)),
            pl.BlockSpec((1, W), lambda b, s: (0, 0)),
            pl.BlockSpec((CONV_K, W), lambda b, s: (0, 0)),
            pl.BlockSpec((1, HG_F), lambda b, s: (0, 0)),
        ] + [ANY_SPEC] * len(deps),
        out_specs=[
            pl.BlockSpec((ts, N_MIX), lambda b, s: (tile(b, s), 0)),
            pl.BlockSpec((ml, W), lambda b, s: (b, 0)),
            pl.BlockSpec((ml, W), lambda b, s: (b, 0)),
            pl.BlockSpec((8, W), lambda b, s: (0, 0)),
            pl.BlockSpec((8, HG_F), lambda b, s: (0, 0)),
            pl.BlockSpec((8, W), lambda b, s: (0, 0)),
        ],
        out_shape=[
            jax.ShapeDtypeStruct((T, nin), BF16),
            jax.ShapeDtypeStruct((bl * ml, W), F32),
            jax.ShapeDtypeStruct((bl * ml, W), F32),
            jax.ShapeDtypeStruct((8, W), F32),
            jax.ShapeDtypeStruct((8, HG_F), F32),
            jax.ShapeDtypeStruct((8, W), F32),
        ],
        input_output_aliases={3: 0},
        scratch_shapes=[pltpu.VMEM((HG_HEADS, HG_F, HG_F), F32), pltpu.VMEM((8, W), F32), pltpu.VMEM((PREV_ROWS, W), F32),
                        pltpu.VMEM((ts, W), F32)],
        compiler_params=_cparams(("arbitrary", "arbitrary")),
    )(p, p, dy, dp_gates, st, opre, mk, mv, lb, conv_w, norm_w, *deps)


def _layer_norm_stats(z):
    mu = jnp.mean(z, axis=-1, keepdims=True)
    zc = z - mu
    rstd = lax.rsqrt(jnp.mean(zc * zc, axis=-1, keepdims=True) + LN_EPS)
    return zc * rstd, rstd


def _gate_specs(tm, d):
    g0 = N_MIX // d
    return [pl.BlockSpec((tm, d), functools.partial(lambda i, k: (i, g0 + k), k=k)) for k in range(N_BRANCH)]


def _merge_fwd(y, p, x0, wb, wo, bg, ln_g, ln_b, *, alpha, tm=256):
    T, d = x0.shape
    assert N_MIX % d == 0
    tm = _pick(T, (tm, 128, 8))

    def body(y_ref, g0_ref, g1_ref, g2_ref, x_ref, wb_ref, wo_ref, bg_ref, lg_ref, lb_ref, r_ref, mg_ref, xh_ref, rs_ref, x1b_ref):
        merged = None
        for i, g_ref in enumerate((g0_ref, g1_ref, g2_ref)):
            r = _dot(y_ref[:, i * W:(i + 1) * W], wb_ref[i * W:(i + 1) * W, :], NN)
            r_ref[:, i * d:(i + 1) * d] = r.astype(BF16)
            t = _sigmoid(g_ref[...].astype(F32) + bg_ref[:, i * d:(i + 1) * d]) * r
            merged = t if merged is None else merged + t
        mb = merged.astype(BF16)
        mg_ref[...] = mb
        z = alpha * x_ref[...] + _dot(mb, wo_ref[...], NN)
        xh, rs = _layer_norm_stats(z)
        xh_ref[...], rs_ref[...] = xh, rs
        x1b_ref[...] = (xh * lg_ref[...] + lb_ref[...]).astype(BF16)

    row = lambda i: (i, 0)
    fix = lambda i: (0, 0)
    return pl.pallas_call(
        body,
        name="merge_fwd",
        grid=(T // tm,),
        in_specs=[pl.BlockSpec((tm, 3 * W), row)] + _gate_specs(tm, d) + [
            pl.BlockSpec((tm, d), row), pl.BlockSpec((3 * W, d), fix), pl.BlockSpec((d, d), fix), pl.BlockSpec((1, 3 * d), fix),
            pl.BlockSpec((1, d), fix), pl.BlockSpec((1, d), fix)],
        out_specs=[pl.BlockSpec((tm, 3 * d), row), pl.BlockSpec((tm, d), row), pl.BlockSpec((tm, d), row), pl.BlockSpec((tm, 1), row),
                   pl.BlockSpec((tm, d), row)],
        out_shape=[jax.ShapeDtypeStruct((T, 3 * d), BF16), jax.ShapeDtypeStruct((T, d), BF16),
                   jax.ShapeDtypeStruct((T, d), F32), jax.ShapeDtypeStruct((T, 1), F32), jax.ShapeDtypeStruct((T, d), BF16)],
        compiler_params=_cparams(("parallel",)),
    )(y, p, p, p, x0, wb, wo, bg, ln_g, ln_b)


def _merge_bwd(dz, p, r, wb, wo, bg, *, tm=256):
    T, d = dz.shape
    nin = p.shape[1]
    tm = _pick(T, (tm, 128, 8))

    def body(dz_ref, g0_ref, g1_ref, g2_ref, r_ref, wb_ref, wo_ref, bg_ref, dr_ref, dp_ref, dy_ref, dbg_ref):
        @pl.when(pl.program_id(0) == 0)
        def _():
            dbg_ref[...] = jnp.zeros_like(dbg_ref)

        dmerged = _dot(dz_ref[...].astype(BF16), wo_ref[...], NT)
        dp_ref[:, 0:N_MIX] = jnp.zeros((tm, N_MIX), BF16)
        for i, g_ref in enumerate((g0_ref, g1_ref, g2_ref)):
            cs = slice(i * d, (i + 1) * d)
            s = _sigmoid(g_ref[...].astype(F32) + bg_ref[:, cs])
            drb = (dmerged * s).astype(BF16)
            dr_ref[:, cs] = drb
            dgate = dmerged * r_ref[:, cs].astype(F32) * s * (1.0 - s)
            dp_ref[:, N_MIX + i * d:N_MIX + (i + 1) * d] = dgate.astype(BF16)
            dbg_ref[0:1, cs] += jnp.sum(dgate, axis=0, keepdims=True)
            dy_ref[:, i * W:(i + 1) * W] = _dot(drb, wb_ref[i * W:(i + 1) * W, :], NT).astype(BF16)

    row = lambda i: (i, 0)
    fix = lambda i: (0, 0)
    return pl.pallas_call(
        body,
        name="merge_bwd",
        grid=(T // tm,),
        in_specs=[pl.BlockSpec((tm, d), row)] + _gate_specs(tm, d) + [
            pl.BlockSpec((tm, 3 * d), row), pl.BlockSpec((3 * W, d), fix), pl.BlockSpec((d, d), fix), pl.BlockSpec((1, 3 * d), fix)],
        out_specs=[pl.BlockSpec((tm, 3 * d), row), pl.BlockSpec((tm, nin), row), pl.BlockSpec((tm, 3 * W), row),
                   pl.BlockSpec((8, 3 * d), fix)],
        out_shape=[jax.ShapeDtypeStruct((T, 3 * d), BF16), jax.ShapeDtypeStruct((T, nin), BF16),
                   jax.ShapeDtypeStruct((T, 3 * W), BF16), jax.ShapeDtypeStruct((8, 3 * d), F32)],
        compiler_params=_cparams(("arbitrary",)),
    )(dz, p, p, p, r, wb, wo, bg)


def _mlp_fwd(xhat1, g1, b1, wu, wd, g2, b2, *, alpha, tm=512, tf=1024):
    T, d = xhat1.shape
    ff = wu.shape[1]
    tm, tf = _pick(T, (tm, 256, 128, 8)), _pick(ff, (tf, 512, 256, 128))
    nf = ff // tf

    def body(xh_ref, g1_ref, b1_ref, wu_ref, wd_ref, g2_ref, b2_ref, a_ref, xh2_ref, rs2_ref, x2_ref, x2b_ref, acc_ref):
        f = pl.program_id(1)
        x1 = xh_ref[...] * g1_ref[...] + b1_ref[...]
        a = _dot(x1.astype(BF16), wu_ref[...], NN)
        a_ref[...] = a.astype(BF16)
        h = jnp.square(jnp.maximum(a, 0.0))
        part = _dot(h.astype(BF16), wd_ref[...], NN)

        @pl.when(f == 0)
        def _():
            acc_ref[...] = part

        @pl.when(f > 0)
        def _():
            acc_ref[...] += part

        @pl.when(f == nf - 1)
        def _():
            xh2, rs2 = _layer_norm_stats(alpha * x1 + acc_ref[...])
            xh2_ref[...] = xh2
            rs2_ref[...] = rs2
            x2 = xh2 * g2_ref[...] + b2_ref[...]
            x2_ref[...] = x2
            x2b_ref[...] = x2.astype(BF16)

    row = lambda i, f: (i, 0)
    fix = lambda i, f: (0, 0)
    return pl.pallas_call(
        body,
        name="mlp_fwd",
        grid=(T // tm, nf),
        in_specs=[pl.BlockSpec((tm, d), row), pl.BlockSpec((1, d), fix), pl.BlockSpec((1, d), fix),
                  pl.BlockSpec((d, tf), lambda i, f: (0, f)), pl.BlockSpec((tf, d), lambda i, f: (f, 0)),
                  pl.BlockSpec((1, d), fix), pl.BlockSpec((1, d), fix)],
        out_specs=[pl.BlockSpec((tm, tf), lambda i, f: (i, f)), pl.BlockSpec((tm, d), row), pl.BlockSpec((tm, 1), row),
                   pl.BlockSpec((tm, d), row), pl.BlockSpec((tm, d), row)],
        out_shape=[jax.ShapeDtypeStruct((T, ff), BF16), jax.ShapeDtypeStruct((T, d), F32), jax.ShapeDtypeStruct((T, 1), F32),
                   jax.ShapeDtypeStruct((T, d), F32), jax.ShapeDtypeStruct((T, d), BF16)],
        scratch_shapes=[pltpu.VMEM((tm, d), F32)],
        compiler_params=_cparams(("parallel", "arbitrary")),
    )(xhat1, g1, b1, wu, wd, g2, b2)


def _ln_bwd(dy, xhat, rstd, g, *, tm=512, deps=()):
    T, d = dy.shape
    tm = _pick(T, (tm, 256, 128, 8))

    def body(dy_ref, xh_ref, rs_ref, g_ref, *rest):
        dz_ref, dzb_ref, dg_ref, db_ref = rest[len(deps):]

        @pl.when(pl.program_id(0) == 0)
        def _():
            dg_ref[...] = jnp.zeros_like(dg_ref)
            db_ref[...] = jnp.zeros_like(db_ref)

        dy_, xh = dy_ref[...], xh_ref[...]
        dg_ref[0:1, :] += jnp.sum(dy_ * xh, axis=0, keepdims=True)
        db_ref[0:1, :] += jnp.sum(dy_, axis=0, keepdims=True)
        dxh = dy_ * g_ref[...]
        dz = rs_ref[...] * (dxh - jnp.mean(dxh, axis=-1, keepdims=True) - xh * jnp.mean(dxh * xh, axis=-1, keepdims=True))
        dz_ref[...] = dz
        dzb_ref[...] = dz.astype(BF16)

    row = lambda i: (i, 0)
    fix = lambda i: (0, 0)
    return pl.pallas_call(
        body,
        name="ln_bwd",
        grid=(T // tm,),
        in_specs=[pl.BlockSpec((tm, d), row), pl.BlockSpec((tm, d), row), pl.BlockSpec((tm, 1), row), pl.BlockSpec((1, d), fix)]
        + [ANY_SPEC] * len(deps),
        out_specs=[pl.BlockSpec((tm, d), row), pl.BlockSpec((tm, d), row), pl.BlockSpec((8, d), fix), pl.BlockSpec((8, d), fix)],
        out_shape=[jax.ShapeDtypeStruct((T, d), F32), jax.ShapeDtypeStruct((T, d), BF16), jax.ShapeDtypeStruct((8, d), F32),
                   jax.ShapeDtypeStruct((8, d), F32)],
        compiler_params=_cparams(("arbitrary",)),
    )(dy, xhat, rstd, g, *deps)


def _loss_head(y, target, *, tm=512):
    T, d = y.shape
    tm = _pick(T, (tm, 256, 128, 8))
    n = T // tm

    def body(y_ref, t_ref, loss_ref, dy_ref, acc_ref):
        i = pl.program_id(0)

        @pl.when(i == 0)
        def _():
            acc_ref[...] = jnp.zeros_like(acc_ref)

        e = y_ref[...] - t_ref[...]
        dy_ref[...] = e * (1.0 / d)
        acc_ref[...] += jnp.sum(e * e, axis=0, keepdims=True)

        @pl.when(i == n - 1)
        def _():
            loss_ref[...] = (0.5 / d) * jnp.sum(acc_ref[...], axis=1, keepdims=True)

    row = lambda i: (i, 0)
    return pl.pallas_call(
        body,
        name="loss_head",
        grid=(n,),
        in_specs=[pl.BlockSpec((tm, d), row), pl.BlockSpec((tm, d), row)],
        out_specs=[pl.BlockSpec((1, 1), lambda i: (0, 0)), pl.BlockSpec((tm, d), row)],
        out_shape=[jax.ShapeDtypeStruct((1, 1), F32), jax.ShapeDtypeStruct((T, d), F32)],
        scratch_shapes=[pltpu.VMEM((1, d), F32)],
        compiler_params=_cparams(("arbitrary",)),
    )(y, target)


def _lower_bounds_fwd(lower_bounds):
    depth, n = lower_bounds.shape

    def body(x_ref, soft_ref, lb_ref):
        x = x_ref[...]
        e = jnp.exp(x - jnp.max(x, axis=0, keepdims=True))
        soft_ref[...] = e / jnp.sum(e, axis=0, keepdims=True)
        run = None
        for l in range(depth):
            run = soft_ref[l:l + 1, :] if run is None else run + soft_ref[l:l + 1, :]
            lb_ref[l:l + 1, :] = run - soft_ref[0:1, :]

    return pl.pallas_call(body, name="lower_bounds_fwd",
                          out_shape=[jax.ShapeDtypeStruct((depth, n), F32), jax.ShapeDtypeStruct((depth, n), F32)])(lower_bounds)


def _lower_bounds_bwd(soft, dlb):
    depth, n = soft.shape

    def body(soft_ref, dlb_ref, out_ref, dsoft_ref):
        total = jnp.sum(dlb_ref[...], axis=0, keepdims=True)
        run = None
        for l in reversed(range(depth)):
            run = dlb_ref[l:l + 1, :] if run is None else run + dlb_ref[l:l + 1, :]
            dsoft_ref[l:l + 1, :] = run - total if l == 0 else run
        s, ds = soft_ref[...], dsoft_ref[...]
        out_ref[...] = s * (ds - jnp.sum(s * ds, axis=0, keepdims=True))

    return pl.pallas_call(body, name="lower_bounds_bwd", out_shape=jax.ShapeDtypeStruct((depth, n), F32),
                          scratch_shapes=[pltpu.VMEM((depth, n), F32)])(soft, dlb)


def _layer_fwd(x0, x0b, mem2, lb, w_in, rest_fn, *, bl, seq, alpha, deps=()):
    p = _matmul("proj_in", x0b, w_in, mode="nn", out_dtype=BF16, deps=deps, tm=1024, tn=1792)
    wts = dict(rest_fn(p), w_in=w_in)
    mk = _matmul("mem_k", mem2, wts["w_mem_k"], mode="nn", out_dtype=BF16)
    mv = _matmul("mem_v", mem2, wts["w_mem_v"], mode="nn", out_dtype=BF16)
    y, st, opre = _mixer_fwd(p, mk, mv, lb, wts["conv_w"], wts["hg_norm_w"], bl=bl, seq=seq)
    r, merged, xhat1, rstd1, x1b = _merge_fwd(y, p, x0, wts["w_branch"], wts["w_o"], wts["b_gate"], wts["ln1_g"], wts["ln1_b"],
                                              alpha=alpha)
    a, xhat2, rstd2, x2, x2b = _mlp_fwd(xhat1, wts["ln1_g"], wts["ln1_b"], wts["w_up"], wts["w_down"], wts["ln2_g"], wts["ln2_b"],
                                        alpha=alpha)
    saved = dict(x0b=x0b, p=p, mk=mk, mv=mv, y=y, st=st, opre=opre, r=r, merged=merged, xhat1=xhat1, rstd1=rstd1, x1b=x1b, a=a,
                 xhat2=xhat2, rstd2=rstd2)
    return x2, x2b, saved, wts


def _relu2_bf16(a):
    return jnp.square(jnp.maximum(a.astype(F32), 0.0)).astype(BF16)


def _mlp_bwd(dz2, dz2b, sv, wts, *, alpha, deps=()):
    g = {}
    da = _matmul("mlp_da", dz2b, wts["w_down"], mode="nt", out_dtype=BF16, tm=1024, deps=deps,
                 epi_fn=lambda acc, a: (acc * (2.0 * jnp.maximum(a.astype(F32), 0.0)),), epi_extra=(sv["a"],))
    g["w_down"] = _matmul_tn("grad_w_down", sv["a"], dz2b, a_fn=_relu2_bf16, out_dtype=BF16, tt=2048)
    g["w_up"] = _matmul_tn("grad_w_up", sv["x1b"], da, out_dtype=BF16, tt=2048)
    dx1 = _matmul("mlp_dx", da, wts["w_up"], mode="nt", epi_fn=lambda acc, dz: (acc + alpha * dz,), epi_extra=(dz2,),
                  tm=512, tk=4096)
    dz1, dz1b, dg1, db1 = _ln_bwd(dx1, sv["xhat1"], sv["rstd1"], wts["ln1_g"])
    g["ln1_g"], g["ln1_b"] = dg1[0:1], db1[0:1]
    return dz1, dz1b, g


def _mix_bwd(dz1, dz1b, sv, mem2, lb, wts, *, bl, seq, alpha, send, below=None, deps=()):
    d = dz1.shape[1]
    g = {}
    g["w_o"] = _matmul_tn("grad_w_o", sv["merged"], dz1b, out_dtype=BF16, tt=2048, deps=deps)
    dr, dp, dy, dbg = _merge_bwd(dz1b, sv["p"], sv["r"], wts["w_branch"], wts["w_o"], wts["b_gate"])
    g["b_gate"] = dbg[0:1]
    g["w_branch"] = jnp.concatenate(
        [_matmul_tn("grad_w_branch", sv["y"], dr, a_cols=(i * W, W), b_cols=(i * d, d), out_dtype=BF16) for i in range(N_BRANCH)],
        axis=0)
    token = send(("w_o", "w_branch"), g)
    dp, dmk, dmv, dcw, dnw, dlb = _mixer_bwd(sv["p"], dy, dp, sv["st"], sv["opre"], sv["mk"], sv["mv"], lb,
                                              wts["conv_w"], wts["hg_norm_w"], bl=bl, seq=seq, deps=(token,))
    g["conv_w"], g["hg_norm_w"], g["lb"] = dcw[0:CONV_K], dnw[0:1], dlb[0:1]
    g["w_mem_k"] = _matmul_tn("grad_w_mem_k", mem2, dmk, out_dtype=BF16)
    g["w_mem_v"] = _matmul_tn("grad_w_mem_v", mem2, dmv, out_dtype=BF16)
    g["w_in"] = _matmul_tn("grad_w_in", sv["x0b"], dp, out_dtype=BF16, tt=2048)
    token = send(("w_in", "w_mem_k", "w_mem_v", "conv_w"), g)
    dx0 = _matmul("proj_in_dx", dp, wts["w_in"], mode="nt", epi_fn=lambda acc, dz: (acc + alpha * dz,), epi_extra=(dz1,),
                  tm=1024, tk=1792, deps=(token,))
    return (dx0 if below is None else _ln_bwd(dx0, *below)), g


N_CHIPS = 4
MESH_IDS = pl.DeviceIdType.MESH


def _axis_slice(ref, axis, start, size):
    idx = [slice(None)] * len(ref.shape)
    idx[axis] = pl.ds(start, size)
    return ref.at[tuple(idx)]


def _chip_exchange(name, items):
    n = len(items)
    out_shapes, meta = [], []
    for arr, kind, axis in items:
        shp = list(arr.shape)
        if kind == "gather":
            per = shp[axis]
            shp[axis] = per * N_CHIPS
            out_shapes.append(jax.ShapeDtypeStruct(tuple(shp), arr.dtype))
        elif kind == "scatter":
            per = shp[axis] // N_CHIPS
            shp[axis] = per
            out_shapes.append(jax.ShapeDtypeStruct((N_CHIPS, *shp), arr.dtype))
        else:
            per = None
            out_shapes.append(jax.ShapeDtypeStruct((N_CHIPS, *shp), arr.dtype))
        meta.append((kind, axis, per))

    def body(*refs):
        ins, outs = refs[:n], refs[n:2 * n]
        send_sems, recv_sems, local_sems = refs[2 * n:]
        x, y, c = lax.axis_index("x"), lax.axis_index("y"), lax.axis_index("c")
        me = 2 * x + y
        peers = [(1 - x, y), (x, 1 - y), (1 - x, 1 - y)]

        def src_for(t, chip):
            kind, axis, per = meta[t]
            return _axis_slice(ins[t], axis, chip * per, per) if kind == "scatter" else ins[t]

        def dst_from(t, chip):
            kind, axis, per = meta[t]
            return _axis_slice(outs[t], axis, chip * per, per) if kind == "gather" else outs[t].at[chip]

        def remote(t, k):
            px, py = peers[k]
            return pltpu.make_async_remote_copy(
                src_ref=src_for(t, 2 * px + py), dst_ref=dst_from(t, me), send_sem=send_sems.at[t * 3 + k],
                recv_sem=recv_sems.at[t * 3 + k], device_id=(px, py, c), device_id_type=MESH_IDS)

        def arrival(t, k):
            px, py = peers[k]
            return pltpu.make_async_remote_copy(
                src_ref=src_for(t, me), dst_ref=dst_from(t, 2 * px + py), send_sem=send_sems.at[t * 3 + k],
                recv_sem=recv_sems.at[t * 3 + k], device_id=(px, py, c), device_id_type=MESH_IDS)

        sends = [remote(t, k) for t in range(n) for k in range(3)]
        for cp in sends:
            cp.start()
        own = [pltpu.make_async_copy(src_for(t, me), dst_from(t, me), local_sems.at[t]) for t in range(n)]
        for cp in own:
            cp.start()
        for t in range(n):
            for k in range(3):
                arrival(t, k).wait_recv()
        for cp in sends:
            cp.wait_send()
        for cp in own:
            cp.wait()

    any_spec = pl.BlockSpec(memory_space=pl.ANY)
    return pl.pallas_call(
        body,
        name=name,
        in_specs=[any_spec] * n,
        out_specs=[any_spec] * n,
        out_shape=out_shapes,
        scratch_shapes=[pltpu.SemaphoreType.DMA((3 * n,)), pltpu.SemaphoreType.DMA((3 * n,)), pltpu.SemaphoreType.DMA((n,))],
        compiler_params=pltpu.CompilerParams(has_side_effects=True),
    )(*[a for a, _, _ in items])


HBM_SPEC = pl.BlockSpec(memory_space=pltpu.HBM)
SEM_SPEC = pl.BlockSpec(memory_space=pltpu.SEMAPHORE)
N_PEERS = N_CHIPS - 1


def _my_chip():
    return (2 * lax.axis_index("x") + lax.axis_index("y")).astype(jnp.int32).reshape(1)


def _own_block_spec(r, c, axis, tr):
    if axis == 1:
        return pl.BlockSpec((tr, c), lambda i, me: (i, me[0]))
    return pl.BlockSpec((tr, c), lambda i, me: (me[0] * (r // tr) + i, 0))


def _place_shard(name, shard, axis, me):
    r, c = shard.shape
    tr = _row_block(r, c, shard.dtype.itemsize)
    shp = (r, c * N_CHIPS) if axis == 1 else (r * N_CHIPS, c)

    def body(me_ref, s_ref, o_ref):
        del me_ref
        o_ref[...] = s_ref[...]

    return pl.pallas_call(
        body, name=name,
        grid_spec=pltpu.PrefetchScalarGridSpec(
            num_scalar_prefetch=1, grid=(r // tr,),
            in_specs=[pl.BlockSpec((tr, c), lambda i, me: (i, 0))], out_specs=_own_block_spec(r, c, axis, tr)),
        out_shape=jax.ShapeDtypeStruct(shp, shard.dtype),
        compiler_params=_cparams(("parallel",)),
    )(me, shard)


class _Split:
    def __init__(self, name, items):
        self.name, self.n = name, len(items)
        self.srcs = [a for a, _, _ in items]
        self.meta, self.land_shapes = [], []
        for arr, kind, axis in items:
            shp = list(arr.shape)
            if kind == "gather":
                per = shp[axis]
                shp[axis] = per * N_CHIPS
                self.land_shapes.append(jax.ShapeDtypeStruct(tuple(shp), arr.dtype))
            else:
                per = shp[axis] // N_CHIPS
                shp[axis] = per
                self.land_shapes.append(jax.ShapeDtypeStruct((N_PEERS, *shp), arr.dtype))
            self.meta.append((kind, axis, per))

    def _src(self, ins, t, chip):
        kind, axis, per = self.meta[t]
        return _axis_slice(ins[t], axis, chip * per, per) if kind == "scatter" else ins[t]

    def _dst(self, lands, t, chip, slot):
        kind, axis, per = self.meta[t]
        return _axis_slice(lands[t], axis, chip * per, per) if kind == "gather" else lands[t].at[slot]

    def landing_zones(self, me):
        return [_place_shard(self.name + "_own", src, axis, me) if kind == "gather" else lax.empty(ls.shape, ls.dtype)
                for src, ls, (kind, axis, _) in zip(self.srcs, self.land_shapes, self.meta)]

    def _copies(self, ins, lands, send_sems, recv_sems, arrivals):
        x, y, c = lax.axis_index("x"), lax.axis_index("y"), lax.axis_index("c")
        me = 2 * x + y
        peers = [(1 - x, y), (x, 1 - y), (1 - x, 1 - y)]
        res = []
        for t in range(self.n):
            for k, (px, py) in enumerate(peers):
                theirs = 2 * px + py
                sems = dict(send_sem=send_sems.at[t * N_PEERS + k], recv_sem=recv_sems.at[t * N_PEERS + k],
                            device_id=(px, py, c), device_id_type=MESH_IDS)
                if arrivals:
                    res.append(pltpu.make_async_remote_copy(src_ref=self._src(ins, t, me), dst_ref=self._dst(lands, t, theirs, k), **sems))
                else:
                    res.append(pltpu.make_async_remote_copy(src_ref=self._src(ins, t, theirs), dst_ref=self._dst(lands, t, me, k), **sems))
        return res

    def start(self, lands, deps=()):
        n, nd = self.n, len(deps)

        def body(*refs):
            ins, lnd = refs[:n], refs[n:2 * n]
            send_sems, recv_sems = refs[2 * n + nd], refs[2 * n + nd + 1]
            token = refs[-1]
            for cp in self._copies(ins, lnd, send_sems, recv_sems, arrivals=False):
                cp.start()
            token[...] = jnp.zeros_like(token)

        hbm = lambda a: pltpu.HBM(a.shape, a.dtype)
        res = pl.pallas_call(
            body, name=self.name + "_start",
            in_specs=[HBM_SPEC] * (2 * n) + [ANY_SPEC] * nd,
            out_specs=[SEM_SPEC, SEM_SPEC] + [HBM_SPEC] * (2 * n) + [pl.BlockSpec(memory_space=pltpu.VMEM)],
            out_shape=[pltpu.SemaphoreType.DMA((N_PEERS * n,)), pltpu.SemaphoreType.DMA((N_PEERS * n,))]
            + [hbm(a) for a in self.srcs] + [hbm(a) for a in self.land_shapes] + [jax.ShapeDtypeStruct((8, 128), F32)],
            input_output_aliases={i: 2 + i for i in range(2 * n)},
            compiler_params=pltpu.CompilerParams(has_side_effects=pltpu.SideEffectType.DATAFLOW_SIDE_EFFECTING),
        )(*[pltpu.with_memory_space_constraint(a, pltpu.HBM) for a in self.srcs],
          *[pltpu.with_memory_space_constraint(a, pltpu.HBM) for a in lands], *deps)
        return res[:-1], res[-1]

    def wait(self, state, after):
        n = self.n
        send_sems, recv_sems = state[0], state[1]
        srcs, lands = state[2:2 + n], state[2 + n:2 + 2 * n]

        def body(*refs):
            ins, lnd = refs[:n], refs[n:2 * n]
            s_sems, r_sems = refs[2 * n], refs[2 * n + 1]
            for cp in self._copies(ins, lnd, s_sems, r_sems, arrivals=True):
                cp.wait_recv()
            for cp in self._copies(ins, lnd, s_sems, r_sems, arrivals=False):
                cp.wait_send()

        hbm = lambda a: pltpu.HBM(a.shape, a.dtype)
        res = pl.pallas_call(
            body, name=self.name + "_wait",
            in_specs=[HBM_SPEC] * (2 * n) + [SEM_SPEC, SEM_SPEC, ANY_SPEC],
            out_specs=[HBM_SPEC] * (2 * n),
            out_shape=[hbm(a) for a in self.srcs] + [hbm(a) for a in self.land_shapes],
            input_output_aliases={i: i for i in range(2 * n)},
            compiler_params=pltpu.CompilerParams(has_side_effects=pltpu.SideEffectType.DATAFLOW_SIDE_EFFECTING),
        )(*srcs, *lands, send_sems, recv_sems, after)
        return res[:n], res[n:]


def _sibling_swap(name, arrays):
    n = len(arrays)

    def body(*refs):
        ins, outs = refs[:n], refs[n:2 * n]
        send_sems, recv_sems = refs[2 * n:]
        sibling = (lax.axis_index("x"), lax.axis_index("y"), 1 - lax.axis_index("c"))
        copies = [pltpu.make_async_remote_copy(src_ref=ins[t], dst_ref=outs[t], send_sem=send_sems.at[t], recv_sem=recv_sems.at[t],
                                               device_id=sibling, device_id_type=MESH_IDS) for t in range(n)]
        for cp in copies:
            cp.start()
        for cp in copies:
            cp.wait()

    any_spec = pl.BlockSpec(memory_space=pl.ANY)
    return pl.pallas_call(
        body,
        name=name,
        in_specs=[any_spec] * n,
        out_specs=[any_spec] * n,
        out_shape=[jax.ShapeDtypeStruct(a.shape, a.dtype) for a in arrays],
        scratch_shapes=[pltpu.SemaphoreType.DMA((n,)), pltpu.SemaphoreType.DMA((n,))],
        compiler_params=pltpu.CompilerParams(has_side_effects=True),
    )(*arrays)


def _row_block(r, c, itemsize=4, target=1 << 20):
    if r % 8 != 0:
        return r
    best = 8
    for tr in range(8, r + 1, 8):
        if r % tr == 0 and tr * c * itemsize <= target:
            best = tr
    return best


def _sum_chips_into(parts, stacked, layer):
    _, r, c = parts.shape
    tr = _row_block(r, c)

    def body(p_ref, s_ref, o_ref):
        del s_ref
        o_ref[...] = ((p_ref[0] + p_ref[1]) + p_ref[2]) + p_ref[3]

    return pl.pallas_call(
        body,
        name="sum_chips",
        grid=(r // tr,),
        in_specs=[pl.BlockSpec((N_CHIPS, tr, c), lambda i: (0, i, 0)), pl.BlockSpec(memory_space=pl.ANY)],
        out_specs=pl.BlockSpec((None, tr, c), lambda i: (layer, i, 0)),
        out_shape=jax.ShapeDtypeStruct(stacked.shape, stacked.dtype),
        input_output_aliases={1: 0},
        compiler_params=_cparams(("parallel",)),
    )(parts, stacked)


def _sum_own_and_peers(me, g, axis, landed):
    _, r, c = landed.shape
    tr = _row_block(r, c)

    def body(me_ref, g_ref, p_ref, o_ref):
        del me_ref
        o_ref[...] = ((g_ref[...].astype(F32) + p_ref[0].astype(F32)) + p_ref[1].astype(F32)) + p_ref[2].astype(F32)

    return pl.pallas_call(
        body, name="sum_chips_own",
        grid_spec=pltpu.PrefetchScalarGridSpec(
            num_scalar_prefetch=1, grid=(r // tr,),
            in_specs=[_own_block_spec(r, c, axis, tr), pl.BlockSpec((N_PEERS, tr, c), lambda i, me: (0, i, 0))],
            out_specs=pl.BlockSpec((tr, c), lambda i, me: (i, 0))),
        out_shape=jax.ShapeDtypeStruct((r, c), F32),
        compiler_params=_cparams(("parallel",)),
    )(me, g, landed)


def _adamw_math(w, m, v, g):
    m_new = ADAM_B1 * m + (1.0 - ADAM_B1) * g
    v_new = ADAM_B2 * v + (1.0 - ADAM_B2) * jnp.square(g)
    m_hat = m_new / (1.0 - ADAM_B1 ** ADAM_STEP)
    v_hat = v_new / (1.0 - ADAM_B2 ** ADAM_STEP)
    return -ADAM_LR * (m_hat / (jnp.sqrt(v_hat) + ADAM_EPS) + ADAM_WD * w), m_new, v_new


def _adamw(w, m, v, g_a, g_b):
    L, r, c = w.shape
    tr = _row_block(r, c, target=1 << 19)

    def body(w_ref, m_ref, v_ref, ga_ref, gb_ref, g_ref, d_ref, nm_ref, nv_ref):
        g = ga_ref[...] + gb_ref[...]
        g_ref[...] = g
        d_ref[...], nm_ref[...], nv_ref[...] = _adamw_math(w_ref[...], m_ref[...], v_ref[...], g)

    spec = pl.BlockSpec((None, tr, c), lambda l, i: (l, i, 0))
    return pl.pallas_call(
        body,
        name="adamw",
        grid=(L, r // tr),
        in_specs=[spec] * 5,
        out_specs=[spec] * 4,
        out_shape=[jax.ShapeDtypeStruct(w.shape, F32)] * 4,
        compiler_params=_cparams(("parallel", "parallel")),
    )(w, m, v, g_a, g_b)


def _adamw_layer(w, m, v, g_a, g_b, layer, outs):
    L, r, c = w.shape
    tr = _row_block(r, c, target=1 << 19)
    n_prev = 0 if outs is None else 4

    def body(w_ref, m_ref, v_ref, ga_ref, gb_ref, *rest):
        g_ref, d_ref, nm_ref, nv_ref = rest[n_prev:]
        g = ga_ref[...] + gb_ref[...]
        g_ref[...] = g
        d_ref[...], nm_ref[...], nv_ref[...] = _adamw_math(w_ref[...], m_ref[...], v_ref[...], g)

    at_layer = pl.BlockSpec((None, tr, c), lambda i: (layer, i, 0))
    flat = pl.BlockSpec((tr, c), lambda i: (i, 0))
    return pl.pallas_call(
        body,
        name="adamw_layer",
        grid=(r // tr,),
        in_specs=[at_layer] * 3 + [flat] * 2 + [ANY_SPEC] * n_prev,
        out_specs=[at_layer] * 4,
        out_shape=[jax.ShapeDtypeStruct(w.shape, F32)] * 4,
        input_output_aliases={5 + k: k for k in range(n_prev)},
        compiler_params=_cparams(("parallel",)),
    )(w, m, v, g_a, g_b, *(outs or ()))


SHARDED = (("w_in", 1), ("conv_w", 1), ("w_mem_k", 0), ("w_mem_v", 0), ("w_branch", 1), ("w_o", 0), ("w_up", 1), ("w_down", 0))
SMALL = ("lower_bounds", "hg_norm_w", "b_gate", "ln1_g", "ln1_b", "ln2_g", "ln2_b")
WEIGHT_ORDER = ("lower_bounds", "w_in", "conv_w", "hg_norm_w", "w_mem_k", "w_mem_v", "w_branch", "b_gate", "w_o", "ln1_g", "ln1_b",
                "w_up", "w_down", "ln2_g", "ln2_b")


def kernel(x, mem, lower_bounds, w_in, conv_w, hg_norm_w, w_mem_k, w_mem_v, w_branch, b_gate, w_o, ln1_g, ln1_b, w_up, w_down, ln2_g, ln2_b, loss_target, m_lower_bounds, m_w_in, m_conv_w, m_hg_norm_w, m_w_mem_k, m_w_mem_v, m_w_branch, m_b_gate, m_w_o, m_ln1_g, m_ln1_b, m_w_up, m_w_down, m_ln2_g, m_ln2_b, v_lower_bounds, v_w_in, v_conv_w, v_hg_norm_w, v_w_mem_k, v_w_mem_v, v_w_branch, v_b_gate, v_w_o, v_ln1_g, v_ln1_b, v_w_up, v_w_down, v_ln2_g, v_ln2_b):
    bl, seq, d = x.shape
    depth = w_in.shape[0]
    weights = dict(lower_bounds=lower_bounds, w_in=w_in, conv_w=conv_w, hg_norm_w=hg_norm_w, w_mem_k=w_mem_k, w_mem_v=w_mem_v,
                   w_branch=w_branch, b_gate=b_gate, w_o=w_o, ln1_g=ln1_g, ln1_b=ln1_b, w_up=w_up, w_down=w_down, ln2_g=ln2_g, ln2_b=ln2_b)
    mom_m = dict(lower_bounds=m_lower_bounds, w_in=m_w_in, conv_w=m_conv_w, hg_norm_w=m_hg_norm_w, w_mem_k=m_w_mem_k, w_mem_v=m_w_mem_v,
                 w_branch=m_w_branch, b_gate=m_b_gate, w_o=m_w_o, ln1_g=m_ln1_g, ln1_b=m_ln1_b, w_up=m_w_up, w_down=m_w_down,
                 ln2_g=m_ln2_g, ln2_b=m_ln2_b)
    mom_v = dict(lower_bounds=v_lower_bounds, w_in=v_w_in, conv_w=v_conv_w, hg_norm_w=v_hg_norm_w, w_mem_k=v_w_mem_k, w_mem_v=v_w_mem_v,
                 w_branch=v_w_branch, b_gate=v_b_gate, w_o=v_w_o, ln1_g=v_ln1_g, ln1_b=v_ln1_b, w_up=v_w_up, w_down=v_w_down,
                 ln2_g=v_ln2_g, ln2_b=v_ln2_b)

    def shard2d(name, l):
        w = weights[name][l]
        if name == "w_branch":
            return w.reshape(N_BRANCH * W, w.shape[-1]).astype(BF16)
        return w if name == "conv_w" else w.astype(BF16)

    me = _my_chip()

    shard_axis = dict(SHARDED)

    def start_exchange(name, kind, items, deps=()):
        ex = _Split(name, [(arr, kind, shard_axis[nm]) for nm, arr in items])
        state, token = ex.start(ex.landing_zones(me), deps)
        return ex, state, [nm for nm, _ in items], token

    def start_gathers(l, deps=()):
        first = start_exchange(f"gather_in_l{l}", "gather", [("w_in", shard2d("w_in", l))], deps)
        rest = start_exchange(f"gather_rest_l{l}", "gather", [(nm, shard2d(nm, l)) for nm, _ in SHARDED if nm != "w_in"],
                              (first[3],))
        return first, rest

    def gathered(pend, after):
        ex, state, names, _ = pend
        return dict(zip(names, ex.wait(state, after=after)[1]))

    x2d, mem2, t2d = x.reshape(bl * seq, d), mem.reshape(-1, d), loss_target.reshape(bl * seq, d)
    alpha = (2.0 * depth) ** 0.25
    soft, lb_all = _lower_bounds_fwd(lower_bounds)

    h, hb, saved, layer_wts = x2d, x2d.astype(BF16), [], []
    pending = start_gathers(0)
    for l in range(depth):
        first, rest = pending
        w_in_l = gathered(first, h)["w_in"]

        def rest_fn(after, l=l, rest=rest):
            wts = gathered(rest, after)
            for name in ("hg_norm_w", "b_gate", "ln1_g", "ln1_b", "ln2_g", "ln2_b"):
                wts[name] = weights[name][l][None, :]
            return wts

        deps = (rest[3],)
        if l + 1 < depth:
            pending = start_gathers(l + 1, (w_in_l, rest[3]))
            deps += (pending[0][3], pending[1][3])
        h, hb, sv, wts = _layer_fwd(h, hb, mem2, lb_all[l:l + 1], w_in_l, rest_fn, bl=bl, seq=seq, alpha=alpha, deps=deps)
        saved.append(sv)
        layer_wts.append(wts)
    loss, dh = _loss_head(h, t2d)

    shape3 = {name: (depth, weights[name].size // (depth * weights[name].shape[-1]), weights[name].shape[-1]) for name, _ in SHARDED}
    partial = [dict() for _ in range(depth)]
    smalls = [None] * depth
    outs = {name: None for name, _ in SHARDED}

    def finish_reduce(pend, l, after):
        ex, state, names, _ = pend
        sent, got = ex.wait(state, after=after)
        for nm, g_full, landed in zip(names, sent, got):
            partial[l][nm] = _sum_own_and_peers(me, g_full, shard_axis[nm], landed)

    def optimizer_step(l):
        names = [name for name, _ in SHARDED]
        theirs = _sibling_swap(f"swap_partials_l{l}", [partial[l][nm] for nm in names])
        for nm, other in zip(names, theirs):
            outs[nm] = _adamw_layer(weights[nm].reshape(shape3[nm]), mom_m[nm].reshape(shape3[nm]), mom_v[nm].reshape(shape3[nm]),
                                    partial[l][nm], other, l, outs[nm])
        return tuple(outs[nm][0] for nm in names)

    pending_mix, deps = [], ()
    dz2, dz2b, dg2, db2 = _ln_bwd(dh, saved[-1]["xhat2"], saved[-1]["rstd2"], layer_wts[-1]["ln2_g"])
    for l in reversed(range(depth)):
        dz1, dz1b, g_mlp = _mlp_bwd(dz2, dz2b, saved[l], layer_wts[l], alpha=alpha, deps=deps)
        g_mlp["ln2_g"], g_mlp["ln2_b"] = dg2[0:1], db2[0:1]
        pending_mlp = start_exchange(f"reduce_mlp_l{l}", "scatter", [(nm, g_mlp[nm]) for nm in ("w_up", "w_down")])
        deps = (pending_mlp[3],)
        if pending_mix:
            for pend in pending_mix:
                finish_reduce(pend, l + 1, dz1)
            deps += optimizer_step(l + 1)
        pending_mix = []

        def send(names, g, l=l, pending_mix=pending_mix):
            pend = start_exchange(f"reduce_{names[0]}_l{l}", "scatter", [(nm, g[nm]) for nm in names])
            pending_mix.append(pend)
            return pend[3]

        below = (saved[l - 1]["xhat2"], saved[l - 1]["rstd2"], layer_wts[l - 1]["ln2_g"]) if l > 0 else None
        out, g = _mix_bwd(dz1, dz1b, saved[l], mem2, lb_all[l:l + 1], layer_wts[l], bl=bl, seq=seq, alpha=alpha, send=send,
                          below=below, deps=deps)
        if l > 0:
            dz2, dz2b, dg2, db2 = out
        else:
            dh = out
        finish_reduce(pending_mlp, l, out[0] if l > 0 else out)
        deps = ()
        g.update(g_mlp, lower_bounds=g["lb"])
        smalls[l] = jnp.concatenate([g[nm] for nm in SMALL], axis=1)
    small_parts = _chip_exchange("reduce_small", [(jnp.stack(smalls), "bcast", 0)])[0]
    small_sum = _sum_chips_into(small_parts.reshape(N_CHIPS, depth, -1), jnp.zeros((1, depth, small_parts.shape[-1]), F32), 0)
    small_sum = small_sum.reshape(depth, 1, -1)
    small_theirs = _sibling_swap("swap_small", [small_sum])[0]
    for pend in pending_mix:
        finish_reduce(pend, 0, small_theirs)
    optimizer_step(0)

    outs = {name: [r.reshape(weights[name].shape) for r in res] for name, res in outs.items()}
    off = 0
    for name in SMALL:
        n = weights[name].shape[1]
        mine, other = small_sum[:, :, off:off + n], small_theirs[:, :, off:off + n]
        off += n
        if name == "lower_bounds":
            mine = _lower_bounds_bwd(soft, mine[:, 0, :])[:, None, :]
            other = _lower_bounds_bwd(soft, other[:, 0, :])[:, None, :]
        shp = (depth, 1, n)
        res = _adamw(weights[name].reshape(shp), mom_m[name].reshape(shp), mom_v[name].reshape(shp), mine, other)
        outs[name] = [r.reshape(weights[name].shape) for r in res]
    assert off == small_sum.shape[-1]

    total_loss = lax.psum(loss[0, 0], ("x", "y", "c"))
    result = [total_loss, dh.reshape(bl, seq, d)]
    for k in range(4):
        result += [outs[name][k] for name in WEIGHT_ORDER]
    return tuple(result)
```

```python
import functools

import jax
import jax.numpy as jnp
from jax import lax
from jax.experimental import pallas as pl
from jax.experimental.pallas import tpu as pltpu

F32 = jnp.float32
BF16 = jnp.bfloat16

HG_HEADS = 4
HG_F = 128
HG_CHUNK = 32
MEM_HEADS = 4
MEM_HEAD_DIM = 128
BRANCH_WIDTH = 512
N_BRANCH = 3
CONV_K = 3
LN_EPS = 1e-5
RMS_EPS = 1e-6
ADAM_LR = 0.001
ADAM_B1 = 0.9
ADAM_B2 = 0.999
ADAM_EPS = 1e-08
ADAM_WD = 0.01
ADAM_STEP = 10

VMEM_LIMIT = 48 * 1024 * 1024


def _cparams(sem):
    return pltpu.CompilerParams(dimension_semantics=sem, vmem_limit_bytes=VMEM_LIMIT)


def _dot(a, b, dims):
    return lax.dot_general(a, b, (dims, ((), ())), preferred_element_type=F32)


NN = ((1,), (0,))
NT = ((1,), (1,))
TN = ((0,), (0,))


def _pick(n, pref):
    for t in pref:
        if n % t == 0:
            return t
    return n


ANY_SPEC = pl.BlockSpec(memory_space=pl.ANY)


def _matmul(name, a, b, *, mode, out_dtype=F32, a_fn=None, a_extra=(), epi_fn=None, epi_extra=(), n_out=1, out_kinds=None,
            tm=512, tn=1024, tk=1024, deps=()):
    M, K = a.shape
    N = b.shape[1] if mode == "nn" else b.shape[0]
    tm, tn, tk = _pick(M, (tm, 256, 128, 8)), _pick(N, (tn, 896, 512, 256, 128)), _pick(K, (tk, 512, 256, 128))
    nk = K // tk
    n_ax, n_ex = len(a_extra), len(epi_extra)
    n_in = 2 + n_ax + n_ex + len(deps)
    out_dtypes = out_dtype if isinstance(out_dtype, (tuple, list)) else (out_dtype,) * n_out
    out_kinds = out_kinds or ("tile",) * n_out

    def body(*refs):
        a_ref, b_ref = refs[0], refs[1]
        ax_refs = refs[2:2 + n_ax]
        ex_refs = refs[2 + n_ax:2 + n_ax + n_ex]
        o_refs = refs[n_in:n_in + n_out]
        at = a_ref[...]
        at = a_fn(at, *[r[...] for r in ax_refs]) if a_fn is not None else at.astype(BF16)
        part = _dot(at, b_ref[...].astype(BF16), NN if mode == "nn" else NT)

        def finish(acc):
            outs = epi_fn(acc, *[r[...] for r in ex_refs]) if epi_fn is not None else (acc,)
            for o_ref, o, kind in zip(o_refs, outs, out_kinds):
                if kind == "rowsum":
                    @pl.when(pl.program_id(1) == 0)
                    def _(o_ref=o_ref):
                        o_ref[...] = jnp.zeros_like(o_ref)

                    o_ref[0:1, :] += o
                else:
                    o_ref[...] = o.astype(o_ref.dtype)

        if nk == 1:
            finish(part)
            return
        acc_ref = refs[-1]
        k = pl.program_id(2)

        @pl.when(k == 0)
        def _():
            acc_ref[...] = part

        @pl.when(jnp.logical_and(k > 0, k < nk - 1))
        def _():
            acc_ref[...] += part

        @pl.when(k == nk - 1)
        def _():
            finish(acc_ref[...] + part)

    b_mode = dict(pipeline_mode=pl.Buffered(1)) if (nk == 1 and N == tn) else {}
    in_specs = [pl.BlockSpec((tm, tk), lambda j, i, k: (i, k)),
                pl.BlockSpec((tk, tn), lambda j, i, k: (k, j), **b_mode) if mode == "nn"
                else pl.BlockSpec((tn, tk), lambda j, i, k: (j, k), **b_mode)]
    in_specs += [pl.BlockSpec((1, tk), lambda j, i, k: (0, k)) for _ in a_extra]
    for e in epi_extra:
        if e.shape[0] == 1:
            in_specs.append(pl.BlockSpec((1, tn), lambda j, i, k: (0, j)))
        elif e.shape[1] == 1:
            in_specs.append(pl.BlockSpec((tm, 1), lambda j, i, k: (i, 0)))
        else:
            in_specs.append(pl.BlockSpec((tm, tn), lambda j, i, k: (i, j)))
    in_specs += [ANY_SPEC] * len(deps)
    out_specs, out_shapes = [], []
    for kind, dt in zip(out_kinds, out_dtypes):
        if kind == "col":
            out_specs.append(pl.BlockSpec((tm, 1), lambda j, i, k: (i, 0)))
            out_shapes.append(jax.ShapeDtypeStruct((M, 1), dt))
        elif kind == "rowsum":
            out_specs.append(pl.BlockSpec((8, tn), lambda j, i, k: (0, j)))
            out_shapes.append(jax.ShapeDtypeStruct((8, N), dt))
        else:
            out_specs.append(pl.BlockSpec((tm, tn), lambda j, i, k: (i, j)))
            out_shapes.append(jax.ShapeDtypeStruct((M, N), dt))
    out = pl.pallas_call(
        body,
        name=name,
        grid=(N // tn, M // tm, nk),
        in_specs=in_specs,
        out_specs=out_specs,
        out_shape=out_shapes,
        scratch_shapes=[pltpu.VMEM((tm, tn), F32)] if nk > 1 else [],
        compiler_params=_cparams(("arbitrary", "arbitrary", "arbitrary")),
    )(a, b, *a_extra, *epi_extra, *deps)
    return out[0] if n_out == 1 else out


def _matmul_tn(name, a, b, *, a_fn=None, a_extra=(), a_cols=None, b_cols=None, ta=1024, tb=1024, tt=1024, out_dtype=F32, deps=()):
    T = a.shape[0]
    a0, Ka = a_cols if a_cols is not None else (0, a.shape[1])
    b0, Nb = b_cols if b_cols is not None else (0, b.shape[1])
    ta, tb, tt = _pick(Ka, (ta, 512, 256, 128)), _pick(Nb, (tb, 896, 512, 256, 128)), _pick(T, (tt, 512, 256, 128))
    assert a0 % ta == 0 and b0 % tb == 0
    a0, b0 = a0 // ta, b0 // tb
    nt = T // tt
    n_ax = len(a_extra)

    def body(*refs):
        a_ref, b_ref = refs[0], refs[1]
        ax_refs = refs[2:2 + n_ax]
        o_ref = refs[2 + n_ax + len(deps)]
        acc_ref = refs[-1]
        t = pl.program_id(2)
        at = a_ref[...]
        at = a_fn(at, *[r[...] for r in ax_refs]) if a_fn is not None else at.astype(BF16)
        part = _dot(at, b_ref[...].astype(BF16), TN)

        @pl.when(t == 0)
        def _():
            acc_ref[...] = part

        @pl.when(jnp.logical_and(t > 0, t < nt - 1))
        def _():
            acc_ref[...] += part

        @pl.when(t == nt - 1)
        def _():
            o_ref[...] = (acc_ref[...] + part if nt > 1 else part).astype(o_ref.dtype)

    in_specs = [pl.BlockSpec((tt, ta), lambda i, j, t: (t, a0 + i)), pl.BlockSpec((tt, tb), lambda i, j, t: (t, b0 + j))]
    in_specs += [pl.BlockSpec((1, ta), lambda i, j, t: (0, a0 + i)) for _ in a_extra]
    in_specs += [ANY_SPEC] * len(deps)
    return pl.pallas_call(
        body,
        name=name,
        grid=(Ka // ta, Nb // tb, nt),
        in_specs=in_specs,
        out_specs=pl.BlockSpec((ta, tb), lambda i, j, t: (i, j)),
        out_shape=jax.ShapeDtypeStruct((Ka, Nb), out_dtype),
        scratch_shapes=[pltpu.VMEM((ta, tb), F32)],
        compiler_params=_cparams(("parallel", "parallel", "arbitrary")),
    )(a, b, *a_extra, *deps)


W = BRANCH_WIDTH
C_CB, C_CC, C_CH, C_HQ, C_HF, C_HI, C_HG, C_MQ, N_MIX = 0, W, 2 * W, 3 * W, 4 * W, 5 * W, 6 * W, 7 * W, 8 * W
TS_MIX = 256
PREV_ROWS = 16


def _sigmoid(x):
    return jax.nn.sigmoid(x)


def _chunk_pos(shape):
    return lax.broadcasted_iota(jnp.int32, shape, 0) & (HG_CHUNK - 1)


def _seg_cumsum(x, pos):
    sh = 1
    while sh < HG_CHUNK:
        x = x + jnp.where(pos >= sh, pltpu.roll(x, sh, 0), 0.0)
        sh *= 2
    return x


def _seg_rev_cumsum(x, pos):
    n = x.shape[0]
    sh = 1
    while sh < HG_CHUNK:
        x = x + jnp.where(pos < HG_CHUNK - sh, pltpu.roll(x, n - sh, 0), 0.0)
        sh *= 2
    return x


def _chunk_mask(ts):
    r = lax.broadcasted_iota(jnp.int32, (ts, ts), 0)
    c = lax.broadcasted_iota(jnp.int32, (ts, ts), 1)
    return jnp.logical_and((r // HG_CHUNK) == (c // HG_CHUNK), c <= r)


def _hgrn_gates(p_ref, lb):
    q = p_ref[:, C_HQ:C_HQ + W].astype(F32)
    fl = p_ref[:, C_HF:C_HF + W].astype(F32)
    sig = _sigmoid(fl)
    f = lb + (1.0 - lb) * sig
    logf = jnp.log(f)
    k = (1.0 - lb) * _sigmoid(-fl)
    sq = _sigmoid(q)
    qs = q * sq
    return q, sq, qs, sig, f, logf, k


def _hgrn_decays(logf, bc_sc, ts):
    pos = _chunk_pos(logf.shape)
    bc = _seg_cumsum(logf, pos)
    bc_sc[...] = bc
    nc = ts // HG_CHUNK
    bref = jnp.concatenate(
        [jnp.broadcast_to(bc_sc[n * HG_CHUNK + HG_CHUNK // 2 - 1:n * HG_CHUNK + HG_CHUNK // 2, :], (HG_CHUNK, W)) for n in range(nc)], axis=0)
    blast = jnp.concatenate(
        [jnp.broadcast_to(bc_sc[(n + 1) * HG_CHUNK - 1:(n + 1) * HG_CHUNK, :], (HG_CHUNK, W)) for n in range(nc)], axis=0)
    return pos, bc, bref, blast


def _conv_shift_down(u, carry_ref, row):
    n = carry_ref.shape[0]
    last, before = carry_ref[n - 1:n, :], carry_ref[n - 2:n - 1, :]
    u1 = jnp.where(row == 0, last, pltpu.roll(u, 1, 0))
    u2 = jnp.where(row == 0, before, jnp.where(row == 1, last, pltpu.roll(u, 2, 0)))
    return u1, u2


def _attn_probs(qh, kh):
    s = _dot(qh, kh, NT) * (MEM_HEAD_DIM ** -0.5)
    e = jnp.exp(s - jnp.max(s, axis=-1, keepdims=True))
    return e / jnp.sum(e, axis=-1, keepdims=True)


def _mixer_fwd(p, mk, mv, lb, conv_w, norm_w, *, bl, seq):
    T = p.shape[0]
    ts = TS_MIX
    ns = seq // ts
    nc = ts // HG_CHUNK
    ml = mk.shape[0] // bl

    def body(p_ref, mk_ref, mv_ref, lb_ref, cw_ref, nw_ref, y_ref, st_ref, opre_ref, state_sc, carry_sc, bc_sc):
        @pl.when(pl.program_id(1) == 0)
        def _():
            state_sc[...] = jnp.zeros_like(state_sc)
            carry_sc[...] = jnp.zeros_like(carry_sc)

        cb, cc, ch = (p_ref[:, c0:c0 + W].astype(F32) for c0 in (C_CB, C_CC, C_CH))
        u = cc * ch
        row = lax.broadcasted_iota(jnp.int32, (ts, W), 0)
        u1, u2 = _conv_shift_down(u, carry_sc, row)
        yconv = u2 * cw_ref[0:1, :] + u1 * cw_ref[1:2, :] + u * cw_ref[2:3, :]
        y_ref[:, 0:W] = (cb * yconv).astype(BF16)
        carry_sc[...] = u[ts - 8:ts, :]

        lbv = lb_ref[...]
        _, _, qs, _, _, logf, k = _hgrn_gates(p_ref, lbv)
        pos, bc, bref, blast = _hgrn_decays(logf, bc_sc, ts)
        a_all = (qs * jnp.exp(bc - bref)).astype(BF16)
        bk_all = (k * jnp.exp(bref - bc)).astype(BF16)
        qin_all = (qs * jnp.exp(bc)).astype(BF16)
        kout_all = (k * jnp.exp(blast - bc)).astype(BF16)
        v_all = p_ref[:, C_HI:C_HI + W].astype(BF16)
        mask = _chunk_mask(ts)
        for h in range(HG_HEADS):
            hs = slice(h * HG_F, (h + 1) * HG_F)
            vb = v_all[:, hs]
            scores = jnp.where(mask, _dot(a_all[:, hs], bk_all[:, hs], NT), 0.0)
            o_intra = _dot(scores.astype(BF16), vb, NN)
            st = state_sc[h]
            o_inter = []
            for n in range(nc):
                rows = slice(n * HG_CHUNK, (n + 1) * HG_CHUNK)
                st_ref[n, h] = st
                o_inter.append(_dot(qin_all[rows, hs], st.astype(BF16), NT))
                kv = _dot(vb[rows], kout_all[rows, hs], TN)
                decay = jnp.exp(bc_sc[(n + 1) * HG_CHUNK - 1:(n + 1) * HG_CHUNK, hs])
                st = st * decay + kv
            state_sc[h] = st
            o = o_intra + jnp.concatenate(o_inter, axis=0)
            opre_ref[:, hs] = o
            on = o * lax.rsqrt(jnp.mean(o * o, axis=-1, keepdims=True) + RMS_EPS) * nw_ref[...]
            g = p_ref[:, C_HG + h * HG_F:C_HG + (h + 1) * HG_F].astype(F32)
            y_ref[:, W + h * HG_F:W + (h + 1) * HG_F] = (on * (g * _sigmoid(g))).astype(BF16)

        for h in range(MEM_HEADS):
            hs = slice(h * MEM_HEAD_DIM, (h + 1) * MEM_HEAD_DIM)
            qh = p_ref[:, C_MQ + h * MEM_HEAD_DIM:C_MQ + (h + 1) * MEM_HEAD_DIM].astype(BF16)
            prob = _attn_probs(qh, mk_ref[:, hs])
            y_ref[:, 2 * W + h * MEM_HEAD_DIM:2 * W + (h + 1) * MEM_HEAD_DIM] = _dot(prob.astype(BF16), mv_ref[:, hs], NN).astype(BF16)

    return pl.pallas_call(
        body,
        name="mixer_fwd",
        grid=(bl, ns),
        in_specs=[
            pl.BlockSpec((ts, N_MIX), lambda b, s: (b * ns + s, 0)),
            pl.BlockSpec((ml, W), lambda b, s: (b, 0)),
            pl.BlockSpec((ml, W), lambda b, s: (b, 0)),
            pl.BlockSpec((1, W), lambda b, s: (0, 0)),
            pl.BlockSpec((CONV_K, W), lambda b, s: (0, 0)),
            pl.BlockSpec((1, HG_F), lambda b, s: (0, 0)),
        ],
        out_specs=[
            pl.BlockSpec((ts, 3 * W), lambda b, s: (b * ns + s, 0)),
            pl.BlockSpec((nc, HG_HEADS, HG_F, HG_F), lambda b, s: (b * ns + s, 0, 0, 0)),
            pl.BlockSpec((ts, W), lambda b, s: (b * ns + s, 0)),
        ],
        out_shape=[
            jax.ShapeDtypeStruct((T, 3 * W), BF16),
            jax.ShapeDtypeStruct((T // HG_CHUNK, HG_HEADS, HG_F, HG_F), F32),
            jax.ShapeDtypeStruct((T, W), F32),
        ],
        scratch_shapes=[pltpu.VMEM((HG_HEADS, HG_F, HG_F), F32), pltpu.VMEM((8, W), F32), pltpu.VMEM((ts, W), F32)],
        compiler_params=_cparams(("arbitrary", "arbitrary")),
    )(p, mk, mv, lb, conv_w, norm_w)


def _mixer_bwd(p, dy, dp_gates, st, opre, mk, mv, lb, conv_w, norm_w, *, bl, seq, deps=()):
    T, nin = p.shape
    ts = TS_MIX
    ns = seq // ts
    nc = ts // HG_CHUNK
    ml = mk.shape[0] // bl
    mid, last = HG_CHUNK // 2 - 1, HG_CHUNK - 1

    def body(p_ref, pprev_ref, dy_ref, dpin_ref, st_ref, opre_ref, mk_ref, mv_ref, lb_ref, cw_ref, nw_ref, *rest):
        dp_ref, dmk_ref, dmv_ref, dcw_ref, dnw_ref, dlb_ref, dstate_sc, carry_sc, uprev_sc, bc_sc = rest[len(deps):]
        del dpin_ref
        b, s = pl.program_id(0), pl.program_id(1)

        @pl.when(s == 0)
        def _():
            dstate_sc[...] = jnp.zeros_like(dstate_sc)
            carry_sc[...] = jnp.zeros_like(carry_sc)
            dmk_ref[...] = jnp.zeros_like(dmk_ref)
            dmv_ref[...] = jnp.zeros_like(dmv_ref)

        @pl.when(jnp.logical_and(b == 0, s == 0))
        def _():
            dcw_ref[...] = jnp.zeros_like(dcw_ref)
            dnw_ref[...] = jnp.zeros_like(dnw_ref)
            dlb_ref[...] = jnp.zeros_like(dlb_ref)

        cb, cc, ch = (p_ref[:, c0:c0 + W].astype(F32) for c0 in (C_CB, C_CC, C_CH))
        u = cc * ch
        row = lax.broadcasted_iota(jnp.int32, (ts, W), 0)
        uprev = pprev_ref[:, C_CC:C_CC + W].astype(F32) * pprev_ref[:, C_CH:C_CH + W].astype(F32)
        uprev_sc[...] = jnp.where(s == ns - 1, 0.0, uprev)
        u1, u2 = _conv_shift_down(u, uprev_sc, row)
        w0, w1, w2 = cw_ref[0:1, :], cw_ref[1:2, :], cw_ref[2:3, :]
        dya = dy_ref[:, 0:W].astype(F32)
        dp_ref[:, C_CB:C_CB + W] = (dya * (u2 * w0 + u1 * w1 + u * w2)).astype(BF16)
        dv = cb * dya
        dv1 = jnp.where(row == ts - 1, carry_sc[0:1, :], pltpu.roll(dv, ts - 1, 0))
        dv2 = jnp.where(row == ts - 1, carry_sc[1:2, :], jnp.where(row == ts - 2, carry_sc[0:1, :], pltpu.roll(dv, ts - 2, 0)))
        du = dv * w2 + dv1 * w1 + dv2 * w0
        dp_ref[:, C_CC:C_CC + W] = (du * ch).astype(BF16)
        dp_ref[:, C_CH:C_CH + W] = (du * cc).astype(BF16)
        dcw_ref[0:1, :] += jnp.sum(dv * u2, axis=0, keepdims=True)
        dcw_ref[1:2, :] += jnp.sum(dv * u1, axis=0, keepdims=True)
        dcw_ref[2:3, :] += jnp.sum(dv * u, axis=0, keepdims=True)
        carry_sc[...] = dv[0:8, :]

        lbv = lb_ref[...]
        q_all, sq_all, qs_all, sig_all, f_all, logf, k_all = _hgrn_gates(p_ref, lbv)
        pos_all, bc, bref, blast = _hgrn_decays(logf, bc_sc, ts)
        ea_all, eb_all, eq_all, ek_all = jnp.exp(bc - bref), jnp.exp(bref - bc), jnp.exp(bc), jnp.exp(blast - bc)
        mask = _chunk_mask(ts)
        pos = _chunk_pos((ts, HG_F))
        pos_c = _chunk_pos((HG_CHUNK, HG_F))
        nw = nw_ref[...]
        for h in range(HG_HEADS):
            hs = slice(h * HG_F, (h + 1) * HG_F)
            qs, k, ea, eb, eq, ek = qs_all[:, hs], k_all[:, hs], ea_all[:, hs], eb_all[:, hs], eq_all[:, hs], ek_all[:, hs]
            a, bk, qin, kout = qs * ea, k * eb, qs * eq, k * ek
            o = opre_ref[:, hs]
            g = p_ref[:, C_HG + h * HG_F:C_HG + (h + 1) * HG_F].astype(F32)
            sg = _sigmoid(g)
            r = lax.rsqrt(jnp.mean(o * o, axis=-1, keepdims=True) + RMS_EPS)
            dyb = dy_ref[:, W + h * HG_F:W + (h + 1) * HG_F].astype(F32)
            dp_ref[:, C_HG + h * HG_F:C_HG + (h + 1) * HG_F] = (dyb * (o * r * nw) * (sg * (1.0 + g * (1.0 - sg)))).astype(BF16)
            don = dyb * (g * sg)
            dnw_ref[0:1, :] += jnp.sum(don * o * r, axis=0, keepdims=True)
            dn = don * nw
            do = r * (dn - o * (r * r) * jnp.mean(dn * o, axis=-1, keepdims=True))
            dob = do.astype(BF16)
            vb = p_ref[:, C_HI + h * HG_F:C_HI + (h + 1) * HG_F].astype(BF16)
            ab, bkb = a.astype(BF16), bk.astype(BF16)
            scores = jnp.where(mask, _dot(ab, bkb, NT), 0.0)
            dscores = jnp.where(mask, _dot(dob, vb, NT), 0.0).astype(BF16)
            dv_h = _dot(scores.astype(BF16), dob, TN)
            da = _dot(dscores, bkb, NN)
            dbk = _dot(dscores, ab, TN)
            koutb, qinb = kout.astype(BF16), qin.astype(BF16)
            dst = dstate_sc[h]
            dqin_p, dkout_p, dvi_p, ddec_p = [None] * nc, [None] * nc, [None] * nc, [None] * nc
            for n in reversed(range(nc)):
                rows = slice(n * HG_CHUNK, (n + 1) * HG_CHUNK)
                st_n = st_ref[n, h]
                decay = jnp.exp(bc_sc[n * HG_CHUNK + last:n * HG_CHUNK + last + 1, hs])
                dstb = dst.astype(BF16)
                dvi_p[n] = _dot(koutb[rows], dstb, NT)
                dkout_p[n] = _dot(vb[rows], dstb, NN)
                ddec_p[n] = jnp.sum(dst * st_n, axis=0, keepdims=True) * decay
                dqin_p[n] = _dot(dob[rows], st_n.astype(BF16), NN)
                dst = dst * decay + _dot(dob[rows], qinb[rows], TN)
            dstate_sc[h] = dst
            dqin = jnp.concatenate(dqin_p, axis=0)
            dkout = jnp.concatenate(dkout_p, axis=0)
            dp_ref[:, C_HI + h * HG_F:C_HI + (h + 1) * HG_F] = (dv_h + jnp.concatenate(dvi_p, axis=0)).astype(BF16)
            dqs = da * ea + dqin * eq
            dk = dbk * eb + dkout * ek
            t_a, t_b, t_q, t_k = da * a, dbk * bk, dqin * qin, dkout * kout
            dbc = t_a - t_b + t_q - t_k
            t_ref = t_b - t_a
            pieces = []
            for n in range(nc):
                rows = slice(n * HG_CHUNK, (n + 1) * HG_CHUNK)
                s_ref = jnp.sum(t_ref[rows], axis=0, keepdims=True)
                s_last = jnp.sum(t_k[rows], axis=0, keepdims=True) + ddec_p[n]
                pieces.append(dbc[rows] + jnp.where(pos_c == mid, s_ref, 0.0) + jnp.where(pos_c == last, s_last, 0.0))
            dlogf = _seg_rev_cumsum(jnp.concatenate(pieces, axis=0), pos)
            sig, lbh = sig_all[:, hs], lbv[:, hs]
            dfk = dlogf / f_all[:, hs] - dk
            dp_ref[:, C_HF + h * HG_F:C_HF + (h + 1) * HG_F] = (dfk * (1.0 - lbh) * sig * (1.0 - sig)).astype(BF16)
            dlb_ref[0:1, hs] += jnp.sum(dfk * (1.0 - sig), axis=0, keepdims=True)
            q, sq = q_all[:, hs], sq_all[:, hs]
            dp_ref[:, C_HQ + h * HG_F:C_HQ + (h + 1) * HG_F] = (dqs * (sq * (1.0 + q * (1.0 - sq)))).astype(BF16)

        for h in range(MEM_HEADS):
            hs = slice(h * MEM_HEAD_DIM, (h + 1) * MEM_HEAD_DIM)
            qh = p_ref[:, C_MQ + h * MEM_HEAD_DIM:C_MQ + (h + 1) * MEM_HEAD_DIM].astype(BF16)
            kh, vh = mk_ref[:, hs], mv_ref[:, hs]
            prob = _attn_probs(qh, kh)
            dob = dy_ref[:, 2 * W + h * MEM_HEAD_DIM:2 * W + (h + 1) * MEM_HEAD_DIM].astype(BF16)
            dmv_ref[:, hs] += _dot(prob.astype(BF16), dob, TN)
            dprob = _dot(dob, vh, NT)
            ds = prob * (dprob - jnp.sum(dprob * prob, axis=-1, keepdims=True)) * (MEM_HEAD_DIM ** -0.5)
            dsb = ds.astype(BF16)
            dp_ref[:, C_MQ + h * MEM_HEAD_DIM:C_MQ + (h + 1) * MEM_HEAD_DIM] = _dot(dsb, kh, NN).astype(BF16)
            dmk_ref[:, hs] += _dot(dsb, qh, TN)

    def tile(b, s):
        return b * ns + (ns - 1 - s)

    return pl.pallas_call(
        body,
        name="mixer_bwd",
        grid=(bl, ns),
        in_specs=[
            pl.BlockSpec((ts, N_MIX), lambda b, s: (tile(b, s), 0)),
            pl.BlockSpec((PREV_ROWS, N_MIX), lambda b, s: (jnp.maximum(tile(b, s) * (ts // PREV_ROWS) - 1, 0), 0)),
            pl.BlockSpec((ts, 3 * W), lambda b, s: (tile(b, s), 0)),
            pl.BlockSpec(memory_space=pl.ANY),
            pl.BlockSpec((nc, HG_HEADS, HG_F, HG_F), lambda b, s: (tile(b, s), 0, 0, 0)),
            pl.BlockSpec((ts, W), lambda b, s: (tile(b, s), 0)),
            pl.BlockSpec((ml, W), lambda b, s: (b, 0)),
            pl.BlockSpec((ml, W), lambda b, s: (b, 0)),
            pl.BlockSpec((1, W), lambda b, s: (0, 0)),
            pl.BlockSpec((CONV_K, W), lambda b, s: (0, 0)),
            pl.BlockSpec((1, HG_F), lambda b, s: (0, 0)),
        ] + [ANY_SPEC] * len(deps),
        out_specs=[
            pl.BlockSpec((ts, N_MIX), lambda b, s: (tile(b, s), 0)),
            pl.BlockSpec((ml, W), lambda b, s: (b, 0)),
            pl.BlockSpec((ml, W), lambda b, s: (b, 0)),
            pl.BlockSpec((8, W), lambda b, s: (0, 0)),
            pl.BlockSpec((8, HG_F), lambda b, s: (0, 0)),
            pl.BlockSpec((8, W), lambda b, s: (0, 0)),
        ],
        out_shape=[
            jax.ShapeDtypeStruct((T, nin), BF16),
            jax.ShapeDtypeStruct((bl * ml, W), F32),
            jax.ShapeDtypeStruct((bl * ml, W), F32),
            jax.ShapeDtypeStruct((8, W), F32),
            jax.ShapeDtypeStruct((8, HG_F), F32),
            jax.ShapeDtypeStruct((8, W), F32),
        ],
        input_output_aliases={3: 0},
        scratch_shapes=[pltpu.VMEM((HG_HEADS, HG_F, HG_F), F32), pltpu.VMEM((8, W), F32), pltpu.VMEM((PREV_ROWS, W), F32),
                        pltpu.VMEM((ts, W), F32)],
        compiler_params=_cparams(("arbitrary", "arbitrary")),
    )(p, p, dy, dp_gates, st, opre, mk, mv, lb, conv_w, norm_w, *deps)


def _layer_norm_stats(z):
    mu = jnp.mean(z, axis=-1, keepdims=True)
    zc = z - mu
    rstd = lax.rsqrt(jnp.mean(zc * zc, axis=-1, keepdims=True) + LN_EPS)
    return zc * rstd, rstd


def _gate_specs(tm, d):
    g0 = N_MIX // d
    return [pl.BlockSpec((tm, d), functools.partial(lambda i, k: (i, g0 + k), k=k)) for k in range(N_BRANCH)]


def _merge_fwd(y, p, x0, wb, wo, bg, ln_g, ln_b, *, alpha, tm=256):
    T, d = x0.shape
    assert N_MIX % d == 0
    tm = _pick(T, (tm, 128, 8))

    def body(y_ref, g0_ref, g1_ref, g2_ref, x_ref, wb_ref, wo_ref, bg_ref, lg_ref, lb_ref, r_ref, mg_ref, xh_ref, rs_ref, x1b_ref):
        merged = None
        for i, g_ref in enumerate((g0_ref, g1_ref, g2_ref)):
            r = _dot(y_ref[:, i * W:(i + 1) * W], wb_ref[i * W:(i + 1) * W, :], NN)
            r_ref[:, i * d:(i + 1) * d] = r.astype(BF16)
            t = _sigmoid(g_ref[...].astype(F32) + bg_ref[:, i * d:(i + 1) * d]) * r
            merged = t if merged is None else merged + t
        mb = merged.astype(BF16)
        mg_ref[...] = mb
        z = alpha * x_ref[...] + _dot(mb, wo_ref[...], NN)
        xh, rs = _layer_norm_stats(z)
        xh_ref[...], rs_ref[...] = xh, rs
        x1b_ref[...] = (xh * lg_ref[...] + lb_ref[...]).astype(BF16)

    row = lambda i: (i, 0)
    fix = lambda i: (0, 0)
    return pl.pallas_call(
        body,
        name="merge_fwd",
        grid=(T // tm,),
        in_specs=[pl.BlockSpec((tm, 3 * W), row)] + _gate_specs(tm, d) + [
            pl.BlockSpec((tm, d), row), pl.BlockSpec((3 * W, d), fix), pl.BlockSpec((d, d), fix), pl.BlockSpec((1, 3 * d), fix),
            pl.BlockSpec((1, d), fix), pl.BlockSpec((1, d), fix)],
        out_specs=[pl.BlockSpec((tm, 3 * d), row), pl.BlockSpec((tm, d), row), pl.BlockSpec((tm, d), row), pl.BlockSpec((tm, 1), row),
                   pl.BlockSpec((tm, d), row)],
        out_shape=[jax.ShapeDtypeStruct((T, 3 * d), BF16), jax.ShapeDtypeStruct((T, d), BF16),
                   jax.ShapeDtypeStruct((T, d), F32), jax.ShapeDtypeStruct((T, 1), F32), jax.ShapeDtypeStruct((T, d), BF16)],
        compiler_params=_cparams(("parallel",)),
    )(y, p, p, p, x0, wb, wo, bg, ln_g, ln_b)


def _merge_bwd(dz, p, r, wb, wo, bg, *, tm=256):
    T, d = dz.shape
    nin = p.shape[1]
    tm = _pick(T, (tm, 128, 8))

    def body(dz_ref, g0_ref, g1_ref, g2_ref, r_ref, wb_ref, wo_ref, bg_ref, dr_ref, dp_ref, dy_ref, dbg_ref):
        @pl.when(pl.program_id(0) == 0)
        def _():
            dbg_ref[...] = jnp.zeros_like(dbg_ref)

        dmerged = _dot(dz_ref[...].astype(BF16), wo_ref[...], NT)
        dp_ref[:, 0:N_MIX] = jnp.zeros((tm, N_MIX), BF16)
        for i, g_ref in enumerate((g0_ref, g1_ref, g2_ref)):
            cs = slice(i * d, (i + 1) * d)
            s = _sigmoid(g_ref[...].astype(F32) + bg_ref[:, cs])
            drb = (dmerged * s).astype(BF16)
            dr_ref[:, cs] = drb
            dgate = dmerged * r_ref[:, cs].astype(F32) * s * (1.0 - s)
            dp_ref[:, N_MIX + i * d:N_MIX + (i + 1) * d] = dgate.astype(BF16)
            dbg_ref[0:1, cs] += jnp.sum(dgate, axis=0, keepdims=True)
            dy_ref[:, i * W:(i + 1) * W] = _dot(drb, wb_ref[i * W:(i + 1) * W, :], NT).astype(BF16)

    row = lambda i: (i, 0)
    fix = lambda i: (0, 0)
    return pl.pallas_call(
        body,
        name="merge_bwd",
        grid=(T // tm,),
        in_specs=[pl.BlockSpec((tm, d), row)] + _gate_specs(tm, d) + [
            pl.BlockSpec((tm, 3 * d), row), pl.BlockSpec((3 * W, d), fix), pl.BlockSpec((d, d), fix), pl.BlockSpec((1, 3 * d), fix)],
        out_specs=[pl.BlockSpec((tm, 3 * d), row), pl.BlockSpec((tm, nin), row), pl.BlockSpec((tm, 3 * W), row),
                   pl.BlockSpec((8, 3 * d), fix)],
        out_shape=[jax.ShapeDtypeStruct((T, 3 * d), BF16), jax.ShapeDtypeStruct((T, nin), BF16),
                   jax.ShapeDtypeStruct((T, 3 * W), BF16), jax.ShapeDtypeStruct((8, 3 * d), F32)],
        compiler_params=_cparams(("arbitrary",)),
    )(dz, p, p, p, r, wb, wo, bg)


def _mlp_fwd(xhat1, x1b, g1, b1, wu, wd, g2, b2, *, alpha, tm=512, tf=2048):
    T, d = xhat1.shape
    ff = wu.shape[1]
    tm, tf = _pick(T, (tm, 256, 128, 8)), _pick(ff, (tf, 1024, 512, 256, 128))
    nf = ff // tf

    def body(xh_ref, x1b_ref, g1_ref, b1_ref, wu_ref, wd_ref, g2_ref, b2_ref, a_ref, xh2_ref, rs2_ref, x2_ref, x2b_ref, acc_ref):
        f = pl.program_id(1)
        a = _dot(x1b_ref[...], wu_ref[...], NN)
        a_ref[...] = a.astype(BF16)
        h = jnp.square(jnp.maximum(a, 0.0))
        part = _dot(h.astype(BF16), wd_ref[...], NN)

        @pl.when(f == 0)
        def _():
            acc_ref[...] = part

        @pl.when(jnp.logical_and(f > 0, f < nf - 1))
        def _():
            acc_ref[...] += part

        @pl.when(f == nf - 1)
        def _():
            x1 = xh_ref[...] * g1_ref[...] + b1_ref[...]
            xh2, rs2 = _layer_norm_stats(alpha * x1 + (acc_ref[...] + part if nf > 1 else part))
            xh2_ref[...] = xh2
            rs2_ref[...] = rs2
            x2 = xh2 * g2_ref[...] + b2_ref[...]
            x2_ref[...] = x2
            x2b_ref[...] = x2.astype(BF16)

    row = lambda i, f: (i, 0)
    fix = lambda i, f: (0, 0)
    return pl.pallas_call(
        body,
        name="mlp_fwd",
        grid=(T // tm, nf),
        in_specs=[pl.BlockSpec((tm, d), row), pl.BlockSpec((tm, d), row), pl.BlockSpec((1, d), fix), pl.BlockSpec((1, d), fix),
                  pl.BlockSpec((d, tf), lambda i, f: (0, f)), pl.BlockSpec((tf, d), lambda i, f: (f, 0)),
                  pl.BlockSpec((1, d), fix), pl.BlockSpec((1, d), fix)],
        out_specs=[pl.BlockSpec((tm, tf), lambda i, f: (i, f)), pl.BlockSpec((tm, d), row), pl.BlockSpec((tm, 1), row),
                   pl.BlockSpec((tm, d), row), pl.BlockSpec((tm, d), row)],
        out_shape=[jax.ShapeDtypeStruct((T, ff), BF16), jax.ShapeDtypeStruct((T, d), F32), jax.ShapeDtypeStruct((T, 1), F32),
                   jax.ShapeDtypeStruct((T, d), F32), jax.ShapeDtypeStruct((T, d), BF16)],
        scratch_shapes=[pltpu.VMEM((tm, d), F32)],
        compiler_params=_cparams(("parallel", "arbitrary")),
    )(xhat1, x1b, g1, b1, wu, wd, g2, b2)


def _ln_bwd(dy, xhat, rstd, g, *, tm=512, deps=()):
    T, d = dy.shape
    tm = _pick(T, (tm, 256, 128, 8))

    def body(dy_ref, xh_ref, rs_ref, g_ref, *rest):
        dz_ref, dzb_ref, dg_ref, db_ref = rest[len(deps):]

        @pl.when(pl.program_id(0) == 0)
        def _():
            dg_ref[...] = jnp.zeros_like(dg_ref)
            db_ref[...] = jnp.zeros_like(db_ref)

        dy_, xh = dy_ref[...], xh_ref[...]
        dg_ref[0:1, :] += jnp.sum(dy_ * xh, axis=0, keepdims=True)
        db_ref[0:1, :] += jnp.sum(dy_, axis=0, keepdims=True)
        dxh = dy_ * g_ref[...]
        dz = rs_ref[...] * (dxh - jnp.mean(dxh, axis=-1, keepdims=True) - xh * jnp.mean(dxh * xh, axis=-1, keepdims=True))
        dz_ref[...] = dz
        dzb_ref[...] = dz.astype(BF16)

    row = lambda i: (i, 0)
    fix = lambda i: (0, 0)
    return pl.pallas_call(
        body,
        name="ln_bwd",
        grid=(T // tm,),
        in_specs=[pl.BlockSpec((tm, d), row), pl.BlockSpec((tm, d), row), pl.BlockSpec((tm, 1), row), pl.BlockSpec((1, d), fix)]
        + [ANY_SPEC] * len(deps),
        out_specs=[pl.BlockSpec((tm, d), row), pl.BlockSpec((tm, d), row), pl.BlockSpec((8, d), fix), pl.BlockSpec((8, d), fix)],
        out_shape=[jax.ShapeDtypeStruct((T, d), F32), jax.ShapeDtypeStruct((T, d), BF16), jax.ShapeDtypeStruct((8, d), F32),
                   jax.ShapeDtypeStruct((8, d), F32)],
        compiler_params=_cparams(("arbitrary",)),
    )(dy, xhat, rstd, g, *deps)


def _loss_head(y, target, *, tm=512):
    T, d = y.shape
    tm = _pick(T, (tm, 256, 128, 8))
    n = T // tm

    def body(y_ref, t_ref, loss_ref, dy_ref, acc_ref):
        i = pl.program_id(0)

        @pl.when(i == 0)
        def _():
            acc_ref[...] = jnp.zeros_like(acc_ref)

        e = y_ref[...] - t_ref[...]
        dy_ref[...] = e * (1.0 / d)
        acc_ref[...] += jnp.sum(e * e, axis=0, keepdims=True)

        @pl.when(i == n - 1)
        def _():
            loss_ref[...] = (0.5 / d) * jnp.sum(acc_ref[...], axis=1, keepdims=True)

    row = lambda i: (i, 0)
    return pl.pallas_call(
        body,
        name="loss_head",
        grid=(n,),
        in_specs=[pl.BlockSpec((tm, d), row), pl.BlockSpec((tm, d), row)],
        out_specs=[pl.BlockSpec((1, 1), lambda i: (0, 0)), pl.BlockSpec((tm, d), row)],
        out_shape=[jax.ShapeDtypeStruct((1, 1), F32), jax.ShapeDtypeStruct((T, d), F32)],
        scratch_shapes=[pltpu.VMEM((1, d), F32)],
        compiler_params=_cparams(("arbitrary",)),
    )(y, target)


def _lower_bounds_fwd(lower_bounds):
    depth, n = lower_bounds.shape

    def body(x_ref, soft_ref, lb_ref):
        x = x_ref[...]
        e = jnp.exp(x - jnp.max(x, axis=0, keepdims=True))
        soft_ref[...] = e / jnp.sum(e, axis=0, keepdims=True)
        run = None
        for l in range(depth):
            run = soft_ref[l:l + 1, :] if run is None else run + soft_ref[l:l + 1, :]
            lb_ref[l:l + 1, :] = run - soft_ref[0:1, :]

    return pl.pallas_call(body, name="lower_bounds_fwd",
                          out_shape=[jax.ShapeDtypeStruct((depth, n), F32), jax.ShapeDtypeStruct((depth, n), F32)])(lower_bounds)


def _lower_bounds_bwd(soft, dlb):
    depth, n = soft.shape

    def body(soft_ref, dlb_ref, out_ref, dsoft_ref):
        total = jnp.sum(dlb_ref[...], axis=0, keepdims=True)
        run = None
        for l in reversed(range(depth)):
            run = dlb_ref[l:l + 1, :] if run is None else run + dlb_ref[l:l + 1, :]
            dsoft_ref[l:l + 1, :] = run - total if l == 0 else run
        s, ds = soft_ref[...], dsoft_ref[...]
        out_ref[...] = s * (ds - jnp.sum(s * ds, axis=0, keepdims=True))

    return pl.pallas_call(body, name="lower_bounds_bwd", out_shape=jax.ShapeDtypeStruct((depth, n), F32),
                          scratch_shapes=[pltpu.VMEM((depth, n), F32)])(soft, dlb)


def _layer_fwd(x0, x0b, mem2, lb, w_in, rest_fn, *, bl, seq, alpha, deps=()):
    p = _matmul("proj_in", x0b, w_in, mode="nn", out_dtype=BF16, deps=deps, tm=1024, tn=1792)
    wts = dict(rest_fn(p), w_in=w_in)
    mk = _matmul("mem_k", mem2, wts["w_mem_k"], mode="nn", out_dtype=BF16)
    mv = _matmul("mem_v", mem2, wts["w_mem_v"], mode="nn", out_dtype=BF16)
    y, st, opre = _mixer_fwd(p, mk, mv, lb, wts["conv_w"], wts["hg_norm_w"], bl=bl, seq=seq)
    r, merged, xhat1, rstd1, x1b = _merge_fwd(y, p, x0, wts["w_branch"], wts["w_o"], wts["b_gate"], wts["ln1_g"], wts["ln1_b"],
                                              alpha=alpha)
    a, xhat2, rstd2, x2, x2b = _mlp_fwd(xhat1, x1b, wts["ln1_g"], wts["ln1_b"], wts["w_up"], wts["w_down"], wts["ln2_g"],
                                        wts["ln2_b"], alpha=alpha)
    saved = dict(x0b=x0b, p=p, mk=mk, mv=mv, y=y, st=st, opre=opre, r=r, merged=merged, xhat1=xhat1, rstd1=rstd1, x1b=x1b, a=a,
                 xhat2=xhat2, rstd2=rstd2)
    return x2, x2b, saved, wts


def _relu2_bf16(a):
    return jnp.square(jnp.maximum(a.astype(F32), 0.0)).astype(BF16)


def _mlp_bwd(dz2, dz2b, sv, wts, *, alpha, deps=()):
    g = {}
    da = _matmul("mlp_da", dz2b, wts["w_down"], mode="nt", out_dtype=BF16, tm=1024, deps=deps,
                 epi_fn=lambda acc, a: (acc * (2.0 * jnp.maximum(a.astype(F32), 0.0)),), epi_extra=(sv["a"],))
    g["w_down"] = _matmul_tn("grad_w_down", sv["a"], dz2b, a_fn=_relu2_bf16, out_dtype=BF16, tt=2048)
    g["w_up"] = _matmul_tn("grad_w_up", sv["x1b"], da, out_dtype=BF16, tt=2048)
    dx1 = _matmul("mlp_dx", da, wts["w_up"], mode="nt", epi_fn=lambda acc, dz: (acc + alpha * dz,), epi_extra=(dz2,),
                  tm=512, tk=4096)
    dz1, dz1b, dg1, db1 = _ln_bwd(dx1, sv["xhat1"], sv["rstd1"], wts["ln1_g"])
    g["ln1_g"], g["ln1_b"] = dg1[0:1], db1[0:1]
    return dz1, dz1b, g


def _mix_bwd(dz1, dz1b, sv, mem2, lb, wts, *, bl, seq, alpha, send, below=None, deps=()):
    d = dz1.shape[1]
    g = {}
    g["w_o"] = _matmul_tn("grad_w_o", sv["merged"], dz1b, out_dtype=BF16, tt=2048, deps=deps)
    dr, dp, dy, dbg = _merge_bwd(dz1b, sv["p"], sv["r"], wts["w_branch"], wts["w_o"], wts["b_gate"])
    g["b_gate"] = dbg[0:1]
    g["w_branch"] = jnp.concatenate(
        [_matmul_tn("grad_w_branch", sv["y"], dr, a_cols=(i * W, W), b_cols=(i * d, d), out_dtype=BF16) for i in range(N_BRANCH)],
        axis=0)
    token = send(("w_o", "w_branch"), g)
    dp, dmk, dmv, dcw, dnw, dlb = _mixer_bwd(sv["p"], dy, dp, sv["st"], sv["opre"], sv["mk"], sv["mv"], lb,
                                              wts["conv_w"], wts["hg_norm_w"], bl=bl, seq=seq, deps=(token,))
    g["conv_w"], g["hg_norm_w"], g["lb"] = dcw[0:CONV_K], dnw[0:1], dlb[0:1]
    g["w_mem_k"] = _matmul_tn("grad_w_mem_k", mem2, dmk, out_dtype=BF16)
    g["w_mem_v"] = _matmul_tn("grad_w_mem_v", mem2, dmv, out_dtype=BF16)
    g["w_in"] = _matmul_tn("grad_w_in", sv["x0b"], dp, out_dtype=BF16, tt=2048)
    token = send(("w_in", "w_mem_k", "w_mem_v", "conv_w"), g)
    dx0 = _matmul("proj_in_dx", dp, wts["w_in"], mode="nt", epi_fn=lambda acc, dz: (acc + alpha * dz,), epi_extra=(dz1,),
                  tm=512, tk=dp.shape[1], deps=(token,))
    return (dx0 if below is None else _ln_bwd(dx0, *below)), g


N_CHIPS = 4
MESH_IDS = pl.DeviceIdType.MESH


def _axis_slice(ref, axis, start, size):
    idx = [slice(None)] * len(ref.shape)
    idx[axis] = pl.ds(start, size)
    return ref.at[tuple(idx)]


def _chip_exchange(name, items):
    n = len(items)
    out_shapes, meta = [], []
    for arr, kind, axis in items:
        shp = list(arr.shape)
        if kind == "gather":
            per = shp[axis]
            shp[axis] = per * N_CHIPS
            out_shapes.append(jax.ShapeDtypeStruct(tuple(shp), arr.dtype))
        elif kind == "scatter":
            per = shp[axis] // N_CHIPS
            shp[axis] = per
            out_shapes.append(jax.ShapeDtypeStruct((N_CHIPS, *shp), arr.dtype))
        else:
            per = None
            out_shapes.append(jax.ShapeDtypeStruct((N_CHIPS, *shp), arr.dtype))
        meta.append((kind, axis, per))

    def body(*refs):
        ins, outs = refs[:n], refs[n:2 * n]
        send_sems, recv_sems, local_sems = refs[2 * n:]
        x, y, c = lax.axis_index("x"), lax.axis_index("y"), lax.axis_index("c")
        me = 2 * x + y
        peers = [(1 - x, y), (x, 1 - y), (1 - x, 1 - y)]

        def src_for(t, chip):
            kind, axis, per = meta[t]
            return _axis_slice(ins[t], axis, chip * per, per) if kind == "scatter" else ins[t]

        def dst_from(t, chip):
            kind, axis, per = meta[t]
            return _axis_slice(outs[t], axis, chip * per, per) if kind == "gather" else outs[t].at[chip]

        def remote(t, k):
            px, py = peers[k]
            return pltpu.make_async_remote_copy(
                src_ref=src_for(t, 2 * px + py), dst_ref=dst_from(t, me), send_sem=send_sems.at[t * 3 + k],
                recv_sem=recv_sems.at[t * 3 + k], device_id=(px, py, c), device_id_type=MESH_IDS)

        def arrival(t, k):
            px, py = peers[k]
            return pltpu.make_async_remote_copy(
                src_ref=src_for(t, me), dst_ref=dst_from(t, 2 * px + py), send_sem=send_sems.at[t * 3 + k],
                recv_sem=recv_sems.at[t * 3 + k], device_id=(px, py, c), device_id_type=MESH_IDS)

        sends = [remote(t, k) for t in range(n) for k in range(3)]
        for cp in sends:
            cp.start()
        own = [pltpu.make_async_copy(src_for(t, me), dst_from(t, me), local_sems.at[t]) for t in range(n)]
        for cp in own:
            cp.start()
        for t in range(n):
            for k in range(3):
                arrival(t, k).wait_recv()
        for cp in sends:
            cp.wait_send()
        for cp in own:
            cp.wait()

    any_spec = pl.BlockSpec(memory_space=pl.ANY)
    return pl.pallas_call(
        body,
        name=name,
        in_specs=[any_spec] * n,
        out_specs=[any_spec] * n,
        out_shape=out_shapes,
        scratch_shapes=[pltpu.SemaphoreType.DMA((3 * n,)), pltpu.SemaphoreType.DMA((3 * n,)), pltpu.SemaphoreType.DMA((n,))],
        compiler_params=pltpu.CompilerParams(has_side_effects=True),
    )(*[a for a, _, _ in items])


HBM_SPEC = pl.BlockSpec(memory_space=pltpu.HBM)
SEM_SPEC = pl.BlockSpec(memory_space=pltpu.SEMAPHORE)
N_PEERS = N_CHIPS - 1


def _my_chip():
    return (2 * lax.axis_index("x") + lax.axis_index("y")).astype(jnp.int32).reshape(1)


def _own_block_spec(r, c, axis, tr):
    if axis == 1:
        return pl.BlockSpec((tr, c), lambda i, me: (i, me[0]))
    return pl.BlockSpec((tr, c), lambda i, me: (me[0] * (r // tr) + i, 0))


def _place_shard(name, shard, axis, me):
    r, c = shard.shape
    tr = _row_block(r, c, shard.dtype.itemsize)
    shp = (r, c * N_CHIPS) if axis == 1 else (r * N_CHIPS, c)

    def body(me_ref, s_ref, o_ref):
        del me_ref
        o_ref[...] = s_ref[...]

    return pl.pallas_call(
        body, name=name,
        grid_spec=pltpu.PrefetchScalarGridSpec(
            num_scalar_prefetch=1, grid=(r // tr,),
            in_specs=[pl.BlockSpec((tr, c), lambda i, me: (i, 0))], out_specs=_own_block_spec(r, c, axis, tr)),
        out_shape=jax.ShapeDtypeStruct(shp, shard.dtype),
        compiler_params=_cparams(("parallel",)),
    )(me, shard)


class _Split:
    def __init__(self, name, items):
        self.name, self.n = name, len(items)
        self.srcs = [a for a, _, _ in items]
        self.meta, self.land_shapes = [], []
        for arr, kind, axis in items:
            shp = list(arr.shape)
            if kind == "gather":
                per = shp[axis]
                shp[axis] = per * N_CHIPS
                self.land_shapes.append(jax.ShapeDtypeStruct(tuple(shp), arr.dtype))
            else:
                per = shp[axis] // N_CHIPS
                shp[axis] = per
                self.land_shapes.append(jax.ShapeDtypeStruct((N_PEERS, *shp), arr.dtype))
            self.meta.append((kind, axis, per))

    def _src(self, ins, t, chip):
        kind, axis, per = self.meta[t]
        return _axis_slice(ins[t], axis, chip * per, per) if kind == "scatter" else ins[t]

    def _dst(self, lands, t, chip, slot):
        kind, axis, per = self.meta[t]
        return _axis_slice(lands[t], axis, chip * per, per) if kind == "gather" else lands[t].at[slot]

    def landing_zones(self, me):
        return [_place_shard(self.name + "_own", src, axis, me) if kind == "gather" else lax.empty(ls.shape, ls.dtype)
                for src, ls, (kind, axis, _) in zip(self.srcs, self.land_shapes, self.meta)]

    def _copies(self, ins, lands, send_sems, recv_sems, arrivals):
        x, y, c = lax.axis_index("x"), lax.axis_index("y"), lax.axis_index("c")
        me = 2 * x + y
        peers = [(1 - x, y), (x, 1 - y), (1 - x, 1 - y)]
        res = []
        for t in range(self.n):
            for k, (px, py) in enumerate(peers):
                theirs = 2 * px + py
                sems = dict(send_sem=send_sems.at[t * N_PEERS + k], recv_sem=recv_sems.at[t * N_PEERS + k],
                            device_id=(px, py, c), device_id_type=MESH_IDS)
                if arrivals:
                    res.append(pltpu.make_async_remote_copy(src_ref=self._src(ins, t, me), dst_ref=self._dst(lands, t, theirs, k), **sems))
                else:
                    res.append(pltpu.make_async_remote_copy(src_ref=self._src(ins, t, theirs), dst_ref=self._dst(lands, t, me, k), **sems))
        return res

    def start(self, lands, deps=()):
        n, nd = self.n, len(deps)

        def body(*refs):
            ins, lnd = refs[:n], refs[n:2 * n]
            send_sems, recv_sems = refs[2 * n + nd], refs[2 * n + nd + 1]
            token = refs[-1]
            for cp in self._copies(ins, lnd, send_sems, recv_sems, arrivals=False):
                cp.start()
            token[...] = jnp.zeros_like(token)

        hbm = lambda a: pltpu.HBM(a.shape, a.dtype)
        res = pl.pallas_call(
            body, name=self.name + "_start",
            in_specs=[HBM_SPEC] * (2 * n) + [ANY_SPEC] * nd,
            out_specs=[SEM_SPEC, SEM_SPEC] + [HBM_SPEC] * (2 * n) + [pl.BlockSpec(memory_space=pltpu.VMEM)],
            out_shape=[pltpu.SemaphoreType.DMA((N_PEERS * n,)), pltpu.SemaphoreType.DMA((N_PEERS * n,))]
            + [hbm(a) for a in self.srcs] + [hbm(a) for a in self.land_shapes] + [jax.ShapeDtypeStruct((8, 128), F32)],
            input_output_aliases={i: 2 + i for i in range(2 * n)},
            compiler_params=pltpu.CompilerParams(has_side_effects=pltpu.SideEffectType.DATAFLOW_SIDE_EFFECTING),
        )(*[pltpu.with_memory_space_constraint(a, pltpu.HBM) for a in self.srcs],
          *[pltpu.with_memory_space_constraint(a, pltpu.HBM) for a in lands], *deps)
        return res[:-1], res[-1]

    def wait(self, state, after):
        n = self.n
        send_sems, recv_sems = state[0], state[1]
        srcs, lands = state[2:2 + n], state[2 + n:2 + 2 * n]

        def body(*refs):
            ins, lnd = refs[:n], refs[n:2 * n]
            s_sems, r_sems = refs[2 * n], refs[2 * n + 1]
            for cp in self._copies(ins, lnd, s_sems, r_sems, arrivals=True):
                cp.wait_recv()
            for cp in self._copies(ins, lnd, s_sems, r_sems, arrivals=False):
                cp.wait_send()

        hbm = lambda a: pltpu.HBM(a.shape, a.dtype)
        res = pl.pallas_call(
            body, name=self.name + "_wait",
            in_specs=[HBM_SPEC] * (2 * n) + [SEM_SPEC, SEM_SPEC, ANY_SPEC],
            out_specs=[HBM_SPEC] * (2 * n),
            out_shape=[hbm(a) for a in self.srcs] + [hbm(a) for a in self.land_shapes],
            input_output_aliases={i: i for i in range(2 * n)},
            compiler_params=pltpu.CompilerParams(has_side_effects=pltpu.SideEffectType.DATAFLOW_SIDE_EFFECTING),
        )(*srcs, *lands, send_sems, recv_sems, after)
        return res[:n], res[n:]


def _sibling_swap(name, arrays):
    n = len(arrays)

    def body(*refs):
        ins, outs = refs[:n], refs[n:2 * n]
        send_sems, recv_sems = refs[2 * n:]
        sibling = (lax.axis_index("x"), lax.axis_index("y"), 1 - lax.axis_index("c"))
        copies = [pltpu.make_async_remote_copy(src_ref=ins[t], dst_ref=outs[t], send_sem=send_sems.at[t], recv_sem=recv_sems.at[t],
                                               device_id=sibling, device_id_type=MESH_IDS) for t in range(n)]
        for cp in copies:
            cp.start()
        for cp in copies:
            cp.wait()

    any_spec = pl.BlockSpec(memory_space=pl.ANY)
    return pl.pallas_call(
        body,
        name=name,
        in_specs=[any_spec] * n,
        out_specs=[any_spec] * n,
        out_shape=[jax.ShapeDtypeStruct(a.shape, a.dtype) for a in arrays],
        scratch_shapes=[pltpu.SemaphoreType.DMA((n,)), pltpu.SemaphoreType.DMA((n,))],
        compiler_params=pltpu.CompilerParams(has_side_effects=True),
    )(*arrays)


def _row_block(r, c, itemsize=4, target=1 << 20):
    if r % 8 != 0:
        return r
    best = 8
    for tr in range(8, r + 1, 8):
        if r % tr == 0 and tr * c * itemsize <= target:
            best = tr
    return best


def _sum_chips_into(parts, stacked, layer):
    _, r, c = parts.shape
    tr = _row_block(r, c)

    def body(p_ref, s_ref, o_ref):
        del s_ref
        o_ref[...] = ((p_ref[0] + p_ref[1]) + p_ref[2]) + p_ref[3]

    return pl.pallas_call(
        body,
        name="sum_chips",
        grid=(r // tr,),
        in_specs=[pl.BlockSpec((N_CHIPS, tr, c), lambda i: (0, i, 0)), pl.BlockSpec(memory_space=pl.ANY)],
        out_specs=pl.BlockSpec((None, tr, c), lambda i: (layer, i, 0)),
        out_shape=jax.ShapeDtypeStruct(stacked.shape, stacked.dtype),
        input_output_aliases={1: 0},
        compiler_params=_cparams(("parallel",)),
    )(parts, stacked)


def _sum_own_and_peers(me, g, axis, landed):
    _, r, c = landed.shape
    tr = _row_block(r, c)

    def body(me_ref, g_ref, p_ref, o_ref):
        del me_ref
        o_ref[...] = ((g_ref[...].astype(F32) + p_ref[0].astype(F32)) + p_ref[1].astype(F32)) + p_ref[2].astype(F32)

    return pl.pallas_call(
        body, name="sum_chips_own",
        grid_spec=pltpu.PrefetchScalarGridSpec(
            num_scalar_prefetch=1, grid=(r // tr,),
            in_specs=[_own_block_spec(r, c, axis, tr), pl.BlockSpec((N_PEERS, tr, c), lambda i, me: (0, i, 0))],
            out_specs=pl.BlockSpec((tr, c), lambda i, me: (i, 0))),
        out_shape=jax.ShapeDtypeStruct((r, c), F32),
        compiler_params=_cparams(("parallel",)),
    )(me, g, landed)


def _adamw_math(w, m, v, g):
    m_new = ADAM_B1 * m + (1.0 - ADAM_B1) * g
    v_new = ADAM_B2 * v + (1.0 - ADAM_B2) * jnp.square(g)
    m_hat = m_new / (1.0 - ADAM_B1 ** ADAM_STEP)
    v_hat = v_new / (1.0 - ADAM_B2 ** ADAM_STEP)
    return -ADAM_LR * (m_hat / (jnp.sqrt(v_hat) + ADAM_EPS) + ADAM_WD * w), m_new, v_new


def _adamw(w, m, v, g_a, g_b):
    L, r, c = w.shape
    tr = _row_block(r, c, target=1 << 19)

    def body(w_ref, m_ref, v_ref, ga_ref, gb_ref, g_ref, d_ref, nm_ref, nv_ref):
        g = ga_ref[...] + gb_ref[...]
        g_ref[...] = g
        d_ref[...], nm_ref[...], nv_ref[...] = _adamw_math(w_ref[...], m_ref[...], v_ref[...], g)

    spec = pl.BlockSpec((None, tr, c), lambda l, i: (l, i, 0))
    return pl.pallas_call(
        body,
        name="adamw",
        grid=(L, r // tr),
        in_specs=[spec] * 5,
        out_specs=[spec] * 4,
        out_shape=[jax.ShapeDtypeStruct(w.shape, F32)] * 4,
        compiler_params=_cparams(("parallel", "parallel")),
    )(w, m, v, g_a, g_b)


def _adamw_layer(w, m, v, g_a, g_b, layer, outs):
    L, r, c = w.shape
    tr = _row_block(r, c, target=1 << 19)
    n_prev = 0 if outs is None else 4

    def body(w_ref, m_ref, v_ref, ga_ref, gb_ref, *rest):
        g_ref, d_ref, nm_ref, nv_ref = rest[n_prev:]
        g = ga_ref[...] + gb_ref[...]
        g_ref[...] = g
        d_ref[...], nm_ref[...], nv_ref[...] = _adamw_math(w_ref[...], m_ref[...], v_ref[...], g)

    at_layer = pl.BlockSpec((None, tr, c), lambda i: (layer, i, 0))
    flat = pl.BlockSpec((tr, c), lambda i: (i, 0))
    return pl.pallas_call(
        body,
        name="adamw_layer",
        grid=(r // tr,),
        in_specs=[at_layer] * 3 + [flat] * 2 + [ANY_SPEC] * n_prev,
        out_specs=[at_layer] * 4,
        out_shape=[jax.ShapeDtypeStruct(w.shape, F32)] * 4,
        input_output_aliases={5 + k: k for k in range(n_prev)},
        compiler_params=_cparams(("parallel",)),
    )(w, m, v, g_a, g_b, *(outs or ()))


SHARDED = (("w_in", 1), ("conv_w", 1), ("w_mem_k", 0), ("w_mem_v", 0), ("w_branch", 1), ("w_o", 0), ("w_up", 1), ("w_down", 0))
SMALL = ("lower_bounds", "hg_norm_w", "b_gate", "ln1_g", "ln1_b", "ln2_g", "ln2_b")
WEIGHT_ORDER = ("lower_bounds", "w_in", "conv_w", "hg_norm_w", "w_mem_k", "w_mem_v", "w_branch", "b_gate", "w_o", "ln1_g", "ln1_b",
                "w_up", "w_down", "ln2_g", "ln2_b")


def kernel(x, mem, lower_bounds, w_in, conv_w, hg_norm_w, w_mem_k, w_mem_v, w_branch, b_gate, w_o, ln1_g, ln1_b, w_up, w_down, ln2_g, ln2_b, loss_target, m_lower_bounds, m_w_in, m_conv_w, m_hg_norm_w, m_w_mem_k, m_w_mem_v, m_w_branch, m_b_gate, m_w_o, m_ln1_g, m_ln1_b, m_w_up, m_w_down, m_ln2_g, m_ln2_b, v_lower_bounds, v_w_in, v_conv_w, v_hg_norm_w, v_w_mem_k, v_w_mem_v, v_w_branch, v_b_gate, v_w_o, v_ln1_g, v_ln1_b, v_w_up, v_w_down, v_ln2_g, v_ln2_b):
    bl, seq, d = x.shape
    depth = w_in.shape[0]
    weights = dict(lower_bounds=lower_bounds, w_in=w_in, conv_w=conv_w, hg_norm_w=hg_norm_w, w_mem_k=w_mem_k, w_mem_v=w_mem_v,
                   w_branch=w_branch, b_gate=b_gate, w_o=w_o, ln1_g=ln1_g, ln1_b=ln1_b, w_up=w_up, w_down=w_down, ln2_g=ln2_g, ln2_b=ln2_b)
    mom_m = dict(lower_bounds=m_lower_bounds, w_in=m_w_in, conv_w=m_conv_w, hg_norm_w=m_hg_norm_w, w_mem_k=m_w_mem_k, w_mem_v=m_w_mem_v,
                 w_branch=m_w_branch, b_gate=m_b_gate, w_o=m_w_o, ln1_g=m_ln1_g, ln1_b=m_ln1_b, w_up=m_w_up, w_down=m_w_down,
                 ln2_g=m_ln2_g, ln2_b=m_ln2_b)
    mom_v = dict(lower_bounds=v_lower_bounds, w_in=v_w_in, conv_w=v_conv_w, hg_norm_w=v_hg_norm_w, w_mem_k=v_w_mem_k, w_mem_v=v_w_mem_v,
                 w_branch=v_w_branch, b_gate=v_b_gate, w_o=v_w_o, ln1_g=v_ln1_g, ln1_b=v_ln1_b, w_up=v_w_up, w_down=v_w_down,
                 ln2_g=v_ln2_g, ln2_b=v_ln2_b)

    def shard2d(name, l):
        w = weights[name][l]
        if name == "w_branch":
            return w.reshape(N_BRANCH * W, w.shape[-1]).astype(BF16)
        return w if name == "conv_w" else w.astype(BF16)

    me = _my_chip()

    shard_axis = dict(SHARDED)

    def start_exchange(name, kind, items, deps=()):
        ex = _Split(name, [(arr, kind, shard_axis[nm]) for nm, arr in items])
        state, token = ex.start(ex.landing_zones(me), deps)
        return ex, state, [nm for nm, _ in items], token

    def start_gathers(l, deps=()):
        first = start_exchange(f"gather_in_l{l}", "gather", [("w_in", shard2d("w_in", l))], deps)
        rest = start_exchange(f"gather_rest_l{l}", "gather", [(nm, shard2d(nm, l)) for nm, _ in SHARDED if nm != "w_in"],
                              (first[3],))
        return first, rest

    def gathered(pend, after):
        ex, state, names, _ = pend
        return dict(zip(names, ex.wait(state, after=after)[1]))

    x2d, mem2, t2d = x.reshape(bl * seq, d), mem.reshape(-1, d), loss_target.reshape(bl * seq, d)
    alpha = (2.0 * depth) ** 0.25
    soft, lb_all = _lower_bounds_fwd(lower_bounds)

    h, hb, saved, layer_wts = x2d, x2d.astype(BF16), [], []
    pending = start_gathers(0)
    for l in range(depth):
        first, rest = pending
        w_in_l = gathered(first, h)["w_in"]

        def rest_fn(after, l=l, rest=rest):
            wts = gathered(rest, after)
            for name in ("hg_norm_w", "b_gate", "ln1_g", "ln1_b", "ln2_g", "ln2_b"):
                wts[name] = weights[name][l][None, :]
            return wts

        deps = (rest[3],)
        if l + 1 < depth:
            pending = start_gathers(l + 1, (w_in_l, rest[3]))
            deps += (pending[0][3], pending[1][3])
        h, hb, sv, wts = _layer_fwd(h, hb, mem2, lb_all[l:l + 1], w_in_l, rest_fn, bl=bl, seq=seq, alpha=alpha, deps=deps)
        saved.append(sv)
        layer_wts.append(wts)
    loss, dh = _loss_head(h, t2d)

    shape3 = {name: (depth, weights[name].size // (depth * weights[name].shape[-1]), weights[name].shape[-1]) for name, _ in SHARDED}
    partial = [dict() for _ in range(depth)]
    smalls = [None] * depth
    outs = {name: None for name, _ in SHARDED}

    def finish_reduce(pend, l, after):
        ex, state, names, _ = pend
        sent, got = ex.wait(state, after=after)
        for nm, g_full, landed in zip(names, sent, got):
            partial[l][nm] = _sum_own_and_peers(me, g_full, shard_axis[nm], landed)

    def optimizer_step(l):
        names = [name for name, _ in SHARDED]
        theirs = _sibling_swap(f"swap_partials_l{l}", [partial[l][nm] for nm in names])
        for nm, other in zip(names, theirs):
            outs[nm] = _adamw_layer(weights[nm].reshape(shape3[nm]), mom_m[nm].reshape(shape3[nm]), mom_v[nm].reshape(shape3[nm]),
                                    partial[l][nm], other, l, outs[nm])
        return tuple(outs[nm][0] for nm in names)

    pending_mix, deps = [], ()
    dz2, dz2b, dg2, db2 = _ln_bwd(dh, saved[-1]["xhat2"], saved[-1]["rstd2"], layer_wts[-1]["ln2_g"])
    for l in reversed(range(depth)):
        dz1, dz1b, g_mlp = _mlp_bwd(dz2, dz2b, saved[l], layer_wts[l], alpha=alpha, deps=deps)
        g_mlp["ln2_g"], g_mlp["ln2_b"] = dg2[0:1], db2[0:1]
        pending_mlp = start_exchange(f"reduce_mlp_l{l}", "scatter", [(nm, g_mlp[nm]) for nm in ("w_up", "w_down")])
        deps = (pending_mlp[3],)
        if pending_mix:
            for pend in pending_mix:
                finish_reduce(pend, l + 1, dz1)
            deps += optimizer_step(l + 1)
        pending_mix = []

        def send(names, g, l=l, pending_mix=pending_mix):
            pend = start_exchange(f"reduce_{names[0]}_l{l}", "scatter", [(nm, g[nm]) for nm in names])
            pending_mix.append(pend)
            return pend[3]

        below = (saved[l - 1]["xhat2"], saved[l - 1]["rstd2"], layer_wts[l - 1]["ln2_g"]) if l > 0 else None
        out, g = _mix_bwd(dz1, dz1b, saved[l], mem2, lb_all[l:l + 1], layer_wts[l], bl=bl, seq=seq, alpha=alpha, send=send,
                          below=below, deps=deps)
        if l > 0:
            dz2, dz2b, dg2, db2 = out
        else:
            dh = out
        finish_reduce(pending_mlp, l, out[0] if l > 0 else out)
        deps = ()
        g.update(g_mlp, lower_bounds=g["lb"])
        smalls[l] = jnp.concatenate([g[nm] for nm in SMALL], axis=1)
    small_parts = _chip_exchange("reduce_small", [(jnp.stack(smalls), "bcast", 0)])[0]
    small_sum = _sum_chips_into(small_parts.reshape(N_CHIPS, depth, -1), jnp.zeros((1, depth, small_parts.shape[-1]), F32), 0)
    small_sum = small_sum.reshape(depth, 1, -1)
    small_theirs = _sibling_swap("swap_small", [small_sum])[0]
    for pend in pending_mix:
        finish_reduce(pend, 0, small_theirs)
    optimizer_step(0)

    outs = {name: [r.reshape(weights[name].shape) for r in res] for name, res in outs.items()}
    off = 0
    for name in SMALL:
        n = weights[name].shape[1]
        mine, other = small_sum[:, :, off:off + n], small_theirs[:, :, off:off + n]
        off += n
        if name == "lower_bounds":
            mine = _lower_bounds_bwd(soft, mine[:, 0, :])[:, None, :]
            other = _lower_bounds_bwd(soft, other[:, 0, :])[:, None, :]
        shp = (depth, 1, n)
        res = _adamw(weights[name].reshape(shp), mom_m[name].reshape(shp), mom_v[name].reshape(shp), mine, other)
        outs[name] = [r.reshape(weights[name].shape) for r in res]
    assert off == small_sum.shape[-1]

    total_loss = lax.psum(loss[0, 0], ("x", "y", "c"))
    result = [total_loss, dh.reshape(bl, seq, d)]
    for k in range(4):
        result += [outs[name][k] for name in WEIGHT_ORDER]
    return tuple(result)
```

```python
import functools

import jax
import jax.numpy as jnp
from jax import lax
from jax.experimental import pallas as pl
from jax.experimental.pallas import tpu as pltpu

F32 = jnp.float32
BF16 = jnp.bfloat16

HG_HEADS = 4
HG_F = 128
HG_CHUNK = 32
MEM_HEADS = 4
MEM_HEAD_DIM = 128
BRANCH_WIDTH = 512
N_BRANCH = 3
CONV_K = 3
LN_EPS = 1e-5
RMS_EPS = 1e-6
ADAM_LR = 0.001
ADAM_B1 = 0.9
ADAM_B2 = 0.999
ADAM_EPS = 1e-08
ADAM_WD = 0.01
ADAM_STEP = 10

VMEM_LIMIT = 48 * 1024 * 1024


def _cparams(sem):
    return pltpu.CompilerParams(dimension_semantics=sem, vmem_limit_bytes=VMEM_LIMIT)


def _dot(a, b, dims):
    return lax.dot_general(a, b, (dims, ((), ())), preferred_element_type=F32)


NN = ((1,), (0,))
NT = ((1,), (1,))
TN = ((0,), (0,))


def _pick(n, pref):
    for t in pref:
        if n % t == 0:
            return t
    return n


ANY_SPEC = pl.BlockSpec(memory_space=pl.ANY)


def _matmul(name, a, b, *, mode, out_dtype=F32, a_fn=None, a_extra=(), epi_fn=None, epi_extra=(), n_out=1, out_kinds=None,
            tm=512, tn=1024, tk=1024, deps=()):
    M, K = a.shape
    N = b.shape[1] if mode == "nn" else b.shape[0]
    tm, tn, tk = _pick(M, (tm, 256, 128, 8)), _pick(N, (tn, 896, 512, 256, 128)), _pick(K, (tk, 512, 256, 128))
    nk = K // tk
    n_ax, n_ex = len(a_extra), len(epi_extra)
    n_in = 2 + n_ax + n_ex + len(deps)
    out_dtypes = out_dtype if isinstance(out_dtype, (tuple, list)) else (out_dtype,) * n_out
    out_kinds = out_kinds or ("tile",) * n_out

    def body(*refs):
        a_ref, b_ref = refs[0], refs[1]
        ax_refs = refs[2:2 + n_ax]
        ex_refs = refs[2 + n_ax:2 + n_ax + n_ex]
        o_refs = refs[n_in:n_in + n_out]
        at = a_ref[...]
        at = a_fn(at, *[r[...] for r in ax_refs]) if a_fn is not None else at.astype(BF16)
        part = _dot(at, b_ref[...].astype(BF16), NN if mode == "nn" else NT)

        def finish(acc):
            outs = epi_fn(acc, *[r[...] for r in ex_refs]) if epi_fn is not None else (acc,)
            for o_ref, o, kind in zip(o_refs, outs, out_kinds):
                if kind == "rowsum":
                    @pl.when(pl.program_id(1) == 0)
                    def _(o_ref=o_ref):
                        o_ref[...] = jnp.zeros_like(o_ref)

                    o_ref[0:1, :] += o
                else:
                    o_ref[...] = o.astype(o_ref.dtype)

        if nk == 1:
            finish(part)
            return
        acc_ref = refs[-1]
        k = pl.program_id(2)

        @pl.when(k == 0)
        def _():
            acc_ref[...] = part

        @pl.when(jnp.logical_and(k > 0, k < nk - 1))
        def _():
            acc_ref[...] += part

        @pl.when(k == nk - 1)
        def _():
            finish(acc_ref[...] + part)

    b_mode = dict(pipeline_mode=pl.Buffered(1)) if (nk == 1 and N == tn) else {}
    in_specs = [pl.BlockSpec((tm, tk), lambda j, i, k: (i, k)),
                pl.BlockSpec((tk, tn), lambda j, i, k: (k, j), **b_mode) if mode == "nn"
                else pl.BlockSpec((tn, tk), lambda j, i, k: (j, k), **b_mode)]
    in_specs += [pl.BlockSpec((1, tk), lambda j, i, k: (0, k)) for _ in a_extra]
    for e in epi_extra:
        if e.shape[0] == 1:
            in_specs.append(pl.BlockSpec((1, tn), lambda j, i, k: (0, j)))
        elif e.shape[1] == 1:
            in_specs.append(pl.BlockSpec((tm, 1), lambda j, i, k: (i, 0)))
        else:
            in_specs.append(pl.BlockSpec((tm, tn), lambda j, i, k: (i, j)))
    in_specs += [ANY_SPEC] * len(deps)
    out_specs, out_shapes = [], []
    for kind, dt in zip(out_kinds, out_dtypes):
        if kind == "col":
            out_specs.append(pl.BlockSpec((tm, 1), lambda j, i, k: (i, 0)))
            out_shapes.append(jax.ShapeDtypeStruct((M, 1), dt))
        elif kind == "rowsum":
            out_specs.append(pl.BlockSpec((8, tn), lambda j, i, k: (0, j)))
            out_shapes.append(jax.ShapeDtypeStruct((8, N), dt))
        else:
            out_specs.append(pl.BlockSpec((tm, tn), lambda j, i, k: (i, j)))
            out_shapes.append(jax.ShapeDtypeStruct((M, N), dt))
    out = pl.pallas_call(
        body,
        name=name,
        grid=(N // tn, M // tm, nk),
        in_specs=in_specs,
        out_specs=out_specs,
        out_shape=out_shapes,
        scratch_shapes=[pltpu.VMEM((tm, tn), F32)] if nk > 1 else [],
        compiler_params=_cparams(("arbitrary", "arbitrary", "arbitrary")),
    )(a, b, *a_extra, *epi_extra, *deps)
    return out[0] if n_out == 1 else out


def _matmul_tn(name, a, b, *, a_fn=None, a_extra=(), a_cols=None, b_cols=None, ta=1024, tb=1024, tt=1024, out_dtype=F32, deps=()):
    T = a.shape[0]
    a0, Ka = a_cols if a_cols is not None else (0, a.shape[1])
    b0, Nb = b_cols if b_cols is not None else (0, b.shape[1])
    ta, tb, tt = _pick(Ka, (ta, 512, 256, 128)), _pick(Nb, (tb, 896, 512, 256, 128)), _pick(T, (tt, 512, 256, 128))
    assert a0 % ta == 0 and b0 % tb == 0
    a0, b0 = a0 // ta, b0 // tb
    nt = T // tt
    n_ax = len(a_extra)

    def body(*refs):
        a_ref, b_ref = refs[0], refs[1]
        ax_refs = refs[2:2 + n_ax]
        o_ref = refs[2 + n_ax + len(deps)]
        acc_ref = refs[-1]
        t = pl.program_id(2)
        at = a_ref[...]
        at = a_fn(at, *[r[...] for r in ax_refs]) if a_fn is not None else at.astype(BF16)
        part = _dot(at, b_ref[...].astype(BF16), TN)

        @pl.when(t == 0)
        def _():
            acc_ref[...] = part

        @pl.when(jnp.logical_and(t > 0, t < nt - 1))
        def _():
            acc_ref[...] += part

        @pl.when(t == nt - 1)
        def _():
            o_ref[...] = (acc_ref[...] + part if nt > 1 else part).astype(o_ref.dtype)

    in_specs = [pl.BlockSpec((tt, ta), lambda i, j, t: (t, a0 + i)), pl.BlockSpec((tt, tb), lambda i, j, t: (t, b0 + j))]
    in_specs += [pl.BlockSpec((1, ta), lambda i, j, t: (0, a0 + i)) for _ in a_extra]
    in_specs += [ANY_SPEC] * len(deps)
    return pl.pallas_call(
        body,
        name=name,
        grid=(Ka // ta, Nb // tb, nt),
        in_specs=in_specs,
        out_specs=pl.BlockSpec((ta, tb), lambda i, j, t: (i, j)),
        out_shape=jax.ShapeDtypeStruct((Ka, Nb), out_dtype),
        scratch_shapes=[pltpu.VMEM((ta, tb), F32)],
        compiler_params=_cparams(("parallel", "parallel", "arbitrary")),
    )(a, b, *a_extra, *deps)


W = BRANCH_WIDTH
C_CB, C_CC, C_CH, C_HQ, C_HF, C_HI, C_HG, C_MQ, N_MIX = 0, W, 2 * W, 3 * W, 4 * W, 5 * W, 6 * W, 7 * W, 8 * W
TS_MIX = 256
PREV_ROWS = 16


def _sigmoid(x):
    return jax.nn.sigmoid(x)


def _chunk_pos(shape):
    return lax.broadcasted_iota(jnp.int32, shape, 0) & (HG_CHUNK - 1)


def _seg_cumsum(x, pos):
    sh = 1
    while sh < HG_CHUNK:
        x = x + jnp.where(pos >= sh, pltpu.roll(x, sh, 0), 0.0)
        sh *= 2
    return x


def _seg_rev_cumsum(x, pos):
    n = x.shape[0]
    sh = 1
    while sh < HG_CHUNK:
        x = x + jnp.where(pos < HG_CHUNK - sh, pltpu.roll(x, n - sh, 0), 0.0)
        sh *= 2
    return x


def _chunk_mask(ts):
    r = lax.broadcasted_iota(jnp.int32, (ts, ts), 0)
    c = lax.broadcasted_iota(jnp.int32, (ts, ts), 1)
    return jnp.logical_and((r // HG_CHUNK) == (c // HG_CHUNK), c <= r)


def _hgrn_gates(p_ref, lb):
    q = p_ref[:, C_HQ:C_HQ + W].astype(F32)
    fl = p_ref[:, C_HF:C_HF + W].astype(F32)
    sig = _sigmoid(fl)
    f = lb + (1.0 - lb) * sig
    logf = jnp.log(f)
    k = (1.0 - lb) * _sigmoid(-fl)
    sq = _sigmoid(q)
    qs = q * sq
    return q, sq, qs, sig, f, logf, k


def _hgrn_decays(logf, bc_sc, ts):
    pos = _chunk_pos(logf.shape)
    bc = _seg_cumsum(logf, pos)
    bc_sc[...] = bc
    nc = ts // HG_CHUNK
    bref = jnp.concatenate(
        [jnp.broadcast_to(bc_sc[n * HG_CHUNK + HG_CHUNK // 2 - 1:n * HG_CHUNK + HG_CHUNK // 2, :], (HG_CHUNK, W)) for n in range(nc)], axis=0)
    blast = jnp.concatenate(
        [jnp.broadcast_to(bc_sc[(n + 1) * HG_CHUNK - 1:(n + 1) * HG_CHUNK, :], (HG_CHUNK, W)) for n in range(nc)], axis=0)
    return pos, bc, bref, blast


def _conv_shift_down(u, carry_ref, row):
    n = carry_ref.shape[0]
    last, before = carry_ref[n - 1:n, :], carry_ref[n - 2:n - 1, :]
    u1 = jnp.where(row == 0, last, pltpu.roll(u, 1, 0))
    u2 = jnp.where(row == 0, before, jnp.where(row == 1, last, pltpu.roll(u, 2, 0)))
    return u1, u2


def _attn_probs(qh, kh):
    s = _dot(qh, kh, NT) * (MEM_HEAD_DIM ** -0.5)
    e = jnp.exp(s - jnp.max(s, axis=-1, keepdims=True))
    return e / jnp.sum(e, axis=-1, keepdims=True)


def _mixer_fwd(p, mk, mv, lb, conv_w, norm_w, *, bl, seq):
    T = p.shape[0]
    ts = TS_MIX
    ns = seq // ts
    nc = ts // HG_CHUNK
    ml = mk.shape[0] // bl

    def body(p_ref, mk_ref, mv_ref, lb_ref, cw_ref, nw_ref, y_ref, st_ref, opre_ref, state_sc, carry_sc, bc_sc):
        @pl.when(pl.program_id(1) == 0)
        def _():
            state_sc[...] = jnp.zeros_like(state_sc)
            carry_sc[...] = jnp.zeros_like(carry_sc)

        cb, cc, ch = (p_ref[:, c0:c0 + W].astype(F32) for c0 in (C_CB, C_CC, C_CH))
        u = cc * ch
        row = lax.broadcasted_iota(jnp.int32, (ts, W), 0)
        u1, u2 = _conv_shift_down(u, carry_sc, row)
        yconv = u2 * cw_ref[0:1, :] + u1 * cw_ref[1:2, :] + u * cw_ref[2:3, :]
        y_ref[:, 0:W] = (cb * yconv).astype(BF16)
        carry_sc[...] = u[ts - 8:ts, :]

        lbv = lb_ref[...]
        _, _, qs, _, _, logf, k = _hgrn_gates(p_ref, lbv)
        pos, bc, bref, blast = _hgrn_decays(logf, bc_sc, ts)
        a_all = (qs * jnp.exp(bc - bref)).astype(BF16)
        bk_all = (k * jnp.exp(bref - bc)).astype(BF16)
        qin_all = (qs * jnp.exp(bc)).astype(BF16)
        kout_all = (k * jnp.exp(blast - bc)).astype(BF16)
        v_all = p_ref[:, C_HI:C_HI + W].astype(BF16)
        mask = _chunk_mask(ts)
        heads = [slice(h * HG_F, (h + 1) * HG_F) for h in range(HG_HEADS)]
        st = [state_sc[h] for h in range(HG_HEADS)]
        o_inter = [[] for _ in range(HG_HEADS)]
        for n in range(nc):
            rows = slice(n * HG_CHUNK, (n + 1) * HG_CHUNK)
            for h, hs in enumerate(heads):
                st_ref[n, h] = st[h]
                o_inter[h].append(_dot(qin_all[rows, hs], st[h].astype(BF16), NT))
                kv = _dot(v_all[rows, hs], kout_all[rows, hs], TN)
                decay = jnp.exp(bc_sc[(n + 1) * HG_CHUNK - 1:(n + 1) * HG_CHUNK, hs])
                st[h] = st[h] * decay + kv
        for h, hs in enumerate(heads):
            state_sc[h] = st[h]
            scores = jnp.where(mask, _dot(a_all[:, hs], bk_all[:, hs], NT), 0.0)
            o = _dot(scores.astype(BF16), v_all[:, hs], NN) + jnp.concatenate(o_inter[h], axis=0)
            opre_ref[:, hs] = o
            on = o * lax.rsqrt(jnp.mean(o * o, axis=-1, keepdims=True) + RMS_EPS) * nw_ref[...]
            g = p_ref[:, C_HG + h * HG_F:C_HG + (h + 1) * HG_F].astype(F32)
            y_ref[:, W + h * HG_F:W + (h + 1) * HG_F] = (on * (g * _sigmoid(g))).astype(BF16)

        for h in range(MEM_HEADS):
            hs = slice(h * MEM_HEAD_DIM, (h + 1) * MEM_HEAD_DIM)
            qh = p_ref[:, C_MQ + h * MEM_HEAD_DIM:C_MQ + (h + 1) * MEM_HEAD_DIM].astype(BF16)
            prob = _attn_probs(qh, mk_ref[:, hs])
            y_ref[:, 2 * W + h * MEM_HEAD_DIM:2 * W + (h + 1) * MEM_HEAD_DIM] = _dot(prob.astype(BF16), mv_ref[:, hs], NN).astype(BF16)

    return pl.pallas_call(
        body,
        name="mixer_fwd",
        grid=(bl, ns),
        in_specs=[
            pl.BlockSpec((ts, N_MIX), lambda b, s: (b * ns + s, 0)),
            pl.BlockSpec((ml, W), lambda b, s: (b, 0)),
            pl.BlockSpec((ml, W), lambda b, s: (b, 0)),
            pl.BlockSpec((1, W), lambda b, s: (0, 0)),
            pl.BlockSpec((CONV_K, W), lambda b, s: (0, 0)),
            pl.BlockSpec((1, HG_F), lambda b, s: (0, 0)),
        ],
        out_specs=[
            pl.BlockSpec((ts, 3 * W), lambda b, s: (b * ns + s, 0)),
            pl.BlockSpec((nc, HG_HEADS, HG_F, HG_F), lambda b, s: (b * ns + s, 0, 0, 0)),
            pl.BlockSpec((ts, W), lambda b, s: (b * ns + s, 0)),
        ],
        out_shape=[
            jax.ShapeDtypeStruct((T, 3 * W), BF16),
            jax.ShapeDtypeStruct((T // HG_CHUNK, HG_HEADS, HG_F, HG_F), F32),
            jax.ShapeDtypeStruct((T, W), F32),
        ],
        scratch_shapes=[pltpu.VMEM((HG_HEADS, HG_F, HG_F), F32), pltpu.VMEM((8, W), F32), pltpu.VMEM((ts, W), F32)],
        compiler_params=_cparams(("arbitrary", "arbitrary")),
    )(p, mk, mv, lb, conv_w, norm_w)


def _mixer_bwd(p, dy, dp_gates, st, opre, mk, mv, lb, conv_w, norm_w, *, bl, seq, deps=()):
    T, nin = p.shape
    ts = TS_MIX
    ns = seq // ts
    nc = ts // HG_CHUNK
    ml = mk.shape[0] // bl
    mid, last = HG_CHUNK // 2 - 1, HG_CHUNK - 1

    def body(p_ref, pprev_ref, dy_ref, dpin_ref, st_ref, opre_ref, mk_ref, mv_ref, lb_ref, cw_ref, nw_ref, *rest):
        (dp_ref, dmk_ref, dmv_ref, dcw_ref, dnw_ref, dlb_ref, dstate_sc, carry_sc, uprev_sc, ab_sc, bkb_sc, qinb_sc, koutb_sc,
         dob_sc, dv_sc, da_sc, dbk_sc, dqin_sc, dkout_sc, dec_sc, ddec_sc) = rest[len(deps):]
        del dpin_ref
        b, s = pl.program_id(0), pl.program_id(1)

        @pl.when(s == 0)
        def _():
            dstate_sc[...] = jnp.zeros_like(dstate_sc)
            carry_sc[...] = jnp.zeros_like(carry_sc)
            dmk_ref[...] = jnp.zeros_like(dmk_ref)
            dmv_ref[...] = jnp.zeros_like(dmv_ref)

        @pl.when(jnp.logical_and(b == 0, s == 0))
        def _():
            dcw_ref[...] = jnp.zeros_like(dcw_ref)
            dnw_ref[...] = jnp.zeros_like(dnw_ref)
            dlb_ref[...] = jnp.zeros_like(dlb_ref)

        cb, cc, ch = (p_ref[:, c0:c0 + W].astype(F32) for c0 in (C_CB, C_CC, C_CH))
        u = cc * ch
        row = lax.broadcasted_iota(jnp.int32, (ts, W), 0)
        uprev = pprev_ref[:, C_CC:C_CC + W].astype(F32) * pprev_ref[:, C_CH:C_CH + W].astype(F32)
        uprev_sc[...] = jnp.where(s == ns - 1, 0.0, uprev)
        u1, u2 = _conv_shift_down(u, uprev_sc, row)
        w0, w1, w2 = cw_ref[0:1, :], cw_ref[1:2, :], cw_ref[2:3, :]
        dya = dy_ref[:, 0:W].astype(F32)
        dp_ref[:, C_CB:C_CB + W] = (dya * (u2 * w0 + u1 * w1 + u * w2)).astype(BF16)
        dv = cb * dya
        dv1 = jnp.where(row == ts - 1, carry_sc[0:1, :], pltpu.roll(dv, ts - 1, 0))
        dv2 = jnp.where(row == ts - 1, carry_sc[1:2, :], jnp.where(row == ts - 2, carry_sc[0:1, :], pltpu.roll(dv, ts - 2, 0)))
        du = dv * w2 + dv1 * w1 + dv2 * w0
        dp_ref[:, C_CC:C_CC + W] = (du * ch).astype(BF16)
        dp_ref[:, C_CH:C_CH + W] = (du * cc).astype(BF16)
        dcw_ref[0:1, :] += jnp.sum(dv * u2, axis=0, keepdims=True)
        dcw_ref[1:2, :] += jnp.sum(dv * u1, axis=0, keepdims=True)
        dcw_ref[2:3, :] += jnp.sum(dv * u, axis=0, keepdims=True)
        carry_sc[...] = dv[0:8, :]

        mask = _chunk_mask(ts)
        pos_c = _chunk_pos((HG_CHUNK, HG_F))
        nw = nw_ref[...]

        def block(n, h):
            rows = slice(n * HG_CHUNK, (n + 1) * HG_CHUNK)
            return rows, slice(h * HG_F, (h + 1) * HG_F)

        def gates(rows, h):
            lbh = lb_ref[:, h * HG_F:(h + 1) * HG_F]
            q = p_ref[rows, C_HQ + h * HG_F:C_HQ + (h + 1) * HG_F].astype(F32)
            fl = p_ref[rows, C_HF + h * HG_F:C_HF + (h + 1) * HG_F].astype(F32)
            sig = _sigmoid(fl)
            f = lbh + (1.0 - lbh) * sig
            k = (1.0 - lbh) * _sigmoid(-fl)
            sq = _sigmoid(q)
            qs = q * sq
            bc = _seg_cumsum(jnp.log(f), pos_c)
            bref = jnp.sum(jnp.where(pos_c == mid, bc, 0.0), axis=0, keepdims=True)
            blast = jnp.sum(jnp.where(pos_c == last, bc, 0.0), axis=0, keepdims=True)
            ea, eb, eq, ek = jnp.exp(bc - bref), jnp.exp(bref - bc), jnp.exp(bc), jnp.exp(blast - bc)
            return dict(lbh=lbh, q=q, sq=sq, sig=sig, f=f, blast=blast, ea=ea, eb=eb, eq=eq, ek=ek,
                        a=qs * ea, bk=k * eb, qin=qs * eq, kout=k * ek)

        dnw = jnp.zeros((1, HG_F), F32)
        for n in range(nc):
            for h in range(HG_HEADS):
                rows, hs = block(n, h)
                fw = gates(rows, h)
                ab_sc[rows, hs] = fw["a"].astype(BF16)
                bkb_sc[rows, hs] = fw["bk"].astype(BF16)
                qinb_sc[rows, hs] = fw["qin"].astype(BF16)
                koutb_sc[rows, hs] = fw["kout"].astype(BF16)
                dec_sc[n:n + 1, hs] = jnp.exp(fw["blast"])
                o = opre_ref[rows, hs]
                g = p_ref[rows, C_HG + h * HG_F:C_HG + (h + 1) * HG_F].astype(F32)
                sg = _sigmoid(g)
                r = lax.rsqrt(jnp.mean(o * o, axis=-1, keepdims=True) + RMS_EPS)
                dyb = dy_ref[rows, W + h * HG_F:W + (h + 1) * HG_F].astype(F32)
                dp_ref[rows, C_HG + h * HG_F:C_HG + (h + 1) * HG_F] = (
                    dyb * (o * r * nw) * (sg * (1.0 + g * (1.0 - sg)))).astype(BF16)
                don = dyb * (g * sg)
                dnw = dnw + jnp.sum(don * o * r, axis=0, keepdims=True)
                dn = don * nw
                dob_sc[rows, hs] = (r * (dn - o * (r * r) * jnp.mean(dn * o, axis=-1, keepdims=True))).astype(BF16)
        dnw_ref[0:1, :] += dnw

        for h in range(HG_HEADS):
            hs = slice(h * HG_F, (h + 1) * HG_F)
            ab, bkb, qinb, koutb, dob = ab_sc[:, hs], bkb_sc[:, hs], qinb_sc[:, hs], koutb_sc[:, hs], dob_sc[:, hs]
            vb = p_ref[:, C_HI + h * HG_F:C_HI + (h + 1) * HG_F].astype(BF16)
            scores = jnp.where(mask, _dot(ab, bkb, NT), 0.0)
            dscores = jnp.where(mask, _dot(dob, vb, NT), 0.0).astype(BF16)
            dv_sc[:, hs] = _dot(scores.astype(BF16), dob, TN)
            da_sc[:, hs] = _dot(dscores, bkb, NN)
            dbk_sc[:, hs] = _dot(dscores, ab, TN)
        dst = [dstate_sc[h] for h in range(HG_HEADS)]
        for n in reversed(range(nc)):
            for h in range(HG_HEADS):
                rows, hs = block(n, h)
                st_n = st_ref[n, h]
                decay = dec_sc[n:n + 1, hs]
                dstb = dst[h].astype(BF16)
                dob_n = dob_sc[rows, hs]
                dv_sc[rows, hs] += _dot(koutb_sc[rows, hs], dstb, NT)
                dkout_sc[rows, hs] = _dot(p_ref[rows, C_HI + h * HG_F:C_HI + (h + 1) * HG_F].astype(BF16), dstb, NN)
                ddec_sc[n:n + 1, hs] = jnp.sum(dst[h] * st_n, axis=0, keepdims=True) * decay
                dqin_sc[rows, hs] = _dot(dob_n, st_n.astype(BF16), NN)
                dst[h] = dst[h] * decay + _dot(dob_n, qinb_sc[rows, hs], TN)
        for h in range(HG_HEADS):
            dstate_sc[h] = dst[h]

        for h in range(HG_HEADS):
            dlb = jnp.zeros((1, HG_F), F32)
            for n in range(nc):
                rows, hs = block(n, h)
                fw = gates(rows, h)
                da, dbk, dqin, dkout = da_sc[rows, hs], dbk_sc[rows, hs], dqin_sc[rows, hs], dkout_sc[rows, hs]
                dqs = da * fw["ea"] + dqin * fw["eq"]
                dk = dbk * fw["eb"] + dkout * fw["ek"]
                t_a, t_b, t_q, t_k = da * fw["a"], dbk * fw["bk"], dqin * fw["qin"], dkout * fw["kout"]
                s_ref = jnp.sum(t_b - t_a, axis=0, keepdims=True)
                s_last = jnp.sum(t_k, axis=0, keepdims=True) + ddec_sc[n:n + 1, hs]
                dbc = (t_a - t_b + t_q - t_k) + jnp.where(pos_c == mid, s_ref, 0.0) + jnp.where(pos_c == last, s_last, 0.0)
                dfk = _seg_rev_cumsum(dbc, pos_c) / fw["f"] - dk
                sig, q, sq = fw["sig"], fw["q"], fw["sq"]
                dp_ref[rows, C_HF + h * HG_F:C_HF + (h + 1) * HG_F] = (dfk * (1.0 - fw["lbh"]) * sig * (1.0 - sig)).astype(BF16)
                dlb = dlb + jnp.sum(dfk * (1.0 - sig), axis=0, keepdims=True)
                dp_ref[rows, C_HQ + h * HG_F:C_HQ + (h + 1) * HG_F] = (dqs * (sq * (1.0 + q * (1.0 - sq)))).astype(BF16)
                dp_ref[rows, C_HI + h * HG_F:C_HI + (h + 1) * HG_F] = dv_sc[rows, hs].astype(BF16)
            dlb_ref[0:1, h * HG_F:(h + 1) * HG_F] += dlb

        for h in range(MEM_HEADS):
            hs = slice(h * MEM_HEAD_DIM, (h + 1) * MEM_HEAD_DIM)
            qh = p_ref[:, C_MQ + h * MEM_HEAD_DIM:C_MQ + (h + 1) * MEM_HEAD_DIM].astype(BF16)
            kh, vh = mk_ref[:, hs], mv_ref[:, hs]
            prob = _attn_probs(qh, kh)
            dob = dy_ref[:, 2 * W + h * MEM_HEAD_DIM:2 * W + (h + 1) * MEM_HEAD_DIM].astype(BF16)
            dmv_ref[:, hs] += _dot(prob.astype(BF16), dob, TN)
            dprob = _dot(dob, vh, NT)
            ds = prob * (dprob - jnp.sum(dprob * prob, axis=-1, keepdims=True)) * (MEM_HEAD_DIM ** -0.5)
            dsb = ds.astype(BF16)
            dp_ref[:, C_MQ + h * MEM_HEAD_DIM:C_MQ + (h + 1) * MEM_HEAD_DIM] = _dot(dsb, kh, NN).astype(BF16)
            dmk_ref[:, hs] += _dot(dsb, qh, TN)

    def tile(b, s):
        return b * ns + (ns - 1 - s)

    return pl.pallas_call(
        body,
        name="mixer_bwd",
        grid=(bl, ns),
        in_specs=[
            pl.BlockSpec((ts, N_MIX), lambda b, s: (tile(b, s), 0)),
            pl.BlockSpec((PREV_ROWS, N_MIX), lambda b, s: (jnp.maximum(tile(b, s) * (ts // PREV_ROWS) - 1, 0), 0)),
            pl.BlockSpec((ts, 3 * W), lambda b, s: (tile(b, s), 0)),
            pl.BlockSpec(memory_space=pl.ANY),
            pl.BlockSpec((nc, HG_HEADS, HG_F, HG_F), lambda b, s: (tile(b, s), 0, 0, 0)),
            pl.BlockSpec((ts, W), lambda b, s: (tile(b, s), 0)),
            pl.BlockSpec((ml, W), lambda b, s: (b, 0)),
            pl.BlockSpec((ml, W), lambda b, s: (b, 0)),
            pl.BlockSpec((1, W), lambda b, s: (0, 0)),
            pl.BlockSpec((CONV_K, W), lambda b, s: (0, 0)),
            pl.BlockSpec((1, HG_F), lambda b, s: (0, 0)),
        ] + [ANY_SPEC] * len(deps),
        out_specs=[
            pl.BlockSpec((ts, N_MIX), lambda b, s: (tile(b, s), 0)),
            pl.BlockSpec((ml, W), lambda b, s: (b, 0)),
            pl.BlockSpec((ml, W), lambda b, s: (b, 0)),
            pl.BlockSpec((8, W), lambda b, s: (0, 0)),
            pl.BlockSpec((8, HG_F), lambda b, s: (0, 0)),
            pl.BlockSpec((8, W), lambda b, s: (0, 0)),
        ],
        out_shape=[
            jax.ShapeDtypeStruct((T, nin), BF16),
            jax.ShapeDtypeStruct((bl * ml, W), F32),
            jax.ShapeDtypeStruct((bl * ml, W), F32),
            jax.ShapeDtypeStruct((8, W), F32),
            jax.ShapeDtypeStruct((8, HG_F), F32),
            jax.ShapeDtypeStruct((8, W), F32),
        ],
        input_output_aliases={3: 0},
        scratch_shapes=[pltpu.VMEM((HG_HEADS, HG_F, HG_F), F32), pltpu.VMEM((8, W), F32), pltpu.VMEM((PREV_ROWS, W), F32)]
        + [pltpu.VMEM((ts, W), BF16)] * 5 + [pltpu.VMEM((ts, W), F32)] * 5 + [pltpu.VMEM((nc, W), F32)] * 2,
        compiler_params=_cparams(("arbitrary", "arbitrary")),
    )(p, p, dy, dp_gates, st, opre, mk, mv, lb, conv_w, norm_w, *deps)


def _layer_norm_stats(z):
    mu = jnp.mean(z, axis=-1, keepdims=True)
    zc = z - mu
    rstd = lax.rsqrt(jnp.mean(zc * zc, axis=-1, keepdims=True) + LN_EPS)
    return zc * rstd, rstd


def _gate_specs(tm, d):
    g0 = N_MIX // d
    return [pl.BlockSpec((tm, d), functools.partial(lambda i, k: (i, g0 + k), k=k)) for k in range(N_BRANCH)]


def _merge_fwd(y, p, x0, wb, wo, bg, ln_g, ln_b, *, alpha, tm=256):
    T, d = x0.shape
    assert N_MIX % d == 0
    tm = _pick(T, (tm, 128, 8))

    def body(y_ref, g0_ref, g1_ref, g2_ref, x_ref, wb_ref, wo_ref, bg_ref, lg_ref, lb_ref, r_ref, mg_ref, xh_ref, rs_ref, x1b_ref):
        merged = None
        for i, g_ref in enumerate((g0_ref, g1_ref, g2_ref)):
            r = _dot(y_ref[:, i * W:(i + 1) * W], wb_ref[i * W:(i + 1) * W, :], NN)
            r_ref[:, i * d:(i + 1) * d] = r.astype(BF16)
            t = _sigmoid(g_ref[...].astype(F32) + bg_ref[:, i * d:(i + 1) * d]) * r
            merged = t if merged is None else merged + t
        mb = merged.astype(BF16)
        mg_ref[...] = mb
        z = alpha * x_ref[...] + _dot(mb, wo_ref[...], NN)
        xh, rs = _layer_norm_stats(z)
        xh_ref[...], rs_ref[...] = xh, rs
        x1b_ref[...] = (xh * lg_ref[...] + lb_ref[...]).astype(BF16)

    row = lambda i: (i, 0)
    fix = lambda i: (0, 0)
    return pl.pallas_call(
        body,
        name="merge_fwd",
        grid=(T // tm,),
        in_specs=[pl.BlockSpec((tm, 3 * W), row)] + _gate_specs(tm, d) + [
            pl.BlockSpec((tm, d), row), pl.BlockSpec((3 * W, d), fix), pl.BlockSpec((d, d), fix), pl.BlockSpec((1, 3 * d), fix),
            pl.BlockSpec((1, d), fix), pl.BlockSpec((1, d), fix)],
        out_specs=[pl.BlockSpec((tm, 3 * d), row), pl.BlockSpec((tm, d), row), pl.BlockSpec((tm, d), row), pl.BlockSpec((tm, 1), row),
                   pl.BlockSpec((tm, d), row)],
        out_shape=[jax.ShapeDtypeStruct((T, 3 * d), BF16), jax.ShapeDtypeStruct((T, d), BF16),
                   jax.ShapeDtypeStruct((T, d), F32), jax.ShapeDtypeStruct((T, 1), F32), jax.ShapeDtypeStruct((T, d), BF16)],
        compiler_params=_cparams(("parallel",)),
    )(y, p, p, p, x0, wb, wo, bg, ln_g, ln_b)


def _merge_bwd(dz, p, r, wb, wo, bg, *, tm=256):
    T, d = dz.shape
    nin = p.shape[1]
    tm = _pick(T, (tm, 128, 8))

    def body(dz_ref, g0_ref, g1_ref, g2_ref, r_ref, wb_ref, wo_ref, bg_ref, dr_ref, dp_ref, dy_ref, dbg_ref):
        @pl.when(pl.program_id(0) == 0)
        def _():
            dbg_ref[...] = jnp.zeros_like(dbg_ref)

        dmerged = _dot(dz_ref[...].astype(BF16), wo_ref[...], NT)
        dp_ref[:, 0:N_MIX] = jnp.zeros((tm, N_MIX), BF16)
        for i, g_ref in enumerate((g0_ref, g1_ref, g2_ref)):
            cs = slice(i * d, (i + 1) * d)
            s = _sigmoid(g_ref[...].astype(F32) + bg_ref[:, cs])
            drb = (dmerged * s).astype(BF16)
            dr_ref[:, cs] = drb
            dgate = dmerged * r_ref[:, cs].astype(F32) * s * (1.0 - s)
            dp_ref[:, N_MIX + i * d:N_MIX + (i + 1) * d] = dgate.astype(BF16)
            dbg_ref[0:1, cs] += jnp.sum(dgate, axis=0, keepdims=True)
            dy_ref[:, i * W:(i + 1) * W] = _dot(drb, wb_ref[i * W:(i + 1) * W, :], NT).astype(BF16)

    row = lambda i: (i, 0)
    fix = lambda i: (0, 0)
    return pl.pallas_call(
        body,
        name="merge_bwd",
        grid=(T // tm,),
        in_specs=[pl.BlockSpec((tm, d), row)] + _gate_specs(tm, d) + [
            pl.BlockSpec((tm, 3 * d), row), pl.BlockSpec((3 * W, d), fix), pl.BlockSpec((d, d), fix), pl.BlockSpec((1, 3 * d), fix)],
        out_specs=[pl.BlockSpec((tm, 3 * d), row), pl.BlockSpec((tm, nin), row), pl.BlockSpec((tm, 3 * W), row),
                   pl.BlockSpec((8, 3 * d), fix)],
        out_shape=[jax.ShapeDtypeStruct((T, 3 * d), BF16), jax.ShapeDtypeStruct((T, nin), BF16),
                   jax.ShapeDtypeStruct((T, 3 * W), BF16), jax.ShapeDtypeStruct((8, 3 * d), F32)],
        compiler_params=_cparams(("arbitrary",)),
    )(dz, p, p, p, r, wb, wo, bg)


def _mlp_fwd(xhat1, x1b, g1, b1, wu, wd, g2, b2, *, alpha, tm=512, tf=2048):
    T, d = xhat1.shape
    ff = wu.shape[1]
    tm, tf = _pick(T, (tm, 256, 128, 8)), _pick(ff, (tf, 1024, 512, 256, 128))
    nf = ff // tf

    def body(xh_ref, x1b_ref, g1_ref, b1_ref, wu_ref, wd_ref, g2_ref, b2_ref, a_ref, xh2_ref, rs2_ref, x2_ref, x2b_ref, acc_ref):
        f = pl.program_id(1)
        a = _dot(x1b_ref[...], wu_ref[...], NN)
        a_ref[...] = a.astype(BF16)
        h = jnp.square(jnp.maximum(a, 0.0))
        part = _dot(h.astype(BF16), wd_ref[...], NN)

        @pl.when(f == 0)
        def _():
            acc_ref[...] = part

        @pl.when(jnp.logical_and(f > 0, f < nf - 1))
        def _():
            acc_ref[...] += part

        @pl.when(f == nf - 1)
        def _():
            x1 = xh_ref[...] * g1_ref[...] + b1_ref[...]
            xh2, rs2 = _layer_norm_stats(alpha * x1 + (acc_ref[...] + part if nf > 1 else part))
            xh2_ref[...] = xh2
            rs2_ref[...] = rs2
            x2 = xh2 * g2_ref[...] + b2_ref[...]
            x2_ref[...] = x2
            x2b_ref[...] = x2.astype(BF16)

    row = lambda i, f: (i, 0)
    fix = lambda i, f: (0, 0)
    return pl.pallas_call(
        body,
        name="mlp_fwd",
        grid=(T // tm, nf),
        in_specs=[pl.BlockSpec((tm, d), row), pl.BlockSpec((tm, d), row), pl.BlockSpec((1, d), fix), pl.BlockSpec((1, d), fix),
                  pl.BlockSpec((d, tf), lambda i, f: (0, f)), pl.BlockSpec((tf, d), lambda i, f: (f, 0)),
                  pl.BlockSpec((1, d), fix), pl.BlockSpec((1, d), fix)],
        out_specs=[pl.BlockSpec((tm, tf), lambda i, f: (i, f)), pl.BlockSpec((tm, d), row), pl.BlockSpec((tm, 1), row),
                   pl.BlockSpec((tm, d), row), pl.BlockSpec((tm, d), row)],
        out_shape=[jax.ShapeDtypeStruct((T, ff), BF16), jax.ShapeDtypeStruct((T, d), F32), jax.ShapeDtypeStruct((T, 1), F32),
                   jax.ShapeDtypeStruct((T, d), F32), jax.ShapeDtypeStruct((T, d), BF16)],
        scratch_shapes=[pltpu.VMEM((tm, d), F32)],
        compiler_params=_cparams(("parallel", "arbitrary")),
    )(xhat1, x1b, g1, b1, wu, wd, g2, b2)


def _ln_bwd(dy, xhat, rstd, g, *, tm=512, deps=()):
    T, d = dy.shape
    tm = _pick(T, (tm, 256, 128, 8))

    def body(dy_ref, xh_ref, rs_ref, g_ref, *rest):
        dz_ref, dzb_ref, dg_ref, db_ref = rest[len(deps):]

        @pl.when(pl.program_id(0) == 0)
        def _():
            dg_ref[...] = jnp.zeros_like(dg_ref)
            db_ref[...] = jnp.zeros_like(db_ref)

        dy_, xh = dy_ref[...], xh_ref[...]
        dg_ref[0:1, :] += jnp.sum(dy_ * xh, axis=0, keepdims=True)
        db_ref[0:1, :] += jnp.sum(dy_, axis=0, keepdims=True)
        dxh = dy_ * g_ref[...]
        dz = rs_ref[...] * (dxh - jnp.mean(dxh, axis=-1, keepdims=True) - xh * jnp.mean(dxh * xh, axis=-1, keepdims=True))
        dz_ref[...] = dz
        dzb_ref[...] = dz.astype(BF16)

    row = lambda i: (i, 0)
    fix = lambda i: (0, 0)
    return pl.pallas_call(
        body,
        name="ln_bwd",
        grid=(T // tm,),
        in_specs=[pl.BlockSpec((tm, d), row), pl.BlockSpec((tm, d), row), pl.BlockSpec((tm, 1), row), pl.BlockSpec((1, d), fix)]
        + [ANY_SPEC] * len(deps),
        out_specs=[pl.BlockSpec((tm, d), row), pl.BlockSpec((tm, d), row), pl.BlockSpec((8, d), fix), pl.BlockSpec((8, d), fix)],
        out_shape=[jax.ShapeDtypeStruct((T, d), F32), jax.ShapeDtypeStruct((T, d), BF16), jax.ShapeDtypeStruct((8, d), F32),
                   jax.ShapeDtypeStruct((8, d), F32)],
        compiler_params=_cparams(("arbitrary",)),
    )(dy, xhat, rstd, g, *deps)


def _loss_head(y, target, *, tm=512):
    T, d = y.shape
    tm = _pick(T, (tm, 256, 128, 8))
    n = T // tm

    def body(y_ref, t_ref, loss_ref, dy_ref, acc_ref):
        i = pl.program_id(0)

        @pl.when(i == 0)
        def _():
            acc_ref[...] = jnp.zeros_like(acc_ref)

        e = y_ref[...] - t_ref[...]
        dy_ref[...] = e * (1.0 / d)
        acc_ref[...] += jnp.sum(e * e, axis=0, keepdims=True)

        @pl.when(i == n - 1)
        def _():
            loss_ref[...] = (0.5 / d) * jnp.sum(acc_ref[...], axis=1, keepdims=True)

    row = lambda i: (i, 0)
    return pl.pallas_call(
        body,
        name="loss_head",
        grid=(n,),
        in_specs=[pl.BlockSpec((tm, d), row), pl.BlockSpec((tm, d), row)],
        out_specs=[pl.BlockSpec((1, 1), lambda i: (0, 0)), pl.BlockSpec((tm, d), row)],
        out_shape=[jax.ShapeDtypeStruct((1, 1), F32), jax.ShapeDtypeStruct((T, d), F32)],
        scratch_shapes=[pltpu.VMEM((1, d), F32)],
        compiler_params=_cparams(("arbitrary",)),
    )(y, target)


def _lower_bounds_fwd(lower_bounds):
    depth, n = lower_bounds.shape

    def body(x_ref, soft_ref, lb_ref):
        x = x_ref[...]
        e = jnp.exp(x - jnp.max(x, axis=0, keepdims=True))
        soft_ref[...] = e / jnp.sum(e, axis=0, keepdims=True)
        run = None
        for l in range(depth):
            run = soft_ref[l:l + 1, :] if run is None else run + soft_ref[l:l + 1, :]
            lb_ref[l:l + 1, :] = run - soft_ref[0:1, :]

    return pl.pallas_call(body, name="lower_bounds_fwd",
                          out_shape=[jax.ShapeDtypeStruct((depth, n), F32), jax.ShapeDtypeStruct((depth, n), F32)])(lower_bounds)


def _lower_bounds_bwd(soft, dlb):
    depth, n = soft.shape

    def body(soft_ref, dlb_ref, out_ref, dsoft_ref):
        total = jnp.sum(dlb_ref[...], axis=0, keepdims=True)
        run = None
        for l in reversed(range(depth)):
            run = dlb_ref[l:l + 1, :] if run is None else run + dlb_ref[l:l + 1, :]
            dsoft_ref[l:l + 1, :] = run - total if l == 0 else run
        s, ds = soft_ref[...], dsoft_ref[...]
        out_ref[...] = s * (ds - jnp.sum(s * ds, axis=0, keepdims=True))

    return pl.pallas_call(body, name="lower_bounds_bwd", out_shape=jax.ShapeDtypeStruct((depth, n), F32),
                          scratch_shapes=[pltpu.VMEM((depth, n), F32)])(soft, dlb)


def _layer_fwd(x0, x0b, mem2, lb, w_in, rest_fn, *, bl, seq, alpha, deps=()):
    p = _matmul("proj_in", x0b, w_in, mode="nn", out_dtype=BF16, deps=deps, tm=1024, tn=1792)
    wts = dict(rest_fn(p), w_in=w_in)
    mk = _matmul("mem_k", mem2, wts["w_mem_k"], mode="nn", out_dtype=BF16)
    mv = _matmul("mem_v", mem2, wts["w_mem_v"], mode="nn", out_dtype=BF16)
    y, st, opre = _mixer_fwd(p, mk, mv, lb, wts["conv_w"], wts["hg_norm_w"], bl=bl, seq=seq)
    r, merged, xhat1, rstd1, x1b = _merge_fwd(y, p, x0, wts["w_branch"], wts["w_o"], wts["b_gate"], wts["ln1_g"], wts["ln1_b"],
                                              alpha=alpha)
    a, xhat2, rstd2, x2, x2b = _mlp_fwd(xhat1, x1b, wts["ln1_g"], wts["ln1_b"], wts["w_up"], wts["w_down"], wts["ln2_g"],
                                        wts["ln2_b"], alpha=alpha)
    saved = dict(x0b=x0b, p=p, mk=mk, mv=mv, y=y, st=st, opre=opre, r=r, merged=merged, xhat1=xhat1, rstd1=rstd1, x1b=x1b, a=a,
                 xhat2=xhat2, rstd2=rstd2)
    return x2, x2b, saved, wts


def _relu2_bf16(a):
    return jnp.square(jnp.maximum(a.astype(F32), 0.0)).astype(BF16)


def _mlp_bwd(dz2, dz2b, sv, wts, *, alpha, deps=()):
    g = {}
    da = _matmul("mlp_da", dz2b, wts["w_down"], mode="nt", out_dtype=BF16, tm=1024, deps=deps,
                 epi_fn=lambda acc, a: (acc * (2.0 * jnp.maximum(a.astype(F32), 0.0)),), epi_extra=(sv["a"],))
    g["w_down"] = _matmul_tn("grad_w_down", sv["a"], dz2b, a_fn=_relu2_bf16, out_dtype=BF16, tt=2048)
    g["w_up"] = _matmul_tn("grad_w_up", sv["x1b"], da, out_dtype=BF16, tt=2048)
    dx1 = _matmul("mlp_dx", da, wts["w_up"], mode="nt", epi_fn=lambda acc, dz: (acc + alpha * dz,), epi_extra=(dz2,),
                  tm=512, tk=4096)
    dz1, dz1b, dg1, db1 = _ln_bwd(dx1, sv["xhat1"], sv["rstd1"], wts["ln1_g"])
    g["ln1_g"], g["ln1_b"] = dg1[0:1], db1[0:1]
    return dz1, dz1b, g


def _mix_bwd(dz1, dz1b, sv, mem2, lb, wts, *, bl, seq, alpha, send, below=None, deps=()):
    d = dz1.shape[1]
    g = {}
    g["w_o"] = _matmul_tn("grad_w_o", sv["merged"], dz1b, out_dtype=BF16, tt=2048, deps=deps)
    dr, dp, dy, dbg = _merge_bwd(dz1b, sv["p"], sv["r"], wts["w_branch"], wts["w_o"], wts["b_gate"])
    g["b_gate"] = dbg[0:1]
    g["w_branch"] = jnp.concatenate(
        [_matmul_tn("grad_w_branch", sv["y"], dr, a_cols=(i * W, W), b_cols=(i * d, d), out_dtype=BF16) for i in range(N_BRANCH)],
        axis=0)
    token = send(("w_o", "w_branch"), g)
    dp, dmk, dmv, dcw, dnw, dlb = _mixer_bwd(sv["p"], dy, dp, sv["st"], sv["opre"], sv["mk"], sv["mv"], lb,
                                              wts["conv_w"], wts["hg_norm_w"], bl=bl, seq=seq, deps=(token,))
    g["conv_w"], g["hg_norm_w"], g["lb"] = dcw[0:CONV_K], dnw[0:1], dlb[0:1]
    g["w_mem_k"] = _matmul_tn("grad_w_mem_k", mem2, dmk, out_dtype=BF16)
    g["w_mem_v"] = _matmul_tn("grad_w_mem_v", mem2, dmv, out_dtype=BF16)
    g["w_in"] = _matmul_tn("grad_w_in", sv["x0b"], dp, out_dtype=BF16, tt=2048)
    token = send(("w_in", "w_mem_k", "w_mem_v", "conv_w"), g)
    dx0 = _matmul("proj_in_dx", dp, wts["w_in"], mode="nt", epi_fn=lambda acc, dz: (acc + alpha * dz,), epi_extra=(dz1,),
                  tm=512, tk=dp.shape[1], deps=(token,))
    return (dx0 if below is None else _ln_bwd(dx0, *below)), g


N_CHIPS = 4
MESH_IDS = pl.DeviceIdType.MESH


def _axis_slice(ref, axis, start, size):
    idx = [slice(None)] * len(ref.shape)
    idx[axis] = pl.ds(start, size)
    return ref.at[tuple(idx)]


def _chip_exchange(name, items):
    n = len(items)
    out_shapes, meta = [], []
    for arr, kind, axis in items:
        shp = list(arr.shape)
        if kind == "gather":
            per = shp[axis]
            shp[axis] = per * N_CHIPS
            out_shapes.append(jax.ShapeDtypeStruct(tuple(shp), arr.dtype))
        elif kind == "scatter":
            per = shp[axis] // N_CHIPS
            shp[axis] = per
            out_shapes.append(jax.ShapeDtypeStruct((N_CHIPS, *shp), arr.dtype))
        else:
            per = None
            out_shapes.append(jax.ShapeDtypeStruct((N_CHIPS, *shp), arr.dtype))
        meta.append((kind, axis, per))

    def body(*refs):
        ins, outs = refs[:n], refs[n:2 * n]
        send_sems, recv_sems, local_sems = refs[2 * n:]
        x, y, c = lax.axis_index("x"), lax.axis_index("y"), lax.axis_index("c")
        me = 2 * x + y
        peers = [(1 - x, y), (x, 1 - y), (1 - x, 1 - y)]

        def src_for(t, chip):
            kind, axis, per = meta[t]
            return _axis_slice(ins[t], axis, chip * per, per) if kind == "scatter" else ins[t]

        def dst_from(t, chip):
            kind, axis, per = meta[t]
            return _axis_slice(outs[t], axis, chip * per, per) if kind == "gather" else outs[t].at[chip]

        def remote(t, k):
            px, py = peers[k]
            return pltpu.make_async_remote_copy(
                src_ref=src_for(t, 2 * px + py), dst_ref=dst_from(t, me), send_sem=send_sems.at[t * 3 + k],
                recv_sem=recv_sems.at[t * 3 + k], device_id=(px, py, c), device_id_type=MESH_IDS)

        def arrival(t, k):
            px, py = peers[k]
            return pltpu.make_async_remote_copy(
                src_ref=src_for(t, me), dst_ref=dst_from(t, 2 * px + py), send_sem=send_sems.at[t * 3 + k],
                recv_sem=recv_sems.at[t * 3 + k], device_id=(px, py, c), device_id_type=MESH_IDS)

        sends = [remote(t, k) for t in range(n) for k in range(3)]
        for cp in sends:
            cp.start()
        own = [pltpu.make_async_copy(src_for(t, me), dst_from(t, me), local_sems.at[t]) for t in range(n)]
        for cp in own:
            cp.start()
        for t in range(n):
            for k in range(3):
                arrival(t, k).wait_recv()
        for cp in sends:
            cp.wait_send()
        for cp in own:
            cp.wait()

    any_spec = pl.BlockSpec(memory_space=pl.ANY)
    return pl.pallas_call(
        body,
        name=name,
        in_specs=[any_spec] * n,
        out_specs=[any_spec] * n,
        out_shape=out_shapes,
        scratch_shapes=[pltpu.SemaphoreType.DMA((3 * n,)), pltpu.SemaphoreType.DMA((3 * n,)), pltpu.SemaphoreType.DMA((n,))],
        compiler_params=pltpu.CompilerParams(has_side_effects=True),
    )(*[a for a, _, _ in items])


HBM_SPEC = pl.BlockSpec(memory_space=pltpu.HBM)
SEM_SPEC = pl.BlockSpec(memory_space=pltpu.SEMAPHORE)
N_PEERS = N_CHIPS - 1


def _my_chip():
    return (2 * lax.axis_index("x") + lax.axis_index("y")).astype(jnp.int32).reshape(1)


def _own_block_spec(r, c, axis, tr):
    if axis == 1:
        return pl.BlockSpec((tr, c), lambda i, me: (i, me[0]))
    return pl.BlockSpec((tr, c), lambda i, me: (me[0] * (r // tr) + i, 0))


def _place_shard(name, shard, axis, me):
    r, c = shard.shape
    tr = _row_block(r, c, shard.dtype.itemsize)
    shp = (r, c * N_CHIPS) if axis == 1 else (r * N_CHIPS, c)

    def body(me_ref, s_ref, o_ref):
        del me_ref
        o_ref[...] = s_ref[...]

    return pl.pallas_call(
        body, name=name,
        grid_spec=pltpu.PrefetchScalarGridSpec(
            num_scalar_prefetch=1, grid=(r // tr,),
            in_specs=[pl.BlockSpec((tr, c), lambda i, me: (i, 0))], out_specs=_own_block_spec(r, c, axis, tr)),
        out_shape=jax.ShapeDtypeStruct(shp, shard.dtype),
        compiler_params=_cparams(("parallel",)),
    )(me, shard)


class _Split:
    def __init__(self, name, items):
        self.name, self.n = name, len(items)
        self.srcs = [a for a, _, _ in items]
        self.meta, self.land_shapes = [], []
        for arr, kind, axis in items:
            shp = list(arr.shape)
            if kind == "gather":
                per = shp[axis]
                shp[axis] = per * N_CHIPS
                self.land_shapes.append(jax.ShapeDtypeStruct(tuple(shp), arr.dtype))
            else:
                per = shp[axis] // N_CHIPS
                shp[axis] = per
                self.land_shapes.append(jax.ShapeDtypeStruct((N_PEERS, *shp), arr.dtype))
            self.meta.append((kind, axis, per))

    def _src(self, ins, t, chip):
        kind, axis, per = self.meta[t]
        return _axis_slice(ins[t], axis, chip * per, per) if kind == "scatter" else ins[t]

    def _dst(self, lands, t, chip, slot):
        kind, axis, per = self.meta[t]
        return _axis_slice(lands[t], axis, chip * per, per) if kind == "gather" else lands[t].at[slot]

    def landing_zones(self, me):
        return [_place_shard(self.name + "_own", src, axis, me) if kind == "gather" else lax.empty(ls.shape, ls.dtype)
                for src, ls, (kind, axis, _) in zip(self.srcs, self.land_shapes, self.meta)]

    def _copies(self, ins, lands, send_sems, recv_sems, arrivals):
        x, y, c = lax.axis_index("x"), lax.axis_index("y"), lax.axis_index("c")
        me = 2 * x + y
        peers = [(1 - x, y), (x, 1 - y), (1 - x, 1 - y)]
        res = []
        for t in range(self.n):
            for k, (px, py) in enumerate(peers):
                theirs = 2 * px + py
                sems = dict(send_sem=send_sems.at[t * N_PEERS + k], recv_sem=recv_sems.at[t * N_PEERS + k],
                            device_id=(px, py, c), device_id_type=MESH_IDS)
                if arrivals:
                    res.append(pltpu.make_async_remote_copy(src_ref=self._src(ins, t, me), dst_ref=self._dst(lands, t, theirs, k), **sems))
                else:
                    res.append(pltpu.make_async_remote_copy(src_ref=self._src(ins, t, theirs), dst_ref=self._dst(lands, t, me, k), **sems))
        return res

    def start(self, lands, deps=()):
        n, nd = self.n, len(deps)

        def body(*refs):
            ins, lnd = refs[:n], refs[n:2 * n]
            send_sems, recv_sems = refs[2 * n + nd], refs[2 * n + nd + 1]
            token = refs[-1]
            for cp in self._copies(ins, lnd, send_sems, recv_sems, arrivals=False):
                cp.start()
            token[...] = jnp.zeros_like(token)

        hbm = lambda a: pltpu.HBM(a.shape, a.dtype)
        res = pl.pallas_call(
            body, name=self.name + "_start",
            in_specs=[HBM_SPEC] * (2 * n) + [ANY_SPEC] * nd,
            out_specs=[SEM_SPEC, SEM_SPEC] + [HBM_SPEC] * (2 * n) + [pl.BlockSpec(memory_space=pltpu.VMEM)],
            out_shape=[pltpu.SemaphoreType.DMA((N_PEERS * n,)), pltpu.SemaphoreType.DMA((N_PEERS * n,))]
            + [hbm(a) for a in self.srcs] + [hbm(a) for a in self.land_shapes] + [jax.ShapeDtypeStruct((8, 128), F32)],
            input_output_aliases={i: 2 + i for i in range(2 * n)},
            compiler_params=pltpu.CompilerParams(has_side_effects=pltpu.SideEffectType.DATAFLOW_SIDE_EFFECTING),
        )(*[pltpu.with_memory_space_constraint(a, pltpu.HBM) for a in self.srcs],
          *[pltpu.with_memory_space_constraint(a, pltpu.HBM) for a in lands], *deps)
        return res[:-1], res[-1]

    def wait(self, state, after):
        n = self.n
        send_sems, recv_sems = state[0], state[1]
        srcs, lands = state[2:2 + n], state[2 + n:2 + 2 * n]

        def body(*refs):
            ins, lnd = refs[:n], refs[n:2 * n]
            s_sems, r_sems = refs[2 * n], refs[2 * n + 1]
            for cp in self._copies(ins, lnd, s_sems, r_sems, arrivals=True):
                cp.wait_recv()
            for cp in self._copies(ins, lnd, s_sems, r_sems, arrivals=False):
                cp.wait_send()

        hbm = lambda a: pltpu.HBM(a.shape, a.dtype)
        res = pl.pallas_call(
            body, name=self.name + "_wait",
            in_specs=[HBM_SPEC] * (2 * n) + [SEM_SPEC, SEM_SPEC, ANY_SPEC],
            out_specs=[HBM_SPEC] * (2 * n),
            out_shape=[hbm(a) for a in self.srcs] + [hbm(a) for a in self.land_shapes],
            input_output_aliases={i: i for i in range(2 * n)},
            compiler_params=pltpu.CompilerParams(has_side_effects=pltpu.SideEffectType.DATAFLOW_SIDE_EFFECTING),
        )(*srcs, *lands, send_sems, recv_sems, after)
        return res[:n], res[n:]


def _sibling_swap(name, arrays):
    n = len(arrays)

    def body(*refs):
        ins, outs = refs[:n], refs[n:2 * n]
        send_sems, recv_sems = refs[2 * n:]
        sibling = (lax.axis_index("x"), lax.axis_index("y"), 1 - lax.axis_index("c"))
        copies = [pltpu.make_async_remote_copy(src_ref=ins[t], dst_ref=outs[t], send_sem=send_sems.at[t], recv_sem=recv_sems.at[t],
                                               device_id=sibling, device_id_type=MESH_IDS) for t in range(n)]
        for cp in copies:
            cp.start()
        for cp in copies:
            cp.wait()

    any_spec = pl.BlockSpec(memory_space=pl.ANY)
    return pl.pallas_call(
        body,
        name=name,
        in_specs=[any_spec] * n,
        out_specs=[any_spec] * n,
        out_shape=[jax.ShapeDtypeStruct(a.shape, a.dtype) for a in arrays],
        scratch_shapes=[pltpu.SemaphoreType.DMA((n,)), pltpu.SemaphoreType.DMA((n,))],
        compiler_params=pltpu.CompilerParams(has_side_effects=True),
    )(*arrays)


def _row_block(r, c, itemsize=4, target=1 << 20):
    if r % 8 != 0:
        return r
    best = 8
    for tr in range(8, r + 1, 8):
        if r % tr == 0 and tr * c * itemsize <= target:
            best = tr
    return best


def _sum_chips_into(parts, stacked, layer):
    _, r, c = parts.shape
    tr = _row_block(r, c)

    def body(p_ref, s_ref, o_ref):
        del s_ref
        o_ref[...] = ((p_ref[0] + p_ref[1]) + p_ref[2]) + p_ref[3]

    return pl.pallas_call(
        body,
        name="sum_chips",
        grid=(r // tr,),
        in_specs=[pl.BlockSpec((N_CHIPS, tr, c), lambda i: (0, i, 0)), pl.BlockSpec(memory_space=pl.ANY)],
        out_specs=pl.BlockSpec((None, tr, c), lambda i: (layer, i, 0)),
        out_shape=jax.ShapeDtypeStruct(stacked.shape, stacked.dtype),
        input_output_aliases={1: 0},
        compiler_params=_cparams(("parallel",)),
    )(parts, stacked)


def _sum_own_and_peers(me, g, axis, landed):
    _, r, c = landed.shape
    tr = _row_block(r, c)

    def body(me_ref, g_ref, p_ref, o_ref):
        del me_ref
        o_ref[...] = ((g_ref[...].astype(F32) + p_ref[0].astype(F32)) + p_ref[1].astype(F32)) + p_ref[2].astype(F32)

    return pl.pallas_call(
        body, name="sum_chips_own",
        grid_spec=pltpu.PrefetchScalarGridSpec(
            num_scalar_prefetch=1, grid=(r // tr,),
            in_specs=[_own_block_spec(r, c, axis, tr), pl.BlockSpec((N_PEERS, tr, c), lambda i, me: (0, i, 0))],
            out_specs=pl.BlockSpec((tr, c), lambda i, me: (i, 0))),
        out_shape=jax.ShapeDtypeStruct((r, c), F32),
        compiler_params=_cparams(("parallel",)),
    )(me, g, landed)


def _adamw_math(w, m, v, g):
    m_new = ADAM_B1 * m + (1.0 - ADAM_B1) * g
    v_new = ADAM_B2 * v + (1.0 - ADAM_B2) * jnp.square(g)
    m_hat = m_new / (1.0 - ADAM_B1 ** ADAM_STEP)
    v_hat = v_new / (1.0 - ADAM_B2 ** ADAM_STEP)
    return -ADAM_LR * (m_hat / (jnp.sqrt(v_hat) + ADAM_EPS) + ADAM_WD * w), m_new, v_new


def _adamw(w, m, v, g_a, g_b):
    L, r, c = w.shape
    tr = _row_block(r, c, target=1 << 19)

    def body(w_ref, m_ref, v_ref, ga_ref, gb_ref, g_ref, d_ref, nm_ref, nv_ref):
        g = ga_ref[...] + gb_ref[...]
        g_ref[...] = g
        d_ref[...], nm_ref[...], nv_ref[...] = _adamw_math(w_ref[...], m_ref[...], v_ref[...], g)

    spec = pl.BlockSpec((None, tr, c), lambda l, i: (l, i, 0))
    return pl.pallas_call(
        body,
        name="adamw",
        grid=(L, r // tr),
        in_specs=[spec] * 5,
        out_specs=[spec] * 4,
        out_shape=[jax.ShapeDtypeStruct(w.shape, F32)] * 4,
        compiler_params=_cparams(("parallel", "parallel")),
    )(w, m, v, g_a, g_b)


def _adamw_layer(w, m, v, g_a, g_b, layer, outs):
    L, r, c = w.shape
    tr = _row_block(r, c, target=1 << 19)
    n_prev = 0 if outs is None else 4

    def body(w_ref, m_ref, v_ref, ga_ref, gb_ref, *rest):
        g_ref, d_ref, nm_ref, nv_ref = rest[n_prev:]
        g = ga_ref[...] + gb_ref[...]
        g_ref[...] = g
        d_ref[...], nm_ref[...], nv_ref[...] = _adamw_math(w_ref[...], m_ref[...], v_ref[...], g)

    at_layer = pl.BlockSpec((None, tr, c), lambda i: (layer, i, 0))
    flat = pl.BlockSpec((tr, c), lambda i: (i, 0))
    return pl.pallas_call(
        body,
        name="adamw_layer",
        grid=(r // tr,),
        in_specs=[at_layer] * 3 + [flat] * 2 + [ANY_SPEC] * n_prev,
        out_specs=[at_layer] * 4,
        out_shape=[jax.ShapeDtypeStruct(w.shape, F32)] * 4,
        input_output_aliases={5 + k: k for k in range(n_prev)},
        compiler_params=_cparams(("parallel",)),
    )(w, m, v, g_a, g_b, *(outs or ()))


SHARDED = (("w_in", 1), ("conv_w", 1), ("w_mem_k", 0), ("w_mem_v", 0), ("w_branch", 1), ("w_o", 0), ("w_up", 1), ("w_down", 0))
SMALL = ("lower_bounds", "hg_norm_w", "b_gate", "ln1_g", "ln1_b", "ln2_g", "ln2_b")
WEIGHT_ORDER = ("lower_bounds", "w_in", "conv_w", "hg_norm_w", "w_mem_k", "w_mem_v", "w_branch", "b_gate", "w_o", "ln1_g", "ln1_b",
                "w_up", "w_down", "ln2_g", "ln2_b")


def kernel(x, mem, lower_bounds, w_in, conv_w, hg_norm_w, w_mem_k, w_mem_v, w_branch, b_gate, w_o, ln1_g, ln1_b, w_up, w_down, ln2_g, ln2_b, loss_target, m_lower_bounds, m_w_in, m_conv_w, m_hg_norm_w, m_w_mem_k, m_w_mem_v, m_w_branch, m_b_gate, m_w_o, m_ln1_g, m_ln1_b, m_w_up, m_w_down, m_ln2_g, m_ln2_b, v_lower_bounds, v_w_in, v_conv_w, v_hg_norm_w, v_w_mem_k, v_w_mem_v, v_w_branch, v_b_gate, v_w_o, v_ln1_g, v_ln1_b, v_w_up, v_w_down, v_ln2_g, v_ln2_b):
    bl, seq, d = x.shape
    depth = w_in.shape[0]
    weights = dict(lower_bounds=lower_bounds, w_in=w_in, conv_w=conv_w, hg_norm_w=hg_norm_w, w_mem_k=w_mem_k, w_mem_v=w_mem_v,
                   w_branch=w_branch, b_gate=b_gate, w_o=w_o, ln1_g=ln1_g, ln1_b=ln1_b, w_up=w_up, w_down=w_down, ln2_g=ln2_g, ln2_b=ln2_b)
    mom_m = dict(lower_bounds=m_lower_bounds, w_in=m_w_in, conv_w=m_conv_w, hg_norm_w=m_hg_norm_w, w_mem_k=m_w_mem_k, w_mem_v=m_w_mem_v,
                 w_branch=m_w_branch, b_gate=m_b_gate, w_o=m_w_o, ln1_g=m_ln1_g, ln1_b=m_ln1_b, w_up=m_w_up, w_down=m_w_down,
                 ln2_g=m_ln2_g, ln2_b=m_ln2_b)
    mom_v = dict(lower_bounds=v_lower_bounds, w_in=v_w_in, conv_w=v_conv_w, hg_norm_w=v_hg_norm_w, w_mem_k=v_w_mem_k, w_mem_v=v_w_mem_v,
                 w_branch=v_w_branch, b_gate=v_b_gate, w_o=v_w_o, ln1_g=v_ln1_g, ln1_b=v_ln1_b, w_up=v_w_up, w_down=v_w_down,
                 ln2_g=v_ln2_g, ln2_b=v_ln2_b)

    def shard2d(name, l):
        w = weights[name][l]
        if name == "w_branch":
            return w.reshape(N_BRANCH * W, w.shape[-1]).astype(BF16)
        return w if name == "conv_w" else w.astype(BF16)

    me = _my_chip()

    shard_axis = dict(SHARDED)

    def start_exchange(name, kind, items, deps=()):
        ex = _Split(name, [(arr, kind, shard_axis[nm]) for nm, arr in items])
        state, token = ex.start(ex.landing_zones(me), deps)
        return ex, state, [nm for nm, _ in items], token

    def start_gathers(l, deps=()):
        first = start_exchange(f"gather_in_l{l}", "gather", [("w_in", shard2d("w_in", l))], deps)
        rest = start_exchange(f"gather_rest_l{l}", "gather", [(nm, shard2d(nm, l)) for nm, _ in SHARDED if nm != "w_in"],
                              (first[3],))
        return first, rest

    def gathered(pend, after):
        ex, state, names, _ = pend
        return dict(zip(names, ex.wait(state, after=after)[1]))

    x2d, mem2, t2d = x.reshape(bl * seq, d), mem.reshape(-1, d), loss_target.reshape(bl * seq, d)
    alpha = (2.0 * depth) ** 0.25
    soft, lb_all = _lower_bounds_fwd(lower_bounds)

    h, hb, saved, layer_wts = x2d, x2d.astype(BF16), [], []
    pending = start_gathers(0)
    for l in range(depth):
        first, rest = pending
        w_in_l = gathered(first, h)["w_in"]

        def rest_fn(after, l=l, rest=rest):
            wts = gathered(rest, after)
            for name in ("hg_norm_w", "b_gate", "ln1_g", "ln1_b", "ln2_g", "ln2_b"):
                wts[name] = weights[name][l][None, :]
            return wts

        deps = (rest[3],)
        if l + 1 < depth:
            pending = start_gathers(l + 1, (w_in_l, rest[3]))
            deps += (pending[0][3], pending[1][3])
        h, hb, sv, wts = _layer_fwd(h, hb, mem2, lb_all[l:l + 1], w_in_l, rest_fn, bl=bl, seq=seq, alpha=alpha, deps=deps)
        saved.append(sv)
        layer_wts.append(wts)
    loss, dh = _loss_head(h, t2d)

    shape3 = {name: (depth, weights[name].size // (depth * weights[name].shape[-1]), weights[name].shape[-1]) for name, _ in SHARDED}
    partial = [dict() for _ in range(depth)]
    smalls = [None] * depth
    outs = {name: None for name, _ in SHARDED}

    def finish_reduce(pend, l, after):
        ex, state, names, _ = pend
        sent, got = ex.wait(state, after=after)
        for nm, g_full, landed in zip(names, sent, got):
            partial[l][nm] = _sum_own_and_peers(me, g_full, shard_axis[nm], landed)

    def optimizer_step(l):
        names = [name for name, _ in SHARDED]
        theirs = _sibling_swap(f"swap_partials_l{l}", [partial[l][nm] for nm in names])
        for nm, other in zip(names, theirs):
            outs[nm] = _adamw_layer(weights[nm].reshape(shape3[nm]), mom_m[nm].reshape(shape3[nm]), mom_v[nm].reshape(shape3[nm]),
                                    partial[l][nm], other, l, outs[nm])
        return tuple(outs[nm][0] for nm in names)

    pending_mix, deps = [], ()
    dz2, dz2b, dg2, db2 = _ln_bwd(dh, saved[-1]["xhat2"], saved[-1]["rstd2"], layer_wts[-1]["ln2_g"])
    for l in reversed(range(depth)):
        dz1, dz1b, g_mlp = _mlp_bwd(dz2, dz2b, saved[l], layer_wts[l], alpha=alpha, deps=deps)
        g_mlp["ln2_g"], g_mlp["ln2_b"] = dg2[0:1], db2[0:1]
        pending_mlp = start_exchange(f"reduce_mlp_l{l}", "scatter", [(nm, g_mlp[nm]) for nm in ("w_up", "w_down")])
        deps = (pending_mlp[3],)
        if pending_mix:
            for pend in pending_mix:
                finish_reduce(pend, l + 1, dz1)
            deps += optimizer_step(l + 1)
        pending_mix = []

        def send(names, g, l=l, pending_mix=pending_mix):
            pend = start_exchange(f"reduce_{names[0]}_l{l}", "scatter", [(nm, g[nm]) for nm in names])
            pending_mix.append(pend)
            return pend[3]

        below = (saved[l - 1]["xhat2"], saved[l - 1]["rstd2"], layer_wts[l - 1]["ln2_g"]) if l > 0 else None
        out, g = _mix_bwd(dz1, dz1b, saved[l], mem2, lb_all[l:l + 1], layer_wts[l], bl=bl, seq=seq, alpha=alpha, send=send,
                          below=below, deps=deps)
        if l > 0:
            dz2, dz2b, dg2, db2 = out
        else:
            dh = out
        finish_reduce(pending_mlp, l, out[0] if l > 0 else out)
        deps = ()
        g.update(g_mlp, lower_bounds=g["lb"])
        smalls[l] = jnp.concatenate([g[nm] for nm in SMALL], axis=1)
    small_parts = _chip_exchange("reduce_small", [(jnp.stack(smalls), "bcast", 0)])[0]
    small_sum = _sum_chips_into(small_parts.reshape(N_CHIPS, depth, -1), jnp.zeros((1, depth, small_parts.shape[-1]), F32), 0)
    small_sum = small_sum.reshape(depth, 1, -1)
    small_theirs = _sibling_swap("swap_small", [small_sum])[0]
    for pend in pending_mix:
        finish_reduce(pend, 0, small_theirs)
    optimizer_step(0)

    outs = {name: [r.reshape(weights[name].shape) for r in res] for name, res in outs.items()}
    off = 0
    for name in SMALL:
        n = weights[name].shape[1]
        mine, other = small_sum[:, :, off:off + n], small_theirs[:, :, off:off + n]
        off += n
        if name == "lower_bounds":
            mine = _lower_bounds_bwd(soft, mine[:, 0, :])[:, None, :]
            other = _lower_bounds_bwd(soft, other[:, 0, :])[:, None, :]
        shp = (depth, 1, n)
        res = _adamw(weights[name].reshape(shp), mom_m[name].reshape(shp), mom_v[name].reshape(shp), mine, other)
        outs[name] = [r.reshape(weights[name].shape) for r in res]
    assert off == small_sum.shape[-1]

    total_loss = lax.psum(loss[0, 0], ("x", "y", "c"))
    result = [total_loss, dh.reshape(bl, seq, d)]
    for k in range(4):
        result += [outs[name][k] for name in WEIGHT_ORDER]
    return tuple(result)
```

```python
import functools

import jax
import jax.numpy as jnp
from jax import lax
from jax.experimental import pallas as pl
from jax.experimental.pallas import tpu as pltpu

F32 = jnp.float32
BF16 = jnp.bfloat16

HG_HEADS = 4
HG_F = 128
HG_CHUNK = 32
MEM_HEADS = 4
MEM_HEAD_DIM = 128
BRANCH_WIDTH = 512
N_BRANCH = 3
CONV_K = 3
LN_EPS = 1e-5
RMS_EPS = 1e-6
ADAM_LR = 0.001
ADAM_B1 = 0.9
ADAM_B2 = 0.999
ADAM_EPS = 1e-08
ADAM_WD = 0.01
ADAM_STEP = 10

VMEM_LIMIT = 48 * 1024 * 1024


def _cparams(sem):
    return pltpu.CompilerParams(dimension_semantics=sem, vmem_limit_bytes=VMEM_LIMIT)


def _dot(a, b, dims):
    return lax.dot_general(a, b, (dims, ((), ())), preferred_element_type=F32)


NN = ((1,), (0,))
NT = ((1,), (1,))
TN = ((0,), (0,))


def _pick(n, pref):
    for t in pref:
        if n % t == 0:
            return t
    return n


ANY_SPEC = pl.BlockSpec(memory_space=pl.ANY)


def _matmul(name, a, b, *, mode, out_dtype=F32, a_fn=None, a_extra=(), epi_fn=None, epi_extra=(), n_out=1, out_kinds=None,
            tm=512, tn=1024, tk=1024, deps=()):
    M, K = a.shape
    N = b.shape[1] if mode == "nn" else b.shape[0]
    tm, tn, tk = _pick(M, (tm, 256, 128, 8)), _pick(N, (tn, 896, 512, 256, 128)), _pick(K, (tk, 512, 256, 128))
    nk = K // tk
    n_ax, n_ex = len(a_extra), len(epi_extra)
    n_in = 2 + n_ax + n_ex + len(deps)
    out_dtypes = out_dtype if isinstance(out_dtype, (tuple, list)) else (out_dtype,) * n_out
    out_kinds = out_kinds or ("tile",) * n_out

    def body(*refs):
        a_ref, b_ref = refs[0], refs[1]
        ax_refs = refs[2:2 + n_ax]
        ex_refs = refs[2 + n_ax:2 + n_ax + n_ex]
        o_refs = refs[n_in:n_in + n_out]
        at = a_ref[...]
        at = a_fn(at, *[r[...] for r in ax_refs]) if a_fn is not None else at.astype(BF16)
        part = _dot(at, b_ref[...].astype(BF16), NN if mode == "nn" else NT)

        def finish(acc):
            outs = epi_fn(acc, *[r[...] for r in ex_refs]) if epi_fn is not None else (acc,)
            for o_ref, o, kind in zip(o_refs, outs, out_kinds):
                if kind == "rowsum":
                    @pl.when(pl.program_id(1) == 0)
                    def _(o_ref=o_ref):
                        o_ref[...] = jnp.zeros_like(o_ref)

                    o_ref[0:1, :] += o
                else:
                    o_ref[...] = o.astype(o_ref.dtype)

        if nk == 1:
            finish(part)
            return
        acc_ref = refs[-1]
        k = pl.program_id(2)

        @pl.when(k == 0)
        def _():
            acc_ref[...] = part

        @pl.when(jnp.logical_and(k > 0, k < nk - 1))
        def _():
            acc_ref[...] += part

        @pl.when(k == nk - 1)
        def _():
            finish(acc_ref[...] + part)

    b_mode = dict(pipeline_mode=pl.Buffered(1)) if (nk == 1 and N == tn) else {}
    in_specs = [pl.BlockSpec((tm, tk), lambda j, i, k: (i, k)),
                pl.BlockSpec((tk, tn), lambda j, i, k: (k, j), **b_mode) if mode == "nn"
                else pl.BlockSpec((tn, tk), lambda j, i, k: (j, k), **b_mode)]
    in_specs += [pl.BlockSpec((1, tk), lambda j, i, k: (0, k)) for _ in a_extra]
    for e in epi_extra:
        if e.shape[0] == 1:
            in_specs.append(pl.BlockSpec((1, tn), lambda j, i, k: (0, j)))
        elif e.shape[1] == 1:
            in_specs.append(pl.BlockSpec((tm, 1), lambda j, i, k: (i, 0)))
        else:
            in_specs.append(pl.BlockSpec((tm, tn), lambda j, i, k: (i, j)))
    in_specs += [ANY_SPEC] * len(deps)
    out_specs, out_shapes = [], []
    for kind, dt in zip(out_kinds, out_dtypes):
        if kind == "col":
            out_specs.append(pl.BlockSpec((tm, 1), lambda j, i, k: (i, 0)))
            out_shapes.append(jax.ShapeDtypeStruct((M, 1), dt))
        elif kind == "rowsum":
            out_specs.append(pl.BlockSpec((8, tn), lambda j, i, k: (0, j)))
            out_shapes.append(jax.ShapeDtypeStruct((8, N), dt))
        else:
            out_specs.append(pl.BlockSpec((tm, tn), lambda j, i, k: (i, j)))
            out_shapes.append(jax.ShapeDtypeStruct((M, N), dt))
    out = pl.pallas_call(
        body,
        name=name,
        grid=(N // tn, M // tm, nk),
        in_specs=in_specs,
        out_specs=out_specs,
        out_shape=out_shapes,
        scratch_shapes=[pltpu.VMEM((tm, tn), F32)] if nk > 1 else [],
        compiler_params=_cparams(("arbitrary", "arbitrary", "arbitrary")),
    )(a, b, *a_extra, *epi_extra, *deps)
    return out[0] if n_out == 1 else out


def _matmul_tn(name, a, b, *, a_fn=None, a_extra=(), a_cols=None, b_cols=None, ta=1024, tb=1024, tt=1024, out_dtype=F32, deps=()):
    T = a.shape[0]
    a0, Ka = a_cols if a_cols is not None else (0, a.shape[1])
    b0, Nb = b_cols if b_cols is not None else (0, b.shape[1])
    ta, tb, tt = _pick(Ka, (ta, 512, 256, 128)), _pick(Nb, (tb, 896, 512, 256, 128)), _pick(T, (tt, 512, 256, 128))
    assert a0 % ta == 0 and b0 % tb == 0
    a0, b0 = a0 // ta, b0 // tb
    nt = T // tt
    n_ax = len(a_extra)

    def body(*refs):
        a_ref, b_ref = refs[0], refs[1]
        ax_refs = refs[2:2 + n_ax]
        o_ref = refs[2 + n_ax + len(deps)]
        acc_ref = refs[-1]
        t = pl.program_id(2)
        at = a_ref[...]
        at = a_fn(at, *[r[...] for r in ax_refs]) if a_fn is not None else at.astype(BF16)
        part = _dot(at, b_ref[...].astype(BF16), TN)

        @pl.when(t == 0)
        def _():
            acc_ref[...] = part

        @pl.when(jnp.logical_and(t > 0, t < nt - 1))
        def _():
            acc_ref[...] += part

        @pl.when(t == nt - 1)
        def _():
            o_ref[...] = (acc_ref[...] + part if nt > 1 else part).astype(o_ref.dtype)

    in_specs = [pl.BlockSpec((tt, ta), lambda i, j, t: (t, a0 + i)), pl.BlockSpec((tt, tb), lambda i, j, t: (t, b0 + j))]
    in_specs += [pl.BlockSpec((1, ta), lambda i, j, t: (0, a0 + i)) for _ in a_extra]
    in_specs += [ANY_SPEC] * len(deps)
    return pl.pallas_call(
        body,
        name=name,
        grid=(Ka // ta, Nb // tb, nt),
        in_specs=in_specs,
        out_specs=pl.BlockSpec((ta, tb), lambda i, j, t: (i, j)),
        out_shape=jax.ShapeDtypeStruct((Ka, Nb), out_dtype),
        scratch_shapes=[pltpu.VMEM((ta, tb), F32)],
        compiler_params=_cparams(("parallel", "parallel", "arbitrary")),
    )(a, b, *a_extra, *deps)


W = BRANCH_WIDTH
C_CB, C_CC, C_CH, C_HQ, C_HF, C_HI, C_HG, C_MQ, N_MIX = 0, W, 2 * W, 3 * W, 4 * W, 5 * W, 6 * W, 7 * W, 8 * W
TS_MIX = 256
PREV_ROWS = 16


def _sigmoid(x):
    return jax.nn.sigmoid(x)


def _chunk_pos(shape):
    return lax.broadcasted_iota(jnp.int32, shape, 0) & (HG_CHUNK - 1)


def _seg_cumsum(x, pos):
    sh = 1
    while sh < HG_CHUNK:
        x = x + jnp.where(pos >= sh, pltpu.roll(x, sh, 0), 0.0)
        sh *= 2
    return x


def _seg_rev_cumsum(x, pos):
    n = x.shape[0]
    sh = 1
    while sh < HG_CHUNK:
        x = x + jnp.where(pos < HG_CHUNK - sh, pltpu.roll(x, n - sh, 0), 0.0)
        sh *= 2
    return x


def _chunk_mask(ts):
    r = lax.broadcasted_iota(jnp.int32, (ts, ts), 0)
    c = lax.broadcasted_iota(jnp.int32, (ts, ts), 1)
    return jnp.logical_and((r // HG_CHUNK) == (c // HG_CHUNK), c <= r)


def _hgrn_gates(p_ref, lb):
    q = p_ref[:, C_HQ:C_HQ + W].astype(F32)
    fl = p_ref[:, C_HF:C_HF + W].astype(F32)
    sig = _sigmoid(fl)
    f = lb + (1.0 - lb) * sig
    logf = jnp.log(f)
    k = (1.0 - lb) * _sigmoid(-fl)
    sq = _sigmoid(q)
    qs = q * sq
    return q, sq, qs, sig, f, logf, k


def _hgrn_decays(logf, bc_sc, ts):
    pos = _chunk_pos(logf.shape)
    bc = _seg_cumsum(logf, pos)
    bc_sc[...] = bc
    nc = ts // HG_CHUNK
    bref = jnp.concatenate(
        [jnp.broadcast_to(bc_sc[n * HG_CHUNK + HG_CHUNK // 2 - 1:n * HG_CHUNK + HG_CHUNK // 2, :], (HG_CHUNK, W)) for n in range(nc)], axis=0)
    blast = jnp.concatenate(
        [jnp.broadcast_to(bc_sc[(n + 1) * HG_CHUNK - 1:(n + 1) * HG_CHUNK, :], (HG_CHUNK, W)) for n in range(nc)], axis=0)
    return pos, bc, bref, blast


def _conv_shift_down(u, carry_ref, row):
    n = carry_ref.shape[0]
    last, before = carry_ref[n - 1:n, :], carry_ref[n - 2:n - 1, :]
    u1 = jnp.where(row == 0, last, pltpu.roll(u, 1, 0))
    u2 = jnp.where(row == 0, before, jnp.where(row == 1, last, pltpu.roll(u, 2, 0)))
    return u1, u2


def _attn_probs(qh, kh):
    s = _dot(qh, kh, NT) * (MEM_HEAD_DIM ** -0.5)
    e = jnp.exp(s - jnp.max(s, axis=-1, keepdims=True))
    return e / jnp.sum(e, axis=-1, keepdims=True)


def _mixer_fwd(p, mk, mv, lb, conv_w, norm_w, *, bl, seq):
    T = p.shape[0]
    ts = TS_MIX
    ns = seq // ts
    nc = ts // HG_CHUNK
    ml = mk.shape[0] // bl

    def body(p_ref, mk_ref, mv_ref, lb_ref, cw_ref, nw_ref, y_ref, st_ref, opre_ref, state_sc, carry_sc, bc_sc):
        @pl.when(pl.program_id(1) == 0)
        def _():
            state_sc[...] = jnp.zeros_like(state_sc)
            carry_sc[...] = jnp.zeros_like(carry_sc)

        cb, cc, ch = (p_ref[:, c0:c0 + W].astype(F32) for c0 in (C_CB, C_CC, C_CH))
        u = cc * ch
        row = lax.broadcasted_iota(jnp.int32, (ts, W), 0)
        u1, u2 = _conv_shift_down(u, carry_sc, row)
        yconv = u2 * cw_ref[0:1, :] + u1 * cw_ref[1:2, :] + u * cw_ref[2:3, :]
        y_ref[:, 0:W] = (cb * yconv).astype(BF16)
        carry_sc[...] = u[ts - 8:ts, :]

        lbv = lb_ref[...]
        _, _, qs, _, _, logf, k = _hgrn_gates(p_ref, lbv)
        pos, bc, bref, blast = _hgrn_decays(logf, bc_sc, ts)
        a_all = (qs * jnp.exp(bc - bref)).astype(BF16)
        bk_all = (k * jnp.exp(bref - bc)).astype(BF16)
        qin_all = (qs * jnp.exp(bc)).astype(BF16)
        kout_all = (k * jnp.exp(blast - bc)).astype(BF16)
        v_all = p_ref[:, C_HI:C_HI + W].astype(BF16)
        mask = _chunk_mask(ts)
        heads = [slice(h * HG_F, (h + 1) * HG_F) for h in range(HG_HEADS)]
        st = [state_sc[h] for h in range(HG_HEADS)]
        o_inter = [[] for _ in range(HG_HEADS)]
        for n in range(nc):
            rows = slice(n * HG_CHUNK, (n + 1) * HG_CHUNK)
            for h, hs in enumerate(heads):
                st_ref[n, h] = st[h]
                o_inter[h].append(_dot(qin_all[rows, hs], st[h].astype(BF16), NT))
                kv = _dot(v_all[rows, hs], kout_all[rows, hs], TN)
                decay = jnp.exp(bc_sc[(n + 1) * HG_CHUNK - 1:(n + 1) * HG_CHUNK, hs])
                st[h] = st[h] * decay + kv
        for h in range(HG_HEADS):
            state_sc[h] = st[h]
        scores = [_dot(a_all[:, hs], bk_all[:, hs], NT) for hs in heads]
        scores = [jnp.where(mask, s, 0.0).astype(BF16) for s in scores]
        outs = [_dot(scores[h], v_all[:, hs], NN) + jnp.concatenate(o_inter[h], axis=0) for h, hs in enumerate(heads)]
        for h, hs in enumerate(heads):
            o = outs[h]
            opre_ref[:, hs] = o
            on = o * lax.rsqrt(jnp.mean(o * o, axis=-1, keepdims=True) + RMS_EPS) * nw_ref[...]
            g = p_ref[:, C_HG + h * HG_F:C_HG + (h + 1) * HG_F].astype(F32)
            y_ref[:, W + h * HG_F:W + (h + 1) * HG_F] = (on * (g * _sigmoid(g))).astype(BF16)

        mheads = [slice(h * MEM_HEAD_DIM, (h + 1) * MEM_HEAD_DIM) for h in range(MEM_HEADS)]
        probs = [_attn_probs(p_ref[:, C_MQ + h * MEM_HEAD_DIM:C_MQ + (h + 1) * MEM_HEAD_DIM].astype(BF16), mk_ref[:, hs])
                 for h, hs in enumerate(mheads)]
        for h, hs in enumerate(mheads):
            y_ref[:, 2 * W + h * MEM_HEAD_DIM:2 * W + (h + 1) * MEM_HEAD_DIM] = _dot(
                probs[h].astype(BF16), mv_ref[:, hs], NN).astype(BF16)

    return pl.pallas_call(
        body,
        name="mixer_fwd",
        grid=(bl, ns),
        in_specs=[
            pl.BlockSpec((ts, N_MIX), lambda b, s: (b * ns + s, 0)),
            pl.BlockSpec((ml, W), lambda b, s: (b, 0)),
            pl.BlockSpec((ml, W), lambda b, s: (b, 0)),
            pl.BlockSpec((1, W), lambda b, s: (0, 0)),
            pl.BlockSpec((CONV_K, W), lambda b, s: (0, 0)),
            pl.BlockSpec((1, HG_F), lambda b, s: (0, 0)),
        ],
        out_specs=[
            pl.BlockSpec((ts, 3 * W), lambda b, s: (b * ns + s, 0)),
            pl.BlockSpec((nc, HG_HEADS, HG_F, HG_F), lambda b, s: (b * ns + s, 0, 0, 0)),
            pl.BlockSpec((ts, W), lambda b, s: (b * ns + s, 0)),
        ],
        out_shape=[
            jax.ShapeDtypeStruct((T, 3 * W), BF16),
            jax.ShapeDtypeStruct((T // HG_CHUNK, HG_HEADS, HG_F, HG_F), F32),
            jax.ShapeDtypeStruct((T, W), F32),
        ],
        scratch_shapes=[pltpu.VMEM((HG_HEADS, HG_F, HG_F), F32), pltpu.VMEM((8, W), F32), pltpu.VMEM((ts, W), F32)],
        compiler_params=_cparams(("arbitrary", "arbitrary")),
    )(p, mk, mv, lb, conv_w, norm_w)


def _mixer_bwd(p, dy, dp_gates, st, opre, mk, mv, lb, conv_w, norm_w, *, bl, seq, deps=()):
    T, nin = p.shape
    ts = TS_MIX
    ns = seq // ts
    nc = ts // HG_CHUNK
    ml = mk.shape[0] // bl
    mid, last = HG_CHUNK // 2 - 1, HG_CHUNK - 1

    def body(p_ref, pprev_ref, dy_ref, dpin_ref, st_ref, opre_ref, mk_ref, mv_ref, lb_ref, cw_ref, nw_ref, *rest):
        (dp_ref, dmk_ref, dmv_ref, dcw_ref, dnw_ref, dlb_ref, dstate_sc, carry_sc, uprev_sc, ab_sc, bkb_sc, qinb_sc, koutb_sc,
         dob_sc, dv_sc, da_sc, dbk_sc, dqin_sc, dkout_sc, dec_sc, ddec_sc) = rest[len(deps):]
        del dpin_ref
        b, s = pl.program_id(0), pl.program_id(1)

        @pl.when(s == 0)
        def _():
            dstate_sc[...] = jnp.zeros_like(dstate_sc)
            carry_sc[...] = jnp.zeros_like(carry_sc)
            dmk_ref[...] = jnp.zeros_like(dmk_ref)
            dmv_ref[...] = jnp.zeros_like(dmv_ref)

        @pl.when(jnp.logical_and(b == 0, s == 0))
        def _():
            dcw_ref[...] = jnp.zeros_like(dcw_ref)
            dnw_ref[...] = jnp.zeros_like(dnw_ref)
            dlb_ref[...] = jnp.zeros_like(dlb_ref)

        cb, cc, ch = (p_ref[:, c0:c0 + W].astype(F32) for c0 in (C_CB, C_CC, C_CH))
        u = cc * ch
        row = lax.broadcasted_iota(jnp.int32, (ts, W), 0)
        uprev = pprev_ref[:, C_CC:C_CC + W].astype(F32) * pprev_ref[:, C_CH:C_CH + W].astype(F32)
        uprev_sc[...] = jnp.where(s == ns - 1, 0.0, uprev)
        u1, u2 = _conv_shift_down(u, uprev_sc, row)
        w0, w1, w2 = cw_ref[0:1, :], cw_ref[1:2, :], cw_ref[2:3, :]
        dya = dy_ref[:, 0:W].astype(F32)
        dp_ref[:, C_CB:C_CB + W] = (dya * (u2 * w0 + u1 * w1 + u * w2)).astype(BF16)
        dv = cb * dya
        dv1 = jnp.where(row == ts - 1, carry_sc[0:1, :], pltpu.roll(dv, ts - 1, 0))
        dv2 = jnp.where(row == ts - 1, carry_sc[1:2, :], jnp.where(row == ts - 2, carry_sc[0:1, :], pltpu.roll(dv, ts - 2, 0)))
        du = dv * w2 + dv1 * w1 + dv2 * w0
        dp_ref[:, C_CC:C_CC + W] = (du * ch).astype(BF16)
        dp_ref[:, C_CH:C_CH + W] = (du * cc).astype(BF16)
        dcw_ref[0:1, :] += jnp.sum(dv * u2, axis=0, keepdims=True)
        dcw_ref[1:2, :] += jnp.sum(dv * u1, axis=0, keepdims=True)
        dcw_ref[2:3, :] += jnp.sum(dv * u, axis=0, keepdims=True)
        carry_sc[...] = dv[0:8, :]

        mask = _chunk_mask(ts)
        pos_c = _chunk_pos((HG_CHUNK, HG_F))
        nw = nw_ref[...]

        def block(n, h):
            rows = slice(n * HG_CHUNK, (n + 1) * HG_CHUNK)
            return rows, slice(h * HG_F, (h + 1) * HG_F)

        def gates(rows, h):
            lbh = lb_ref[:, h * HG_F:(h + 1) * HG_F]
            q = p_ref[rows, C_HQ + h * HG_F:C_HQ + (h + 1) * HG_F].astype(F32)
            fl = p_ref[rows, C_HF + h * HG_F:C_HF + (h + 1) * HG_F].astype(F32)
            sig = _sigmoid(fl)
            f = lbh + (1.0 - lbh) * sig
            k = (1.0 - lbh) * _sigmoid(-fl)
            sq = _sigmoid(q)
            qs = q * sq
            bc = _seg_cumsum(jnp.log(f), pos_c)
            bref = jnp.sum(jnp.where(pos_c == mid, bc, 0.0), axis=0, keepdims=True)
            blast = jnp.sum(jnp.where(pos_c == last, bc, 0.0), axis=0, keepdims=True)
            ea, eb, eq, ek = jnp.exp(bc - bref), jnp.exp(bref - bc), jnp.exp(bc), jnp.exp(blast - bc)
            return dict(lbh=lbh, q=q, sq=sq, sig=sig, f=f, blast=blast, ea=ea, eb=eb, eq=eq, ek=ek,
                        a=qs * ea, bk=k * eb, qin=qs * eq, kout=k * ek)

        dnw = jnp.zeros((1, HG_F), F32)
        for n in range(nc):
            for h in range(HG_HEADS):
                rows, hs = block(n, h)
                fw = gates(rows, h)
                ab_sc[rows, hs] = fw["a"].astype(BF16)
                bkb_sc[rows, hs] = fw["bk"].astype(BF16)
                qinb_sc[rows, hs] = fw["qin"].astype(BF16)
                koutb_sc[rows, hs] = fw["kout"].astype(BF16)
                dec_sc[n:n + 1, hs] = jnp.exp(fw["blast"])
                o = opre_ref[rows, hs]
                g = p_ref[rows, C_HG + h * HG_F:C_HG + (h + 1) * HG_F].astype(F32)
                sg = _sigmoid(g)
                r = lax.rsqrt(jnp.mean(o * o, axis=-1, keepdims=True) + RMS_EPS)
                dyb = dy_ref[rows, W + h * HG_F:W + (h + 1) * HG_F].astype(F32)
                dp_ref[rows, C_HG + h * HG_F:C_HG + (h + 1) * HG_F] = (
                    dyb * (o * r * nw) * (sg * (1.0 + g * (1.0 - sg)))).astype(BF16)
                don = dyb * (g * sg)
                dnw = dnw + jnp.sum(don * o * r, axis=0, keepdims=True)
                dn = don * nw
                dob_sc[rows, hs] = (r * (dn - o * (r * r) * jnp.mean(dn * o, axis=-1, keepdims=True))).astype(BF16)
        dnw_ref[0:1, :] += dnw

        heads = [slice(h * HG_F, (h + 1) * HG_F) for h in range(HG_HEADS)]
        scores = [_dot(ab_sc[:, hs], bkb_sc[:, hs], NT) for hs in heads]
        dscores = [_dot(dob_sc[:, hs], p_ref[:, C_HI + h * HG_F:C_HI + (h + 1) * HG_F].astype(BF16), NT)
                   for h, hs in enumerate(heads)]
        scores = [jnp.where(mask, s, 0.0).astype(BF16) for s in scores]
        dscores = [jnp.where(mask, s, 0.0).astype(BF16) for s in dscores]
        for h, hs in enumerate(heads):
            dv_sc[:, hs] = _dot(scores[h], dob_sc[:, hs], TN)
            da_sc[:, hs] = _dot(dscores[h], bkb_sc[:, hs], NN)
            dbk_sc[:, hs] = _dot(dscores[h], ab_sc[:, hs], TN)
        dst = [dstate_sc[h] for h in range(HG_HEADS)]
        for n in reversed(range(nc)):
            for h in range(HG_HEADS):
                rows, hs = block(n, h)
                st_n = st_ref[n, h]
                decay = dec_sc[n:n + 1, hs]
                dstb = dst[h].astype(BF16)
                dob_n = dob_sc[rows, hs]
                dv_sc[rows, hs] += _dot(koutb_sc[rows, hs], dstb, NT)
                dkout_sc[rows, hs] = _dot(p_ref[rows, C_HI + h * HG_F:C_HI + (h + 1) * HG_F].astype(BF16), dstb, NN)
                ddec_sc[n:n + 1, hs] = jnp.sum(dst[h] * st_n, axis=0, keepdims=True) * decay
                dqin_sc[rows, hs] = _dot(dob_n, st_n.astype(BF16), NN)
                dst[h] = dst[h] * decay + _dot(dob_n, qinb_sc[rows, hs], TN)
        for h in range(HG_HEADS):
            dstate_sc[h] = dst[h]

        for h in range(HG_HEADS):
            dlb = jnp.zeros((1, HG_F), F32)
            for n in range(nc):
                rows, hs = block(n, h)
                fw = gates(rows, h)
                da, dbk, dqin, dkout = da_sc[rows, hs], dbk_sc[rows, hs], dqin_sc[rows, hs], dkout_sc[rows, hs]
                dqs = da * fw["ea"] + dqin * fw["eq"]
                dk = dbk * fw["eb"] + dkout * fw["ek"]
                t_a, t_b, t_q, t_k = da * fw["a"], dbk * fw["bk"], dqin * fw["qin"], dkout * fw["kout"]
                s_ref = jnp.sum(t_b - t_a, axis=0, keepdims=True)
                s_last = jnp.sum(t_k, axis=0, keepdims=True) + ddec_sc[n:n + 1, hs]
                dbc = (t_a - t_b + t_q - t_k) + jnp.where(pos_c == mid, s_ref, 0.0) + jnp.where(pos_c == last, s_last, 0.0)
                dfk = _seg_rev_cumsum(dbc, pos_c) / fw["f"] - dk
                sig, q, sq = fw["sig"], fw["q"], fw["sq"]
                dp_ref[rows, C_HF + h * HG_F:C_HF + (h + 1) * HG_F] = (dfk * (1.0 - fw["lbh"]) * sig * (1.0 - sig)).astype(BF16)
                dlb = dlb + jnp.sum(dfk * (1.0 - sig), axis=0, keepdims=True)
                dp_ref[rows, C_HQ + h * HG_F:C_HQ + (h + 1) * HG_F] = (dqs * (sq * (1.0 + q * (1.0 - sq)))).astype(BF16)
                dp_ref[rows, C_HI + h * HG_F:C_HI + (h + 1) * HG_F] = dv_sc[rows, hs].astype(BF16)
            dlb_ref[0:1, h * HG_F:(h + 1) * HG_F] += dlb

        mheads = [slice(h * MEM_HEAD_DIM, (h + 1) * MEM_HEAD_DIM) for h in range(MEM_HEADS)]
        qhs = [p_ref[:, C_MQ + h * MEM_HEAD_DIM:C_MQ + (h + 1) * MEM_HEAD_DIM].astype(BF16) for h in range(MEM_HEADS)]
        dobs = [dy_ref[:, 2 * W + h * MEM_HEAD_DIM:2 * W + (h + 1) * MEM_HEAD_DIM].astype(BF16) for h in range(MEM_HEADS)]
        probs = [_attn_probs(qhs[h], mk_ref[:, hs]) for h, hs in enumerate(mheads)]
        dprobs = [_dot(dobs[h], mv_ref[:, hs], NT) for h, hs in enumerate(mheads)]
        for h, hs in enumerate(mheads):
            prob = probs[h]
            dmv_ref[:, hs] += _dot(prob.astype(BF16), dobs[h], TN)
            ds = prob * (dprobs[h] - jnp.sum(dprobs[h] * prob, axis=-1, keepdims=True)) * (MEM_HEAD_DIM ** -0.5)
            dsb = ds.astype(BF16)
            dp_ref[:, C_MQ + h * MEM_HEAD_DIM:C_MQ + (h + 1) * MEM_HEAD_DIM] = _dot(dsb, mk_ref[:, hs], NN).astype(BF16)
            dmk_ref[:, hs] += _dot(dsb, qhs[h], TN)

    def tile(b, s):
        return b * ns + (ns - 1 - s)

    return pl.pallas_call(
        body,
        name="mixer_bwd",
        grid=(bl, ns),
        in_specs=[
            pl.BlockSpec((ts, N_MIX), lambda b, s: (tile(b, s), 0)),
            pl.BlockSpec((PREV_ROWS, N_MIX), lambda b, s: (jnp.maximum(tile(b, s) * (ts // PREV_ROWS) - 1, 0), 0)),
            pl.BlockSpec((ts, 3 * W), lambda b, s: (tile(b, s), 0)),
            pl.BlockSpec(memory_space=pl.ANY),
            pl.BlockSpec((nc, HG_HEADS, HG_F, HG_F), lambda b, s: (tile(b, s), 0, 0, 0)),
            pl.BlockSpec((ts, W), lambda b, s: (tile(b, s), 0)),
            pl.BlockSpec((ml, W), lambda b, s: (b, 0)),
            pl.BlockSpec((ml, W), lambda b, s: (b, 0)),
            pl.BlockSpec((1, W), lambda b, s: (0, 0)),
            pl.BlockSpec((CONV_K, W), lambda b, s: (0, 0)),
            pl.BlockSpec((1, HG_F), lambda b, s: (0, 0)),
        ] + [ANY_SPEC] * len(deps),
        out_specs=[
            pl.BlockSpec((ts, N_MIX), lambda b, s: (tile(b, s), 0)),
            pl.BlockSpec((ml, W), lambda b, s: (b, 0)),
            pl.BlockSpec((ml, W), lambda b, s: (b, 0)),
            pl.BlockSpec((8, W), lambda b, s: (0, 0)),
            pl.BlockSpec((8, HG_F), lambda b, s: (0, 0)),
            pl.BlockSpec((8, W), lambda b, s: (0, 0)),
        ],
        out_shape=[
            jax.ShapeDtypeStruct((T, nin), BF16),
            jax.ShapeDtypeStruct((bl * ml, W), F32),
            jax.ShapeDtypeStruct((bl * ml, W), F32),
            jax.ShapeDtypeStruct((8, W), F32),
            jax.ShapeDtypeStruct((8, HG_F), F32),
            jax.ShapeDtypeStruct((8, W), F32),
        ],
        input_output_aliases={3: 0},
        scratch_shapes=[pltpu.VMEM((HG_HEADS, HG_F, HG_F), F32), pltpu.VMEM((8, W), F32), pltpu.VMEM((PREV_ROWS, W), F32)]
        + [pltpu.VMEM((ts, W), BF16)] * 5 + [pltpu.VMEM((ts, W), F32)] * 5 + [pltpu.VMEM((nc, W), F32)] * 2,
        compiler_params=_cparams(("arbitrary", "arbitrary")),
    )(p, p, dy, dp_gates, st, opre, mk, mv, lb, conv_w, norm_w, *deps)


def _layer_norm_stats(z):
    mu = jnp.mean(z, axis=-1, keepdims=True)
    zc = z - mu
    rstd = lax.rsqrt(jnp.mean(zc * zc, axis=-1, keepdims=True) + LN_EPS)
    return zc * rstd, rstd


def _gate_specs(tm, d):
    g0 = N_MIX // d
    return [pl.BlockSpec((tm, d), functools.partial(lambda i, k: (i, g0 + k), k=k)) for k in range(N_BRANCH)]


def _merge_fwd(y, p, x0, wb, wo, bg, ln_g, ln_b, *, alpha, tm=256):
    T, d = x0.shape
    assert N_MIX % d == 0
    tm = _pick(T, (tm, 128, 8))

    def body(y_ref, g0_ref, g1_ref, g2_ref, x_ref, wb_ref, wo_ref, bg_ref, lg_ref, lb_ref, mg_ref, xh_ref, rs_ref, x1b_ref):
        merged = None
        for i, g_ref in enumerate((g0_ref, g1_ref, g2_ref)):
            r = _dot(y_ref[:, i * W:(i + 1) * W], wb_ref[i * W:(i + 1) * W, :], NN)
            t = _sigmoid(g_ref[...].astype(F32) + bg_ref[:, i * d:(i + 1) * d]) * r
            merged = t if merged is None else merged + t
        mb = merged.astype(BF16)
        mg_ref[...] = mb
        z = alpha * x_ref[...] + _dot(mb, wo_ref[...], NN)
        xh, rs = _layer_norm_stats(z)
        xh_ref[...], rs_ref[...] = xh, rs
        x1b_ref[...] = (xh * lg_ref[...] + lb_ref[...]).astype(BF16)

    row = lambda i: (i, 0)
    fix = lambda i: (0, 0)
    return pl.pallas_call(
        body,
        name="merge_fwd",
        grid=(T // tm,),
        in_specs=[pl.BlockSpec((tm, 3 * W), row)] + _gate_specs(tm, d) + [
            pl.BlockSpec((tm, d), row), pl.BlockSpec((3 * W, d), fix), pl.BlockSpec((d, d), fix), pl.BlockSpec((1, 3 * d), fix),
            pl.BlockSpec((1, d), fix), pl.BlockSpec((1, d), fix)],
        out_specs=[pl.BlockSpec((tm, d), row), pl.BlockSpec((tm, d), row), pl.BlockSpec((tm, 1), row), pl.BlockSpec((tm, d), row)],
        out_shape=[jax.ShapeDtypeStruct((T, d), BF16), jax.ShapeDtypeStruct((T, d), F32), jax.ShapeDtypeStruct((T, 1), F32),
                   jax.ShapeDtypeStruct((T, d), BF16)],
        compiler_params=_cparams(("parallel",)),
    )(y, p, p, p, x0, wb, wo, bg, ln_g, ln_b)


def _merge_bwd(dz, p, y, wb, wo, bg, *, tm=256):
    T, d = dz.shape
    nin = p.shape[1]
    tm = _pick(T, (tm, 128, 8))

    def body(dz_ref, g0_ref, g1_ref, g2_ref, y_ref, wb_ref, wo_ref, bg_ref, dr_ref, dp_ref, dy_ref, dbg_ref):
        @pl.when(pl.program_id(0) == 0)
        def _():
            dbg_ref[...] = jnp.zeros_like(dbg_ref)

        dmerged = _dot(dz_ref[...].astype(BF16), wo_ref[...], NT)
        dp_ref[:, 0:N_MIX] = jnp.zeros((tm, N_MIX), BF16)
        for i, g_ref in enumerate((g0_ref, g1_ref, g2_ref)):
            cs = slice(i * d, (i + 1) * d)
            s = _sigmoid(g_ref[...].astype(F32) + bg_ref[:, cs])
            drb = (dmerged * s).astype(BF16)
            dr_ref[:, cs] = drb
            dgate = dmerged * _dot(y_ref[:, i * W:(i + 1) * W], wb_ref[i * W:(i + 1) * W, :], NN) * s * (1.0 - s)
            dp_ref[:, N_MIX + i * d:N_MIX + (i + 1) * d] = dgate.astype(BF16)
            dbg_ref[0:1, cs] += jnp.sum(dgate, axis=0, keepdims=True)
            dy_ref[:, i * W:(i + 1) * W] = _dot(drb, wb_ref[i * W:(i + 1) * W, :], NT).astype(BF16)

    row = lambda i: (i, 0)
    fix = lambda i: (0, 0)
    return pl.pallas_call(
        body,
        name="merge_bwd",
        grid=(T // tm,),
        in_specs=[pl.BlockSpec((tm, d), row)] + _gate_specs(tm, d) + [
            pl.BlockSpec((tm, 3 * W), row), pl.BlockSpec((3 * W, d), fix), pl.BlockSpec((d, d), fix), pl.BlockSpec((1, 3 * d), fix)],
        out_specs=[pl.BlockSpec((tm, 3 * d), row), pl.BlockSpec((tm, nin), row), pl.BlockSpec((tm, 3 * W), row),
                   pl.BlockSpec((8, 3 * d), fix)],
        out_shape=[jax.ShapeDtypeStruct((T, 3 * d), BF16), jax.ShapeDtypeStruct((T, nin), BF16),
                   jax.ShapeDtypeStruct((T, 3 * W), BF16), jax.ShapeDtypeStruct((8, 3 * d), F32)],
        compiler_params=_cparams(("arbitrary",)),
    )(dz, p, p, p, y, wb, wo, bg)


def _mlp_fwd(xhat1, x1b, g1, b1, wu, wd, g2, b2, *, alpha, tm=512, tf=2048):
    T, d = xhat1.shape
    ff = wu.shape[1]
    tm, tf = _pick(T, (tm, 256, 128, 8)), _pick(ff, (tf, 1024, 512, 256, 128))
    nf = ff // tf

    def body(xh_ref, x1b_ref, g1_ref, b1_ref, wu_ref, wd_ref, g2_ref, b2_ref, a_ref, xh2_ref, rs2_ref, x2_ref, x2b_ref, acc_ref):
        f = pl.program_id(1)
        a = _dot(x1b_ref[...], wu_ref[...], NN)
        a_ref[...] = a.astype(BF16)
        h = jnp.square(jnp.maximum(a, 0.0))
        part = _dot(h.astype(BF16), wd_ref[...], NN)

        @pl.when(f == 0)
        def _():
            acc_ref[...] = part

        @pl.when(jnp.logical_and(f > 0, f < nf - 1))
        def _():
            acc_ref[...] += part

        @pl.when(f == nf - 1)
        def _():
            x1 = xh_ref[...] * g1_ref[...] + b1_ref[...]
            xh2, rs2 = _layer_norm_stats(alpha * x1 + (acc_ref[...] + part if nf > 1 else part))
            xh2_ref[...] = xh2
            rs2_ref[...] = rs2
            x2 = xh2 * g2_ref[...] + b2_ref[...]
            x2_ref[...] = x2
            x2b_ref[...] = x2.astype(BF16)

    row = lambda i, f: (i, 0)
    fix = lambda i, f: (0, 0)
    return pl.pallas_call(
        body,
        name="mlp_fwd",
        grid=(T // tm, nf),
        in_specs=[pl.BlockSpec((tm, d), row), pl.BlockSpec((tm, d), row), pl.BlockSpec((1, d), fix), pl.BlockSpec((1, d), fix),
                  pl.BlockSpec((d, tf), lambda i, f: (0, f)), pl.BlockSpec((tf, d), lambda i, f: (f, 0)),
                  pl.BlockSpec((1, d), fix), pl.BlockSpec((1, d), fix)],
        out_specs=[pl.BlockSpec((tm, tf), lambda i, f: (i, f)), pl.BlockSpec((tm, d), row), pl.BlockSpec((tm, 1), row),
                   pl.BlockSpec((tm, d), row), pl.BlockSpec((tm, d), row)],
        out_shape=[jax.ShapeDtypeStruct((T, ff), BF16), jax.ShapeDtypeStruct((T, d), F32), jax.ShapeDtypeStruct((T, 1), F32),
                   jax.ShapeDtypeStruct((T, d), F32), jax.ShapeDtypeStruct((T, d), BF16)],
        scratch_shapes=[pltpu.VMEM((tm, d), F32)],
        compiler_params=_cparams(("parallel", "arbitrary")),
    )(xhat1, x1b, g1, b1, wu, wd, g2, b2)


def _ln_bwd(dy, xhat, rstd, g, *, tm=512, deps=()):
    T, d = dy.shape
    tm = _pick(T, (tm, 256, 128, 8))

    def body(dy_ref, xh_ref, rs_ref, g_ref, *rest):
        dz_ref, dzb_ref, dg_ref, db_ref = rest[len(deps):]

        @pl.when(pl.program_id(0) == 0)
        def _():
            dg_ref[...] = jnp.zeros_like(dg_ref)
            db_ref[...] = jnp.zeros_like(db_ref)

        dy_, xh = dy_ref[...], xh_ref[...]
        dg_ref[0:1, :] += jnp.sum(dy_ * xh, axis=0, keepdims=True)
        db_ref[0:1, :] += jnp.sum(dy_, axis=0, keepdims=True)
        dxh = dy_ * g_ref[...]
        dz = rs_ref[...] * (dxh - jnp.mean(dxh, axis=-1, keepdims=True) - xh * jnp.mean(dxh * xh, axis=-1, keepdims=True))
        dz_ref[...] = dz
        dzb_ref[...] = dz.astype(BF16)

    row = lambda i: (i, 0)
    fix = lambda i: (0, 0)
    return pl.pallas_call(
        body,
        name="ln_bwd",
        grid=(T // tm,),
        in_specs=[pl.BlockSpec((tm, d), row), pl.BlockSpec((tm, d), row), pl.BlockSpec((tm, 1), row), pl.BlockSpec((1, d), fix)]
        + [ANY_SPEC] * len(deps),
        out_specs=[pl.BlockSpec((tm, d), row), pl.BlockSpec((tm, d), row), pl.BlockSpec((8, d), fix), pl.BlockSpec((8, d), fix)],
        out_shape=[jax.ShapeDtypeStruct((T, d), F32), jax.ShapeDtypeStruct((T, d), BF16), jax.ShapeDtypeStruct((8, d), F32),
                   jax.ShapeDtypeStruct((8, d), F32)],
        compiler_params=_cparams(("arbitrary",)),
    )(dy, xhat, rstd, g, *deps)


def _loss_head(y, target, *, tm=512):
    T, d = y.shape
    tm = _pick(T, (tm, 256, 128, 8))
    n = T // tm

    def body(y_ref, t_ref, loss_ref, dy_ref, acc_ref):
        i = pl.program_id(0)

        @pl.when(i == 0)
        def _():
            acc_ref[...] = jnp.zeros_like(acc_ref)

        e = y_ref[...] - t_ref[...]
        dy_ref[...] = e * (1.0 / d)
        acc_ref[...] += jnp.sum(e * e, axis=0, keepdims=True)

        @pl.when(i == n - 1)
        def _():
            loss_ref[...] = (0.5 / d) * jnp.sum(acc_ref[...], axis=1, keepdims=True)

    row = lambda i: (i, 0)
    return pl.pallas_call(
        body,
        name="loss_head",
        grid=(n,),
        in_specs=[pl.BlockSpec((tm, d), row), pl.BlockSpec((tm, d), row)],
        out_specs=[pl.BlockSpec((1, 1), lambda i: (0, 0)), pl.BlockSpec((tm, d), row)],
        out_shape=[jax.ShapeDtypeStruct((1, 1), F32), jax.ShapeDtypeStruct((T, d), F32)],
        scratch_shapes=[pltpu.VMEM((1, d), F32)],
        compiler_params=_cparams(("arbitrary",)),
    )(y, target)


def _lower_bounds_fwd(lower_bounds):
    depth, n = lower_bounds.shape

    def body(x_ref, soft_ref, lb_ref):
        x = x_ref[...]
        e = jnp.exp(x - jnp.max(x, axis=0, keepdims=True))
        soft_ref[...] = e / jnp.sum(e, axis=0, keepdims=True)
        run = None
        for l in range(depth):
            run = soft_ref[l:l + 1, :] if run is None else run + soft_ref[l:l + 1, :]
            lb_ref[l:l + 1, :] = run - soft_ref[0:1, :]

    return pl.pallas_call(body, name="lower_bounds_fwd",
                          out_shape=[jax.ShapeDtypeStruct((depth, n), F32), jax.ShapeDtypeStruct((depth, n), F32)])(lower_bounds)


def _lower_bounds_bwd(soft, dlb):
    depth, n = soft.shape

    def body(soft_ref, dlb_ref, out_ref, dsoft_ref):
        total = jnp.sum(dlb_ref[...], axis=0, keepdims=True)
        run = None
        for l in reversed(range(depth)):
            run = dlb_ref[l:l + 1, :] if run is None else run + dlb_ref[l:l + 1, :]
            dsoft_ref[l:l + 1, :] = run - total if l == 0 else run
        s, ds = soft_ref[...], dsoft_ref[...]
        out_ref[...] = s * (ds - jnp.sum(s * ds, axis=0, keepdims=True))

    return pl.pallas_call(body, name="lower_bounds_bwd", out_shape=jax.ShapeDtypeStruct((depth, n), F32),
                          scratch_shapes=[pltpu.VMEM((depth, n), F32)])(soft, dlb)


def _layer_fwd(x0, x0b, mem2, lb, w_in, rest_fn, *, bl, seq, alpha, deps=()):
    p = _matmul("proj_in", x0b, w_in, mode="nn", out_dtype=BF16, deps=deps, tm=1024, tn=1792)
    wts = dict(rest_fn(p), w_in=w_in)
    mk = _matmul("mem_k", mem2, wts["w_mem_k"], mode="nn", out_dtype=BF16)
    mv = _matmul("mem_v", mem2, wts["w_mem_v"], mode="nn", out_dtype=BF16)
    y, st, opre = _mixer_fwd(p, mk, mv, lb, wts["conv_w"], wts["hg_norm_w"], bl=bl, seq=seq)
    merged, xhat1, rstd1, x1b = _merge_fwd(y, p, x0, wts["w_branch"], wts["w_o"], wts["b_gate"], wts["ln1_g"], wts["ln1_b"],
                                           alpha=alpha)
    a, xhat2, rstd2, x2, x2b = _mlp_fwd(xhat1, x1b, wts["ln1_g"], wts["ln1_b"], wts["w_up"], wts["w_down"], wts["ln2_g"],
                                        wts["ln2_b"], alpha=alpha)
    saved = dict(x0b=x0b, p=p, mk=mk, mv=mv, y=y, st=st, opre=opre, merged=merged, xhat1=xhat1, rstd1=rstd1, x1b=x1b, a=a,
                 xhat2=xhat2, rstd2=rstd2)
    return x2, x2b, saved, wts


def _relu2_bf16(a):
    return jnp.square(jnp.maximum(a.astype(F32), 0.0)).astype(BF16)


def _mlp_bwd(dz2, dz2b, sv, wts, *, alpha, deps=()):
    g = {}
    da = _matmul("mlp_da", dz2b, wts["w_down"], mode="nt", out_dtype=BF16, tm=1024, deps=deps,
                 epi_fn=lambda acc, a: (acc * (2.0 * jnp.maximum(a.astype(F32), 0.0)),), epi_extra=(sv["a"],))
    g["w_down"] = _matmul_tn("grad_w_down", sv["a"], dz2b, a_fn=_relu2_bf16, out_dtype=BF16, tt=2048)
    g["w_up"] = _matmul_tn("grad_w_up", sv["x1b"], da, out_dtype=BF16, tt=2048)
    dx1 = _matmul("mlp_dx", da, wts["w_up"], mode="nt", epi_fn=lambda acc, dz: (acc + alpha * dz,), epi_extra=(dz2,),
                  tm=512, tk=4096)
    dz1, dz1b, dg1, db1 = _ln_bwd(dx1, sv["xhat1"], sv["rstd1"], wts["ln1_g"])
    g["ln1_g"], g["ln1_b"] = dg1[0:1], db1[0:1]
    return dz1, dz1b, g


def _mix_bwd(dz1, dz1b, sv, mem2, lb, wts, *, bl, seq, alpha, send, below=None, deps=()):
    d = dz1.shape[1]
    g = {}
    g["w_o"] = _matmul_tn("grad_w_o", sv["merged"], dz1b, out_dtype=BF16, tt=2048, deps=deps)
    dr, dp, dy, dbg = _merge_bwd(dz1b, sv["p"], sv["y"], wts["w_branch"], wts["w_o"], wts["b_gate"])
    g["b_gate"] = dbg[0:1]
    g["w_branch"] = jnp.concatenate(
        [_matmul_tn("grad_w_branch", sv["y"], dr, a_cols=(i * W, W), b_cols=(i * d, d), out_dtype=BF16) for i in range(N_BRANCH)],
        axis=0)
    token = send(("w_o", "w_branch"), g)
    dp, dmk, dmv, dcw, dnw, dlb = _mixer_bwd(sv["p"], dy, dp, sv["st"], sv["opre"], sv["mk"], sv["mv"], lb,
                                              wts["conv_w"], wts["hg_norm_w"], bl=bl, seq=seq, deps=(token,))
    g["conv_w"], g["hg_norm_w"], g["lb"] = dcw[0:CONV_K], dnw[0:1], dlb[0:1]
    g["w_mem_k"] = _matmul_tn("grad_w_mem_k", mem2, dmk, out_dtype=BF16)
    g["w_mem_v"] = _matmul_tn("grad_w_mem_v", mem2, dmv, out_dtype=BF16)
    g["w_in"] = _matmul_tn("grad_w_in", sv["x0b"], dp, out_dtype=BF16, tt=2048)
    token = send(("w_in", "w_mem_k", "w_mem_v", "conv_w"), g)
    dx0 = _matmul("proj_in_dx", dp, wts["w_in"], mode="nt", epi_fn=lambda acc, dz: (acc + alpha * dz,), epi_extra=(dz1,),
                  tm=512, tk=dp.shape[1], deps=(token,))
    return (dx0 if below is None else _ln_bwd(dx0, *below)), g


N_CHIPS = 4
MESH_IDS = pl.DeviceIdType.MESH


def _axis_slice(ref, axis, start, size):
    idx = [slice(None)] * len(ref.shape)
    idx[axis] = pl.ds(start, size)
    return ref.at[tuple(idx)]


def _chip_exchange(name, items):
    n = len(items)
    out_shapes, meta = [], []
    for arr, kind, axis in items:
        shp = list(arr.shape)
        if kind == "gather":
            per = shp[axis]
            shp[axis] = per * N_CHIPS
            out_shapes.append(jax.ShapeDtypeStruct(tuple(shp), arr.dtype))
        elif kind == "scatter":
            per = shp[axis] // N_CHIPS
            shp[axis] = per
            out_shapes.append(jax.ShapeDtypeStruct((N_CHIPS, *shp), arr.dtype))
        else:
            per = None
            out_shapes.append(jax.ShapeDtypeStruct((N_CHIPS, *shp), arr.dtype))
        meta.append((kind, axis, per))

    def body(*refs):
        ins, outs = refs[:n], refs[n:2 * n]
        send_sems, recv_sems, local_sems = refs[2 * n:]
        x, y, c = lax.axis_index("x"), lax.axis_index("y"), lax.axis_index("c")
        me = 2 * x + y
        peers = [(1 - x, y), (x, 1 - y), (1 - x, 1 - y)]

        def src_for(t, chip):
            kind, axis, per = meta[t]
            return _axis_slice(ins[t], axis, chip * per, per) if kind == "scatter" else ins[t]

        def dst_from(t, chip):
            kind, axis, per = meta[t]
            return _axis_slice(outs[t], axis, chip * per, per) if kind == "gather" else outs[t].at[chip]

        def remote(t, k):
            px, py = peers[k]
            return pltpu.make_async_remote_copy(
                src_ref=src_for(t, 2 * px + py), dst_ref=dst_from(t, me), send_sem=send_sems.at[t * 3 + k],
                recv_sem=recv_sems.at[t * 3 + k], device_id=(px, py, c), device_id_type=MESH_IDS)

        def arrival(t, k):
            px, py = peers[k]
            return pltpu.make_async_remote_copy(
                src_ref=src_for(t, me), dst_ref=dst_from(t, 2 * px + py), send_sem=send_sems.at[t * 3 + k],
                recv_sem=recv_sems.at[t * 3 + k], device_id=(px, py, c), device_id_type=MESH_IDS)

        sends = [remote(t, k) for t in range(n) for k in range(3)]
        for cp in sends:
            cp.start()
        own = [pltpu.make_async_copy(src_for(t, me), dst_from(t, me), local_sems.at[t]) for t in range(n)]
        for cp in own:
            cp.start()
        for t in range(n):
            for k in range(3):
                arrival(t, k).wait_recv()
        for cp in sends:
            cp.wait_send()
        for cp in own:
            cp.wait()

    any_spec = pl.BlockSpec(memory_space=pl.ANY)
    return pl.pallas_call(
        body,
        name=name,
        in_specs=[any_spec] * n,
        out_specs=[any_spec] * n,
        out_shape=out_shapes,
        scratch_shapes=[pltpu.SemaphoreType.DMA((3 * n,)), pltpu.SemaphoreType.DMA((3 * n,)), pltpu.SemaphoreType.DMA((n,))],
        compiler_params=pltpu.CompilerParams(has_side_effects=True),
    )(*[a for a, _, _ in items])


HBM_SPEC = pl.BlockSpec(memory_space=pltpu.HBM)
SEM_SPEC = pl.BlockSpec(memory_space=pltpu.SEMAPHORE)
N_PEERS = N_CHIPS - 1


def _my_chip():
    return (2 * lax.axis_index("x") + lax.axis_index("y")).astype(jnp.int32).reshape(1)


def _own_block_spec(r, c, axis, tr):
    if axis == 1:
        return pl.BlockSpec((tr, c), lambda i, me: (i, me[0]))
    return pl.BlockSpec((tr, c), lambda i, me: (me[0] * (r // tr) + i, 0))


def _place_shard(name, shard, axis, me):
    r, c = shard.shape
    tr = _row_block(r, c, shard.dtype.itemsize)
    shp = (r, c * N_CHIPS) if axis == 1 else (r * N_CHIPS, c)

    def body(me_ref, s_ref, o_ref):
        del me_ref
        o_ref[...] = s_ref[...]

    return pl.pallas_call(
        body, name=name,
        grid_spec=pltpu.PrefetchScalarGridSpec(
            num_scalar_prefetch=1, grid=(r // tr,),
            in_specs=[pl.BlockSpec((tr, c), lambda i, me: (i, 0))], out_specs=_own_block_spec(r, c, axis, tr)),
        out_shape=jax.ShapeDtypeStruct(shp, shard.dtype),
        compiler_params=_cparams(("parallel",)),
    )(me, shard)


class _Split:
    def __init__(self, name, items):
        self.name, self.n = name, len(items)
        self.srcs = [a for a, _, _ in items]
        self.meta, self.land_shapes = [], []
        for arr, kind, axis in items:
            shp = list(arr.shape)
            if kind == "gather":
                per = shp[axis]
                shp[axis] = per * N_CHIPS
                self.land_shapes.append(jax.ShapeDtypeStruct(tuple(shp), arr.dtype))
            else:
                per = shp[axis] // N_CHIPS
                shp[axis] = per
                self.land_shapes.append(jax.ShapeDtypeStruct((N_PEERS, *shp), arr.dtype))
            self.meta.append((kind, axis, per))

    def _src(self, ins, t, chip):
        kind, axis, per = self.meta[t]
        return _axis_slice(ins[t], axis, chip * per, per) if kind == "scatter" else ins[t]

    def _dst(self, lands, t, chip, slot):
        kind, axis, per = self.meta[t]
        return _axis_slice(lands[t], axis, chip * per, per) if kind == "gather" else lands[t].at[slot]

    def landing_zones(self, me):
        return [_place_shard(self.name + "_own", src, axis, me) if kind == "gather" else lax.empty(ls.shape, ls.dtype)
                for src, ls, (kind, axis, _) in zip(self.srcs, self.land_shapes, self.meta)]

    def _copies(self, ins, lands, send_sems, recv_sems, arrivals):
        x, y, c = lax.axis_index("x"), lax.axis_index("y"), lax.axis_index("c")
        me = 2 * x + y
        peers = [(1 - x, y), (x, 1 - y), (1 - x, 1 - y)]
        res = []
        for t in range(self.n):
            for k, (px, py) in enumerate(peers):
                theirs = 2 * px + py
                sems = dict(send_sem=send_sems.at[t * N_PEERS + k], recv_sem=recv_sems.at[t * N_PEERS + k],
                            device_id=(px, py, c), device_id_type=MESH_IDS)
                if arrivals:
                    res.append(pltpu.make_async_remote_copy(src_ref=self._src(ins, t, me), dst_ref=self._dst(lands, t, theirs, k), **sems))
                else:
                    res.append(pltpu.make_async_remote_copy(src_ref=self._src(ins, t, theirs), dst_ref=self._dst(lands, t, me, k), **sems))
        return res

    def start(self, lands, deps=()):
        n, nd = self.n, len(deps)

        def body(*refs):
            ins, lnd = refs[:n], refs[n:2 * n]
            send_sems, recv_sems = refs[2 * n + nd], refs[2 * n + nd + 1]
            token = refs[-1]
            for cp in self._copies(ins, lnd, send_sems, recv_sems, arrivals=False):
                cp.start()
            token[...] = jnp.zeros_like(token)

        hbm = lambda a: pltpu.HBM(a.shape, a.dtype)
        res = pl.pallas_call(
            body, name=self.name + "_start",
            in_specs=[HBM_SPEC] * (2 * n) + [ANY_SPEC] * nd,
            out_specs=[SEM_SPEC, SEM_SPEC] + [HBM_SPEC] * (2 * n) + [pl.BlockSpec(memory_space=pltpu.VMEM)],
            out_shape=[pltpu.SemaphoreType.DMA((N_PEERS * n,)), pltpu.SemaphoreType.DMA((N_PEERS * n,))]
            + [hbm(a) for a in self.srcs] + [hbm(a) for a in self.land_shapes] + [jax.ShapeDtypeStruct((8, 128), F32)],
            input_output_aliases={i: 2 + i for i in range(2 * n)},
            compiler_params=pltpu.CompilerParams(has_side_effects=pltpu.SideEffectType.DATAFLOW_SIDE_EFFECTING),
        )(*[pltpu.with_memory_space_constraint(a, pltpu.HBM) for a in self.srcs],
          *[pltpu.with_memory_space_constraint(a, pltpu.HBM) for a in lands], *deps)
        return res[:-1], res[-1]

    def wait(self, state, after):
        n = self.n
        send_sems, recv_sems = state[0], state[1]
        srcs, lands = state[2:2 + n], state[2 + n:2 + 2 * n]

        def body(*refs):
            ins, lnd = refs[:n], refs[n:2 * n]
            s_sems, r_sems = refs[2 * n], refs[2 * n + 1]
            for cp in self._copies(ins, lnd, s_sems, r_sems, arrivals=True):
                cp.wait_recv()
            for cp in self._copies(ins, lnd, s_sems, r_sems, arrivals=False):
                cp.wait_send()

        hbm = lambda a: pltpu.HBM(a.shape, a.dtype)
        res = pl.pallas_call(
            body, name=self.name + "_wait",
            in_specs=[HBM_SPEC] * (2 * n) + [SEM_SPEC, SEM_SPEC, ANY_SPEC],
            out_specs=[HBM_SPEC] * (2 * n),
            out_shape=[hbm(a) for a in self.srcs] + [hbm(a) for a in self.land_shapes],
            input_output_aliases={i: i for i in range(2 * n)},
            compiler_params=pltpu.CompilerParams(has_side_effects=pltpu.SideEffectType.DATAFLOW_SIDE_EFFECTING),
        )(*srcs, *lands, send_sems, recv_sems, after)
        return res[:n], res[n:]


def _sibling_swap(name, arrays):
    n = len(arrays)

    def body(*refs):
        ins, outs = refs[:n], refs[n:2 * n]
        send_sems, recv_sems = refs[2 * n:]
        sibling = (lax.axis_index("x"), lax.axis_index("y"), 1 - lax.axis_index("c"))
        copies = [pltpu.make_async_remote_copy(src_ref=ins[t], dst_ref=outs[t], send_sem=send_sems.at[t], recv_sem=recv_sems.at[t],
                                               device_id=sibling, device_id_type=MESH_IDS) for t in range(n)]
        for cp in copies:
            cp.start()
        for cp in copies:
            cp.wait()

    any_spec = pl.BlockSpec(memory_space=pl.ANY)
    return pl.pallas_call(
        body,
        name=name,
        in_specs=[any_spec] * n,
        out_specs=[any_spec] * n,
        out_shape=[jax.ShapeDtypeStruct(a.shape, a.dtype) for a in arrays],
        scratch_shapes=[pltpu.SemaphoreType.DMA((n,)), pltpu.SemaphoreType.DMA((n,))],
        compiler_params=pltpu.CompilerParams(has_side_effects=True),
    )(*arrays)


def _row_block(r, c, itemsize=4, target=1 << 20):
    if r % 8 != 0:
        return r
    best = 8
    for tr in range(8, r + 1, 8):
        if r % tr == 0 and tr * c * itemsize <= target:
            best = tr
    return best


def _sum_chips_into(parts, stacked, layer):
    _, r, c = parts.shape
    tr = _row_block(r, c)

    def body(p_ref, s_ref, o_ref):
        del s_ref
        o_ref[...] = ((p_ref[0] + p_ref[1]) + p_ref[2]) + p_ref[3]

    return pl.pallas_call(
        body,
        name="sum_chips",
        grid=(r // tr,),
        in_specs=[pl.BlockSpec((N_CHIPS, tr, c), lambda i: (0, i, 0)), pl.BlockSpec(memory_space=pl.ANY)],
        out_specs=pl.BlockSpec((None, tr, c), lambda i: (layer, i, 0)),
        out_shape=jax.ShapeDtypeStruct(stacked.shape, stacked.dtype),
        input_output_aliases={1: 0},
        compiler_params=_cparams(("parallel",)),
    )(parts, stacked)


def _sum_own_and_peers(me, g, axis, landed):
    _, r, c = landed.shape
    tr = _row_block(r, c)

    def body(me_ref, g_ref, p_ref, o_ref):
        del me_ref
        o_ref[...] = ((g_ref[...].astype(F32) + p_ref[0].astype(F32)) + p_ref[1].astype(F32)) + p_ref[2].astype(F32)

    return pl.pallas_call(
        body, name="sum_chips_own",
        grid_spec=pltpu.PrefetchScalarGridSpec(
            num_scalar_prefetch=1, grid=(r // tr,),
            in_specs=[_own_block_spec(r, c, axis, tr), pl.BlockSpec((N_PEERS, tr, c), lambda i, me: (0, i, 0))],
            out_specs=pl.BlockSpec((tr, c), lambda i, me: (i, 0))),
        out_shape=jax.ShapeDtypeStruct((r, c), F32),
        compiler_params=_cparams(("parallel",)),
    )(me, g, landed)


def _adamw_math(w, m, v, g):
    m_new = ADAM_B1 * m + (1.0 - ADAM_B1) * g
    v_new = ADAM_B2 * v + (1.0 - ADAM_B2) * jnp.square(g)
    m_hat = m_new / (1.0 - ADAM_B1 ** ADAM_STEP)
    v_hat = v_new / (1.0 - ADAM_B2 ** ADAM_STEP)
    return -ADAM_LR * (m_hat / (jnp.sqrt(v_hat) + ADAM_EPS) + ADAM_WD * w), m_new, v_new


def _adamw(w, m, v, g_a, g_b):
    L, r, c = w.shape
    tr = _row_block(r, c, target=1 << 19)

    def body(w_ref, m_ref, v_ref, ga_ref, gb_ref, g_ref, d_ref, nm_ref, nv_ref):
        g = ga_ref[...] + gb_ref[...]
        g_ref[...] = g
        d_ref[...], nm_ref[...], nv_ref[...] = _adamw_math(w_ref[...], m_ref[...], v_ref[...], g)

    spec = pl.BlockSpec((None, tr, c), lambda l, i: (l, i, 0))
    return pl.pallas_call(
        body,
        name="adamw",
        grid=(L, r // tr),
        in_specs=[spec] * 5,
        out_specs=[spec] * 4,
        out_shape=[jax.ShapeDtypeStruct(w.shape, F32)] * 4,
        compiler_params=_cparams(("parallel", "parallel")),
    )(w, m, v, g_a, g_b)


def _adamw_layer(w, m, v, g_a, g_b, layer, outs):
    L, r, c = w.shape
    tr = _row_block(r, c, target=1 << 19)
    n_prev = 0 if outs is None else 4

    def body(w_ref, m_ref, v_ref, ga_ref, gb_ref, *rest):
        g_ref, d_ref, nm_ref, nv_ref = rest[n_prev:]
        g = ga_ref[...] + gb_ref[...]
        g_ref[...] = g
        d_ref[...], nm_ref[...], nv_ref[...] = _adamw_math(w_ref[...], m_ref[...], v_ref[...], g)

    at_layer = pl.BlockSpec((None, tr, c), lambda i: (layer, i, 0))
    flat = pl.BlockSpec((tr, c), lambda i: (i, 0))
    return pl.pallas_call(
        body,
        name="adamw_layer",
        grid=(r // tr,),
        in_specs=[at_layer] * 3 + [flat] * 2 + [ANY_SPEC] * n_prev,
        out_specs=[at_layer] * 4,
        out_shape=[jax.ShapeDtypeStruct(w.shape, F32)] * 4,
        input_output_aliases={5 + k: k for k in range(n_prev)},
        compiler_params=_cparams(("parallel",)),
    )(w, m, v, g_a, g_b, *(outs or ()))


SHARDED = (("w_in", 1), ("conv_w", 1), ("w_mem_k", 0), ("w_mem_v", 0), ("w_branch", 1), ("w_o", 0), ("w_up", 1), ("w_down", 0))
SMALL = ("lower_bounds", "hg_norm_w", "b_gate", "ln1_g", "ln1_b", "ln2_g", "ln2_b")
WEIGHT_ORDER = ("lower_bounds", "w_in", "conv_w", "hg_norm_w", "w_mem_k", "w_mem_v", "w_branch", "b_gate", "w_o", "ln1_g", "ln1_b",
                "w_up", "w_down", "ln2_g", "ln2_b")


def kernel(x, mem, lower_bounds, w_in, conv_w, hg_norm_w, w_mem_k, w_mem_v, w_branch, b_gate, w_o, ln1_g, ln1_b, w_up, w_down, ln2_g, ln2_b, loss_target, m_lower_bounds, m_w_in, m_conv_w, m_hg_norm_w, m_w_mem_k, m_w_mem_v, m_w_branch, m_b_gate, m_w_o, m_ln1_g, m_ln1_b, m_w_up, m_w_down, m_ln2_g, m_ln2_b, v_lower_bounds, v_w_in, v_conv_w, v_hg_norm_w, v_w_mem_k, v_w_mem_v, v_w_branch, v_b_gate, v_w_o, v_ln1_g, v_ln1_b, v_w_up, v_w_down, v_ln2_g, v_ln2_b):
    bl, seq, d = x.shape
    depth = w_in.shape[0]
    weights = dict(lower_bounds=lower_bounds, w_in=w_in, conv_w=conv_w, hg_norm_w=hg_norm_w, w_mem_k=w_mem_k, w_mem_v=w_mem_v,
                   w_branch=w_branch, b_gate=b_gate, w_o=w_o, ln1_g=ln1_g, ln1_b=ln1_b, w_up=w_up, w_down=w_down, ln2_g=ln2_g, ln2_b=ln2_b)
    mom_m = dict(lower_bounds=m_lower_bounds, w_in=m_w_in, conv_w=m_conv_w, hg_norm_w=m_hg_norm_w, w_mem_k=m_w_mem_k, w_mem_v=m_w_mem_v,
                 w_branch=m_w_branch, b_gate=m_b_gate, w_o=m_w_o, ln1_g=m_ln1_g, ln1_b=m_ln1_b, w_up=m_w_up, w_down=m_w_down,
                 ln2_g=m_ln2_g, ln2_b=m_ln2_b)
    mom_v = dict(lower_bounds=v_lower_bounds, w_in=v_w_in, conv_w=v_conv_w, hg_norm_w=v_hg_norm_w, w_mem_k=v_w_mem_k, w_mem_v=v_w_mem_v,
                 w_branch=v_w_branch, b_gate=v_b_gate, w_o=v_w_o, ln1_g=v_ln1_g, ln1_b=v_ln1_b, w_up=v_w_up, w_down=v_w_down,
                 ln2_g=v_ln2_g, ln2_b=v_ln2_b)

    def shard2d(name, l):
        w = weights[name][l]
        if name == "w_branch":
            return w.reshape(N_BRANCH * W, w.shape[-1]).astype(BF16)
        return w if name == "conv_w" else w.astype(BF16)

    me = _my_chip()

    shard_axis = dict(SHARDED)

    def start_exchange(name, kind, items, deps=()):
        ex = _Split(name, [(arr, kind, shard_axis[nm]) for nm, arr in items])
        state, token = ex.start(ex.landing_zones(me), deps)
        return ex, state, [nm for nm, _ in items], token

    def start_gathers(l, deps=()):
        first = start_exchange(f"gather_in_l{l}", "gather", [("w_in", shard2d("w_in", l))], deps)
        rest = start_exchange(f"gather_rest_l{l}", "gather", [(nm, shard2d(nm, l)) for nm, _ in SHARDED if nm != "w_in"],
                              (first[3],))
        return first, rest

    def gathered(pend, after):
        ex, state, names, _ = pend
        return dict(zip(names, ex.wait(state, after=after)[1]))

    x2d, mem2, t2d = x.reshape(bl * seq, d), mem.reshape(-1, d), loss_target.reshape(bl * seq, d)
    alpha = (2.0 * depth) ** 0.25
    soft, lb_all = _lower_bounds_fwd(lower_bounds)

    h, hb, saved, layer_wts = x2d, x2d.astype(BF16), [], []
    pending = start_gathers(0)
    for l in range(depth):
        first, rest = pending
        w_in_l = gathered(first, h)["w_in"]

        def rest_fn(after, l=l, rest=rest):
            wts = gathered(rest, after)
            for name in ("hg_norm_w", "b_gate", "ln1_g", "ln1_b", "ln2_g", "ln2_b"):
                wts[name] = weights[name][l][None, :]
            return wts

        deps = (rest[3],)
        if l + 1 < depth:
            pending = start_gathers(l + 1, (w_in_l, rest[3]))
            deps += (pending[0][3], pending[1][3])
        h, hb, sv, wts = _layer_fwd(h, hb, mem2, lb_all[l:l + 1], w_in_l, rest_fn, bl=bl, seq=seq, alpha=alpha, deps=deps)
        saved.append(sv)
        layer_wts.append(wts)
    loss, dh = _loss_head(h, t2d)

    shape3 = {name: (depth, weights[name].size // (depth * weights[name].shape[-1]), weights[name].shape[-1]) for name, _ in SHARDED}
    partial = [dict() for _ in range(depth)]
    smalls = [None] * depth
    outs = {name: None for name, _ in SHARDED}

    def finish_reduce(pend, l, after):
        ex, state, names, _ = pend
        sent, got = ex.wait(state, after=after)
        for nm, g_full, landed in zip(names, sent, got):
            partial[l][nm] = _sum_own_and_peers(me, g_full, shard_axis[nm], landed)

    def optimizer_step(l):
        names = [name for name, _ in SHARDED]
        theirs = _sibling_swap(f"swap_partials_l{l}", [partial[l][nm] for nm in names])
        for nm, other in zip(names, theirs):
            outs[nm] = _adamw_layer(weights[nm].reshape(shape3[nm]), mom_m[nm].reshape(shape3[nm]), mom_v[nm].reshape(shape3[nm]),
                                    partial[l][nm], other, l, outs[nm])
        return tuple(outs[nm][0] for nm in names)

    pending_mix, deps = [], ()
    dz2, dz2b, dg2, db2 = _ln_bwd(dh, saved[-1]["xhat2"], saved[-1]["rstd2"], layer_wts[-1]["ln2_g"])
    for l in reversed(range(depth)):
        dz1, dz1b, g_mlp = _mlp_bwd(dz2, dz2b, saved[l], layer_wts[l], alpha=alpha, deps=deps)
        g_mlp["ln2_g"], g_mlp["ln2_b"] = dg2[0:1], db2[0:1]
        pending_mlp = start_exchange(f"reduce_mlp_l{l}", "scatter", [(nm, g_mlp[nm]) for nm in ("w_up", "w_down")])
        deps = (pending_mlp[3],)
        if pending_mix:
            for pend in pending_mix:
                finish_reduce(pend, l + 1, dz1)
            deps += optimizer_step(l + 1)
        pending_mix = []

        def send(names, g, l=l, pending_mix=pending_mix):
            pend = start_exchange(f"reduce_{names[0]}_l{l}", "scatter", [(nm, g[nm]) for nm in names])
            pending_mix.append(pend)
            return pend[3]

        below = (saved[l - 1]["xhat2"], saved[l - 1]["rstd2"], layer_wts[l - 1]["ln2_g"]) if l > 0 else None
        out, g = _mix_bwd(dz1, dz1b, saved[l], mem2, lb_all[l:l + 1], layer_wts[l], bl=bl, seq=seq, alpha=alpha, send=send,
                          below=below, deps=deps)
        if l > 0:
            dz2, dz2b, dg2, db2 = out
        else:
            dh = out
        finish_reduce(pending_mlp, l, out[0] if l > 0 else out)
        deps = ()
        g.update(g_mlp, lower_bounds=g["lb"])
        smalls[l] = jnp.concatenate([g[nm] for nm in SMALL], axis=1)
    small_parts = _chip_exchange("reduce_small", [(jnp.stack(smalls), "bcast", 0)])[0]
    small_sum = _sum_chips_into(small_parts.reshape(N_CHIPS, depth, -1), jnp.zeros((1, depth, small_parts.shape[-1]), F32), 0)
    small_sum = small_sum.reshape(depth, 1, -1)
    small_theirs = _sibling_swap("swap_small", [small_sum])[0]
    for pend in pending_mix:
        finish_reduce(pend, 0, small_theirs)
    optimizer_step(0)

    outs = {name: [r.reshape(weights[name].shape) for r in res] for name, res in outs.items()}
    off = 0
    for name in SMALL:
        n = weights[name].shape[1]
        mine, other = small_sum[:, :, off:off + n], small_theirs[:, :, off:off + n]
        off += n
        if name == "lower_bounds":
            mine = _lower_bounds_bwd(soft, mine[:, 0, :])[:, None, :]
            other = _lower_bounds_bwd(soft, other[:, 0, :])[:, None, :]
        shp = (depth, 1, n)
        res = _adamw(weights[name].reshape(shp), mom_m[name].reshape(shp), mom_v[name].reshape(shp), mine, other)
        outs[name] = [r.reshape(weights[name].shape) for r in res]
    assert off == small_sum.shape[-1]

    total_loss = lax.psum(loss[0, 0], ("x", "y", "c"))
    result = [total_loss, dh.reshape(bl, seq, d)]
    for k in range(4):
        result += [outs[name][k] for name in WEIGHT_ORDER]
    return tuple(result)
```

```python
import functools

import jax
import jax.numpy as jnp
from jax import lax
from jax.experimental import pallas as pl
from jax.experimental.pallas import tpu as pltpu

F32 = jnp.float32
BF16 = jnp.bfloat16

HG_HEADS = 4
HG_F = 128
HG_CHUNK = 32
MEM_HEADS = 4
MEM_HEAD_DIM = 128
BRANCH_WIDTH = 512
N_BRANCH = 3
CONV_K = 3
LN_EPS = 1e-5
RMS_EPS = 1e-6
ADAM_LR = 0.001
ADAM_B1 = 0.9
ADAM_B2 = 0.999
ADAM_EPS = 1e-08
ADAM_WD = 0.01
ADAM_STEP = 10

VMEM_LIMIT = 48 * 1024 * 1024


def _cparams(sem):
    return pltpu.CompilerParams(dimension_semantics=sem, vmem_limit_bytes=VMEM_LIMIT)


def _dot(a, b, dims):
    return lax.dot_general(a, b, (dims, ((), ())), preferred_element_type=F32)


NN = ((1,), (0,))
NT = ((1,), (1,))
TN = ((0,), (0,))


def _pick(n, pref):
    for t in pref:
        if n % t == 0:
            return t
    return n


ANY_SPEC = pl.BlockSpec(memory_space=pl.ANY)


def _matmul(name, a, b, *, mode, out_dtype=F32, a_fn=None, a_extra=(), epi_fn=None, epi_extra=(), n_out=1, out_kinds=None,
            tm=512, tn=1024, tk=1024, deps=()):
    M, K = a.shape
    N = b.shape[1] if mode == "nn" else b.shape[0]
    tm, tn, tk = _pick(M, (tm, 256, 128, 8)), _pick(N, (tn, 896, 512, 256, 128)), _pick(K, (tk, 512, 256, 128))
    nk = K // tk
    n_ax, n_ex = len(a_extra), len(epi_extra)
    n_in = 2 + n_ax + n_ex + len(deps)
    out_dtypes = out_dtype if isinstance(out_dtype, (tuple, list)) else (out_dtype,) * n_out
    out_kinds = out_kinds or ("tile",) * n_out

    def body(*refs):
        a_ref, b_ref = refs[0], refs[1]
        ax_refs = refs[2:2 + n_ax]
        ex_refs = refs[2 + n_ax:2 + n_ax + n_ex]
        o_refs = refs[n_in:n_in + n_out]
        at = a_ref[...]
        at = a_fn(at, *[r[...] for r in ax_refs]) if a_fn is not None else at.astype(BF16)
        part = _dot(at, b_ref[...].astype(BF16), NN if mode == "nn" else NT)

        def finish(acc):
            outs = epi_fn(acc, *[r[...] for r in ex_refs]) if epi_fn is not None else (acc,)
            for o_ref, o, kind in zip(o_refs, outs, out_kinds):
                if kind == "rowsum":
                    @pl.when(pl.program_id(1) == 0)
                    def _(o_ref=o_ref):
                        o_ref[...] = jnp.zeros_like(o_ref)

                    o_ref[0:1, :] += o
                else:
                    o_ref[...] = o.astype(o_ref.dtype)

        if nk == 1:
            finish(part)
            return
        acc_ref = refs[-1]
        k = pl.program_id(2)

        @pl.when(k == 0)
        def _():
            acc_ref[...] = part

        @pl.when(jnp.logical_and(k > 0, k < nk - 1))
        def _():
            acc_ref[...] += part

        @pl.when(k == nk - 1)
        def _():
            finish(acc_ref[...] + part)

    b_mode = dict(pipeline_mode=pl.Buffered(1)) if (nk == 1 and N == tn) else {}
    in_specs = [pl.BlockSpec((tm, tk), lambda j, i, k: (i, k)),
                pl.BlockSpec((tk, tn), lambda j, i, k: (k, j), **b_mode) if mode == "nn"
                else pl.BlockSpec((tn, tk), lambda j, i, k: (j, k), **b_mode)]
    in_specs += [pl.BlockSpec((1, tk), lambda j, i, k: (0, k)) for _ in a_extra]
    for e in epi_extra:
        if e.shape[0] == 1:
            in_specs.append(pl.BlockSpec((1, tn), lambda j, i, k: (0, j)))
        elif e.shape[1] == 1:
            in_specs.append(pl.BlockSpec((tm, 1), lambda j, i, k: (i, 0)))
        else:
            in_specs.append(pl.BlockSpec((tm, tn), lambda j, i, k: (i, j)))
    in_specs += [ANY_SPEC] * len(deps)
    out_specs, out_shapes = [], []
    for kind, dt in zip(out_kinds, out_dtypes):
        if kind == "col":
            out_specs.append(pl.BlockSpec((tm, 1), lambda j, i, k: (i, 0)))
            out_shapes.append(jax.ShapeDtypeStruct((M, 1), dt))
        elif kind == "rowsum":
            out_specs.append(pl.BlockSpec((8, tn), lambda j, i, k: (0, j)))
            out_shapes.append(jax.ShapeDtypeStruct((8, N), dt))
        else:
            out_specs.append(pl.BlockSpec((tm, tn), lambda j, i, k: (i, j)))
            out_shapes.append(jax.ShapeDtypeStruct((M, N), dt))
    out = pl.pallas_call(
        body,
        name=name,
        grid=(N // tn, M // tm, nk),
        in_specs=in_specs,
        out_specs=out_specs,
        out_shape=out_shapes,
        scratch_shapes=[pltpu.VMEM((tm, tn), F32)] if nk > 1 else [],
        compiler_params=_cparams(("arbitrary", "arbitrary", "arbitrary")),
    )(a, b, *a_extra, *epi_extra, *deps)
    return out[0] if n_out == 1 else out


def _matmul_tn(name, a, b, *, a_fn=None, a_extra=(), a_cols=None, b_cols=None, ta=1024, tb=1024, tt=1024, out_dtype=F32, deps=()):
    T = a.shape[0]
    a0, Ka = a_cols if a_cols is not None else (0, a.shape[1])
    b0, Nb = b_cols if b_cols is not None else (0, b.shape[1])
    ta, tb, tt = _pick(Ka, (ta, 512, 256, 128)), _pick(Nb, (tb, 896, 512, 256, 128)), _pick(T, (tt, 512, 256, 128))
    assert a0 % ta == 0 and b0 % tb == 0
    a0, b0 = a0 // ta, b0 // tb
    nt = T // tt
    n_ax = len(a_extra)

    def body(*refs):
        a_ref, b_ref = refs[0], refs[1]
        ax_refs = refs[2:2 + n_ax]
        o_ref = refs[2 + n_ax + len(deps)]
        acc_ref = refs[-1]
        t = pl.program_id(2)
        at = a_ref[...]
        at = a_fn(at, *[r[...] for r in ax_refs]) if a_fn is not None else at.astype(BF16)
        part = _dot(at, b_ref[...].astype(BF16), TN)

        @pl.when(t == 0)
        def _():
            acc_ref[...] = part

        @pl.when(jnp.logical_and(t > 0, t < nt - 1))
        def _():
            acc_ref[...] += part

        @pl.when(t == nt - 1)
        def _():
            o_ref[...] = (acc_ref[...] + part if nt > 1 else part).astype(o_ref.dtype)

    in_specs = [pl.BlockSpec((tt, ta), lambda i, j, t: (t, a0 + i)), pl.BlockSpec((tt, tb), lambda i, j, t: (t, b0 + j))]
    in_specs += [pl.BlockSpec((1, ta), lambda i, j, t: (0, a0 + i)) for _ in a_extra]
    in_specs += [ANY_SPEC] * len(deps)
    return pl.pallas_call(
        body,
        name=name,
        grid=(Ka // ta, Nb // tb, nt),
        in_specs=in_specs,
        out_specs=pl.BlockSpec((ta, tb), lambda i, j, t: (i, j)),
        out_shape=jax.ShapeDtypeStruct((Ka, Nb), out_dtype),
        scratch_shapes=[pltpu.VMEM((ta, tb), F32)],
        compiler_params=_cparams(("parallel", "parallel", "arbitrary")),
    )(a, b, *a_extra, *deps)


W = BRANCH_WIDTH
C_CB, C_CC, C_CH, C_HQ, C_HF, C_HI, C_HG, C_MQ, N_MIX = 0, W, 2 * W, 3 * W, 4 * W, 5 * W, 6 * W, 7 * W, 8 * W
TS_MIX = 256
PREV_ROWS = 16
KEEP_NAMES = ("sq", "qs", "k", "sig", "f", "ea", "eb", "eq", "ek")


def _sigmoid(x):
    return jax.nn.sigmoid(x)


def _chunk_pos(shape):
    return lax.broadcasted_iota(jnp.int32, shape, 0) & (HG_CHUNK - 1)


def _seg_cumsum(x, pos):
    sh = 1
    while sh < HG_CHUNK:
        x = x + jnp.where(pos >= sh, pltpu.roll(x, sh, 0), 0.0)
        sh *= 2
    return x


def _seg_rev_cumsum(x, pos):
    n = x.shape[0]
    sh = 1
    while sh < HG_CHUNK:
        x = x + jnp.where(pos < HG_CHUNK - sh, pltpu.roll(x, n - sh, 0), 0.0)
        sh *= 2
    return x


def _chunk_mask(ts):
    r = lax.broadcasted_iota(jnp.int32, (ts, ts), 0)
    c = lax.broadcasted_iota(jnp.int32, (ts, ts), 1)
    return jnp.logical_and((r // HG_CHUNK) == (c // HG_CHUNK), c <= r)


def _hgrn_gates(p_ref, lb):
    q = p_ref[:, C_HQ:C_HQ + W].astype(F32)
    fl = p_ref[:, C_HF:C_HF + W].astype(F32)
    sig = _sigmoid(fl)
    f = lb + (1.0 - lb) * sig
    logf = jnp.log(f)
    k = (1.0 - lb) * _sigmoid(-fl)
    sq = _sigmoid(q)
    qs = q * sq
    return q, sq, qs, sig, f, logf, k


def _hgrn_decays(logf, bc_sc, ts):
    pos = _chunk_pos(logf.shape)
    bc = _seg_cumsum(logf, pos)
    bc_sc[...] = bc
    nc = ts // HG_CHUNK
    bref = jnp.concatenate(
        [jnp.broadcast_to(bc_sc[n * HG_CHUNK + HG_CHUNK // 2 - 1:n * HG_CHUNK + HG_CHUNK // 2, :], (HG_CHUNK, W)) for n in range(nc)], axis=0)
    blast = jnp.concatenate(
        [jnp.broadcast_to(bc_sc[(n + 1) * HG_CHUNK - 1:(n + 1) * HG_CHUNK, :], (HG_CHUNK, W)) for n in range(nc)], axis=0)
    return pos, bc, bref, blast


def _conv_shift_down(u, carry_ref, row):
    n = carry_ref.shape[0]
    last, before = carry_ref[n - 1:n, :], carry_ref[n - 2:n - 1, :]
    u1 = jnp.where(row == 0, last, pltpu.roll(u, 1, 0))
    u2 = jnp.where(row == 0, before, jnp.where(row == 1, last, pltpu.roll(u, 2, 0)))
    return u1, u2


def _attn_probs(qh, kh):
    s = _dot(qh, kh, NT) * (MEM_HEAD_DIM ** -0.5)
    e = jnp.exp(s - jnp.max(s, axis=-1, keepdims=True))
    return e / jnp.sum(e, axis=-1, keepdims=True)


def _mixer_fwd(p, mk, mv, lb, conv_w, norm_w, *, bl, seq):
    T = p.shape[0]
    ts = TS_MIX
    ns = seq // ts
    nc = ts // HG_CHUNK
    ml = mk.shape[0] // bl

    def body(p_ref, mk_ref, mv_ref, lb_ref, cw_ref, nw_ref, y_ref, st_ref, opre_ref, state_sc, carry_sc, bc_sc):
        @pl.when(pl.program_id(1) == 0)
        def _():
            state_sc[...] = jnp.zeros_like(state_sc)
            carry_sc[...] = jnp.zeros_like(carry_sc)

        cb, cc, ch = (p_ref[:, c0:c0 + W].astype(F32) for c0 in (C_CB, C_CC, C_CH))
        u = cc * ch
        row = lax.broadcasted_iota(jnp.int32, (ts, W), 0)
        u1, u2 = _conv_shift_down(u, carry_sc, row)
        yconv = u2 * cw_ref[0:1, :] + u1 * cw_ref[1:2, :] + u * cw_ref[2:3, :]
        y_ref[:, 0:W] = (cb * yconv).astype(BF16)
        carry_sc[...] = u[ts - 8:ts, :]

        lbv = lb_ref[...]
        _, _, qs, _, _, logf, k = _hgrn_gates(p_ref, lbv)
        pos, bc, bref, blast = _hgrn_decays(logf, bc_sc, ts)
        a_all = (qs * jnp.exp(bc - bref)).astype(BF16)
        bk_all = (k * jnp.exp(bref - bc)).astype(BF16)
        qin_all = (qs * jnp.exp(bc)).astype(BF16)
        kout_all = (k * jnp.exp(blast - bc)).astype(BF16)
        v_all = p_ref[:, C_HI:C_HI + W].astype(BF16)
        mask = _chunk_mask(ts)
        heads = [slice(h * HG_F, (h + 1) * HG_F) for h in range(HG_HEADS)]
        st = [state_sc[h] for h in range(HG_HEADS)]
        o_inter = [[] for _ in range(HG_HEADS)]
        for n in range(nc):
            rows = slice(n * HG_CHUNK, (n + 1) * HG_CHUNK)
            for h, hs in enumerate(heads):
                st_ref[n, h] = st[h]
                o_inter[h].append(_dot(qin_all[rows, hs], st[h].astype(BF16), NT))
                kv = _dot(v_all[rows, hs], kout_all[rows, hs], TN)
                decay = jnp.exp(bc_sc[(n + 1) * HG_CHUNK - 1:(n + 1) * HG_CHUNK, hs])
                st[h] = st[h] * decay + kv
        for h in range(HG_HEADS):
            state_sc[h] = st[h]
        scores = [_dot(a_all[:, hs], bk_all[:, hs], NT) for hs in heads]
        scores = [jnp.where(mask, s, 0.0).astype(BF16) for s in scores]
        outs = [_dot(scores[h], v_all[:, hs], NN) + jnp.concatenate(o_inter[h], axis=0) for h, hs in enumerate(heads)]
        for h, hs in enumerate(heads):
            o = outs[h]
            opre_ref[:, hs] = o
            on = o * lax.rsqrt(jnp.mean(o * o, axis=-1, keepdims=True) + RMS_EPS) * nw_ref[...]
            g = p_ref[:, C_HG + h * HG_F:C_HG + (h + 1) * HG_F].astype(F32)
            y_ref[:, W + h * HG_F:W + (h + 1) * HG_F] = (on * (g * _sigmoid(g))).astype(BF16)

        mheads = [slice(h * MEM_HEAD_DIM, (h + 1) * MEM_HEAD_DIM) for h in range(MEM_HEADS)]
        probs = [_attn_probs(p_ref[:, C_MQ + h * MEM_HEAD_DIM:C_MQ + (h + 1) * MEM_HEAD_DIM].astype(BF16), mk_ref[:, hs])
                 for h, hs in enumerate(mheads)]
        for h, hs in enumerate(mheads):
            y_ref[:, 2 * W + h * MEM_HEAD_DIM:2 * W + (h + 1) * MEM_HEAD_DIM] = _dot(
                probs[h].astype(BF16), mv_ref[:, hs], NN).astype(BF16)

    return pl.pallas_call(
        body,
        name="mixer_fwd",
        grid=(bl, ns),
        in_specs=[
            pl.BlockSpec((ts, N_MIX), lambda b, s: (b * ns + s, 0)),
            pl.BlockSpec((ml, W), lambda b, s: (b, 0)),
            pl.BlockSpec((ml, W), lambda b, s: (b, 0)),
            pl.BlockSpec((1, W), lambda b, s: (0, 0)),
            pl.BlockSpec((CONV_K, W), lambda b, s: (0, 0)),
            pl.BlockSpec((1, HG_F), lambda b, s: (0, 0)),
        ],
        out_specs=[
            pl.BlockSpec((ts, 3 * W), lambda b, s: (b * ns + s, 0)),
            pl.BlockSpec((nc, HG_HEADS, HG_F, HG_F), lambda b, s: (b * ns + s, 0, 0, 0)),
            pl.BlockSpec((ts, W), lambda b, s: (b * ns + s, 0)),
        ],
        out_shape=[
            jax.ShapeDtypeStruct((T, 3 * W), BF16),
            jax.ShapeDtypeStruct((T // HG_CHUNK, HG_HEADS, HG_F, HG_F), F32),
            jax.ShapeDtypeStruct((T, W), F32),
        ],
        scratch_shapes=[pltpu.VMEM((HG_HEADS, HG_F, HG_F), F32), pltpu.VMEM((8, W), F32), pltpu.VMEM((ts, W), F32)],
        compiler_params=_cparams(("arbitrary", "arbitrary")),
    )(p, mk, mv, lb, conv_w, norm_w)


def _mixer_bwd(p, dy, dp_gates, st, opre, mk, mv, lb, conv_w, norm_w, *, bl, seq, deps=()):
    T, nin = p.shape
    ts = TS_MIX
    ns = seq // ts
    nc = ts // HG_CHUNK
    ml = mk.shape[0] // bl
    mid, last = HG_CHUNK // 2 - 1, HG_CHUNK - 1

    def body(p_ref, pprev_ref, dy_ref, dpin_ref, st_ref, opre_ref, mk_ref, mv_ref, lb_ref, cw_ref, nw_ref, *rest):
        (dp_ref, dmk_ref, dmv_ref, dcw_ref, dnw_ref, dlb_ref, dstate_sc, carry_sc, uprev_sc, ab_sc, bkb_sc, qinb_sc, koutb_sc,
         dob_sc, dv_sc, da_sc, dbk_sc, dqin_sc, dkout_sc, dec_sc, ddec_sc, *keep_scs) = rest[len(deps):]
        del dpin_ref
        b, s = pl.program_id(0), pl.program_id(1)

        @pl.when(s == 0)
        def _():
            dstate_sc[...] = jnp.zeros_like(dstate_sc)
            carry_sc[...] = jnp.zeros_like(carry_sc)
            dmk_ref[...] = jnp.zeros_like(dmk_ref)
            dmv_ref[...] = jnp.zeros_like(dmv_ref)

        @pl.when(jnp.logical_and(b == 0, s == 0))
        def _():
            dcw_ref[...] = jnp.zeros_like(dcw_ref)
            dnw_ref[...] = jnp.zeros_like(dnw_ref)
            dlb_ref[...] = jnp.zeros_like(dlb_ref)

        cb, cc, ch = (p_ref[:, c0:c0 + W].astype(F32) for c0 in (C_CB, C_CC, C_CH))
        u = cc * ch
        row = lax.broadcasted_iota(jnp.int32, (ts, W), 0)
        uprev = pprev_ref[:, C_CC:C_CC + W].astype(F32) * pprev_ref[:, C_CH:C_CH + W].astype(F32)
        uprev_sc[...] = jnp.where(s == ns - 1, 0.0, uprev)
        u1, u2 = _conv_shift_down(u, uprev_sc, row)
        w0, w1, w2 = cw_ref[0:1, :], cw_ref[1:2, :], cw_ref[2:3, :]
        dya = dy_ref[:, 0:W].astype(F32)
        dp_ref[:, C_CB:C_CB + W] = (dya * (u2 * w0 + u1 * w1 + u * w2)).astype(BF16)
        dv = cb * dya
        dv1 = jnp.where(row == ts - 1, carry_sc[0:1, :], pltpu.roll(dv, ts - 1, 0))
        dv2 = jnp.where(row == ts - 1, carry_sc[1:2, :], jnp.where(row == ts - 2, carry_sc[0:1, :], pltpu.roll(dv, ts - 2, 0)))
        du = dv * w2 + dv1 * w1 + dv2 * w0
        dp_ref[:, C_CC:C_CC + W] = (du * ch).astype(BF16)
        dp_ref[:, C_CH:C_CH + W] = (du * cc).astype(BF16)
        dcw_ref[0:1, :] += jnp.sum(dv * u2, axis=0, keepdims=True)
        dcw_ref[1:2, :] += jnp.sum(dv * u1, axis=0, keepdims=True)
        dcw_ref[2:3, :] += jnp.sum(dv * u, axis=0, keepdims=True)
        carry_sc[...] = dv[0:8, :]

        mask = _chunk_mask(ts)
        pos_c = _chunk_pos((HG_CHUNK, HG_F))
        nw = nw_ref[...]

        def block(n, h):
            rows = slice(n * HG_CHUNK, (n + 1) * HG_CHUNK)
            return rows, slice(h * HG_F, (h + 1) * HG_F)

        keep = dict(zip(KEEP_NAMES, keep_scs))

        def gates(rows, h):
            lbh = lb_ref[:, h * HG_F:(h + 1) * HG_F]
            q = p_ref[rows, C_HQ + h * HG_F:C_HQ + (h + 1) * HG_F].astype(F32)
            fl = p_ref[rows, C_HF + h * HG_F:C_HF + (h + 1) * HG_F].astype(F32)
            sig = _sigmoid(fl)
            f = lbh + (1.0 - lbh) * sig
            k = (1.0 - lbh) * _sigmoid(-fl)
            sq = _sigmoid(q)
            qs = q * sq
            bc = _seg_cumsum(jnp.log(f), pos_c)
            bref = jnp.sum(jnp.where(pos_c == mid, bc, 0.0), axis=0, keepdims=True)
            blast = jnp.sum(jnp.where(pos_c == last, bc, 0.0), axis=0, keepdims=True)
            ea, eb, eq, ek = jnp.exp(bc - bref), jnp.exp(bref - bc), jnp.exp(bc), jnp.exp(blast - bc)
            return dict(sq=sq, qs=qs, k=k, sig=sig, f=f, ea=ea, eb=eb, eq=eq, ek=ek), blast

        dnw = jnp.zeros((1, HG_F), F32)
        for n in range(nc):
            for h in range(HG_HEADS):
                rows, hs = block(n, h)
                fw, blast = gates(rows, h)
                for name in KEEP_NAMES:
                    keep[name][rows, hs] = fw[name]
                ab_sc[rows, hs] = (fw["qs"] * fw["ea"]).astype(BF16)
                bkb_sc[rows, hs] = (fw["k"] * fw["eb"]).astype(BF16)
                qinb_sc[rows, hs] = (fw["qs"] * fw["eq"]).astype(BF16)
                koutb_sc[rows, hs] = (fw["k"] * fw["ek"]).astype(BF16)
                dec_sc[n:n + 1, hs] = jnp.exp(blast)
                o = opre_ref[rows, hs]
                g = p_ref[rows, C_HG + h * HG_F:C_HG + (h + 1) * HG_F].astype(F32)
                sg = _sigmoid(g)
                r = lax.rsqrt(jnp.mean(o * o, axis=-1, keepdims=True) + RMS_EPS)
                dyb = dy_ref[rows, W + h * HG_F:W + (h + 1) * HG_F].astype(F32)
                dp_ref[rows, C_HG + h * HG_F:C_HG + (h + 1) * HG_F] = (
                    dyb * (o * r * nw) * (sg * (1.0 + g * (1.0 - sg)))).astype(BF16)
                don = dyb * (g * sg)
                dnw = dnw + jnp.sum(don * o * r, axis=0, keepdims=True)
                dn = don * nw
                dob_sc[rows, hs] = (r * (dn - o * (r * r) * jnp.mean(dn * o, axis=-1, keepdims=True))).astype(BF16)
        dnw_ref[0:1, :] += dnw

        heads = [slice(h * HG_F, (h + 1) * HG_F) for h in range(HG_HEADS)]
        scores = [_dot(ab_sc[:, hs], bkb_sc[:, hs], NT) for hs in heads]
        dscores = [_dot(dob_sc[:, hs], p_ref[:, C_HI + h * HG_F:C_HI + (h + 1) * HG_F].astype(BF16), NT)
                   for h, hs in enumerate(heads)]
        scores = [jnp.where(mask, s, 0.0).astype(BF16) for s in scores]
        dscores = [jnp.where(mask, s, 0.0).astype(BF16) for s in dscores]
        for h, hs in enumerate(heads):
            dv_sc[:, hs] = _dot(scores[h], dob_sc[:, hs], TN)
            da_sc[:, hs] = _dot(dscores[h], bkb_sc[:, hs], NN)
            dbk_sc[:, hs] = _dot(dscores[h], ab_sc[:, hs], TN)
        dst = [dstate_sc[h] for h in range(HG_HEADS)]
        for n in reversed(range(nc)):
            for h in range(HG_HEADS):
                rows, hs = block(n, h)
                st_n = st_ref[n, h]
                decay = dec_sc[n:n + 1, hs]
                dstb = dst[h].astype(BF16)
                dob_n = dob_sc[rows, hs]
                dv_sc[rows, hs] += _dot(koutb_sc[rows, hs], dstb, NT)
                dkout_sc[rows, hs] = _dot(p_ref[rows, C_HI + h * HG_F:C_HI + (h + 1) * HG_F].astype(BF16), dstb, NN)
                ddec_sc[n:n + 1, hs] = jnp.sum(dst[h] * st_n, axis=0, keepdims=True) * decay
                dqin_sc[rows, hs] = _dot(dob_n, st_n.astype(BF16), NN)
                dst[h] = dst[h] * decay + _dot(dob_n, qinb_sc[rows, hs], TN)
        for h in range(HG_HEADS):
            dstate_sc[h] = dst[h]

        for h in range(HG_HEADS):
            dlb = jnp.zeros((1, HG_F), F32)
            for n in range(nc):
                rows, hs = block(n, h)
                fw = {name: keep[name][rows, hs] for name in KEEP_NAMES}
                lbh = lb_ref[:, h * HG_F:(h + 1) * HG_F]
                q = p_ref[rows, C_HQ + h * HG_F:C_HQ + (h + 1) * HG_F].astype(F32)
                da, dbk, dqin, dkout = da_sc[rows, hs], dbk_sc[rows, hs], dqin_sc[rows, hs], dkout_sc[rows, hs]
                w_a, w_b, w_q, w_k = da * fw["ea"], dbk * fw["eb"], dqin * fw["eq"], dkout * fw["ek"]
                dqs, dk = w_a + w_q, w_b + w_k
                t_a, t_b, t_q, t_k = w_a * fw["qs"], w_b * fw["k"], w_q * fw["qs"], w_k * fw["k"]
                s_ref = jnp.sum(t_b - t_a, axis=0, keepdims=True)
                s_last = jnp.sum(t_k, axis=0, keepdims=True) + ddec_sc[n:n + 1, hs]
                dbc = (t_a - t_b + t_q - t_k) + jnp.where(pos_c == mid, s_ref, 0.0) + jnp.where(pos_c == last, s_last, 0.0)
                dfk = _seg_rev_cumsum(dbc, pos_c) / fw["f"] - dk
                sig, sq = fw["sig"], fw["sq"]
                dp_ref[rows, C_HF + h * HG_F:C_HF + (h + 1) * HG_F] = (dfk * (1.0 - lbh) * sig * (1.0 - sig)).astype(BF16)
                dlb = dlb + jnp.sum(dfk * (1.0 - sig), axis=0, keepdims=True)
                dp_ref[rows, C_HQ + h * HG_F:C_HQ + (h + 1) * HG_F] = (dqs * (sq * (1.0 + q * (1.0 - sq)))).astype(BF16)
                dp_ref[rows, C_HI + h * HG_F:C_HI + (h + 1) * HG_F] = dv_sc[rows, hs].astype(BF16)
            dlb_ref[0:1, h * HG_F:(h + 1) * HG_F] += dlb

        mheads = [slice(h * MEM_HEAD_DIM, (h + 1) * MEM_HEAD_DIM) for h in range(MEM_HEADS)]
        qhs = [p_ref[:, C_MQ + h * MEM_HEAD_DIM:C_MQ + (h + 1) * MEM_HEAD_DIM].astype(BF16) for h in range(MEM_HEADS)]
        dobs = [dy_ref[:, 2 * W + h * MEM_HEAD_DIM:2 * W + (h + 1) * MEM_HEAD_DIM].astype(BF16) for h in range(MEM_HEADS)]
        probs = [_attn_probs(qhs[h], mk_ref[:, hs]) for h, hs in enumerate(mheads)]
        dprobs = [_dot(dobs[h], mv_ref[:, hs], NT) for h, hs in enumerate(mheads)]
        for h, hs in enumerate(mheads):
            prob = probs[h]
            dmv_ref[:, hs] += _dot(prob.astype(BF16), dobs[h], TN)
            ds = prob * (dprobs[h] - jnp.sum(dprobs[h] * prob, axis=-1, keepdims=True)) * (MEM_HEAD_DIM ** -0.5)
            dsb = ds.astype(BF16)
            dp_ref[:, C_MQ + h * MEM_HEAD_DIM:C_MQ + (h + 1) * MEM_HEAD_DIM] = _dot(dsb, mk_ref[:, hs], NN).astype(BF16)
            dmk_ref[:, hs] += _dot(dsb, qhs[h], TN)

    def tile(b, s):
        return b * ns + (ns - 1 - s)

    return pl.pallas_call(
        body,
        name="mixer_bwd",
        grid=(bl, ns),
        in_specs=[
            pl.BlockSpec((ts, N_MIX), lambda b, s: (tile(b, s), 0)),
            pl.BlockSpec((PREV_ROWS, N_MIX), lambda b, s: (jnp.maximum(tile(b, s) * (ts // PREV_ROWS) - 1, 0), 0)),
            pl.BlockSpec((ts, 3 * W), lambda b, s: (tile(b, s), 0)),
            pl.BlockSpec(memory_space=pl.ANY),
            pl.BlockSpec((nc, HG_HEADS, HG_F, HG_F), lambda b, s: (tile(b, s), 0, 0, 0)),
            pl.BlockSpec((ts, W), lambda b, s: (tile(b, s), 0)),
            pl.BlockSpec((ml, W), lambda b, s: (b, 0)),
            pl.BlockSpec((ml, W), lambda b, s: (b, 0)),
            pl.BlockSpec((1, W), lambda b, s: (0, 0)),
            pl.BlockSpec((CONV_K, W), lambda b, s: (0, 0)),
            pl.BlockSpec((1, HG_F), lambda b, s: (0, 0)),
        ] + [ANY_SPEC] * len(deps),
        out_specs=[
            pl.BlockSpec((ts, N_MIX), lambda b, s: (tile(b, s), 0)),
            pl.BlockSpec((ml, W), lambda b, s: (b, 0)),
            pl.BlockSpec((ml, W), lambda b, s: (b, 0)),
            pl.BlockSpec((8, W), lambda b, s: (0, 0)),
            pl.BlockSpec((8, HG_F), lambda b, s: (0, 0)),
            pl.BlockSpec((8, W), lambda b, s: (0, 0)),
        ],
        out_shape=[
            jax.ShapeDtypeStruct((T, nin), BF16),
            jax.ShapeDtypeStruct((bl * ml, W), F32),
            jax.ShapeDtypeStruct((bl * ml, W), F32),
            jax.ShapeDtypeStruct((8, W), F32),
            jax.ShapeDtypeStruct((8, HG_F), F32),
            jax.ShapeDtypeStruct((8, W), F32),
        ],
        input_output_aliases={3: 0},
        scratch_shapes=[pltpu.VMEM((HG_HEADS, HG_F, HG_F), F32), pltpu.VMEM((8, W), F32), pltpu.VMEM((PREV_ROWS, W), F32)]
        + [pltpu.VMEM((ts, W), BF16)] * 5 + [pltpu.VMEM((ts, W), F32)] * 5 + [pltpu.VMEM((nc, W), F32)] * 2
        + [pltpu.VMEM((ts, W), F32)] * len(KEEP_NAMES),
        compiler_params=_cparams(("arbitrary", "arbitrary")),
    )(p, p, dy, dp_gates, st, opre, mk, mv, lb, conv_w, norm_w, *deps)


def _layer_norm_stats(z):
    mu = jnp.mean(z, axis=-1, keepdims=True)
    zc = z - mu
    rstd = lax.rsqrt(jnp.mean(zc * zc, axis=-1, keepdims=True) + LN_EPS)
    return zc * rstd, rstd


def _gate_specs(tm, d):
    g0 = N_MIX // d
    return [pl.BlockSpec((tm, d), functools.partial(lambda i, k: (i, g0 + k), k=k)) for k in range(N_BRANCH)]


def _merge_fwd(y, p, x0, wb, wo, bg, ln_g, ln_b, *, alpha, tm=512):
    T, d = x0.shape
    assert N_MIX % d == 0
    tm = _pick(T, (tm, 128, 8))

    def body(y_ref, g0_ref, g1_ref, g2_ref, x_ref, wb_ref, wo_ref, bg_ref, lg_ref, lb_ref, mg_ref, xh_ref, rs_ref, x1b_ref):
        merged = None
        for i, g_ref in enumerate((g0_ref, g1_ref, g2_ref)):
            r = _dot(y_ref[:, i * W:(i + 1) * W], wb_ref[i * W:(i + 1) * W, :], NN)
            t = _sigmoid(g_ref[...].astype(F32) + bg_ref[:, i * d:(i + 1) * d]) * r
            merged = t if merged is None else merged + t
        mb = merged.astype(BF16)
        mg_ref[...] = mb
        z = alpha * x_ref[...] + _dot(mb, wo_ref[...], NN)
        xh, rs = _layer_norm_stats(z)
        xh_ref[...], rs_ref[...] = xh, rs
        x1b_ref[...] = (xh * lg_ref[...] + lb_ref[...]).astype(BF16)

    row = lambda i: (i, 0)
    fix = lambda i: (0, 0)
    return pl.pallas_call(
        body,
        name="merge_fwd",
        grid=(T // tm,),
        in_specs=[pl.BlockSpec((tm, 3 * W), row)] + _gate_specs(tm, d) + [
            pl.BlockSpec((tm, d), row), pl.BlockSpec((3 * W, d), fix, pipeline_mode=pl.Buffered(1)),
            pl.BlockSpec((d, d), fix, pipeline_mode=pl.Buffered(1)), pl.BlockSpec((1, 3 * d), fix),
            pl.BlockSpec((1, d), fix), pl.BlockSpec((1, d), fix)],
        out_specs=[pl.BlockSpec((tm, d), row), pl.BlockSpec((tm, d), row), pl.BlockSpec((tm, 1), row), pl.BlockSpec((tm, d), row)],
        out_shape=[jax.ShapeDtypeStruct((T, d), BF16), jax.ShapeDtypeStruct((T, d), F32), jax.ShapeDtypeStruct((T, 1), F32),
                   jax.ShapeDtypeStruct((T, d), BF16)],
        compiler_params=_cparams(("parallel",)),
    )(y, p, p, p, x0, wb, wo, bg, ln_g, ln_b)


def _merge_bwd(dz, p, y, wb, wo, bg, *, tm=256):
    T, d = dz.shape
    nin = p.shape[1]
    tm = _pick(T, (tm, 128, 8))

    def body(dz_ref, g0_ref, g1_ref, g2_ref, y_ref, wb_ref, wo_ref, bg_ref, dr_ref, dp_ref, dy_ref, dbg_ref):
        @pl.when(pl.program_id(0) == 0)
        def _():
            dbg_ref[...] = jnp.zeros_like(dbg_ref)

        dmerged = _dot(dz_ref[...].astype(BF16), wo_ref[...], NT)
        dp_ref[:, 0:N_MIX] = jnp.zeros((tm, N_MIX), BF16)
        for i, g_ref in enumerate((g0_ref, g1_ref, g2_ref)):
            cs = slice(i * d, (i + 1) * d)
            s = _sigmoid(g_ref[...].astype(F32) + bg_ref[:, cs])
            drb = (dmerged * s).astype(BF16)
            dr_ref[:, cs] = drb
            dgate = dmerged * _dot(y_ref[:, i * W:(i + 1) * W], wb_ref[i * W:(i + 1) * W, :], NN) * s * (1.0 - s)
            dp_ref[:, N_MIX + i * d:N_MIX + (i + 1) * d] = dgate.astype(BF16)
            dbg_ref[0:1, cs] += jnp.sum(dgate, axis=0, keepdims=True)
            dy_ref[:, i * W:(i + 1) * W] = _dot(drb, wb_ref[i * W:(i + 1) * W, :], NT).astype(BF16)

    row = lambda i: (i, 0)
    fix = lambda i: (0, 0)
    return pl.pallas_call(
        body,
        name="merge_bwd",
        grid=(T // tm,),
        in_specs=[pl.BlockSpec((tm, d), row)] + _gate_specs(tm, d) + [
            pl.BlockSpec((tm, 3 * W), row), pl.BlockSpec((3 * W, d), fix, pipeline_mode=pl.Buffered(1)),
            pl.BlockSpec((d, d), fix, pipeline_mode=pl.Buffered(1)), pl.BlockSpec((1, 3 * d), fix)],
        out_specs=[pl.BlockSpec((tm, 3 * d), row), pl.BlockSpec((tm, nin), row), pl.BlockSpec((tm, 3 * W), row),
                   pl.BlockSpec((8, 3 * d), fix)],
        out_shape=[jax.ShapeDtypeStruct((T, 3 * d), BF16), jax.ShapeDtypeStruct((T, nin), BF16),
                   jax.ShapeDtypeStruct((T, 3 * W), BF16), jax.ShapeDtypeStruct((8, 3 * d), F32)],
        compiler_params=_cparams(("arbitrary",)),
    )(dz, p, p, p, y, wb, wo, bg)


def _mlp_fwd(xhat1, x1b, g1, b1, wu, wd, g2, b2, *, alpha, tm=512, tf=2048):
    T, d = xhat1.shape
    ff = wu.shape[1]
    tm, tf = _pick(T, (tm, 256, 128, 8)), _pick(ff, (tf, 1024, 512, 256, 128))
    nf = ff // tf

    def body(xh_ref, x1b_ref, g1_ref, b1_ref, wu_ref, wd_ref, g2_ref, b2_ref, a_ref, xh2_ref, rs2_ref, x2_ref, x2b_ref, acc_ref):
        f = pl.program_id(1)
        a = _dot(x1b_ref[...], wu_ref[...], NN)
        a_ref[...] = a.astype(BF16)
        h = jnp.square(jnp.maximum(a, 0.0))
        part = _dot(h.astype(BF16), wd_ref[...], NN)

        @pl.when(f == 0)
        def _():
            acc_ref[...] = part

        @pl.when(jnp.logical_and(f > 0, f < nf - 1))
        def _():
            acc_ref[...] += part

        @pl.when(f == nf - 1)
        def _():
            x1 = xh_ref[...] * g1_ref[...] + b1_ref[...]
            xh2, rs2 = _layer_norm_stats(alpha * x1 + (acc_ref[...] + part if nf > 1 else part))
            xh2_ref[...] = xh2
            rs2_ref[...] = rs2
            x2 = xh2 * g2_ref[...] + b2_ref[...]
            x2_ref[...] = x2
            x2b_ref[...] = x2.astype(BF16)

    row = lambda i, f: (i, 0)
    fix = lambda i, f: (0, 0)
    return pl.pallas_call(
        body,
        name="mlp_fwd",
        grid=(T // tm, nf),
        in_specs=[pl.BlockSpec((tm, d), row), pl.BlockSpec((tm, d), row), pl.BlockSpec((1, d), fix), pl.BlockSpec((1, d), fix),
                  pl.BlockSpec((d, tf), lambda i, f: (0, f)), pl.BlockSpec((tf, d), lambda i, f: (f, 0)),
                  pl.BlockSpec((1, d), fix), pl.BlockSpec((1, d), fix)],
        out_specs=[pl.BlockSpec((tm, tf), lambda i, f: (i, f)), pl.BlockSpec((tm, d), row), pl.BlockSpec((tm, 1), row),
                   pl.BlockSpec((tm, d), row), pl.BlockSpec((tm, d), row)],
        out_shape=[jax.ShapeDtypeStruct((T, ff), BF16), jax.ShapeDtypeStruct((T, d), F32), jax.ShapeDtypeStruct((T, 1), F32),
                   jax.ShapeDtypeStruct((T, d), F32), jax.ShapeDtypeStruct((T, d), BF16)],
        scratch_shapes=[pltpu.VMEM((tm, d), F32)],
        compiler_params=_cparams(("parallel", "arbitrary")),
    )(xhat1, x1b, g1, b1, wu, wd, g2, b2)


def _ln_bwd(dy, xhat, rstd, g, *, tm=512, deps=()):
    T, d = dy.shape
    tm = _pick(T, (tm, 256, 128, 8))

    def body(dy_ref, xh_ref, rs_ref, g_ref, *rest):
        dz_ref, dzb_ref, dg_ref, db_ref = rest[len(deps):]

        @pl.when(pl.program_id(0) == 0)
        def _():
            dg_ref[...] = jnp.zeros_like(dg_ref)
            db_ref[...] = jnp.zeros_like(db_ref)

        dy_, xh = dy_ref[...], xh_ref[...]
        dg_ref[0:1, :] += jnp.sum(dy_ * xh, axis=0, keepdims=True)
        db_ref[0:1, :] += jnp.sum(dy_, axis=0, keepdims=True)
        dxh = dy_ * g_ref[...]
        dz = rs_ref[...] * (dxh - jnp.mean(dxh, axis=-1, keepdims=True) - xh * jnp.mean(dxh * xh, axis=-1, keepdims=True))
        dz_ref[...] = dz
        dzb_ref[...] = dz.astype(BF16)

    row = lambda i: (i, 0)
    fix = lambda i: (0, 0)
    return pl.pallas_call(
        body,
        name="ln_bwd",
        grid=(T // tm,),
        in_specs=[pl.BlockSpec((tm, d), row), pl.BlockSpec((tm, d), row), pl.BlockSpec((tm, 1), row), pl.BlockSpec((1, d), fix)]
        + [ANY_SPEC] * len(deps),
        out_specs=[pl.BlockSpec((tm, d), row), pl.BlockSpec((tm, d), row), pl.BlockSpec((8, d), fix), pl.BlockSpec((8, d), fix)],
        out_shape=[jax.ShapeDtypeStruct((T, d), F32), jax.ShapeDtypeStruct((T, d), BF16), jax.ShapeDtypeStruct((8, d), F32),
                   jax.ShapeDtypeStruct((8, d), F32)],
        compiler_params=_cparams(("arbitrary",)),
    )(dy, xhat, rstd, g, *deps)


def _loss_head(y, target, *, tm=512):
    T, d = y.shape
    tm = _pick(T, (tm, 256, 128, 8))
    n = T // tm

    def body(y_ref, t_ref, loss_ref, dy_ref, acc_ref):
        i = pl.program_id(0)

        @pl.when(i == 0)
        def _():
            acc_ref[...] = jnp.zeros_like(acc_ref)

        e = y_ref[...] - t_ref[...]
        dy_ref[...] = e * (1.0 / d)
        acc_ref[...] += jnp.sum(e * e, axis=0, keepdims=True)

        @pl.when(i == n - 1)
        def _():
            loss_ref[...] = (0.5 / d) * jnp.sum(acc_ref[...], axis=1, keepdims=True)

    row = lambda i: (i, 0)
    return pl.pallas_call(
        body,
        name="loss_head",
        grid=(n,),
        in_specs=[pl.BlockSpec((tm, d), row), pl.BlockSpec((tm, d), row)],
        out_specs=[pl.BlockSpec((1, 1), lambda i: (0, 0)), pl.BlockSpec((tm, d), row)],
        out_shape=[jax.ShapeDtypeStruct((1, 1), F32), jax.ShapeDtypeStruct((T, d), F32)],
        scratch_shapes=[pltpu.VMEM((1, d), F32)],
        compiler_params=_cparams(("arbitrary",)),
    )(y, target)


def _lower_bounds_fwd(lower_bounds):
    depth, n = lower_bounds.shape

    def body(x_ref, soft_ref, lb_ref):
        x = x_ref[...]
        e = jnp.exp(x - jnp.max(x, axis=0, keepdims=True))
        soft_ref[...] = e / jnp.sum(e, axis=0, keepdims=True)
        run = None
        for l in range(depth):
            run = soft_ref[l:l + 1, :] if run is None else run + soft_ref[l:l + 1, :]
            lb_ref[l:l + 1, :] = run - soft_ref[0:1, :]

    return pl.pallas_call(body, name="lower_bounds_fwd",
                          out_shape=[jax.ShapeDtypeStruct((depth, n), F32), jax.ShapeDtypeStruct((depth, n), F32)])(lower_bounds)


def _lower_bounds_bwd(soft, dlb):
    depth, n = soft.shape

    def body(soft_ref, dlb_ref, out_ref, dsoft_ref):
        total = jnp.sum(dlb_ref[...], axis=0, keepdims=True)
        run = None
        for l in reversed(range(depth)):
            run = dlb_ref[l:l + 1, :] if run is None else run + dlb_ref[l:l + 1, :]
            dsoft_ref[l:l + 1, :] = run - total if l == 0 else run
        s, ds = soft_ref[...], dsoft_ref[...]
        out_ref[...] = s * (ds - jnp.sum(s * ds, axis=0, keepdims=True))

    return pl.pallas_call(body, name="lower_bounds_bwd", out_shape=jax.ShapeDtypeStruct((depth, n), F32),
                          scratch_shapes=[pltpu.VMEM((depth, n), F32)])(soft, dlb)


def _layer_fwd(x0, x0b, mem2, lb, w_in, rest_fn, *, bl, seq, alpha, deps=()):
    p = _matmul("proj_in", x0b, w_in, mode="nn", out_dtype=BF16, deps=deps, tm=1024, tn=1792)
    wts = dict(rest_fn(p), w_in=w_in)
    mk = _matmul("mem_k", mem2, wts["w_mem_k"], mode="nn", out_dtype=BF16)
    mv = _matmul("mem_v", mem2, wts["w_mem_v"], mode="nn", out_dtype=BF16)
    y, st, opre = _mixer_fwd(p, mk, mv, lb, wts["conv_w"], wts["hg_norm_w"], bl=bl, seq=seq)
    merged, xhat1, rstd1, x1b = _merge_fwd(y, p, x0, wts["w_branch"], wts["w_o"], wts["b_gate"], wts["ln1_g"], wts["ln1_b"],
                                           alpha=alpha)
    a, xhat2, rstd2, x2, x2b = _mlp_fwd(xhat1, x1b, wts["ln1_g"], wts["ln1_b"], wts["w_up"], wts["w_down"], wts["ln2_g"],
                                        wts["ln2_b"], alpha=alpha)
    saved = dict(x0b=x0b, p=p, mk=mk, mv=mv, y=y, st=st, opre=opre, merged=merged, xhat1=xhat1, rstd1=rstd1, x1b=x1b, a=a,
                 xhat2=xhat2, rstd2=rstd2)
    return x2, x2b, saved, wts


def _relu2_bf16(a):
    return jnp.square(jnp.maximum(a.astype(F32), 0.0)).astype(BF16)


def _mlp_bwd(dz2, dz2b, sv, wts, *, alpha, deps=()):
    g = {}
    da = _matmul("mlp_da", dz2b, wts["w_down"], mode="nt", out_dtype=BF16, tm=1024, deps=deps,
                 epi_fn=lambda acc, a: (acc * (2.0 * jnp.maximum(a.astype(F32), 0.0)),), epi_extra=(sv["a"],))
    g["w_down"] = _matmul_tn("grad_w_down", sv["a"], dz2b, a_fn=_relu2_bf16, out_dtype=BF16, tt=2048)
    g["w_up"] = _matmul_tn("grad_w_up", sv["x1b"], da, out_dtype=BF16, tt=2048)
    dx1 = _matmul("mlp_dx", da, wts["w_up"], mode="nt", epi_fn=lambda acc, dz: (acc + alpha * dz,), epi_extra=(dz2,),
                  tm=512, tk=4096)
    dz1, dz1b, dg1, db1 = _ln_bwd(dx1, sv["xhat1"], sv["rstd1"], wts["ln1_g"])
    g["ln1_g"], g["ln1_b"] = dg1[0:1], db1[0:1]
    return dz1, dz1b, g


def _mix_bwd(dz1, dz1b, sv, mem2, lb, wts, *, bl, seq, alpha, send, below=None, deps=()):
    d = dz1.shape[1]
    g = {}
    g["w_o"] = _matmul_tn("grad_w_o", sv["merged"], dz1b, out_dtype=BF16, tt=2048, deps=deps)
    dr, dp, dy, dbg = _merge_bwd(dz1b, sv["p"], sv["y"], wts["w_branch"], wts["w_o"], wts["b_gate"])
    g["b_gate"] = dbg[0:1]
    g["w_branch"] = jnp.concatenate(
        [_matmul_tn("grad_w_branch", sv["y"], dr, a_cols=(i * W, W), b_cols=(i * d, d), out_dtype=BF16) for i in range(N_BRANCH)],
        axis=0)
    token = send(("w_o", "w_branch"), g)
    dp, dmk, dmv, dcw, dnw, dlb = _mixer_bwd(sv["p"], dy, dp, sv["st"], sv["opre"], sv["mk"], sv["mv"], lb,
                                              wts["conv_w"], wts["hg_norm_w"], bl=bl, seq=seq, deps=(token,))
    g["conv_w"], g["hg_norm_w"], g["lb"] = dcw[0:CONV_K], dnw[0:1], dlb[0:1]
    g["w_mem_k"] = _matmul_tn("grad_w_mem_k", mem2, dmk, out_dtype=BF16)
    g["w_mem_v"] = _matmul_tn("grad_w_mem_v", mem2, dmv, out_dtype=BF16)
    g["w_in"] = _matmul_tn("grad_w_in", sv["x0b"], dp, out_dtype=BF16, tt=2048)
    token = send(("w_in", "w_mem_k", "w_mem_v", "conv_w"), g)
    dx0 = _matmul("proj_in_dx", dp, wts["w_in"], mode="nt", epi_fn=lambda acc, dz: (acc + alpha * dz,), epi_extra=(dz1,),
                  tm=512, tk=dp.shape[1], deps=(token,))
    return (dx0 if below is None else _ln_bwd(dx0, *below)), g


N_CHIPS = 4
MESH_IDS = pl.DeviceIdType.MESH


def _axis_slice(ref, axis, start, size):
    idx = [slice(None)] * len(ref.shape)
    idx[axis] = pl.ds(start, size)
    return ref.at[tuple(idx)]


def _chip_exchange(name, items):
    n = len(items)
    out_shapes, meta = [], []
    for arr, kind, axis in items:
        shp = list(arr.shape)
        if kind == "gather":
            per = shp[axis]
            shp[axis] = per * N_CHIPS
            out_shapes.append(jax.ShapeDtypeStruct(tuple(shp), arr.dtype))
        elif kind == "scatter":
            per = shp[axis] // N_CHIPS
            shp[axis] = per
            out_shapes.append(jax.ShapeDtypeStruct((N_CHIPS, *shp), arr.dtype))
        else:
            per = None
            out_shapes.append(jax.ShapeDtypeStruct((N_CHIPS, *shp), arr.dtype))
        meta.append((kind, axis, per))

    def body(*refs):
        ins, outs = refs[:n], refs[n:2 * n]
        send_sems, recv_sems, local_sems = refs[2 * n:]
        x, y, c = lax.axis_index("x"), lax.axis_index("y"), lax.axis_index("c")
        me = 2 * x + y
        peers = [(1 - x, y), (x, 1 - y), (1 - x, 1 - y)]

        def src_for(t, chip):
            kind, axis, per = meta[t]
            return _axis_slice(ins[t], axis, chip * per, per) if kind == "scatter" else ins[t]

        def dst_from(t, chip):
            kind, axis, per = meta[t]
            return _axis_slice(outs[t], axis, chip * per, per) if kind == "gather" else outs[t].at[chip]

        def remote(t, k):
            px, py = peers[k]
            return pltpu.make_async_remote_copy(
                src_ref=src_for(t, 2 * px + py), dst_ref=dst_from(t, me), send_sem=send_sems.at[t * 3 + k],
                recv_sem=recv_sems.at[t * 3 + k], device_id=(px, py, c), device_id_type=MESH_IDS)

        def arrival(t, k):
            px, py = peers[k]
            return pltpu.make_async_remote_copy(
                src_ref=src_for(t, me), dst_ref=dst_from(t, 2 * px + py), send_sem=send_sems.at[t * 3 + k],
                recv_sem=recv_sems.at[t * 3 + k], device_id=(px, py, c), device_id_type=MESH_IDS)

        sends = [remote(t, k) for t in range(n) for k in range(3)]
        for cp in sends:
            cp.start()
        own = [pltpu.make_async_copy(src_for(t, me), dst_from(t, me), local_sems.at[t]) for t in range(n)]
        for cp in own:
            cp.start()
        for t in range(n):
            for k in range(3):
                arrival(t, k).wait_recv()
        for cp in sends:
            cp.wait_send()
        for cp in own:
            cp.wait()

    any_spec = pl.BlockSpec(memory_space=pl.ANY)
    return pl.pallas_call(
        body,
        name=name,
        in_specs=[any_spec] * n,
        out_specs=[any_spec] * n,
        out_shape=out_shapes,
        scratch_shapes=[pltpu.SemaphoreType.DMA((3 * n,)), pltpu.SemaphoreType.DMA((3 * n,)), pltpu.SemaphoreType.DMA((n,))],
        compiler_params=pltpu.CompilerParams(has_side_effects=True),
    )(*[a for a, _, _ in items])


HBM_SPEC = pl.BlockSpec(memory_space=pltpu.HBM)
SEM_SPEC = pl.BlockSpec(memory_space=pltpu.SEMAPHORE)
N_PEERS = N_CHIPS - 1


def _my_chip():
    return (2 * lax.axis_index("x") + lax.axis_index("y")).astype(jnp.int32).reshape(1)


def _own_block_spec(r, c, axis, tr):
    if axis == 1:
        return pl.BlockSpec((tr, c), lambda i, me: (i, me[0]))
    return pl.BlockSpec((tr, c), lambda i, me: (me[0] * (r // tr) + i, 0))


def _place_shard(name, shard, axis, me):
    r, c = shard.shape
    tr = _row_block(r, c, shard.dtype.itemsize)
    shp = (r, c * N_CHIPS) if axis == 1 else (r * N_CHIPS, c)

    def body(me_ref, s_ref, o_ref):
        del me_ref
        o_ref[...] = s_ref[...]

    return pl.pallas_call(
        body, name=name,
        grid_spec=pltpu.PrefetchScalarGridSpec(
            num_scalar_prefetch=1, grid=(r // tr,),
            in_specs=[pl.BlockSpec((tr, c), lambda i, me: (i, 0))], out_specs=_own_block_spec(r, c, axis, tr)),
        out_shape=jax.ShapeDtypeStruct(shp, shard.dtype),
        compiler_params=_cparams(("parallel",)),
    )(me, shard)


class _Split:
    def __init__(self, name, items):
        self.name, self.n = name, len(items)
        self.srcs = [a for a, _, _ in items]
        self.meta, self.land_shapes = [], []
        for arr, kind, axis in items:
            shp = list(arr.shape)
            if kind == "gather":
                per = shp[axis]
                shp[axis] = per * N_CHIPS
                self.land_shapes.append(jax.ShapeDtypeStruct(tuple(shp), arr.dtype))
            else:
                per = shp[axis] // N_CHIPS
                shp[axis] = per
                self.land_shapes.append(jax.ShapeDtypeStruct((N_PEERS, *shp), arr.dtype))
            self.meta.append((kind, axis, per))

    def _src(self, ins, t, chip):
        kind, axis, per = self.meta[t]
        return _axis_slice(ins[t], axis, chip * per, per) if kind == "scatter" else ins[t]

    def _dst(self, lands, t, chip, slot):
        kind, axis, per = self.meta[t]
        return _axis_slice(lands[t], axis, chip * per, per) if kind == "gather" else lands[t].at[slot]

    def landing_zones(self, me):
        return [_place_shard(self.name + "_own", src, axis, me) if kind == "gather" else lax.empty(ls.shape, ls.dtype)
                for src, ls, (kind, axis, _) in zip(self.srcs, self.land_shapes, self.meta)]

    def _copies(self, ins, lands, send_sems, recv_sems, arrivals):
        x, y, c = lax.axis_index("x"), lax.axis_index("y"), lax.axis_index("c")
        me = 2 * x + y
        peers = [(1 - x, y), (x, 1 - y), (1 - x, 1 - y)]
        res = []
        for t in range(self.n):
            for k, (px, py) in enumerate(peers):
                theirs = 2 * px + py
                sems = dict(send_sem=send_sems.at[t * N_PEERS + k], recv_sem=recv_sems.at[t * N_PEERS + k],
                            device_id=(px, py, c), device_id_type=MESH_IDS)
                if arrivals:
                    res.append(pltpu.make_async_remote_copy(src_ref=self._src(ins, t, me), dst_ref=self._dst(lands, t, theirs, k), **sems))
                else:
                    res.append(pltpu.make_async_remote_copy(src_ref=self._src(ins, t, theirs), dst_ref=self._dst(lands, t, me, k), **sems))
        return res

    def start(self, lands, deps=()):
        n, nd = self.n, len(deps)

        def body(*refs):
            ins, lnd = refs[:n], refs[n:2 * n]
            send_sems, recv_sems = refs[2 * n + nd], refs[2 * n + nd + 1]
            token = refs[-1]
            for cp in self._copies(ins, lnd, send_sems, recv_sems, arrivals=False):
                cp.start()
            token[...] = jnp.zeros_like(token)

        hbm = lambda a: pltpu.HBM(a.shape, a.dtype)
        res = pl.pallas_call(
            body, name=self.name + "_start",
            in_specs=[HBM_SPEC] * (2 * n) + [ANY_SPEC] * nd,
            out_specs=[SEM_SPEC, SEM_SPEC] + [HBM_SPEC] * (2 * n) + [pl.BlockSpec(memory_space=pltpu.VMEM)],
            out_shape=[pltpu.SemaphoreType.DMA((N_PEERS * n,)), pltpu.SemaphoreType.DMA((N_PEERS * n,))]
            + [hbm(a) for a in self.srcs] + [hbm(a) for a in self.land_shapes] + [jax.ShapeDtypeStruct((8, 128), F32)],
            input_output_aliases={i: 2 + i for i in range(2 * n)},
            compiler_params=pltpu.CompilerParams(has_side_effects=pltpu.SideEffectType.DATAFLOW_SIDE_EFFECTING),
        )(*[pltpu.with_memory_space_constraint(a, pltpu.HBM) for a in self.srcs],
          *[pltpu.with_memory_space_constraint(a, pltpu.HBM) for a in lands], *deps)
        return res[:-1], res[-1]

    def wait(self, state, after):
        n = self.n
        send_sems, recv_sems = state[0], state[1]
        srcs, lands = state[2:2 + n], state[2 + n:2 + 2 * n]

        def body(*refs):
            ins, lnd = refs[:n], refs[n:2 * n]
            s_sems, r_sems = refs[2 * n], refs[2 * n + 1]
            for cp in self._copies(ins, lnd, s_sems, r_sems, arrivals=True):
                cp.wait_recv()
            for cp in self._copies(ins, lnd, s_sems, r_sems, arrivals=False):
                cp.wait_send()

        hbm = lambda a: pltpu.HBM(a.shape, a.dtype)
        res = pl.pallas_call(
            body, name=self.name + "_wait",
            in_specs=[HBM_SPEC] * (2 * n) + [SEM_SPEC, SEM_SPEC, ANY_SPEC],
            out_specs=[HBM_SPEC] * (2 * n),
            out_shape=[hbm(a) for a in self.srcs] + [hbm(a) for a in self.land_shapes],
            input_output_aliases={i: i for i in range(2 * n)},
            compiler_params=pltpu.CompilerParams(has_side_effects=pltpu.SideEffectType.DATAFLOW_SIDE_EFFECTING),
        )(*srcs, *lands, send_sems, recv_sems, after)
        return res[:n], res[n:]


def _sibling_swap(name, arrays):
    n = len(arrays)

    def body(*refs):
        ins, outs = refs[:n], refs[n:2 * n]
        send_sems, recv_sems = refs[2 * n:]
        sibling = (lax.axis_index("x"), lax.axis_index("y"), 1 - lax.axis_index("c"))
        copies = [pltpu.make_async_remote_copy(src_ref=ins[t], dst_ref=outs[t], send_sem=send_sems.at[t], recv_sem=recv_sems.at[t],
                                               device_id=sibling, device_id_type=MESH_IDS) for t in range(n)]
        for cp in copies:
            cp.start()
        for cp in copies:
            cp.wait()

    any_spec = pl.BlockSpec(memory_space=pl.ANY)
    return pl.pallas_call(
        body,
        name=name,
        in_specs=[any_spec] * n,
        out_specs=[any_spec] * n,
        out_shape=[jax.ShapeDtypeStruct(a.shape, a.dtype) for a in arrays],
        scratch_shapes=[pltpu.SemaphoreType.DMA((n,)), pltpu.SemaphoreType.DMA((n,))],
        compiler_params=pltpu.CompilerParams(has_side_effects=True),
    )(*arrays)


def _row_block(r, c, itemsize=4, target=1 << 20):
    if r % 8 != 0:
        return r
    best = 8
    for tr in range(8, r + 1, 8):
        if r % tr == 0 and tr * c * itemsize <= target:
            best = tr
    return best


def _sum_chips_into(parts, stacked, layer):
    _, r, c = parts.shape
    tr = _row_block(r, c)

    def body(p_ref, s_ref, o_ref):
        del s_ref
        o_ref[...] = ((p_ref[0] + p_ref[1]) + p_ref[2]) + p_ref[3]

    return pl.pallas_call(
        body,
        name="sum_chips",
        grid=(r // tr,),
        in_specs=[pl.BlockSpec((N_CHIPS, tr, c), lambda i: (0, i, 0)), pl.BlockSpec(memory_space=pl.ANY)],
        out_specs=pl.BlockSpec((None, tr, c), lambda i: (layer, i, 0)),
        out_shape=jax.ShapeDtypeStruct(stacked.shape, stacked.dtype),
        input_output_aliases={1: 0},
        compiler_params=_cparams(("parallel",)),
    )(parts, stacked)


def _sum_own_and_peers(me, g, axis, landed):
    _, r, c = landed.shape
    tr = _row_block(r, c)

    def body(me_ref, g_ref, p_ref, o_ref):
        del me_ref
        o_ref[...] = ((g_ref[...].astype(F32) + p_ref[0].astype(F32)) + p_ref[1].astype(F32)) + p_ref[2].astype(F32)

    return pl.pallas_call(
        body, name="sum_chips_own",
        grid_spec=pltpu.PrefetchScalarGridSpec(
            num_scalar_prefetch=1, grid=(r // tr,),
            in_specs=[_own_block_spec(r, c, axis, tr), pl.BlockSpec((N_PEERS, tr, c), lambda i, me: (0, i, 0))],
            out_specs=pl.BlockSpec((tr, c), lambda i, me: (i, 0))),
        out_shape=jax.ShapeDtypeStruct((r, c), F32),
        compiler_params=_cparams(("parallel",)),
    )(me, g, landed)


def _adamw_math(w, m, v, g):
    m_new = ADAM_B1 * m + (1.0 - ADAM_B1) * g
    v_new = ADAM_B2 * v + (1.0 - ADAM_B2) * jnp.square(g)
    m_hat = m_new / (1.0 - ADAM_B1 ** ADAM_STEP)
    v_hat = v_new / (1.0 - ADAM_B2 ** ADAM_STEP)
    return -ADAM_LR * (m_hat / (jnp.sqrt(v_hat) + ADAM_EPS) + ADAM_WD * w), m_new, v_new


def _adamw(w, m, v, g_a, g_b):
    L, r, c = w.shape
    tr = _row_block(r, c, target=1 << 19)

    def body(w_ref, m_ref, v_ref, ga_ref, gb_ref, g_ref, d_ref, nm_ref, nv_ref):
        g = ga_ref[...] + gb_ref[...]
        g_ref[...] = g
        d_ref[...], nm_ref[...], nv_ref[...] = _adamw_math(w_ref[...], m_ref[...], v_ref[...], g)

    spec = pl.BlockSpec((None, tr, c), lambda l, i: (l, i, 0))
    return pl.pallas_call(
        body,
        name="adamw",
        grid=(L, r // tr),
        in_specs=[spec] * 5,
        out_specs=[spec] * 4,
        out_shape=[jax.ShapeDtypeStruct(w.shape, F32)] * 4,
        compiler_params=_cparams(("parallel", "parallel")),
    )(w, m, v, g_a, g_b)


def _adamw_layer(w, m, v, g_a, g_b, layer, outs):
    L, r, c = w.shape
    tr = _row_block(r, c, target=1 << 19)
    n_prev = 0 if outs is None else 4

    def body(w_ref, m_ref, v_ref, ga_ref, gb_ref, *rest):
        g_ref, d_ref, nm_ref, nv_ref = rest[n_prev:]
        g = ga_ref[...] + gb_ref[...]
        g_ref[...] = g
        d_ref[...], nm_ref[...], nv_ref[...] = _adamw_math(w_ref[...], m_ref[...], v_ref[...], g)

    at_layer = pl.BlockSpec((None, tr, c), lambda i: (layer, i, 0))
    flat = pl.BlockSpec((tr, c), lambda i: (i, 0))
    return pl.pallas_call(
        body,
        name="adamw_layer",
        grid=(r // tr,),
        in_specs=[at_layer] * 3 + [flat] * 2 + [ANY_SPEC] * n_prev,
        out_specs=[at_layer] * 4,
        out_shape=[jax.ShapeDtypeStruct(w.shape, F32)] * 4,
        input_output_aliases={5 + k: k for k in range(n_prev)},
        compiler_params=_cparams(("parallel",)),
    )(w, m, v, g_a, g_b, *(outs or ()))


SHARDED = (("w_in", 1), ("conv_w", 1), ("w_mem_k", 0), ("w_mem_v", 0), ("w_branch", 1), ("w_o", 0), ("w_up", 1), ("w_down", 0))
SMALL = ("lower_bounds", "hg_norm_w", "b_gate", "ln1_g", "ln1_b", "ln2_g", "ln2_b")
WEIGHT_ORDER = ("lower_bounds", "w_in", "conv_w", "hg_norm_w", "w_mem_k", "w_mem_v", "w_branch", "b_gate", "w_o", "ln1_g", "ln1_b",
                "w_up", "w_down", "ln2_g", "ln2_b")


def kernel(x, mem, lower_bounds, w_in, conv_w, hg_norm_w, w_mem_k, w_mem_v, w_branch, b_gate, w_o, ln1_g, ln1_b, w_up, w_down, ln2_g, ln2_b, loss_target, m_lower_bounds, m_w_in, m_conv_w, m_hg_norm_w, m_w_mem_k, m_w_mem_v, m_w_branch, m_b_gate, m_w_o, m_ln1_g, m_ln1_b, m_w_up, m_w_down, m_ln2_g, m_ln2_b, v_lower_bounds, v_w_in, v_conv_w, v_hg_norm_w, v_w_mem_k, v_w_mem_v, v_w_branch, v_b_gate, v_w_o, v_ln1_g, v_ln1_b, v_w_up, v_w_down, v_ln2_g, v_ln2_b):
    bl, seq, d = x.shape
    depth = w_in.shape[0]
    weights = dict(lower_bounds=lower_bounds, w_in=w_in, conv_w=conv_w, hg_norm_w=hg_norm_w, w_mem_k=w_mem_k, w_mem_v=w_mem_v,
                   w_branch=w_branch, b_gate=b_gate, w_o=w_o, ln1_g=ln1_g, ln1_b=ln1_b, w_up=w_up, w_down=w_down, ln2_g=ln2_g, ln2_b=ln2_b)
    mom_m = dict(lower_bounds=m_lower_bounds, w_in=m_w_in, conv_w=m_conv_w, hg_norm_w=m_hg_norm_w, w_mem_k=m_w_mem_k, w_mem_v=m_w_mem_v,
                 w_branch=m_w_branch, b_gate=m_b_gate, w_o=m_w_o, ln1_g=m_ln1_g, ln1_b=m_ln1_b, w_up=m_w_up, w_down=m_w_down,
                 ln2_g=m_ln2_g, ln2_b=m_ln2_b)
    mom_v = dict(lower_bounds=v_lower_bounds, w_in=v_w_in, conv_w=v_conv_w, hg_norm_w=v_hg_norm_w, w_mem_k=v_w_mem_k, w_mem_v=v_w_mem_v,
                 w_branch=v_w_branch, b_gate=v_b_gate, w_o=v_w_o, ln1_g=v_ln1_g, ln1_b=v_ln1_b, w_up=v_w_up, w_down=v_w_down,
                 ln2_g=v_ln2_g, ln2_b=v_ln2_b)

    def shard2d(name, l):
        w = weights[name][l]
        if name == "w_branch":
            return w.reshape(N_BRANCH * W, w.shape[-1]).astype(BF16)
        return w if name == "conv_w" else w.astype(BF16)

    me = _my_chip()

    shard_axis = dict(SHARDED)

    def start_exchange(name, kind, items, deps=()):
        ex = _Split(name, [(arr, kind, shard_axis[nm]) for nm, arr in items])
        state, token = ex.start(ex.landing_zones(me), deps)
        return ex, state, [nm for nm, _ in items], token

    def start_gathers(l, deps=()):
        first = start_exchange(f"gather_in_l{l}", "gather", [("w_in", shard2d("w_in", l))], deps)
        rest = start_exchange(f"gather_rest_l{l}", "gather", [(nm, shard2d(nm, l)) for nm, _ in SHARDED if nm != "w_in"],
                              (first[3],))
        return first, rest

    def gathered(pend, after):
        ex, state, names, _ = pend
        return dict(zip(names, ex.wait(state, after=after)[1]))

    x2d, mem2, t2d = x.reshape(bl * seq, d), mem.reshape(-1, d), loss_target.reshape(bl * seq, d)
    alpha = (2.0 * depth) ** 0.25
    soft, lb_all = _lower_bounds_fwd(lower_bounds)

    h, hb, saved, layer_wts = x2d, x2d.astype(BF16), [], []
    pending = start_gathers(0)
    for l in range(depth):
        first, rest = pending
        w_in_l = gathered(first, h)["w_in"]

        def rest_fn(after, l=l, rest=rest):
            wts = gathered(rest, after)
            for name in ("hg_norm_w", "b_gate", "ln1_g", "ln1_b", "ln2_g", "ln2_b"):
                wts[name] = weights[name][l][None, :]
            return wts

        deps = (rest[3],)
        if l + 1 < depth:
            pending = start_gathers(l + 1, (w_in_l, rest[3]))
            deps += (pending[0][3], pending[1][3])
        h, hb, sv, wts = _layer_fwd(h, hb, mem2, lb_all[l:l + 1], w_in_l, rest_fn, bl=bl, seq=seq, alpha=alpha, deps=deps)
        saved.append(sv)
        layer_wts.append(wts)
    loss, dh = _loss_head(h, t2d)

    shape3 = {name: (depth, weights[name].size // (depth * weights[name].shape[-1]), weights[name].shape[-1]) for name, _ in SHARDED}
    partial = [dict() for _ in range(depth)]
    smalls = [None] * depth
    outs = {name: None for name, _ in SHARDED}

    def finish_reduce(pend, l, after):
        ex, state, names, _ = pend
        sent, got = ex.wait(state, after=after)
        for nm, g_full, landed in zip(names, sent, got):
            partial[l][nm] = _sum_own_and_peers(me, g_full, shard_axis[nm], landed)

    def optimizer_step(l):
        names = [name for name, _ in SHARDED]
        theirs = _sibling_swap(f"swap_partials_l{l}", [partial[l][nm] for nm in names])
        for nm, other in zip(names, theirs):
            outs[nm] = _adamw_layer(weights[nm].reshape(shape3[nm]), mom_m[nm].reshape(shape3[nm]), mom_v[nm].reshape(shape3[nm]),
                                    partial[l][nm], other, l, outs[nm])
        return tuple(outs[nm][0] for nm in names)

    pending_mix, deps = [], ()
    dz2, dz2b, dg2, db2 = _ln_bwd(dh, saved[-1]["xhat2"], saved[-1]["rstd2"], layer_wts[-1]["ln2_g"])
    for l in reversed(range(depth)):
        dz1, dz1b, g_mlp = _mlp_bwd(dz2, dz2b, saved[l], layer_wts[l], alpha=alpha, deps=deps)
        g_mlp["ln2_g"], g_mlp["ln2_b"] = dg2[0:1], db2[0:1]
        pending_mlp = start_exchange(f"reduce_mlp_l{l}", "scatter", [(nm, g_mlp[nm]) for nm in ("w_up", "w_down")])
        deps = (pending_mlp[3],)
        if pending_mix:
            for pend in pending_mix:
                finish_reduce(pend, l + 1, dz1)
            deps += optimizer_step(l + 1)
        pending_mix = []

        def send(names, g, l=l, pending_mix=pending_mix):
            pend = start_exchange(f"reduce_{names[0]}_l{l}", "scatter", [(nm, g[nm]) for nm in names])
            pending_mix.append(pend)
            return pend[3]

        below = (saved[l - 1]["xhat2"], saved[l - 1]["rstd2"], layer_wts[l - 1]["ln2_g"]) if l > 0 else None
        out, g = _mix_bwd(dz1, dz1b, saved[l], mem2, lb_all[l:l + 1], layer_wts[l], bl=bl, seq=seq, alpha=alpha, send=send,
                          below=below, deps=deps)
        if l > 0:
            dz2, dz2b, dg2, db2 = out
        else:
            dh = out
        finish_reduce(pending_mlp, l, out[0] if l > 0 else out)
        deps = ()
        g.update(g_mlp, lower_bounds=g["lb"])
        smalls[l] = jnp.concatenate([g[nm] for nm in SMALL], axis=1)
    small_parts = _chip_exchange("reduce_small", [(jnp.stack(smalls), "bcast", 0)])[0]
    small_sum = _sum_chips_into(small_parts.reshape(N_CHIPS, depth, -1), jnp.zeros((1, depth, small_parts.shape[-1]), F32), 0)
    small_sum = small_sum.reshape(depth, 1, -1)
    small_theirs = _sibling_swap("swap_small", [small_sum])[0]
    for pend in pending_mix:
        finish_reduce(pend, 0, small_theirs)
    optimizer_step(0)

    outs = {name: [r.reshape(weights[name].shape) for r in res] for name, res in outs.items()}
    off = 0
    for name in SMALL:
        n = weights[name].shape[1]
        mine, other = small_sum[:, :, off:off + n], small_theirs[:, :, off:off + n]
        off += n
        if name == "lower_bounds":
            mine = _lower_bounds_bwd(soft, mine[:, 0, :])[:, None, :]
            other = _lower_bounds_bwd(soft, other[:, 0, :])[:, None, :]
        shp = (depth, 1, n)
        res = _adamw(weights[name].reshape(shp), mom_m[name].reshape(shp), mom_v[name].reshape(shp), mine, other)
        outs[name] = [r.reshape(weights[name].shape) for r in res]
    assert off == small_sum.shape[-1]

    total_loss = lax.psum(loss[0, 0], ("x", "y", "c"))
    result = [total_loss, dh.reshape(bl, seq, d)]
    for k in range(4):
        result += [outs[name][k] for name in WEIGHT_ORDER]
    return tuple(result)
```

```python
import functools

import jax
import jax.numpy as jnp
from jax import lax
from jax.experimental import pallas as pl
from jax.experimental.pallas import tpu as pltpu

F32 = jnp.float32
BF16 = jnp.bfloat16

HG_HEADS = 4
HG_F = 128
HG_CHUNK = 32
MEM_HEADS = 4
MEM_HEAD_DIM = 128
BRANCH_WIDTH = 512
N_BRANCH = 3
CONV_K = 3
LN_EPS = 1e-5
RMS_EPS = 1e-6
ADAM_LR = 0.001
ADAM_B1 = 0.9
ADAM_B2 = 0.999
ADAM_EPS = 1e-08
ADAM_WD = 0.01
ADAM_STEP = 10

VMEM_LIMIT = 48 * 1024 * 1024


def _cparams(sem):
    return pltpu.CompilerParams(dimension_semantics=sem, vmem_limit_bytes=VMEM_LIMIT)


def _dot(a, b, dims):
    return lax.dot_general(a, b, (dims, ((), ())), preferred_element_type=F32)


NN = ((1,), (0,))
NT = ((1,), (1,))
TN = ((0,), (0,))


def _pick(n, pref):
    for t in pref:
        if n % t == 0:
            return t
    return n


ANY_SPEC = pl.BlockSpec(memory_space=pl.ANY)


def _matmul(name, a, b, *, mode, out_dtype=F32, a_fn=None, a_extra=(), epi_fn=None, epi_extra=(), n_out=1, out_kinds=None,
            tm=512, tn=1024, tk=1024, deps=()):
    M, K = a.shape
    N = b.shape[1] if mode == "nn" else b.shape[0]
    tm, tn, tk = _pick(M, (tm, 256, 128, 8)), _pick(N, (tn, 896, 512, 256, 128)), _pick(K, (tk, 512, 256, 128))
    nk = K // tk
    n_ax, n_ex = len(a_extra), len(epi_extra)
    n_in = 2 + n_ax + n_ex + len(deps)
    out_dtypes = out_dtype if isinstance(out_dtype, (tuple, list)) else (out_dtype,) * n_out
    out_kinds = out_kinds or ("tile",) * n_out

    def body(*refs):
        a_ref, b_ref = refs[0], refs[1]
        ax_refs = refs[2:2 + n_ax]
        ex_refs = refs[2 + n_ax:2 + n_ax + n_ex]
        o_refs = refs[n_in:n_in + n_out]
        at = a_ref[...]
        at = a_fn(at, *[r[...] for r in ax_refs]) if a_fn is not None else at.astype(BF16)
        part = _dot(at, b_ref[...].astype(BF16), NN if mode == "nn" else NT)

        def finish(acc):
            outs = epi_fn(acc, *[r[...] for r in ex_refs]) if epi_fn is not None else (acc,)
            for o_ref, o, kind in zip(o_refs, outs, out_kinds):
                if kind == "rowsum":
                    @pl.when(pl.program_id(1) == 0)
                    def _(o_ref=o_ref):
                        o_ref[...] = jnp.zeros_like(o_ref)

                    o_ref[0:1, :] += o
                else:
                    o_ref[...] = o.astype(o_ref.dtype)

        if nk == 1:
            finish(part)
            return
        acc_ref = refs[-1]
        k = pl.program_id(2)

        @pl.when(k == 0)
        def _():
            acc_ref[...] = part

        @pl.when(jnp.logical_and(k > 0, k < nk - 1))
        def _():
            acc_ref[...] += part

        @pl.when(k == nk - 1)
        def _():
            finish(acc_ref[...] + part)

    b_mode = dict(pipeline_mode=pl.Buffered(1)) if (nk == 1 and N == tn) else {}
    in_specs = [pl.BlockSpec((tm, tk), lambda j, i, k: (i, k)),
                pl.BlockSpec((tk, tn), lambda j, i, k: (k, j), **b_mode) if mode == "nn"
                else pl.BlockSpec((tn, tk), lambda j, i, k: (j, k), **b_mode)]
    in_specs += [pl.BlockSpec((1, tk), lambda j, i, k: (0, k)) for _ in a_extra]
    for e in epi_extra:
        if e.shape[0] == 1:
            in_specs.append(pl.BlockSpec((1, tn), lambda j, i, k: (0, j)))
        elif e.shape[1] == 1:
            in_specs.append(pl.BlockSpec((tm, 1), lambda j, i, k: (i, 0)))
        else:
            in_specs.append(pl.BlockSpec((tm, tn), lambda j, i, k: (i, j)))
    in_specs += [ANY_SPEC] * len(deps)
    out_specs, out_shapes = [], []
    for kind, dt in zip(out_kinds, out_dtypes):
        if kind == "col":
            out_specs.append(pl.BlockSpec((tm, 1), lambda j, i, k: (i, 0)))
            out_shapes.append(jax.ShapeDtypeStruct((M, 1), dt))
        elif kind == "rowsum":
            out_specs.append(pl.BlockSpec((8, tn), lambda j, i, k: (0, j)))
            out_shapes.append(jax.ShapeDtypeStruct((8, N), dt))
        else:
            out_specs.append(pl.BlockSpec((tm, tn), lambda j, i, k: (i, j)))
            out_shapes.append(jax.ShapeDtypeStruct((M, N), dt))
    out = pl.pallas_call(
        body,
        name=name,
        grid=(N // tn, M // tm, nk),
        in_specs=in_specs,
        out_specs=out_specs,
        out_shape=out_shapes,
        scratch_shapes=[pltpu.VMEM((tm, tn), F32)] if nk > 1 else [],
        compiler_params=_cparams(("arbitrary", "arbitrary", "arbitrary")),
    )(a, b, *a_extra, *epi_extra, *deps)
    return out[0] if n_out == 1 else out


def _matmul_tn(name, a, b, *, a_fn=None, a_extra=(), a_cols=None, b_cols=None, ta=1024, tb=1024, tt=1024, out_dtype=F32, deps=()):
    T = a.shape[0]
    a0, Ka = a_cols if a_cols is not None else (0, a.shape[1])
    b0, Nb = b_cols if b_cols is not None else (0, b.shape[1])
    ta, tb, tt = _pick(Ka, (ta, 512, 256, 128)), _pick(Nb, (tb, 896, 512, 256, 128)), _pick(T, (tt, 512, 256, 128))
    assert a0 % ta == 0 and b0 % tb == 0
    a0, b0 = a0 // ta, b0 // tb
    nt = T // tt
    n_ax = len(a_extra)

    def body(*refs):
        a_ref, b_ref = refs[0], refs[1]
        ax_refs = refs[2:2 + n_ax]
        o_ref = refs[2 + n_ax + len(deps)]
        acc_ref = refs[-1]
        t = pl.program_id(2)
        at = a_ref[...]
        at = a_fn(at, *[r[...] for r in ax_refs]) if a_fn is not None else at.astype(BF16)
        part = _dot(at, b_ref[...].astype(BF16), TN)

        @pl.when(t == 0)
        def _():
            acc_ref[...] = part

        @pl.when(jnp.logical_and(t > 0, t < nt - 1))
        def _():
            acc_ref[...] += part

        @pl.when(t == nt - 1)
        def _():
            o_ref[...] = (acc_ref[...] + part if nt > 1 else part).astype(o_ref.dtype)

    in_specs = [pl.BlockSpec((tt, ta), lambda i, j, t: (t, a0 + i)), pl.BlockSpec((tt, tb), lambda i, j, t: (t, b0 + j))]
    in_specs += [pl.BlockSpec((1, ta), lambda i, j, t: (0, a0 + i)) for _ in a_extra]
    in_specs += [ANY_SPEC] * len(deps)
    return pl.pallas_call(
        body,
        name=name,
        grid=(Ka // ta, Nb // tb, nt),
        in_specs=in_specs,
        out_specs=pl.BlockSpec((ta, tb), lambda i, j, t: (i, j)),
        out_shape=jax.ShapeDtypeStruct((Ka, Nb), out_dtype),
        scratch_shapes=[pltpu.VMEM((ta, tb), F32)],
        compiler_params=_cparams(("parallel", "parallel", "arbitrary")),
    )(a, b, *a_extra, *deps)


W = BRANCH_WIDTH
C_CB, C_CC, C_CH, C_HQ, C_HF, C_HI, C_HG, C_MQ, N_MIX = 0, W, 2 * W, 3 * W, 4 * W, 5 * W, 6 * W, 7 * W, 8 * W
TS_MIX = 256
PREV_ROWS = 16
KEEP_NAMES = ("sq", "qs", "k", "sig", "f", "ea", "eb", "eq", "ek")


def _sigmoid(x):
    return jax.nn.sigmoid(x)


def _chunk_pos(shape):
    return lax.broadcasted_iota(jnp.int32, shape, 0) & (HG_CHUNK - 1)


def _seg_cumsum(x, pos):
    sh = 1
    while sh < HG_CHUNK:
        x = x + jnp.where(pos >= sh, pltpu.roll(x, sh, 0), 0.0)
        sh *= 2
    return x


def _seg_rev_cumsum(x, pos):
    n = x.shape[0]
    sh = 1
    while sh < HG_CHUNK:
        x = x + jnp.where(pos < HG_CHUNK - sh, pltpu.roll(x, n - sh, 0), 0.0)
        sh *= 2
    return x


def _chunk_mask(ts):
    r = lax.broadcasted_iota(jnp.int32, (ts, ts), 0)
    c = lax.broadcasted_iota(jnp.int32, (ts, ts), 1)
    return jnp.logical_and((r // HG_CHUNK) == (c // HG_CHUNK), c <= r)


def _hgrn_gates(p_ref, lb):
    q = p_ref[:, C_HQ:C_HQ + W].astype(F32)
    fl = p_ref[:, C_HF:C_HF + W].astype(F32)
    sig = _sigmoid(fl)
    f = lb + (1.0 - lb) * sig
    logf = jnp.log(f)
    k = (1.0 - lb) * _sigmoid(-fl)
    sq = _sigmoid(q)
    qs = q * sq
    return q, sq, qs, sig, f, logf, k


def _hgrn_decays(logf, bc_sc, ts):
    pos = _chunk_pos(logf.shape)
    bc = _seg_cumsum(logf, pos)
    bc_sc[...] = bc
    nc = ts // HG_CHUNK
    bref = jnp.concatenate(
        [jnp.broadcast_to(bc_sc[n * HG_CHUNK + HG_CHUNK // 2 - 1:n * HG_CHUNK + HG_CHUNK // 2, :], (HG_CHUNK, W)) for n in range(nc)], axis=0)
    blast = jnp.concatenate(
        [jnp.broadcast_to(bc_sc[(n + 1) * HG_CHUNK - 1:(n + 1) * HG_CHUNK, :], (HG_CHUNK, W)) for n in range(nc)], axis=0)
    return pos, bc, bref, blast


def _conv_shift_down(u, carry_ref, row):
    n = carry_ref.shape[0]
    last, before = carry_ref[n - 1:n, :], carry_ref[n - 2:n - 1, :]
    u1 = jnp.where(row == 0, last, pltpu.roll(u, 1, 0))
    u2 = jnp.where(row == 0, before, jnp.where(row == 1, last, pltpu.roll(u, 2, 0)))
    return u1, u2


def _attn_probs(qh, kh):
    s = _dot(qh, kh, NT) * (MEM_HEAD_DIM ** -0.5)
    e = jnp.exp(s - jnp.max(s, axis=-1, keepdims=True))
    return e / jnp.sum(e, axis=-1, keepdims=True)


def _mixer_fwd(p, mk, mv, lb, conv_w, norm_w, *, bl, seq):
    T = p.shape[0]
    ts = TS_MIX
    ns = seq // ts
    nc = ts // HG_CHUNK
    ml = mk.shape[0] // bl

    def body(p_ref, mk_ref, mv_ref, lb_ref, cw_ref, nw_ref, y_ref, st_ref, opre_ref, state_sc, carry_sc, bc_sc):
        @pl.when(pl.program_id(1) == 0)
        def _():
            state_sc[...] = jnp.zeros_like(state_sc)
            carry_sc[...] = jnp.zeros_like(carry_sc)

        cb, cc, ch = (p_ref[:, c0:c0 + W].astype(F32) for c0 in (C_CB, C_CC, C_CH))
        u = cc * ch
        row = lax.broadcasted_iota(jnp.int32, (ts, W), 0)
        u1, u2 = _conv_shift_down(u, carry_sc, row)
        yconv = u2 * cw_ref[0:1, :] + u1 * cw_ref[1:2, :] + u * cw_ref[2:3, :]
        y_ref[:, 0:W] = (cb * yconv).astype(BF16)
        carry_sc[...] = u[ts - 8:ts, :]

        lbv = lb_ref[...]
        _, _, qs, _, _, logf, k = _hgrn_gates(p_ref, lbv)
        pos, bc, bref, blast = _hgrn_decays(logf, bc_sc, ts)
        a_all = (qs * jnp.exp(bc - bref)).astype(BF16)
        bk_all = (k * jnp.exp(bref - bc)).astype(BF16)
        qin_all = (qs * jnp.exp(bc)).astype(BF16)
        kout_all = (k * jnp.exp(blast - bc)).astype(BF16)
        v_all = p_ref[:, C_HI:C_HI + W].astype(BF16)
        mask = _chunk_mask(ts)
        heads = [slice(h * HG_F, (h + 1) * HG_F) for h in range(HG_HEADS)]
        st = [state_sc[h] for h in range(HG_HEADS)]
        o_inter = [[] for _ in range(HG_HEADS)]
        for n in range(nc):
            rows = slice(n * HG_CHUNK, (n + 1) * HG_CHUNK)
            for h, hs in enumerate(heads):
                st_ref[n, h] = st[h]
                o_inter[h].append(_dot(qin_all[rows, hs], st[h].astype(BF16), NT))
                kv = _dot(v_all[rows, hs], kout_all[rows, hs], TN)
                decay = jnp.exp(bc_sc[(n + 1) * HG_CHUNK - 1:(n + 1) * HG_CHUNK, hs])
                st[h] = st[h] * decay + kv
        for h in range(HG_HEADS):
            state_sc[h] = st[h]
        scores = [_dot(a_all[:, hs], bk_all[:, hs], NT) for hs in heads]
        scores = [jnp.where(mask, s, 0.0).astype(BF16) for s in scores]
        outs = [_dot(scores[h], v_all[:, hs], NN) + jnp.concatenate(o_inter[h], axis=0) for h, hs in enumerate(heads)]
        for h, hs in enumerate(heads):
            o = outs[h]
            opre_ref[:, hs] = o
            on = o * lax.rsqrt(jnp.mean(o * o, axis=-1, keepdims=True) + RMS_EPS) * nw_ref[...]
            g = p_ref[:, C_HG + h * HG_F:C_HG + (h + 1) * HG_F].astype(F32)
            y_ref[:, W + h * HG_F:W + (h + 1) * HG_F] = (on * (g * _sigmoid(g))).astype(BF16)

        mheads = [slice(h * MEM_HEAD_DIM, (h + 1) * MEM_HEAD_DIM) for h in range(MEM_HEADS)]
        probs = [_attn_probs(p_ref[:, C_MQ + h * MEM_HEAD_DIM:C_MQ + (h + 1) * MEM_HEAD_DIM].astype(BF16), mk_ref[:, hs])
                 for h, hs in enumerate(mheads)]
        for h, hs in enumerate(mheads):
            y_ref[:, 2 * W + h * MEM_HEAD_DIM:2 * W + (h + 1) * MEM_HEAD_DIM] = _dot(
                probs[h].astype(BF16), mv_ref[:, hs], NN).astype(BF16)

    return pl.pallas_call(
        body,
        name="mixer_fwd",
        grid=(bl, ns),
        in_specs=[
            pl.BlockSpec((ts, N_MIX), lambda b, s: (b * ns + s, 0)),
            pl.BlockSpec((ml, W), lambda b, s: (b, 0)),
            pl.BlockSpec((ml, W), lambda b, s: (b, 0)),
            pl.BlockSpec((1, W), lambda b, s: (0, 0)),
            pl.BlockSpec((CONV_K, W), lambda b, s: (0, 0)),
            pl.BlockSpec((1, HG_F), lambda b, s: (0, 0)),
        ],
        out_specs=[
            pl.BlockSpec((ts, 3 * W), lambda b, s: (b * ns + s, 0)),
            pl.BlockSpec((nc, HG_HEADS, HG_F, HG_F), lambda b, s: (b * ns + s, 0, 0, 0)),
            pl.BlockSpec((ts, W), lambda b, s: (b * ns + s, 0)),
        ],
        out_shape=[
            jax.ShapeDtypeStruct((T, 3 * W), BF16),
            jax.ShapeDtypeStruct((T // HG_CHUNK, HG_HEADS, HG_F, HG_F), F32),
            jax.ShapeDtypeStruct((T, W), F32),
        ],
        scratch_shapes=[pltpu.VMEM((HG_HEADS, HG_F, HG_F), F32), pltpu.VMEM((8, W), F32), pltpu.VMEM((ts, W), F32)],
        compiler_params=_cparams(("arbitrary", "arbitrary")),
    )(p, mk, mv, lb, conv_w, norm_w)


def _mixer_bwd(p, dy, dp_gates, st, opre, mk, mv, lb, conv_w, norm_w, *, bl, seq, deps=()):
    T, nin = p.shape
    ts = TS_MIX
    ns = seq // ts
    nc = ts // HG_CHUNK
    ml = mk.shape[0] // bl
    mid, last = HG_CHUNK // 2 - 1, HG_CHUNK - 1

    def body(p_ref, pprev_ref, dy_ref, dpin_ref, st_ref, opre_ref, mk_ref, mv_ref, lb_ref, cw_ref, nw_ref, *rest):
        (dp_ref, dmk_ref, dmv_ref, dcw_ref, dnw_ref, dlb_ref, dstate_sc, carry_sc, uprev_sc, ab_sc, bkb_sc, qinb_sc, koutb_sc,
         dob_sc, dv_sc, da_sc, dbk_sc, dqin_sc, dkout_sc, dec_sc, ddec_sc, *keep_scs) = rest[len(deps):]
        del dpin_ref
        b, s = pl.program_id(0), pl.program_id(1)

        @pl.when(s == 0)
        def _():
            dstate_sc[...] = jnp.zeros_like(dstate_sc)
            carry_sc[...] = jnp.zeros_like(carry_sc)
            dmk_ref[...] = jnp.zeros_like(dmk_ref)
            dmv_ref[...] = jnp.zeros_like(dmv_ref)

        @pl.when(jnp.logical_and(b == 0, s == 0))
        def _():
            dcw_ref[...] = jnp.zeros_like(dcw_ref)
            dnw_ref[...] = jnp.zeros_like(dnw_ref)
            dlb_ref[...] = jnp.zeros_like(dlb_ref)

        cb, cc, ch = (p_ref[:, c0:c0 + W].astype(F32) for c0 in (C_CB, C_CC, C_CH))
        u = cc * ch
        row = lax.broadcasted_iota(jnp.int32, (ts, W), 0)
        uprev = pprev_ref[:, C_CC:C_CC + W].astype(F32) * pprev_ref[:, C_CH:C_CH + W].astype(F32)
        uprev_sc[...] = jnp.where(s == ns - 1, 0.0, uprev)
        u1, u2 = _conv_shift_down(u, uprev_sc, row)
        w0, w1, w2 = cw_ref[0:1, :], cw_ref[1:2, :], cw_ref[2:3, :]
        dya = dy_ref[:, 0:W].astype(F32)
        dp_ref[:, C_CB:C_CB + W] = (dya * (u2 * w0 + u1 * w1 + u * w2)).astype(BF16)
        dv = cb * dya
        dv1 = jnp.where(row == ts - 1, carry_sc[0:1, :], pltpu.roll(dv, ts - 1, 0))
        dv2 = jnp.where(row == ts - 1, carry_sc[1:2, :], jnp.where(row == ts - 2, carry_sc[0:1, :], pltpu.roll(dv, ts - 2, 0)))
        du = dv * w2 + dv1 * w1 + dv2 * w0
        dp_ref[:, C_CC:C_CC + W] = (du * ch).astype(BF16)
        dp_ref[:, C_CH:C_CH + W] = (du * cc).astype(BF16)
        dcw_ref[0:1, :] += jnp.sum(dv * u2, axis=0, keepdims=True)
        dcw_ref[1:2, :] += jnp.sum(dv * u1, axis=0, keepdims=True)
        dcw_ref[2:3, :] += jnp.sum(dv * u, axis=0, keepdims=True)
        carry_sc[...] = dv[0:8, :]

        mask = _chunk_mask(ts)
        pos_c = _chunk_pos((HG_CHUNK, HG_F))
        nw = nw_ref[...]

        def block(n, h):
            rows = slice(n * HG_CHUNK, (n + 1) * HG_CHUNK)
            return rows, slice(h * HG_F, (h + 1) * HG_F)

        keep = dict(zip(KEEP_NAMES, keep_scs))

        def gates(rows, h):
            lbh = lb_ref[:, h * HG_F:(h + 1) * HG_F]
            q = p_ref[rows, C_HQ + h * HG_F:C_HQ + (h + 1) * HG_F].astype(F32)
            fl = p_ref[rows, C_HF + h * HG_F:C_HF + (h + 1) * HG_F].astype(F32)
            sig = _sigmoid(fl)
            f = lbh + (1.0 - lbh) * sig
            k = (1.0 - lbh) * _sigmoid(-fl)
            sq = _sigmoid(q)
            qs = q * sq
            bc = _seg_cumsum(jnp.log(f), pos_c)
            bref = jnp.sum(jnp.where(pos_c == mid, bc, 0.0), axis=0, keepdims=True)
            blast = jnp.sum(jnp.where(pos_c == last, bc, 0.0), axis=0, keepdims=True)
            ea, eb, eq, ek = jnp.exp(bc - bref), jnp.exp(bref - bc), jnp.exp(bc), jnp.exp(blast - bc)
            return dict(sq=sq, qs=qs, k=k, sig=sig, f=f, ea=ea, eb=eb, eq=eq, ek=ek), blast

        dnw = jnp.zeros((1, HG_F), F32)
        for n in range(nc):
            for h in range(HG_HEADS):
                rows, hs = block(n, h)
                fw, blast = gates(rows, h)
                for name in KEEP_NAMES:
                    keep[name][rows, hs] = fw[name]
                ab_sc[rows, hs] = (fw["qs"] * fw["ea"]).astype(BF16)
                bkb_sc[rows, hs] = (fw["k"] * fw["eb"]).astype(BF16)
                qinb_sc[rows, hs] = (fw["qs"] * fw["eq"]).astype(BF16)
                koutb_sc[rows, hs] = (fw["k"] * fw["ek"]).astype(BF16)
                dec_sc[n:n + 1, hs] = jnp.exp(blast)
                o = opre_ref[rows, hs]
                g = p_ref[rows, C_HG + h * HG_F:C_HG + (h + 1) * HG_F].astype(F32)
                sg = _sigmoid(g)
                r = lax.rsqrt(jnp.mean(o * o, axis=-1, keepdims=True) + RMS_EPS)
                dyb = dy_ref[rows, W + h * HG_F:W + (h + 1) * HG_F].astype(F32)
                dp_ref[rows, C_HG + h * HG_F:C_HG + (h + 1) * HG_F] = (
                    dyb * (o * r * nw) * (sg * (1.0 + g * (1.0 - sg)))).astype(BF16)
                don = dyb * (g * sg)
                dnw = dnw + jnp.sum(don * o * r, axis=0, keepdims=True)
                dn = don * nw
                dob_sc[rows, hs] = (r * (dn - o * (r * r) * jnp.mean(dn * o, axis=-1, keepdims=True))).astype(BF16)
        dnw_ref[0:1, :] += dnw

        heads = [slice(h * HG_F, (h + 1) * HG_F) for h in range(HG_HEADS)]
        scores = [_dot(ab_sc[:, hs], bkb_sc[:, hs], NT) for hs in heads]
        dscores = [_dot(dob_sc[:, hs], p_ref[:, C_HI + h * HG_F:C_HI + (h + 1) * HG_F].astype(BF16), NT)
                   for h, hs in enumerate(heads)]
        scores = [jnp.where(mask, s, 0.0).astype(BF16) for s in scores]
        dscores = [jnp.where(mask, s, 0.0).astype(BF16) for s in dscores]
        for h, hs in enumerate(heads):
            dv_sc[:, hs] = _dot(scores[h], dob_sc[:, hs], TN)
            da_sc[:, hs] = _dot(dscores[h], bkb_sc[:, hs], NN)
            dbk_sc[:, hs] = _dot(dscores[h], ab_sc[:, hs], TN)
        dst = [dstate_sc[h] for h in range(HG_HEADS)]
        for n in reversed(range(nc)):
            for h in range(HG_HEADS):
                rows, hs = block(n, h)
                st_n = st_ref[n, h]
                decay = dec_sc[n:n + 1, hs]
                dstb = dst[h].astype(BF16)
                dob_n = dob_sc[rows, hs]
                dv_sc[rows, hs] += _dot(koutb_sc[rows, hs], dstb, NT)
                dkout_sc[rows, hs] = _dot(p_ref[rows, C_HI + h * HG_F:C_HI + (h + 1) * HG_F].astype(BF16), dstb, NN)
                ddec_sc[n:n + 1, hs] = jnp.sum(dst[h] * st_n, axis=0, keepdims=True) * decay
                dqin_sc[rows, hs] = _dot(dob_n, st_n.astype(BF16), NN)
                dst[h] = dst[h] * decay + _dot(dob_n, qinb_sc[rows, hs], TN)
        for h in range(HG_HEADS):
            dstate_sc[h] = dst[h]

        for h in range(HG_HEADS):
            dlb = jnp.zeros((1, HG_F), F32)
            for n in range(nc):
                rows, hs = block(n, h)
                fw = {name: keep[name][rows, hs] for name in KEEP_NAMES}
                lbh = lb_ref[:, h * HG_F:(h + 1) * HG_F]
                q = p_ref[rows, C_HQ + h * HG_F:C_HQ + (h + 1) * HG_F].astype(F32)
                da, dbk, dqin, dkout = da_sc[rows, hs], dbk_sc[rows, hs], dqin_sc[rows, hs], dkout_sc[rows, hs]
                w_a, w_b, w_q, w_k = da * fw["ea"], dbk * fw["eb"], dqin * fw["eq"], dkout * fw["ek"]
                dqs, dk = w_a + w_q, w_b + w_k
                t_a, t_b, t_q, t_k = w_a * fw["qs"], w_b * fw["k"], w_q * fw["qs"], w_k * fw["k"]
                s_ref = jnp.sum(t_b - t_a, axis=0, keepdims=True)
                s_last = jnp.sum(t_k, axis=0, keepdims=True) + ddec_sc[n:n + 1, hs]
                dbc = (t_a - t_b + t_q - t_k) + jnp.where(pos_c == mid, s_ref, 0.0) + jnp.where(pos_c == last, s_last, 0.0)
                dfk = _seg_rev_cumsum(dbc, pos_c) / fw["f"] - dk
                sig, sq = fw["sig"], fw["sq"]
                dp_ref[rows, C_HF + h * HG_F:C_HF + (h + 1) * HG_F] = (dfk * (1.0 - lbh) * sig * (1.0 - sig)).astype(BF16)
                dlb = dlb + jnp.sum(dfk * (1.0 - sig), axis=0, keepdims=True)
                dp_ref[rows, C_HQ + h * HG_F:C_HQ + (h + 1) * HG_F] = (dqs * (sq * (1.0 + q * (1.0 - sq)))).astype(BF16)
                dp_ref[rows, C_HI + h * HG_F:C_HI + (h + 1) * HG_F] = dv_sc[rows, hs].astype(BF16)
            dlb_ref[0:1, h * HG_F:(h + 1) * HG_F] += dlb

        mheads = [slice(h * MEM_HEAD_DIM, (h + 1) * MEM_HEAD_DIM) for h in range(MEM_HEADS)]
        qhs = [p_ref[:, C_MQ + h * MEM_HEAD_DIM:C_MQ + (h + 1) * MEM_HEAD_DIM].astype(BF16) for h in range(MEM_HEADS)]
        dobs = [dy_ref[:, 2 * W + h * MEM_HEAD_DIM:2 * W + (h + 1) * MEM_HEAD_DIM].astype(BF16) for h in range(MEM_HEADS)]
        probs = [_attn_probs(qhs[h], mk_ref[:, hs]) for h, hs in enumerate(mheads)]
        dprobs = [_dot(dobs[h], mv_ref[:, hs], NT) for h, hs in enumerate(mheads)]
        for h, hs in enumerate(mheads):
            prob = probs[h]
            dmv_ref[:, hs] += _dot(prob.astype(BF16), dobs[h], TN)
            ds = prob * (dprobs[h] - jnp.sum(dprobs[h] * prob, axis=-1, keepdims=True)) * (MEM_HEAD_DIM ** -0.5)
            dsb = ds.astype(BF16)
            dp_ref[:, C_MQ + h * MEM_HEAD_DIM:C_MQ + (h + 1) * MEM_HEAD_DIM] = _dot(dsb, mk_ref[:, hs], NN).astype(BF16)
            dmk_ref[:, hs] += _dot(dsb, qhs[h], TN)

    def tile(b, s):
        return b * ns + (ns - 1 - s)

    return pl.pallas_call(
        body,
        name="mixer_bwd",
        grid=(bl, ns),
        in_specs=[
            pl.BlockSpec((ts, N_MIX), lambda b, s: (tile(b, s), 0)),
            pl.BlockSpec((PREV_ROWS, N_MIX), lambda b, s: (jnp.maximum(tile(b, s) * (ts // PREV_ROWS) - 1, 0), 0)),
            pl.BlockSpec((ts, 3 * W), lambda b, s: (tile(b, s), 0)),
            pl.BlockSpec(memory_space=pl.ANY),
            pl.BlockSpec((nc, HG_HEADS, HG_F, HG_F), lambda b, s: (tile(b, s), 0, 0, 0)),
            pl.BlockSpec((ts, W), lambda b, s: (tile(b, s), 0)),
            pl.BlockSpec((ml, W), lambda b, s: (b, 0)),
            pl.BlockSpec((ml, W), lambda b, s: (b, 0)),
            pl.BlockSpec((1, W), lambda b, s: (0, 0)),
            pl.BlockSpec((CONV_K, W), lambda b, s: (0, 0)),
            pl.BlockSpec((1, HG_F), lambda b, s: (0, 0)),
        ] + [ANY_SPEC] * len(deps),
        out_specs=[
            pl.BlockSpec((ts, N_MIX), lambda b, s: (tile(b, s), 0)),
            pl.BlockSpec((ml, W), lambda b, s: (b, 0)),
            pl.BlockSpec((ml, W), lambda b, s: (b, 0)),
            pl.BlockSpec((8, W), lambda b, s: (0, 0)),
            pl.BlockSpec((8, HG_F), lambda b, s: (0, 0)),
            pl.BlockSpec((8, W), lambda b, s: (0, 0)),
        ],
        out_shape=[
            jax.ShapeDtypeStruct((T, nin), BF16),
            jax.ShapeDtypeStruct((bl * ml, W), F32),
            jax.ShapeDtypeStruct((bl * ml, W), F32),
            jax.ShapeDtypeStruct((8, W), F32),
            jax.ShapeDtypeStruct((8, HG_F), F32),
            jax.ShapeDtypeStruct((8, W), F32),
        ],
        input_output_aliases={3: 0},
        scratch_shapes=[pltpu.VMEM((HG_HEADS, HG_F, HG_F), F32), pltpu.VMEM((8, W), F32), pltpu.VMEM((PREV_ROWS, W), F32)]
        + [pltpu.VMEM((ts, W), BF16)] * 5 + [pltpu.VMEM((ts, W), F32)] * 5 + [pltpu.VMEM((nc, W), F32)] * 2
        + [pltpu.VMEM((ts, W), F32)] * len(KEEP_NAMES),
        compiler_params=_cparams(("arbitrary", "arbitrary")),
    )(p, p, dy, dp_gates, st, opre, mk, mv, lb, conv_w, norm_w, *deps)


def _layer_norm_stats(z):
    mu = jnp.mean(z, axis=-1, keepdims=True)
    zc = z - mu
    rstd = lax.rsqrt(jnp.mean(zc * zc, axis=-1, keepdims=True) + LN_EPS)
    return zc * rstd, rstd


def _gate_specs(tm, d):
    g0 = N_MIX // d
    return [pl.BlockSpec((tm, d), functools.partial(lambda i, k: (i, g0 + k), k=k)) for k in range(N_BRANCH)]


def _merge_fwd(y, p, x0, wb, wo, bg, ln_g, ln_b, *, alpha, tm=512):
    T, d = x0.shape
    assert N_MIX % d == 0
    tm = _pick(T, (tm, 128, 8))

    def body(y_ref, g0_ref, g1_ref, g2_ref, x_ref, wb_ref, wo_ref, bg_ref, lg_ref, lb_ref, mg_ref, xh_ref, rs_ref, x1b_ref):
        merged = None
        for i, g_ref in enumerate((g0_ref, g1_ref, g2_ref)):
            r = _dot(y_ref[:, i * W:(i + 1) * W], wb_ref[i * W:(i + 1) * W, :], NN)
            t = _sigmoid(g_ref[...].astype(F32) + bg_ref[:, i * d:(i + 1) * d]) * r
            merged = t if merged is None else merged + t
        mb = merged.astype(BF16)
        mg_ref[...] = mb
        z = alpha * x_ref[...] + _dot(mb, wo_ref[...], NN)
        xh, rs = _layer_norm_stats(z)
        xh_ref[...], rs_ref[...] = xh, rs
        x1b_ref[...] = (xh * lg_ref[...] + lb_ref[...]).astype(BF16)

    row = lambda i: (i, 0)
    fix = lambda i: (0, 0)
    return pl.pallas_call(
        body,
        name="merge_fwd",
        grid=(T // tm,),
        in_specs=[pl.BlockSpec((tm, 3 * W), row)] + _gate_specs(tm, d) + [
            pl.BlockSpec((tm, d), row), pl.BlockSpec((3 * W, d), fix, pipeline_mode=pl.Buffered(1)),
            pl.BlockSpec((d, d), fix, pipeline_mode=pl.Buffered(1)), pl.BlockSpec((1, 3 * d), fix),
            pl.BlockSpec((1, d), fix), pl.BlockSpec((1, d), fix)],
        out_specs=[pl.BlockSpec((tm, d), row), pl.BlockSpec((tm, d), row), pl.BlockSpec((tm, 1), row), pl.BlockSpec((tm, d), row)],
        out_shape=[jax.ShapeDtypeStruct((T, d), BF16), jax.ShapeDtypeStruct((T, d), F32), jax.ShapeDtypeStruct((T, 1), F32),
                   jax.ShapeDtypeStruct((T, d), BF16)],
        compiler_params=_cparams(("parallel",)),
    )(y, p, p, p, x0, wb, wo, bg, ln_g, ln_b)


def _merge_bwd(dz, p, y, wb, wo, bg, *, tm=512):
    T, d = dz.shape
    nin = p.shape[1]
    tm = _pick(T, (tm, 128, 8))

    def body(dz_ref, g0_ref, g1_ref, g2_ref, y_ref, wb_ref, wo_ref, bg_ref, dr_ref, dp_ref, dy_ref, dbg_ref):
        @pl.when(pl.program_id(0) == 0)
        def _():
            dbg_ref[...] = jnp.zeros_like(dbg_ref)

        dmerged = _dot(dz_ref[...].astype(BF16), wo_ref[...], NT)
        dp_ref[:, 0:N_MIX] = jnp.zeros((tm, N_MIX), BF16)
        for i, g_ref in enumerate((g0_ref, g1_ref, g2_ref)):
            cs = slice(i * d, (i + 1) * d)
            s = _sigmoid(g_ref[...].astype(F32) + bg_ref[:, cs])
            drb = (dmerged * s).astype(BF16)
            dr_ref[:, cs] = drb
            dgate = dmerged * _dot(y_ref[:, i * W:(i + 1) * W], wb_ref[i * W:(i + 1) * W, :], NN) * s * (1.0 - s)
            dp_ref[:, N_MIX + i * d:N_MIX + (i + 1) * d] = dgate.astype(BF16)
            dbg_ref[0:1, cs] += jnp.sum(dgate, axis=0, keepdims=True)
            dy_ref[:, i * W:(i + 1) * W] = _dot(drb, wb_ref[i * W:(i + 1) * W, :], NT).astype(BF16)

    row = lambda i: (i, 0)
    fix = lambda i: (0, 0)
    return pl.pallas_call(
        body,
        name="merge_bwd",
        grid=(T // tm,),
        in_specs=[pl.BlockSpec((tm, d), row)] + _gate_specs(tm, d) + [
            pl.BlockSpec((tm, 3 * W), row), pl.BlockSpec((3 * W, d), fix, pipeline_mode=pl.Buffered(1)),
            pl.BlockSpec((d, d), fix, pipeline_mode=pl.Buffered(1)), pl.BlockSpec((1, 3 * d), fix)],
        out_specs=[pl.BlockSpec((tm, 3 * d), row), pl.BlockSpec((tm, nin), row), pl.BlockSpec((tm, 3 * W), row),
                   pl.BlockSpec((8, 3 * d), fix)],
        out_shape=[jax.ShapeDtypeStruct((T, 3 * d), BF16), jax.ShapeDtypeStruct((T, nin), BF16),
                   jax.ShapeDtypeStruct((T, 3 * W), BF16), jax.ShapeDtypeStruct((8, 3 * d), F32)],
        compiler_params=_cparams(("arbitrary",)),
    )(dz, p, p, p, y, wb, wo, bg)


def _mlp_fwd(xhat1, x1b, g1, b1, wu, wd, g2, b2, *, alpha, tm=512, tf=2048):
    T, d = xhat1.shape
    ff = wu.shape[1]
    tm, tf = _pick(T, (tm, 256, 128, 8)), _pick(ff, (tf, 1024, 512, 256, 128))
    nf = ff // tf

    def body(xh_ref, x1b_ref, g1_ref, b1_ref, wu_ref, wd_ref, g2_ref, b2_ref, a_ref, xh2_ref, rs2_ref, x2_ref, x2b_ref, acc_ref):
        f = pl.program_id(1)
        a = _dot(x1b_ref[...], wu_ref[...], NN)
        a_ref[...] = a.astype(BF16)
        h = jnp.square(jnp.maximum(a, 0.0))
        part = _dot(h.astype(BF16), wd_ref[...], NN)

        @pl.when(f == 0)
        def _():
            acc_ref[...] = part

        @pl.when(jnp.logical_and(f > 0, f < nf - 1))
        def _():
            acc_ref[...] += part

        @pl.when(f == nf - 1)
        def _():
            x1 = xh_ref[...] * g1_ref[...] + b1_ref[...]
            xh2, rs2 = _layer_norm_stats(alpha * x1 + (acc_ref[...] + part if nf > 1 else part))
            xh2_ref[...] = xh2
            rs2_ref[...] = rs2
            x2 = xh2 * g2_ref[...] + b2_ref[...]
            x2_ref[...] = x2
            x2b_ref[...] = x2.astype(BF16)

    row = lambda i, f: (i, 0)
    fix = lambda i, f: (0, 0)
    return pl.pallas_call(
        body,
        name="mlp_fwd",
        grid=(T // tm, nf),
        in_specs=[pl.BlockSpec((tm, d), row), pl.BlockSpec((tm, d), row), pl.BlockSpec((1, d), fix), pl.BlockSpec((1, d), fix),
                  pl.BlockSpec((d, tf), lambda i, f: (0, f)), pl.BlockSpec((tf, d), lambda i, f: (f, 0)),
                  pl.BlockSpec((1, d), fix), pl.BlockSpec((1, d), fix)],
        out_specs=[pl.BlockSpec((tm, tf), lambda i, f: (i, f)), pl.BlockSpec((tm, d), row), pl.BlockSpec((tm, 1), row),
                   pl.BlockSpec((tm, d), row), pl.BlockSpec((tm, d), row)],
        out_shape=[jax.ShapeDtypeStruct((T, ff), BF16), jax.ShapeDtypeStruct((T, d), F32), jax.ShapeDtypeStruct((T, 1), F32),
                   jax.ShapeDtypeStruct((T, d), F32), jax.ShapeDtypeStruct((T, d), BF16)],
        scratch_shapes=[pltpu.VMEM((tm, d), F32)],
        compiler_params=_cparams(("parallel", "arbitrary")),
    )(xhat1, x1b, g1, b1, wu, wd, g2, b2)


def _ln_bwd(dy, xhat, rstd, g, *, tm=512, deps=()):
    T, d = dy.shape
    tm = _pick(T, (tm, 256, 128, 8))

    def body(dy_ref, xh_ref, rs_ref, g_ref, *rest):
        dz_ref, dzb_ref, dg_ref, db_ref = rest[len(deps):]

        @pl.when(pl.program_id(0) == 0)
        def _():
            dg_ref[...] = jnp.zeros_like(dg_ref)
            db_ref[...] = jnp.zeros_like(db_ref)

        dy_, xh = dy_ref[...], xh_ref[...]
        dg_ref[0:1, :] += jnp.sum(dy_ * xh, axis=0, keepdims=True)
        db_ref[0:1, :] += jnp.sum(dy_, axis=0, keepdims=True)
        dxh = dy_ * g_ref[...]
        dz = rs_ref[...] * (dxh - jnp.mean(dxh, axis=-1, keepdims=True) - xh * jnp.mean(dxh * xh, axis=-1, keepdims=True))
        dz_ref[...] = dz
        dzb_ref[...] = dz.astype(BF16)

    row = lambda i: (i, 0)
    fix = lambda i: (0, 0)
    return pl.pallas_call(
        body,
        name="ln_bwd",
        grid=(T // tm,),
        in_specs=[pl.BlockSpec((tm, d), row), pl.BlockSpec((tm, d), row), pl.BlockSpec((tm, 1), row), pl.BlockSpec((1, d), fix)]
        + [ANY_SPEC] * len(deps),
        out_specs=[pl.BlockSpec((tm, d), row), pl.BlockSpec((tm, d), row), pl.BlockSpec((8, d), fix), pl.BlockSpec((8, d), fix)],
        out_shape=[jax.ShapeDtypeStruct((T, d), F32), jax.ShapeDtypeStruct((T, d), BF16), jax.ShapeDtypeStruct((8, d), F32),
                   jax.ShapeDtypeStruct((8, d), F32)],
        compiler_params=_cparams(("arbitrary",)),
    )(dy, xhat, rstd, g, *deps)


def _loss_head(y, target, *, tm=512):
    T, d = y.shape
    tm = _pick(T, (tm, 256, 128, 8))
    n = T // tm

    def body(y_ref, t_ref, loss_ref, dy_ref, acc_ref):
        i = pl.program_id(0)

        @pl.when(i == 0)
        def _():
            acc_ref[...] = jnp.zeros_like(acc_ref)

        e = y_ref[...] - t_ref[...]
        dy_ref[...] = e * (1.0 / d)
        acc_ref[...] += jnp.sum(e * e, axis=0, keepdims=True)

        @pl.when(i == n - 1)
        def _():
            loss_ref[...] = (0.5 / d) * jnp.sum(acc_ref[...], axis=1, keepdims=True)

    row = lambda i: (i, 0)
    return pl.pallas_call(
        body,
        name="loss_head",
        grid=(n,),
        in_specs=[pl.BlockSpec((tm, d), row), pl.BlockSpec((tm, d), row)],
        out_specs=[pl.BlockSpec((1, 1), lambda i: (0, 0)), pl.BlockSpec((tm, d), row)],
        out_shape=[jax.ShapeDtypeStruct((1, 1), F32), jax.ShapeDtypeStruct((T, d), F32)],
        scratch_shapes=[pltpu.VMEM((1, d), F32)],
        compiler_params=_cparams(("arbitrary",)),
    )(y, target)


def _lower_bounds_fwd(lower_bounds):
    depth, n = lower_bounds.shape

    def body(x_ref, soft_ref, lb_ref):
        x = x_ref[...]
        e = jnp.exp(x - jnp.max(x, axis=0, keepdims=True))
        soft_ref[...] = e / jnp.sum(e, axis=0, keepdims=True)
        run = None
        for l in range(depth):
            run = soft_ref[l:l + 1, :] if run is None else run + soft_ref[l:l + 1, :]
            lb_ref[l:l + 1, :] = run - soft_ref[0:1, :]

    return pl.pallas_call(body, name="lower_bounds_fwd",
                          out_shape=[jax.ShapeDtypeStruct((depth, n), F32), jax.ShapeDtypeStruct((depth, n), F32)])(lower_bounds)


def _lower_bounds_bwd(soft, dlb):
    depth, n = soft.shape

    def body(soft_ref, dlb_ref, out_ref, dsoft_ref):
        total = jnp.sum(dlb_ref[...], axis=0, keepdims=True)
        run = None
        for l in reversed(range(depth)):
            run = dlb_ref[l:l + 1, :] if run is None else run + dlb_ref[l:l + 1, :]
            dsoft_ref[l:l + 1, :] = run - total if l == 0 else run
        s, ds = soft_ref[...], dsoft_ref[...]
        out_ref[...] = s * (ds - jnp.sum(s * ds, axis=0, keepdims=True))

    return pl.pallas_call(body, name="lower_bounds_bwd", out_shape=jax.ShapeDtypeStruct((depth, n), F32),
                          scratch_shapes=[pltpu.VMEM((depth, n), F32)])(soft, dlb)


def _layer_fwd(x0, x0b, mem2, lb, w_in, rest_fn, *, bl, seq, alpha, deps=()):
    p = _matmul("proj_in", x0b, w_in, mode="nn", out_dtype=BF16, deps=deps, tm=1024, tn=1792)
    wts = dict(rest_fn(p), w_in=w_in)
    mk = _matmul("mem_k", mem2, wts["w_mem_k"], mode="nn", out_dtype=BF16)
    mv = _matmul("mem_v", mem2, wts["w_mem_v"], mode="nn", out_dtype=BF16)
    y, st, opre = _mixer_fwd(p, mk, mv, lb, wts["conv_w"], wts["hg_norm_w"], bl=bl, seq=seq)
    merged, xhat1, rstd1, x1b = _merge_fwd(y, p, x0, wts["w_branch"], wts["w_o"], wts["b_gate"], wts["ln1_g"], wts["ln1_b"],
                                           alpha=alpha)
    a, xhat2, rstd2, x2, x2b = _mlp_fwd(xhat1, x1b, wts["ln1_g"], wts["ln1_b"], wts["w_up"], wts["w_down"], wts["ln2_g"],
                                        wts["ln2_b"], alpha=alpha)
    saved = dict(x0b=x0b, p=p, mk=mk, mv=mv, y=y, st=st, opre=opre, merged=merged, xhat1=xhat1, rstd1=rstd1, x1b=x1b, a=a,
                 xhat2=xhat2, rstd2=rstd2)
    return x2, x2b, saved, wts


def _relu2_bf16(a):
    return jnp.square(jnp.maximum(a.astype(F32), 0.0)).astype(BF16)


def _mlp_bwd(dz2, dz2b, sv, wts, *, alpha, deps=()):
    g = {}
    da = _matmul("mlp_da", dz2b, wts["w_down"], mode="nt", out_dtype=BF16, tm=1024, deps=deps,
                 epi_fn=lambda acc, a: (acc * (2.0 * jnp.maximum(a.astype(F32), 0.0)),), epi_extra=(sv["a"],))
    g["w_down"] = _matmul_tn("grad_w_down", sv["a"], dz2b, a_fn=_relu2_bf16, out_dtype=BF16, tt=2048)
    g["w_up"] = _matmul_tn("grad_w_up", sv["x1b"], da, out_dtype=BF16, tt=2048)
    dx1 = _matmul("mlp_dx", da, wts["w_up"], mode="nt", epi_fn=lambda acc, dz: (acc + alpha * dz,), epi_extra=(dz2,),
                  tm=512, tk=4096)
    dz1, dz1b, dg1, db1 = _ln_bwd(dx1, sv["xhat1"], sv["rstd1"], wts["ln1_g"])
    g["ln1_g"], g["ln1_b"] = dg1[0:1], db1[0:1]
    return dz1, dz1b, g


def _mix_bwd(dz1, dz1b, sv, mem2, lb, wts, *, bl, seq, alpha, send, below=None, deps=()):
    d = dz1.shape[1]
    g = {}
    g["w_o"] = _matmul_tn("grad_w_o", sv["merged"], dz1b, out_dtype=BF16, tt=2048, deps=deps)
    dr, dp, dy, dbg = _merge_bwd(dz1b, sv["p"], sv["y"], wts["w_branch"], wts["w_o"], wts["b_gate"])
    g["b_gate"] = dbg[0:1]
    g["w_branch"] = jnp.concatenate(
        [_matmul_tn("grad_w_branch", sv["y"], dr, a_cols=(i * W, W), b_cols=(i * d, d), out_dtype=BF16) for i in range(N_BRANCH)],
        axis=0)
    token = send(("w_o", "w_branch"), g)
    dp, dmk, dmv, dcw, dnw, dlb = _mixer_bwd(sv["p"], dy, dp, sv["st"], sv["opre"], sv["mk"], sv["mv"], lb,
                                              wts["conv_w"], wts["hg_norm_w"], bl=bl, seq=seq, deps=(token,))
    g["conv_w"], g["hg_norm_w"], g["lb"] = dcw[0:CONV_K], dnw[0:1], dlb[0:1]
    g["w_mem_k"] = _matmul_tn("grad_w_mem_k", mem2, dmk, out_dtype=BF16)
    g["w_mem_v"] = _matmul_tn("grad_w_mem_v", mem2, dmv, out_dtype=BF16)
    g["w_in"] = _matmul_tn("grad_w_in", sv["x0b"], dp, out_dtype=BF16, tt=2048)
    token = send(("w_in", "w_mem_k", "w_mem_v", "conv_w"), g)
    dx0 = _matmul("proj_in_dx", dp, wts["w_in"], mode="nt", epi_fn=lambda acc, dz: (acc + alpha * dz,), epi_extra=(dz1,),
                  tm=512, tk=dp.shape[1], deps=(token,))
    return (dx0 if below is None else _ln_bwd(dx0, *below)), g


N_CHIPS = 4
MESH_IDS = pl.DeviceIdType.MESH


def _axis_slice(ref, axis, start, size):
    idx = [slice(None)] * len(ref.shape)
    idx[axis] = pl.ds(start, size)
    return ref.at[tuple(idx)]


def _chip_exchange(name, items):
    n = len(items)
    out_shapes, meta = [], []
    for arr, kind, axis in items:
        shp = list(arr.shape)
        if kind == "gather":
            per = shp[axis]
            shp[axis] = per * N_CHIPS
            out_shapes.append(jax.ShapeDtypeStruct(tuple(shp), arr.dtype))
        elif kind == "scatter":
            per = shp[axis] // N_CHIPS
            shp[axis] = per
            out_shapes.append(jax.ShapeDtypeStruct((N_CHIPS, *shp), arr.dtype))
        else:
            per = None
            out_shapes.append(jax.ShapeDtypeStruct((N_CHIPS, *shp), arr.dtype))
        meta.append((kind, axis, per))

    def body(*refs):
        ins, outs = refs[:n], refs[n:2 * n]
        send_sems, recv_sems, local_sems = refs[2 * n:]
        x, y, c = lax.axis_index("x"), lax.axis_index("y"), lax.axis_index("c")
        me = 2 * x + y
        peers = [(1 - x, y), (x, 1 - y), (1 - x, 1 - y)]

        def src_for(t, chip):
            kind, axis, per = meta[t]
            return _axis_slice(ins[t], axis, chip * per, per) if kind == "scatter" else ins[t]

        def dst_from(t, chip):
            kind, axis, per = meta[t]
            return _axis_slice(outs[t], axis, chip * per, per) if kind == "gather" else outs[t].at[chip]

        def remote(t, k):
            px, py = peers[k]
            return pltpu.make_async_remote_copy(
                src_ref=src_for(t, 2 * px + py), dst_ref=dst_from(t, me), send_sem=send_sems.at[t * 3 + k],
                recv_sem=recv_sems.at[t * 3 + k], device_id=(px, py, c), device_id_type=MESH_IDS)

        def arrival(t, k):
            px, py = peers[k]
            return pltpu.make_async_remote_copy(
                src_ref=src_for(t, me), dst_ref=dst_from(t, 2 * px + py), send_sem=send_sems.at[t * 3 + k],
                recv_sem=recv_sems.at[t * 3 + k], device_id=(px, py, c), device_id_type=MESH_IDS)

        sends = [remote(t, k) for t in range(n) for k in range(3)]
        for cp in sends:
            cp.start()
        own = [pltpu.make_async_copy(src_for(t, me), dst_from(t, me), local_sems.at[t]) for t in range(n)]
        for cp in own:
            cp.start()
        for t in range(n):
            for k in range(3):
                arrival(t, k).wait_recv()
        for cp in sends:
            cp.wait_send()
        for cp in own:
            cp.wait()

    any_spec = pl.BlockSpec(memory_space=pl.ANY)
    return pl.pallas_call(
        body,
        name=name,
        in_specs=[any_spec] * n,
        out_specs=[any_spec] * n,
        out_shape=out_shapes,
        scratch_shapes=[pltpu.SemaphoreType.DMA((3 * n,)), pltpu.SemaphoreType.DMA((3 * n,)), pltpu.SemaphoreType.DMA((n,))],
        compiler_params=pltpu.CompilerParams(has_side_effects=True),
    )(*[a for a, _, _ in items])


HBM_SPEC = pl.BlockSpec(memory_space=pltpu.HBM)
SEM_SPEC = pl.BlockSpec(memory_space=pltpu.SEMAPHORE)
N_PEERS = N_CHIPS - 1


def _my_chip():
    return (2 * lax.axis_index("x") + lax.axis_index("y")).astype(jnp.int32).reshape(1)


def _own_block_spec(r, c, axis, tr):
    if axis == 1:
        return pl.BlockSpec((tr, c), lambda i, me: (i, me[0]))
    return pl.BlockSpec((tr, c), lambda i, me: (me[0] * (r // tr) + i, 0))


def _place_shard(name, shard, axis, me):
    r, c = shard.shape
    tr = _row_block(r, c, shard.dtype.itemsize)
    shp = (r, c * N_CHIPS) if axis == 1 else (r * N_CHIPS, c)

    def body(me_ref, s_ref, o_ref):
        del me_ref
        o_ref[...] = s_ref[...]

    return pl.pallas_call(
        body, name=name,
        grid_spec=pltpu.PrefetchScalarGridSpec(
            num_scalar_prefetch=1, grid=(r // tr,),
            in_specs=[pl.BlockSpec((tr, c), lambda i, me: (i, 0))], out_specs=_own_block_spec(r, c, axis, tr)),
        out_shape=jax.ShapeDtypeStruct(shp, shard.dtype),
        compiler_params=_cparams(("parallel",)),
    )(me, shard)


class _Split:
    def __init__(self, name, items):
        self.name, self.n = name, len(items)
        self.srcs = [a for a, _, _ in items]
        self.meta, self.land_shapes = [], []
        for arr, kind, axis in items:
            shp = list(arr.shape)
            if kind == "gather":
                per = shp[axis]
                shp[axis] = per * N_CHIPS
                self.land_shapes.append(jax.ShapeDtypeStruct(tuple(shp), arr.dtype))
            else:
                per = shp[axis] // N_CHIPS
                shp[axis] = per
                self.land_shapes.append(jax.ShapeDtypeStruct((N_PEERS, *shp), arr.dtype))
            self.meta.append((kind, axis, per))

    def _src(self, ins, t, chip):
        kind, axis, per = self.meta[t]
        return _axis_slice(ins[t], axis, chip * per, per) if kind == "scatter" else ins[t]

    def _dst(self, lands, t, chip, slot):
        kind, axis, per = self.meta[t]
        return _axis_slice(lands[t], axis, chip * per, per) if kind == "gather" else lands[t].at[slot]

    def landing_zones(self, me):
        return [_place_shard(self.name + "_own", src, axis, me) if kind == "gather" else lax.empty(ls.shape, ls.dtype)
                for src, ls, (kind, axis, _) in zip(self.srcs, self.land_shapes, self.meta)]

    def _copies(self, ins, lands, send_sems, recv_sems, arrivals):
        x, y, c = lax.axis_index("x"), lax.axis_index("y"), lax.axis_index("c")
        me = 2 * x + y
        peers = [(1 - x, y), (x, 1 - y), (1 - x, 1 - y)]
        res = []
        for t in range(self.n):
            for k, (px, py) in enumerate(peers):
                theirs = 2 * px + py
                sems = dict(send_sem=send_sems.at[t * N_PEERS + k], recv_sem=recv_sems.at[t * N_PEERS + k],
                            device_id=(px, py, c), device_id_type=MESH_IDS)
                if arrivals:
                    res.append(pltpu.make_async_remote_copy(src_ref=self._src(ins, t, me), dst_ref=self._dst(lands, t, theirs, k), **sems))
                else:
                    res.append(pltpu.make_async_remote_copy(src_ref=self._src(ins, t, theirs), dst_ref=self._dst(lands, t, me, k), **sems))
        return res

    def start(self, lands, deps=()):
        n, nd = self.n, len(deps)

        def body(*refs):
            ins, lnd = refs[:n], refs[n:2 * n]
            send_sems, recv_sems = refs[2 * n + nd], refs[2 * n + nd + 1]
            token = refs[-1]
            for cp in self._copies(ins, lnd, send_sems, recv_sems, arrivals=False):
                cp.start()
            token[...] = jnp.zeros_like(token)

        hbm = lambda a: pltpu.HBM(a.shape, a.dtype)
        res = pl.pallas_call(
            body, name=self.name + "_start",
            in_specs=[HBM_SPEC] * (2 * n) + [ANY_SPEC] * nd,
            out_specs=[SEM_SPEC, SEM_SPEC] + [HBM_SPEC] * (2 * n) + [pl.BlockSpec(memory_space=pltpu.VMEM)],
            out_shape=[pltpu.SemaphoreType.DMA((N_PEERS * n,)), pltpu.SemaphoreType.DMA((N_PEERS * n,))]
            + [hbm(a) for a in self.srcs] + [hbm(a) for a in self.land_shapes] + [jax.ShapeDtypeStruct((8, 128), F32)],
            input_output_aliases={i: 2 + i for i in range(2 * n)},
            compiler_params=pltpu.CompilerParams(has_side_effects=pltpu.SideEffectType.DATAFLOW_SIDE_EFFECTING),
        )(*[pltpu.with_memory_space_constraint(a, pltpu.HBM) for a in self.srcs],
          *[pltpu.with_memory_space_constraint(a, pltpu.HBM) for a in lands], *deps)
        return res[:-1], res[-1]

    def wait(self, state, after):
        n = self.n
        after = tuple(after) if isinstance(after, (tuple, list)) else (after,)
        send_sems, recv_sems = state[0], state[1]
        srcs, lands = state[2:2 + n], state[2 + n:2 + 2 * n]

        def body(*refs):
            ins, lnd = refs[:n], refs[n:2 * n]
            s_sems, r_sems = refs[2 * n], refs[2 * n + 1]
            for cp in self._copies(ins, lnd, s_sems, r_sems, arrivals=True):
                cp.wait_recv()
            for cp in self._copies(ins, lnd, s_sems, r_sems, arrivals=False):
                cp.wait_send()

        hbm = lambda a: pltpu.HBM(a.shape, a.dtype)
        res = pl.pallas_call(
            body, name=self.name + "_wait",
            in_specs=[HBM_SPEC] * (2 * n) + [SEM_SPEC, SEM_SPEC] + [ANY_SPEC] * len(after),
            out_specs=[HBM_SPEC] * (2 * n),
            out_shape=[hbm(a) for a in self.srcs] + [hbm(a) for a in self.land_shapes],
            input_output_aliases={i: i for i in range(2 * n)},
            compiler_params=pltpu.CompilerParams(has_side_effects=pltpu.SideEffectType.DATAFLOW_SIDE_EFFECTING),
        )(*srcs, *lands, send_sems, recv_sems, *after)
        return res[:n], res[n:]


def _sibling_swap(name, arrays):
    n = len(arrays)

    def body(*refs):
        ins, outs = refs[:n], refs[n:2 * n]
        send_sems, recv_sems = refs[2 * n:]
        sibling = (lax.axis_index("x"), lax.axis_index("y"), 1 - lax.axis_index("c"))
        copies = [pltpu.make_async_remote_copy(src_ref=ins[t], dst_ref=outs[t], send_sem=send_sems.at[t], recv_sem=recv_sems.at[t],
                                               device_id=sibling, device_id_type=MESH_IDS) for t in range(n)]
        for cp in copies:
            cp.start()
        for cp in copies:
            cp.wait()

    any_spec = pl.BlockSpec(memory_space=pl.ANY)
    return pl.pallas_call(
        body,
        name=name,
        in_specs=[any_spec] * n,
        out_specs=[any_spec] * n,
        out_shape=[jax.ShapeDtypeStruct(a.shape, a.dtype) for a in arrays],
        scratch_shapes=[pltpu.SemaphoreType.DMA((n,)), pltpu.SemaphoreType.DMA((n,))],
        compiler_params=pltpu.CompilerParams(has_side_effects=True),
    )(*arrays)


def _row_block(r, c, itemsize=4, target=1 << 20):
    if r % 8 != 0:
        return r
    best = 8
    for tr in range(8, r + 1, 8):
        if r % tr == 0 and tr * c * itemsize <= target:
            best = tr
    return best


def _sum_chips_into(parts, stacked, layer):
    _, r, c = parts.shape
    tr = _row_block(r, c)

    def body(p_ref, s_ref, o_ref):
        del s_ref
        o_ref[...] = ((p_ref[0] + p_ref[1]) + p_ref[2]) + p_ref[3]

    return pl.pallas_call(
        body,
        name="sum_chips",
        grid=(r // tr,),
        in_specs=[pl.BlockSpec((N_CHIPS, tr, c), lambda i: (0, i, 0)), pl.BlockSpec(memory_space=pl.ANY)],
        out_specs=pl.BlockSpec((None, tr, c), lambda i: (layer, i, 0)),
        out_shape=jax.ShapeDtypeStruct(stacked.shape, stacked.dtype),
        input_output_aliases={1: 0},
        compiler_params=_cparams(("parallel",)),
    )(parts, stacked)


def _sum_own_and_peers(me, g, axis, landed):
    _, r, c = landed.shape
    tr = _row_block(r, c)

    def body(me_ref, g_ref, p_ref, o_ref):
        del me_ref
        o_ref[...] = ((g_ref[...].astype(F32) + p_ref[0].astype(F32)) + p_ref[1].astype(F32)) + p_ref[2].astype(F32)

    return pl.pallas_call(
        body, name="sum_chips_own",
        grid_spec=pltpu.PrefetchScalarGridSpec(
            num_scalar_prefetch=1, grid=(r // tr,),
            in_specs=[_own_block_spec(r, c, axis, tr), pl.BlockSpec((N_PEERS, tr, c), lambda i, me: (0, i, 0))],
            out_specs=pl.BlockSpec((tr, c), lambda i, me: (i, 0))),
        out_shape=jax.ShapeDtypeStruct((r, c), F32),
        compiler_params=_cparams(("parallel",)),
    )(me, g, landed)


def _adamw_math(w, m, v, g):
    m_new = ADAM_B1 * m + (1.0 - ADAM_B1) * g
    v_new = ADAM_B2 * v + (1.0 - ADAM_B2) * jnp.square(g)
    m_hat = m_new / (1.0 - ADAM_B1 ** ADAM_STEP)
    v_hat = v_new / (1.0 - ADAM_B2 ** ADAM_STEP)
    return -ADAM_LR * (m_hat / (jnp.sqrt(v_hat) + ADAM_EPS) + ADAM_WD * w), m_new, v_new


def _adamw(w, m, v, g_a, g_b):
    L, r, c = w.shape
    tr = _row_block(r, c, target=1 << 19)

    def body(w_ref, m_ref, v_ref, ga_ref, gb_ref, g_ref, d_ref, nm_ref, nv_ref):
        g = ga_ref[...] + gb_ref[...]
        g_ref[...] = g
        d_ref[...], nm_ref[...], nv_ref[...] = _adamw_math(w_ref[...], m_ref[...], v_ref[...], g)

    spec = pl.BlockSpec((None, tr, c), lambda l, i: (l, i, 0))
    return pl.pallas_call(
        body,
        name="adamw",
        grid=(L, r // tr),
        in_specs=[spec] * 5,
        out_specs=[spec] * 4,
        out_shape=[jax.ShapeDtypeStruct(w.shape, F32)] * 4,
        compiler_params=_cparams(("parallel", "parallel")),
    )(w, m, v, g_a, g_b)


def _adamw_layer(w, m, v, g_a, g_b, layer, outs):
    L, r, c = w.shape
    tr = _row_block(r, c, target=1 << 19)
    n_prev = 0 if outs is None else 4

    def body(w_ref, m_ref, v_ref, ga_ref, gb_ref, *rest):
        g_ref, d_ref, nm_ref, nv_ref = rest[n_prev:]
        g = ga_ref[...] + gb_ref[...]
        g_ref[...] = g
        d_ref[...], nm_ref[...], nv_ref[...] = _adamw_math(w_ref[...], m_ref[...], v_ref[...], g)

    at_layer = pl.BlockSpec((None, tr, c), lambda i: (layer, i, 0))
    flat = pl.BlockSpec((tr, c), lambda i: (i, 0))
    return pl.pallas_call(
        body,
        name="adamw_layer",
        grid=(r // tr,),
        in_specs=[at_layer] * 3 + [flat] * 2 + [ANY_SPEC] * n_prev,
        out_specs=[at_layer] * 4,
        out_shape=[jax.ShapeDtypeStruct(w.shape, F32)] * 4,
        input_output_aliases={5 + k: k for k in range(n_prev)},
        compiler_params=_cparams(("parallel",)),
    )(w, m, v, g_a, g_b, *(outs or ()))


SHARDED = (("w_in", 1), ("conv_w", 1), ("w_mem_k", 0), ("w_mem_v", 0), ("w_branch", 1), ("w_o", 0), ("w_up", 1), ("w_down", 0))
SMALL = ("lower_bounds", "hg_norm_w", "b_gate", "ln1_g", "ln1_b", "ln2_g", "ln2_b")
WEIGHT_ORDER = ("lower_bounds", "w_in", "conv_w", "hg_norm_w", "w_mem_k", "w_mem_v", "w_branch", "b_gate", "w_o", "ln1_g", "ln1_b",
                "w_up", "w_down", "ln2_g", "ln2_b")


def kernel(x, mem, lower_bounds, w_in, conv_w, hg_norm_w, w_mem_k, w_mem_v, w_branch, b_gate, w_o, ln1_g, ln1_b, w_up, w_down, ln2_g, ln2_b, loss_target, m_lower_bounds, m_w_in, m_conv_w, m_hg_norm_w, m_w_mem_k, m_w_mem_v, m_w_branch, m_b_gate, m_w_o, m_ln1_g, m_ln1_b, m_w_up, m_w_down, m_ln2_g, m_ln2_b, v_lower_bounds, v_w_in, v_conv_w, v_hg_norm_w, v_w_mem_k, v_w_mem_v, v_w_branch, v_b_gate, v_w_o, v_ln1_g, v_ln1_b, v_w_up, v_w_down, v_ln2_g, v_ln2_b):
    bl, seq, d = x.shape
    depth = w_in.shape[0]
    weights = dict(lower_bounds=lower_bounds, w_in=w_in, conv_w=conv_w, hg_norm_w=hg_norm_w, w_mem_k=w_mem_k, w_mem_v=w_mem_v,
                   w_branch=w_branch, b_gate=b_gate, w_o=w_o, ln1_g=ln1_g, ln1_b=ln1_b, w_up=w_up, w_down=w_down, ln2_g=ln2_g, ln2_b=ln2_b)
    mom_m = dict(lower_bounds=m_lower_bounds, w_in=m_w_in, conv_w=m_conv_w, hg_norm_w=m_hg_norm_w, w_mem_k=m_w_mem_k, w_mem_v=m_w_mem_v,
                 w_branch=m_w_branch, b_gate=m_b_gate, w_o=m_w_o, ln1_g=m_ln1_g, ln1_b=m_ln1_b, w_up=m_w_up, w_down=m_w_down,
                 ln2_g=m_ln2_g, ln2_b=m_ln2_b)
    mom_v = dict(lower_bounds=v_lower_bounds, w_in=v_w_in, conv_w=v_conv_w, hg_norm_w=v_hg_norm_w, w_mem_k=v_w_mem_k, w_mem_v=v_w_mem_v,
                 w_branch=v_w_branch, b_gate=v_b_gate, w_o=v_w_o, ln1_g=v_ln1_g, ln1_b=v_ln1_b, w_up=v_w_up, w_down=v_w_down,
                 ln2_g=v_ln2_g, ln2_b=v_ln2_b)

    def shard2d(name, l):
        w = weights[name][l]
        if name == "w_branch":
            return w.reshape(N_BRANCH * W, w.shape[-1]).astype(BF16)
        return w if name == "conv_w" else w.astype(BF16)

    me = _my_chip()

    shard_axis = dict(SHARDED)

    def prepare_exchange(name, kind, items):
        ex = _Split(name, [(arr, kind, shard_axis[nm]) for nm, arr in items])
        return ex, ex.landing_zones(me), [nm for nm, _ in items]

    def launch(prepared, deps=()):
        ex, lands, names = prepared
        state, token = ex.start(lands, deps)
        return ex, state, names, token

    def start_exchange(name, kind, items, deps=()):
        return launch(prepare_exchange(name, kind, items), deps)

    def prepare_gathers(l):
        return (prepare_exchange(f"gather_in_l{l}", "gather", [("w_in", shard2d("w_in", l))]),
                prepare_exchange(f"gather_rest_l{l}", "gather", [(nm, shard2d(nm, l)) for nm, _ in SHARDED if nm != "w_in"]))

    def start_gathers(prepared, deps=()):
        first = launch(prepared[0], deps)
        return first, launch(prepared[1], (first[3],))

    def gathered(pend, after):
        ex, state, names, _ = pend
        return dict(zip(names, ex.wait(state, after=after)[1]))

    pending = start_gathers(prepare_gathers(0))
    tokens = (pending[0][3], pending[1][3])
    tokens, x, mem, loss_target, weights, mom_m, mom_v = lax.optimization_barrier((tokens, x, mem, loss_target, weights, mom_m, mom_v))
    pending = ((*pending[0][:3], tokens[0]), (*pending[1][:3], tokens[1]))
    lower_bounds = weights["lower_bounds"]

    x2d, mem2, t2d = x.reshape(bl * seq, d), mem.reshape(-1, d), loss_target.reshape(bl * seq, d)
    alpha = (2.0 * depth) ** 0.25
    soft, lb_all = _lower_bounds_fwd(lower_bounds)

    prepared = [None] + [prepare_gathers(l) for l in range(1, depth)]
    early = [x2d.astype(BF16), lb_all] + [z for prep in prepared[1:] for _, lands, _ in prep for z in lands]

    h, hb, saved, layer_wts = x2d, early[0], [], []
    for l in range(depth):
        first, rest = pending
        w_in_l = gathered(first, early if l == 0 else h)["w_in"]

        def rest_fn(after, l=l, rest=rest):
            wts = gathered(rest, after)
            for name in ("hg_norm_w", "b_gate", "ln1_g", "ln1_b", "ln2_g", "ln2_b"):
                wts[name] = weights[name][l][None, :]
            return wts

        deps = (rest[3],)
        if l + 1 < depth:
            pending = start_gathers(prepared[l + 1], (w_in_l, rest[3]))
            deps += (pending[0][3], pending[1][3])
        h, hb, sv, wts = _layer_fwd(h, hb, mem2, lb_all[l:l + 1], w_in_l, rest_fn, bl=bl, seq=seq, alpha=alpha, deps=deps)
        saved.append(sv)
        layer_wts.append(wts)
    loss, dh = _loss_head(h, t2d)

    shape3 = {name: (depth, weights[name].size // (depth * weights[name].shape[-1]), weights[name].shape[-1]) for name, _ in SHARDED}
    partial = [dict() for _ in range(depth)]
    smalls = [None] * depth
    outs = {name: None for name, _ in SHARDED}

    def finish_reduce(pend, l, after):
        ex, state, names, _ = pend
        sent, got = ex.wait(state, after=after)
        for nm, g_full, landed in zip(names, sent, got):
            partial[l][nm] = _sum_own_and_peers(me, g_full, shard_axis[nm], landed)

    def optimizer_step(l):
        names = [name for name, _ in SHARDED]
        theirs = _sibling_swap(f"swap_partials_l{l}", [partial[l][nm] for nm in names])
        for nm, other in zip(names, theirs):
            outs[nm] = _adamw_layer(weights[nm].reshape(shape3[nm]), mom_m[nm].reshape(shape3[nm]), mom_v[nm].reshape(shape3[nm]),
                                    partial[l][nm], other, l, outs[nm])
        return tuple(outs[nm][0] for nm in names)

    pending_mix, deps = [], ()
    dz2, dz2b, dg2, db2 = _ln_bwd(dh, saved[-1]["xhat2"], saved[-1]["rstd2"], layer_wts[-1]["ln2_g"])
    for l in reversed(range(depth)):
        dz1, dz1b, g_mlp = _mlp_bwd(dz2, dz2b, saved[l], layer_wts[l], alpha=alpha, deps=deps)
        g_mlp["ln2_g"], g_mlp["ln2_b"] = dg2[0:1], db2[0:1]
        pending_mlp = start_exchange(f"reduce_mlp_l{l}", "scatter", [(nm, g_mlp[nm]) for nm in ("w_up", "w_down")])
        deps = (pending_mlp[3],)
        if pending_mix:
            for pend in pending_mix:
                finish_reduce(pend, l + 1, dz1)
            deps += optimizer_step(l + 1)
        pending_mix = []

        def send(names, g, l=l, pending_mix=pending_mix):
            pend = start_exchange(f"reduce_{names[0]}_l{l}", "scatter", [(nm, g[nm]) for nm in names])
            pending_mix.append(pend)
            return pend[3]

        below = (saved[l - 1]["xhat2"], saved[l - 1]["rstd2"], layer_wts[l - 1]["ln2_g"]) if l > 0 else None
        out, g = _mix_bwd(dz1, dz1b, saved[l], mem2, lb_all[l:l + 1], layer_wts[l], bl=bl, seq=seq, alpha=alpha, send=send,
                          below=below, deps=deps)
        if l > 0:
            dz2, dz2b, dg2, db2 = out
        else:
            dh = out
        finish_reduce(pending_mlp, l, out[0] if l > 0 else out)
        deps = ()
        g.update(g_mlp, lower_bounds=g["lb"])
        smalls[l] = jnp.concatenate([g[nm] for nm in SMALL], axis=1)
    small_parts = _chip_exchange("reduce_small", [(jnp.stack(smalls), "bcast", 0)])[0]
    small_sum = _sum_chips_into(small_parts.reshape(N_CHIPS, depth, -1), jnp.zeros((1, depth, small_parts.shape[-1]), F32), 0)
    small_sum = small_sum.reshape(depth, 1, -1)
    small_theirs = _sibling_swap("swap_small", [small_sum])[0]
    for pend in pending_mix:
        finish_reduce(pend, 0, small_theirs)
    optimizer_step(0)

    outs = {name: [r.reshape(weights[name].shape) for r in res] for name, res in outs.items()}
    off = 0
    for name in SMALL:
        n = weights[name].shape[1]
        mine, other = small_sum[:, :, off:off + n], small_theirs[:, :, off:off + n]
        off += n
        if name == "lower_bounds":
            mine = _lower_bounds_bwd(soft, mine[:, 0, :])[:, None, :]
            other = _lower_bounds_bwd(soft, other[:, 0, :])[:, None, :]
        shp = (depth, 1, n)
        res = _adamw(weights[name].reshape(shp), mom_m[name].reshape(shp), mom_v[name].reshape(shp), mine, other)
        outs[name] = [r.reshape(weights[name].shape) for r in res]
    assert off == small_sum.shape[-1]

    total_loss = lax.psum(loss[0, 0], ("x", "y", "c"))
    result = [total_loss, dh.reshape(bl, seq, d)]
    for k in range(4):
        result += [outs[name][k] for name in WEIGHT_ORDER]
    return tuple(result)
```

```python
import functools

import jax
import jax.numpy as jnp
from jax import lax
from jax.experimental import pallas as pl
from jax.experimental.pallas import tpu as pltpu

F32 = jnp.float32
BF16 = jnp.bfloat16

HG_HEADS = 4
HG_F = 128
HG_CHUNK = 32
MEM_HEADS = 4
MEM_HEAD_DIM = 128
BRANCH_WIDTH = 512
N_BRANCH = 3
CONV_K = 3
LN_EPS = 1e-5
RMS_EPS = 1e-6
ADAM_LR = 0.001
ADAM_B1 = 0.9
ADAM_B2 = 0.999
ADAM_EPS = 1e-08
ADAM_WD = 0.01
ADAM_STEP = 10

VMEM_LIMIT = 48 * 1024 * 1024


def _cparams(sem):
    return pltpu.CompilerParams(dimension_semantics=sem, vmem_limit_bytes=VMEM_LIMIT)


def _dot(a, b, dims):
    return lax.dot_general(a, b, (dims, ((), ())), preferred_element_type=F32)


NN = ((1,), (0,))
NT = ((1,), (1,))
TN = ((0,), (0,))


def _pick(n, pref):
    for t in pref:
        if n % t == 0:
            return t
    return n


ANY_SPEC = pl.BlockSpec(memory_space=pl.ANY)


def _matmul(name, a, b, *, mode, out_dtype=F32, a_fn=None, a_extra=(), epi_fn=None, epi_extra=(), n_out=1, out_kinds=None,
            tm=512, tn=1024, tk=1024, deps=()):
    M, K = a.shape
    N = b.shape[1] if mode == "nn" else b.shape[0]
    tm, tn, tk = _pick(M, (tm, 256, 128, 8)), _pick(N, (tn, 896, 512, 256, 128)), _pick(K, (tk, 512, 256, 128))
    nk = K // tk
    n_ax, n_ex = len(a_extra), len(epi_extra)
    n_in = 2 + n_ax + n_ex + len(deps)
    out_dtypes = out_dtype if isinstance(out_dtype, (tuple, list)) else (out_dtype,) * n_out
    out_kinds = out_kinds or ("tile",) * n_out

    def body(*refs):
        a_ref, b_ref = refs[0], refs[1]
        ax_refs = refs[2:2 + n_ax]
        ex_refs = refs[2 + n_ax:2 + n_ax + n_ex]
        o_refs = refs[n_in:n_in + n_out]
        at = a_ref[...]
        at = a_fn(at, *[r[...] for r in ax_refs]) if a_fn is not None else at.astype(BF16)
        part = _dot(at, b_ref[...].astype(BF16), NN if mode == "nn" else NT)

        def finish(acc):
            outs = epi_fn(acc, *[r[...] for r in ex_refs]) if epi_fn is not None else (acc,)
            for o_ref, o, kind in zip(o_refs, outs, out_kinds):
                if kind == "rowsum":
                    @pl.when(pl.program_id(1) == 0)
                    def _(o_ref=o_ref):
                        o_ref[...] = jnp.zeros_like(o_ref)

                    o_ref[0:1, :] += o
                else:
                    o_ref[...] = o.astype(o_ref.dtype)

        if nk == 1:
            finish(part)
            return
        acc_ref = refs[-1]
        k = pl.program_id(2)

        @pl.when(k == 0)
        def _():
            acc_ref[...] = part

        @pl.when(jnp.logical_and(k > 0, k < nk - 1))
        def _():
            acc_ref[...] += part

        @pl.when(k == nk - 1)
        def _():
            finish(acc_ref[...] + part)

    b_mode = dict(pipeline_mode=pl.Buffered(1)) if (nk == 1 and N == tn) else {}
    in_specs = [pl.BlockSpec((tm, tk), lambda j, i, k: (i, k)),
                pl.BlockSpec((tk, tn), lambda j, i, k: (k, j), **b_mode) if mode == "nn"
                else pl.BlockSpec((tn, tk), lambda j, i, k: (j, k), **b_mode)]
    in_specs += [pl.BlockSpec((1, tk), lambda j, i, k: (0, k)) for _ in a_extra]
    for e in epi_extra:
        if e.shape[0] == 1:
            in_specs.append(pl.BlockSpec((1, tn), lambda j, i, k: (0, j)))
        elif e.shape[1] == 1:
            in_specs.append(pl.BlockSpec((tm, 1), lambda j, i, k: (i, 0)))
        else:
            in_specs.append(pl.BlockSpec((tm, tn), lambda j, i, k: (i, j)))
    in_specs += [ANY_SPEC] * len(deps)
    out_specs, out_shapes = [], []
    for kind, dt in zip(out_kinds, out_dtypes):
        if kind == "col":
            out_specs.append(pl.BlockSpec((tm, 1), lambda j, i, k: (i, 0)))
            out_shapes.append(jax.ShapeDtypeStruct((M, 1), dt))
        elif kind == "rowsum":
            out_specs.append(pl.BlockSpec((8, tn), lambda j, i, k: (0, j)))
            out_shapes.append(jax.ShapeDtypeStruct((8, N), dt))
        else:
            out_specs.append(pl.BlockSpec((tm, tn), lambda j, i, k: (i, j)))
            out_shapes.append(jax.ShapeDtypeStruct((M, N), dt))
    out = pl.pallas_call(
        body,
        name=name,
        grid=(N // tn, M // tm, nk),
        in_specs=in_specs,
        out_specs=out_specs,
        out_shape=out_shapes,
        scratch_shapes=[pltpu.VMEM((tm, tn), F32)] if nk > 1 else [],
        compiler_params=_cparams(("arbitrary", "arbitrary", "arbitrary")),
    )(a, b, *a_extra, *epi_extra, *deps)
    return out[0] if n_out == 1 else out


def _matmul_tn(name, a, b, *, a_fn=None, a_extra=(), a_cols=None, b_cols=None, ta=1024, tb=1024, tt=1024, out_dtype=F32, deps=()):
    T = a.shape[0]
    a0, Ka = a_cols if a_cols is not None else (0, a.shape[1])
    b0, Nb = b_cols if b_cols is not None else (0, b.shape[1])
    ta, tb, tt = _pick(Ka, (ta, 512, 256, 128)), _pick(Nb, (tb, 896, 512, 256, 128)), _pick(T, (tt, 512, 256, 128))
    assert a0 % ta == 0 and b0 % tb == 0
    a0, b0 = a0 // ta, b0 // tb
    nt = T // tt
    n_ax = len(a_extra)

    def body(*refs):
        a_ref, b_ref = refs[0], refs[1]
        ax_refs = refs[2:2 + n_ax]
        o_ref = refs[2 + n_ax + len(deps)]
        acc_ref = refs[-1]
        t = pl.program_id(2)
        at = a_ref[...]
        at = a_fn(at, *[r[...] for r in ax_refs]) if a_fn is not None else at.astype(BF16)
        part = _dot(at, b_ref[...].astype(BF16), TN)

        @pl.when(t == 0)
        def _():
            acc_ref[...] = part

        @pl.when(jnp.logical_and(t > 0, t < nt - 1))
        def _():
            acc_ref[...] += part

        @pl.when(t == nt - 1)
        def _():
            o_ref[...] = (acc_ref[...] + part if nt > 1 else part).astype(o_ref.dtype)

    in_specs = [pl.BlockSpec((tt, ta), lambda i, j, t: (t, a0 + i)), pl.BlockSpec((tt, tb), lambda i, j, t: (t, b0 + j))]
    in_specs += [pl.BlockSpec((1, ta), lambda i, j, t: (0, a0 + i)) for _ in a_extra]
    in_specs += [ANY_SPEC] * len(deps)
    return pl.pallas_call(
        body,
        name=name,
        grid=(Ka // ta, Nb // tb, nt),
        in_specs=in_specs,
        out_specs=pl.BlockSpec((ta, tb), lambda i, j, t: (i, j)),
        out_shape=jax.ShapeDtypeStruct((Ka, Nb), out_dtype),
        scratch_shapes=[pltpu.VMEM((ta, tb), F32)],
        compiler_params=_cparams(("parallel", "parallel", "arbitrary")),
    )(a, b, *a_extra, *deps)


W = BRANCH_WIDTH
C_CB, C_CC, C_CH, C_HQ, C_HF, C_HI, C_HG, C_MQ, N_MIX = 0, W, 2 * W, 3 * W, 4 * W, 5 * W, 6 * W, 7 * W, 8 * W
TS_MIX = 256
PREV_ROWS = 16
KEEP_NAMES = ("sq", "qs", "k", "sig", "f", "ea", "eb", "eq", "ek")


def _sigmoid(x):
    return jax.nn.sigmoid(x)


def _chunk_pos(shape):
    return lax.broadcasted_iota(jnp.int32, shape, 0) & (HG_CHUNK - 1)


def _seg_cumsum(x, pos):
    sh = 1
    while sh < HG_CHUNK:
        x = x + jnp.where(pos >= sh, pltpu.roll(x, sh, 0), 0.0)
        sh *= 2
    return x


def _seg_rev_cumsum(x, pos):
    n = x.shape[0]
    sh = 1
    while sh < HG_CHUNK:
        x = x + jnp.where(pos < HG_CHUNK - sh, pltpu.roll(x, n - sh, 0), 0.0)
        sh *= 2
    return x


def _chunk_mask(ts):
    r = lax.broadcasted_iota(jnp.int32, (ts, ts), 0)
    c = lax.broadcasted_iota(jnp.int32, (ts, ts), 1)
    return jnp.logical_and((r // HG_CHUNK) == (c // HG_CHUNK), c <= r)


def _hgrn_gates(p_ref, lb):
    q = p_ref[:, C_HQ:C_HQ + W].astype(F32)
    fl = p_ref[:, C_HF:C_HF + W].astype(F32)
    sig = _sigmoid(fl)
    f = lb + (1.0 - lb) * sig
    logf = jnp.log(f)
    k = (1.0 - lb) * _sigmoid(-fl)
    sq = _sigmoid(q)
    qs = q * sq
    return q, sq, qs, sig, f, logf, k


def _hgrn_decays(logf, bc_sc, ts):
    pos = _chunk_pos(logf.shape)
    bc = _seg_cumsum(logf, pos)
    bc_sc[...] = bc
    nc = ts // HG_CHUNK
    bref = jnp.concatenate(
        [jnp.broadcast_to(bc_sc[n * HG_CHUNK + HG_CHUNK // 2 - 1:n * HG_CHUNK + HG_CHUNK // 2, :], (HG_CHUNK, W)) for n in range(nc)], axis=0)
    blast = jnp.concatenate(
        [jnp.broadcast_to(bc_sc[(n + 1) * HG_CHUNK - 1:(n + 1) * HG_CHUNK, :], (HG_CHUNK, W)) for n in range(nc)], axis=0)
    return pos, bc, bref, blast


def _conv_shift_down(u, carry_ref, row):
    n = carry_ref.shape[0]
    last, before = carry_ref[n - 1:n, :], carry_ref[n - 2:n - 1, :]
    u1 = jnp.where(row == 0, last, pltpu.roll(u, 1, 0))
    u2 = jnp.where(row == 0, before, jnp.where(row == 1, last, pltpu.roll(u, 2, 0)))
    return u1, u2


def _attn_probs(qh, kh):
    s = _dot(qh, kh, NT) * (MEM_HEAD_DIM ** -0.5)
    e = jnp.exp(s - jnp.max(s, axis=-1, keepdims=True))
    return e / jnp.sum(e, axis=-1, keepdims=True)


def _mixer_fwd(p, mk, mv, lb, conv_w, norm_w, *, bl, seq):
    T = p.shape[0]
    ts = TS_MIX
    ns = seq // ts
    nc = ts // HG_CHUNK
    ml = mk.shape[0] // bl

    def body(p_ref, mk_ref, mv_ref, lb_ref, cw_ref, nw_ref, y_ref, st_ref, opre_ref, state_sc, carry_sc, bc_sc):
        @pl.when(pl.program_id(1) == 0)
        def _():
            state_sc[...] = jnp.zeros_like(state_sc)
            carry_sc[...] = jnp.zeros_like(carry_sc)

        cb, cc, ch = (p_ref[:, c0:c0 + W].astype(F32) for c0 in (C_CB, C_CC, C_CH))
        u = cc * ch
        row = lax.broadcasted_iota(jnp.int32, (ts, W), 0)
        u1, u2 = _conv_shift_down(u, carry_sc, row)
        yconv = u2 * cw_ref[0:1, :] + u1 * cw_ref[1:2, :] + u * cw_ref[2:3, :]
        y_ref[:, 0:W] = (cb * yconv).astype(BF16)
        carry_sc[...] = u[ts - 8:ts, :]

        lbv = lb_ref[...]
        _, _, qs, _, _, logf, k = _hgrn_gates(p_ref, lbv)
        pos, bc, bref, blast = _hgrn_decays(logf, bc_sc, ts)
        a_all = (qs * jnp.exp(bc - bref)).astype(BF16)
        bk_all = (k * jnp.exp(bref - bc)).astype(BF16)
        qin_all = (qs * jnp.exp(bc)).astype(BF16)
        kout_all = (k * jnp.exp(blast - bc)).astype(BF16)
        v_all = p_ref[:, C_HI:C_HI + W].astype(BF16)
        mask = _chunk_mask(ts)
        heads = [slice(h * HG_F, (h + 1) * HG_F) for h in range(HG_HEADS)]
        st = [state_sc[h] for h in range(HG_HEADS)]
        o_inter = [[] for _ in range(HG_HEADS)]
        for n in range(nc):
            rows = slice(n * HG_CHUNK, (n + 1) * HG_CHUNK)
            for h, hs in enumerate(heads):
                st_ref[n, h] = st[h]
                o_inter[h].append(_dot(qin_all[rows, hs], st[h].astype(BF16), NT))
                kv = _dot(v_all[rows, hs], kout_all[rows, hs], TN)
                decay = jnp.exp(bc_sc[(n + 1) * HG_CHUNK - 1:(n + 1) * HG_CHUNK, hs])
                st[h] = st[h] * decay + kv
        for h in range(HG_HEADS):
            state_sc[h] = st[h]
        scores = [_dot(a_all[:, hs], bk_all[:, hs], NT) for hs in heads]
        scores = [jnp.where(mask, s, 0.0).astype(BF16) for s in scores]
        outs = [_dot(scores[h], v_all[:, hs], NN) + jnp.concatenate(o_inter[h], axis=0) for h, hs in enumerate(heads)]
        for h, hs in enumerate(heads):
            o = outs[h]
            opre_ref[:, hs] = o
            on = o * lax.rsqrt(jnp.mean(o * o, axis=-1, keepdims=True) + RMS_EPS) * nw_ref[...]
            g = p_ref[:, C_HG + h * HG_F:C_HG + (h + 1) * HG_F].astype(F32)
            y_ref[:, W + h * HG_F:W + (h + 1) * HG_F] = (on * (g * _sigmoid(g))).astype(BF16)

        mheads = [slice(h * MEM_HEAD_DIM, (h + 1) * MEM_HEAD_DIM) for h in range(MEM_HEADS)]
        probs = [_attn_probs(p_ref[:, C_MQ + h * MEM_HEAD_DIM:C_MQ + (h + 1) * MEM_HEAD_DIM].astype(BF16), mk_ref[:, hs])
                 for h, hs in enumerate(mheads)]
        for h, hs in enumerate(mheads):
            y_ref[:, 2 * W + h * MEM_HEAD_DIM:2 * W + (h + 1) * MEM_HEAD_DIM] = _dot(
                probs[h].astype(BF16), mv_ref[:, hs], NN).astype(BF16)

    return pl.pallas_call(
        body,
        name="mixer_fwd",
        grid=(bl, ns),
        in_specs=[
            pl.BlockSpec((ts, N_MIX), lambda b, s: (b * ns + s, 0)),
            pl.BlockSpec((ml, W), lambda b, s: (b, 0)),
            pl.BlockSpec((ml, W), lambda b, s: (b, 0)),
            pl.BlockSpec((1, W), lambda b, s: (0, 0)),
            pl.BlockSpec((CONV_K, W), lambda b, s: (0, 0)),
            pl.BlockSpec((1, HG_F), lambda b, s: (0, 0)),
        ],
        out_specs=[
            pl.BlockSpec((ts, 3 * W), lambda b, s: (b * ns + s, 0)),
            pl.BlockSpec((nc, HG_HEADS, HG_F, HG_F), lambda b, s: (b * ns + s, 0, 0, 0)),
            pl.BlockSpec((ts, W), lambda b, s: (b * ns + s, 0)),
        ],
        out_shape=[
            jax.ShapeDtypeStruct((T, 3 * W), BF16),
            jax.ShapeDtypeStruct((T // HG_CHUNK, HG_HEADS, HG_F, HG_F), F32),
            jax.ShapeDtypeStruct((T, W), F32),
        ],
        scratch_shapes=[pltpu.VMEM((HG_HEADS, HG_F, HG_F), F32), pltpu.VMEM((8, W), F32), pltpu.VMEM((ts, W), F32)],
        compiler_params=_cparams(("arbitrary", "arbitrary")),
    )(p, mk, mv, lb, conv_w, norm_w)


def _mixer_bwd(p, dy, dp_gates, st, opre, mk, mv, lb, conv_w, norm_w, *, bl, seq, deps=()):
    T, nin = p.shape
    ts = TS_MIX
    ns = seq // ts
    nc = ts // HG_CHUNK
    ml = mk.shape[0] // bl
    mid, last = HG_CHUNK // 2 - 1, HG_CHUNK - 1

    def body(p_ref, pprev_ref, dy_ref, dpin_ref, st_ref, opre_ref, mk_ref, mv_ref, lb_ref, cw_ref, nw_ref, *rest):
        (dp_ref, dmk_ref, dmv_ref, dcw_ref, dnw_ref, dlb_ref, dstate_sc, carry_sc, uprev_sc, ab_sc, bkb_sc, qinb_sc, koutb_sc,
         dob_sc, dv_sc, da_sc, dbk_sc, dqin_sc, dkout_sc, dec_sc, ddec_sc, *keep_scs) = rest[len(deps):]
        del dpin_ref
        b, s = pl.program_id(0), pl.program_id(1)

        @pl.when(s == 0)
        def _():
            dstate_sc[...] = jnp.zeros_like(dstate_sc)
            carry_sc[...] = jnp.zeros_like(carry_sc)
            dmk_ref[...] = jnp.zeros_like(dmk_ref)
            dmv_ref[...] = jnp.zeros_like(dmv_ref)

        @pl.when(jnp.logical_and(b == 0, s == 0))
        def _():
            dcw_ref[...] = jnp.zeros_like(dcw_ref)
            dnw_ref[...] = jnp.zeros_like(dnw_ref)
            dlb_ref[...] = jnp.zeros_like(dlb_ref)

        cb, cc, ch = (p_ref[:, c0:c0 + W].astype(F32) for c0 in (C_CB, C_CC, C_CH))
        u = cc * ch
        row = lax.broadcasted_iota(jnp.int32, (ts, W), 0)
        uprev = pprev_ref[:, C_CC:C_CC + W].astype(F32) * pprev_ref[:, C_CH:C_CH + W].astype(F32)
        uprev_sc[...] = jnp.where(s == ns - 1, 0.0, uprev)
        u1, u2 = _conv_shift_down(u, uprev_sc, row)
        w0, w1, w2 = cw_ref[0:1, :], cw_ref[1:2, :], cw_ref[2:3, :]
        dya = dy_ref[:, 0:W].astype(F32)
        dp_ref[:, C_CB:C_CB + W] = (dya * (u2 * w0 + u1 * w1 + u * w2)).astype(BF16)
        dv = cb * dya
        dv1 = jnp.where(row == ts - 1, carry_sc[0:1, :], pltpu.roll(dv, ts - 1, 0))
        dv2 = jnp.where(row == ts - 1, carry_sc[1:2, :], jnp.where(row == ts - 2, carry_sc[0:1, :], pltpu.roll(dv, ts - 2, 0)))
        du = dv * w2 + dv1 * w1 + dv2 * w0
        dp_ref[:, C_CC:C_CC + W] = (du * ch).astype(BF16)
        dp_ref[:, C_CH:C_CH + W] = (du * cc).astype(BF16)
        dcw_ref[0:1, :] += jnp.sum(dv * u2, axis=0, keepdims=True)
        dcw_ref[1:2, :] += jnp.sum(dv * u1, axis=0, keepdims=True)
        dcw_ref[2:3, :] += jnp.sum(dv * u, axis=0, keepdims=True)
        carry_sc[...] = dv[0:8, :]

        mask = _chunk_mask(ts)
        pos_c = _chunk_pos((HG_CHUNK, HG_F))
        nw = nw_ref[...]

        def block(n, h):
            rows = slice(n * HG_CHUNK, (n + 1) * HG_CHUNK)
            return rows, slice(h * HG_F, (h + 1) * HG_F)

        keep = dict(zip(KEEP_NAMES, keep_scs))

        def gates(rows, h):
            lbh = lb_ref[:, h * HG_F:(h + 1) * HG_F]
            q = p_ref[rows, C_HQ + h * HG_F:C_HQ + (h + 1) * HG_F].astype(F32)
            fl = p_ref[rows, C_HF + h * HG_F:C_HF + (h + 1) * HG_F].astype(F32)
            sig = _sigmoid(fl)
            f = lbh + (1.0 - lbh) * sig
            k = (1.0 - lbh) * _sigmoid(-fl)
            sq = _sigmoid(q)
            qs = q * sq
            bc = _seg_cumsum(jnp.log(f), pos_c)
            bref = jnp.sum(jnp.where(pos_c == mid, bc, 0.0), axis=0, keepdims=True)
            blast = jnp.sum(jnp.where(pos_c == last, bc, 0.0), axis=0, keepdims=True)
            ea, eb, eq, ek = jnp.exp(bc - bref), jnp.exp(bref - bc), jnp.exp(bc), jnp.exp(blast - bc)
            return dict(sq=sq, qs=qs, k=k, sig=sig, f=f, ea=ea, eb=eb, eq=eq, ek=ek), blast

        dnw = jnp.zeros((1, HG_F), F32)
        for n in range(nc):
            for h in range(HG_HEADS):
                rows, hs = block(n, h)
                fw, blast = gates(rows, h)
                for name in KEEP_NAMES:
                    keep[name][rows, hs] = fw[name]
                ab_sc[rows, hs] = (fw["qs"] * fw["ea"]).astype(BF16)
                bkb_sc[rows, hs] = (fw["k"] * fw["eb"]).astype(BF16)
                qinb_sc[rows, hs] = (fw["qs"] * fw["eq"]).astype(BF16)
                koutb_sc[rows, hs] = (fw["k"] * fw["ek"]).astype(BF16)
                dec_sc[n:n + 1, hs] = jnp.exp(blast)
                o = opre_ref[rows, hs]
                g = p_ref[rows, C_HG + h * HG_F:C_HG + (h + 1) * HG_F].astype(F32)
                sg = _sigmoid(g)
                r = lax.rsqrt(jnp.mean(o * o, axis=-1, keepdims=True) + RMS_EPS)
                dyb = dy_ref[rows, W + h * HG_F:W + (h + 1) * HG_F].astype(F32)
                dp_ref[rows, C_HG + h * HG_F:C_HG + (h + 1) * HG_F] = (
                    dyb * (o * r * nw) * (sg * (1.0 + g * (1.0 - sg)))).astype(BF16)
                don = dyb * (g * sg)
                dnw = dnw + jnp.sum(don * o * r, axis=0, keepdims=True)
                dn = don * nw
                dob_sc[rows, hs] = (r * (dn - o * (r * r) * jnp.mean(dn * o, axis=-1, keepdims=True))).astype(BF16)
        dnw_ref[0:1, :] += dnw

        heads = [slice(h * HG_F, (h + 1) * HG_F) for h in range(HG_HEADS)]
        scores = [_dot(ab_sc[:, hs], bkb_sc[:, hs], NT) for hs in heads]
        dscores = [_dot(dob_sc[:, hs], p_ref[:, C_HI + h * HG_F:C_HI + (h + 1) * HG_F].astype(BF16), NT)
                   for h, hs in enumerate(heads)]
        scores = [jnp.where(mask, s, 0.0).astype(BF16) for s in scores]
        dscores = [jnp.where(mask, s, 0.0).astype(BF16) for s in dscores]
        for h, hs in enumerate(heads):
            dv_sc[:, hs] = _dot(scores[h], dob_sc[:, hs], TN)
            da_sc[:, hs] = _dot(dscores[h], bkb_sc[:, hs], NN)
            dbk_sc[:, hs] = _dot(dscores[h], ab_sc[:, hs], TN)
        dst = [dstate_sc[h] for h in range(HG_HEADS)]
        for n in reversed(range(nc)):
            for h in range(HG_HEADS):
                rows, hs = block(n, h)
                st_n = st_ref[n, h]
                decay = dec_sc[n:n + 1, hs]
                dstb = dst[h].astype(BF16)
                dob_n = dob_sc[rows, hs]
                dv_sc[rows, hs] += _dot(koutb_sc[rows, hs], dstb, NT)
                dkout_sc[rows, hs] = _dot(p_ref[rows, C_HI + h * HG_F:C_HI + (h + 1) * HG_F].astype(BF16), dstb, NN)
                ddec_sc[n:n + 1, hs] = jnp.sum(dst[h] * st_n, axis=0, keepdims=True) * decay
                dqin_sc[rows, hs] = _dot(dob_n, st_n.astype(BF16), NN)
                dst[h] = dst[h] * decay + _dot(dob_n, qinb_sc[rows, hs], TN)
        for h in range(HG_HEADS):
            dstate_sc[h] = dst[h]

        for h in range(HG_HEADS):
            dlb = jnp.zeros((1, HG_F), F32)
            for n in range(nc):
                rows, hs = block(n, h)
                fw = {name: keep[name][rows, hs] for name in KEEP_NAMES}
                lbh = lb_ref[:, h * HG_F:(h + 1) * HG_F]
                q = p_ref[rows, C_HQ + h * HG_F:C_HQ + (h + 1) * HG_F].astype(F32)
                da, dbk, dqin, dkout = da_sc[rows, hs], dbk_sc[rows, hs], dqin_sc[rows, hs], dkout_sc[rows, hs]
                w_a, w_b, w_q, w_k = da * fw["ea"], dbk * fw["eb"], dqin * fw["eq"], dkout * fw["ek"]
                dqs, dk = w_a + w_q, w_b + w_k
                t_a, t_b, t_q, t_k = w_a * fw["qs"], w_b * fw["k"], w_q * fw["qs"], w_k * fw["k"]
                s_ref = jnp.sum(t_b - t_a, axis=0, keepdims=True)
                s_last = jnp.sum(t_k, axis=0, keepdims=True) + ddec_sc[n:n + 1, hs]
                dbc = (t_a - t_b + t_q - t_k) + jnp.where(pos_c == mid, s_ref, 0.0) + jnp.where(pos_c == last, s_last, 0.0)
                dfk = _seg_rev_cumsum(dbc, pos_c) / fw["f"] - dk
                sig, sq = fw["sig"], fw["sq"]
                dp_ref[rows, C_HF + h * HG_F:C_HF + (h + 1) * HG_F] = (dfk * (1.0 - lbh) * sig * (1.0 - sig)).astype(BF16)
                dlb = dlb + jnp.sum(dfk * (1.0 - sig), axis=0, keepdims=True)
                dp_ref[rows, C_HQ + h * HG_F:C_HQ + (h + 1) * HG_F] = (dqs * (sq * (1.0 + q * (1.0 - sq)))).astype(BF16)
                dp_ref[rows, C_HI + h * HG_F:C_HI + (h + 1) * HG_F] = dv_sc[rows, hs].astype(BF16)
            dlb_ref[0:1, h * HG_F:(h + 1) * HG_F] += dlb

        mheads = [slice(h * MEM_HEAD_DIM, (h + 1) * MEM_HEAD_DIM) for h in range(MEM_HEADS)]
        qhs = [p_ref[:, C_MQ + h * MEM_HEAD_DIM:C_MQ + (h + 1) * MEM_HEAD_DIM].astype(BF16) for h in range(MEM_HEADS)]
        dobs = [dy_ref[:, 2 * W + h * MEM_HEAD_DIM:2 * W + (h + 1) * MEM_HEAD_DIM].astype(BF16) for h in range(MEM_HEADS)]
        probs = [_attn_probs(qhs[h], mk_ref[:, hs]) for h, hs in enumerate(mheads)]
        dprobs = [_dot(dobs[h], mv_ref[:, hs], NT) for h, hs in enumerate(mheads)]
        for h, hs in enumerate(mheads):
            prob = probs[h]
            dmv_ref[:, hs] += _dot(prob.astype(BF16), dobs[h], TN)
            ds = prob * (dprobs[h] - jnp.sum(dprobs[h] * prob, axis=-1, keepdims=True)) * (MEM_HEAD_DIM ** -0.5)
            dsb = ds.astype(BF16)
            dp_ref[:, C_MQ + h * MEM_HEAD_DIM:C_MQ + (h + 1) * MEM_HEAD_DIM] = _dot(dsb, mk_ref[:, hs], NN).astype(BF16)
            dmk_ref[:, hs] += _dot(dsb, qhs[h], TN)

    def tile(b, s):
        return b * ns + (ns - 1 - s)

    return pl.pallas_call(
        body,
        name="mixer_bwd",
        grid=(bl, ns),
        in_specs=[
            pl.BlockSpec((ts, N_MIX), lambda b, s: (tile(b, s), 0)),
            pl.BlockSpec((PREV_ROWS, N_MIX), lambda b, s: (jnp.maximum(tile(b, s) * (ts // PREV_ROWS) - 1, 0), 0)),
            pl.BlockSpec((ts, 3 * W), lambda b, s: (tile(b, s), 0)),
            pl.BlockSpec(memory_space=pl.ANY),
            pl.BlockSpec((nc, HG_HEADS, HG_F, HG_F), lambda b, s: (tile(b, s), 0, 0, 0)),
            pl.BlockSpec((ts, W), lambda b, s: (tile(b, s), 0)),
            pl.BlockSpec((ml, W), lambda b, s: (b, 0)),
            pl.BlockSpec((ml, W), lambda b, s: (b, 0)),
            pl.BlockSpec((1, W), lambda b, s: (0, 0)),
            pl.BlockSpec((CONV_K, W), lambda b, s: (0, 0)),
            pl.BlockSpec((1, HG_F), lambda b, s: (0, 0)),
        ] + [ANY_SPEC] * len(deps),
        out_specs=[
            pl.BlockSpec((ts, N_MIX), lambda b, s: (tile(b, s), 0)),
            pl.BlockSpec((ml, W), lambda b, s: (b, 0)),
            pl.BlockSpec((ml, W), lambda b, s: (b, 0)),
            pl.BlockSpec((8, W), lambda b, s: (0, 0)),
            pl.BlockSpec((8, HG_F), lambda b, s: (0, 0)),
            pl.BlockSpec((8, W), lambda b, s: (0, 0)),
        ],
        out_shape=[
            jax.ShapeDtypeStruct((T, nin), BF16),
            jax.ShapeDtypeStruct((bl * ml, W), F32),
            jax.ShapeDtypeStruct((bl * ml, W), F32),
            jax.ShapeDtypeStruct((8, W), F32),
            jax.ShapeDtypeStruct((8, HG_F), F32),
            jax.ShapeDtypeStruct((8, W), F32),
        ],
        input_output_aliases={3: 0},
        scratch_shapes=[pltpu.VMEM((HG_HEADS, HG_F, HG_F), F32), pltpu.VMEM((8, W), F32), pltpu.VMEM((PREV_ROWS, W), F32)]
        + [pltpu.VMEM((ts, W), BF16)] * 5 + [pltpu.VMEM((ts, W), F32)] * 5 + [pltpu.VMEM((nc, W), F32)] * 2
        + [pltpu.VMEM((ts, W), F32)] * len(KEEP_NAMES),
        compiler_params=_cparams(("arbitrary", "arbitrary")),
    )(p, p, dy, dp_gates, st, opre, mk, mv, lb, conv_w, norm_w, *deps)


def _layer_norm_stats(z):
    mu = jnp.mean(z, axis=-1, keepdims=True)
    zc = z - mu
    rstd = lax.rsqrt(jnp.mean(zc * zc, axis=-1, keepdims=True) + LN_EPS)
    return zc * rstd, rstd


def _gate_specs(tm, d):
    g0 = N_MIX // d
    return [pl.BlockSpec((tm, d), functools.partial(lambda i, k: (i, g0 + k), k=k)) for k in range(N_BRANCH)]


def _merge_fwd(y, p, x0, wb, wo, bg, ln_g, ln_b, *, alpha, tm=512):
    T, d = x0.shape
    assert N_MIX % d == 0
    tm = _pick(T, (tm, 128, 8))

    def body(y_ref, g0_ref, g1_ref, g2_ref, x_ref, wb_ref, wo_ref, bg_ref, lg_ref, lb_ref, mg_ref, xh_ref, rs_ref, x1b_ref):
        merged = None
        for i, g_ref in enumerate((g0_ref, g1_ref, g2_ref)):
            r = _dot(y_ref[:, i * W:(i + 1) * W], wb_ref[i * W:(i + 1) * W, :], NN)
            t = _sigmoid(g_ref[...].astype(F32) + bg_ref[:, i * d:(i + 1) * d]) * r
            merged = t if merged is None else merged + t
        mb = merged.astype(BF16)
        mg_ref[...] = mb
        z = alpha * x_ref[...] + _dot(mb, wo_ref[...], NN)
        xh, rs = _layer_norm_stats(z)
        xh_ref[...], rs_ref[...] = xh, rs
        x1b_ref[...] = (xh * lg_ref[...] + lb_ref[...]).astype(BF16)

    row = lambda i: (i, 0)
    fix = lambda i: (0, 0)
    return pl.pallas_call(
        body,
        name="merge_fwd",
        grid=(T // tm,),
        in_specs=[pl.BlockSpec((tm, 3 * W), row)] + _gate_specs(tm, d) + [
            pl.BlockSpec((tm, d), row), pl.BlockSpec((3 * W, d), fix, pipeline_mode=pl.Buffered(1)),
            pl.BlockSpec((d, d), fix, pipeline_mode=pl.Buffered(1)), pl.BlockSpec((1, 3 * d), fix),
            pl.BlockSpec((1, d), fix), pl.BlockSpec((1, d), fix)],
        out_specs=[pl.BlockSpec((tm, d), row), pl.BlockSpec((tm, d), row), pl.BlockSpec((tm, 1), row), pl.BlockSpec((tm, d), row)],
        out_shape=[jax.ShapeDtypeStruct((T, d), BF16), jax.ShapeDtypeStruct((T, d), F32), jax.ShapeDtypeStruct((T, 1), F32),
                   jax.ShapeDtypeStruct((T, d), BF16)],
        compiler_params=_cparams(("parallel",)),
    )(y, p, p, p, x0, wb, wo, bg, ln_g, ln_b)


def _merge_bwd(dz, p, y, wb, wo, bg, *, tm=512):
    T, d = dz.shape
    nin = p.shape[1]
    tm = _pick(T, (tm, 128, 8))

    def body(dz_ref, g0_ref, g1_ref, g2_ref, y_ref, wb_ref, wo_ref, bg_ref, dr_ref, dp_ref, dy_ref, dbg_ref):
        @pl.when(pl.program_id(0) == 0)
        def _():
            dbg_ref[...] = jnp.zeros_like(dbg_ref)

        dmerged = _dot(dz_ref[...].astype(BF16), wo_ref[...], NT)
        dp_ref[:, 0:N_MIX] = jnp.zeros((tm, N_MIX), BF16)
        for i, g_ref in enumerate((g0_ref, g1_ref, g2_ref)):
            cs = slice(i * d, (i + 1) * d)
            s = _sigmoid(g_ref[...].astype(F32) + bg_ref[:, cs])
            drb = (dmerged * s).astype(BF16)
            dr_ref[:, cs] = drb
            dgate = dmerged * _dot(y_ref[:, i * W:(i + 1) * W], wb_ref[i * W:(i + 1) * W, :], NN) * s * (1.0 - s)
            dp_ref[:, N_MIX + i * d:N_MIX + (i + 1) * d] = dgate.astype(BF16)
            dbg_ref[0:1, cs] += jnp.sum(dgate, axis=0, keepdims=True)
            dy_ref[:, i * W:(i + 1) * W] = _dot(drb, wb_ref[i * W:(i + 1) * W, :], NT).astype(BF16)

    row = lambda i: (i, 0)
    fix = lambda i: (0, 0)
    return pl.pallas_call(
        body,
        name="merge_bwd",
        grid=(T // tm,),
        in_specs=[pl.BlockSpec((tm, d), row)] + _gate_specs(tm, d) + [
            pl.BlockSpec((tm, 3 * W), row), pl.BlockSpec((3 * W, d), fix, pipeline_mode=pl.Buffered(1)),
            pl.BlockSpec((d, d), fix, pipeline_mode=pl.Buffered(1)), pl.BlockSpec((1, 3 * d), fix)],
        out_specs=[pl.BlockSpec((tm, 3 * d), row), pl.BlockSpec((tm, nin), row), pl.BlockSpec((tm, 3 * W), row),
                   pl.BlockSpec((8, 3 * d), fix)],
        out_shape=[jax.ShapeDtypeStruct((T, 3 * d), BF16), jax.ShapeDtypeStruct((T, nin), BF16),
                   jax.ShapeDtypeStruct((T, 3 * W), BF16), jax.ShapeDtypeStruct((8, 3 * d), F32)],
        compiler_params=_cparams(("arbitrary",)),
    )(dz, p, p, p, y, wb, wo, bg)


def _mlp_fwd(xhat1, x1b, g1, b1, wu, wd, g2, b2, *, alpha, tm=512, tf=2048):
    T, d = xhat1.shape
    ff = wu.shape[1]
    tm, tf = _pick(T, (tm, 256, 128, 8)), _pick(ff, (tf, 1024, 512, 256, 128))
    nf = ff // tf

    def body(xh_ref, x1b_ref, g1_ref, b1_ref, wu_ref, wd_ref, g2_ref, b2_ref, a_ref, xh2_ref, rs2_ref, x2_ref, x2b_ref, acc_ref):
        f = pl.program_id(1)
        a = _dot(x1b_ref[...], wu_ref[...], NN)
        a_ref[...] = a.astype(BF16)
        h = jnp.square(jnp.maximum(a, 0.0))
        part = _dot(h.astype(BF16), wd_ref[...], NN)

        @pl.when(f == 0)
        def _():
            acc_ref[...] = part

        @pl.when(jnp.logical_and(f > 0, f < nf - 1))
        def _():
            acc_ref[...] += part

        @pl.when(f == nf - 1)
        def _():
            x1 = xh_ref[...] * g1_ref[...] + b1_ref[...]
            xh2, rs2 = _layer_norm_stats(alpha * x1 + (acc_ref[...] + part if nf > 1 else part))
            xh2_ref[...] = xh2
            rs2_ref[...] = rs2
            x2 = xh2 * g2_ref[...] + b2_ref[...]
            x2_ref[...] = x2
            x2b_ref[...] = x2.astype(BF16)

    row = lambda i, f: (i, 0)
    fix = lambda i, f: (0, 0)
    return pl.pallas_call(
        body,
        name="mlp_fwd",
        grid=(T // tm, nf),
        in_specs=[pl.BlockSpec((tm, d), row), pl.BlockSpec((tm, d), row), pl.BlockSpec((1, d), fix), pl.BlockSpec((1, d), fix),
                  pl.BlockSpec((d, tf), lambda i, f: (0, f)), pl.BlockSpec((tf, d), lambda i, f: (f, 0)),
                  pl.BlockSpec((1, d), fix), pl.BlockSpec((1, d), fix)],
        out_specs=[pl.BlockSpec((tm, tf), lambda i, f: (i, f)), pl.BlockSpec((tm, d), row), pl.BlockSpec((tm, 1), row),
                   pl.BlockSpec((tm, d), row), pl.BlockSpec((tm, d), row)],
        out_shape=[jax.ShapeDtypeStruct((T, ff), BF16), jax.ShapeDtypeStruct((T, d), F32), jax.ShapeDtypeStruct((T, 1), F32),
                   jax.ShapeDtypeStruct((T, d), F32), jax.ShapeDtypeStruct((T, d), BF16)],
        scratch_shapes=[pltpu.VMEM((tm, d), F32)],
        compiler_params=_cparams(("parallel", "arbitrary")),
    )(xhat1, x1b, g1, b1, wu, wd, g2, b2)


def _ln_bwd(dy, xhat, rstd, g, *, tm=512, deps=()):
    T, d = dy.shape
    tm = _pick(T, (tm, 256, 128, 8))

    def body(dy_ref, xh_ref, rs_ref, g_ref, *rest):
        dz_ref, dzb_ref, dg_ref, db_ref = rest[len(deps):]

        @pl.when(pl.program_id(0) == 0)
        def _():
            dg_ref[...] = jnp.zeros_like(dg_ref)
            db_ref[...] = jnp.zeros_like(db_ref)

        dy_, xh = dy_ref[...], xh_ref[...]
        dg_ref[0:1, :] += jnp.sum(dy_ * xh, axis=0, keepdims=True)
        db_ref[0:1, :] += jnp.sum(dy_, axis=0, keepdims=True)
        dxh = dy_ * g_ref[...]
        dz = rs_ref[...] * (dxh - jnp.mean(dxh, axis=-1, keepdims=True) - xh * jnp.mean(dxh * xh, axis=-1, keepdims=True))
        dz_ref[...] = dz
        dzb_ref[...] = dz.astype(BF16)

    row = lambda i: (i, 0)
    fix = lambda i: (0, 0)
    return pl.pallas_call(
        body,
        name="ln_bwd",
        grid=(T // tm,),
        in_specs=[pl.BlockSpec((tm, d), row), pl.BlockSpec((tm, d), row), pl.BlockSpec((tm, 1), row), pl.BlockSpec((1, d), fix)]
        + [ANY_SPEC] * len(deps),
        out_specs=[pl.BlockSpec((tm, d), row), pl.BlockSpec((tm, d), row), pl.BlockSpec((8, d), fix), pl.BlockSpec((8, d), fix)],
        out_shape=[jax.ShapeDtypeStruct((T, d), F32), jax.ShapeDtypeStruct((T, d), BF16), jax.ShapeDtypeStruct((8, d), F32),
                   jax.ShapeDtypeStruct((8, d), F32)],
        compiler_params=_cparams(("arbitrary",)),
    )(dy, xhat, rstd, g, *deps)


def _loss_head(y, target, *, tm=512):
    T, d = y.shape
    tm = _pick(T, (tm, 256, 128, 8))
    n = T // tm

    def body(y_ref, t_ref, loss_ref, dy_ref, acc_ref):
        i = pl.program_id(0)

        @pl.when(i == 0)
        def _():
            acc_ref[...] = jnp.zeros_like(acc_ref)

        e = y_ref[...] - t_ref[...]
        dy_ref[...] = e * (1.0 / d)
        acc_ref[...] += jnp.sum(e * e, axis=0, keepdims=True)

        @pl.when(i == n - 1)
        def _():
            loss_ref[...] = (0.5 / d) * jnp.sum(acc_ref[...], axis=1, keepdims=True)

    row = lambda i: (i, 0)
    return pl.pallas_call(
        body,
        name="loss_head",
        grid=(n,),
        in_specs=[pl.BlockSpec((tm, d), row), pl.BlockSpec((tm, d), row)],
        out_specs=[pl.BlockSpec((1, 1), lambda i: (0, 0)), pl.BlockSpec((tm, d), row)],
        out_shape=[jax.ShapeDtypeStruct((1, 1), F32), jax.ShapeDtypeStruct((T, d), F32)],
        scratch_shapes=[pltpu.VMEM((1, d), F32)],
        compiler_params=_cparams(("arbitrary",)),
    )(y, target)


def _lower_bounds_fwd(lower_bounds):
    depth, n = lower_bounds.shape

    def body(x_ref, soft_ref, lb_ref):
        x = x_ref[...]
        e = jnp.exp(x - jnp.max(x, axis=0, keepdims=True))
        soft_ref[...] = e / jnp.sum(e, axis=0, keepdims=True)
        run = None
        for l in range(depth):
            run = soft_ref[l:l + 1, :] if run is None else run + soft_ref[l:l + 1, :]
            lb_ref[l:l + 1, :] = run - soft_ref[0:1, :]

    return pl.pallas_call(body, name="lower_bounds_fwd",
                          out_shape=[jax.ShapeDtypeStruct((depth, n), F32), jax.ShapeDtypeStruct((depth, n), F32)])(lower_bounds)


def _lower_bounds_bwd(soft, dlb):
    depth, n = soft.shape

    def body(soft_ref, dlb_ref, out_ref, dsoft_ref):
        total = jnp.sum(dlb_ref[...], axis=0, keepdims=True)
        run = None
        for l in reversed(range(depth)):
            run = dlb_ref[l:l + 1, :] if run is None else run + dlb_ref[l:l + 1, :]
            dsoft_ref[l:l + 1, :] = run - total if l == 0 else run
        s, ds = soft_ref[...], dsoft_ref[...]
        out_ref[...] = s * (ds - jnp.sum(s * ds, axis=0, keepdims=True))

    return pl.pallas_call(body, name="lower_bounds_bwd", out_shape=jax.ShapeDtypeStruct((depth, n), F32),
                          scratch_shapes=[pltpu.VMEM((depth, n), F32)])(soft, dlb)


def _layer_fwd(x0, x0b, mem2, lb, w_in, mix_fn, late_fn, *, bl, seq, alpha, deps=()):
    p = _matmul("proj_in", x0b, w_in, mode="nn", out_dtype=BF16, deps=deps, tm=1024, tn=1792)
    wts = dict(mix_fn(p), w_in=w_in)
    mk = _matmul("mem_k", mem2, wts["w_mem_k"], mode="nn", out_dtype=BF16)
    mv = _matmul("mem_v", mem2, wts["w_mem_v"], mode="nn", out_dtype=BF16)
    y, st, opre = _mixer_fwd(p, mk, mv, lb, wts["conv_w"], wts["hg_norm_w"], bl=bl, seq=seq)
    wts.update(late_fn(y))
    merged, xhat1, rstd1, x1b = _merge_fwd(y, p, x0, wts["w_branch"], wts["w_o"], wts["b_gate"], wts["ln1_g"], wts["ln1_b"],
                                           alpha=alpha)
    a, xhat2, rstd2, x2, x2b = _mlp_fwd(xhat1, x1b, wts["ln1_g"], wts["ln1_b"], wts["w_up"], wts["w_down"], wts["ln2_g"],
                                        wts["ln2_b"], alpha=alpha)
    saved = dict(x0b=x0b, p=p, mk=mk, mv=mv, y=y, st=st, opre=opre, merged=merged, xhat1=xhat1, rstd1=rstd1, x1b=x1b, a=a,
                 xhat2=xhat2, rstd2=rstd2)
    return x2, x2b, saved, wts


def _relu2_bf16(a):
    return jnp.square(jnp.maximum(a.astype(F32), 0.0)).astype(BF16)


def _mlp_bwd(dz2, dz2b, sv, wts, *, alpha, deps=()):
    g = {}
    da = _matmul("mlp_da", dz2b, wts["w_down"], mode="nt", out_dtype=BF16, tm=1024, deps=deps,
                 epi_fn=lambda acc, a: (acc * (2.0 * jnp.maximum(a.astype(F32), 0.0)),), epi_extra=(sv["a"],))
    g["w_down"] = _matmul_tn("grad_w_down", sv["a"], dz2b, a_fn=_relu2_bf16, out_dtype=BF16, tt=2048)
    g["w_up"] = _matmul_tn("grad_w_up", sv["x1b"], da, out_dtype=BF16, tt=2048)
    dx1 = _matmul("mlp_dx", da, wts["w_up"], mode="nt", epi_fn=lambda acc, dz: (acc + alpha * dz,), epi_extra=(dz2,),
                  tm=512, tk=4096)
    dz1, dz1b, dg1, db1 = _ln_bwd(dx1, sv["xhat1"], sv["rstd1"], wts["ln1_g"])
    g["ln1_g"], g["ln1_b"] = dg1[0:1], db1[0:1]
    return dz1, dz1b, g


def _mix_bwd(dz1, dz1b, sv, mem2, lb, wts, *, bl, seq, alpha, send, below=None, deps=()):
    d = dz1.shape[1]
    g = {}
    g["w_o"] = _matmul_tn("grad_w_o", sv["merged"], dz1b, out_dtype=BF16, tt=2048, deps=deps)
    dr, dp, dy, dbg = _merge_bwd(dz1b, sv["p"], sv["y"], wts["w_branch"], wts["w_o"], wts["b_gate"])
    g["b_gate"] = dbg[0:1]
    g["w_branch"] = jnp.concatenate(
        [_matmul_tn("grad_w_branch", sv["y"], dr, a_cols=(i * W, W), b_cols=(i * d, d), out_dtype=BF16) for i in range(N_BRANCH)],
        axis=0)
    token = send(("w_o", "w_branch"), g)
    dp, dmk, dmv, dcw, dnw, dlb = _mixer_bwd(sv["p"], dy, dp, sv["st"], sv["opre"], sv["mk"], sv["mv"], lb,
                                              wts["conv_w"], wts["hg_norm_w"], bl=bl, seq=seq, deps=(token,))
    g["conv_w"], g["hg_norm_w"], g["lb"] = dcw[0:CONV_K], dnw[0:1], dlb[0:1]
    g["w_mem_k"] = _matmul_tn("grad_w_mem_k", mem2, dmk, out_dtype=BF16)
    g["w_mem_v"] = _matmul_tn("grad_w_mem_v", mem2, dmv, out_dtype=BF16)
    g["w_in"] = _matmul_tn("grad_w_in", sv["x0b"], dp, out_dtype=BF16, tt=2048)
    token = send(("w_in", "w_mem_k", "w_mem_v", "conv_w"), g)
    dx0 = _matmul("proj_in_dx", dp, wts["w_in"], mode="nt", epi_fn=lambda acc, dz: (acc + alpha * dz,), epi_extra=(dz1,),
                  tm=512, tk=dp.shape[1], deps=(token,))
    return (dx0 if below is None else _ln_bwd(dx0, *below)), g


N_CHIPS = 4
MESH_IDS = pl.DeviceIdType.MESH


def _axis_slice(ref, axis, start, size):
    idx = [slice(None)] * len(ref.shape)
    idx[axis] = pl.ds(start, size)
    return ref.at[tuple(idx)]


def _chip_exchange(name, items):
    n = len(items)
    out_shapes, meta = [], []
    for arr, kind, axis in items:
        shp = list(arr.shape)
        if kind == "gather":
            per = shp[axis]
            shp[axis] = per * N_CHIPS
            out_shapes.append(jax.ShapeDtypeStruct(tuple(shp), arr.dtype))
        elif kind == "scatter":
            per = shp[axis] // N_CHIPS
            shp[axis] = per
            out_shapes.append(jax.ShapeDtypeStruct((N_CHIPS, *shp), arr.dtype))
        else:
            per = None
            out_shapes.append(jax.ShapeDtypeStruct((N_CHIPS, *shp), arr.dtype))
        meta.append((kind, axis, per))

    def body(*refs):
        ins, outs = refs[:n], refs[n:2 * n]
        send_sems, recv_sems, local_sems = refs[2 * n:]
        x, y, c = lax.axis_index("x"), lax.axis_index("y"), lax.axis_index("c")
        me = 2 * x + y
        peers = [(1 - x, y), (x, 1 - y), (1 - x, 1 - y)]

        def src_for(t, chip):
            kind, axis, per = meta[t]
            return _axis_slice(ins[t], axis, chip * per, per) if kind == "scatter" else ins[t]

        def dst_from(t, chip):
            kind, axis, per = meta[t]
            return _axis_slice(outs[t], axis, chip * per, per) if kind == "gather" else outs[t].at[chip]

        def remote(t, k):
            px, py = peers[k]
            return pltpu.make_async_remote_copy(
                src_ref=src_for(t, 2 * px + py), dst_ref=dst_from(t, me), send_sem=send_sems.at[t * 3 + k],
                recv_sem=recv_sems.at[t * 3 + k], device_id=(px, py, c), device_id_type=MESH_IDS)

        def arrival(t, k):
            px, py = peers[k]
            return pltpu.make_async_remote_copy(
                src_ref=src_for(t, me), dst_ref=dst_from(t, 2 * px + py), send_sem=send_sems.at[t * 3 + k],
                recv_sem=recv_sems.at[t * 3 + k], device_id=(px, py, c), device_id_type=MESH_IDS)

        sends = [remote(t, k) for t in range(n) for k in range(3)]
        for cp in sends:
            cp.start()
        own = [pltpu.make_async_copy(src_for(t, me), dst_from(t, me), local_sems.at[t]) for t in range(n)]
        for cp in own:
            cp.start()
        for t in range(n):
            for k in range(3):
                arrival(t, k).wait_recv()
        for cp in sends:
            cp.wait_send()
        for cp in own:
            cp.wait()

    any_spec = pl.BlockSpec(memory_space=pl.ANY)
    return pl.pallas_call(
        body,
        name=name,
        in_specs=[any_spec] * n,
        out_specs=[any_spec] * n,
        out_shape=out_shapes,
        scratch_shapes=[pltpu.SemaphoreType.DMA((3 * n,)), pltpu.SemaphoreType.DMA((3 * n,)), pltpu.SemaphoreType.DMA((n,))],
        compiler_params=pltpu.CompilerParams(has_side_effects=True),
    )(*[a for a, _, _ in items])


HBM_SPEC = pl.BlockSpec(memory_space=pltpu.HBM)
SEM_SPEC = pl.BlockSpec(memory_space=pltpu.SEMAPHORE)
N_PEERS = N_CHIPS - 1


def _my_chip():
    return (2 * lax.axis_index("x") + lax.axis_index("y")).astype(jnp.int32).reshape(1)


def _own_block_spec(r, c, axis, tr):
    if axis == 1:
        return pl.BlockSpec((tr, c), lambda i, me: (i, me[0]))
    return pl.BlockSpec((tr, c), lambda i, me: (me[0] * (r // tr) + i, 0))


def _place_shard(name, shard, axis, me):
    r, c = shard.shape
    tr = _row_block(r, c, shard.dtype.itemsize)
    shp = (r, c * N_CHIPS) if axis == 1 else (r * N_CHIPS, c)

    def body(me_ref, s_ref, o_ref):
        del me_ref
        o_ref[...] = s_ref[...]

    return pl.pallas_call(
        body, name=name,
        grid_spec=pltpu.PrefetchScalarGridSpec(
            num_scalar_prefetch=1, grid=(r // tr,),
            in_specs=[pl.BlockSpec((tr, c), lambda i, me: (i, 0))], out_specs=_own_block_spec(r, c, axis, tr)),
        out_shape=jax.ShapeDtypeStruct(shp, shard.dtype),
        compiler_params=_cparams(("parallel",)),
    )(me, shard)


class _Split:
    def __init__(self, name, items):
        self.name, self.n = name, len(items)
        self.srcs = [a for a, _, _ in items]
        self.meta, self.land_shapes = [], []
        for arr, kind, axis in items:
            shp = list(arr.shape)
            if kind == "gather":
                per = shp[axis]
                shp[axis] = per * N_CHIPS
                self.land_shapes.append(jax.ShapeDtypeStruct(tuple(shp), arr.dtype))
            else:
                per = shp[axis] // N_CHIPS
                shp[axis] = per
                self.land_shapes.append(jax.ShapeDtypeStruct((N_PEERS, *shp), arr.dtype))
            self.meta.append((kind, axis, per))

    def _src(self, ins, t, chip):
        kind, axis, per = self.meta[t]
        return _axis_slice(ins[t], axis, chip * per, per) if kind == "scatter" else ins[t]

    def _dst(self, lands, t, chip, slot):
        kind, axis, per = self.meta[t]
        return _axis_slice(lands[t], axis, chip * per, per) if kind == "gather" else lands[t].at[slot]

    def landing_zones(self, me):
        return [_place_shard(self.name + "_own", src, axis, me) if kind == "gather" else lax.empty(ls.shape, ls.dtype)
                for src, ls, (kind, axis, _) in zip(self.srcs, self.land_shapes, self.meta)]

    def _copies(self, ins, lands, send_sems, recv_sems, arrivals):
        x, y, c = lax.axis_index("x"), lax.axis_index("y"), lax.axis_index("c")
        me = 2 * x + y
        peers = [(1 - x, y), (x, 1 - y), (1 - x, 1 - y)]
        res = []
        for t in range(self.n):
            for k, (px, py) in enumerate(peers):
                theirs = 2 * px + py
                sems = dict(send_sem=send_sems.at[t * N_PEERS + k], recv_sem=recv_sems.at[t * N_PEERS + k],
                            device_id=(px, py, c), device_id_type=MESH_IDS)
                if arrivals:
                    res.append(pltpu.make_async_remote_copy(src_ref=self._src(ins, t, me), dst_ref=self._dst(lands, t, theirs, k), **sems))
                else:
                    res.append(pltpu.make_async_remote_copy(src_ref=self._src(ins, t, theirs), dst_ref=self._dst(lands, t, me, k), **sems))
        return res

    def start(self, lands, deps=()):
        n, nd = self.n, len(deps)

        def body(*refs):
            ins, lnd = refs[:n], refs[n:2 * n]
            send_sems, recv_sems = refs[2 * n + nd], refs[2 * n + nd + 1]
            token = refs[-1]
            for cp in self._copies(ins, lnd, send_sems, recv_sems, arrivals=False):
                cp.start()
            token[...] = jnp.zeros_like(token)

        hbm = lambda a: pltpu.HBM(a.shape, a.dtype)
        res = pl.pallas_call(
            body, name=self.name + "_start",
            in_specs=[HBM_SPEC] * (2 * n) + [ANY_SPEC] * nd,
            out_specs=[SEM_SPEC, SEM_SPEC] + [HBM_SPEC] * (2 * n) + [pl.BlockSpec(memory_space=pltpu.VMEM)],
            out_shape=[pltpu.SemaphoreType.DMA((N_PEERS * n,)), pltpu.SemaphoreType.DMA((N_PEERS * n,))]
            + [hbm(a) for a in self.srcs] + [hbm(a) for a in self.land_shapes] + [jax.ShapeDtypeStruct((8, 128), F32)],
            input_output_aliases={i: 2 + i for i in range(2 * n)},
            compiler_params=pltpu.CompilerParams(has_side_effects=pltpu.SideEffectType.DATAFLOW_SIDE_EFFECTING),
        )(*[pltpu.with_memory_space_constraint(a, pltpu.HBM) for a in self.srcs],
          *[pltpu.with_memory_space_constraint(a, pltpu.HBM) for a in lands], *deps)
        return res[:-1], res[-1]

    def wait(self, state, after):
        n = self.n
        after = tuple(after) if isinstance(after, (tuple, list)) else (after,)
        send_sems, recv_sems = state[0], state[1]
        srcs, lands = state[2:2 + n], state[2 + n:2 + 2 * n]

        def body(*refs):
            ins, lnd = refs[:n], refs[n:2 * n]
            s_sems, r_sems = refs[2 * n], refs[2 * n + 1]
            for cp in self._copies(ins, lnd, s_sems, r_sems, arrivals=True):
                cp.wait_recv()
            for cp in self._copies(ins, lnd, s_sems, r_sems, arrivals=False):
                cp.wait_send()

        hbm = lambda a: pltpu.HBM(a.shape, a.dtype)
        res = pl.pallas_call(
            body, name=self.name + "_wait",
            in_specs=[HBM_SPEC] * (2 * n) + [SEM_SPEC, SEM_SPEC] + [ANY_SPEC] * len(after),
            out_specs=[HBM_SPEC] * (2 * n),
            out_shape=[hbm(a) for a in self.srcs] + [hbm(a) for a in self.land_shapes],
            input_output_aliases={i: i for i in range(2 * n)},
            compiler_params=pltpu.CompilerParams(has_side_effects=pltpu.SideEffectType.DATAFLOW_SIDE_EFFECTING),
        )(*srcs, *lands, send_sems, recv_sems, *after)
        return res[:n], res[n:]


def _sibling_swap(name, arrays):
    n = len(arrays)

    def body(*refs):
        ins, outs = refs[:n], refs[n:2 * n]
        send_sems, recv_sems = refs[2 * n:]
        sibling = (lax.axis_index("x"), lax.axis_index("y"), 1 - lax.axis_index("c"))
        copies = [pltpu.make_async_remote_copy(src_ref=ins[t], dst_ref=outs[t], send_sem=send_sems.at[t], recv_sem=recv_sems.at[t],
                                               device_id=sibling, device_id_type=MESH_IDS) for t in range(n)]
        for cp in copies:
            cp.start()
        for cp in copies:
            cp.wait()

    any_spec = pl.BlockSpec(memory_space=pl.ANY)
    return pl.pallas_call(
        body,
        name=name,
        in_specs=[any_spec] * n,
        out_specs=[any_spec] * n,
        out_shape=[jax.ShapeDtypeStruct(a.shape, a.dtype) for a in arrays],
        scratch_shapes=[pltpu.SemaphoreType.DMA((n,)), pltpu.SemaphoreType.DMA((n,))],
        compiler_params=pltpu.CompilerParams(has_side_effects=True),
    )(*arrays)


def _row_block(r, c, itemsize=4, target=1 << 20):
    if r % 8 != 0:
        return r
    best = 8
    for tr in range(8, r + 1, 8):
        if r % tr == 0 and tr * c * itemsize <= target:
            best = tr
    return best


def _sum_chips_into(parts, stacked, layer):
    _, r, c = parts.shape
    tr = _row_block(r, c)

    def body(p_ref, s_ref, o_ref):
        del s_ref
        o_ref[...] = ((p_ref[0] + p_ref[1]) + p_ref[2]) + p_ref[3]

    return pl.pallas_call(
        body,
        name="sum_chips",
        grid=(r // tr,),
        in_specs=[pl.BlockSpec((N_CHIPS, tr, c), lambda i: (0, i, 0)), pl.BlockSpec(memory_space=pl.ANY)],
        out_specs=pl.BlockSpec((None, tr, c), lambda i: (layer, i, 0)),
        out_shape=jax.ShapeDtypeStruct(stacked.shape, stacked.dtype),
        input_output_aliases={1: 0},
        compiler_params=_cparams(("parallel",)),
    )(parts, stacked)


def _sum_own_and_peers(me, g, axis, landed):
    _, r, c = landed.shape
    tr = _row_block(r, c)

    def body(me_ref, g_ref, p_ref, o_ref):
        del me_ref
        o_ref[...] = ((g_ref[...].astype(F32) + p_ref[0].astype(F32)) + p_ref[1].astype(F32)) + p_ref[2].astype(F32)

    return pl.pallas_call(
        body, name="sum_chips_own",
        grid_spec=pltpu.PrefetchScalarGridSpec(
            num_scalar_prefetch=1, grid=(r // tr,),
            in_specs=[_own_block_spec(r, c, axis, tr), pl.BlockSpec((N_PEERS, tr, c), lambda i, me: (0, i, 0))],
            out_specs=pl.BlockSpec((tr, c), lambda i, me: (i, 0))),
        out_shape=jax.ShapeDtypeStruct((r, c), F32),
        compiler_params=_cparams(("parallel",)),
    )(me, g, landed)


def _adamw_math(w, m, v, g):
    m_new = ADAM_B1 * m + (1.0 - ADAM_B1) * g
    v_new = ADAM_B2 * v + (1.0 - ADAM_B2) * jnp.square(g)
    m_hat = m_new / (1.0 - ADAM_B1 ** ADAM_STEP)
    v_hat = v_new / (1.0 - ADAM_B2 ** ADAM_STEP)
    return -ADAM_LR * (m_hat / (jnp.sqrt(v_hat) + ADAM_EPS) + ADAM_WD * w), m_new, v_new


def _adamw(w, m, v, g_a, g_b):
    L, r, c = w.shape
    tr = _row_block(r, c, target=1 << 19)

    def body(w_ref, m_ref, v_ref, ga_ref, gb_ref, g_ref, d_ref, nm_ref, nv_ref):
        g = ga_ref[...] + gb_ref[...]
        g_ref[...] = g
        d_ref[...], nm_ref[...], nv_ref[...] = _adamw_math(w_ref[...], m_ref[...], v_ref[...], g)

    spec = pl.BlockSpec((None, tr, c), lambda l, i: (l, i, 0))
    return pl.pallas_call(
        body,
        name="adamw",
        grid=(L, r // tr),
        in_specs=[spec] * 5,
        out_specs=[spec] * 4,
        out_shape=[jax.ShapeDtypeStruct(w.shape, F32)] * 4,
        compiler_params=_cparams(("parallel", "parallel")),
    )(w, m, v, g_a, g_b)


def _adamw_layer(w, m, v, g_a, g_b, layer, outs):
    L, r, c = w.shape
    tr = _row_block(r, c, target=1 << 19)
    n_prev = 0 if outs is None else 4

    def body(w_ref, m_ref, v_ref, ga_ref, gb_ref, *rest):
        g_ref, d_ref, nm_ref, nv_ref = rest[n_prev:]
        g = ga_ref[...] + gb_ref[...]
        g_ref[...] = g
        d_ref[...], nm_ref[...], nv_ref[...] = _adamw_math(w_ref[...], m_ref[...], v_ref[...], g)

    at_layer = pl.BlockSpec((None, tr, c), lambda i: (layer, i, 0))
    flat = pl.BlockSpec((tr, c), lambda i: (i, 0))
    return pl.pallas_call(
        body,
        name="adamw_layer",
        grid=(r // tr,),
        in_specs=[at_layer] * 3 + [flat] * 2 + [ANY_SPEC] * n_prev,
        out_specs=[at_layer] * 4,
        out_shape=[jax.ShapeDtypeStruct(w.shape, F32)] * 4,
        input_output_aliases={5 + k: k for k in range(n_prev)},
        compiler_params=_cparams(("parallel",)),
    )(w, m, v, g_a, g_b, *(outs or ()))


SHARDED = (("w_in", 1), ("conv_w", 1), ("w_mem_k", 0), ("w_mem_v", 0), ("w_branch", 1), ("w_o", 0), ("w_up", 1), ("w_down", 0))
SMALL = ("lower_bounds", "hg_norm_w", "b_gate", "ln1_g", "ln1_b", "ln2_g", "ln2_b")
WEIGHT_ORDER = ("lower_bounds", "w_in", "conv_w", "hg_norm_w", "w_mem_k", "w_mem_v", "w_branch", "b_gate", "w_o", "ln1_g", "ln1_b",
                "w_up", "w_down", "ln2_g", "ln2_b")


def kernel(x, mem, lower_bounds, w_in, conv_w, hg_norm_w, w_mem_k, w_mem_v, w_branch, b_gate, w_o, ln1_g, ln1_b, w_up, w_down, ln2_g, ln2_b, loss_target, m_lower_bounds, m_w_in, m_conv_w, m_hg_norm_w, m_w_mem_k, m_w_mem_v, m_w_branch, m_b_gate, m_w_o, m_ln1_g, m_ln1_b, m_w_up, m_w_down, m_ln2_g, m_ln2_b, v_lower_bounds, v_w_in, v_conv_w, v_hg_norm_w, v_w_mem_k, v_w_mem_v, v_w_branch, v_b_gate, v_w_o, v_ln1_g, v_ln1_b, v_w_up, v_w_down, v_ln2_g, v_ln2_b):
    bl, seq, d = x.shape
    depth = w_in.shape[0]
    weights = dict(lower_bounds=lower_bounds, w_in=w_in, conv_w=conv_w, hg_norm_w=hg_norm_w, w_mem_k=w_mem_k, w_mem_v=w_mem_v,
                   w_branch=w_branch, b_gate=b_gate, w_o=w_o, ln1_g=ln1_g, ln1_b=ln1_b, w_up=w_up, w_down=w_down, ln2_g=ln2_g, ln2_b=ln2_b)
    mom_m = dict(lower_bounds=m_lower_bounds, w_in=m_w_in, conv_w=m_conv_w, hg_norm_w=m_hg_norm_w, w_mem_k=m_w_mem_k, w_mem_v=m_w_mem_v,
                 w_branch=m_w_branch, b_gate=m_b_gate, w_o=m_w_o, ln1_g=m_ln1_g, ln1_b=m_ln1_b, w_up=m_w_up, w_down=m_w_down,
                 ln2_g=m_ln2_g, ln2_b=m_ln2_b)
    mom_v = dict(lower_bounds=v_lower_bounds, w_in=v_w_in, conv_w=v_conv_w, hg_norm_w=v_hg_norm_w, w_mem_k=v_w_mem_k, w_mem_v=v_w_mem_v,
                 w_branch=v_w_branch, b_gate=v_b_gate, w_o=v_w_o, ln1_g=v_ln1_g, ln1_b=v_ln1_b, w_up=v_w_up, w_down=v_w_down,
                 ln2_g=v_ln2_g, ln2_b=v_ln2_b)

    def shard2d(name, l):
        w = weights[name][l]
        if name == "w_branch":
            return w.reshape(N_BRANCH * W, w.shape[-1]).astype(BF16)
        return w if name == "conv_w" else w.astype(BF16)

    me = _my_chip()

    shard_axis = dict(SHARDED)

    def prepare_exchange(name, kind, items):
        ex = _Split(name, [(arr, kind, shard_axis[nm]) for nm, arr in items])
        return ex, ex.landing_zones(me), [nm for nm, _ in items]

    def launch(prepared, deps=()):
        ex, lands, names = prepared
        state, token = ex.start(lands, deps)
        return ex, state, names, token

    def start_exchange(name, kind, items, deps=()):
        return launch(prepare_exchange(name, kind, items), deps)

    def prepare_gathers(l):
        groups = (("in", ("w_in",)), ("mix", ("conv_w", "w_mem_k", "w_mem_v")), ("rest", ("w_branch", "w_o", "w_up", "w_down")))
        return tuple(prepare_exchange(f"gather_{tag}_l{l}", "gather", [(nm, shard2d(nm, l)) for nm in names]) for tag, names in groups)

    def start_gathers(prepared, deps=()):
        started = []
        for prep in prepared:
            started.append(launch(prep, deps))
            deps = (started[-1][3],)
        return tuple(started)

    def gathered(pend, after):
        ex, state, names, _ = pend
        return dict(zip(names, ex.wait(state, after=after)[1]))

    pending = start_gathers(prepare_gathers(0))
    tokens = tuple(pend[3] for pend in pending)
    tokens, x, mem, loss_target, weights, mom_m, mom_v = lax.optimization_barrier((tokens, x, mem, loss_target, weights, mom_m, mom_v))
    pending = tuple((*pend[:3], tok) for pend, tok in zip(pending, tokens))
    lower_bounds = weights["lower_bounds"]

    x2d, mem2, t2d = x.reshape(bl * seq, d), mem.reshape(-1, d), loss_target.reshape(bl * seq, d)
    alpha = (2.0 * depth) ** 0.25
    soft, lb_all = _lower_bounds_fwd(lower_bounds)

    prepared = [None] + [prepare_gathers(l) for l in range(1, depth)]
    early = [x2d.astype(BF16), lb_all] + [z for prep in prepared[1:] for _, lands, _ in prep for z in lands]

    h, hb, saved, layer_wts = x2d, early[0], [], []
    for l in range(depth):
        first, mix, rest = pending
        w_in_l = gathered(first, early if l == 0 else h)["w_in"]

        def mix_fn(after, l=l, mix=mix):
            return dict(gathered(mix, after), hg_norm_w=weights["hg_norm_w"][l][None, :])

        def late_fn(after, l=l, rest=rest):
            wts = gathered(rest, after)
            for name in ("b_gate", "ln1_g", "ln1_b", "ln2_g", "ln2_b"):
                wts[name] = weights[name][l][None, :]
            return wts

        deps = (rest[3],)
        if l + 1 < depth:
            pending = start_gathers(prepared[l + 1], (w_in_l, rest[3]))
            deps += tuple(pend[3] for pend in pending)
        h, hb, sv, wts = _layer_fwd(h, hb, mem2, lb_all[l:l + 1], w_in_l, mix_fn, late_fn, bl=bl, seq=seq, alpha=alpha, deps=deps)
        saved.append(sv)
        layer_wts.append(wts)
    loss, dh = _loss_head(h, t2d)

    shape3 = {name: (depth, weights[name].size // (depth * weights[name].shape[-1]), weights[name].shape[-1]) for name, _ in SHARDED}
    partial = [dict() for _ in range(depth)]
    smalls = [None] * depth
    outs = {name: None for name, _ in SHARDED}

    def finish_reduce(pend, l, after):
        ex, state, names, _ = pend
        sent, got = ex.wait(state, after=after)
        for nm, g_full, landed in zip(names, sent, got):
            partial[l][nm] = _sum_own_and_peers(me, g_full, shard_axis[nm], landed)

    def optimizer_step(l):
        names = [name for name, _ in SHARDED]
        theirs = _sibling_swap(f"swap_partials_l{l}", [partial[l][nm] for nm in names])
        for nm, other in zip(names, theirs):
            outs[nm] = _adamw_layer(weights[nm].reshape(shape3[nm]), mom_m[nm].reshape(shape3[nm]), mom_v[nm].reshape(shape3[nm]),
                                    partial[l][nm], other, l, outs[nm])
        return tuple(outs[nm][0] for nm in names)

    pending_mix, deps = [], ()
    dz2, dz2b, dg2, db2 = _ln_bwd(dh, saved[-1]["xhat2"], saved[-1]["rstd2"], layer_wts[-1]["ln2_g"])
    for l in reversed(range(depth)):
        dz1, dz1b, g_mlp = _mlp_bwd(dz2, dz2b, saved[l], layer_wts[l], alpha=alpha, deps=deps)
        g_mlp["ln2_g"], g_mlp["ln2_b"] = dg2[0:1], db2[0:1]
        pending_mlp = start_exchange(f"reduce_mlp_l{l}", "scatter", [(nm, g_mlp[nm]) for nm in ("w_up", "w_down")])
        deps = (pending_mlp[3],)
        if pending_mix:
            for pend in pending_mix:
                finish_reduce(pend, l + 1, dz1)
            deps += optimizer_step(l + 1)
        pending_mix = []

        def send(names, g, l=l, pending_mix=pending_mix):
            pend = start_exchange(f"reduce_{names[0]}_l{l}", "scatter", [(nm, g[nm]) for nm in names])
            pending_mix.append(pend)
            return pend[3]

        below = (saved[l - 1]["xhat2"], saved[l - 1]["rstd2"], layer_wts[l - 1]["ln2_g"]) if l > 0 else None
        out, g = _mix_bwd(dz1, dz1b, saved[l], mem2, lb_all[l:l + 1], layer_wts[l], bl=bl, seq=seq, alpha=alpha, send=send,
                          below=below, deps=deps)
        if l > 0:
            dz2, dz2b, dg2, db2 = out
        else:
            dh = out
        finish_reduce(pending_mlp, l, out[0] if l > 0 else out)
        deps = ()
        g.update(g_mlp, lower_bounds=g["lb"])
        smalls[l] = jnp.concatenate([g[nm] for nm in SMALL], axis=1)
    small_parts = _chip_exchange("reduce_small", [(jnp.stack(smalls), "bcast", 0)])[0]
    small_sum = _sum_chips_into(small_parts.reshape(N_CHIPS, depth, -1), jnp.zeros((1, depth, small_parts.shape[-1]), F32), 0)
    small_sum = small_sum.reshape(depth, 1, -1)
    small_theirs = _sibling_swap("swap_small", [small_sum])[0]
    for pend in pending_mix:
        finish_reduce(pend, 0, small_theirs)
    optimizer_step(0)

    outs = {name: [r.reshape(weights[name].shape) for r in res] for name, res in outs.items()}
    off = 0
    for name in SMALL:
        n = weights[name].shape[1]
        mine, other = small_sum[:, :, off:off + n], small_theirs[:, :, off:off + n]
        off += n
        if name == "lower_bounds":
            mine = _lower_bounds_bwd(soft, mine[:, 0, :])[:, None, :]
            other = _lower_bounds_bwd(soft, other[:, 0, :])[:, None, :]
        shp = (depth, 1, n)
        res = _adamw(weights[name].reshape(shp), mom_m[name].reshape(shp), mom_v[name].reshape(shp), mine, other)
        outs[name] = [r.reshape(weights[name].shape) for r in res]
    assert off == small_sum.shape[-1]

    total_loss = lax.psum(loss[0, 0], ("x", "y", "c"))
    result = [total_loss, dh.reshape(bl, seq, d)]
    for k in range(4):
        result += [outs[name][k] for name in WEIGHT_ORDER]
    return tuple(result)
```

```python
import functools

import jax
import jax.numpy as jnp
from jax import lax
from jax.experimental import pallas as pl
from jax.experimental.pallas import tpu as pltpu

F32 = jnp.float32
BF16 = jnp.bfloat16

HG_HEADS = 4
HG_F = 128
HG_CHUNK = 32
MEM_HEADS = 4
MEM_HEAD_DIM = 128
BRANCH_WIDTH = 512
N_BRANCH = 3
CONV_K = 3
LN_EPS = 1e-5
RMS_EPS = 1e-6
ADAM_LR = 0.001
ADAM_B1 = 0.9
ADAM_B2 = 0.999
ADAM_EPS = 1e-08
ADAM_WD = 0.01
ADAM_STEP = 10

VMEM_LIMIT = 48 * 1024 * 1024


def _cparams(sem):
    return pltpu.CompilerParams(dimension_semantics=sem, vmem_limit_bytes=VMEM_LIMIT)


def _dot(a, b, dims):
    return lax.dot_general(a, b, (dims, ((), ())), preferred_element_type=F32)


NN = ((1,), (0,))
NT = ((1,), (1,))
TN = ((0,), (0,))


def _pick(n, pref):
    for t in pref:
        if n % t == 0:
            return t
    return n


ANY_SPEC = pl.BlockSpec(memory_space=pl.ANY)


def _matmul(name, a, b, *, mode, out_dtype=F32, a_fn=None, a_extra=(), epi_fn=None, epi_extra=(), n_out=1, out_kinds=None,
            tm=512, tn=1024, tk=1024, deps=()):
    M, K = a.shape
    N = b.shape[1] if mode == "nn" else b.shape[0]
    tm, tn, tk = _pick(M, (tm, 256, 128, 8)), _pick(N, (tn, 896, 512, 256, 128)), _pick(K, (tk, 512, 256, 128))
    nk = K // tk
    n_ax, n_ex = len(a_extra), len(epi_extra)
    n_in = 2 + n_ax + n_ex + len(deps)
    out_dtypes = out_dtype if isinstance(out_dtype, (tuple, list)) else (out_dtype,) * n_out
    out_kinds = out_kinds or ("tile",) * n_out

    def body(*refs):
        a_ref, b_ref = refs[0], refs[1]
        ax_refs = refs[2:2 + n_ax]
        ex_refs = refs[2 + n_ax:2 + n_ax + n_ex]
        o_refs = refs[n_in:n_in + n_out]
        at = a_ref[...]
        at = a_fn(at, *[r[...] for r in ax_refs]) if a_fn is not None else at.astype(BF16)
        part = _dot(at, b_ref[...].astype(BF16), NN if mode == "nn" else NT)

        def finish(acc):
            outs = epi_fn(acc, *[r[...] for r in ex_refs]) if epi_fn is not None else (acc,)
            for o_ref, o, kind in zip(o_refs, outs, out_kinds):
                if kind == "rowsum":
                    @pl.when(pl.program_id(1) == 0)
                    def _(o_ref=o_ref):
                        o_ref[...] = jnp.zeros_like(o_ref)

                    o_ref[0:1, :] += o
                else:
                    o_ref[...] = o.astype(o_ref.dtype)

        if nk == 1:
            finish(part)
            return
        acc_ref = refs[-1]
        k = pl.program_id(2)

        @pl.when(k == 0)
        def _():
            acc_ref[...] = part

        @pl.when(jnp.logical_and(k > 0, k < nk - 1))
        def _():
            acc_ref[...] += part

        @pl.when(k == nk - 1)
        def _():
            finish(acc_ref[...] + part)

    b_mode = dict(pipeline_mode=pl.Buffered(1)) if (nk == 1 and N == tn) else {}
    in_specs = [pl.BlockSpec((tm, tk), lambda j, i, k: (i, k)),
                pl.BlockSpec((tk, tn), lambda j, i, k: (k, j), **b_mode) if mode == "nn"
                else pl.BlockSpec((tn, tk), lambda j, i, k: (j, k), **b_mode)]
    in_specs += [pl.BlockSpec((1, tk), lambda j, i, k: (0, k)) for _ in a_extra]
    for e in epi_extra:
        if e.shape[0] == 1:
            in_specs.append(pl.BlockSpec((1, tn), lambda j, i, k: (0, j)))
        elif e.shape[1] == 1:
            in_specs.append(pl.BlockSpec((tm, 1), lambda j, i, k: (i, 0)))
        else:
            in_specs.append(pl.BlockSpec((tm, tn), lambda j, i, k: (i, j)))
    in_specs += [ANY_SPEC] * len(deps)
    out_specs, out_shapes = [], []
    for kind, dt in zip(out_kinds, out_dtypes):
        if kind == "col":
            out_specs.append(pl.BlockSpec((tm, 1), lambda j, i, k: (i, 0)))
            out_shapes.append(jax.ShapeDtypeStruct((M, 1), dt))
        elif kind == "rowsum":
            out_specs.append(pl.BlockSpec((8, tn), lambda j, i, k: (0, j)))
            out_shapes.append(jax.ShapeDtypeStruct((8, N), dt))
        else:
            out_specs.append(pl.BlockSpec((tm, tn), lambda j, i, k: (i, j)))
            out_shapes.append(jax.ShapeDtypeStruct((M, N), dt))
    out = pl.pallas_call(
        body,
        name=name,
        grid=(N // tn, M // tm, nk),
        in_specs=in_specs,
        out_specs=out_specs,
        out_shape=out_shapes,
        scratch_shapes=[pltpu.VMEM((tm, tn), F32)] if nk > 1 else [],
        compiler_params=_cparams(("arbitrary", "arbitrary", "arbitrary")),
    )(a, b, *a_extra, *epi_extra, *deps)
    return out[0] if n_out == 1 else out


def _matmul_tn(name, a, b, *, a_fn=None, a_extra=(), a_cols=None, b_cols=None, ta=1024, tb=1024, tt=1024, out_dtype=F32, deps=()):
    T = a.shape[0]
    a0, Ka = a_cols if a_cols is not None else (0, a.shape[1])
    b0, Nb = b_cols if b_cols is not None else (0, b.shape[1])
    ta, tb, tt = _pick(Ka, (ta, 512, 256, 128)), _pick(Nb, (tb, 896, 512, 256, 128)), _pick(T, (tt, 512, 256, 128))
    assert a0 % ta == 0 and b0 % tb == 0
    a0, b0 = a0 // ta, b0 // tb
    nt = T // tt
    n_ax = len(a_extra)

    def body(*refs):
        a_ref, b_ref = refs[0], refs[1]
        ax_refs = refs[2:2 + n_ax]
        o_ref = refs[2 + n_ax + len(deps)]
        acc_ref = refs[-1]
        t = pl.program_id(2)
        at = a_ref[...]
        at = a_fn(at, *[r[...] for r in ax_refs]) if a_fn is not None else at.astype(BF16)
        part = _dot(at, b_ref[...].astype(BF16), TN)

        @pl.when(t == 0)
        def _():
            acc_ref[...] = part

        @pl.when(jnp.logical_and(t > 0, t < nt - 1))
        def _():
            acc_ref[...] += part

        @pl.when(t == nt - 1)
        def _():
            o_ref[...] = (acc_ref[...] + part if nt > 1 else part).astype(o_ref.dtype)

    in_specs = [pl.BlockSpec((tt, ta), lambda i, j, t: (t, a0 + i)), pl.BlockSpec((tt, tb), lambda i, j, t: (t, b0 + j))]
    in_specs += [pl.BlockSpec((1, ta), lambda i, j, t: (0, a0 + i)) for _ in a_extra]
    in_specs += [ANY_SPEC] * len(deps)
    return pl.pallas_call(
        body,
        name=name,
        grid=(Ka // ta, Nb // tb, nt),
        in_specs=in_specs,
        out_specs=pl.BlockSpec((ta, tb), lambda i, j, t: (i, j)),
        out_shape=jax.ShapeDtypeStruct((Ka, Nb), out_dtype),
        scratch_shapes=[pltpu.VMEM((ta, tb), F32)],
        compiler_params=_cparams(("parallel", "parallel", "arbitrary")),
    )(a, b, *a_extra, *deps)


W = BRANCH_WIDTH
C_CB, C_CC, C_CH, C_HQ, C_HF, C_HI, C_HG, C_MQ, N_MIX = 0, W, 2 * W, 3 * W, 4 * W, 5 * W, 6 * W, 7 * W, 8 * W
TS_MIX = 256
PREV_ROWS = 16
KEEP_NAMES = ("sq", "qs", "k", "sig", "f", "ea", "eb", "eq", "ek")


def _sigmoid(x):
    return jax.nn.sigmoid(x)


def _chunk_pos(shape):
    return lax.broadcasted_iota(jnp.int32, shape, 0) & (HG_CHUNK - 1)


def _seg_cumsum(x, pos):
    sh = 1
    while sh < HG_CHUNK:
        x = x + jnp.where(pos >= sh, pltpu.roll(x, sh, 0), 0.0)
        sh *= 2
    return x


def _seg_rev_cumsum(x, pos):
    n = x.shape[0]
    sh = 1
    while sh < HG_CHUNK:
        x = x + jnp.where(pos < HG_CHUNK - sh, pltpu.roll(x, n - sh, 0), 0.0)
        sh *= 2
    return x


def _chunk_mask(ts):
    r = lax.broadcasted_iota(jnp.int32, (ts, ts), 0)
    c = lax.broadcasted_iota(jnp.int32, (ts, ts), 1)
    return jnp.logical_and((r // HG_CHUNK) == (c // HG_CHUNK), c <= r)


def _hgrn_gates(p_ref, lb):
    q = p_ref[:, C_HQ:C_HQ + W].astype(F32)
    fl = p_ref[:, C_HF:C_HF + W].astype(F32)
    sig = _sigmoid(fl)
    f = lb + (1.0 - lb) * sig
    logf = jnp.log(f)
    k = (1.0 - lb) * _sigmoid(-fl)
    sq = _sigmoid(q)
    qs = q * sq
    return q, sq, qs, sig, f, logf, k


def _hgrn_decays(logf, bc_sc, ts):
    pos = _chunk_pos(logf.shape)
    bc = _seg_cumsum(logf, pos)
    bc_sc[...] = bc
    nc = ts // HG_CHUNK
    bref = jnp.concatenate(
        [jnp.broadcast_to(bc_sc[n * HG_CHUNK + HG_CHUNK // 2 - 1:n * HG_CHUNK + HG_CHUNK // 2, :], (HG_CHUNK, W)) for n in range(nc)], axis=0)
    blast = jnp.concatenate(
        [jnp.broadcast_to(bc_sc[(n + 1) * HG_CHUNK - 1:(n + 1) * HG_CHUNK, :], (HG_CHUNK, W)) for n in range(nc)], axis=0)
    return pos, bc, bref, blast


def _conv_shift_down(u, carry_ref, row):
    n = carry_ref.shape[0]
    last, before = carry_ref[n - 1:n, :], carry_ref[n - 2:n - 1, :]
    u1 = jnp.where(row == 0, last, pltpu.roll(u, 1, 0))
    u2 = jnp.where(row == 0, before, jnp.where(row == 1, last, pltpu.roll(u, 2, 0)))
    return u1, u2


def _attn_probs(qh, kh):
    s = _dot(qh, kh, NT) * (MEM_HEAD_DIM ** -0.5)
    e = jnp.exp(s - jnp.max(s, axis=-1, keepdims=True))
    return e / jnp.sum(e, axis=-1, keepdims=True)


def _mixer_fwd(p, mk, mv, lb, conv_w, norm_w, *, bl, seq):
    T = p.shape[0]
    ts = TS_MIX
    ns = seq // ts
    nc = ts // HG_CHUNK
    ml = mk.shape[0] // bl

    def body(p_ref, mk_ref, mv_ref, lb_ref, cw_ref, nw_ref, y_ref, st_ref, opre_ref, state_sc, carry_sc, bc_sc):
        @pl.when(pl.program_id(1) == 0)
        def _():
            state_sc[...] = jnp.zeros_like(state_sc)
            carry_sc[...] = jnp.zeros_like(carry_sc)

        cb, cc, ch = (p_ref[:, c0:c0 + W].astype(F32) for c0 in (C_CB, C_CC, C_CH))
        u = cc * ch
        row = lax.broadcasted_iota(jnp.int32, (ts, W), 0)
        u1, u2 = _conv_shift_down(u, carry_sc, row)
        yconv = u2 * cw_ref[0:1, :] + u1 * cw_ref[1:2, :] + u * cw_ref[2:3, :]
        y_ref[:, 0:W] = (cb * yconv).astype(BF16)
        carry_sc[...] = u[ts - 8:ts, :]

        lbv = lb_ref[...]
        _, _, qs, _, _, logf, k = _hgrn_gates(p_ref, lbv)
        pos, bc, bref, blast = _hgrn_decays(logf, bc_sc, ts)
        a_all = (qs * jnp.exp(bc - bref)).astype(BF16)
        bk_all = (k * jnp.exp(bref - bc)).astype(BF16)
        qin_all = (qs * jnp.exp(bc)).astype(BF16)
        kout_all = (k * jnp.exp(blast - bc)).astype(BF16)
        v_all = p_ref[:, C_HI:C_HI + W].astype(BF16)
        mask = _chunk_mask(ts)
        heads = [slice(h * HG_F, (h + 1) * HG_F) for h in range(HG_HEADS)]
        st = [state_sc[h] for h in range(HG_HEADS)]
        o_inter = [[] for _ in range(HG_HEADS)]
        for n in range(nc):
            rows = slice(n * HG_CHUNK, (n + 1) * HG_CHUNK)
            for h, hs in enumerate(heads):
                st_ref[n, h] = st[h]
                o_inter[h].append(_dot(qin_all[rows, hs], st[h].astype(BF16), NT))
                kv = _dot(v_all[rows, hs], kout_all[rows, hs], TN)
                decay = jnp.exp(bc_sc[(n + 1) * HG_CHUNK - 1:(n + 1) * HG_CHUNK, hs])
                st[h] = st[h] * decay + kv
        for h in range(HG_HEADS):
            state_sc[h] = st[h]
        scores = [_dot(a_all[:, hs], bk_all[:, hs], NT) for hs in heads]
        scores = [jnp.where(mask, s, 0.0).astype(BF16) for s in scores]
        outs = [_dot(scores[h], v_all[:, hs], NN) + jnp.concatenate(o_inter[h], axis=0) for h, hs in enumerate(heads)]
        for h, hs in enumerate(heads):
            o = outs[h]
            opre_ref[:, hs] = o
            on = o * lax.rsqrt(jnp.mean(o * o, axis=-1, keepdims=True) + RMS_EPS) * nw_ref[...]
            g = p_ref[:, C_HG + h * HG_F:C_HG + (h + 1) * HG_F].astype(F32)
            y_ref[:, W + h * HG_F:W + (h + 1) * HG_F] = (on * (g * _sigmoid(g))).astype(BF16)

        mheads = [slice(h * MEM_HEAD_DIM, (h + 1) * MEM_HEAD_DIM) for h in range(MEM_HEADS)]
        probs = [_attn_probs(p_ref[:, C_MQ + h * MEM_HEAD_DIM:C_MQ + (h + 1) * MEM_HEAD_DIM].astype(BF16), mk_ref[:, hs])
                 for h, hs in enumerate(mheads)]
        for h, hs in enumerate(mheads):
            y_ref[:, 2 * W + h * MEM_HEAD_DIM:2 * W + (h + 1) * MEM_HEAD_DIM] = _dot(
                probs[h].astype(BF16), mv_ref[:, hs], NN).astype(BF16)

    return pl.pallas_call(
        body,
        name="mixer_fwd",
        grid=(bl, ns),
        in_specs=[
            pl.BlockSpec((ts, N_MIX), lambda b, s: (b * ns + s, 0)),
            pl.BlockSpec((ml, W), lambda b, s: (b, 0)),
            pl.BlockSpec((ml, W), lambda b, s: (b, 0)),
            pl.BlockSpec((1, W), lambda b, s: (0, 0)),
            pl.BlockSpec((CONV_K, W), lambda b, s: (0, 0)),
            pl.BlockSpec((1, HG_F), lambda b, s: (0, 0)),
        ],
        out_specs=[
            pl.BlockSpec((ts, 3 * W), lambda b, s: (b * ns + s, 0)),
            pl.BlockSpec((nc, HG_HEADS, HG_F, HG_F), lambda b, s: (b * ns + s, 0, 0, 0)),
            pl.BlockSpec((ts, W), lambda b, s: (b * ns + s, 0)),
        ],
        out_shape=[
            jax.ShapeDtypeStruct((T, 3 * W), BF16),
            jax.ShapeDtypeStruct((T // HG_CHUNK, HG_HEADS, HG_F, HG_F), F32),
            jax.ShapeDtypeStruct((T, W), F32),
        ],
        scratch_shapes=[pltpu.VMEM((HG_HEADS, HG_F, HG_F), F32), pltpu.VMEM((8, W), F32), pltpu.VMEM((ts, W), F32)],
        compiler_params=_cparams(("arbitrary", "arbitrary")),
    )(p, mk, mv, lb, conv_w, norm_w)


def _mixer_bwd(p, dy, dp_gates, st, opre, mk, mv, lb, conv_w, norm_w, *, bl, seq, deps=()):
    T, nin = p.shape
    ts = TS_MIX
    ns = seq // ts
    nc = ts // HG_CHUNK
    ml = mk.shape[0] // bl
    mid, last = HG_CHUNK // 2 - 1, HG_CHUNK - 1

    def body(p_ref, pprev_ref, dy_ref, dpin_ref, st_ref, opre_ref, mk_ref, mv_ref, lb_ref, cw_ref, nw_ref, *rest):
        (dp_ref, dmk_ref, dmv_ref, dcw_ref, dnw_ref, dlb_ref, dstate_sc, carry_sc, uprev_sc, ab_sc, bkb_sc, qinb_sc, koutb_sc,
         dob_sc, dv_sc, da_sc, dbk_sc, dqin_sc, dkout_sc, dec_sc, ddec_sc, *keep_scs) = rest[len(deps):]
        del dpin_ref
        b, s = pl.program_id(0), pl.program_id(1)

        @pl.when(s == 0)
        def _():
            dstate_sc[...] = jnp.zeros_like(dstate_sc)
            carry_sc[...] = jnp.zeros_like(carry_sc)
            dmk_ref[...] = jnp.zeros_like(dmk_ref)
            dmv_ref[...] = jnp.zeros_like(dmv_ref)

        @pl.when(jnp.logical_and(b == 0, s == 0))
        def _():
            dcw_ref[...] = jnp.zeros_like(dcw_ref)
            dnw_ref[...] = jnp.zeros_like(dnw_ref)
            dlb_ref[...] = jnp.zeros_like(dlb_ref)

        cb, cc, ch = (p_ref[:, c0:c0 + W].astype(F32) for c0 in (C_CB, C_CC, C_CH))
        u = cc * ch
        row = lax.broadcasted_iota(jnp.int32, (ts, W), 0)
        uprev = pprev_ref[:, C_CC:C_CC + W].astype(F32) * pprev_ref[:, C_CH:C_CH + W].astype(F32)
        uprev_sc[...] = jnp.where(s == ns - 1, 0.0, uprev)
        u1, u2 = _conv_shift_down(u, uprev_sc, row)
        w0, w1, w2 = cw_ref[0:1, :], cw_ref[1:2, :], cw_ref[2:3, :]
        dya = dy_ref[:, 0:W].astype(F32)
        dp_ref[:, C_CB:C_CB + W] = (dya * (u2 * w0 + u1 * w1 + u * w2)).astype(BF16)
        dv = cb * dya
        dv1 = jnp.where(row == ts - 1, carry_sc[0:1, :], pltpu.roll(dv, ts - 1, 0))
        dv2 = jnp.where(row == ts - 1, carry_sc[1:2, :], jnp.where(row == ts - 2, carry_sc[0:1, :], pltpu.roll(dv, ts - 2, 0)))
        du = dv * w2 + dv1 * w1 + dv2 * w0
        dp_ref[:, C_CC:C_CC + W] = (du * ch).astype(BF16)
        dp_ref[:, C_CH:C_CH + W] = (du * cc).astype(BF16)
        dcw_ref[0:1, :] += jnp.sum(dv * u2, axis=0, keepdims=True)
        dcw_ref[1:2, :] += jnp.sum(dv * u1, axis=0, keepdims=True)
        dcw_ref[2:3, :] += jnp.sum(dv * u, axis=0, keepdims=True)
        carry_sc[...] = dv[0:8, :]

        mask = _chunk_mask(ts)
        pos_c = _chunk_pos((HG_CHUNK, HG_F))
        nw = nw_ref[...]

        def block(n, h):
            rows = slice(n * HG_CHUNK, (n + 1) * HG_CHUNK)
            return rows, slice(h * HG_F, (h + 1) * HG_F)

        keep = dict(zip(KEEP_NAMES, keep_scs))

        def gates(rows, h):
            lbh = lb_ref[:, h * HG_F:(h + 1) * HG_F]
            q = p_ref[rows, C_HQ + h * HG_F:C_HQ + (h + 1) * HG_F].astype(F32)
            fl = p_ref[rows, C_HF + h * HG_F:C_HF + (h + 1) * HG_F].astype(F32)
            sig = _sigmoid(fl)
            f = lbh + (1.0 - lbh) * sig
            k = (1.0 - lbh) * _sigmoid(-fl)
            sq = _sigmoid(q)
            qs = q * sq
            bc = _seg_cumsum(jnp.log(f), pos_c)
            bref = jnp.sum(jnp.where(pos_c == mid, bc, 0.0), axis=0, keepdims=True)
            blast = jnp.sum(jnp.where(pos_c == last, bc, 0.0), axis=0, keepdims=True)
            ea, eb, eq, ek = jnp.exp(bc - bref), jnp.exp(bref - bc), jnp.exp(bc), jnp.exp(blast - bc)
            return dict(sq=sq, qs=qs, k=k, sig=sig, f=f, ea=ea, eb=eb, eq=eq, ek=ek), blast

        dnw = jnp.zeros((1, HG_F), F32)
        for n in range(nc):
            for h in range(HG_HEADS):
                rows, hs = block(n, h)
                fw, blast = gates(rows, h)
                for name in KEEP_NAMES:
                    keep[name][rows, hs] = fw[name]
                ab_sc[rows, hs] = (fw["qs"] * fw["ea"]).astype(BF16)
                bkb_sc[rows, hs] = (fw["k"] * fw["eb"]).astype(BF16)
                qinb_sc[rows, hs] = (fw["qs"] * fw["eq"]).astype(BF16)
                koutb_sc[rows, hs] = (fw["k"] * fw["ek"]).astype(BF16)
                dec_sc[n:n + 1, hs] = jnp.exp(blast)
                o = opre_ref[rows, hs]
                g = p_ref[rows, C_HG + h * HG_F:C_HG + (h + 1) * HG_F].astype(F32)
                sg = _sigmoid(g)
                r = lax.rsqrt(jnp.mean(o * o, axis=-1, keepdims=True) + RMS_EPS)
                dyb = dy_ref[rows, W + h * HG_F:W + (h + 1) * HG_F].astype(F32)
                dp_ref[rows, C_HG + h * HG_F:C_HG + (h + 1) * HG_F] = (
                    dyb * (o * r * nw) * (sg * (1.0 + g * (1.0 - sg)))).astype(BF16)
                don = dyb * (g * sg)
                dnw = dnw + jnp.sum(don * o * r, axis=0, keepdims=True)
                dn = don * nw
                dob_sc[rows, hs] = (r * (dn - o * (r * r) * jnp.mean(dn * o, axis=-1, keepdims=True))).astype(BF16)
        dnw_ref[0:1, :] += dnw

        heads = [slice(h * HG_F, (h + 1) * HG_F) for h in range(HG_HEADS)]
        scores = [_dot(ab_sc[:, hs], bkb_sc[:, hs], NT) for hs in heads]
        dscores = [_dot(dob_sc[:, hs], p_ref[:, C_HI + h * HG_F:C_HI + (h + 1) * HG_F].astype(BF16), NT)
                   for h, hs in enumerate(heads)]
        scores = [jnp.where(mask, s, 0.0).astype(BF16) for s in scores]
        dscores = [jnp.where(mask, s, 0.0).astype(BF16) for s in dscores]
        for h, hs in enumerate(heads):
            dv_sc[:, hs] = _dot(scores[h], dob_sc[:, hs], TN)
            da_sc[:, hs] = _dot(dscores[h], bkb_sc[:, hs], NN)
            dbk_sc[:, hs] = _dot(dscores[h], ab_sc[:, hs], TN)
        dst = [dstate_sc[h] for h in range(HG_HEADS)]
        for n in reversed(range(nc)):
            for h in range(HG_HEADS):
                rows, hs = block(n, h)
                st_n = st_ref[n, h]
                decay = dec_sc[n:n + 1, hs]
                dstb = dst[h].astype(BF16)
                dob_n = dob_sc[rows, hs]
                dv_sc[rows, hs] += _dot(koutb_sc[rows, hs], dstb, NT)
                dkout_sc[rows, hs] = _dot(p_ref[rows, C_HI + h * HG_F:C_HI + (h + 1) * HG_F].astype(BF16), dstb, NN)
                ddec_sc[n:n + 1, hs] = jnp.sum(dst[h] * st_n, axis=0, keepdims=True) * decay
                dqin_sc[rows, hs] = _dot(dob_n, st_n.astype(BF16), NN)
                dst[h] = dst[h] * decay + _dot(dob_n, qinb_sc[rows, hs], TN)
        for h in range(HG_HEADS):
            dstate_sc[h] = dst[h]

        for h in range(HG_HEADS):
            dlb = jnp.zeros((1, HG_F), F32)
            for n in range(nc):
                rows, hs = block(n, h)
                fw = {name: keep[name][rows, hs] for name in KEEP_NAMES}
                lbh = lb_ref[:, h * HG_F:(h + 1) * HG_F]
                q = p_ref[rows, C_HQ + h * HG_F:C_HQ + (h + 1) * HG_F].astype(F32)
                da, dbk, dqin, dkout = da_sc[rows, hs], dbk_sc[rows, hs], dqin_sc[rows, hs], dkout_sc[rows, hs]
                w_a, w_b, w_q, w_k = da * fw["ea"], dbk * fw["eb"], dqin * fw["eq"], dkout * fw["ek"]
                dqs, dk = w_a + w_q, w_b + w_k
                t_a, t_b, t_q, t_k = w_a * fw["qs"], w_b * fw["k"], w_q * fw["qs"], w_k * fw["k"]
                s_ref = jnp.sum(t_b - t_a, axis=0, keepdims=True)
                s_last = jnp.sum(t_k, axis=0, keepdims=True) + ddec_sc[n:n + 1, hs]
                dbc = (t_a - t_b + t_q - t_k) + jnp.where(pos_c == mid, s_ref, 0.0) + jnp.where(pos_c == last, s_last, 0.0)
                dfk = _seg_rev_cumsum(dbc, pos_c) / fw["f"] - dk
                sig, sq = fw["sig"], fw["sq"]
                dp_ref[rows, C_HF + h * HG_F:C_HF + (h + 1) * HG_F] = (dfk * (1.0 - lbh) * sig * (1.0 - sig)).astype(BF16)
                dlb = dlb + jnp.sum(dfk * (1.0 - sig), axis=0, keepdims=True)
                dp_ref[rows, C_HQ + h * HG_F:C_HQ + (h + 1) * HG_F] = (dqs * (sq * (1.0 + q * (1.0 - sq)))).astype(BF16)
                dp_ref[rows, C_HI + h * HG_F:C_HI + (h + 1) * HG_F] = dv_sc[rows, hs].astype(BF16)
            dlb_ref[0:1, h * HG_F:(h + 1) * HG_F] += dlb

        mheads = [slice(h * MEM_HEAD_DIM, (h + 1) * MEM_HEAD_DIM) for h in range(MEM_HEADS)]
        qhs = [p_ref[:, C_MQ + h * MEM_HEAD_DIM:C_MQ + (h + 1) * MEM_HEAD_DIM].astype(BF16) for h in range(MEM_HEADS)]
        dobs = [dy_ref[:, 2 * W + h * MEM_HEAD_DIM:2 * W + (h + 1) * MEM_HEAD_DIM].astype(BF16) for h in range(MEM_HEADS)]
        probs = [_attn_probs(qhs[h], mk_ref[:, hs]) for h, hs in enumerate(mheads)]
        dprobs = [_dot(dobs[h], mv_ref[:, hs], NT) for h, hs in enumerate(mheads)]
        for h, hs in enumerate(mheads):
            prob = probs[h]
            dmv_ref[:, hs] += _dot(prob.astype(BF16), dobs[h], TN)
            ds = prob * (dprobs[h] - jnp.sum(dprobs[h] * prob, axis=-1, keepdims=True)) * (MEM_HEAD_DIM ** -0.5)
            dsb = ds.astype(BF16)
            dp_ref[:, C_MQ + h * MEM_HEAD_DIM:C_MQ + (h + 1) * MEM_HEAD_DIM] = _dot(dsb, mk_ref[:, hs], NN).astype(BF16)
            dmk_ref[:, hs] += _dot(dsb, qhs[h], TN)

    def tile(b, s):
        return b * ns + (ns - 1 - s)

    return pl.pallas_call(
        body,
        name="mixer_bwd",
        grid=(bl, ns),
        in_specs=[
            pl.BlockSpec((ts, N_MIX), lambda b, s: (tile(b, s), 0)),
            pl.BlockSpec((PREV_ROWS, N_MIX), lambda b, s: (jnp.maximum(tile(b, s) * (ts // PREV_ROWS) - 1, 0), 0)),
            pl.BlockSpec((ts, 3 * W), lambda b, s: (tile(b, s), 0)),
            pl.BlockSpec(memory_space=pl.ANY),
            pl.BlockSpec((nc, HG_HEADS, HG_F, HG_F), lambda b, s: (tile(b, s), 0, 0, 0)),
            pl.BlockSpec((ts, W), lambda b, s: (tile(b, s), 0)),
            pl.BlockSpec((ml, W), lambda b, s: (b, 0)),
            pl.BlockSpec((ml, W), lambda b, s: (b, 0)),
            pl.BlockSpec((1, W), lambda b, s: (0, 0)),
            pl.BlockSpec((CONV_K, W), lambda b, s: (0, 0)),
            pl.BlockSpec((1, HG_F), lambda b, s: (0, 0)),
        ] + [ANY_SPEC] * len(deps),
        out_specs=[
            pl.BlockSpec((ts, N_MIX), lambda b, s: (tile(b, s), 0)),
            pl.BlockSpec((ml, W), lambda b, s: (b, 0)),
            pl.BlockSpec((ml, W), lambda b, s: (b, 0)),
            pl.BlockSpec((8, W), lambda b, s: (0, 0)),
            pl.BlockSpec((8, HG_F), lambda b, s: (0, 0)),
            pl.BlockSpec((8, W), lambda b, s: (0, 0)),
        ],
        out_shape=[
            jax.ShapeDtypeStruct((T, nin), BF16),
            jax.ShapeDtypeStruct((bl * ml, W), F32),
            jax.ShapeDtypeStruct((bl * ml, W), F32),
            jax.ShapeDtypeStruct((8, W), F32),
            jax.ShapeDtypeStruct((8, HG_F), F32),
            jax.ShapeDtypeStruct((8, W), F32),
        ],
        input_output_aliases={3: 0},
        scratch_shapes=[pltpu.VMEM((HG_HEADS, HG_F, HG_F), F32), pltpu.VMEM((8, W), F32), pltpu.VMEM((PREV_ROWS, W), F32)]
        + [pltpu.VMEM((ts, W), BF16)] * 5 + [pltpu.VMEM((ts, W), F32)] * 5 + [pltpu.VMEM((nc, W), F32)] * 2
        + [pltpu.VMEM((ts, W), F32)] * len(KEEP_NAMES),
        compiler_params=_cparams(("arbitrary", "arbitrary")),
    )(p, p, dy, dp_gates, st, opre, mk, mv, lb, conv_w, norm_w, *deps)


def _layer_norm_stats(z):
    mu = jnp.mean(z, axis=-1, keepdims=True)
    zc = z - mu
    rstd = lax.rsqrt(jnp.mean(zc * zc, axis=-1, keepdims=True) + LN_EPS)
    return zc * rstd, rstd


def _gate_specs(tm, d):
    g0 = N_MIX // d
    return [pl.BlockSpec((tm, d), functools.partial(lambda i, k: (i, g0 + k), k=k)) for k in range(N_BRANCH)]


def _merge_fwd(y, p, x0, wb, wo, bg, ln_g, ln_b, *, alpha, tm=512):
    T, d = x0.shape
    assert N_MIX % d == 0
    tm = _pick(T, (tm, 128, 8))

    def body(y_ref, g0_ref, g1_ref, g2_ref, x_ref, wb_ref, wo_ref, bg_ref, lg_ref, lb_ref, mg_ref, xh_ref, rs_ref, x1b_ref):
        merged = None
        for i, g_ref in enumerate((g0_ref, g1_ref, g2_ref)):
            r = _dot(y_ref[:, i * W:(i + 1) * W], wb_ref[i * W:(i + 1) * W, :], NN)
            t = _sigmoid(g_ref[...].astype(F32) + bg_ref[:, i * d:(i + 1) * d]) * r
            merged = t if merged is None else merged + t
        mb = merged.astype(BF16)
        mg_ref[...] = mb
        z = alpha * x_ref[...] + _dot(mb, wo_ref[...], NN)
        xh, rs = _layer_norm_stats(z)
        xh_ref[...], rs_ref[...] = xh, rs
        x1b_ref[...] = (xh * lg_ref[...] + lb_ref[...]).astype(BF16)

    row = lambda i: (i, 0)
    fix = lambda i: (0, 0)
    return pl.pallas_call(
        body,
        name="merge_fwd",
        grid=(T // tm,),
        in_specs=[pl.BlockSpec((tm, 3 * W), row)] + _gate_specs(tm, d) + [
            pl.BlockSpec((tm, d), row), pl.BlockSpec((3 * W, d), fix, pipeline_mode=pl.Buffered(1)),
            pl.BlockSpec((d, d), fix, pipeline_mode=pl.Buffered(1)), pl.BlockSpec((1, 3 * d), fix),
            pl.BlockSpec((1, d), fix), pl.BlockSpec((1, d), fix)],
        out_specs=[pl.BlockSpec((tm, d), row), pl.BlockSpec((tm, d), row), pl.BlockSpec((tm, 1), row), pl.BlockSpec((tm, d), row)],
        out_shape=[jax.ShapeDtypeStruct((T, d), BF16), jax.ShapeDtypeStruct((T, d), F32), jax.ShapeDtypeStruct((T, 1), F32),
                   jax.ShapeDtypeStruct((T, d), BF16)],
        compiler_params=_cparams(("parallel",)),
    )(y, p, p, p, x0, wb, wo, bg, ln_g, ln_b)


def _merge_bwd(dz, p, y, wb, wo, bg, *, tm=512):
    T, d = dz.shape
    nin = p.shape[1]
    tm = _pick(T, (tm, 128, 8))

    def body(dz_ref, g0_ref, g1_ref, g2_ref, y_ref, wb_ref, wo_ref, bg_ref, dr_ref, dp_ref, dy_ref, dbg_ref):
        @pl.when(pl.program_id(0) == 0)
        def _():
            dbg_ref[...] = jnp.zeros_like(dbg_ref)

        dmerged = _dot(dz_ref[...].astype(BF16), wo_ref[...], NT)
        dp_ref[:, 0:N_MIX] = jnp.zeros((tm, N_MIX), BF16)
        for i, g_ref in enumerate((g0_ref, g1_ref, g2_ref)):
            cs = slice(i * d, (i + 1) * d)
            s = _sigmoid(g_ref[...].astype(F32) + bg_ref[:, cs])
            drb = (dmerged * s).astype(BF16)
            dr_ref[:, cs] = drb
            dgate = dmerged * _dot(y_ref[:, i * W:(i + 1) * W], wb_ref[i * W:(i + 1) * W, :], NN) * s * (1.0 - s)
            dp_ref[:, N_MIX + i * d:N_MIX + (i + 1) * d] = dgate.astype(BF16)
            dbg_ref[0:1, cs] += jnp.sum(dgate, axis=0, keepdims=True)
            dy_ref[:, i * W:(i + 1) * W] = _dot(drb, wb_ref[i * W:(i + 1) * W, :], NT).astype(BF16)

    row = lambda i: (i, 0)
    fix = lambda i: (0, 0)
    return pl.pallas_call(
        body,
        name="merge_bwd",
        grid=(T // tm,),
        in_specs=[pl.BlockSpec((tm, d), row)] + _gate_specs(tm, d) + [
            pl.BlockSpec((tm, 3 * W), row), pl.BlockSpec((3 * W, d), fix, pipeline_mode=pl.Buffered(1)),
            pl.BlockSpec((d, d), fix, pipeline_mode=pl.Buffered(1)), pl.BlockSpec((1, 3 * d), fix)],
        out_specs=[pl.BlockSpec((tm, 3 * d), row), pl.BlockSpec((tm, nin), row), pl.BlockSpec((tm, 3 * W), row),
                   pl.BlockSpec((8, 3 * d), fix)],
        out_shape=[jax.ShapeDtypeStruct((T, 3 * d), BF16), jax.ShapeDtypeStruct((T, nin), BF16),
                   jax.ShapeDtypeStruct((T, 3 * W), BF16), jax.ShapeDtypeStruct((8, 3 * d), F32)],
        compiler_params=_cparams(("arbitrary",)),
    )(dz, p, p, p, y, wb, wo, bg)


def _mlp_fwd(xhat1, x1b, g1, b1, wu, wd, g2, b2, *, alpha, tm=512, tf=2048):
    T, d = xhat1.shape
    ff = wu.shape[1]
    tm, tf = _pick(T, (tm, 256, 128, 8)), _pick(ff, (tf, 1024, 512, 256, 128))
    nf = ff // tf

    def body(xh_ref, x1b_ref, g1_ref, b1_ref, wu_ref, wd_ref, g2_ref, b2_ref, a_ref, xh2_ref, rs2_ref, x2_ref, x2b_ref, acc_ref):
        f = pl.program_id(1)
        a = _dot(x1b_ref[...], wu_ref[...], NN)
        a_ref[...] = a.astype(BF16)
        h = jnp.square(jnp.maximum(a, 0.0))
        part = _dot(h.astype(BF16), wd_ref[...], NN)

        @pl.when(f == 0)
        def _():
            acc_ref[...] = part

        @pl.when(jnp.logical_and(f > 0, f < nf - 1))
        def _():
            acc_ref[...] += part

        @pl.when(f == nf - 1)
        def _():
            x1 = xh_ref[...] * g1_ref[...] + b1_ref[...]
            xh2, rs2 = _layer_norm_stats(alpha * x1 + (acc_ref[...] + part if nf > 1 else part))
            xh2_ref[...] = xh2
            rs2_ref[...] = rs2
            x2 = xh2 * g2_ref[...] + b2_ref[...]
            x2_ref[...] = x2
            x2b_ref[...] = x2.astype(BF16)

    row = lambda i, f: (i, 0)
    fix = lambda i, f: (0, 0)
    return pl.pallas_call(
        body,
        name="mlp_fwd",
        grid=(T // tm, nf),
        in_specs=[pl.BlockSpec((tm, d), row), pl.BlockSpec((tm, d), row), pl.BlockSpec((1, d), fix), pl.BlockSpec((1, d), fix),
                  pl.BlockSpec((d, tf), lambda i, f: (0, f)), pl.BlockSpec((tf, d), lambda i, f: (f, 0)),
                  pl.BlockSpec((1, d), fix), pl.BlockSpec((1, d), fix)],
        out_specs=[pl.BlockSpec((tm, tf), lambda i, f: (i, f)), pl.BlockSpec((tm, d), row), pl.BlockSpec((tm, 1), row),
                   pl.BlockSpec((tm, d), row), pl.BlockSpec((tm, d), row)],
        out_shape=[jax.ShapeDtypeStruct((T, ff), BF16), jax.ShapeDtypeStruct((T, d), F32), jax.ShapeDtypeStruct((T, 1), F32),
                   jax.ShapeDtypeStruct((T, d), F32), jax.ShapeDtypeStruct((T, d), BF16)],
        scratch_shapes=[pltpu.VMEM((tm, d), F32)],
        compiler_params=_cparams(("parallel", "arbitrary")),
    )(xhat1, x1b, g1, b1, wu, wd, g2, b2)


def _ln_bwd(dy, xhat, rstd, g, *, tm=512, deps=()):
    T, d = dy.shape
    tm = _pick(T, (tm, 256, 128, 8))

    def body(dy_ref, xh_ref, rs_ref, g_ref, *rest):
        dz_ref, dzb_ref, dg_ref, db_ref = rest[len(deps):]

        @pl.when(pl.program_id(0) == 0)
        def _():
            dg_ref[...] = jnp.zeros_like(dg_ref)
            db_ref[...] = jnp.zeros_like(db_ref)

        dy_, xh = dy_ref[...], xh_ref[...]
        dg_ref[0:1, :] += jnp.sum(dy_ * xh, axis=0, keepdims=True)
        db_ref[0:1, :] += jnp.sum(dy_, axis=0, keepdims=True)
        dxh = dy_ * g_ref[...]
        dz = rs_ref[...] * (dxh - jnp.mean(dxh, axis=-1, keepdims=True) - xh * jnp.mean(dxh * xh, axis=-1, keepdims=True))
        dz_ref[...] = dz
        dzb_ref[...] = dz.astype(BF16)

    row = lambda i: (i, 0)
    fix = lambda i: (0, 0)
    return pl.pallas_call(
        body,
        name="ln_bwd",
        grid=(T // tm,),
        in_specs=[pl.BlockSpec((tm, d), row), pl.BlockSpec((tm, d), row), pl.BlockSpec((tm, 1), row), pl.BlockSpec((1, d), fix)]
        + [ANY_SPEC] * len(deps),
        out_specs=[pl.BlockSpec((tm, d), row), pl.BlockSpec((tm, d), row), pl.BlockSpec((8, d), fix), pl.BlockSpec((8, d), fix)],
        out_shape=[jax.ShapeDtypeStruct((T, d), F32), jax.ShapeDtypeStruct((T, d), BF16), jax.ShapeDtypeStruct((8, d), F32),
                   jax.ShapeDtypeStruct((8, d), F32)],
        compiler_params=_cparams(("arbitrary",)),
    )(dy, xhat, rstd, g, *deps)


def _loss_head(y, target, *, tm=512):
    T, d = y.shape
    tm = _pick(T, (tm, 256, 128, 8))
    n = T // tm

    def body(y_ref, t_ref, loss_ref, dy_ref, acc_ref):
        i = pl.program_id(0)

        @pl.when(i == 0)
        def _():
            acc_ref[...] = jnp.zeros_like(acc_ref)

        e = y_ref[...] - t_ref[...]
        dy_ref[...] = e * (1.0 / d)
        acc_ref[...] += jnp.sum(e * e, axis=0, keepdims=True)

        @pl.when(i == n - 1)
        def _():
            loss_ref[...] = (0.5 / d) * jnp.sum(acc_ref[...], axis=1, keepdims=True)

    row = lambda i: (i, 0)
    return pl.pallas_call(
        body,
        name="loss_head",
        grid=(n,),
        in_specs=[pl.BlockSpec((tm, d), row), pl.BlockSpec((tm, d), row)],
        out_specs=[pl.BlockSpec((1, 1), lambda i: (0, 0)), pl.BlockSpec((tm, d), row)],
        out_shape=[jax.ShapeDtypeStruct((1, 1), F32), jax.ShapeDtypeStruct((T, d), F32)],
        scratch_shapes=[pltpu.VMEM((1, d), F32)],
        compiler_params=_cparams(("arbitrary",)),
    )(y, target)


def _lower_bounds_fwd(lower_bounds):
    depth, n = lower_bounds.shape

    def body(x_ref, soft_ref, lb_ref):
        x = x_ref[...]
        e = jnp.exp(x - jnp.max(x, axis=0, keepdims=True))
        soft_ref[...] = e / jnp.sum(e, axis=0, keepdims=True)
        run = None
        for l in range(depth):
            run = soft_ref[l:l + 1, :] if run is None else run + soft_ref[l:l + 1, :]
            lb_ref[l:l + 1, :] = run - soft_ref[0:1, :]

    return pl.pallas_call(body, name="lower_bounds_fwd",
                          out_shape=[jax.ShapeDtypeStruct((depth, n), F32), jax.ShapeDtypeStruct((depth, n), F32)])(lower_bounds)


def _lower_bounds_bwd(soft, dlb):
    depth, n = soft.shape

    def body(soft_ref, dlb_ref, out_ref, dsoft_ref):
        total = jnp.sum(dlb_ref[...], axis=0, keepdims=True)
        run = None
        for l in reversed(range(depth)):
            run = dlb_ref[l:l + 1, :] if run is None else run + dlb_ref[l:l + 1, :]
            dsoft_ref[l:l + 1, :] = run - total if l == 0 else run
        s, ds = soft_ref[...], dsoft_ref[...]
        out_ref[...] = s * (ds - jnp.sum(s * ds, axis=0, keepdims=True))

    return pl.pallas_call(body, name="lower_bounds_bwd", out_shape=jax.ShapeDtypeStruct((depth, n), F32),
                          scratch_shapes=[pltpu.VMEM((depth, n), F32)])(soft, dlb)


def _layer_fwd(x0, x0b, mem2, lb, w_in, mix_fn, late_fn, *, bl, seq, alpha, deps=()):
    p = _matmul("proj_in", x0b, w_in, mode="nn", out_dtype=BF16, deps=deps, tm=1024, tn=1792)
    wts = dict(mix_fn(p), w_in=w_in)
    mk = _matmul("mem_k", mem2, wts["w_mem_k"], mode="nn", out_dtype=BF16)
    mv = _matmul("mem_v", mem2, wts["w_mem_v"], mode="nn", out_dtype=BF16)
    y, st, opre = _mixer_fwd(p, mk, mv, lb, wts["conv_w"], wts["hg_norm_w"], bl=bl, seq=seq)
    wts.update(late_fn(y))
    merged, xhat1, rstd1, x1b = _merge_fwd(y, p, x0, wts["w_branch"], wts["w_o"], wts["b_gate"], wts["ln1_g"], wts["ln1_b"],
                                           alpha=alpha)
    a, xhat2, rstd2, x2, x2b = _mlp_fwd(xhat1, x1b, wts["ln1_g"], wts["ln1_b"], wts["w_up"], wts["w_down"], wts["ln2_g"],
                                        wts["ln2_b"], alpha=alpha)
    saved = dict(x0b=x0b, p=p, mk=mk, mv=mv, y=y, st=st, opre=opre, merged=merged, xhat1=xhat1, rstd1=rstd1, x1b=x1b, a=a,
                 xhat2=xhat2, rstd2=rstd2)
    return x2, x2b, saved, wts


def _relu2_bf16(a):
    return jnp.square(jnp.maximum(a.astype(F32), 0.0)).astype(BF16)


def _mlp_bwd(dz2, dz2b, sv, wts, *, alpha, deps=()):
    g = {}
    da = _matmul("mlp_da", dz2b, wts["w_down"], mode="nt", out_dtype=BF16, tm=512, tn=wts["w_down"].shape[0], deps=deps,
                 epi_fn=lambda acc, a: (acc * (2.0 * jnp.maximum(a.astype(F32), 0.0)),), epi_extra=(sv["a"],))
    g["w_down"] = _matmul_tn("grad_w_down", sv["a"], dz2b, a_fn=_relu2_bf16, out_dtype=BF16, tt=2048)
    g["w_up"] = _matmul_tn("grad_w_up", sv["x1b"], da, out_dtype=BF16, tt=2048)
    dx1 = _matmul("mlp_dx", da, wts["w_up"], mode="nt", epi_fn=lambda acc, dz: (acc + alpha * dz,), epi_extra=(dz2,),
                  tm=512, tk=4096)
    dz1, dz1b, dg1, db1 = _ln_bwd(dx1, sv["xhat1"], sv["rstd1"], wts["ln1_g"])
    g["ln1_g"], g["ln1_b"] = dg1[0:1], db1[0:1]
    return dz1, dz1b, g


def _mix_bwd(dz1, dz1b, sv, mem2, lb, wts, *, bl, seq, alpha, send, below=None, deps=()):
    d = dz1.shape[1]
    g = {}
    g["w_o"] = _matmul_tn("grad_w_o", sv["merged"], dz1b, out_dtype=BF16, tt=2048, deps=deps)
    dr, dp, dy, dbg = _merge_bwd(dz1b, sv["p"], sv["y"], wts["w_branch"], wts["w_o"], wts["b_gate"])
    g["b_gate"] = dbg[0:1]
    g["w_branch"] = jnp.concatenate(
        [_matmul_tn("grad_w_branch", sv["y"], dr, a_cols=(i * W, W), b_cols=(i * d, d), out_dtype=BF16) for i in range(N_BRANCH)],
        axis=0)
    token = send(("w_o", "w_branch"), g)
    dp, dmk, dmv, dcw, dnw, dlb = _mixer_bwd(sv["p"], dy, dp, sv["st"], sv["opre"], sv["mk"], sv["mv"], lb,
                                              wts["conv_w"], wts["hg_norm_w"], bl=bl, seq=seq, deps=(token,))
    g["conv_w"], g["hg_norm_w"], g["lb"] = dcw[0:CONV_K], dnw[0:1], dlb[0:1]
    g["w_mem_k"] = _matmul_tn("grad_w_mem_k", mem2, dmk, out_dtype=BF16)
    g["w_mem_v"] = _matmul_tn("grad_w_mem_v", mem2, dmv, out_dtype=BF16)
    g["w_in"] = _matmul_tn("grad_w_in", sv["x0b"], dp, out_dtype=BF16, tt=2048)
    token = send(("w_in", "w_mem_k", "w_mem_v", "conv_w"), g)
    dx0 = _matmul("proj_in_dx", dp, wts["w_in"], mode="nt", epi_fn=lambda acc, dz: (acc + alpha * dz,), epi_extra=(dz1,),
                  tm=512, tk=dp.shape[1], deps=(token,))
    return (dx0 if below is None else _ln_bwd(dx0, *below)), g


N_CHIPS = 4
MESH_IDS = pl.DeviceIdType.MESH


def _axis_slice(ref, axis, start, size):
    idx = [slice(None)] * len(ref.shape)
    idx[axis] = pl.ds(start, size)
    return ref.at[tuple(idx)]


def _chip_exchange(name, items):
    n = len(items)
    out_shapes, meta = [], []
    for arr, kind, axis in items:
        shp = list(arr.shape)
        if kind == "gather":
            per = shp[axis]
            shp[axis] = per * N_CHIPS
            out_shapes.append(jax.ShapeDtypeStruct(tuple(shp), arr.dtype))
        elif kind == "scatter":
            per = shp[axis] // N_CHIPS
            shp[axis] = per
            out_shapes.append(jax.ShapeDtypeStruct((N_CHIPS, *shp), arr.dtype))
        else:
            per = None
            out_shapes.append(jax.ShapeDtypeStruct((N_CHIPS, *shp), arr.dtype))
        meta.append((kind, axis, per))

    def body(*refs):
        ins, outs = refs[:n], refs[n:2 * n]
        send_sems, recv_sems, local_sems = refs[2 * n:]
        x, y, c = lax.axis_index("x"), lax.axis_index("y"), lax.axis_index("c")
        me = 2 * x + y
        peers = [(1 - x, y), (x, 1 - y), (1 - x, 1 - y)]

        def src_for(t, chip):
            kind, axis, per = meta[t]
            return _axis_slice(ins[t], axis, chip * per, per) if kind == "scatter" else ins[t]

        def dst_from(t, chip):
            kind, axis, per = meta[t]
            return _axis_slice(outs[t], axis, chip * per, per) if kind == "gather" else outs[t].at[chip]

        def remote(t, k):
            px, py = peers[k]
            return pltpu.make_async_remote_copy(
                src_ref=src_for(t, 2 * px + py), dst_ref=dst_from(t, me), send_sem=send_sems.at[t * 3 + k],
                recv_sem=recv_sems.at[t * 3 + k], device_id=(px, py, c), device_id_type=MESH_IDS)

        def arrival(t, k):
            px, py = peers[k]
            return pltpu.make_async_remote_copy(
                src_ref=src_for(t, me), dst_ref=dst_from(t, 2 * px + py), send_sem=send_sems.at[t * 3 + k],
                recv_sem=recv_sems.at[t * 3 + k], device_id=(px, py, c), device_id_type=MESH_IDS)

        sends = [remote(t, k) for t in range(n) for k in range(3)]
        for cp in sends:
            cp.start()
        own = [pltpu.make_async_copy(src_for(t, me), dst_from(t, me), local_sems.at[t]) for t in range(n)]
        for cp in own:
            cp.start()
        for t in range(n):
            for k in range(3):
                arrival(t, k).wait_recv()
        for cp in sends:
            cp.wait_send()
        for cp in own:
            cp.wait()

    any_spec = pl.BlockSpec(memory_space=pl.ANY)
    return pl.pallas_call(
        body,
        name=name,
        in_specs=[any_spec] * n,
        out_specs=[any_spec] * n,
        out_shape=out_shapes,
        scratch_shapes=[pltpu.SemaphoreType.DMA((3 * n,)), pltpu.SemaphoreType.DMA((3 * n,)), pltpu.SemaphoreType.DMA((n,))],
        compiler_params=pltpu.CompilerParams(has_side_effects=True),
    )(*[a for a, _, _ in items])


HBM_SPEC = pl.BlockSpec(memory_space=pltpu.HBM)
SEM_SPEC = pl.BlockSpec(memory_space=pltpu.SEMAPHORE)
N_PEERS = N_CHIPS - 1


def _my_chip():
    return (2 * lax.axis_index("x") + lax.axis_index("y")).astype(jnp.int32).reshape(1)


def _own_block_spec(r, c, axis, tr):
    if axis == 1:
        return pl.BlockSpec((tr, c), lambda i, me: (i, me[0]))
    return pl.BlockSpec((tr, c), lambda i, me: (me[0] * (r // tr) + i, 0))


def _place_shard(name, shard, axis, me):
    r, c = shard.shape
    tr = _row_block(r, c, shard.dtype.itemsize)
    shp = (r, c * N_CHIPS) if axis == 1 else (r * N_CHIPS, c)

    def body(me_ref, s_ref, o_ref):
        del me_ref
        o_ref[...] = s_ref[...]

    return pl.pallas_call(
        body, name=name,
        grid_spec=pltpu.PrefetchScalarGridSpec(
            num_scalar_prefetch=1, grid=(r // tr,),
            in_specs=[pl.BlockSpec((tr, c), lambda i, me: (i, 0))], out_specs=_own_block_spec(r, c, axis, tr)),
        out_shape=jax.ShapeDtypeStruct(shp, shard.dtype),
        compiler_params=_cparams(("parallel",)),
    )(me, shard)


class _Split:
    def __init__(self, name, items):
        self.name, self.n = name, len(items)
        self.srcs = [a for a, _, _ in items]
        self.meta, self.land_shapes = [], []
        for arr, kind, axis in items:
            shp = list(arr.shape)
            if kind == "gather":
                per = shp[axis]
                shp[axis] = per * N_CHIPS
                self.land_shapes.append(jax.ShapeDtypeStruct(tuple(shp), arr.dtype))
            else:
                per = shp[axis] // N_CHIPS
                shp[axis] = per
                self.land_shapes.append(jax.ShapeDtypeStruct((N_PEERS, *shp), arr.dtype))
            self.meta.append((kind, axis, per))

    def _src(self, ins, t, chip):
        kind, axis, per = self.meta[t]
        return _axis_slice(ins[t], axis, chip * per, per) if kind == "scatter" else ins[t]

    def _dst(self, lands, t, chip, slot):
        kind, axis, per = self.meta[t]
        return _axis_slice(lands[t], axis, chip * per, per) if kind == "gather" else lands[t].at[slot]

    def landing_zones(self, me):
        return [_place_shard(self.name + "_own", src, axis, me) if kind == "gather" else lax.empty(ls.shape, ls.dtype)
                for src, ls, (kind, axis, _) in zip(self.srcs, self.land_shapes, self.meta)]

    def _copies(self, ins, lands, send_sems, recv_sems, arrivals):
        x, y, c = lax.axis_index("x"), lax.axis_index("y"), lax.axis_index("c")
        me = 2 * x + y
        peers = [(1 - x, y), (x, 1 - y), (1 - x, 1 - y)]
        res = []
        for t in range(self.n):
            for k, (px, py) in enumerate(peers):
                theirs = 2 * px + py
                sems = dict(send_sem=send_sems.at[t * N_PEERS + k], recv_sem=recv_sems.at[t * N_PEERS + k],
                            device_id=(px, py, c), device_id_type=MESH_IDS)
                if arrivals:
                    res.append(pltpu.make_async_remote_copy(src_ref=self._src(ins, t, me), dst_ref=self._dst(lands, t, theirs, k), **sems))
                else:
                    res.append(pltpu.make_async_remote_copy(src_ref=self._src(ins, t, theirs), dst_ref=self._dst(lands, t, me, k), **sems))
        return res

    def start(self, lands, deps=()):
        n, nd = self.n, len(deps)

        def body(*refs):
            ins, lnd = refs[:n], refs[n:2 * n]
            send_sems, recv_sems = refs[2 * n + nd], refs[2 * n + nd + 1]
            token = refs[-1]
            for cp in self._copies(ins, lnd, send_sems, recv_sems, arrivals=False):
                cp.start()
            token[...] = jnp.zeros_like(token)

        hbm = lambda a: pltpu.HBM(a.shape, a.dtype)
        res = pl.pallas_call(
            body, name=self.name + "_start",
            in_specs=[HBM_SPEC] * (2 * n) + [ANY_SPEC] * nd,
            out_specs=[SEM_SPEC, SEM_SPEC] + [HBM_SPEC] * (2 * n) + [pl.BlockSpec(memory_space=pltpu.VMEM)],
            out_shape=[pltpu.SemaphoreType.DMA((N_PEERS * n,)), pltpu.SemaphoreType.DMA((N_PEERS * n,))]
            + [hbm(a) for a in self.srcs] + [hbm(a) for a in self.land_shapes] + [jax.ShapeDtypeStruct((8, 128), F32)],
            input_output_aliases={i: 2 + i for i in range(2 * n)},
            compiler_params=pltpu.CompilerParams(has_side_effects=pltpu.SideEffectType.DATAFLOW_SIDE_EFFECTING),
        )(*[pltpu.with_memory_space_constraint(a, pltpu.HBM) for a in self.srcs],
          *[pltpu.with_memory_space_constraint(a, pltpu.HBM) for a in lands], *deps)
        return res[:-1], res[-1]

    def wait(self, state, after):
        n = self.n
        after = tuple(after) if isinstance(after, (tuple, list)) else (after,)
        send_sems, recv_sems = state[0], state[1]
        srcs, lands = state[2:2 + n], state[2 + n:2 + 2 * n]

        def body(*refs):
            ins, lnd = refs[:n], refs[n:2 * n]
            s_sems, r_sems = refs[2 * n], refs[2 * n + 1]
            for cp in self._copies(ins, lnd, s_sems, r_sems, arrivals=True):
                cp.wait_recv()
            for cp in self._copies(ins, lnd, s_sems, r_sems, arrivals=False):
                cp.wait_send()

        hbm = lambda a: pltpu.HBM(a.shape, a.dtype)
        res = pl.pallas_call(
            body, name=self.name + "_wait",
            in_specs=[HBM_SPEC] * (2 * n) + [SEM_SPEC, SEM_SPEC] + [ANY_SPEC] * len(after),
            out_specs=[HBM_SPEC] * (2 * n),
            out_shape=[hbm(a) for a in self.srcs] + [hbm(a) for a in self.land_shapes],
            input_output_aliases={i: i for i in range(2 * n)},
            compiler_params=pltpu.CompilerParams(has_side_effects=pltpu.SideEffectType.DATAFLOW_SIDE_EFFECTING),
        )(*srcs, *lands, send_sems, recv_sems, *after)
        return res[:n], res[n:]


class _SiblingSplit:
    def __init__(self, name, arrays):
        self.name, self.n, self.arrays = name, len(arrays), list(arrays)

    def _copies(self, ins, lands, send_sems, recv_sems):
        sibling = (lax.axis_index("x"), lax.axis_index("y"), 1 - lax.axis_index("c"))
        return [pltpu.make_async_remote_copy(src_ref=ins[t], dst_ref=lands[t], send_sem=send_sems.at[t], recv_sem=recv_sems.at[t],
                                             device_id=sibling, device_id_type=MESH_IDS) for t in range(self.n)]

    def start(self, deps=()):
        n, nd = self.n, len(deps)

        def body(*refs):
            for cp in self._copies(refs[:n], refs[n:2 * n], refs[2 * n + nd], refs[2 * n + nd + 1]):
                cp.start()
            refs[-1][...] = jnp.zeros_like(refs[-1])

        hbm = [pltpu.HBM(a.shape, a.dtype) for a in self.arrays]
        res = pl.pallas_call(
            body, name=self.name + "_start",
            in_specs=[HBM_SPEC] * (2 * n) + [ANY_SPEC] * nd,
            out_specs=[SEM_SPEC, SEM_SPEC] + [HBM_SPEC] * (2 * n) + [pl.BlockSpec(memory_space=pltpu.VMEM)],
            out_shape=[pltpu.SemaphoreType.DMA((n,)), pltpu.SemaphoreType.DMA((n,))] + hbm + hbm + [jax.ShapeDtypeStruct((8, 128), F32)],
            input_output_aliases={i: 2 + i for i in range(2 * n)},
            compiler_params=pltpu.CompilerParams(has_side_effects=pltpu.SideEffectType.DATAFLOW_SIDE_EFFECTING),
        )(*[pltpu.with_memory_space_constraint(a, pltpu.HBM) for a in self.arrays],
          *[pltpu.with_memory_space_constraint(lax.empty(a.shape, a.dtype), pltpu.HBM) for a in self.arrays], *deps)
        return res[:-1], res[-1]

    def wait(self, state, after):
        n = self.n
        after = tuple(after) if isinstance(after, (tuple, list)) else (after,)

        def body(*refs):
            for cp in self._copies(refs[:n], refs[n:2 * n], refs[2 * n], refs[2 * n + 1]):
                cp.wait()

        hbm = [pltpu.HBM(a.shape, a.dtype) for a in self.arrays]
        res = pl.pallas_call(
            body, name=self.name + "_wait",
            in_specs=[HBM_SPEC] * (2 * n) + [SEM_SPEC, SEM_SPEC] + [ANY_SPEC] * len(after),
            out_specs=[HBM_SPEC] * (2 * n),
            out_shape=hbm + hbm,
            input_output_aliases={i: i for i in range(2 * n)},
            compiler_params=pltpu.CompilerParams(has_side_effects=pltpu.SideEffectType.DATAFLOW_SIDE_EFFECTING),
        )(*state[2:2 + 2 * n], state[0], state[1], *after)
        return res[:n], res[n:]


def _sibling_swap(name, arrays):
    n = len(arrays)

    def body(*refs):
        ins, outs = refs[:n], refs[n:2 * n]
        send_sems, recv_sems = refs[2 * n:]
        sibling = (lax.axis_index("x"), lax.axis_index("y"), 1 - lax.axis_index("c"))
        copies = [pltpu.make_async_remote_copy(src_ref=ins[t], dst_ref=outs[t], send_sem=send_sems.at[t], recv_sem=recv_sems.at[t],
                                               device_id=sibling, device_id_type=MESH_IDS) for t in range(n)]
        for cp in copies:
            cp.start()
        for cp in copies:
            cp.wait()

    any_spec = pl.BlockSpec(memory_space=pl.ANY)
    return pl.pallas_call(
        body,
        name=name,
        in_specs=[any_spec] * n,
        out_specs=[any_spec] * n,
        out_shape=[jax.ShapeDtypeStruct(a.shape, a.dtype) for a in arrays],
        scratch_shapes=[pltpu.SemaphoreType.DMA((n,)), pltpu.SemaphoreType.DMA((n,))],
        compiler_params=pltpu.CompilerParams(has_side_effects=True),
    )(*arrays)


def _row_block(r, c, itemsize=4, target=1 << 20):
    if r % 8 != 0:
        return r
    best = 8
    for tr in range(8, r + 1, 8):
        if r % tr == 0 and tr * c * itemsize <= target:
            best = tr
    return best


def _sum_chips_into(parts, stacked, layer):
    _, r, c = parts.shape
    tr = _row_block(r, c)

    def body(p_ref, s_ref, o_ref):
        del s_ref
        o_ref[...] = ((p_ref[0] + p_ref[1]) + p_ref[2]) + p_ref[3]

    return pl.pallas_call(
        body,
        name="sum_chips",
        grid=(r // tr,),
        in_specs=[pl.BlockSpec((N_CHIPS, tr, c), lambda i: (0, i, 0)), pl.BlockSpec(memory_space=pl.ANY)],
        out_specs=pl.BlockSpec((None, tr, c), lambda i: (layer, i, 0)),
        out_shape=jax.ShapeDtypeStruct(stacked.shape, stacked.dtype),
        input_output_aliases={1: 0},
        compiler_params=_cparams(("parallel",)),
    )(parts, stacked)


def _sum_own_and_peers(me, g, axis, landed):
    _, r, c = landed.shape
    tr = _row_block(r, c)

    def body(me_ref, g_ref, p_ref, o_ref):
        del me_ref
        o_ref[...] = ((g_ref[...].astype(F32) + p_ref[0].astype(F32)) + p_ref[1].astype(F32)) + p_ref[2].astype(F32)

    return pl.pallas_call(
        body, name="sum_chips_own",
        grid_spec=pltpu.PrefetchScalarGridSpec(
            num_scalar_prefetch=1, grid=(r // tr,),
            in_specs=[_own_block_spec(r, c, axis, tr), pl.BlockSpec((N_PEERS, tr, c), lambda i, me: (0, i, 0))],
            out_specs=pl.BlockSpec((tr, c), lambda i, me: (i, 0))),
        out_shape=jax.ShapeDtypeStruct((r, c), F32),
        compiler_params=_cparams(("parallel",)),
    )(me, g, landed)


def _adamw_math(w, m, v, g):
    m_new = ADAM_B1 * m + (1.0 - ADAM_B1) * g
    v_new = ADAM_B2 * v + (1.0 - ADAM_B2) * jnp.square(g)
    m_hat = m_new / (1.0 - ADAM_B1 ** ADAM_STEP)
    v_hat = v_new / (1.0 - ADAM_B2 ** ADAM_STEP)
    return -ADAM_LR * (m_hat / (jnp.sqrt(v_hat) + ADAM_EPS) + ADAM_WD * w), m_new, v_new


def _adamw(w, m, v, g_a, g_b):
    L, r, c = w.shape
    tr = _row_block(r, c, target=1 << 19)

    def body(w_ref, m_ref, v_ref, ga_ref, gb_ref, g_ref, d_ref, nm_ref, nv_ref):
        g = ga_ref[...] + gb_ref[...]
        g_ref[...] = g
        d_ref[...], nm_ref[...], nv_ref[...] = _adamw_math(w_ref[...], m_ref[...], v_ref[...], g)

    spec = pl.BlockSpec((None, tr, c), lambda l, i: (l, i, 0))
    return pl.pallas_call(
        body,
        name="adamw",
        grid=(L, r // tr),
        in_specs=[spec] * 5,
        out_specs=[spec] * 4,
        out_shape=[jax.ShapeDtypeStruct(w.shape, F32)] * 4,
        compiler_params=_cparams(("parallel", "parallel")),
    )(w, m, v, g_a, g_b)


def _adamw_layer(w, m, v, g_a, g_b, layer, outs):
    L, r, c = w.shape
    tr = _row_block(r, c, target=1 << 19)
    n_prev = 0 if outs is None else 4

    def body(w_ref, m_ref, v_ref, ga_ref, gb_ref, *rest):
        g_ref, d_ref, nm_ref, nv_ref = rest[n_prev:]
        g = ga_ref[...] + gb_ref[...]
        g_ref[...] = g
        d_ref[...], nm_ref[...], nv_ref[...] = _adamw_math(w_ref[...], m_ref[...], v_ref[...], g)

    at_layer = pl.BlockSpec((None, tr, c), lambda i: (layer, i, 0))
    flat = pl.BlockSpec((tr, c), lambda i: (i, 0))
    return pl.pallas_call(
        body,
        name="adamw_layer",
        grid=(r // tr,),
        in_specs=[at_layer] * 3 + [flat] * 2 + [ANY_SPEC] * n_prev,
        out_specs=[at_layer] * 4,
        out_shape=[jax.ShapeDtypeStruct(w.shape, F32)] * 4,
        input_output_aliases={5 + k: k for k in range(n_prev)},
        compiler_params=_cparams(("parallel",)),
    )(w, m, v, g_a, g_b, *(outs or ()))


SHARDED = (("w_in", 1), ("conv_w", 1), ("w_mem_k", 0), ("w_mem_v", 0), ("w_branch", 1), ("w_o", 0), ("w_up", 1), ("w_down", 0))
SMALL = ("lower_bounds", "hg_norm_w", "b_gate", "ln1_g", "ln1_b", "ln2_g", "ln2_b")
WEIGHT_ORDER = ("lower_bounds", "w_in", "conv_w", "hg_norm_w", "w_mem_k", "w_mem_v", "w_branch", "b_gate", "w_o", "ln1_g", "ln1_b",
                "w_up", "w_down", "ln2_g", "ln2_b")


def kernel(x, mem, lower_bounds, w_in, conv_w, hg_norm_w, w_mem_k, w_mem_v, w_branch, b_gate, w_o, ln1_g, ln1_b, w_up, w_down, ln2_g, ln2_b, loss_target, m_lower_bounds, m_w_in, m_conv_w, m_hg_norm_w, m_w_mem_k, m_w_mem_v, m_w_branch, m_b_gate, m_w_o, m_ln1_g, m_ln1_b, m_w_up, m_w_down, m_ln2_g, m_ln2_b, v_lower_bounds, v_w_in, v_conv_w, v_hg_norm_w, v_w_mem_k, v_w_mem_v, v_w_branch, v_b_gate, v_w_o, v_ln1_g, v_ln1_b, v_w_up, v_w_down, v_ln2_g, v_ln2_b):
    bl, seq, d = x.shape
    depth = w_in.shape[0]
    weights = dict(lower_bounds=lower_bounds, w_in=w_in, conv_w=conv_w, hg_norm_w=hg_norm_w, w_mem_k=w_mem_k, w_mem_v=w_mem_v,
                   w_branch=w_branch, b_gate=b_gate, w_o=w_o, ln1_g=ln1_g, ln1_b=ln1_b, w_up=w_up, w_down=w_down, ln2_g=ln2_g, ln2_b=ln2_b)
    mom_m = dict(lower_bounds=m_lower_bounds, w_in=m_w_in, conv_w=m_conv_w, hg_norm_w=m_hg_norm_w, w_mem_k=m_w_mem_k, w_mem_v=m_w_mem_v,
                 w_branch=m_w_branch, b_gate=m_b_gate, w_o=m_w_o, ln1_g=m_ln1_g, ln1_b=m_ln1_b, w_up=m_w_up, w_down=m_w_down,
                 ln2_g=m_ln2_g, ln2_b=m_ln2_b)
    mom_v = dict(lower_bounds=v_lower_bounds, w_in=v_w_in, conv_w=v_conv_w, hg_norm_w=v_hg_norm_w, w_mem_k=v_w_mem_k, w_mem_v=v_w_mem_v,
                 w_branch=v_w_branch, b_gate=v_b_gate, w_o=v_w_o, ln1_g=v_ln1_g, ln1_b=v_ln1_b, w_up=v_w_up, w_down=v_w_down,
                 ln2_g=v_ln2_g, ln2_b=v_ln2_b)

    def shard2d(name, l):
        w = weights[name][l]
        if name == "w_branch":
            return w.reshape(N_BRANCH * W, w.shape[-1]).astype(BF16)
        return w if name == "conv_w" else w.astype(BF16)

    me = _my_chip()

    shard_axis = dict(SHARDED)

    def prepare_exchange(name, kind, items):
        ex = _Split(name, [(arr, kind, shard_axis[nm]) for nm, arr in items])
        return ex, ex.landing_zones(me), [nm for nm, _ in items]

    def launch(prepared, deps=()):
        ex, lands, names = prepared
        state, token = ex.start(lands, deps)
        return ex, state, names, token

    def start_exchange(name, kind, items, deps=()):
        return launch(prepare_exchange(name, kind, items), deps)

    def prepare_gathers(l):
        groups = (("in", ("w_in",)), ("mix", ("conv_w", "w_mem_k", "w_mem_v")), ("rest", ("w_branch", "w_o", "w_up", "w_down")))
        return tuple(prepare_exchange(f"gather_{tag}_l{l}", "gather", [(nm, shard2d(nm, l)) for nm in names]) for tag, names in groups)

    def start_gathers(prepared, deps=()):
        started = []
        for prep in prepared:
            started.append(launch(prep, deps))
            deps = (started[-1][3],)
        return tuple(started)

    def gathered(pend, after):
        ex, state, names, _ = pend
        return dict(zip(names, ex.wait(state, after=after)[1]))

    pending = start_gathers(prepare_gathers(0))
    tokens = tuple(pend[3] for pend in pending)
    tokens, x, mem, loss_target, weights, mom_m, mom_v = lax.optimization_barrier((tokens, x, mem, loss_target, weights, mom_m, mom_v))
    pending = tuple((*pend[:3], tok) for pend, tok in zip(pending, tokens))
    lower_bounds = weights["lower_bounds"]

    x2d, mem2, t2d = x.reshape(bl * seq, d), mem.reshape(-1, d), loss_target.reshape(bl * seq, d)
    alpha = (2.0 * depth) ** 0.25
    soft, lb_all = _lower_bounds_fwd(lower_bounds)

    prepared = [None] + [prepare_gathers(l) for l in range(1, depth)]
    early = [x2d.astype(BF16), lb_all] + [z for prep in prepared[1:] for _, lands, _ in prep for z in lands]

    h, hb, saved, layer_wts = x2d, early[0], [], []
    for l in range(depth):
        first, mix, rest = pending
        w_in_l = gathered(first, early if l == 0 else h)["w_in"]

        def mix_fn(after, l=l, mix=mix):
            return dict(gathered(mix, after), hg_norm_w=weights["hg_norm_w"][l][None, :])

        def late_fn(after, l=l, rest=rest):
            wts = gathered(rest, after)
            for name in ("b_gate", "ln1_g", "ln1_b", "ln2_g", "ln2_b"):
                wts[name] = weights[name][l][None, :]
            return wts

        deps = (rest[3],)
        if l + 1 < depth:
            pending = start_gathers(prepared[l + 1], (w_in_l, rest[3]))
            deps += tuple(pend[3] for pend in pending)
        h, hb, sv, wts = _layer_fwd(h, hb, mem2, lb_all[l:l + 1], w_in_l, mix_fn, late_fn, bl=bl, seq=seq, alpha=alpha, deps=deps)
        saved.append(sv)
        layer_wts.append(wts)
    loss, dh = _loss_head(h, t2d)

    shape3 = {name: (depth, weights[name].size // (depth * weights[name].shape[-1]), weights[name].shape[-1]) for name, _ in SHARDED}
    partial = [dict() for _ in range(depth)]
    smalls = [None] * depth
    outs = {name: None for name, _ in SHARDED}

    def finish_reduce(pend, l, after):
        ex, state, names, _ = pend
        sent, got = ex.wait(state, after=after)
        for nm, g_full, landed in zip(names, sent, got):
            partial[l][nm] = _sum_own_and_peers(me, g_full, shard_axis[nm], landed)

    names_sharded = [name for name, _ in SHARDED]

    def start_swap(l):
        swap = _SiblingSplit(f"swap_partials_l{l}", [partial[l][nm] for nm in names_sharded])
        state, token = swap.start()
        return swap, state, token

    def optimizer_step(l, pend, after):
        swap, state, _ = pend
        mine, theirs = swap.wait(state, after)
        for nm, own, other in zip(names_sharded, mine, theirs):
            outs[nm] = _adamw_layer(weights[nm].reshape(shape3[nm]), mom_m[nm].reshape(shape3[nm]), mom_v[nm].reshape(shape3[nm]),
                                    own, other, l, outs[nm])
        return tuple(outs[nm][0] for nm in names_sharded)

    pending_mix, pending_swap, deps = [], None, ()
    dz2, dz2b, dg2, db2 = _ln_bwd(dh, saved[-1]["xhat2"], saved[-1]["rstd2"], layer_wts[-1]["ln2_g"])
    for l in reversed(range(depth)):
        dz1, dz1b, g_mlp = _mlp_bwd(dz2, dz2b, saved[l], layer_wts[l], alpha=alpha, deps=deps)
        g_mlp["ln2_g"], g_mlp["ln2_b"] = dg2[0:1], db2[0:1]
        pending_mlp = start_exchange(f"reduce_mlp_l{l}", "scatter", [(nm, g_mlp[nm]) for nm in ("w_up", "w_down")])
        deps = (pending_mlp[3],)
        if pending_mix:
            for pend in pending_mix:
                finish_reduce(pend, l + 1, dz1)
            pending_swap = start_swap(l + 1)
            deps += (pending_swap[2],)
        pending_mix = []

        def send(names, g, l=l, pending_mix=pending_mix):
            pend = start_exchange(f"reduce_{names[0]}_l{l}", "scatter", [(nm, g[nm]) for nm in names])
            pending_mix.append(pend)
            return pend[3]

        below = (saved[l - 1]["xhat2"], saved[l - 1]["rstd2"], layer_wts[l - 1]["ln2_g"]) if l > 0 else None
        out, g = _mix_bwd(dz1, dz1b, saved[l], mem2, lb_all[l:l + 1], layer_wts[l], bl=bl, seq=seq, alpha=alpha, send=send,
                          below=below, deps=deps)
        if l > 0:
            dz2, dz2b, dg2, db2 = out
        else:
            dh = out
        finish_reduce(pending_mlp, l, out[0] if l > 0 else out)
        deps = ()
        if pending_swap is not None:
            deps = optimizer_step(l + 1, pending_swap, out[0] if l > 0 else out)
            pending_swap = None
        g.update(g_mlp, lower_bounds=g["lb"])
        smalls[l] = jnp.concatenate([g[nm] for nm in SMALL], axis=1)
    small_parts = _chip_exchange("reduce_small", [(jnp.stack(smalls), "bcast", 0)])[0]
    small_sum = _sum_chips_into(small_parts.reshape(N_CHIPS, depth, -1), jnp.zeros((1, depth, small_parts.shape[-1]), F32), 0)
    small_sum = small_sum.reshape(depth, 1, -1)
    small_theirs = _sibling_swap("swap_small", [small_sum])[0]
    for pend in pending_mix:
        finish_reduce(pend, 0, small_theirs)
    optimizer_step(0, start_swap(0), small_theirs)

    outs = {name: [r.reshape(weights[name].shape) for r in res] for name, res in outs.items()}
    off = 0
    for name in SMALL:
        n = weights[name].shape[1]
        mine, other = small_sum[:, :, off:off + n], small_theirs[:, :, off:off + n]
        off += n
        if name == "lower_bounds":
            mine = _lower_bounds_bwd(soft, mine[:, 0, :])[:, None, :]
            other = _lower_bounds_bwd(soft, other[:, 0, :])[:, None, :]
        shp = (depth, 1, n)
        res = _adamw(weights[name].reshape(shp), mom_m[name].reshape(shp), mom_v[name].reshape(shp), mine, other)
        outs[name] = [r.reshape(weights[name].shape) for r in res]
    assert off == small_sum.shape[-1]

    total_loss = lax.psum(loss[0, 0], ("x", "y", "c"))
    result = [total_loss, dh.reshape(bl, seq, d)]
    for k in range(4):
        result += [outs[name][k] for name in WEIGHT_ORDER]
    return tuple(result)
```

```python
import functools

import jax
import jax.numpy as jnp
from jax import lax
from jax.experimental import pallas as pl
from jax.experimental.pallas import tpu as pltpu

F32 = jnp.float32
BF16 = jnp.bfloat16

HG_HEADS = 4
HG_F = 128
HG_CHUNK = 32
MEM_HEADS = 4
MEM_HEAD_DIM = 128
BRANCH_WIDTH = 512
N_BRANCH = 3
CONV_K = 3
LN_EPS = 1e-5
RMS_EPS = 1e-6
ADAM_LR = 0.001
ADAM_B1 = 0.9
ADAM_B2 = 0.999
ADAM_EPS = 1e-08
ADAM_WD = 0.01
ADAM_STEP = 10

VMEM_LIMIT = 48 * 1024 * 1024


def _cparams(sem):
    return pltpu.CompilerParams(dimension_semantics=sem, vmem_limit_bytes=VMEM_LIMIT)


def _dot(a, b, dims):
    return lax.dot_general(a, b, (dims, ((), ())), preferred_element_type=F32)


NN = ((1,), (0,))
NT = ((1,), (1,))
TN = ((0,), (0,))


def _pick(n, pref):
    for t in pref:
        if n % t == 0:
            return t
    return n


ANY_SPEC = pl.BlockSpec(memory_space=pl.ANY)


def _matmul(name, a, b, *, mode, out_dtype=F32, a_fn=None, a_extra=(), epi_fn=None, epi_extra=(), n_out=1, out_kinds=None,
            tm=512, tn=1024, tk=1024, deps=()):
    M, K = a.shape
    N = b.shape[1] if mode == "nn" else b.shape[0]
    tm, tn, tk = _pick(M, (tm, 256, 128, 8)), _pick(N, (tn, 896, 512, 256, 128)), _pick(K, (tk, 512, 256, 128))
    nk = K // tk
    n_ax, n_ex = len(a_extra), len(epi_extra)
    n_in = 2 + n_ax + n_ex + len(deps)
    out_dtypes = out_dtype if isinstance(out_dtype, (tuple, list)) else (out_dtype,) * n_out
    out_kinds = out_kinds or ("tile",) * n_out

    def body(*refs):
        a_ref, b_ref = refs[0], refs[1]
        ax_refs = refs[2:2 + n_ax]
        ex_refs = refs[2 + n_ax:2 + n_ax + n_ex]
        o_refs = refs[n_in:n_in + n_out]
        at = a_ref[...]
        at = a_fn(at, *[r[...] for r in ax_refs]) if a_fn is not None else at.astype(BF16)
        part = _dot(at, b_ref[...].astype(BF16), NN if mode == "nn" else NT)

        def finish(acc):
            outs = epi_fn(acc, *[r[...] for r in ex_refs]) if epi_fn is not None else (acc,)
            for o_ref, o, kind in zip(o_refs, outs, out_kinds):
                if kind == "rowsum":
                    @pl.when(pl.program_id(1) == 0)
                    def _(o_ref=o_ref):
                        o_ref[...] = jnp.zeros_like(o_ref)

                    o_ref[0:1, :] += o
                else:
                    o_ref[...] = o.astype(o_ref.dtype)

        if nk == 1:
            finish(part)
            return
        acc_ref = refs[-1]
        k = pl.program_id(2)

        @pl.when(k == 0)
        def _():
            acc_ref[...] = part

        @pl.when(jnp.logical_and(k > 0, k < nk - 1))
        def _():
            acc_ref[...] += part

        @pl.when(k == nk - 1)
        def _():
            finish(acc_ref[...] + part)

    b_mode = dict(pipeline_mode=pl.Buffered(1)) if (nk == 1 and N == tn) else {}
    in_specs = [pl.BlockSpec((tm, tk), lambda j, i, k: (i, k)),
                pl.BlockSpec((tk, tn), lambda j, i, k: (k, j), **b_mode) if mode == "nn"
                else pl.BlockSpec((tn, tk), lambda j, i, k: (j, k), **b_mode)]
    in_specs += [pl.BlockSpec((1, tk), lambda j, i, k: (0, k)) for _ in a_extra]
    for e in epi_extra:
        if e.shape[0] == 1:
            in_specs.append(pl.BlockSpec((1, tn), lambda j, i, k: (0, j)))
        elif e.shape[1] == 1:
            in_specs.append(pl.BlockSpec((tm, 1), lambda j, i, k: (i, 0)))
        else:
            in_specs.append(pl.BlockSpec((tm, tn), lambda j, i, k: (i, j)))
    in_specs += [ANY_SPEC] * len(deps)
    out_specs, out_shapes = [], []
    for kind, dt in zip(out_kinds, out_dtypes):
        if kind == "col":
            out_specs.append(pl.BlockSpec((tm, 1), lambda j, i, k: (i, 0)))
            out_shapes.append(jax.ShapeDtypeStruct((M, 1), dt))
        elif kind == "rowsum":
            out_specs.append(pl.BlockSpec((8, tn), lambda j, i, k: (0, j)))
            out_shapes.append(jax.ShapeDtypeStruct((8, N), dt))
        else:
            out_specs.append(pl.BlockSpec((tm, tn), lambda j, i, k: (i, j)))
            out_shapes.append(jax.ShapeDtypeStruct((M, N), dt))
    out = pl.pallas_call(
        body,
        name=name,
        grid=(N // tn, M // tm, nk),
        in_specs=in_specs,
        out_specs=out_specs,
        out_shape=out_shapes,
        scratch_shapes=[pltpu.VMEM((tm, tn), F32)] if nk > 1 else [],
        compiler_params=_cparams(("arbitrary", "arbitrary", "arbitrary")),
    )(a, b, *a_extra, *epi_extra, *deps)
    return out[0] if n_out == 1 else out


def _matmul_tn(name, a, b, *, a_fn=None, a_extra=(), a_cols=None, b_cols=None, ta=1024, tb=1024, tt=1024, out_dtype=F32, deps=()):
    T = a.shape[0]
    a0, Ka = a_cols if a_cols is not None else (0, a.shape[1])
    b0, Nb = b_cols if b_cols is not None else (0, b.shape[1])
    ta, tb, tt = _pick(Ka, (ta, 512, 256, 128)), _pick(Nb, (tb, 896, 512, 256, 128)), _pick(T, (tt, 512, 256, 128))
    assert a0 % ta == 0 and b0 % tb == 0
    a0, b0 = a0 // ta, b0 // tb
    nt = T // tt
    n_ax = len(a_extra)

    def body(*refs):
        a_ref, b_ref = refs[0], refs[1]
        ax_refs = refs[2:2 + n_ax]
        o_ref = refs[2 + n_ax + len(deps)]
        acc_ref = refs[-1]
        t = pl.program_id(2)
        at = a_ref[...]
        at = a_fn(at, *[r[...] for r in ax_refs]) if a_fn is not None else at.astype(BF16)
        part = _dot(at, b_ref[...].astype(BF16), TN)

        @pl.when(t == 0)
        def _():
            acc_ref[...] = part

        @pl.when(jnp.logical_and(t > 0, t < nt - 1))
        def _():
            acc_ref[...] += part

        @pl.when(t == nt - 1)
        def _():
            o_ref[...] = (acc_ref[...] + part if nt > 1 else part).astype(o_ref.dtype)

    in_specs = [pl.BlockSpec((tt, ta), lambda i, j, t: (t, a0 + i)), pl.BlockSpec((tt, tb), lambda i, j, t: (t, b0 + j))]
    in_specs += [pl.BlockSpec((1, ta), lambda i, j, t: (0, a0 + i)) for _ in a_extra]
    in_specs += [ANY_SPEC] * len(deps)
    return pl.pallas_call(
        body,
        name=name,
        grid=(Ka // ta, Nb // tb, nt),
        in_specs=in_specs,
        out_specs=pl.BlockSpec((ta, tb), lambda i, j, t: (i, j)),
        out_shape=jax.ShapeDtypeStruct((Ka, Nb), out_dtype),
        scratch_shapes=[pltpu.VMEM((ta, tb), F32)],
        compiler_params=_cparams(("parallel", "parallel", "arbitrary")),
    )(a, b, *a_extra, *deps)


W = BRANCH_WIDTH
C_CB, C_CC, C_CH, C_HQ, C_HF, C_HI, C_HG, C_MQ, N_MIX = 0, W, 2 * W, 3 * W, 4 * W, 5 * W, 6 * W, 7 * W, 8 * W
TS_MIX = 256
PREV_ROWS = 16
KEEP_NAMES = ("sq", "qs", "k", "sig", "f", "ea", "eb", "eq", "ek")


def _sigmoid(x):
    return jax.nn.sigmoid(x)


def _chunk_pos(shape):
    return lax.broadcasted_iota(jnp.int32, shape, 0) & (HG_CHUNK - 1)


def _seg_cumsum(x, pos):
    sh = 1
    while sh < HG_CHUNK:
        x = x + jnp.where(pos >= sh, pltpu.roll(x, sh, 0), 0.0)
        sh *= 2
    return x


def _seg_rev_cumsum(x, pos):
    n = x.shape[0]
    sh = 1
    while sh < HG_CHUNK:
        x = x + jnp.where(pos < HG_CHUNK - sh, pltpu.roll(x, n - sh, 0), 0.0)
        sh *= 2
    return x


def _chunk_mask(ts):
    r = lax.broadcasted_iota(jnp.int32, (ts, ts), 0)
    c = lax.broadcasted_iota(jnp.int32, (ts, ts), 1)
    return jnp.logical_and((r // HG_CHUNK) == (c // HG_CHUNK), c <= r)


def _hgrn_gates(p_ref, lb):
    q = p_ref[:, C_HQ:C_HQ + W].astype(F32)
    fl = p_ref[:, C_HF:C_HF + W].astype(F32)
    sig = _sigmoid(fl)
    f = lb + (1.0 - lb) * sig
    logf = jnp.log(f)
    k = (1.0 - lb) * _sigmoid(-fl)
    sq = _sigmoid(q)
    qs = q * sq
    return q, sq, qs, sig, f, logf, k


def _hgrn_decays(logf, bc_sc, ts):
    pos = _chunk_pos(logf.shape)
    bc = _seg_cumsum(logf, pos)
    bc_sc[...] = bc
    nc = ts // HG_CHUNK
    bref = jnp.concatenate(
        [jnp.broadcast_to(bc_sc[n * HG_CHUNK + HG_CHUNK // 2 - 1:n * HG_CHUNK + HG_CHUNK // 2, :], (HG_CHUNK, W)) for n in range(nc)], axis=0)
    blast = jnp.concatenate(
        [jnp.broadcast_to(bc_sc[(n + 1) * HG_CHUNK - 1:(n + 1) * HG_CHUNK, :], (HG_CHUNK, W)) for n in range(nc)], axis=0)
    return pos, bc, bref, blast


def _conv_shift_down(u, carry_ref, row):
    n = carry_ref.shape[0]
    last, before = carry_ref[n - 1:n, :], carry_ref[n - 2:n - 1, :]
    u1 = jnp.where(row == 0, last, pltpu.roll(u, 1, 0))
    u2 = jnp.where(row == 0, before, jnp.where(row == 1, last, pltpu.roll(u, 2, 0)))
    return u1, u2


def _attn_probs(qh, kh):
    s = _dot(qh, kh, NT) * (MEM_HEAD_DIM ** -0.5)
    e = jnp.exp(s - jnp.max(s, axis=-1, keepdims=True))
    return e / jnp.sum(e, axis=-1, keepdims=True)


def _mixer_fwd(p, mk, mv, lb, conv_w, norm_w, *, bl, seq):
    T = p.shape[0]
    ts = TS_MIX
    ns = seq // ts
    nc = ts // HG_CHUNK
    ml = mk.shape[0] // bl

    def body(p_ref, mk_ref, mv_ref, lb_ref, cw_ref, nw_ref, y_ref, st_ref, opre_ref, state_sc, carry_sc, bc_sc):
        @pl.when(pl.program_id(1) == 0)
        def _():
            state_sc[...] = jnp.zeros_like(state_sc)
            carry_sc[...] = jnp.zeros_like(carry_sc)

        cb, cc, ch = (p_ref[:, c0:c0 + W].astype(F32) for c0 in (C_CB, C_CC, C_CH))
        u = cc * ch
        row = lax.broadcasted_iota(jnp.int32, (ts, W), 0)
        u1, u2 = _conv_shift_down(u, carry_sc, row)
        yconv = u2 * cw_ref[0:1, :] + u1 * cw_ref[1:2, :] + u * cw_ref[2:3, :]
        y_ref[:, 0:W] = (cb * yconv).astype(BF16)
        carry_sc[...] = u[ts - 8:ts, :]

        lbv = lb_ref[...]
        _, _, qs, _, _, logf, k = _hgrn_gates(p_ref, lbv)
        pos, bc, bref, blast = _hgrn_decays(logf, bc_sc, ts)
        a_all = (qs * jnp.exp(bc - bref)).astype(BF16)
        bk_all = (k * jnp.exp(bref - bc)).astype(BF16)
        qin_all = (qs * jnp.exp(bc)).astype(BF16)
        kout_all = (k * jnp.exp(blast - bc)).astype(BF16)
        v_all = p_ref[:, C_HI:C_HI + W].astype(BF16)
        mask = _chunk_mask(ts)
        heads = [slice(h * HG_F, (h + 1) * HG_F) for h in range(HG_HEADS)]
        st = [state_sc[h] for h in range(HG_HEADS)]
        o_inter = [[] for _ in range(HG_HEADS)]
        for n in range(nc):
            rows = slice(n * HG_CHUNK, (n + 1) * HG_CHUNK)
            for h, hs in enumerate(heads):
                st_ref[n, h] = st[h]
                o_inter[h].append(_dot(qin_all[rows, hs], st[h].astype(BF16), NT))
                kv = _dot(v_all[rows, hs], kout_all[rows, hs], TN)
                decay = jnp.exp(bc_sc[(n + 1) * HG_CHUNK - 1:(n + 1) * HG_CHUNK, hs])
                st[h] = st[h] * decay + kv
        for h in range(HG_HEADS):
            state_sc[h] = st[h]
        scores = [_dot(a_all[:, hs], bk_all[:, hs], NT) for hs in heads]
        scores = [jnp.where(mask, s, 0.0).astype(BF16) for s in scores]
        outs = [_dot(scores[h], v_all[:, hs], NN) + jnp.concatenate(o_inter[h], axis=0) for h, hs in enumerate(heads)]
        for h, hs in enumerate(heads):
            o = outs[h]
            opre_ref[:, hs] = o
            on = o * lax.rsqrt(jnp.mean(o * o, axis=-1, keepdims=True) + RMS_EPS) * nw_ref[...]
            g = p_ref[:, C_HG + h * HG_F:C_HG + (h + 1) * HG_F].astype(F32)
            y_ref[:, W + h * HG_F:W + (h + 1) * HG_F] = (on * (g * _sigmoid(g))).astype(BF16)

        mheads = [slice(h * MEM_HEAD_DIM, (h + 1) * MEM_HEAD_DIM) for h in range(MEM_HEADS)]
        probs = [_attn_probs(p_ref[:, C_MQ + h * MEM_HEAD_DIM:C_MQ + (h + 1) * MEM_HEAD_DIM].astype(BF16), mk_ref[:, hs])
                 for h, hs in enumerate(mheads)]
        for h, hs in enumerate(mheads):
            y_ref[:, 2 * W + h * MEM_HEAD_DIM:2 * W + (h + 1) * MEM_HEAD_DIM] = _dot(
                probs[h].astype(BF16), mv_ref[:, hs], NN).astype(BF16)

    return pl.pallas_call(
        body,
        name="mixer_fwd",
        grid=(bl, ns),
        in_specs=[
            pl.BlockSpec((ts, N_MIX), lambda b, s: (b * ns + s, 0)),
            pl.BlockSpec((ml, W), lambda b, s: (b, 0)),
            pl.BlockSpec((ml, W), lambda b, s: (b, 0)),
            pl.BlockSpec((1, W), lambda b, s: (0, 0)),
            pl.BlockSpec((CONV_K, W), lambda b, s: (0, 0)),
            pl.BlockSpec((1, HG_F), lambda b, s: (0, 0)),
        ],
        out_specs=[
            pl.BlockSpec((ts, 3 * W), lambda b, s: (b * ns + s, 0)),
            pl.BlockSpec((nc, HG_HEADS, HG_F, HG_F), lambda b, s: (b * ns + s, 0, 0, 0)),
            pl.BlockSpec((ts, W), lambda b, s: (b * ns + s, 0)),
        ],
        out_shape=[
            jax.ShapeDtypeStruct((T, 3 * W), BF16),
            jax.ShapeDtypeStruct((T // HG_CHUNK, HG_HEADS, HG_F, HG_F), F32),
            jax.ShapeDtypeStruct((T, W), F32),
        ],
        scratch_shapes=[pltpu.VMEM((HG_HEADS, HG_F, HG_F), F32), pltpu.VMEM((8, W), F32), pltpu.VMEM((ts, W), F32)],
        compiler_params=_cparams(("arbitrary", "arbitrary")),
    )(p, mk, mv, lb, conv_w, norm_w)


def _mixer_bwd(p, dy, dp_gates, st, opre, mk, mv, lb, conv_w, norm_w, *, bl, seq, deps=()):
    T, nin = p.shape
    ts = TS_MIX
    ns = seq // ts
    nc = ts // HG_CHUNK
    ml = mk.shape[0] // bl
    mid, last = HG_CHUNK // 2 - 1, HG_CHUNK - 1

    def body(p_ref, pprev_ref, dy_ref, dpin_ref, st_ref, opre_ref, mk_ref, mv_ref, lb_ref, cw_ref, nw_ref, *rest):
        (dp_ref, dmk_ref, dmv_ref, dcw_ref, dnw_ref, dlb_ref, dstate_sc, carry_sc, uprev_sc, ab_sc, bkb_sc, qinb_sc, koutb_sc,
         dob_sc, dv_sc, da_sc, dbk_sc, dqin_sc, dkout_sc, dec_sc, ddec_sc, *keep_scs) = rest[len(deps):]
        del dpin_ref
        b, s = pl.program_id(0), pl.program_id(1)

        @pl.when(s == 0)
        def _():
            dstate_sc[...] = jnp.zeros_like(dstate_sc)
            carry_sc[...] = jnp.zeros_like(carry_sc)
            dmk_ref[...] = jnp.zeros_like(dmk_ref)
            dmv_ref[...] = jnp.zeros_like(dmv_ref)

        @pl.when(jnp.logical_and(b == 0, s == 0))
        def _():
            dcw_ref[...] = jnp.zeros_like(dcw_ref)
            dnw_ref[...] = jnp.zeros_like(dnw_ref)
            dlb_ref[...] = jnp.zeros_like(dlb_ref)

        cb, cc, ch = (p_ref[:, c0:c0 + W].astype(F32) for c0 in (C_CB, C_CC, C_CH))
        u = cc * ch
        row = lax.broadcasted_iota(jnp.int32, (ts, W), 0)
        uprev = pprev_ref[:, C_CC:C_CC + W].astype(F32) * pprev_ref[:, C_CH:C_CH + W].astype(F32)
        uprev_sc[...] = jnp.where(s == ns - 1, 0.0, uprev)
        u1, u2 = _conv_shift_down(u, uprev_sc, row)
        w0, w1, w2 = cw_ref[0:1, :], cw_ref[1:2, :], cw_ref[2:3, :]
        dya = dy_ref[:, 0:W].astype(F32)
        dp_ref[:, C_CB:C_CB + W] = (dya * (u2 * w0 + u1 * w1 + u * w2)).astype(BF16)
        dv = cb * dya
        dv1 = jnp.where(row == ts - 1, carry_sc[0:1, :], pltpu.roll(dv, ts - 1, 0))
        dv2 = jnp.where(row == ts - 1, carry_sc[1:2, :], jnp.where(row == ts - 2, carry_sc[0:1, :], pltpu.roll(dv, ts - 2, 0)))
        du = dv * w2 + dv1 * w1 + dv2 * w0
        dp_ref[:, C_CC:C_CC + W] = (du * ch).astype(BF16)
        dp_ref[:, C_CH:C_CH + W] = (du * cc).astype(BF16)
        dcw_ref[0:1, :] += jnp.sum(dv * u2, axis=0, keepdims=True)
        dcw_ref[1:2, :] += jnp.sum(dv * u1, axis=0, keepdims=True)
        dcw_ref[2:3, :] += jnp.sum(dv * u, axis=0, keepdims=True)
        carry_sc[...] = dv[0:8, :]

        mask = _chunk_mask(ts)
        pos_c = _chunk_pos((HG_CHUNK, HG_F))
        nw = nw_ref[...]

        def block(n, h):
            rows = slice(n * HG_CHUNK, (n + 1) * HG_CHUNK)
            return rows, slice(h * HG_F, (h + 1) * HG_F)

        keep = dict(zip(KEEP_NAMES, keep_scs))

        def gates(rows, h):
            lbh = lb_ref[:, h * HG_F:(h + 1) * HG_F]
            q = p_ref[rows, C_HQ + h * HG_F:C_HQ + (h + 1) * HG_F].astype(F32)
            fl = p_ref[rows, C_HF + h * HG_F:C_HF + (h + 1) * HG_F].astype(F32)
            sig = _sigmoid(fl)
            f = lbh + (1.0 - lbh) * sig
            k = (1.0 - lbh) * _sigmoid(-fl)
            sq = _sigmoid(q)
            qs = q * sq
            bc = _seg_cumsum(jnp.log(f), pos_c)
            bref = jnp.sum(jnp.where(pos_c == mid, bc, 0.0), axis=0, keepdims=True)
            blast = jnp.sum(jnp.where(pos_c == last, bc, 0.0), axis=0, keepdims=True)
            ea, eb, eq, ek = jnp.exp(bc - bref), jnp.exp(bref - bc), jnp.exp(bc), jnp.exp(blast - bc)
            return dict(sq=sq, qs=qs, k=k, sig=sig, f=f, ea=ea, eb=eb, eq=eq, ek=ek), blast

        dnw = jnp.zeros((1, HG_F), F32)
        for n in range(nc):
            for h in range(HG_HEADS):
                rows, hs = block(n, h)
                fw, blast = gates(rows, h)
                for name in KEEP_NAMES:
                    keep[name][rows, hs] = fw[name]
                ab_sc[rows, hs] = (fw["qs"] * fw["ea"]).astype(BF16)
                bkb_sc[rows, hs] = (fw["k"] * fw["eb"]).astype(BF16)
                qinb_sc[rows, hs] = (fw["qs"] * fw["eq"]).astype(BF16)
                koutb_sc[rows, hs] = (fw["k"] * fw["ek"]).astype(BF16)
                dec_sc[n:n + 1, hs] = jnp.exp(blast)
                o = opre_ref[rows, hs]
                g = p_ref[rows, C_HG + h * HG_F:C_HG + (h + 1) * HG_F].astype(F32)
                sg = _sigmoid(g)
                r = lax.rsqrt(jnp.mean(o * o, axis=-1, keepdims=True) + RMS_EPS)
                dyb = dy_ref[rows, W + h * HG_F:W + (h + 1) * HG_F].astype(F32)
                dp_ref[rows, C_HG + h * HG_F:C_HG + (h + 1) * HG_F] = (
                    dyb * (o * r * nw) * (sg * (1.0 + g * (1.0 - sg)))).astype(BF16)
                don = dyb * (g * sg)
                dnw = dnw + jnp.sum(don * o * r, axis=0, keepdims=True)
                dn = don * nw
                dob_sc[rows, hs] = (r * (dn - o * (r * r) * jnp.mean(dn * o, axis=-1, keepdims=True))).astype(BF16)
        dnw_ref[0:1, :] += dnw

        heads = [slice(h * HG_F, (h + 1) * HG_F) for h in range(HG_HEADS)]
        scores = [_dot(ab_sc[:, hs], bkb_sc[:, hs], NT) for hs in heads]
        dscores = [_dot(dob_sc[:, hs], p_ref[:, C_HI + h * HG_F:C_HI + (h + 1) * HG_F].astype(BF16), NT)
                   for h, hs in enumerate(heads)]
        scores = [jnp.where(mask, s, 0.0).astype(BF16) for s in scores]
        dscores = [jnp.where(mask, s, 0.0).astype(BF16) for s in dscores]
        for h, hs in enumerate(heads):
            dv_sc[:, hs] = _dot(scores[h], dob_sc[:, hs], TN)
            da_sc[:, hs] = _dot(dscores[h], bkb_sc[:, hs], NN)
            dbk_sc[:, hs] = _dot(dscores[h], ab_sc[:, hs], TN)
        dst = [dstate_sc[h] for h in range(HG_HEADS)]
        for n in reversed(range(nc)):
            for h in range(HG_HEADS):
                rows, hs = block(n, h)
                st_n = st_ref[n, h]
                decay = dec_sc[n:n + 1, hs]
                dstb = dst[h].astype(BF16)
                dob_n = dob_sc[rows, hs]
                dv_sc[rows, hs] += _dot(koutb_sc[rows, hs], dstb, NT)
                dkout_sc[rows, hs] = _dot(p_ref[rows, C_HI + h * HG_F:C_HI + (h + 1) * HG_F].astype(BF16), dstb, NN)
                ddec_sc[n:n + 1, hs] = jnp.sum(dst[h] * st_n, axis=0, keepdims=True) * decay
                dqin_sc[rows, hs] = _dot(dob_n, st_n.astype(BF16), NN)
                dst[h] = dst[h] * decay + _dot(dob_n, qinb_sc[rows, hs], TN)
        for h in range(HG_HEADS):
            dstate_sc[h] = dst[h]

        for h in range(HG_HEADS):
            dlb = jnp.zeros((1, HG_F), F32)
            for n in range(nc):
                rows, hs = block(n, h)
                fw = {name: keep[name][rows, hs] for name in KEEP_NAMES}
                lbh = lb_ref[:, h * HG_F:(h + 1) * HG_F]
                q = p_ref[rows, C_HQ + h * HG_F:C_HQ + (h + 1) * HG_F].astype(F32)
                da, dbk, dqin, dkout = da_sc[rows, hs], dbk_sc[rows, hs], dqin_sc[rows, hs], dkout_sc[rows, hs]
                w_a, w_b, w_q, w_k = da * fw["ea"], dbk * fw["eb"], dqin * fw["eq"], dkout * fw["ek"]
                dqs, dk = w_a + w_q, w_b + w_k
                t_a, t_b, t_q, t_k = w_a * fw["qs"], w_b * fw["k"], w_q * fw["qs"], w_k * fw["k"]
                s_ref = jnp.sum(t_b - t_a, axis=0, keepdims=True)
                s_last = jnp.sum(t_k, axis=0, keepdims=True) + ddec_sc[n:n + 1, hs]
                dbc = (t_a - t_b + t_q - t_k) + jnp.where(pos_c == mid, s_ref, 0.0) + jnp.where(pos_c == last, s_last, 0.0)
                dfk = _seg_rev_cumsum(dbc, pos_c) / fw["f"] - dk
                sig, sq = fw["sig"], fw["sq"]
                dp_ref[rows, C_HF + h * HG_F:C_HF + (h + 1) * HG_F] = (dfk * (1.0 - lbh) * sig * (1.0 - sig)).astype(BF16)
                dlb = dlb + jnp.sum(dfk * (1.0 - sig), axis=0, keepdims=True)
                dp_ref[rows, C_HQ + h * HG_F:C_HQ + (h + 1) * HG_F] = (dqs * (sq * (1.0 + q * (1.0 - sq)))).astype(BF16)
                dp_ref[rows, C_HI + h * HG_F:C_HI + (h + 1) * HG_F] = dv_sc[rows, hs].astype(BF16)
            dlb_ref[0:1, h * HG_F:(h + 1) * HG_F] += dlb

        mheads = [slice(h * MEM_HEAD_DIM, (h + 1) * MEM_HEAD_DIM) for h in range(MEM_HEADS)]
        qhs = [p_ref[:, C_MQ + h * MEM_HEAD_DIM:C_MQ + (h + 1) * MEM_HEAD_DIM].astype(BF16) for h in range(MEM_HEADS)]
        dobs = [dy_ref[:, 2 * W + h * MEM_HEAD_DIM:2 * W + (h + 1) * MEM_HEAD_DIM].astype(BF16) for h in range(MEM_HEADS)]
        probs = [_attn_probs(qhs[h], mk_ref[:, hs]) for h, hs in enumerate(mheads)]
        dprobs = [_dot(dobs[h], mv_ref[:, hs], NT) for h, hs in enumerate(mheads)]
        for h, hs in enumerate(mheads):
            prob = probs[h]
            dmv_ref[:, hs] += _dot(prob.astype(BF16), dobs[h], TN)
            ds = prob * (dprobs[h] - jnp.sum(dprobs[h] * prob, axis=-1, keepdims=True)) * (MEM_HEAD_DIM ** -0.5)
            dsb = ds.astype(BF16)
            dp_ref[:, C_MQ + h * MEM_HEAD_DIM:C_MQ + (h + 1) * MEM_HEAD_DIM] = _dot(dsb, mk_ref[:, hs], NN).astype(BF16)
            dmk_ref[:, hs] += _dot(dsb, qhs[h], TN)

    def tile(b, s):
        return b * ns + (ns - 1 - s)

    return pl.pallas_call(
        body,
        name="mixer_bwd",
        grid=(bl, ns),
        in_specs=[
            pl.BlockSpec((ts, N_MIX), lambda b, s: (tile(b, s), 0)),
            pl.BlockSpec((PREV_ROWS, N_MIX), lambda b, s: (jnp.maximum(tile(b, s) * (ts // PREV_ROWS) - 1, 0), 0)),
            pl.BlockSpec((ts, 3 * W), lambda b, s: (tile(b, s), 0)),
            pl.BlockSpec(memory_space=pl.ANY),
            pl.BlockSpec((nc, HG_HEADS, HG_F, HG_F), lambda b, s: (tile(b, s), 0, 0, 0)),
            pl.BlockSpec((ts, W), lambda b, s: (tile(b, s), 0)),
            pl.BlockSpec((ml, W), lambda b, s: (b, 0)),
            pl.BlockSpec((ml, W), lambda b, s: (b, 0)),
            pl.BlockSpec((1, W), lambda b, s: (0, 0)),
            pl.BlockSpec((CONV_K, W), lambda b, s: (0, 0)),
            pl.BlockSpec((1, HG_F), lambda b, s: (0, 0)),
        ] + [ANY_SPEC] * len(deps),
        out_specs=[
            pl.BlockSpec((ts, N_MIX), lambda b, s: (tile(b, s), 0)),
            pl.BlockSpec((ml, W), lambda b, s: (b, 0)),
            pl.BlockSpec((ml, W), lambda b, s: (b, 0)),
            pl.BlockSpec((8, W), lambda b, s: (0, 0)),
            pl.BlockSpec((8, HG_F), lambda b, s: (0, 0)),
            pl.BlockSpec((8, W), lambda b, s: (0, 0)),
        ],
        out_shape=[
            jax.ShapeDtypeStruct((T, nin), BF16),
            jax.ShapeDtypeStruct((bl * ml, W), F32),
            jax.ShapeDtypeStruct((bl * ml, W), F32),
            jax.ShapeDtypeStruct((8, W), F32),
            jax.ShapeDtypeStruct((8, HG_F), F32),
            jax.ShapeDtypeStruct((8, W), F32),
        ],
        input_output_aliases={3: 0},
        scratch_shapes=[pltpu.VMEM((HG_HEADS, HG_F, HG_F), F32), pltpu.VMEM((8, W), F32), pltpu.VMEM((PREV_ROWS, W), F32)]
        + [pltpu.VMEM((ts, W), BF16)] * 5 + [pltpu.VMEM((ts, W), F32)] * 5 + [pltpu.VMEM((nc, W), F32)] * 2
        + [pltpu.VMEM((ts, W), F32)] * len(KEEP_NAMES),
        compiler_params=_cparams(("arbitrary", "arbitrary")),
    )(p, p, dy, dp_gates, st, opre, mk, mv, lb, conv_w, norm_w, *deps)


def _layer_norm_stats(z):
    mu = jnp.mean(z, axis=-1, keepdims=True)
    zc = z - mu
    rstd = lax.rsqrt(jnp.mean(zc * zc, axis=-1, keepdims=True) + LN_EPS)
    return zc * rstd, rstd


def _gate_specs(tm, d):
    g0 = N_MIX // d
    return [pl.BlockSpec((tm, d), functools.partial(lambda i, k: (i, g0 + k), k=k)) for k in range(N_BRANCH)]


def _merge_fwd(y, p, x0, wb, wo, bg, ln_g, ln_b, *, alpha, tm=512):
    T, d = x0.shape
    assert N_MIX % d == 0
    tm = _pick(T, (tm, 128, 8))

    def body(y_ref, g0_ref, g1_ref, g2_ref, x_ref, wb_ref, wo_ref, bg_ref, lg_ref, lb_ref, mg_ref, xh_ref, rs_ref, x1b_ref):
        merged = None
        for i, g_ref in enumerate((g0_ref, g1_ref, g2_ref)):
            r = _dot(y_ref[:, i * W:(i + 1) * W], wb_ref[i * W:(i + 1) * W, :], NN)
            t = _sigmoid(g_ref[...].astype(F32) + bg_ref[:, i * d:(i + 1) * d]) * r
            merged = t if merged is None else merged + t
        mb = merged.astype(BF16)
        mg_ref[...] = mb
        z = alpha * x_ref[...] + _dot(mb, wo_ref[...], NN)
        xh, rs = _layer_norm_stats(z)
        xh_ref[...], rs_ref[...] = xh, rs
        x1b_ref[...] = (xh * lg_ref[...] + lb_ref[...]).astype(BF16)

    row = lambda i: (i, 0)
    fix = lambda i: (0, 0)
    return pl.pallas_call(
        body,
        name="merge_fwd",
        grid=(T // tm,),
        in_specs=[pl.BlockSpec((tm, 3 * W), row)] + _gate_specs(tm, d) + [
            pl.BlockSpec((tm, d), row), pl.BlockSpec((3 * W, d), fix, pipeline_mode=pl.Buffered(1)),
            pl.BlockSpec((d, d), fix, pipeline_mode=pl.Buffered(1)), pl.BlockSpec((1, 3 * d), fix),
            pl.BlockSpec((1, d), fix), pl.BlockSpec((1, d), fix)],
        out_specs=[pl.BlockSpec((tm, d), row), pl.BlockSpec((tm, d), row), pl.BlockSpec((tm, 1), row), pl.BlockSpec((tm, d), row)],
        out_shape=[jax.ShapeDtypeStruct((T, d), BF16), jax.ShapeDtypeStruct((T, d), F32), jax.ShapeDtypeStruct((T, 1), F32),
                   jax.ShapeDtypeStruct((T, d), BF16)],
        compiler_params=_cparams(("parallel",)),
    )(y, p, p, p, x0, wb, wo, bg, ln_g, ln_b)


def _merge_bwd(dz, p, y, wb, wo, bg, *, tm=512):
    T, d = dz.shape
    nin = p.shape[1]
    tm = _pick(T, (tm, 128, 8))

    def body(dz_ref, g0_ref, g1_ref, g2_ref, y_ref, wb_ref, wo_ref, bg_ref, dr_ref, dp_ref, dy_ref, dbg_ref):
        @pl.when(pl.program_id(0) == 0)
        def _():
            dbg_ref[...] = jnp.zeros_like(dbg_ref)

        dmerged = _dot(dz_ref[...].astype(BF16), wo_ref[...], NT)
        dp_ref[:, 0:N_MIX] = jnp.zeros((tm, N_MIX), BF16)
        for i, g_ref in enumerate((g0_ref, g1_ref, g2_ref)):
            cs = slice(i * d, (i + 1) * d)
            s = _sigmoid(g_ref[...].astype(F32) + bg_ref[:, cs])
            drb = (dmerged * s).astype(BF16)
            dr_ref[:, cs] = drb
            dgate = dmerged * _dot(y_ref[:, i * W:(i + 1) * W], wb_ref[i * W:(i + 1) * W, :], NN) * s * (1.0 - s)
            dp_ref[:, N_MIX + i * d:N_MIX + (i + 1) * d] = dgate.astype(BF16)
            dbg_ref[0:1, cs] += jnp.sum(dgate, axis=0, keepdims=True)
            dy_ref[:, i * W:(i + 1) * W] = _dot(drb, wb_ref[i * W:(i + 1) * W, :], NT).astype(BF16)

    row = lambda i: (i, 0)
    fix = lambda i: (0, 0)
    return pl.pallas_call(
        body,
        name="merge_bwd",
        grid=(T // tm,),
        in_specs=[pl.BlockSpec((tm, d), row)] + _gate_specs(tm, d) + [
            pl.BlockSpec((tm, 3 * W), row), pl.BlockSpec((3 * W, d), fix, pipeline_mode=pl.Buffered(1)),
            pl.BlockSpec((d, d), fix, pipeline_mode=pl.Buffered(1)), pl.BlockSpec((1, 3 * d), fix)],
        out_specs=[pl.BlockSpec((tm, 3 * d), row), pl.BlockSpec((tm, nin), row), pl.BlockSpec((tm, 3 * W), row),
                   pl.BlockSpec((8, 3 * d), fix)],
        out_shape=[jax.ShapeDtypeStruct((T, 3 * d), BF16), jax.ShapeDtypeStruct((T, nin), BF16),
                   jax.ShapeDtypeStruct((T, 3 * W), BF16), jax.ShapeDtypeStruct((8, 3 * d), F32)],
        compiler_params=_cparams(("arbitrary",)),
    )(dz, p, p, p, y, wb, wo, bg)


MLP_VMEM_LIMIT = 58 * 1024 * 1024


def _mlp_fwd(xhat1, x1b, g1, b1, wu, wd, g2, b2, *, alpha, tm=512, tf=1024):
    T, d = xhat1.shape
    ff = wu.shape[1]
    tm, tf = _pick(T, (tm, 256, 128, 8)), _pick(ff, (tf, 1024, 512, 256, 128))

    def body(xh_ref, x1b_ref, g1_ref, b1_ref, wu_ref, wd_ref, g2_ref, b2_ref, a_ref, xh2_ref, rs2_ref, x2_ref, x2b_ref):
        xb = x1b_ref[...]
        acc = None
        a = _dot(xb, wu_ref[:, 0:tf], NN)
        for c0 in range(0, ff, tf):
            a_next = _dot(xb, wu_ref[:, c0 + tf:c0 + 2 * tf], NN) if c0 + tf < ff else None
            a_ref[:, c0:c0 + tf] = a.astype(BF16)
            part = _dot(jnp.square(jnp.maximum(a, 0.0)).astype(BF16), wd_ref[c0:c0 + tf, :], NN)
            acc = part if acc is None else acc + part
            a = a_next
        x1 = xh_ref[...] * g1_ref[...] + b1_ref[...]
        xh2, rs2 = _layer_norm_stats(alpha * x1 + acc)
        xh2_ref[...] = xh2
        rs2_ref[...] = rs2
        x2 = xh2 * g2_ref[...] + b2_ref[...]
        x2_ref[...] = x2
        x2b_ref[...] = x2.astype(BF16)

    row = lambda i: (i, 0)
    fix = lambda i: (0, 0)
    once = dict(pipeline_mode=pl.Buffered(1))
    return pl.pallas_call(
        body,
        name="mlp_fwd",
        grid=(T // tm,),
        in_specs=[pl.BlockSpec((tm, d), row), pl.BlockSpec((tm, d), row), pl.BlockSpec((1, d), fix), pl.BlockSpec((1, d), fix),
                  pl.BlockSpec((d, ff), fix, **once), pl.BlockSpec((ff, d), fix, **once),
                  pl.BlockSpec((1, d), fix), pl.BlockSpec((1, d), fix)],
        out_specs=[pl.BlockSpec((tm, ff), row), pl.BlockSpec((tm, d), row), pl.BlockSpec((tm, 1), row),
                   pl.BlockSpec((tm, d), row), pl.BlockSpec((tm, d), row)],
        out_shape=[jax.ShapeDtypeStruct((T, ff), BF16), jax.ShapeDtypeStruct((T, d), F32), jax.ShapeDtypeStruct((T, 1), F32),
                   jax.ShapeDtypeStruct((T, d), F32), jax.ShapeDtypeStruct((T, d), BF16)],
        compiler_params=pltpu.CompilerParams(dimension_semantics=("parallel",), vmem_limit_bytes=MLP_VMEM_LIMIT),
    )(xhat1, x1b, g1, b1, wu, wd, g2, b2)


def _ln_bwd(dy, xhat, rstd, g, *, tm=512, deps=()):
    T, d = dy.shape
    tm = _pick(T, (tm, 256, 128, 8))

    def body(dy_ref, xh_ref, rs_ref, g_ref, *rest):
        dz_ref, dzb_ref, dg_ref, db_ref = rest[len(deps):]

        @pl.when(pl.program_id(0) == 0)
        def _():
            dg_ref[...] = jnp.zeros_like(dg_ref)
            db_ref[...] = jnp.zeros_like(db_ref)

        dy_, xh = dy_ref[...], xh_ref[...]
        dg_ref[0:1, :] += jnp.sum(dy_ * xh, axis=0, keepdims=True)
        db_ref[0:1, :] += jnp.sum(dy_, axis=0, keepdims=True)
        dxh = dy_ * g_ref[...]
        dz = rs_ref[...] * (dxh - jnp.mean(dxh, axis=-1, keepdims=True) - xh * jnp.mean(dxh * xh, axis=-1, keepdims=True))
        dz_ref[...] = dz
        dzb_ref[...] = dz.astype(BF16)

    row = lambda i: (i, 0)
    fix = lambda i: (0, 0)
    return pl.pallas_call(
        body,
        name="ln_bwd",
        grid=(T // tm,),
        in_specs=[pl.BlockSpec((tm, d), row), pl.BlockSpec((tm, d), row), pl.BlockSpec((tm, 1), row), pl.BlockSpec((1, d), fix)]
        + [ANY_SPEC] * len(deps),
        out_specs=[pl.BlockSpec((tm, d), row), pl.BlockSpec((tm, d), row), pl.BlockSpec((8, d), fix), pl.BlockSpec((8, d), fix)],
        out_shape=[jax.ShapeDtypeStruct((T, d), F32), jax.ShapeDtypeStruct((T, d), BF16), jax.ShapeDtypeStruct((8, d), F32),
                   jax.ShapeDtypeStruct((8, d), F32)],
        compiler_params=_cparams(("arbitrary",)),
    )(dy, xhat, rstd, g, *deps)


def _loss_head(y, target, *, tm=512):
    T, d = y.shape
    tm = _pick(T, (tm, 256, 128, 8))
    n = T // tm

    def body(y_ref, t_ref, loss_ref, dy_ref, acc_ref):
        i = pl.program_id(0)

        @pl.when(i == 0)
        def _():
            acc_ref[...] = jnp.zeros_like(acc_ref)

        e = y_ref[...] - t_ref[...]
        dy_ref[...] = e * (1.0 / d)
        acc_ref[...] += jnp.sum(e * e, axis=0, keepdims=True)

        @pl.when(i == n - 1)
        def _():
            loss_ref[...] = (0.5 / d) * jnp.sum(acc_ref[...], axis=1, keepdims=True)

    row = lambda i: (i, 0)
    return pl.pallas_call(
        body,
        name="loss_head",
        grid=(n,),
        in_specs=[pl.BlockSpec((tm, d), row), pl.BlockSpec((tm, d), row)],
        out_specs=[pl.BlockSpec((1, 1), lambda i: (0, 0)), pl.BlockSpec((tm, d), row)],
        out_shape=[jax.ShapeDtypeStruct((1, 1), F32), jax.ShapeDtypeStruct((T, d), F32)],
        scratch_shapes=[pltpu.VMEM((1, d), F32)],
        compiler_params=_cparams(("arbitrary",)),
    )(y, target)


def _lower_bounds_fwd(lower_bounds):
    depth, n = lower_bounds.shape

    def body(x_ref, soft_ref, lb_ref):
        x = x_ref[...]
        e = jnp.exp(x - jnp.max(x, axis=0, keepdims=True))
        soft_ref[...] = e / jnp.sum(e, axis=0, keepdims=True)
        run = None
        for l in range(depth):
            run = soft_ref[l:l + 1, :] if run is None else run + soft_ref[l:l + 1, :]
            lb_ref[l:l + 1, :] = run - soft_ref[0:1, :]

    return pl.pallas_call(body, name="lower_bounds_fwd",
                          out_shape=[jax.ShapeDtypeStruct((depth, n), F32), jax.ShapeDtypeStruct((depth, n), F32)])(lower_bounds)


def _lower_bounds_bwd(soft, dlb):
    depth, n = soft.shape

    def body(soft_ref, dlb_ref, out_ref, dsoft_ref):
        total = jnp.sum(dlb_ref[...], axis=0, keepdims=True)
        run = None
        for l in reversed(range(depth)):
            run = dlb_ref[l:l + 1, :] if run is None else run + dlb_ref[l:l + 1, :]
            dsoft_ref[l:l + 1, :] = run - total if l == 0 else run
        s, ds = soft_ref[...], dsoft_ref[...]
        out_ref[...] = s * (ds - jnp.sum(s * ds, axis=0, keepdims=True))

    return pl.pallas_call(body, name="lower_bounds_bwd", out_shape=jax.ShapeDtypeStruct((depth, n), F32),
                          scratch_shapes=[pltpu.VMEM((depth, n), F32)])(soft, dlb)


def _layer_fwd(x0, x0b, mem2, lb, w_in, mix_fn, late_fn, *, bl, seq, alpha, deps=()):
    p = _matmul("proj_in", x0b, w_in, mode="nn", out_dtype=BF16, deps=deps, tm=1024, tn=1792)
    wts = dict(mix_fn(p), w_in=w_in)
    mk = _matmul("mem_k", mem2, wts["w_mem_k"], mode="nn", out_dtype=BF16)
    mv = _matmul("mem_v", mem2, wts["w_mem_v"], mode="nn", out_dtype=BF16)
    y, st, opre = _mixer_fwd(p, mk, mv, lb, wts["conv_w"], wts["hg_norm_w"], bl=bl, seq=seq)
    wts.update(late_fn(y))
    merged, xhat1, rstd1, x1b = _merge_fwd(y, p, x0, wts["w_branch"], wts["w_o"], wts["b_gate"], wts["ln1_g"], wts["ln1_b"],
                                           alpha=alpha)
    a, xhat2, rstd2, x2, x2b = _mlp_fwd(xhat1, x1b, wts["ln1_g"], wts["ln1_b"], wts["w_up"], wts["w_down"], wts["ln2_g"],
                                        wts["ln2_b"], alpha=alpha)
    saved = dict(x0b=x0b, p=p, mk=mk, mv=mv, y=y, st=st, opre=opre, merged=merged, xhat1=xhat1, rstd1=rstd1, x1b=x1b, a=a,
                 xhat2=xhat2, rstd2=rstd2)
    return x2, x2b, saved, wts


def _relu2_bf16(a):
    return jnp.square(jnp.maximum(a.astype(F32), 0.0)).astype(BF16)


def _mlp_bwd(dz2, dz2b, sv, wts, *, alpha, deps=()):
    g = {}
    da = _matmul("mlp_da", dz2b, wts["w_down"], mode="nt", out_dtype=BF16, tm=512, tn=wts["w_down"].shape[0], deps=deps,
                 epi_fn=lambda acc, a: (acc * (2.0 * jnp.maximum(a.astype(F32), 0.0)),), epi_extra=(sv["a"],))
    g["w_down"] = _matmul_tn("grad_w_down", sv["a"], dz2b, a_fn=_relu2_bf16, out_dtype=BF16, tt=2048)
    g["w_up"] = _matmul_tn("grad_w_up", sv["x1b"], da, out_dtype=BF16, tt=2048)
    dx1 = _matmul("mlp_dx", da, wts["w_up"], mode="nt", epi_fn=lambda acc, dz: (acc + alpha * dz,), epi_extra=(dz2,),
                  tm=512, tk=4096)
    dz1, dz1b, dg1, db1 = _ln_bwd(dx1, sv["xhat1"], sv["rstd1"], wts["ln1_g"])
    g["ln1_g"], g["ln1_b"] = dg1[0:1], db1[0:1]
    return dz1, dz1b, g


def _mix_bwd(dz1, dz1b, sv, mem2, lb, wts, *, bl, seq, alpha, send, below=None, deps=()):
    d = dz1.shape[1]
    g = {}
    g["w_o"] = _matmul_tn("grad_w_o", sv["merged"], dz1b, out_dtype=BF16, tt=2048, deps=deps)
    dr, dp, dy, dbg = _merge_bwd(dz1b, sv["p"], sv["y"], wts["w_branch"], wts["w_o"], wts["b_gate"])
    g["b_gate"] = dbg[0:1]
    g["w_branch"] = jnp.concatenate(
        [_matmul_tn("grad_w_branch", sv["y"], dr, a_cols=(i * W, W), b_cols=(i * d, d), out_dtype=BF16) for i in range(N_BRANCH)],
        axis=0)
    token = send(("w_o", "w_branch"), g)
    dp, dmk, dmv, dcw, dnw, dlb = _mixer_bwd(sv["p"], dy, dp, sv["st"], sv["opre"], sv["mk"], sv["mv"], lb,
                                              wts["conv_w"], wts["hg_norm_w"], bl=bl, seq=seq, deps=(token,))
    g["conv_w"], g["hg_norm_w"], g["lb"] = dcw[0:CONV_K], dnw[0:1], dlb[0:1]
    g["w_mem_k"] = _matmul_tn("grad_w_mem_k", mem2, dmk, out_dtype=BF16)
    g["w_mem_v"] = _matmul_tn("grad_w_mem_v", mem2, dmv, out_dtype=BF16)
    g["w_in"] = _matmul_tn("grad_w_in", sv["x0b"], dp, out_dtype=BF16, tt=2048)
    token = send(("w_in", "w_mem_k", "w_mem_v", "conv_w"), g)
    dx0 = _matmul("proj_in_dx", dp, wts["w_in"], mode="nt", epi_fn=lambda acc, dz: (acc + alpha * dz,), epi_extra=(dz1,),
                  tm=512, tk=dp.shape[1], deps=(token,))
    return (dx0 if below is None else _ln_bwd(dx0, *below)), g


N_CHIPS = 4
MESH_IDS = pl.DeviceIdType.MESH


def _axis_slice(ref, axis, start, size):
    idx = [slice(None)] * len(ref.shape)
    idx[axis] = pl.ds(start, size)
    return ref.at[tuple(idx)]


def _chip_exchange(name, items):
    n = len(items)
    out_shapes, meta = [], []
    for arr, kind, axis in items:
        shp = list(arr.shape)
        if kind == "gather":
            per = shp[axis]
            shp[axis] = per * N_CHIPS
            out_shapes.append(jax.ShapeDtypeStruct(tuple(shp), arr.dtype))
        elif kind == "scatter":
            per = shp[axis] // N_CHIPS
            shp[axis] = per
            out_shapes.append(jax.ShapeDtypeStruct((N_CHIPS, *shp), arr.dtype))
        else:
            per = None
            out_shapes.append(jax.ShapeDtypeStruct((N_CHIPS, *shp), arr.dtype))
        meta.append((kind, axis, per))

    def body(*refs):
        ins, outs = refs[:n], refs[n:2 * n]
        send_sems, recv_sems, local_sems = refs[2 * n:]
        x, y, c = lax.axis_index("x"), lax.axis_index("y"), lax.axis_index("c")
        me = 2 * x + y
        peers = [(1 - x, y), (x, 1 - y), (1 - x, 1 - y)]

        def src_for(t, chip):
            kind, axis, per = meta[t]
            return _axis_slice(ins[t], axis, chip * per, per) if kind == "scatter" else ins[t]

        def dst_from(t, chip):
            kind, axis, per = meta[t]
            return _axis_slice(outs[t], axis, chip * per, per) if kind == "gather" else outs[t].at[chip]

        def remote(t, k):
            px, py = peers[k]
            return pltpu.make_async_remote_copy(
                src_ref=src_for(t, 2 * px + py), dst_ref=dst_from(t, me), send_sem=send_sems.at[t * 3 + k],
                recv_sem=recv_sems.at[t * 3 + k], device_id=(px, py, c), device_id_type=MESH_IDS)

        def arrival(t, k):
            px, py = peers[k]
            return pltpu.make_async_remote_copy(
                src_ref=src_for(t, me), dst_ref=dst_from(t, 2 * px + py), send_sem=send_sems.at[t * 3 + k],
                recv_sem=recv_sems.at[t * 3 + k], device_id=(px, py, c), device_id_type=MESH_IDS)

        sends = [remote(t, k) for t in range(n) for k in range(3)]
        for cp in sends:
            cp.start()
        own = [pltpu.make_async_copy(src_for(t, me), dst_from(t, me), local_sems.at[t]) for t in range(n)]
        for cp in own:
            cp.start()
        for t in range(n):
            for k in range(3):
                arrival(t, k).wait_recv()
        for cp in sends:
            cp.wait_send()
        for cp in own:
            cp.wait()

    any_spec = pl.BlockSpec(memory_space=pl.ANY)
    return pl.pallas_call(
        body,
        name=name,
        in_specs=[any_spec] * n,
        out_specs=[any_spec] * n,
        out_shape=out_shapes,
        scratch_shapes=[pltpu.SemaphoreType.DMA((3 * n,)), pltpu.SemaphoreType.DMA((3 * n,)), pltpu.SemaphoreType.DMA((n,))],
        compiler_params=pltpu.CompilerParams(has_side_effects=True),
    )(*[a for a, _, _ in items])


HBM_SPEC = pl.BlockSpec(memory_space=pltpu.HBM)
SEM_SPEC = pl.BlockSpec(memory_space=pltpu.SEMAPHORE)
N_PEERS = N_CHIPS - 1


def _my_chip():
    return (2 * lax.axis_index("x") + lax.axis_index("y")).astype(jnp.int32).reshape(1)


def _own_block_spec(r, c, axis, tr):
    if axis == 1:
        return pl.BlockSpec((tr, c), lambda i, me: (i, me[0]))
    return pl.BlockSpec((tr, c), lambda i, me: (me[0] * (r // tr) + i, 0))


def _place_shard(name, shard, axis, me):
    r, c = shard.shape
    tr = _row_block(r, c, shard.dtype.itemsize)
    shp = (r, c * N_CHIPS) if axis == 1 else (r * N_CHIPS, c)

    def body(me_ref, s_ref, o_ref):
        del me_ref
        o_ref[...] = s_ref[...]

    return pl.pallas_call(
        body, name=name,
        grid_spec=pltpu.PrefetchScalarGridSpec(
            num_scalar_prefetch=1, grid=(r // tr,),
            in_specs=[pl.BlockSpec((tr, c), lambda i, me: (i, 0))], out_specs=_own_block_spec(r, c, axis, tr)),
        out_shape=jax.ShapeDtypeStruct(shp, shard.dtype),
        compiler_params=_cparams(("parallel",)),
    )(me, shard)


class _Split:
    def __init__(self, name, items):
        self.name, self.n = name, len(items)
        self.srcs = [a for a, _, _ in items]
        self.meta, self.land_shapes = [], []
        for arr, kind, axis in items:
            shp = list(arr.shape)
            if kind == "gather":
                per = shp[axis]
                shp[axis] = per * N_CHIPS
                self.land_shapes.append(jax.ShapeDtypeStruct(tuple(shp), arr.dtype))
            else:
                per = shp[axis] // N_CHIPS
                shp[axis] = per
                self.land_shapes.append(jax.ShapeDtypeStruct((N_PEERS, *shp), arr.dtype))
            self.meta.append((kind, axis, per))

    def _src(self, ins, t, chip):
        kind, axis, per = self.meta[t]
        return _axis_slice(ins[t], axis, chip * per, per) if kind == "scatter" else ins[t]

    def _dst(self, lands, t, chip, slot):
        kind, axis, per = self.meta[t]
        return _axis_slice(lands[t], axis, chip * per, per) if kind == "gather" else lands[t].at[slot]

    def landing_zones(self, me):
        return [_place_shard(self.name + "_own", src, axis, me) if kind == "gather" else lax.empty(ls.shape, ls.dtype)
                for src, ls, (kind, axis, _) in zip(self.srcs, self.land_shapes, self.meta)]

    def _copies(self, ins, lands, send_sems, recv_sems, arrivals):
        x, y, c = lax.axis_index("x"), lax.axis_index("y"), lax.axis_index("c")
        me = 2 * x + y
        peers = [(1 - x, y), (x, 1 - y), (1 - x, 1 - y)]
        res = []
        for t in range(self.n):
            for k, (px, py) in enumerate(peers):
                theirs = 2 * px + py
                sems = dict(send_sem=send_sems.at[t * N_PEERS + k], recv_sem=recv_sems.at[t * N_PEERS + k],
                            device_id=(px, py, c), device_id_type=MESH_IDS)
                if arrivals:
                    res.append(pltpu.make_async_remote_copy(src_ref=self._src(ins, t, me), dst_ref=self._dst(lands, t, theirs, k), **sems))
                else:
                    res.append(pltpu.make_async_remote_copy(src_ref=self._src(ins, t, theirs), dst_ref=self._dst(lands, t, me, k), **sems))
        return res

    def start(self, lands, deps=()):
        n, nd = self.n, len(deps)

        def body(*refs):
            ins, lnd = refs[:n], refs[n:2 * n]
            send_sems, recv_sems = refs[2 * n + nd], refs[2 * n + nd + 1]
            token = refs[-1]
            for cp in self._copies(ins, lnd, send_sems, recv_sems, arrivals=False):
                cp.start()
            token[...] = jnp.zeros_like(token)

        hbm = lambda a: pltpu.HBM(a.shape, a.dtype)
        res = pl.pallas_call(
            body, name=self.name + "_start",
            in_specs=[HBM_SPEC] * (2 * n) + [ANY_SPEC] * nd,
            out_specs=[SEM_SPEC, SEM_SPEC] + [HBM_SPEC] * (2 * n) + [pl.BlockSpec(memory_space=pltpu.VMEM)],
            out_shape=[pltpu.SemaphoreType.DMA((N_PEERS * n,)), pltpu.SemaphoreType.DMA((N_PEERS * n,))]
            + [hbm(a) for a in self.srcs] + [hbm(a) for a in self.land_shapes] + [jax.ShapeDtypeStruct((8, 128), F32)],
            input_output_aliases={i: 2 + i for i in range(2 * n)},
            compiler_params=pltpu.CompilerParams(has_side_effects=pltpu.SideEffectType.DATAFLOW_SIDE_EFFECTING),
        )(*[pltpu.with_memory_space_constraint(a, pltpu.HBM) for a in self.srcs],
          *[pltpu.with_memory_space_constraint(a, pltpu.HBM) for a in lands], *deps)
        return res[:-1], res[-1]

    def wait(self, state, after):
        n = self.n
        after = tuple(after) if isinstance(after, (tuple, list)) else (after,)
        send_sems, recv_sems = state[0], state[1]
        srcs, lands = state[2:2 + n], state[2 + n:2 + 2 * n]

        def body(*refs):
            ins, lnd = refs[:n], refs[n:2 * n]
            s_sems, r_sems = refs[2 * n], refs[2 * n + 1]
            for cp in self._copies(ins, lnd, s_sems, r_sems, arrivals=True):
                cp.wait_recv()
            for cp in self._copies(ins, lnd, s_sems, r_sems, arrivals=False):
                cp.wait_send()

        hbm = lambda a: pltpu.HBM(a.shape, a.dtype)
        res = pl.pallas_call(
            body, name=self.name + "_wait",
            in_specs=[HBM_SPEC] * (2 * n) + [SEM_SPEC, SEM_SPEC] + [ANY_SPEC] * len(after),
            out_specs=[HBM_SPEC] * (2 * n),
            out_shape=[hbm(a) for a in self.srcs] + [hbm(a) for a in self.land_shapes],
            input_output_aliases={i: i for i in range(2 * n)},
            compiler_params=pltpu.CompilerParams(has_side_effects=pltpu.SideEffectType.DATAFLOW_SIDE_EFFECTING),
        )(*srcs, *lands, send_sems, recv_sems, *after)
        return res[:n], res[n:]


class _SiblingSplit:
    def __init__(self, name, arrays):
        self.name, self.n, self.arrays = name, len(arrays), list(arrays)

    def _copies(self, ins, lands, send_sems, recv_sems):
        sibling = (lax.axis_index("x"), lax.axis_index("y"), 1 - lax.axis_index("c"))
        return [pltpu.make_async_remote_copy(src_ref=ins[t], dst_ref=lands[t], send_sem=send_sems.at[t], recv_sem=recv_sems.at[t],
                                             device_id=sibling, device_id_type=MESH_IDS) for t in range(self.n)]

    def start(self, deps=()):
        n, nd = self.n, len(deps)

        def body(*refs):
            for cp in self._copies(refs[:n], refs[n:2 * n], refs[2 * n + nd], refs[2 * n + nd + 1]):
                cp.start()
            refs[-1][...] = jnp.zeros_like(refs[-1])

        hbm = [pltpu.HBM(a.shape, a.dtype) for a in self.arrays]
        res = pl.pallas_call(
            body, name=self.name + "_start",
            in_specs=[HBM_SPEC] * (2 * n) + [ANY_SPEC] * nd,
            out_specs=[SEM_SPEC, SEM_SPEC] + [HBM_SPEC] * (2 * n) + [pl.BlockSpec(memory_space=pltpu.VMEM)],
            out_shape=[pltpu.SemaphoreType.DMA((n,)), pltpu.SemaphoreType.DMA((n,))] + hbm + hbm + [jax.ShapeDtypeStruct((8, 128), F32)],
            input_output_aliases={i: 2 + i for i in range(2 * n)},
            compiler_params=pltpu.CompilerParams(has_side_effects=pltpu.SideEffectType.DATAFLOW_SIDE_EFFECTING),
        )(*[pltpu.with_memory_space_constraint(a, pltpu.HBM) for a in self.arrays],
          *[pltpu.with_memory_space_constraint(lax.empty(a.shape, a.dtype), pltpu.HBM) for a in self.arrays], *deps)
        return res[:-1], res[-1]

    def wait(self, state, after):
        n = self.n
        after = tuple(after) if isinstance(after, (tuple, list)) else (after,)

        def body(*refs):
            for cp in self._copies(refs[:n], refs[n:2 * n], refs[2 * n], refs[2 * n + 1]):
                cp.wait()

        hbm = [pltpu.HBM(a.shape, a.dtype) for a in self.arrays]
        res = pl.pallas_call(
            body, name=self.name + "_wait",
            in_specs=[HBM_SPEC] * (2 * n) + [SEM_SPEC, SEM_SPEC] + [ANY_SPEC] * len(after),
            out_specs=[HBM_SPEC] * (2 * n),
            out_shape=hbm + hbm,
            input_output_aliases={i: i for i in range(2 * n)},
            compiler_params=pltpu.CompilerParams(has_side_effects=pltpu.SideEffectType.DATAFLOW_SIDE_EFFECTING),
        )(*state[2:2 + 2 * n], state[0], state[1], *after)
        return res[:n], res[n:]


def _sibling_swap(name, arrays):
    n = len(arrays)

    def body(*refs):
        ins, outs = refs[:n], refs[n:2 * n]
        send_sems, recv_sems = refs[2 * n:]
        sibling = (lax.axis_index("x"), lax.axis_index("y"), 1 - lax.axis_index("c"))
        copies = [pltpu.make_async_remote_copy(src_ref=ins[t], dst_ref=outs[t], send_sem=send_sems.at[t], recv_sem=recv_sems.at[t],
                                               device_id=sibling, device_id_type=MESH_IDS) for t in range(n)]
        for cp in copies:
            cp.start()
        for cp in copies:
            cp.wait()

    any_spec = pl.BlockSpec(memory_space=pl.ANY)
    return pl.pallas_call(
        body,
        name=name,
        in_specs=[any_spec] * n,
        out_specs=[any_spec] * n,
        out_shape=[jax.ShapeDtypeStruct(a.shape, a.dtype) for a in arrays],
        scratch_shapes=[pltpu.SemaphoreType.DMA((n,)), pltpu.SemaphoreType.DMA((n,))],
        compiler_params=pltpu.CompilerParams(has_side_effects=True),
    )(*arrays)


def _row_block(r, c, itemsize=4, target=1 << 20):
    if r % 8 != 0:
        return r
    best = 8
    for tr in range(8, r + 1, 8):
        if r % tr == 0 and tr * c * itemsize <= target:
            best = tr
    return best


def _sum_chips_into(parts, stacked, layer):
    _, r, c = parts.shape
    tr = _row_block(r, c)

    def body(p_ref, s_ref, o_ref):
        del s_ref
        o_ref[...] = ((p_ref[0] + p_ref[1]) + p_ref[2]) + p_ref[3]

    return pl.pallas_call(
        body,
        name="sum_chips",
        grid=(r // tr,),
        in_specs=[pl.BlockSpec((N_CHIPS, tr, c), lambda i: (0, i, 0)), pl.BlockSpec(memory_space=pl.ANY)],
        out_specs=pl.BlockSpec((None, tr, c), lambda i: (layer, i, 0)),
        out_shape=jax.ShapeDtypeStruct(stacked.shape, stacked.dtype),
        input_output_aliases={1: 0},
        compiler_params=_cparams(("parallel",)),
    )(parts, stacked)


def _sum_own_and_peers(me, g, axis, landed):
    _, r, c = landed.shape
    tr = _row_block(r, c)

    def body(me_ref, g_ref, p_ref, o_ref):
        del me_ref
        o_ref[...] = ((g_ref[...].astype(F32) + p_ref[0].astype(F32)) + p_ref[1].astype(F32)) + p_ref[2].astype(F32)

    return pl.pallas_call(
        body, name="sum_chips_own",
        grid_spec=pltpu.PrefetchScalarGridSpec(
            num_scalar_prefetch=1, grid=(r // tr,),
            in_specs=[_own_block_spec(r, c, axis, tr), pl.BlockSpec((N_PEERS, tr, c), lambda i, me: (0, i, 0))],
            out_specs=pl.BlockSpec((tr, c), lambda i, me: (i, 0))),
        out_shape=jax.ShapeDtypeStruct((r, c), F32),
        compiler_params=_cparams(("parallel",)),
    )(me, g, landed)


def _adamw_math(w, m, v, g):
    m_new = ADAM_B1 * m + (1.0 - ADAM_B1) * g
    v_new = ADAM_B2 * v + (1.0 - ADAM_B2) * jnp.square(g)
    m_hat = m_new / (1.0 - ADAM_B1 ** ADAM_STEP)
    v_hat = v_new / (1.0 - ADAM_B2 ** ADAM_STEP)
    return -ADAM_LR * (m_hat / (jnp.sqrt(v_hat) + ADAM_EPS) + ADAM_WD * w), m_new, v_new


def _adamw(w, m, v, g_a, g_b):
    L, r, c = w.shape
    tr = _row_block(r, c, target=1 << 19)

    def body(w_ref, m_ref, v_ref, ga_ref, gb_ref, g_ref, d_ref, nm_ref, nv_ref):
        g = ga_ref[...] + gb_ref[...]
        g_ref[...] = g
        d_ref[...], nm_ref[...], nv_ref[...] = _adamw_math(w_ref[...], m_ref[...], v_ref[...], g)

    spec = pl.BlockSpec((None, tr, c), lambda l, i: (l, i, 0))
    return pl.pallas_call(
        body,
        name="adamw",
        grid=(L, r // tr),
        in_specs=[spec] * 5,
        out_specs=[spec] * 4,
        out_shape=[jax.ShapeDtypeStruct(w.shape, F32)] * 4,
        compiler_params=_cparams(("parallel", "parallel")),
    )(w, m, v, g_a, g_b)


def _adamw_layer(w, m, v, g_a, g_b, layer, outs):
    L, r, c = w.shape
    tr = _row_block(r, c, target=1 << 19)
    n_prev = 0 if outs is None else 4

    def body(w_ref, m_ref, v_ref, ga_ref, gb_ref, *rest):
        g_ref, d_ref, nm_ref, nv_ref = rest[n_prev:]
        g = ga_ref[...] + gb_ref[...]
        g_ref[...] = g
        d_ref[...], nm_ref[...], nv_ref[...] = _adamw_math(w_ref[...], m_ref[...], v_ref[...], g)

    at_layer = pl.BlockSpec((None, tr, c), lambda i: (layer, i, 0))
    flat = pl.BlockSpec((tr, c), lambda i: (i, 0))
    return pl.pallas_call(
        body,
        name="adamw_layer",
        grid=(r // tr,),
        in_specs=[at_layer] * 3 + [flat] * 2 + [ANY_SPEC] * n_prev,
        out_specs=[at_layer] * 4,
        out_shape=[jax.ShapeDtypeStruct(w.shape, F32)] * 4,
        input_output_aliases={5 + k: k for k in range(n_prev)},
        compiler_params=_cparams(("parallel",)),
    )(w, m, v, g_a, g_b, *(outs or ()))


SHARDED = (("w_in", 1), ("conv_w", 1), ("w_mem_k", 0), ("w_mem_v", 0), ("w_branch", 1), ("w_o", 0), ("w_up", 1), ("w_down", 0))
SMALL = ("lower_bounds", "hg_norm_w", "b_gate", "ln1_g", "ln1_b", "ln2_g", "ln2_b")
WEIGHT_ORDER = ("lower_bounds", "w_in", "conv_w", "hg_norm_w", "w_mem_k", "w_mem_v", "w_branch", "b_gate", "w_o", "ln1_g", "ln1_b",
                "w_up", "w_down", "ln2_g", "ln2_b")


def kernel(x, mem, lower_bounds, w_in, conv_w, hg_norm_w, w_mem_k, w_mem_v, w_branch, b_gate, w_o, ln1_g, ln1_b, w_up, w_down, ln2_g, ln2_b, loss_target, m_lower_bounds, m_w_in, m_conv_w, m_hg_norm_w, m_w_mem_k, m_w_mem_v, m_w_branch, m_b_gate, m_w_o, m_ln1_g, m_ln1_b, m_w_up, m_w_down, m_ln2_g, m_ln2_b, v_lower_bounds, v_w_in, v_conv_w, v_hg_norm_w, v_w_mem_k, v_w_mem_v, v_w_branch, v_b_gate, v_w_o, v_ln1_g, v_ln1_b, v_w_up, v_w_down, v_ln2_g, v_ln2_b):
    bl, seq, d = x.shape
    depth = w_in.shape[0]
    weights = dict(lower_bounds=lower_bounds, w_in=w_in, conv_w=conv_w, hg_norm_w=hg_norm_w, w_mem_k=w_mem_k, w_mem_v=w_mem_v,
                   w_branch=w_branch, b_gate=b_gate, w_o=w_o, ln1_g=ln1_g, ln1_b=ln1_b, w_up=w_up, w_down=w_down, ln2_g=ln2_g, ln2_b=ln2_b)
    mom_m = dict(lower_bounds=m_lower_bounds, w_in=m_w_in, conv_w=m_conv_w, hg_norm_w=m_hg_norm_w, w_mem_k=m_w_mem_k, w_mem_v=m_w_mem_v,
                 w_branch=m_w_branch, b_gate=m_b_gate, w_o=m_w_o, ln1_g=m_ln1_g, ln1_b=m_ln1_b, w_up=m_w_up, w_down=m_w_down,
                 ln2_g=m_ln2_g, ln2_b=m_ln2_b)
    mom_v = dict(lower_bounds=v_lower_bounds, w_in=v_w_in, conv_w=v_conv_w, hg_norm_w=v_hg_norm_w, w_mem_k=v_w_mem_k, w_mem_v=v_w_mem_v,
                 w_branch=v_w_branch, b_gate=v_b_gate, w_o=v_w_o, ln1_g=v_ln1_g, ln1_b=v_ln1_b, w_up=v_w_up, w_down=v_w_down,
                 ln2_g=v_ln2_g, ln2_b=v_ln2_b)

    def shard2d(name, l):
        w = weights[name][l]
        if name == "w_branch":
            return w.reshape(N_BRANCH * W, w.shape[-1]).astype(BF16)
        return w if name == "conv_w" else w.astype(BF16)

    me = _my_chip()

    shard_axis = dict(SHARDED)

    def prepare_exchange(name, kind, items):
        ex = _Split(name, [(arr, kind, shard_axis[nm]) for nm, arr in items])
        return ex, ex.landing_zones(me), [nm for nm, _ in items]

    def launch(prepared, deps=()):
        ex, lands, names = prepared
        state, token = ex.start(lands, deps)
        return ex, state, names, token

    def start_exchange(name, kind, items, deps=()):
        return launch(prepare_exchange(name, kind, items), deps)

    def prepare_gathers(l):
        groups = (("in", ("w_in",)), ("mix", ("conv_w", "w_mem_k", "w_mem_v")), ("rest", ("w_branch", "w_o", "w_up", "w_down")))
        return tuple(prepare_exchange(f"gather_{tag}_l{l}", "gather", [(nm, shard2d(nm, l)) for nm in names]) for tag, names in groups)

    def start_gathers(prepared, deps=()):
        started = []
        for prep in prepared:
            started.append(launch(prep, deps))
            deps = (started[-1][3],)
        return tuple(started)

    def gathered(pend, after):
        ex, state, names, _ = pend
        return dict(zip(names, ex.wait(state, after=after)[1]))

    pending = start_gathers(prepare_gathers(0))
    tokens = tuple(pend[3] for pend in pending)
    tokens, x, mem, loss_target, weights, mom_m, mom_v = lax.optimization_barrier((tokens, x, mem, loss_target, weights, mom_m, mom_v))
    pending = tuple((*pend[:3], tok) for pend, tok in zip(pending, tokens))
    lower_bounds = weights["lower_bounds"]

    x2d, mem2, t2d = x.reshape(bl * seq, d), mem.reshape(-1, d), loss_target.reshape(bl * seq, d)
    alpha = (2.0 * depth) ** 0.25
    soft, lb_all = _lower_bounds_fwd(lower_bounds)

    prepared = [None] + [prepare_gathers(l) for l in range(1, depth)]
    early = [x2d.astype(BF16), lb_all] + [z for prep in prepared[1:] for _, lands, _ in prep for z in lands]

    h, hb, saved, layer_wts = x2d, early[0], [], []
    for l in range(depth):
        first, mix, rest = pending
        w_in_l = gathered(first, early if l == 0 else h)["w_in"]

        def mix_fn(after, l=l, mix=mix):
            return dict(gathered(mix, after), hg_norm_w=weights["hg_norm_w"][l][None, :])

        def late_fn(after, l=l, rest=rest):
            wts = gathered(rest, after)
            for name in ("b_gate", "ln1_g", "ln1_b", "ln2_g", "ln2_b"):
                wts[name] = weights[name][l][None, :]
            return wts

        deps = (rest[3],)
        if l + 1 < depth:
            pending = start_gathers(prepared[l + 1], (w_in_l, rest[3]))
            deps += tuple(pend[3] for pend in pending)
        h, hb, sv, wts = _layer_fwd(h, hb, mem2, lb_all[l:l + 1], w_in_l, mix_fn, late_fn, bl=bl, seq=seq, alpha=alpha, deps=deps)
        saved.append(sv)
        layer_wts.append(wts)
    loss, dh = _loss_head(h, t2d)

    shape3 = {name: (depth, weights[name].size // (depth * weights[name].shape[-1]), weights[name].shape[-1]) for name, _ in SHARDED}
    partial = [dict() for _ in range(depth)]
    smalls = [None] * depth
    outs = {name: None for name, _ in SHARDED}

    def finish_reduce(pend, l, after):
        ex, state, names, _ = pend
        sent, got = ex.wait(state, after=after)
        for nm, g_full, landed in zip(names, sent, got):
            partial[l][nm] = _sum_own_and_peers(me, g_full, shard_axis[nm], landed)

    names_sharded = [name for name, _ in SHARDED]

    def start_swap(l):
        swap = _SiblingSplit(f"swap_partials_l{l}", [partial[l][nm] for nm in names_sharded])
        state, token = swap.start()
        return swap, state, token

    def optimizer_step(l, pend, after):
        swap, state, _ = pend
        mine, theirs = swap.wait(state, after)
        for nm, own, other in zip(names_sharded, mine, theirs):
            outs[nm] = _adamw_layer(weights[nm].reshape(shape3[nm]), mom_m[nm].reshape(shape3[nm]), mom_v[nm].reshape(shape3[nm]),
                                    own, other, l, outs[nm])
        return tuple(outs[nm][0] for nm in names_sharded)

    pending_mix, pending_swap, deps = [], None, ()
    dz2, dz2b, dg2, db2 = _ln_bwd(dh, saved[-1]["xhat2"], saved[-1]["rstd2"], layer_wts[-1]["ln2_g"])
    for l in reversed(range(depth)):
        dz1, dz1b, g_mlp = _mlp_bwd(dz2, dz2b, saved[l], layer_wts[l], alpha=alpha, deps=deps)
        g_mlp["ln2_g"], g_mlp["ln2_b"] = dg2[0:1], db2[0:1]
        pending_mlp = start_exchange(f"reduce_mlp_l{l}", "scatter", [(nm, g_mlp[nm]) for nm in ("w_up", "w_down")])
        deps = (pending_mlp[3],)
        if pending_mix:
            for pend in pending_mix:
                finish_reduce(pend, l + 1, dz1)
            pending_swap = start_swap(l + 1)
            deps += (pending_swap[2],)
        pending_mix = []

        def send(names, g, l=l, pending_mix=pending_mix):
            pend = start_exchange(f"reduce_{names[0]}_l{l}", "scatter", [(nm, g[nm]) for nm in names])
            pending_mix.append(pend)
            return pend[3]

        below = (saved[l - 1]["xhat2"], saved[l - 1]["rstd2"], layer_wts[l - 1]["ln2_g"]) if l > 0 else None
        out, g = _mix_bwd(dz1, dz1b, saved[l], mem2, lb_all[l:l + 1], layer_wts[l], bl=bl, seq=seq, alpha=alpha, send=send,
                          below=below, deps=deps)
        if l > 0:
            dz2, dz2b, dg2, db2 = out
        else:
            dh = out
        finish_reduce(pending_mlp, l, out[0] if l > 0 else out)
        deps = ()
        if pending_swap is not None:
            deps = optimizer_step(l + 1, pending_swap, out[0] if l > 0 else out)
            pending_swap = None
        g.update(g_mlp, lower_bounds=g["lb"])
        smalls[l] = jnp.concatenate([g[nm] for nm in SMALL], axis=1)
    small_parts = _chip_exchange("reduce_small", [(jnp.stack(smalls), "bcast", 0)])[0]
    small_sum = _sum_chips_into(small_parts.reshape(N_CHIPS, depth, -1), jnp.zeros((1, depth, small_parts.shape[-1]), F32), 0)
    small_sum = small_sum.reshape(depth, 1, -1)
    small_theirs = _sibling_swap("swap_small", [small_sum])[0]
    for pend in pending_mix:
        finish_reduce(pend, 0, small_theirs)
    optimizer_step(0, start_swap(0), small_theirs)

    outs = {name: [r.reshape(weights[name].shape) for r in res] for name, res in outs.items()}
    off = 0
    for name in SMALL:
        n = weights[name].shape[1]
        mine, other = small_sum[:, :, off:off + n], small_theirs[:, :, off:off + n]
        off += n
        if name == "lower_bounds":
            mine = _lower_bounds_bwd(soft, mine[:, 0, :])[:, None, :]
            other = _lower_bounds_bwd(soft, other[:, 0, :])[:, None, :]
        shp = (depth, 1, n)
        res = _adamw(weights[name].reshape(shp), mom_m[name].reshape(shp), mom_v[name].reshape(shp), mine, other)
        outs[name] = [r.reshape(weights[name].shape) for r in res]
    assert off == small_sum.shape[-1]

    total_loss = lax.psum(loss[0, 0], ("x", "y", "c"))
    result = [total_loss, dh.reshape(bl, seq, d)]
    for k in range(4):
        result += [outs[name][k] for name in WEIGHT_ORDER]
    return tuple(result)
```

```python
import functools

import jax
import jax.numpy as jnp
from jax import lax
from jax.experimental import pallas as pl
from jax.experimental.pallas import tpu as pltpu

F32 = jnp.float32
BF16 = jnp.bfloat16

HG_HEADS = 4
HG_F = 128
HG_CHUNK = 32
MEM_HEADS = 4
MEM_HEAD_DIM = 128
BRANCH_WIDTH = 512
N_BRANCH = 3
CONV_K = 3
LN_EPS = 1e-5
RMS_EPS = 1e-6
ADAM_LR = 0.001
ADAM_B1 = 0.9
ADAM_B2 = 0.999
ADAM_EPS = 1e-08
ADAM_WD = 0.01
ADAM_STEP = 10

VMEM_LIMIT = 48 * 1024 * 1024


def _cparams(sem):
    return pltpu.CompilerParams(dimension_semantics=sem, vmem_limit_bytes=VMEM_LIMIT)


def _dot(a, b, dims):
    return lax.dot_general(a, b, (dims, ((), ())), preferred_element_type=F32)


NN = ((1,), (0,))
NT = ((1,), (1,))
TN = ((0,), (0,))


def _pick(n, pref):
    for t in pref:
        if n % t == 0:
            return t
    return n


ANY_SPEC = pl.BlockSpec(memory_space=pl.ANY)


def _matmul(name, a, b, *, mode, out_dtype=F32, a_fn=None, a_extra=(), epi_fn=None, epi_extra=(), n_out=1, out_kinds=None,
            tm=512, tn=1024, tk=1024, deps=()):
    M, K = a.shape
    N = b.shape[1] if mode == "nn" else b.shape[0]
    tm, tn, tk = _pick(M, (tm, 256, 128, 8)), _pick(N, (tn, 896, 512, 256, 128)), _pick(K, (tk, 512, 256, 128))
    nk = K // tk
    n_ax, n_ex = len(a_extra), len(epi_extra)
    n_in = 2 + n_ax + n_ex + len(deps)
    out_dtypes = out_dtype if isinstance(out_dtype, (tuple, list)) else (out_dtype,) * n_out
    out_kinds = out_kinds or ("tile",) * n_out

    def body(*refs):
        a_ref, b_ref = refs[0], refs[1]
        ax_refs = refs[2:2 + n_ax]
        ex_refs = refs[2 + n_ax:2 + n_ax + n_ex]
        o_refs = refs[n_in:n_in + n_out]
        at = a_ref[...]
        at = a_fn(at, *[r[...] for r in ax_refs]) if a_fn is not None else at.astype(BF16)
        part = _dot(at, b_ref[...].astype(BF16), NN if mode == "nn" else NT)

        def finish(acc):
            outs = epi_fn(acc, *[r[...] for r in ex_refs]) if epi_fn is not None else (acc,)
            for o_ref, o, kind in zip(o_refs, outs, out_kinds):
                if kind == "rowsum":
                    @pl.when(pl.program_id(1) == 0)
                    def _(o_ref=o_ref):
                        o_ref[...] = jnp.zeros_like(o_ref)

                    o_ref[0:1, :] += o
                else:
                    o_ref[...] = o.astype(o_ref.dtype)

        if nk == 1:
            finish(part)
            return
        acc_ref = refs[-1]
        k = pl.program_id(2)

        @pl.when(k == 0)
        def _():
            acc_ref[...] = part

        @pl.when(jnp.logical_and(k > 0, k < nk - 1))
        def _():
            acc_ref[...] += part

        @pl.when(k == nk - 1)
        def _():
            finish(acc_ref[...] + part)

    b_mode = dict(pipeline_mode=pl.Buffered(1)) if (nk == 1 and N == tn) else {}
    in_specs = [pl.BlockSpec((tm, tk), lambda j, i, k: (i, k)),
                pl.BlockSpec((tk, tn), lambda j, i, k: (k, j), **b_mode) if mode == "nn"
                else pl.BlockSpec((tn, tk), lambda j, i, k: (j, k), **b_mode)]
    in_specs += [pl.BlockSpec((1, tk), lambda j, i, k: (0, k)) for _ in a_extra]
    for e in epi_extra:
        if e.shape[0] == 1:
            in_specs.append(pl.BlockSpec((1, tn), lambda j, i, k: (0, j)))
        elif e.shape[1] == 1:
            in_specs.append(pl.BlockSpec((tm, 1), lambda j, i, k: (i, 0)))
        else:
            in_specs.append(pl.BlockSpec((tm, tn), lambda j, i, k: (i, j)))
    in_specs += [ANY_SPEC] * len(deps)
    out_specs, out_shapes = [], []
    for kind, dt in zip(out_kinds, out_dtypes):
        if kind == "col":
            out_specs.append(pl.BlockSpec((tm, 1), lambda j, i, k: (i, 0)))
            out_shapes.append(jax.ShapeDtypeStruct((M, 1), dt))
        elif kind == "rowsum":
            out_specs.append(pl.BlockSpec((8, tn), lambda j, i, k: (0, j)))
            out_shapes.append(jax.ShapeDtypeStruct((8, N), dt))
        else:
            out_specs.append(pl.BlockSpec((tm, tn), lambda j, i, k: (i, j)))
            out_shapes.append(jax.ShapeDtypeStruct((M, N), dt))
    out = pl.pallas_call(
        body,
        name=name,
        grid=(N // tn, M // tm, nk),
        in_specs=in_specs,
        out_specs=out_specs,
        out_shape=out_shapes,
        scratch_shapes=[pltpu.VMEM((tm, tn), F32)] if nk > 1 else [],
        compiler_params=_cparams(("arbitrary", "arbitrary", "arbitrary")),
    )(a, b, *a_extra, *epi_extra, *deps)
    return out[0] if n_out == 1 else out


def _matmul_tn(name, a, b, *, a_fn=None, a_extra=(), a_cols=None, b_cols=None, ta=1024, tb=1024, tt=1024, out_dtype=F32, deps=()):
    T = a.shape[0]
    a0, Ka = a_cols if a_cols is not None else (0, a.shape[1])
    b0, Nb = b_cols if b_cols is not None else (0, b.shape[1])
    ta, tb, tt = _pick(Ka, (ta, 512, 256, 128)), _pick(Nb, (tb, 896, 512, 256, 128)), _pick(T, (tt, 512, 256, 128))
    assert a0 % ta == 0 and b0 % tb == 0
    a0, b0 = a0 // ta, b0 // tb
    nt = T // tt
    n_ax = len(a_extra)

    def body(*refs):
        a_ref, b_ref = refs[0], refs[1]
        ax_refs = refs[2:2 + n_ax]
        o_ref = refs[2 + n_ax + len(deps)]
        acc_ref = refs[-1]
        t = pl.program_id(2)
        at = a_ref[...]
        at = a_fn(at, *[r[...] for r in ax_refs]) if a_fn is not None else at.astype(BF16)
        part = _dot(at, b_ref[...].astype(BF16), TN)

        @pl.when(t == 0)
        def _():
            acc_ref[...] = part

        @pl.when(jnp.logical_and(t > 0, t < nt - 1))
        def _():
            acc_ref[...] += part

        @pl.when(t == nt - 1)
        def _():
            o_ref[...] = (acc_ref[...] + part if nt > 1 else part).astype(o_ref.dtype)

    in_specs = [pl.BlockSpec((tt, ta), lambda i, j, t: (t, a0 + i)), pl.BlockSpec((tt, tb), lambda i, j, t: (t, b0 + j))]
    in_specs += [pl.BlockSpec((1, ta), lambda i, j, t: (0, a0 + i)) for _ in a_extra]
    in_specs += [ANY_SPEC] * len(deps)
    return pl.pallas_call(
        body,
        name=name,
        grid=(Ka // ta, Nb // tb, nt),
        in_specs=in_specs,
        out_specs=pl.BlockSpec((ta, tb), lambda i, j, t: (i, j)),
        out_shape=jax.ShapeDtypeStruct((Ka, Nb), out_dtype),
        scratch_shapes=[pltpu.VMEM((ta, tb), F32)],
        compiler_params=_cparams(("parallel", "parallel", "arbitrary")),
    )(a, b, *a_extra, *deps)


W = BRANCH_WIDTH
C_CB, C_CC, C_CH, C_HQ, C_HF, C_HI, C_HG, C_MQ, N_MIX = 0, W, 2 * W, 3 * W, 4 * W, 5 * W, 6 * W, 7 * W, 8 * W
TS_MIX = 256
PREV_ROWS = 16
KEEP_NAMES = ("sq", "qs", "k", "sig", "f", "ea", "eb", "eq", "ek")


def _sigmoid(x):
    return jax.nn.sigmoid(x)


def _chunk_pos(shape):
    return lax.broadcasted_iota(jnp.int32, shape, 0) & (HG_CHUNK - 1)


def _seg_cumsum(x, pos):
    sh = 1
    while sh < HG_CHUNK:
        x = x + jnp.where(pos >= sh, pltpu.roll(x, sh, 0), 0.0)
        sh *= 2
    return x


def _seg_rev_cumsum(x, pos):
    n = x.shape[0]
    sh = 1
    while sh < HG_CHUNK:
        x = x + jnp.where(pos < HG_CHUNK - sh, pltpu.roll(x, n - sh, 0), 0.0)
        sh *= 2
    return x


def _chunk_mask(ts):
    r = lax.broadcasted_iota(jnp.int32, (ts, ts), 0)
    c = lax.broadcasted_iota(jnp.int32, (ts, ts), 1)
    return jnp.logical_and((r // HG_CHUNK) == (c // HG_CHUNK), c <= r)


def _hgrn_gates(p_ref, lb):
    q = p_ref[:, C_HQ:C_HQ + W].astype(F32)
    fl = p_ref[:, C_HF:C_HF + W].astype(F32)
    sig = _sigmoid(fl)
    f = lb + (1.0 - lb) * sig
    logf = jnp.log(f)
    k = (1.0 - lb) * _sigmoid(-fl)
    sq = _sigmoid(q)
    qs = q * sq
    return q, sq, qs, sig, f, logf, k


def _hgrn_decays(logf, bc_sc, ts):
    pos = _chunk_pos(logf.shape)
    bc = _seg_cumsum(logf, pos)
    bc_sc[...] = bc
    nc = ts // HG_CHUNK
    bref = jnp.concatenate(
        [jnp.broadcast_to(bc_sc[n * HG_CHUNK + HG_CHUNK // 2 - 1:n * HG_CHUNK + HG_CHUNK // 2, :], (HG_CHUNK, W)) for n in range(nc)], axis=0)
    blast = jnp.concatenate(
        [jnp.broadcast_to(bc_sc[(n + 1) * HG_CHUNK - 1:(n + 1) * HG_CHUNK, :], (HG_CHUNK, W)) for n in range(nc)], axis=0)
    return pos, bc, bref, blast


def _conv_shift_down(u, carry_ref, row):
    n = carry_ref.shape[0]
    last, before = carry_ref[n - 1:n, :], carry_ref[n - 2:n - 1, :]
    u1 = jnp.where(row == 0, last, pltpu.roll(u, 1, 0))
    u2 = jnp.where(row == 0, before, jnp.where(row == 1, last, pltpu.roll(u, 2, 0)))
    return u1, u2


def _attn_probs(qh, kh):
    s = _dot(qh, kh, NT) * (MEM_HEAD_DIM ** -0.5)
    e = jnp.exp(s - jnp.max(s, axis=-1, keepdims=True))
    return e / jnp.sum(e, axis=-1, keepdims=True)


def _mixer_fwd(p, mk, mv, lb, conv_w, norm_w, *, bl, seq):
    T = p.shape[0]
    ts = TS_MIX
    ns = seq // ts
    nc = ts // HG_CHUNK
    ml = mk.shape[0] // bl

    def body(p_ref, mk_ref, mv_ref, lb_ref, cw_ref, nw_ref, y_ref, st_ref, opre_ref, state_sc, carry_sc, bc_sc):
        @pl.when(pl.program_id(1) == 0)
        def _():
            state_sc[...] = jnp.zeros_like(state_sc)
            carry_sc[...] = jnp.zeros_like(carry_sc)

        cb, cc, ch = (p_ref[:, c0:c0 + W].astype(F32) for c0 in (C_CB, C_CC, C_CH))
        u = cc * ch
        row = lax.broadcasted_iota(jnp.int32, (ts, W), 0)
        u1, u2 = _conv_shift_down(u, carry_sc, row)
        yconv = u2 * cw_ref[0:1, :] + u1 * cw_ref[1:2, :] + u * cw_ref[2:3, :]
        y_ref[:, 0:W] = (cb * yconv).astype(BF16)
        carry_sc[...] = u[ts - 8:ts, :]

        lbv = lb_ref[...]
        _, _, qs, _, _, logf, k = _hgrn_gates(p_ref, lbv)
        pos, bc, bref, blast = _hgrn_decays(logf, bc_sc, ts)
        a_all = (qs * jnp.exp(bc - bref)).astype(BF16)
        bk_all = (k * jnp.exp(bref - bc)).astype(BF16)
        qin_all = (qs * jnp.exp(bc)).astype(BF16)
        kout_all = (k * jnp.exp(blast - bc)).astype(BF16)
        v_all = p_ref[:, C_HI:C_HI + W].astype(BF16)
        mask = _chunk_mask(ts)
        heads = [slice(h * HG_F, (h + 1) * HG_F) for h in range(HG_HEADS)]
        st = [state_sc[h] for h in range(HG_HEADS)]
        o_inter = [[] for _ in range(HG_HEADS)]
        for n in range(nc):
            rows = slice(n * HG_CHUNK, (n + 1) * HG_CHUNK)
            for h, hs in enumerate(heads):
                st_ref[n, h] = st[h]
                o_inter[h].append(_dot(qin_all[rows, hs], st[h].astype(BF16), NT))
                kv = _dot(v_all[rows, hs], kout_all[rows, hs], TN)
                decay = jnp.exp(bc_sc[(n + 1) * HG_CHUNK - 1:(n + 1) * HG_CHUNK, hs])
                st[h] = st[h] * decay + kv
        for h in range(HG_HEADS):
            state_sc[h] = st[h]
        scores = [_dot(a_all[:, hs], bk_all[:, hs], NT) for hs in heads]
        scores = [jnp.where(mask, s, 0.0).astype(BF16) for s in scores]
        outs = [_dot(scores[h], v_all[:, hs], NN) + jnp.concatenate(o_inter[h], axis=0) for h, hs in enumerate(heads)]
        for h, hs in enumerate(heads):
            o = outs[h]
            opre_ref[:, hs] = o
            on = o * lax.rsqrt(jnp.mean(o * o, axis=-1, keepdims=True) + RMS_EPS) * nw_ref[...]
            g = p_ref[:, C_HG + h * HG_F:C_HG + (h + 1) * HG_F].astype(F32)
            y_ref[:, W + h * HG_F:W + (h + 1) * HG_F] = (on * (g * _sigmoid(g))).astype(BF16)

        mheads = [slice(h * MEM_HEAD_DIM, (h + 1) * MEM_HEAD_DIM) for h in range(MEM_HEADS)]
        probs = [_attn_probs(p_ref[:, C_MQ + h * MEM_HEAD_DIM:C_MQ + (h + 1) * MEM_HEAD_DIM].astype(BF16), mk_ref[:, hs])
                 for h, hs in enumerate(mheads)]
        for h, hs in enumerate(mheads):
            y_ref[:, 2 * W + h * MEM_HEAD_DIM:2 * W + (h + 1) * MEM_HEAD_DIM] = _dot(
                probs[h].astype(BF16), mv_ref[:, hs], NN).astype(BF16)

    return pl.pallas_call(
        body,
        name="mixer_fwd",
        grid=(bl, ns),
        in_specs=[
            pl.BlockSpec((ts, N_MIX), lambda b, s: (b * ns + s, 0)),
            pl.BlockSpec((ml, W), lambda b, s: (b, 0)),
            pl.BlockSpec((ml, W), lambda b, s: (b, 0)),
            pl.BlockSpec((1, W), lambda b, s: (0, 0)),
            pl.BlockSpec((CONV_K, W), lambda b, s: (0, 0)),
            pl.BlockSpec((1, HG_F), lambda b, s: (0, 0)),
        ],
        out_specs=[
            pl.BlockSpec((ts, 3 * W), lambda b, s: (b * ns + s, 0)),
            pl.BlockSpec((nc, HG_HEADS, HG_F, HG_F), lambda b, s: (b * ns + s, 0, 0, 0)),
            pl.BlockSpec((ts, W), lambda b, s: (b * ns + s, 0)),
        ],
        out_shape=[
            jax.ShapeDtypeStruct((T, 3 * W), BF16),
            jax.ShapeDtypeStruct((T // HG_CHUNK, HG_HEADS, HG_F, HG_F), F32),
            jax.ShapeDtypeStruct((T, W), F32),
        ],
        scratch_shapes=[pltpu.VMEM((HG_HEADS, HG_F, HG_F), F32), pltpu.VMEM((8, W), F32), pltpu.VMEM((ts, W), F32)],
        compiler_params=_cparams(("arbitrary", "arbitrary")),
    )(p, mk, mv, lb, conv_w, norm_w)


def _mixer_bwd(p, dy, dp_gates, st, opre, mk, mv, lb, conv_w, norm_w, *, bl, seq, deps=()):
    T, nin = p.shape
    ts = TS_MIX
    ns = seq // ts
    nc = ts // HG_CHUNK
    ml = mk.shape[0] // bl
    mid, last = HG_CHUNK // 2 - 1, HG_CHUNK - 1

    def body(p_ref, pprev_ref, dy_ref, dpin_ref, st_ref, opre_ref, mk_ref, mv_ref, lb_ref, cw_ref, nw_ref, *rest):
        (dp_ref, dmk_ref, dmv_ref, dcw_ref, dnw_ref, dlb_ref, dstate_sc, carry_sc, uprev_sc, ab_sc, bkb_sc, qinb_sc, koutb_sc,
         dob_sc, dv_sc, da_sc, dbk_sc, dqin_sc, dkout_sc, dec_sc, ddec_sc, *keep_scs) = rest[len(deps):]
        del dpin_ref
        b, s = pl.program_id(0), pl.program_id(1)

        @pl.when(s == 0)
        def _():
            dstate_sc[...] = jnp.zeros_like(dstate_sc)
            carry_sc[...] = jnp.zeros_like(carry_sc)
            dmk_ref[...] = jnp.zeros_like(dmk_ref)
            dmv_ref[...] = jnp.zeros_like(dmv_ref)

        @pl.when(jnp.logical_and(b == 0, s == 0))
        def _():
            dcw_ref[...] = jnp.zeros_like(dcw_ref)
            dnw_ref[...] = jnp.zeros_like(dnw_ref)
            dlb_ref[...] = jnp.zeros_like(dlb_ref)

        cb, cc, ch = (p_ref[:, c0:c0 + W].astype(F32) for c0 in (C_CB, C_CC, C_CH))
        u = cc * ch
        row = lax.broadcasted_iota(jnp.int32, (ts, W), 0)
        uprev = pprev_ref[:, C_CC:C_CC + W].astype(F32) * pprev_ref[:, C_CH:C_CH + W].astype(F32)
        uprev_sc[...] = jnp.where(s == ns - 1, 0.0, uprev)
        u1, u2 = _conv_shift_down(u, uprev_sc, row)
        w0, w1, w2 = cw_ref[0:1, :], cw_ref[1:2, :], cw_ref[2:3, :]
        dya = dy_ref[:, 0:W].astype(F32)
        dp_ref[:, C_CB:C_CB + W] = (dya * (u2 * w0 + u1 * w1 + u * w2)).astype(BF16)
        dv = cb * dya
        dv1 = jnp.where(row == ts - 1, carry_sc[0:1, :], pltpu.roll(dv, ts - 1, 0))
        dv2 = jnp.where(row == ts - 1, carry_sc[1:2, :], jnp.where(row == ts - 2, carry_sc[0:1, :], pltpu.roll(dv, ts - 2, 0)))
        du = dv * w2 + dv1 * w1 + dv2 * w0
        dp_ref[:, C_CC:C_CC + W] = (du * ch).astype(BF16)
        dp_ref[:, C_CH:C_CH + W] = (du * cc).astype(BF16)
        dcw_ref[0:1, :] += jnp.sum(dv * u2, axis=0, keepdims=True)
        dcw_ref[1:2, :] += jnp.sum(dv * u1, axis=0, keepdims=True)
        dcw_ref[2:3, :] += jnp.sum(dv * u, axis=0, keepdims=True)
        carry_sc[...] = dv[0:8, :]

        mask = _chunk_mask(ts)
        pos_c = _chunk_pos((HG_CHUNK, HG_F))
        nw = nw_ref[...]

        def block(n, h):
            rows = slice(n * HG_CHUNK, (n + 1) * HG_CHUNK)
            return rows, slice(h * HG_F, (h + 1) * HG_F)

        keep = dict(zip(KEEP_NAMES, keep_scs))

        def gates(rows, h):
            lbh = lb_ref[:, h * HG_F:(h + 1) * HG_F]
            q = p_ref[rows, C_HQ + h * HG_F:C_HQ + (h + 1) * HG_F].astype(F32)
            fl = p_ref[rows, C_HF + h * HG_F:C_HF + (h + 1) * HG_F].astype(F32)
            sig = _sigmoid(fl)
            f = lbh + (1.0 - lbh) * sig
            k = (1.0 - lbh) * _sigmoid(-fl)
            sq = _sigmoid(q)
            qs = q * sq
            bc = _seg_cumsum(jnp.log(f), pos_c)
            bref = jnp.sum(jnp.where(pos_c == mid, bc, 0.0), axis=0, keepdims=True)
            blast = jnp.sum(jnp.where(pos_c == last, bc, 0.0), axis=0, keepdims=True)
            ea, eb, eq, ek = jnp.exp(bc - bref), jnp.exp(bref - bc), jnp.exp(bc), jnp.exp(blast - bc)
            return dict(sq=sq, qs=qs, k=k, sig=sig, f=f, ea=ea, eb=eb, eq=eq, ek=ek), blast

        dnw = jnp.zeros((1, HG_F), F32)
        for n in range(nc):
            for h in range(HG_HEADS):
                rows, hs = block(n, h)
                fw, blast = gates(rows, h)
                for name in KEEP_NAMES:
                    keep[name][rows, hs] = fw[name]
                ab_sc[rows, hs] = (fw["qs"] * fw["ea"]).astype(BF16)
                bkb_sc[rows, hs] = (fw["k"] * fw["eb"]).astype(BF16)
                qinb_sc[rows, hs] = (fw["qs"] * fw["eq"]).astype(BF16)
                koutb_sc[rows, hs] = (fw["k"] * fw["ek"]).astype(BF16)
                dec_sc[n:n + 1, hs] = jnp.exp(blast)
                o = opre_ref[rows, hs]
                g = p_ref[rows, C_HG + h * HG_F:C_HG + (h + 1) * HG_F].astype(F32)
                sg = _sigmoid(g)
                r = lax.rsqrt(jnp.mean(o * o, axis=-1, keepdims=True) + RMS_EPS)
                dyb = dy_ref[rows, W + h * HG_F:W + (h + 1) * HG_F].astype(F32)
                dp_ref[rows, C_HG + h * HG_F:C_HG + (h + 1) * HG_F] = (
                    dyb * (o * r * nw) * (sg * (1.0 + g * (1.0 - sg)))).astype(BF16)
                don = dyb * (g * sg)
                dnw = dnw + jnp.sum(don * o * r, axis=0, keepdims=True)
                dn = don * nw
                dob_sc[rows, hs] = (r * (dn - o * (r * r) * jnp.mean(dn * o, axis=-1, keepdims=True))).astype(BF16)
        dnw_ref[0:1, :] += dnw

        heads = [slice(h * HG_F, (h + 1) * HG_F) for h in range(HG_HEADS)]
        scores = [_dot(ab_sc[:, hs], bkb_sc[:, hs], NT) for hs in heads]
        dscores = [_dot(dob_sc[:, hs], p_ref[:, C_HI + h * HG_F:C_HI + (h + 1) * HG_F].astype(BF16), NT)
                   for h, hs in enumerate(heads)]
        scores = [jnp.where(mask, s, 0.0).astype(BF16) for s in scores]
        dscores = [jnp.where(mask, s, 0.0).astype(BF16) for s in dscores]
        for h, hs in enumerate(heads):
            dv_sc[:, hs] = _dot(scores[h], dob_sc[:, hs], TN)
            da_sc[:, hs] = _dot(dscores[h], bkb_sc[:, hs], NN)
            dbk_sc[:, hs] = _dot(dscores[h], ab_sc[:, hs], TN)
        dst = [dstate_sc[h] for h in range(HG_HEADS)]
        for n in reversed(range(nc)):
            for h in range(HG_HEADS):
                rows, hs = block(n, h)
                st_n = st_ref[n, h]
                decay = dec_sc[n:n + 1, hs]
                dstb = dst[h].astype(BF16)
                dob_n = dob_sc[rows, hs]
                dv_sc[rows, hs] += _dot(koutb_sc[rows, hs], dstb, NT)
                dkout_sc[rows, hs] = _dot(p_ref[rows, C_HI + h * HG_F:C_HI + (h + 1) * HG_F].astype(BF16), dstb, NN)
                ddec_sc[n:n + 1, hs] = jnp.sum(dst[h] * st_n, axis=0, keepdims=True) * decay
                dqin_sc[rows, hs] = _dot(dob_n, st_n.astype(BF16), NN)
                dst[h] = dst[h] * decay + _dot(dob_n, qinb_sc[rows, hs], TN)
        for h in range(HG_HEADS):
            dstate_sc[h] = dst[h]

        for h in range(HG_HEADS):
            dlb = jnp.zeros((1, HG_F), F32)
            for n in range(nc):
                rows, hs = block(n, h)
                fw = {name: keep[name][rows, hs] for name in KEEP_NAMES}
                lbh = lb_ref[:, h * HG_F:(h + 1) * HG_F]
                q = p_ref[rows, C_HQ + h * HG_F:C_HQ + (h + 1) * HG_F].astype(F32)
                da, dbk, dqin, dkout = da_sc[rows, hs], dbk_sc[rows, hs], dqin_sc[rows, hs], dkout_sc[rows, hs]
                w_a, w_b, w_q, w_k = da * fw["ea"], dbk * fw["eb"], dqin * fw["eq"], dkout * fw["ek"]
                dqs, dk = w_a + w_q, w_b + w_k
                t_a, t_b, t_q, t_k = w_a * fw["qs"], w_b * fw["k"], w_q * fw["qs"], w_k * fw["k"]
                s_ref = jnp.sum(t_b - t_a, axis=0, keepdims=True)
                s_last = jnp.sum(t_k, axis=0, keepdims=True) + ddec_sc[n:n + 1, hs]
                dbc = (t_a - t_b + t_q - t_k) + jnp.where(pos_c == mid, s_ref, 0.0) + jnp.where(pos_c == last, s_last, 0.0)
                dfk = _seg_rev_cumsum(dbc, pos_c) / fw["f"] - dk
                sig, sq = fw["sig"], fw["sq"]
                dp_ref[rows, C_HF + h * HG_F:C_HF + (h + 1) * HG_F] = (dfk * (1.0 - lbh) * sig * (1.0 - sig)).astype(BF16)
                dlb = dlb + jnp.sum(dfk * (1.0 - sig), axis=0, keepdims=True)
                dp_ref[rows, C_HQ + h * HG_F:C_HQ + (h + 1) * HG_F] = (dqs * (sq * (1.0 + q * (1.0 - sq)))).astype(BF16)
                dp_ref[rows, C_HI + h * HG_F:C_HI + (h + 1) * HG_F] = dv_sc[rows, hs].astype(BF16)
            dlb_ref[0:1, h * HG_F:(h + 1) * HG_F] += dlb

        mheads = [slice(h * MEM_HEAD_DIM, (h + 1) * MEM_HEAD_DIM) for h in range(MEM_HEADS)]
        qhs = [p_ref[:, C_MQ + h * MEM_HEAD_DIM:C_MQ + (h + 1) * MEM_HEAD_DIM].astype(BF16) for h in range(MEM_HEADS)]
        dobs = [dy_ref[:, 2 * W + h * MEM_HEAD_DIM:2 * W + (h + 1) * MEM_HEAD_DIM].astype(BF16) for h in range(MEM_HEADS)]
        probs = [_attn_probs(qhs[h], mk_ref[:, hs]) for h, hs in enumerate(mheads)]
        dprobs = [_dot(dobs[h], mv_ref[:, hs], NT) for h, hs in enumerate(mheads)]
        for h, hs in enumerate(mheads):
            prob = probs[h]
            dmv_ref[:, hs] += _dot(prob.astype(BF16), dobs[h], TN)
            ds = prob * (dprobs[h] - jnp.sum(dprobs[h] * prob, axis=-1, keepdims=True)) * (MEM_HEAD_DIM ** -0.5)
            dsb = ds.astype(BF16)
            dp_ref[:, C_MQ + h * MEM_HEAD_DIM:C_MQ + (h + 1) * MEM_HEAD_DIM] = _dot(dsb, mk_ref[:, hs], NN).astype(BF16)
            dmk_ref[:, hs] += _dot(dsb, qhs[h], TN)

    def tile(b, s):
        return b * ns + (ns - 1 - s)

    return pl.pallas_call(
        body,
        name="mixer_bwd",
        grid=(bl, ns),
        in_specs=[
            pl.BlockSpec((ts, N_MIX), lambda b, s: (tile(b, s), 0)),
            pl.BlockSpec((PREV_ROWS, N_MIX), lambda b, s: (jnp.maximum(tile(b, s) * (ts // PREV_ROWS) - 1, 0), 0)),
            pl.BlockSpec((ts, 3 * W), lambda b, s: (tile(b, s), 0)),
            pl.BlockSpec(memory_space=pl.ANY),
            pl.BlockSpec((nc, HG_HEADS, HG_F, HG_F), lambda b, s: (tile(b, s), 0, 0, 0)),
            pl.BlockSpec((ts, W), lambda b, s: (tile(b, s), 0)),
            pl.BlockSpec((ml, W), lambda b, s: (b, 0)),
            pl.BlockSpec((ml, W), lambda b, s: (b, 0)),
            pl.BlockSpec((1, W), lambda b, s: (0, 0)),
            pl.BlockSpec((CONV_K, W), lambda b, s: (0, 0)),
            pl.BlockSpec((1, HG_F), lambda b, s: (0, 0)),
        ] + [ANY_SPEC] * len(deps),
        out_specs=[
            pl.BlockSpec((ts, N_MIX), lambda b, s: (tile(b, s), 0)),
            pl.BlockSpec((ml, W), lambda b, s: (b, 0)),
            pl.BlockSpec((ml, W), lambda b, s: (b, 0)),
            pl.BlockSpec((8, W), lambda b, s: (0, 0)),
            pl.BlockSpec((8, HG_F), lambda b, s: (0, 0)),
            pl.BlockSpec((8, W), lambda b, s: (0, 0)),
        ],
        out_shape=[
            jax.ShapeDtypeStruct((T, nin), BF16),
            jax.ShapeDtypeStruct((bl * ml, W), F32),
            jax.ShapeDtypeStruct((bl * ml, W), F32),
            jax.ShapeDtypeStruct((8, W), F32),
            jax.ShapeDtypeStruct((8, HG_F), F32),
            jax.ShapeDtypeStruct((8, W), F32),
        ],
        input_output_aliases={3: 0},
        scratch_shapes=[pltpu.VMEM((HG_HEADS, HG_F, HG_F), F32), pltpu.VMEM((8, W), F32), pltpu.VMEM((PREV_ROWS, W), F32)]
        + [pltpu.VMEM((ts, W), BF16)] * 5 + [pltpu.VMEM((ts, W), F32)] * 5 + [pltpu.VMEM((nc, W), F32)] * 2
        + [pltpu.VMEM((ts, W), F32)] * len(KEEP_NAMES),
        compiler_params=_cparams(("arbitrary", "arbitrary")),
    )(p, p, dy, dp_gates, st, opre, mk, mv, lb, conv_w, norm_w, *deps)


def _layer_norm_stats(z):
    mu = jnp.mean(z, axis=-1, keepdims=True)
    zc = z - mu
    rstd = lax.rsqrt(jnp.mean(zc * zc, axis=-1, keepdims=True) + LN_EPS)
    return zc * rstd, rstd


def _gate_specs(tm, d):
    g0 = N_MIX // d
    return [pl.BlockSpec((tm, d), functools.partial(lambda i, k: (i, g0 + k), k=k)) for k in range(N_BRANCH)]


def _merge_fwd(y, p, x0, wb, wo, bg, ln_g, ln_b, *, alpha, tm=512):
    T, d = x0.shape
    assert N_MIX % d == 0
    tm = _pick(T, (tm, 128, 8))

    def body(y_ref, g0_ref, g1_ref, g2_ref, x_ref, wb_ref, wo_ref, bg_ref, lg_ref, lb_ref, mg_ref, xh_ref, rs_ref, x1b_ref):
        merged = None
        for i, g_ref in enumerate((g0_ref, g1_ref, g2_ref)):
            r = _dot(y_ref[:, i * W:(i + 1) * W], wb_ref[i * W:(i + 1) * W, :], NN)
            t = _sigmoid(g_ref[...].astype(F32) + bg_ref[:, i * d:(i + 1) * d]) * r
            merged = t if merged is None else merged + t
        mb = merged.astype(BF16)
        mg_ref[...] = mb
        z = alpha * x_ref[...] + _dot(mb, wo_ref[...], NN)
        xh, rs = _layer_norm_stats(z)
        xh_ref[...], rs_ref[...] = xh, rs
        x1b_ref[...] = (xh * lg_ref[...] + lb_ref[...]).astype(BF16)

    row = lambda i: (i, 0)
    fix = lambda i: (0, 0)
    return pl.pallas_call(
        body,
        name="merge_fwd",
        grid=(T // tm,),
        in_specs=[pl.BlockSpec((tm, 3 * W), row)] + _gate_specs(tm, d) + [
            pl.BlockSpec((tm, d), row), pl.BlockSpec((3 * W, d), fix, pipeline_mode=pl.Buffered(1)),
            pl.BlockSpec((d, d), fix, pipeline_mode=pl.Buffered(1)), pl.BlockSpec((1, 3 * d), fix),
            pl.BlockSpec((1, d), fix), pl.BlockSpec((1, d), fix)],
        out_specs=[pl.BlockSpec((tm, d), row), pl.BlockSpec((tm, d), row), pl.BlockSpec((tm, 1), row), pl.BlockSpec((tm, d), row)],
        out_shape=[jax.ShapeDtypeStruct((T, d), BF16), jax.ShapeDtypeStruct((T, d), F32), jax.ShapeDtypeStruct((T, 1), F32),
                   jax.ShapeDtypeStruct((T, d), BF16)],
        compiler_params=_cparams(("parallel",)),
    )(y, p, p, p, x0, wb, wo, bg, ln_g, ln_b)


def _merge_bwd(dz, p, y, wb, wo, bg, *, tm=512):
    T, d = dz.shape
    nin = p.shape[1]
    tm = _pick(T, (tm, 128, 8))

    def body(dz_ref, g0_ref, g1_ref, g2_ref, y_ref, wb_ref, wo_ref, bg_ref, dr_ref, dp_ref, dy_ref, dbg_ref):
        @pl.when(pl.program_id(0) == 0)
        def _():
            dbg_ref[...] = jnp.zeros_like(dbg_ref)

        dmerged = _dot(dz_ref[...].astype(BF16), wo_ref[...], NT)
        dp_ref[:, 0:N_MIX] = jnp.zeros((tm, N_MIX), BF16)
        for i, g_ref in enumerate((g0_ref, g1_ref, g2_ref)):
            cs = slice(i * d, (i + 1) * d)
            s = _sigmoid(g_ref[...].astype(F32) + bg_ref[:, cs])
            drb = (dmerged * s).astype(BF16)
            dr_ref[:, cs] = drb
            dgate = dmerged * _dot(y_ref[:, i * W:(i + 1) * W], wb_ref[i * W:(i + 1) * W, :], NN) * s * (1.0 - s)
            dp_ref[:, N_MIX + i * d:N_MIX + (i + 1) * d] = dgate.astype(BF16)
            dbg_ref[0:1, cs] += jnp.sum(dgate, axis=0, keepdims=True)
            dy_ref[:, i * W:(i + 1) * W] = _dot(drb, wb_ref[i * W:(i + 1) * W, :], NT).astype(BF16)

    row = lambda i: (i, 0)
    fix = lambda i: (0, 0)
    return pl.pallas_call(
        body,
        name="merge_bwd",
        grid=(T // tm,),
        in_specs=[pl.BlockSpec((tm, d), row)] + _gate_specs(tm, d) + [
            pl.BlockSpec((tm, 3 * W), row), pl.BlockSpec((3 * W, d), fix, pipeline_mode=pl.Buffered(1)),
            pl.BlockSpec((d, d), fix, pipeline_mode=pl.Buffered(1)), pl.BlockSpec((1, 3 * d), fix)],
        out_specs=[pl.BlockSpec((tm, 3 * d), row), pl.BlockSpec((tm, nin), row), pl.BlockSpec((tm, 3 * W), row),
                   pl.BlockSpec((8, 3 * d), fix)],
        out_shape=[jax.ShapeDtypeStruct((T, 3 * d), BF16), jax.ShapeDtypeStruct((T, nin), BF16),
                   jax.ShapeDtypeStruct((T, 3 * W), BF16), jax.ShapeDtypeStruct((8, 3 * d), F32)],
        compiler_params=_cparams(("arbitrary",)),
    )(dz, p, p, p, y, wb, wo, bg)


MLP_VMEM_LIMIT = 58 * 1024 * 1024


def _mlp_fwd(xhat1, x1b, g1, b1, wu, wd, g2, b2, *, alpha, tm=512, tf=1024):
    T, d = xhat1.shape
    ff = wu.shape[1]
    tm, tf = _pick(T, (tm, 256, 128, 8)), _pick(ff, (tf, 1024, 512, 256, 128))

    def body(xh_ref, x1b_ref, g1_ref, b1_ref, wu_ref, wd_ref, g2_ref, b2_ref, a_ref, xh2_ref, rs2_ref, x2_ref, x2b_ref):
        xb = x1b_ref[...]
        acc = None
        a = _dot(xb, wu_ref[:, 0:tf], NN)
        for c0 in range(0, ff, tf):
            a_next = _dot(xb, wu_ref[:, c0 + tf:c0 + 2 * tf], NN) if c0 + tf < ff else None
            a_ref[:, c0:c0 + tf] = a.astype(BF16)
            part = _dot(jnp.square(jnp.maximum(a, 0.0)).astype(BF16), wd_ref[c0:c0 + tf, :], NN)
            acc = part if acc is None else acc + part
            a = a_next
        x1 = xh_ref[...] * g1_ref[...] + b1_ref[...]
        xh2, rs2 = _layer_norm_stats(alpha * x1 + acc)
        xh2_ref[...] = xh2
        rs2_ref[...] = rs2
        x2 = xh2 * g2_ref[...] + b2_ref[...]
        x2_ref[...] = x2
        x2b_ref[...] = x2.astype(BF16)

    row = lambda i: (i, 0)
    fix = lambda i: (0, 0)
    once = dict(pipeline_mode=pl.Buffered(1))
    return pl.pallas_call(
        body,
        name="mlp_fwd",
        grid=(T // tm,),
        in_specs=[pl.BlockSpec((tm, d), row), pl.BlockSpec((tm, d), row), pl.BlockSpec((1, d), fix), pl.BlockSpec((1, d), fix),
                  pl.BlockSpec((d, ff), fix, **once), pl.BlockSpec((ff, d), fix, **once),
                  pl.BlockSpec((1, d), fix), pl.BlockSpec((1, d), fix)],
        out_specs=[pl.BlockSpec((tm, ff), row), pl.BlockSpec((tm, d), row), pl.BlockSpec((tm, 1), row),
                   pl.BlockSpec((tm, d), row), pl.BlockSpec((tm, d), row)],
        out_shape=[jax.ShapeDtypeStruct((T, ff), BF16), jax.ShapeDtypeStruct((T, d), F32), jax.ShapeDtypeStruct((T, 1), F32),
                   jax.ShapeDtypeStruct((T, d), F32), jax.ShapeDtypeStruct((T, d), BF16)],
        compiler_params=pltpu.CompilerParams(dimension_semantics=("parallel",), vmem_limit_bytes=MLP_VMEM_LIMIT),
    )(xhat1, x1b, g1, b1, wu, wd, g2, b2)


def _ln_bwd(dy, xhat, rstd, g, *, tm=512, deps=()):
    T, d = dy.shape
    tm = _pick(T, (tm, 256, 128, 8))

    def body(dy_ref, xh_ref, rs_ref, g_ref, *rest):
        dz_ref, dzb_ref, dg_ref, db_ref = rest[len(deps):]

        @pl.when(pl.program_id(0) == 0)
        def _():
            dg_ref[...] = jnp.zeros_like(dg_ref)
            db_ref[...] = jnp.zeros_like(db_ref)

        dy_, xh = dy_ref[...], xh_ref[...]
        dg_ref[0:1, :] += jnp.sum(dy_ * xh, axis=0, keepdims=True)
        db_ref[0:1, :] += jnp.sum(dy_, axis=0, keepdims=True)
        dxh = dy_ * g_ref[...]
        dz = rs_ref[...] * (dxh - jnp.mean(dxh, axis=-1, keepdims=True) - xh * jnp.mean(dxh * xh, axis=-1, keepdims=True))
        dz_ref[...] = dz
        dzb_ref[...] = dz.astype(BF16)

    row = lambda i: (i, 0)
    fix = lambda i: (0, 0)
    return pl.pallas_call(
        body,
        name="ln_bwd",
        grid=(T // tm,),
        in_specs=[pl.BlockSpec((tm, d), row), pl.BlockSpec((tm, d), row), pl.BlockSpec((tm, 1), row), pl.BlockSpec((1, d), fix)]
        + [ANY_SPEC] * len(deps),
        out_specs=[pl.BlockSpec((tm, d), row), pl.BlockSpec((tm, d), row), pl.BlockSpec((8, d), fix), pl.BlockSpec((8, d), fix)],
        out_shape=[jax.ShapeDtypeStruct((T, d), F32), jax.ShapeDtypeStruct((T, d), BF16), jax.ShapeDtypeStruct((8, d), F32),
                   jax.ShapeDtypeStruct((8, d), F32)],
        compiler_params=_cparams(("arbitrary",)),
    )(dy, xhat, rstd, g, *deps)


def _loss_head(y, target, xhat, rstd, g, *, tm=512):
    T, d = y.shape
    tm = _pick(T, (tm, 256, 128, 8))
    n = T // tm

    def body(y_ref, t_ref, xh_ref, rs_ref, g_ref, loss_ref, dz_ref, dzb_ref, dg_ref, db_ref, acc_ref):
        i = pl.program_id(0)

        @pl.when(i == 0)
        def _():
            acc_ref[...] = jnp.zeros_like(acc_ref)
            dg_ref[...] = jnp.zeros_like(dg_ref)
            db_ref[...] = jnp.zeros_like(db_ref)

        e = y_ref[...] - t_ref[...]
        acc_ref[...] += jnp.sum(e * e, axis=0, keepdims=True)
        dy_, xh = e * (1.0 / d), xh_ref[...]
        dg_ref[0:1, :] += jnp.sum(dy_ * xh, axis=0, keepdims=True)
        db_ref[0:1, :] += jnp.sum(dy_, axis=0, keepdims=True)
        dxh = dy_ * g_ref[...]
        dz = rs_ref[...] * (dxh - jnp.mean(dxh, axis=-1, keepdims=True) - xh * jnp.mean(dxh * xh, axis=-1, keepdims=True))
        dz_ref[...] = dz
        dzb_ref[...] = dz.astype(BF16)

        @pl.when(i == n - 1)
        def _():
            loss_ref[...] = (0.5 / d) * jnp.sum(acc_ref[...], axis=1, keepdims=True)

    row = lambda i: (i, 0)
    fix = lambda i: (0, 0)
    return pl.pallas_call(
        body,
        name="loss_head",
        grid=(n,),
        in_specs=[pl.BlockSpec((tm, d), row), pl.BlockSpec((tm, d), row), pl.BlockSpec((tm, d), row), pl.BlockSpec((tm, 1), row),
                  pl.BlockSpec((1, d), fix)],
        out_specs=[pl.BlockSpec((1, 1), fix), pl.BlockSpec((tm, d), row), pl.BlockSpec((tm, d), row), pl.BlockSpec((8, d), fix),
                   pl.BlockSpec((8, d), fix)],
        out_shape=[jax.ShapeDtypeStruct((1, 1), F32), jax.ShapeDtypeStruct((T, d), F32), jax.ShapeDtypeStruct((T, d), BF16),
                   jax.ShapeDtypeStruct((8, d), F32), jax.ShapeDtypeStruct((8, d), F32)],
        scratch_shapes=[pltpu.VMEM((1, d), F32)],
        compiler_params=_cparams(("arbitrary",)),
    )(y, target, xhat, rstd, g)


def _lower_bounds_fwd(lower_bounds):
    depth, n = lower_bounds.shape

    def body(x_ref, soft_ref, lb_ref):
        x = x_ref[...]
        e = jnp.exp(x - jnp.max(x, axis=0, keepdims=True))
        soft_ref[...] = e / jnp.sum(e, axis=0, keepdims=True)
        run = None
        for l in range(depth):
            run = soft_ref[l:l + 1, :] if run is None else run + soft_ref[l:l + 1, :]
            lb_ref[l:l + 1, :] = run - soft_ref[0:1, :]

    return pl.pallas_call(body, name="lower_bounds_fwd",
                          out_shape=[jax.ShapeDtypeStruct((depth, n), F32), jax.ShapeDtypeStruct((depth, n), F32)])(lower_bounds)


def _lower_bounds_bwd(soft, dlb):
    depth, n = soft.shape

    def body(soft_ref, dlb_ref, out_ref, dsoft_ref):
        total = jnp.sum(dlb_ref[...], axis=0, keepdims=True)
        run = None
        for l in reversed(range(depth)):
            run = dlb_ref[l:l + 1, :] if run is None else run + dlb_ref[l:l + 1, :]
            dsoft_ref[l:l + 1, :] = run - total if l == 0 else run
        s, ds = soft_ref[...], dsoft_ref[...]
        out_ref[...] = s * (ds - jnp.sum(s * ds, axis=0, keepdims=True))

    return pl.pallas_call(body, name="lower_bounds_bwd", out_shape=jax.ShapeDtypeStruct((depth, n), F32),
                          scratch_shapes=[pltpu.VMEM((depth, n), F32)])(soft, dlb)


def _layer_fwd(x0, x0b, mem2, lb, w_in, mix_fn, late_fn, *, bl, seq, alpha, deps=()):
    p = _matmul("proj_in", x0b, w_in, mode="nn", out_dtype=BF16, deps=deps, tm=1024, tn=1792)
    wts = dict(mix_fn(p), w_in=w_in)
    mk = _matmul("mem_k", mem2, wts["w_mem_k"], mode="nn", out_dtype=BF16)
    mv = _matmul("mem_v", mem2, wts["w_mem_v"], mode="nn", out_dtype=BF16)
    y, st, opre = _mixer_fwd(p, mk, mv, lb, wts["conv_w"], wts["hg_norm_w"], bl=bl, seq=seq)
    wts.update(late_fn(y))
    merged, xhat1, rstd1, x1b = _merge_fwd(y, p, x0, wts["w_branch"], wts["w_o"], wts["b_gate"], wts["ln1_g"], wts["ln1_b"],
                                           alpha=alpha)
    a, xhat2, rstd2, x2, x2b = _mlp_fwd(xhat1, x1b, wts["ln1_g"], wts["ln1_b"], wts["w_up"], wts["w_down"], wts["ln2_g"],
                                        wts["ln2_b"], alpha=alpha)
    saved = dict(x0b=x0b, p=p, mk=mk, mv=mv, y=y, st=st, opre=opre, merged=merged, xhat1=xhat1, rstd1=rstd1, x1b=x1b, a=a,
                 xhat2=xhat2, rstd2=rstd2)
    return x2, x2b, saved, wts


def _relu2_bf16(a):
    return jnp.square(jnp.maximum(a.astype(F32), 0.0)).astype(BF16)


def _mlp_bwd(dz2, dz2b, sv, wts, *, alpha, deps=()):
    g = {}
    da = _matmul("mlp_da", dz2b, wts["w_down"], mode="nt", out_dtype=BF16, tm=512, tn=wts["w_down"].shape[0], deps=deps,
                 epi_fn=lambda acc, a: (acc * (2.0 * jnp.maximum(a.astype(F32), 0.0)),), epi_extra=(sv["a"],))
    g["w_down"] = _matmul_tn("grad_w_down", sv["a"], dz2b, a_fn=_relu2_bf16, out_dtype=BF16, tt=2048)
    g["w_up"] = _matmul_tn("grad_w_up", sv["x1b"], da, out_dtype=BF16, tt=2048)
    dx1 = _matmul("mlp_dx", da, wts["w_up"], mode="nt", epi_fn=lambda acc, dz: (acc + alpha * dz,), epi_extra=(dz2,),
                  tm=512, tk=4096)
    dz1, dz1b, dg1, db1 = _ln_bwd(dx1, sv["xhat1"], sv["rstd1"], wts["ln1_g"])
    g["ln1_g"], g["ln1_b"] = dg1[0:1], db1[0:1]
    return dz1, dz1b, g


def _mix_bwd(dz1, dz1b, sv, mem2, lb, wts, *, bl, seq, alpha, send, below=None, deps=()):
    d = dz1.shape[1]
    g = {}
    g["w_o"] = _matmul_tn("grad_w_o", sv["merged"], dz1b, out_dtype=BF16, tt=2048, deps=deps)
    dr, dp, dy, dbg = _merge_bwd(dz1b, sv["p"], sv["y"], wts["w_branch"], wts["w_o"], wts["b_gate"])
    g["b_gate"] = dbg[0:1]
    g["w_branch"] = jnp.concatenate(
        [_matmul_tn("grad_w_branch", sv["y"], dr, a_cols=(i * W, W), b_cols=(i * d, d), out_dtype=BF16, tt=4096)
         for i in range(N_BRANCH)],
        axis=0)
    token = send(("w_o", "w_branch"), g)
    dp, dmk, dmv, dcw, dnw, dlb = _mixer_bwd(sv["p"], dy, dp, sv["st"], sv["opre"], sv["mk"], sv["mv"], lb,
                                              wts["conv_w"], wts["hg_norm_w"], bl=bl, seq=seq, deps=(token,))
    g["conv_w"], g["hg_norm_w"], g["lb"] = dcw[0:CONV_K], dnw[0:1], dlb[0:1]
    g["w_mem_k"] = _matmul_tn("grad_w_mem_k", mem2, dmk, out_dtype=BF16)
    g["w_mem_v"] = _matmul_tn("grad_w_mem_v", mem2, dmv, out_dtype=BF16)
    g["w_in"] = _matmul_tn("grad_w_in", sv["x0b"], dp, out_dtype=BF16, tt=2048)
    token = send(("w_in", "w_mem_k", "w_mem_v", "conv_w"), g)
    dx0 = _matmul("proj_in_dx", dp, wts["w_in"], mode="nt", epi_fn=lambda acc, dz: (acc + alpha * dz,), epi_extra=(dz1,),
                  tm=512, tk=dp.shape[1], deps=(token,))
    return (dx0 if below is None else _ln_bwd(dx0, *below)), g


N_CHIPS = 4
MESH_IDS = pl.DeviceIdType.MESH


def _axis_slice(ref, axis, start, size):
    idx = [slice(None)] * len(ref.shape)
    idx[axis] = pl.ds(start, size)
    return ref.at[tuple(idx)]


def _chip_exchange(name, items):
    n = len(items)
    out_shapes, meta = [], []
    for arr, kind, axis in items:
        shp = list(arr.shape)
        if kind == "gather":
            per = shp[axis]
            shp[axis] = per * N_CHIPS
            out_shapes.append(jax.ShapeDtypeStruct(tuple(shp), arr.dtype))
        elif kind == "scatter":
            per = shp[axis] // N_CHIPS
            shp[axis] = per
            out_shapes.append(jax.ShapeDtypeStruct((N_CHIPS, *shp), arr.dtype))
        else:
            per = None
            out_shapes.append(jax.ShapeDtypeStruct((N_CHIPS, *shp), arr.dtype))
        meta.append((kind, axis, per))

    def body(*refs):
        ins, outs = refs[:n], refs[n:2 * n]
        send_sems, recv_sems, local_sems = refs[2 * n:]
        x, y, c = lax.axis_index("x"), lax.axis_index("y"), lax.axis_index("c")
        me = 2 * x + y
        peers = [(1 - x, y), (x, 1 - y), (1 - x, 1 - y)]

        def src_for(t, chip):
            kind, axis, per = meta[t]
            return _axis_slice(ins[t], axis, chip * per, per) if kind == "scatter" else ins[t]

        def dst_from(t, chip):
            kind, axis, per = meta[t]
            return _axis_slice(outs[t], axis, chip * per, per) if kind == "gather" else outs[t].at[chip]

        def remote(t, k):
            px, py = peers[k]
            return pltpu.make_async_remote_copy(
                src_ref=src_for(t, 2 * px + py), dst_ref=dst_from(t, me), send_sem=send_sems.at[t * 3 + k],
                recv_sem=recv_sems.at[t * 3 + k], device_id=(px, py, c), device_id_type=MESH_IDS)

        def arrival(t, k):
            px, py = peers[k]
            return pltpu.make_async_remote_copy(
                src_ref=src_for(t, me), dst_ref=dst_from(t, 2 * px + py), send_sem=send_sems.at[t * 3 + k],
                recv_sem=recv_sems.at[t * 3 + k], device_id=(px, py, c), device_id_type=MESH_IDS)

        sends = [remote(t, k) for t in range(n) for k in range(3)]
        for cp in sends:
            cp.start()
        own = [pltpu.make_async_copy(src_for(t, me), dst_from(t, me), local_sems.at[t]) for t in range(n)]
        for cp in own:
            cp.start()
        for t in range(n):
            for k in range(3):
                arrival(t, k).wait_recv()
        for cp in sends:
            cp.wait_send()
        for cp in own:
            cp.wait()

    any_spec = pl.BlockSpec(memory_space=pl.ANY)
    return pl.pallas_call(
        body,
        name=name,
        in_specs=[any_spec] * n,
        out_specs=[any_spec] * n,
        out_shape=out_shapes,
        scratch_shapes=[pltpu.SemaphoreType.DMA((3 * n,)), pltpu.SemaphoreType.DMA((3 * n,)), pltpu.SemaphoreType.DMA((n,))],
        compiler_params=pltpu.CompilerParams(has_side_effects=True),
    )(*[a for a, _, _ in items])


HBM_SPEC = pl.BlockSpec(memory_space=pltpu.HBM)
SEM_SPEC = pl.BlockSpec(memory_space=pltpu.SEMAPHORE)
N_PEERS = N_CHIPS - 1


def _my_chip():
    return (2 * lax.axis_index("x") + lax.axis_index("y")).astype(jnp.int32).reshape(1)


def _own_block_spec(r, c, axis, tr):
    if axis == 1:
        return pl.BlockSpec((tr, c), lambda i, me: (i, me[0]))
    return pl.BlockSpec((tr, c), lambda i, me: (me[0] * (r // tr) + i, 0))


def _place_shard(name, shard, axis, me):
    r, c = shard.shape
    tr = _row_block(r, c, shard.dtype.itemsize)
    shp = (r, c * N_CHIPS) if axis == 1 else (r * N_CHIPS, c)

    def body(me_ref, s_ref, o_ref):
        del me_ref
        o_ref[...] = s_ref[...]

    return pl.pallas_call(
        body, name=name,
        grid_spec=pltpu.PrefetchScalarGridSpec(
            num_scalar_prefetch=1, grid=(r // tr,),
            in_specs=[pl.BlockSpec((tr, c), lambda i, me: (i, 0))], out_specs=_own_block_spec(r, c, axis, tr)),
        out_shape=jax.ShapeDtypeStruct(shp, shard.dtype),
        compiler_params=_cparams(("parallel",)),
    )(me, shard)


class _Split:
    def __init__(self, name, items):
        self.name, self.n = name, len(items)
        self.srcs = [a for a, _, _ in items]
        self.meta, self.land_shapes = [], []
        for arr, kind, axis in items:
            shp = list(arr.shape)
            if kind == "gather":
                per = shp[axis]
                shp[axis] = per * N_CHIPS
                self.land_shapes.append(jax.ShapeDtypeStruct(tuple(shp), arr.dtype))
            else:
                per = shp[axis] // N_CHIPS
                shp[axis] = per
                self.land_shapes.append(jax.ShapeDtypeStruct((N_PEERS, *shp), arr.dtype))
            self.meta.append((kind, axis, per))

    def _src(self, ins, t, chip):
        kind, axis, per = self.meta[t]
        return _axis_slice(ins[t], axis, chip * per, per) if kind == "scatter" else ins[t]

    def _dst(self, lands, t, chip, slot):
        kind, axis, per = self.meta[t]
        return _axis_slice(lands[t], axis, chip * per, per) if kind == "gather" else lands[t].at[slot]

    def landing_zones(self, me):
        return [_place_shard(self.name + "_own", src, axis, me) if kind == "gather" else lax.empty(ls.shape, ls.dtype)
                for src, ls, (kind, axis, _) in zip(self.srcs, self.land_shapes, self.meta)]

    def _copies(self, ins, lands, send_sems, recv_sems, arrivals):
        x, y, c = lax.axis_index("x"), lax.axis_index("y"), lax.axis_index("c")
        me = 2 * x + y
        peers = [(1 - x, y), (x, 1 - y), (1 - x, 1 - y)]
        res = []
        for t in range(self.n):
            for k, (px, py) in enumerate(peers):
                theirs = 2 * px + py
                sems = dict(send_sem=send_sems.at[t * N_PEERS + k], recv_sem=recv_sems.at[t * N_PEERS + k],
                            device_id=(px, py, c), device_id_type=MESH_IDS)
                if arrivals:
                    res.append(pltpu.make_async_remote_copy(src_ref=self._src(ins, t, me), dst_ref=self._dst(lands, t, theirs, k), **sems))
                else:
                    res.append(pltpu.make_async_remote_copy(src_ref=self._src(ins, t, theirs), dst_ref=self._dst(lands, t, me, k), **sems))
        return res

    def start(self, lands, deps=()):
        n, nd = self.n, len(deps)

        def body(*refs):
            ins, lnd = refs[:n], refs[n:2 * n]
            send_sems, recv_sems = refs[2 * n + nd], refs[2 * n + nd + 1]
            token = refs[-1]
            for cp in self._copies(ins, lnd, send_sems, recv_sems, arrivals=False):
                cp.start()
            token[...] = jnp.zeros_like(token)

        hbm = lambda a: pltpu.HBM(a.shape, a.dtype)
        res = pl.pallas_call(
            body, name=self.name + "_start",
            in_specs=[HBM_SPEC] * (2 * n) + [ANY_SPEC] * nd,
            out_specs=[SEM_SPEC, SEM_SPEC] + [HBM_SPEC] * (2 * n) + [pl.BlockSpec(memory_space=pltpu.VMEM)],
            out_shape=[pltpu.SemaphoreType.DMA((N_PEERS * n,)), pltpu.SemaphoreType.DMA((N_PEERS * n,))]
            + [hbm(a) for a in self.srcs] + [hbm(a) for a in self.land_shapes] + [jax.ShapeDtypeStruct((8, 128), F32)],
            input_output_aliases={i: 2 + i for i in range(2 * n)},
            compiler_params=pltpu.CompilerParams(has_side_effects=pltpu.SideEffectType.DATAFLOW_SIDE_EFFECTING),
        )(*[pltpu.with_memory_space_constraint(a, pltpu.HBM) for a in self.srcs],
          *[pltpu.with_memory_space_constraint(a, pltpu.HBM) for a in lands], *deps)
        return res[:-1], res[-1]

    def wait(self, state, after):
        n = self.n
        after = tuple(after) if isinstance(after, (tuple, list)) else (after,)
        send_sems, recv_sems = state[0], state[1]
        srcs, lands = state[2:2 + n], state[2 + n:2 + 2 * n]

        def body(*refs):
            ins, lnd = refs[:n], refs[n:2 * n]
            s_sems, r_sems = refs[2 * n], refs[2 * n + 1]
            for cp in self._copies(ins, lnd, s_sems, r_sems, arrivals=True):
                cp.wait_recv()
            for cp in self._copies(ins, lnd, s_sems, r_sems, arrivals=False):
                cp.wait_send()

        hbm = lambda a: pltpu.HBM(a.shape, a.dtype)
        res = pl.pallas_call(
            body, name=self.name + "_wait",
            in_specs=[HBM_SPEC] * (2 * n) + [SEM_SPEC, SEM_SPEC] + [ANY_SPEC] * len(after),
            out_specs=[HBM_SPEC] * (2 * n),
            out_shape=[hbm(a) for a in self.srcs] + [hbm(a) for a in self.land_shapes],
            input_output_aliases={i: i for i in range(2 * n)},
            compiler_params=pltpu.CompilerParams(has_side_effects=pltpu.SideEffectType.DATAFLOW_SIDE_EFFECTING),
        )(*srcs, *lands, send_sems, recv_sems, *after)
        return res[:n], res[n:]


class _SiblingSplit:
    def __init__(self, name, arrays):
        self.name, self.n, self.arrays = name, len(arrays), list(arrays)

    def _copies(self, ins, lands, send_sems, recv_sems):
        sibling = (lax.axis_index("x"), lax.axis_index("y"), 1 - lax.axis_index("c"))
        return [pltpu.make_async_remote_copy(src_ref=ins[t], dst_ref=lands[t], send_sem=send_sems.at[t], recv_sem=recv_sems.at[t],
                                             device_id=sibling, device_id_type=MESH_IDS) for t in range(self.n)]

    def start(self, deps=()):
        n, nd = self.n, len(deps)

        def body(*refs):
            for cp in self._copies(refs[:n], refs[n:2 * n], refs[2 * n + nd], refs[2 * n + nd + 1]):
                cp.start()
            refs[-1][...] = jnp.zeros_like(refs[-1])

        hbm = [pltpu.HBM(a.shape, a.dtype) for a in self.arrays]
        res = pl.pallas_call(
            body, name=self.name + "_start",
            in_specs=[HBM_SPEC] * (2 * n) + [ANY_SPEC] * nd,
            out_specs=[SEM_SPEC, SEM_SPEC] + [HBM_SPEC] * (2 * n) + [pl.BlockSpec(memory_space=pltpu.VMEM)],
            out_shape=[pltpu.SemaphoreType.DMA((n,)), pltpu.SemaphoreType.DMA((n,))] + hbm + hbm + [jax.ShapeDtypeStruct((8, 128), F32)],
            input_output_aliases={i: 2 + i for i in range(2 * n)},
            compiler_params=pltpu.CompilerParams(has_side_effects=pltpu.SideEffectType.DATAFLOW_SIDE_EFFECTING),
        )(*[pltpu.with_memory_space_constraint(a, pltpu.HBM) for a in self.arrays],
          *[pltpu.with_memory_space_constraint(lax.empty(a.shape, a.dtype), pltpu.HBM) for a in self.arrays], *deps)
        return res[:-1], res[-1]

    def wait(self, state, after):
        n = self.n
        after = tuple(after) if isinstance(after, (tuple, list)) else (after,)

        def body(*refs):
            for cp in self._copies(refs[:n], refs[n:2 * n], refs[2 * n], refs[2 * n + 1]):
                cp.wait()

        hbm = [pltpu.HBM(a.shape, a.dtype) for a in self.arrays]
        res = pl.pallas_call(
            body, name=self.name + "_wait",
            in_specs=[HBM_SPEC] * (2 * n) + [SEM_SPEC, SEM_SPEC] + [ANY_SPEC] * len(after),
            out_specs=[HBM_SPEC] * (2 * n),
            out_shape=hbm + hbm,
            input_output_aliases={i: i for i in range(2 * n)},
            compiler_params=pltpu.CompilerParams(has_side_effects=pltpu.SideEffectType.DATAFLOW_SIDE_EFFECTING),
        )(*state[2:2 + 2 * n], state[0], state[1], *after)
        return res[:n], res[n:]


def _sibling_swap(name, arrays):
    n = len(arrays)

    def body(*refs):
        ins, outs = refs[:n], refs[n:2 * n]
        send_sems, recv_sems = refs[2 * n:]
        sibling = (lax.axis_index("x"), lax.axis_index("y"), 1 - lax.axis_index("c"))
        copies = [pltpu.make_async_remote_copy(src_ref=ins[t], dst_ref=outs[t], send_sem=send_sems.at[t], recv_sem=recv_sems.at[t],
                                               device_id=sibling, device_id_type=MESH_IDS) for t in range(n)]
        for cp in copies:
            cp.start()
        for cp in copies:
            cp.wait()

    any_spec = pl.BlockSpec(memory_space=pl.ANY)
    return pl.pallas_call(
        body,
        name=name,
        in_specs=[any_spec] * n,
        out_specs=[any_spec] * n,
        out_shape=[jax.ShapeDtypeStruct(a.shape, a.dtype) for a in arrays],
        scratch_shapes=[pltpu.SemaphoreType.DMA((n,)), pltpu.SemaphoreType.DMA((n,))],
        compiler_params=pltpu.CompilerParams(has_side_effects=True),
    )(*arrays)


def _row_block(r, c, itemsize=4, target=1 << 20):
    if r % 8 != 0:
        return r
    best = 8
    for tr in range(8, r + 1, 8):
        if r % tr == 0 and tr * c * itemsize <= target:
            best = tr
    return best


def _sum_chips_into(parts, stacked, layer):
    _, r, c = parts.shape
    tr = _row_block(r, c)

    def body(p_ref, s_ref, o_ref):
        del s_ref
        o_ref[...] = ((p_ref[0] + p_ref[1]) + p_ref[2]) + p_ref[3]

    return pl.pallas_call(
        body,
        name="sum_chips",
        grid=(r // tr,),
        in_specs=[pl.BlockSpec((N_CHIPS, tr, c), lambda i: (0, i, 0)), pl.BlockSpec(memory_space=pl.ANY)],
        out_specs=pl.BlockSpec((None, tr, c), lambda i: (layer, i, 0)),
        out_shape=jax.ShapeDtypeStruct(stacked.shape, stacked.dtype),
        input_output_aliases={1: 0},
        compiler_params=_cparams(("parallel",)),
    )(parts, stacked)


def _sum_own_and_peers(me, g, axis, landed):
    _, r, c = landed.shape
    tr = _row_block(r, c)

    def body(me_ref, g_ref, p_ref, o_ref):
        del me_ref
        o_ref[...] = ((g_ref[...].astype(F32) + p_ref[0].astype(F32)) + p_ref[1].astype(F32)) + p_ref[2].astype(F32)

    return pl.pallas_call(
        body, name="sum_chips_own",
        grid_spec=pltpu.PrefetchScalarGridSpec(
            num_scalar_prefetch=1, grid=(r // tr,),
            in_specs=[_own_block_spec(r, c, axis, tr), pl.BlockSpec((N_PEERS, tr, c), lambda i, me: (0, i, 0))],
            out_specs=pl.BlockSpec((tr, c), lambda i, me: (i, 0))),
        out_shape=jax.ShapeDtypeStruct((r, c), F32),
        compiler_params=_cparams(("parallel",)),
    )(me, g, landed)


def _adamw_math(w, m, v, g):
    m_new = ADAM_B1 * m + (1.0 - ADAM_B1) * g
    v_new = ADAM_B2 * v + (1.0 - ADAM_B2) * jnp.square(g)
    m_hat = m_new / (1.0 - ADAM_B1 ** ADAM_STEP)
    v_hat = v_new / (1.0 - ADAM_B2 ** ADAM_STEP)
    return -ADAM_LR * (m_hat / (jnp.sqrt(v_hat) + ADAM_EPS) + ADAM_WD * w), m_new, v_new


def _adamw(w, m, v, g_a, g_b):
    L, r, c = w.shape
    tr = _row_block(r, c, target=1 << 19)

    def body(w_ref, m_ref, v_ref, ga_ref, gb_ref, g_ref, d_ref, nm_ref, nv_ref):
        g = ga_ref[...] + gb_ref[...]
        g_ref[...] = g
        d_ref[...], nm_ref[...], nv_ref[...] = _adamw_math(w_ref[...], m_ref[...], v_ref[...], g)

    spec = pl.BlockSpec((None, tr, c), lambda l, i: (l, i, 0))
    return pl.pallas_call(
        body,
        name="adamw",
        grid=(L, r // tr),
        in_specs=[spec] * 5,
        out_specs=[spec] * 4,
        out_shape=[jax.ShapeDtypeStruct(w.shape, F32)] * 4,
        compiler_params=_cparams(("parallel", "parallel")),
    )(w, m, v, g_a, g_b)


def _adamw_layer(w, m, v, g_a, g_b, layer, outs):
    L, r, c = w.shape
    tr = _row_block(r, c, target=1 << 19)
    n_prev = 0 if outs is None else 4

    def body(w_ref, m_ref, v_ref, ga_ref, gb_ref, *rest):
        g_ref, d_ref, nm_ref, nv_ref = rest[n_prev:]
        g = ga_ref[...] + gb_ref[...]
        g_ref[...] = g
        d_ref[...], nm_ref[...], nv_ref[...] = _adamw_math(w_ref[...], m_ref[...], v_ref[...], g)

    at_layer = pl.BlockSpec((None, tr, c), lambda i: (layer, i, 0))
    flat = pl.BlockSpec((tr, c), lambda i: (i, 0))
    return pl.pallas_call(
        body,
        name="adamw_layer",
        grid=(r // tr,),
        in_specs=[at_layer] * 3 + [flat] * 2 + [ANY_SPEC] * n_prev,
        out_specs=[at_layer] * 4,
        out_shape=[jax.ShapeDtypeStruct(w.shape, F32)] * 4,
        input_output_aliases={5 + k: k for k in range(n_prev)},
        compiler_params=_cparams(("parallel",)),
    )(w, m, v, g_a, g_b, *(outs or ()))


SHARDED = (("w_in", 1), ("conv_w", 1), ("w_mem_k", 0), ("w_mem_v", 0), ("w_branch", 1), ("w_o", 0), ("w_up", 1), ("w_down", 0))
SMALL = ("lower_bounds", "hg_norm_w", "b_gate", "ln1_g", "ln1_b", "ln2_g", "ln2_b")
WEIGHT_ORDER = ("lower_bounds", "w_in", "conv_w", "hg_norm_w", "w_mem_k", "w_mem_v", "w_branch", "b_gate", "w_o", "ln1_g", "ln1_b",
                "w_up", "w_down", "ln2_g", "ln2_b")


def kernel(x, mem, lower_bounds, w_in, conv_w, hg_norm_w, w_mem_k, w_mem_v, w_branch, b_gate, w_o, ln1_g, ln1_b, w_up, w_down, ln2_g, ln2_b, loss_target, m_lower_bounds, m_w_in, m_conv_w, m_hg_norm_w, m_w_mem_k, m_w_mem_v, m_w_branch, m_b_gate, m_w_o, m_ln1_g, m_ln1_b, m_w_up, m_w_down, m_ln2_g, m_ln2_b, v_lower_bounds, v_w_in, v_conv_w, v_hg_norm_w, v_w_mem_k, v_w_mem_v, v_w_branch, v_b_gate, v_w_o, v_ln1_g, v_ln1_b, v_w_up, v_w_down, v_ln2_g, v_ln2_b):
    bl, seq, d = x.shape
    depth = w_in.shape[0]
    weights = dict(lower_bounds=lower_bounds, w_in=w_in, conv_w=conv_w, hg_norm_w=hg_norm_w, w_mem_k=w_mem_k, w_mem_v=w_mem_v,
                   w_branch=w_branch, b_gate=b_gate, w_o=w_o, ln1_g=ln1_g, ln1_b=ln1_b, w_up=w_up, w_down=w_down, ln2_g=ln2_g, ln2_b=ln2_b)
    mom_m = dict(lower_bounds=m_lower_bounds, w_in=m_w_in, conv_w=m_conv_w, hg_norm_w=m_hg_norm_w, w_mem_k=m_w_mem_k, w_mem_v=m_w_mem_v,
                 w_branch=m_w_branch, b_gate=m_b_gate, w_o=m_w_o, ln1_g=m_ln1_g, ln1_b=m_ln1_b, w_up=m_w_up, w_down=m_w_down,
                 ln2_g=m_ln2_g, ln2_b=m_ln2_b)
    mom_v = dict(lower_bounds=v_lower_bounds, w_in=v_w_in, conv_w=v_conv_w, hg_norm_w=v_hg_norm_w, w_mem_k=v_w_mem_k, w_mem_v=v_w_mem_v,
                 w_branch=v_w_branch, b_gate=v_b_gate, w_o=v_w_o, ln1_g=v_ln1_g, ln1_b=v_ln1_b, w_up=v_w_up, w_down=v_w_down,
                 ln2_g=v_ln2_g, ln2_b=v_ln2_b)

    def shard2d(name, l):
        w = weights[name][l]
        if name == "w_branch":
            return w.reshape(N_BRANCH * W, w.shape[-1]).astype(BF16)
        return w if name == "conv_w" else w.astype(BF16)

    me = _my_chip()

    shard_axis = dict(SHARDED)

    def prepare_exchange(name, kind, items):
        ex = _Split(name, [(arr, kind, shard_axis[nm]) for nm, arr in items])
        return ex, ex.landing_zones(me), [nm for nm, _ in items]

    def launch(prepared, deps=()):
        ex, lands, names = prepared
        state, token = ex.start(lands, deps)
        return ex, state, names, token

    def start_exchange(name, kind, items, deps=()):
        return launch(prepare_exchange(name, kind, items), deps)

    def prepare_gathers(l):
        groups = (("in", ("w_in",)), ("mix", ("conv_w", "w_mem_k", "w_mem_v")), ("rest", ("w_branch", "w_o", "w_up", "w_down")))
        return tuple(prepare_exchange(f"gather_{tag}_l{l}", "gather", [(nm, shard2d(nm, l)) for nm in names]) for tag, names in groups)

    def start_gathers(prepared, deps=()):
        started = []
        for prep in prepared:
            started.append(launch(prep, deps))
            deps = (started[-1][3],)
        return tuple(started)

    def gathered(pend, after):
        ex, state, names, _ = pend
        return dict(zip(names, ex.wait(state, after=after)[1]))

    pending = start_gathers(prepare_gathers(0))
    tokens = tuple(pend[3] for pend in pending)
    tokens, x, mem, loss_target, weights, mom_m, mom_v = lax.optimization_barrier((tokens, x, mem, loss_target, weights, mom_m, mom_v))
    pending = tuple((*pend[:3], tok) for pend, tok in zip(pending, tokens))
    lower_bounds = weights["lower_bounds"]

    x2d, mem2, t2d = x.reshape(bl * seq, d), mem.reshape(-1, d), loss_target.reshape(bl * seq, d)
    alpha = (2.0 * depth) ** 0.25
    soft, lb_all = _lower_bounds_fwd(lower_bounds)

    prepared = [None] + [prepare_gathers(l) for l in range(1, depth)]
    early = [x2d.astype(BF16), lb_all] + [z for prep in prepared[1:] for _, lands, _ in prep for z in lands]

    h, hb, saved, layer_wts = x2d, early[0], [], []
    for l in range(depth):
        first, mix, rest = pending
        w_in_l = gathered(first, early if l == 0 else h)["w_in"]

        def mix_fn(after, l=l, mix=mix):
            return dict(gathered(mix, after), hg_norm_w=weights["hg_norm_w"][l][None, :])

        def late_fn(after, l=l, rest=rest):
            wts = gathered(rest, after)
            for name in ("b_gate", "ln1_g", "ln1_b", "ln2_g", "ln2_b"):
                wts[name] = weights[name][l][None, :]
            return wts

        deps = (rest[3],)
        if l + 1 < depth:
            pending = start_gathers(prepared[l + 1], (w_in_l, rest[3]))
            deps += tuple(pend[3] for pend in pending)
        h, hb, sv, wts = _layer_fwd(h, hb, mem2, lb_all[l:l + 1], w_in_l, mix_fn, late_fn, bl=bl, seq=seq, alpha=alpha, deps=deps)
        saved.append(sv)
        layer_wts.append(wts)
    loss, dz2, dz2b, dg2, db2 = _loss_head(h, t2d, saved[-1]["xhat2"], saved[-1]["rstd2"], layer_wts[-1]["ln2_g"])

    shape3 = {name: (depth, weights[name].size // (depth * weights[name].shape[-1]), weights[name].shape[-1]) for name, _ in SHARDED}
    partial = [dict() for _ in range(depth)]
    smalls = [None] * depth
    outs = {name: None for name, _ in SHARDED}

    def finish_reduce(pend, l, after):
        ex, state, names, _ = pend
        sent, got = ex.wait(state, after=after)
        for nm, g_full, landed in zip(names, sent, got):
            partial[l][nm] = _sum_own_and_peers(me, g_full, shard_axis[nm], landed)

    names_sharded = [name for name, _ in SHARDED]

    def start_swap(l):
        swap = _SiblingSplit(f"swap_partials_l{l}", [partial[l][nm] for nm in names_sharded])
        state, token = swap.start()
        return swap, state, token

    def optimizer_step(l, pend, after):
        swap, state, _ = pend
        mine, theirs = swap.wait(state, after)
        for nm, own, other in zip(names_sharded, mine, theirs):
            outs[nm] = _adamw_layer(weights[nm].reshape(shape3[nm]), mom_m[nm].reshape(shape3[nm]), mom_v[nm].reshape(shape3[nm]),
                                    own, other, l, outs[nm])
        return tuple(outs[nm][0] for nm in names_sharded)

    pending_mix, pending_swap, deps = [], None, ()
    for l in reversed(range(depth)):
        dz1, dz1b, g_mlp = _mlp_bwd(dz2, dz2b, saved[l], layer_wts[l], alpha=alpha, deps=deps)
        g_mlp["ln2_g"], g_mlp["ln2_b"] = dg2[0:1], db2[0:1]
        pending_mlp = start_exchange(f"reduce_mlp_l{l}", "scatter", [(nm, g_mlp[nm]) for nm in ("w_up", "w_down")])
        deps = (pending_mlp[3],)
        if pending_mix:
            for pend in pending_mix:
                finish_reduce(pend, l + 1, dz1)
            pending_swap = start_swap(l + 1)
            deps += (pending_swap[2],)
        pending_mix = []

        def send(names, g, l=l, pending_mix=pending_mix):
            pend = start_exchange(f"reduce_{names[0]}_l{l}", "scatter", [(nm, g[nm]) for nm in names])
            pending_mix.append(pend)
            return pend[3]

        below = (saved[l - 1]["xhat2"], saved[l - 1]["rstd2"], layer_wts[l - 1]["ln2_g"]) if l > 0 else None
        out, g = _mix_bwd(dz1, dz1b, saved[l], mem2, lb_all[l:l + 1], layer_wts[l], bl=bl, seq=seq, alpha=alpha, send=send,
                          below=below, deps=deps)
        if l > 0:
            dz2, dz2b, dg2, db2 = out
        else:
            dh = out
        finish_reduce(pending_mlp, l, out[0] if l > 0 else out)
        deps = ()
        if pending_swap is not None:
            deps = optimizer_step(l + 1, pending_swap, out[0] if l > 0 else out)
            pending_swap = None
        g.update(g_mlp, lower_bounds=g["lb"])
        smalls[l] = jnp.concatenate([g[nm] for nm in SMALL], axis=1)
    small_parts = _chip_exchange("reduce_small", [(jnp.stack(smalls), "bcast", 0)])[0]
    small_sum = _sum_chips_into(small_parts.reshape(N_CHIPS, depth, -1), jnp.zeros((1, depth, small_parts.shape[-1]), F32), 0)
    small_sum = small_sum.reshape(depth, 1, -1)
    small_theirs = _sibling_swap("swap_small", [small_sum])[0]
    for pend in pending_mix:
        finish_reduce(pend, 0, small_theirs)
    optimizer_step(0, start_swap(0), small_theirs)

    outs = {name: [r.reshape(weights[name].shape) for r in res] for name, res in outs.items()}
    off = 0
    for name in SMALL:
        n = weights[name].shape[1]
        mine, other = small_sum[:, :, off:off + n], small_theirs[:, :, off:off + n]
        off += n
        if name == "lower_bounds":
            mine = _lower_bounds_bwd(soft, mine[:, 0, :])[:, None, :]
            other = _lower_bounds_bwd(soft, other[:, 0, :])[:, None, :]
        shp = (depth, 1, n)
        res = _adamw(weights[name].reshape(shp), mom_m[name].reshape(shp), mom_v[name].reshape(shp), mine, other)
        outs[name] = [r.reshape(weights[name].shape) for r in res]
    assert off == small_sum.shape[-1]

    total_loss = lax.psum(loss[0, 0], ("x", "y", "c"))
    result = [total_loss, dh.reshape(bl, seq, d)]
    for k in range(4):
        result += [outs[name][k] for name in WEIGHT_ORDER]
    return tuple(result)
```

```python
import functools

import jax
import jax.numpy as jnp
from jax import lax
from jax.experimental import pallas as pl
from jax.experimental.pallas import tpu as pltpu

F32 = jnp.float32
BF16 = jnp.bfloat16

HG_HEADS = 4
HG_F = 128
HG_CHUNK = 32
MEM_HEADS = 4
MEM_HEAD_DIM = 128
BRANCH_WIDTH = 512
N_BRANCH = 3
CONV_K = 3
LN_EPS = 1e-5
RMS_EPS = 1e-6
ADAM_LR = 0.001
ADAM_B1 = 0.9
ADAM_B2 = 0.999
ADAM_EPS = 1e-08
ADAM_WD = 0.01
ADAM_STEP = 10

VMEM_LIMIT = 48 * 1024 * 1024


def _cparams(sem):
    return pltpu.CompilerParams(dimension_semantics=sem, vmem_limit_bytes=VMEM_LIMIT)


def _dot(a, b, dims):
    return lax.dot_general(a, b, (dims, ((), ())), preferred_element_type=F32)


NN = ((1,), (0,))
NT = ((1,), (1,))
TN = ((0,), (0,))


def _pick(n, pref):
    for t in pref:
        if n % t == 0:
            return t
    return n


ANY_SPEC = pl.BlockSpec(memory_space=pl.ANY)


def _matmul(name, a, b, *, mode, out_dtype=F32, a_fn=None, a_extra=(), epi_fn=None, epi_extra=(), n_out=1, out_kinds=None,
            tm=512, tn=1024, tk=1024, deps=()):
    M, K = a.shape
    N = b.shape[1] if mode == "nn" else b.shape[0]
    tm, tn, tk = _pick(M, (tm, 256, 128, 8)), _pick(N, (tn, 896, 512, 256, 128)), _pick(K, (tk, 512, 256, 128))
    nk = K // tk
    n_ax, n_ex = len(a_extra), len(epi_extra)
    n_in = 2 + n_ax + n_ex + len(deps)
    out_dtypes = out_dtype if isinstance(out_dtype, (tuple, list)) else (out_dtype,) * n_out
    out_kinds = out_kinds or ("tile",) * n_out

    def body(*refs):
        a_ref, b_ref = refs[0], refs[1]
        ax_refs = refs[2:2 + n_ax]
        ex_refs = refs[2 + n_ax:2 + n_ax + n_ex]
        o_refs = refs[n_in:n_in + n_out]
        at = a_ref[...]
        at = a_fn(at, *[r[...] for r in ax_refs]) if a_fn is not None else at.astype(BF16)
        part = _dot(at, b_ref[...].astype(BF16), NN if mode == "nn" else NT)

        def finish(acc):
            outs = epi_fn(acc, *[r[...] for r in ex_refs]) if epi_fn is not None else (acc,)
            for o_ref, o, kind in zip(o_refs, outs, out_kinds):
                if kind == "rowsum":
                    @pl.when(pl.program_id(1) == 0)
                    def _(o_ref=o_ref):
                        o_ref[...] = jnp.zeros_like(o_ref)

                    o_ref[0:1, :] += o
                else:
                    o_ref[...] = o.astype(o_ref.dtype)

        if nk == 1:
            finish(part)
            return
        acc_ref = refs[-1]
        k = pl.program_id(2)

        @pl.when(k == 0)
        def _():
            acc_ref[...] = part

        @pl.when(jnp.logical_and(k > 0, k < nk - 1))
        def _():
            acc_ref[...] += part

        @pl.when(k == nk - 1)
        def _():
            finish(acc_ref[...] + part)

    b_mode = dict(pipeline_mode=pl.Buffered(1)) if (nk == 1 and N == tn) else {}
    in_specs = [pl.BlockSpec((tm, tk), lambda j, i, k: (i, k)),
                pl.BlockSpec((tk, tn), lambda j, i, k: (k, j), **b_mode) if mode == "nn"
                else pl.BlockSpec((tn, tk), lambda j, i, k: (j, k), **b_mode)]
    in_specs += [pl.BlockSpec((1, tk), lambda j, i, k: (0, k)) for _ in a_extra]
    for e in epi_extra:
        if e.shape[0] == 1:
            in_specs.append(pl.BlockSpec((1, tn), lambda j, i, k: (0, j)))
        elif e.shape[1] == 1:
            in_specs.append(pl.BlockSpec((tm, 1), lambda j, i, k: (i, 0)))
        else:
            in_specs.append(pl.BlockSpec((tm, tn), lambda j, i, k: (i, j)))
    in_specs += [ANY_SPEC] * len(deps)
    out_specs, out_shapes = [], []
    for kind, dt in zip(out_kinds, out_dtypes):
        if kind == "col":
            out_specs.append(pl.BlockSpec((tm, 1), lambda j, i, k: (i, 0)))
            out_shapes.append(jax.ShapeDtypeStruct((M, 1), dt))
        elif kind == "rowsum":
            out_specs.append(pl.BlockSpec((8, tn), lambda j, i, k: (0, j)))
            out_shapes.append(jax.ShapeDtypeStruct((8, N), dt))
        else:
            out_specs.append(pl.BlockSpec((tm, tn), lambda j, i, k: (i, j)))
            out_shapes.append(jax.ShapeDtypeStruct((M, N), dt))
    out = pl.pallas_call(
        body,
        name=name,
        grid=(N // tn, M // tm, nk),
        in_specs=in_specs,
        out_specs=out_specs,
        out_shape=out_shapes,
        scratch_shapes=[pltpu.VMEM((tm, tn), F32)] if nk > 1 else [],
        compiler_params=_cparams(("arbitrary", "arbitrary", "arbitrary")),
    )(a, b, *a_extra, *epi_extra, *deps)
    return out[0] if n_out == 1 else out


def _matmul_tn(name, a, b, *, a_fn=None, a_extra=(), a_cols=None, b_cols=None, ta=1024, tb=1024, tt=1024, out_dtype=F32, deps=()):
    T = a.shape[0]
    a0, Ka = a_cols if a_cols is not None else (0, a.shape[1])
    b0, Nb = b_cols if b_cols is not None else (0, b.shape[1])
    ta, tb, tt = _pick(Ka, (ta, 512, 256, 128)), _pick(Nb, (tb, 896, 512, 256, 128)), _pick(T, (tt, 512, 256, 128))
    assert a0 % ta == 0 and b0 % tb == 0
    a0, b0 = a0 // ta, b0 // tb
    nt = T // tt
    n_ax = len(a_extra)

    def body(*refs):
        a_ref, b_ref = refs[0], refs[1]
        ax_refs = refs[2:2 + n_ax]
        o_ref = refs[2 + n_ax + len(deps)]
        acc_ref = refs[-1]
        t = pl.program_id(2)
        at = a_ref[...]
        at = a_fn(at, *[r[...] for r in ax_refs]) if a_fn is not None else at.astype(BF16)
        part = _dot(at, b_ref[...].astype(BF16), TN)

        @pl.when(t == 0)
        def _():
            acc_ref[...] = part

        @pl.when(jnp.logical_and(t > 0, t < nt - 1))
        def _():
            acc_ref[...] += part

        @pl.when(t == nt - 1)
        def _():
            o_ref[...] = (acc_ref[...] + part if nt > 1 else part).astype(o_ref.dtype)

    in_specs = [pl.BlockSpec((tt, ta), lambda i, j, t: (t, a0 + i)), pl.BlockSpec((tt, tb), lambda i, j, t: (t, b0 + j))]
    in_specs += [pl.BlockSpec((1, ta), lambda i, j, t: (0, a0 + i)) for _ in a_extra]
    in_specs += [ANY_SPEC] * len(deps)
    return pl.pallas_call(
        body,
        name=name,
        grid=(Ka // ta, Nb // tb, nt),
        in_specs=in_specs,
        out_specs=pl.BlockSpec((ta, tb), lambda i, j, t: (i, j)),
        out_shape=jax.ShapeDtypeStruct((Ka, Nb), out_dtype),
        scratch_shapes=[pltpu.VMEM((ta, tb), F32)],
        compiler_params=_cparams(("parallel", "parallel", "arbitrary")),
    )(a, b, *a_extra, *deps)


W = BRANCH_WIDTH
C_CB, C_CC, C_CH, C_HQ, C_HF, C_HI, C_HG, C_MQ, N_MIX = 0, W, 2 * W, 3 * W, 4 * W, 5 * W, 6 * W, 7 * W, 8 * W
TS_MIX = 256
PREV_ROWS = 16
KEEP_NAMES = ("sq", "qs", "k", "sig", "f", "ea", "eb", "eq", "ek")


def _sigmoid(x):
    return jax.nn.sigmoid(x)


def _chunk_pos(shape):
    return lax.broadcasted_iota(jnp.int32, shape, 0) & (HG_CHUNK - 1)


def _seg_cumsum(x, pos):
    sh = 1
    while sh < HG_CHUNK:
        x = x + jnp.where(pos >= sh, pltpu.roll(x, sh, 0), 0.0)
        sh *= 2
    return x


def _seg_rev_cumsum(x, pos):
    n = x.shape[0]
    sh = 1
    while sh < HG_CHUNK:
        x = x + jnp.where(pos < HG_CHUNK - sh, pltpu.roll(x, n - sh, 0), 0.0)
        sh *= 2
    return x


def _chunk_mask(ts):
    r = lax.broadcasted_iota(jnp.int32, (ts, ts), 0)
    c = lax.broadcasted_iota(jnp.int32, (ts, ts), 1)
    return jnp.logical_and((r // HG_CHUNK) == (c // HG_CHUNK), c <= r)


def _hgrn_gates(p_ref, lb):
    q = p_ref[:, C_HQ:C_HQ + W].astype(F32)
    fl = p_ref[:, C_HF:C_HF + W].astype(F32)
    sig = _sigmoid(fl)
    f = lb + (1.0 - lb) * sig
    logf = jnp.log(f)
    k = (1.0 - lb) * _sigmoid(-fl)
    sq = _sigmoid(q)
    qs = q * sq
    return q, sq, qs, sig, f, logf, k


def _hgrn_decays(logf, bc_sc, ts):
    pos = _chunk_pos(logf.shape)
    bc = _seg_cumsum(logf, pos)
    bc_sc[...] = bc
    nc = ts // HG_CHUNK
    bref = jnp.concatenate(
        [jnp.broadcast_to(bc_sc[n * HG_CHUNK + HG_CHUNK // 2 - 1:n * HG_CHUNK + HG_CHUNK // 2, :], (HG_CHUNK, W)) for n in range(nc)], axis=0)
    blast = jnp.concatenate(
        [jnp.broadcast_to(bc_sc[(n + 1) * HG_CHUNK - 1:(n + 1) * HG_CHUNK, :], (HG_CHUNK, W)) for n in range(nc)], axis=0)
    return pos, bc, bref, blast


def _conv_shift_down(u, carry_ref, row):
    n = carry_ref.shape[0]
    last, before = carry_ref[n - 1:n, :], carry_ref[n - 2:n - 1, :]
    u1 = jnp.where(row == 0, last, pltpu.roll(u, 1, 0))
    u2 = jnp.where(row == 0, before, jnp.where(row == 1, last, pltpu.roll(u, 2, 0)))
    return u1, u2


def _attn_probs(qh, kh):
    s = _dot(qh, kh, NT) * (MEM_HEAD_DIM ** -0.5)
    e = jnp.exp(s - jnp.max(s, axis=-1, keepdims=True))
    return e / jnp.sum(e, axis=-1, keepdims=True)


def _mixer_fwd(p, mk, mv, lb, conv_w, norm_w, *, bl, seq):
    T = p.shape[0]
    ts = TS_MIX
    ns = seq // ts
    nc = ts // HG_CHUNK
    ml = mk.shape[0] // bl

    def body(p_ref, mk_ref, mv_ref, lb_ref, cw_ref, nw_ref, y_ref, st_ref, opre_ref, state_sc, carry_sc, bc_sc):
        @pl.when(pl.program_id(1) == 0)
        def _():
            state_sc[...] = jnp.zeros_like(state_sc)
            carry_sc[...] = jnp.zeros_like(carry_sc)

        cb, cc, ch = (p_ref[:, c0:c0 + W].astype(F32) for c0 in (C_CB, C_CC, C_CH))
        u = cc * ch
        row = lax.broadcasted_iota(jnp.int32, (ts, W), 0)
        u1, u2 = _conv_shift_down(u, carry_sc, row)
        yconv = u2 * cw_ref[0:1, :] + u1 * cw_ref[1:2, :] + u * cw_ref[2:3, :]
        y_ref[:, 0:W] = (cb * yconv).astype(BF16)
        carry_sc[...] = u[ts - 8:ts, :]

        lbv = lb_ref[...]
        _, _, qs, _, _, logf, k = _hgrn_gates(p_ref, lbv)
        pos, bc, bref, blast = _hgrn_decays(logf, bc_sc, ts)
        a_all = (qs * jnp.exp(bc - bref)).astype(BF16)
        bk_all = (k * jnp.exp(bref - bc)).astype(BF16)
        qin_all = (qs * jnp.exp(bc)).astype(BF16)
        kout_all = (k * jnp.exp(blast - bc)).astype(BF16)
        v_all = p_ref[:, C_HI:C_HI + W].astype(BF16)
        mask = _chunk_mask(ts)
        heads = [slice(h * HG_F, (h + 1) * HG_F) for h in range(HG_HEADS)]
        st = [state_sc[h] for h in range(HG_HEADS)]
        o_inter = [[] for _ in range(HG_HEADS)]
        for n in range(nc):
            rows = slice(n * HG_CHUNK, (n + 1) * HG_CHUNK)
            for h, hs in enumerate(heads):
                st_ref[n, h] = st[h]
                o_inter[h].append(_dot(qin_all[rows, hs], st[h].astype(BF16), NT))
                kv = _dot(v_all[rows, hs], kout_all[rows, hs], TN)
                decay = jnp.exp(bc_sc[(n + 1) * HG_CHUNK - 1:(n + 1) * HG_CHUNK, hs])
                st[h] = st[h] * decay + kv
        for h in range(HG_HEADS):
            state_sc[h] = st[h]
        scores = [_dot(a_all[:, hs], bk_all[:, hs], NT) for hs in heads]
        scores = [jnp.where(mask, s, 0.0).astype(BF16) for s in scores]
        outs = [_dot(scores[h], v_all[:, hs], NN) + jnp.concatenate(o_inter[h], axis=0) for h, hs in enumerate(heads)]
        for h, hs in enumerate(heads):
            o = outs[h]
            opre_ref[:, hs] = o
            on = o * lax.rsqrt(jnp.mean(o * o, axis=-1, keepdims=True) + RMS_EPS) * nw_ref[...]
            g = p_ref[:, C_HG + h * HG_F:C_HG + (h + 1) * HG_F].astype(F32)
            y_ref[:, W + h * HG_F:W + (h + 1) * HG_F] = (on * (g * _sigmoid(g))).astype(BF16)

        mheads = [slice(h * MEM_HEAD_DIM, (h + 1) * MEM_HEAD_DIM) for h in range(MEM_HEADS)]
        probs = [_attn_probs(p_ref[:, C_MQ + h * MEM_HEAD_DIM:C_MQ + (h + 1) * MEM_HEAD_DIM].astype(BF16), mk_ref[:, hs])
                 for h, hs in enumerate(mheads)]
        for h, hs in enumerate(mheads):
            y_ref[:, 2 * W + h * MEM_HEAD_DIM:2 * W + (h + 1) * MEM_HEAD_DIM] = _dot(
                probs[h].astype(BF16), mv_ref[:, hs], NN).astype(BF16)

    return pl.pallas_call(
        body,
        name="mixer_fwd",
        grid=(bl, ns),
        in_specs=[
            pl.BlockSpec((ts, N_MIX), lambda b, s: (b * ns + s, 0)),
            pl.BlockSpec((ml, W), lambda b, s: (b, 0)),
            pl.BlockSpec((ml, W), lambda b, s: (b, 0)),
            pl.BlockSpec((1, W), lambda b, s: (0, 0)),
            pl.BlockSpec((CONV_K, W), lambda b, s: (0, 0)),
            pl.BlockSpec((1, HG_F), lambda b, s: (0, 0)),
        ],
        out_specs=[
            pl.BlockSpec((ts, 3 * W), lambda b, s: (b * ns + s, 0)),
            pl.BlockSpec((nc, HG_HEADS, HG_F, HG_F), lambda b, s: (b * ns + s, 0, 0, 0)),
            pl.BlockSpec((ts, W), lambda b, s: (b * ns + s, 0)),
        ],
        out_shape=[
            jax.ShapeDtypeStruct((T, 3 * W), BF16),
            jax.ShapeDtypeStruct((T // HG_CHUNK, HG_HEADS, HG_F, HG_F), F32),
            jax.ShapeDtypeStruct((T, W), F32),
        ],
        scratch_shapes=[pltpu.VMEM((HG_HEADS, HG_F, HG_F), F32), pltpu.VMEM((8, W), F32), pltpu.VMEM((ts, W), F32)],
        compiler_params=_cparams(("arbitrary", "arbitrary")),
    )(p, mk, mv, lb, conv_w, norm_w)


def _mixer_bwd(p, dy, dp_gates, st, opre, mk, mv, lb, conv_w, norm_w, *, bl, seq, deps=()):
    T, nin = p.shape
    ts = TS_MIX
    ns = seq // ts
    nc = ts // HG_CHUNK
    ml = mk.shape[0] // bl
    mid, last = HG_CHUNK // 2 - 1, HG_CHUNK - 1

    def body(p_ref, pprev_ref, dy_ref, dpin_ref, st_ref, opre_ref, mk_ref, mv_ref, lb_ref, cw_ref, nw_ref, *rest):
        (dp_ref, dmk_ref, dmv_ref, dcw_ref, dnw_ref, dlb_ref, dstate_sc, carry_sc, uprev_sc, ab_sc, bkb_sc, qinb_sc, koutb_sc,
         dob_sc, dv_sc, da_sc, dbk_sc, dqin_sc, dkout_sc, dec_sc, ddec_sc, *keep_scs) = rest[len(deps):]
        del dpin_ref
        b, s = pl.program_id(0), pl.program_id(1)

        @pl.when(s == 0)
        def _():
            dstate_sc[...] = jnp.zeros_like(dstate_sc)
            carry_sc[...] = jnp.zeros_like(carry_sc)
            dmk_ref[...] = jnp.zeros_like(dmk_ref)
            dmv_ref[...] = jnp.zeros_like(dmv_ref)

        @pl.when(jnp.logical_and(b == 0, s == 0))
        def _():
            dcw_ref[...] = jnp.zeros_like(dcw_ref)
            dnw_ref[...] = jnp.zeros_like(dnw_ref)
            dlb_ref[...] = jnp.zeros_like(dlb_ref)

        cb, cc, ch = (p_ref[:, c0:c0 + W].astype(F32) for c0 in (C_CB, C_CC, C_CH))
        u = cc * ch
        row = lax.broadcasted_iota(jnp.int32, (ts, W), 0)
        uprev = pprev_ref[:, C_CC:C_CC + W].astype(F32) * pprev_ref[:, C_CH:C_CH + W].astype(F32)
        uprev_sc[...] = jnp.where(s == ns - 1, 0.0, uprev)
        u1, u2 = _conv_shift_down(u, uprev_sc, row)
        w0, w1, w2 = cw_ref[0:1, :], cw_ref[1:2, :], cw_ref[2:3, :]
        dya = dy_ref[:, 0:W].astype(F32)
        dp_ref[:, C_CB:C_CB + W] = (dya * (u2 * w0 + u1 * w1 + u * w2)).astype(BF16)
        dv = cb * dya
        dv1 = jnp.where(row == ts - 1, carry_sc[0:1, :], pltpu.roll(dv, ts - 1, 0))
        dv2 = jnp.where(row == ts - 1, carry_sc[1:2, :], jnp.where(row == ts - 2, carry_sc[0:1, :], pltpu.roll(dv, ts - 2, 0)))
        du = dv * w2 + dv1 * w1 + dv2 * w0
        dp_ref[:, C_CC:C_CC + W] = (du * ch).astype(BF16)
        dp_ref[:, C_CH:C_CH + W] = (du * cc).astype(BF16)
        dcw_ref[0:1, :] += jnp.sum(dv * u2, axis=0, keepdims=True)
        dcw_ref[1:2, :] += jnp.sum(dv * u1, axis=0, keepdims=True)
        dcw_ref[2:3, :] += jnp.sum(dv * u, axis=0, keepdims=True)
        carry_sc[...] = dv[0:8, :]

        mask = _chunk_mask(ts)
        pos_c = _chunk_pos((HG_CHUNK, HG_F))
        nw = nw_ref[...]

        def block(n, h):
            rows = slice(n * HG_CHUNK, (n + 1) * HG_CHUNK)
            return rows, slice(h * HG_F, (h + 1) * HG_F)

        keep = dict(zip(KEEP_NAMES, keep_scs))

        def gates(rows, h):
            lbh = lb_ref[:, h * HG_F:(h + 1) * HG_F]
            q = p_ref[rows, C_HQ + h * HG_F:C_HQ + (h + 1) * HG_F].astype(F32)
            fl = p_ref[rows, C_HF + h * HG_F:C_HF + (h + 1) * HG_F].astype(F32)
            sig = _sigmoid(fl)
            f = lbh + (1.0 - lbh) * sig
            k = (1.0 - lbh) * _sigmoid(-fl)
            sq = _sigmoid(q)
            qs = q * sq
            bc = _seg_cumsum(jnp.log(f), pos_c)
            bref = jnp.sum(jnp.where(pos_c == mid, bc, 0.0), axis=0, keepdims=True)
            blast = jnp.sum(jnp.where(pos_c == last, bc, 0.0), axis=0, keepdims=True)
            ea, eb, eq, ek = jnp.exp(bc - bref), jnp.exp(bref - bc), jnp.exp(bc), jnp.exp(blast - bc)
            return dict(sq=sq, qs=qs, k=k, sig=sig, f=f, ea=ea, eb=eb, eq=eq, ek=ek), blast

        dnw = jnp.zeros((1, HG_F), F32)
        for n in range(nc):
            for h in range(HG_HEADS):
                rows, hs = block(n, h)
                fw, blast = gates(rows, h)
                for name in KEEP_NAMES:
                    keep[name][rows, hs] = fw[name]
                ab_sc[rows, hs] = (fw["qs"] * fw["ea"]).astype(BF16)
                bkb_sc[rows, hs] = (fw["k"] * fw["eb"]).astype(BF16)
                qinb_sc[rows, hs] = (fw["qs"] * fw["eq"]).astype(BF16)
                koutb_sc[rows, hs] = (fw["k"] * fw["ek"]).astype(BF16)
                dec_sc[n:n + 1, hs] = jnp.exp(blast)
                o = opre_ref[rows, hs]
                g = p_ref[rows, C_HG + h * HG_F:C_HG + (h + 1) * HG_F].astype(F32)
                sg = _sigmoid(g)
                r = lax.rsqrt(jnp.mean(o * o, axis=-1, keepdims=True) + RMS_EPS)
                dyb = dy_ref[rows, W + h * HG_F:W + (h + 1) * HG_F].astype(F32)
                dp_ref[rows, C_HG + h * HG_F:C_HG + (h + 1) * HG_F] = (
                    dyb * (o * r * nw) * (sg * (1.0 + g * (1.0 - sg)))).astype(BF16)
                don = dyb * (g * sg)
                dnw = dnw + jnp.sum(don * o * r, axis=0, keepdims=True)
                dn = don * nw
                dob_sc[rows, hs] = (r * (dn - o * (r * r) * jnp.mean(dn * o, axis=-1, keepdims=True))).astype(BF16)
        dnw_ref[0:1, :] += dnw

        heads = [slice(h * HG_F, (h + 1) * HG_F) for h in range(HG_HEADS)]
        scores = [_dot(ab_sc[:, hs], bkb_sc[:, hs], NT) for hs in heads]
        dscores = [_dot(dob_sc[:, hs], p_ref[:, C_HI + h * HG_F:C_HI + (h + 1) * HG_F].astype(BF16), NT)
                   for h, hs in enumerate(heads)]
        scores = [jnp.where(mask, s, 0.0).astype(BF16) for s in scores]
        dscores = [jnp.where(mask, s, 0.0).astype(BF16) for s in dscores]
        for h, hs in enumerate(heads):
            dv_sc[:, hs] = _dot(scores[h], dob_sc[:, hs], TN)
            da_sc[:, hs] = _dot(dscores[h], bkb_sc[:, hs], NN)
            dbk_sc[:, hs] = _dot(dscores[h], ab_sc[:, hs], TN)
        dst = [dstate_sc[h] for h in range(HG_HEADS)]
        for n in reversed(range(nc)):
            for h in range(HG_HEADS):
                rows, hs = block(n, h)
                st_n = st_ref[n, h]
                decay = dec_sc[n:n + 1, hs]
                dstb = dst[h].astype(BF16)
                dob_n = dob_sc[rows, hs]
                dv_sc[rows, hs] += _dot(koutb_sc[rows, hs], dstb, NT)
                dkout_sc[rows, hs] = _dot(p_ref[rows, C_HI + h * HG_F:C_HI + (h + 1) * HG_F].astype(BF16), dstb, NN)
                ddec_sc[n:n + 1, hs] = jnp.sum(dst[h] * st_n, axis=0, keepdims=True) * decay
                dqin_sc[rows, hs] = _dot(dob_n, st_n.astype(BF16), NN)
                dst[h] = dst[h] * decay + _dot(dob_n, qinb_sc[rows, hs], TN)
        for h in range(HG_HEADS):
            dstate_sc[h] = dst[h]

        for h in range(HG_HEADS):
            dlb = jnp.zeros((1, HG_F), F32)
            for n in range(nc):
                rows, hs = block(n, h)
                fw = {name: keep[name][rows, hs] for name in KEEP_NAMES}
                lbh = lb_ref[:, h * HG_F:(h + 1) * HG_F]
                q = p_ref[rows, C_HQ + h * HG_F:C_HQ + (h + 1) * HG_F].astype(F32)
                da, dbk, dqin, dkout = da_sc[rows, hs], dbk_sc[rows, hs], dqin_sc[rows, hs], dkout_sc[rows, hs]
                w_a, w_b, w_q, w_k = da * fw["ea"], dbk * fw["eb"], dqin * fw["eq"], dkout * fw["ek"]
                dqs, dk = w_a + w_q, w_b + w_k
                t_a, t_b, t_q, t_k = w_a * fw["qs"], w_b * fw["k"], w_q * fw["qs"], w_k * fw["k"]
                s_ref = jnp.sum(t_b - t_a, axis=0, keepdims=True)
                s_last = jnp.sum(t_k, axis=0, keepdims=True) + ddec_sc[n:n + 1, hs]
                dbc = (t_a - t_b + t_q - t_k) + jnp.where(pos_c == mid, s_ref, 0.0) + jnp.where(pos_c == last, s_last, 0.0)
                dfk = _seg_rev_cumsum(dbc, pos_c) / fw["f"] - dk
                sig, sq = fw["sig"], fw["sq"]
                dp_ref[rows, C_HF + h * HG_F:C_HF + (h + 1) * HG_F] = (dfk * (1.0 - lbh) * sig * (1.0 - sig)).astype(BF16)
                dlb = dlb + jnp.sum(dfk * (1.0 - sig), axis=0, keepdims=True)
                dp_ref[rows, C_HQ + h * HG_F:C_HQ + (h + 1) * HG_F] = (dqs * (sq * (1.0 + q * (1.0 - sq)))).astype(BF16)
                dp_ref[rows, C_HI + h * HG_F:C_HI + (h + 1) * HG_F] = dv_sc[rows, hs].astype(BF16)
            dlb_ref[0:1, h * HG_F:(h + 1) * HG_F] += dlb

        mheads = [slice(h * MEM_HEAD_DIM, (h + 1) * MEM_HEAD_DIM) for h in range(MEM_HEADS)]
        qhs = [p_ref[:, C_MQ + h * MEM_HEAD_DIM:C_MQ + (h + 1) * MEM_HEAD_DIM].astype(BF16) for h in range(MEM_HEADS)]
        dobs = [dy_ref[:, 2 * W + h * MEM_HEAD_DIM:2 * W + (h + 1) * MEM_HEAD_DIM].astype(BF16) for h in range(MEM_HEADS)]
        probs = [_attn_probs(qhs[h], mk_ref[:, hs]) for h, hs in enumerate(mheads)]
        dprobs = [_dot(dobs[h], mv_ref[:, hs], NT) for h, hs in enumerate(mheads)]
        for h, hs in enumerate(mheads):
            prob = probs[h]
            dmv_ref[:, hs] += _dot(prob.astype(BF16), dobs[h], TN)
            ds = prob * (dprobs[h] - jnp.sum(dprobs[h] * prob, axis=-1, keepdims=True)) * (MEM_HEAD_DIM ** -0.5)
            dsb = ds.astype(BF16)
            dp_ref[:, C_MQ + h * MEM_HEAD_DIM:C_MQ + (h + 1) * MEM_HEAD_DIM] = _dot(dsb, mk_ref[:, hs], NN).astype(BF16)
            dmk_ref[:, hs] += _dot(dsb, qhs[h], TN)

    def tile(b, s):
        return b * ns + (ns - 1 - s)

    return pl.pallas_call(
        body,
        name="mixer_bwd",
        grid=(bl, ns),
        in_specs=[
            pl.BlockSpec((ts, N_MIX), lambda b, s: (tile(b, s), 0)),
            pl.BlockSpec((PREV_ROWS, N_MIX), lambda b, s: (jnp.maximum(tile(b, s) * (ts // PREV_ROWS) - 1, 0), 0)),
            pl.BlockSpec((ts, 3 * W), lambda b, s: (tile(b, s), 0)),
            pl.BlockSpec(memory_space=pl.ANY),
            pl.BlockSpec((nc, HG_HEADS, HG_F, HG_F), lambda b, s: (tile(b, s), 0, 0, 0)),
            pl.BlockSpec((ts, W), lambda b, s: (tile(b, s), 0)),
            pl.BlockSpec((ml, W), lambda b, s: (b, 0)),
            pl.BlockSpec((ml, W), lambda b, s: (b, 0)),
            pl.BlockSpec((1, W), lambda b, s: (0, 0)),
            pl.BlockSpec((CONV_K, W), lambda b, s: (0, 0)),
            pl.BlockSpec((1, HG_F), lambda b, s: (0, 0)),
        ] + [ANY_SPEC] * len(deps),
        out_specs=[
            pl.BlockSpec((ts, N_MIX), lambda b, s: (tile(b, s), 0)),
            pl.BlockSpec((ml, W), lambda b, s: (b, 0)),
            pl.BlockSpec((ml, W), lambda b, s: (b, 0)),
            pl.BlockSpec((8, W), lambda b, s: (0, 0)),
            pl.BlockSpec((8, HG_F), lambda b, s: (0, 0)),
            pl.BlockSpec((8, W), lambda b, s: (0, 0)),
        ],
        out_shape=[
            jax.ShapeDtypeStruct((T, nin), BF16),
            jax.ShapeDtypeStruct((bl * ml, W), F32),
            jax.ShapeDtypeStruct((bl * ml, W), F32),
            jax.ShapeDtypeStruct((8, W), F32),
            jax.ShapeDtypeStruct((8, HG_F), F32),
            jax.ShapeDtypeStruct((8, W), F32),
        ],
        input_output_aliases={3: 0},
        scratch_shapes=[pltpu.VMEM((HG_HEADS, HG_F, HG_F), F32), pltpu.VMEM((8, W), F32), pltpu.VMEM((PREV_ROWS, W), F32)]
        + [pltpu.VMEM((ts, W), BF16)] * 5 + [pltpu.VMEM((ts, W), F32)] * 5 + [pltpu.VMEM((nc, W), F32)] * 2
        + [pltpu.VMEM((ts, W), F32)] * len(KEEP_NAMES),
        compiler_params=_cparams(("arbitrary", "arbitrary")),
    )(p, p, dy, dp_gates, st, opre, mk, mv, lb, conv_w, norm_w, *deps)


def _layer_norm_stats(z):
    mu = jnp.mean(z, axis=-1, keepdims=True)
    zc = z - mu
    rstd = lax.rsqrt(jnp.mean(zc * zc, axis=-1, keepdims=True) + LN_EPS)
    return zc * rstd, rstd


def _gate_specs(tm, d):
    g0 = N_MIX // d
    return [pl.BlockSpec((tm, d), functools.partial(lambda i, k: (i, g0 + k), k=k)) for k in range(N_BRANCH)]


def _merge_fwd(y, p, x0, wb, wo, bg, ln_g, ln_b, *, alpha, tm=512):
    T, d = x0.shape
    assert N_MIX % d == 0
    tm = _pick(T, (tm, 128, 8))

    def body(y_ref, g0_ref, g1_ref, g2_ref, x_ref, wb_ref, wo_ref, bg_ref, lg_ref, lb_ref, mg_ref, xh_ref, rs_ref, x1b_ref):
        merged = None
        for i, g_ref in enumerate((g0_ref, g1_ref, g2_ref)):
            r = _dot(y_ref[:, i * W:(i + 1) * W], wb_ref[i * W:(i + 1) * W, :], NN)
            t = _sigmoid(g_ref[...].astype(F32) + bg_ref[:, i * d:(i + 1) * d]) * r
            merged = t if merged is None else merged + t
        mb = merged.astype(BF16)
        mg_ref[...] = mb
        z = alpha * x_ref[...] + _dot(mb, wo_ref[...], NN)
        xh, rs = _layer_norm_stats(z)
        xh_ref[...], rs_ref[...] = xh, rs
        x1b_ref[...] = (xh * lg_ref[...] + lb_ref[...]).astype(BF16)

    row = lambda i: (i, 0)
    fix = lambda i: (0, 0)
    return pl.pallas_call(
        body,
        name="merge_fwd",
        grid=(T // tm,),
        in_specs=[pl.BlockSpec((tm, 3 * W), row)] + _gate_specs(tm, d) + [
            pl.BlockSpec((tm, d), row), pl.BlockSpec((3 * W, d), fix, pipeline_mode=pl.Buffered(1)),
            pl.BlockSpec((d, d), fix, pipeline_mode=pl.Buffered(1)), pl.BlockSpec((1, 3 * d), fix),
            pl.BlockSpec((1, d), fix), pl.BlockSpec((1, d), fix)],
        out_specs=[pl.BlockSpec((tm, d), row), pl.BlockSpec((tm, d), row), pl.BlockSpec((tm, 1), row), pl.BlockSpec((tm, d), row)],
        out_shape=[jax.ShapeDtypeStruct((T, d), BF16), jax.ShapeDtypeStruct((T, d), F32), jax.ShapeDtypeStruct((T, 1), F32),
                   jax.ShapeDtypeStruct((T, d), BF16)],
        compiler_params=_cparams(("parallel",)),
    )(y, p, p, p, x0, wb, wo, bg, ln_g, ln_b)


def _merge_bwd(dz, p, y, wb, wo, bg, *, tm=512):
    T, d = dz.shape
    nin = p.shape[1]
    tm = _pick(T, (tm, 128, 8))

    def body(dz_ref, g0_ref, g1_ref, g2_ref, y_ref, wb_ref, wo_ref, bg_ref, dr_ref, dp_ref, dy_ref, dbg_ref):
        @pl.when(pl.program_id(0) == 0)
        def _():
            dbg_ref[...] = jnp.zeros_like(dbg_ref)

        dmerged = _dot(dz_ref[...].astype(BF16), wo_ref[...], NT)
        dp_ref[:, 0:N_MIX] = jnp.zeros((tm, N_MIX), BF16)
        for i, g_ref in enumerate((g0_ref, g1_ref, g2_ref)):
            cs = slice(i * d, (i + 1) * d)
            s = _sigmoid(g_ref[...].astype(F32) + bg_ref[:, cs])
            drb = (dmerged * s).astype(BF16)
            dr_ref[:, cs] = drb
            dgate = dmerged * _dot(y_ref[:, i * W:(i + 1) * W], wb_ref[i * W:(i + 1) * W, :], NN) * s * (1.0 - s)
            dp_ref[:, N_MIX + i * d:N_MIX + (i + 1) * d] = dgate.astype(BF16)
            dbg_ref[0:1, cs] += jnp.sum(dgate, axis=0, keepdims=True)
            dy_ref[:, i * W:(i + 1) * W] = _dot(drb, wb_ref[i * W:(i + 1) * W, :], NT).astype(BF16)

    row = lambda i: (i, 0)
    fix = lambda i: (0, 0)
    return pl.pallas_call(
        body,
        name="merge_bwd",
        grid=(T // tm,),
        in_specs=[pl.BlockSpec((tm, d), row)] + _gate_specs(tm, d) + [
            pl.BlockSpec((tm, 3 * W), row), pl.BlockSpec((3 * W, d), fix, pipeline_mode=pl.Buffered(1)),
            pl.BlockSpec((d, d), fix, pipeline_mode=pl.Buffered(1)), pl.BlockSpec((1, 3 * d), fix)],
        out_specs=[pl.BlockSpec((tm, 3 * d), row), pl.BlockSpec((tm, nin), row), pl.BlockSpec((tm, 3 * W), row),
                   pl.BlockSpec((8, 3 * d), fix)],
        out_shape=[jax.ShapeDtypeStruct((T, 3 * d), BF16), jax.ShapeDtypeStruct((T, nin), BF16),
                   jax.ShapeDtypeStruct((T, 3 * W), BF16), jax.ShapeDtypeStruct((8, 3 * d), F32)],
        compiler_params=_cparams(("arbitrary",)),
    )(dz, p, p, p, y, wb, wo, bg)


MLP_VMEM_LIMIT = 58 * 1024 * 1024


def _mlp_fwd(xhat1, x1b, g1, b1, wu, wd, g2, b2, *, alpha, tm=512, tf=1024):
    T, d = xhat1.shape
    ff = wu.shape[1]
    tm, tf = _pick(T, (tm, 256, 128, 8)), _pick(ff, (tf, 1024, 512, 256, 128))

    def body(xh_ref, x1b_ref, g1_ref, b1_ref, wu_ref, wd_ref, g2_ref, b2_ref, a_ref, xh2_ref, rs2_ref, x2_ref, x2b_ref):
        xb = x1b_ref[...]
        acc = None
        a = _dot(xb, wu_ref[:, 0:tf], NN)
        for c0 in range(0, ff, tf):
            a_next = _dot(xb, wu_ref[:, c0 + tf:c0 + 2 * tf], NN) if c0 + tf < ff else None
            a_ref[:, c0:c0 + tf] = a.astype(BF16)
            part = _dot(jnp.square(jnp.maximum(a, 0.0)).astype(BF16), wd_ref[c0:c0 + tf, :], NN)
            acc = part if acc is None else acc + part
            a = a_next
        x1 = xh_ref[...] * g1_ref[...] + b1_ref[...]
        xh2, rs2 = _layer_norm_stats(alpha * x1 + acc)
        xh2_ref[...] = xh2
        rs2_ref[...] = rs2
        x2 = xh2 * g2_ref[...] + b2_ref[...]
        x2_ref[...] = x2
        x2b_ref[...] = x2.astype(BF16)

    row = lambda i: (i, 0)
    fix = lambda i: (0, 0)
    once = dict(pipeline_mode=pl.Buffered(1))
    return pl.pallas_call(
        body,
        name="mlp_fwd",
        grid=(T // tm,),
        in_specs=[pl.BlockSpec((tm, d), row), pl.BlockSpec((tm, d), row), pl.BlockSpec((1, d), fix), pl.BlockSpec((1, d), fix),
                  pl.BlockSpec((d, ff), fix, **once), pl.BlockSpec((ff, d), fix, **once),
                  pl.BlockSpec((1, d), fix), pl.BlockSpec((1, d), fix)],
        out_specs=[pl.BlockSpec((tm, ff), row), pl.BlockSpec((tm, d), row), pl.BlockSpec((tm, 1), row),
                   pl.BlockSpec((tm, d), row), pl.BlockSpec((tm, d), row)],
        out_shape=[jax.ShapeDtypeStruct((T, ff), BF16), jax.ShapeDtypeStruct((T, d), F32), jax.ShapeDtypeStruct((T, 1), F32),
                   jax.ShapeDtypeStruct((T, d), F32), jax.ShapeDtypeStruct((T, d), BF16)],
        compiler_params=pltpu.CompilerParams(dimension_semantics=("parallel",), vmem_limit_bytes=MLP_VMEM_LIMIT),
    )(xhat1, x1b, g1, b1, wu, wd, g2, b2)


def _ln_bwd(dy, xhat, rstd, g, *, tm=512, deps=()):
    T, d = dy.shape
    tm = _pick(T, (tm, 256, 128, 8))

    def body(dy_ref, xh_ref, rs_ref, g_ref, *rest):
        dz_ref, dzb_ref, dg_ref, db_ref = rest[len(deps):]

        @pl.when(pl.program_id(0) == 0)
        def _():
            dg_ref[...] = jnp.zeros_like(dg_ref)
            db_ref[...] = jnp.zeros_like(db_ref)

        dy_, xh = dy_ref[...], xh_ref[...]
        dg_ref[0:1, :] += jnp.sum(dy_ * xh, axis=0, keepdims=True)
        db_ref[0:1, :] += jnp.sum(dy_, axis=0, keepdims=True)
        dxh = dy_ * g_ref[...]
        dz = rs_ref[...] * (dxh - jnp.mean(dxh, axis=-1, keepdims=True) - xh * jnp.mean(dxh * xh, axis=-1, keepdims=True))
        dz_ref[...] = dz
        dzb_ref[...] = dz.astype(BF16)

    row = lambda i: (i, 0)
    fix = lambda i: (0, 0)
    return pl.pallas_call(
        body,
        name="ln_bwd",
        grid=(T // tm,),
        in_specs=[pl.BlockSpec((tm, d), row), pl.BlockSpec((tm, d), row), pl.BlockSpec((tm, 1), row), pl.BlockSpec((1, d), fix)]
        + [ANY_SPEC] * len(deps),
        out_specs=[pl.BlockSpec((tm, d), row), pl.BlockSpec((tm, d), row), pl.BlockSpec((8, d), fix), pl.BlockSpec((8, d), fix)],
        out_shape=[jax.ShapeDtypeStruct((T, d), F32), jax.ShapeDtypeStruct((T, d), BF16), jax.ShapeDtypeStruct((8, d), F32),
                   jax.ShapeDtypeStruct((8, d), F32)],
        compiler_params=_cparams(("arbitrary",)),
    )(dy, xhat, rstd, g, *deps)


def _loss_head(y, target, xhat, rstd, g, *, tm=512):
    T, d = y.shape
    tm = _pick(T, (tm, 256, 128, 8))
    n = T // tm

    def body(y_ref, t_ref, xh_ref, rs_ref, g_ref, loss_ref, dz_ref, dzb_ref, dg_ref, db_ref, acc_ref):
        i = pl.program_id(0)

        @pl.when(i == 0)
        def _():
            acc_ref[...] = jnp.zeros_like(acc_ref)
            dg_ref[...] = jnp.zeros_like(dg_ref)
            db_ref[...] = jnp.zeros_like(db_ref)

        e = y_ref[...] - t_ref[...]
        acc_ref[...] += jnp.sum(e * e, axis=0, keepdims=True)
        dy_, xh = e * (1.0 / d), xh_ref[...]
        dg_ref[0:1, :] += jnp.sum(dy_ * xh, axis=0, keepdims=True)
        db_ref[0:1, :] += jnp.sum(dy_, axis=0, keepdims=True)
        dxh = dy_ * g_ref[...]
        dz = rs_ref[...] * (dxh - jnp.mean(dxh, axis=-1, keepdims=True) - xh * jnp.mean(dxh * xh, axis=-1, keepdims=True))
        dz_ref[...] = dz
        dzb_ref[...] = dz.astype(BF16)

        @pl.when(i == n - 1)
        def _():
            loss_ref[...] = (0.5 / d) * jnp.sum(acc_ref[...], axis=1, keepdims=True)

    row = lambda i: (i, 0)
    fix = lambda i: (0, 0)
    return pl.pallas_call(
        body,
        name="loss_head",
        grid=(n,),
        in_specs=[pl.BlockSpec((tm, d), row), pl.BlockSpec((tm, d), row), pl.BlockSpec((tm, d), row), pl.BlockSpec((tm, 1), row),
                  pl.BlockSpec((1, d), fix)],
        out_specs=[pl.BlockSpec((1, 1), fix), pl.BlockSpec((tm, d), row), pl.BlockSpec((tm, d), row), pl.BlockSpec((8, d), fix),
                   pl.BlockSpec((8, d), fix)],
        out_shape=[jax.ShapeDtypeStruct((1, 1), F32), jax.ShapeDtypeStruct((T, d), F32), jax.ShapeDtypeStruct((T, d), BF16),
                   jax.ShapeDtypeStruct((8, d), F32), jax.ShapeDtypeStruct((8, d), F32)],
        scratch_shapes=[pltpu.VMEM((1, d), F32)],
        compiler_params=_cparams(("arbitrary",)),
    )(y, target, xhat, rstd, g)


def _lower_bounds_fwd(lower_bounds):
    depth, n = lower_bounds.shape

    def body(x_ref, soft_ref, lb_ref):
        x = x_ref[...]
        e = jnp.exp(x - jnp.max(x, axis=0, keepdims=True))
        soft_ref[...] = e / jnp.sum(e, axis=0, keepdims=True)
        run = None
        for l in range(depth):
            run = soft_ref[l:l + 1, :] if run is None else run + soft_ref[l:l + 1, :]
            lb_ref[l:l + 1, :] = run - soft_ref[0:1, :]

    return pl.pallas_call(body, name="lower_bounds_fwd",
                          out_shape=[jax.ShapeDtypeStruct((depth, n), F32), jax.ShapeDtypeStruct((depth, n), F32)])(lower_bounds)


def _lower_bounds_bwd(soft, dlb):
    depth, n = soft.shape

    def body(soft_ref, dlb_ref, out_ref, dsoft_ref):
        total = jnp.sum(dlb_ref[...], axis=0, keepdims=True)
        run = None
        for l in reversed(range(depth)):
            run = dlb_ref[l:l + 1, :] if run is None else run + dlb_ref[l:l + 1, :]
            dsoft_ref[l:l + 1, :] = run - total if l == 0 else run
        s, ds = soft_ref[...], dsoft_ref[...]
        out_ref[...] = s * (ds - jnp.sum(s * ds, axis=0, keepdims=True))

    return pl.pallas_call(body, name="lower_bounds_bwd", out_shape=jax.ShapeDtypeStruct((depth, n), F32),
                          scratch_shapes=[pltpu.VMEM((depth, n), F32)])(soft, dlb)


def _layer_fwd(x0, x0b, mem2, lb, w_in, mix_fn, late_fn, *, bl, seq, alpha, deps=()):
    p = _matmul("proj_in", x0b, w_in, mode="nn", out_dtype=BF16, deps=deps, tm=1024, tn=1792)
    wts = dict(mix_fn(p), w_in=w_in)
    mk = _matmul("mem_k", mem2, wts["w_mem_k"], mode="nn", out_dtype=BF16)
    mv = _matmul("mem_v", mem2, wts["w_mem_v"], mode="nn", out_dtype=BF16)
    y, st, opre = _mixer_fwd(p, mk, mv, lb, wts["conv_w"], wts["hg_norm_w"], bl=bl, seq=seq)
    wts.update(late_fn(y))
    merged, xhat1, rstd1, x1b = _merge_fwd(y, p, x0, wts["w_branch"], wts["w_o"], wts["b_gate"], wts["ln1_g"], wts["ln1_b"],
                                           alpha=alpha)
    a, xhat2, rstd2, x2, x2b = _mlp_fwd(xhat1, x1b, wts["ln1_g"], wts["ln1_b"], wts["w_up"], wts["w_down"], wts["ln2_g"],
                                        wts["ln2_b"], alpha=alpha)
    saved = dict(x0b=x0b, p=p, mk=mk, mv=mv, y=y, st=st, opre=opre, merged=merged, xhat1=xhat1, rstd1=rstd1, x1b=x1b, a=a,
                 xhat2=xhat2, rstd2=rstd2)
    return x2, x2b, saved, wts


def _relu2_bf16(a):
    return jnp.square(jnp.maximum(a.astype(F32), 0.0)).astype(BF16)


def _mlp_bwd(dz2, dz2b, sv, wts, *, alpha, deps=()):
    g = {}
    da = _matmul("mlp_da", dz2b, wts["w_down"], mode="nt", out_dtype=BF16, tm=512, tn=wts["w_down"].shape[0], deps=deps,
                 epi_fn=lambda acc, a: (acc * (2.0 * jnp.maximum(a.astype(F32), 0.0)),), epi_extra=(sv["a"],))
    g["w_down"] = _matmul_tn("grad_w_down", sv["a"], dz2b, a_fn=_relu2_bf16, out_dtype=BF16, tt=2048)
    g["w_up"] = _matmul_tn("grad_w_up", sv["x1b"], da, out_dtype=BF16, tt=2048)
    dx1 = _matmul("mlp_dx", da, wts["w_up"], mode="nt", epi_fn=lambda acc, dz: (acc + alpha * dz,), epi_extra=(dz2,),
                  tm=512, tk=4096)
    dz1, dz1b, dg1, db1 = _ln_bwd(dx1, sv["xhat1"], sv["rstd1"], wts["ln1_g"])
    g["ln1_g"], g["ln1_b"] = dg1[0:1], db1[0:1]
    return dz1, dz1b, g


def _mix_bwd(dz1, dz1b, sv, mem2, lb, wts, *, bl, seq, alpha, send, below=None, deps=()):
    d = dz1.shape[1]
    g = {}
    g["w_o"] = _matmul_tn("grad_w_o", sv["merged"], dz1b, out_dtype=BF16, tt=2048, deps=deps)
    dr, dp, dy, dbg = _merge_bwd(dz1b, sv["p"], sv["y"], wts["w_branch"], wts["w_o"], wts["b_gate"])
    g["b_gate"] = dbg[0:1]
    g["w_branch"] = jnp.concatenate(
        [_matmul_tn("grad_w_branch", sv["y"], dr, a_cols=(i * W, W), b_cols=(i * d, d), out_dtype=BF16, tt=4096)
         for i in range(N_BRANCH)],
        axis=0)
    token = send(("w_o", "w_branch"), g)
    dp, dmk, dmv, dcw, dnw, dlb = _mixer_bwd(sv["p"], dy, dp, sv["st"], sv["opre"], sv["mk"], sv["mv"], lb,
                                              wts["conv_w"], wts["hg_norm_w"], bl=bl, seq=seq, deps=(token,))
    g["conv_w"], g["hg_norm_w"], g["lb"] = dcw[0:CONV_K], dnw[0:1], dlb[0:1]
    g["w_mem_k"] = _matmul_tn("grad_w_mem_k", mem2, dmk, out_dtype=BF16)
    g["w_mem_v"] = _matmul_tn("grad_w_mem_v", mem2, dmv, out_dtype=BF16)
    g["w_in"] = _matmul_tn("grad_w_in", sv["x0b"], dp, out_dtype=BF16, tt=2048)
    token = send(("w_in", "w_mem_k", "w_mem_v", "conv_w"), g)
    dx0 = _matmul("proj_in_dx", dp, wts["w_in"], mode="nt", epi_fn=lambda acc, dz: (acc + alpha * dz,), epi_extra=(dz1,),
                  tm=512, tk=dp.shape[1], deps=(token,))
    return (dx0 if below is None else _ln_bwd(dx0, *below)), g


N_CHIPS = 4
MESH_IDS = pl.DeviceIdType.MESH


def _axis_slice(ref, axis, start, size):
    idx = [slice(None)] * len(ref.shape)
    idx[axis] = pl.ds(start, size)
    return ref.at[tuple(idx)]


def _chip_exchange(name, items):
    n = len(items)
    out_shapes, meta = [], []
    for arr, kind, axis in items:
        shp = list(arr.shape)
        if kind == "gather":
            per = shp[axis]
            shp[axis] = per * N_CHIPS
            out_shapes.append(jax.ShapeDtypeStruct(tuple(shp), arr.dtype))
        elif kind == "scatter":
            per = shp[axis] // N_CHIPS
            shp[axis] = per
            out_shapes.append(jax.ShapeDtypeStruct((N_CHIPS, *shp), arr.dtype))
        else:
            per = None
            out_shapes.append(jax.ShapeDtypeStruct((N_CHIPS, *shp), arr.dtype))
        meta.append((kind, axis, per))

    def body(*refs):
        ins, outs = refs[:n], refs[n:2 * n]
        send_sems, recv_sems, local_sems = refs[2 * n:]
        x, y, c = lax.axis_index("x"), lax.axis_index("y"), lax.axis_index("c")
        me = 2 * x + y
        peers = [(1 - x, y), (x, 1 - y), (1 - x, 1 - y)]

        def src_for(t, chip):
            kind, axis, per = meta[t]
            return _axis_slice(ins[t], axis, chip * per, per) if kind == "scatter" else ins[t]

        def dst_from(t, chip):
            kind, axis, per = meta[t]
            return _axis_slice(outs[t], axis, chip * per, per) if kind == "gather" else outs[t].at[chip]

        def remote(t, k):
            px, py = peers[k]
            return pltpu.make_async_remote_copy(
                src_ref=src_for(t, 2 * px + py), dst_ref=dst_from(t, me), send_sem=send_sems.at[t * 3 + k],
                recv_sem=recv_sems.at[t * 3 + k], device_id=(px, py, c), device_id_type=MESH_IDS)

        def arrival(t, k):
            px, py = peers[k]
            return pltpu.make_async_remote_copy(
                src_ref=src_for(t, me), dst_ref=dst_from(t, 2 * px + py), send_sem=send_sems.at[t * 3 + k],
                recv_sem=recv_sems.at[t * 3 + k], device_id=(px, py, c), device_id_type=MESH_IDS)

        sends = [remote(t, k) for t in range(n) for k in range(3)]
        for cp in sends:
            cp.start()
        own = [pltpu.make_async_copy(src_for(t, me), dst_from(t, me), local_sems.at[t]) for t in range(n)]
        for cp in own:
            cp.start()
        for t in range(n):
            for k in range(3):
                arrival(t, k).wait_recv()
        for cp in sends:
            cp.wait_send()
        for cp in own:
            cp.wait()

    any_spec = pl.BlockSpec(memory_space=pl.ANY)
    return pl.pallas_call(
        body,
        name=name,
        in_specs=[any_spec] * n,
        out_specs=[any_spec] * n,
        out_shape=out_shapes,
        scratch_shapes=[pltpu.SemaphoreType.DMA((3 * n,)), pltpu.SemaphoreType.DMA((3 * n,)), pltpu.SemaphoreType.DMA((n,))],
        compiler_params=pltpu.CompilerParams(has_side_effects=True),
    )(*[a for a, _, _ in items])


HBM_SPEC = pl.BlockSpec(memory_space=pltpu.HBM)
SEM_SPEC = pl.BlockSpec(memory_space=pltpu.SEMAPHORE)
N_PEERS = N_CHIPS - 1


def _my_chip():
    return (2 * lax.axis_index("x") + lax.axis_index("y")).astype(jnp.int32).reshape(1)


def _own_block_spec(r, c, axis, tr):
    if axis == 1:
        return pl.BlockSpec((tr, c), lambda i, me: (i, me[0]))
    return pl.BlockSpec((tr, c), lambda i, me: (me[0] * (r // tr) + i, 0))


def _place_shard(name, shard, axis, me):
    r, c = shard.shape
    tr = _row_block(r, c, shard.dtype.itemsize)
    shp = (r, c * N_CHIPS) if axis == 1 else (r * N_CHIPS, c)

    def body(me_ref, s_ref, buf_ref, o_ref):
        del me_ref, buf_ref
        o_ref[...] = s_ref[...]

    buf = pltpu.with_memory_space_constraint(lax.empty(shp, shard.dtype), pltpu.HBM)
    return pl.pallas_call(
        body, name=name,
        grid_spec=pltpu.PrefetchScalarGridSpec(
            num_scalar_prefetch=1, grid=(r // tr,),
            in_specs=[pl.BlockSpec((tr, c), lambda i, me: (i, 0)), ANY_SPEC], out_specs=_own_block_spec(r, c, axis, tr)),
        out_shape=jax.ShapeDtypeStruct(shp, shard.dtype),
        input_output_aliases={2: 0},
        compiler_params=_cparams(("parallel",)),
    )(me, shard, buf)


class _Split:
    def __init__(self, name, items):
        self.name, self.n = name, len(items)
        self.srcs = [a for a, _, _ in items]
        self.meta, self.land_shapes = [], []
        for arr, kind, axis in items:
            shp = list(arr.shape)
            if kind == "gather":
                per = shp[axis]
                shp[axis] = per * N_CHIPS
                self.land_shapes.append(jax.ShapeDtypeStruct(tuple(shp), arr.dtype))
            else:
                per = shp[axis] // N_CHIPS
                shp[axis] = per
                self.land_shapes.append(jax.ShapeDtypeStruct((N_PEERS, *shp), arr.dtype))
            self.meta.append((kind, axis, per))

    def _src(self, ins, t, chip):
        kind, axis, per = self.meta[t]
        return _axis_slice(ins[t], axis, chip * per, per) if kind == "scatter" else ins[t]

    def _dst(self, lands, t, chip, slot):
        kind, axis, per = self.meta[t]
        return _axis_slice(lands[t], axis, chip * per, per) if kind == "gather" else lands[t].at[slot]

    def landing_zones(self, me):
        return [_place_shard(self.name + "_own", src, axis, me) if kind == "gather" else lax.empty(ls.shape, ls.dtype)
                for src, ls, (kind, axis, _) in zip(self.srcs, self.land_shapes, self.meta)]

    def _copies(self, ins, lands, send_sems, recv_sems, arrivals):
        x, y, c = lax.axis_index("x"), lax.axis_index("y"), lax.axis_index("c")
        me = 2 * x + y
        peers = [(1 - x, y), (x, 1 - y), (1 - x, 1 - y)]
        res = []
        for t in range(self.n):
            for k, (px, py) in enumerate(peers):
                theirs = 2 * px + py
                sems = dict(send_sem=send_sems.at[t * N_PEERS + k], recv_sem=recv_sems.at[t * N_PEERS + k],
                            device_id=(px, py, c), device_id_type=MESH_IDS)
                if arrivals:
                    res.append(pltpu.make_async_remote_copy(src_ref=self._src(ins, t, me), dst_ref=self._dst(lands, t, theirs, k), **sems))
                else:
                    res.append(pltpu.make_async_remote_copy(src_ref=self._src(ins, t, theirs), dst_ref=self._dst(lands, t, me, k), **sems))
        return res

    def start(self, lands, deps=()):
        n, nd = self.n, len(deps)

        def body(*refs):
            ins, lnd = refs[:n], refs[n:2 * n]
            send_sems, recv_sems = refs[2 * n + nd], refs[2 * n + nd + 1]
            token = refs[-1]
            for cp in self._copies(ins, lnd, send_sems, recv_sems, arrivals=False):
                cp.start()
            token[...] = jnp.zeros_like(token)

        hbm = lambda a: pltpu.HBM(a.shape, a.dtype)
        res = pl.pallas_call(
            body, name=self.name + "_start",
            in_specs=[HBM_SPEC] * (2 * n) + [ANY_SPEC] * nd,
            out_specs=[SEM_SPEC, SEM_SPEC] + [HBM_SPEC] * (2 * n) + [pl.BlockSpec(memory_space=pltpu.VMEM)],
            out_shape=[pltpu.SemaphoreType.DMA((N_PEERS * n,)), pltpu.SemaphoreType.DMA((N_PEERS * n,))]
            + [hbm(a) for a in self.srcs] + [hbm(a) for a in self.land_shapes] + [jax.ShapeDtypeStruct((8, 128), F32)],
            input_output_aliases={i: 2 + i for i in range(2 * n)},
            compiler_params=pltpu.CompilerParams(has_side_effects=pltpu.SideEffectType.DATAFLOW_SIDE_EFFECTING),
        )(*[pltpu.with_memory_space_constraint(a, pltpu.HBM) for a in self.srcs],
          *[pltpu.with_memory_space_constraint(a, pltpu.HBM) for a in lands], *deps)
        return res[:-1], res[-1]

    def wait(self, state, after):
        n = self.n
        after = tuple(after) if isinstance(after, (tuple, list)) else (after,)
        send_sems, recv_sems = state[0], state[1]
        srcs, lands = state[2:2 + n], state[2 + n:2 + 2 * n]

        def body(*refs):
            ins, lnd = refs[:n], refs[n:2 * n]
            s_sems, r_sems = refs[2 * n], refs[2 * n + 1]
            for cp in self._copies(ins, lnd, s_sems, r_sems, arrivals=True):
                cp.wait_recv()
            for cp in self._copies(ins, lnd, s_sems, r_sems, arrivals=False):
                cp.wait_send()

        hbm = lambda a: pltpu.HBM(a.shape, a.dtype)
        res = pl.pallas_call(
            body, name=self.name + "_wait",
            in_specs=[HBM_SPEC] * (2 * n) + [SEM_SPEC, SEM_SPEC] + [ANY_SPEC] * len(after),
            out_specs=[HBM_SPEC] * (2 * n),
            out_shape=[hbm(a) for a in self.srcs] + [hbm(a) for a in self.land_shapes],
            input_output_aliases={i: i for i in range(2 * n)},
            compiler_params=pltpu.CompilerParams(has_side_effects=pltpu.SideEffectType.DATAFLOW_SIDE_EFFECTING),
        )(*srcs, *lands, send_sems, recv_sems, *after)
        return res[:n], res[n:]


class _SiblingSplit:
    def __init__(self, name, arrays):
        self.name, self.n, self.arrays = name, len(arrays), list(arrays)

    def _copies(self, ins, lands, send_sems, recv_sems):
        sibling = (lax.axis_index("x"), lax.axis_index("y"), 1 - lax.axis_index("c"))
        return [pltpu.make_async_remote_copy(src_ref=ins[t], dst_ref=lands[t], send_sem=send_sems.at[t], recv_sem=recv_sems.at[t],
                                             device_id=sibling, device_id_type=MESH_IDS) for t in range(self.n)]

    def start(self, deps=()):
        n, nd = self.n, len(deps)

        def body(*refs):
            for cp in self._copies(refs[:n], refs[n:2 * n], refs[2 * n + nd], refs[2 * n + nd + 1]):
                cp.start()
            refs[-1][...] = jnp.zeros_like(refs[-1])

        hbm = [pltpu.HBM(a.shape, a.dtype) for a in self.arrays]
        res = pl.pallas_call(
            body, name=self.name + "_start",
            in_specs=[HBM_SPEC] * (2 * n) + [ANY_SPEC] * nd,
            out_specs=[SEM_SPEC, SEM_SPEC] + [HBM_SPEC] * (2 * n) + [pl.BlockSpec(memory_space=pltpu.VMEM)],
            out_shape=[pltpu.SemaphoreType.DMA((n,)), pltpu.SemaphoreType.DMA((n,))] + hbm + hbm + [jax.ShapeDtypeStruct((8, 128), F32)],
            input_output_aliases={i: 2 + i for i in range(2 * n)},
            compiler_params=pltpu.CompilerParams(has_side_effects=pltpu.SideEffectType.DATAFLOW_SIDE_EFFECTING),
        )(*[pltpu.with_memory_space_constraint(a, pltpu.HBM) for a in self.arrays],
          *[pltpu.with_memory_space_constraint(lax.empty(a.shape, a.dtype), pltpu.HBM) for a in self.arrays], *deps)
        return res[:-1], res[-1]

    def wait(self, state, after):
        n = self.n
        after = tuple(after) if isinstance(after, (tuple, list)) else (after,)

        def body(*refs):
            for cp in self._copies(refs[:n], refs[n:2 * n], refs[2 * n], refs[2 * n + 1]):
                cp.wait()

        hbm = [pltpu.HBM(a.shape, a.dtype) for a in self.arrays]
        res = pl.pallas_call(
            body, name=self.name + "_wait",
            in_specs=[HBM_SPEC] * (2 * n) + [SEM_SPEC, SEM_SPEC] + [ANY_SPEC] * len(after),
            out_specs=[HBM_SPEC] * (2 * n),
            out_shape=hbm + hbm,
            input_output_aliases={i: i for i in range(2 * n)},
            compiler_params=pltpu.CompilerParams(has_side_effects=pltpu.SideEffectType.DATAFLOW_SIDE_EFFECTING),
        )(*state[2:2 + 2 * n], state[0], state[1], *after)
        return res[:n], res[n:]


def _sibling_swap(name, arrays):
    n = len(arrays)

    def body(*refs):
        ins, outs = refs[:n], refs[n:2 * n]
        send_sems, recv_sems = refs[2 * n:]
        sibling = (lax.axis_index("x"), lax.axis_index("y"), 1 - lax.axis_index("c"))
        copies = [pltpu.make_async_remote_copy(src_ref=ins[t], dst_ref=outs[t], send_sem=send_sems.at[t], recv_sem=recv_sems.at[t],
                                               device_id=sibling, device_id_type=MESH_IDS) for t in range(n)]
        for cp in copies:
            cp.start()
        for cp in copies:
            cp.wait()

    any_spec = pl.BlockSpec(memory_space=pl.ANY)
    return pl.pallas_call(
        body,
        name=name,
        in_specs=[any_spec] * n,
        out_specs=[any_spec] * n,
        out_shape=[jax.ShapeDtypeStruct(a.shape, a.dtype) for a in arrays],
        scratch_shapes=[pltpu.SemaphoreType.DMA((n,)), pltpu.SemaphoreType.DMA((n,))],
        compiler_params=pltpu.CompilerParams(has_side_effects=True),
    )(*arrays)


def _row_block(r, c, itemsize=4, target=1 << 20):
    if r % 8 != 0:
        return r
    best = 8
    for tr in range(8, r + 1, 8):
        if r % tr == 0 and tr * c * itemsize <= target:
            best = tr
    return best


def _sum_chips_into(parts, stacked, layer):
    _, r, c = parts.shape
    tr = _row_block(r, c)

    def body(p_ref, s_ref, o_ref):
        del s_ref
        o_ref[...] = ((p_ref[0] + p_ref[1]) + p_ref[2]) + p_ref[3]

    return pl.pallas_call(
        body,
        name="sum_chips",
        grid=(r // tr,),
        in_specs=[pl.BlockSpec((N_CHIPS, tr, c), lambda i: (0, i, 0)), pl.BlockSpec(memory_space=pl.ANY)],
        out_specs=pl.BlockSpec((None, tr, c), lambda i: (layer, i, 0)),
        out_shape=jax.ShapeDtypeStruct(stacked.shape, stacked.dtype),
        input_output_aliases={1: 0},
        compiler_params=_cparams(("parallel",)),
    )(parts, stacked)


def _sum_own_and_peers(me, g, axis, landed):
    _, r, c = landed.shape
    tr = _row_block(r, c)

    def body(me_ref, g_ref, p_ref, o_ref):
        del me_ref
        o_ref[...] = ((g_ref[...].astype(F32) + p_ref[0].astype(F32)) + p_ref[1].astype(F32)) + p_ref[2].astype(F32)

    return pl.pallas_call(
        body, name="sum_chips_own",
        grid_spec=pltpu.PrefetchScalarGridSpec(
            num_scalar_prefetch=1, grid=(r // tr,),
            in_specs=[_own_block_spec(r, c, axis, tr), pl.BlockSpec((N_PEERS, tr, c), lambda i, me: (0, i, 0))],
            out_specs=pl.BlockSpec((tr, c), lambda i, me: (i, 0))),
        out_shape=jax.ShapeDtypeStruct((r, c), F32),
        compiler_params=_cparams(("parallel",)),
    )(me, g, landed)


def _adamw_math(w, m, v, g):
    m_new = ADAM_B1 * m + (1.0 - ADAM_B1) * g
    v_new = ADAM_B2 * v + (1.0 - ADAM_B2) * jnp.square(g)
    m_hat = m_new / (1.0 - ADAM_B1 ** ADAM_STEP)
    v_hat = v_new / (1.0 - ADAM_B2 ** ADAM_STEP)
    return -ADAM_LR * (m_hat / (jnp.sqrt(v_hat) + ADAM_EPS) + ADAM_WD * w), m_new, v_new


def _adamw(w, m, v, g_a, g_b):
    L, r, c = w.shape
    tr = _row_block(r, c, target=1 << 19)

    def body(w_ref, m_ref, v_ref, ga_ref, gb_ref, g_ref, d_ref, nm_ref, nv_ref):
        g = ga_ref[...] + gb_ref[...]
        g_ref[...] = g
        d_ref[...], nm_ref[...], nv_ref[...] = _adamw_math(w_ref[...], m_ref[...], v_ref[...], g)

    spec = pl.BlockSpec((None, tr, c), lambda l, i: (l, i, 0))
    return pl.pallas_call(
        body,
        name="adamw",
        grid=(L, r // tr),
        in_specs=[spec] * 5,
        out_specs=[spec] * 4,
        out_shape=[jax.ShapeDtypeStruct(w.shape, F32)] * 4,
        compiler_params=_cparams(("parallel", "parallel")),
    )(w, m, v, g_a, g_b)


def _adamw_layer(w, m, v, g_a, g_b, layer, outs):
    L, r, c = w.shape
    tr = _row_block(r, c, target=1 << 19)
    n_prev = 0 if outs is None else 4

    def body(w_ref, m_ref, v_ref, ga_ref, gb_ref, *rest):
        g_ref, d_ref, nm_ref, nv_ref = rest[n_prev:]
        g = ga_ref[...] + gb_ref[...]
        g_ref[...] = g
        d_ref[...], nm_ref[...], nv_ref[...] = _adamw_math(w_ref[...], m_ref[...], v_ref[...], g)

    at_layer = pl.BlockSpec((None, tr, c), lambda i: (layer, i, 0))
    flat = pl.BlockSpec((tr, c), lambda i: (i, 0))
    return pl.pallas_call(
        body,
        name="adamw_layer",
        grid=(r // tr,),
        in_specs=[at_layer] * 3 + [flat] * 2 + [ANY_SPEC] * n_prev,
        out_specs=[at_layer] * 4,
        out_shape=[jax.ShapeDtypeStruct(w.shape, F32)] * 4,
        input_output_aliases={5 + k: k for k in range(n_prev)},
        compiler_params=_cparams(("parallel",)),
    )(w, m, v, g_a, g_b, *(outs or ()))


SHARDED = (("w_in", 1), ("conv_w", 1), ("w_mem_k", 0), ("w_mem_v", 0), ("w_branch", 1), ("w_o", 0), ("w_up", 1), ("w_down", 0))
SMALL = ("lower_bounds", "hg_norm_w", "b_gate", "ln1_g", "ln1_b", "ln2_g", "ln2_b")
WEIGHT_ORDER = ("lower_bounds", "w_in", "conv_w", "hg_norm_w", "w_mem_k", "w_mem_v", "w_branch", "b_gate", "w_o", "ln1_g", "ln1_b",
                "w_up", "w_down", "ln2_g", "ln2_b")


def kernel(x, mem, lower_bounds, w_in, conv_w, hg_norm_w, w_mem_k, w_mem_v, w_branch, b_gate, w_o, ln1_g, ln1_b, w_up, w_down, ln2_g, ln2_b, loss_target, m_lower_bounds, m_w_in, m_conv_w, m_hg_norm_w, m_w_mem_k, m_w_mem_v, m_w_branch, m_b_gate, m_w_o, m_ln1_g, m_ln1_b, m_w_up, m_w_down, m_ln2_g, m_ln2_b, v_lower_bounds, v_w_in, v_conv_w, v_hg_norm_w, v_w_mem_k, v_w_mem_v, v_w_branch, v_b_gate, v_w_o, v_ln1_g, v_ln1_b, v_w_up, v_w_down, v_ln2_g, v_ln2_b):
    bl, seq, d = x.shape
    depth = w_in.shape[0]
    weights = dict(lower_bounds=lower_bounds, w_in=w_in, conv_w=conv_w, hg_norm_w=hg_norm_w, w_mem_k=w_mem_k, w_mem_v=w_mem_v,
                   w_branch=w_branch, b_gate=b_gate, w_o=w_o, ln1_g=ln1_g, ln1_b=ln1_b, w_up=w_up, w_down=w_down, ln2_g=ln2_g, ln2_b=ln2_b)
    mom_m = dict(lower_bounds=m_lower_bounds, w_in=m_w_in, conv_w=m_conv_w, hg_norm_w=m_hg_norm_w, w_mem_k=m_w_mem_k, w_mem_v=m_w_mem_v,
                 w_branch=m_w_branch, b_gate=m_b_gate, w_o=m_w_o, ln1_g=m_ln1_g, ln1_b=m_ln1_b, w_up=m_w_up, w_down=m_w_down,
                 ln2_g=m_ln2_g, ln2_b=m_ln2_b)
    mom_v = dict(lower_bounds=v_lower_bounds, w_in=v_w_in, conv_w=v_conv_w, hg_norm_w=v_hg_norm_w, w_mem_k=v_w_mem_k, w_mem_v=v_w_mem_v,
                 w_branch=v_w_branch, b_gate=v_b_gate, w_o=v_w_o, ln1_g=v_ln1_g, ln1_b=v_ln1_b, w_up=v_w_up, w_down=v_w_down,
                 ln2_g=v_ln2_g, ln2_b=v_ln2_b)

    def shard2d(name, l):
        w = weights[name][l]
        if name == "w_branch":
            return w.reshape(N_BRANCH * W, w.shape[-1]).astype(BF16)
        return w if name == "conv_w" else w.astype(BF16)

    me = _my_chip()

    shard_axis = dict(SHARDED)

    def prepare_exchange(name, kind, items):
        ex = _Split(name, [(arr, kind, shard_axis[nm]) for nm, arr in items])
        return ex, ex.landing_zones(me), [nm for nm, _ in items]

    def launch(prepared, deps=()):
        ex, lands, names = prepared
        state, token = ex.start(lands, deps)
        return ex, state, names, token

    def start_exchange(name, kind, items, deps=()):
        return launch(prepare_exchange(name, kind, items), deps)

    def prepare_gathers(l):
        groups = (("in", ("w_in",)), ("mix", ("conv_w", "w_mem_k", "w_mem_v")), ("rest", ("w_branch", "w_o", "w_up", "w_down")))
        return tuple(prepare_exchange(f"gather_{tag}_l{l}", "gather", [(nm, shard2d(nm, l)) for nm in names]) for tag, names in groups)

    def start_gathers(prepared, deps=()):
        started = []
        for prep in prepared:
            started.append(launch(prep, deps))
            deps = (started[-1][3],)
        return tuple(started)

    def gathered(pend, after):
        ex, state, names, _ = pend
        return dict(zip(names, ex.wait(state, after=after)[1]))

    pending = start_gathers(prepare_gathers(0))
    tokens = tuple(pend[3] for pend in pending)
    tokens, x, mem, loss_target, weights, mom_m, mom_v = lax.optimization_barrier((tokens, x, mem, loss_target, weights, mom_m, mom_v))
    pending = tuple((*pend[:3], tok) for pend, tok in zip(pending, tokens))
    lower_bounds = weights["lower_bounds"]

    x2d, mem2, t2d = x.reshape(bl * seq, d), mem.reshape(-1, d), loss_target.reshape(bl * seq, d)
    alpha = (2.0 * depth) ** 0.25
    soft, lb_all = _lower_bounds_fwd(lower_bounds)

    prepared = [None] + [prepare_gathers(l) for l in range(1, depth)]
    early = [x2d.astype(BF16), lb_all] + [z for prep in prepared[1:] for _, lands, _ in prep for z in lands]

    h, hb, saved, layer_wts = x2d, early[0], [], []
    for l in range(depth):
        first, mix, rest = pending
        w_in_l = gathered(first, early if l == 0 else h)["w_in"]

        def mix_fn(after, l=l, mix=mix):
            return dict(gathered(mix, after), hg_norm_w=weights["hg_norm_w"][l][None, :])

        def late_fn(after, l=l, rest=rest):
            wts = gathered(rest, after)
            for name in ("b_gate", "ln1_g", "ln1_b", "ln2_g", "ln2_b"):
                wts[name] = weights[name][l][None, :]
            return wts

        deps = (rest[3],)
        if l + 1 < depth:
            pending = start_gathers(prepared[l + 1], (w_in_l, rest[3]))
            deps += tuple(pend[3] for pend in pending)
        h, hb, sv, wts = _layer_fwd(h, hb, mem2, lb_all[l:l + 1], w_in_l, mix_fn, late_fn, bl=bl, seq=seq, alpha=alpha, deps=deps)
        saved.append(sv)
        layer_wts.append(wts)
    loss, dz2, dz2b, dg2, db2 = _loss_head(h, t2d, saved[-1]["xhat2"], saved[-1]["rstd2"], layer_wts[-1]["ln2_g"])

    shape3 = {name: (depth, weights[name].size // (depth * weights[name].shape[-1]), weights[name].shape[-1]) for name, _ in SHARDED}
    partial = [dict() for _ in range(depth)]
    smalls = [None] * depth
    outs = {name: None for name, _ in SHARDED}

    def finish_reduce(pend, l, after):
        ex, state, names, _ = pend
        sent, got = ex.wait(state, after=after)
        for nm, g_full, landed in zip(names, sent, got):
            partial[l][nm] = _sum_own_and_peers(me, g_full, shard_axis[nm], landed)

    names_sharded = [name for name, _ in SHARDED]

    def start_swap(l):
        swap = _SiblingSplit(f"swap_partials_l{l}", [partial[l][nm] for nm in names_sharded])
        state, token = swap.start()
        return swap, state, token

    def optimizer_step(l, pend, after):
        swap, state, _ = pend
        mine, theirs = swap.wait(state, after)
        for nm, own, other in zip(names_sharded, mine, theirs):
            outs[nm] = _adamw_layer(weights[nm].reshape(shape3[nm]), mom_m[nm].reshape(shape3[nm]), mom_v[nm].reshape(shape3[nm]),
                                    own, other, l, outs[nm])
        return tuple(outs[nm][0] for nm in names_sharded)

    pending_mix, pending_swap, deps = [], None, ()
    for l in reversed(range(depth)):
        dz1, dz1b, g_mlp = _mlp_bwd(dz2, dz2b, saved[l], layer_wts[l], alpha=alpha, deps=deps)
        g_mlp["ln2_g"], g_mlp["ln2_b"] = dg2[0:1], db2[0:1]
        pending_mlp = start_exchange(f"reduce_mlp_l{l}", "scatter", [(nm, g_mlp[nm]) for nm in ("w_up", "w_down")])
        deps = (pending_mlp[3],)
        if pending_mix:
            for pend in pending_mix:
                finish_reduce(pend, l + 1, dz1)
            pending_swap = start_swap(l + 1)
            deps += (pending_swap[2],)
        pending_mix = []

        def send(names, g, l=l, pending_mix=pending_mix):
            pend = start_exchange(f"reduce_{names[0]}_l{l}", "scatter", [(nm, g[nm]) for nm in names])
            pending_mix.append(pend)
            return pend[3]

        below = (saved[l - 1]["xhat2"], saved[l - 1]["rstd2"], layer_wts[l - 1]["ln2_g"]) if l > 0 else None
        out, g = _mix_bwd(dz1, dz1b, saved[l], mem2, lb_all[l:l + 1], layer_wts[l], bl=bl, seq=seq, alpha=alpha, send=send,
                          below=below, deps=deps)
        if l > 0:
            dz2, dz2b, dg2, db2 = out
        else:
            dh = out
        finish_reduce(pending_mlp, l, out[0] if l > 0 else out)
        deps = ()
        if pending_swap is not None:
            deps = optimizer_step(l + 1, pending_swap, out[0] if l > 0 else out)
            pending_swap = None
        g.update(g_mlp, lower_bounds=g["lb"])
        smalls[l] = jnp.concatenate([g[nm] for nm in SMALL], axis=1)
    small_parts = _chip_exchange("reduce_small", [(jnp.stack(smalls), "bcast", 0)])[0]
    small_sum = _sum_chips_into(small_parts.reshape(N_CHIPS, depth, -1), jnp.zeros((1, depth, small_parts.shape[-1]), F32), 0)
    small_sum = small_sum.reshape(depth, 1, -1)
    small_theirs = _sibling_swap("swap_small", [small_sum])[0]
    for pend in pending_mix:
        finish_reduce(pend, 0, small_theirs)
    optimizer_step(0, start_swap(0), small_theirs)

    outs = {name: [r.reshape(weights[name].shape) for r in res] for name, res in outs.items()}
    off = 0
    for name in SMALL:
        n = weights[name].shape[1]
        mine, other = small_sum[:, :, off:off + n], small_theirs[:, :, off:off + n]
        off += n
        if name == "lower_bounds":
            mine = _lower_bounds_bwd(soft, mine[:, 0, :])[:, None, :]
            other = _lower_bounds_bwd(soft, other[:, 0, :])[:, None, :]
        shp = (depth, 1, n)
        res = _adamw(weights[name].reshape(shp), mom_m[name].reshape(shp), mom_v[name].reshape(shp), mine, other)
        outs[name] = [r.reshape(weights[name].shape) for r in res]
    assert off == small_sum.shape[-1]

    total_loss = lax.psum(loss[0, 0], ("x", "y", "c"))
    result = [total_loss, dh.reshape(bl, seq, d)]
    for k in range(4):
        result += [outs[name][k] for name in WEIGHT_ORDER]
    return tuple(result)
```

```python
import functools

import jax
import jax.numpy as jnp
from jax import lax
from jax.experimental import pallas as pl
from jax.experimental.pallas import tpu as pltpu

F32 = jnp.float32
BF16 = jnp.bfloat16

HG_HEADS = 4
HG_F = 128
HG_CHUNK = 32
MEM_HEADS = 4
MEM_HEAD_DIM = 128
BRANCH_WIDTH = 512
N_BRANCH = 3
CONV_K = 3
LN_EPS = 1e-5
RMS_EPS = 1e-6
ADAM_LR = 0.001
ADAM_B1 = 0.9
ADAM_B2 = 0.999
ADAM_EPS = 1e-08
ADAM_WD = 0.01
ADAM_STEP = 10

VMEM_LIMIT = 48 * 1024 * 1024


def _cparams(sem):
    return pltpu.CompilerParams(dimension_semantics=sem, vmem_limit_bytes=VMEM_LIMIT)


def _dot(a, b, dims):
    return lax.dot_general(a, b, (dims, ((), ())), preferred_element_type=F32)


NN = ((1,), (0,))
NT = ((1,), (1,))
TN = ((0,), (0,))


def _pick(n, pref):
    for t in pref:
        if n % t == 0:
            return t
    return n


ANY_SPEC = pl.BlockSpec(memory_space=pl.ANY)


def _matmul(name, a, b, *, mode, out_dtype=F32, a_fn=None, a_extra=(), epi_fn=None, epi_extra=(), n_out=1, out_kinds=None,
            tm=512, tn=1024, tk=1024, deps=()):
    M, K = a.shape
    N = b.shape[1] if mode == "nn" else b.shape[0]
    tm, tn, tk = _pick(M, (tm, 256, 128, 8)), _pick(N, (tn, 896, 512, 256, 128)), _pick(K, (tk, 512, 256, 128))
    nk = K // tk
    n_ax, n_ex = len(a_extra), len(epi_extra)
    n_in = 2 + n_ax + n_ex + len(deps)
    out_dtypes = out_dtype if isinstance(out_dtype, (tuple, list)) else (out_dtype,) * n_out
    out_kinds = out_kinds or ("tile",) * n_out

    def body(*refs):
        a_ref, b_ref = refs[0], refs[1]
        ax_refs = refs[2:2 + n_ax]
        ex_refs = refs[2 + n_ax:2 + n_ax + n_ex]
        o_refs = refs[n_in:n_in + n_out]
        at = a_ref[...]
        at = a_fn(at, *[r[...] for r in ax_refs]) if a_fn is not None else at.astype(BF16)
        part = _dot(at, b_ref[...].astype(BF16), NN if mode == "nn" else NT)

        def finish(acc):
            outs = epi_fn(acc, *[r[...] for r in ex_refs]) if epi_fn is not None else (acc,)
            for o_ref, o, kind in zip(o_refs, outs, out_kinds):
                if kind == "rowsum":
                    @pl.when(pl.program_id(1) == 0)
                    def _(o_ref=o_ref):
                        o_ref[...] = jnp.zeros_like(o_ref)

                    o_ref[0:1, :] += o
                else:
                    o_ref[...] = o.astype(o_ref.dtype)

        if nk == 1:
            finish(part)
            return
        acc_ref = refs[-1]
        k = pl.program_id(2)

        @pl.when(k == 0)
        def _():
            acc_ref[...] = part

        @pl.when(jnp.logical_and(k > 0, k < nk - 1))
        def _():
            acc_ref[...] += part

        @pl.when(k == nk - 1)
        def _():
            finish(acc_ref[...] + part)

    b_mode = dict(pipeline_mode=pl.Buffered(1)) if (nk == 1 and N == tn) else {}
    in_specs = [pl.BlockSpec((tm, tk), lambda j, i, k: (i, k)),
                pl.BlockSpec((tk, tn), lambda j, i, k: (k, j), **b_mode) if mode == "nn"
                else pl.BlockSpec((tn, tk), lambda j, i, k: (j, k), **b_mode)]
    in_specs += [pl.BlockSpec((1, tk), lambda j, i, k: (0, k)) for _ in a_extra]
    for e in epi_extra:
        if e.shape[0] == 1:
            in_specs.append(pl.BlockSpec((1, tn), lambda j, i, k: (0, j)))
        elif e.shape[1] == 1:
            in_specs.append(pl.BlockSpec((tm, 1), lambda j, i, k: (i, 0)))
        else:
            in_specs.append(pl.BlockSpec((tm, tn), lambda j, i, k: (i, j)))
    in_specs += [ANY_SPEC] * len(deps)
    out_specs, out_shapes = [], []
    for kind, dt in zip(out_kinds, out_dtypes):
        if kind == "col":
            out_specs.append(pl.BlockSpec((tm, 1), lambda j, i, k: (i, 0)))
            out_shapes.append(jax.ShapeDtypeStruct((M, 1), dt))
        elif kind == "rowsum":
            out_specs.append(pl.BlockSpec((8, tn), lambda j, i, k: (0, j)))
            out_shapes.append(jax.ShapeDtypeStruct((8, N), dt))
        else:
            out_specs.append(pl.BlockSpec((tm, tn), lambda j, i, k: (i, j)))
            out_shapes.append(jax.ShapeDtypeStruct((M, N), dt))
    out = pl.pallas_call(
        body,
        name=name,
        grid=(N // tn, M // tm, nk),
        in_specs=in_specs,
        out_specs=out_specs,
        out_shape=out_shapes,
        scratch_shapes=[pltpu.VMEM((tm, tn), F32)] if nk > 1 else [],
        compiler_params=_cparams(("arbitrary", "arbitrary", "arbitrary")),
    )(a, b, *a_extra, *epi_extra, *deps)
    return out[0] if n_out == 1 else out


def _matmul_tn(name, a, b, *, a_fn=None, a_extra=(), a_cols=None, b_cols=None, ta=1024, tb=1024, tt=1024, out_dtype=F32, deps=()):
    T = a.shape[0]
    a0, Ka = a_cols if a_cols is not None else (0, a.shape[1])
    b0, Nb = b_cols if b_cols is not None else (0, b.shape[1])
    ta, tb, tt = _pick(Ka, (ta, 512, 256, 128)), _pick(Nb, (tb, 896, 512, 256, 128)), _pick(T, (tt, 512, 256, 128))
    assert a0 % ta == 0 and b0 % tb == 0
    a0, b0 = a0 // ta, b0 // tb
    nt = T // tt
    n_ax = len(a_extra)

    def body(*refs):
        a_ref, b_ref = refs[0], refs[1]
        ax_refs = refs[2:2 + n_ax]
        o_ref = refs[2 + n_ax + len(deps)]
        acc_ref = refs[-1]
        t = pl.program_id(2)
        at = a_ref[...]
        at = a_fn(at, *[r[...] for r in ax_refs]) if a_fn is not None else at.astype(BF16)
        part = _dot(at, b_ref[...].astype(BF16), TN)

        @pl.when(t == 0)
        def _():
            acc_ref[...] = part

        @pl.when(jnp.logical_and(t > 0, t < nt - 1))
        def _():
            acc_ref[...] += part

        @pl.when(t == nt - 1)
        def _():
            o_ref[...] = (acc_ref[...] + part if nt > 1 else part).astype(o_ref.dtype)

    in_specs = [pl.BlockSpec((tt, ta), lambda i, j, t: (t, a0 + i)), pl.BlockSpec((tt, tb), lambda i, j, t: (t, b0 + j))]
    in_specs += [pl.BlockSpec((1, ta), lambda i, j, t: (0, a0 + i)) for _ in a_extra]
    in_specs += [ANY_SPEC] * len(deps)
    return pl.pallas_call(
        body,
        name=name,
        grid=(Ka // ta, Nb // tb, nt),
        in_specs=in_specs,
        out_specs=pl.BlockSpec((ta, tb), lambda i, j, t: (i, j)),
        out_shape=jax.ShapeDtypeStruct((Ka, Nb), out_dtype),
        scratch_shapes=[pltpu.VMEM((ta, tb), F32)],
        compiler_params=_cparams(("parallel", "parallel", "arbitrary")),
    )(a, b, *a_extra, *deps)


W = BRANCH_WIDTH
C_CB, C_CC, C_CH, C_HQ, C_HF, C_HI, C_HG, C_MQ, N_MIX = 0, W, 2 * W, 3 * W, 4 * W, 5 * W, 6 * W, 7 * W, 8 * W
TS_MIX = 256
PREV_ROWS = 16
KEEP_NAMES = ("sq", "qs", "k", "sig", "f", "ea", "eb", "eq", "ek")


def _sigmoid(x):
    return jax.nn.sigmoid(x)


def _chunk_pos(shape):
    return lax.broadcasted_iota(jnp.int32, shape, 0) & (HG_CHUNK - 1)


def _seg_cumsum(x, pos):
    sh = 1
    while sh < HG_CHUNK:
        x = x + jnp.where(pos >= sh, pltpu.roll(x, sh, 0), 0.0)
        sh *= 2
    return x


def _seg_rev_cumsum(x, pos):
    n = x.shape[0]
    sh = 1
    while sh < HG_CHUNK:
        x = x + jnp.where(pos < HG_CHUNK - sh, pltpu.roll(x, n - sh, 0), 0.0)
        sh *= 2
    return x


def _chunk_mask(ts):
    r = lax.broadcasted_iota(jnp.int32, (ts, ts), 0)
    c = lax.broadcasted_iota(jnp.int32, (ts, ts), 1)
    return jnp.logical_and((r // HG_CHUNK) == (c // HG_CHUNK), c <= r)


def _hgrn_gates(p_ref, lb):
    q = p_ref[:, C_HQ:C_HQ + W].astype(F32)
    fl = p_ref[:, C_HF:C_HF + W].astype(F32)
    sig = _sigmoid(fl)
    f = lb + (1.0 - lb) * sig
    logf = jnp.log(f)
    k = (1.0 - lb) * _sigmoid(-fl)
    sq = _sigmoid(q)
    qs = q * sq
    return q, sq, qs, sig, f, logf, k


def _hgrn_decays(logf, bc_sc, ts):
    pos = _chunk_pos(logf.shape)
    bc = _seg_cumsum(logf, pos)
    bc_sc[...] = bc
    nc = ts // HG_CHUNK
    bref = jnp.concatenate(
        [jnp.broadcast_to(bc_sc[n * HG_CHUNK + HG_CHUNK // 2 - 1:n * HG_CHUNK + HG_CHUNK // 2, :], (HG_CHUNK, W)) for n in range(nc)], axis=0)
    blast = jnp.concatenate(
        [jnp.broadcast_to(bc_sc[(n + 1) * HG_CHUNK - 1:(n + 1) * HG_CHUNK, :], (HG_CHUNK, W)) for n in range(nc)], axis=0)
    return pos, bc, bref, blast


def _conv_shift_down(u, carry_ref, row):
    n = carry_ref.shape[0]
    last, before = carry_ref[n - 1:n, :], carry_ref[n - 2:n - 1, :]
    u1 = jnp.where(row == 0, last, pltpu.roll(u, 1, 0))
    u2 = jnp.where(row == 0, before, jnp.where(row == 1, last, pltpu.roll(u, 2, 0)))
    return u1, u2


def _attn_probs(qh, kh):
    s = _dot(qh, kh, NT) * (MEM_HEAD_DIM ** -0.5)
    e = jnp.exp(s - jnp.max(s, axis=-1, keepdims=True))
    return e / jnp.sum(e, axis=-1, keepdims=True)


def _mixer_fwd(p, mk, mv, lb, conv_w, norm_w, *, bl, seq):
    T = p.shape[0]
    ts = TS_MIX
    ns = seq // ts
    nc = ts // HG_CHUNK
    ml = mk.shape[0] // bl

    def body(p_ref, mk_ref, mv_ref, lb_ref, cw_ref, nw_ref, y_ref, st_ref, opre_ref, state_sc, carry_sc, bc_sc):
        @pl.when(pl.program_id(1) == 0)
        def _():
            state_sc[...] = jnp.zeros_like(state_sc)
            carry_sc[...] = jnp.zeros_like(carry_sc)

        cb, cc, ch = (p_ref[:, c0:c0 + W].astype(F32) for c0 in (C_CB, C_CC, C_CH))
        u = cc * ch
        row = lax.broadcasted_iota(jnp.int32, (ts, W), 0)
        u1, u2 = _conv_shift_down(u, carry_sc, row)
        yconv = u2 * cw_ref[0:1, :] + u1 * cw_ref[1:2, :] + u * cw_ref[2:3, :]
        y_ref[:, 0:W] = (cb * yconv).astype(BF16)
        carry_sc[...] = u[ts - 8:ts, :]

        lbv = lb_ref[...]
        _, _, qs, _, _, logf, k = _hgrn_gates(p_ref, lbv)
        pos, bc, bref, blast = _hgrn_decays(logf, bc_sc, ts)
        a_all = (qs * jnp.exp(bc - bref)).astype(BF16)
        bk_all = (k * jnp.exp(bref - bc)).astype(BF16)
        qin_all = (qs * jnp.exp(bc)).astype(BF16)
        kout_all = (k * jnp.exp(blast - bc)).astype(BF16)
        v_all = p_ref[:, C_HI:C_HI + W].astype(BF16)
        mask = _chunk_mask(ts)
        heads = [slice(h * HG_F, (h + 1) * HG_F) for h in range(HG_HEADS)]
        st = [state_sc[h] for h in range(HG_HEADS)]
        o_inter = [[] for _ in range(HG_HEADS)]
        for n in range(nc):
            rows = slice(n * HG_CHUNK, (n + 1) * HG_CHUNK)
            for h, hs in enumerate(heads):
                st_ref[n, h] = st[h]
                o_inter[h].append(_dot(qin_all[rows, hs], st[h].astype(BF16), NT))
                kv = _dot(v_all[rows, hs], kout_all[rows, hs], TN)
                decay = jnp.exp(bc_sc[(n + 1) * HG_CHUNK - 1:(n + 1) * HG_CHUNK, hs])
                st[h] = st[h] * decay + kv
        for h in range(HG_HEADS):
            state_sc[h] = st[h]
        scores = [_dot(a_all[:, hs], bk_all[:, hs], NT) for hs in heads]
        scores = [jnp.where(mask, s, 0.0).astype(BF16) for s in scores]
        outs = [_dot(scores[h], v_all[:, hs], NN) + jnp.concatenate(o_inter[h], axis=0) for h, hs in enumerate(heads)]
        for h, hs in enumerate(heads):
            o = outs[h]
            opre_ref[:, hs] = o
            on = o * lax.rsqrt(jnp.mean(o * o, axis=-1, keepdims=True) + RMS_EPS) * nw_ref[...]
            g = p_ref[:, C_HG + h * HG_F:C_HG + (h + 1) * HG_F].astype(F32)
            y_ref[:, W + h * HG_F:W + (h + 1) * HG_F] = (on * (g * _sigmoid(g))).astype(BF16)

        mheads = [slice(h * MEM_HEAD_DIM, (h + 1) * MEM_HEAD_DIM) for h in range(MEM_HEADS)]
        probs = [_attn_probs(p_ref[:, C_MQ + h * MEM_HEAD_DIM:C_MQ + (h + 1) * MEM_HEAD_DIM].astype(BF16), mk_ref[:, hs])
                 for h, hs in enumerate(mheads)]
        for h, hs in enumerate(mheads):
            y_ref[:, 2 * W + h * MEM_HEAD_DIM:2 * W + (h + 1) * MEM_HEAD_DIM] = _dot(
                probs[h].astype(BF16), mv_ref[:, hs], NN).astype(BF16)

    return pl.pallas_call(
        body,
        name="mixer_fwd",
        grid=(bl, ns),
        in_specs=[
            pl.BlockSpec((ts, N_MIX), lambda b, s: (b * ns + s, 0)),
            pl.BlockSpec((ml, W), lambda b, s: (b, 0)),
            pl.BlockSpec((ml, W), lambda b, s: (b, 0)),
            pl.BlockSpec((1, W), lambda b, s: (0, 0)),
            pl.BlockSpec((CONV_K, W), lambda b, s: (0, 0)),
            pl.BlockSpec((1, HG_F), lambda b, s: (0, 0)),
        ],
        out_specs=[
            pl.BlockSpec((ts, 3 * W), lambda b, s: (b * ns + s, 0)),
            pl.BlockSpec((nc, HG_HEADS, HG_F, HG_F), lambda b, s: (b * ns + s, 0, 0, 0)),
            pl.BlockSpec((ts, W), lambda b, s: (b * ns + s, 0)),
        ],
        out_shape=[
            jax.ShapeDtypeStruct((T, 3 * W), BF16),
            jax.ShapeDtypeStruct((T // HG_CHUNK, HG_HEADS, HG_F, HG_F), F32),
            jax.ShapeDtypeStruct((T, W), F32),
        ],
        scratch_shapes=[pltpu.VMEM((HG_HEADS, HG_F, HG_F), F32), pltpu.VMEM((8, W), F32), pltpu.VMEM((ts, W), F32)],
        compiler_params=_cparams(("arbitrary", "arbitrary")),
    )(p, mk, mv, lb, conv_w, norm_w)


def _mixer_bwd(p, dy, dp_gates, st, opre, mk, mv, lb, conv_w, norm_w, *, bl, seq, deps=()):
    T, nin = p.shape
    ts = TS_MIX
    ns = seq // ts
    nc = ts // HG_CHUNK
    ml = mk.shape[0] // bl
    mid, last = HG_CHUNK // 2 - 1, HG_CHUNK - 1

    def body(p_ref, pprev_ref, dy_ref, dpin_ref, st_ref, opre_ref, mk_ref, mv_ref, lb_ref, cw_ref, nw_ref, *rest):
        (dp_ref, dmk_ref, dmv_ref, dcw_ref, dnw_ref, dlb_ref, dstate_sc, carry_sc, uprev_sc, ab_sc, bkb_sc, qinb_sc, koutb_sc,
         dob_sc, dv_sc, da_sc, dbk_sc, dqin_sc, dkout_sc, dec_sc, ddec_sc, *keep_scs) = rest[len(deps):]
        del dpin_ref
        b, s = pl.program_id(0), pl.program_id(1)

        @pl.when(s == 0)
        def _():
            dstate_sc[...] = jnp.zeros_like(dstate_sc)
            carry_sc[...] = jnp.zeros_like(carry_sc)
            dmk_ref[...] = jnp.zeros_like(dmk_ref)
            dmv_ref[...] = jnp.zeros_like(dmv_ref)

        @pl.when(jnp.logical_and(b == 0, s == 0))
        def _():
            dcw_ref[...] = jnp.zeros_like(dcw_ref)
            dnw_ref[...] = jnp.zeros_like(dnw_ref)
            dlb_ref[...] = jnp.zeros_like(dlb_ref)

        cb, cc, ch = (p_ref[:, c0:c0 + W].astype(F32) for c0 in (C_CB, C_CC, C_CH))
        u = cc * ch
        row = lax.broadcasted_iota(jnp.int32, (ts, W), 0)
        uprev = pprev_ref[:, C_CC:C_CC + W].astype(F32) * pprev_ref[:, C_CH:C_CH + W].astype(F32)
        uprev_sc[...] = jnp.where(s == ns - 1, 0.0, uprev)
        u1, u2 = _conv_shift_down(u, uprev_sc, row)
        w0, w1, w2 = cw_ref[0:1, :], cw_ref[1:2, :], cw_ref[2:3, :]
        dya = dy_ref[:, 0:W].astype(F32)
        dp_ref[:, C_CB:C_CB + W] = (dya * (u2 * w0 + u1 * w1 + u * w2)).astype(BF16)
        dv = cb * dya
        dv1 = jnp.where(row == ts - 1, carry_sc[0:1, :], pltpu.roll(dv, ts - 1, 0))
        dv2 = jnp.where(row == ts - 1, carry_sc[1:2, :], jnp.where(row == ts - 2, carry_sc[0:1, :], pltpu.roll(dv, ts - 2, 0)))
        du = dv * w2 + dv1 * w1 + dv2 * w0
        dp_ref[:, C_CC:C_CC + W] = (du * ch).astype(BF16)
        dp_ref[:, C_CH:C_CH + W] = (du * cc).astype(BF16)
        dcw_ref[0:1, :] += jnp.sum(dv * u2, axis=0, keepdims=True)
        dcw_ref[1:2, :] += jnp.sum(dv * u1, axis=0, keepdims=True)
        dcw_ref[2:3, :] += jnp.sum(dv * u, axis=0, keepdims=True)
        carry_sc[...] = dv[0:8, :]

        mask = _chunk_mask(ts)
        pos_c = _chunk_pos((HG_CHUNK, HG_F))
        nw = nw_ref[...]

        def block(n, h):
            rows = slice(n * HG_CHUNK, (n + 1) * HG_CHUNK)
            return rows, slice(h * HG_F, (h + 1) * HG_F)

        keep = dict(zip(KEEP_NAMES, keep_scs))

        def gates(rows, h):
            lbh = lb_ref[:, h * HG_F:(h + 1) * HG_F]
            q = p_ref[rows, C_HQ + h * HG_F:C_HQ + (h + 1) * HG_F].astype(F32)
            fl = p_ref[rows, C_HF + h * HG_F:C_HF + (h + 1) * HG_F].astype(F32)
            sig = _sigmoid(fl)
            f = lbh + (1.0 - lbh) * sig
            k = (1.0 - lbh) * _sigmoid(-fl)
            sq = _sigmoid(q)
            qs = q * sq
            bc = _seg_cumsum(jnp.log(f), pos_c)
            bref = jnp.sum(jnp.where(pos_c == mid, bc, 0.0), axis=0, keepdims=True)
            blast = jnp.sum(jnp.where(pos_c == last, bc, 0.0), axis=0, keepdims=True)
            ea, eb, eq, ek = jnp.exp(bc - bref), jnp.exp(bref - bc), jnp.exp(bc), jnp.exp(blast - bc)
            return dict(sq=sq, qs=qs, k=k, sig=sig, f=f, ea=ea, eb=eb, eq=eq, ek=ek), blast

        dnw = jnp.zeros((1, HG_F), F32)
        for n in range(nc):
            for h in range(HG_HEADS):
                rows, hs = block(n, h)
                fw, blast = gates(rows, h)
                for name in KEEP_NAMES:
                    keep[name][rows, hs] = fw[name]
                ab_sc[rows, hs] = (fw["qs"] * fw["ea"]).astype(BF16)
                bkb_sc[rows, hs] = (fw["k"] * fw["eb"]).astype(BF16)
                qinb_sc[rows, hs] = (fw["qs"] * fw["eq"]).astype(BF16)
                koutb_sc[rows, hs] = (fw["k"] * fw["ek"]).astype(BF16)
                dec_sc[n:n + 1, hs] = jnp.exp(blast)
                o = opre_ref[rows, hs]
                g = p_ref[rows, C_HG + h * HG_F:C_HG + (h + 1) * HG_F].astype(F32)
                sg = _sigmoid(g)
                r = lax.rsqrt(jnp.mean(o * o, axis=-1, keepdims=True) + RMS_EPS)
                dyb = dy_ref[rows, W + h * HG_F:W + (h + 1) * HG_F].astype(F32)
                dp_ref[rows, C_HG + h * HG_F:C_HG + (h + 1) * HG_F] = (
                    dyb * (o * r * nw) * (sg * (1.0 + g * (1.0 - sg)))).astype(BF16)
                don = dyb * (g * sg)
                dnw = dnw + jnp.sum(don * o * r, axis=0, keepdims=True)
                dn = don * nw
                dob_sc[rows, hs] = (r * (dn - o * (r * r) * jnp.mean(dn * o, axis=-1, keepdims=True))).astype(BF16)
        dnw_ref[0:1, :] += dnw

        heads = [slice(h * HG_F, (h + 1) * HG_F) for h in range(HG_HEADS)]
        scores = [_dot(ab_sc[:, hs], bkb_sc[:, hs], NT) for hs in heads]
        dscores = [_dot(dob_sc[:, hs], p_ref[:, C_HI + h * HG_F:C_HI + (h + 1) * HG_F].astype(BF16), NT)
                   for h, hs in enumerate(heads)]
        scores = [jnp.where(mask, s, 0.0).astype(BF16) for s in scores]
        dscores = [jnp.where(mask, s, 0.0).astype(BF16) for s in dscores]
        for h, hs in enumerate(heads):
            dv_sc[:, hs] = _dot(scores[h], dob_sc[:, hs], TN)
            da_sc[:, hs] = _dot(dscores[h], bkb_sc[:, hs], NN)
            dbk_sc[:, hs] = _dot(dscores[h], ab_sc[:, hs], TN)
        dst = [dstate_sc[h] for h in range(HG_HEADS)]
        for n in reversed(range(nc)):
            for h in range(HG_HEADS):
                rows, hs = block(n, h)
                st_n = st_ref[n, h]
                decay = dec_sc[n:n + 1, hs]
                dstb = dst[h].astype(BF16)
                dob_n = dob_sc[rows, hs]
                dv_sc[rows, hs] += _dot(koutb_sc[rows, hs], dstb, NT)
                dkout_sc[rows, hs] = _dot(p_ref[rows, C_HI + h * HG_F:C_HI + (h + 1) * HG_F].astype(BF16), dstb, NN)
                ddec_sc[n:n + 1, hs] = jnp.sum(dst[h] * st_n, axis=0, keepdims=True) * decay
                dqin_sc[rows, hs] = _dot(dob_n, st_n.astype(BF16), NN)
                dst[h] = dst[h] * decay + _dot(dob_n, qinb_sc[rows, hs], TN)
        for h in range(HG_HEADS):
            dstate_sc[h] = dst[h]

        for h in range(HG_HEADS):
            dlb = jnp.zeros((1, HG_F), F32)
            for n in range(nc):
                rows, hs = block(n, h)
                fw = {name: keep[name][rows, hs] for name in KEEP_NAMES}
                lbh = lb_ref[:, h * HG_F:(h + 1) * HG_F]
                q = p_ref[rows, C_HQ + h * HG_F:C_HQ + (h + 1) * HG_F].astype(F32)
                da, dbk, dqin, dkout = da_sc[rows, hs], dbk_sc[rows, hs], dqin_sc[rows, hs], dkout_sc[rows, hs]
                w_a, w_b, w_q, w_k = da * fw["ea"], dbk * fw["eb"], dqin * fw["eq"], dkout * fw["ek"]
                dqs, dk = w_a + w_q, w_b + w_k
                t_a, t_b, t_q, t_k = w_a * fw["qs"], w_b * fw["k"], w_q * fw["qs"], w_k * fw["k"]
                s_ref = jnp.sum(t_b - t_a, axis=0, keepdims=True)
                s_last = jnp.sum(t_k, axis=0, keepdims=True) + ddec_sc[n:n + 1, hs]
                dbc = (t_a - t_b + t_q - t_k) + jnp.where(pos_c == mid, s_ref, 0.0) + jnp.where(pos_c == last, s_last, 0.0)
                dfk = _seg_rev_cumsum(dbc, pos_c) / fw["f"] - dk
                sig, sq = fw["sig"], fw["sq"]
                dp_ref[rows, C_HF + h * HG_F:C_HF + (h + 1) * HG_F] = (dfk * (1.0 - lbh) * sig * (1.0 - sig)).astype(BF16)
                dlb = dlb + jnp.sum(dfk * (1.0 - sig), axis=0, keepdims=True)
                dp_ref[rows, C_HQ + h * HG_F:C_HQ + (h + 1) * HG_F] = (dqs * (sq * (1.0 + q * (1.0 - sq)))).astype(BF16)
                dp_ref[rows, C_HI + h * HG_F:C_HI + (h + 1) * HG_F] = dv_sc[rows, hs].astype(BF16)
            dlb_ref[0:1, h * HG_F:(h + 1) * HG_F] += dlb

        mheads = [slice(h * MEM_HEAD_DIM, (h + 1) * MEM_HEAD_DIM) for h in range(MEM_HEADS)]
        qhs = [p_ref[:, C_MQ + h * MEM_HEAD_DIM:C_MQ + (h + 1) * MEM_HEAD_DIM].astype(BF16) for h in range(MEM_HEADS)]
        dobs = [dy_ref[:, 2 * W + h * MEM_HEAD_DIM:2 * W + (h + 1) * MEM_HEAD_DIM].astype(BF16) for h in range(MEM_HEADS)]
        probs = [_attn_probs(qhs[h], mk_ref[:, hs]) for h, hs in enumerate(mheads)]
        dprobs = [_dot(dobs[h], mv_ref[:, hs], NT) for h, hs in enumerate(mheads)]
        for h, hs in enumerate(mheads):
            prob = probs[h]
            dmv_ref[:, hs] += _dot(prob.astype(BF16), dobs[h], TN)
            ds = prob * (dprobs[h] - jnp.sum(dprobs[h] * prob, axis=-1, keepdims=True)) * (MEM_HEAD_DIM ** -0.5)
            dsb = ds.astype(BF16)
            dp_ref[:, C_MQ + h * MEM_HEAD_DIM:C_MQ + (h + 1) * MEM_HEAD_DIM] = _dot(dsb, mk_ref[:, hs], NN).astype(BF16)
            dmk_ref[:, hs] += _dot(dsb, qhs[h], TN)

    def tile(b, s):
        return b * ns + (ns - 1 - s)

    return pl.pallas_call(
        body,
        name="mixer_bwd",
        grid=(bl, ns),
        in_specs=[
            pl.BlockSpec((ts, N_MIX), lambda b, s: (tile(b, s), 0)),
            pl.BlockSpec((PREV_ROWS, N_MIX), lambda b, s: (jnp.maximum(tile(b, s) * (ts // PREV_ROWS) - 1, 0), 0)),
            pl.BlockSpec((ts, 3 * W), lambda b, s: (tile(b, s), 0)),
            pl.BlockSpec(memory_space=pl.ANY),
            pl.BlockSpec((nc, HG_HEADS, HG_F, HG_F), lambda b, s: (tile(b, s), 0, 0, 0)),
            pl.BlockSpec((ts, W), lambda b, s: (tile(b, s), 0)),
            pl.BlockSpec((ml, W), lambda b, s: (b, 0)),
            pl.BlockSpec((ml, W), lambda b, s: (b, 0)),
            pl.BlockSpec((1, W), lambda b, s: (0, 0)),
            pl.BlockSpec((CONV_K, W), lambda b, s: (0, 0)),
            pl.BlockSpec((1, HG_F), lambda b, s: (0, 0)),
        ] + [ANY_SPEC] * len(deps),
        out_specs=[
            pl.BlockSpec((ts, N_MIX), lambda b, s: (tile(b, s), 0)),
            pl.BlockSpec((ml, W), lambda b, s: (b, 0)),
            pl.BlockSpec((ml, W), lambda b, s: (b, 0)),
            pl.BlockSpec((8, W), lambda b, s: (0, 0)),
            pl.BlockSpec((8, HG_F), lambda b, s: (0, 0)),
            pl.BlockSpec((8, W), lambda b, s: (0, 0)),
        ],
        out_shape=[
            jax.ShapeDtypeStruct((T, nin), BF16),
            jax.ShapeDtypeStruct((bl * ml, W), F32),
            jax.ShapeDtypeStruct((bl * ml, W), F32),
            jax.ShapeDtypeStruct((8, W), F32),
            jax.ShapeDtypeStruct((8, HG_F), F32),
            jax.ShapeDtypeStruct((8, W), F32),
        ],
        input_output_aliases={3: 0},
        scratch_shapes=[pltpu.VMEM((HG_HEADS, HG_F, HG_F), F32), pltpu.VMEM((8, W), F32), pltpu.VMEM((PREV_ROWS, W), F32)]
        + [pltpu.VMEM((ts, W), BF16)] * 5 + [pltpu.VMEM((ts, W), F32)] * 5 + [pltpu.VMEM((nc, W), F32)] * 2
        + [pltpu.VMEM((ts, W), F32)] * len(KEEP_NAMES),
        compiler_params=_cparams(("arbitrary", "arbitrary")),
    )(p, p, dy, dp_gates, st, opre, mk, mv, lb, conv_w, norm_w, *deps)


def _layer_norm_stats(z):
    mu = jnp.mean(z, axis=-1, keepdims=True)
    zc = z - mu
    rstd = lax.rsqrt(jnp.mean(zc * zc, axis=-1, keepdims=True) + LN_EPS)
    return zc * rstd, rstd


def _gate_specs(tm, d):
    g0 = N_MIX // d
    return [pl.BlockSpec((tm, d), functools.partial(lambda i, k: (i, g0 + k), k=k)) for k in range(N_BRANCH)]


def _merge_fwd(y, p, x0, wb, wo, bg, ln_g, ln_b, *, alpha, tm=512):
    T, d = x0.shape
    assert N_MIX % d == 0
    tm = _pick(T, (tm, 128, 8))

    def body(y_ref, g0_ref, g1_ref, g2_ref, x_ref, wb_ref, wo_ref, bg_ref, lg_ref, lb_ref, mg_ref, xh_ref, rs_ref, x1b_ref):
        merged = None
        for i, g_ref in enumerate((g0_ref, g1_ref, g2_ref)):
            r = _dot(y_ref[:, i * W:(i + 1) * W], wb_ref[i * W:(i + 1) * W, :], NN)
            t = _sigmoid(g_ref[...].astype(F32) + bg_ref[:, i * d:(i + 1) * d]) * r
            merged = t if merged is None else merged + t
        mb = merged.astype(BF16)
        mg_ref[...] = mb
        z = alpha * x_ref[...] + _dot(mb, wo_ref[...], NN)
        xh, rs = _layer_norm_stats(z)
        xh_ref[...], rs_ref[...] = xh, rs
        x1b_ref[...] = (xh * lg_ref[...] + lb_ref[...]).astype(BF16)

    row = lambda i: (i, 0)
    fix = lambda i: (0, 0)
    return pl.pallas_call(
        body,
        name="merge_fwd",
        grid=(T // tm,),
        in_specs=[pl.BlockSpec((tm, 3 * W), row)] + _gate_specs(tm, d) + [
            pl.BlockSpec((tm, d), row), pl.BlockSpec((3 * W, d), fix, pipeline_mode=pl.Buffered(1)),
            pl.BlockSpec((d, d), fix, pipeline_mode=pl.Buffered(1)), pl.BlockSpec((1, 3 * d), fix),
            pl.BlockSpec((1, d), fix), pl.BlockSpec((1, d), fix)],
        out_specs=[pl.BlockSpec((tm, d), row), pl.BlockSpec((tm, d), row), pl.BlockSpec((tm, 1), row), pl.BlockSpec((tm, d), row)],
        out_shape=[jax.ShapeDtypeStruct((T, d), BF16), jax.ShapeDtypeStruct((T, d), F32), jax.ShapeDtypeStruct((T, 1), F32),
                   jax.ShapeDtypeStruct((T, d), BF16)],
        compiler_params=_cparams(("parallel",)),
    )(y, p, p, p, x0, wb, wo, bg, ln_g, ln_b)


def _merge_bwd(dz, p, y, wb, wo, bg, *, tm=512):
    T, d = dz.shape
    nin = p.shape[1]
    tm = _pick(T, (tm, 128, 8))

    def body(dz_ref, g0_ref, g1_ref, g2_ref, y_ref, wb_ref, wo_ref, bg_ref, dr_ref, dp_ref, dy_ref, dbg_ref):
        @pl.when(pl.program_id(0) == 0)
        def _():
            dbg_ref[...] = jnp.zeros_like(dbg_ref)

        dmerged = _dot(dz_ref[...].astype(BF16), wo_ref[...], NT)
        dp_ref[:, 0:N_MIX] = jnp.zeros((tm, N_MIX), BF16)
        for i, g_ref in enumerate((g0_ref, g1_ref, g2_ref)):
            cs = slice(i * d, (i + 1) * d)
            s = _sigmoid(g_ref[...].astype(F32) + bg_ref[:, cs])
            drb = (dmerged * s).astype(BF16)
            dr_ref[:, cs] = drb
            dgate = dmerged * _dot(y_ref[:, i * W:(i + 1) * W], wb_ref[i * W:(i + 1) * W, :], NN) * s * (1.0 - s)
            dp_ref[:, N_MIX + i * d:N_MIX + (i + 1) * d] = dgate.astype(BF16)
            dbg_ref[0:1, cs] += jnp.sum(dgate, axis=0, keepdims=True)
            dy_ref[:, i * W:(i + 1) * W] = _dot(drb, wb_ref[i * W:(i + 1) * W, :], NT).astype(BF16)

    row = lambda i: (i, 0)
    fix = lambda i: (0, 0)
    return pl.pallas_call(
        body,
        name="merge_bwd",
        grid=(T // tm,),
        in_specs=[pl.BlockSpec((tm, d), row)] + _gate_specs(tm, d) + [
            pl.BlockSpec((tm, 3 * W), row), pl.BlockSpec((3 * W, d), fix, pipeline_mode=pl.Buffered(1)),
            pl.BlockSpec((d, d), fix, pipeline_mode=pl.Buffered(1)), pl.BlockSpec((1, 3 * d), fix)],
        out_specs=[pl.BlockSpec((tm, 3 * d), row), pl.BlockSpec((tm, nin), row), pl.BlockSpec((tm, 3 * W), row),
                   pl.BlockSpec((8, 3 * d), fix)],
        out_shape=[jax.ShapeDtypeStruct((T, 3 * d), BF16), jax.ShapeDtypeStruct((T, nin), BF16),
                   jax.ShapeDtypeStruct((T, 3 * W), BF16), jax.ShapeDtypeStruct((8, 3 * d), F32)],
        compiler_params=_cparams(("arbitrary",)),
    )(dz, p, p, p, y, wb, wo, bg)


MLP_VMEM_LIMIT = 58 * 1024 * 1024


def _mlp_fwd(xhat1, x1b, g1, b1, wu, wd, g2, b2, *, alpha, tm=512, tf=1024):
    T, d = xhat1.shape
    ff = wu.shape[1]
    tm, tf = _pick(T, (tm, 256, 128, 8)), _pick(ff, (tf, 1024, 512, 256, 128))

    def body(xh_ref, x1b_ref, g1_ref, b1_ref, wu_ref, wd_ref, g2_ref, b2_ref, a_ref, xh2_ref, rs2_ref, x2_ref, x2b_ref):
        xb = x1b_ref[...]
        acc = None
        a = _dot(xb, wu_ref[:, 0:tf], NN)
        for c0 in range(0, ff, tf):
            a_next = _dot(xb, wu_ref[:, c0 + tf:c0 + 2 * tf], NN) if c0 + tf < ff else None
            a_ref[:, c0:c0 + tf] = a.astype(BF16)
            part = _dot(jnp.square(jnp.maximum(a, 0.0)).astype(BF16), wd_ref[c0:c0 + tf, :], NN)
            acc = part if acc is None else acc + part
            a = a_next
        x1 = xh_ref[...] * g1_ref[...] + b1_ref[...]
        xh2, rs2 = _layer_norm_stats(alpha * x1 + acc)
        xh2_ref[...] = xh2
        rs2_ref[...] = rs2
        x2 = xh2 * g2_ref[...] + b2_ref[...]
        x2_ref[...] = x2
        x2b_ref[...] = x2.astype(BF16)

    row = lambda i: (i, 0)
    fix = lambda i: (0, 0)
    once = dict(pipeline_mode=pl.Buffered(1))
    return pl.pallas_call(
        body,
        name="mlp_fwd",
        grid=(T // tm,),
        in_specs=[pl.BlockSpec((tm, d), row), pl.BlockSpec((tm, d), row), pl.BlockSpec((1, d), fix), pl.BlockSpec((1, d), fix),
                  pl.BlockSpec((d, ff), fix, **once), pl.BlockSpec((ff, d), fix, **once),
                  pl.BlockSpec((1, d), fix), pl.BlockSpec((1, d), fix)],
        out_specs=[pl.BlockSpec((tm, ff), row), pl.BlockSpec((tm, d), row), pl.BlockSpec((tm, 1), row),
                   pl.BlockSpec((tm, d), row), pl.BlockSpec((tm, d), row)],
        out_shape=[jax.ShapeDtypeStruct((T, ff), BF16), jax.ShapeDtypeStruct((T, d), F32), jax.ShapeDtypeStruct((T, 1), F32),
                   jax.ShapeDtypeStruct((T, d), F32), jax.ShapeDtypeStruct((T, d), BF16)],
        compiler_params=pltpu.CompilerParams(dimension_semantics=("parallel",), vmem_limit_bytes=MLP_VMEM_LIMIT),
    )(xhat1, x1b, g1, b1, wu, wd, g2, b2)


def _ln_bwd(dy, xhat, rstd, g, *, tm=1024, deps=()):
    T, d = dy.shape
    tm = _pick(T, (tm, 256, 128, 8))

    def body(dy_ref, xh_ref, rs_ref, g_ref, *rest):
        dz_ref, dzb_ref, dg_ref, db_ref = rest[len(deps):]

        @pl.when(pl.program_id(0) == 0)
        def _():
            dg_ref[...] = jnp.zeros_like(dg_ref)
            db_ref[...] = jnp.zeros_like(db_ref)

        dy_, xh = dy_ref[...], xh_ref[...]
        dg_ref[0:1, :] += jnp.sum(dy_ * xh, axis=0, keepdims=True)
        db_ref[0:1, :] += jnp.sum(dy_, axis=0, keepdims=True)
        dxh = dy_ * g_ref[...]
        dz = rs_ref[...] * (dxh - jnp.mean(dxh, axis=-1, keepdims=True) - xh * jnp.mean(dxh * xh, axis=-1, keepdims=True))
        dz_ref[...] = dz
        dzb_ref[...] = dz.astype(BF16)

    row = lambda i: (i, 0)
    fix = lambda i: (0, 0)
    return pl.pallas_call(
        body,
        name="ln_bwd",
        grid=(T // tm,),
        in_specs=[pl.BlockSpec((tm, d), row), pl.BlockSpec((tm, d), row), pl.BlockSpec((tm, 1), row), pl.BlockSpec((1, d), fix)]
        + [ANY_SPEC] * len(deps),
        out_specs=[pl.BlockSpec((tm, d), row), pl.BlockSpec((tm, d), row), pl.BlockSpec((8, d), fix), pl.BlockSpec((8, d), fix)],
        out_shape=[jax.ShapeDtypeStruct((T, d), F32), jax.ShapeDtypeStruct((T, d), BF16), jax.ShapeDtypeStruct((8, d), F32),
                   jax.ShapeDtypeStruct((8, d), F32)],
        compiler_params=_cparams(("arbitrary",)),
    )(dy, xhat, rstd, g, *deps)


def _loss_head(y, target, xhat, rstd, g, *, tm=512):
    T, d = y.shape
    tm = _pick(T, (tm, 256, 128, 8))
    n = T // tm

    def body(y_ref, t_ref, xh_ref, rs_ref, g_ref, loss_ref, dz_ref, dzb_ref, dg_ref, db_ref, acc_ref):
        i = pl.program_id(0)

        @pl.when(i == 0)
        def _():
            acc_ref[...] = jnp.zeros_like(acc_ref)
            dg_ref[...] = jnp.zeros_like(dg_ref)
            db_ref[...] = jnp.zeros_like(db_ref)

        e = y_ref[...] - t_ref[...]
        acc_ref[...] += jnp.sum(e * e, axis=0, keepdims=True)
        dy_, xh = e * (1.0 / d), xh_ref[...]
        dg_ref[0:1, :] += jnp.sum(dy_ * xh, axis=0, keepdims=True)
        db_ref[0:1, :] += jnp.sum(dy_, axis=0, keepdims=True)
        dxh = dy_ * g_ref[...]
        dz = rs_ref[...] * (dxh - jnp.mean(dxh, axis=-1, keepdims=True) - xh * jnp.mean(dxh * xh, axis=-1, keepdims=True))
        dz_ref[...] = dz
        dzb_ref[...] = dz.astype(BF16)

        @pl.when(i == n - 1)
        def _():
            loss_ref[...] = (0.5 / d) * jnp.sum(acc_ref[...], axis=1, keepdims=True)

    row = lambda i: (i, 0)
    fix = lambda i: (0, 0)
    return pl.pallas_call(
        body,
        name="loss_head",
        grid=(n,),
        in_specs=[pl.BlockSpec((tm, d), row), pl.BlockSpec((tm, d), row), pl.BlockSpec((tm, d), row), pl.BlockSpec((tm, 1), row),
                  pl.BlockSpec((1, d), fix)],
        out_specs=[pl.BlockSpec((1, 1), fix), pl.BlockSpec((tm, d), row), pl.BlockSpec((tm, d), row), pl.BlockSpec((8, d), fix),
                   pl.BlockSpec((8, d), fix)],
        out_shape=[jax.ShapeDtypeStruct((1, 1), F32), jax.ShapeDtypeStruct((T, d), F32), jax.ShapeDtypeStruct((T, d), BF16),
                   jax.ShapeDtypeStruct((8, d), F32), jax.ShapeDtypeStruct((8, d), F32)],
        scratch_shapes=[pltpu.VMEM((1, d), F32)],
        compiler_params=_cparams(("arbitrary",)),
    )(y, target, xhat, rstd, g)


def _lower_bounds_fwd(lower_bounds):
    depth, n = lower_bounds.shape

    def body(x_ref, soft_ref, lb_ref):
        x = x_ref[...]
        e = jnp.exp(x - jnp.max(x, axis=0, keepdims=True))
        soft_ref[...] = e / jnp.sum(e, axis=0, keepdims=True)
        run = None
        for l in range(depth):
            run = soft_ref[l:l + 1, :] if run is None else run + soft_ref[l:l + 1, :]
            lb_ref[l:l + 1, :] = run - soft_ref[0:1, :]

    return pl.pallas_call(body, name="lower_bounds_fwd",
                          out_shape=[jax.ShapeDtypeStruct((depth, n), F32), jax.ShapeDtypeStruct((depth, n), F32)])(lower_bounds)


def _lower_bounds_bwd(soft, dlb):
    depth, n = soft.shape

    def body(soft_ref, dlb_ref, out_ref, dsoft_ref):
        total = jnp.sum(dlb_ref[...], axis=0, keepdims=True)
        run = None
        for l in reversed(range(depth)):
            run = dlb_ref[l:l + 1, :] if run is None else run + dlb_ref[l:l + 1, :]
            dsoft_ref[l:l + 1, :] = run - total if l == 0 else run
        s, ds = soft_ref[...], dsoft_ref[...]
        out_ref[...] = s * (ds - jnp.sum(s * ds, axis=0, keepdims=True))

    return pl.pallas_call(body, name="lower_bounds_bwd", out_shape=jax.ShapeDtypeStruct((depth, n), F32),
                          scratch_shapes=[pltpu.VMEM((depth, n), F32)])(soft, dlb)


def _layer_fwd(x0, x0b, mem2, lb, w_in, mix_fn, late_fn, *, bl, seq, alpha, deps=()):
    p = _matmul("proj_in", x0b, w_in, mode="nn", out_dtype=BF16, deps=deps, tm=1024, tn=1792)
    wts = dict(mix_fn(p), w_in=w_in)
    mk = _matmul("mem_k", mem2, wts["w_mem_k"], mode="nn", out_dtype=BF16)
    mv = _matmul("mem_v", mem2, wts["w_mem_v"], mode="nn", out_dtype=BF16)
    y, st, opre = _mixer_fwd(p, mk, mv, lb, wts["conv_w"], wts["hg_norm_w"], bl=bl, seq=seq)
    wts.update(late_fn(y))
    merged, xhat1, rstd1, x1b = _merge_fwd(y, p, x0, wts["w_branch"], wts["w_o"], wts["b_gate"], wts["ln1_g"], wts["ln1_b"],
                                           alpha=alpha)
    a, xhat2, rstd2, x2, x2b = _mlp_fwd(xhat1, x1b, wts["ln1_g"], wts["ln1_b"], wts["w_up"], wts["w_down"], wts["ln2_g"],
                                        wts["ln2_b"], alpha=alpha)
    saved = dict(x0b=x0b, p=p, mk=mk, mv=mv, y=y, st=st, opre=opre, merged=merged, xhat1=xhat1, rstd1=rstd1, x1b=x1b, a=a,
                 xhat2=xhat2, rstd2=rstd2)
    return x2, x2b, saved, wts


def _relu2_bf16(a):
    return jnp.square(jnp.maximum(a.astype(F32), 0.0)).astype(BF16)


def _mlp_bwd(dz2, dz2b, sv, wts, *, alpha, deps=()):
    g = {}
    da = _matmul("mlp_da", dz2b, wts["w_down"], mode="nt", out_dtype=BF16, tm=512, tn=wts["w_down"].shape[0], deps=deps,
                 epi_fn=lambda acc, a: (acc * (2.0 * jnp.maximum(a.astype(F32), 0.0)),), epi_extra=(sv["a"],))
    g["w_down"] = _matmul_tn("grad_w_down", sv["a"], dz2b, a_fn=_relu2_bf16, out_dtype=BF16, tt=2048)
    g["w_up"] = _matmul_tn("grad_w_up", sv["x1b"], da, out_dtype=BF16, tt=2048)
    dx1 = _matmul("mlp_dx", da, wts["w_up"], mode="nt", epi_fn=lambda acc, dz: (acc + alpha * dz,), epi_extra=(dz2,),
                  tm=512, tk=4096)
    dz1, dz1b, dg1, db1 = _ln_bwd(dx1, sv["xhat1"], sv["rstd1"], wts["ln1_g"])
    g["ln1_g"], g["ln1_b"] = dg1[0:1], db1[0:1]
    return dz1, dz1b, g


def _mix_bwd(dz1, dz1b, sv, mem2, lb, wts, *, bl, seq, alpha, send, below=None, deps=()):
    d = dz1.shape[1]
    g = {}
    g["w_o"] = _matmul_tn("grad_w_o", sv["merged"], dz1b, out_dtype=BF16, tt=2048, deps=deps)
    dr, dp, dy, dbg = _merge_bwd(dz1b, sv["p"], sv["y"], wts["w_branch"], wts["w_o"], wts["b_gate"])
    g["b_gate"] = dbg[0:1]
    g["w_branch"] = jnp.concatenate(
        [_matmul_tn("grad_w_branch", sv["y"], dr, a_cols=(i * W, W), b_cols=(i * d, d), out_dtype=BF16, tt=4096)
         for i in range(N_BRANCH)],
        axis=0)
    token = send(("w_o", "w_branch"), g)
    dp, dmk, dmv, dcw, dnw, dlb = _mixer_bwd(sv["p"], dy, dp, sv["st"], sv["opre"], sv["mk"], sv["mv"], lb,
                                              wts["conv_w"], wts["hg_norm_w"], bl=bl, seq=seq, deps=(token,))
    g["conv_w"], g["hg_norm_w"], g["lb"] = dcw[0:CONV_K], dnw[0:1], dlb[0:1]
    g["w_mem_k"] = _matmul_tn("grad_w_mem_k", mem2, dmk, out_dtype=BF16)
    g["w_mem_v"] = _matmul_tn("grad_w_mem_v", mem2, dmv, out_dtype=BF16)
    g["w_in"] = _matmul_tn("grad_w_in", sv["x0b"], dp, out_dtype=BF16, tt=2048)
    token = send(("w_in", "w_mem_k", "w_mem_v", "conv_w"), g)
    dx0 = _matmul("proj_in_dx", dp, wts["w_in"], mode="nt", epi_fn=lambda acc, dz: (acc + alpha * dz,), epi_extra=(dz1,),
                  tm=512, tk=dp.shape[1], deps=(token,))
    return (dx0 if below is None else _ln_bwd(dx0, *below)), g


N_CHIPS = 4
MESH_IDS = pl.DeviceIdType.MESH


def _axis_slice(ref, axis, start, size):
    idx = [slice(None)] * len(ref.shape)
    idx[axis] = pl.ds(start, size)
    return ref.at[tuple(idx)]


def _chip_exchange(name, items):
    n = len(items)
    out_shapes, meta = [], []
    for arr, kind, axis in items:
        shp = list(arr.shape)
        if kind == "gather":
            per = shp[axis]
            shp[axis] = per * N_CHIPS
            out_shapes.append(jax.ShapeDtypeStruct(tuple(shp), arr.dtype))
        elif kind == "scatter":
            per = shp[axis] // N_CHIPS
            shp[axis] = per
            out_shapes.append(jax.ShapeDtypeStruct((N_CHIPS, *shp), arr.dtype))
        else:
            per = None
            out_shapes.append(jax.ShapeDtypeStruct((N_CHIPS, *shp), arr.dtype))
        meta.append((kind, axis, per))

    def body(*refs):
        ins, outs = refs[:n], refs[n:2 * n]
        send_sems, recv_sems, local_sems = refs[2 * n:]
        x, y, c = lax.axis_index("x"), lax.axis_index("y"), lax.axis_index("c")
        me = 2 * x + y
        peers = [(1 - x, y), (x, 1 - y), (1 - x, 1 - y)]

        def src_for(t, chip):
            kind, axis, per = meta[t]
            return _axis_slice(ins[t], axis, chip * per, per) if kind == "scatter" else ins[t]

        def dst_from(t, chip):
            kind, axis, per = meta[t]
            return _axis_slice(outs[t], axis, chip * per, per) if kind == "gather" else outs[t].at[chip]

        def remote(t, k):
            px, py = peers[k]
            return pltpu.make_async_remote_copy(
                src_ref=src_for(t, 2 * px + py), dst_ref=dst_from(t, me), send_sem=send_sems.at[t * 3 + k],
                recv_sem=recv_sems.at[t * 3 + k], device_id=(px, py, c), device_id_type=MESH_IDS)

        def arrival(t, k):
            px, py = peers[k]
            return pltpu.make_async_remote_copy(
                src_ref=src_for(t, me), dst_ref=dst_from(t, 2 * px + py), send_sem=send_sems.at[t * 3 + k],
                recv_sem=recv_sems.at[t * 3 + k], device_id=(px, py, c), device_id_type=MESH_IDS)

        sends = [remote(t, k) for t in range(n) for k in range(3)]
        for cp in sends:
            cp.start()
        own = [pltpu.make_async_copy(src_for(t, me), dst_from(t, me), local_sems.at[t]) for t in range(n)]
        for cp in own:
            cp.start()
        for t in range(n):
            for k in range(3):
                arrival(t, k).wait_recv()
        for cp in sends:
            cp.wait_send()
        for cp in own:
            cp.wait()

    any_spec = pl.BlockSpec(memory_space=pl.ANY)
    return pl.pallas_call(
        body,
        name=name,
        in_specs=[any_spec] * n,
        out_specs=[any_spec] * n,
        out_shape=out_shapes,
        scratch_shapes=[pltpu.SemaphoreType.DMA((3 * n,)), pltpu.SemaphoreType.DMA((3 * n,)), pltpu.SemaphoreType.DMA((n,))],
        compiler_params=pltpu.CompilerParams(has_side_effects=True),
    )(*[a for a, _, _ in items])


HBM_SPEC = pl.BlockSpec(memory_space=pltpu.HBM)
SEM_SPEC = pl.BlockSpec(memory_space=pltpu.SEMAPHORE)
N_PEERS = N_CHIPS - 1


def _my_chip():
    return (2 * lax.axis_index("x") + lax.axis_index("y")).astype(jnp.int32).reshape(1)


def _own_block_spec(r, c, axis, tr):
    if axis == 1:
        return pl.BlockSpec((tr, c), lambda i, me: (i, me[0]))
    return pl.BlockSpec((tr, c), lambda i, me: (me[0] * (r // tr) + i, 0))


def _place_shard(name, shard, axis, me):
    r, c = shard.shape
    tr = _row_block(r, c, shard.dtype.itemsize)
    shp = (r, c * N_CHIPS) if axis == 1 else (r * N_CHIPS, c)

    def body(me_ref, s_ref, buf_ref, o_ref):
        del me_ref, buf_ref
        o_ref[...] = s_ref[...]

    buf = pltpu.with_memory_space_constraint(lax.empty(shp, shard.dtype), pltpu.HBM)
    return pl.pallas_call(
        body, name=name,
        grid_spec=pltpu.PrefetchScalarGridSpec(
            num_scalar_prefetch=1, grid=(r // tr,),
            in_specs=[pl.BlockSpec((tr, c), lambda i, me: (i, 0)), ANY_SPEC], out_specs=_own_block_spec(r, c, axis, tr)),
        out_shape=jax.ShapeDtypeStruct(shp, shard.dtype),
        input_output_aliases={2: 0},
        compiler_params=_cparams(("parallel",)),
    )(me, shard, buf)


class _Split:
    def __init__(self, name, items):
        self.name, self.n = name, len(items)
        self.srcs = [a for a, _, _ in items]
        self.meta, self.land_shapes = [], []
        for arr, kind, axis in items:
            shp = list(arr.shape)
            if kind == "gather":
                per = shp[axis]
                shp[axis] = per * N_CHIPS
                self.land_shapes.append(jax.ShapeDtypeStruct(tuple(shp), arr.dtype))
            else:
                per = shp[axis] // N_CHIPS
                shp[axis] = per
                self.land_shapes.append(jax.ShapeDtypeStruct((N_PEERS, *shp), arr.dtype))
            self.meta.append((kind, axis, per))

    def _src(self, ins, t, chip):
        kind, axis, per = self.meta[t]
        return _axis_slice(ins[t], axis, chip * per, per) if kind == "scatter" else ins[t]

    def _dst(self, lands, t, chip, slot):
        kind, axis, per = self.meta[t]
        return _axis_slice(lands[t], axis, chip * per, per) if kind == "gather" else lands[t].at[slot]

    def landing_zones(self, me):
        return [_place_shard(self.name + "_own", src, axis, me) if kind == "gather" else lax.empty(ls.shape, ls.dtype)
                for src, ls, (kind, axis, _) in zip(self.srcs, self.land_shapes, self.meta)]

    def _copies(self, ins, lands, send_sems, recv_sems, arrivals):
        x, y, c = lax.axis_index("x"), lax.axis_index("y"), lax.axis_index("c")
        me = 2 * x + y
        peers = [(1 - x, y), (x, 1 - y), (1 - x, 1 - y)]
        res = []
        for t in range(self.n):
            for k, (px, py) in enumerate(peers):
                theirs = 2 * px + py
                sems = dict(send_sem=send_sems.at[t * N_PEERS + k], recv_sem=recv_sems.at[t * N_PEERS + k],
                            device_id=(px, py, c), device_id_type=MESH_IDS)
                if arrivals:
                    res.append(pltpu.make_async_remote_copy(src_ref=self._src(ins, t, me), dst_ref=self._dst(lands, t, theirs, k), **sems))
                else:
                    res.append(pltpu.make_async_remote_copy(src_ref=self._src(ins, t, theirs), dst_ref=self._dst(lands, t, me, k), **sems))
        return res

    def start(self, lands, deps=()):
        n, nd = self.n, len(deps)

        def body(*refs):
            ins, lnd = refs[:n], refs[n:2 * n]
            send_sems, recv_sems = refs[2 * n + nd], refs[2 * n + nd + 1]
            token = refs[-1]
            for cp in self._copies(ins, lnd, send_sems, recv_sems, arrivals=False):
                cp.start()
            token[...] = jnp.zeros_like(token)

        hbm = lambda a: pltpu.HBM(a.shape, a.dtype)
        res = pl.pallas_call(
            body, name=self.name + "_start",
            in_specs=[HBM_SPEC] * (2 * n) + [ANY_SPEC] * nd,
            out_specs=[SEM_SPEC, SEM_SPEC] + [HBM_SPEC] * (2 * n) + [pl.BlockSpec(memory_space=pltpu.VMEM)],
            out_shape=[pltpu.SemaphoreType.DMA((N_PEERS * n,)), pltpu.SemaphoreType.DMA((N_PEERS * n,))]
            + [hbm(a) for a in self.srcs] + [hbm(a) for a in self.land_shapes] + [jax.ShapeDtypeStruct((8, 128), F32)],
            input_output_aliases={i: 2 + i for i in range(2 * n)},
            compiler_params=pltpu.CompilerParams(has_side_effects=pltpu.SideEffectType.DATAFLOW_SIDE_EFFECTING),
        )(*[pltpu.with_memory_space_constraint(a, pltpu.HBM) for a in self.srcs],
          *[pltpu.with_memory_space_constraint(a, pltpu.HBM) for a in lands], *deps)
        return res[:-1], res[-1]

    def wait(self, state, after):
        n = self.n
        after = tuple(after) if isinstance(after, (tuple, list)) else (after,)
        send_sems, recv_sems = state[0], state[1]
        srcs, lands = state[2:2 + n], state[2 + n:2 + 2 * n]

        def body(*refs):
            ins, lnd = refs[:n], refs[n:2 * n]
            s_sems, r_sems = refs[2 * n], refs[2 * n + 1]
            for cp in self._copies(ins, lnd, s_sems, r_sems, arrivals=True):
                cp.wait_recv()
            for cp in self._copies(ins, lnd, s_sems, r_sems, arrivals=False):
                cp.wait_send()

        hbm = lambda a: pltpu.HBM(a.shape, a.dtype)
        res = pl.pallas_call(
            body, name=self.name + "_wait",
            in_specs=[HBM_SPEC] * (2 * n) + [SEM_SPEC, SEM_SPEC] + [ANY_SPEC] * len(after),
            out_specs=[HBM_SPEC] * (2 * n),
            out_shape=[hbm(a) for a in self.srcs] + [hbm(a) for a in self.land_shapes],
            input_output_aliases={i: i for i in range(2 * n)},
            compiler_params=pltpu.CompilerParams(has_side_effects=pltpu.SideEffectType.DATAFLOW_SIDE_EFFECTING),
        )(*srcs, *lands, send_sems, recv_sems, *after)
        return res[:n], res[n:]


class _SiblingSplit:
    def __init__(self, name, arrays):
        self.name, self.n, self.arrays = name, len(arrays), list(arrays)

    def _copies(self, ins, lands, send_sems, recv_sems):
        sibling = (lax.axis_index("x"), lax.axis_index("y"), 1 - lax.axis_index("c"))
        return [pltpu.make_async_remote_copy(src_ref=ins[t], dst_ref=lands[t], send_sem=send_sems.at[t], recv_sem=recv_sems.at[t],
                                             device_id=sibling, device_id_type=MESH_IDS) for t in range(self.n)]

    def start(self, deps=()):
        n, nd = self.n, len(deps)

        def body(*refs):
            for cp in self._copies(refs[:n], refs[n:2 * n], refs[2 * n + nd], refs[2 * n + nd + 1]):
                cp.start()
            refs[-1][...] = jnp.zeros_like(refs[-1])

        hbm = [pltpu.HBM(a.shape, a.dtype) for a in self.arrays]
        res = pl.pallas_call(
            body, name=self.name + "_start",
            in_specs=[HBM_SPEC] * (2 * n) + [ANY_SPEC] * nd,
            out_specs=[SEM_SPEC, SEM_SPEC] + [HBM_SPEC] * (2 * n) + [pl.BlockSpec(memory_space=pltpu.VMEM)],
            out_shape=[pltpu.SemaphoreType.DMA((n,)), pltpu.SemaphoreType.DMA((n,))] + hbm + hbm + [jax.ShapeDtypeStruct((8, 128), F32)],
            input_output_aliases={i: 2 + i for i in range(2 * n)},
            compiler_params=pltpu.CompilerParams(has_side_effects=pltpu.SideEffectType.DATAFLOW_SIDE_EFFECTING),
        )(*[pltpu.with_memory_space_constraint(a, pltpu.HBM) for a in self.arrays],
          *[pltpu.with_memory_space_constraint(lax.empty(a.shape, a.dtype), pltpu.HBM) for a in self.arrays], *deps)
        return res[:-1], res[-1]

    def wait(self, state, after):
        n = self.n
        after = tuple(after) if isinstance(after, (tuple, list)) else (after,)

        def body(*refs):
            for cp in self._copies(refs[:n], refs[n:2 * n], refs[2 * n], refs[2 * n + 1]):
                cp.wait()

        hbm = [pltpu.HBM(a.shape, a.dtype) for a in self.arrays]
        res = pl.pallas_call(
            body, name=self.name + "_wait",
            in_specs=[HBM_SPEC] * (2 * n) + [SEM_SPEC, SEM_SPEC] + [ANY_SPEC] * len(after),
            out_specs=[HBM_SPEC] * (2 * n),
            out_shape=hbm + hbm,
            input_output_aliases={i: i for i in range(2 * n)},
            compiler_params=pltpu.CompilerParams(has_side_effects=pltpu.SideEffectType.DATAFLOW_SIDE_EFFECTING),
        )(*state[2:2 + 2 * n], state[0], state[1], *after)
        return res[:n], res[n:]


def _sibling_swap(name, arrays):
    n = len(arrays)

    def body(*refs):
        ins, outs = refs[:n], refs[n:2 * n]
        send_sems, recv_sems = refs[2 * n:]
        sibling = (lax.axis_index("x"), lax.axis_index("y"), 1 - lax.axis_index("c"))
        copies = [pltpu.make_async_remote_copy(src_ref=ins[t], dst_ref=outs[t], send_sem=send_sems.at[t], recv_sem=recv_sems.at[t],
                                               device_id=sibling, device_id_type=MESH_IDS) for t in range(n)]
        for cp in copies:
            cp.start()
        for cp in copies:
            cp.wait()

    any_spec = pl.BlockSpec(memory_space=pl.ANY)
    return pl.pallas_call(
        body,
        name=name,
        in_specs=[any_spec] * n,
        out_specs=[any_spec] * n,
        out_shape=[jax.ShapeDtypeStruct(a.shape, a.dtype) for a in arrays],
        scratch_shapes=[pltpu.SemaphoreType.DMA((n,)), pltpu.SemaphoreType.DMA((n,))],
        compiler_params=pltpu.CompilerParams(has_side_effects=True),
    )(*arrays)


def _row_block(r, c, itemsize=4, target=1 << 20):
    if r % 8 != 0:
        return r
    best = 8
    for tr in range(8, r + 1, 8):
        if r % tr == 0 and tr * c * itemsize <= target:
            best = tr
    return best


def _sum_chips_into(parts, stacked, layer):
    _, r, c = parts.shape
    tr = _row_block(r, c)

    def body(p_ref, s_ref, o_ref):
        del s_ref
        o_ref[...] = ((p_ref[0] + p_ref[1]) + p_ref[2]) + p_ref[3]

    return pl.pallas_call(
        body,
        name="sum_chips",
        grid=(r // tr,),
        in_specs=[pl.BlockSpec((N_CHIPS, tr, c), lambda i: (0, i, 0)), pl.BlockSpec(memory_space=pl.ANY)],
        out_specs=pl.BlockSpec((None, tr, c), lambda i: (layer, i, 0)),
        out_shape=jax.ShapeDtypeStruct(stacked.shape, stacked.dtype),
        input_output_aliases={1: 0},
        compiler_params=_cparams(("parallel",)),
    )(parts, stacked)


def _sum_own_and_peers(me, g, axis, landed):
    _, r, c = landed.shape
    tr = _row_block(r, c)

    def body(me_ref, g_ref, p_ref, o_ref):
        del me_ref
        o_ref[...] = ((g_ref[...].astype(F32) + p_ref[0].astype(F32)) + p_ref[1].astype(F32)) + p_ref[2].astype(F32)

    return pl.pallas_call(
        body, name="sum_chips_own",
        grid_spec=pltpu.PrefetchScalarGridSpec(
            num_scalar_prefetch=1, grid=(r // tr,),
            in_specs=[_own_block_spec(r, c, axis, tr), pl.BlockSpec((N_PEERS, tr, c), lambda i, me: (0, i, 0))],
            out_specs=pl.BlockSpec((tr, c), lambda i, me: (i, 0))),
        out_shape=jax.ShapeDtypeStruct((r, c), F32),
        compiler_params=_cparams(("parallel",)),
    )(me, g, landed)


ADAMW_BLOCK_BYTES = 3 << 19


def _adamw_math(w, m, v, g):
    m_new = ADAM_B1 * m + (1.0 - ADAM_B1) * g
    v_new = ADAM_B2 * v + (1.0 - ADAM_B2) * jnp.square(g)
    m_hat = m_new / (1.0 - ADAM_B1 ** ADAM_STEP)
    v_hat = v_new / (1.0 - ADAM_B2 ** ADAM_STEP)
    return -ADAM_LR * (m_hat / (jnp.sqrt(v_hat) + ADAM_EPS) + ADAM_WD * w), m_new, v_new


def _adamw(w, m, v, g_a, g_b):
    L, r, c = w.shape
    tr = _row_block(r, c, target=ADAMW_BLOCK_BYTES)

    def body(w_ref, m_ref, v_ref, ga_ref, gb_ref, g_ref, d_ref, nm_ref, nv_ref):
        g = ga_ref[...] + gb_ref[...]
        g_ref[...] = g
        d_ref[...], nm_ref[...], nv_ref[...] = _adamw_math(w_ref[...], m_ref[...], v_ref[...], g)

    spec = pl.BlockSpec((None, tr, c), lambda l, i: (l, i, 0))
    return pl.pallas_call(
        body,
        name="adamw",
        grid=(L, r // tr),
        in_specs=[spec] * 5,
        out_specs=[spec] * 4,
        out_shape=[jax.ShapeDtypeStruct(w.shape, F32)] * 4,
        compiler_params=_cparams(("parallel", "parallel")),
    )(w, m, v, g_a, g_b)


def _adamw_layer(w, m, v, g_a, g_b, layer, outs):
    L, r, c = w.shape
    tr = _row_block(r, c, target=ADAMW_BLOCK_BYTES)
    n_prev = 0 if outs is None else 4

    def body(w_ref, m_ref, v_ref, ga_ref, gb_ref, *rest):
        g_ref, d_ref, nm_ref, nv_ref = rest[n_prev:]
        g = ga_ref[...] + gb_ref[...]
        g_ref[...] = g
        d_ref[...], nm_ref[...], nv_ref[...] = _adamw_math(w_ref[...], m_ref[...], v_ref[...], g)

    at_layer = pl.BlockSpec((None, tr, c), lambda i: (layer, i, 0))
    flat = pl.BlockSpec((tr, c), lambda i: (i, 0))
    return pl.pallas_call(
        body,
        name="adamw_layer",
        grid=(r // tr,),
        in_specs=[at_layer] * 3 + [flat] * 2 + [ANY_SPEC] * n_prev,
        out_specs=[at_layer] * 4,
        out_shape=[jax.ShapeDtypeStruct(w.shape, F32)] * 4,
        input_output_aliases={5 + k: k for k in range(n_prev)},
        compiler_params=_cparams(("parallel",)),
    )(w, m, v, g_a, g_b, *(outs or ()))


SHARDED = (("w_in", 1), ("conv_w", 1), ("w_mem_k", 0), ("w_mem_v", 0), ("w_branch", 1), ("w_o", 0), ("w_up", 1), ("w_down", 0))
SMALL = ("lower_bounds", "hg_norm_w", "b_gate", "ln1_g", "ln1_b", "ln2_g", "ln2_b")
WEIGHT_ORDER = ("lower_bounds", "w_in", "conv_w", "hg_norm_w", "w_mem_k", "w_mem_v", "w_branch", "b_gate", "w_o", "ln1_g", "ln1_b",
                "w_up", "w_down", "ln2_g", "ln2_b")


def kernel(x, mem, lower_bounds, w_in, conv_w, hg_norm_w, w_mem_k, w_mem_v, w_branch, b_gate, w_o, ln1_g, ln1_b, w_up, w_down, ln2_g, ln2_b, loss_target, m_lower_bounds, m_w_in, m_conv_w, m_hg_norm_w, m_w_mem_k, m_w_mem_v, m_w_branch, m_b_gate, m_w_o, m_ln1_g, m_ln1_b, m_w_up, m_w_down, m_ln2_g, m_ln2_b, v_lower_bounds, v_w_in, v_conv_w, v_hg_norm_w, v_w_mem_k, v_w_mem_v, v_w_branch, v_b_gate, v_w_o, v_ln1_g, v_ln1_b, v_w_up, v_w_down, v_ln2_g, v_ln2_b):
    bl, seq, d = x.shape
    depth = w_in.shape[0]
    weights = dict(lower_bounds=lower_bounds, w_in=w_in, conv_w=conv_w, hg_norm_w=hg_norm_w, w_mem_k=w_mem_k, w_mem_v=w_mem_v,
                   w_branch=w_branch, b_gate=b_gate, w_o=w_o, ln1_g=ln1_g, ln1_b=ln1_b, w_up=w_up, w_down=w_down, ln2_g=ln2_g, ln2_b=ln2_b)
    mom_m = dict(lower_bounds=m_lower_bounds, w_in=m_w_in, conv_w=m_conv_w, hg_norm_w=m_hg_norm_w, w_mem_k=m_w_mem_k, w_mem_v=m_w_mem_v,
                 w_branch=m_w_branch, b_gate=m_b_gate, w_o=m_w_o, ln1_g=m_ln1_g, ln1_b=m_ln1_b, w_up=m_w_up, w_down=m_w_down,
                 ln2_g=m_ln2_g, ln2_b=m_ln2_b)
    mom_v = dict(lower_bounds=v_lower_bounds, w_in=v_w_in, conv_w=v_conv_w, hg_norm_w=v_hg_norm_w, w_mem_k=v_w_mem_k, w_mem_v=v_w_mem_v,
                 w_branch=v_w_branch, b_gate=v_b_gate, w_o=v_w_o, ln1_g=v_ln1_g, ln1_b=v_ln1_b, w_up=v_w_up, w_down=v_w_down,
                 ln2_g=v_ln2_g, ln2_b=v_ln2_b)

    def shard2d(name, l):
        w = weights[name][l]
        if name == "w_branch":
            return w.reshape(N_BRANCH * W, w.shape[-1]).astype(BF16)
        return w if name == "conv_w" else w.astype(BF16)

    me = _my_chip()

    shard_axis = dict(SHARDED)

    def prepare_exchange(name, kind, items):
        ex = _Split(name, [(arr, kind, shard_axis[nm]) for nm, arr in items])
        return ex, ex.landing_zones(me), [nm for nm, _ in items]

    def launch(prepared, deps=()):
        ex, lands, names = prepared
        state, token = ex.start(lands, deps)
        return ex, state, names, token

    def start_exchange(name, kind, items, deps=()):
        return launch(prepare_exchange(name, kind, items), deps)

    def prepare_gathers(l):
        groups = (("in", ("w_in",)), ("mix", ("conv_w", "w_mem_k", "w_mem_v")), ("rest", ("w_branch", "w_o", "w_up", "w_down")))
        return tuple(prepare_exchange(f"gather_{tag}_l{l}", "gather", [(nm, shard2d(nm, l)) for nm in names]) for tag, names in groups)

    def start_gathers(prepared, deps=()):
        started = []
        for prep in prepared:
            started.append(launch(prep, deps))
            deps = (started[-1][3],)
        return tuple(started)

    def gathered(pend, after):
        ex, state, names, _ = pend
        return dict(zip(names, ex.wait(state, after=after)[1]))

    pending = start_gathers(prepare_gathers(0))
    tokens = tuple(pend[3] for pend in pending)
    tokens, x, mem, loss_target, weights, mom_m, mom_v = lax.optimization_barrier((tokens, x, mem, loss_target, weights, mom_m, mom_v))
    pending = tuple((*pend[:3], tok) for pend, tok in zip(pending, tokens))
    lower_bounds = weights["lower_bounds"]

    x2d, mem2, t2d = x.reshape(bl * seq, d), mem.reshape(-1, d), loss_target.reshape(bl * seq, d)
    alpha = (2.0 * depth) ** 0.25
    soft, lb_all = _lower_bounds_fwd(lower_bounds)

    prepared = [None] + [prepare_gathers(l) for l in range(1, depth)]
    early = [x2d.astype(BF16), lb_all] + [z for prep in prepared[1:] for _, lands, _ in prep for z in lands]

    h, hb, saved, layer_wts = x2d, early[0], [], []
    for l in range(depth):
        first, mix, rest = pending
        w_in_l = gathered(first, early if l == 0 else h)["w_in"]

        def mix_fn(after, l=l, mix=mix):
            return dict(gathered(mix, after), hg_norm_w=weights["hg_norm_w"][l][None, :])

        def late_fn(after, l=l, rest=rest):
            wts = gathered(rest, after)
            for name in ("b_gate", "ln1_g", "ln1_b", "ln2_g", "ln2_b"):
                wts[name] = weights[name][l][None, :]
            return wts

        deps = (rest[3],)
        if l + 1 < depth:
            pending = start_gathers(prepared[l + 1], (w_in_l, rest[3]))
            deps += tuple(pend[3] for pend in pending)
        h, hb, sv, wts = _layer_fwd(h, hb, mem2, lb_all[l:l + 1], w_in_l, mix_fn, late_fn, bl=bl, seq=seq, alpha=alpha, deps=deps)
        saved.append(sv)
        layer_wts.append(wts)
    loss, dz2, dz2b, dg2, db2 = _loss_head(h, t2d, saved[-1]["xhat2"], saved[-1]["rstd2"], layer_wts[-1]["ln2_g"])

    shape3 = {name: (depth, weights[name].size // (depth * weights[name].shape[-1]), weights[name].shape[-1]) for name, _ in SHARDED}
    partial = [dict() for _ in range(depth)]
    smalls = [None] * depth
    outs = {name: None for name, _ in SHARDED}

    def finish_reduce(pend, l, after):
        ex, state, names, _ = pend
        sent, got = ex.wait(state, after=after)
        for nm, g_full, landed in zip(names, sent, got):
            partial[l][nm] = _sum_own_and_peers(me, g_full, shard_axis[nm], landed)

    names_sharded = [name for name, _ in SHARDED]

    def start_swap(l):
        swap = _SiblingSplit(f"swap_partials_l{l}", [partial[l][nm] for nm in names_sharded])
        state, token = swap.start()
        return swap, state, token

    def optimizer_step(l, pend, after):
        swap, state, _ = pend
        mine, theirs = swap.wait(state, after)
        for nm, own, other in zip(names_sharded, mine, theirs):
            outs[nm] = _adamw_layer(weights[nm].reshape(shape3[nm]), mom_m[nm].reshape(shape3[nm]), mom_v[nm].reshape(shape3[nm]),
                                    own, other, l, outs[nm])
        return tuple(outs[nm][0] for nm in names_sharded)

    pending_mix, pending_swap, deps = [], None, ()
    for l in reversed(range(depth)):
        dz1, dz1b, g_mlp = _mlp_bwd(dz2, dz2b, saved[l], layer_wts[l], alpha=alpha, deps=deps)
        g_mlp["ln2_g"], g_mlp["ln2_b"] = dg2[0:1], db2[0:1]
        pending_mlp = start_exchange(f"reduce_mlp_l{l}", "scatter", [(nm, g_mlp[nm]) for nm in ("w_up", "w_down")])
        deps = (pending_mlp[3],)
        if pending_mix:
            for pend in pending_mix:
                finish_reduce(pend, l + 1, dz1)
            pending_swap = start_swap(l + 1)
            deps += (pending_swap[2],)
        pending_mix = []

        def send(names, g, l=l, pending_mix=pending_mix):
            pend = start_exchange(f"reduce_{names[0]}_l{l}", "scatter", [(nm, g[nm]) for nm in names])
            pending_mix.append(pend)
            return pend[3]

        below = (saved[l - 1]["xhat2"], saved[l - 1]["rstd2"], layer_wts[l - 1]["ln2_g"]) if l > 0 else None
        out, g = _mix_bwd(dz1, dz1b, saved[l], mem2, lb_all[l:l + 1], layer_wts[l], bl=bl, seq=seq, alpha=alpha, send=send,
                          below=below, deps=deps)
        if l > 0:
            dz2, dz2b, dg2, db2 = out
        else:
            dh = out
        finish_reduce(pending_mlp, l, out[0] if l > 0 else out)
        deps = ()
        if pending_swap is not None:
            deps = optimizer_step(l + 1, pending_swap, out[0] if l > 0 else out)
            pending_swap = None
        g.update(g_mlp, lower_bounds=g["lb"])
        smalls[l] = jnp.concatenate([g[nm] for nm in SMALL], axis=1)
    small_parts = _chip_exchange("reduce_small", [(jnp.stack(smalls), "bcast", 0)])[0]
    small_sum = _sum_chips_into(small_parts.reshape(N_CHIPS, depth, -1), jnp.zeros((1, depth, small_parts.shape[-1]), F32), 0)
    small_sum = small_sum.reshape(depth, 1, -1)
    small_theirs = _sibling_swap("swap_small", [small_sum])[0]
    for pend in pending_mix:
        finish_reduce(pend, 0, small_theirs)
    optimizer_step(0, start_swap(0), small_theirs)

    outs = {name: [r.reshape(weights[name].shape) for r in res] for name, res in outs.items()}
    off = 0
    for name in SMALL:
        n = weights[name].shape[1]
        mine, other = small_sum[:, :, off:off + n], small_theirs[:, :, off:off + n]
        off += n
        if name == "lower_bounds":
            mine = _lower_bounds_bwd(soft, mine[:, 0, :])[:, None, :]
            other = _lower_bounds_bwd(soft, other[:, 0, :])[:, None, :]
        shp = (depth, 1, n)
        res = _adamw(weights[name].reshape(shp), mom_m[name].reshape(shp), mom_v[name].reshape(shp), mine, other)
        outs[name] = [r.reshape(weights[name].shape) for r in res]
    assert off == small_sum.shape[-1]

    total_loss = lax.psum(loss[0, 0], ("x", "y", "c"))
    result = [total_loss, dh.reshape(bl, seq, d)]
    for k in range(4):
        result += [outs[name][k] for name in WEIGHT_ORDER]
    return tuple(result)
```

```python
import functools

import jax
import jax.numpy as jnp
from jax import lax
from jax.experimental import pallas as pl
from jax.experimental.pallas import tpu as pltpu

F32 = jnp.float32
BF16 = jnp.bfloat16

HG_HEADS = 4
HG_F = 128
HG_CHUNK = 32
MEM_HEADS = 4
MEM_HEAD_DIM = 128
BRANCH_WIDTH = 512
N_BRANCH = 3
CONV_K = 3
LN_EPS = 1e-5
RMS_EPS = 1e-6
ADAM_LR = 0.001
ADAM_B1 = 0.9
ADAM_B2 = 0.999
ADAM_EPS = 1e-08
ADAM_WD = 0.01
ADAM_STEP = 10

VMEM_LIMIT = 48 * 1024 * 1024


def _cparams(sem):
    return pltpu.CompilerParams(dimension_semantics=sem, vmem_limit_bytes=VMEM_LIMIT)


def _dot(a, b, dims):
    return lax.dot_general(a, b, (dims, ((), ())), preferred_element_type=F32)


NN = ((1,), (0,))
NT = ((1,), (1,))
TN = ((0,), (0,))


def _pick(n, pref):
    for t in pref:
        if n % t == 0:
            return t
    return n


ANY_SPEC = pl.BlockSpec(memory_space=pl.ANY)


def _matmul(name, a, b, *, mode, out_dtype=F32, a_fn=None, a_extra=(), epi_fn=None, epi_extra=(), n_out=1, out_kinds=None,
            tm=512, tn=1024, tk=1024, deps=()):
    M, K = a.shape
    N = b.shape[1] if mode == "nn" else b.shape[0]
    tm, tn, tk = _pick(M, (tm, 256, 128, 8)), _pick(N, (tn, 896, 512, 256, 128)), _pick(K, (tk, 512, 256, 128))
    nk = K // tk
    n_ax, n_ex = len(a_extra), len(epi_extra)
    n_in = 2 + n_ax + n_ex + len(deps)
    out_dtypes = out_dtype if isinstance(out_dtype, (tuple, list)) else (out_dtype,) * n_out
    out_kinds = out_kinds or ("tile",) * n_out

    def body(*refs):
        a_ref, b_ref = refs[0], refs[1]
        ax_refs = refs[2:2 + n_ax]
        ex_refs = refs[2 + n_ax:2 + n_ax + n_ex]
        o_refs = refs[n_in:n_in + n_out]
        at = a_ref[...]
        at = a_fn(at, *[r[...] for r in ax_refs]) if a_fn is not None else at.astype(BF16)
        part = _dot(at, b_ref[...].astype(BF16), NN if mode == "nn" else NT)

        def finish(acc):
            outs = epi_fn(acc, *[r[...] for r in ex_refs]) if epi_fn is not None else (acc,)
            for o_ref, o, kind in zip(o_refs, outs, out_kinds):
                if kind == "rowsum":
                    @pl.when(pl.program_id(1) == 0)
                    def _(o_ref=o_ref):
                        o_ref[...] = jnp.zeros_like(o_ref)

                    o_ref[0:1, :] += o
                else:
                    o_ref[...] = o.astype(o_ref.dtype)

        if nk == 1:
            finish(part)
            return
        acc_ref = refs[-1]
        k = pl.program_id(2)

        @pl.when(k == 0)
        def _():
            acc_ref[...] = part

        @pl.when(jnp.logical_and(k > 0, k < nk - 1))
        def _():
            acc_ref[...] += part

        @pl.when(k == nk - 1)
        def _():
            finish(acc_ref[...] + part)

    b_mode = dict(pipeline_mode=pl.Buffered(1)) if (nk == 1 and N == tn) else {}
    in_specs = [pl.BlockSpec((tm, tk), lambda j, i, k: (i, k)),
                pl.BlockSpec((tk, tn), lambda j, i, k: (k, j), **b_mode) if mode == "nn"
                else pl.BlockSpec((tn, tk), lambda j, i, k: (j, k), **b_mode)]
    in_specs += [pl.BlockSpec((1, tk), lambda j, i, k: (0, k)) for _ in a_extra]
    for e in epi_extra:
        if e.shape[0] == 1:
            in_specs.append(pl.BlockSpec((1, tn), lambda j, i, k: (0, j)))
        elif e.shape[1] == 1:
            in_specs.append(pl.BlockSpec((tm, 1), lambda j, i, k: (i, 0)))
        else:
            in_specs.append(pl.BlockSpec((tm, tn), lambda j, i, k: (i, j)))
    in_specs += [ANY_SPEC] * len(deps)
    out_specs, out_shapes = [], []
    for kind, dt in zip(out_kinds, out_dtypes):
        if kind == "col":
            out_specs.append(pl.BlockSpec((tm, 1), lambda j, i, k: (i, 0)))
            out_shapes.append(jax.ShapeDtypeStruct((M, 1), dt))
        elif kind == "rowsum":
            out_specs.append(pl.BlockSpec((8, tn), lambda j, i, k: (0, j)))
            out_shapes.append(jax.ShapeDtypeStruct((8, N), dt))
        else:
            out_specs.append(pl.BlockSpec((tm, tn), lambda j, i, k: (i, j)))
            out_shapes.append(jax.ShapeDtypeStruct((M, N), dt))
    out = pl.pallas_call(
        body,
        name=name,
        grid=(N // tn, M // tm, nk),
        in_specs=in_specs,
        out_specs=out_specs,
        out_shape=out_shapes,
        scratch_shapes=[pltpu.VMEM((tm, tn), F32)] if nk > 1 else [],
        compiler_params=_cparams(("arbitrary", "arbitrary", "arbitrary")),
    )(a, b, *a_extra, *epi_extra, *deps)
    return out[0] if n_out == 1 else out


def _matmul_tn(name, a, b, *, a_fn=None, a_extra=(), a_cols=None, b_cols=None, ta=1024, tb=1024, tt=1024, out_dtype=F32, deps=()):
    T = a.shape[0]
    a0, Ka = a_cols if a_cols is not None else (0, a.shape[1])
    b0, Nb = b_cols if b_cols is not None else (0, b.shape[1])
    ta, tb, tt = _pick(Ka, (ta, 512, 256, 128)), _pick(Nb, (tb, 896, 512, 256, 128)), _pick(T, (tt, 512, 256, 128))
    assert a0 % ta == 0 and b0 % tb == 0
    a0, b0 = a0 // ta, b0 // tb
    nt = T // tt
    n_ax = len(a_extra)

    def body(*refs):
        a_ref, b_ref = refs[0], refs[1]
        ax_refs = refs[2:2 + n_ax]
        o_ref = refs[2 + n_ax + len(deps)]
        acc_ref = refs[-1]
        t = pl.program_id(2)
        at = a_ref[...]
        at = a_fn(at, *[r[...] for r in ax_refs]) if a_fn is not None else at.astype(BF16)
        part = _dot(at, b_ref[...].astype(BF16), TN)

        @pl.when(t == 0)
        def _():
            acc_ref[...] = part

        @pl.when(jnp.logical_and(t > 0, t < nt - 1))
        def _():
            acc_ref[...] += part

        @pl.when(t == nt - 1)
        def _():
            o_ref[...] = (acc_ref[...] + part if nt > 1 else part).astype(o_ref.dtype)

    in_specs = [pl.BlockSpec((tt, ta), lambda i, j, t: (t, a0 + i)), pl.BlockSpec((tt, tb), lambda i, j, t: (t, b0 + j))]
    in_specs += [pl.BlockSpec((1, ta), lambda i, j, t: (0, a0 + i)) for _ in a_extra]
    in_specs += [ANY_SPEC] * len(deps)
    return pl.pallas_call(
        body,
        name=name,
        grid=(Ka // ta, Nb // tb, nt),
        in_specs=in_specs,
        out_specs=pl.BlockSpec((ta, tb), lambda i, j, t: (i, j)),
        out_shape=jax.ShapeDtypeStruct((Ka, Nb), out_dtype),
        scratch_shapes=[pltpu.VMEM((ta, tb), F32)],
        compiler_params=_cparams(("parallel", "parallel", "arbitrary")),
    )(a, b, *a_extra, *deps)


W = BRANCH_WIDTH
C_CB, C_CC, C_CH, C_HQ, C_HF, C_HI, C_HG, C_MQ, N_MIX = 0, W, 2 * W, 3 * W, 4 * W, 5 * W, 6 * W, 7 * W, 8 * W
TS_MIX = 256
PREV_ROWS = 16
KEEP_NAMES = ("sq", "qs", "k", "sig", "f", "ea", "eb", "eq", "ek")


def _sigmoid(x):
    return jax.nn.sigmoid(x)


def _chunk_pos(shape):
    return lax.broadcasted_iota(jnp.int32, shape, 0) & (HG_CHUNK - 1)


def _seg_cumsum(x, pos):
    sh = 1
    while sh < HG_CHUNK:
        x = x + jnp.where(pos >= sh, pltpu.roll(x, sh, 0), 0.0)
        sh *= 2
    return x


def _seg_rev_cumsum(x, pos):
    n = x.shape[0]
    sh = 1
    while sh < HG_CHUNK:
        x = x + jnp.where(pos < HG_CHUNK - sh, pltpu.roll(x, n - sh, 0), 0.0)
        sh *= 2
    return x


def _chunk_mask(ts):
    r = lax.broadcasted_iota(jnp.int32, (ts, ts), 0)
    c = lax.broadcasted_iota(jnp.int32, (ts, ts), 1)
    return jnp.logical_and((r // HG_CHUNK) == (c // HG_CHUNK), c <= r)


def _hgrn_gates(p_ref, lb):
    q = p_ref[:, C_HQ:C_HQ + W].astype(F32)
    fl = p_ref[:, C_HF:C_HF + W].astype(F32)
    sig = _sigmoid(fl)
    f = lb + (1.0 - lb) * sig
    logf = jnp.log(f)
    k = (1.0 - lb) * _sigmoid(-fl)
    sq = _sigmoid(q)
    qs = q * sq
    return q, sq, qs, sig, f, logf, k


def _hgrn_decays(logf, bc_sc, ts):
    pos = _chunk_pos(logf.shape)
    bc = _seg_cumsum(logf, pos)
    bc_sc[...] = bc
    nc = ts // HG_CHUNK
    bref = jnp.concatenate(
        [jnp.broadcast_to(bc_sc[n * HG_CHUNK + HG_CHUNK // 2 - 1:n * HG_CHUNK + HG_CHUNK // 2, :], (HG_CHUNK, W)) for n in range(nc)], axis=0)
    blast = jnp.concatenate(
        [jnp.broadcast_to(bc_sc[(n + 1) * HG_CHUNK - 1:(n + 1) * HG_CHUNK, :], (HG_CHUNK, W)) for n in range(nc)], axis=0)
    return pos, bc, bref, blast


def _conv_shift_down(u, carry_ref, row):
    n = carry_ref.shape[0]
    last, before = carry_ref[n - 1:n, :], carry_ref[n - 2:n - 1, :]
    u1 = jnp.where(row == 0, last, pltpu.roll(u, 1, 0))
    u2 = jnp.where(row == 0, before, jnp.where(row == 1, last, pltpu.roll(u, 2, 0)))
    return u1, u2


def _attn_probs(qh, kh):
    s = _dot(qh, kh, NT) * (MEM_HEAD_DIM ** -0.5)
    e = jnp.exp(s - jnp.max(s, axis=-1, keepdims=True))
    return e / jnp.sum(e, axis=-1, keepdims=True)


def _mixer_fwd(p, mk, mv, lb, conv_w, norm_w, *, bl, seq):
    T = p.shape[0]
    ts = TS_MIX
    ns = seq // ts
    nc = ts // HG_CHUNK
    ml = mk.shape[0] // bl

    def body(p_ref, mk_ref, mv_ref, lb_ref, cw_ref, nw_ref, y_ref, st_ref, opre_ref, state_sc, carry_sc, bc_sc):
        @pl.when(pl.program_id(1) == 0)
        def _():
            state_sc[...] = jnp.zeros_like(state_sc)
            carry_sc[...] = jnp.zeros_like(carry_sc)

        cb, cc, ch = (p_ref[:, c0:c0 + W].astype(F32) for c0 in (C_CB, C_CC, C_CH))
        u = cc * ch
        row = lax.broadcasted_iota(jnp.int32, (ts, W), 0)
        u1, u2 = _conv_shift_down(u, carry_sc, row)
        yconv = u2 * cw_ref[0:1, :] + u1 * cw_ref[1:2, :] + u * cw_ref[2:3, :]
        y_ref[:, 0:W] = (cb * yconv).astype(BF16)
        carry_sc[...] = u[ts - 8:ts, :]

        lbv = lb_ref[...]
        _, _, qs, _, _, logf, k = _hgrn_gates(p_ref, lbv)
        pos, bc, bref, blast = _hgrn_decays(logf, bc_sc, ts)
        a_all = (qs * jnp.exp(bc - bref)).astype(BF16)
        bk_all = (k * jnp.exp(bref - bc)).astype(BF16)
        qin_all = (qs * jnp.exp(bc)).astype(BF16)
        kout_all = (k * jnp.exp(blast - bc)).astype(BF16)
        v_all = p_ref[:, C_HI:C_HI + W].astype(BF16)
        mask = _chunk_mask(ts)
        heads = [slice(h * HG_F, (h + 1) * HG_F) for h in range(HG_HEADS)]
        st = [state_sc[h] for h in range(HG_HEADS)]
        o_inter = [[] for _ in range(HG_HEADS)]
        for n in range(nc):
            rows = slice(n * HG_CHUNK, (n + 1) * HG_CHUNK)
            for h, hs in enumerate(heads):
                st_ref[n, h] = st[h]
                o_inter[h].append(_dot(qin_all[rows, hs], st[h].astype(BF16), NT))
                kv = _dot(v_all[rows, hs], kout_all[rows, hs], TN)
                decay = jnp.exp(bc_sc[(n + 1) * HG_CHUNK - 1:(n + 1) * HG_CHUNK, hs])
                st[h] = st[h] * decay + kv
        for h in range(HG_HEADS):
            state_sc[h] = st[h]
        scores = [_dot(a_all[:, hs], bk_all[:, hs], NT) for hs in heads]
        scores = [jnp.where(mask, s, 0.0).astype(BF16) for s in scores]
        outs = [_dot(scores[h], v_all[:, hs], NN) + jnp.concatenate(o_inter[h], axis=0) for h, hs in enumerate(heads)]
        for h, hs in enumerate(heads):
            o = outs[h]
            opre_ref[:, hs] = o
            on = o * lax.rsqrt(jnp.mean(o * o, axis=-1, keepdims=True) + RMS_EPS) * nw_ref[...]
            g = p_ref[:, C_HG + h * HG_F:C_HG + (h + 1) * HG_F].astype(F32)
            y_ref[:, W + h * HG_F:W + (h + 1) * HG_F] = (on * (g * _sigmoid(g))).astype(BF16)

        mheads = [slice(h * MEM_HEAD_DIM, (h + 1) * MEM_HEAD_DIM) for h in range(MEM_HEADS)]
        probs = [_attn_probs(p_ref[:, C_MQ + h * MEM_HEAD_DIM:C_MQ + (h + 1) * MEM_HEAD_DIM].astype(BF16), mk_ref[:, hs])
                 for h, hs in enumerate(mheads)]
        for h, hs in enumerate(mheads):
            y_ref[:, 2 * W + h * MEM_HEAD_DIM:2 * W + (h + 1) * MEM_HEAD_DIM] = _dot(
                probs[h].astype(BF16), mv_ref[:, hs], NN).astype(BF16)

    return pl.pallas_call(
        body,
        name="mixer_fwd",
        grid=(bl, ns),
        in_specs=[
            pl.BlockSpec((ts, N_MIX), lambda b, s: (b * ns + s, 0)),
            pl.BlockSpec((ml, W), lambda b, s: (b, 0)),
            pl.BlockSpec((ml, W), lambda b, s: (b, 0)),
            pl.BlockSpec((1, W), lambda b, s: (0, 0)),
            pl.BlockSpec((CONV_K, W), lambda b, s: (0, 0)),
            pl.BlockSpec((1, HG_F), lambda b, s: (0, 0)),
        ],
        out_specs=[
            pl.BlockSpec((ts, 3 * W), lambda b, s: (b * ns + s, 0)),
            pl.BlockSpec((nc, HG_HEADS, HG_F, HG_F), lambda b, s: (b * ns + s, 0, 0, 0)),
            pl.BlockSpec((ts, W), lambda b, s: (b * ns + s, 0)),
        ],
        out_shape=[
            jax.ShapeDtypeStruct((T, 3 * W), BF16),
            jax.ShapeDtypeStruct((T // HG_CHUNK, HG_HEADS, HG_F, HG_F), F32),
            jax.ShapeDtypeStruct((T, W), F32),
        ],
        scratch_shapes=[pltpu.VMEM((HG_HEADS, HG_F, HG_F), F32), pltpu.VMEM((8, W), F32), pltpu.VMEM((ts, W), F32)],
        compiler_params=_cparams(("arbitrary", "arbitrary")),
    )(p, mk, mv, lb, conv_w, norm_w)


def _mixer_bwd(p, dy, dp_gates, st, opre, mk, mv, lb, conv_w, norm_w, *, bl, seq, deps=()):
    T, nin = p.shape
    ts = TS_MIX
    ns = seq // ts
    nc = ts // HG_CHUNK
    ml = mk.shape[0] // bl
    mid, last = HG_CHUNK // 2 - 1, HG_CHUNK - 1

    def body(p_ref, pprev_ref, dy_ref, dpin_ref, st_ref, opre_ref, mk_ref, mv_ref, lb_ref, cw_ref, nw_ref, *rest):
        (dp_ref, dmk_ref, dmv_ref, dcw_ref, dnw_ref, dlb_ref, dstate_sc, carry_sc, uprev_sc, ab_sc, bkb_sc, qinb_sc, koutb_sc,
         dob_sc, dv_sc, da_sc, dbk_sc, dqin_sc, dkout_sc, dec_sc, ddec_sc, *keep_scs) = rest[len(deps):]
        del dpin_ref
        b, s = pl.program_id(0), pl.program_id(1)

        @pl.when(s == 0)
        def _():
            dstate_sc[...] = jnp.zeros_like(dstate_sc)
            carry_sc[...] = jnp.zeros_like(carry_sc)
            dmk_ref[...] = jnp.zeros_like(dmk_ref)
            dmv_ref[...] = jnp.zeros_like(dmv_ref)

        @pl.when(jnp.logical_and(b == 0, s == 0))
        def _():
            dcw_ref[...] = jnp.zeros_like(dcw_ref)
            dnw_ref[...] = jnp.zeros_like(dnw_ref)
            dlb_ref[...] = jnp.zeros_like(dlb_ref)

        cb, cc, ch = (p_ref[:, c0:c0 + W].astype(F32) for c0 in (C_CB, C_CC, C_CH))
        u = cc * ch
        row = lax.broadcasted_iota(jnp.int32, (ts, W), 0)
        uprev = pprev_ref[:, C_CC:C_CC + W].astype(F32) * pprev_ref[:, C_CH:C_CH + W].astype(F32)
        uprev_sc[...] = jnp.where(s == ns - 1, 0.0, uprev)
        u1, u2 = _conv_shift_down(u, uprev_sc, row)
        w0, w1, w2 = cw_ref[0:1, :], cw_ref[1:2, :], cw_ref[2:3, :]
        dya = dy_ref[:, 0:W].astype(F32)
        dp_ref[:, C_CB:C_CB + W] = (dya * (u2 * w0 + u1 * w1 + u * w2)).astype(BF16)
        dv = cb * dya
        dv1 = jnp.where(row == ts - 1, carry_sc[0:1, :], pltpu.roll(dv, ts - 1, 0))
        dv2 = jnp.where(row == ts - 1, carry_sc[1:2, :], jnp.where(row == ts - 2, carry_sc[0:1, :], pltpu.roll(dv, ts - 2, 0)))
        du = dv * w2 + dv1 * w1 + dv2 * w0
        dp_ref[:, C_CC:C_CC + W] = (du * ch).astype(BF16)
        dp_ref[:, C_CH:C_CH + W] = (du * cc).astype(BF16)
        dcw_ref[0:1, :] += jnp.sum(dv * u2, axis=0, keepdims=True)
        dcw_ref[1:2, :] += jnp.sum(dv * u1, axis=0, keepdims=True)
        dcw_ref[2:3, :] += jnp.sum(dv * u, axis=0, keepdims=True)
        carry_sc[...] = dv[0:8, :]

        mask = _chunk_mask(ts)
        pos_c = _chunk_pos((HG_CHUNK, HG_F))
        nw = nw_ref[...]

        def block(n, h):
            rows = slice(n * HG_CHUNK, (n + 1) * HG_CHUNK)
            return rows, slice(h * HG_F, (h + 1) * HG_F)

        keep = dict(zip(KEEP_NAMES, keep_scs))

        def gates(rows, h):
            lbh = lb_ref[:, h * HG_F:(h + 1) * HG_F]
            q = p_ref[rows, C_HQ + h * HG_F:C_HQ + (h + 1) * HG_F].astype(F32)
            fl = p_ref[rows, C_HF + h * HG_F:C_HF + (h + 1) * HG_F].astype(F32)
            sig = _sigmoid(fl)
            f = lbh + (1.0 - lbh) * sig
            k = (1.0 - lbh) * _sigmoid(-fl)
            sq = _sigmoid(q)
            qs = q * sq
            bc = _seg_cumsum(jnp.log(f), pos_c)
            bref = jnp.sum(jnp.where(pos_c == mid, bc, 0.0), axis=0, keepdims=True)
            blast = jnp.sum(jnp.where(pos_c == last, bc, 0.0), axis=0, keepdims=True)
            ea, eb, eq, ek = jnp.exp(bc - bref), jnp.exp(bref - bc), jnp.exp(bc), jnp.exp(blast - bc)
            return dict(sq=sq, qs=qs, k=k, sig=sig, f=f, ea=ea, eb=eb, eq=eq, ek=ek), blast

        dnw = jnp.zeros((1, HG_F), F32)
        for n in range(nc):
            for h in range(HG_HEADS):
                rows, hs = block(n, h)
                fw, blast = gates(rows, h)
                for name in KEEP_NAMES:
                    keep[name][rows, hs] = fw[name]
                ab_sc[rows, hs] = (fw["qs"] * fw["ea"]).astype(BF16)
                bkb_sc[rows, hs] = (fw["k"] * fw["eb"]).astype(BF16)
                qinb_sc[rows, hs] = (fw["qs"] * fw["eq"]).astype(BF16)
                koutb_sc[rows, hs] = (fw["k"] * fw["ek"]).astype(BF16)
                dec_sc[n:n + 1, hs] = jnp.exp(blast)
                o = opre_ref[rows, hs]
                g = p_ref[rows, C_HG + h * HG_F:C_HG + (h + 1) * HG_F].astype(F32)
                sg = _sigmoid(g)
                r = lax.rsqrt(jnp.mean(o * o, axis=-1, keepdims=True) + RMS_EPS)
                dyb = dy_ref[rows, W + h * HG_F:W + (h + 1) * HG_F].astype(F32)
                dp_ref[rows, C_HG + h * HG_F:C_HG + (h + 1) * HG_F] = (
                    dyb * (o * r * nw) * (sg * (1.0 + g * (1.0 - sg)))).astype(BF16)
                don = dyb * (g * sg)
                dnw = dnw + jnp.sum(don * o * r, axis=0, keepdims=True)
                dn = don * nw
                dob_sc[rows, hs] = (r * (dn - o * (r * r) * jnp.mean(dn * o, axis=-1, keepdims=True))).astype(BF16)
        dnw_ref[0:1, :] += dnw

        heads = [slice(h * HG_F, (h + 1) * HG_F) for h in range(HG_HEADS)]
        scores = [_dot(ab_sc[:, hs], bkb_sc[:, hs], NT) for hs in heads]
        dscores = [_dot(dob_sc[:, hs], p_ref[:, C_HI + h * HG_F:C_HI + (h + 1) * HG_F].astype(BF16), NT)
                   for h, hs in enumerate(heads)]
        scores = [jnp.where(mask, s, 0.0).astype(BF16) for s in scores]
        dscores = [jnp.where(mask, s, 0.0).astype(BF16) for s in dscores]
        for h, hs in enumerate(heads):
            dv_sc[:, hs] = _dot(scores[h], dob_sc[:, hs], TN)
            da_sc[:, hs] = _dot(dscores[h], bkb_sc[:, hs], NN)
            dbk_sc[:, hs] = _dot(dscores[h], ab_sc[:, hs], TN)
        dst = [dstate_sc[h] for h in range(HG_HEADS)]
        for n in reversed(range(nc)):
            for h in range(HG_HEADS):
                rows, hs = block(n, h)
                st_n = st_ref[n, h]
                decay = dec_sc[n:n + 1, hs]
                dstb = dst[h].astype(BF16)
                dob_n = dob_sc[rows, hs]
                dv_sc[rows, hs] += _dot(koutb_sc[rows, hs], dstb, NT)
                dkout_sc[rows, hs] = _dot(p_ref[rows, C_HI + h * HG_F:C_HI + (h + 1) * HG_F].astype(BF16), dstb, NN)
                ddec_sc[n:n + 1, hs] = jnp.sum(dst[h] * st_n, axis=0, keepdims=True) * decay
                dqin_sc[rows, hs] = _dot(dob_n, st_n.astype(BF16), NN)
                dst[h] = dst[h] * decay + _dot(dob_n, qinb_sc[rows, hs], TN)
        for h in range(HG_HEADS):
            dstate_sc[h] = dst[h]

        for h in range(HG_HEADS):
            dlb = jnp.zeros((1, HG_F), F32)
            for n in range(nc):
                rows, hs = block(n, h)
                fw = {name: keep[name][rows, hs] for name in KEEP_NAMES}
                lbh = lb_ref[:, h * HG_F:(h + 1) * HG_F]
                q = p_ref[rows, C_HQ + h * HG_F:C_HQ + (h + 1) * HG_F].astype(F32)
                da, dbk, dqin, dkout = da_sc[rows, hs], dbk_sc[rows, hs], dqin_sc[rows, hs], dkout_sc[rows, hs]
                w_a, w_b, w_q, w_k = da * fw["ea"], dbk * fw["eb"], dqin * fw["eq"], dkout * fw["ek"]
                dqs, dk = w_a + w_q, w_b + w_k
                t_a, t_b, t_q, t_k = w_a * fw["qs"], w_b * fw["k"], w_q * fw["qs"], w_k * fw["k"]
                s_ref = jnp.sum(t_b - t_a, axis=0, keepdims=True)
                s_last = jnp.sum(t_k, axis=0, keepdims=True) + ddec_sc[n:n + 1, hs]
                dbc = (t_a - t_b + t_q - t_k) + jnp.where(pos_c == mid, s_ref, 0.0) + jnp.where(pos_c == last, s_last, 0.0)
                dfk = _seg_rev_cumsum(dbc, pos_c) / fw["f"] - dk
                sig, sq = fw["sig"], fw["sq"]
                dp_ref[rows, C_HF + h * HG_F:C_HF + (h + 1) * HG_F] = (dfk * (1.0 - lbh) * sig * (1.0 - sig)).astype(BF16)
                dlb = dlb + jnp.sum(dfk * (1.0 - sig), axis=0, keepdims=True)
                dp_ref[rows, C_HQ + h * HG_F:C_HQ + (h + 1) * HG_F] = (dqs * (sq * (1.0 + q * (1.0 - sq)))).astype(BF16)
                dp_ref[rows, C_HI + h * HG_F:C_HI + (h + 1) * HG_F] = dv_sc[rows, hs].astype(BF16)
            dlb_ref[0:1, h * HG_F:(h + 1) * HG_F] += dlb

        mheads = [slice(h * MEM_HEAD_DIM, (h + 1) * MEM_HEAD_DIM) for h in range(MEM_HEADS)]
        qhs = [p_ref[:, C_MQ + h * MEM_HEAD_DIM:C_MQ + (h + 1) * MEM_HEAD_DIM].astype(BF16) for h in range(MEM_HEADS)]
        dobs = [dy_ref[:, 2 * W + h * MEM_HEAD_DIM:2 * W + (h + 1) * MEM_HEAD_DIM].astype(BF16) for h in range(MEM_HEADS)]
        probs = [_attn_probs(qhs[h], mk_ref[:, hs]) for h, hs in enumerate(mheads)]
        dprobs = [_dot(dobs[h], mv_ref[:, hs], NT) for h, hs in enumerate(mheads)]
        for h, hs in enumerate(mheads):
            prob = probs[h]
            dmv_ref[:, hs] += _dot(prob.astype(BF16), dobs[h], TN)
            ds = prob * (dprobs[h] - jnp.sum(dprobs[h] * prob, axis=-1, keepdims=True)) * (MEM_HEAD_DIM ** -0.5)
            dsb = ds.astype(BF16)
            dp_ref[:, C_MQ + h * MEM_HEAD_DIM:C_MQ + (h + 1) * MEM_HEAD_DIM] = _dot(dsb, mk_ref[:, hs], NN).astype(BF16)
            dmk_ref[:, hs] += _dot(dsb, qhs[h], TN)

    def tile(b, s):
        return b * ns + (ns - 1 - s)

    return pl.pallas_call(
        body,
        name="mixer_bwd",
        grid=(bl, ns),
        in_specs=[
            pl.BlockSpec((ts, N_MIX), lambda b, s: (tile(b, s), 0)),
            pl.BlockSpec((PREV_ROWS, N_MIX), lambda b, s: (jnp.maximum(tile(b, s) * (ts // PREV_ROWS) - 1, 0), 0)),
            pl.BlockSpec((ts, 3 * W), lambda b, s: (tile(b, s), 0)),
            pl.BlockSpec(memory_space=pl.ANY),
            pl.BlockSpec((nc, HG_HEADS, HG_F, HG_F), lambda b, s: (tile(b, s), 0, 0, 0)),
            pl.BlockSpec((ts, W), lambda b, s: (tile(b, s), 0)),
            pl.BlockSpec((ml, W), lambda b, s: (b, 0)),
            pl.BlockSpec((ml, W), lambda b, s: (b, 0)),
            pl.BlockSpec((1, W), lambda b, s: (0, 0)),
            pl.BlockSpec((CONV_K, W), lambda b, s: (0, 0)),
            pl.BlockSpec((1, HG_F), lambda b, s: (0, 0)),
        ] + [ANY_SPEC] * len(deps),
        out_specs=[
            pl.BlockSpec((ts, N_MIX), lambda b, s: (tile(b, s), 0)),
            pl.BlockSpec((ml, W), lambda b, s: (b, 0)),
            pl.BlockSpec((ml, W), lambda b, s: (b, 0)),
            pl.BlockSpec((8, W), lambda b, s: (0, 0)),
            pl.BlockSpec((8, HG_F), lambda b, s: (0, 0)),
            pl.BlockSpec((8, W), lambda b, s: (0, 0)),
        ],
        out_shape=[
            jax.ShapeDtypeStruct((T, nin), BF16),
            jax.ShapeDtypeStruct((bl * ml, W), F32),
            jax.ShapeDtypeStruct((bl * ml, W), F32),
            jax.ShapeDtypeStruct((8, W), F32),
            jax.ShapeDtypeStruct((8, HG_F), F32),
            jax.ShapeDtypeStruct((8, W), F32),
        ],
        input_output_aliases={3: 0},
        scratch_shapes=[pltpu.VMEM((HG_HEADS, HG_F, HG_F), F32), pltpu.VMEM((8, W), F32), pltpu.VMEM((PREV_ROWS, W), F32)]
        + [pltpu.VMEM((ts, W), BF16)] * 5 + [pltpu.VMEM((ts, W), F32)] * 5 + [pltpu.VMEM((nc, W), F32)] * 2
        + [pltpu.VMEM((ts, W), F32)] * len(KEEP_NAMES),
        compiler_params=_cparams(("arbitrary", "arbitrary")),
    )(p, p, dy, dp_gates, st, opre, mk, mv, lb, conv_w, norm_w, *deps)


def _layer_norm_stats(z):
    mu = jnp.mean(z, axis=-1, keepdims=True)
    zc = z - mu
    rstd = lax.rsqrt(jnp.mean(zc * zc, axis=-1, keepdims=True) + LN_EPS)
    return zc * rstd, rstd


def _gate_specs(tm, d):
    g0 = N_MIX // d
    return [pl.BlockSpec((tm, d), functools.partial(lambda i, k: (i, g0 + k), k=k)) for k in range(N_BRANCH)]


def _merge_fwd(y, p, x0, wb, wo, bg, ln_g, ln_b, *, alpha, tm=512):
    T, d = x0.shape
    assert N_MIX % d == 0
    tm = _pick(T, (tm, 128, 8))

    def body(y_ref, g0_ref, g1_ref, g2_ref, x_ref, wb_ref, wo_ref, bg_ref, lg_ref, lb_ref, mg_ref, xh_ref, rs_ref, x1b_ref):
        merged = None
        for i, g_ref in enumerate((g0_ref, g1_ref, g2_ref)):
            r = _dot(y_ref[:, i * W:(i + 1) * W], wb_ref[i * W:(i + 1) * W, :], NN)
            t = _sigmoid(g_ref[...].astype(F32) + bg_ref[:, i * d:(i + 1) * d]) * r
            merged = t if merged is None else merged + t
        mb = merged.astype(BF16)
        mg_ref[...] = mb
        z = alpha * x_ref[...] + _dot(mb, wo_ref[...], NN)
        xh, rs = _layer_norm_stats(z)
        xh_ref[...], rs_ref[...] = xh, rs
        x1b_ref[...] = (xh * lg_ref[...] + lb_ref[...]).astype(BF16)

    row = lambda i: (i, 0)
    fix = lambda i: (0, 0)
    return pl.pallas_call(
        body,
        name="merge_fwd",
        grid=(T // tm,),
        in_specs=[pl.BlockSpec((tm, 3 * W), row)] + _gate_specs(tm, d) + [
            pl.BlockSpec((tm, d), row), pl.BlockSpec((3 * W, d), fix, pipeline_mode=pl.Buffered(1)),
            pl.BlockSpec((d, d), fix, pipeline_mode=pl.Buffered(1)), pl.BlockSpec((1, 3 * d), fix),
            pl.BlockSpec((1, d), fix), pl.BlockSpec((1, d), fix)],
        out_specs=[pl.BlockSpec((tm, d), row), pl.BlockSpec((tm, d), row), pl.BlockSpec((tm, 1), row), pl.BlockSpec((tm, d), row)],
        out_shape=[jax.ShapeDtypeStruct((T, d), BF16), jax.ShapeDtypeStruct((T, d), F32), jax.ShapeDtypeStruct((T, 1), F32),
                   jax.ShapeDtypeStruct((T, d), BF16)],
        compiler_params=_cparams(("parallel",)),
    )(y, p, p, p, x0, wb, wo, bg, ln_g, ln_b)


def _merge_bwd(dz, p, y, wb, wo, bg, *, tm=512):
    T, d = dz.shape
    nin = p.shape[1]
    tm = _pick(T, (tm, 128, 8))

    def body(dz_ref, g0_ref, g1_ref, g2_ref, y_ref, wb_ref, wo_ref, bg_ref, dr_ref, dp_ref, dy_ref, dbg_ref):
        @pl.when(pl.program_id(0) == 0)
        def _():
            dbg_ref[...] = jnp.zeros_like(dbg_ref)

        dmerged = _dot(dz_ref[...].astype(BF16), wo_ref[...], NT)
        dp_ref[:, 0:N_MIX] = jnp.zeros((tm, N_MIX), BF16)
        for i, g_ref in enumerate((g0_ref, g1_ref, g2_ref)):
            cs = slice(i * d, (i + 1) * d)
            s = _sigmoid(g_ref[...].astype(F32) + bg_ref[:, cs])
            drb = (dmerged * s).astype(BF16)
            dr_ref[:, cs] = drb
            dgate = dmerged * _dot(y_ref[:, i * W:(i + 1) * W], wb_ref[i * W:(i + 1) * W, :], NN) * s * (1.0 - s)
            dp_ref[:, N_MIX + i * d:N_MIX + (i + 1) * d] = dgate.astype(BF16)
            dbg_ref[0:1, cs] += jnp.sum(dgate, axis=0, keepdims=True)
            dy_ref[:, i * W:(i + 1) * W] = _dot(drb, wb_ref[i * W:(i + 1) * W, :], NT).astype(BF16)

    row = lambda i: (i, 0)
    fix = lambda i: (0, 0)
    return pl.pallas_call(
        body,
        name="merge_bwd",
        grid=(T // tm,),
        in_specs=[pl.BlockSpec((tm, d), row)] + _gate_specs(tm, d) + [
            pl.BlockSpec((tm, 3 * W), row), pl.BlockSpec((3 * W, d), fix, pipeline_mode=pl.Buffered(1)),
            pl.BlockSpec((d, d), fix, pipeline_mode=pl.Buffered(1)), pl.BlockSpec((1, 3 * d), fix)],
        out_specs=[pl.BlockSpec((tm, 3 * d), row), pl.BlockSpec((tm, nin), row), pl.BlockSpec((tm, 3 * W), row),
                   pl.BlockSpec((8, 3 * d), fix)],
        out_shape=[jax.ShapeDtypeStruct((T, 3 * d), BF16), jax.ShapeDtypeStruct((T, nin), BF16),
                   jax.ShapeDtypeStruct((T, 3 * W), BF16), jax.ShapeDtypeStruct((8, 3 * d), F32)],
        compiler_params=_cparams(("arbitrary",)),
    )(dz, p, p, p, y, wb, wo, bg)


MLP_VMEM_LIMIT = 58 * 1024 * 1024


def _mlp_fwd(xhat1, x1b, g1, b1, wu, wd, g2, b2, *, alpha, tm=512, tf=1024):
    T, d = xhat1.shape
    ff = wu.shape[1]
    tm, tf = _pick(T, (tm, 256, 128, 8)), _pick(ff, (tf, 1024, 512, 256, 128))

    def body(xh_ref, x1b_ref, g1_ref, b1_ref, wu_ref, wd_ref, g2_ref, b2_ref, a_ref, xh2_ref, rs2_ref, x2_ref, x2b_ref):
        xb = x1b_ref[...]
        acc = None
        a = _dot(xb, wu_ref[:, 0:tf], NN)
        for c0 in range(0, ff, tf):
            a_next = _dot(xb, wu_ref[:, c0 + tf:c0 + 2 * tf], NN) if c0 + tf < ff else None
            a_ref[:, c0:c0 + tf] = a.astype(BF16)
            part = _dot(jnp.square(jnp.maximum(a, 0.0)).astype(BF16), wd_ref[c0:c0 + tf, :], NN)
            acc = part if acc is None else acc + part
            a = a_next
        x1 = xh_ref[...] * g1_ref[...] + b1_ref[...]
        xh2, rs2 = _layer_norm_stats(alpha * x1 + acc)
        xh2_ref[...] = xh2
        rs2_ref[...] = rs2
        x2 = xh2 * g2_ref[...] + b2_ref[...]
        x2_ref[...] = x2
        x2b_ref[...] = x2.astype(BF16)

    row = lambda i: (i, 0)
    fix = lambda i: (0, 0)
    once = dict(pipeline_mode=pl.Buffered(1))
    return pl.pallas_call(
        body,
        name="mlp_fwd",
        grid=(T // tm,),
        in_specs=[pl.BlockSpec((tm, d), row), pl.BlockSpec((tm, d), row), pl.BlockSpec((1, d), fix), pl.BlockSpec((1, d), fix),
                  pl.BlockSpec((d, ff), fix, **once), pl.BlockSpec((ff, d), fix, **once),
                  pl.BlockSpec((1, d), fix), pl.BlockSpec((1, d), fix)],
        out_specs=[pl.BlockSpec((tm, ff), row), pl.BlockSpec((tm, d), row), pl.BlockSpec((tm, 1), row),
                   pl.BlockSpec((tm, d), row), pl.BlockSpec((tm, d), row)],
        out_shape=[jax.ShapeDtypeStruct((T, ff), BF16), jax.ShapeDtypeStruct((T, d), F32), jax.ShapeDtypeStruct((T, 1), F32),
                   jax.ShapeDtypeStruct((T, d), F32), jax.ShapeDtypeStruct((T, d), BF16)],
        compiler_params=pltpu.CompilerParams(dimension_semantics=("parallel",), vmem_limit_bytes=MLP_VMEM_LIMIT),
    )(xhat1, x1b, g1, b1, wu, wd, g2, b2)


def _ln_bwd(dy, xhat, rstd, g, *, tm=1024, deps=()):
    T, d = dy.shape
    tm = _pick(T, (tm, 256, 128, 8))

    def body(dy_ref, xh_ref, rs_ref, g_ref, *rest):
        dz_ref, dzb_ref, dg_ref, db_ref = rest[len(deps):]

        @pl.when(pl.program_id(0) == 0)
        def _():
            dg_ref[...] = jnp.zeros_like(dg_ref)
            db_ref[...] = jnp.zeros_like(db_ref)

        dy_, xh = dy_ref[...], xh_ref[...]
        dg_ref[0:1, :] += jnp.sum(dy_ * xh, axis=0, keepdims=True)
        db_ref[0:1, :] += jnp.sum(dy_, axis=0, keepdims=True)
        dxh = dy_ * g_ref[...]
        dz = rs_ref[...] * (dxh - jnp.mean(dxh, axis=-1, keepdims=True) - xh * jnp.mean(dxh * xh, axis=-1, keepdims=True))
        dz_ref[...] = dz
        dzb_ref[...] = dz.astype(BF16)

    row = lambda i: (i, 0)
    fix = lambda i: (0, 0)
    return pl.pallas_call(
        body,
        name="ln_bwd",
        grid=(T // tm,),
        in_specs=[pl.BlockSpec((tm, d), row), pl.BlockSpec((tm, d), row), pl.BlockSpec((tm, 1), row), pl.BlockSpec((1, d), fix)]
        + [ANY_SPEC] * len(deps),
        out_specs=[pl.BlockSpec((tm, d), row), pl.BlockSpec((tm, d), row), pl.BlockSpec((8, d), fix), pl.BlockSpec((8, d), fix)],
        out_shape=[jax.ShapeDtypeStruct((T, d), F32), jax.ShapeDtypeStruct((T, d), BF16), jax.ShapeDtypeStruct((8, d), F32),
                   jax.ShapeDtypeStruct((8, d), F32)],
        compiler_params=_cparams(("arbitrary",)),
    )(dy, xhat, rstd, g, *deps)


def _loss_head(y, target, xhat, rstd, g, *, tm=512):
    T, d = y.shape
    tm = _pick(T, (tm, 256, 128, 8))
    n = T // tm

    def body(y_ref, t_ref, xh_ref, rs_ref, g_ref, loss_ref, dz_ref, dzb_ref, dg_ref, db_ref, acc_ref):
        i = pl.program_id(0)

        @pl.when(i == 0)
        def _():
            acc_ref[...] = jnp.zeros_like(acc_ref)
            dg_ref[...] = jnp.zeros_like(dg_ref)
            db_ref[...] = jnp.zeros_like(db_ref)

        e = y_ref[...] - t_ref[...]
        acc_ref[...] += jnp.sum(e * e, axis=0, keepdims=True)
        dy_, xh = e * (1.0 / d), xh_ref[...]
        dg_ref[0:1, :] += jnp.sum(dy_ * xh, axis=0, keepdims=True)
        db_ref[0:1, :] += jnp.sum(dy_, axis=0, keepdims=True)
        dxh = dy_ * g_ref[...]
        dz = rs_ref[...] * (dxh - jnp.mean(dxh, axis=-1, keepdims=True) - xh * jnp.mean(dxh * xh, axis=-1, keepdims=True))
        dz_ref[...] = dz
        dzb_ref[...] = dz.astype(BF16)

        @pl.when(i == n - 1)
        def _():
            loss_ref[...] = (0.5 / d) * jnp.sum(acc_ref[...], axis=1, keepdims=True)

    row = lambda i: (i, 0)
    fix = lambda i: (0, 0)
    return pl.pallas_call(
        body,
        name="loss_head",
        grid=(n,),
        in_specs=[pl.BlockSpec((tm, d), row), pl.BlockSpec((tm, d), row), pl.BlockSpec((tm, d), row), pl.BlockSpec((tm, 1), row),
                  pl.BlockSpec((1, d), fix)],
        out_specs=[pl.BlockSpec((1, 1), fix), pl.BlockSpec((tm, d), row), pl.BlockSpec((tm, d), row), pl.BlockSpec((8, d), fix),
                   pl.BlockSpec((8, d), fix)],
        out_shape=[jax.ShapeDtypeStruct((1, 1), F32), jax.ShapeDtypeStruct((T, d), F32), jax.ShapeDtypeStruct((T, d), BF16),
                   jax.ShapeDtypeStruct((8, d), F32), jax.ShapeDtypeStruct((8, d), F32)],
        scratch_shapes=[pltpu.VMEM((1, d), F32)],
        compiler_params=_cparams(("arbitrary",)),
    )(y, target, xhat, rstd, g)


def _lower_bounds_fwd(lower_bounds):
    depth, n = lower_bounds.shape

    def body(x_ref, soft_ref, lb_ref):
        x = x_ref[...]
        e = jnp.exp(x - jnp.max(x, axis=0, keepdims=True))
        soft_ref[...] = e / jnp.sum(e, axis=0, keepdims=True)
        run = None
        for l in range(depth):
            run = soft_ref[l:l + 1, :] if run is None else run + soft_ref[l:l + 1, :]
            lb_ref[l:l + 1, :] = run - soft_ref[0:1, :]

    return pl.pallas_call(body, name="lower_bounds_fwd",
                          out_shape=[jax.ShapeDtypeStruct((depth, n), F32), jax.ShapeDtypeStruct((depth, n), F32)])(lower_bounds)


def _lower_bounds_bwd(soft, dlb):
    depth, n = soft.shape

    def body(soft_ref, dlb_ref, out_ref, dsoft_ref):
        total = jnp.sum(dlb_ref[...], axis=0, keepdims=True)
        run = None
        for l in reversed(range(depth)):
            run = dlb_ref[l:l + 1, :] if run is None else run + dlb_ref[l:l + 1, :]
            dsoft_ref[l:l + 1, :] = run - total if l == 0 else run
        s, ds = soft_ref[...], dsoft_ref[...]
        out_ref[...] = s * (ds - jnp.sum(s * ds, axis=0, keepdims=True))

    return pl.pallas_call(body, name="lower_bounds_bwd", out_shape=jax.ShapeDtypeStruct((depth, n), F32),
                          scratch_shapes=[pltpu.VMEM((depth, n), F32)])(soft, dlb)


def _layer_fwd(x0, x0b, mem2, lb, w_in, mix_fn, late_fn, *, bl, seq, alpha, deps=()):
    p = _matmul("proj_in", x0b, w_in, mode="nn", out_dtype=BF16, deps=deps, tm=1024, tn=1792)
    wts = dict(mix_fn(p), w_in=w_in)
    mk = _matmul("mem_k", mem2, wts["w_mem_k"], mode="nn", out_dtype=BF16)
    mv = _matmul("mem_v", mem2, wts["w_mem_v"], mode="nn", out_dtype=BF16)
    y, st, opre = _mixer_fwd(p, mk, mv, lb, wts["conv_w"], wts["hg_norm_w"], bl=bl, seq=seq)
    wts.update(late_fn(y))
    merged, xhat1, rstd1, x1b = _merge_fwd(y, p, x0, wts["w_branch"], wts["w_o"], wts["b_gate"], wts["ln1_g"], wts["ln1_b"],
                                           alpha=alpha)
    a, xhat2, rstd2, x2, x2b = _mlp_fwd(xhat1, x1b, wts["ln1_g"], wts["ln1_b"], wts["w_up"], wts["w_down"], wts["ln2_g"],
                                        wts["ln2_b"], alpha=alpha)
    saved = dict(x0b=x0b, p=p, mk=mk, mv=mv, y=y, st=st, opre=opre, merged=merged, xhat1=xhat1, rstd1=rstd1, x1b=x1b, a=a,
                 xhat2=xhat2, rstd2=rstd2)
    return x2, x2b, saved, wts


def _relu2_bf16(a):
    return jnp.square(jnp.maximum(a.astype(F32), 0.0)).astype(BF16)


def _mlp_bwd(dz2, dz2b, sv, wts, *, alpha, deps=()):
    g = {}
    da = _matmul("mlp_da", dz2b, wts["w_down"], mode="nt", out_dtype=BF16, tm=512, tn=wts["w_down"].shape[0], deps=deps,
                 epi_fn=lambda acc, a: (acc * (2.0 * jnp.maximum(a.astype(F32), 0.0)),), epi_extra=(sv["a"],))
    g["w_down"] = _matmul_tn("grad_w_down", sv["a"], dz2b, a_fn=_relu2_bf16, out_dtype=BF16, tt=2048)
    g["w_up"] = _matmul_tn("grad_w_up", sv["x1b"], da, out_dtype=BF16, tt=2048)
    dx1 = _matmul("mlp_dx", da, wts["w_up"], mode="nt", epi_fn=lambda acc, dz: (acc + alpha * dz,), epi_extra=(dz2,),
                  tm=512, tk=4096)
    dz1, dz1b, dg1, db1 = _ln_bwd(dx1, sv["xhat1"], sv["rstd1"], wts["ln1_g"])
    g["ln1_g"], g["ln1_b"] = dg1[0:1], db1[0:1]
    return dz1, dz1b, g


def _mix_bwd(dz1, dz1b, sv, mem2, lb, wts, *, bl, seq, alpha, send, below=None, deps=()):
    d = dz1.shape[1]
    g = {}
    g["w_o"] = _matmul_tn("grad_w_o", sv["merged"], dz1b, out_dtype=BF16, tt=2048, deps=deps)
    dr, dp, dy, dbg = _merge_bwd(dz1b, sv["p"], sv["y"], wts["w_branch"], wts["w_o"], wts["b_gate"])
    g["b_gate"] = dbg[0:1]
    g["w_branch"] = jnp.concatenate(
        [_matmul_tn("grad_w_branch", sv["y"], dr, a_cols=(i * W, W), b_cols=(i * d, d), out_dtype=BF16, tt=4096)
         for i in range(N_BRANCH)],
        axis=0)
    token = send(("w_o", "w_branch"), g)
    dp, dmk, dmv, dcw, dnw, dlb = _mixer_bwd(sv["p"], dy, dp, sv["st"], sv["opre"], sv["mk"], sv["mv"], lb,
                                              wts["conv_w"], wts["hg_norm_w"], bl=bl, seq=seq, deps=(token,))
    g["conv_w"], g["hg_norm_w"], g["lb"] = dcw[0:CONV_K], dnw[0:1], dlb[0:1]
    g["w_mem_k"] = _matmul_tn("grad_w_mem_k", mem2, dmk, out_dtype=BF16)
    g["w_mem_v"] = _matmul_tn("grad_w_mem_v", mem2, dmv, out_dtype=BF16)
    g["w_in"] = _matmul_tn("grad_w_in", sv["x0b"], dp, out_dtype=BF16, tt=2048)
    token = send(("w_in", "w_mem_k", "w_mem_v", "conv_w"), g)
    dx0 = _matmul("proj_in_dx", dp, wts["w_in"], mode="nt", epi_fn=lambda acc, dz: (acc + alpha * dz,), epi_extra=(dz1,),
                  tm=512, tk=dp.shape[1], deps=(token,))
    return (dx0 if below is None else _ln_bwd(dx0, *below)), g


N_CHIPS = 4
MESH_IDS = pl.DeviceIdType.MESH


def _axis_slice(ref, axis, start, size):
    idx = [slice(None)] * len(ref.shape)
    idx[axis] = pl.ds(start, size)
    return ref.at[tuple(idx)]


def _chip_exchange(name, items):
    n = len(items)
    out_shapes, meta = [], []
    for arr, kind, axis in items:
        shp = list(arr.shape)
        if kind == "gather":
            per = shp[axis]
            shp[axis] = per * N_CHIPS
            out_shapes.append(jax.ShapeDtypeStruct(tuple(shp), arr.dtype))
        elif kind == "scatter":
            per = shp[axis] // N_CHIPS
            shp[axis] = per
            out_shapes.append(jax.ShapeDtypeStruct((N_CHIPS, *shp), arr.dtype))
        else:
            per = None
            out_shapes.append(jax.ShapeDtypeStruct((N_CHIPS, *shp), arr.dtype))
        meta.append((kind, axis, per))

    def body(*refs):
        ins, outs = refs[:n], refs[n:2 * n]
        send_sems, recv_sems, local_sems = refs[2 * n:]
        x, y, c = lax.axis_index("x"), lax.axis_index("y"), lax.axis_index("c")
        me = 2 * x + y
        peers = [(1 - x, y), (x, 1 - y), (1 - x, 1 - y)]

        def src_for(t, chip):
            kind, axis, per = meta[t]
            return _axis_slice(ins[t], axis, chip * per, per) if kind == "scatter" else ins[t]

        def dst_from(t, chip):
            kind, axis, per = meta[t]
            return _axis_slice(outs[t], axis, chip * per, per) if kind == "gather" else outs[t].at[chip]

        def remote(t, k):
            px, py = peers[k]
            return pltpu.make_async_remote_copy(
                src_ref=src_for(t, 2 * px + py), dst_ref=dst_from(t, me), send_sem=send_sems.at[t * 3 + k],
                recv_sem=recv_sems.at[t * 3 + k], device_id=(px, py, c), device_id_type=MESH_IDS)

        def arrival(t, k):
            px, py = peers[k]
            return pltpu.make_async_remote_copy(
                src_ref=src_for(t, me), dst_ref=dst_from(t, 2 * px + py), send_sem=send_sems.at[t * 3 + k],
                recv_sem=recv_sems.at[t * 3 + k], device_id=(px, py, c), device_id_type=MESH_IDS)

        sends = [remote(t, k) for t in range(n) for k in range(3)]
        for cp in sends:
            cp.start()
        own = [pltpu.make_async_copy(src_for(t, me), dst_from(t, me), local_sems.at[t]) for t in range(n)]
        for cp in own:
            cp.start()
        for t in range(n):
            for k in range(3):
                arrival(t, k).wait_recv()
        for cp in sends:
            cp.wait_send()
        for cp in own:
            cp.wait()

    any_spec = pl.BlockSpec(memory_space=pl.ANY)
    return pl.pallas_call(
        body,
        name=name,
        in_specs=[any_spec] * n,
        out_specs=[any_spec] * n,
        out_shape=out_shapes,
        scratch_shapes=[pltpu.SemaphoreType.DMA((3 * n,)), pltpu.SemaphoreType.DMA((3 * n,)), pltpu.SemaphoreType.DMA((n,))],
        compiler_params=pltpu.CompilerParams(has_side_effects=True),
    )(*[a for a, _, _ in items])


HBM_SPEC = pl.BlockSpec(memory_space=pltpu.HBM)
SEM_SPEC = pl.BlockSpec(memory_space=pltpu.SEMAPHORE)
N_PEERS = N_CHIPS - 1


def _my_chip():
    return (2 * lax.axis_index("x") + lax.axis_index("y")).astype(jnp.int32).reshape(1)


def _own_block_spec(r, c, axis, tr):
    if axis == 1:
        return pl.BlockSpec((tr, c), lambda i, me: (i, me[0]))
    return pl.BlockSpec((tr, c), lambda i, me: (me[0] * (r // tr) + i, 0))


def _place_shard(name, shard, axis, me):
    r, c = shard.shape
    tr = _row_block(r, c, shard.dtype.itemsize)
    shp = (r, c * N_CHIPS) if axis == 1 else (r * N_CHIPS, c)

    def body(me_ref, s_ref, buf_ref, o_ref):
        del me_ref, buf_ref
        o_ref[...] = s_ref[...]

    buf = pltpu.with_memory_space_constraint(lax.empty(shp, shard.dtype), pltpu.HBM)
    return pl.pallas_call(
        body, name=name,
        grid_spec=pltpu.PrefetchScalarGridSpec(
            num_scalar_prefetch=1, grid=(r // tr,),
            in_specs=[pl.BlockSpec((tr, c), lambda i, me: (i, 0)), ANY_SPEC], out_specs=_own_block_spec(r, c, axis, tr)),
        out_shape=jax.ShapeDtypeStruct(shp, shard.dtype),
        input_output_aliases={2: 0},
        compiler_params=_cparams(("parallel",)),
    )(me, shard, buf)


class _Split:
    def __init__(self, name, items):
        self.name, self.n = name, len(items)
        self.srcs = [a for a, _, _ in items]
        self.meta, self.land_shapes = [], []
        for arr, kind, axis in items:
            shp = list(arr.shape)
            if kind == "gather":
                per = shp[axis]
                shp[axis] = per * N_CHIPS
                self.land_shapes.append(jax.ShapeDtypeStruct(tuple(shp), arr.dtype))
            else:
                per = shp[axis] // N_CHIPS
                shp[axis] = per
                self.land_shapes.append(jax.ShapeDtypeStruct((N_PEERS, *shp), arr.dtype))
            self.meta.append((kind, axis, per))

    def _src(self, ins, t, chip):
        kind, axis, per = self.meta[t]
        return _axis_slice(ins[t], axis, chip * per, per) if kind == "scatter" else ins[t]

    def _dst(self, lands, t, chip, slot):
        kind, axis, per = self.meta[t]
        return _axis_slice(lands[t], axis, chip * per, per) if kind == "gather" else lands[t].at[slot]

    def landing_zones(self, me):
        return [_place_shard(self.name + "_own", src, axis, me) if kind == "gather" else lax.empty(ls.shape, ls.dtype)
                for src, ls, (kind, axis, _) in zip(self.srcs, self.land_shapes, self.meta)]

    def _copies(self, ins, lands, send_sems, recv_sems, arrivals):
        x, y, c = lax.axis_index("x"), lax.axis_index("y"), lax.axis_index("c")
        me = 2 * x + y
        peers = [(1 - x, y), (x, 1 - y), (1 - x, 1 - y)]
        res = []
        for t in range(self.n):
            for k, (px, py) in enumerate(peers):
                theirs = 2 * px + py
                sems = dict(send_sem=send_sems.at[t * N_PEERS + k], recv_sem=recv_sems.at[t * N_PEERS + k],
                            device_id=(px, py, c), device_id_type=MESH_IDS)
                if arrivals:
                    res.append(pltpu.make_async_remote_copy(src_ref=self._src(ins, t, me), dst_ref=self._dst(lands, t, theirs, k), **sems))
                else:
                    res.append(pltpu.make_async_remote_copy(src_ref=self._src(ins, t, theirs), dst_ref=self._dst(lands, t, me, k), **sems))
        return res

    def start(self, lands, deps=()):
        n, nd = self.n, len(deps)

        def body(*refs):
            ins, lnd = refs[:n], refs[n:2 * n]
            send_sems, recv_sems = refs[2 * n + nd], refs[2 * n + nd + 1]
            token = refs[-1]
            for cp in self._copies(ins, lnd, send_sems, recv_sems, arrivals=False):
                cp.start()
            token[...] = jnp.zeros_like(token)

        hbm = lambda a: pltpu.HBM(a.shape, a.dtype)
        res = pl.pallas_call(
            body, name=self.name + "_start",
            in_specs=[HBM_SPEC] * (2 * n) + [ANY_SPEC] * nd,
            out_specs=[SEM_SPEC, SEM_SPEC] + [HBM_SPEC] * (2 * n) + [pl.BlockSpec(memory_space=pltpu.VMEM)],
            out_shape=[pltpu.SemaphoreType.DMA((N_PEERS * n,)), pltpu.SemaphoreType.DMA((N_PEERS * n,))]
            + [hbm(a) for a in self.srcs] + [hbm(a) for a in self.land_shapes] + [jax.ShapeDtypeStruct((8, 128), F32)],
            input_output_aliases={i: 2 + i for i in range(2 * n)},
            compiler_params=pltpu.CompilerParams(has_side_effects=pltpu.SideEffectType.DATAFLOW_SIDE_EFFECTING),
        )(*[pltpu.with_memory_space_constraint(a, pltpu.HBM) for a in self.srcs],
          *[pltpu.with_memory_space_constraint(a, pltpu.HBM) for a in lands], *deps)
        return res[:-1], res[-1]

    def wait(self, state, after):
        n = self.n
        after = tuple(after) if isinstance(after, (tuple, list)) else (after,)
        send_sems, recv_sems = state[0], state[1]
        srcs, lands = state[2:2 + n], state[2 + n:2 + 2 * n]

        def body(*refs):
            ins, lnd = refs[:n], refs[n:2 * n]
            s_sems, r_sems = refs[2 * n], refs[2 * n + 1]
            for cp in self._copies(ins, lnd, s_sems, r_sems, arrivals=True):
                cp.wait_recv()
            for cp in self._copies(ins, lnd, s_sems, r_sems, arrivals=False):
                cp.wait_send()

        hbm = lambda a: pltpu.HBM(a.shape, a.dtype)
        res = pl.pallas_call(
            body, name=self.name + "_wait",
            in_specs=[HBM_SPEC] * (2 * n) + [SEM_SPEC, SEM_SPEC] + [ANY_SPEC] * len(after),
            out_specs=[HBM_SPEC] * (2 * n),
            out_shape=[hbm(a) for a in self.srcs] + [hbm(a) for a in self.land_shapes],
            input_output_aliases={i: i for i in range(2 * n)},
            compiler_params=pltpu.CompilerParams(has_side_effects=pltpu.SideEffectType.DATAFLOW_SIDE_EFFECTING),
        )(*srcs, *lands, send_sems, recv_sems, *after)
        return res[:n], res[n:]


class _SiblingSplit:
    def __init__(self, name, arrays):
        self.name, self.n, self.arrays = name, len(arrays), list(arrays)

    def _copies(self, ins, lands, send_sems, recv_sems):
        sibling = (lax.axis_index("x"), lax.axis_index("y"), 1 - lax.axis_index("c"))
        return [pltpu.make_async_remote_copy(src_ref=ins[t], dst_ref=lands[t], send_sem=send_sems.at[t], recv_sem=recv_sems.at[t],
                                             device_id=sibling, device_id_type=MESH_IDS) for t in range(self.n)]

    def start(self, deps=()):
        n, nd = self.n, len(deps)

        def body(*refs):
            for cp in self._copies(refs[:n], refs[n:2 * n], refs[2 * n + nd], refs[2 * n + nd + 1]):
                cp.start()
            refs[-1][...] = jnp.zeros_like(refs[-1])

        hbm = [pltpu.HBM(a.shape, a.dtype) for a in self.arrays]
        res = pl.pallas_call(
            body, name=self.name + "_start",
            in_specs=[HBM_SPEC] * (2 * n) + [ANY_SPEC] * nd,
            out_specs=[SEM_SPEC, SEM_SPEC] + [HBM_SPEC] * (2 * n) + [pl.BlockSpec(memory_space=pltpu.VMEM)],
            out_shape=[pltpu.SemaphoreType.DMA((n,)), pltpu.SemaphoreType.DMA((n,))] + hbm + hbm + [jax.ShapeDtypeStruct((8, 128), F32)],
            input_output_aliases={i: 2 + i for i in range(2 * n)},
            compiler_params=pltpu.CompilerParams(has_side_effects=pltpu.SideEffectType.DATAFLOW_SIDE_EFFECTING),
        )(*[pltpu.with_memory_space_constraint(a, pltpu.HBM) for a in self.arrays],
          *[pltpu.with_memory_space_constraint(lax.empty(a.shape, a.dtype), pltpu.HBM) for a in self.arrays], *deps)
        return res[:-1], res[-1]

    def wait(self, state, after):
        n = self.n
        after = tuple(after) if isinstance(after, (tuple, list)) else (after,)

        def body(*refs):
            for cp in self._copies(refs[:n], refs[n:2 * n], refs[2 * n], refs[2 * n + 1]):
                cp.wait()

        hbm = [pltpu.HBM(a.shape, a.dtype) for a in self.arrays]
        res = pl.pallas_call(
            body, name=self.name + "_wait",
            in_specs=[HBM_SPEC] * (2 * n) + [SEM_SPEC, SEM_SPEC] + [ANY_SPEC] * len(after),
            out_specs=[HBM_SPEC] * (2 * n),
            out_shape=hbm + hbm,
            input_output_aliases={i: i for i in range(2 * n)},
            compiler_params=pltpu.CompilerParams(has_side_effects=pltpu.SideEffectType.DATAFLOW_SIDE_EFFECTING),
        )(*state[2:2 + 2 * n], state[0], state[1], *after)
        return res[:n], res[n:]


def _sibling_swap(name, arrays):
    n = len(arrays)

    def body(*refs):
        ins, outs = refs[:n], refs[n:2 * n]
        send_sems, recv_sems = refs[2 * n:]
        sibling = (lax.axis_index("x"), lax.axis_index("y"), 1 - lax.axis_index("c"))
        copies = [pltpu.make_async_remote_copy(src_ref=ins[t], dst_ref=outs[t], send_sem=send_sems.at[t], recv_sem=recv_sems.at[t],
                                               device_id=sibling, device_id_type=MESH_IDS) for t in range(n)]
        for cp in copies:
            cp.start()
        for cp in copies:
            cp.wait()

    any_spec = pl.BlockSpec(memory_space=pl.ANY)
    return pl.pallas_call(
        body,
        name=name,
        in_specs=[any_spec] * n,
        out_specs=[any_spec] * n,
        out_shape=[jax.ShapeDtypeStruct(a.shape, a.dtype) for a in arrays],
        scratch_shapes=[pltpu.SemaphoreType.DMA((n,)), pltpu.SemaphoreType.DMA((n,))],
        compiler_params=pltpu.CompilerParams(has_side_effects=True),
    )(*arrays)


def _row_block(r, c, itemsize=4, target=1 << 20):
    if r % 8 != 0:
        return r
    best = 8
    for tr in range(8, r + 1, 8):
        if r % tr == 0 and tr * c * itemsize <= target:
            best = tr
    return best


def _sum_chips_into(parts, stacked, layer):
    _, r, c = parts.shape
    tr = _row_block(r, c)

    def body(p_ref, s_ref, o_ref):
        del s_ref
        o_ref[...] = ((p_ref[0] + p_ref[1]) + p_ref[2]) + p_ref[3]

    return pl.pallas_call(
        body,
        name="sum_chips",
        grid=(r // tr,),
        in_specs=[pl.BlockSpec((N_CHIPS, tr, c), lambda i: (0, i, 0)), pl.BlockSpec(memory_space=pl.ANY)],
        out_specs=pl.BlockSpec((None, tr, c), lambda i: (layer, i, 0)),
        out_shape=jax.ShapeDtypeStruct(stacked.shape, stacked.dtype),
        input_output_aliases={1: 0},
        compiler_params=_cparams(("parallel",)),
    )(parts, stacked)


def _sum_own_and_peers(me, g, axis, landed):
    _, r, c = landed.shape
    tr = _row_block(r, c, target=1 << 21)

    def body(me_ref, g_ref, p_ref, o_ref):
        del me_ref
        o_ref[...] = ((g_ref[...].astype(F32) + p_ref[0].astype(F32)) + p_ref[1].astype(F32)) + p_ref[2].astype(F32)

    return pl.pallas_call(
        body, name="sum_chips_own",
        grid_spec=pltpu.PrefetchScalarGridSpec(
            num_scalar_prefetch=1, grid=(r // tr,),
            in_specs=[_own_block_spec(r, c, axis, tr), pl.BlockSpec((N_PEERS, tr, c), lambda i, me: (0, i, 0))],
            out_specs=pl.BlockSpec((tr, c), lambda i, me: (i, 0))),
        out_shape=jax.ShapeDtypeStruct((r, c), F32),
        compiler_params=_cparams(("parallel",)),
    )(me, g, landed)


ADAMW_BLOCK_BYTES = 3 << 19


def _adamw_math(w, m, v, g):
    m_new = ADAM_B1 * m + (1.0 - ADAM_B1) * g
    v_new = ADAM_B2 * v + (1.0 - ADAM_B2) * jnp.square(g)
    m_hat = m_new / (1.0 - ADAM_B1 ** ADAM_STEP)
    v_hat = v_new / (1.0 - ADAM_B2 ** ADAM_STEP)
    return -ADAM_LR * (m_hat / (jnp.sqrt(v_hat) + ADAM_EPS) + ADAM_WD * w), m_new, v_new


def _adamw(w, m, v, g_a, g_b):
    L, r, c = w.shape
    tr = _row_block(r, c, target=ADAMW_BLOCK_BYTES)

    def body(w_ref, m_ref, v_ref, ga_ref, gb_ref, g_ref, d_ref, nm_ref, nv_ref):
        g = ga_ref[...] + gb_ref[...]
        g_ref[...] = g
        d_ref[...], nm_ref[...], nv_ref[...] = _adamw_math(w_ref[...], m_ref[...], v_ref[...], g)

    spec = pl.BlockSpec((None, tr, c), lambda l, i: (l, i, 0))
    return pl.pallas_call(
        body,
        name="adamw",
        grid=(L, r // tr),
        in_specs=[spec] * 5,
        out_specs=[spec] * 4,
        out_shape=[jax.ShapeDtypeStruct(w.shape, F32)] * 4,
        compiler_params=_cparams(("parallel", "parallel")),
    )(w, m, v, g_a, g_b)


def _adamw_layer(w, m, v, g_a, g_b, layer, outs):
    L, r, c = w.shape
    tr = _row_block(r, c, target=ADAMW_BLOCK_BYTES)
    n_prev = 0 if outs is None else 4

    def body(w_ref, m_ref, v_ref, ga_ref, gb_ref, *rest):
        g_ref, d_ref, nm_ref, nv_ref = rest[n_prev:]
        g = ga_ref[...] + gb_ref[...]
        g_ref[...] = g
        d_ref[...], nm_ref[...], nv_ref[...] = _adamw_math(w_ref[...], m_ref[...], v_ref[...], g)

    at_layer = pl.BlockSpec((None, tr, c), lambda i: (layer, i, 0))
    flat = pl.BlockSpec((tr, c), lambda i: (i, 0))
    return pl.pallas_call(
        body,
        name="adamw_layer",
        grid=(r // tr,),
        in_specs=[at_layer] * 3 + [flat] * 2 + [ANY_SPEC] * n_prev,
        out_specs=[at_layer] * 4,
        out_shape=[jax.ShapeDtypeStruct(w.shape, F32)] * 4,
        input_output_aliases={5 + k: k for k in range(n_prev)},
        compiler_params=_cparams(("parallel",)),
    )(w, m, v, g_a, g_b, *(outs or ()))


SHARDED = (("w_in", 1), ("conv_w", 1), ("w_mem_k", 0), ("w_mem_v", 0), ("w_branch", 1), ("w_o", 0), ("w_up", 1), ("w_down", 0))
SMALL = ("lower_bounds", "hg_norm_w", "b_gate", "ln1_g", "ln1_b", "ln2_g", "ln2_b")
WEIGHT_ORDER = ("lower_bounds", "w_in", "conv_w", "hg_norm_w", "w_mem_k", "w_mem_v", "w_branch", "b_gate", "w_o", "ln1_g", "ln1_b",
                "w_up", "w_down", "ln2_g", "ln2_b")


def kernel(x, mem, lower_bounds, w_in, conv_w, hg_norm_w, w_mem_k, w_mem_v, w_branch, b_gate, w_o, ln1_g, ln1_b, w_up, w_down, ln2_g, ln2_b, loss_target, m_lower_bounds, m_w_in, m_conv_w, m_hg_norm_w, m_w_mem_k, m_w_mem_v, m_w_branch, m_b_gate, m_w_o, m_ln1_g, m_ln1_b, m_w_up, m_w_down, m_ln2_g, m_ln2_b, v_lower_bounds, v_w_in, v_conv_w, v_hg_norm_w, v_w_mem_k, v_w_mem_v, v_w_branch, v_b_gate, v_w_o, v_ln1_g, v_ln1_b, v_w_up, v_w_down, v_ln2_g, v_ln2_b):
    bl, seq, d = x.shape
    depth = w_in.shape[0]
    weights = dict(lower_bounds=lower_bounds, w_in=w_in, conv_w=conv_w, hg_norm_w=hg_norm_w, w_mem_k=w_mem_k, w_mem_v=w_mem_v,
                   w_branch=w_branch, b_gate=b_gate, w_o=w_o, ln1_g=ln1_g, ln1_b=ln1_b, w_up=w_up, w_down=w_down, ln2_g=ln2_g, ln2_b=ln2_b)
    mom_m = dict(lower_bounds=m_lower_bounds, w_in=m_w_in, conv_w=m_conv_w, hg_norm_w=m_hg_norm_w, w_mem_k=m_w_mem_k, w_mem_v=m_w_mem_v,
                 w_branch=m_w_branch, b_gate=m_b_gate, w_o=m_w_o, ln1_g=m_ln1_g, ln1_b=m_ln1_b, w_up=m_w_up, w_down=m_w_down,
                 ln2_g=m_ln2_g, ln2_b=m_ln2_b)
    mom_v = dict(lower_bounds=v_lower_bounds, w_in=v_w_in, conv_w=v_conv_w, hg_norm_w=v_hg_norm_w, w_mem_k=v_w_mem_k, w_mem_v=v_w_mem_v,
                 w_branch=v_w_branch, b_gate=v_b_gate, w_o=v_w_o, ln1_g=v_ln1_g, ln1_b=v_ln1_b, w_up=v_w_up, w_down=v_w_down,
                 ln2_g=v_ln2_g, ln2_b=v_ln2_b)

    def shard2d(name, l):
        w = weights[name][l]
        if name == "w_branch":
            return w.reshape(N_BRANCH * W, w.shape[-1]).astype(BF16)
        return w if name == "conv_w" else w.astype(BF16)

    me = _my_chip()

    shard_axis = dict(SHARDED)

    def prepare_exchange(name, kind, items):
        ex = _Split(name, [(arr, kind, shard_axis[nm]) for nm, arr in items])
        return ex, ex.landing_zones(me), [nm for nm, _ in items]

    def launch(prepared, deps=()):
        ex, lands, names = prepared
        state, token = ex.start(lands, deps)
        return ex, state, names, token

    def start_exchange(name, kind, items, deps=()):
        return launch(prepare_exchange(name, kind, items), deps)

    def prepare_gathers(l):
        groups = (("in", ("w_in",)), ("mix", ("conv_w", "w_mem_k", "w_mem_v")), ("rest", ("w_branch", "w_o", "w_up", "w_down")))
        return tuple(prepare_exchange(f"gather_{tag}_l{l}", "gather", [(nm, shard2d(nm, l)) for nm in names]) for tag, names in groups)

    def start_gathers(prepared, deps=()):
        started = []
        for prep in prepared:
            started.append(launch(prep, deps))
            deps = (started[-1][3],)
        return tuple(started)

    def gathered(pend, after):
        ex, state, names, _ = pend
        return dict(zip(names, ex.wait(state, after=after)[1]))

    pending = start_gathers(prepare_gathers(0))
    tokens = tuple(pend[3] for pend in pending)
    tokens, x, mem, loss_target, weights, mom_m, mom_v = lax.optimization_barrier((tokens, x, mem, loss_target, weights, mom_m, mom_v))
    pending = tuple((*pend[:3], tok) for pend, tok in zip(pending, tokens))
    lower_bounds = weights["lower_bounds"]

    x2d, mem2, t2d = x.reshape(bl * seq, d), mem.reshape(-1, d), loss_target.reshape(bl * seq, d)
    alpha = (2.0 * depth) ** 0.25
    soft, lb_all = _lower_bounds_fwd(lower_bounds)

    prepared = [None] + [prepare_gathers(l) for l in range(1, depth)]
    early = [x2d.astype(BF16), lb_all] + [z for prep in prepared[1:] for _, lands, _ in prep for z in lands]

    h, hb, saved, layer_wts = x2d, early[0], [], []
    for l in range(depth):
        first, mix, rest = pending
        w_in_l = gathered(first, early if l == 0 else h)["w_in"]

        def mix_fn(after, l=l, mix=mix):
            return dict(gathered(mix, after), hg_norm_w=weights["hg_norm_w"][l][None, :])

        def late_fn(after, l=l, rest=rest):
            wts = gathered(rest, after)
            for name in ("b_gate", "ln1_g", "ln1_b", "ln2_g", "ln2_b"):
                wts[name] = weights[name][l][None, :]
            return wts

        deps = (rest[3],)
        if l + 1 < depth:
            pending = start_gathers(prepared[l + 1], (w_in_l, rest[3]))
            deps += tuple(pend[3] for pend in pending)
        h, hb, sv, wts = _layer_fwd(h, hb, mem2, lb_all[l:l + 1], w_in_l, mix_fn, late_fn, bl=bl, seq=seq, alpha=alpha, deps=deps)
        saved.append(sv)
        layer_wts.append(wts)
    loss, dz2, dz2b, dg2, db2 = _loss_head(h, t2d, saved[-1]["xhat2"], saved[-1]["rstd2"], layer_wts[-1]["ln2_g"])

    shape3 = {name: (depth, weights[name].size // (depth * weights[name].shape[-1]), weights[name].shape[-1]) for name, _ in SHARDED}
    partial = [dict() for _ in range(depth)]
    smalls = [None] * depth
    outs = {name: None for name, _ in SHARDED}

    def finish_reduce(pend, l, after):
        ex, state, names, _ = pend
        sent, got = ex.wait(state, after=after)
        for nm, g_full, landed in zip(names, sent, got):
            partial[l][nm] = _sum_own_and_peers(me, g_full, shard_axis[nm], landed)

    names_sharded = [name for name, _ in SHARDED]

    def start_swap(l):
        swap = _SiblingSplit(f"swap_partials_l{l}", [partial[l][nm] for nm in names_sharded])
        state, token = swap.start()
        return swap, state, token

    def optimizer_step(l, pend, after):
        swap, state, _ = pend
        mine, theirs = swap.wait(state, after)
        for nm, own, other in zip(names_sharded, mine, theirs):
            outs[nm] = _adamw_layer(weights[nm].reshape(shape3[nm]), mom_m[nm].reshape(shape3[nm]), mom_v[nm].reshape(shape3[nm]),
                                    own, other, l, outs[nm])
        return tuple(outs[nm][0] for nm in names_sharded)

    pending_mix, pending_swap, deps = [], None, ()
    for l in reversed(range(depth)):
        dz1, dz1b, g_mlp = _mlp_bwd(dz2, dz2b, saved[l], layer_wts[l], alpha=alpha, deps=deps)
        g_mlp["ln2_g"], g_mlp["ln2_b"] = dg2[0:1], db2[0:1]
        pending_mlp = start_exchange(f"reduce_mlp_l{l}", "scatter", [(nm, g_mlp[nm]) for nm in ("w_up", "w_down")])
        deps = (pending_mlp[3],)
        if pending_mix:
            for pend in pending_mix:
                finish_reduce(pend, l + 1, dz1)
            pending_swap = start_swap(l + 1)
            deps += (pending_swap[2],)
        pending_mix = []

        def send(names, g, l=l, pending_mix=pending_mix):
            pend = start_exchange(f"reduce_{names[0]}_l{l}", "scatter", [(nm, g[nm]) for nm in names])
            pending_mix.append(pend)
            return pend[3]

        below = (saved[l - 1]["xhat2"], saved[l - 1]["rstd2"], layer_wts[l - 1]["ln2_g"]) if l > 0 else None
        out, g = _mix_bwd(dz1, dz1b, saved[l], mem2, lb_all[l:l + 1], layer_wts[l], bl=bl, seq=seq, alpha=alpha, send=send,
                          below=below, deps=deps)
        if l > 0:
            dz2, dz2b, dg2, db2 = out
        else:
            dh = out
        finish_reduce(pending_mlp, l, out[0] if l > 0 else out)
        deps = ()
        if pending_swap is not None:
            deps = optimizer_step(l + 1, pending_swap, out[0] if l > 0 else out)
            pending_swap = None
        g.update(g_mlp, lower_bounds=g["lb"])
        smalls[l] = jnp.concatenate([g[nm] for nm in SMALL], axis=1)
    small_parts = _chip_exchange("reduce_small", [(jnp.stack(smalls), "bcast", 0)])[0]
    small_sum = _sum_chips_into(small_parts.reshape(N_CHIPS, depth, -1), jnp.zeros((1, depth, small_parts.shape[-1]), F32), 0)
    small_sum = small_sum.reshape(depth, 1, -1)
    small_theirs = _sibling_swap("swap_small", [small_sum])[0]
    for pend in pending_mix:
        finish_reduce(pend, 0, small_theirs)
    optimizer_step(0, start_swap(0), small_theirs)

    outs = {name: [r.reshape(weights[name].shape) for r in res] for name, res in outs.items()}
    off = 0
    for name in SMALL:
        n = weights[name].shape[1]
        mine, other = small_sum[:, :, off:off + n], small_theirs[:, :, off:off + n]
        off += n
        if name == "lower_bounds":
            mine = _lower_bounds_bwd(soft, mine[:, 0, :])[:, None, :]
            other = _lower_bounds_bwd(soft, other[:, 0, :])[:, None, :]
        shp = (depth, 1, n)
        res = _adamw(weights[name].reshape(shp), mom_m[name].reshape(shp), mom_v[name].reshape(shp), mine, other)
        outs[name] = [r.reshape(weights[name].shape) for r in res]
    assert off == small_sum.shape[-1]

    total_loss = lax.psum(loss[0, 0], ("x", "y", "c"))
    result = [total_loss, dh.reshape(bl, seq, d)]
    for k in range(4):
        result += [outs[name][k] for name in WEIGHT_ORDER]
    return tuple(result)
```

```python
import functools

import jax
import jax.numpy as jnp
from jax import lax
from jax.experimental import pallas as pl
from jax.experimental.pallas import tpu as pltpu

F32 = jnp.float32
BF16 = jnp.bfloat16

HG_HEADS = 4
HG_F = 128
HG_CHUNK = 32
MEM_HEADS = 4
MEM_HEAD_DIM = 128
BRANCH_WIDTH = 512
N_BRANCH = 3
CONV_K = 3
LN_EPS = 1e-5
RMS_EPS = 1e-6
ADAM_LR = 0.001
ADAM_B1 = 0.9
ADAM_B2 = 0.999
ADAM_EPS = 1e-08
ADAM_WD = 0.01
ADAM_STEP = 10

VMEM_LIMIT = 48 * 1024 * 1024


def _cparams(sem):
    return pltpu.CompilerParams(dimension_semantics=sem, vmem_limit_bytes=VMEM_LIMIT)


def _dot(a, b, dims):
    return lax.dot_general(a, b, (dims, ((), ())), preferred_element_type=F32)


NN = ((1,), (0,))
NT = ((1,), (1,))
TN = ((0,), (0,))


def _pick(n, pref):
    for t in pref:
        if n % t == 0:
            return t
    return n


ANY_SPEC = pl.BlockSpec(memory_space=pl.ANY)


def _matmul(name, a, b, *, mode, out_dtype=F32, a_fn=None, a_extra=(), epi_fn=None, epi_extra=(), n_out=1, out_kinds=None,
            tm=512, tn=1024, tk=1024, deps=()):
    M, K = a.shape
    N = b.shape[1] if mode == "nn" else b.shape[0]
    tm, tn, tk = _pick(M, (tm, 256, 128, 8)), _pick(N, (tn, 896, 512, 256, 128)), _pick(K, (tk, 512, 256, 128))
    nk = K // tk
    n_ax, n_ex = len(a_extra), len(epi_extra)
    n_in = 2 + n_ax + n_ex + len(deps)
    out_dtypes = out_dtype if isinstance(out_dtype, (tuple, list)) else (out_dtype,) * n_out
    out_kinds = out_kinds or ("tile",) * n_out

    def body(*refs):
        a_ref, b_ref = refs[0], refs[1]
        ax_refs = refs[2:2 + n_ax]
        ex_refs = refs[2 + n_ax:2 + n_ax + n_ex]
        o_refs = refs[n_in:n_in + n_out]
        at = a_ref[...]
        at = a_fn(at, *[r[...] for r in ax_refs]) if a_fn is not None else at.astype(BF16)
        part = _dot(at, b_ref[...].astype(BF16), NN if mode == "nn" else NT)

        def finish(acc):
            outs = epi_fn(acc, *[r[...] for r in ex_refs]) if epi_fn is not None else (acc,)
            for o_ref, o, kind in zip(o_refs, outs, out_kinds):
                if kind == "rowsum":
                    @pl.when(pl.program_id(1) == 0)
                    def _(o_ref=o_ref):
                        o_ref[...] = jnp.zeros_like(o_ref)

                    o_ref[0:1, :] += o
                else:
                    o_ref[...] = o.astype(o_ref.dtype)

        if nk == 1:
            finish(part)
            return
        acc_ref = refs[-1]
        k = pl.program_id(2)

        @pl.when(k == 0)
        def _():
            acc_ref[...] = part

        @pl.when(jnp.logical_and(k > 0, k < nk - 1))
        def _():
            acc_ref[...] += part

        @pl.when(k == nk - 1)
        def _():
            finish(acc_ref[...] + part)

    b_mode = dict(pipeline_mode=pl.Buffered(1)) if (nk == 1 and N == tn) else {}
    in_specs = [pl.BlockSpec((tm, tk), lambda j, i, k: (i, k)),
                pl.BlockSpec((tk, tn), lambda j, i, k: (k, j), **b_mode) if mode == "nn"
                else pl.BlockSpec((tn, tk), lambda j, i, k: (j, k), **b_mode)]
    in_specs += [pl.BlockSpec((1, tk), lambda j, i, k: (0, k)) for _ in a_extra]
    for e in epi_extra:
        if e.shape[0] == 1:
            in_specs.append(pl.BlockSpec((1, tn), lambda j, i, k: (0, j)))
        elif e.shape[1] == 1:
            in_specs.append(pl.BlockSpec((tm, 1), lambda j, i, k: (i, 0)))
        else:
            in_specs.append(pl.BlockSpec((tm, tn), lambda j, i, k: (i, j)))
    in_specs += [ANY_SPEC] * len(deps)
    out_specs, out_shapes = [], []
    for kind, dt in zip(out_kinds, out_dtypes):
        if kind == "col":
            out_specs.append(pl.BlockSpec((tm, 1), lambda j, i, k: (i, 0)))
            out_shapes.append(jax.ShapeDtypeStruct((M, 1), dt))
        elif kind == "rowsum":
            out_specs.append(pl.BlockSpec((8, tn), lambda j, i, k: (0, j)))
            out_shapes.append(jax.ShapeDtypeStruct((8, N), dt))
        else:
            out_specs.append(pl.BlockSpec((tm, tn), lambda j, i, k: (i, j)))
            out_shapes.append(jax.ShapeDtypeStruct((M, N), dt))
    out = pl.pallas_call(
        body,
        name=name,
        grid=(N // tn, M // tm, nk),
        in_specs=in_specs,
        out_specs=out_specs,
        out_shape=out_shapes,
        scratch_shapes=[pltpu.VMEM((tm, tn), F32)] if nk > 1 else [],
        compiler_params=_cparams(("arbitrary", "arbitrary", "arbitrary")),
    )(a, b, *a_extra, *epi_extra, *deps)
    return out[0] if n_out == 1 else out


def _matmul_tn(name, a, b, *, a_fn=None, a_extra=(), a_cols=None, b_cols=None, ta=1024, tb=1024, tt=1024, out_dtype=F32, deps=(),
               paired=False):
    T = a.shape[0]
    a0, Ka = a_cols if a_cols is not None else (0, a.shape[1])
    b0, Nb = b_cols if b_cols is not None else (0, b.shape[1])
    ta, tb, tt = _pick(Ka, (ta, 512, 256, 128)), _pick(Nb, (tb, 896, 512, 256, 128)), _pick(T, (tt, 512, 256, 128))
    assert a0 % ta == 0 and b0 % tb == 0
    assert not paired or Ka // ta == Nb // tb
    a0, b0 = a0 // ta, b0 // tb
    nt = T // tt
    n_ax = len(a_extra)

    def body(*refs):
        a_ref, b_ref = refs[0], refs[1]
        ax_refs = refs[2:2 + n_ax]
        o_ref = refs[2 + n_ax + len(deps)]
        acc_ref = refs[-1]
        t = pl.program_id(2)
        at = a_ref[...]
        at = a_fn(at, *[r[...] for r in ax_refs]) if a_fn is not None else at.astype(BF16)
        part = _dot(at, b_ref[...].astype(BF16), TN)

        @pl.when(t == 0)
        def _():
            acc_ref[...] = part

        @pl.when(jnp.logical_and(t > 0, t < nt - 1))
        def _():
            acc_ref[...] += part

        @pl.when(t == nt - 1)
        def _():
            o_ref[...] = (acc_ref[...] + part if nt > 1 else part).astype(o_ref.dtype)

    b_map = (lambda i, j, t: (t, b0 + i)) if paired else (lambda i, j, t: (t, b0 + j))
    in_specs = [pl.BlockSpec((tt, ta), lambda i, j, t: (t, a0 + i)), pl.BlockSpec((tt, tb), b_map)]
    in_specs += [pl.BlockSpec((1, ta), lambda i, j, t: (0, a0 + i)) for _ in a_extra]
    in_specs += [ANY_SPEC] * len(deps)
    return pl.pallas_call(
        body,
        name=name,
        grid=(Ka // ta, 1 if paired else Nb // tb, nt),
        in_specs=in_specs,
        out_specs=pl.BlockSpec((ta, tb), lambda i, j, t: (i, j)),
        out_shape=jax.ShapeDtypeStruct((Ka, tb if paired else Nb), out_dtype),
        scratch_shapes=[pltpu.VMEM((ta, tb), F32)],
        compiler_params=_cparams(("parallel", "parallel", "arbitrary")),
    )(a, b, *a_extra, *deps)


W = BRANCH_WIDTH
C_CB, C_CC, C_CH, C_HQ, C_HF, C_HI, C_HG, C_MQ, N_MIX = 0, W, 2 * W, 3 * W, 4 * W, 5 * W, 6 * W, 7 * W, 8 * W
TS_MIX = 256
PREV_ROWS = 16
KEEP_NAMES = ("sq", "qs", "k", "sig", "f", "ea", "eb", "eq", "ek")


def _sigmoid(x):
    return jax.nn.sigmoid(x)


def _chunk_pos(shape):
    return lax.broadcasted_iota(jnp.int32, shape, 0) & (HG_CHUNK - 1)


def _seg_cumsum(x, pos):
    sh = 1
    while sh < HG_CHUNK:
        x = x + jnp.where(pos >= sh, pltpu.roll(x, sh, 0), 0.0)
        sh *= 2
    return x


def _seg_rev_cumsum(x, pos):
    n = x.shape[0]
    sh = 1
    while sh < HG_CHUNK:
        x = x + jnp.where(pos < HG_CHUNK - sh, pltpu.roll(x, n - sh, 0), 0.0)
        sh *= 2
    return x


def _chunk_mask(ts):
    r = lax.broadcasted_iota(jnp.int32, (ts, ts), 0)
    c = lax.broadcasted_iota(jnp.int32, (ts, ts), 1)
    return jnp.logical_and((r // HG_CHUNK) == (c // HG_CHUNK), c <= r)


def _hgrn_gates(p_ref, lb):
    q = p_ref[:, C_HQ:C_HQ + W].astype(F32)
    fl = p_ref[:, C_HF:C_HF + W].astype(F32)
    sig = _sigmoid(fl)
    f = lb + (1.0 - lb) * sig
    logf = jnp.log(f)
    k = (1.0 - lb) * _sigmoid(-fl)
    sq = _sigmoid(q)
    qs = q * sq
    return q, sq, qs, sig, f, logf, k


def _hgrn_decays(logf, bc_sc, ts):
    pos = _chunk_pos(logf.shape)
    bc = _seg_cumsum(logf, pos)
    bc_sc[...] = bc
    nc = ts // HG_CHUNK
    bref = jnp.concatenate(
        [jnp.broadcast_to(bc_sc[n * HG_CHUNK + HG_CHUNK // 2 - 1:n * HG_CHUNK + HG_CHUNK // 2, :], (HG_CHUNK, W)) for n in range(nc)], axis=0)
    blast = jnp.concatenate(
        [jnp.broadcast_to(bc_sc[(n + 1) * HG_CHUNK - 1:(n + 1) * HG_CHUNK, :], (HG_CHUNK, W)) for n in range(nc)], axis=0)
    return pos, bc, bref, blast


def _conv_shift_down(u, carry_ref, row):
    n = carry_ref.shape[0]
    last, before = carry_ref[n - 1:n, :], carry_ref[n - 2:n - 1, :]
    u1 = jnp.where(row == 0, last, pltpu.roll(u, 1, 0))
    u2 = jnp.where(row == 0, before, jnp.where(row == 1, last, pltpu.roll(u, 2, 0)))
    return u1, u2


def _attn_probs(qh, kh):
    s = _dot(qh, kh, NT) * (MEM_HEAD_DIM ** -0.5)
    e = jnp.exp(s - jnp.max(s, axis=-1, keepdims=True))
    return e / jnp.sum(e, axis=-1, keepdims=True)


def _mixer_fwd(p, mk, mv, lb, conv_w, norm_w, *, bl, seq):
    T = p.shape[0]
    ts = TS_MIX
    ns = seq // ts
    nc = ts // HG_CHUNK
    ml = mk.shape[0] // bl

    def body(p_ref, mk_ref, mv_ref, lb_ref, cw_ref, nw_ref, y_ref, st_ref, opre_ref, state_sc, carry_sc, bc_sc):
        @pl.when(pl.program_id(1) == 0)
        def _():
            state_sc[...] = jnp.zeros_like(state_sc)
            carry_sc[...] = jnp.zeros_like(carry_sc)

        cb, cc, ch = (p_ref[:, c0:c0 + W].astype(F32) for c0 in (C_CB, C_CC, C_CH))
        u = cc * ch
        row = lax.broadcasted_iota(jnp.int32, (ts, W), 0)
        u1, u2 = _conv_shift_down(u, carry_sc, row)
        yconv = u2 * cw_ref[0:1, :] + u1 * cw_ref[1:2, :] + u * cw_ref[2:3, :]
        y_ref[:, 0:W] = (cb * yconv).astype(BF16)
        carry_sc[...] = u[ts - 8:ts, :]

        lbv = lb_ref[...]
        _, _, qs, _, _, logf, k = _hgrn_gates(p_ref, lbv)
        pos, bc, bref, blast = _hgrn_decays(logf, bc_sc, ts)
        a_all = (qs * jnp.exp(bc - bref)).astype(BF16)
        bk_all = (k * jnp.exp(bref - bc)).astype(BF16)
        qin_all = (qs * jnp.exp(bc)).astype(BF16)
        kout_all = (k * jnp.exp(blast - bc)).astype(BF16)
        v_all = p_ref[:, C_HI:C_HI + W].astype(BF16)
        mask = _chunk_mask(ts)
        heads = [slice(h * HG_F, (h + 1) * HG_F) for h in range(HG_HEADS)]
        st = [state_sc[h] for h in range(HG_HEADS)]
        o_inter = [[] for _ in range(HG_HEADS)]
        for n in range(nc):
            rows = slice(n * HG_CHUNK, (n + 1) * HG_CHUNK)
            for h, hs in enumerate(heads):
                st_ref[n, h] = st[h]
                o_inter[h].append(_dot(qin_all[rows, hs], st[h].astype(BF16), NT))
                kv = _dot(v_all[rows, hs], kout_all[rows, hs], TN)
                decay = jnp.exp(bc_sc[(n + 1) * HG_CHUNK - 1:(n + 1) * HG_CHUNK, hs])
                st[h] = st[h] * decay + kv
        for h in range(HG_HEADS):
            state_sc[h] = st[h]
        scores = [_dot(a_all[:, hs], bk_all[:, hs], NT) for hs in heads]
        scores = [jnp.where(mask, s, 0.0).astype(BF16) for s in scores]
        outs = [_dot(scores[h], v_all[:, hs], NN) + jnp.concatenate(o_inter[h], axis=0) for h, hs in enumerate(heads)]
        for h, hs in enumerate(heads):
            o = outs[h]
            opre_ref[:, hs] = o
            on = o * lax.rsqrt(jnp.mean(o * o, axis=-1, keepdims=True) + RMS_EPS) * nw_ref[...]
            g = p_ref[:, C_HG + h * HG_F:C_HG + (h + 1) * HG_F].astype(F32)
            y_ref[:, W + h * HG_F:W + (h + 1) * HG_F] = (on * (g * _sigmoid(g))).astype(BF16)

        mheads = [slice(h * MEM_HEAD_DIM, (h + 1) * MEM_HEAD_DIM) for h in range(MEM_HEADS)]
        probs = [_attn_probs(p_ref[:, C_MQ + h * MEM_HEAD_DIM:C_MQ + (h + 1) * MEM_HEAD_DIM].astype(BF16), mk_ref[:, hs])
                 for h, hs in enumerate(mheads)]
        for h, hs in enumerate(mheads):
            y_ref[:, 2 * W + h * MEM_HEAD_DIM:2 * W + (h + 1) * MEM_HEAD_DIM] = _dot(
                probs[h].astype(BF16), mv_ref[:, hs], NN).astype(BF16)

    return pl.pallas_call(
        body,
        name="mixer_fwd",
        grid=(bl, ns),
        in_specs=[
            pl.BlockSpec((ts, N_MIX), lambda b, s: (b * ns + s, 0)),
            pl.BlockSpec((ml, W), lambda b, s: (b, 0)),
            pl.BlockSpec((ml, W), lambda b, s: (b, 0)),
            pl.BlockSpec((1, W), lambda b, s: (0, 0)),
            pl.BlockSpec((CONV_K, W), lambda b, s: (0, 0)),
            pl.BlockSpec((1, HG_F), lambda b, s: (0, 0)),
        ],
        out_specs=[
            pl.BlockSpec((ts, 3 * W), lambda b, s: (b * ns + s, 0)),
            pl.BlockSpec((nc, HG_HEADS, HG_F, HG_F), lambda b, s: (b * ns + s, 0, 0, 0)),
            pl.BlockSpec((ts, W), lambda b, s: (b * ns + s, 0)),
        ],
        out_shape=[
            jax.ShapeDtypeStruct((T, 3 * W), BF16),
            jax.ShapeDtypeStruct((T // HG_CHUNK, HG_HEADS, HG_F, HG_F), F32),
            jax.ShapeDtypeStruct((T, W), F32),
        ],
        scratch_shapes=[pltpu.VMEM((HG_HEADS, HG_F, HG_F), F32), pltpu.VMEM((8, W), F32), pltpu.VMEM((ts, W), F32)],
        compiler_params=_cparams(("arbitrary", "arbitrary")),
    )(p, mk, mv, lb, conv_w, norm_w)


def _mixer_bwd(p, dy, dp_gates, st, opre, mk, mv, lb, conv_w, norm_w, *, bl, seq, deps=()):
    T, nin = p.shape
    ts = TS_MIX
    ns = seq // ts
    nc = ts // HG_CHUNK
    ml = mk.shape[0] // bl
    mid, last = HG_CHUNK // 2 - 1, HG_CHUNK - 1

    def body(p_ref, pprev_ref, dy_ref, dpin_ref, st_ref, opre_ref, mk_ref, mv_ref, lb_ref, cw_ref, nw_ref, *rest):
        (dp_ref, dmk_ref, dmv_ref, dcw_ref, dnw_ref, dlb_ref, dstate_sc, carry_sc, uprev_sc, ab_sc, bkb_sc, qinb_sc, koutb_sc,
         dob_sc, dv_sc, da_sc, dbk_sc, dqin_sc, dkout_sc, dec_sc, ddec_sc, *keep_scs) = rest[len(deps):]
        del dpin_ref
        b, s = pl.program_id(0), pl.program_id(1)

        @pl.when(s == 0)
        def _():
            dstate_sc[...] = jnp.zeros_like(dstate_sc)
            carry_sc[...] = jnp.zeros_like(carry_sc)
            dmk_ref[...] = jnp.zeros_like(dmk_ref)
            dmv_ref[...] = jnp.zeros_like(dmv_ref)

        @pl.when(jnp.logical_and(b == 0, s == 0))
        def _():
            dcw_ref[...] = jnp.zeros_like(dcw_ref)
            dnw_ref[...] = jnp.zeros_like(dnw_ref)
            dlb_ref[...] = jnp.zeros_like(dlb_ref)

        cb, cc, ch = (p_ref[:, c0:c0 + W].astype(F32) for c0 in (C_CB, C_CC, C_CH))
        u = cc * ch
        row = lax.broadcasted_iota(jnp.int32, (ts, W), 0)
        uprev = pprev_ref[:, C_CC:C_CC + W].astype(F32) * pprev_ref[:, C_CH:C_CH + W].astype(F32)
        uprev_sc[...] = jnp.where(s == ns - 1, 0.0, uprev)
        u1, u2 = _conv_shift_down(u, uprev_sc, row)
        w0, w1, w2 = cw_ref[0:1, :], cw_ref[1:2, :], cw_ref[2:3, :]
        dya = dy_ref[:, 0:W].astype(F32)
        dp_ref[:, C_CB:C_CB + W] = (dya * (u2 * w0 + u1 * w1 + u * w2)).astype(BF16)
        dv = cb * dya
        dv1 = jnp.where(row == ts - 1, carry_sc[0:1, :], pltpu.roll(dv, ts - 1, 0))
        dv2 = jnp.where(row == ts - 1, carry_sc[1:2, :], jnp.where(row == ts - 2, carry_sc[0:1, :], pltpu.roll(dv, ts - 2, 0)))
        du = dv * w2 + dv1 * w1 + dv2 * w0
        dp_ref[:, C_CC:C_CC + W] = (du * ch).astype(BF16)
        dp_ref[:, C_CH:C_CH + W] = (du * cc).astype(BF16)
        dcw_ref[0:1, :] += jnp.sum(dv * u2, axis=0, keepdims=True)
        dcw_ref[1:2, :] += jnp.sum(dv * u1, axis=0, keepdims=True)
        dcw_ref[2:3, :] += jnp.sum(dv * u, axis=0, keepdims=True)
        carry_sc[...] = dv[0:8, :]

        mask = _chunk_mask(ts)
        pos_c = _chunk_pos((HG_CHUNK, HG_F))
        nw = nw_ref[...]

        def block(n, h):
            rows = slice(n * HG_CHUNK, (n + 1) * HG_CHUNK)
            return rows, slice(h * HG_F, (h + 1) * HG_F)

        keep = dict(zip(KEEP_NAMES, keep_scs))

        def gates(rows, h):
            lbh = lb_ref[:, h * HG_F:(h + 1) * HG_F]
            q = p_ref[rows, C_HQ + h * HG_F:C_HQ + (h + 1) * HG_F].astype(F32)
            fl = p_ref[rows, C_HF + h * HG_F:C_HF + (h + 1) * HG_F].astype(F32)
            sig = _sigmoid(fl)
            f = lbh + (1.0 - lbh) * sig
            k = (1.0 - lbh) * _sigmoid(-fl)
            sq = _sigmoid(q)
            qs = q * sq
            bc = _seg_cumsum(jnp.log(f), pos_c)
            bref = jnp.sum(jnp.where(pos_c == mid, bc, 0.0), axis=0, keepdims=True)
            blast = jnp.sum(jnp.where(pos_c == last, bc, 0.0), axis=0, keepdims=True)
            ea, eb, eq, ek = jnp.exp(bc - bref), jnp.exp(bref - bc), jnp.exp(bc), jnp.exp(blast - bc)
            return dict(sq=sq, qs=qs, k=k, sig=sig, f=f, ea=ea, eb=eb, eq=eq, ek=ek), blast

        dnw = jnp.zeros((1, HG_F), F32)
        for n in range(nc):
            for h in range(HG_HEADS):
                rows, hs = block(n, h)
                fw, blast = gates(rows, h)
                for name in KEEP_NAMES:
                    keep[name][rows, hs] = fw[name]
                ab_sc[rows, hs] = (fw["qs"] * fw["ea"]).astype(BF16)
                bkb_sc[rows, hs] = (fw["k"] * fw["eb"]).astype(BF16)
                qinb_sc[rows, hs] = (fw["qs"] * fw["eq"]).astype(BF16)
                koutb_sc[rows, hs] = (fw["k"] * fw["ek"]).astype(BF16)
                dec_sc[n:n + 1, hs] = jnp.exp(blast)
                o = opre_ref[rows, hs]
                g = p_ref[rows, C_HG + h * HG_F:C_HG + (h + 1) * HG_F].astype(F32)
                sg = _sigmoid(g)
                r = lax.rsqrt(jnp.mean(o * o, axis=-1, keepdims=True) + RMS_EPS)
                dyb = dy_ref[rows, W + h * HG_F:W + (h + 1) * HG_F].astype(F32)
                dp_ref[rows, C_HG + h * HG_F:C_HG + (h + 1) * HG_F] = (
                    dyb * (o * r * nw) * (sg * (1.0 + g * (1.0 - sg)))).astype(BF16)
                don = dyb * (g * sg)
                dnw = dnw + jnp.sum(don * o * r, axis=0, keepdims=True)
                dn = don * nw
                dob_sc[rows, hs] = (r * (dn - o * (r * r) * jnp.mean(dn * o, axis=-1, keepdims=True))).astype(BF16)
        dnw_ref[0:1, :] += dnw

        heads = [slice(h * HG_F, (h + 1) * HG_F) for h in range(HG_HEADS)]
        scores = [_dot(ab_sc[:, hs], bkb_sc[:, hs], NT) for hs in heads]
        dscores = [_dot(dob_sc[:, hs], p_ref[:, C_HI + h * HG_F:C_HI + (h + 1) * HG_F].astype(BF16), NT)
                   for h, hs in enumerate(heads)]
        scores = [jnp.where(mask, s, 0.0).astype(BF16) for s in scores]
        dscores = [jnp.where(mask, s, 0.0).astype(BF16) for s in dscores]
        for h, hs in enumerate(heads):
            dv_sc[:, hs] = _dot(scores[h], dob_sc[:, hs], TN)
            da_sc[:, hs] = _dot(dscores[h], bkb_sc[:, hs], NN)
            dbk_sc[:, hs] = _dot(dscores[h], ab_sc[:, hs], TN)
        dst = [dstate_sc[h] for h in range(HG_HEADS)]
        for n in reversed(range(nc)):
            for h in range(HG_HEADS):
                rows, hs = block(n, h)
                st_n = st_ref[n, h]
                decay = dec_sc[n:n + 1, hs]
                dstb = dst[h].astype(BF16)
                dob_n = dob_sc[rows, hs]
                dv_sc[rows, hs] += _dot(koutb_sc[rows, hs], dstb, NT)
                dkout_sc[rows, hs] = _dot(p_ref[rows, C_HI + h * HG_F:C_HI + (h + 1) * HG_F].astype(BF16), dstb, NN)
                ddec_sc[n:n + 1, hs] = jnp.sum(dst[h] * st_n, axis=0, keepdims=True) * decay
                dqin_sc[rows, hs] = _dot(dob_n, st_n.astype(BF16), NN)
                dst[h] = dst[h] * decay + _dot(dob_n, qinb_sc[rows, hs], TN)
        for h in range(HG_HEADS):
            dstate_sc[h] = dst[h]

        for h in range(HG_HEADS):
            dlb = jnp.zeros((1, HG_F), F32)
            for n in range(nc):
                rows, hs = block(n, h)
                fw = {name: keep[name][rows, hs] for name in KEEP_NAMES}
                lbh = lb_ref[:, h * HG_F:(h + 1) * HG_F]
                q = p_ref[rows, C_HQ + h * HG_F:C_HQ + (h + 1) * HG_F].astype(F32)
                da, dbk, dqin, dkout = da_sc[rows, hs], dbk_sc[rows, hs], dqin_sc[rows, hs], dkout_sc[rows, hs]
                w_a, w_b, w_q, w_k = da * fw["ea"], dbk * fw["eb"], dqin * fw["eq"], dkout * fw["ek"]
                dqs, dk = w_a + w_q, w_b + w_k
                t_a, t_b, t_q, t_k = w_a * fw["qs"], w_b * fw["k"], w_q * fw["qs"], w_k * fw["k"]
                s_ref = jnp.sum(t_b - t_a, axis=0, keepdims=True)
                s_last = jnp.sum(t_k, axis=0, keepdims=True) + ddec_sc[n:n + 1, hs]
                dbc = (t_a - t_b + t_q - t_k) + jnp.where(pos_c == mid, s_ref, 0.0) + jnp.where(pos_c == last, s_last, 0.0)
                dfk = _seg_rev_cumsum(dbc, pos_c) / fw["f"] - dk
                sig, sq = fw["sig"], fw["sq"]
                dp_ref[rows, C_HF + h * HG_F:C_HF + (h + 1) * HG_F] = (dfk * (1.0 - lbh) * sig * (1.0 - sig)).astype(BF16)
                dlb = dlb + jnp.sum(dfk * (1.0 - sig), axis=0, keepdims=True)
                dp_ref[rows, C_HQ + h * HG_F:C_HQ + (h + 1) * HG_F] = (dqs * (sq * (1.0 + q * (1.0 - sq)))).astype(BF16)
                dp_ref[rows, C_HI + h * HG_F:C_HI + (h + 1) * HG_F] = dv_sc[rows, hs].astype(BF16)
            dlb_ref[0:1, h * HG_F:(h + 1) * HG_F] += dlb

        mheads = [slice(h * MEM_HEAD_DIM, (h + 1) * MEM_HEAD_DIM) for h in range(MEM_HEADS)]
        qhs = [p_ref[:, C_MQ + h * MEM_HEAD_DIM:C_MQ + (h + 1) * MEM_HEAD_DIM].astype(BF16) for h in range(MEM_HEADS)]
        dobs = [dy_ref[:, 2 * W + h * MEM_HEAD_DIM:2 * W + (h + 1) * MEM_HEAD_DIM].astype(BF16) for h in range(MEM_HEADS)]
        probs = [_attn_probs(qhs[h], mk_ref[:, hs]) for h, hs in enumerate(mheads)]
        dprobs = [_dot(dobs[h], mv_ref[:, hs], NT) for h, hs in enumerate(mheads)]
        for h, hs in enumerate(mheads):
            prob = probs[h]
            dmv_ref[:, hs] += _dot(prob.astype(BF16), dobs[h], TN)
            ds = prob * (dprobs[h] - jnp.sum(dprobs[h] * prob, axis=-1, keepdims=True)) * (MEM_HEAD_DIM ** -0.5)
            dsb = ds.astype(BF16)
            dp_ref[:, C_MQ + h * MEM_HEAD_DIM:C_MQ + (h + 1) * MEM_HEAD_DIM] = _dot(dsb, mk_ref[:, hs], NN).astype(BF16)
            dmk_ref[:, hs] += _dot(dsb, qhs[h], TN)

    def tile(b, s):
        return b * ns + (ns - 1 - s)

    return pl.pallas_call(
        body,
        name="mixer_bwd",
        grid=(bl, ns),
        in_specs=[
            pl.BlockSpec((ts, N_MIX), lambda b, s: (tile(b, s), 0)),
            pl.BlockSpec((PREV_ROWS, N_MIX), lambda b, s: (jnp.maximum(tile(b, s) * (ts // PREV_ROWS) - 1, 0), 0)),
            pl.BlockSpec((ts, 3 * W), lambda b, s: (tile(b, s), 0)),
            pl.BlockSpec(memory_space=pl.ANY),
            pl.BlockSpec((nc, HG_HEADS, HG_F, HG_F), lambda b, s: (tile(b, s), 0, 0, 0)),
            pl.BlockSpec((ts, W), lambda b, s: (tile(b, s), 0)),
            pl.BlockSpec((ml, W), lambda b, s: (b, 0)),
            pl.BlockSpec((ml, W), lambda b, s: (b, 0)),
            pl.BlockSpec((1, W), lambda b, s: (0, 0)),
            pl.BlockSpec((CONV_K, W), lambda b, s: (0, 0)),
            pl.BlockSpec((1, HG_F), lambda b, s: (0, 0)),
        ] + [ANY_SPEC] * len(deps),
        out_specs=[
            pl.BlockSpec((ts, N_MIX), lambda b, s: (tile(b, s), 0)),
            pl.BlockSpec((ml, W), lambda b, s: (b, 0)),
            pl.BlockSpec((ml, W), lambda b, s: (b, 0)),
            pl.BlockSpec((8, W), lambda b, s: (0, 0)),
            pl.BlockSpec((8, HG_F), lambda b, s: (0, 0)),
            pl.BlockSpec((8, W), lambda b, s: (0, 0)),
        ],
        out_shape=[
            jax.ShapeDtypeStruct((T, nin), BF16),
            jax.ShapeDtypeStruct((bl * ml, W), F32),
            jax.ShapeDtypeStruct((bl * ml, W), F32),
            jax.ShapeDtypeStruct((8, W), F32),
            jax.ShapeDtypeStruct((8, HG_F), F32),
            jax.ShapeDtypeStruct((8, W), F32),
        ],
        input_output_aliases={3: 0},
        scratch_shapes=[pltpu.VMEM((HG_HEADS, HG_F, HG_F), F32), pltpu.VMEM((8, W), F32), pltpu.VMEM((PREV_ROWS, W), F32)]
        + [pltpu.VMEM((ts, W), BF16)] * 5 + [pltpu.VMEM((ts, W), F32)] * 5 + [pltpu.VMEM((nc, W), F32)] * 2
        + [pltpu.VMEM((ts, W), F32)] * len(KEEP_NAMES),
        compiler_params=_cparams(("arbitrary", "arbitrary")),
    )(p, p, dy, dp_gates, st, opre, mk, mv, lb, conv_w, norm_w, *deps)


def _layer_norm_stats(z):
    mu = jnp.mean(z, axis=-1, keepdims=True)
    zc = z - mu
    rstd = lax.rsqrt(jnp.mean(zc * zc, axis=-1, keepdims=True) + LN_EPS)
    return zc * rstd, rstd


def _gate_specs(tm, d):
    g0 = N_MIX // d
    return [pl.BlockSpec((tm, d), functools.partial(lambda i, k: (i, g0 + k), k=k)) for k in range(N_BRANCH)]


def _merge_fwd(y, p, x0, wb, wo, bg, ln_g, ln_b, *, alpha, tm=512):
    T, d = x0.shape
    assert N_MIX % d == 0
    tm = _pick(T, (tm, 128, 8))

    def body(y_ref, g0_ref, g1_ref, g2_ref, x_ref, wb_ref, wo_ref, bg_ref, lg_ref, lb_ref, mg_ref, xh_ref, rs_ref, x1b_ref):
        merged = None
        for i, g_ref in enumerate((g0_ref, g1_ref, g2_ref)):
            r = _dot(y_ref[:, i * W:(i + 1) * W], wb_ref[i * W:(i + 1) * W, :], NN)
            t = _sigmoid(g_ref[...].astype(F32) + bg_ref[:, i * d:(i + 1) * d]) * r
            merged = t if merged is None else merged + t
        mb = merged.astype(BF16)
        mg_ref[...] = mb
        z = alpha * x_ref[...] + _dot(mb, wo_ref[...], NN)
        xh, rs = _layer_norm_stats(z)
        xh_ref[...], rs_ref[...] = xh, rs
        x1b_ref[...] = (xh * lg_ref[...] + lb_ref[...]).astype(BF16)

    row = lambda i: (i, 0)
    fix = lambda i: (0, 0)
    return pl.pallas_call(
        body,
        name="merge_fwd",
        grid=(T // tm,),
        in_specs=[pl.BlockSpec((tm, 3 * W), row)] + _gate_specs(tm, d) + [
            pl.BlockSpec((tm, d), row), pl.BlockSpec((3 * W, d), fix, pipeline_mode=pl.Buffered(1)),
            pl.BlockSpec((d, d), fix, pipeline_mode=pl.Buffered(1)), pl.BlockSpec((1, 3 * d), fix),
            pl.BlockSpec((1, d), fix), pl.BlockSpec((1, d), fix)],
        out_specs=[pl.BlockSpec((tm, d), row), pl.BlockSpec((tm, d), row), pl.BlockSpec((tm, 1), row), pl.BlockSpec((tm, d), row)],
        out_shape=[jax.ShapeDtypeStruct((T, d), BF16), jax.ShapeDtypeStruct((T, d), F32), jax.ShapeDtypeStruct((T, 1), F32),
                   jax.ShapeDtypeStruct((T, d), BF16)],
        compiler_params=_cparams(("parallel",)),
    )(y, p, p, p, x0, wb, wo, bg, ln_g, ln_b)


def _merge_bwd(dz, p, y, wb, wo, bg, *, tm=512):
    T, d = dz.shape
    nin = p.shape[1]
    tm = _pick(T, (tm, 128, 8))

    def body(dz_ref, g0_ref, g1_ref, g2_ref, y_ref, wb_ref, wo_ref, bg_ref, dr_ref, dp_ref, dy_ref, dbg_ref):
        @pl.when(pl.program_id(0) == 0)
        def _():
            dbg_ref[...] = jnp.zeros_like(dbg_ref)

        dmerged = _dot(dz_ref[...].astype(BF16), wo_ref[...], NT)
        dp_ref[:, 0:N_MIX] = jnp.zeros((tm, N_MIX), BF16)
        for i, g_ref in enumerate((g0_ref, g1_ref, g2_ref)):
            cs = slice(i * d, (i + 1) * d)
            s = _sigmoid(g_ref[...].astype(F32) + bg_ref[:, cs])
            drb = (dmerged * s).astype(BF16)
            dr_ref[:, cs] = drb
            dgate = dmerged * _dot(y_ref[:, i * W:(i + 1) * W], wb_ref[i * W:(i + 1) * W, :], NN) * s * (1.0 - s)
            dp_ref[:, N_MIX + i * d:N_MIX + (i + 1) * d] = dgate.astype(BF16)
            dbg_ref[0:1, cs] += jnp.sum(dgate, axis=0, keepdims=True)
            dy_ref[:, i * W:(i + 1) * W] = _dot(drb, wb_ref[i * W:(i + 1) * W, :], NT).astype(BF16)

    row = lambda i: (i, 0)
    fix = lambda i: (0, 0)
    return pl.pallas_call(
        body,
        name="merge_bwd",
        grid=(T // tm,),
        in_specs=[pl.BlockSpec((tm, d), row)] + _gate_specs(tm, d) + [
            pl.BlockSpec((tm, 3 * W), row), pl.BlockSpec((3 * W, d), fix, pipeline_mode=pl.Buffered(1)),
            pl.BlockSpec((d, d), fix, pipeline_mode=pl.Buffered(1)), pl.BlockSpec((1, 3 * d), fix)],
        out_specs=[pl.BlockSpec((tm, 3 * d), row), pl.BlockSpec((tm, nin), row), pl.BlockSpec((tm, 3 * W), row),
                   pl.BlockSpec((8, 3 * d), fix)],
        out_shape=[jax.ShapeDtypeStruct((T, 3 * d), BF16), jax.ShapeDtypeStruct((T, nin), BF16),
                   jax.ShapeDtypeStruct((T, 3 * W), BF16), jax.ShapeDtypeStruct((8, 3 * d), F32)],
        compiler_params=_cparams(("arbitrary",)),
    )(dz, p, p, p, y, wb, wo, bg)


MLP_VMEM_LIMIT = 58 * 1024 * 1024


def _mlp_fwd(xhat1, x1b, g1, b1, wu, wd, g2, b2, *, alpha, tm=512, tf=1024):
    T, d = xhat1.shape
    ff = wu.shape[1]
    tm, tf = _pick(T, (tm, 256, 128, 8)), _pick(ff, (tf, 1024, 512, 256, 128))

    def body(xh_ref, x1b_ref, g1_ref, b1_ref, wu_ref, wd_ref, g2_ref, b2_ref, a_ref, xh2_ref, rs2_ref, x2_ref, x2b_ref):
        xb = x1b_ref[...]
        acc = None
        a = _dot(xb, wu_ref[:, 0:tf], NN)
        for c0 in range(0, ff, tf):
            a_next = _dot(xb, wu_ref[:, c0 + tf:c0 + 2 * tf], NN) if c0 + tf < ff else None
            a_ref[:, c0:c0 + tf] = a.astype(BF16)
            part = _dot(jnp.square(jnp.maximum(a, 0.0)).astype(BF16), wd_ref[c0:c0 + tf, :], NN)
            acc = part if acc is None else acc + part
            a = a_next
        x1 = xh_ref[...] * g1_ref[...] + b1_ref[...]
        xh2, rs2 = _layer_norm_stats(alpha * x1 + acc)
        xh2_ref[...] = xh2
        rs2_ref[...] = rs2
        x2 = xh2 * g2_ref[...] + b2_ref[...]
        x2_ref[...] = x2
        x2b_ref[...] = x2.astype(BF16)

    row = lambda i: (i, 0)
    fix = lambda i: (0, 0)
    once = dict(pipeline_mode=pl.Buffered(1))
    return pl.pallas_call(
        body,
        name="mlp_fwd",
        grid=(T // tm,),
        in_specs=[pl.BlockSpec((tm, d), row), pl.BlockSpec((tm, d), row), pl.BlockSpec((1, d), fix), pl.BlockSpec((1, d), fix),
                  pl.BlockSpec((d, ff), fix, **once), pl.BlockSpec((ff, d), fix, **once),
                  pl.BlockSpec((1, d), fix), pl.BlockSpec((1, d), fix)],
        out_specs=[pl.BlockSpec((tm, ff), row), pl.BlockSpec((tm, d), row), pl.BlockSpec((tm, 1), row),
                   pl.BlockSpec((tm, d), row), pl.BlockSpec((tm, d), row)],
        out_shape=[jax.ShapeDtypeStruct((T, ff), BF16), jax.ShapeDtypeStruct((T, d), F32), jax.ShapeDtypeStruct((T, 1), F32),
                   jax.ShapeDtypeStruct((T, d), F32), jax.ShapeDtypeStruct((T, d), BF16)],
        compiler_params=pltpu.CompilerParams(dimension_semantics=("parallel",), vmem_limit_bytes=MLP_VMEM_LIMIT),
    )(xhat1, x1b, g1, b1, wu, wd, g2, b2)


def _ln_bwd(dy, xhat, rstd, g, *, tm=1024, deps=()):
    T, d = dy.shape
    tm = _pick(T, (tm, 256, 128, 8))

    def body(dy_ref, xh_ref, rs_ref, g_ref, *rest):
        dz_ref, dzb_ref, dg_ref, db_ref = rest[len(deps):]

        @pl.when(pl.program_id(0) == 0)
        def _():
            dg_ref[...] = jnp.zeros_like(dg_ref)
            db_ref[...] = jnp.zeros_like(db_ref)

        dy_, xh = dy_ref[...], xh_ref[...]
        dg_ref[0:1, :] += jnp.sum(dy_ * xh, axis=0, keepdims=True)
        db_ref[0:1, :] += jnp.sum(dy_, axis=0, keepdims=True)
        dxh = dy_ * g_ref[...]
        dz = rs_ref[...] * (dxh - jnp.mean(dxh, axis=-1, keepdims=True) - xh * jnp.mean(dxh * xh, axis=-1, keepdims=True))
        dz_ref[...] = dz
        dzb_ref[...] = dz.astype(BF16)

    row = lambda i: (i, 0)
    fix = lambda i: (0, 0)
    return pl.pallas_call(
        body,
        name="ln_bwd",
        grid=(T // tm,),
        in_specs=[pl.BlockSpec((tm, d), row), pl.BlockSpec((tm, d), row), pl.BlockSpec((tm, 1), row), pl.BlockSpec((1, d), fix)]
        + [ANY_SPEC] * len(deps),
        out_specs=[pl.BlockSpec((tm, d), row), pl.BlockSpec((tm, d), row), pl.BlockSpec((8, d), fix), pl.BlockSpec((8, d), fix)],
        out_shape=[jax.ShapeDtypeStruct((T, d), F32), jax.ShapeDtypeStruct((T, d), BF16), jax.ShapeDtypeStruct((8, d), F32),
                   jax.ShapeDtypeStruct((8, d), F32)],
        compiler_params=_cparams(("arbitrary",)),
    )(dy, xhat, rstd, g, *deps)


def _loss_head(y, target, xhat, rstd, g, *, tm=512):
    T, d = y.shape
    tm = _pick(T, (tm, 256, 128, 8))
    n = T // tm

    def body(y_ref, t_ref, xh_ref, rs_ref, g_ref, loss_ref, dz_ref, dzb_ref, dg_ref, db_ref, acc_ref):
        i = pl.program_id(0)

        @pl.when(i == 0)
        def _():
            acc_ref[...] = jnp.zeros_like(acc_ref)
            dg_ref[...] = jnp.zeros_like(dg_ref)
            db_ref[...] = jnp.zeros_like(db_ref)

        e = y_ref[...] - t_ref[...]
        acc_ref[...] += jnp.sum(e * e, axis=0, keepdims=True)
        dy_, xh = e * (1.0 / d), xh_ref[...]
        dg_ref[0:1, :] += jnp.sum(dy_ * xh, axis=0, keepdims=True)
        db_ref[0:1, :] += jnp.sum(dy_, axis=0, keepdims=True)
        dxh = dy_ * g_ref[...]
        dz = rs_ref[...] * (dxh - jnp.mean(dxh, axis=-1, keepdims=True) - xh * jnp.mean(dxh * xh, axis=-1, keepdims=True))
        dz_ref[...] = dz
        dzb_ref[...] = dz.astype(BF16)

        @pl.when(i == n - 1)
        def _():
            loss_ref[...] = (0.5 / d) * jnp.sum(acc_ref[...], axis=1, keepdims=True)

    row = lambda i: (i, 0)
    fix = lambda i: (0, 0)
    return pl.pallas_call(
        body,
        name="loss_head",
        grid=(n,),
        in_specs=[pl.BlockSpec((tm, d), row), pl.BlockSpec((tm, d), row), pl.BlockSpec((tm, d), row), pl.BlockSpec((tm, 1), row),
                  pl.BlockSpec((1, d), fix)],
        out_specs=[pl.BlockSpec((1, 1), fix), pl.BlockSpec((tm, d), row), pl.BlockSpec((tm, d), row), pl.BlockSpec((8, d), fix),
                   pl.BlockSpec((8, d), fix)],
        out_shape=[jax.ShapeDtypeStruct((1, 1), F32), jax.ShapeDtypeStruct((T, d), F32), jax.ShapeDtypeStruct((T, d), BF16),
                   jax.ShapeDtypeStruct((8, d), F32), jax.ShapeDtypeStruct((8, d), F32)],
        scratch_shapes=[pltpu.VMEM((1, d), F32)],
        compiler_params=_cparams(("arbitrary",)),
    )(y, target, xhat, rstd, g)


def _lower_bounds_fwd(lower_bounds):
    depth, n = lower_bounds.shape

    def body(x_ref, soft_ref, lb_ref):
        x = x_ref[...]
        e = jnp.exp(x - jnp.max(x, axis=0, keepdims=True))
        soft_ref[...] = e / jnp.sum(e, axis=0, keepdims=True)
        run = None
        for l in range(depth):
            run = soft_ref[l:l + 1, :] if run is None else run + soft_ref[l:l + 1, :]
            lb_ref[l:l + 1, :] = run - soft_ref[0:1, :]

    return pl.pallas_call(body, name="lower_bounds_fwd",
                          out_shape=[jax.ShapeDtypeStruct((depth, n), F32), jax.ShapeDtypeStruct((depth, n), F32)])(lower_bounds)


def _lower_bounds_bwd(soft, dlb):
    depth, n = soft.shape

    def body(soft_ref, dlb_ref, out_ref, dsoft_ref):
        total = jnp.sum(dlb_ref[...], axis=0, keepdims=True)
        run = None
        for l in reversed(range(depth)):
            run = dlb_ref[l:l + 1, :] if run is None else run + dlb_ref[l:l + 1, :]
            dsoft_ref[l:l + 1, :] = run - total if l == 0 else run
        s, ds = soft_ref[...], dsoft_ref[...]
        out_ref[...] = s * (ds - jnp.sum(s * ds, axis=0, keepdims=True))

    return pl.pallas_call(body, name="lower_bounds_bwd", out_shape=jax.ShapeDtypeStruct((depth, n), F32),
                          scratch_shapes=[pltpu.VMEM((depth, n), F32)])(soft, dlb)


def _layer_fwd(x0, x0b, mem2, lb, w_in, mix_fn, late_fn, *, bl, seq, alpha, deps=()):
    p = _matmul("proj_in", x0b, w_in, mode="nn", out_dtype=BF16, deps=deps, tm=1024, tn=1792)
    wts = dict(mix_fn(p), w_in=w_in)
    mk = _matmul("mem_k", mem2, wts["w_mem_k"], mode="nn", out_dtype=BF16)
    mv = _matmul("mem_v", mem2, wts["w_mem_v"], mode="nn", out_dtype=BF16)
    y, st, opre = _mixer_fwd(p, mk, mv, lb, wts["conv_w"], wts["hg_norm_w"], bl=bl, seq=seq)
    wts.update(late_fn(y))
    merged, xhat1, rstd1, x1b = _merge_fwd(y, p, x0, wts["w_branch"], wts["w_o"], wts["b_gate"], wts["ln1_g"], wts["ln1_b"],
                                           alpha=alpha)
    a, xhat2, rstd2, x2, x2b = _mlp_fwd(xhat1, x1b, wts["ln1_g"], wts["ln1_b"], wts["w_up"], wts["w_down"], wts["ln2_g"],
                                        wts["ln2_b"], alpha=alpha)
    saved = dict(x0b=x0b, p=p, mk=mk, mv=mv, y=y, st=st, opre=opre, merged=merged, xhat1=xhat1, rstd1=rstd1, x1b=x1b, a=a,
                 xhat2=xhat2, rstd2=rstd2)
    return x2, x2b, saved, wts


def _relu2_bf16(a):
    return jnp.square(jnp.maximum(a.astype(F32), 0.0)).astype(BF16)


def _mlp_bwd(dz2, dz2b, sv, wts, *, alpha, deps=()):
    g = {}
    da = _matmul("mlp_da", dz2b, wts["w_down"], mode="nt", out_dtype=BF16, tm=512, tn=wts["w_down"].shape[0], deps=deps,
                 epi_fn=lambda acc, a: (acc * (2.0 * jnp.maximum(a.astype(F32), 0.0)),), epi_extra=(sv["a"],))
    g["w_down"] = _matmul_tn("grad_w_down", sv["a"], dz2b, a_fn=_relu2_bf16, out_dtype=BF16, tt=2048)
    g["w_up"] = _matmul_tn("grad_w_up", sv["x1b"], da, out_dtype=BF16, tt=2048)
    dx1 = _matmul("mlp_dx", da, wts["w_up"], mode="nt", epi_fn=lambda acc, dz: (acc + alpha * dz,), epi_extra=(dz2,),
                  tm=512, tk=4096)
    dz1, dz1b, dg1, db1 = _ln_bwd(dx1, sv["xhat1"], sv["rstd1"], wts["ln1_g"])
    g["ln1_g"], g["ln1_b"] = dg1[0:1], db1[0:1]
    return dz1, dz1b, g


def _mix_bwd(dz1, dz1b, sv, mem2, lb, wts, *, bl, seq, alpha, send, below=None, deps=()):
    d = dz1.shape[1]
    g = {}
    g["w_o"] = _matmul_tn("grad_w_o", sv["merged"], dz1b, out_dtype=BF16, tt=2048, deps=deps)
    dr, dp, dy, dbg = _merge_bwd(dz1b, sv["p"], sv["y"], wts["w_branch"], wts["w_o"], wts["b_gate"])
    g["b_gate"] = dbg[0:1]
    g["w_branch"] = _matmul_tn("grad_w_branch", sv["y"], dr, ta=W, tb=d, tt=4096, out_dtype=BF16, paired=True)
    token = send(("w_o", "w_branch"), g)
    dp, dmk, dmv, dcw, dnw, dlb = _mixer_bwd(sv["p"], dy, dp, sv["st"], sv["opre"], sv["mk"], sv["mv"], lb,
                                              wts["conv_w"], wts["hg_norm_w"], bl=bl, seq=seq, deps=(token,))
    g["conv_w"], g["hg_norm_w"], g["lb"] = dcw[0:CONV_K], dnw[0:1], dlb[0:1]
    g["w_mem_k"] = _matmul_tn("grad_w_mem_k", mem2, dmk, out_dtype=BF16)
    g["w_mem_v"] = _matmul_tn("grad_w_mem_v", mem2, dmv, out_dtype=BF16)
    g["w_in"] = _matmul_tn("grad_w_in", sv["x0b"], dp, out_dtype=BF16, tt=2048)
    token = send(("w_in", "w_mem_k", "w_mem_v", "conv_w"), g)
    dx0 = _matmul("proj_in_dx", dp, wts["w_in"], mode="nt", epi_fn=lambda acc, dz: (acc + alpha * dz,), epi_extra=(dz1,),
                  tm=512, tk=dp.shape[1], deps=(token,))
    return (dx0 if below is None else _ln_bwd(dx0, *below)), g


N_CHIPS = 4
MESH_IDS = pl.DeviceIdType.MESH


def _axis_slice(ref, axis, start, size):
    idx = [slice(None)] * len(ref.shape)
    idx[axis] = pl.ds(start, size)
    return ref.at[tuple(idx)]


def _chip_exchange(name, items):
    n = len(items)
    out_shapes, meta = [], []
    for arr, kind, axis in items:
        shp = list(arr.shape)
        if kind == "gather":
            per = shp[axis]
            shp[axis] = per * N_CHIPS
            out_shapes.append(jax.ShapeDtypeStruct(tuple(shp), arr.dtype))
        elif kind == "scatter":
            per = shp[axis] // N_CHIPS
            shp[axis] = per
            out_shapes.append(jax.ShapeDtypeStruct((N_CHIPS, *shp), arr.dtype))
        else:
            per = None
            out_shapes.append(jax.ShapeDtypeStruct((N_CHIPS, *shp), arr.dtype))
        meta.append((kind, axis, per))

    def body(*refs):
        ins, outs = refs[:n], refs[n:2 * n]
        send_sems, recv_sems, local_sems = refs[2 * n:]
        x, y, c = lax.axis_index("x"), lax.axis_index("y"), lax.axis_index("c")
        me = 2 * x + y
        peers = [(1 - x, y), (x, 1 - y), (1 - x, 1 - y)]

        def src_for(t, chip):
            kind, axis, per = meta[t]
            return _axis_slice(ins[t], axis, chip * per, per) if kind == "scatter" else ins[t]

        def dst_from(t, chip):
            kind, axis, per = meta[t]
            return _axis_slice(outs[t], axis, chip * per, per) if kind == "gather" else outs[t].at[chip]

        def remote(t, k):
            px, py = peers[k]
            return pltpu.make_async_remote_copy(
                src_ref=src_for(t, 2 * px + py), dst_ref=dst_from(t, me), send_sem=send_sems.at[t * 3 + k],
                recv_sem=recv_sems.at[t * 3 + k], device_id=(px, py, c), device_id_type=MESH_IDS)

        def arrival(t, k):
            px, py = peers[k]
            return pltpu.make_async_remote_copy(
                src_ref=src_for(t, me), dst_ref=dst_from(t, 2 * px + py), send_sem=send_sems.at[t * 3 + k],
                recv_sem=recv_sems.at[t * 3 + k], device_id=(px, py, c), device_id_type=MESH_IDS)

        sends = [remote(t, k) for t in range(n) for k in range(3)]
        for cp in sends:
            cp.start()
        own = [pltpu.make_async_copy(src_for(t, me), dst_from(t, me), local_sems.at[t]) for t in range(n)]
        for cp in own:
            cp.start()
        for t in range(n):
            for k in range(3):
                arrival(t, k).wait_recv()
        for cp in sends:
            cp.wait_send()
        for cp in own:
            cp.wait()

    any_spec = pl.BlockSpec(memory_space=pl.ANY)
    return pl.pallas_call(
        body,
        name=name,
        in_specs=[any_spec] * n,
        out_specs=[any_spec] * n,
        out_shape=out_shapes,
        scratch_shapes=[pltpu.SemaphoreType.DMA((3 * n,)), pltpu.SemaphoreType.DMA((3 * n,)), pltpu.SemaphoreType.DMA((n,))],
        compiler_params=pltpu.CompilerParams(has_side_effects=True),
    )(*[a for a, _, _ in items])


HBM_SPEC = pl.BlockSpec(memory_space=pltpu.HBM)
SEM_SPEC = pl.BlockSpec(memory_space=pltpu.SEMAPHORE)
N_PEERS = N_CHIPS - 1


def _my_chip():
    return (2 * lax.axis_index("x") + lax.axis_index("y")).astype(jnp.int32).reshape(1)


def _own_block_spec(r, c, axis, tr):
    if axis == 1:
        return pl.BlockSpec((tr, c), lambda i, me: (i, me[0]))
    return pl.BlockSpec((tr, c), lambda i, me: (me[0] * (r // tr) + i, 0))


def _place_shard(name, shard, axis, me):
    r, c = shard.shape
    tr = _row_block(r, c, shard.dtype.itemsize)
    shp = (r, c * N_CHIPS) if axis == 1 else (r * N_CHIPS, c)

    def body(me_ref, s_ref, buf_ref, o_ref):
        del me_ref, buf_ref
        o_ref[...] = s_ref[...]

    buf = pltpu.with_memory_space_constraint(lax.empty(shp, shard.dtype), pltpu.HBM)
    return pl.pallas_call(
        body, name=name,
        grid_spec=pltpu.PrefetchScalarGridSpec(
            num_scalar_prefetch=1, grid=(r // tr,),
            in_specs=[pl.BlockSpec((tr, c), lambda i, me: (i, 0)), ANY_SPEC], out_specs=_own_block_spec(r, c, axis, tr)),
        out_shape=jax.ShapeDtypeStruct(shp, shard.dtype),
        input_output_aliases={2: 0},
        compiler_params=_cparams(("parallel",)),
    )(me, shard, buf)


class _Split:
    def __init__(self, name, items):
        self.name, self.n = name, len(items)
        self.srcs = [a for a, _, _ in items]
        self.meta, self.land_shapes = [], []
        for arr, kind, axis in items:
            shp = list(arr.shape)
            if kind == "gather":
                per = shp[axis]
                shp[axis] = per * N_CHIPS
                self.land_shapes.append(jax.ShapeDtypeStruct(tuple(shp), arr.dtype))
            else:
                per = shp[axis] // N_CHIPS
                shp[axis] = per
                self.land_shapes.append(jax.ShapeDtypeStruct((N_PEERS, *shp), arr.dtype))
            self.meta.append((kind, axis, per))

    def _src(self, ins, t, chip):
        kind, axis, per = self.meta[t]
        return _axis_slice(ins[t], axis, chip * per, per) if kind == "scatter" else ins[t]

    def _dst(self, lands, t, chip, slot):
        kind, axis, per = self.meta[t]
        return _axis_slice(lands[t], axis, chip * per, per) if kind == "gather" else lands[t].at[slot]

    def landing_zones(self, me):
        return [_place_shard(self.name + "_own", src, axis, me) if kind == "gather" else lax.empty(ls.shape, ls.dtype)
                for src, ls, (kind, axis, _) in zip(self.srcs, self.land_shapes, self.meta)]

    def _copies(self, ins, lands, send_sems, recv_sems, arrivals):
        x, y, c = lax.axis_index("x"), lax.axis_index("y"), lax.axis_index("c")
        me = 2 * x + y
        peers = [(1 - x, y), (x, 1 - y), (1 - x, 1 - y)]
        res = []
        for t in range(self.n):
            for k, (px, py) in enumerate(peers):
                theirs = 2 * px + py
                sems = dict(send_sem=send_sems.at[t * N_PEERS + k], recv_sem=recv_sems.at[t * N_PEERS + k],
                            device_id=(px, py, c), device_id_type=MESH_IDS)
                if arrivals:
                    res.append(pltpu.make_async_remote_copy(src_ref=self._src(ins, t, me), dst_ref=self._dst(lands, t, theirs, k), **sems))
                else:
                    res.append(pltpu.make_async_remote_copy(src_ref=self._src(ins, t, theirs), dst_ref=self._dst(lands, t, me, k), **sems))
        return res

    def start(self, lands, deps=()):
        n, nd = self.n, len(deps)

        def body(*refs):
            ins, lnd = refs[:n], refs[n:2 * n]
            send_sems, recv_sems = refs[2 * n + nd], refs[2 * n + nd + 1]
            token = refs[-1]
            for cp in self._copies(ins, lnd, send_sems, recv_sems, arrivals=False):
                cp.start()
            token[...] = jnp.zeros_like(token)

        hbm = lambda a: pltpu.HBM(a.shape, a.dtype)
        res = pl.pallas_call(
            body, name=self.name + "_start",
            in_specs=[HBM_SPEC] * (2 * n) + [ANY_SPEC] * nd,
            out_specs=[SEM_SPEC, SEM_SPEC] + [HBM_SPEC] * (2 * n) + [pl.BlockSpec(memory_space=pltpu.VMEM)],
            out_shape=[pltpu.SemaphoreType.DMA((N_PEERS * n,)), pltpu.SemaphoreType.DMA((N_PEERS * n,))]
            + [hbm(a) for a in self.srcs] + [hbm(a) for a in self.land_shapes] + [jax.ShapeDtypeStruct((8, 128), F32)],
            input_output_aliases={i: 2 + i for i in range(2 * n)},
            compiler_params=pltpu.CompilerParams(has_side_effects=pltpu.SideEffectType.DATAFLOW_SIDE_EFFECTING),
        )(*[pltpu.with_memory_space_constraint(a, pltpu.HBM) for a in self.srcs],
          *[pltpu.with_memory_space_constraint(a, pltpu.HBM) for a in lands], *deps)
        return res[:-1], res[-1]

    def wait(self, state, after):
        n = self.n
        after = tuple(after) if isinstance(after, (tuple, list)) else (after,)
        send_sems, recv_sems = state[0], state[1]
        srcs, lands = state[2:2 + n], state[2 + n:2 + 2 * n]

        def body(*refs):
            ins, lnd = refs[:n], refs[n:2 * n]
            s_sems, r_sems = refs[2 * n], refs[2 * n + 1]
            for cp in self._copies(ins, lnd, s_sems, r_sems, arrivals=True):
                cp.wait_recv()
            for cp in self._copies(ins, lnd, s_sems, r_sems, arrivals=False):
                cp.wait_send()

        hbm = lambda a: pltpu.HBM(a.shape, a.dtype)
        res = pl.pallas_call(
            body, name=self.name + "_wait",
            in_specs=[HBM_SPEC] * (2 * n) + [SEM_SPEC, SEM_SPEC] + [ANY_SPEC] * len(after),
            out_specs=[HBM_SPEC] * (2 * n),
            out_shape=[hbm(a) for a in self.srcs] + [hbm(a) for a in self.land_shapes],
            input_output_aliases={i: i for i in range(2 * n)},
            compiler_params=pltpu.CompilerParams(has_side_effects=pltpu.SideEffectType.DATAFLOW_SIDE_EFFECTING),
        )(*srcs, *lands, send_sems, recv_sems, *after)
        return res[:n], res[n:]


class _SiblingSplit:
    def __init__(self, name, arrays):
        self.name, self.n, self.arrays = name, len(arrays), list(arrays)

    def _copies(self, ins, lands, send_sems, recv_sems):
        sibling = (lax.axis_index("x"), lax.axis_index("y"), 1 - lax.axis_index("c"))
        return [pltpu.make_async_remote_copy(src_ref=ins[t], dst_ref=lands[t], send_sem=send_sems.at[t], recv_sem=recv_sems.at[t],
                                             device_id=sibling, device_id_type=MESH_IDS) for t in range(self.n)]

    def start(self, deps=()):
        n, nd = self.n, len(deps)

        def body(*refs):
            for cp in self._copies(refs[:n], refs[n:2 * n], refs[2 * n + nd], refs[2 * n + nd + 1]):
                cp.start()
            refs[-1][...] = jnp.zeros_like(refs[-1])

        hbm = [pltpu.HBM(a.shape, a.dtype) for a in self.arrays]
        res = pl.pallas_call(
            body, name=self.name + "_start",
            in_specs=[HBM_SPEC] * (2 * n) + [ANY_SPEC] * nd,
            out_specs=[SEM_SPEC, SEM_SPEC] + [HBM_SPEC] * (2 * n) + [pl.BlockSpec(memory_space=pltpu.VMEM)],
            out_shape=[pltpu.SemaphoreType.DMA((n,)), pltpu.SemaphoreType.DMA((n,))] + hbm + hbm + [jax.ShapeDtypeStruct((8, 128), F32)],
            input_output_aliases={i: 2 + i for i in range(2 * n)},
            compiler_params=pltpu.CompilerParams(has_side_effects=pltpu.SideEffectType.DATAFLOW_SIDE_EFFECTING),
        )(*[pltpu.with_memory_space_constraint(a, pltpu.HBM) for a in self.arrays],
          *[pltpu.with_memory_space_constraint(lax.empty(a.shape, a.dtype), pltpu.HBM) for a in self.arrays], *deps)
        return res[:-1], res[-1]

    def wait(self, state, after):
        n = self.n
        after = tuple(after) if isinstance(after, (tuple, list)) else (after,)

        def body(*refs):
            for cp in self._copies(refs[:n], refs[n:2 * n], refs[2 * n], refs[2 * n + 1]):
                cp.wait()

        hbm = [pltpu.HBM(a.shape, a.dtype) for a in self.arrays]
        res = pl.pallas_call(
            body, name=self.name + "_wait",
            in_specs=[HBM_SPEC] * (2 * n) + [SEM_SPEC, SEM_SPEC] + [ANY_SPEC] * len(after),
            out_specs=[HBM_SPEC] * (2 * n),
            out_shape=hbm + hbm,
            input_output_aliases={i: i for i in range(2 * n)},
            compiler_params=pltpu.CompilerParams(has_side_effects=pltpu.SideEffectType.DATAFLOW_SIDE_EFFECTING),
        )(*state[2:2 + 2 * n], state[0], state[1], *after)
        return res[:n], res[n:]


def _sibling_swap(name, arrays):
    n = len(arrays)

    def body(*refs):
        ins, outs = refs[:n], refs[n:2 * n]
        send_sems, recv_sems = refs[2 * n:]
        sibling = (lax.axis_index("x"), lax.axis_index("y"), 1 - lax.axis_index("c"))
        copies = [pltpu.make_async_remote_copy(src_ref=ins[t], dst_ref=outs[t], send_sem=send_sems.at[t], recv_sem=recv_sems.at[t],
                                               device_id=sibling, device_id_type=MESH_IDS) for t in range(n)]
        for cp in copies:
            cp.start()
        for cp in copies:
            cp.wait()

    any_spec = pl.BlockSpec(memory_space=pl.ANY)
    return pl.pallas_call(
        body,
        name=name,
        in_specs=[any_spec] * n,
        out_specs=[any_spec] * n,
        out_shape=[jax.ShapeDtypeStruct(a.shape, a.dtype) for a in arrays],
        scratch_shapes=[pltpu.SemaphoreType.DMA((n,)), pltpu.SemaphoreType.DMA((n,))],
        compiler_params=pltpu.CompilerParams(has_side_effects=True),
    )(*arrays)


def _row_block(r, c, itemsize=4, target=1 << 20):
    if r % 8 != 0:
        return r
    best = 8
    for tr in range(8, r + 1, 8):
        if r % tr == 0 and tr * c * itemsize <= target:
            best = tr
    return best


def _sum_chips_into(parts, stacked, layer):
    _, r, c = parts.shape
    tr = _row_block(r, c)

    def body(p_ref, s_ref, o_ref):
        del s_ref
        o_ref[...] = ((p_ref[0] + p_ref[1]) + p_ref[2]) + p_ref[3]

    return pl.pallas_call(
        body,
        name="sum_chips",
        grid=(r // tr,),
        in_specs=[pl.BlockSpec((N_CHIPS, tr, c), lambda i: (0, i, 0)), pl.BlockSpec(memory_space=pl.ANY)],
        out_specs=pl.BlockSpec((None, tr, c), lambda i: (layer, i, 0)),
        out_shape=jax.ShapeDtypeStruct(stacked.shape, stacked.dtype),
        input_output_aliases={1: 0},
        compiler_params=_cparams(("parallel",)),
    )(parts, stacked)


def _sum_own_and_peers(me, g, axis, landed):
    _, r, c = landed.shape
    tr = _row_block(r, c, target=1 << 21)

    def body(me_ref, g_ref, p_ref, o_ref):
        del me_ref
        o_ref[...] = ((g_ref[...].astype(F32) + p_ref[0].astype(F32)) + p_ref[1].astype(F32)) + p_ref[2].astype(F32)

    return pl.pallas_call(
        body, name="sum_chips_own",
        grid_spec=pltpu.PrefetchScalarGridSpec(
            num_scalar_prefetch=1, grid=(r // tr,),
            in_specs=[_own_block_spec(r, c, axis, tr), pl.BlockSpec((N_PEERS, tr, c), lambda i, me: (0, i, 0))],
            out_specs=pl.BlockSpec((tr, c), lambda i, me: (i, 0))),
        out_shape=jax.ShapeDtypeStruct((r, c), F32),
        compiler_params=_cparams(("parallel",)),
    )(me, g, landed)


ADAMW_BLOCK_BYTES = 3 << 19


def _adamw_math(w, m, v, g):
    m_new = ADAM_B1 * m + (1.0 - ADAM_B1) * g
    v_new = ADAM_B2 * v + (1.0 - ADAM_B2) * jnp.square(g)
    m_hat = m_new / (1.0 - ADAM_B1 ** ADAM_STEP)
    v_hat = v_new / (1.0 - ADAM_B2 ** ADAM_STEP)
    return -ADAM_LR * (m_hat / (jnp.sqrt(v_hat) + ADAM_EPS) + ADAM_WD * w), m_new, v_new


def _adamw(w, m, v, g_a, g_b):
    L, r, c = w.shape
    tr = _row_block(r, c, target=ADAMW_BLOCK_BYTES)

    def body(w_ref, m_ref, v_ref, ga_ref, gb_ref, g_ref, d_ref, nm_ref, nv_ref):
        g = ga_ref[...] + gb_ref[...]
        g_ref[...] = g
        d_ref[...], nm_ref[...], nv_ref[...] = _adamw_math(w_ref[...], m_ref[...], v_ref[...], g)

    spec = pl.BlockSpec((None, tr, c), lambda l, i: (l, i, 0))
    return pl.pallas_call(
        body,
        name="adamw",
        grid=(L, r // tr),
        in_specs=[spec] * 5,
        out_specs=[spec] * 4,
        out_shape=[jax.ShapeDtypeStruct(w.shape, F32)] * 4,
        compiler_params=_cparams(("parallel", "parallel")),
    )(w, m, v, g_a, g_b)


def _adamw_layer(w, m, v, g_a, g_b, layer, outs):
    L, r, c = w.shape
    tr = _row_block(r, c, target=ADAMW_BLOCK_BYTES)
    n_prev = 0 if outs is None else 4

    def body(w_ref, m_ref, v_ref, ga_ref, gb_ref, *rest):
        g_ref, d_ref, nm_ref, nv_ref = rest[n_prev:]
        g = ga_ref[...] + gb_ref[...]
        g_ref[...] = g
        d_ref[...], nm_ref[...], nv_ref[...] = _adamw_math(w_ref[...], m_ref[...], v_ref[...], g)

    at_layer = pl.BlockSpec((None, tr, c), lambda i: (layer, i, 0))
    flat = pl.BlockSpec((tr, c), lambda i: (i, 0))
    return pl.pallas_call(
        body,
        name="adamw_layer",
        grid=(r // tr,),
        in_specs=[at_layer] * 3 + [flat] * 2 + [ANY_SPEC] * n_prev,
        out_specs=[at_layer] * 4,
        out_shape=[jax.ShapeDtypeStruct(w.shape, F32)] * 4,
        input_output_aliases={5 + k: k for k in range(n_prev)},
        compiler_params=_cparams(("parallel",)),
    )(w, m, v, g_a, g_b, *(outs or ()))


SHARDED = (("w_in", 1), ("conv_w", 1), ("w_mem_k", 0), ("w_mem_v", 0), ("w_branch", 1), ("w_o", 0), ("w_up", 1), ("w_down", 0))
SMALL = ("lower_bounds", "hg_norm_w", "b_gate", "ln1_g", "ln1_b", "ln2_g", "ln2_b")
WEIGHT_ORDER = ("lower_bounds", "w_in", "conv_w", "hg_norm_w", "w_mem_k", "w_mem_v", "w_branch", "b_gate", "w_o", "ln1_g", "ln1_b",
                "w_up", "w_down", "ln2_g", "ln2_b")


def kernel(x, mem, lower_bounds, w_in, conv_w, hg_norm_w, w_mem_k, w_mem_v, w_branch, b_gate, w_o, ln1_g, ln1_b, w_up, w_down, ln2_g, ln2_b, loss_target, m_lower_bounds, m_w_in, m_conv_w, m_hg_norm_w, m_w_mem_k, m_w_mem_v, m_w_branch, m_b_gate, m_w_o, m_ln1_g, m_ln1_b, m_w_up, m_w_down, m_ln2_g, m_ln2_b, v_lower_bounds, v_w_in, v_conv_w, v_hg_norm_w, v_w_mem_k, v_w_mem_v, v_w_branch, v_b_gate, v_w_o, v_ln1_g, v_ln1_b, v_w_up, v_w_down, v_ln2_g, v_ln2_b):
    bl, seq, d = x.shape
    depth = w_in.shape[0]
    weights = dict(lower_bounds=lower_bounds, w_in=w_in, conv_w=conv_w, hg_norm_w=hg_norm_w, w_mem_k=w_mem_k, w_mem_v=w_mem_v,
                   w_branch=w_branch, b_gate=b_gate, w_o=w_o, ln1_g=ln1_g, ln1_b=ln1_b, w_up=w_up, w_down=w_down, ln2_g=ln2_g, ln2_b=ln2_b)
    mom_m = dict(lower_bounds=m_lower_bounds, w_in=m_w_in, conv_w=m_conv_w, hg_norm_w=m_hg_norm_w, w_mem_k=m_w_mem_k, w_mem_v=m_w_mem_v,
                 w_branch=m_w_branch, b_gate=m_b_gate, w_o=m_w_o, ln1_g=m_ln1_g, ln1_b=m_ln1_b, w_up=m_w_up, w_down=m_w_down,
                 ln2_g=m_ln2_g, ln2_b=m_ln2_b)
    mom_v = dict(lower_bounds=v_lower_bounds, w_in=v_w_in, conv_w=v_conv_w, hg_norm_w=v_hg_norm_w, w_mem_k=v_w_mem_k, w_mem_v=v_w_mem_v,
                 w_branch=v_w_branch, b_gate=v_b_gate, w_o=v_w_o, ln1_g=v_ln1_g, ln1_b=v_ln1_b, w_up=v_w_up, w_down=v_w_down,
                 ln2_g=v_ln2_g, ln2_b=v_ln2_b)

    def shard2d(name, l):
        w = weights[name][l]
        if name == "w_branch":
            return w.reshape(N_BRANCH * W, w.shape[-1]).astype(BF16)
        return w if name == "conv_w" else w.astype(BF16)

    me = _my_chip()

    shard_axis = dict(SHARDED)

    def prepare_exchange(name, kind, items):
        ex = _Split(name, [(arr, kind, shard_axis[nm]) for nm, arr in items])
        return ex, ex.landing_zones(me), [nm for nm, _ in items]

    def launch(prepared, deps=()):
        ex, lands, names = prepared
        state, token = ex.start(lands, deps)
        return ex, state, names, token

    def start_exchange(name, kind, items, deps=()):
        return launch(prepare_exchange(name, kind, items), deps)

    def prepare_gathers(l):
        groups = (("in", ("w_in",)), ("mix", ("conv_w", "w_mem_k", "w_mem_v")), ("rest", ("w_branch", "w_o", "w_up", "w_down")))
        return tuple(prepare_exchange(f"gather_{tag}_l{l}", "gather", [(nm, shard2d(nm, l)) for nm in names]) for tag, names in groups)

    def start_gathers(prepared, deps=()):
        started = []
        for prep in prepared:
            started.append(launch(prep, deps))
            deps = (started[-1][3],)
        return tuple(started)

    def gathered(pend, after):
        ex, state, names, _ = pend
        return dict(zip(names, ex.wait(state, after=after)[1]))

    pending = start_gathers(prepare_gathers(0))
    tokens = tuple(pend[3] for pend in pending)
    tokens, x, mem, loss_target, weights, mom_m, mom_v = lax.optimization_barrier((tokens, x, mem, loss_target, weights, mom_m, mom_v))
    pending = tuple((*pend[:3], tok) for pend, tok in zip(pending, tokens))
    lower_bounds = weights["lower_bounds"]

    x2d, mem2, t2d = x.reshape(bl * seq, d), mem.reshape(-1, d), loss_target.reshape(bl * seq, d)
    alpha = (2.0 * depth) ** 0.25
    soft, lb_all = _lower_bounds_fwd(lower_bounds)

    prepared = [None] + [prepare_gathers(l) for l in range(1, depth)]
    early = [x2d.astype(BF16), lb_all] + [z for prep in prepared[1:] for _, lands, _ in prep for z in lands]

    h, hb, saved, layer_wts = x2d, early[0], [], []
    for l in range(depth):
        first, mix, rest = pending
        w_in_l = gathered(first, early if l == 0 else h)["w_in"]

        def mix_fn(after, l=l, mix=mix):
            return dict(gathered(mix, after), hg_norm_w=weights["hg_norm_w"][l][None, :])

        def late_fn(after, l=l, rest=rest):
            wts = gathered(rest, after)
            for name in ("b_gate", "ln1_g", "ln1_b", "ln2_g", "ln2_b"):
                wts[name] = weights[name][l][None, :]
            return wts

        deps = (rest[3],)
        if l + 1 < depth:
            pending = start_gathers(prepared[l + 1], (w_in_l, rest[3]))
            deps += tuple(pend[3] for pend in pending)
        h, hb, sv, wts = _layer_fwd(h, hb, mem2, lb_all[l:l + 1], w_in_l, mix_fn, late_fn, bl=bl, seq=seq, alpha=alpha, deps=deps)
        saved.append(sv)
        layer_wts.append(wts)
    loss, dz2, dz2b, dg2, db2 = _loss_head(h, t2d, saved[-1]["xhat2"], saved[-1]["rstd2"], layer_wts[-1]["ln2_g"])

    shape3 = {name: (depth, weights[name].size // (depth * weights[name].shape[-1]), weights[name].shape[-1]) for name, _ in SHARDED}
    partial = [dict() for _ in range(depth)]
    smalls = [None] * depth
    outs = {name: None for name, _ in SHARDED}

    def finish_reduce(pend, l, after):
        ex, state, names, _ = pend
        sent, got = ex.wait(state, after=after)
        for nm, g_full, landed in zip(names, sent, got):
            partial[l][nm] = _sum_own_and_peers(me, g_full, shard_axis[nm], landed)

    names_sharded = [name for name, _ in SHARDED]

    def start_swap(l):
        swap = _SiblingSplit(f"swap_partials_l{l}", [partial[l][nm] for nm in names_sharded])
        state, token = swap.start()
        return swap, state, token

    def optimizer_step(l, pend, after):
        swap, state, _ = pend
        mine, theirs = swap.wait(state, after)
        for nm, own, other in zip(names_sharded, mine, theirs):
            outs[nm] = _adamw_layer(weights[nm].reshape(shape3[nm]), mom_m[nm].reshape(shape3[nm]), mom_v[nm].reshape(shape3[nm]),
                                    own, other, l, outs[nm])
        return tuple(outs[nm][0] for nm in names_sharded)

    pending_mix, pending_swap, deps = [], None, ()
    for l in reversed(range(depth)):
        dz1, dz1b, g_mlp = _mlp_bwd(dz2, dz2b, saved[l], layer_wts[l], alpha=alpha, deps=deps)
        g_mlp["ln2_g"], g_mlp["ln2_b"] = dg2[0:1], db2[0:1]
        pending_mlp = start_exchange(f"reduce_mlp_l{l}", "scatter", [(nm, g_mlp[nm]) for nm in ("w_up", "w_down")])
        deps = (pending_mlp[3],)
        if pending_mix:
            for pend in pending_mix:
                finish_reduce(pend, l + 1, dz1)
            pending_swap = start_swap(l + 1)
            deps += (pending_swap[2],)
        pending_mix = []

        def send(names, g, l=l, pending_mix=pending_mix):
            pend = start_exchange(f"reduce_{names[0]}_l{l}", "scatter", [(nm, g[nm]) for nm in names])
            pending_mix.append(pend)
            return pend[3]

        below = (saved[l - 1]["xhat2"], saved[l - 1]["rstd2"], layer_wts[l - 1]["ln2_g"]) if l > 0 else None
        out, g = _mix_bwd(dz1, dz1b, saved[l], mem2, lb_all[l:l + 1], layer_wts[l], bl=bl, seq=seq, alpha=alpha, send=send,
                          below=below, deps=deps)
        if l > 0:
            dz2, dz2b, dg2, db2 = out
        else:
            dh = out
        finish_reduce(pending_mlp, l, out[0] if l > 0 else out)
        deps = ()
        if pending_swap is not None:
            deps = optimizer_step(l + 1, pending_swap, out[0] if l > 0 else out)
            pending_swap = None
        g.update(g_mlp, lower_bounds=g["lb"])
        smalls[l] = jnp.concatenate([g[nm] for nm in SMALL], axis=1)
    small_parts = _chip_exchange("reduce_small", [(jnp.stack(smalls), "bcast", 0)])[0]
    small_sum = _sum_chips_into(small_parts.reshape(N_CHIPS, depth, -1), jnp.zeros((1, depth, small_parts.shape[-1]), F32), 0)
    small_sum = small_sum.reshape(depth, 1, -1)
    small_theirs = _sibling_swap("swap_small", [small_sum])[0]
    for pend in pending_mix:
        finish_reduce(pend, 0, small_theirs)
    optimizer_step(0, start_swap(0), small_theirs)

    outs = {name: [r.reshape(weights[name].shape) for r in res] for name, res in outs.items()}
    off = 0
    for name in SMALL:
        n = weights[name].shape[1]
        mine, other = small_sum[:, :, off:off + n], small_theirs[:, :, off:off + n]
        off += n
        if name == "lower_bounds":
            mine = _lower_bounds_bwd(soft, mine[:, 0, :])[:, None, :]
            other = _lower_bounds_bwd(soft, other[:, 0, :])[:, None, :]
        shp = (depth, 1, n)
        res = _adamw(weights[name].reshape(shp), mom_m[name].reshape(shp), mom_v[name].reshape(shp), mine, other)
        outs[name] = [r.reshape(weights[name].shape) for r in res]
    assert off == small_sum.shape[-1]

    total_loss = lax.psum(loss[0, 0], ("x", "y", "c"))
    result = [total_loss, dh.reshape(bl, seq, d)]
    for k in range(4):
        result += [outs[name][k] for name in WEIGHT_ORDER]
    return tuple(result)
```

```python
import functools

import jax
import jax.numpy as jnp
from jax import lax
from jax.experimental import pallas as pl
from jax.experimental.pallas import tpu as pltpu

F32 = jnp.float32
BF16 = jnp.bfloat16

HG_HEADS = 4
HG_F = 128
HG_CHUNK = 32
MEM_HEADS = 4
MEM_HEAD_DIM = 128
BRANCH_WIDTH = 512
N_BRANCH = 3
CONV_K = 3
LN_EPS = 1e-5
RMS_EPS = 1e-6
ADAM_LR = 0.001
ADAM_B1 = 0.9
ADAM_B2 = 0.999
ADAM_EPS = 1e-08
ADAM_WD = 0.01
ADAM_STEP = 10

VMEM_LIMIT = 48 * 1024 * 1024


def _cparams(sem):
    return pltpu.CompilerParams(dimension_semantics=sem, vmem_limit_bytes=VMEM_LIMIT)


def _dot(a, b, dims):
    return lax.dot_general(a, b, (dims, ((), ())), preferred_element_type=F32)


NN = ((1,), (0,))
NT = ((1,), (1,))
TN = ((0,), (0,))


def _pick(n, pref):
    for t in pref:
        if n % t == 0:
            return t
    return n


ANY_SPEC = pl.BlockSpec(memory_space=pl.ANY)


def _matmul(name, a, b, *, mode, out_dtype=F32, a_fn=None, a_extra=(), epi_fn=None, epi_extra=(), n_out=1, out_kinds=None,
            tm=512, tn=1024, tk=1024, deps=()):
    M, K = a.shape
    N = b.shape[1] if mode == "nn" else b.shape[0]
    tm, tn, tk = _pick(M, (tm, 256, 128, 8)), _pick(N, (tn, 896, 512, 256, 128)), _pick(K, (tk, 512, 256, 128))
    nk = K // tk
    n_ax, n_ex = len(a_extra), len(epi_extra)
    n_in = 2 + n_ax + n_ex + len(deps)
    out_dtypes = out_dtype if isinstance(out_dtype, (tuple, list)) else (out_dtype,) * n_out
    out_kinds = out_kinds or ("tile",) * n_out

    def body(*refs):
        a_ref, b_ref = refs[0], refs[1]
        ax_refs = refs[2:2 + n_ax]
        ex_refs = refs[2 + n_ax:2 + n_ax + n_ex]
        o_refs = refs[n_in:n_in + n_out]
        at = a_ref[...]
        at = a_fn(at, *[r[...] for r in ax_refs]) if a_fn is not None else at.astype(BF16)
        part = _dot(at, b_ref[...].astype(BF16), NN if mode == "nn" else NT)

        def finish(acc):
            outs = epi_fn(acc, *[r[...] for r in ex_refs]) if epi_fn is not None else (acc,)
            for o_ref, o, kind in zip(o_refs, outs, out_kinds):
                if kind == "rowsum":
                    @pl.when(pl.program_id(1) == 0)
                    def _(o_ref=o_ref):
                        o_ref[...] = jnp.zeros_like(o_ref)

                    o_ref[0:1, :] += o
                else:
                    o_ref[...] = o.astype(o_ref.dtype)

        if nk == 1:
            finish(part)
            return
        acc_ref = refs[-1]
        k = pl.program_id(2)

        @pl.when(k == 0)
        def _():
            acc_ref[...] = part

        @pl.when(jnp.logical_and(k > 0, k < nk - 1))
        def _():
            acc_ref[...] += part

        @pl.when(k == nk - 1)
        def _():
            finish(acc_ref[...] + part)

    b_mode = dict(pipeline_mode=pl.Buffered(1)) if (nk == 1 and N == tn) else {}
    in_specs = [pl.BlockSpec((tm, tk), lambda j, i, k: (i, k)),
                pl.BlockSpec((tk, tn), lambda j, i, k: (k, j), **b_mode) if mode == "nn"
                else pl.BlockSpec((tn, tk), lambda j, i, k: (j, k), **b_mode)]
    in_specs += [pl.BlockSpec((1, tk), lambda j, i, k: (0, k)) for _ in a_extra]
    for e in epi_extra:
        if e.shape[0] == 1:
            in_specs.append(pl.BlockSpec((1, tn), lambda j, i, k: (0, j)))
        elif e.shape[1] == 1:
            in_specs.append(pl.BlockSpec((tm, 1), lambda j, i, k: (i, 0)))
        else:
            in_specs.append(pl.BlockSpec((tm, tn), lambda j, i, k: (i, j)))
    in_specs += [ANY_SPEC] * len(deps)
    out_specs, out_shapes = [], []
    for kind, dt in zip(out_kinds, out_dtypes):
        if kind == "col":
            out_specs.append(pl.BlockSpec((tm, 1), lambda j, i, k: (i, 0)))
            out_shapes.append(jax.ShapeDtypeStruct((M, 1), dt))
        elif kind == "rowsum":
            out_specs.append(pl.BlockSpec((8, tn), lambda j, i, k: (0, j)))
            out_shapes.append(jax.ShapeDtypeStruct((8, N), dt))
        else:
            out_specs.append(pl.BlockSpec((tm, tn), lambda j, i, k: (i, j)))
            out_shapes.append(jax.ShapeDtypeStruct((M, N), dt))
    out = pl.pallas_call(
        body,
        name=name,
        grid=(N // tn, M // tm, nk),
        in_specs=in_specs,
        out_specs=out_specs,
        out_shape=out_shapes,
        scratch_shapes=[pltpu.VMEM((tm, tn), F32)] if nk > 1 else [],
        compiler_params=_cparams(("arbitrary", "arbitrary", "arbitrary")),
    )(a, b, *a_extra, *epi_extra, *deps)
    return out[0] if n_out == 1 else out


def _matmul_tn(name, a, b, *, a_fn=None, a_extra=(), a_cols=None, b_cols=None, ta=1024, tb=1024, tt=1024, out_dtype=F32, deps=(),
               paired=False):
    T = a.shape[0]
    a0, Ka = a_cols if a_cols is not None else (0, a.shape[1])
    b0, Nb = b_cols if b_cols is not None else (0, b.shape[1])
    ta, tb, tt = _pick(Ka, (ta, 512, 256, 128)), _pick(Nb, (tb, 896, 512, 256, 128)), _pick(T, (tt, 512, 256, 128))
    assert a0 % ta == 0 and b0 % tb == 0
    assert not paired or Ka // ta == Nb // tb
    a0, b0 = a0 // ta, b0 // tb
    nt = T // tt
    n_ax = len(a_extra)

    def body(*refs):
        a_ref, b_ref = refs[0], refs[1]
        ax_refs = refs[2:2 + n_ax]
        o_ref = refs[2 + n_ax + len(deps)]
        acc_ref = refs[-1]
        t = pl.program_id(2)
        at = a_ref[...]
        at = a_fn(at, *[r[...] for r in ax_refs]) if a_fn is not None else at.astype(BF16)
        part = _dot(at, b_ref[...].astype(BF16), TN)

        @pl.when(t == 0)
        def _():
            acc_ref[...] = part

        @pl.when(jnp.logical_and(t > 0, t < nt - 1))
        def _():
            acc_ref[...] += part

        @pl.when(t == nt - 1)
        def _():
            o_ref[...] = (acc_ref[...] + part if nt > 1 else part).astype(o_ref.dtype)

    b_map = (lambda i, j, t: (t, b0 + i)) if paired else (lambda i, j, t: (t, b0 + j))
    in_specs = [pl.BlockSpec((tt, ta), lambda i, j, t: (t, a0 + i)), pl.BlockSpec((tt, tb), b_map)]
    in_specs += [pl.BlockSpec((1, ta), lambda i, j, t: (0, a0 + i)) for _ in a_extra]
    in_specs += [ANY_SPEC] * len(deps)
    return pl.pallas_call(
        body,
        name=name,
        grid=(Ka // ta, 1 if paired else Nb // tb, nt),
        in_specs=in_specs,
        out_specs=pl.BlockSpec((ta, tb), lambda i, j, t: (i, j)),
        out_shape=jax.ShapeDtypeStruct((Ka, tb if paired else Nb), out_dtype),
        scratch_shapes=[pltpu.VMEM((ta, tb), F32)],
        compiler_params=_cparams(("parallel", "parallel", "arbitrary")),
    )(a, b, *a_extra, *deps)


W = BRANCH_WIDTH
C_CB, C_CC, C_CH, C_HQ, C_HF, C_HI, C_HG, C_MQ, N_MIX = 0, W, 2 * W, 3 * W, 4 * W, 5 * W, 6 * W, 7 * W, 8 * W
TS_MIX = 256
PREV_ROWS = 16
KEEP_NAMES = ("sq", "qs", "k", "sig", "f", "ea", "eb", "eq", "ek")


def _sigmoid(x):
    return jax.nn.sigmoid(x)


def _chunk_pos(shape):
    return lax.broadcasted_iota(jnp.int32, shape, 0) & (HG_CHUNK - 1)


def _seg_cumsum(x, pos):
    sh = 1
    while sh < HG_CHUNK:
        x = x + jnp.where(pos >= sh, pltpu.roll(x, sh, 0), 0.0)
        sh *= 2
    return x


def _seg_rev_cumsum(x, pos):
    n = x.shape[0]
    sh = 1
    while sh < HG_CHUNK:
        x = x + jnp.where(pos < HG_CHUNK - sh, pltpu.roll(x, n - sh, 0), 0.0)
        sh *= 2
    return x


def _chunk_mask(ts):
    r = lax.broadcasted_iota(jnp.int32, (ts, ts), 0)
    c = lax.broadcasted_iota(jnp.int32, (ts, ts), 1)
    return jnp.logical_and((r // HG_CHUNK) == (c // HG_CHUNK), c <= r)


def _hgrn_gates(p_ref, lb):
    q = p_ref[:, C_HQ:C_HQ + W].astype(F32)
    fl = p_ref[:, C_HF:C_HF + W].astype(F32)
    sig = _sigmoid(fl)
    f = lb + (1.0 - lb) * sig
    logf = jnp.log(f)
    k = (1.0 - lb) * _sigmoid(-fl)
    sq = _sigmoid(q)
    qs = q * sq
    return q, sq, qs, sig, f, logf, k


def _hgrn_decays(logf, bc_sc, ts):
    pos = _chunk_pos(logf.shape)
    bc = _seg_cumsum(logf, pos)
    bc_sc[...] = bc
    nc = ts // HG_CHUNK
    bref = jnp.concatenate(
        [jnp.broadcast_to(bc_sc[n * HG_CHUNK + HG_CHUNK // 2 - 1:n * HG_CHUNK + HG_CHUNK // 2, :], (HG_CHUNK, W)) for n in range(nc)], axis=0)
    blast = jnp.concatenate(
        [jnp.broadcast_to(bc_sc[(n + 1) * HG_CHUNK - 1:(n + 1) * HG_CHUNK, :], (HG_CHUNK, W)) for n in range(nc)], axis=0)
    return pos, bc, bref, blast


def _conv_shift_down(u, carry_ref, row):
    n = carry_ref.shape[0]
    last, before = carry_ref[n - 1:n, :], carry_ref[n - 2:n - 1, :]
    u1 = jnp.where(row == 0, last, pltpu.roll(u, 1, 0))
    u2 = jnp.where(row == 0, before, jnp.where(row == 1, last, pltpu.roll(u, 2, 0)))
    return u1, u2


def _attn_probs(qh, kh):
    s = _dot(qh, kh, NT) * (MEM_HEAD_DIM ** -0.5)
    e = jnp.exp(s - jnp.max(s, axis=-1, keepdims=True))
    return e / jnp.sum(e, axis=-1, keepdims=True)


def _mixer_fwd(p, mk, mv, lb, conv_w, norm_w, *, bl, seq):
    T = p.shape[0]
    ts = TS_MIX
    ns = seq // ts
    nc = ts // HG_CHUNK
    ml = mk.shape[0] // bl

    def body(p_ref, mk_ref, mv_ref, lb_ref, cw_ref, nw_ref, y_ref, st_ref, opre_ref, state_sc, carry_sc, bc_sc):
        @pl.when(pl.program_id(1) == 0)
        def _():
            state_sc[...] = jnp.zeros_like(state_sc)
            carry_sc[...] = jnp.zeros_like(carry_sc)

        cb, cc, ch = (p_ref[:, c0:c0 + W].astype(F32) for c0 in (C_CB, C_CC, C_CH))
        u = cc * ch
        row = lax.broadcasted_iota(jnp.int32, (ts, W), 0)
        u1, u2 = _conv_shift_down(u, carry_sc, row)
        yconv = u2 * cw_ref[0:1, :] + u1 * cw_ref[1:2, :] + u * cw_ref[2:3, :]
        y_ref[:, 0:W] = (cb * yconv).astype(BF16)
        carry_sc[...] = u[ts - 8:ts, :]

        lbv = lb_ref[...]
        _, _, qs, _, _, logf, k = _hgrn_gates(p_ref, lbv)
        pos, bc, bref, blast = _hgrn_decays(logf, bc_sc, ts)
        a_all = (qs * jnp.exp(bc - bref)).astype(BF16)
        bk_all = (k * jnp.exp(bref - bc)).astype(BF16)
        qin_all = (qs * jnp.exp(bc)).astype(BF16)
        kout_all = (k * jnp.exp(blast - bc)).astype(BF16)
        v_all = p_ref[:, C_HI:C_HI + W].astype(BF16)
        mask = _chunk_mask(ts)
        heads = [slice(h * HG_F, (h + 1) * HG_F) for h in range(HG_HEADS)]
        st = [state_sc[h] for h in range(HG_HEADS)]
        o_inter = [[] for _ in range(HG_HEADS)]
        for n in range(nc):
            rows = slice(n * HG_CHUNK, (n + 1) * HG_CHUNK)
            for h, hs in enumerate(heads):
                st_ref[n, h] = st[h]
                o_inter[h].append(_dot(qin_all[rows, hs], st[h].astype(BF16), NT))
                kv = _dot(v_all[rows, hs], kout_all[rows, hs], TN)
                decay = jnp.exp(bc_sc[(n + 1) * HG_CHUNK - 1:(n + 1) * HG_CHUNK, hs])
                st[h] = st[h] * decay + kv
        for h in range(HG_HEADS):
            state_sc[h] = st[h]
        scores = [_dot(a_all[:, hs], bk_all[:, hs], NT) for hs in heads]
        scores = [jnp.where(mask, s, 0.0).astype(BF16) for s in scores]
        outs = [_dot(scores[h], v_all[:, hs], NN) + jnp.concatenate(o_inter[h], axis=0) for h, hs in enumerate(heads)]
        for h, hs in enumerate(heads):
            o = outs[h]
            opre_ref[:, hs] = o
            on = o * lax.rsqrt(jnp.mean(o * o, axis=-1, keepdims=True) + RMS_EPS) * nw_ref[...]
            g = p_ref[:, C_HG + h * HG_F:C_HG + (h + 1) * HG_F].astype(F32)
            y_ref[:, W + h * HG_F:W + (h + 1) * HG_F] = (on * (g * _sigmoid(g))).astype(BF16)

        mheads = [slice(h * MEM_HEAD_DIM, (h + 1) * MEM_HEAD_DIM) for h in range(MEM_HEADS)]
        probs = [_attn_probs(p_ref[:, C_MQ + h * MEM_HEAD_DIM:C_MQ + (h + 1) * MEM_HEAD_DIM].astype(BF16), mk_ref[:, hs])
                 for h, hs in enumerate(mheads)]
        for h, hs in enumerate(mheads):
            y_ref[:, 2 * W + h * MEM_HEAD_DIM:2 * W + (h + 1) * MEM_HEAD_DIM] = _dot(
                probs[h].astype(BF16), mv_ref[:, hs], NN).astype(BF16)

    return pl.pallas_call(
        body,
        name="mixer_fwd",
        grid=(bl, ns),
        in_specs=[
            pl.BlockSpec((ts, N_MIX), lambda b, s: (b * ns + s, 0)),
            pl.BlockSpec((ml, W), lambda b, s: (b, 0)),
            pl.BlockSpec((ml, W), lambda b, s: (b, 0)),
            pl.BlockSpec((1, W), lambda b, s: (0, 0)),
            pl.BlockSpec((CONV_K, W), lambda b, s: (0, 0)),
            pl.BlockSpec((1, HG_F), lambda b, s: (0, 0)),
        ],
        out_specs=[
            pl.BlockSpec((ts, 3 * W), lambda b, s: (b * ns + s, 0)),
            pl.BlockSpec((nc, HG_HEADS, HG_F, HG_F), lambda b, s: (b * ns + s, 0, 0, 0)),
            pl.BlockSpec((ts, W), lambda b, s: (b * ns + s, 0)),
        ],
        out_shape=[
            jax.ShapeDtypeStruct((T, 3 * W), BF16),
            jax.ShapeDtypeStruct((T // HG_CHUNK, HG_HEADS, HG_F, HG_F), F32),
            jax.ShapeDtypeStruct((T, W), F32),
        ],
        scratch_shapes=[pltpu.VMEM((HG_HEADS, HG_F, HG_F), F32), pltpu.VMEM((8, W), F32), pltpu.VMEM((ts, W), F32)],
        compiler_params=_cparams(("arbitrary", "arbitrary")),
    )(p, mk, mv, lb, conv_w, norm_w)


def _mixer_bwd(p, dy, dp_gates, st, opre, mk, mv, lb, conv_w, norm_w, *, bl, seq, deps=()):
    T, nin = p.shape
    ts = TS_MIX
    ns = seq // ts
    nc = ts // HG_CHUNK
    ml = mk.shape[0] // bl
    mid, last = HG_CHUNK // 2 - 1, HG_CHUNK - 1

    def body(p_ref, pprev_ref, dy_ref, dpin_ref, st_ref, opre_ref, mk_ref, mv_ref, lb_ref, cw_ref, nw_ref, *rest):
        (dp_ref, dmk_ref, dmv_ref, dcw_ref, dnw_ref, dlb_ref, dstate_sc, carry_sc, uprev_sc, ab_sc, bkb_sc, qinb_sc, koutb_sc,
         dob_sc, dv_sc, da_sc, dbk_sc, dqin_sc, dkout_sc, dec_sc, ddec_sc, *keep_scs) = rest[len(deps):]
        del dpin_ref
        b, s = pl.program_id(0), pl.program_id(1)

        @pl.when(s == 0)
        def _():
            dstate_sc[...] = jnp.zeros_like(dstate_sc)
            carry_sc[...] = jnp.zeros_like(carry_sc)
            dmk_ref[...] = jnp.zeros_like(dmk_ref)
            dmv_ref[...] = jnp.zeros_like(dmv_ref)

        @pl.when(jnp.logical_and(b == 0, s == 0))
        def _():
            dcw_ref[...] = jnp.zeros_like(dcw_ref)
            dnw_ref[...] = jnp.zeros_like(dnw_ref)
            dlb_ref[...] = jnp.zeros_like(dlb_ref)

        cb, cc, ch = (p_ref[:, c0:c0 + W].astype(F32) for c0 in (C_CB, C_CC, C_CH))
        u = cc * ch
        row = lax.broadcasted_iota(jnp.int32, (ts, W), 0)
        uprev = pprev_ref[:, C_CC:C_CC + W].astype(F32) * pprev_ref[:, C_CH:C_CH + W].astype(F32)
        uprev_sc[...] = jnp.where(s == ns - 1, 0.0, uprev)
        u1, u2 = _conv_shift_down(u, uprev_sc, row)
        w0, w1, w2 = cw_ref[0:1, :], cw_ref[1:2, :], cw_ref[2:3, :]
        dya = dy_ref[:, 0:W].astype(F32)
        dp_ref[:, C_CB:C_CB + W] = (dya * (u2 * w0 + u1 * w1 + u * w2)).astype(BF16)
        dv = cb * dya
        dv1 = jnp.where(row == ts - 1, carry_sc[0:1, :], pltpu.roll(dv, ts - 1, 0))
        dv2 = jnp.where(row == ts - 1, carry_sc[1:2, :], jnp.where(row == ts - 2, carry_sc[0:1, :], pltpu.roll(dv, ts - 2, 0)))
        du = dv * w2 + dv1 * w1 + dv2 * w0
        dp_ref[:, C_CC:C_CC + W] = (du * ch).astype(BF16)
        dp_ref[:, C_CH:C_CH + W] = (du * cc).astype(BF16)
        dcw_ref[0:1, :] += jnp.sum(dv * u2, axis=0, keepdims=True)
        dcw_ref[1:2, :] += jnp.sum(dv * u1, axis=0, keepdims=True)
        dcw_ref[2:3, :] += jnp.sum(dv * u, axis=0, keepdims=True)
        carry_sc[...] = dv[0:8, :]

        mask = _chunk_mask(ts)
        pos_c = _chunk_pos((HG_CHUNK, HG_F))
        nw = nw_ref[...]

        def block(n, h):
            rows = slice(n * HG_CHUNK, (n + 1) * HG_CHUNK)
            return rows, slice(h * HG_F, (h + 1) * HG_F)

        keep = dict(zip(KEEP_NAMES, keep_scs))

        def gates(rows, h):
            lbh = lb_ref[:, h * HG_F:(h + 1) * HG_F]
            q = p_ref[rows, C_HQ + h * HG_F:C_HQ + (h + 1) * HG_F].astype(F32)
            fl = p_ref[rows, C_HF + h * HG_F:C_HF + (h + 1) * HG_F].astype(F32)
            sig = _sigmoid(fl)
            f = lbh + (1.0 - lbh) * sig
            k = (1.0 - lbh) * _sigmoid(-fl)
            sq = _sigmoid(q)
            qs = q * sq
            bc = _seg_cumsum(jnp.log(f), pos_c)
            bref = jnp.sum(jnp.where(pos_c == mid, bc, 0.0), axis=0, keepdims=True)
            blast = jnp.sum(jnp.where(pos_c == last, bc, 0.0), axis=0, keepdims=True)
            ea, eb, eq, ek = jnp.exp(bc - bref), jnp.exp(bref - bc), jnp.exp(bc), jnp.exp(blast - bc)
            return dict(sq=sq, qs=qs, k=k, sig=sig, f=f, ea=ea, eb=eb, eq=eq, ek=ek), blast

        dnw = jnp.zeros((1, HG_F), F32)
        for n in range(nc):
            for h in range(HG_HEADS):
                rows, hs = block(n, h)
                fw, blast = gates(rows, h)
                for name in KEEP_NAMES:
                    keep[name][rows, hs] = fw[name]
                ab_sc[rows, hs] = (fw["qs"] * fw["ea"]).astype(BF16)
                bkb_sc[rows, hs] = (fw["k"] * fw["eb"]).astype(BF16)
                qinb_sc[rows, hs] = (fw["qs"] * fw["eq"]).astype(BF16)
                koutb_sc[rows, hs] = (fw["k"] * fw["ek"]).astype(BF16)
                dec_sc[n:n + 1, hs] = jnp.exp(blast)
                o = opre_ref[rows, hs]
                g = p_ref[rows, C_HG + h * HG_F:C_HG + (h + 1) * HG_F].astype(F32)
                sg = _sigmoid(g)
                r = lax.rsqrt(jnp.mean(o * o, axis=-1, keepdims=True) + RMS_EPS)
                dyb = dy_ref[rows, W + h * HG_F:W + (h + 1) * HG_F].astype(F32)
                dp_ref[rows, C_HG + h * HG_F:C_HG + (h + 1) * HG_F] = (
                    dyb * (o * r * nw) * (sg * (1.0 + g * (1.0 - sg)))).astype(BF16)
                don = dyb * (g * sg)
                dnw = dnw + jnp.sum(don * o * r, axis=0, keepdims=True)
                dn = don * nw
                dob_sc[rows, hs] = (r * (dn - o * (r * r) * jnp.mean(dn * o, axis=-1, keepdims=True))).astype(BF16)
        dnw_ref[0:1, :] += dnw

        heads = [slice(h * HG_F, (h + 1) * HG_F) for h in range(HG_HEADS)]
        scores = [_dot(ab_sc[:, hs], bkb_sc[:, hs], NT) for hs in heads]
        dscores = [_dot(dob_sc[:, hs], p_ref[:, C_HI + h * HG_F:C_HI + (h + 1) * HG_F].astype(BF16), NT)
                   for h, hs in enumerate(heads)]
        scores = [jnp.where(mask, s, 0.0).astype(BF16) for s in scores]
        dscores = [jnp.where(mask, s, 0.0).astype(BF16) for s in dscores]
        for h, hs in enumerate(heads):
            dv_sc[:, hs] = _dot(scores[h], dob_sc[:, hs], TN)
            da_sc[:, hs] = _dot(dscores[h], bkb_sc[:, hs], NN)
            dbk_sc[:, hs] = _dot(dscores[h], ab_sc[:, hs], TN)
        dst = [dstate_sc[h] for h in range(HG_HEADS)]
        for n in reversed(range(nc)):
            for h in range(HG_HEADS):
                rows, hs = block(n, h)
                st_n = st_ref[n, h]
                decay = dec_sc[n:n + 1, hs]
                dstb = dst[h].astype(BF16)
                dob_n = dob_sc[rows, hs]
                dv_sc[rows, hs] += _dot(koutb_sc[rows, hs], dstb, NT)
                dkout_sc[rows, hs] = _dot(p_ref[rows, C_HI + h * HG_F:C_HI + (h + 1) * HG_F].astype(BF16), dstb, NN)
                ddec_sc[n:n + 1, hs] = jnp.sum(dst[h] * st_n, axis=0, keepdims=True) * decay
                dqin_sc[rows, hs] = _dot(dob_n, st_n.astype(BF16), NN)
                dst[h] = dst[h] * decay + _dot(dob_n, qinb_sc[rows, hs], TN)
        for h in range(HG_HEADS):
            dstate_sc[h] = dst[h]

        for h in range(HG_HEADS):
            dlb = jnp.zeros((1, HG_F), F32)
            for n in range(nc):
                rows, hs = block(n, h)
                fw = {name: keep[name][rows, hs] for name in KEEP_NAMES}
                lbh = lb_ref[:, h * HG_F:(h + 1) * HG_F]
                q = p_ref[rows, C_HQ + h * HG_F:C_HQ + (h + 1) * HG_F].astype(F32)
                da, dbk, dqin, dkout = da_sc[rows, hs], dbk_sc[rows, hs], dqin_sc[rows, hs], dkout_sc[rows, hs]
                w_a, w_b, w_q, w_k = da * fw["ea"], dbk * fw["eb"], dqin * fw["eq"], dkout * fw["ek"]
                dqs, dk = w_a + w_q, w_b + w_k
                t_a, t_b, t_q, t_k = w_a * fw["qs"], w_b * fw["k"], w_q * fw["qs"], w_k * fw["k"]
                s_ref = jnp.sum(t_b - t_a, axis=0, keepdims=True)
                s_last = jnp.sum(t_k, axis=0, keepdims=True) + ddec_sc[n:n + 1, hs]
                dbc = (t_a - t_b + t_q - t_k) + jnp.where(pos_c == mid, s_ref, 0.0) + jnp.where(pos_c == last, s_last, 0.0)
                dfk = _seg_rev_cumsum(dbc, pos_c) / fw["f"] - dk
                sig, sq = fw["sig"], fw["sq"]
                dp_ref[rows, C_HF + h * HG_F:C_HF + (h + 1) * HG_F] = (dfk * (1.0 - lbh) * sig * (1.0 - sig)).astype(BF16)
                dlb = dlb + jnp.sum(dfk * (1.0 - sig), axis=0, keepdims=True)
                dp_ref[rows, C_HQ + h * HG_F:C_HQ + (h + 1) * HG_F] = (dqs * (sq * (1.0 + q * (1.0 - sq)))).astype(BF16)
                dp_ref[rows, C_HI + h * HG_F:C_HI + (h + 1) * HG_F] = dv_sc[rows, hs].astype(BF16)
            dlb_ref[0:1, h * HG_F:(h + 1) * HG_F] += dlb

        mheads = [slice(h * MEM_HEAD_DIM, (h + 1) * MEM_HEAD_DIM) for h in range(MEM_HEADS)]
        qhs = [p_ref[:, C_MQ + h * MEM_HEAD_DIM:C_MQ + (h + 1) * MEM_HEAD_DIM].astype(BF16) for h in range(MEM_HEADS)]
        dobs = [dy_ref[:, 2 * W + h * MEM_HEAD_DIM:2 * W + (h + 1) * MEM_HEAD_DIM].astype(BF16) for h in range(MEM_HEADS)]
        probs = [_attn_probs(qhs[h], mk_ref[:, hs]) for h, hs in enumerate(mheads)]
        dprobs = [_dot(dobs[h], mv_ref[:, hs], NT) for h, hs in enumerate(mheads)]
        for h, hs in enumerate(mheads):
            prob = probs[h]
            dmv_ref[:, hs] += _dot(prob.astype(BF16), dobs[h], TN)
            ds = prob * (dprobs[h] - jnp.sum(dprobs[h] * prob, axis=-1, keepdims=True)) * (MEM_HEAD_DIM ** -0.5)
            dsb = ds.astype(BF16)
            dp_ref[:, C_MQ + h * MEM_HEAD_DIM:C_MQ + (h + 1) * MEM_HEAD_DIM] = _dot(dsb, mk_ref[:, hs], NN).astype(BF16)
            dmk_ref[:, hs] += _dot(dsb, qhs[h], TN)

    def tile(b, s):
        return b * ns + (ns - 1 - s)

    return pl.pallas_call(
        body,
        name="mixer_bwd",
        grid=(bl, ns),
        in_specs=[
            pl.BlockSpec((ts, N_MIX), lambda b, s: (tile(b, s), 0)),
            pl.BlockSpec((PREV_ROWS, N_MIX), lambda b, s: (jnp.maximum(tile(b, s) * (ts // PREV_ROWS) - 1, 0), 0)),
            pl.BlockSpec((ts, 3 * W), lambda b, s: (tile(b, s), 0)),
            pl.BlockSpec(memory_space=pl.ANY),
            pl.BlockSpec((nc, HG_HEADS, HG_F, HG_F), lambda b, s: (tile(b, s), 0, 0, 0)),
            pl.BlockSpec((ts, W), lambda b, s: (tile(b, s), 0)),
            pl.BlockSpec((ml, W), lambda b, s: (b, 0)),
            pl.BlockSpec((ml, W), lambda b, s: (b, 0)),
            pl.BlockSpec((1, W), lambda b, s: (0, 0)),
            pl.BlockSpec((CONV_K, W), lambda b, s: (0, 0)),
            pl.BlockSpec((1, HG_F), lambda b, s: (0, 0)),
        ] + [ANY_SPEC] * len(deps),
        out_specs=[
            pl.BlockSpec((ts, N_MIX), lambda b, s: (tile(b, s), 0)),
            pl.BlockSpec((ml, W), lambda b, s: (b, 0)),
            pl.BlockSpec((ml, W), lambda b, s: (b, 0)),
            pl.BlockSpec((8, W), lambda b, s: (0, 0)),
            pl.BlockSpec((8, HG_F), lambda b, s: (0, 0)),
            pl.BlockSpec((8, W), lambda b, s: (0, 0)),
        ],
        out_shape=[
            jax.ShapeDtypeStruct((T, nin), BF16),
            jax.ShapeDtypeStruct((bl * ml, W), F32),
            jax.ShapeDtypeStruct((bl * ml, W), F32),
            jax.ShapeDtypeStruct((8, W), F32),
            jax.ShapeDtypeStruct((8, HG_F), F32),
            jax.ShapeDtypeStruct((8, W), F32),
        ],
        input_output_aliases={3: 0},
        scratch_shapes=[pltpu.VMEM((HG_HEADS, HG_F, HG_F), F32), pltpu.VMEM((8, W), F32), pltpu.VMEM((PREV_ROWS, W), F32)]
        + [pltpu.VMEM((ts, W), BF16)] * 5 + [pltpu.VMEM((ts, W), F32)] * 5 + [pltpu.VMEM((nc, W), F32)] * 2
        + [pltpu.VMEM((ts, W), F32)] * len(KEEP_NAMES),
        compiler_params=_cparams(("arbitrary", "arbitrary")),
    )(p, p, dy, dp_gates, st, opre, mk, mv, lb, conv_w, norm_w, *deps)


def _layer_norm_stats(z):
    mu = jnp.mean(z, axis=-1, keepdims=True)
    zc = z - mu
    rstd = lax.rsqrt(jnp.mean(zc * zc, axis=-1, keepdims=True) + LN_EPS)
    return zc * rstd, rstd


def _gate_specs(tm, d):
    g0 = N_MIX // d
    return [pl.BlockSpec((tm, d), functools.partial(lambda i, k: (i, g0 + k), k=k)) for k in range(N_BRANCH)]


def _merge_fwd(y, p, x0, wb, wo, bg, ln_g, ln_b, *, alpha, tm=512):
    T, d = x0.shape
    assert N_MIX % d == 0
    tm = _pick(T, (tm, 128, 8))

    def body(y_ref, g0_ref, g1_ref, g2_ref, x_ref, wb_ref, wo_ref, bg_ref, lg_ref, lb_ref, mg_ref, xh_ref, rs_ref, x1b_ref):
        merged = None
        for i, g_ref in enumerate((g0_ref, g1_ref, g2_ref)):
            r = _dot(y_ref[:, i * W:(i + 1) * W], wb_ref[i * W:(i + 1) * W, :], NN)
            t = _sigmoid(g_ref[...].astype(F32) + bg_ref[:, i * d:(i + 1) * d]) * r
            merged = t if merged is None else merged + t
        mb = merged.astype(BF16)
        mg_ref[...] = mb
        z = alpha * x_ref[...] + _dot(mb, wo_ref[...], NN)
        xh, rs = _layer_norm_stats(z)
        xh_ref[...], rs_ref[...] = xh, rs
        x1b_ref[...] = (xh * lg_ref[...] + lb_ref[...]).astype(BF16)

    row = lambda i: (i, 0)
    fix = lambda i: (0, 0)
    return pl.pallas_call(
        body,
        name="merge_fwd",
        grid=(T // tm,),
        in_specs=[pl.BlockSpec((tm, 3 * W), row)] + _gate_specs(tm, d) + [
            pl.BlockSpec((tm, d), row), pl.BlockSpec((3 * W, d), fix, pipeline_mode=pl.Buffered(1)),
            pl.BlockSpec((d, d), fix, pipeline_mode=pl.Buffered(1)), pl.BlockSpec((1, 3 * d), fix),
            pl.BlockSpec((1, d), fix), pl.BlockSpec((1, d), fix)],
        out_specs=[pl.BlockSpec((tm, d), row), pl.BlockSpec((tm, d), row), pl.BlockSpec((tm, 1), row), pl.BlockSpec((tm, d), row)],
        out_shape=[jax.ShapeDtypeStruct((T, d), BF16), jax.ShapeDtypeStruct((T, d), F32), jax.ShapeDtypeStruct((T, 1), F32),
                   jax.ShapeDtypeStruct((T, d), BF16)],
        compiler_params=_cparams(("parallel",)),
    )(y, p, p, p, x0, wb, wo, bg, ln_g, ln_b)


def _merge_bwd(dz, p, y, wb, wo, bg, *, tm=512):
    T, d = dz.shape
    nin = p.shape[1]
    tm = _pick(T, (tm, 128, 8))

    def body(dz_ref, g0_ref, g1_ref, g2_ref, y_ref, wb_ref, wo_ref, bg_ref, dr_ref, dp_ref, dy_ref, dbg_ref):
        @pl.when(pl.program_id(0) == 0)
        def _():
            dbg_ref[...] = jnp.zeros_like(dbg_ref)

        dmerged = _dot(dz_ref[...].astype(BF16), wo_ref[...], NT)
        dp_ref[:, 0:N_MIX] = jnp.zeros((tm, N_MIX), BF16)
        for i, g_ref in enumerate((g0_ref, g1_ref, g2_ref)):
            cs = slice(i * d, (i + 1) * d)
            s = _sigmoid(g_ref[...].astype(F32) + bg_ref[:, cs])
            drb = (dmerged * s).astype(BF16)
            dr_ref[:, cs] = drb
            dgate = dmerged * _dot(y_ref[:, i * W:(i + 1) * W], wb_ref[i * W:(i + 1) * W, :], NN) * s * (1.0 - s)
            dp_ref[:, N_MIX + i * d:N_MIX + (i + 1) * d] = dgate.astype(BF16)
            dbg_ref[0:1, cs] += jnp.sum(dgate, axis=0, keepdims=True)
            dy_ref[:, i * W:(i + 1) * W] = _dot(drb, wb_ref[i * W:(i + 1) * W, :], NT).astype(BF16)

    row = lambda i: (i, 0)
    fix = lambda i: (0, 0)
    return pl.pallas_call(
        body,
        name="merge_bwd",
        grid=(T // tm,),
        in_specs=[pl.BlockSpec((tm, d), row)] + _gate_specs(tm, d) + [
            pl.BlockSpec((tm, 3 * W), row), pl.BlockSpec((3 * W, d), fix, pipeline_mode=pl.Buffered(1)),
            pl.BlockSpec((d, d), fix, pipeline_mode=pl.Buffered(1)), pl.BlockSpec((1, 3 * d), fix)],
        out_specs=[pl.BlockSpec((tm, 3 * d), row), pl.BlockSpec((tm, nin), row), pl.BlockSpec((tm, 3 * W), row),
                   pl.BlockSpec((8, 3 * d), fix)],
        out_shape=[jax.ShapeDtypeStruct((T, 3 * d), BF16), jax.ShapeDtypeStruct((T, nin), BF16),
                   jax.ShapeDtypeStruct((T, 3 * W), BF16), jax.ShapeDtypeStruct((8, 3 * d), F32)],
        compiler_params=_cparams(("arbitrary",)),
    )(dz, p, p, p, y, wb, wo, bg)


MLP_VMEM_LIMIT = 58 * 1024 * 1024


def _mlp_fwd(xhat1, x1b, g1, b1, wu, wd, g2, b2, *, alpha, tm=512, tf=1024):
    T, d = xhat1.shape
    ff = wu.shape[1]
    tm, tf = _pick(T, (tm, 256, 128, 8)), _pick(ff, (tf, 1024, 512, 256, 128))

    def body(xh_ref, x1b_ref, g1_ref, b1_ref, wu_ref, wd_ref, g2_ref, b2_ref, a_ref, xh2_ref, rs2_ref, x2_ref, x2b_ref):
        xb = x1b_ref[...]
        acc = None
        a = _dot(xb, wu_ref[:, 0:tf], NN)
        for c0 in range(0, ff, tf):
            a_next = _dot(xb, wu_ref[:, c0 + tf:c0 + 2 * tf], NN) if c0 + tf < ff else None
            a_ref[:, c0:c0 + tf] = a.astype(BF16)
            part = _dot(jnp.square(jnp.maximum(a, 0.0)).astype(BF16), wd_ref[c0:c0 + tf, :], NN)
            acc = part if acc is None else acc + part
            a = a_next
        x1 = xh_ref[...] * g1_ref[...] + b1_ref[...]
        xh2, rs2 = _layer_norm_stats(alpha * x1 + acc)
        xh2_ref[...] = xh2
        rs2_ref[...] = rs2
        x2 = xh2 * g2_ref[...] + b2_ref[...]
        x2_ref[...] = x2
        x2b_ref[...] = x2.astype(BF16)

    row = lambda i: (i, 0)
    fix = lambda i: (0, 0)
    once = dict(pipeline_mode=pl.Buffered(1))
    return pl.pallas_call(
        body,
        name="mlp_fwd",
        grid=(T // tm,),
        in_specs=[pl.BlockSpec((tm, d), row), pl.BlockSpec((tm, d), row), pl.BlockSpec((1, d), fix), pl.BlockSpec((1, d), fix),
                  pl.BlockSpec((d, ff), fix, **once), pl.BlockSpec((ff, d), fix, **once),
                  pl.BlockSpec((1, d), fix), pl.BlockSpec((1, d), fix)],
        out_specs=[pl.BlockSpec((tm, ff), row), pl.BlockSpec((tm, d), row), pl.BlockSpec((tm, 1), row),
                   pl.BlockSpec((tm, d), row), pl.BlockSpec((tm, d), row)],
        out_shape=[jax.ShapeDtypeStruct((T, ff), BF16), jax.ShapeDtypeStruct((T, d), F32), jax.ShapeDtypeStruct((T, 1), F32),
                   jax.ShapeDtypeStruct((T, d), F32), jax.ShapeDtypeStruct((T, d), BF16)],
        compiler_params=pltpu.CompilerParams(dimension_semantics=("parallel",), vmem_limit_bytes=MLP_VMEM_LIMIT),
    )(xhat1, x1b, g1, b1, wu, wd, g2, b2)


def _ln_bwd(dy, xhat, rstd, g, *, tm=1024, deps=()):
    T, d = dy.shape
    tm = _pick(T, (tm, 256, 128, 8))

    def body(dy_ref, xh_ref, rs_ref, g_ref, *rest):
        dz_ref, dzb_ref, dg_ref, db_ref = rest[len(deps):]

        @pl.when(pl.program_id(0) == 0)
        def _():
            dg_ref[...] = jnp.zeros_like(dg_ref)
            db_ref[...] = jnp.zeros_like(db_ref)

        dy_, xh = dy_ref[...], xh_ref[...]
        dg_ref[0:1, :] += jnp.sum(dy_ * xh, axis=0, keepdims=True)
        db_ref[0:1, :] += jnp.sum(dy_, axis=0, keepdims=True)
        dxh = dy_ * g_ref[...]
        dz = rs_ref[...] * (dxh - jnp.mean(dxh, axis=-1, keepdims=True) - xh * jnp.mean(dxh * xh, axis=-1, keepdims=True))
        dz_ref[...] = dz
        dzb_ref[...] = dz.astype(BF16)

    row = lambda i: (i, 0)
    fix = lambda i: (0, 0)
    return pl.pallas_call(
        body,
        name="ln_bwd",
        grid=(T // tm,),
        in_specs=[pl.BlockSpec((tm, d), row), pl.BlockSpec((tm, d), row), pl.BlockSpec((tm, 1), row), pl.BlockSpec((1, d), fix)]
        + [ANY_SPEC] * len(deps),
        out_specs=[pl.BlockSpec((tm, d), row), pl.BlockSpec((tm, d), row), pl.BlockSpec((8, d), fix), pl.BlockSpec((8, d), fix)],
        out_shape=[jax.ShapeDtypeStruct((T, d), F32), jax.ShapeDtypeStruct((T, d), BF16), jax.ShapeDtypeStruct((8, d), F32),
                   jax.ShapeDtypeStruct((8, d), F32)],
        compiler_params=_cparams(("arbitrary",)),
    )(dy, xhat, rstd, g, *deps)


def _loss_head(y, target, xhat, rstd, g, *, tm=512):
    T, d = y.shape
    tm = _pick(T, (tm, 256, 128, 8))
    n = T // tm

    def body(y_ref, t_ref, xh_ref, rs_ref, g_ref, loss_ref, dz_ref, dzb_ref, dg_ref, db_ref, acc_ref):
        i = pl.program_id(0)

        @pl.when(i == 0)
        def _():
            acc_ref[...] = jnp.zeros_like(acc_ref)
            dg_ref[...] = jnp.zeros_like(dg_ref)
            db_ref[...] = jnp.zeros_like(db_ref)

        e = y_ref[...] - t_ref[...]
        acc_ref[...] += jnp.sum(e * e, axis=0, keepdims=True)
        dy_, xh = e * (1.0 / d), xh_ref[...]
        dg_ref[0:1, :] += jnp.sum(dy_ * xh, axis=0, keepdims=True)
        db_ref[0:1, :] += jnp.sum(dy_, axis=0, keepdims=True)
        dxh = dy_ * g_ref[...]
        dz = rs_ref[...] * (dxh - jnp.mean(dxh, axis=-1, keepdims=True) - xh * jnp.mean(dxh * xh, axis=-1, keepdims=True))
        dz_ref[...] = dz
        dzb_ref[...] = dz.astype(BF16)

        @pl.when(i == n - 1)
        def _():
            loss_ref[...] = (0.5 / d) * jnp.sum(acc_ref[...], axis=1, keepdims=True)

    row = lambda i: (i, 0)
    fix = lambda i: (0, 0)
    return pl.pallas_call(
        body,
        name="loss_head",
        grid=(n,),
        in_specs=[pl.BlockSpec((tm, d), row), pl.BlockSpec((tm, d), row), pl.BlockSpec((tm, d), row), pl.BlockSpec((tm, 1), row),
                  pl.BlockSpec((1, d), fix)],
        out_specs=[pl.BlockSpec((1, 1), fix), pl.BlockSpec((tm, d), row), pl.BlockSpec((tm, d), row), pl.BlockSpec((8, d), fix),
                   pl.BlockSpec((8, d), fix)],
        out_shape=[jax.ShapeDtypeStruct((1, 1), F32), jax.ShapeDtypeStruct((T, d), F32), jax.ShapeDtypeStruct((T, d), BF16),
                   jax.ShapeDtypeStruct((8, d), F32), jax.ShapeDtypeStruct((8, d), F32)],
        scratch_shapes=[pltpu.VMEM((1, d), F32)],
        compiler_params=_cparams(("arbitrary",)),
    )(y, target, xhat, rstd, g)


def _lower_bounds_fwd(lower_bounds):
    depth, n = lower_bounds.shape

    def body(x_ref, soft_ref, lb_ref):
        x = x_ref[...]
        e = jnp.exp(x - jnp.max(x, axis=0, keepdims=True))
        soft_ref[...] = e / jnp.sum(e, axis=0, keepdims=True)
        run = None
        for l in range(depth):
            run = soft_ref[l:l + 1, :] if run is None else run + soft_ref[l:l + 1, :]
            lb_ref[l:l + 1, :] = run - soft_ref[0:1, :]

    return pl.pallas_call(body, name="lower_bounds_fwd",
                          out_shape=[jax.ShapeDtypeStruct((depth, n), F32), jax.ShapeDtypeStruct((depth, n), F32)])(lower_bounds)


def _lower_bounds_bwd(soft, dlb):
    depth, n = soft.shape

    def body(soft_ref, dlb_ref, out_ref, dsoft_ref):
        total = jnp.sum(dlb_ref[...], axis=0, keepdims=True)
        run = None
        for l in reversed(range(depth)):
            run = dlb_ref[l:l + 1, :] if run is None else run + dlb_ref[l:l + 1, :]
            dsoft_ref[l:l + 1, :] = run - total if l == 0 else run
        s, ds = soft_ref[...], dsoft_ref[...]
        out_ref[...] = s * (ds - jnp.sum(s * ds, axis=0, keepdims=True))

    return pl.pallas_call(body, name="lower_bounds_bwd", out_shape=jax.ShapeDtypeStruct((depth, n), F32),
                          scratch_shapes=[pltpu.VMEM((depth, n), F32)])(soft, dlb)


def _layer_fwd(x0, x0b, mem2, lb, w_in, mix_fn, late_fn, *, bl, seq, alpha, deps=()):
    p = _matmul("proj_in", x0b, w_in, mode="nn", out_dtype=BF16, deps=deps, tm=1024, tn=1792)
    wts = dict(mix_fn(p), w_in=w_in)
    mk, mv = _mem_proj(mem2, wts["w_mem_k"], wts["w_mem_v"])
    y, st, opre = _mixer_fwd(p, mk, mv, lb, wts["conv_w"], wts["hg_norm_w"], bl=bl, seq=seq)
    wts.update(late_fn(y))
    merged, xhat1, rstd1, x1b = _merge_fwd(y, p, x0, wts["w_branch"], wts["w_o"], wts["b_gate"], wts["ln1_g"], wts["ln1_b"],
                                           alpha=alpha)
    a, xhat2, rstd2, x2, x2b = _mlp_fwd(xhat1, x1b, wts["ln1_g"], wts["ln1_b"], wts["w_up"], wts["w_down"], wts["ln2_g"],
                                        wts["ln2_b"], alpha=alpha)
    saved = dict(x0b=x0b, p=p, mk=mk, mv=mv, y=y, st=st, opre=opre, merged=merged, xhat1=xhat1, rstd1=rstd1, x1b=x1b, a=a,
                 xhat2=xhat2, rstd2=rstd2)
    return x2, x2b, saved, wts


def _mem_proj(mem2, wk, wv):
    def body(m_ref, wk_ref, wv_ref, k_ref, v_ref):
        m = m_ref[...].astype(BF16)
        k_ref[...] = _dot(m, wk_ref[...], NN).astype(BF16)
        v_ref[...] = _dot(m, wv_ref[...], NN).astype(BF16)

    shape = jax.ShapeDtypeStruct((mem2.shape[0], wk.shape[1]), BF16)
    return pl.pallas_call(body, name="mem_proj", out_shape=[shape, shape], compiler_params=pltpu.CompilerParams(vmem_limit_bytes=VMEM_LIMIT))(mem2, wk, wv)


def _mem_grads(mem2, dmk, dmv):
    def body(m_ref, dk_ref, dv_ref, gk_ref, gv_ref):
        m = m_ref[...].astype(BF16)
        gk_ref[...] = _dot(m, dk_ref[...].astype(BF16), TN).astype(BF16)
        gv_ref[...] = _dot(m, dv_ref[...].astype(BF16), TN).astype(BF16)

    shape = jax.ShapeDtypeStruct((mem2.shape[1], dmk.shape[1]), BF16)
    return pl.pallas_call(body, name="mem_grads", out_shape=[shape, shape], compiler_params=pltpu.CompilerParams(vmem_limit_bytes=VMEM_LIMIT))(mem2, dmk, dmv)


def _relu2_bf16(a):
    return jnp.square(jnp.maximum(a.astype(F32), 0.0)).astype(BF16)


def _mlp_bwd(dz2, dz2b, sv, wts, *, alpha, deps=()):
    g = {}
    da = _matmul("mlp_da", dz2b, wts["w_down"], mode="nt", out_dtype=BF16, tm=512, tn=wts["w_down"].shape[0], deps=deps,
                 epi_fn=lambda acc, a: (acc * (2.0 * jnp.maximum(a.astype(F32), 0.0)),), epi_extra=(sv["a"],))
    g["w_down"] = _matmul_tn("grad_w_down", sv["a"], dz2b, a_fn=_relu2_bf16, out_dtype=BF16, tt=2048)
    g["w_up"] = _matmul_tn("grad_w_up", sv["x1b"], da, out_dtype=BF16, tt=2048)
    dx1 = _matmul("mlp_dx", da, wts["w_up"], mode="nt", epi_fn=lambda acc, dz: (acc + alpha * dz,), epi_extra=(dz2,),
                  tm=512, tk=4096)
    dz1, dz1b, dg1, db1 = _ln_bwd(dx1, sv["xhat1"], sv["rstd1"], wts["ln1_g"])
    g["ln1_g"], g["ln1_b"] = dg1[0:1], db1[0:1]
    return dz1, dz1b, g


def _mix_bwd(dz1, dz1b, sv, mem2, lb, wts, *, bl, seq, alpha, send, below=None, deps=()):
    d = dz1.shape[1]
    g = {}
    g["w_o"] = _matmul_tn("grad_w_o", sv["merged"], dz1b, out_dtype=BF16, tt=2048, deps=deps)
    dr, dp, dy, dbg = _merge_bwd(dz1b, sv["p"], sv["y"], wts["w_branch"], wts["w_o"], wts["b_gate"])
    g["b_gate"] = dbg[0:1]
    g["w_branch"] = _matmul_tn("grad_w_branch", sv["y"], dr, ta=W, tb=d, tt=4096, out_dtype=BF16, paired=True)
    token = send(("w_o", "w_branch"), g)
    dp, dmk, dmv, dcw, dnw, dlb = _mixer_bwd(sv["p"], dy, dp, sv["st"], sv["opre"], sv["mk"], sv["mv"], lb,
                                              wts["conv_w"], wts["hg_norm_w"], bl=bl, seq=seq, deps=(token,))
    g["conv_w"], g["hg_norm_w"], g["lb"] = dcw[0:CONV_K], dnw[0:1], dlb[0:1]
    g["w_mem_k"], g["w_mem_v"] = _mem_grads(mem2, dmk, dmv)
    g["w_in"] = _matmul_tn("grad_w_in", sv["x0b"], dp, out_dtype=BF16, tt=2048)
    token = send(("w_in", "w_mem_k", "w_mem_v", "conv_w"), g)
    dx0 = _matmul("proj_in_dx", dp, wts["w_in"], mode="nt", epi_fn=lambda acc, dz: (acc + alpha * dz,), epi_extra=(dz1,),
                  tm=512, tk=dp.shape[1], deps=(token,))
    return (dx0 if below is None else _ln_bwd(dx0, *below)), g


N_CHIPS = 4
MESH_IDS = pl.DeviceIdType.MESH


def _axis_slice(ref, axis, start, size):
    idx = [slice(None)] * len(ref.shape)
    idx[axis] = pl.ds(start, size)
    return ref.at[tuple(idx)]


def _chip_exchange(name, items):
    n = len(items)
    out_shapes, meta = [], []
    for arr, kind, axis in items:
        shp = list(arr.shape)
        if kind == "gather":
            per = shp[axis]
            shp[axis] = per * N_CHIPS
            out_shapes.append(jax.ShapeDtypeStruct(tuple(shp), arr.dtype))
        elif kind == "scatter":
            per = shp[axis] // N_CHIPS
            shp[axis] = per
            out_shapes.append(jax.ShapeDtypeStruct((N_CHIPS, *shp), arr.dtype))
        else:
            per = None
            out_shapes.append(jax.ShapeDtypeStruct((N_CHIPS, *shp), arr.dtype))
        meta.append((kind, axis, per))

    def body(*refs):
        ins, outs = refs[:n], refs[n:2 * n]
        send_sems, recv_sems, local_sems = refs[2 * n:]
        x, y, c = lax.axis_index("x"), lax.axis_index("y"), lax.axis_index("c")
        me = 2 * x + y
        peers = [(1 - x, y), (x, 1 - y), (1 - x, 1 - y)]

        def src_for(t, chip):
            kind, axis, per = meta[t]
            return _axis_slice(ins[t], axis, chip * per, per) if kind == "scatter" else ins[t]

        def dst_from(t, chip):
            kind, axis, per = meta[t]
            return _axis_slice(outs[t], axis, chip * per, per) if kind == "gather" else outs[t].at[chip]

        def remote(t, k):
            px, py = peers[k]
            return pltpu.make_async_remote_copy(
                src_ref=src_for(t, 2 * px + py), dst_ref=dst_from(t, me), send_sem=send_sems.at[t * 3 + k],
                recv_sem=recv_sems.at[t * 3 + k], device_id=(px, py, c), device_id_type=MESH_IDS)

        def arrival(t, k):
            px, py = peers[k]
            return pltpu.make_async_remote_copy(
                src_ref=src_for(t, me), dst_ref=dst_from(t, 2 * px + py), send_sem=send_sems.at[t * 3 + k],
                recv_sem=recv_sems.at[t * 3 + k], device_id=(px, py, c), device_id_type=MESH_IDS)

        sends = [remote(t, k) for t in range(n) for k in range(3)]
        for cp in sends:
            cp.start()
        own = [pltpu.make_async_copy(src_for(t, me), dst_from(t, me), local_sems.at[t]) for t in range(n)]
        for cp in own:
            cp.start()
        for t in range(n):
            for k in range(3):
                arrival(t, k).wait_recv()
        for cp in sends:
            cp.wait_send()
        for cp in own:
            cp.wait()

    any_spec = pl.BlockSpec(memory_space=pl.ANY)
    return pl.pallas_call(
        body,
        name=name,
        in_specs=[any_spec] * n,
        out_specs=[any_spec] * n,
        out_shape=out_shapes,
        scratch_shapes=[pltpu.SemaphoreType.DMA((3 * n,)), pltpu.SemaphoreType.DMA((3 * n,)), pltpu.SemaphoreType.DMA((n,))],
        compiler_params=pltpu.CompilerParams(has_side_effects=True),
    )(*[a for a, _, _ in items])


HBM_SPEC = pl.BlockSpec(memory_space=pltpu.HBM)
SEM_SPEC = pl.BlockSpec(memory_space=pltpu.SEMAPHORE)
N_PEERS = N_CHIPS - 1


def _my_chip():
    return (2 * lax.axis_index("x") + lax.axis_index("y")).astype(jnp.int32).reshape(1)


def _own_block_spec(r, c, axis, tr):
    if axis == 1:
        return pl.BlockSpec((tr, c), lambda i, me: (i, me[0]))
    return pl.BlockSpec((tr, c), lambda i, me: (me[0] * (r // tr) + i, 0))


def _place_shard(name, shard, axis, me):
    r, c = shard.shape
    tr = _row_block(r, c, shard.dtype.itemsize)
    shp = (r, c * N_CHIPS) if axis == 1 else (r * N_CHIPS, c)

    def body(me_ref, s_ref, buf_ref, o_ref):
        del me_ref, buf_ref
        o_ref[...] = s_ref[...]

    buf = pltpu.with_memory_space_constraint(lax.empty(shp, shard.dtype), pltpu.HBM)
    return pl.pallas_call(
        body, name=name,
        grid_spec=pltpu.PrefetchScalarGridSpec(
            num_scalar_prefetch=1, grid=(r // tr,),
            in_specs=[pl.BlockSpec((tr, c), lambda i, me: (i, 0)), ANY_SPEC], out_specs=_own_block_spec(r, c, axis, tr)),
        out_shape=jax.ShapeDtypeStruct(shp, shard.dtype),
        input_output_aliases={2: 0},
        compiler_params=_cparams(("parallel",)),
    )(me, shard, buf)


class _Split:
    def __init__(self, name, items):
        self.name, self.n = name, len(items)
        self.srcs = [a for a, _, _ in items]
        self.meta, self.land_shapes = [], []
        for arr, kind, axis in items:
            shp = list(arr.shape)
            if kind == "gather":
                per = shp[axis]
                shp[axis] = per * N_CHIPS
                self.land_shapes.append(jax.ShapeDtypeStruct(tuple(shp), arr.dtype))
            else:
                per = shp[axis] // N_CHIPS
                shp[axis] = per
                self.land_shapes.append(jax.ShapeDtypeStruct((N_PEERS, *shp), arr.dtype))
            self.meta.append((kind, axis, per))

    def _src(self, ins, t, chip):
        kind, axis, per = self.meta[t]
        return _axis_slice(ins[t], axis, chip * per, per) if kind == "scatter" else ins[t]

    def _dst(self, lands, t, chip, slot):
        kind, axis, per = self.meta[t]
        return _axis_slice(lands[t], axis, chip * per, per) if kind == "gather" else lands[t].at[slot]

    def landing_zones(self, me):
        return [_place_shard(self.name + "_own", src, axis, me) if kind == "gather" else lax.empty(ls.shape, ls.dtype)
                for src, ls, (kind, axis, _) in zip(self.srcs, self.land_shapes, self.meta)]

    def _copies(self, ins, lands, send_sems, recv_sems, arrivals):
        x, y, c = lax.axis_index("x"), lax.axis_index("y"), lax.axis_index("c")
        me = 2 * x + y
        peers = [(1 - x, y), (x, 1 - y), (1 - x, 1 - y)]
        res = []
        for t in range(self.n):
            for k, (px, py) in enumerate(peers):
                theirs = 2 * px + py
                sems = dict(send_sem=send_sems.at[t * N_PEERS + k], recv_sem=recv_sems.at[t * N_PEERS + k],
                            device_id=(px, py, c), device_id_type=MESH_IDS)
                if arrivals:
                    res.append(pltpu.make_async_remote_copy(src_ref=self._src(ins, t, me), dst_ref=self._dst(lands, t, theirs, k), **sems))
                else:
                    res.append(pltpu.make_async_remote_copy(src_ref=self._src(ins, t, theirs), dst_ref=self._dst(lands, t, me, k), **sems))
        return res

    def start(self, lands, deps=()):
        n, nd = self.n, len(deps)

        def body(*refs):
            ins, lnd = refs[:n], refs[n:2 * n]
            send_sems, recv_sems = refs[2 * n + nd], refs[2 * n + nd + 1]
            token = refs[-1]
            for cp in self._copies(ins, lnd, send_sems, recv_sems, arrivals=False):
                cp.start()
            token[...] = jnp.zeros_like(token)

        hbm = lambda a: pltpu.HBM(a.shape, a.dtype)
        res = pl.pallas_call(
            body, name=self.name + "_start",
            in_specs=[HBM_SPEC] * (2 * n) + [ANY_SPEC] * nd,
            out_specs=[SEM_SPEC, SEM_SPEC] + [HBM_SPEC] * (2 * n) + [pl.BlockSpec(memory_space=pltpu.VMEM)],
            out_shape=[pltpu.SemaphoreType.DMA((N_PEERS * n,)), pltpu.SemaphoreType.DMA((N_PEERS * n,))]
            + [hbm(a) for a in self.srcs] + [hbm(a) for a in self.land_shapes] + [jax.ShapeDtypeStruct((8, 128), F32)],
            input_output_aliases={i: 2 + i for i in range(2 * n)},
            compiler_params=pltpu.CompilerParams(has_side_effects=pltpu.SideEffectType.DATAFLOW_SIDE_EFFECTING),
        )(*[pltpu.with_memory_space_constraint(a, pltpu.HBM) for a in self.srcs],
          *[pltpu.with_memory_space_constraint(a, pltpu.HBM) for a in lands], *deps)
        return res[:-1], res[-1]

    def wait(self, state, after):
        n = self.n
        after = tuple(after) if isinstance(after, (tuple, list)) else (after,)
        send_sems, recv_sems = state[0], state[1]
        srcs, lands = state[2:2 + n], state[2 + n:2 + 2 * n]

        def body(*refs):
            ins, lnd = refs[:n], refs[n:2 * n]
            s_sems, r_sems = refs[2 * n], refs[2 * n + 1]
            for cp in self._copies(ins, lnd, s_sems, r_sems, arrivals=True):
                cp.wait_recv()
            for cp in self._copies(ins, lnd, s_sems, r_sems, arrivals=False):
                cp.wait_send()

        hbm = lambda a: pltpu.HBM(a.shape, a.dtype)
        res = pl.pallas_call(
            body, name=self.name + "_wait",
            in_specs=[HBM_SPEC] * (2 * n) + [SEM_SPEC, SEM_SPEC] + [ANY_SPEC] * len(after),
            out_specs=[HBM_SPEC] * (2 * n),
            out_shape=[hbm(a) for a in self.srcs] + [hbm(a) for a in self.land_shapes],
            input_output_aliases={i: i for i in range(2 * n)},
            compiler_params=pltpu.CompilerParams(has_side_effects=pltpu.SideEffectType.DATAFLOW_SIDE_EFFECTING),
        )(*srcs, *lands, send_sems, recv_sems, *after)
        return res[:n], res[n:]


class _SiblingSplit:
    def __init__(self, name, arrays):
        self.name, self.n, self.arrays = name, len(arrays), list(arrays)

    def _copies(self, ins, lands, send_sems, recv_sems):
        sibling = (lax.axis_index("x"), lax.axis_index("y"), 1 - lax.axis_index("c"))
        return [pltpu.make_async_remote_copy(src_ref=ins[t], dst_ref=lands[t], send_sem=send_sems.at[t], recv_sem=recv_sems.at[t],
                                             device_id=sibling, device_id_type=MESH_IDS) for t in range(self.n)]

    def start(self, deps=()):
        n, nd = self.n, len(deps)

        def body(*refs):
            for cp in self._copies(refs[:n], refs[n:2 * n], refs[2 * n + nd], refs[2 * n + nd + 1]):
                cp.start()
            refs[-1][...] = jnp.zeros_like(refs[-1])

        hbm = [pltpu.HBM(a.shape, a.dtype) for a in self.arrays]
        res = pl.pallas_call(
            body, name=self.name + "_start",
            in_specs=[HBM_SPEC] * (2 * n) + [ANY_SPEC] * nd,
            out_specs=[SEM_SPEC, SEM_SPEC] + [HBM_SPEC] * (2 * n) + [pl.BlockSpec(memory_space=pltpu.VMEM)],
            out_shape=[pltpu.SemaphoreType.DMA((n,)), pltpu.SemaphoreType.DMA((n,))] + hbm + hbm + [jax.ShapeDtypeStruct((8, 128), F32)],
            input_output_aliases={i: 2 + i for i in range(2 * n)},
            compiler_params=pltpu.CompilerParams(has_side_effects=pltpu.SideEffectType.DATAFLOW_SIDE_EFFECTING),
        )(*[pltpu.with_memory_space_constraint(a, pltpu.HBM) for a in self.arrays],
          *[pltpu.with_memory_space_constraint(lax.empty(a.shape, a.dtype), pltpu.HBM) for a in self.arrays], *deps)
        return res[:-1], res[-1]

    def wait(self, state, after):
        n = self.n
        after = tuple(after) if isinstance(after, (tuple, list)) else (after,)

        def body(*refs):
            for cp in self._copies(refs[:n], refs[n:2 * n], refs[2 * n], refs[2 * n + 1]):
                cp.wait()

        hbm = [pltpu.HBM(a.shape, a.dtype) for a in self.arrays]
        res = pl.pallas_call(
            body, name=self.name + "_wait",
            in_specs=[HBM_SPEC] * (2 * n) + [SEM_SPEC, SEM_SPEC] + [ANY_SPEC] * len(after),
            out_specs=[HBM_SPEC] * (2 * n),
            out_shape=hbm + hbm,
            input_output_aliases={i: i for i in range(2 * n)},
            compiler_params=pltpu.CompilerParams(has_side_effects=pltpu.SideEffectType.DATAFLOW_SIDE_EFFECTING),
        )(*state[2:2 + 2 * n], state[0], state[1], *after)
        return res[:n], res[n:]


def _sibling_swap(name, arrays):
    n = len(arrays)

    def body(*refs):
        ins, outs = refs[:n], refs[n:2 * n]
        send_sems, recv_sems = refs[2 * n:]
        sibling = (lax.axis_index("x"), lax.axis_index("y"), 1 - lax.axis_index("c"))
        copies = [pltpu.make_async_remote_copy(src_ref=ins[t], dst_ref=outs[t], send_sem=send_sems.at[t], recv_sem=recv_sems.at[t],
                                               device_id=sibling, device_id_type=MESH_IDS) for t in range(n)]
        for cp in copies:
            cp.start()
        for cp in copies:
            cp.wait()

    any_spec = pl.BlockSpec(memory_space=pl.ANY)
    return pl.pallas_call(
        body,
        name=name,
        in_specs=[any_spec] * n,
        out_specs=[any_spec] * n,
        out_shape=[jax.ShapeDtypeStruct(a.shape, a.dtype) for a in arrays],
        scratch_shapes=[pltpu.SemaphoreType.DMA((n,)), pltpu.SemaphoreType.DMA((n,))],
        compiler_params=pltpu.CompilerParams(has_side_effects=True),
    )(*arrays)


def _row_block(r, c, itemsize=4, target=1 << 20):
    if r % 8 != 0:
        return r
    best = 8
    for tr in range(8, r + 1, 8):
        if r % tr == 0 and tr * c * itemsize <= target:
            best = tr
    return best


def _sum_chips_into(parts, stacked, layer):
    _, r, c = parts.shape
    tr = _row_block(r, c)

    def body(p_ref, s_ref, o_ref):
        del s_ref
        o_ref[...] = ((p_ref[0] + p_ref[1]) + p_ref[2]) + p_ref[3]

    return pl.pallas_call(
        body,
        name="sum_chips",
        grid=(r // tr,),
        in_specs=[pl.BlockSpec((N_CHIPS, tr, c), lambda i: (0, i, 0)), pl.BlockSpec(memory_space=pl.ANY)],
        out_specs=pl.BlockSpec((None, tr, c), lambda i: (layer, i, 0)),
        out_shape=jax.ShapeDtypeStruct(stacked.shape, stacked.dtype),
        input_output_aliases={1: 0},
        compiler_params=_cparams(("parallel",)),
    )(parts, stacked)


def _sum_own_and_peers(me, g, axis, landed):
    _, r, c = landed.shape
    tr = _row_block(r, c, target=1 << 21)

    def body(me_ref, g_ref, p_ref, o_ref):
        del me_ref
        o_ref[...] = ((g_ref[...].astype(F32) + p_ref[0].astype(F32)) + p_ref[1].astype(F32)) + p_ref[2].astype(F32)

    return pl.pallas_call(
        body, name="sum_chips_own",
        grid_spec=pltpu.PrefetchScalarGridSpec(
            num_scalar_prefetch=1, grid=(r // tr,),
            in_specs=[_own_block_spec(r, c, axis, tr), pl.BlockSpec((N_PEERS, tr, c), lambda i, me: (0, i, 0))],
            out_specs=pl.BlockSpec((tr, c), lambda i, me: (i, 0))),
        out_shape=jax.ShapeDtypeStruct((r, c), F32),
        compiler_params=_cparams(("parallel",)),
    )(me, g, landed)


ADAMW_BLOCK_BYTES = 3 << 19


def _adamw_math(w, m, v, g):
    m_new = ADAM_B1 * m + (1.0 - ADAM_B1) * g
    v_new = ADAM_B2 * v + (1.0 - ADAM_B2) * jnp.square(g)
    m_hat = m_new / (1.0 - ADAM_B1 ** ADAM_STEP)
    v_hat = v_new / (1.0 - ADAM_B2 ** ADAM_STEP)
    return -ADAM_LR * (m_hat / (jnp.sqrt(v_hat) + ADAM_EPS) + ADAM_WD * w), m_new, v_new


def _adamw(w, m, v, g_a, g_b):
    L, r, c = w.shape
    tr = _row_block(r, c, target=ADAMW_BLOCK_BYTES)

    def body(w_ref, m_ref, v_ref, ga_ref, gb_ref, g_ref, d_ref, nm_ref, nv_ref):
        g = ga_ref[...] + gb_ref[...]
        g_ref[...] = g
        d_ref[...], nm_ref[...], nv_ref[...] = _adamw_math(w_ref[...], m_ref[...], v_ref[...], g)

    spec = pl.BlockSpec((None, tr, c), lambda l, i: (l, i, 0))
    return pl.pallas_call(
        body,
        name="adamw",
        grid=(L, r // tr),
        in_specs=[spec] * 5,
        out_specs=[spec] * 4,
        out_shape=[jax.ShapeDtypeStruct(w.shape, F32)] * 4,
        compiler_params=_cparams(("parallel", "parallel")),
    )(w, m, v, g_a, g_b)


def _adamw_layer(w, m, v, g_a, g_b, layer, outs):
    L, r, c = w.shape
    tr = _row_block(r, c, target=ADAMW_BLOCK_BYTES)
    n_prev = 0 if outs is None else 4

    def body(w_ref, m_ref, v_ref, ga_ref, gb_ref, *rest):
        g_ref, d_ref, nm_ref, nv_ref = rest[n_prev:]
        g = ga_ref[...] + gb_ref[...]
        g_ref[...] = g
        d_ref[...], nm_ref[...], nv_ref[...] = _adamw_math(w_ref[...], m_ref[...], v_ref[...], g)

    at_layer = pl.BlockSpec((None, tr, c), lambda i: (layer, i, 0))
    flat = pl.BlockSpec((tr, c), lambda i: (i, 0))
    return pl.pallas_call(
        body,
        name="adamw_layer",
        grid=(r // tr,),
        in_specs=[at_layer] * 3 + [flat] * 2 + [ANY_SPEC] * n_prev,
        out_specs=[at_layer] * 4,
        out_shape=[jax.ShapeDtypeStruct(w.shape, F32)] * 4,
        input_output_aliases={5 + k: k for k in range(n_prev)},
        compiler_params=_cparams(("parallel",)),
    )(w, m, v, g_a, g_b, *(outs or ()))


SHARDED = (("w_in", 1), ("conv_w", 1), ("w_mem_k", 0), ("w_mem_v", 0), ("w_branch", 1), ("w_o", 0), ("w_up", 1), ("w_down", 0))
SMALL = ("lower_bounds", "hg_norm_w", "b_gate", "ln1_g", "ln1_b", "ln2_g", "ln2_b")
WEIGHT_ORDER = ("lower_bounds", "w_in", "conv_w", "hg_norm_w", "w_mem_k", "w_mem_v", "w_branch", "b_gate", "w_o", "ln1_g", "ln1_b",
                "w_up", "w_down", "ln2_g", "ln2_b")


def kernel(x, mem, lower_bounds, w_in, conv_w, hg_norm_w, w_mem_k, w_mem_v, w_branch, b_gate, w_o, ln1_g, ln1_b, w_up, w_down, ln2_g, ln2_b, loss_target, m_lower_bounds, m_w_in, m_conv_w, m_hg_norm_w, m_w_mem_k, m_w_mem_v, m_w_branch, m_b_gate, m_w_o, m_ln1_g, m_ln1_b, m_w_up, m_w_down, m_ln2_g, m_ln2_b, v_lower_bounds, v_w_in, v_conv_w, v_hg_norm_w, v_w_mem_k, v_w_mem_v, v_w_branch, v_b_gate, v_w_o, v_ln1_g, v_ln1_b, v_w_up, v_w_down, v_ln2_g, v_ln2_b):
    bl, seq, d = x.shape
    depth = w_in.shape[0]
    weights = dict(lower_bounds=lower_bounds, w_in=w_in, conv_w=conv_w, hg_norm_w=hg_norm_w, w_mem_k=w_mem_k, w_mem_v=w_mem_v,
                   w_branch=w_branch, b_gate=b_gate, w_o=w_o, ln1_g=ln1_g, ln1_b=ln1_b, w_up=w_up, w_down=w_down, ln2_g=ln2_g, ln2_b=ln2_b)
    mom_m = dict(lower_bounds=m_lower_bounds, w_in=m_w_in, conv_w=m_conv_w, hg_norm_w=m_hg_norm_w, w_mem_k=m_w_mem_k, w_mem_v=m_w_mem_v,
                 w_branch=m_w_branch, b_gate=m_b_gate, w_o=m_w_o, ln1_g=m_ln1_g, ln1_b=m_ln1_b, w_up=m_w_up, w_down=m_w_down,
                 ln2_g=m_ln2_g, ln2_b=m_ln2_b)
    mom_v = dict(lower_bounds=v_lower_bounds, w_in=v_w_in, conv_w=v_conv_w, hg_norm_w=v_hg_norm_w, w_mem_k=v_w_mem_k, w_mem_v=v_w_mem_v,
                 w_branch=v_w_branch, b_gate=v_b_gate, w_o=v_w_o, ln1_g=v_ln1_g, ln1_b=v_ln1_b, w_up=v_w_up, w_down=v_w_down,
                 ln2_g=v_ln2_g, ln2_b=v_ln2_b)

    def shard2d(name, l):
        w = weights[name][l]
        if name == "w_branch":
            return w.reshape(N_BRANCH * W, w.shape[-1]).astype(BF16)
        return w if name == "conv_w" else w.astype(BF16)

    me = _my_chip()

    shard_axis = dict(SHARDED)

    def prepare_exchange(name, kind, items):
        ex = _Split(name, [(arr, kind, shard_axis[nm]) for nm, arr in items])
        return ex, ex.landing_zones(me), [nm for nm, _ in items]

    def launch(prepared, deps=()):
        ex, lands, names = prepared
        state, token = ex.start(lands, deps)
        return ex, state, names, token

    def start_exchange(name, kind, items, deps=()):
        return launch(prepare_exchange(name, kind, items), deps)

    def prepare_gathers(l):
        groups = (("in", ("w_in",)), ("mix", ("conv_w", "w_mem_k", "w_mem_v")), ("rest", ("w_branch", "w_o", "w_up", "w_down")))
        return tuple(prepare_exchange(f"gather_{tag}_l{l}", "gather", [(nm, shard2d(nm, l)) for nm in names]) for tag, names in groups)

    def start_gathers(prepared, deps=()):
        started = []
        for prep in prepared:
            started.append(launch(prep, deps))
            deps = (started[-1][3],)
        return tuple(started)

    def gathered(pend, after):
        ex, state, names, _ = pend
        return dict(zip(names, ex.wait(state, after=after)[1]))

    pending = start_gathers(prepare_gathers(0))
    tokens = tuple(pend[3] for pend in pending)
    tokens, x, mem, loss_target, weights, mom_m, mom_v = lax.optimization_barrier((tokens, x, mem, loss_target, weights, mom_m, mom_v))
    pending = tuple((*pend[:3], tok) for pend, tok in zip(pending, tokens))
    lower_bounds = weights["lower_bounds"]

    x2d, mem2, t2d = x.reshape(bl * seq, d), mem.reshape(-1, d), loss_target.reshape(bl * seq, d)
    alpha = (2.0 * depth) ** 0.25
    soft, lb_all = _lower_bounds_fwd(lower_bounds)

    prepared = [None] + [prepare_gathers(l) for l in range(1, depth)]
    early = [x2d.astype(BF16), lb_all] + [z for prep in prepared[1:] for _, lands, _ in prep for z in lands]

    h, hb, saved, layer_wts = x2d, early[0], [], []
    for l in range(depth):
        first, mix, rest = pending
        w_in_l = gathered(first, early if l == 0 else h)["w_in"]

        def mix_fn(after, l=l, mix=mix):
            return dict(gathered(mix, after), hg_norm_w=weights["hg_norm_w"][l][None, :])

        def late_fn(after, l=l, rest=rest):
            wts = gathered(rest, after)
            for name in ("b_gate", "ln1_g", "ln1_b", "ln2_g", "ln2_b"):
                wts[name] = weights[name][l][None, :]
            return wts

        deps = (rest[3],)
        if l + 1 < depth:
            pending = start_gathers(prepared[l + 1], (w_in_l, rest[3]))
            deps += tuple(pend[3] for pend in pending)
        h, hb, sv, wts = _layer_fwd(h, hb, mem2, lb_all[l:l + 1], w_in_l, mix_fn, late_fn, bl=bl, seq=seq, alpha=alpha, deps=deps)
        saved.append(sv)
        layer_wts.append(wts)
    loss, dz2, dz2b, dg2, db2 = _loss_head(h, t2d, saved[-1]["xhat2"], saved[-1]["rstd2"], layer_wts[-1]["ln2_g"])

    shape3 = {name: (depth, weights[name].size // (depth * weights[name].shape[-1]), weights[name].shape[-1]) for name, _ in SHARDED}
    partial = [dict() for _ in range(depth)]
    smalls = [None] * depth
    outs = {name: None for name, _ in SHARDED}

    def finish_reduce(pend, l, after):
        ex, state, names, _ = pend
        sent, got = ex.wait(state, after=after)
        for nm, g_full, landed in zip(names, sent, got):
            partial[l][nm] = _sum_own_and_peers(me, g_full, shard_axis[nm], landed)

    names_sharded = [name for name, _ in SHARDED]

    def start_swap(l):
        swap = _SiblingSplit(f"swap_partials_l{l}", [partial[l][nm] for nm in names_sharded])
        state, token = swap.start()
        return swap, state, token

    def optimizer_step(l, pend, after):
        swap, state, _ = pend
        mine, theirs = swap.wait(state, after)
        for nm, own, other in zip(names_sharded, mine, theirs):
            outs[nm] = _adamw_layer(weights[nm].reshape(shape3[nm]), mom_m[nm].reshape(shape3[nm]), mom_v[nm].reshape(shape3[nm]),
                                    own, other, l, outs[nm])
        return tuple(outs[nm][0] for nm in names_sharded)

    pending_mix, pending_swap, deps = [], None, ()
    for l in reversed(range(depth)):
        dz1, dz1b, g_mlp = _mlp_bwd(dz2, dz2b, saved[l], layer_wts[l], alpha=alpha, deps=deps)
        g_mlp["ln2_g"], g_mlp["ln2_b"] = dg2[0:1], db2[0:1]
        pending_mlp = start_exchange(f"reduce_mlp_l{l}", "scatter", [(nm, g_mlp[nm]) for nm in ("w_up", "w_down")])
        deps = (pending_mlp[3],)
        if pending_mix:
            for pend in pending_mix:
                finish_reduce(pend, l + 1, dz1)
            pending_swap = start_swap(l + 1)
            deps += (pending_swap[2],)
        pending_mix = []

        def send(names, g, l=l, pending_mix=pending_mix):
            pend = start_exchange(f"reduce_{names[0]}_l{l}", "scatter", [(nm, g[nm]) for nm in names])
            pending_mix.append(pend)
            return pend[3]

        below = (saved[l - 1]["xhat2"], saved[l - 1]["rstd2"], layer_wts[l - 1]["ln2_g"]) if l > 0 else None
        out, g = _mix_bwd(dz1, dz1b, saved[l], mem2, lb_all[l:l + 1], layer_wts[l], bl=bl, seq=seq, alpha=alpha, send=send,
                          below=below, deps=deps)
        if l > 0:
            dz2, dz2b, dg2, db2 = out
        else:
            dh = out
        finish_reduce(pending_mlp, l, out[0] if l > 0 else out)
        deps = ()
        if pending_swap is not None:
            deps = optimizer_step(l + 1, pending_swap, out[0] if l > 0 else out)
            pending_swap = None
        g.update(g_mlp, lower_bounds=g["lb"])
        smalls[l] = jnp.concatenate([g[nm] for nm in SMALL], axis=1)
    small_parts = _chip_exchange("reduce_small", [(jnp.stack(smalls), "bcast", 0)])[0]
    small_sum = _sum_chips_into(small_parts.reshape(N_CHIPS, depth, -1), jnp.zeros((1, depth, small_parts.shape[-1]), F32), 0)
    small_sum = small_sum.reshape(depth, 1, -1)
    small_theirs = _sibling_swap("swap_small", [small_sum])[0]
    for pend in pending_mix:
        finish_reduce(pend, 0, small_theirs)
    optimizer_step(0, start_swap(0), small_theirs)

    outs = {name: [r.reshape(weights[name].shape) for r in res] for name, res in outs.items()}
    off = 0
    for name in SMALL:
        n = weights[name].shape[1]
        mine, other = small_sum[:, :, off:off + n], small_theirs[:, :, off:off + n]
        off += n
        if name == "lower_bounds":
            mine = _lower_bounds_bwd(soft, mine[:, 0, :])[:, None, :]
            other = _lower_bounds_bwd(soft, other[:, 0, :])[:, None, :]
        shp = (depth, 1, n)
        res = _adamw(weights[name].reshape(shp), mom_m[name].reshape(shp), mom_v[name].reshape(shp), mine, other)
        outs[name] = [r.reshape(weights[name].shape) for r in res]
    assert off == small_sum.shape[-1]

    total_loss = lax.psum(loss[0, 0], ("x", "y", "c"))
    result = [total_loss, dh.reshape(bl, seq, d)]
    for k in range(4):
        result += [outs[name][k] for name in WEIGHT_ORDER]
    return tuple(result)
```
